```python
import math
import jax, jax.numpy as jnp
from jax import lax
import numpy as np

D_MODEL = 1024
BATCH = 8
SEQ = 2048
DEPTH = 2

HEAD_DIM = 64
N_Q_HEADS = 8
N_KV_HEADS = 2
WINDOW = 128
BLOCK = 128
N_BUCKETS = 32
MAX_DISTANCE = 128
SSM_HEADS = 8
SSM_HEAD_DIM = 64
SSM_GROUPS = 2
SSM_STATE = 128
CONV_WIDTH = 4
CHUNK = 128
D_FF = 4 * D_MODEL

D_ATTN = N_Q_HEADS * HEAD_DIM
D_KV = N_KV_HEADS * HEAD_DIM
D_SSM = SSM_HEADS * SSM_HEAD_DIM
D_BC = SSM_GROUPS * SSM_STATE
D_CONV = D_SSM + 2 * D_BC
D_MIX = D_ATTN + D_SSM
D_IN = D_ATTN + 2 * D_KV + D_SSM + D_CONV + SSM_HEADS
SPLITS = [D_ATTN, D_ATTN + D_KV, D_ATTN + 2 * D_KV, D_ATTN + 2 * D_KV + D_SSM,
          D_ATTN + 2 * D_KV + D_SSM + D_CONV]
EPS = 1e-6

kernel_name = "hymba_swa_sink_ssd_hybrid"


def rms_norm(x, g):
    xf = x.astype(jnp.float32)
    y = xf * lax.rsqrt(jnp.mean(jnp.square(xf), axis=-1, keepdims=True) + EPS)
    return (y * g.astype(jnp.float32)).astype(x.dtype)


def t5_causal_bucket(dist):
    max_exact = N_BUCKETS // 2
    d_f = jnp.maximum(dist, 1).astype(jnp.float32)
    large = max_exact + (jnp.log(d_f / max_exact) / math.log(MAX_DISTANCE / max_exact)
                         * (N_BUCKETS - max_exact)).astype(jnp.int32)
    large = jnp.minimum(large, N_BUCKETS - 1)
    return jnp.where(dist < max_exact, dist, large)


def band_bias_and_mask(rel_bias, n_blocks):
    qi = jnp.arange(BLOCK)[:, None]
    kj = jnp.arange(2 * BLOCK)[None, :]
    dist = qi + BLOCK - kj
    in_window = (dist >= 0) & (dist < WINDOW)
    bucket = t5_causal_bucket(jnp.clip(dist, 0, None))
    bias = jnp.transpose(rel_bias[bucket], (2, 0, 1))
    key_pos = jnp.arange(n_blocks)[:, None] * BLOCK + kj - BLOCK
    mask = in_window[None] & (key_pos >= 0)[:, None, :]
    return bias, mask


def sliding_window_attention(q, k, v, q_gain, k_gain, sinks, bias, mask):
    b, s = q.shape[:2]
    nb = s // BLOCK
    g = N_Q_HEADS // N_KV_HEADS
    q = rms_norm(q.reshape(b, s, N_Q_HEADS, HEAD_DIM), q_gain)
    k = rms_norm(k.reshape(b, s, N_KV_HEADS, HEAD_DIM), k_gain)
    v = v.reshape(b, s, N_KV_HEADS, HEAD_DIM)
    qb = q.reshape(b, nb, BLOCK, N_KV_HEADS, g, HEAD_DIM)

    def band(t):
        t = t.reshape(b, nb, BLOCK, N_KV_HEADS, HEAD_DIM)
        prev = jnp.pad(t, ((0, 0), (1, 0), (0, 0), (0, 0), (0, 0)))[:, :-1]
        return jnp.concatenate([prev, t], axis=2)

    kb, vb = band(k), band(v)
    scores = jnp.einsum('bnqhgd,bnkhd->bnhgqk', qb, kb).astype(jnp.float32) * (HEAD_DIM ** -0.5)
    scores = scores + bias.reshape(N_KV_HEADS, g, BLOCK, 2 * BLOCK).astype(jnp.float32)
    scores = jnp.where(mask[None, :, None, None], scores, -jnp.inf)
    sink = jnp.broadcast_to(sinks.reshape(N_KV_HEADS, g, 1, 1).astype(jnp.float32),
                            scores.shape[:-1] + (1,))
    probs = jax.nn.softmax(jnp.concatenate([scores, sink], axis=-1), axis=-1)[..., :-1]
    out = jnp.einsum('bnhgqk,bnkhd->bnqhgd', probs.astype(v.dtype), vb)
    return out.reshape(b, s, D_ATTN)


def causal_depthwise_conv(u, w, bias):
    out = lax.conv_general_dilated(u, w[:, None, :], window_strides=(1,),
                                   padding=[(CONV_WIDTH - 1, 0)],
                                   dimension_numbers=('NWC', 'WIO', 'NWC'),
                                   feature_group_count=u.shape[-1])
    return out + bias


def ssd_mixer(z, xbc, dt_raw, conv_w, conv_b, dt_bias, a_log, d_skip, norm_g):
    f32 = jnp.float32
    b, s = z.shape[:2]
    nc = s // CHUNK
    r = SSM_HEADS // SSM_GROUPS
    xbc = jax.nn.silu(causal_depthwise_conv(xbc, conv_w, conv_b))
    xs, bm, cm = jnp.split(xbc, [D_SSM, D_SSM + D_BC], axis=-1)
    xs = xs.astype(f32).reshape(b, nc, CHUNK, SSM_GROUPS, r, SSM_HEAD_DIM)
    bm = bm.astype(f32).reshape(b, nc, CHUNK, SSM_GROUPS, SSM_STATE)
    cm = cm.astype(f32).reshape(b, nc, CHUNK, SSM_GROUPS, SSM_STATE)
    dt = jax.nn.softplus(dt_raw.astype(f32) + dt_bias.astype(f32)).reshape(b, nc, CHUNK, SSM_GROUPS, r)
    a = -jnp.exp(a_log.astype(f32)).reshape(SSM_GROUPS, r)
    a_cs = jnp.cumsum(dt * a, axis=2)
    xdt = xs * dt[..., None]
    li = jnp.arange(CHUNK)
    causal = (li[:, None] >= li[None, :])[:, :, None, None]
    seg = a_cs[:, :, :, None] - a_cs[:, :, None, :]
    decay = jnp.exp(jnp.where(causal, seg, -jnp.inf))
    cb = jnp.einsum('bclgn,bcsgn->bclsg', cm, bm)
    y_diag = jnp.einsum('bclsgr,bcsgrp->bclgrp', cb[..., None] * decay, xdt)
    decay_to_end = jnp.exp(a_cs[:, :, -1:] - a_cs)
    states = jnp.einsum('bclgn,bclgr,bclgrp->bcgrpn', bm, decay_to_end, xdt)
    chunk_decay = jnp.exp(a_cs[:, :, -1])

    def step(h, inp):
        st, dec = inp
        return h * dec[..., None, None] + st, h

    h0 = jnp.zeros((b, SSM_GROUPS, r, SSM_HEAD_DIM, SSM_STATE), f32)
    _, prev = lax.scan(step, h0, (jnp.moveaxis(states, 1, 0), jnp.moveaxis(chunk_decay, 1, 0)))
    prev = jnp.moveaxis(prev, 0, 1)
    y_off = jnp.einsum('bclgn,bcgrpn,bclgr->bclgrp', cm, prev, jnp.exp(a_cs))
    y = y_diag + y_off + xs * d_skip.astype(f32).reshape(SSM_GROUPS, r)[:, :, None]
    y = y.reshape(b, s, D_SSM) * jax.nn.silu(z.astype(f32))
    yg = y.reshape(b, s, SSM_GROUPS, D_SSM // SSM_GROUPS)
    yg = yg * lax.rsqrt(jnp.mean(jnp.square(yg), axis=-1, keepdims=True) + EPS)
    y = yg.reshape(b, s, D_SSM) * norm_g.astype(f32)
    return y.astype(z.dtype)


def _fwd_setup_inputs(seed: int = 0) -> dict:
    key = jax.random.key(seed)
    ks = jax.random.split(key, 20)
    nrm = jax.random.normal
    dt0 = jnp.exp(jax.random.uniform(ks[9], (DEPTH, SSM_HEADS), minval=math.log(1e-3), maxval=math.log(1e-1)))
    return {
        "x": nrm(ks[0], (BATCH, SEQ, D_MODEL), jnp.float32),
        "mix_norm_g": 1.0 + 0.01 * nrm(ks[1], (DEPTH, D_MODEL), jnp.float32),
        "w_in": nrm(ks[2], (DEPTH, D_MODEL, D_IN), jnp.float32) * D_MODEL ** -0.5,
        "q_gain": 1.0 + 0.01 * nrm(ks[3], (DEPTH, HEAD_DIM), jnp.float32),
        "k_gain": 1.0 + 0.01 * nrm(ks[4], (DEPTH, HEAD_DIM), jnp.float32),
        "sinks": 0.5 * nrm(ks[5], (DEPTH, N_Q_HEADS), jnp.float32),
        "rel_bias": 0.1 * nrm(ks[6], (N_BUCKETS, N_Q_HEADS), jnp.float32),
        "conv_w": nrm(ks[7], (DEPTH, CONV_WIDTH, D_CONV), jnp.float32) * CONV_WIDTH ** -0.5,
        "conv_b": 0.01 * nrm(ks[8], (DEPTH, D_CONV), jnp.float32),
        "dt_bias": dt0 + jnp.log(-jnp.expm1(-dt0)),
        "a_log": jnp.log(jax.random.uniform(ks[10], (DEPTH, SSM_HEADS), minval=1.0, maxval=16.0)),
        "d_skip": 1.0 + 0.01 * nrm(ks[11], (DEPTH, SSM_HEADS), jnp.float32),
        "ssm_norm_g": 1.0 + 0.01 * nrm(ks[12], (DEPTH, D_SSM), jnp.float32),
        "w_out": nrm(ks[13], (DEPTH, D_MIX, D_MODEL), jnp.float32) * D_MIX ** -0.5,
        "mlp_norm_g": 1.0 + 0.01 * nrm(ks[14], (DEPTH, D_MODEL), jnp.float32),
        "w_up": nrm(ks[15], (DEPTH, D_MODEL, D_FF), jnp.float32) * D_MODEL ** -0.5,
        "w_down": nrm(ks[16], (DEPTH, D_FF, D_MODEL), jnp.float32) * D_FF ** -0.5,
    }


def _fwd_reference(x, mix_norm_g, w_in, q_gain, k_gain, sinks, rel_bias, conv_w, conv_b,
              dt_bias, a_log, d_skip, ssm_norm_g, w_out, mlp_norm_g, w_up, w_down):
    bias, mask = band_bias_and_mask(rel_bias, x.shape[1] // BLOCK)
    for l in range(DEPTH):
        h = rms_norm(x, mix_norm_g[l])
        proj = h @ w_in[l]
        q, k, v, z, xbc, dt_raw = jnp.split(proj, SPLITS, axis=-1)
        attn = sliding_window_attention(q, k, v, q_gain[l], k_gain[l], sinks[l], bias, mask)
        ssm = ssd_mixer(z, xbc, dt_raw, conv_w[l], conv_b[l], dt_bias[l], a_log[l],
                        d_skip[l], ssm_norm_g[l])
        x = x + jnp.concatenate([attn, ssm], axis=-1) @ w_out[l]
        h = rms_norm(x, mlp_norm_g[l])
        x = x + jnp.square(jax.nn.relu(h @ w_up[l])) @ w_down[l]
    return x


import jax as _jax
import jax.numpy as _jnp

TWIN_FORMAT = 'train_step'
FWD_PARAMS = ['x', 'mix_norm_g', 'w_in', 'q_gain', 'k_gain', 'sinks', 'rel_bias', 'conv_w', 'conv_b', 'dt_bias', 'a_log', 'd_skip', 'ssm_norm_g', 'w_out', 'mlp_norm_g', 'w_up', 'w_down']
TWIN_WEIGHTS = ['mix_norm_g', 'w_in', 'q_gain', 'k_gain', 'sinks', 'rel_bias', 'conv_w', 'conv_b', 'dt_bias', 'a_log', 'd_skip', 'ssm_norm_g', 'w_out', 'mlp_norm_g', 'w_up', 'w_down']
TWIN_DIFF_INPUT = 'x'
TWIN_INPUTS = ['x', 'mix_norm_g', 'w_in', 'q_gain', 'k_gain', 'sinks', 'rel_bias', 'conv_w', 'conv_b', 'dt_bias', 'a_log', 'd_skip', 'ssm_norm_g', 'w_out', 'mlp_norm_g', 'w_up', 'w_down', 'loss_target', 'm_mix_norm_g', 'm_w_in', 'm_q_gain', 'm_k_gain', 'm_sinks', 'm_rel_bias', 'm_conv_w', 'm_conv_b', 'm_dt_bias', 'm_a_log', 'm_d_skip', 'm_ssm_norm_g', 'm_w_out', 'm_mlp_norm_g', 'm_w_up', 'm_w_down', 'v_mix_norm_g', 'v_w_in', 'v_q_gain', 'v_k_gain', 'v_sinks', 'v_rel_bias', 'v_conv_w', 'v_conv_b', 'v_dt_bias', 'v_a_log', 'v_d_skip', 'v_ssm_norm_g', 'v_w_out', 'v_mlp_norm_g', 'v_w_up', 'v_w_down']
TWIN_OUTPUTS = ['loss', 'grad_x', 'grad_mix_norm_g', 'grad_w_in', 'grad_q_gain', 'grad_k_gain', 'grad_sinks', 'grad_rel_bias', 'grad_conv_w', 'grad_conv_b', 'grad_dt_bias', 'grad_a_log', 'grad_d_skip', 'grad_ssm_norm_g', 'grad_w_out', 'grad_mlp_norm_g', 'grad_w_up', 'grad_w_down', 'delta_mix_norm_g', 'delta_w_in', 'delta_q_gain', 'delta_k_gain', 'delta_sinks', 'delta_rel_bias', 'delta_conv_w', 'delta_conv_b', 'delta_dt_bias', 'delta_a_log', 'delta_d_skip', 'delta_ssm_norm_g', 'delta_w_out', 'delta_mlp_norm_g', 'delta_w_up', 'delta_w_down', 'new_m_mix_norm_g', 'new_m_w_in', 'new_m_q_gain', 'new_m_k_gain', 'new_m_sinks', 'new_m_rel_bias', 'new_m_conv_w', 'new_m_conv_b', 'new_m_dt_bias', 'new_m_a_log', 'new_m_d_skip', 'new_m_ssm_norm_g', 'new_m_w_out', 'new_m_mlp_norm_g', 'new_m_w_up', 'new_m_w_down', 'new_v_mix_norm_g', 'new_v_w_in', 'new_v_q_gain', 'new_v_k_gain', 'new_v_sinks', 'new_v_rel_bias', 'new_v_conv_w', 'new_v_conv_b', 'new_v_dt_bias', 'new_v_a_log', 'new_v_d_skip', 'new_v_ssm_norm_g', 'new_v_w_out', 'new_v_mlp_norm_g', 'new_v_w_up', 'new_v_w_down']
TWIN_LEAF_KINDS = {'loss': 'loss', 'grad_x': 'grad_x', 'grad_mix_norm_g': 'grad_w', 'grad_w_in': 'grad_w', 'grad_q_gain': 'grad_w', 'grad_k_gain': 'grad_w', 'grad_sinks': 'grad_w', 'grad_rel_bias': 'grad_w', 'grad_conv_w': 'grad_w', 'grad_conv_b': 'grad_w', 'grad_dt_bias': 'grad_w', 'grad_a_log': 'grad_w', 'grad_d_skip': 'grad_w', 'grad_ssm_norm_g': 'grad_w', 'grad_w_out': 'grad_w', 'grad_mlp_norm_g': 'grad_w', 'grad_w_up': 'grad_w', 'grad_w_down': 'grad_w', 'delta_mix_norm_g': 'delta_w', 'delta_w_in': 'delta_w', 'delta_q_gain': 'delta_w', 'delta_k_gain': 'delta_w', 'delta_sinks': 'delta_w', 'delta_rel_bias': 'delta_w', 'delta_conv_w': 'delta_w', 'delta_conv_b': 'delta_w', 'delta_dt_bias': 'delta_w', 'delta_a_log': 'delta_w', 'delta_d_skip': 'delta_w', 'delta_ssm_norm_g': 'delta_w', 'delta_w_out': 'delta_w', 'delta_mlp_norm_g': 'delta_w', 'delta_w_up': 'delta_w', 'delta_w_down': 'delta_w', 'new_m_mix_norm_g': 'new_m', 'new_m_w_in': 'new_m', 'new_m_q_gain': 'new_m', 'new_m_k_gain': 'new_m', 'new_m_sinks': 'new_m', 'new_m_rel_bias': 'new_m', 'new_m_conv_w': 'new_m', 'new_m_conv_b': 'new_m', 'new_m_dt_bias': 'new_m', 'new_m_a_log': 'new_m', 'new_m_d_skip': 'new_m', 'new_m_ssm_norm_g': 'new_m', 'new_m_w_out': 'new_m', 'new_m_mlp_norm_g': 'new_m', 'new_m_w_up': 'new_m', 'new_m_w_down': 'new_m', 'new_v_mix_norm_g': 'new_v', 'new_v_w_in': 'new_v', 'new_v_q_gain': 'new_v', 'new_v_k_gain': 'new_v', 'new_v_sinks': 'new_v', 'new_v_rel_bias': 'new_v', 'new_v_conv_w': 'new_v', 'new_v_conv_b': 'new_v', 'new_v_dt_bias': 'new_v', 'new_v_a_log': 'new_v', 'new_v_d_skip': 'new_v', 'new_v_ssm_norm_g': 'new_v', 'new_v_w_out': 'new_v', 'new_v_mlp_norm_g': 'new_v', 'new_v_w_up': 'new_v', 'new_v_w_down': 'new_v'}


def _forward(args):
    return _fwd_reference(*[args[k] for k in FWD_PARAMS])


def _output_shape():
    out = _jax.eval_shape(lambda: _forward(_fwd_setup_inputs(0)))
    return out.shape, out.dtype

N_MICROBATCH = 1
ADAM_LR = 0.001
ADAM_B1 = 0.9
ADAM_B2 = 0.999
ADAM_EPS = 1e-08
ADAM_WD = 0.01
ADAM_STEP = 10
PER_EXAMPLE_BATCH_AXIS = {'x': 0, 'loss_target': 0}
SHARED_INPUTS = []
_WEIGHT_DTYPES = {'mix_norm_g': _jnp.float32, 'w_in': _jnp.float32, 'q_gain': _jnp.float32, 'k_gain': _jnp.float32, 'sinks': _jnp.float32, 'rel_bias': _jnp.float32, 'conv_w': _jnp.float32, 'conv_b': _jnp.float32, 'dt_bias': _jnp.float32, 'a_log': _jnp.float32, 'd_skip': _jnp.float32, 'ssm_norm_g': _jnp.float32, 'w_out': _jnp.float32, 'mlp_norm_g': _jnp.float32, 'w_up': _jnp.float32, 'w_down': _jnp.float32}
MOMENT_SCALE = {'mix_norm_g': 3.811241e+00, 'w_in': 2.210728e+00, 'q_gain': 1.597345e+00, 'k_gain': 1.601297e+00, 'sinks': 9.441474e-01, 'rel_bias': 2.488617e-01, 'conv_w': 2.093000e+00, 'conv_b': 6.806156e+00, 'dt_bias': 1.606943e+00, 'a_log': 1.346614e+01, 'd_skip': 7.984750e+00, 'ssm_norm_g': 2.467656e+01, 'w_out': 4.482086e+00, 'mlp_norm_g': 4.853435e+01, 'w_up': 2.331165e+00, 'w_down': 8.896435e+00}


def _to_microbatches(a, axis):
    t = _jnp.moveaxis(a, axis, 0)
    t = t.reshape((N_MICROBATCH, t.shape[0] // N_MICROBATCH) + t.shape[1:])
    return _jnp.moveaxis(t, 1, axis + 1)


def setup_inputs(seed: int = 0) -> dict:
    inp = _fwd_setup_inputs(seed)
    key = _jax.random.fold_in(_jax.random.key(seed), 7919)
    shape, _ = _output_shape()
    out = dict(inp)
    out["loss_target"] = _jax.random.normal(_jax.random.fold_in(key, 0), shape, _jnp.float32)
    for i, name in enumerate(TWIN_WEIGHTS):
        w = inp[name].astype(_jnp.float32)
        if MOMENT_SCALE is None:
            s = _jnp.sqrt(_jnp.mean(_jnp.square(w)) + 1e-30)
        else:
            s = MOMENT_SCALE[name]
        km, kv = _jax.random.split(_jax.random.fold_in(key, i + 1))
        out[name] = w
        out["m_" + name] = s * _jax.random.normal(km, w.shape, _jnp.float32)
        out["v_" + name] = (s * s) * _jax.random.uniform(kv, w.shape, _jnp.float32, 0.5, 1.5)
    if N_MICROBATCH > 1:
        for name, axis in PER_EXAMPLE_BATCH_AXIS.items():
            out[name] = _to_microbatches(out[name], axis)
    return {'x': out['x'], 'mix_norm_g': out['mix_norm_g'], 'w_in': out['w_in'], 'q_gain': out['q_gain'], 'k_gain': out['k_gain'], 'sinks': out['sinks'], 'rel_bias': out['rel_bias'], 'conv_w': out['conv_w'], 'conv_b': out['conv_b'], 'dt_bias': out['dt_bias'], 'a_log': out['a_log'], 'd_skip': out['d_skip'], 'ssm_norm_g': out['ssm_norm_g'], 'w_out': out['w_out'], 'mlp_norm_g': out['mlp_norm_g'], 'w_up': out['w_up'], 'w_down': out['w_down'], 'loss_target': out['loss_target'], 'm_mix_norm_g': out['m_mix_norm_g'], 'm_w_in': out['m_w_in'], 'm_q_gain': out['m_q_gain'], 'm_k_gain': out['m_k_gain'], 'm_sinks': out['m_sinks'], 'm_rel_bias': out['m_rel_bias'], 'm_conv_w': out['m_conv_w'], 'm_conv_b': out['m_conv_b'], 'm_dt_bias': out['m_dt_bias'], 'm_a_log': out['m_a_log'], 'm_d_skip': out['m_d_skip'], 'm_ssm_norm_g': out['m_ssm_norm_g'], 'm_w_out': out['m_w_out'], 'm_mlp_norm_g': out['m_mlp_norm_g'], 'm_w_up': out['m_w_up'], 'm_w_down': out['m_w_down'], 'v_mix_norm_g': out['v_mix_norm_g'], 'v_w_in': out['v_w_in'], 'v_q_gain': out['v_q_gain'], 'v_k_gain': out['v_k_gain'], 'v_sinks': out['v_sinks'], 'v_rel_bias': out['v_rel_bias'], 'v_conv_w': out['v_conv_w'], 'v_conv_b': out['v_conv_b'], 'v_dt_bias': out['v_dt_bias'], 'v_a_log': out['v_a_log'], 'v_d_skip': out['v_d_skip'], 'v_ssm_norm_g': out['v_ssm_norm_g'], 'v_w_out': out['v_w_out'], 'v_mlp_norm_g': out['v_mlp_norm_g'], 'v_w_up': out['v_w_up'], 'v_w_down': out['v_w_down']}


def _loss(weights, diff, rest, loss_target):
    with _jax.named_scope("forward"):
        args = {**rest, TWIN_DIFF_INPUT: diff, **{k: w.astype(_WEIGHT_DTYPES[k]) for k, w in weights.items()}}
        y = _forward(args)
    with _jax.named_scope("loss_head"):
        err = _jnp.square(y.astype(_jnp.float32) - loss_target)
        return 0.5 * _jnp.sum(_jnp.mean(err, axis=-1)) if err.ndim else 0.5 * err


def _adamw(w, g, m, v):
    m = ADAM_B1 * m + (1.0 - ADAM_B1) * g
    v = ADAM_B2 * v + (1.0 - ADAM_B2) * _jnp.square(g)
    m_hat = m / (1.0 - ADAM_B1 ** ADAM_STEP)
    v_hat = v / (1.0 - ADAM_B2 ** ADAM_STEP)
    delta = -ADAM_LR * (m_hat / (_jnp.sqrt(v_hat) + ADAM_EPS) + ADAM_WD * w)
    return delta, m, v


def reference(x, mix_norm_g, w_in, q_gain, k_gain, sinks, rel_bias, conv_w, conv_b, dt_bias, a_log, d_skip, ssm_norm_g, w_out, mlp_norm_g, w_up, w_down, loss_target, m_mix_norm_g, m_w_in, m_q_gain, m_k_gain, m_sinks, m_rel_bias, m_conv_w, m_conv_b, m_dt_bias, m_a_log, m_d_skip, m_ssm_norm_g, m_w_out, m_mlp_norm_g, m_w_up, m_w_down, v_mix_norm_g, v_w_in, v_q_gain, v_k_gain, v_sinks, v_rel_bias, v_conv_w, v_conv_b, v_dt_bias, v_a_log, v_d_skip, v_ssm_norm_g, v_w_out, v_mlp_norm_g, v_w_up, v_w_down):
    given = dict(x=x, mix_norm_g=mix_norm_g, w_in=w_in, q_gain=q_gain, k_gain=k_gain, sinks=sinks, rel_bias=rel_bias, conv_w=conv_w, conv_b=conv_b, dt_bias=dt_bias, a_log=a_log, d_skip=d_skip, ssm_norm_g=ssm_norm_g, w_out=w_out, mlp_norm_g=mlp_norm_g, w_up=w_up, w_down=w_down, loss_target=loss_target, m_mix_norm_g=m_mix_norm_g, m_w_in=m_w_in, m_q_gain=m_q_gain, m_k_gain=m_k_gain, m_sinks=m_sinks, m_rel_bias=m_rel_bias, m_conv_w=m_conv_w, m_conv_b=m_conv_b, m_dt_bias=m_dt_bias, m_a_log=m_a_log, m_d_skip=m_d_skip, m_ssm_norm_g=m_ssm_norm_g, m_w_out=m_w_out, m_mlp_norm_g=m_mlp_norm_g, m_w_up=m_w_up, m_w_down=m_w_down, v_mix_norm_g=v_mix_norm_g, v_w_in=v_w_in, v_q_gain=v_q_gain, v_k_gain=v_k_gain, v_sinks=v_sinks, v_rel_bias=v_rel_bias, v_conv_w=v_conv_w, v_conv_b=v_conv_b, v_dt_bias=v_dt_bias, v_a_log=v_a_log, v_d_skip=v_d_skip, v_ssm_norm_g=v_ssm_norm_g, v_w_out=v_w_out, v_mlp_norm_g=v_mlp_norm_g, v_w_up=v_w_up, v_w_down=v_w_down)
    weights = {n: given[n] for n in TWIN_WEIGHTS}
    shared = {n: given[n] for n in SHARED_INPUTS}
    per_example = {n: given[n] for n in ['x']}
    grad_fn = _jax.value_and_grad(_loss, argnums=(0, 1))

    def one_microbatch(ex, loss_target):
        ex = dict(ex)
        diff = ex.pop(TWIN_DIFF_INPUT)
        return grad_fn(weights, diff, {**shared, **ex}, loss_target)

    if N_MICROBATCH == 1:
        loss, (grad_w, grad_x) = one_microbatch(per_example, given["loss_target"])
    else:
        def body(carry, xs):
            loss_sum, grad_sum = carry
            l_k, (gw_k, gx_k) = one_microbatch(xs[0], xs[1])
            with _jax.named_scope("update"):
                return (loss_sum + l_k, _jax.tree.map(_jnp.add, grad_sum, gw_k)), gx_k

        init = (_jnp.zeros((), _jnp.float32), _jax.tree.map(_jnp.zeros_like, weights))
        (loss, grad_w), grad_x = _jax.lax.scan(body, init, (per_example, given["loss_target"]))
    with _jax.named_scope("update"):
        delta_w, new_m, new_v = {}, {}, {}
        for n in TWIN_WEIGHTS:
            delta_w[n], new_m[n], new_v[n] = _adamw(weights[n], grad_w[n], given["m_" + n], given["v_" + n])
    return (loss, grad_x, *[grad_w[n] for n in TWIN_WEIGHTS], *[delta_w[n] for n in TWIN_WEIGHTS],
            *[new_m[n] for n in TWIN_WEIGHTS], *[new_v[n] for n in TWIN_WEIGHTS])
```

```python
import functools
import math

import numpy as np
import jax
import jax.numpy as jnp
from jax import lax
from jax.experimental import pallas as pl
from jax.experimental.pallas import tpu as pltpu

f32 = jnp.float32
bf16 = jnp.bfloat16
SDS = jax.ShapeDtypeStruct
MESH = pl.DeviceIdType.MESH
HIGHEST = lax.Precision.HIGHEST

S = 2048
D = 1024
DEPTH = 2
BLK = 128
NBLK = S // BLK
HD = 64
NQ = 8
NKV = 2
NSSM = 8
NGRP = 2
NSTATE = 128
D_ATTN = 512
D_SSM = 512
D_CONV = 1024
D_FF = 4096
D_IN = 2312
D_IN_PAD = 2560
N_BUCKETS = 32
EPS = 1e-6
N_DEV = 8
VMEM_LIMIT = 48 * 1024 * 1024

ADAM_LR = 0.001
ADAM_B1 = 0.9
ADAM_B2 = 0.999
ADAM_EPS = 1e-08
ADAM_WD = 0.01
ADAM_STEP = 10

NT_DIMS = (((1,), (1,)), ((), ()))
TN_DIMS = (((0,), (0,)), ((), ()))
NN_DIMS = (((1,), (0,)), ((), ()))

ROW_MIXG = 0
ROW_MLPG = 2
ROW_CONVB = 4
ROW_SSMG = 6
ROW_MISC = 8
ROW_RELB = 10
ROW_CONVW = 11
ROW_LOSS = 19
SMALL_ROWS = 24


def _dot(a, b, dims):
    return lax.dot_general(a, b, dims, preferred_element_type=f32)


def _cparams(n_axes):
    return pltpu.CompilerParams(dimension_semantics=("arbitrary",) * n_axes, vmem_limit_bytes=VMEM_LIMIT)


def _sum11(v):
    return jnp.sum(jnp.sum(v, axis=1, keepdims=True), axis=0, keepdims=True)


def _sigmoid(v):
    return 1.0 / (1.0 + jnp.exp(-v))


def _matmul(name, mode, a, b, *, layer=0, tm, tn, tk, out_shape, out_specs, epilogue, extras=(), extra_specs=()):
    if mode == "tn":
        t_dim, m_dim = a.shape
        n_dim = b.shape[1]
        grid = (m_dim // tm, n_dim // tn, t_dim // tk)
        a_spec = pl.BlockSpec((tk, tm), lambda i, j, k: (k, i))
        b_spec = pl.BlockSpec((tk, tn), lambda i, j, k: (k, j))
        dims = TN_DIMS
    elif mode == "nn":
        m_dim, k_dim = a.shape
        n_dim = b.shape[-1]
        grid = (m_dim // tm, n_dim // tn, k_dim // tk)
        a_spec = pl.BlockSpec((tm, tk), lambda i, j, k: (i, k))
        b_spec = pl.BlockSpec((None, tk, tn), lambda i, j, k: (layer, k, j))
        dims = NN_DIMS
    else:
        m_dim, k_dim = a.shape
        n_dim = b.shape[-2]
        grid = (m_dim // tm, n_dim // tn, k_dim // tk)
        a_spec = pl.BlockSpec((tm, tk), lambda i, j, k: (i, k))
        b_spec = pl.BlockSpec((None, tn, tk), lambda i, j, k: (layer, j, k))
        dims = NT_DIMS
    nk = grid[2]
    n_ex = len(extras)

    def body(a_ref, b_ref, *rest):
        ex = rest[:n_ex]
        outs = rest[n_ex:-1]
        acc = rest[-1]
        i = pl.program_id(0)
        j = pl.program_id(1)
        k = pl.program_id(2)
        part = _dot(a_ref[...].astype(bf16), b_ref[...].astype(bf16), dims)
        if nk == 1:
            epilogue(part, i, j, ex, outs)
        else:
            @pl.when(k == 0)
            def _():
                acc[...] = part

            @pl.when(k > 0)
            def _():
                acc[...] += part

            @pl.when(k == nk - 1)
            def _():
                epilogue(acc[...], i, j, ex, outs)

    return pl.pallas_call(
        body, grid=grid, in_specs=[a_spec, b_spec, *extra_specs], out_specs=out_specs, out_shape=out_shape,
        scratch_shapes=[pltpu.VMEM((tm, tn) if nk > 1 else (8, 128), f32)], name=name, compiler_params=_cparams(3),
    )(a, b, *extras)


def _rms_bwd_epilogue(layer):
    def epi(acc, i, j, ex, outs):
        x_ref, g_ref, dres_ref = ex
        dx_ref, dg_ref = outs
        xv = x_ref[...]
        r = lax.rsqrt(jnp.mean(xv * xv, axis=-1, keepdims=True) + EPS)
        xhat = xv * r
        w = acc * g_ref[layer:layer + 1, :]
        dx_ref[...] = dres_ref[...] + r * (w - xhat * jnp.mean(xhat * w, axis=-1, keepdims=True))
        dg = jnp.sum(acc * xhat, axis=0, keepdims=True)

        @pl.when(i == 0)
        def _():
            dg_ref[...] = dg

        @pl.when(i > 0)
        def _():
            dg_ref[...] += dg
    return epi


def _rms_fwd(name, x, g, layer):
    tr = 512

    def body(x_ref, g_ref, h_ref):
        xv = x_ref[...]
        r = lax.rsqrt(jnp.mean(xv * xv, axis=-1, keepdims=True) + EPS)
        h_ref[...] = (xv * r * g_ref[layer:layer + 1, :]).astype(bf16)

    return pl.pallas_call(
        body, grid=(S // tr,),
        in_specs=[pl.BlockSpec((tr, D), lambda i: (i, 0)), pl.BlockSpec((DEPTH, D), lambda i: (0, 0))],
        out_specs=pl.BlockSpec((tr, D), lambda i: (i, 0)), out_shape=SDS((S, D), bf16), name=name,
        compiler_params=_cparams(1),
    )(x, g)


def _loss_kernel(y, tgt):
    tr = 512

    def body(y_ref, t_ref, dy_ref, loss_ref):
        err = y_ref[...] - t_ref[...]
        dy_ref[...] = err * (1.0 / D)
        part = 0.5 * jnp.sum(jnp.mean(err * err, axis=-1, keepdims=True), axis=0, keepdims=True)

        @pl.when(pl.program_id(0) == 0)
        def _():
            loss_ref[...] = jnp.zeros_like(loss_ref)

        loss_ref[...] += jnp.broadcast_to(part, loss_ref.shape)

    return pl.pallas_call(
        body, grid=(S // tr,),
        in_specs=[pl.BlockSpec((tr, D), lambda i: (i, 0)), pl.BlockSpec((tr, D), lambda i: (i, 0))],
        out_specs=[pl.BlockSpec((tr, D), lambda i: (i, 0)), pl.BlockSpec((1, 128), lambda i: (0, 0))],
        out_shape=[SDS((S, D), f32), SDS((1, 128), f32)], name="loss", compiler_params=_cparams(1),
    )(y, tgt)


def _pack_dproj(dqkv, dz, dxbc, ddt):
    tr = 256

    def body(a_ref, b_ref, c_ref, d_ref, o_ref):
        o_ref[:, 0:768] = a_ref[...].astype(bf16)
        o_ref[:, 768:1280] = b_ref[...].astype(bf16)
        o_ref[:, 1280:2304] = c_ref[...].astype(bf16)
        o_ref[:, 2304:2432] = d_ref[...].astype(bf16)
        o_ref[:, 2432:D_IN_PAD] = jnp.zeros((tr, D_IN_PAD - 2432), bf16)

    return pl.pallas_call(
        body, grid=(S // tr,),
        in_specs=[pl.BlockSpec((tr, 768), lambda i: (i, 0)), pl.BlockSpec((tr, 512), lambda i: (i, 0)),
                  pl.BlockSpec((tr, 1024), lambda i: (i, 0)), pl.BlockSpec((tr, 128), lambda i: (i, 0))],
        out_specs=pl.BlockSpec((tr, D_IN_PAD), lambda i: (i, 0)), out_shape=SDS((S, D_IN_PAD), bf16),
        name="pack_dproj", compiler_params=_cparams(1),
    )(dqkv, dz, dxbc, ddt)


def _cast_bf16(name, w):
    rows, cols = w.shape
    tr = 256

    def body(w_ref, o_ref):
        o_ref[...] = w_ref[...].astype(bf16)

    return pl.pallas_call(
        body, grid=(rows // tr,), in_specs=[pl.BlockSpec((tr, cols), lambda i: (i, 0))],
        out_specs=pl.BlockSpec((tr, cols), lambda i: (i, 0)), out_shape=SDS((rows, cols), bf16), name=name,
        compiler_params=_cparams(1),
    )(w)


def _adamw(name, w, m, v, parts, tr):
    rows, cols = w.shape
    n_parts = parts.shape[0]

    def body(w_ref, m_ref, v_ref, p_ref, g_ref, d_ref, mo_ref, vo_ref):
        g = p_ref[0].astype(f32)
        for p in range(1, n_parts):
            g = g + p_ref[p].astype(f32)
        m_new = ADAM_B1 * m_ref[...] + (1.0 - ADAM_B1) * g
        v_new = ADAM_B2 * v_ref[...] + (1.0 - ADAM_B2) * (g * g)
        m_hat = m_new / (1.0 - ADAM_B1 ** ADAM_STEP)
        v_hat = v_new / (1.0 - ADAM_B2 ** ADAM_STEP)
        g_ref[...] = g
        d_ref[...] = -ADAM_LR * (m_hat / (jnp.sqrt(v_hat) + ADAM_EPS) + ADAM_WD * w_ref[...])
        mo_ref[...] = m_new
        vo_ref[...] = v_new

    blk = pl.BlockSpec((tr, cols), lambda i: (i, 0))
    return pl.pallas_call(
        body, grid=(rows // tr,),
        in_specs=[blk, blk, blk, pl.BlockSpec((n_parts, tr, cols), lambda i: (0, i, 0))],
        out_specs=[blk, blk, blk, blk], out_shape=[SDS((rows, cols), f32)] * 4, name=name, compiler_params=_cparams(1),
    )(w, m, v, parts)


def _bucket_table():
    qi = np.arange(BLK)[:, None]
    kj = np.arange(2 * BLK)[None, :]
    dist = qi + BLK - kj
    dcl = np.clip(dist, 0, None)
    max_exact = N_BUCKETS // 2
    d_f = np.maximum(dcl, 1).astype(np.float32)
    large = max_exact + (np.log(d_f / np.float32(max_exact)) / np.float32(math.log(128 / max_exact))
                         * np.float32(N_BUCKETS - max_exact)).astype(np.int32)
    large = np.minimum(large, N_BUCKETS - 1)
    bucket = np.where(dcl < max_exact, dcl, large)
    in_window = (dist >= 0) & (dist < BLK)
    return bucket.astype(np.int32), in_window


def _onehot_buckets():
    bucket, _ = _bucket_table()
    oh = (bucket.reshape(-1)[None, :] == np.arange(N_BUCKETS)[:, None]).astype(np.float32)
    return oh


def _bias_build(rel_bias_t, onehot_t):
    def body(r_ref, o_ref, out_ref):
        out_ref[...] = jnp.dot(r_ref[...], o_ref[...], preferred_element_type=f32, precision=HIGHEST)

    tn = 4096
    return pl.pallas_call(
        body, grid=(BLK * 2 * BLK // tn,),
        in_specs=[pl.BlockSpec((NQ, N_BUCKETS), lambda i: (0, 0)), pl.BlockSpec((N_BUCKETS, tn), lambda i: (0, i))],
        out_specs=pl.BlockSpec((NQ, tn), lambda i: (0, i)), out_shape=SDS((NQ, BLK * 2 * BLK), f32), name="bias_build",
        compiler_params=_cparams(1),
    )(rel_bias_t, onehot_t)


def _bias_grad(dbias0, dbias1, onehot_t):
    tn = 4096
    nsteps = BLK * 2 * BLK // tn

    def body(a_ref, b_ref, o_ref, out_ref):
        part = lax.dot_general(a_ref[...] + b_ref[...], o_ref[...], NT_DIMS, preferred_element_type=f32, precision=HIGHEST)

        @pl.when(pl.program_id(0) == 0)
        def _():
            out_ref[...] = part

        @pl.when(pl.program_id(0) > 0)
        def _():
            out_ref[...] += part

    return pl.pallas_call(
        body, grid=(nsteps,),
        in_specs=[pl.BlockSpec((NQ, tn), lambda i: (0, i)), pl.BlockSpec((NQ, tn), lambda i: (0, i)),
                  pl.BlockSpec((N_BUCKETS, tn), lambda i: (0, i))],
        out_specs=pl.BlockSpec((NQ, N_BUCKETS), lambda i: (0, 0)), out_shape=SDS((NQ, N_BUCKETS), f32), name="bias_grad",
        compiler_params=_cparams(1),
    )(dbias0, dbias1, onehot_t)


def _attn_mask(n):
    qi = lax.broadcasted_iota(jnp.int32, (BLK, 2 * BLK), 0)
    kj = lax.broadcasted_iota(jnp.int32, (BLK, 2 * BLK), 1)
    dist = qi + BLK - kj
    first_key = jnp.where(n > 0, 0, BLK)
    return (dist >= 0) & (dist < BLK) & (kj >= first_key)


def _head_norm(t, gain):
    r = lax.rsqrt(jnp.mean(t * t, axis=-1, keepdims=True) + EPS)
    that = t * r
    return that, r, that * gain


def _softmax_with_sink(s, sink):
    m = jnp.maximum(jnp.max(s, axis=-1, keepdims=True), sink)
    p = jnp.exp(s - m)
    psink = jnp.exp(sink - m)
    inv = 1.0 / (jnp.sum(p, axis=-1, keepdims=True) + psink)
    return p * inv, psink * inv


def _attn_fwd(qkv, q_gain, k_gain, sinks, bias, layer):
    def body(q_ref, kc_ref, kp_ref, vc_ref, vp_ref, qg_ref, kg_ref, sk_ref, bias_ref, o_ref):
        n = pl.program_id(0)
        mask = _attn_mask(n)
        qg = qg_ref[layer:layer + 1, :]
        kg = kg_ref[layer:layer + 1, :]
        for j in range(NKV):
            cols = pl.ds(HD * j, HD)
            kb = jnp.concatenate([kp_ref[:, cols], kc_ref[:, cols]], axis=0)
            vb = jnp.concatenate([vp_ref[:, cols], vc_ref[:, cols]], axis=0).astype(bf16)
            kn = _head_norm(kb, kg)[2].astype(bf16)
            for g in range(NQ // NKV):
                h = j * (NQ // NKV) + g
                qn = _head_norm(q_ref[:, pl.ds(HD * h, HD)], qg)[2].astype(bf16)
                s = _dot(qn, kn, NT_DIMS) * (HD ** -0.5) + bias_ref[h]
                s = jnp.where(mask, s, -jnp.inf)
                p, _ = _softmax_with_sink(s, sk_ref[layer:layer + 1, h:h + 1])
                o_ref[:, pl.ds(HD * h, HD)] = _dot(p.astype(bf16), vb, NN_DIMS).astype(bf16)

    prev = lambda n: jnp.maximum(n - 1, 0)
    small = lambda shape: pl.BlockSpec(shape, lambda n: (0,) * len(shape))
    return pl.pallas_call(
        body, grid=(NBLK,),
        in_specs=[pl.BlockSpec((BLK, D_ATTN), lambda n: (n, 0)),
                  pl.BlockSpec((BLK, 128), lambda n: (n, 4)), pl.BlockSpec((BLK, 128), lambda n: (prev(n), 4)),
                  pl.BlockSpec((BLK, 128), lambda n: (n, 5)), pl.BlockSpec((BLK, 128), lambda n: (prev(n), 5)),
                  small((DEPTH, HD)), small((DEPTH, HD)), small((DEPTH, NQ)), small((NQ, BLK, 2 * BLK))],
        out_specs=pl.BlockSpec((BLK, D_ATTN), lambda n: (n, 0)), out_shape=SDS((S, D_ATTN), bf16),
        name="attn_fwd", compiler_params=_cparams(1),
    )(qkv, qkv, qkv, qkv, qkv, q_gain, k_gain, sinks, bias)


def _attn_bwd(qkv, dmix, q_gain, k_gain, sinks, bias, layer):
    def body(q_ref, kc_ref, kp_ref, vc_ref, vp_ref, do_ref, qg_ref, kg_ref, sk_ref, bias_ref,
             dqkv_ref, dbias_ref, dsm_ref, carry):
        i = pl.program_id(0)
        n = NBLK - 1 - i
        mask = _attn_mask(n)
        qg = qg_ref[layer:layer + 1, :]
        kg = kg_ref[layer:layer + 1, :]
        lane = lax.broadcasted_iota(jnp.int32, (1, 128), 1)

        @pl.when(i == 0)
        def _():
            carry[...] = jnp.zeros_like(carry)
            dbias_ref[...] = jnp.zeros_like(dbias_ref)
            dsm_ref[...] = jnp.zeros_like(dsm_ref)

        dqg = jnp.zeros((1, HD), f32)
        dkg = jnp.zeros((1, HD), f32)
        dsink = jnp.zeros((1, 128), f32)
        for j in range(NKV):
            cols = pl.ds(HD * j, HD)
            kb = jnp.concatenate([kp_ref[:, cols], kc_ref[:, cols]], axis=0)
            vb = jnp.concatenate([vp_ref[:, cols], vc_ref[:, cols]], axis=0).astype(bf16)
            khat, rk, kn = _head_norm(kb, kg)
            kn_b = kn.astype(bf16)
            dkn = jnp.zeros((2 * BLK, HD), f32)
            dv = jnp.zeros((2 * BLK, HD), f32)
            for g in range(NQ // NKV):
                h = j * (NQ // NKV) + g
                qhat, rq, qn = _head_norm(q_ref[:, pl.ds(HD * h, HD)], qg)
                qn_b = qn.astype(bf16)
                s = _dot(qn_b, kn_b, NT_DIMS) * (HD ** -0.5) + bias_ref[h]
                s = jnp.where(mask, s, -jnp.inf)
                p, psink = _softmax_with_sink(s, sk_ref[layer:layer + 1, h:h + 1])
                do_b = do_ref[:, pl.ds(HD * h, HD)].astype(bf16)
                dp = _dot(do_b, vb, NT_DIMS)
                delta = jnp.sum(p * dp, axis=-1, keepdims=True)
                ds = p * (dp - delta)
                dsink = dsink + jnp.where(lane == h, -_sum11(psink * delta), 0.0)
                dbias_ref[h] += ds
                ds_b = (ds * (HD ** -0.5)).astype(bf16)
                dqn = _dot(ds_b, kn_b, NN_DIMS)
                dkn = dkn + _dot(ds_b, qn_b, TN_DIMS)
                dv = dv + _dot(p.astype(bf16), do_b, TN_DIMS)
                w = dqn * qg
                dqkv_ref[:, pl.ds(HD * h, HD)] = rq * (w - qhat * jnp.mean(qhat * w, axis=-1, keepdims=True))
                dqg = dqg + jnp.sum(dqn * qhat, axis=0, keepdims=True)
            w = dkn * kg
            dk = rk * (w - khat * jnp.mean(khat * w, axis=-1, keepdims=True))
            dkg = dkg + jnp.sum(dkn * khat, axis=0, keepdims=True)
            dqkv_ref[:, pl.ds(D_ATTN + HD * j, HD)] = dk[BLK:, :] + carry[:, pl.ds(HD * j, HD)]
            dqkv_ref[:, pl.ds(D_ATTN + 128 + HD * j, HD)] = dv[BLK:, :] + carry[:, pl.ds(128 + HD * j, HD)]
            carry[:, pl.ds(HD * j, HD)] = dk[:BLK, :]
            carry[:, pl.ds(128 + HD * j, HD)] = dv[:BLK, :]
        dsm_ref[0:1, 0:HD] += dqg
        dsm_ref[1:2, 0:HD] += dkg
        dsm_ref[2:3, :] += dsink

    rev = lambda i: NBLK - 1 - i
    prev = lambda i: jnp.maximum(NBLK - 2 - i, 0)
    small = lambda shape: pl.BlockSpec(shape, lambda i: (0,) * len(shape))
    return pl.pallas_call(
        body, grid=(NBLK,),
        in_specs=[pl.BlockSpec((BLK, D_ATTN), lambda i: (rev(i), 0)),
                  pl.BlockSpec((BLK, 128), lambda i: (rev(i), 4)), pl.BlockSpec((BLK, 128), lambda i: (prev(i), 4)),
                  pl.BlockSpec((BLK, 128), lambda i: (rev(i), 5)), pl.BlockSpec((BLK, 128), lambda i: (prev(i), 5)),
                  pl.BlockSpec((BLK, D_ATTN), lambda i: (rev(i), 0)),
                  small((DEPTH, HD)), small((DEPTH, HD)), small((DEPTH, NQ)), small((NQ, BLK, 2 * BLK))],
        out_specs=[pl.BlockSpec((BLK, 768), lambda i: (rev(i), 0)), small((NQ, BLK, 2 * BLK)), small((8, 128))],
        out_shape=[SDS((S, 768), f32), SDS((NQ, BLK, 2 * BLK), f32), SDS((8, 128), f32)],
        scratch_shapes=[pltpu.VMEM((BLK, 256), f32)], name="attn_bwd", compiler_params=_cparams(1),
    )(qkv, qkv, qkv, qkv, qkv, dmix, q_gain, k_gain, sinks, bias)


CONV_TC = 128


def _shift_down(u, s):
    if s == 0:
        return u
    rows = lax.broadcasted_iota(jnp.int32, u.shape, 0)
    return jnp.where(rows >= s, pltpu.roll(u, s, 0), 0.0)


def _shift_up(u, s):
    if s == 0:
        return u
    rows = lax.broadcasted_iota(jnp.int32, u.shape, 0)
    return jnp.where(rows < u.shape[0] - s, pltpu.roll(u, u.shape[0] - s, 0), 0.0)


def _conv_specs():
    return [pl.BlockSpec((S, CONV_TC), lambda c: (0, c)),
            pl.BlockSpec((None, 4, CONV_TC), lambda c: (0, 0, c)),
            pl.BlockSpec((DEPTH, CONV_TC), lambda c: (0, c))]


def _conv_pre(u, w_ref, b_ref, layer):
    pre = b_ref[layer:layer + 1, :] + w_ref[3:4, :] * u
    for k in range(3):
        pre = pre + w_ref[k:k + 1, :] * _shift_down(u, 3 - k)
    return pre


def _conv_fwd(xbc, conv_w, conv_b, layer):
    def body(u_ref, w_ref, b_ref, o_ref):
        pre = _conv_pre(u_ref[...], w_ref, b_ref, layer)
        o_ref[...] = pre * _sigmoid(pre)

    specs = _conv_specs()
    specs[1] = pl.BlockSpec((None, 4, CONV_TC), lambda c: (layer, 0, c))
    return pl.pallas_call(
        body, grid=(D_CONV // CONV_TC,), in_specs=specs, out_specs=pl.BlockSpec((S, CONV_TC), lambda c: (0, c)),
        out_shape=SDS((S, D_CONV), f32), name="conv_fwd", compiler_params=_cparams(1),
    )(xbc, conv_w, conv_b)


def _conv_bwd(xbc, dact, conv_w, conv_b, layer):
    def body(u_ref, w_ref, b_ref, da_ref, du_ref, dw_ref, db_ref):
        u = u_ref[...]
        pre = _conv_pre(u, w_ref, b_ref, layer)
        sg = _sigmoid(pre)
        dpre = da_ref[...] * (sg * (1.0 + pre * (1.0 - sg)))
        du = w_ref[3:4, :] * dpre
        for k in range(3):
            du = du + w_ref[k:k + 1, :] * _shift_up(dpre, 3 - k)
        du_ref[...] = du
        db_ref[...] = jnp.broadcast_to(jnp.sum(dpre, axis=0, keepdims=True), db_ref.shape)
        dw_ref[...] = jnp.zeros_like(dw_ref)
        for k in range(4):
            dw_ref[k:k + 1, :] = jnp.sum(dpre * _shift_down(u, 3 - k), axis=0, keepdims=True)

    specs = _conv_specs()
    specs[1] = pl.BlockSpec((None, 4, CONV_TC), lambda c: (layer, 0, c))
    col = pl.BlockSpec((S, CONV_TC), lambda c: (0, c))
    row8 = pl.BlockSpec((8, CONV_TC), lambda c: (0, c))
    return pl.pallas_call(
        body, grid=(D_CONV // CONV_TC,), in_specs=[*specs, col], out_specs=[col, row8, row8],
        out_shape=[SDS((S, D_CONV), f32), SDS((8, D_CONV), f32), SDS((8, D_CONV), f32)], name="conv_bwd",
        compiler_params=_cparams(1),
    )(xbc, conv_w, conv_b, dact)


def _tri():
    return (lax.broadcasted_iota(jnp.int32, (BLK, BLK), 0) >= lax.broadcasted_iota(jnp.int32, (BLK, BLK), 1))


def _ssd_scalars(dt_ref, dtb_ref, alog_ref, layer):
    raw = dt_ref[:, 0:NSSM] + dtb_ref[layer:layer + 1, :]
    dtv = jnp.maximum(raw, 0.0) + jnp.log(1.0 + jnp.exp(-jnp.abs(raw)))
    a = -jnp.exp(alog_ref[layer:layer + 1, :])
    acs = jnp.dot(_tri().astype(f32), dtv * a, preferred_element_type=f32, precision=HIGHEST)
    return raw, dtv, a, acs


def _ssd_head_fwd(xc_ref, cb, cm_b, h, dtv, acs, acs_t, e_all, prev_b, dskip):
    xs = xc_ref[:, pl.ds(HD * h, HD)]
    xdt = xs * dtv[:, h:h + 1]
    seg = acs[:, h:h + 1] - acs_t[h:h + 1, :]
    decay = jnp.exp(jnp.where(_tri(), seg, -jnp.inf))
    m = cb * decay
    t_off = _dot(cm_b, prev_b, NT_DIMS)
    y_off = t_off * e_all[:, h:h + 1]
    y = _dot(m.astype(bf16), xdt.astype(bf16), NN_DIMS) + y_off + xs * dskip[:, h:h + 1]
    return xs, xdt, decay, m, t_off, y_off, y


def _ssd_fwd(xact, z, dt, attn, dt_bias, a_log, d_skip, norm_g, layer):
    def body(xc_ref, z_ref, dt_ref, at_ref, dtb_ref, alog_ref, dsk_ref, ng_ref, mix_ref, hs_ref, h_ref, y_ref):
        n = pl.program_id(0)

        @pl.when(n == 0)
        def _():
            h_ref[...] = jnp.zeros_like(h_ref)

        _, dtv, _, acs = _ssd_scalars(dt_ref, dtb_ref, alog_ref, layer)
        acs_t = acs.T
        last = acs[BLK - 1:BLK, :]
        dte = jnp.exp(last - acs)
        e_all = jnp.exp(acs)
        cd = jnp.exp(last)
        dskip = dsk_ref[layer:layer + 1, :]
        hs_ref[...] = h_ref[...]
        for g in range(NGRP):
            bm = xc_ref[:, pl.ds(D_SSM + NSTATE * g, NSTATE)]
            cm_b = xc_ref[:, pl.ds(D_SSM + NGRP * NSTATE + NSTATE * g, NSTATE)].astype(bf16)
            cb = _dot(cm_b, bm.astype(bf16), NT_DIMS)
            for r in range(NSSM // NGRP):
                h = g * (NSSM // NGRP) + r
                rows = pl.ds(HD * h, HD)
                prev = h_ref[rows, :]
                _, xdt, _, _, _, _, y = _ssd_head_fwd(xc_ref, cb, cm_b, h, dtv, acs, acs_t, e_all, prev.astype(bf16), dskip)
                y_ref[:, pl.ds(HD * h, HD)] = y
                bd = (bm * dte[:, h:h + 1]).astype(bf16)
                h_ref[rows, :] = prev * cd[:, h:h + 1] + _dot(xdt.astype(bf16), bd, TN_DIMS)
        zv = z_ref[...]
        yz = y_ref[...] * (zv * _sigmoid(zv))
        mix_ref[:, 0:D_ATTN] = at_ref[...]
        gw = D_SSM // NGRP
        for g in range(NGRP):
            yg = yz[:, gw * g:gw * (g + 1)]
            rs = lax.rsqrt(jnp.mean(yg * yg, axis=-1, keepdims=True) + EPS)
            mix_ref[:, D_ATTN + gw * g:D_ATTN + gw * (g + 1)] = (yg * rs * ng_ref[layer:layer + 1, gw * g:gw * (g + 1)]).astype(bf16)

    small = lambda shape: pl.BlockSpec(shape, lambda n: (0,) * len(shape))
    return pl.pallas_call(
        body, grid=(NBLK,),
        in_specs=[pl.BlockSpec((BLK, D_CONV), lambda n: (n, 0)), pl.BlockSpec((BLK, D_SSM), lambda n: (n, 0)),
                  pl.BlockSpec((BLK, 128), lambda n: (n, 0)), pl.BlockSpec((BLK, D_ATTN), lambda n: (n, 0)),
                  small((DEPTH, NSSM)), small((DEPTH, NSSM)), small((DEPTH, NSSM)), small((DEPTH, D_SSM))],
        out_specs=[pl.BlockSpec((BLK, D), lambda n: (n, 0)), pl.BlockSpec((None, NSSM * HD, NSTATE), lambda n: (n, 0, 0))],
        out_shape=[SDS((S, D), bf16), SDS((NBLK, NSSM * HD, NSTATE), f32)],
        scratch_shapes=[pltpu.VMEM((NSSM * HD, NSTATE), f32), pltpu.VMEM((BLK, D_SSM), f32)],
        name="ssd_fwd", compiler_params=_cparams(1),
    )(xact, z, dt, attn, dt_bias, a_log, d_skip, norm_g)


def _ssd_bwd(xact, z, dt, dmix, hs, dt_bias, a_log, d_skip, norm_g, layer):
    def body(xc_ref, z_ref, dt_ref, do_ref, hs_ref, dtb_ref, alog_ref, dsk_ref, ng_ref,
             dz_ref, dx_ref, ddt_ref, dsm_ref, dh_ref, y_ref, dy_ref):
        i = pl.program_id(0)

        @pl.when(i == 0)
        def _():
            dh_ref[...] = jnp.zeros_like(dh_ref)
            dsm_ref[...] = jnp.zeros_like(dsm_ref)

        raw, dtv, a, acs = _ssd_scalars(dt_ref, dtb_ref, alog_ref, layer)
        acs_t = acs.T
        last = acs[BLK - 1:BLK, :]
        dte = jnp.exp(last - acs)
        e_all = jnp.exp(acs)
        cd = jnp.exp(last)
        dskip = dsk_ref[layer:layer + 1, :]
        lane8 = lax.broadcasted_iota(jnp.int32, (1, NSSM), 1)
        sub8 = lax.broadcasted_iota(jnp.int32, (NSSM, 1), 0)

        for g in range(NGRP):
            bm_b = xc_ref[:, pl.ds(D_SSM + NSTATE * g, NSTATE)].astype(bf16)
            cm_b = xc_ref[:, pl.ds(D_SSM + NGRP * NSTATE + NSTATE * g, NSTATE)].astype(bf16)
            cb = _dot(cm_b, bm_b, NT_DIMS)
            for r in range(NSSM // NGRP):
                h = g * (NSSM // NGRP) + r
                prev_b = hs_ref[pl.ds(HD * h, HD), :].astype(bf16)
                y_ref[:, pl.ds(HD * h, HD)] = _ssd_head_fwd(xc_ref, cb, cm_b, h, dtv, acs, acs_t, e_all, prev_b, dskip)[6]

        zv = z_ref[...]
        sz = _sigmoid(zv)
        gz = zv * sz
        yv = y_ref[...]
        yz = yv * gz
        gw = D_SSM // NGRP
        for g in range(NGRP):
            sl = slice(gw * g, gw * (g + 1))
            yg = yz[:, sl]
            rs = lax.rsqrt(jnp.mean(yg * yg, axis=-1, keepdims=True) + EPS)
            yhat = yg * rs
            dog = do_ref[:, sl]
            w = dog * ng_ref[layer:layer + 1, sl]
            dyz = rs * (w - yhat * jnp.mean(yhat * w, axis=-1, keepdims=True))
            dsm_ref[0:1, sl] += jnp.sum(dog * yhat, axis=0, keepdims=True)
            dy_ref[:, sl] = dyz * gz[:, sl]
            dz_ref[:, sl] = dyz * yv[:, sl] * (sz[:, sl] * (1.0 + zv[:, sl] * (1.0 - sz[:, sl])))

        dacs = jnp.zeros((BLK, NSSM), f32)
        dacs_cols = jnp.zeros((NSSM, BLK), f32)
        dlast = jnp.zeros((1, NSSM), f32)
        ddtv = jnp.zeros((BLK, NSSM), f32)
        ddsk = jnp.zeros((1, NSSM), f32)
        for g in range(NGRP):
            bm = xc_ref[:, pl.ds(D_SSM + NSTATE * g, NSTATE)]
            bm_b = bm.astype(bf16)
            cm_b = xc_ref[:, pl.ds(D_SSM + NGRP * NSTATE + NSTATE * g, NSTATE)].astype(bf16)
            cb = _dot(cm_b, bm_b, NT_DIMS)
            dcb = jnp.zeros((BLK, BLK), f32)
            dbm = jnp.zeros((BLK, NSTATE), f32)
            dcm = jnp.zeros((BLK, NSTATE), f32)
            for r in range(NSSM // NGRP):
                h = g * (NSSM // NGRP) + r
                rows = pl.ds(HD * h, HD)
                oh = (lane8 == h).astype(f32)
                prev = hs_ref[rows, :]
                prev_b = prev.astype(bf16)
                xs, xdt, decay, m, t_off, y_off, _ = _ssd_head_fwd(xc_ref, cb, cm_b, h, dtv, acs, acs_t, e_all, prev_b, dskip)
                xdt_b = xdt.astype(bf16)
                dy = dy_ref[:, pl.ds(HD * h, HD)]
                dy_b = dy.astype(bf16)
                ddsk = ddsk + oh * _sum11(dy * xs)
                dxs = dy * dskip[:, h:h + 1]
                dm = _dot(dy_b, xdt_b, NT_DIMS)
                dxdt = _dot(m.astype(bf16), dy_b, TN_DIMS)
                dseg = dm * m
                dcb = dcb + dm * decay
                dacs = dacs + oh * jnp.sum(dseg, axis=1, keepdims=True)
                dacs_cols = dacs_cols + (sub8 == h).astype(f32) * jnp.sum(dseg, axis=0, keepdims=True)
                dt_off = (dy * e_all[:, h:h + 1]).astype(bf16)
                dacs = dacs + oh * jnp.sum(dy * y_off, axis=1, keepdims=True)
                dcm = dcm + _dot(dt_off, prev_b, NN_DIMS)
                dprev = _dot(dt_off, cm_b, TN_DIMS)
                dhn = dh_ref[rows, :]
                dhn_b = dhn.astype(bf16)
                cd_h = cd[:, h:h + 1]
                dprev = dprev + dhn * cd_h
                dlast = dlast + oh * (_sum11(dhn * prev) * cd_h)
                dte_h = dte[:, h:h + 1]
                bd_b = (bm * dte_h).astype(bf16)
                dxdt = dxdt + _dot(bd_b, dhn_b, NT_DIMS)
                dbd = _dot(xdt_b, dhn_b, NN_DIMS)
                dbm = dbm + dbd * dte_h
                tmp = jnp.sum(dbd * bm, axis=1, keepdims=True) * dte_h
                dlast = dlast + oh * _sum11(tmp)
                dacs = dacs - oh * tmp
                dxs = dxs + dxdt * dtv[:, h:h + 1]
                ddtv = ddtv + oh * jnp.sum(dxdt * xs, axis=1, keepdims=True)
                dh_ref[rows, :] = dprev
                dx_ref[:, pl.ds(HD * h, HD)] = dxs
            dcb_b = dcb.astype(bf16)
            dx_ref[:, pl.ds(D_SSM + NSTATE * g, NSTATE)] = dbm + _dot(dcb_b, cm_b, TN_DIMS)
            dx_ref[:, pl.ds(D_SSM + NGRP * NSTATE + NSTATE * g, NSTATE)] = dcm + _dot(dcb_b, bm_b, NN_DIMS)

        row = lax.broadcasted_iota(jnp.int32, (BLK, 1), 0)
        dacs = dacs - dacs_cols.T + jnp.where(row == BLK - 1, dlast, 0.0)
        dda = lax.dot_general(_tri().astype(f32), dacs, TN_DIMS, preferred_element_type=f32, precision=HIGHEST)
        ddtv = ddtv + dda * a
        da = jnp.sum(dda * dtv, axis=0, keepdims=True)
        draw = ddtv * _sigmoid(raw)
        ddt_ref[...] = jnp.zeros_like(ddt_ref)
        ddt_ref[:, 0:NSSM] = draw
        dsm_ref[1:2, 0:NSSM] += jnp.sum(draw, axis=0, keepdims=True)
        dsm_ref[2:3, 0:NSSM] += da * a
        dsm_ref[3:4, 0:NSSM] += ddsk

    rev = lambda i: NBLK - 1 - i
    small = lambda shape: pl.BlockSpec(shape, lambda i: (0,) * len(shape))
    return pl.pallas_call(
        body, grid=(NBLK,),
        in_specs=[pl.BlockSpec((BLK, D_CONV), lambda i: (rev(i), 0)), pl.BlockSpec((BLK, D_SSM), lambda i: (rev(i), 0)),
                  pl.BlockSpec((BLK, 128), lambda i: (rev(i), 0)), pl.BlockSpec((BLK, D_SSM), lambda i: (rev(i), 1)),
                  pl.BlockSpec((None, NSSM * HD, NSTATE), lambda i: (rev(i), 0, 0)),
                  small((DEPTH, NSSM)), small((DEPTH, NSSM)), small((DEPTH, NSSM)), small((DEPTH, D_SSM))],
        out_specs=[pl.BlockSpec((BLK, D_SSM), lambda i: (rev(i), 0)), pl.BlockSpec((BLK, D_CONV), lambda i: (rev(i), 0)),
                   pl.BlockSpec((BLK, 128), lambda i: (rev(i), 0)), small((8, D_SSM))],
        out_shape=[SDS((S, D_SSM), f32), SDS((S, D_CONV), f32), SDS((S, 128), f32), SDS((8, D_SSM), f32)],
        scratch_shapes=[pltpu.VMEM((NSSM * HD, NSTATE), f32), pltpu.VMEM((BLK, D_SSM), f32), pltpu.VMEM((BLK, D_SSM), f32)],
        name="ssd_bwd", compiler_params=_cparams(1),
    )(xact, z, dt, dmix, hs, dt_bias, a_log, d_skip, norm_g)


def _my_place():
    return lax.axis_index("x"), lax.axis_index("y"), lax.axis_index("c")


def _dev_index(px, py, pc):
    return 4 * px + 2 * py + pc


def _slab(kind, ref, idx):
    if kind == "stack":
        return ref.at[idx]
    if kind == "rows128":
        return ref.at[:, pl.ds(pl.multiple_of(idx * 128, 128), 128), :]
    if kind == "rows512":
        return ref.at[:, pl.ds(pl.multiple_of(idx * 512, 512), 512), :]
    if kind == "cols512":
        return ref.at[:, :, pl.ds(pl.multiple_of(idx * 512, 512), 512)]
    raise ValueError(kind)


AG_KINDS = ("stack", "rows128", "cols512", "rows512", "stack")


def _all_gather_weights(shards, out_shapes):
    n_t = len(shards)

    def body(*refs):
        srcs = refs[:n_t]
        outs = refs[n_t:2 * n_t]
        send_sems, recv_sems, local_sems = refs[2 * n_t:]
        x, y, c = _my_place()
        me, sibling = (x, y, c), (x, y, 1 - c)
        chips = [(1 - x, y), (x, 1 - y), (1 - x, 1 - y)]

        def region(t, dev):
            return _slab(AG_KINDS[t], outs[t], _dev_index(*dev))

        def copy(t, k, block, to, src=None):
            return pltpu.make_async_remote_copy(
                src_ref=region(t, block) if src is None else src, dst_ref=region(t, block),
                send_sem=send_sems.at[t, k], recv_sem=recv_sems.at[t, k], device_id=to, device_id_type=MESH)

        mine = [pltpu.make_async_copy(srcs[t], region(t, me), local_sems.at[t]) for t in range(n_t)]
        for cp in mine:
            cp.start()
        first = []
        for t in range(n_t):
            first.append(copy(t, 0, me, sibling, src=srcs[t]))
            first += [copy(t, 1 + j, me, (*chip, c), src=srcs[t]) for j, chip in enumerate(chips)]
        for cp in first:
            cp.start()
        passed = []
        for j, chip in enumerate(chips):
            for t in range(n_t):
                copy(t, 1 + j, (*chip, c), me).wait_recv()
                fwd = copy(t, 4 + j, (*chip, c), sibling)
                fwd.start()
                passed.append(fwd)
        for t in range(n_t):
            copy(t, 0, sibling, me).wait_recv()
            for j, chip in enumerate(chips):
                copy(t, 4 + j, (*chip, 1 - c), me).wait_recv()
        for cp in first + passed:
            cp.wait_send()
        for cp in mine:
            cp.wait()

    anyspec = pl.BlockSpec(memory_space=pl.ANY)
    return pl.pallas_call(
        body, in_specs=[anyspec] * n_t, out_specs=[anyspec] * n_t, out_shape=out_shapes,
        scratch_shapes=[pltpu.SemaphoreType.DMA((n_t, 7)), pltpu.SemaphoreType.DMA((n_t, 7)), pltpu.SemaphoreType.DMA((n_t,))],
        name="all_gather_weights",
    )(*shards)


RS_KINDS = ("stack", "stack", "rows128", "rows128", "cols512", "cols512", "rows512", "rows512")
RS_LAYER = (0, 1, 0, 1, 0, 1, 0, 1)
RS_LAND = (0, 0, 1, 1, 2, 2, 3, 3)


def _slab2(kind, ref, idx):
    if kind == "stack":
        return ref.at[idx]
    if kind == "rows128":
        return ref.at[pl.ds(pl.multiple_of(idx * 128, 128), 128), :]
    if kind == "rows512":
        return ref.at[pl.ds(pl.multiple_of(idx * 512, 512), 512), :]
    return ref.at[:, pl.ds(pl.multiple_of(idx * 512, 512), 512)]


def _exchange_grads(grads, land_shapes):
    n_t = len(grads)
    n_l = len(land_shapes)

    def body(*refs):
        srcs = refs[:n_t]
        lands = refs[n_t:n_t + n_l]
        send_sems, recv_sems, local_sems = refs[n_t + n_l:]
        x, y, c = _my_place()
        my_idx = _dev_index(x, y, c)
        peers = []
        for r in range(1, N_DEV):
            peers.append((x ^ ((r >> 2) & 1), y ^ ((r >> 1) & 1), c ^ (r & 1)))

        def dst(t, src_idx):
            return lands[RS_LAND[t]].at[src_idx, RS_LAYER[t]]

        def copy(t, r, peer):
            return pltpu.make_async_remote_copy(
                src_ref=_slab2(RS_KINDS[t], srcs[t], _dev_index(*peer)), dst_ref=dst(t, my_idx),
                send_sem=send_sems.at[t, r], recv_sem=recv_sems.at[t, r], device_id=peer, device_id_type=MESH)

        mine = [pltpu.make_async_copy(_slab2(RS_KINDS[t], srcs[t], my_idx), dst(t, my_idx), local_sems.at[t]) for t in range(n_t)]
        for cp in mine:
            cp.start()
        sends = [copy(t, r, peer) for t in range(n_t) for r, peer in enumerate(peers)]
        for cp in sends:
            cp.start()
        for t in range(n_t):
            for r, peer in enumerate(peers):
                pltpu.make_async_remote_copy(
                    src_ref=_slab2(RS_KINDS[t], srcs[t], my_idx), dst_ref=dst(t, _dev_index(*peer)),
                    send_sem=send_sems.at[t, r], recv_sem=recv_sems.at[t, r], device_id=peer, device_id_type=MESH).wait_recv()
        for cp in sends:
            cp.wait_send()
        for cp in mine:
            cp.wait()

    anyspec = pl.BlockSpec(memory_space=pl.ANY)
    return pl.pallas_call(
        body, in_specs=[anyspec] * n_t, out_specs=[anyspec] * n_l, out_shape=land_shapes,
        scratch_shapes=[pltpu.SemaphoreType.DMA((n_t, 7)), pltpu.SemaphoreType.DMA((n_t, 7)), pltpu.SemaphoreType.DMA((n_t,))],
        name="exchange_grads",
    )(*grads)


def _all_reduce_small(part):
    def body(p_ref, o_ref, land, send_sems, recv_sems):
        x, y, c = _my_place()
        my_idx = _dev_index(x, y, c)
        peers = []
        for r in range(1, N_DEV):
            peers.append((x ^ ((r >> 2) & 1), y ^ ((r >> 1) & 1), c ^ (r & 1)))
        land[my_idx] = p_ref[...]
        sends = [pltpu.make_async_remote_copy(src_ref=p_ref, dst_ref=land.at[my_idx], send_sem=send_sems.at[r],
                                              recv_sem=recv_sems.at[r], device_id=peer, device_id_type=MESH)
                 for r, peer in enumerate(peers)]
        for cp in sends:
            cp.start()
        for r, peer in enumerate(peers):
            pltpu.make_async_remote_copy(src_ref=p_ref, dst_ref=land.at[_dev_index(*peer)], send_sem=send_sems.at[r],
                                         recv_sem=recv_sems.at[r], device_id=peer, device_id_type=MESH).wait_recv()
        for cp in sends:
            cp.wait_send()
        tot = land[0]
        for d in range(1, N_DEV):
            tot = tot + land[d]
        o_ref[...] = tot

    vm = pl.BlockSpec(memory_space=pltpu.VMEM)
    return pl.pallas_call(
        body, in_specs=[vm], out_specs=vm, out_shape=SDS((SMALL_ROWS, D), f32),
        scratch_shapes=[pltpu.VMEM((N_DEV, SMALL_ROWS, D), f32), pltpu.SemaphoreType.DMA((7,)), pltpu.SemaphoreType.DMA((7,))],
        name="all_reduce_small",
    )(part)


def _plain(tm, tn):
    return pl.BlockSpec((tm, tn), lambda i, j, k: (i, j))


def _rowblk(tm, width):
    return pl.BlockSpec((tm, width), lambda i, j, k: (i, 0))


def _store_epi(dtype):
    def epi(acc, i, j, ex, outs):
        outs[0][...] = acc.astype(dtype)
    return epi


def _layer_fwd(l, x, p, bias):
    h1 = _rms_fwd("rms_mix", x, p["mix_norm_g"], l)

    def inproj_epi(acc, i, j, ex, outs):
        outs[0][...] = acc[:, 0:768]
        outs[1][...] = acc[:, 768:1280]
        outs[2][...] = acc[:, 1280:2304]
        outs[3][...] = acc[:, 2304:2432]

    tm = 256
    qkv, z, xbc, dt = _matmul(
        "in_proj", "nn", h1, p["w_in"], layer=l, tm=tm, tn=D_IN_PAD, tk=D,
        out_shape=[SDS((S, 768), f32), SDS((S, 512), f32), SDS((S, 1024), f32), SDS((S, 128), f32)],
        out_specs=[_rowblk(tm, 768), _rowblk(tm, 512), _rowblk(tm, 1024), _rowblk(tm, 128)], epilogue=inproj_epi)
    attn = _attn_fwd(qkv, p["q_gain"], p["k_gain"], p["sinks"], bias, l)
    xact = _conv_fwd(xbc, p["conv_w"], p["conv_b"], l)
    mix, hs = _ssd_fwd(xact, z, dt, attn, p["dt_bias"], p["a_log"], p["d_skip"], p["ssm_norm_g"], l)

    def resid_epi(acc, i, j, ex, outs):
        outs[0][...] = ex[0][...] + acc

    x_mid = _matmul("out_proj", "nn", mix, p["w_out"], layer=l, tm=512, tn=D, tk=512, out_shape=SDS((S, D), f32),
                    out_specs=_plain(512, D), epilogue=resid_epi, extras=(x,), extra_specs=(_plain(512, D),))
    h2 = _rms_fwd("rms_mlp", x_mid, p["mlp_norm_g"], l)

    def up_epi(acc, i, j, ex, outs):
        r = jnp.maximum(acc, 0.0)
        outs[0][...] = r.astype(bf16)
        outs[1][...] = (r * r).astype(bf16)

    r_act, a_act = _matmul("mlp_up", "nn", h2, p["w_up"], layer=l, tm=512, tn=512, tk=D,
                           out_shape=[SDS((S, D_FF), bf16)] * 2, out_specs=[_plain(512, 512)] * 2, epilogue=up_epi)
    x_out = _matmul("mlp_down", "nn", a_act, p["w_down"], layer=l, tm=512, tn=D, tk=512, out_shape=SDS((S, D), f32),
                    out_specs=_plain(512, D), epilogue=resid_epi, extras=(x_mid,), extra_specs=(_plain(512, D),))
    saved = dict(x=x, h1=h1, qkv=qkv, z=z, xbc=xbc, dt=dt, xact=xact, mix=mix, hs=hs, x_mid=x_mid, h2=h2, r=r_act, a=a_act)
    return x_out, saved


def _layer_bwd(l, dx_out, sv, p, bias):
    def du_epi(acc, i, j, ex, outs):
        outs[0][...] = (acc * (2.0 * ex[0][...].astype(f32))).astype(bf16)

    du = _matmul("mlp_da", "nt", dx_out, p["w_down"], layer=l, tm=512, tn=512, tk=D, out_shape=SDS((S, D_FF), bf16),
                 out_specs=_plain(512, 512), epilogue=du_epi, extras=(sv["r"],), extra_specs=(_plain(512, 512),))
    dw_down = _matmul("dw_down", "tn", sv["a"], dx_out, tm=1024, tn=D, tk=512, out_shape=SDS((D_FF, D), bf16),
                      out_specs=_plain(1024, D), epilogue=_store_epi(bf16))
    dw_up = _matmul("dw_up", "tn", sv["h2"], du, tm=D, tn=512, tk=512, out_shape=SDS((D, D_FF), bf16),
                    out_specs=_plain(D, 512), epilogue=_store_epi(bf16))
    gfull = pl.BlockSpec((DEPTH, D), lambda i, j, k: (0, 0))
    grow = pl.BlockSpec((1, D), lambda i, j, k: (0, 0))
    dx_mid, dg_mlp = _matmul(
        "mlp_dh", "nt", du, p["w_up"], layer=l, tm=512, tn=D, tk=512, out_shape=[SDS((S, D), f32), SDS((1, D), f32)],
        out_specs=[_plain(512, D), grow], epilogue=_rms_bwd_epilogue(l),
        extras=(sv["x_mid"], p["mlp_norm_g"], dx_out), extra_specs=(_plain(512, D), gfull, _plain(512, D)))
    dmix = _matmul("out_proj_da", "nt", dx_mid, p["w_out"], layer=l, tm=512, tn=512, tk=D, out_shape=SDS((S, D), f32),
                   out_specs=_plain(512, 512), epilogue=_store_epi(f32))
    dw_out = _matmul("dw_out", "tn", sv["mix"], dx_mid, tm=D, tn=512, tk=512, out_shape=SDS((D, D), bf16),
                     out_specs=_plain(D, 512), epilogue=_store_epi(bf16))
    dz, dxact, ddt, dsm_ssd = _ssd_bwd(sv["xact"], sv["z"], sv["dt"], dmix, sv["hs"], p["dt_bias"], p["a_log"],
                                       p["d_skip"], p["ssm_norm_g"], l)
    dxbc, dconv_w, dconv_b = _conv_bwd(sv["xbc"], dxact, p["conv_w"], p["conv_b"], l)
    dqkv, dbias, dsm_attn = _attn_bwd(sv["qkv"], dmix, p["q_gain"], p["k_gain"], p["sinks"], bias, l)
    dproj = _pack_dproj(dqkv, dz, dxbc, ddt)
    dw_in = _matmul("dw_in", "tn", sv["h1"], dproj, tm=D, tn=640, tk=512, out_shape=SDS((D, D_IN_PAD), bf16),
                    out_specs=_plain(D, 640), epilogue=_store_epi(bf16))
    dx, dg_mix = _matmul(
        "in_proj_dh", "nt", dproj, p["w_in"], layer=l, tm=256, tn=D, tk=D_IN_PAD, out_shape=[SDS((S, D), f32), SDS((1, D), f32)],
        out_specs=[_plain(256, D), grow], epilogue=_rms_bwd_epilogue(l),
        extras=(sv["x"], p["mix_norm_g"], dx_mid), extra_specs=(_plain(256, D), gfull, _plain(256, D)))
    grads = dict(w_in=dw_in, w_out=dw_out, w_up=dw_up, w_down=dw_down)
    small = dict(mix_norm_g=dg_mix, mlp_norm_g=dg_mlp, conv_w=dconv_w[0:4], conv_b=dconv_b[0:1], ssm_norm_g=dsm_ssd[0:1],
                 dt_bias=dsm_ssd[1:2, 0:8], a_log=dsm_ssd[2:3, 0:8], d_skip=dsm_ssd[3:4, 0:8],
                 q_gain=dsm_attn[0:1, 0:64], k_gain=dsm_attn[1:2, 0:64], sinks=dsm_attn[2:3, 0:8], dbias=dbias)
    return dx, grads, small


def _local_step(x, tgt, p):
    onehot_t = jnp.asarray(_onehot_buckets())
    bias = _bias_build(p["rel_bias"].T, onehot_t).reshape(NQ, BLK, 2 * BLK)
    saved = []
    h = x
    for l in range(DEPTH):
        h, sv = _layer_fwd(l, h, p, bias)
        saved.append(sv)
    dy, loss = _loss_kernel(h, tgt)
    grads = [None] * DEPTH
    smalls = [None] * DEPTH
    dx = dy
    for l in reversed(range(DEPTH)):
        dx, grads[l], smalls[l] = _layer_bwd(l, dx, saved[l], p, bias)
    drel_t = _bias_grad(smalls[0]["dbias"].reshape(NQ, -1), smalls[1]["dbias"].reshape(NQ, -1), onehot_t)

    def row(v, width=D):
        v = v.reshape(1, -1)
        return jnp.pad(v, ((0, 0), (0, width - v.shape[1])))

    def misc_row(sm):
        parts = [sm["q_gain"], sm["k_gain"], row(sm["sinks"], 128), row(sm["dt_bias"], 128), row(sm["a_log"], 128),
                 row(sm["d_skip"], 128)]
        return row(jnp.concatenate([q.reshape(1, -1) for q in parts], axis=1))

    rows = []
    rows += [smalls[l]["mix_norm_g"] for l in range(DEPTH)]
    rows += [smalls[l]["mlp_norm_g"] for l in range(DEPTH)]
    rows += [smalls[l]["conv_b"] for l in range(DEPTH)]
    rows += [row(smalls[l]["ssm_norm_g"]) for l in range(DEPTH)]
    rows += [misc_row(smalls[l]) for l in range(DEPTH)]
    rows += [row(drel_t.T)]
    rows += [smalls[l]["conv_w"] for l in range(DEPTH)]
    rows += [row(loss[0:1, 0:1])]
    rows += [jnp.zeros((SMALL_ROWS - ROW_LOSS - 1, D), f32)]
    small_packed = jnp.concatenate(rows, axis=0)
    return dx, grads, small_packed


def _pack_small_params(t):
    def row(v, width=D):
        v = v.reshape(1, -1)
        return jnp.pad(v, ((0, 0), (0, width - v.shape[1])))

    rows = [t["mix_norm_g"], t["mlp_norm_g"], t["conv_b"]]
    rows += [jnp.pad(t["ssm_norm_g"], ((0, 0), (0, D - D_SSM)))]
    for l in range(DEPTH):
        parts = [t["q_gain"][l], t["k_gain"][l], row(t["sinks"][l], 128), row(t["dt_bias"][l], 128), row(t["a_log"][l], 128),
                 row(t["d_skip"][l], 128)]
        rows.append(row(jnp.concatenate([q.reshape(1, -1) for q in parts], axis=1)))
    rows += [row(t["rel_bias"])]
    rows += [jnp.zeros((SMALL_ROWS - ROW_CONVW, D), f32)]
    return jnp.concatenate(rows, axis=0)


def _unpack_small(buf):
    out = dict(mix_norm_g=buf[ROW_MIXG:ROW_MIXG + 2], mlp_norm_g=buf[ROW_MLPG:ROW_MLPG + 2], conv_b=buf[ROW_CONVB:ROW_CONVB + 2],
               ssm_norm_g=buf[ROW_SSMG:ROW_SSMG + 2, 0:D_SSM])
    misc = buf[ROW_MISC:ROW_MISC + 2]
    out.update(q_gain=misc[:, 0:64], k_gain=misc[:, 64:128], sinks=misc[:, 128:136], dt_bias=misc[:, 256:264],
               a_log=misc[:, 384:392], d_skip=misc[:, 512:520])
    out["rel_bias"] = buf[ROW_RELB, 0:256].reshape(N_BUCKETS, NQ)
    return out


SMALL_NAMES = ("mix_norm_g", "q_gain", "k_gain", "sinks", "rel_bias", "conv_b", "dt_bias", "a_log", "d_skip", "ssm_norm_g", "mlp_norm_g")
WEIGHT_ORDER = ("mix_norm_g", "w_in", "q_gain", "k_gain", "sinks", "rel_bias", "conv_w", "conv_b", "dt_bias", "a_log", "d_skip",
                "ssm_norm_g", "w_out", "mlp_norm_g", "w_up", "w_down")


def kernel(x, mix_norm_g, w_in, q_gain, k_gain, sinks, rel_bias, conv_w, conv_b, dt_bias, a_log, d_skip, ssm_norm_g, w_out, mlp_norm_g, w_up, w_down, loss_target, m_mix_norm_g, m_w_in, m_q_gain, m_k_gain, m_sinks, m_rel_bias, m_conv_w, m_conv_b, m_dt_bias, m_a_log, m_d_skip, m_ssm_norm_g, m_w_out, m_mlp_norm_g, m_w_up, m_w_down, v_mix_norm_g, v_w_in, v_q_gain, v_k_gain, v_sinks, v_rel_bias, v_conv_w, v_conv_b, v_dt_bias, v_a_log, v_d_skip, v_ssm_norm_g, v_w_out, v_mlp_norm_g, v_w_up, v_w_down):
    w = dict(mix_norm_g=mix_norm_g, w_in=w_in, q_gain=q_gain, k_gain=k_gain, sinks=sinks, rel_bias=rel_bias, conv_w=conv_w,
             conv_b=conv_b, dt_bias=dt_bias, a_log=a_log, d_skip=d_skip, ssm_norm_g=ssm_norm_g, w_out=w_out,
             mlp_norm_g=mlp_norm_g, w_up=w_up, w_down=w_down)
    m = dict(mix_norm_g=m_mix_norm_g, w_in=m_w_in, q_gain=m_q_gain, k_gain=m_k_gain, sinks=m_sinks, rel_bias=m_rel_bias,
             conv_w=m_conv_w, conv_b=m_conv_b, dt_bias=m_dt_bias, a_log=m_a_log, d_skip=m_d_skip, ssm_norm_g=m_ssm_norm_g,
             w_out=m_w_out, mlp_norm_g=m_mlp_norm_g, w_up=m_w_up, w_down=m_w_down)
    v = dict(mix_norm_g=v_mix_norm_g, w_in=v_w_in, q_gain=v_q_gain, k_gain=v_k_gain, sinks=v_sinks, rel_bias=v_rel_bias,
             conv_w=v_conv_w, conv_b=v_conv_b, dt_bias=v_dt_bias, a_log=v_a_log, d_skip=v_d_skip, ssm_norm_g=v_ssm_norm_g,
             w_out=v_w_out, mlp_norm_g=v_mlp_norm_g, w_up=v_w_up, w_down=v_w_down)
    my_idx = _dev_index(*_my_place())
    sh_in = D_IN // N_DEV

    win_b = _cast_bf16("cast_w_in", w_in.reshape(DEPTH * D, sh_in)).reshape(DEPTH, D, sh_in)
    wout_b = _cast_bf16("cast_w_out", w_out.reshape(DEPTH * 128, D)).reshape(DEPTH, 128, D)
    wup_b = _cast_bf16("cast_w_up", w_up.reshape(DEPTH * D, 512)).reshape(DEPTH, D, 512)
    wdown_b = _cast_bf16("cast_w_down", w_down.reshape(DEPTH * 512, D)).reshape(DEPTH, 512, D)
    g_in, g_out, g_up, g_down, g_conv = _all_gather_weights(
        [win_b, wout_b, wup_b, wdown_b, conv_w],
        [SDS((N_DEV, DEPTH, D, sh_in), bf16), SDS((DEPTH, D, D), bf16), SDS((DEPTH, D, D_FF), bf16), SDS((DEPTH, D_FF, D), bf16),
         SDS((N_DEV, DEPTH, 4, 128), f32)])
    w_in_full = jnp.transpose(g_in, (1, 2, 0, 3)).reshape(DEPTH, D, D_IN)
    w_in_pad = jnp.pad(w_in_full, ((0, 0), (0, 0), (0, D_IN_PAD - D_IN)))
    conv_w_full = jnp.transpose(g_conv, (1, 2, 0, 3)).reshape(DEPTH, 4, D_CONV)
    p = dict(w, w_in=w_in_pad, w_out=g_out, w_up=g_up, w_down=g_down, conv_w=conv_w_full)

    dx, grads, small_part = _local_step(x.reshape(S, D), loss_target.reshape(S, D), p)

    def win_slabs(g):
        return jnp.transpose(g[:, :D_IN].reshape(D, N_DEV, sh_in), (1, 0, 2))

    lands = _exchange_grads(
        [win_slabs(grads[0]["w_in"]), win_slabs(grads[1]["w_in"]), grads[0]["w_out"], grads[1]["w_out"],
         grads[0]["w_up"], grads[1]["w_up"], grads[0]["w_down"], grads[1]["w_down"]],
        [SDS((N_DEV, DEPTH, D, sh_in), bf16), SDS((N_DEV, DEPTH, 128, D), bf16), SDS((N_DEV, DEPTH, D, 512), bf16),
         SDS((N_DEV, DEPTH, 512, D), bf16)])
    res = {}
    for name, land, tr in (("w_in", lands[0], 256), ("w_out", lands[1], 256), ("w_up", lands[2], 256), ("w_down", lands[3], 256)):
        shp = w[name].shape
        rows, cols = shp[0] * shp[1], shp[2]
        outs = _adamw("adamw_" + name, w[name].reshape(rows, cols), m[name].reshape(rows, cols), v[name].reshape(rows, cols),
                      land.reshape(N_DEV, rows, cols), tr)
        res[name] = [o.reshape(shp) for o in outs]

    small_sum = _all_reduce_small(small_part)
    loss = small_sum[ROW_LOSS, 0]
    outs = _adamw("adamw_small", _pack_small_params(w), _pack_small_params(m), _pack_small_params(v),
                  small_sum.reshape(1, SMALL_ROWS, D), SMALL_ROWS)
    unpacked = [_unpack_small(o) for o in outs]
    for name in SMALL_NAMES:
        res[name] = [u[name] for u in unpacked]
    gconv = lax.dynamic_slice_in_dim(small_sum[ROW_CONVW:ROW_CONVW + 8], my_idx * 128, 128, axis=1)
    outs = _adamw("adamw_conv_w", conv_w.reshape(8, 128), m["conv_w"].reshape(8, 128), v["conv_w"].reshape(8, 128),
                  gconv.reshape(1, 8, 128), 8)
    res["conv_w"] = [o.reshape(DEPTH, 4, 128) for o in outs]

    result = [loss, dx.reshape(1, S, D)]
    for k in range(4):
        result += [res[name][k] for name in WEIGHT_ORDER]
    return tuple(result)
```

```python
import functools
import math

import numpy as np
import jax
import jax.numpy as jnp
from jax import lax
from jax.experimental import pallas as pl
from jax.experimental.pallas import tpu as pltpu

f32 = jnp.float32
bf16 = jnp.bfloat16
SDS = jax.ShapeDtypeStruct
MESH = pl.DeviceIdType.MESH
HIGHEST = lax.Precision.HIGHEST

S = 2048
D = 1024
DEPTH = 2
BLK = 128
NBLK = S // BLK
HD = 64
NQ = 8
NKV = 2
NSSM = 8
NGRP = 2
NSTATE = 128
D_ATTN = 512
D_SSM = 512
D_CONV = 1024
D_FF = 4096
D_IN = 2312
D_IN_PAD = 2560
N_BUCKETS = 32
EPS = 1e-6
N_DEV = 8
VMEM_LIMIT = 48 * 1024 * 1024

ADAM_LR = 0.001
ADAM_B1 = 0.9
ADAM_B2 = 0.999
ADAM_EPS = 1e-08
ADAM_WD = 0.01
ADAM_STEP = 10

NT_DIMS = (((1,), (1,)), ((), ()))
TN_DIMS = (((0,), (0,)), ((), ()))
NN_DIMS = (((1,), (0,)), ((), ()))

ROW_MIXG = 0
ROW_MLPG = 2
ROW_CONVB = 4
ROW_SSMG = 6
ROW_MISC = 8
ROW_RELB = 10
ROW_CONVW = 18
ROW_LOSS = 26
SMALL_ROWS = 32
LANE_QG, LANE_KG, LANE_SINK, LANE_DTB, LANE_ALOG, LANE_DSKIP = 0, 64, 128, 256, 384, 512


def _dot(a, b, dims):
    return lax.dot_general(a, b, dims, preferred_element_type=f32)


def _cparams(n_axes):
    return pltpu.CompilerParams(dimension_semantics=("arbitrary",) * n_axes, vmem_limit_bytes=VMEM_LIMIT)


def _sum11(v):
    return jnp.sum(jnp.sum(v, axis=1, keepdims=True), axis=0, keepdims=True)


def _sigmoid(v):
    return 1.0 / (1.0 + jnp.exp(-v))


ANY_SPEC = pl.BlockSpec(memory_space=pl.ANY)


def _matmul(name, mode, a, b, *, layer=0, tm, tn, tk, out_shape, out_specs, epilogue, extras=(), extra_specs=(), deps=()):
    extras = tuple(extras) + tuple(deps)
    extra_specs = tuple(extra_specs) + (ANY_SPEC,) * len(deps)
    if mode == "tn":
        t_dim, m_dim = a.shape
        n_dim = b.shape[1]
        grid = (m_dim // tm, n_dim // tn, t_dim // tk)
        a_spec = pl.BlockSpec((tk, tm), lambda i, j, k: (k, i))
        b_spec = pl.BlockSpec((tk, tn), lambda i, j, k: (k, j))
        dims = TN_DIMS
    elif mode == "nn":
        m_dim, k_dim = a.shape
        n_dim = b.shape[-1]
        grid = (m_dim // tm, n_dim // tn, k_dim // tk)
        a_spec = pl.BlockSpec((tm, tk), lambda i, j, k: (i, k))
        b_spec = pl.BlockSpec((None, tk, tn), lambda i, j, k: (layer, k, j))
        dims = NN_DIMS
    else:
        m_dim, k_dim = a.shape
        n_dim = b.shape[-2]
        grid = (m_dim // tm, n_dim // tn, k_dim // tk)
        a_spec = pl.BlockSpec((tm, tk), lambda i, j, k: (i, k))
        b_spec = pl.BlockSpec((None, tn, tk), lambda i, j, k: (layer, j, k))
        dims = NT_DIMS
    nk = grid[2]
    n_ex = len(extras)

    def body(a_ref, b_ref, *rest):
        ex = rest[:n_ex - len(deps)]
        outs = rest[n_ex:-1]
        acc = rest[-1]
        i = pl.program_id(0)
        j = pl.program_id(1)
        k = pl.program_id(2)
        part = _dot(a_ref[...].astype(bf16), b_ref[...].astype(bf16), dims)
        if nk == 1:
            epilogue(part, i, j, ex, outs)
        else:
            @pl.when(k == 0)
            def _():
                acc[...] = part

            @pl.when(k > 0)
            def _():
                acc[...] += part

            @pl.when(k == nk - 1)
            def _():
                epilogue(acc[...], i, j, ex, outs)

    return pl.pallas_call(
        body, grid=grid, in_specs=[a_spec, b_spec, *extra_specs], out_specs=out_specs, out_shape=out_shape,
        scratch_shapes=[pltpu.VMEM((tm, tn) if nk > 1 else (8, 128), f32)], name=name, compiler_params=_cparams(3),
    )(a, b, *extras)


def _rms_bwd_epilogue(layer):
    def epi(acc, i, j, ex, outs):
        x_ref, g_ref, dres_ref = ex
        dx_ref, dg_ref = outs
        xv = x_ref[...]
        r = lax.rsqrt(jnp.mean(xv * xv, axis=-1, keepdims=True) + EPS)
        xhat = xv * r
        w = acc * g_ref[layer:layer + 1, :]
        dx_ref[...] = dres_ref[...] + r * (w - xhat * jnp.mean(xhat * w, axis=-1, keepdims=True))
        dg = jnp.sum(acc * xhat, axis=0, keepdims=True)

        @pl.when(i == 0)
        def _():
            dg_ref[...] = dg

        @pl.when(i > 0)
        def _():
            dg_ref[...] += dg
    return epi


def _rms_fwd(name, x, g, layer, deps=()):
    tr = 512

    def body(x_ref, g_ref, *rest):
        h_ref = rest[-1]
        xv = x_ref[...]
        r = lax.rsqrt(jnp.mean(xv * xv, axis=-1, keepdims=True) + EPS)
        h_ref[...] = (xv * r * g_ref[layer:layer + 1, :]).astype(bf16)

    return pl.pallas_call(
        body, grid=(S // tr,),
        in_specs=[pl.BlockSpec((tr, D), lambda i: (i, 0)), pl.BlockSpec((DEPTH, D), lambda i: (0, 0))] + [ANY_SPEC] * len(deps),
        out_specs=pl.BlockSpec((tr, D), lambda i: (i, 0)), out_shape=SDS((S, D), bf16), name=name,
        compiler_params=_cparams(1),
    )(x, g, *deps)


def _loss_kernel(y, tgt):
    tr = 512

    def body(y_ref, t_ref, dy_ref, loss_ref):
        err = y_ref[...] - t_ref[...]
        dy_ref[...] = err * (1.0 / D)
        part = 0.5 * jnp.sum(jnp.mean(err * err, axis=-1, keepdims=True), axis=0, keepdims=True)

        @pl.when(pl.program_id(0) == 0)
        def _():
            loss_ref[...] = jnp.zeros_like(loss_ref)

        loss_ref[...] += jnp.broadcast_to(part, loss_ref.shape)

    return pl.pallas_call(
        body, grid=(S // tr,),
        in_specs=[pl.BlockSpec((tr, D), lambda i: (i, 0)), pl.BlockSpec((tr, D), lambda i: (i, 0))],
        out_specs=[pl.BlockSpec((tr, D), lambda i: (i, 0)), pl.BlockSpec((1, 128), lambda i: (0, 0))],
        out_shape=[SDS((S, D), f32), SDS((1, 128), f32)], name="loss", compiler_params=_cparams(1),
    )(y, tgt)


def _pack_dproj(dqkv, dz, dxbc, ddt):
    tr = 256

    def body(a_ref, b_ref, c_ref, d_ref, o_ref):
        o_ref[:, 0:768] = a_ref[...].astype(bf16)
        o_ref[:, 768:1280] = b_ref[...].astype(bf16)
        o_ref[:, 1280:2304] = c_ref[...].astype(bf16)
        o_ref[:, 2304:2432] = d_ref[...].astype(bf16)
        o_ref[:, 2432:D_IN_PAD] = jnp.zeros((tr, D_IN_PAD - 2432), bf16)

    return pl.pallas_call(
        body, grid=(S // tr,),
        in_specs=[pl.BlockSpec((tr, 768), lambda i: (i, 0)), pl.BlockSpec((tr, 512), lambda i: (i, 0)),
                  pl.BlockSpec((tr, 1024), lambda i: (i, 0)), pl.BlockSpec((tr, 128), lambda i: (i, 0))],
        out_specs=pl.BlockSpec((tr, D_IN_PAD), lambda i: (i, 0)), out_shape=SDS((S, D_IN_PAD), bf16),
        name="pack_dproj", compiler_params=_cparams(1),
    )(dqkv, dz, dxbc, ddt)


def _cast_bf16(name, w):
    _, rows, cols = w.shape
    tr = min(rows, 256)

    def body(w_ref, o0_ref, o1_ref):
        o0_ref[...] = w_ref[0].astype(bf16)
        o1_ref[...] = w_ref[1].astype(bf16)

    blk = pl.BlockSpec((tr, cols), lambda i: (i, 0))
    return pl.pallas_call(
        body, grid=(rows // tr,), in_specs=[pl.BlockSpec((DEPTH, tr, cols), lambda i: (0, i, 0))],
        out_specs=[blk, blk], out_shape=[SDS((rows, cols), bf16)] * 2, name=name, compiler_params=_cparams(1),
    )(w)


def _adamw_math(w, m, v, g):
    m_new = ADAM_B1 * m + (1.0 - ADAM_B1) * g
    v_new = ADAM_B2 * v + (1.0 - ADAM_B2) * (g * g)
    m_hat = m_new / (1.0 - ADAM_B1 ** ADAM_STEP)
    v_hat = v_new / (1.0 - ADAM_B2 ** ADAM_STEP)
    delta = -ADAM_LR * (m_hat / (jnp.sqrt(v_hat) + ADAM_EPS) + ADAM_WD * w)
    return delta, m_new, v_new


def _adamw_layer(name, kind, layer, w, m, v, land, g_full, my_idx, prev, tr):
    rows2, cols = w.shape
    rows = rows2 // DEPTH
    nblk = rows // tr
    if kind == "stack":
        own_spec = pl.BlockSpec((None, tr, cols), lambda i, idx: (idx[0], i, 0))
    elif kind == "cols512":
        own_spec = pl.BlockSpec((tr, cols), lambda i, idx: (i, idx[0]))
    else:
        own_spec = pl.BlockSpec((tr, cols), lambda i, idx: (idx[0] * nblk + i, 0))
    n_prev = 0 if prev is None else 4

    def body(idx_ref, w_ref, m_ref, v_ref, land_ref, own_ref, *rest):
        g_ref, d_ref, mo_ref, vo_ref = rest[n_prev:]
        me = idx_ref[0]
        g = None
        for p in range(N_DEV):
            part = jnp.where(me == p, own_ref[...], land_ref[p]).astype(f32)
            g = part if g is None else g + part
        delta, m_new, v_new = _adamw_math(w_ref[...], m_ref[...], v_ref[...], g)
        g_ref[...] = g
        d_ref[...] = delta
        mo_ref[...] = m_new
        vo_ref[...] = v_new

    blk = pl.BlockSpec((tr, cols), lambda i, idx: (layer * nblk + i, 0))
    grid_spec = pltpu.PrefetchScalarGridSpec(
        num_scalar_prefetch=1, grid=(nblk,),
        in_specs=[blk, blk, blk, pl.BlockSpec((N_DEV, tr, cols), lambda i, idx: (0, i, 0)), own_spec] + [ANY_SPEC] * n_prev,
        out_specs=[blk, blk, blk, blk])
    aliases = {} if prev is None else {6 + k: k for k in range(4)}
    return pl.pallas_call(
        body, grid_spec=grid_spec, out_shape=[SDS((rows2, cols), f32)] * 4, name=name, input_output_aliases=aliases,
        compiler_params=_cparams(1),
    )(my_idx, w, m, v, land, g_full, *([] if prev is None else prev))


def _bucket_table():
    qi = np.arange(BLK)[:, None]
    kj = np.arange(2 * BLK)[None, :]
    dist = qi + BLK - kj
    dcl = np.clip(dist, 0, None)
    max_exact = N_BUCKETS // 2
    d_f = np.maximum(dcl, 1).astype(np.float32)
    large = max_exact + (np.log(d_f / np.float32(max_exact)) / np.float32(math.log(128 / max_exact))
                         * np.float32(N_BUCKETS - max_exact)).astype(np.int32)
    large = np.minimum(large, N_BUCKETS - 1)
    bucket = np.where(dcl < max_exact, dcl, large)
    in_window = (dist >= 0) & (dist < BLK)
    return bucket.astype(np.int32), in_window


def _onehot_buckets():
    bucket, _ = _bucket_table()
    oh = (bucket.reshape(-1)[None, :] == np.arange(N_BUCKETS)[:, None]).astype(np.float32)
    return oh


def _bias_build(rel_bias_t, onehot_t):
    def body(r_ref, o_ref, out_ref):
        out_ref[...] = jnp.dot(r_ref[...], o_ref[...], preferred_element_type=f32, precision=HIGHEST)

    tn = 4096
    return pl.pallas_call(
        body, grid=(BLK * 2 * BLK // tn,),
        in_specs=[pl.BlockSpec((NQ, N_BUCKETS), lambda i: (0, 0)), pl.BlockSpec((N_BUCKETS, tn), lambda i: (0, i))],
        out_specs=pl.BlockSpec((NQ, tn), lambda i: (0, i)), out_shape=SDS((NQ, BLK * 2 * BLK), f32), name="bias_build",
        compiler_params=_cparams(1),
    )(rel_bias_t, onehot_t)


def _bias_grad(dbias0, dbias1, onehot_t):
    tn = 4096
    nsteps = BLK * 2 * BLK // tn

    def body(a_ref, b_ref, o_ref, out_ref):
        part = lax.dot_general(a_ref[...] + b_ref[...], o_ref[...], NT_DIMS, preferred_element_type=f32, precision=HIGHEST)

        @pl.when(pl.program_id(0) == 0)
        def _():
            out_ref[...] = part

        @pl.when(pl.program_id(0) > 0)
        def _():
            out_ref[...] += part

    return pl.pallas_call(
        body, grid=(nsteps,),
        in_specs=[pl.BlockSpec((NQ, tn), lambda i: (0, i)), pl.BlockSpec((NQ, tn), lambda i: (0, i)),
                  pl.BlockSpec((N_BUCKETS, tn), lambda i: (0, i))],
        out_specs=pl.BlockSpec((NQ, N_BUCKETS), lambda i: (0, 0)), out_shape=SDS((NQ, N_BUCKETS), f32), name="bias_grad",
        compiler_params=_cparams(1),
    )(dbias0, dbias1, onehot_t)


def _attn_mask(n):
    qi = lax.broadcasted_iota(jnp.int32, (BLK, 2 * BLK), 0)
    kj = lax.broadcasted_iota(jnp.int32, (BLK, 2 * BLK), 1)
    dist = qi + BLK - kj
    first_key = jnp.where(n > 0, 0, BLK)
    return (dist >= 0) & (dist < BLK) & (kj >= first_key)


def _head_norm(t, gain):
    r = lax.rsqrt(jnp.mean(t * t, axis=-1, keepdims=True) + EPS)
    that = t * r
    return that, r, that * gain


def _softmax_with_sink(s, sink):
    m = jnp.maximum(jnp.max(s, axis=-1, keepdims=True), sink)
    p = jnp.exp(s - m)
    psink = jnp.exp(sink - m)
    inv = 1.0 / (jnp.sum(p, axis=-1, keepdims=True) + psink)
    return p * inv, psink * inv


def _attn_fwd(qkv, q_gain, k_gain, sinks, bias, layer):
    def body(q_ref, kc_ref, kp_ref, vc_ref, vp_ref, qg_ref, kg_ref, sk_ref, bias_ref, o_ref):
        n = pl.program_id(0)
        mask = _attn_mask(n)
        qg = qg_ref[layer:layer + 1, :]
        kg = kg_ref[layer:layer + 1, :]
        for j in range(NKV):
            cols = pl.ds(HD * j, HD)
            kb = jnp.concatenate([kp_ref[:, cols], kc_ref[:, cols]], axis=0)
            vb = jnp.concatenate([vp_ref[:, cols], vc_ref[:, cols]], axis=0).astype(bf16)
            kn = _head_norm(kb, kg)[2].astype(bf16)
            for g in range(NQ // NKV):
                h = j * (NQ // NKV) + g
                qn = _head_norm(q_ref[:, pl.ds(HD * h, HD)], qg)[2].astype(bf16)
                s = _dot(qn, kn, NT_DIMS) * (HD ** -0.5) + bias_ref[h]
                s = jnp.where(mask, s, -jnp.inf)
                p, _ = _softmax_with_sink(s, sk_ref[layer:layer + 1, h:h + 1])
                o_ref[:, pl.ds(HD * h, HD)] = _dot(p.astype(bf16), vb, NN_DIMS).astype(bf16)

    prev = lambda n: jnp.maximum(n - 1, 0)
    small = lambda shape: pl.BlockSpec(shape, lambda n: (0,) * len(shape))
    return pl.pallas_call(
        body, grid=(NBLK,),
        in_specs=[pl.BlockSpec((BLK, D_ATTN), lambda n: (n, 0)),
                  pl.BlockSpec((BLK, 128), lambda n: (n, 4)), pl.BlockSpec((BLK, 128), lambda n: (prev(n), 4)),
                  pl.BlockSpec((BLK, 128), lambda n: (n, 5)), pl.BlockSpec((BLK, 128), lambda n: (prev(n), 5)),
                  small((DEPTH, HD)), small((DEPTH, HD)), small((DEPTH, NQ)), small((NQ, BLK, 2 * BLK))],
        out_specs=pl.BlockSpec((BLK, D_ATTN), lambda n: (n, 0)), out_shape=SDS((S, D_ATTN), bf16),
        name="attn_fwd", compiler_params=_cparams(1),
    )(qkv, qkv, qkv, qkv, qkv, q_gain, k_gain, sinks, bias)


def _attn_bwd(qkv, dmix, q_gain, k_gain, sinks, bias, layer):
    def body(q_ref, kc_ref, kp_ref, vc_ref, vp_ref, do_ref, qg_ref, kg_ref, sk_ref, bias_ref,
             dqkv_ref, dbias_ref, dsm_ref, carry):
        i = pl.program_id(0)
        n = NBLK - 1 - i
        mask = _attn_mask(n)
        qg = qg_ref[layer:layer + 1, :]
        kg = kg_ref[layer:layer + 1, :]
        lane = lax.broadcasted_iota(jnp.int32, (1, 128), 1)

        @pl.when(i == 0)
        def _():
            carry[...] = jnp.zeros_like(carry)
            dbias_ref[...] = jnp.zeros_like(dbias_ref)
            dsm_ref[...] = jnp.zeros_like(dsm_ref)

        dqg = jnp.zeros((1, HD), f32)
        dkg = jnp.zeros((1, HD), f32)
        dsink = jnp.zeros((1, 128), f32)
        for j in range(NKV):
            cols = pl.ds(HD * j, HD)
            kb = jnp.concatenate([kp_ref[:, cols], kc_ref[:, cols]], axis=0)
            vb = jnp.concatenate([vp_ref[:, cols], vc_ref[:, cols]], axis=0).astype(bf16)
            khat, rk, kn = _head_norm(kb, kg)
            kn_b = kn.astype(bf16)
            dkn = jnp.zeros((2 * BLK, HD), f32)
            dv = jnp.zeros((2 * BLK, HD), f32)
            for g in range(NQ // NKV):
                h = j * (NQ // NKV) + g
                qhat, rq, qn = _head_norm(q_ref[:, pl.ds(HD * h, HD)], qg)
                qn_b = qn.astype(bf16)
                s = _dot(qn_b, kn_b, NT_DIMS) * (HD ** -0.5) + bias_ref[h]
                s = jnp.where(mask, s, -jnp.inf)
                p, psink = _softmax_with_sink(s, sk_ref[layer:layer + 1, h:h + 1])
                do_b = do_ref[:, pl.ds(HD * h, HD)].astype(bf16)
                dp = _dot(do_b, vb, NT_DIMS)
                delta = jnp.sum(p * dp, axis=-1, keepdims=True)
                ds = p * (dp - delta)
                dsink = dsink + jnp.where(lane == h, -_sum11(psink * delta), 0.0)
                dbias_ref[h] += ds
                ds_b = (ds * (HD ** -0.5)).astype(bf16)
                dqn = _dot(ds_b, kn_b, NN_DIMS)
                dkn = dkn + _dot(ds_b, qn_b, TN_DIMS)
                dv = dv + _dot(p.astype(bf16), do_b, TN_DIMS)
                w = dqn * qg
                dqkv_ref[:, pl.ds(HD * h, HD)] = rq * (w - qhat * jnp.mean(qhat * w, axis=-1, keepdims=True))
                dqg = dqg + jnp.sum(dqn * qhat, axis=0, keepdims=True)
            w = dkn * kg
            dk = rk * (w - khat * jnp.mean(khat * w, axis=-1, keepdims=True))
            dkg = dkg + jnp.sum(dkn * khat, axis=0, keepdims=True)
            dqkv_ref[:, pl.ds(D_ATTN + HD * j, HD)] = dk[BLK:, :] + carry[:, pl.ds(HD * j, HD)]
            dqkv_ref[:, pl.ds(D_ATTN + 128 + HD * j, HD)] = dv[BLK:, :] + carry[:, pl.ds(128 + HD * j, HD)]
            carry[:, pl.ds(HD * j, HD)] = dk[:BLK, :]
            carry[:, pl.ds(128 + HD * j, HD)] = dv[:BLK, :]
        dsm_ref[0:1, 0:HD] += dqg
        dsm_ref[1:2, 0:HD] += dkg
        dsm_ref[2:3, :] += dsink

    rev = lambda i: NBLK - 1 - i
    prev = lambda i: jnp.maximum(NBLK - 2 - i, 0)
    small = lambda shape: pl.BlockSpec(shape, lambda i: (0,) * len(shape))
    return pl.pallas_call(
        body, grid=(NBLK,),
        in_specs=[pl.BlockSpec((BLK, D_ATTN), lambda i: (rev(i), 0)),
                  pl.BlockSpec((BLK, 128), lambda i: (rev(i), 4)), pl.BlockSpec((BLK, 128), lambda i: (prev(i), 4)),
                  pl.BlockSpec((BLK, 128), lambda i: (rev(i), 5)), pl.BlockSpec((BLK, 128), lambda i: (prev(i), 5)),
                  pl.BlockSpec((BLK, D_ATTN), lambda i: (rev(i), 0)),
                  small((DEPTH, HD)), small((DEPTH, HD)), small((DEPTH, NQ)), small((NQ, BLK, 2 * BLK))],
        out_specs=[pl.BlockSpec((BLK, 768), lambda i: (rev(i), 0)), small((NQ, BLK, 2 * BLK)), small((8, 128))],
        out_shape=[SDS((S, 768), f32), SDS((NQ, BLK, 2 * BLK), f32), SDS((8, 128), f32)],
        scratch_shapes=[pltpu.VMEM((BLK, 256), f32)], name="attn_bwd", compiler_params=_cparams(1),
    )(qkv, qkv, qkv, qkv, qkv, dmix, q_gain, k_gain, sinks, bias)


CONV_TC = 128


def _shift_down(u, s):
    if s == 0:
        return u
    rows = lax.broadcasted_iota(jnp.int32, u.shape, 0)
    return jnp.where(rows >= s, pltpu.roll(u, s, 0), 0.0)


def _shift_up(u, s):
    if s == 0:
        return u
    rows = lax.broadcasted_iota(jnp.int32, u.shape, 0)
    return jnp.where(rows < u.shape[0] - s, pltpu.roll(u, u.shape[0] - s, 0), 0.0)


def _conv_specs():
    return [pl.BlockSpec((S, CONV_TC), lambda c: (0, c)),
            pl.BlockSpec((None, 4, CONV_TC), lambda c: (0, 0, c)),
            pl.BlockSpec((DEPTH, CONV_TC), lambda c: (0, c))]


def _conv_pre(u, w_ref, b_ref, layer):
    pre = b_ref[layer:layer + 1, :] + w_ref[3:4, :] * u
    for k in range(3):
        pre = pre + w_ref[k:k + 1, :] * _shift_down(u, 3 - k)
    return pre


def _conv_fwd(xbc, conv_w, conv_b, layer):
    def body(u_ref, w_ref, b_ref, o_ref):
        pre = _conv_pre(u_ref[...], w_ref, b_ref, layer)
        o_ref[...] = pre * _sigmoid(pre)

    specs = _conv_specs()
    specs[1] = pl.BlockSpec((None, 4, CONV_TC), lambda c: (layer, 0, c))
    return pl.pallas_call(
        body, grid=(D_CONV // CONV_TC,), in_specs=specs, out_specs=pl.BlockSpec((S, CONV_TC), lambda c: (0, c)),
        out_shape=SDS((S, D_CONV), f32), name="conv_fwd", compiler_params=_cparams(1),
    )(xbc, conv_w, conv_b)


def _conv_bwd(xbc, dact, conv_w, conv_b, layer):
    def body(u_ref, w_ref, b_ref, da_ref, du_ref, dw_ref, db_ref):
        u = u_ref[...]
        pre = _conv_pre(u, w_ref, b_ref, layer)
        sg = _sigmoid(pre)
        dpre = da_ref[...] * (sg * (1.0 + pre * (1.0 - sg)))
        du = w_ref[3:4, :] * dpre
        for k in range(3):
            du = du + w_ref[k:k + 1, :] * _shift_up(dpre, 3 - k)
        du_ref[...] = du
        db_ref[...] = jnp.broadcast_to(jnp.sum(dpre, axis=0, keepdims=True), db_ref.shape)
        dw_ref[...] = jnp.zeros_like(dw_ref)
        for k in range(4):
            dw_ref[k:k + 1, :] = jnp.sum(dpre * _shift_down(u, 3 - k), axis=0, keepdims=True)

    specs = _conv_specs()
    specs[1] = pl.BlockSpec((None, 4, CONV_TC), lambda c: (layer, 0, c))
    col = pl.BlockSpec((S, CONV_TC), lambda c: (0, c))
    row8 = pl.BlockSpec((8, CONV_TC), lambda c: (0, c))
    return pl.pallas_call(
        body, grid=(D_CONV // CONV_TC,), in_specs=[*specs, col], out_specs=[col, row8, row8],
        out_shape=[SDS((S, D_CONV), f32), SDS((8, D_CONV), f32), SDS((8, D_CONV), f32)], name="conv_bwd",
        compiler_params=_cparams(1),
    )(xbc, conv_w, conv_b, dact)


def _tri():
    return (lax.broadcasted_iota(jnp.int32, (BLK, BLK), 0) >= lax.broadcasted_iota(jnp.int32, (BLK, BLK), 1))


def _ssd_scalars(dt_ref, dtb_ref, alog_ref, layer):
    raw = dt_ref[:, 0:NSSM] + dtb_ref[layer:layer + 1, :]
    dtv = jnp.maximum(raw, 0.0) + jnp.log(1.0 + jnp.exp(-jnp.abs(raw)))
    a = -jnp.exp(alog_ref[layer:layer + 1, :])
    acs = jnp.dot(_tri().astype(f32), dtv * a, preferred_element_type=f32, precision=HIGHEST)
    return raw, dtv, a, acs


def _ssd_head_fwd(xc_ref, cb, cm_b, h, dtv, acs, acs_t, e_all, prev_b, dskip):
    xs = xc_ref[:, pl.ds(HD * h, HD)]
    xdt = xs * dtv[:, h:h + 1]
    seg = acs[:, h:h + 1] - acs_t[h:h + 1, :]
    decay = jnp.exp(jnp.where(_tri(), seg, -jnp.inf))
    m = cb * decay
    t_off = _dot(cm_b, prev_b, NT_DIMS)
    y_off = t_off * e_all[:, h:h + 1]
    y = _dot(m.astype(bf16), xdt.astype(bf16), NN_DIMS) + y_off + xs * dskip[:, h:h + 1]
    return xs, xdt, decay, m, t_off, y_off, y


def _ssd_fwd(xact, z, dt, attn, dt_bias, a_log, d_skip, norm_g, layer):
    def body(xc_ref, z_ref, dt_ref, at_ref, dtb_ref, alog_ref, dsk_ref, ng_ref, mix_ref, hs_ref, h_ref, y_ref):
        n = pl.program_id(0)

        @pl.when(n == 0)
        def _():
            h_ref[...] = jnp.zeros_like(h_ref)

        _, dtv, _, acs = _ssd_scalars(dt_ref, dtb_ref, alog_ref, layer)
        acs_t = acs.T
        last = acs[BLK - 1:BLK, :]
        dte = jnp.exp(last - acs)
        e_all = jnp.exp(acs)
        cd = jnp.exp(last)
        dskip = dsk_ref[layer:layer + 1, :]
        hs_ref[...] = h_ref[...]
        for g in range(NGRP):
            bm = xc_ref[:, pl.ds(D_SSM + NSTATE * g, NSTATE)]
            cm_b = xc_ref[:, pl.ds(D_SSM + NGRP * NSTATE + NSTATE * g, NSTATE)].astype(bf16)
            cb = _dot(cm_b, bm.astype(bf16), NT_DIMS)
            for r in range(NSSM // NGRP):
                h = g * (NSSM // NGRP) + r
                rows = pl.ds(HD * h, HD)
                prev = h_ref[rows, :]
                _, xdt, _, _, _, _, y = _ssd_head_fwd(xc_ref, cb, cm_b, h, dtv, acs, acs_t, e_all, prev.astype(bf16), dskip)
                y_ref[:, pl.ds(HD * h, HD)] = y
                bd = (bm * dte[:, h:h + 1]).astype(bf16)
                h_ref[rows, :] = prev * cd[:, h:h + 1] + _dot(xdt.astype(bf16), bd, TN_DIMS)
        zv = z_ref[...]
        yz = y_ref[...] * (zv * _sigmoid(zv))
        mix_ref[:, 0:D_ATTN] = at_ref[...]
        gw = D_SSM // NGRP
        for g in range(NGRP):
            yg = yz[:, gw * g:gw * (g + 1)]
            rs = lax.rsqrt(jnp.mean(yg * yg, axis=-1, keepdims=True) + EPS)
            mix_ref[:, D_ATTN + gw * g:D_ATTN + gw * (g + 1)] = (yg * rs * ng_ref[layer:layer + 1, gw * g:gw * (g + 1)]).astype(bf16)

    small = lambda shape: pl.BlockSpec(shape, lambda n: (0,) * len(shape))
    return pl.pallas_call(
        body, grid=(NBLK,),
        in_specs=[pl.BlockSpec((BLK, D_CONV), lambda n: (n, 0)), pl.BlockSpec((BLK, D_SSM), lambda n: (n, 0)),
                  pl.BlockSpec((BLK, 128), lambda n: (n, 0)), pl.BlockSpec((BLK, D_ATTN), lambda n: (n, 0)),
                  small((DEPTH, NSSM)), small((DEPTH, NSSM)), small((DEPTH, NSSM)), small((DEPTH, D_SSM))],
        out_specs=[pl.BlockSpec((BLK, D), lambda n: (n, 0)), pl.BlockSpec((None, NSSM * HD, NSTATE), lambda n: (n, 0, 0))],
        out_shape=[SDS((S, D), bf16), SDS((NBLK, NSSM * HD, NSTATE), f32)],
        scratch_shapes=[pltpu.VMEM((NSSM * HD, NSTATE), f32), pltpu.VMEM((BLK, D_SSM), f32)],
        name="ssd_fwd", compiler_params=_cparams(1),
    )(xact, z, dt, attn, dt_bias, a_log, d_skip, norm_g)


def _ssd_bwd(xact, z, dt, dmix, hs, dt_bias, a_log, d_skip, norm_g, layer):
    def body(xc_ref, z_ref, dt_ref, do_ref, hs_ref, dtb_ref, alog_ref, dsk_ref, ng_ref,
             dz_ref, dx_ref, ddt_ref, dsm_ref, dh_ref, y_ref, dy_ref):
        i = pl.program_id(0)

        @pl.when(i == 0)
        def _():
            dh_ref[...] = jnp.zeros_like(dh_ref)
            dsm_ref[...] = jnp.zeros_like(dsm_ref)

        raw, dtv, a, acs = _ssd_scalars(dt_ref, dtb_ref, alog_ref, layer)
        acs_t = acs.T
        last = acs[BLK - 1:BLK, :]
        dte = jnp.exp(last - acs)
        e_all = jnp.exp(acs)
        cd = jnp.exp(last)
        dskip = dsk_ref[layer:layer + 1, :]
        lane8 = lax.broadcasted_iota(jnp.int32, (1, NSSM), 1)
        sub8 = lax.broadcasted_iota(jnp.int32, (NSSM, 1), 0)

        for g in range(NGRP):
            bm_b = xc_ref[:, pl.ds(D_SSM + NSTATE * g, NSTATE)].astype(bf16)
            cm_b = xc_ref[:, pl.ds(D_SSM + NGRP * NSTATE + NSTATE * g, NSTATE)].astype(bf16)
            cb = _dot(cm_b, bm_b, NT_DIMS)
            for r in range(NSSM // NGRP):
                h = g * (NSSM // NGRP) + r
                prev_b = hs_ref[pl.ds(HD * h, HD), :].astype(bf16)
                y_ref[:, pl.ds(HD * h, HD)] = _ssd_head_fwd(xc_ref, cb, cm_b, h, dtv, acs, acs_t, e_all, prev_b, dskip)[6]

        zv = z_ref[...]
        sz = _sigmoid(zv)
        gz = zv * sz
        yv = y_ref[...]
        yz = yv * gz
        gw = D_SSM // NGRP
        for g in range(NGRP):
            sl = slice(gw * g, gw * (g + 1))
            yg = yz[:, sl]
            rs = lax.rsqrt(jnp.mean(yg * yg, axis=-1, keepdims=True) + EPS)
            yhat = yg * rs
            dog = do_ref[:, sl]
            w = dog * ng_ref[layer:layer + 1, sl]
            dyz = rs * (w - yhat * jnp.mean(yhat * w, axis=-1, keepdims=True))
            dsm_ref[0:1, sl] += jnp.sum(dog * yhat, axis=0, keepdims=True)
            dy_ref[:, sl] = dyz * gz[:, sl]
            dz_ref[:, sl] = dyz * yv[:, sl] * (sz[:, sl] * (1.0 + zv[:, sl] * (1.0 - sz[:, sl])))

        dacs = jnp.zeros((BLK, NSSM), f32)
        dacs_cols = jnp.zeros((NSSM, BLK), f32)
        dlast = jnp.zeros((1, NSSM), f32)
        ddtv = jnp.zeros((BLK, NSSM), f32)
        ddsk = jnp.zeros((1, NSSM), f32)
        for g in range(NGRP):
            bm = xc_ref[:, pl.ds(D_SSM + NSTATE * g, NSTATE)]
            bm_b = bm.astype(bf16)
            cm_b = xc_ref[:, pl.ds(D_SSM + NGRP * NSTATE + NSTATE * g, NSTATE)].astype(bf16)
            cb = _dot(cm_b, bm_b, NT_DIMS)
            dcb = jnp.zeros((BLK, BLK), f32)
            dbm = jnp.zeros((BLK, NSTATE), f32)
            dcm = jnp.zeros((BLK, NSTATE), f32)
            for r in range(NSSM // NGRP):
                h = g * (NSSM // NGRP) + r
                rows = pl.ds(HD * h, HD)
                oh = (lane8 == h).astype(f32)
                prev = hs_ref[rows, :]
                prev_b = prev.astype(bf16)
                xs, xdt, decay, m, t_off, y_off, _ = _ssd_head_fwd(xc_ref, cb, cm_b, h, dtv, acs, acs_t, e_all, prev_b, dskip)
                xdt_b = xdt.astype(bf16)
                dy = dy_ref[:, pl.ds(HD * h, HD)]
                dy_b = dy.astype(bf16)
                ddsk = ddsk + oh * _sum11(dy * xs)
                dxs = dy * dskip[:, h:h + 1]
                dm = _dot(dy_b, xdt_b, NT_DIMS)
                dxdt = _dot(m.astype(bf16), dy_b, TN_DIMS)
                dseg = dm * m
                dcb = dcb + dm * decay
                dacs = dacs + oh * jnp.sum(dseg, axis=1, keepdims=True)
                dacs_cols = dacs_cols + (sub8 == h).astype(f32) * jnp.sum(dseg, axis=0, keepdims=True)
                dt_off = (dy * e_all[:, h:h + 1]).astype(bf16)
                dacs = dacs + oh * jnp.sum(dy * y_off, axis=1, keepdims=True)
                dcm = dcm + _dot(dt_off, prev_b, NN_DIMS)
                dprev = _dot(dt_off, cm_b, TN_DIMS)
                dhn = dh_ref[rows, :]
                dhn_b = dhn.astype(bf16)
                cd_h = cd[:, h:h + 1]
                dprev = dprev + dhn * cd_h
                dlast = dlast + oh * (_sum11(dhn * prev) * cd_h)
                dte_h = dte[:, h:h + 1]
                bd_b = (bm * dte_h).astype(bf16)
                dxdt = dxdt + _dot(bd_b, dhn_b, NT_DIMS)
                dbd = _dot(xdt_b, dhn_b, NN_DIMS)
                dbm = dbm + dbd * dte_h
                tmp = jnp.sum(dbd * bm, axis=1, keepdims=True) * dte_h
                dlast = dlast + oh * _sum11(tmp)
                dacs = dacs - oh * tmp
                dxs = dxs + dxdt * dtv[:, h:h + 1]
                ddtv = ddtv + oh * jnp.sum(dxdt * xs, axis=1, keepdims=True)
                dh_ref[rows, :] = dprev
                dx_ref[:, pl.ds(HD * h, HD)] = dxs
            dcb_b = dcb.astype(bf16)
            dx_ref[:, pl.ds(D_SSM + NSTATE * g, NSTATE)] = dbm + _dot(dcb_b, cm_b, TN_DIMS)
            dx_ref[:, pl.ds(D_SSM + NGRP * NSTATE + NSTATE * g, NSTATE)] = dcm + _dot(dcb_b, bm_b, NN_DIMS)

        row = lax.broadcasted_iota(jnp.int32, (BLK, 1), 0)
        dacs = dacs - dacs_cols.T + jnp.where(row == BLK - 1, dlast, 0.0)
        dda = lax.dot_general(_tri().astype(f32), dacs, TN_DIMS, preferred_element_type=f32, precision=HIGHEST)
        ddtv = ddtv + dda * a
        da = jnp.sum(dda * dtv, axis=0, keepdims=True)
        draw = ddtv * _sigmoid(raw)
        ddt_ref[...] = jnp.zeros_like(ddt_ref)
        ddt_ref[:, 0:NSSM] = draw
        dsm_ref[1:2, 0:NSSM] += jnp.sum(draw, axis=0, keepdims=True)
        dsm_ref[2:3, 0:NSSM] += da * a
        dsm_ref[3:4, 0:NSSM] += ddsk

    rev = lambda i: NBLK - 1 - i
    small = lambda shape: pl.BlockSpec(shape, lambda i: (0,) * len(shape))
    return pl.pallas_call(
        body, grid=(NBLK,),
        in_specs=[pl.BlockSpec((BLK, D_CONV), lambda i: (rev(i), 0)), pl.BlockSpec((BLK, D_SSM), lambda i: (rev(i), 0)),
                  pl.BlockSpec((BLK, 128), lambda i: (rev(i), 0)), pl.BlockSpec((BLK, D_SSM), lambda i: (rev(i), 1)),
                  pl.BlockSpec((None, NSSM * HD, NSTATE), lambda i: (rev(i), 0, 0)),
                  small((DEPTH, NSSM)), small((DEPTH, NSSM)), small((DEPTH, NSSM)), small((DEPTH, D_SSM))],
        out_specs=[pl.BlockSpec((BLK, D_SSM), lambda i: (rev(i), 0)), pl.BlockSpec((BLK, D_CONV), lambda i: (rev(i), 0)),
                   pl.BlockSpec((BLK, 128), lambda i: (rev(i), 0)), small((8, D_SSM))],
        out_shape=[SDS((S, D_SSM), f32), SDS((S, D_CONV), f32), SDS((S, 128), f32), SDS((8, D_SSM), f32)],
        scratch_shapes=[pltpu.VMEM((NSSM * HD, NSTATE), f32), pltpu.VMEM((BLK, D_SSM), f32), pltpu.VMEM((BLK, D_SSM), f32)],
        name="ssd_bwd", compiler_params=_cparams(1),
    )(xact, z, dt, dmix, hs, dt_bias, a_log, d_skip, norm_g)


def _my_place():
    return lax.axis_index("x"), lax.axis_index("y"), lax.axis_index("c")


def _dev_index(px, py, pc):
    return 4 * px + 2 * py + pc


def _slab2(kind, ref, idx):
    if kind == "stack":
        return ref.at[idx]
    if kind == "rows128":
        return ref.at[pl.ds(pl.multiple_of(idx * 128, 128), 128), :]
    if kind == "rows512":
        return ref.at[pl.ds(pl.multiple_of(idx * 512, 512), 512), :]
    return ref.at[:, pl.ds(pl.multiple_of(idx * 512, 512), 512)]


KIND = dict(w_in="stack", w_out="rows128", w_up="cols512", w_down="rows512", conv_w="stack")
FULL_SHAPE = dict(w_in=(N_DEV, D, D_IN // N_DEV), w_out=(D, D), w_up=(D, D_FF), w_down=(D_FF, D))
HBM_SPEC = pl.BlockSpec(memory_space=pltpu.HBM)
SEM_SPEC = pl.BlockSpec(memory_space=pltpu.SEMAPHORE)
SIDE_EFFECT = pltpu.SideEffectType.DATAFLOW_SIDE_EFFECTING


def _peers_all():
    x, y, c = _my_place()
    return [(x ^ ((r >> 2) & 1), y ^ ((r >> 1) & 1), c ^ (r & 1)) for r in range(1, N_DEV)]


def _split_start(name, bufs, n_copies, plan, deps=()):
    nb = len(bufs)

    def body(*refs):
        ins = refs[:nb]
        send_sems, recv_sems = refs[nb + len(deps)], refs[nb + len(deps) + 1]
        token = refs[-1]
        for i, (src, dst, dev) in enumerate(plan(ins)):
            pltpu.make_async_remote_copy(src_ref=src, dst_ref=dst, send_sem=send_sems.at[i], recv_sem=recv_sems.at[i],
                                         device_id=dev, device_id_type=MESH).start()
        token[...] = jnp.zeros_like(token)

    outs = pl.pallas_call(
        body, name=name,
        out_shape=(pltpu.SemaphoreType.DMA((n_copies,)), pltpu.SemaphoreType.DMA((n_copies,)),
                   *[pltpu.HBM(b.shape, b.dtype) for b in bufs], SDS((8, 128), f32)),
        in_specs=[HBM_SPEC] * nb + [ANY_SPEC] * len(deps),
        out_specs=(SEM_SPEC, SEM_SPEC, *[HBM_SPEC] * nb, pl.BlockSpec(memory_space=pltpu.VMEM)),
        input_output_aliases={i: 2 + i for i in range(nb)},
        compiler_params=pltpu.CompilerParams(has_side_effects=SIDE_EFFECT),
    )(*[pltpu.with_memory_space_constraint(b, pltpu.HBM) for b in bufs], *deps)
    return dict(send=outs[0], recv=outs[1], bufs=list(outs[2:2 + nb]), token=outs[-1], plan=plan, n=n_copies)


def _split_wait(name, started, after):
    bufs = started["bufs"]
    nb = len(bufs)
    plan = started["plan"]

    def body(*refs):
        ins = refs[:nb]
        send_sems, recv_sems = refs[nb], refs[nb + 1]
        for i, (src, dst, dev) in enumerate(plan(ins)):
            cp = pltpu.make_async_remote_copy(src_ref=src, dst_ref=dst, send_sem=send_sems.at[i], recv_sem=recv_sems.at[i],
                                              device_id=dev, device_id_type=MESH)
            cp.wait_send()
            cp.wait_recv()

    outs = pl.pallas_call(
        body, name=name, out_shape=tuple(pltpu.HBM(b.shape, b.dtype) for b in bufs),
        in_specs=[HBM_SPEC] * nb + [SEM_SPEC, SEM_SPEC] + [ANY_SPEC] * len(after), out_specs=(HBM_SPEC,) * nb,
        input_output_aliases={i: i for i in range(nb)},
        compiler_params=pltpu.CompilerParams(has_side_effects=SIDE_EFFECT),
    )(*bufs, started["send"], started["recv"], *after)
    return list(outs)


def _gather_start(name, names, shards, deps):
    n_t = len(names)
    lands = [lax.empty((N_DEV,) + s.shape if KIND[n] == "stack" else FULL_SHAPE[n], s.dtype) for n, s in zip(names, shards)]

    def plan(refs):
        x, y, c = _my_place()
        my_idx = _dev_index(x, y, c)
        targets = [(x, y, 1 - c), (1 - x, y, c), (x, 1 - y, c), (1 - x, 1 - y, c)]
        return [(refs[t], _slab2(KIND[names[t]], refs[n_t + t], my_idx), dev) for t in range(n_t) for dev in targets]

    return _split_start(name, list(shards) + lands, 4 * n_t, plan, deps)


def _gather_finish(name, names, started, after):
    n_t = len(names)
    bufs = _split_wait(name + "_wait", started, after)
    shards, lands = bufs[:n_t], bufs[n_t:]

    def body(*refs):
        srcs = refs[:n_t]
        ins = refs[n_t:2 * n_t]
        outs = refs[2 * n_t:3 * n_t]
        send_sems, recv_sems, local_sems = refs[3 * n_t:]
        x, y, c = _my_place()
        chips = [(1 - x, y), (x, 1 - y), (1 - x, 1 - y)]
        mine = [pltpu.make_async_copy(srcs[t], _slab2(KIND[names[t]], outs[t], _dev_index(x, y, c)), local_sems.at[t])
                for t in range(n_t)]
        for cp in mine:
            cp.start()

        def copy(t, j, core):
            idx = _dev_index(*chips[j], core)
            return pltpu.make_async_remote_copy(
                src_ref=_slab2(KIND[names[t]], ins[t], idx), dst_ref=_slab2(KIND[names[t]], outs[t], idx),
                send_sem=send_sems.at[t, j], recv_sem=recv_sems.at[t, j], device_id=(x, y, 1 - c), device_id_type=MESH)

        sends = [copy(t, j, c) for t in range(n_t) for j in range(3)]
        for cp in sends:
            cp.start()
        for t in range(n_t):
            for j in range(3):
                copy(t, j, 1 - c).wait_recv()
        for cp in sends:
            cp.wait_send()
        for cp in mine:
            cp.wait()

    return pl.pallas_call(
        body, in_specs=[ANY_SPEC] * (2 * n_t), out_specs=[ANY_SPEC] * n_t, out_shape=[SDS(b.shape, b.dtype) for b in lands],
        input_output_aliases={n_t + t: t for t in range(n_t)},
        scratch_shapes=[pltpu.SemaphoreType.DMA((n_t, 3)), pltpu.SemaphoreType.DMA((n_t, 3)), pltpu.SemaphoreType.DMA((n_t,))],
        name=name + "_pass",
    )(*shards, *lands)


def _exchange_start(name, names, grads, deps):
    n_t = len(names)
    lands = []
    for n, g in zip(names, grads):
        shard_shape = g.shape[1:] if KIND[n] == "stack" else {"rows128": (128, D), "cols512": (D, 512), "rows512": (512, D)}[KIND[n]]
        lands.append(lax.empty((N_DEV,) + shard_shape, g.dtype))

    def plan(refs):
        my_idx = _dev_index(*_my_place())
        return [(_slab2(KIND[names[t]], refs[t], _dev_index(*peer)), refs[n_t + t].at[my_idx], peer)
                for t in range(n_t) for peer in _peers_all()]

    return _split_start(name, list(grads) + lands, 7 * n_t, plan, deps)


def _all_reduce_small(part):
    def body(p_ref, o_ref, land, send_sems, recv_sems):
        x, y, c = _my_place()
        my_idx = _dev_index(x, y, c)
        peers = []
        for r in range(1, N_DEV):
            peers.append((x ^ ((r >> 2) & 1), y ^ ((r >> 1) & 1), c ^ (r & 1)))
        land[my_idx] = p_ref[...]
        sends = [pltpu.make_async_remote_copy(src_ref=p_ref, dst_ref=land.at[my_idx], send_sem=send_sems.at[r],
                                              recv_sem=recv_sems.at[r], device_id=peer, device_id_type=MESH)
                 for r, peer in enumerate(peers)]
        for cp in sends:
            cp.start()
        for r, peer in enumerate(peers):
            pltpu.make_async_remote_copy(src_ref=p_ref, dst_ref=land.at[_dev_index(*peer)], send_sem=send_sems.at[r],
                                         recv_sem=recv_sems.at[r], device_id=peer, device_id_type=MESH).wait_recv()
        for cp in sends:
            cp.wait_send()
        tot = land[0]
        for d in range(1, N_DEV):
            tot = tot + land[d]
        o_ref[...] = tot

    vm = pl.BlockSpec(memory_space=pltpu.VMEM)
    return pl.pallas_call(
        body, in_specs=[vm], out_specs=vm, out_shape=SDS((SMALL_ROWS, D), f32),
        scratch_shapes=[pltpu.VMEM((N_DEV, SMALL_ROWS, D), f32), pltpu.SemaphoreType.DMA((7,)), pltpu.SemaphoreType.DMA((7,))],
        name="all_reduce_small",
    )(part)


SMALL_NAMES = ("mix_norm_g", "mlp_norm_g", "conv_b", "ssm_norm_g", "q_gain", "k_gain", "sinks", "dt_bias", "a_log", "d_skip",
               "rel_bias", "conv_w")
MISC_LANES = dict(q_gain=(LANE_QG, HD), k_gain=(LANE_KG, HD), sinks=(LANE_SINK, NQ), dt_bias=(LANE_DTB, NSSM),
                  a_log=(LANE_ALOG, NSSM), d_skip=(LANE_DSKIP, NSSM))


def _pack_small_grads(smalls, drel_t, loss):
    def body(*refs):
        o_ref = refs[-1]
        drel_ref, loss_ref = refs[-3], refs[-2]
        o_ref[...] = jnp.zeros_like(o_ref)
        for l in range(DEPTH):
            mixg, mlpg, convb, convw, ssd, attn = refs[6 * l:6 * l + 6]
            o_ref[ROW_MIXG + l:ROW_MIXG + l + 1, :] = mixg[...]
            o_ref[ROW_MLPG + l:ROW_MLPG + l + 1, :] = mlpg[...]
            o_ref[ROW_CONVB + l:ROW_CONVB + l + 1, :] = convb[0:1, :]
            o_ref[ROW_SSMG + l:ROW_SSMG + l + 1, 0:D_SSM] = ssd[0:1, :]
            o_ref[ROW_CONVW + 4 * l:ROW_CONVW + 4 * l + 4, :] = convw[0:4, :]
            row = slice(ROW_MISC + l, ROW_MISC + l + 1)
            o_ref[row, LANE_QG:LANE_QG + HD] = attn[0:1, 0:HD]
            o_ref[row, LANE_KG:LANE_KG + HD] = attn[1:2, 0:HD]
            o_ref[row, LANE_SINK:LANE_SINK + NQ] = attn[2:3, 0:NQ]
            o_ref[row, LANE_DTB:LANE_DTB + NSSM] = ssd[1:2, 0:NSSM]
            o_ref[row, LANE_ALOG:LANE_ALOG + NSSM] = ssd[2:3, 0:NSSM]
            o_ref[row, LANE_DSKIP:LANE_DSKIP + NSSM] = ssd[3:4, 0:NSSM]
        o_ref[ROW_RELB:ROW_RELB + NQ, 0:N_BUCKETS] = drel_ref[...]
        o_ref[ROW_LOSS:ROW_LOSS + 1, 0:1] = loss_ref[0:1, 0:1]

    args = []
    for sm in smalls:
        args += [sm["mix_norm_g"], sm["mlp_norm_g"], sm["conv_b"], sm["conv_w"], sm["ssd"], sm["attn"]]
    args += [drel_t, loss]
    return pl.pallas_call(body, out_shape=SDS((SMALL_ROWS, D), f32), name="pack_small_grads")(*args)


def _adamw_small(gsum, w, m, v):
    n = len(SMALL_NAMES)

    def grad_of(name, g_ref):
        if name == "mix_norm_g":
            return g_ref[ROW_MIXG:ROW_MIXG + DEPTH, :]
        if name == "mlp_norm_g":
            return g_ref[ROW_MLPG:ROW_MLPG + DEPTH, :]
        if name == "conv_b":
            return g_ref[ROW_CONVB:ROW_CONVB + DEPTH, :]
        if name == "ssm_norm_g":
            return g_ref[ROW_SSMG:ROW_SSMG + DEPTH, 0:D_SSM]
        if name == "rel_bias":
            return g_ref[ROW_RELB:ROW_RELB + NQ, 0:N_BUCKETS].T
        lane, width = MISC_LANES[name]
        return g_ref[ROW_MISC:ROW_MISC + DEPTH, lane:lane + width]

    def body(g_ref, *refs):
        ws, ms, vs = refs[:n], refs[n:2 * n], refs[2 * n:3 * n]
        loss_ref = refs[3 * n]
        outs = refs[3 * n + 1:]
        loss_ref[...] = g_ref[ROW_LOSS:ROW_LOSS + 1, 0:128]
        my_cols = pl.ds(pl.multiple_of(_dev_index(*_my_place()) * 128, 128), 128)
        for k, name in enumerate(SMALL_NAMES):
            g_out, d_out, m_out, v_out = outs[4 * k:4 * k + 4]
            if name == "conv_w":
                for l in range(DEPTH):
                    g = g_ref[ROW_CONVW + 4 * l:ROW_CONVW + 4 * l + 4, my_cols]
                    delta, m_new, v_new = _adamw_math(ws[k][l], ms[k][l], vs[k][l], g)
                    g_out[l], d_out[l], m_out[l], v_out[l] = g, delta, m_new, v_new
            else:
                g = grad_of(name, g_ref)
                delta, m_new, v_new = _adamw_math(ws[k][...], ms[k][...], vs[k][...], g)
                g_out[...], d_out[...], m_out[...], v_out[...] = g, delta, m_new, v_new

    ws = [w[name] for name in SMALL_NAMES]
    out_shape = [SDS((1, 128), f32)]
    for a in ws:
        out_shape += [SDS(a.shape, f32)] * 4
    return pl.pallas_call(body, out_shape=out_shape, name="adamw_small")(
        gsum, *ws, *[m[name] for name in SMALL_NAMES], *[v[name] for name in SMALL_NAMES])


def _plain(tm, tn):
    return pl.BlockSpec((tm, tn), lambda i, j, k: (i, j))


def _rowblk(tm, width):
    return pl.BlockSpec((tm, width), lambda i, j, k: (i, 0))


def _store_epi(dtype):
    def epi(acc, i, j, ex, outs):
        outs[0][...] = acc.astype(dtype)
    return epi


def _layer_fwd(l, x, p, get_weights, bias, deps):
    wts = get_weights(l, "in", [x])
    h1 = _rms_fwd("rms_mix", x, p["mix_norm_g"], l, deps)

    def inproj_epi(acc, i, j, ex, outs):
        outs[0][...] = acc[:, 0:768]
        outs[1][...] = acc[:, 768:1280]
        outs[2][...] = acc[:, 1280:2304]
        outs[3][...] = acc[:, 2304:2432]

    tm = 256
    qkv, z, xbc, dt = _matmul(
        "in_proj", "nn", h1, wts["w_in"], tm=tm, tn=D_IN_PAD, tk=D,
        out_shape=[SDS((S, 768), f32), SDS((S, 512), f32), SDS((S, 1024), f32), SDS((S, 128), f32)],
        out_specs=[_rowblk(tm, 768), _rowblk(tm, 512), _rowblk(tm, 1024), _rowblk(tm, 128)], epilogue=inproj_epi)
    attn = _attn_fwd(qkv, p["q_gain"], p["k_gain"], p["sinks"], bias, l)
    xact = _conv_fwd(xbc, wts["conv_w"], p["conv_b"], l)
    mix, hs = _ssd_fwd(xact, z, dt, attn, p["dt_bias"], p["a_log"], p["d_skip"], p["ssm_norm_g"], l)
    wts = dict(wts, **get_weights(l, "rest", [mix]))

    def resid_epi(acc, i, j, ex, outs):
        outs[0][...] = ex[0][...] + acc

    x_mid = _matmul("out_proj", "nn", mix, wts["w_out"], tm=512, tn=D, tk=512, out_shape=SDS((S, D), f32),
                    out_specs=_plain(512, D), epilogue=resid_epi, extras=(x,), extra_specs=(_plain(512, D),))
    h2 = _rms_fwd("rms_mlp", x_mid, p["mlp_norm_g"], l)

    def up_epi(acc, i, j, ex, outs):
        r = jnp.maximum(acc, 0.0)
        outs[0][...] = r.astype(bf16)
        outs[1][...] = (r * r).astype(bf16)

    r_act, a_act = _matmul("mlp_up", "nn", h2, wts["w_up"], tm=512, tn=512, tk=D,
                           out_shape=[SDS((S, D_FF), bf16)] * 2, out_specs=[_plain(512, 512)] * 2, epilogue=up_epi)
    x_out = _matmul("mlp_down", "nn", a_act, wts["w_down"], tm=512, tn=D, tk=512, out_shape=SDS((S, D), f32),
                    out_specs=_plain(512, D), epilogue=resid_epi, extras=(x_mid,), extra_specs=(_plain(512, D),))
    saved = dict(x=x, h1=h1, qkv=qkv, z=z, xbc=xbc, dt=dt, xact=xact, mix=mix, hs=hs, x_mid=x_mid, h2=h2, r=r_act, a=a_act,
                 wts=wts)
    return x_out, saved


def _w_in_slabs(dw_in):
    return jnp.transpose(dw_in[:, :D_IN].reshape(D, N_DEV, D_IN // N_DEV), (1, 0, 2))


def _layer_bwd(l, dx_out, sv, p, bias, deps, send):
    wts = sv["wts"]

    def du_epi(acc, i, j, ex, outs):
        outs[0][...] = (acc * (2.0 * ex[0][...].astype(f32))).astype(bf16)

    du = _matmul("mlp_da", "nt", dx_out, wts["w_down"], tm=512, tn=512, tk=D, out_shape=SDS((S, D_FF), bf16),
                 out_specs=_plain(512, 512), epilogue=du_epi, extras=(sv["r"],), extra_specs=(_plain(512, 512),), deps=deps)
    dw_down = _matmul("dw_down", "tn", sv["a"], dx_out, tm=1024, tn=D, tk=512, out_shape=SDS((D_FF, D), bf16),
                      out_specs=_plain(1024, D), epilogue=_store_epi(bf16))
    dw_up = _matmul("dw_up", "tn", sv["h2"], du, tm=D, tn=512, tk=512, out_shape=SDS((D, D_FF), bf16),
                    out_specs=_plain(D, 512), epilogue=_store_epi(bf16))
    deps = send(l, dict(w_down=dw_down, w_up=dw_up))
    gfull = pl.BlockSpec((DEPTH, D), lambda i, j, k: (0, 0))
    grow = pl.BlockSpec((1, D), lambda i, j, k: (0, 0))
    dx_mid, dg_mlp = _matmul(
        "mlp_dh", "nt", du, wts["w_up"], tm=512, tn=D, tk=512, out_shape=[SDS((S, D), f32), SDS((1, D), f32)],
        out_specs=[_plain(512, D), grow], epilogue=_rms_bwd_epilogue(l),
        extras=(sv["x_mid"], p["mlp_norm_g"], dx_out), extra_specs=(_plain(512, D), gfull, _plain(512, D)), deps=deps)
    dmix = _matmul("out_proj_da", "nt", dx_mid, wts["w_out"], tm=512, tn=512, tk=D, out_shape=SDS((S, D), f32),
                   out_specs=_plain(512, 512), epilogue=_store_epi(f32))
    dw_out = _matmul("dw_out", "tn", sv["mix"], dx_mid, tm=D, tn=512, tk=512, out_shape=SDS((D, D), bf16),
                     out_specs=_plain(D, 512), epilogue=_store_epi(bf16))
    dz, dxact, ddt, dsm_ssd = _ssd_bwd(sv["xact"], sv["z"], sv["dt"], dmix, sv["hs"], p["dt_bias"], p["a_log"],
                                       p["d_skip"], p["ssm_norm_g"], l)
    dxbc, dconv_w, dconv_b = _conv_bwd(sv["xbc"], dxact, wts["conv_w"], p["conv_b"], l)
    dqkv, dbias, dsm_attn = _attn_bwd(sv["qkv"], dmix, p["q_gain"], p["k_gain"], p["sinks"], bias, l)
    dproj = _pack_dproj(dqkv, dz, dxbc, ddt)
    dw_in = _matmul("dw_in", "tn", sv["h1"], dproj, tm=D, tn=640, tk=512, out_shape=SDS((D, D_IN_PAD), bf16),
                    out_specs=_plain(D, 640), epilogue=_store_epi(bf16))
    deps = send(l, dict(w_out=dw_out, w_in=_w_in_slabs(dw_in)))
    dx, dg_mix = _matmul(
        "in_proj_dh", "nt", dproj, wts["w_in"], tm=256, tn=D, tk=D_IN_PAD, out_shape=[SDS((S, D), f32), SDS((1, D), f32)],
        out_specs=[_plain(256, D), grow], epilogue=_rms_bwd_epilogue(l),
        extras=(sv["x"], p["mix_norm_g"], dx_mid), extra_specs=(_plain(256, D), gfull, _plain(256, D)), deps=deps)
    small = dict(mix_norm_g=dg_mix, mlp_norm_g=dg_mlp, conv_w=dconv_w, conv_b=dconv_b, ssd=dsm_ssd, attn=dsm_attn, dbias=dbias)
    return dx, small, deps


def _local_step(x, tgt, p, get_weights, send, deps):
    onehot_t = jnp.asarray(_onehot_buckets())
    bias = _bias_build(p["rel_bias"].T, onehot_t).reshape(NQ, BLK, 2 * BLK)
    saved = []
    h = x
    for l in range(DEPTH):
        h, sv = _layer_fwd(l, h, p, get_weights, bias, deps if l == 0 else ())
        saved.append(sv)
    dx, loss = _loss_kernel(h, tgt)
    smalls = [None] * DEPTH
    deps = ()
    for l in reversed(range(DEPTH)):
        dx, smalls[l], deps = _layer_bwd(l, dx, saved[l], p, bias, deps, send)
    drel_t = _bias_grad(smalls[0]["dbias"].reshape(NQ, -1), smalls[1]["dbias"].reshape(NQ, -1), onehot_t)
    return dx, _pack_small_grads(smalls, drel_t, loss), deps


WEIGHT_ORDER = ("mix_norm_g", "w_in", "q_gain", "k_gain", "sinks", "rel_bias", "conv_w", "conv_b", "dt_bias", "a_log", "d_skip",
                "ssm_norm_g", "w_out", "mlp_norm_g", "w_up", "w_down")


def kernel(x, mix_norm_g, w_in, q_gain, k_gain, sinks, rel_bias, conv_w, conv_b, dt_bias, a_log, d_skip, ssm_norm_g, w_out, mlp_norm_g, w_up, w_down, loss_target, m_mix_norm_g, m_w_in, m_q_gain, m_k_gain, m_sinks, m_rel_bias, m_conv_w, m_conv_b, m_dt_bias, m_a_log, m_d_skip, m_ssm_norm_g, m_w_out, m_mlp_norm_g, m_w_up, m_w_down, v_mix_norm_g, v_w_in, v_q_gain, v_k_gain, v_sinks, v_rel_bias, v_conv_w, v_conv_b, v_dt_bias, v_a_log, v_d_skip, v_ssm_norm_g, v_w_out, v_mlp_norm_g, v_w_up, v_w_down):
    w = dict(mix_norm_g=mix_norm_g, w_in=w_in, q_gain=q_gain, k_gain=k_gain, sinks=sinks, rel_bias=rel_bias, conv_w=conv_w,
             conv_b=conv_b, dt_bias=dt_bias, a_log=a_log, d_skip=d_skip, ssm_norm_g=ssm_norm_g, w_out=w_out,
             mlp_norm_g=mlp_norm_g, w_up=w_up, w_down=w_down)
    m = dict(mix_norm_g=m_mix_norm_g, w_in=m_w_in, q_gain=m_q_gain, k_gain=m_k_gain, sinks=m_sinks, rel_bias=m_rel_bias,
             conv_w=m_conv_w, conv_b=m_conv_b, dt_bias=m_dt_bias, a_log=m_a_log, d_skip=m_d_skip, ssm_norm_g=m_ssm_norm_g,
             w_out=m_w_out, mlp_norm_g=m_mlp_norm_g, w_up=m_w_up, w_down=m_w_down)
    v = dict(mix_norm_g=v_mix_norm_g, w_in=v_w_in, q_gain=v_q_gain, k_gain=v_k_gain, sinks=v_sinks, rel_bias=v_rel_bias,
             conv_w=v_conv_w, conv_b=v_conv_b, dt_bias=v_dt_bias, a_log=v_a_log, d_skip=v_d_skip, ssm_norm_g=v_ssm_norm_g,
             w_out=v_w_out, mlp_norm_g=v_mlp_norm_g, w_up=v_w_up, w_down=v_w_down)
    big = ("w_in", "w_out", "w_up", "w_down")

    shards = {n: _cast_bf16("cast_" + n, w[n]) for n in big}
    rest = ["w_out", "w_up", "w_down"]
    g0 = _gather_start("gather0", ["w_in", "conv_w"], [shards["w_in"][0], conv_w], ())
    g1 = _gather_start("gather1", rest, [shards[n][0] for n in rest], (g0["token"],))
    g2 = _gather_start("gather2", list(big), [shards[n][1] for n in big], (g1["token"],))
    held = {}

    def w_in_padded(land):
        full = jnp.transpose(land, (1, 0, 2)).reshape(D, D_IN)
        return jnp.pad(full, ((0, 0), (0, D_IN_PAD - D_IN)))[None]

    def get_weights(l, part, after):
        if l == 0 and part == "in":
            land_in, land_conv = _gather_finish("gather0", ["w_in", "conv_w"], g0, after)
            held["conv_w"] = jnp.transpose(land_conv, (1, 2, 0, 3)).reshape(DEPTH, 4, D_CONV)
            return dict(w_in=w_in_padded(land_in), conv_w=held["conv_w"])
        if l == 0:
            full = _gather_finish("gather1", rest, g1, after)
            return {n: f[None] for n, f in zip(rest, full)}
        if part == "in":
            held["layer1"] = _gather_finish("gather2", list(big), g2, after)
            return dict(w_in=w_in_padded(held["layer1"][0]), conv_w=held["conv_w"])
        return {n: f[None] for n, f in zip(rest, held["layer1"][1:])}

    pending = []

    def send(l, grads):
        names = list(grads)
        started = _exchange_start("exchange%d_%s" % (l, names[0]), names, [grads[n] for n in names], ())
        pending.append((l, names, started))
        return (started["token"],)

    dx, small_part, _ = _local_step(x.reshape(S, D), loss_target.reshape(S, D), w, get_weights, send, (g2["token"],))

    small_sum = _all_reduce_small(small_part)
    small_outs = _adamw_small(small_sum, w, m, v)
    loss = small_outs[0][0, 0]
    res = {name: small_outs[1 + 4 * k:5 + 4 * k] for k, name in enumerate(SMALL_NAMES)}

    my_idx = _dev_index(*_my_place()).astype(jnp.int32).reshape(1)
    tiles = dict(w_in=256, w_out=128, w_up=256, w_down=256)
    flat = lambda a: a.reshape(a.shape[0] * a.shape[1], a.shape[2])
    outs_of = {n: None for n in big}
    for l, names, started in pending:
        bufs = _split_wait("exchange%d_%s_wait" % (l, names[0]), started, [dx, small_sum])
        for t, n in enumerate(names):
            outs_of[n] = _adamw_layer("adamw_%s%d" % (n, l), KIND[n], l, flat(w[n]), flat(m[n]), flat(v[n]),
                                      bufs[len(names) + t], bufs[t], my_idx, outs_of[n], tiles[n])
    for n in big:
        res[n] = [o.reshape(w[n].shape) for o in outs_of[n]]

    result = [loss, dx.reshape(1, S, D)]
    for k in range(4):
        result += [res[name][k] for name in WEIGHT_ORDER]
    return tuple(result)
```

```python
import functools
import math

import numpy as np
import jax
import jax.numpy as jnp
from jax import lax
from jax.experimental import pallas as pl
from jax.experimental.pallas import tpu as pltpu

f32 = jnp.float32
bf16 = jnp.bfloat16
SDS = jax.ShapeDtypeStruct
MESH = pl.DeviceIdType.MESH
HIGHEST = lax.Precision.HIGHEST

S = 2048
D = 1024
DEPTH = 2
BLK = 128
NBLK = S // BLK
HD = 64
NQ = 8
NKV = 2
NSSM = 8
NGRP = 2
NSTATE = 128
D_ATTN = 512
D_SSM = 512
D_CONV = 1024
D_FF = 4096
D_IN = 2312
D_IN_PAD = 2560
N_BUCKETS = 32
EPS = 1e-6
N_DEV = 8
VMEM_LIMIT = 48 * 1024 * 1024

ADAM_LR = 0.001
ADAM_B1 = 0.9
ADAM_B2 = 0.999
ADAM_EPS = 1e-08
ADAM_WD = 0.01
ADAM_STEP = 10

NT_DIMS = (((1,), (1,)), ((), ()))
TN_DIMS = (((0,), (0,)), ((), ()))
NN_DIMS = (((1,), (0,)), ((), ()))

ROW_MIXG = 0
ROW_MLPG = 2
ROW_CONVB = 4
ROW_SSMG = 6
ROW_MISC = 8
ROW_RELB = 10
ROW_CONVW = 18
ROW_LOSS = 26
SMALL_ROWS = 32
LANE_QG, LANE_KG, LANE_SINK, LANE_DTB, LANE_ALOG, LANE_DSKIP = 0, 64, 128, 256, 384, 512


def _dot(a, b, dims):
    return lax.dot_general(a, b, dims, preferred_element_type=f32)


def _cparams(n_axes):
    return pltpu.CompilerParams(dimension_semantics=("arbitrary",) * n_axes, vmem_limit_bytes=VMEM_LIMIT)


def _sum11(v):
    return jnp.sum(jnp.sum(v, axis=1, keepdims=True), axis=0, keepdims=True)


def _sigmoid(v):
    return 1.0 / (1.0 + jnp.exp(-v))


ANY_SPEC = pl.BlockSpec(memory_space=pl.ANY)


def _matmul(name, mode, a, b, *, layer=0, tm, tn, tk, out_shape, out_specs, epilogue, extras=(), extra_specs=(), deps=()):
    extras = tuple(extras) + tuple(deps)
    extra_specs = tuple(extra_specs) + (ANY_SPEC,) * len(deps)
    if mode == "tn":
        t_dim, m_dim = a.shape
        n_dim = b.shape[1]
        grid = (m_dim // tm, n_dim // tn, t_dim // tk)
        a_spec = pl.BlockSpec((tk, tm), lambda i, j, k: (k, i))
        b_spec = pl.BlockSpec((tk, tn), lambda i, j, k: (k, j))
        dims = TN_DIMS
    elif mode == "nn":
        m_dim, k_dim = a.shape
        n_dim = b.shape[-1]
        grid = (m_dim // tm, n_dim // tn, k_dim // tk)
        a_spec = pl.BlockSpec((tm, tk), lambda i, j, k: (i, k))
        b_spec = pl.BlockSpec((None, tk, tn), lambda i, j, k: (layer, k, j))
        dims = NN_DIMS
    else:
        m_dim, k_dim = a.shape
        n_dim = b.shape[-2]
        grid = (m_dim // tm, n_dim // tn, k_dim // tk)
        a_spec = pl.BlockSpec((tm, tk), lambda i, j, k: (i, k))
        b_spec = pl.BlockSpec((None, tn, tk), lambda i, j, k: (layer, j, k))
        dims = NT_DIMS
    nk = grid[2]
    n_ex = len(extras)

    def body(a_ref, b_ref, *rest):
        ex = rest[:n_ex - len(deps)]
        outs = rest[n_ex:-1]
        acc = rest[-1]
        i = pl.program_id(0)
        j = pl.program_id(1)
        k = pl.program_id(2)
        part = _dot(a_ref[...].astype(bf16), b_ref[...].astype(bf16), dims)
        if nk == 1:
            epilogue(part, i, j, ex, outs)
        else:
            @pl.when(k == 0)
            def _():
                acc[...] = part

            @pl.when(k > 0)
            def _():
                acc[...] += part

            @pl.when(k == nk - 1)
            def _():
                epilogue(acc[...], i, j, ex, outs)

    return pl.pallas_call(
        body, grid=grid, in_specs=[a_spec, b_spec, *extra_specs], out_specs=out_specs, out_shape=out_shape,
        scratch_shapes=[pltpu.VMEM((tm, tn) if nk > 1 else (8, 128), f32)], name=name, compiler_params=_cparams(3),
    )(a, b, *extras)


def _rms_bwd_epilogue(layer):
    def epi(acc, i, j, ex, outs):
        x_ref, g_ref, dres_ref = ex
        dx_ref, dg_ref = outs
        xv = x_ref[...]
        r = lax.rsqrt(jnp.mean(xv * xv, axis=-1, keepdims=True) + EPS)
        xhat = xv * r
        w = acc * g_ref[layer:layer + 1, :]
        dx_ref[...] = dres_ref[...] + r * (w - xhat * jnp.mean(xhat * w, axis=-1, keepdims=True))
        dg = jnp.sum(acc * xhat, axis=0, keepdims=True)

        @pl.when(i == 0)
        def _():
            dg_ref[...] = dg

        @pl.when(i > 0)
        def _():
            dg_ref[...] += dg
    return epi


def _rms_fwd(name, x, g, layer, deps=()):
    tr = 512

    def body(x_ref, g_ref, *rest):
        h_ref = rest[-1]
        xv = x_ref[...]
        r = lax.rsqrt(jnp.mean(xv * xv, axis=-1, keepdims=True) + EPS)
        h_ref[...] = (xv * r * g_ref[layer:layer + 1, :]).astype(bf16)

    return pl.pallas_call(
        body, grid=(S // tr,),
        in_specs=[pl.BlockSpec((tr, D), lambda i: (i, 0)), pl.BlockSpec((DEPTH, D), lambda i: (0, 0))] + [ANY_SPEC] * len(deps),
        out_specs=pl.BlockSpec((tr, D), lambda i: (i, 0)), out_shape=SDS((S, D), bf16), name=name,
        compiler_params=_cparams(1),
    )(x, g, *deps)


def _loss_kernel(y, tgt):
    tr = 512

    def body(y_ref, t_ref, dy_ref, loss_ref):
        err = y_ref[...] - t_ref[...]
        dy_ref[...] = err * (1.0 / D)
        part = 0.5 * jnp.sum(jnp.mean(err * err, axis=-1, keepdims=True), axis=0, keepdims=True)

        @pl.when(pl.program_id(0) == 0)
        def _():
            loss_ref[...] = jnp.zeros_like(loss_ref)

        loss_ref[...] += jnp.broadcast_to(part, loss_ref.shape)

    return pl.pallas_call(
        body, grid=(S // tr,),
        in_specs=[pl.BlockSpec((tr, D), lambda i: (i, 0)), pl.BlockSpec((tr, D), lambda i: (i, 0))],
        out_specs=[pl.BlockSpec((tr, D), lambda i: (i, 0)), pl.BlockSpec((1, 128), lambda i: (0, 0))],
        out_shape=[SDS((S, D), f32), SDS((1, 128), f32)], name="loss", compiler_params=_cparams(1),
    )(y, tgt)


def _pack_dproj(dqkv, dz, dxbc, ddt):
    tr = 256

    def body(a_ref, b_ref, c_ref, d_ref, o_ref):
        o_ref[:, 0:768] = a_ref[...].astype(bf16)
        o_ref[:, 768:1280] = b_ref[...].astype(bf16)
        o_ref[:, 1280:2304] = c_ref[...].astype(bf16)
        o_ref[:, 2304:2432] = d_ref[...].astype(bf16)
        o_ref[:, 2432:D_IN_PAD] = jnp.zeros((tr, D_IN_PAD - 2432), bf16)

    return pl.pallas_call(
        body, grid=(S // tr,),
        in_specs=[pl.BlockSpec((tr, 768), lambda i: (i, 0)), pl.BlockSpec((tr, 512), lambda i: (i, 0)),
                  pl.BlockSpec((tr, 1024), lambda i: (i, 0)), pl.BlockSpec((tr, 128), lambda i: (i, 0))],
        out_specs=pl.BlockSpec((tr, D_IN_PAD), lambda i: (i, 0)), out_shape=SDS((S, D_IN_PAD), bf16),
        name="pack_dproj", compiler_params=_cparams(1),
    )(dqkv, dz, dxbc, ddt)


def _own_slab_spec(kind, tr, cols, nblk):
    if kind == "stack":
        return pl.BlockSpec((None, tr, cols), lambda i, idx: (idx[0], i, 0))
    if kind == "cols512":
        return pl.BlockSpec((tr, cols), lambda i, idx: (i, idx[0]))
    return pl.BlockSpec((tr, cols), lambda i, idx: (idx[0] * nblk + i, 0))


def _cast_to_full(name, w, kind, full_shape, my_idx, dtype):
    n_layers, rows, cols = w.shape
    tr = min(rows, 256)
    nblk = rows // tr

    def body(idx_ref, w_ref, *o_refs):
        for l in range(n_layers):
            o_refs[l][...] = w_ref[l].astype(dtype)

    grid_spec = pltpu.PrefetchScalarGridSpec(
        num_scalar_prefetch=1, grid=(nblk,), in_specs=[pl.BlockSpec((n_layers, tr, cols), lambda i, idx: (0, i, 0))],
        out_specs=[_own_slab_spec(kind, tr, cols, nblk)] * n_layers)
    return pl.pallas_call(body, grid_spec=grid_spec, out_shape=[SDS(full_shape, dtype)] * n_layers, name=name,
                          compiler_params=_cparams(1))(my_idx, w)


def _adamw_math(w, m, v, g):
    m_new = ADAM_B1 * m + (1.0 - ADAM_B1) * g
    v_new = ADAM_B2 * v + (1.0 - ADAM_B2) * (g * g)
    m_hat = m_new / (1.0 - ADAM_B1 ** ADAM_STEP)
    v_hat = v_new / (1.0 - ADAM_B2 ** ADAM_STEP)
    delta = -ADAM_LR * (m_hat / (jnp.sqrt(v_hat) + ADAM_EPS) + ADAM_WD * w)
    return delta, m_new, v_new


def _adamw_layer(name, kind, layer, w, m, v, land, g_full, my_idx, prev, tr):
    rows2, cols = w.shape
    rows = rows2 // DEPTH
    nblk = rows // tr
    own_spec = _own_slab_spec(kind, tr, cols, nblk)
    n_prev = 0 if prev is None else 4

    def body(idx_ref, w_ref, m_ref, v_ref, land_ref, own_ref, *rest):
        g_ref, d_ref, mo_ref, vo_ref = rest[n_prev:]
        me = idx_ref[0]
        g = None
        for p in range(N_DEV):
            part = jnp.where(me == p, own_ref[...], land_ref[p]).astype(f32)
            g = part if g is None else g + part
        delta, m_new, v_new = _adamw_math(w_ref[...], m_ref[...], v_ref[...], g)
        g_ref[...] = g
        d_ref[...] = delta
        mo_ref[...] = m_new
        vo_ref[...] = v_new

    blk = pl.BlockSpec((tr, cols), lambda i, idx: (layer * nblk + i, 0))
    grid_spec = pltpu.PrefetchScalarGridSpec(
        num_scalar_prefetch=1, grid=(nblk,),
        in_specs=[blk, blk, blk, pl.BlockSpec((N_DEV, tr, cols), lambda i, idx: (0, i, 0)), own_spec] + [ANY_SPEC] * n_prev,
        out_specs=[blk, blk, blk, blk])
    aliases = {} if prev is None else {6 + k: k for k in range(4)}
    return pl.pallas_call(
        body, grid_spec=grid_spec, out_shape=[SDS((rows2, cols), f32)] * 4, name=name, input_output_aliases=aliases,
        compiler_params=_cparams(1),
    )(my_idx, w, m, v, land, g_full, *([] if prev is None else prev))


def _bucket_table():
    qi = np.arange(BLK)[:, None]
    kj = np.arange(2 * BLK)[None, :]
    dist = qi + BLK - kj
    dcl = np.clip(dist, 0, None)
    max_exact = N_BUCKETS // 2
    d_f = np.maximum(dcl, 1).astype(np.float32)
    large = max_exact + (np.log(d_f / np.float32(max_exact)) / np.float32(math.log(128 / max_exact))
                         * np.float32(N_BUCKETS - max_exact)).astype(np.int32)
    large = np.minimum(large, N_BUCKETS - 1)
    bucket = np.where(dcl < max_exact, dcl, large)
    in_window = (dist >= 0) & (dist < BLK)
    return bucket.astype(np.int32), in_window


def _onehot_buckets():
    bucket, _ = _bucket_table()
    oh = (bucket.reshape(-1)[None, :] == np.arange(N_BUCKETS)[:, None]).astype(np.float32)
    return oh


def _bias_build(rel_bias_t, onehot_t):
    def body(r_ref, o_ref, out_ref):
        out_ref[...] = jnp.dot(r_ref[...], o_ref[...], preferred_element_type=f32, precision=HIGHEST)

    tn = 4096
    return pl.pallas_call(
        body, grid=(BLK * 2 * BLK // tn,),
        in_specs=[pl.BlockSpec((NQ, N_BUCKETS), lambda i: (0, 0)), pl.BlockSpec((N_BUCKETS, tn), lambda i: (0, i))],
        out_specs=pl.BlockSpec((NQ, tn), lambda i: (0, i)), out_shape=SDS((NQ, BLK * 2 * BLK), f32), name="bias_build",
        compiler_params=_cparams(1),
    )(rel_bias_t, onehot_t)


def _bias_grad(dbias0, dbias1, onehot_t):
    tn = 4096
    nsteps = BLK * 2 * BLK // tn

    def body(a_ref, b_ref, o_ref, out_ref):
        part = lax.dot_general(a_ref[...] + b_ref[...], o_ref[...], NT_DIMS, preferred_element_type=f32, precision=HIGHEST)

        @pl.when(pl.program_id(0) == 0)
        def _():
            out_ref[...] = part

        @pl.when(pl.program_id(0) > 0)
        def _():
            out_ref[...] += part

    return pl.pallas_call(
        body, grid=(nsteps,),
        in_specs=[pl.BlockSpec((NQ, tn), lambda i: (0, i)), pl.BlockSpec((NQ, tn), lambda i: (0, i)),
                  pl.BlockSpec((N_BUCKETS, tn), lambda i: (0, i))],
        out_specs=pl.BlockSpec((NQ, N_BUCKETS), lambda i: (0, 0)), out_shape=SDS((NQ, N_BUCKETS), f32), name="bias_grad",
        compiler_params=_cparams(1),
    )(dbias0, dbias1, onehot_t)


def _attn_mask(n):
    qi = lax.broadcasted_iota(jnp.int32, (BLK, 2 * BLK), 0)
    kj = lax.broadcasted_iota(jnp.int32, (BLK, 2 * BLK), 1)
    dist = qi + BLK - kj
    first_key = jnp.where(n > 0, 0, BLK)
    return (dist >= 0) & (dist < BLK) & (kj >= first_key)


def _head_norm(t, gain):
    r = lax.rsqrt(jnp.mean(t * t, axis=-1, keepdims=True) + EPS)
    that = t * r
    return that, r, that * gain


def _softmax_with_sink(s, sink):
    m = jnp.maximum(jnp.max(s, axis=-1, keepdims=True), sink)
    p = jnp.exp(s - m)
    psink = jnp.exp(sink - m)
    inv = 1.0 / (jnp.sum(p, axis=-1, keepdims=True) + psink)
    return p * inv, psink * inv


def _attn_fwd(qkv, q_gain, k_gain, sinks, bias, layer):
    def body(q_ref, kc_ref, kp_ref, vc_ref, vp_ref, qg_ref, kg_ref, sk_ref, bias_ref, o_ref):
        n = pl.program_id(0)
        mask = _attn_mask(n)
        qg = qg_ref[layer:layer + 1, :]
        kg = kg_ref[layer:layer + 1, :]
        for j in range(NKV):
            cols = pl.ds(HD * j, HD)
            kb = jnp.concatenate([kp_ref[:, cols], kc_ref[:, cols]], axis=0)
            vb = jnp.concatenate([vp_ref[:, cols], vc_ref[:, cols]], axis=0).astype(bf16)
            kn = _head_norm(kb, kg)[2].astype(bf16)
            for g in range(NQ // NKV):
                h = j * (NQ // NKV) + g
                qn = _head_norm(q_ref[:, pl.ds(HD * h, HD)], qg)[2].astype(bf16)
                s = _dot(qn, kn, NT_DIMS) * (HD ** -0.5) + bias_ref[h]
                s = jnp.where(mask, s, -jnp.inf)
                p, _ = _softmax_with_sink(s, sk_ref[layer:layer + 1, h:h + 1])
                o_ref[:, pl.ds(HD * h, HD)] = _dot(p.astype(bf16), vb, NN_DIMS).astype(bf16)

    prev = lambda n: jnp.maximum(n - 1, 0)
    small = lambda shape: pl.BlockSpec(shape, lambda n: (0,) * len(shape))
    return pl.pallas_call(
        body, grid=(NBLK,),
        in_specs=[pl.BlockSpec((BLK, D_ATTN), lambda n: (n, 0)),
                  pl.BlockSpec((BLK, 128), lambda n: (n, 4)), pl.BlockSpec((BLK, 128), lambda n: (prev(n), 4)),
                  pl.BlockSpec((BLK, 128), lambda n: (n, 5)), pl.BlockSpec((BLK, 128), lambda n: (prev(n), 5)),
                  small((DEPTH, HD)), small((DEPTH, HD)), small((DEPTH, NQ)), small((NQ, BLK, 2 * BLK))],
        out_specs=pl.BlockSpec((BLK, D_ATTN), lambda n: (n, 0)), out_shape=SDS((S, D_ATTN), bf16),
        name="attn_fwd", compiler_params=_cparams(1),
    )(qkv, qkv, qkv, qkv, qkv, q_gain, k_gain, sinks, bias)


def _attn_bwd(qkv, dmix, q_gain, k_gain, sinks, bias, layer):
    def body(q_ref, kc_ref, kp_ref, vc_ref, vp_ref, do_ref, qg_ref, kg_ref, sk_ref, bias_ref,
             dqkv_ref, dbias_ref, dsm_ref, carry):
        i = pl.program_id(0)
        n = NBLK - 1 - i
        mask = _attn_mask(n)
        qg = qg_ref[layer:layer + 1, :]
        kg = kg_ref[layer:layer + 1, :]
        lane = lax.broadcasted_iota(jnp.int32, (1, 128), 1)

        @pl.when(i == 0)
        def _():
            carry[...] = jnp.zeros_like(carry)
            dbias_ref[...] = jnp.zeros_like(dbias_ref)
            dsm_ref[...] = jnp.zeros_like(dsm_ref)

        dqg = jnp.zeros((1, HD), f32)
        dkg = jnp.zeros((1, HD), f32)
        dsink = jnp.zeros((1, 128), f32)
        for j in range(NKV):
            cols = pl.ds(HD * j, HD)
            kb = jnp.concatenate([kp_ref[:, cols], kc_ref[:, cols]], axis=0)
            vb = jnp.concatenate([vp_ref[:, cols], vc_ref[:, cols]], axis=0).astype(bf16)
            khat, rk, kn = _head_norm(kb, kg)
            kn_b = kn.astype(bf16)
            dkn = jnp.zeros((2 * BLK, HD), f32)
            dv = jnp.zeros((2 * BLK, HD), f32)
            for g in range(NQ // NKV):
                h = j * (NQ // NKV) + g
                qhat, rq, qn = _head_norm(q_ref[:, pl.ds(HD * h, HD)], qg)
                qn_b = qn.astype(bf16)
                s = _dot(qn_b, kn_b, NT_DIMS) * (HD ** -0.5) + bias_ref[h]
                s = jnp.where(mask, s, -jnp.inf)
                p, psink = _softmax_with_sink(s, sk_ref[layer:layer + 1, h:h + 1])
                do_b = do_ref[:, pl.ds(HD * h, HD)].astype(bf16)
                dp = _dot(do_b, vb, NT_DIMS)
                delta = jnp.sum(p * dp, axis=-1, keepdims=True)
                ds = p * (dp - delta)
                dsink = dsink + jnp.where(lane == h, -_sum11(psink * delta), 0.0)
                dbias_ref[h] += ds
                ds_b = (ds * (HD ** -0.5)).astype(bf16)
                dqn = _dot(ds_b, kn_b, NN_DIMS)
                dkn = dkn + _dot(ds_b, qn_b, TN_DIMS)
                dv = dv + _dot(p.astype(bf16), do_b, TN_DIMS)
                w = dqn * qg
                dqkv_ref[:, pl.ds(HD * h, HD)] = rq * (w - qhat * jnp.mean(qhat * w, axis=-1, keepdims=True))
                dqg = dqg + jnp.sum(dqn * qhat, axis=0, keepdims=True)
            w = dkn * kg
            dk = rk * (w - khat * jnp.mean(khat * w, axis=-1, keepdims=True))
            dkg = dkg + jnp.sum(dkn * khat, axis=0, keepdims=True)
            dqkv_ref[:, pl.ds(D_ATTN + HD * j, HD)] = dk[BLK:, :] + carry[:, pl.ds(HD * j, HD)]
            dqkv_ref[:, pl.ds(D_ATTN + 128 + HD * j, HD)] = dv[BLK:, :] + carry[:, pl.ds(128 + HD * j, HD)]
            carry[:, pl.ds(HD * j, HD)] = dk[:BLK, :]
            carry[:, pl.ds(128 + HD * j, HD)] = dv[:BLK, :]
        dsm_ref[0:1, 0:HD] += dqg
        dsm_ref[1:2, 0:HD] += dkg
        dsm_ref[2:3, :] += dsink

    rev = lambda i: NBLK - 1 - i
    prev = lambda i: jnp.maximum(NBLK - 2 - i, 0)
    small = lambda shape: pl.BlockSpec(shape, lambda i: (0,) * len(shape))
    return pl.pallas_call(
        body, grid=(NBLK,),
        in_specs=[pl.BlockSpec((BLK, D_ATTN), lambda i: (rev(i), 0)),
                  pl.BlockSpec((BLK, 128), lambda i: (rev(i), 4)), pl.BlockSpec((BLK, 128), lambda i: (prev(i), 4)),
                  pl.BlockSpec((BLK, 128), lambda i: (rev(i), 5)), pl.BlockSpec((BLK, 128), lambda i: (prev(i), 5)),
                  pl.BlockSpec((BLK, D_ATTN), lambda i: (rev(i), 0)),
                  small((DEPTH, HD)), small((DEPTH, HD)), small((DEPTH, NQ)), small((NQ, BLK, 2 * BLK))],
        out_specs=[pl.BlockSpec((BLK, 768), lambda i: (rev(i), 0)), small((NQ, BLK, 2 * BLK)), small((8, 128))],
        out_shape=[SDS((S, 768), f32), SDS((NQ, BLK, 2 * BLK), f32), SDS((8, 128), f32)],
        scratch_shapes=[pltpu.VMEM((BLK, 256), f32)], name="attn_bwd", compiler_params=_cparams(1),
    )(qkv, qkv, qkv, qkv, qkv, dmix, q_gain, k_gain, sinks, bias)


CONV_TC = 128


def _shift_down(u, s):
    if s == 0:
        return u
    rows = lax.broadcasted_iota(jnp.int32, u.shape, 0)
    return jnp.where(rows >= s, pltpu.roll(u, s, 0), 0.0)


def _shift_up(u, s):
    if s == 0:
        return u
    rows = lax.broadcasted_iota(jnp.int32, u.shape, 0)
    return jnp.where(rows < u.shape[0] - s, pltpu.roll(u, u.shape[0] - s, 0), 0.0)


def _conv_specs():
    return [pl.BlockSpec((S, CONV_TC), lambda c: (0, c)),
            pl.BlockSpec((None, 4, CONV_TC), lambda c: (0, 0, c)),
            pl.BlockSpec((DEPTH, CONV_TC), lambda c: (0, c))]


def _conv_pre(u, w_ref, b_ref, layer):
    pre = b_ref[layer:layer + 1, :] + w_ref[3:4, :] * u
    for k in range(3):
        pre = pre + w_ref[k:k + 1, :] * _shift_down(u, 3 - k)
    return pre


def _conv_fwd(xbc, conv_w, conv_b, layer):
    def body(u_ref, w_ref, b_ref, o_ref):
        pre = _conv_pre(u_ref[...], w_ref, b_ref, layer)
        o_ref[...] = pre * _sigmoid(pre)

    specs = _conv_specs()
    specs[1] = pl.BlockSpec((None, 4, CONV_TC), lambda c: (layer, 0, c))
    return pl.pallas_call(
        body, grid=(D_CONV // CONV_TC,), in_specs=specs, out_specs=pl.BlockSpec((S, CONV_TC), lambda c: (0, c)),
        out_shape=SDS((S, D_CONV), f32), name="conv_fwd", compiler_params=_cparams(1),
    )(xbc, conv_w, conv_b)


def _conv_bwd(xbc, dact, conv_w, conv_b, layer):
    def body(u_ref, w_ref, b_ref, da_ref, du_ref, dw_ref, db_ref):
        u = u_ref[...]
        pre = _conv_pre(u, w_ref, b_ref, layer)
        sg = _sigmoid(pre)
        dpre = da_ref[...] * (sg * (1.0 + pre * (1.0 - sg)))
        du = w_ref[3:4, :] * dpre
        for k in range(3):
            du = du + w_ref[k:k + 1, :] * _shift_up(dpre, 3 - k)
        du_ref[...] = du
        db_ref[...] = jnp.broadcast_to(jnp.sum(dpre, axis=0, keepdims=True), db_ref.shape)
        dw_ref[...] = jnp.zeros_like(dw_ref)
        for k in range(4):
            dw_ref[k:k + 1, :] = jnp.sum(dpre * _shift_down(u, 3 - k), axis=0, keepdims=True)

    specs = _conv_specs()
    specs[1] = pl.BlockSpec((None, 4, CONV_TC), lambda c: (layer, 0, c))
    col = pl.BlockSpec((S, CONV_TC), lambda c: (0, c))
    row8 = pl.BlockSpec((8, CONV_TC), lambda c: (0, c))
    return pl.pallas_call(
        body, grid=(D_CONV // CONV_TC,), in_specs=[*specs, col], out_specs=[col, row8, row8],
        out_shape=[SDS((S, D_CONV), f32), SDS((8, D_CONV), f32), SDS((8, D_CONV), f32)], name="conv_bwd",
        compiler_params=_cparams(1),
    )(xbc, conv_w, conv_b, dact)


def _tri():
    return (lax.broadcasted_iota(jnp.int32, (BLK, BLK), 0) >= lax.broadcasted_iota(jnp.int32, (BLK, BLK), 1))


def _ssd_scalars(dt_ref, dtb_ref, alog_ref, layer):
    raw = dt_ref[:, 0:NSSM] + dtb_ref[layer:layer + 1, :]
    dtv = jnp.maximum(raw, 0.0) + jnp.log(1.0 + jnp.exp(-jnp.abs(raw)))
    a = -jnp.exp(alog_ref[layer:layer + 1, :])
    acs = jnp.dot(_tri().astype(f32), dtv * a, preferred_element_type=f32, precision=HIGHEST)
    return raw, dtv, a, acs


def _ssd_head_fwd(xc_ref, cb, cm_b, h, dtv, acs, acs_t, e_all, prev_b, dskip):
    xs = xc_ref[:, pl.ds(HD * h, HD)]
    xdt = xs * dtv[:, h:h + 1]
    seg = acs[:, h:h + 1] - acs_t[h:h + 1, :]
    decay = jnp.exp(jnp.where(_tri(), seg, -jnp.inf))
    m = cb * decay
    t_off = _dot(cm_b, prev_b, NT_DIMS)
    y_off = t_off * e_all[:, h:h + 1]
    y = _dot(m.astype(bf16), xdt.astype(bf16), NN_DIMS) + y_off + xs * dskip[:, h:h + 1]
    return xs, xdt, decay, m, t_off, y_off, y


def _ssd_fwd(xact, z, dt, attn, dt_bias, a_log, d_skip, norm_g, layer):
    def body(xc_ref, z_ref, dt_ref, at_ref, dtb_ref, alog_ref, dsk_ref, ng_ref, mix_ref, hs_ref, h_ref, y_ref):
        n = pl.program_id(0)

        @pl.when(n == 0)
        def _():
            h_ref[...] = jnp.zeros_like(h_ref)

        _, dtv, _, acs = _ssd_scalars(dt_ref, dtb_ref, alog_ref, layer)
        acs_t = acs.T
        last = acs[BLK - 1:BLK, :]
        dte = jnp.exp(last - acs)
        e_all = jnp.exp(acs)
        cd = jnp.exp(last)
        dskip = dsk_ref[layer:layer + 1, :]
        hs_ref[...] = h_ref[...]
        for g in range(NGRP):
            bm = xc_ref[:, pl.ds(D_SSM + NSTATE * g, NSTATE)]
            cm_b = xc_ref[:, pl.ds(D_SSM + NGRP * NSTATE + NSTATE * g, NSTATE)].astype(bf16)
            cb = _dot(cm_b, bm.astype(bf16), NT_DIMS)
            for r in range(NSSM // NGRP):
                h = g * (NSSM // NGRP) + r
                rows = pl.ds(HD * h, HD)
                prev = h_ref[rows, :]
                _, xdt, _, _, _, _, y = _ssd_head_fwd(xc_ref, cb, cm_b, h, dtv, acs, acs_t, e_all, prev.astype(bf16), dskip)
                y_ref[:, pl.ds(HD * h, HD)] = y
                bd = (bm * dte[:, h:h + 1]).astype(bf16)
                h_ref[rows, :] = prev * cd[:, h:h + 1] + _dot(xdt.astype(bf16), bd, TN_DIMS)
        zv = z_ref[...]
        yz = y_ref[...] * (zv * _sigmoid(zv))
        mix_ref[:, 0:D_ATTN] = at_ref[...]
        gw = D_SSM // NGRP
        for g in range(NGRP):
            yg = yz[:, gw * g:gw * (g + 1)]
            rs = lax.rsqrt(jnp.mean(yg * yg, axis=-1, keepdims=True) + EPS)
            mix_ref[:, D_ATTN + gw * g:D_ATTN + gw * (g + 1)] = (yg * rs * ng_ref[layer:layer + 1, gw * g:gw * (g + 1)]).astype(bf16)

    small = lambda shape: pl.BlockSpec(shape, lambda n: (0,) * len(shape))
    return pl.pallas_call(
        body, grid=(NBLK,),
        in_specs=[pl.BlockSpec((BLK, D_CONV), lambda n: (n, 0)), pl.BlockSpec((BLK, D_SSM), lambda n: (n, 0)),
                  pl.BlockSpec((BLK, 128), lambda n: (n, 0)), pl.BlockSpec((BLK, D_ATTN), lambda n: (n, 0)),
                  small((DEPTH, NSSM)), small((DEPTH, NSSM)), small((DEPTH, NSSM)), small((DEPTH, D_SSM))],
        out_specs=[pl.BlockSpec((BLK, D), lambda n: (n, 0)), pl.BlockSpec((None, NSSM * HD, NSTATE), lambda n: (n, 0, 0))],
        out_shape=[SDS((S, D), bf16), SDS((NBLK, NSSM * HD, NSTATE), f32)],
        scratch_shapes=[pltpu.VMEM((NSSM * HD, NSTATE), f32), pltpu.VMEM((BLK, D_SSM), f32)],
        name="ssd_fwd", compiler_params=_cparams(1),
    )(xact, z, dt, attn, dt_bias, a_log, d_skip, norm_g)


def _ssd_bwd(xact, z, dt, dmix, hs, dt_bias, a_log, d_skip, norm_g, layer):
    def body(xc_ref, z_ref, dt_ref, do_ref, hs_ref, dtb_ref, alog_ref, dsk_ref, ng_ref,
             dz_ref, dx_ref, ddt_ref, dsm_ref, dh_ref, y_ref, dy_ref):
        i = pl.program_id(0)

        @pl.when(i == 0)
        def _():
            dh_ref[...] = jnp.zeros_like(dh_ref)
            dsm_ref[...] = jnp.zeros_like(dsm_ref)

        raw, dtv, a, acs = _ssd_scalars(dt_ref, dtb_ref, alog_ref, layer)
        acs_t = acs.T
        last = acs[BLK - 1:BLK, :]
        dte = jnp.exp(last - acs)
        e_all = jnp.exp(acs)
        cd = jnp.exp(last)
        dskip = dsk_ref[layer:layer + 1, :]
        lane8 = lax.broadcasted_iota(jnp.int32, (1, NSSM), 1)
        sub8 = lax.broadcasted_iota(jnp.int32, (NSSM, 1), 0)

        for g in range(NGRP):
            bm_b = xc_ref[:, pl.ds(D_SSM + NSTATE * g, NSTATE)].astype(bf16)
            cm_b = xc_ref[:, pl.ds(D_SSM + NGRP * NSTATE + NSTATE * g, NSTATE)].astype(bf16)
            cb = _dot(cm_b, bm_b, NT_DIMS)
            for r in range(NSSM // NGRP):
                h = g * (NSSM // NGRP) + r
                prev_b = hs_ref[pl.ds(HD * h, HD), :].astype(bf16)
                y_ref[:, pl.ds(HD * h, HD)] = _ssd_head_fwd(xc_ref, cb, cm_b, h, dtv, acs, acs_t, e_all, prev_b, dskip)[6]

        zv = z_ref[...]
        sz = _sigmoid(zv)
        gz = zv * sz
        yv = y_ref[...]
        yz = yv * gz
        gw = D_SSM // NGRP
        for g in range(NGRP):
            sl = slice(gw * g, gw * (g + 1))
            yg = yz[:, sl]
            rs = lax.rsqrt(jnp.mean(yg * yg, axis=-1, keepdims=True) + EPS)
            yhat = yg * rs
            dog = do_ref[:, sl]
            w = dog * ng_ref[layer:layer + 1, sl]
            dyz = rs * (w - yhat * jnp.mean(yhat * w, axis=-1, keepdims=True))
            dsm_ref[0:1, sl] += jnp.sum(dog * yhat, axis=0, keepdims=True)
            dy_ref[:, sl] = dyz * gz[:, sl]
            dz_ref[:, sl] = dyz * yv[:, sl] * (sz[:, sl] * (1.0 + zv[:, sl] * (1.0 - sz[:, sl])))

        dacs = jnp.zeros((BLK, NSSM), f32)
        dacs_cols = jnp.zeros((NSSM, BLK), f32)
        dlast = jnp.zeros((1, NSSM), f32)
        ddtv = jnp.zeros((BLK, NSSM), f32)
        ddsk = jnp.zeros((1, NSSM), f32)
        for g in range(NGRP):
            bm = xc_ref[:, pl.ds(D_SSM + NSTATE * g, NSTATE)]
            bm_b = bm.astype(bf16)
            cm_b = xc_ref[:, pl.ds(D_SSM + NGRP * NSTATE + NSTATE * g, NSTATE)].astype(bf16)
            cb = _dot(cm_b, bm_b, NT_DIMS)
            dcb = jnp.zeros((BLK, BLK), f32)
            dbm = jnp.zeros((BLK, NSTATE), f32)
            dcm = jnp.zeros((BLK, NSTATE), f32)
            for r in range(NSSM // NGRP):
                h = g * (NSSM // NGRP) + r
                rows = pl.ds(HD * h, HD)
                oh = (lane8 == h).astype(f32)
                prev = hs_ref[rows, :]
                prev_b = prev.astype(bf16)
                xs, xdt, decay, m, t_off, y_off, _ = _ssd_head_fwd(xc_ref, cb, cm_b, h, dtv, acs, acs_t, e_all, prev_b, dskip)
                xdt_b = xdt.astype(bf16)
                dy = dy_ref[:, pl.ds(HD * h, HD)]
                dy_b = dy.astype(bf16)
                ddsk = ddsk + oh * _sum11(dy * xs)
                dxs = dy * dskip[:, h:h + 1]
                dm = _dot(dy_b, xdt_b, NT_DIMS)
                dxdt = _dot(m.astype(bf16), dy_b, TN_DIMS)
                dseg = dm * m
                dcb = dcb + dm * decay
                dacs = dacs + oh * jnp.sum(dseg, axis=1, keepdims=True)
                dacs_cols = dacs_cols + (sub8 == h).astype(f32) * jnp.sum(dseg, axis=0, keepdims=True)
                dt_off = (dy * e_all[:, h:h + 1]).astype(bf16)
                dacs = dacs + oh * jnp.sum(dy * y_off, axis=1, keepdims=True)
                dcm = dcm + _dot(dt_off, prev_b, NN_DIMS)
                dprev = _dot(dt_off, cm_b, TN_DIMS)
                dhn = dh_ref[rows, :]
                dhn_b = dhn.astype(bf16)
                cd_h = cd[:, h:h + 1]
                dprev = dprev + dhn * cd_h
                dlast = dlast + oh * (_sum11(dhn * prev) * cd_h)
                dte_h = dte[:, h:h + 1]
                bd_b = (bm * dte_h).astype(bf16)
                dxdt = dxdt + _dot(bd_b, dhn_b, NT_DIMS)
                dbd = _dot(xdt_b, dhn_b, NN_DIMS)
                dbm = dbm + dbd * dte_h
                tmp = jnp.sum(dbd * bm, axis=1, keepdims=True) * dte_h
                dlast = dlast + oh * _sum11(tmp)
                dacs = dacs - oh * tmp
                dxs = dxs + dxdt * dtv[:, h:h + 1]
                ddtv = ddtv + oh * jnp.sum(dxdt * xs, axis=1, keepdims=True)
                dh_ref[rows, :] = dprev
                dx_ref[:, pl.ds(HD * h, HD)] = dxs
            dcb_b = dcb.astype(bf16)
            dx_ref[:, pl.ds(D_SSM + NSTATE * g, NSTATE)] = dbm + _dot(dcb_b, cm_b, TN_DIMS)
            dx_ref[:, pl.ds(D_SSM + NGRP * NSTATE + NSTATE * g, NSTATE)] = dcm + _dot(dcb_b, bm_b, NN_DIMS)

        row = lax.broadcasted_iota(jnp.int32, (BLK, 1), 0)
        dacs = dacs - dacs_cols.T + jnp.where(row == BLK - 1, dlast, 0.0)
        dda = lax.dot_general(_tri().astype(f32), dacs, TN_DIMS, preferred_element_type=f32, precision=HIGHEST)
        ddtv = ddtv + dda * a
        da = jnp.sum(dda * dtv, axis=0, keepdims=True)
        draw = ddtv * _sigmoid(raw)
        ddt_ref[...] = jnp.zeros_like(ddt_ref)
        ddt_ref[:, 0:NSSM] = draw
        dsm_ref[1:2, 0:NSSM] += jnp.sum(draw, axis=0, keepdims=True)
        dsm_ref[2:3, 0:NSSM] += da * a
        dsm_ref[3:4, 0:NSSM] += ddsk

    rev = lambda i: NBLK - 1 - i
    small = lambda shape: pl.BlockSpec(shape, lambda i: (0,) * len(shape))
    return pl.pallas_call(
        body, grid=(NBLK,),
        in_specs=[pl.BlockSpec((BLK, D_CONV), lambda i: (rev(i), 0)), pl.BlockSpec((BLK, D_SSM), lambda i: (rev(i), 0)),
                  pl.BlockSpec((BLK, 128), lambda i: (rev(i), 0)), pl.BlockSpec((BLK, D_SSM), lambda i: (rev(i), 1)),
                  pl.BlockSpec((None, NSSM * HD, NSTATE), lambda i: (rev(i), 0, 0)),
                  small((DEPTH, NSSM)), small((DEPTH, NSSM)), small((DEPTH, NSSM)), small((DEPTH, D_SSM))],
        out_specs=[pl.BlockSpec((BLK, D_SSM), lambda i: (rev(i), 0)), pl.BlockSpec((BLK, D_CONV), lambda i: (rev(i), 0)),
                   pl.BlockSpec((BLK, 128), lambda i: (rev(i), 0)), small((8, D_SSM))],
        out_shape=[SDS((S, D_SSM), f32), SDS((S, D_CONV), f32), SDS((S, 128), f32), SDS((8, D_SSM), f32)],
        scratch_shapes=[pltpu.VMEM((NSSM * HD, NSTATE), f32), pltpu.VMEM((BLK, D_SSM), f32), pltpu.VMEM((BLK, D_SSM), f32)],
        name="ssd_bwd", compiler_params=_cparams(1),
    )(xact, z, dt, dmix, hs, dt_bias, a_log, d_skip, norm_g)


def _my_place():
    return lax.axis_index("x"), lax.axis_index("y"), lax.axis_index("c")


def _dev_index(px, py, pc):
    return 4 * px + 2 * py + pc


def _slab2(kind, ref, idx):
    if kind == "stack":
        return ref.at[idx]
    if kind == "rows128":
        return ref.at[pl.ds(pl.multiple_of(idx * 128, 128), 128), :]
    if kind == "rows512":
        return ref.at[pl.ds(pl.multiple_of(idx * 512, 512), 512), :]
    return ref.at[:, pl.ds(pl.multiple_of(idx * 512, 512), 512)]


def _slab_shape(kind, full_shape):
    if kind == "stack":
        return tuple(full_shape[1:])
    if kind == "rows128":
        return (128, full_shape[1])
    if kind == "rows512":
        return (512, full_shape[1])
    return (full_shape[0], 512)


KIND = dict(w_in="stack", w_out="rows128", w_up="cols512", w_down="rows512", conv_w="stack")
FULL_SHAPE = dict(w_in=(N_DEV, D, D_IN // N_DEV), w_out=(D, D), w_up=(D, D_FF), w_down=(D_FF, D))
HBM_SPEC = pl.BlockSpec(memory_space=pltpu.HBM)
SEM_SPEC = pl.BlockSpec(memory_space=pltpu.SEMAPHORE)
SIDE_EFFECT = pltpu.SideEffectType.DATAFLOW_SIDE_EFFECTING


def _peers_all():
    x, y, c = _my_place()
    return [(x ^ ((r >> 2) & 1), y ^ ((r >> 1) & 1), c ^ (r & 1)) for r in range(1, N_DEV)]


def _split_start(name, bufs, n_copies, plan, deps=()):
    nb = len(bufs)

    def body(*refs):
        ins = refs[:nb]
        send_sems, recv_sems = refs[nb + len(deps)], refs[nb + len(deps) + 1]
        token = refs[-1]
        for i, (src, dst, dev) in enumerate(plan(ins)):
            pltpu.make_async_remote_copy(src_ref=src, dst_ref=dst, send_sem=send_sems.at[i], recv_sem=recv_sems.at[i],
                                         device_id=dev, device_id_type=MESH).start()
        token[...] = jnp.zeros_like(token)

    outs = pl.pallas_call(
        body, name=name,
        out_shape=(pltpu.SemaphoreType.DMA((n_copies,)), pltpu.SemaphoreType.DMA((n_copies,)),
                   *[pltpu.HBM(b.shape, b.dtype) for b in bufs], SDS((8, 128), f32)),
        in_specs=[HBM_SPEC] * nb + [ANY_SPEC] * len(deps),
        out_specs=(SEM_SPEC, SEM_SPEC, *[HBM_SPEC] * nb, pl.BlockSpec(memory_space=pltpu.VMEM)),
        input_output_aliases={i: 2 + i for i in range(nb)},
        compiler_params=pltpu.CompilerParams(has_side_effects=SIDE_EFFECT),
    )(*[pltpu.with_memory_space_constraint(b, pltpu.HBM) for b in bufs], *deps)
    return dict(send=outs[0], recv=outs[1], bufs=list(outs[2:2 + nb]), token=outs[-1], plan=plan, n=n_copies)


def _split_wait(name, started, after):
    bufs = started["bufs"]
    nb = len(bufs)
    plan = started["plan"]

    def body(*refs):
        ins = refs[:nb]
        send_sems, recv_sems = refs[nb], refs[nb + 1]
        for i, (src, dst, dev) in enumerate(plan(ins)):
            cp = pltpu.make_async_remote_copy(src_ref=src, dst_ref=dst, send_sem=send_sems.at[i], recv_sem=recv_sems.at[i],
                                              device_id=dev, device_id_type=MESH)
            cp.wait_send()
            cp.wait_recv()

    outs = pl.pallas_call(
        body, name=name, out_shape=tuple(pltpu.HBM(b.shape, b.dtype) for b in bufs),
        in_specs=[HBM_SPEC] * nb + [SEM_SPEC, SEM_SPEC] + [ANY_SPEC] * len(after), out_specs=(HBM_SPEC,) * nb,
        input_output_aliases={i: i for i in range(nb)},
        compiler_params=pltpu.CompilerParams(has_side_effects=SIDE_EFFECT),
    )(*bufs, started["send"], started["recv"], *after)
    return list(outs)


def _gather_start(name, names, fulls, deps):
    n_t = len(names)

    def plan(refs):
        x, y, c = _my_place()
        my_idx = _dev_index(x, y, c)
        targets = [(x, y, 1 - c), (1 - x, y, c), (x, 1 - y, c), (1 - x, 1 - y, c)]
        slabs = [_slab2(KIND[names[t]], refs[t], my_idx) for t in range(n_t)]
        return [(slabs[t], slabs[t], dev) for t in range(n_t) for dev in targets]

    return _split_start(name, list(fulls), 4 * n_t, plan, deps)


def _gather_finish(name, names, started, after):
    n_t = len(names)
    fulls = _split_wait(name + "_wait", started, after)
    slab_shapes = [SDS(_slab_shape(KIND[n], f.shape), f.dtype) for n, f in zip(names, fulls)]

    def body(*refs):
        ins = refs[:n_t]
        outs = refs[n_t:2 * n_t]
        stage = refs[2 * n_t:3 * n_t]
        load_sems, send_sems, recv_sems = refs[3 * n_t:]
        x, y, c = _my_place()
        chips = [(1 - x, y), (x, 1 - y), (1 - x, 1 - y)]
        pairs = [(t, j) for t in range(n_t) for j in range(3)]
        loads = [pltpu.make_async_copy(_slab2(KIND[names[t]], ins[t], _dev_index(*chips[j], c)), stage[t].at[j], load_sems.at[t, j])
                 for t, j in pairs]
        for cp in loads:
            cp.start()

        def copy(t, j, core):
            return pltpu.make_async_remote_copy(
                src_ref=stage[t].at[j], dst_ref=_slab2(KIND[names[t]], outs[t], _dev_index(*chips[j], core)),
                send_sem=send_sems.at[t, j], recv_sem=recv_sems.at[t, j], device_id=(x, y, 1 - c), device_id_type=MESH)

        sends = [copy(t, j, c) for t, j in pairs]
        for ld, cp in zip(loads, sends):
            ld.wait()
            cp.start()
        for t, j in pairs:
            copy(t, j, 1 - c).wait_recv()
        for cp in sends:
            cp.wait_send()

    return pl.pallas_call(
        body, in_specs=[ANY_SPEC] * n_t, out_specs=[ANY_SPEC] * n_t, out_shape=[SDS(b.shape, b.dtype) for b in fulls],
        input_output_aliases={t: t for t in range(n_t)},
        scratch_shapes=[pltpu.VMEM((3,) + s.shape, s.dtype) for s in slab_shapes]
        + [pltpu.SemaphoreType.DMA((n_t, 3)), pltpu.SemaphoreType.DMA((n_t, 3)), pltpu.SemaphoreType.DMA((n_t, 3))],
        name=name + "_pass", compiler_params=pltpu.CompilerParams(vmem_limit_bytes=VMEM_LIMIT),
    )(*fulls)


def _exchange_start(name, names, grads, deps):
    n_t = len(names)
    lands = [lax.empty((N_DEV,) + _slab_shape(KIND[n], g.shape), g.dtype) for n, g in zip(names, grads)]

    def plan(refs):
        my_idx = _dev_index(*_my_place())
        return [(_slab2(KIND[names[t]], refs[t], _dev_index(*peer)), refs[n_t + t].at[my_idx], peer)
                for t in range(n_t) for peer in _peers_all()]

    return _split_start(name, list(grads) + lands, 7 * n_t, plan, deps)


def _small_exchange_start(part, deps):
    land = lax.empty((N_DEV,) + part.shape, part.dtype)

    def plan(refs):
        my_idx = _dev_index(*_my_place())
        return [(refs[0], refs[1].at[my_idx], peer) for peer in _peers_all()]

    return _split_start("small_exchange", [part, land], N_DEV - 1, plan, deps)


def _w_in_assemble(stacked):
    tr = 256
    sh = D_IN // N_DEV

    def body(i_ref, o_ref):
        for j in range(N_DEV):
            o_ref[:, sh * j:sh * (j + 1)] = i_ref[j]
        o_ref[:, D_IN:D_IN_PAD] = jnp.zeros((tr, D_IN_PAD - D_IN), bf16)

    return pl.pallas_call(
        body, grid=(D // tr,), in_specs=[pl.BlockSpec((N_DEV, tr, sh), lambda i: (0, i, 0))],
        out_specs=pl.BlockSpec((None, tr, D_IN_PAD), lambda i: (0, i, 0)), out_shape=SDS((1, D, D_IN_PAD), bf16),
        name="w_in_assemble", compiler_params=_cparams(1),
    )(stacked)


def _w_in_slabs(dw_in):
    tr = 256
    sh = D_IN // N_DEV

    def body(i_ref, o_ref):
        for j in range(N_DEV):
            o_ref[j] = i_ref[:, sh * j:sh * (j + 1)]

    return pl.pallas_call(
        body, grid=(D // tr,), in_specs=[pl.BlockSpec((tr, D_IN_PAD), lambda i: (i, 0))],
        out_specs=pl.BlockSpec((N_DEV, tr, sh), lambda i: (0, i, 0)), out_shape=SDS((N_DEV, D, sh), bf16),
        name="w_in_slabs", compiler_params=_cparams(1),
    )(dw_in)


SMALL_NAMES = ("mix_norm_g", "mlp_norm_g", "conv_b", "ssm_norm_g", "q_gain", "k_gain", "sinks", "dt_bias", "a_log", "d_skip",
               "rel_bias", "conv_w")
MISC_LANES = dict(q_gain=(LANE_QG, HD), k_gain=(LANE_KG, HD), sinks=(LANE_SINK, NQ), dt_bias=(LANE_DTB, NSSM),
                  a_log=(LANE_ALOG, NSSM), d_skip=(LANE_DSKIP, NSSM))


def _pack_small_grads(smalls, drel_t, loss):
    def body(*refs):
        o_ref = refs[-1]
        drel_ref, loss_ref = refs[-3], refs[-2]
        o_ref[...] = jnp.zeros_like(o_ref)
        for l in range(DEPTH):
            mixg, mlpg, convb, convw, ssd, attn = refs[6 * l:6 * l + 6]
            o_ref[ROW_MIXG + l:ROW_MIXG + l + 1, :] = mixg[...]
            o_ref[ROW_MLPG + l:ROW_MLPG + l + 1, :] = mlpg[...]
            o_ref[ROW_CONVB + l:ROW_CONVB + l + 1, :] = convb[0:1, :]
            o_ref[ROW_SSMG + l:ROW_SSMG + l + 1, 0:D_SSM] = ssd[0:1, :]
            o_ref[ROW_CONVW + 4 * l:ROW_CONVW + 4 * l + 4, :] = convw[0:4, :]
            row = slice(ROW_MISC + l, ROW_MISC + l + 1)
            o_ref[row, LANE_QG:LANE_QG + HD] = attn[0:1, 0:HD]
            o_ref[row, LANE_KG:LANE_KG + HD] = attn[1:2, 0:HD]
            o_ref[row, LANE_SINK:LANE_SINK + NQ] = attn[2:3, 0:NQ]
            o_ref[row, LANE_DTB:LANE_DTB + NSSM] = ssd[1:2, 0:NSSM]
            o_ref[row, LANE_ALOG:LANE_ALOG + NSSM] = ssd[2:3, 0:NSSM]
            o_ref[row, LANE_DSKIP:LANE_DSKIP + NSSM] = ssd[3:4, 0:NSSM]
        o_ref[ROW_RELB:ROW_RELB + NQ, 0:N_BUCKETS] = drel_ref[...]
        o_ref[ROW_LOSS:ROW_LOSS + 1, 0:1] = loss_ref[0:1, 0:1]

    args = []
    for sm in smalls:
        args += [sm["mix_norm_g"], sm["mlp_norm_g"], sm["conv_b"], sm["conv_w"], sm["ssd"], sm["attn"]]
    args += [drel_t, loss]
    return pl.pallas_call(body, out_shape=SDS((SMALL_ROWS, D), f32), name="pack_small_grads")(*args)


def _adamw_small(part, land, w, m, v):
    n = len(SMALL_NAMES)

    def grad_of(name, g_ref):
        if name == "mix_norm_g":
            return g_ref[ROW_MIXG:ROW_MIXG + DEPTH, :]
        if name == "mlp_norm_g":
            return g_ref[ROW_MLPG:ROW_MLPG + DEPTH, :]
        if name == "conv_b":
            return g_ref[ROW_CONVB:ROW_CONVB + DEPTH, :]
        if name == "ssm_norm_g":
            return g_ref[ROW_SSMG:ROW_SSMG + DEPTH, 0:D_SSM]
        if name == "rel_bias":
            return g_ref[ROW_RELB:ROW_RELB + NQ, 0:N_BUCKETS].T
        lane, width = MISC_LANES[name]
        return g_ref[ROW_MISC:ROW_MISC + DEPTH, lane:lane + width]

    def body(part_ref, land_ref, *refs):
        ws, ms, vs = refs[:n], refs[n:2 * n], refs[2 * n:3 * n]
        loss_ref = refs[3 * n]
        outs = refs[3 * n + 1:-1]
        g_ref = refs[-1]
        me = _dev_index(*_my_place())
        for p in range(N_DEV):
            term = jnp.where(me == p, part_ref[...], land_ref[p])
            if p == 0:
                g_ref[...] = term
            else:
                g_ref[...] += term
        loss_ref[...] = g_ref[ROW_LOSS:ROW_LOSS + 1, 0:128]
        my_cols = pl.ds(pl.multiple_of(me * 128, 128), 128)
        for k, name in enumerate(SMALL_NAMES):
            g_out, d_out, m_out, v_out = outs[4 * k:4 * k + 4]
            if name == "conv_w":
                for l in range(DEPTH):
                    g = g_ref[ROW_CONVW + 4 * l:ROW_CONVW + 4 * l + 4, my_cols]
                    delta, m_new, v_new = _adamw_math(ws[k][l], ms[k][l], vs[k][l], g)
                    g_out[l], d_out[l], m_out[l], v_out[l] = g, delta, m_new, v_new
            else:
                g = grad_of(name, g_ref)
                delta, m_new, v_new = _adamw_math(ws[k][...], ms[k][...], vs[k][...], g)
                g_out[...], d_out[...], m_out[...], v_out[...] = g, delta, m_new, v_new

    ws = [w[name] for name in SMALL_NAMES]
    out_shape = [SDS((1, 128), f32)]
    for a in ws:
        out_shape += [SDS(a.shape, f32)] * 4
    return pl.pallas_call(body, out_shape=out_shape, name="adamw_small", scratch_shapes=[pltpu.VMEM((SMALL_ROWS, D), f32)])(
        part, land, *ws, *[m[name] for name in SMALL_NAMES], *[v[name] for name in SMALL_NAMES])


def _plain(tm, tn):
    return pl.BlockSpec((tm, tn), lambda i, j, k: (i, j))


def _rowblk(tm, width):
    return pl.BlockSpec((tm, width), lambda i, j, k: (i, 0))


def _store_epi(dtype):
    def epi(acc, i, j, ex, outs):
        outs[0][...] = acc.astype(dtype)
    return epi


def _layer_fwd(l, x, p, get_weights, bias, deps):
    wts = get_weights(l, "in", [x])
    h1 = _rms_fwd("rms_mix", x, p["mix_norm_g"], l, deps)

    def inproj_epi(acc, i, j, ex, outs):
        outs[0][...] = acc[:, 0:768]
        outs[1][...] = acc[:, 768:1280]
        outs[2][...] = acc[:, 1280:2304]
        outs[3][...] = acc[:, 2304:2432]

    tm = 256
    qkv, z, xbc, dt = _matmul(
        "in_proj", "nn", h1, wts["w_in"], tm=tm, tn=D_IN_PAD, tk=D,
        out_shape=[SDS((S, 768), f32), SDS((S, 512), f32), SDS((S, 1024), f32), SDS((S, 128), f32)],
        out_specs=[_rowblk(tm, 768), _rowblk(tm, 512), _rowblk(tm, 1024), _rowblk(tm, 128)], epilogue=inproj_epi)
    attn = _attn_fwd(qkv, p["q_gain"], p["k_gain"], p["sinks"], bias, l)
    xact = _conv_fwd(xbc, wts["conv_w"], p["conv_b"], l)
    mix, hs = _ssd_fwd(xact, z, dt, attn, p["dt_bias"], p["a_log"], p["d_skip"], p["ssm_norm_g"], l)
    wts = dict(wts, **get_weights(l, "rest", [mix]))

    def resid_epi(acc, i, j, ex, outs):
        outs[0][...] = ex[0][...] + acc

    x_mid = _matmul("out_proj", "nn", mix, wts["w_out"], tm=256, tn=D, tk=D, out_shape=SDS((S, D), f32),
                    out_specs=_plain(256, D), epilogue=resid_epi, extras=(x,), extra_specs=(_plain(256, D),))
    h2 = _rms_fwd("rms_mlp", x_mid, p["mlp_norm_g"], l)

    def up_epi(acc, i, j, ex, outs):
        r = jnp.maximum(acc, 0.0)
        outs[0][...] = r.astype(bf16)
        outs[1][...] = (r * r).astype(bf16)

    r_act, a_act = _matmul("mlp_up", "nn", h2, wts["w_up"], tm=256, tn=D_FF, tk=D,
                           out_shape=[SDS((S, D_FF), bf16)] * 2, out_specs=[_plain(256, D_FF)] * 2, epilogue=up_epi)
    x_out = _matmul("mlp_down", "nn", a_act, wts["w_down"], tm=256, tn=D, tk=D_FF, out_shape=SDS((S, D), f32),
                    out_specs=_plain(256, D), epilogue=resid_epi, extras=(x_mid,), extra_specs=(_plain(256, D),))
    saved = dict(x=x, h1=h1, qkv=qkv, z=z, xbc=xbc, dt=dt, xact=xact, mix=mix, hs=hs, x_mid=x_mid, h2=h2, r=r_act, a=a_act,
                 wts=wts)
    return x_out, saved


def _layer_bwd(l, dx_out, sv, p, bias, deps, send):
    wts = sv["wts"]

    def du_epi(acc, i, j, ex, outs):
        outs[0][...] = (acc * (2.0 * ex[0][...].astype(f32))).astype(bf16)

    du = _matmul("mlp_da", "nt", dx_out, wts["w_down"], tm=256, tn=D_FF, tk=D, out_shape=SDS((S, D_FF), bf16),
                 out_specs=_plain(256, D_FF), epilogue=du_epi, extras=(sv["r"],), extra_specs=(_plain(256, D_FF),), deps=deps)
    dw_down = _matmul("dw_down", "tn", sv["a"], dx_out, tm=1024, tn=D, tk=S, out_shape=SDS((D_FF, D), bf16),
                      out_specs=_plain(1024, D), epilogue=_store_epi(bf16))
    dw_up = _matmul("dw_up", "tn", sv["h2"], du, tm=D, tn=1024, tk=S, out_shape=SDS((D, D_FF), bf16),
                    out_specs=_plain(D, 1024), epilogue=_store_epi(bf16))
    deps = send(l, dict(w_down=dw_down, w_up=dw_up))
    gfull = pl.BlockSpec((DEPTH, D), lambda i, j, k: (0, 0))
    grow = pl.BlockSpec((1, D), lambda i, j, k: (0, 0))
    dx_mid, dg_mlp = _matmul(
        "mlp_dh", "nt", du, wts["w_up"], tm=256, tn=D, tk=D_FF, out_shape=[SDS((S, D), f32), SDS((1, D), f32)],
        out_specs=[_plain(256, D), grow], epilogue=_rms_bwd_epilogue(l),
        extras=(sv["x_mid"], p["mlp_norm_g"], dx_out), extra_specs=(_plain(256, D), gfull, _plain(256, D)), deps=deps)
    dmix = _matmul("out_proj_da", "nt", dx_mid, wts["w_out"], tm=256, tn=D, tk=D, out_shape=SDS((S, D), f32),
                   out_specs=_plain(256, D), epilogue=_store_epi(f32))
    dw_out = _matmul("dw_out", "tn", sv["mix"], dx_mid, tm=D, tn=512, tk=512, out_shape=SDS((D, D), bf16),
                     out_specs=_plain(D, 512), epilogue=_store_epi(bf16))
    dz, dxact, ddt, dsm_ssd = _ssd_bwd(sv["xact"], sv["z"], sv["dt"], dmix, sv["hs"], p["dt_bias"], p["a_log"],
                                       p["d_skip"], p["ssm_norm_g"], l)
    dxbc, dconv_w, dconv_b = _conv_bwd(sv["xbc"], dxact, wts["conv_w"], p["conv_b"], l)
    dqkv, dbias, dsm_attn = _attn_bwd(sv["qkv"], dmix, p["q_gain"], p["k_gain"], p["sinks"], bias, l)
    dproj = _pack_dproj(dqkv, dz, dxbc, ddt)
    dw_in = _matmul("dw_in", "tn", sv["h1"], dproj, tm=D, tn=640, tk=S, out_shape=SDS((D, D_IN_PAD), bf16),
                    out_specs=_plain(D, 640), epilogue=_store_epi(bf16))
    deps = send(l, dict(w_out=dw_out, w_in=_w_in_slabs(dw_in)))
    dx, dg_mix = _matmul(
        "in_proj_dh", "nt", dproj, wts["w_in"], tm=256, tn=D, tk=D_IN_PAD, out_shape=[SDS((S, D), f32), SDS((1, D), f32)],
        out_specs=[_plain(256, D), grow], epilogue=_rms_bwd_epilogue(l),
        extras=(sv["x"], p["mix_norm_g"], dx_mid), extra_specs=(_plain(256, D), gfull, _plain(256, D)), deps=deps)
    small = dict(mix_norm_g=dg_mix, mlp_norm_g=dg_mlp, conv_w=dconv_w, conv_b=dconv_b, ssd=dsm_ssd, attn=dsm_attn, dbias=dbias)
    return dx, small, deps


def _local_step(x, tgt, p, get_weights, send, deps):
    onehot_t = jnp.asarray(_onehot_buckets())
    bias = _bias_build(p["rel_bias"].T, onehot_t).reshape(NQ, BLK, 2 * BLK)
    saved = []
    h = x
    for l in range(DEPTH):
        h, sv = _layer_fwd(l, h, p, get_weights, bias, deps if l == 0 else ())
        saved.append(sv)
    dx, loss = _loss_kernel(h, tgt)
    smalls = [None] * DEPTH
    deps = ()
    for l in reversed(range(DEPTH)):
        dx, smalls[l], deps = _layer_bwd(l, dx, saved[l], p, bias, deps, send)
    drel_t = _bias_grad(smalls[0]["dbias"].reshape(NQ, -1), smalls[1]["dbias"].reshape(NQ, -1), onehot_t)
    return dx, _pack_small_grads(smalls, drel_t, loss), deps


WEIGHT_ORDER = ("mix_norm_g", "w_in", "q_gain", "k_gain", "sinks", "rel_bias", "conv_w", "conv_b", "dt_bias", "a_log", "d_skip",
                "ssm_norm_g", "w_out", "mlp_norm_g", "w_up", "w_down")


def kernel(x, mix_norm_g, w_in, q_gain, k_gain, sinks, rel_bias, conv_w, conv_b, dt_bias, a_log, d_skip, ssm_norm_g, w_out, mlp_norm_g, w_up, w_down, loss_target, m_mix_norm_g, m_w_in, m_q_gain, m_k_gain, m_sinks, m_rel_bias, m_conv_w, m_conv_b, m_dt_bias, m_a_log, m_d_skip, m_ssm_norm_g, m_w_out, m_mlp_norm_g, m_w_up, m_w_down, v_mix_norm_g, v_w_in, v_q_gain, v_k_gain, v_sinks, v_rel_bias, v_conv_w, v_conv_b, v_dt_bias, v_a_log, v_d_skip, v_ssm_norm_g, v_w_out, v_mlp_norm_g, v_w_up, v_w_down):
    w = dict(mix_norm_g=mix_norm_g, w_in=w_in, q_gain=q_gain, k_gain=k_gain, sinks=sinks, rel_bias=rel_bias, conv_w=conv_w,
             conv_b=conv_b, dt_bias=dt_bias, a_log=a_log, d_skip=d_skip, ssm_norm_g=ssm_norm_g, w_out=w_out,
             mlp_norm_g=mlp_norm_g, w_up=w_up, w_down=w_down)
    m = dict(mix_norm_g=m_mix_norm_g, w_in=m_w_in, q_gain=m_q_gain, k_gain=m_k_gain, sinks=m_sinks, rel_bias=m_rel_bias,
             conv_w=m_conv_w, conv_b=m_conv_b, dt_bias=m_dt_bias, a_log=m_a_log, d_skip=m_d_skip, ssm_norm_g=m_ssm_norm_g,
             w_out=m_w_out, mlp_norm_g=m_mlp_norm_g, w_up=m_w_up, w_down=m_w_down)
    v = dict(mix_norm_g=v_mix_norm_g, w_in=v_w_in, q_gain=v_q_gain, k_gain=v_k_gain, sinks=v_sinks, rel_bias=v_rel_bias,
             conv_w=v_conv_w, conv_b=v_conv_b, dt_bias=v_dt_bias, a_log=v_a_log, d_skip=v_d_skip, ssm_norm_g=v_ssm_norm_g,
             w_out=v_w_out, mlp_norm_g=v_mlp_norm_g, w_up=v_w_up, w_down=v_w_down)
    big = ("w_in", "w_out", "w_up", "w_down")

    my_idx = _dev_index(*_my_place()).astype(jnp.int32).reshape(1)

    fulls = {n: _cast_to_full("cast_" + n, w[n], KIND[n], FULL_SHAPE[n], my_idx, bf16) for n in big}
    conv_full = _cast_to_full("cast_conv_w", conv_w.reshape(1, DEPTH * 4, 128), "stack", (N_DEV, DEPTH * 4, 128), my_idx, f32)[0]
    rest = ["w_out", "w_up", "w_down"]
    g0 = _gather_start("gather0", ["w_in", "conv_w"], [fulls["w_in"][0], conv_full], ())
    g1 = _gather_start("gather1", rest, [fulls[n][0] for n in rest], (g0["token"],))
    g2 = _gather_start("gather2", list(big), [fulls[n][1] for n in big], (g1["token"],))
    held = {}

    def get_weights(l, part, after):
        if l == 0 and part == "in":
            full_in, full_conv = _gather_finish("gather0", ["w_in", "conv_w"], g0, list(after) + [g2["token"]])
            held["conv_w"] = jnp.transpose(full_conv.reshape(N_DEV, DEPTH, 4, 128), (1, 2, 0, 3)).reshape(DEPTH, 4, D_CONV)
            return dict(w_in=_w_in_assemble(full_in), conv_w=held["conv_w"])
        if l == 0:
            full = _gather_finish("gather1", rest, g1, after)
            return {n: f[None] for n, f in zip(rest, full)}
        if part == "in":
            held["layer1"] = _gather_finish("gather2", list(big), g2, after)
            return dict(w_in=_w_in_assemble(held["layer1"][0]), conv_w=held["conv_w"])
        return {n: f[None] for n, f in zip(rest, held["layer1"][1:])}

    pending = []

    def send(l, grads):
        names = list(grads)
        started = _exchange_start("exchange%d_%s" % (l, names[0]), names, [grads[n] for n in names], ())
        pending.append((l, names, started))
        return (started["token"],)

    dx, small_part, _ = _local_step(x.reshape(S, D), loss_target.reshape(S, D), w, get_weights, send, (g2["token"],))

    small = _small_exchange_start(small_part, ())
    tiles = dict(w_in=256, w_out=128, w_up=256, w_down=256)
    flat = lambda a: a.reshape(a.shape[0] * a.shape[1], a.shape[2])
    outs_of = {n: None for n in big}
    after = [dx, small["token"]]
    for l, names, started in pending:
        bufs = _split_wait("exchange%d_%s_wait" % (l, names[0]), started, after)
        for t, n in enumerate(names):
            outs_of[n] = _adamw_layer("adamw_%s%d" % (n, l), KIND[n], l, flat(w[n]), flat(m[n]), flat(v[n]),
                                      bufs[len(names) + t], bufs[t], my_idx, outs_of[n], tiles[n])
        after = [outs_of[names[-1]][0]]
    res = {n: [o.reshape(w[n].shape) for o in outs_of[n]] for n in big}
    small_part, small_land = _split_wait("small_exchange_wait", small, after)
    small_outs = _adamw_small(small_part, small_land, w, m, v)
    loss = small_outs[0][0, 0]
    for k, name in enumerate(SMALL_NAMES):
        res[name] = small_outs[1 + 4 * k:5 + 4 * k]

    result = [loss, dx.reshape(1, S, D)]
    for k in range(4):
        result += [res[name][k] for name in WEIGHT_ORDER]
    return tuple(result)
```

```python
import functools
import math

import numpy as np
import jax
import jax.numpy as jnp
from jax import lax
from jax.experimental import pallas as pl
from jax.experimental.pallas import tpu as pltpu

f32 = jnp.float32
bf16 = jnp.bfloat16
SDS = jax.ShapeDtypeStruct
MESH = pl.DeviceIdType.MESH
HIGHEST = lax.Precision.HIGHEST

S = 2048
D = 1024
DEPTH = 2
BLK = 128
NBLK = S // BLK
HD = 64
NQ = 8
NKV = 2
NSSM = 8
NGRP = 2
NSTATE = 128
D_ATTN = 512
D_SSM = 512
D_CONV = 1024
D_FF = 4096
D_IN = 2312
D_IN_PAD = 2560
N_BUCKETS = 32
EPS = 1e-6
N_DEV = 8
VMEM_LIMIT = 48 * 1024 * 1024

ADAM_LR = 0.001
ADAM_B1 = 0.9
ADAM_B2 = 0.999
ADAM_EPS = 1e-08
ADAM_WD = 0.01
ADAM_STEP = 10

NT_DIMS = (((1,), (1,)), ((), ()))
TN_DIMS = (((0,), (0,)), ((), ()))
NN_DIMS = (((1,), (0,)), ((), ()))

ROW_MIXG = 0
ROW_MLPG = 2
ROW_CONVB = 4
ROW_SSMG = 6
ROW_MISC = 8
ROW_RELB = 10
ROW_CONVW = 18
ROW_LOSS = 26
SMALL_ROWS = 32
LANE_QG, LANE_KG, LANE_SINK, LANE_DTB, LANE_ALOG, LANE_DSKIP = 0, 64, 128, 256, 384, 512


def _dot(a, b, dims):
    return lax.dot_general(a, b, dims, preferred_element_type=f32)


def _cparams(n_axes):
    return pltpu.CompilerParams(dimension_semantics=("arbitrary",) * n_axes, vmem_limit_bytes=VMEM_LIMIT)


def _sum11(v):
    return jnp.sum(jnp.sum(v, axis=1, keepdims=True), axis=0, keepdims=True)


def _sigmoid(v):
    return 1.0 / (1.0 + jnp.exp(-v))


ANY_SPEC = pl.BlockSpec(memory_space=pl.ANY)


def _matmul(name, mode, a, b, *, layer=0, tm, tn, tk, out_shape, out_specs, epilogue, extras=(), extra_specs=(), deps=()):
    extras = tuple(extras) + tuple(deps)
    extra_specs = tuple(extra_specs) + (ANY_SPEC,) * len(deps)
    if mode == "tn":
        t_dim, m_dim = a.shape
        n_dim = b.shape[1]
        grid = (m_dim // tm, n_dim // tn, t_dim // tk)
        a_spec = pl.BlockSpec((tk, tm), lambda i, j, k: (k, i))
        b_spec = pl.BlockSpec((tk, tn), lambda i, j, k: (k, j))
        dims = TN_DIMS
    elif mode == "nn":
        m_dim, k_dim = a.shape
        n_dim = b.shape[-1]
        grid = (m_dim // tm, n_dim // tn, k_dim // tk)
        a_spec = pl.BlockSpec((tm, tk), lambda i, j, k: (i, k))
        b_spec = pl.BlockSpec((None, tk, tn), lambda i, j, k: (layer, k, j))
        dims = NN_DIMS
    else:
        m_dim, k_dim = a.shape
        n_dim = b.shape[-2]
        grid = (m_dim // tm, n_dim // tn, k_dim // tk)
        a_spec = pl.BlockSpec((tm, tk), lambda i, j, k: (i, k))
        b_spec = pl.BlockSpec((None, tn, tk), lambda i, j, k: (layer, j, k))
        dims = NT_DIMS
    nk = grid[2]
    n_ex = len(extras)

    def body(a_ref, b_ref, *rest):
        ex = rest[:n_ex - len(deps)]
        outs = rest[n_ex:-1]
        acc = rest[-1]
        i = pl.program_id(0)
        j = pl.program_id(1)
        k = pl.program_id(2)
        part = _dot(a_ref[...].astype(bf16), b_ref[...].astype(bf16), dims)
        if nk == 1:
            epilogue(part, i, j, ex, outs)
        else:
            @pl.when(k == 0)
            def _():
                acc[...] = part

            @pl.when(k > 0)
            def _():
                acc[...] += part

            @pl.when(k == nk - 1)
            def _():
                epilogue(acc[...], i, j, ex, outs)

    return pl.pallas_call(
        body, grid=grid, in_specs=[a_spec, b_spec, *extra_specs], out_specs=out_specs, out_shape=out_shape,
        scratch_shapes=[pltpu.VMEM((tm, tn) if nk > 1 else (8, 128), f32)], name=name, compiler_params=_cparams(3),
    )(a, b, *extras)


def _rms_bwd_epilogue(layer):
    def epi(acc, i, j, ex, outs):
        x_ref, g_ref, dres_ref = ex
        dx_ref, dg_ref = outs
        xv = x_ref[...]
        r = lax.rsqrt(jnp.mean(xv * xv, axis=-1, keepdims=True) + EPS)
        xhat = xv * r
        w = acc * g_ref[layer:layer + 1, :]
        dx_ref[...] = dres_ref[...] + r * (w - xhat * jnp.mean(xhat * w, axis=-1, keepdims=True))
        dg = jnp.sum(acc * xhat, axis=0, keepdims=True)

        @pl.when(i == 0)
        def _():
            dg_ref[...] = dg

        @pl.when(i > 0)
        def _():
            dg_ref[...] += dg
    return epi


def _rms_fwd(name, x, g, layer, deps=()):
    tr = 512

    def body(x_ref, g_ref, *rest):
        h_ref = rest[-1]
        xv = x_ref[...]
        r = lax.rsqrt(jnp.mean(xv * xv, axis=-1, keepdims=True) + EPS)
        h_ref[...] = (xv * r * g_ref[layer:layer + 1, :]).astype(bf16)

    return pl.pallas_call(
        body, grid=(S // tr,),
        in_specs=[pl.BlockSpec((tr, D), lambda i: (i, 0)), pl.BlockSpec((DEPTH, D), lambda i: (0, 0))] + [ANY_SPEC] * len(deps),
        out_specs=pl.BlockSpec((tr, D), lambda i: (i, 0)), out_shape=SDS((S, D), bf16), name=name,
        compiler_params=_cparams(1),
    )(x, g, *deps)


def _loss_kernel(y, tgt):
    tr = 512

    def body(y_ref, t_ref, dy_ref, loss_ref):
        err = y_ref[...] - t_ref[...]
        dy_ref[...] = err * (1.0 / D)
        part = 0.5 * jnp.sum(jnp.mean(err * err, axis=-1, keepdims=True), axis=0, keepdims=True)

        @pl.when(pl.program_id(0) == 0)
        def _():
            loss_ref[...] = jnp.zeros_like(loss_ref)

        loss_ref[...] += jnp.broadcast_to(part, loss_ref.shape)

    return pl.pallas_call(
        body, grid=(S // tr,),
        in_specs=[pl.BlockSpec((tr, D), lambda i: (i, 0)), pl.BlockSpec((tr, D), lambda i: (i, 0))],
        out_specs=[pl.BlockSpec((tr, D), lambda i: (i, 0)), pl.BlockSpec((1, 128), lambda i: (0, 0))],
        out_shape=[SDS((S, D), f32), SDS((1, 128), f32)], name="loss", compiler_params=_cparams(1),
    )(y, tgt)


def _pack_dproj(dqkv, dz, dxbc, ddt):
    tr = 256

    def body(a_ref, b_ref, c_ref, d_ref, o_ref):
        o_ref[:, 0:768] = a_ref[...].astype(bf16)
        o_ref[:, 768:1280] = b_ref[...].astype(bf16)
        o_ref[:, 1280:2304] = c_ref[...].astype(bf16)
        o_ref[:, 2304:2432] = d_ref[...].astype(bf16)
        o_ref[:, 2432:D_IN_PAD] = jnp.zeros((tr, D_IN_PAD - 2432), bf16)

    return pl.pallas_call(
        body, grid=(S // tr,),
        in_specs=[pl.BlockSpec((tr, 768), lambda i: (i, 0)), pl.BlockSpec((tr, 512), lambda i: (i, 0)),
                  pl.BlockSpec((tr, 1024), lambda i: (i, 0)), pl.BlockSpec((tr, 128), lambda i: (i, 0))],
        out_specs=pl.BlockSpec((tr, D_IN_PAD), lambda i: (i, 0)), out_shape=SDS((S, D_IN_PAD), bf16),
        name="pack_dproj", compiler_params=_cparams(1),
    )(dqkv, dz, dxbc, ddt)


def _own_slab_spec(kind, tr, cols, nblk):
    if kind == "stack":
        return pl.BlockSpec((None, tr, cols), lambda i, idx: (idx[0], i, 0))
    if kind == "cols512":
        return pl.BlockSpec((tr, cols), lambda i, idx: (i, idx[0]))
    return pl.BlockSpec((tr, cols), lambda i, idx: (idx[0] * nblk + i, 0))


def _cast_to_full(name, w, kind, full_shape, my_idx, dtype):
    n_layers, rows, cols = w.shape
    tr = min(rows, 256)
    nblk = rows // tr

    def body(idx_ref, w_ref, *o_refs):
        for l in range(n_layers):
            o_refs[l][...] = w_ref[l].astype(dtype)

    grid_spec = pltpu.PrefetchScalarGridSpec(
        num_scalar_prefetch=1, grid=(nblk,), in_specs=[pl.BlockSpec((n_layers, tr, cols), lambda i, idx: (0, i, 0))],
        out_specs=[_own_slab_spec(kind, tr, cols, nblk)] * n_layers)
    return pl.pallas_call(body, grid_spec=grid_spec, out_shape=[SDS(full_shape, dtype)] * n_layers, name=name,
                          compiler_params=_cparams(1))(my_idx, w)


def _adamw_math(w, m, v, g):
    m_new = ADAM_B1 * m + (1.0 - ADAM_B1) * g
    v_new = ADAM_B2 * v + (1.0 - ADAM_B2) * (g * g)
    m_hat = m_new / (1.0 - ADAM_B1 ** ADAM_STEP)
    v_hat = v_new / (1.0 - ADAM_B2 ** ADAM_STEP)
    delta = -ADAM_LR * (m_hat / (jnp.sqrt(v_hat) + ADAM_EPS) + ADAM_WD * w)
    return delta, m_new, v_new


def _adamw_layer(name, kind, layer, w, m, v, land, g_full, my_idx, prev, tr):
    rows2, cols = w.shape
    rows = rows2 // DEPTH
    nblk = rows // tr
    own_spec = _own_slab_spec(kind, tr, cols, nblk)
    n_prev = 0 if prev is None else 4

    def body(idx_ref, w_ref, m_ref, v_ref, land_ref, own_ref, *rest):
        g_ref, d_ref, mo_ref, vo_ref = rest[n_prev:]
        me = idx_ref[0]
        g = None
        for p in range(N_DEV):
            part = jnp.where(me == p, own_ref[...], land_ref[p]).astype(f32)
            g = part if g is None else g + part
        delta, m_new, v_new = _adamw_math(w_ref[...], m_ref[...], v_ref[...], g)
        g_ref[...] = g
        d_ref[...] = delta
        mo_ref[...] = m_new
        vo_ref[...] = v_new

    blk = pl.BlockSpec((tr, cols), lambda i, idx: (layer * nblk + i, 0))
    grid_spec = pltpu.PrefetchScalarGridSpec(
        num_scalar_prefetch=1, grid=(nblk,),
        in_specs=[blk, blk, blk, pl.BlockSpec((N_DEV, tr, cols), lambda i, idx: (0, i, 0)), own_spec] + [ANY_SPEC] * n_prev,
        out_specs=[blk, blk, blk, blk])
    aliases = {} if prev is None else {6 + k: k for k in range(4)}
    return pl.pallas_call(
        body, grid_spec=grid_spec, out_shape=[SDS((rows2, cols), f32)] * 4, name=name, input_output_aliases=aliases,
        compiler_params=_cparams(1),
    )(my_idx, w, m, v, land, g_full, *([] if prev is None else prev))


def _bucket_table():
    qi = np.arange(BLK)[:, None]
    kj = np.arange(2 * BLK)[None, :]
    dist = qi + BLK - kj
    dcl = np.clip(dist, 0, None)
    max_exact = N_BUCKETS // 2
    d_f = np.maximum(dcl, 1).astype(np.float32)
    large = max_exact + (np.log(d_f / np.float32(max_exact)) / np.float32(math.log(128 / max_exact))
                         * np.float32(N_BUCKETS - max_exact)).astype(np.int32)
    large = np.minimum(large, N_BUCKETS - 1)
    bucket = np.where(dcl < max_exact, dcl, large)
    in_window = (dist >= 0) & (dist < BLK)
    return bucket.astype(np.int32), in_window


def _onehot_buckets():
    bucket, _ = _bucket_table()
    oh = (bucket.reshape(-1)[None, :] == np.arange(N_BUCKETS)[:, None]).astype(np.float32)
    return oh


def _bias_build(rel_bias_t, onehot_t):
    def body(r_ref, o_ref, out_ref):
        out_ref[...] = jnp.dot(r_ref[...], o_ref[...], preferred_element_type=f32, precision=HIGHEST)

    tn = 4096
    return pl.pallas_call(
        body, grid=(BLK * 2 * BLK // tn,),
        in_specs=[pl.BlockSpec((NQ, N_BUCKETS), lambda i: (0, 0)), pl.BlockSpec((N_BUCKETS, tn), lambda i: (0, i))],
        out_specs=pl.BlockSpec((NQ, tn), lambda i: (0, i)), out_shape=SDS((NQ, BLK * 2 * BLK), f32), name="bias_build",
        compiler_params=_cparams(1),
    )(rel_bias_t, onehot_t)


def _bias_grad(dbias0, dbias1, onehot_t):
    tn = 4096
    nsteps = BLK * 2 * BLK // tn

    def body(a_ref, b_ref, o_ref, out_ref):
        part = lax.dot_general(a_ref[...] + b_ref[...], o_ref[...], NT_DIMS, preferred_element_type=f32, precision=HIGHEST)

        @pl.when(pl.program_id(0) == 0)
        def _():
            out_ref[...] = part

        @pl.when(pl.program_id(0) > 0)
        def _():
            out_ref[...] += part

    return pl.pallas_call(
        body, grid=(nsteps,),
        in_specs=[pl.BlockSpec((NQ, tn), lambda i: (0, i)), pl.BlockSpec((NQ, tn), lambda i: (0, i)),
                  pl.BlockSpec((N_BUCKETS, tn), lambda i: (0, i))],
        out_specs=pl.BlockSpec((NQ, N_BUCKETS), lambda i: (0, 0)), out_shape=SDS((NQ, N_BUCKETS), f32), name="bias_grad",
        compiler_params=_cparams(1),
    )(dbias0, dbias1, onehot_t)


def _attn_mask(n):
    qi = lax.broadcasted_iota(jnp.int32, (BLK, 2 * BLK), 0)
    kj = lax.broadcasted_iota(jnp.int32, (BLK, 2 * BLK), 1)
    dist = qi + BLK - kj
    first_key = jnp.where(n > 0, 0, BLK)
    return (dist >= 0) & (dist < BLK) & (kj >= first_key)


def _head_norm(t, gain):
    r = lax.rsqrt(jnp.mean(t * t, axis=-1, keepdims=True) + EPS)
    that = t * r
    return that, r, that * gain


def _softmax_with_sink(s, sink):
    m = jnp.maximum(jnp.max(s, axis=-1, keepdims=True), sink)
    p = jnp.exp(s - m)
    psink = jnp.exp(sink - m)
    inv = 1.0 / (jnp.sum(p, axis=-1, keepdims=True) + psink)
    return p * inv, psink * inv


GQ = NQ // NKV


def _group_rows(x_ref, sk_ref, layer, j):
    heads = [GQ * j + g for g in range(GQ)]
    xs = jnp.concatenate([x_ref[:, pl.ds(HD * h, HD)] for h in heads], axis=0)
    sink = jnp.concatenate([jnp.broadcast_to(sk_ref[layer:layer + 1, h:h + 1], (BLK, 1)) for h in heads], axis=0)
    return xs, sink


def _attn_fwd(qkv, q_gain, k_gain, sinks, bias, layer):
    def body(q_ref, kc_ref, kp_ref, vc_ref, vp_ref, qg_ref, kg_ref, sk_ref, bias_ref, o_ref):
        n = pl.program_id(0)
        mask = jnp.tile(_attn_mask(n), (GQ, 1))
        qg = qg_ref[layer:layer + 1, :]
        kg = kg_ref[layer:layer + 1, :]
        grp = range(NKV)
        kbs = [jnp.concatenate([kp_ref[:, pl.ds(HD * j, HD)], kc_ref[:, pl.ds(HD * j, HD)]], axis=0) for j in grp]
        vbs = [jnp.concatenate([vp_ref[:, pl.ds(HD * j, HD)], vc_ref[:, pl.ds(HD * j, HD)]], axis=0).astype(bf16) for j in grp]
        kn_b = [_head_norm(kbs[j], kg)[2].astype(bf16) for j in grp]
        rows = [_group_rows(q_ref, sk_ref, layer, j) for j in grp]
        qn_b = [_head_norm(rows[j][0], qg)[2].astype(bf16) for j in grp]
        ss = [_dot(qn_b[j], kn_b[j], NT_DIMS) * (HD ** -0.5) + bias_ref[GQ * j:GQ * (j + 1)].reshape(GQ * BLK, 2 * BLK) for j in grp]
        ps = [_softmax_with_sink(jnp.where(mask, ss[j], -jnp.inf), rows[j][1])[0] for j in grp]
        outs = [_dot(ps[j].astype(bf16), vbs[j], NN_DIMS).astype(bf16) for j in grp]
        for j in grp:
            for g in range(GQ):
                o_ref[:, pl.ds(HD * (GQ * j + g), HD)] = outs[j][BLK * g:BLK * (g + 1), :]

    prev = lambda n: jnp.maximum(n - 1, 0)
    small = lambda shape: pl.BlockSpec(shape, lambda n: (0,) * len(shape))
    return pl.pallas_call(
        body, grid=(NBLK,),
        in_specs=[pl.BlockSpec((BLK, D_ATTN), lambda n: (n, 0)),
                  pl.BlockSpec((BLK, 128), lambda n: (n, 4)), pl.BlockSpec((BLK, 128), lambda n: (prev(n), 4)),
                  pl.BlockSpec((BLK, 128), lambda n: (n, 5)), pl.BlockSpec((BLK, 128), lambda n: (prev(n), 5)),
                  small((DEPTH, HD)), small((DEPTH, HD)), small((DEPTH, NQ)), small((NQ, BLK, 2 * BLK))],
        out_specs=pl.BlockSpec((BLK, D_ATTN), lambda n: (n, 0)), out_shape=SDS((S, D_ATTN), bf16),
        name="attn_fwd", compiler_params=_cparams(1),
    )(qkv, qkv, qkv, qkv, qkv, q_gain, k_gain, sinks, bias)


def _attn_bwd(qkv, dmix, q_gain, k_gain, sinks, bias, layer):
    def body(q_ref, kc_ref, kp_ref, vc_ref, vp_ref, do_ref, qg_ref, kg_ref, sk_ref, bias_ref,
             dqkv_ref, dbias_ref, dsm_ref, carry):
        i = pl.program_id(0)
        n = NBLK - 1 - i
        mask = jnp.tile(_attn_mask(n), (GQ, 1))
        qg = qg_ref[layer:layer + 1, :]
        kg = kg_ref[layer:layer + 1, :]
        lane = lax.broadcasted_iota(jnp.int32, (1, 128), 1)

        @pl.when(i == 0)
        def _():
            carry[...] = jnp.zeros_like(carry)
            dbias_ref[...] = jnp.zeros_like(dbias_ref)
            dsm_ref[...] = jnp.zeros_like(dsm_ref)

        grp = range(NKV)
        kbs = [jnp.concatenate([kp_ref[:, pl.ds(HD * j, HD)], kc_ref[:, pl.ds(HD * j, HD)]], axis=0) for j in grp]
        vbs = [jnp.concatenate([vp_ref[:, pl.ds(HD * j, HD)], vc_ref[:, pl.ds(HD * j, HD)]], axis=0).astype(bf16) for j in grp]
        knorm = [_head_norm(kbs[j], kg) for j in grp]
        kn_b = [knorm[j][2].astype(bf16) for j in grp]
        rows = [_group_rows(q_ref, sk_ref, layer, j) for j in grp]
        qnorm = [_head_norm(rows[j][0], qg) for j in grp]
        qn_b = [qnorm[j][2].astype(bf16) for j in grp]
        ss = [_dot(qn_b[j], kn_b[j], NT_DIMS) * (HD ** -0.5) + bias_ref[GQ * j:GQ * (j + 1)].reshape(GQ * BLK, 2 * BLK) for j in grp]
        sm = [_softmax_with_sink(jnp.where(mask, ss[j], -jnp.inf), rows[j][1]) for j in grp]
        do_b = [jnp.concatenate([do_ref[:, pl.ds(HD * (GQ * j + g), HD)] for g in range(GQ)], axis=0).astype(bf16) for j in grp]
        dps = [_dot(do_b[j], vbs[j], NT_DIMS) for j in grp]
        deltas = [jnp.sum(sm[j][0] * dps[j], axis=-1, keepdims=True) for j in grp]
        dss = [sm[j][0] * (dps[j] - deltas[j]) for j in grp]
        ds_b = [(dss[j] * (HD ** -0.5)).astype(bf16) for j in grp]
        dqn = [_dot(ds_b[j], kn_b[j], NN_DIMS) for j in grp]
        dkn = [_dot(ds_b[j], qn_b[j], TN_DIMS) for j in grp]
        dvs = [_dot(sm[j][0].astype(bf16), do_b[j], TN_DIMS) for j in grp]
        dqg = jnp.zeros((1, HD), f32)
        dkg = jnp.zeros((1, HD), f32)
        dsink = jnp.zeros((1, 128), f32)
        for j in grp:
            dbias_ref[GQ * j:GQ * (j + 1)] += dss[j].reshape(GQ, BLK, 2 * BLK)
            dsk = sm[j][1] * deltas[j]
            for g in range(GQ):
                dsink = dsink + jnp.where(lane == GQ * j + g, -_sum11(dsk[BLK * g:BLK * (g + 1), :]), 0.0)
            qhat, rq, _ = qnorm[j]
            w = dqn[j] * qg
            dq = rq * (w - qhat * jnp.mean(qhat * w, axis=-1, keepdims=True))
            for g in range(GQ):
                dqkv_ref[:, pl.ds(HD * (GQ * j + g), HD)] = dq[BLK * g:BLK * (g + 1), :]
            dqg = dqg + jnp.sum(dqn[j] * qhat, axis=0, keepdims=True)
            khat, rk, _ = knorm[j]
            w = dkn[j] * kg
            dk = rk * (w - khat * jnp.mean(khat * w, axis=-1, keepdims=True))
            dkg = dkg + jnp.sum(dkn[j] * khat, axis=0, keepdims=True)
            dqkv_ref[:, pl.ds(D_ATTN + HD * j, HD)] = dk[BLK:, :] + carry[:, pl.ds(HD * j, HD)]
            dqkv_ref[:, pl.ds(D_ATTN + 128 + HD * j, HD)] = dvs[j][BLK:, :] + carry[:, pl.ds(128 + HD * j, HD)]
            carry[:, pl.ds(HD * j, HD)] = dk[:BLK, :]
            carry[:, pl.ds(128 + HD * j, HD)] = dvs[j][:BLK, :]
        dsm_ref[0:1, 0:HD] += dqg
        dsm_ref[1:2, 0:HD] += dkg
        dsm_ref[2:3, :] += dsink

    rev = lambda i: NBLK - 1 - i
    prev = lambda i: jnp.maximum(NBLK - 2 - i, 0)
    small = lambda shape: pl.BlockSpec(shape, lambda i: (0,) * len(shape))
    return pl.pallas_call(
        body, grid=(NBLK,),
        in_specs=[pl.BlockSpec((BLK, D_ATTN), lambda i: (rev(i), 0)),
                  pl.BlockSpec((BLK, 128), lambda i: (rev(i), 4)), pl.BlockSpec((BLK, 128), lambda i: (prev(i), 4)),
                  pl.BlockSpec((BLK, 128), lambda i: (rev(i), 5)), pl.BlockSpec((BLK, 128), lambda i: (prev(i), 5)),
                  pl.BlockSpec((BLK, D_ATTN), lambda i: (rev(i), 0)),
                  small((DEPTH, HD)), small((DEPTH, HD)), small((DEPTH, NQ)), small((NQ, BLK, 2 * BLK))],
        out_specs=[pl.BlockSpec((BLK, 768), lambda i: (rev(i), 0)), small((NQ, BLK, 2 * BLK)), small((8, 128))],
        out_shape=[SDS((S, 768), f32), SDS((NQ, BLK, 2 * BLK), f32), SDS((8, 128), f32)],
        scratch_shapes=[pltpu.VMEM((BLK, 256), f32)], name="attn_bwd", compiler_params=_cparams(1),
    )(qkv, qkv, qkv, qkv, qkv, dmix, q_gain, k_gain, sinks, bias)


CONV_TC = 128


def _shift_down(u, s):
    if s == 0:
        return u
    rows = lax.broadcasted_iota(jnp.int32, u.shape, 0)
    return jnp.where(rows >= s, pltpu.roll(u, s, 0), 0.0)


def _shift_up(u, s):
    if s == 0:
        return u
    rows = lax.broadcasted_iota(jnp.int32, u.shape, 0)
    return jnp.where(rows < u.shape[0] - s, pltpu.roll(u, u.shape[0] - s, 0), 0.0)


def _conv_specs():
    return [pl.BlockSpec((S, CONV_TC), lambda c: (0, c)),
            pl.BlockSpec((None, 4, CONV_TC), lambda c: (0, 0, c)),
            pl.BlockSpec((DEPTH, CONV_TC), lambda c: (0, c))]


def _conv_pre(u, w_ref, b_ref, layer):
    pre = b_ref[layer:layer + 1, :] + w_ref[3:4, :] * u
    for k in range(3):
        pre = pre + w_ref[k:k + 1, :] * _shift_down(u, 3 - k)
    return pre


def _conv_fwd(xbc, conv_w, conv_b, layer):
    def body(u_ref, w_ref, b_ref, o_ref):
        pre = _conv_pre(u_ref[...], w_ref, b_ref, layer)
        o_ref[...] = pre * _sigmoid(pre)

    specs = _conv_specs()
    specs[1] = pl.BlockSpec((None, 4, CONV_TC), lambda c: (layer, 0, c))
    return pl.pallas_call(
        body, grid=(D_CONV // CONV_TC,), in_specs=specs, out_specs=pl.BlockSpec((S, CONV_TC), lambda c: (0, c)),
        out_shape=SDS((S, D_CONV), f32), name="conv_fwd", compiler_params=_cparams(1),
    )(xbc, conv_w, conv_b)


def _conv_bwd(xbc, dact, conv_w, conv_b, layer):
    def body(u_ref, w_ref, b_ref, da_ref, du_ref, dw_ref, db_ref):
        u = u_ref[...]
        pre = _conv_pre(u, w_ref, b_ref, layer)
        sg = _sigmoid(pre)
        dpre = da_ref[...] * (sg * (1.0 + pre * (1.0 - sg)))
        du = w_ref[3:4, :] * dpre
        for k in range(3):
            du = du + w_ref[k:k + 1, :] * _shift_up(dpre, 3 - k)
        du_ref[...] = du
        db_ref[...] = jnp.broadcast_to(jnp.sum(dpre, axis=0, keepdims=True), db_ref.shape)
        dw_ref[...] = jnp.zeros_like(dw_ref)
        for k in range(4):
            dw_ref[k:k + 1, :] = jnp.sum(dpre * _shift_down(u, 3 - k), axis=0, keepdims=True)

    specs = _conv_specs()
    specs[1] = pl.BlockSpec((None, 4, CONV_TC), lambda c: (layer, 0, c))
    col = pl.BlockSpec((S, CONV_TC), lambda c: (0, c))
    row8 = pl.BlockSpec((8, CONV_TC), lambda c: (0, c))
    return pl.pallas_call(
        body, grid=(D_CONV // CONV_TC,), in_specs=[*specs, col], out_specs=[col, row8, row8],
        out_shape=[SDS((S, D_CONV), f32), SDS((8, D_CONV), f32), SDS((8, D_CONV), f32)], name="conv_bwd",
        compiler_params=_cparams(1),
    )(xbc, conv_w, conv_b, dact)


def _tri():
    return (lax.broadcasted_iota(jnp.int32, (BLK, BLK), 0) >= lax.broadcasted_iota(jnp.int32, (BLK, BLK), 1))


def _ssd_scalars(dt_ref, dtb_ref, alog_ref, layer):
    raw = dt_ref[:, 0:NSSM] + dtb_ref[layer:layer + 1, :]
    dtv = jnp.maximum(raw, 0.0) + jnp.log(1.0 + jnp.exp(-jnp.abs(raw)))
    a = -jnp.exp(alog_ref[layer:layer + 1, :])
    acs = jnp.dot(_tri().astype(f32), dtv * a, preferred_element_type=f32, precision=HIGHEST)
    return raw, dtv, a, acs


HG = NSSM // NGRP
GW = HG * HD


def _lane_expand(cols, g):
    lane_head = lax.broadcasted_iota(jnp.int32, (1, GW), 1) // HD
    out = cols[:, HG * g + HG - 1:HG * g + HG]
    for r in range(HG - 2, -1, -1):
        out = jnp.where(lane_head == r, cols[:, HG * g + r:HG * g + r + 1], out)
    return out


def _row_expand(vals, g):
    row_head = lax.broadcasted_iota(jnp.int32, (GW, 1), 0) // HD
    out = vals[:, HG * g + HG - 1:HG * g + HG]
    for r in range(HG - 2, -1, -1):
        out = jnp.where(row_head == r, vals[:, HG * g + r:HG * g + r + 1], out)
    return out


def _head_rowsums(a):
    sel = (lax.broadcasted_iota(jnp.int32, (GW, HG), 0) // HD == lax.broadcasted_iota(jnp.int32, (GW, HG), 1)).astype(bf16)
    hi = a.astype(bf16)
    lo = (a - hi.astype(f32)).astype(bf16)
    sums = _dot(hi, sel, NN_DIMS) + _dot(lo, sel, NN_DIMS)
    return [sums[:, r:r + 1] for r in range(HG)]


def _ssd_chunk_common(xc_ref, dt_ref, dtb_ref, alog_ref, h_rows, layer):
    raw, dtv, a, acs = _ssd_scalars(dt_ref, dtb_ref, alog_ref, layer)
    acs_t = acs.T
    last = acs[BLK - 1:BLK, :]
    c = dict(raw=raw, dtv=dtv, a=a, acs=acs, last=last, dte=jnp.exp(last - acs), e_all=jnp.exp(acs), cd=jnp.exp(last))
    grp, heads, tri = range(NGRP), range(NSSM), _tri()
    c["bm"] = [xc_ref[:, pl.ds(D_SSM + NSTATE * g, NSTATE)] for g in grp]
    c["bm_b"] = [c["bm"][g].astype(bf16) for g in grp]
    c["cm_b"] = [xc_ref[:, pl.ds(D_SSM + NGRP * NSTATE + NSTATE * g, NSTATE)].astype(bf16) for g in grp]
    c["cb"] = [_dot(c["cm_b"][g], c["bm_b"][g], NT_DIMS) for g in grp]
    c["x"] = [xc_ref[:, pl.ds(GW * g, GW)] for g in grp]
    c["dt"] = [_lane_expand(dtv, g) for g in grp]
    c["xdt"] = [c["x"][g] * c["dt"][g] for g in grp]
    c["xdt_b"] = [c["xdt"][g].astype(bf16) for g in grp]
    c["prev"] = [h_rows(g) for g in grp]
    c["prev_b"] = [c["prev"][g].astype(bf16) for g in grp]
    c["e"] = [_lane_expand(c["e_all"], g) for g in grp]
    c["y_off"] = [_dot(c["cm_b"][g], c["prev_b"][g], NT_DIMS) * c["e"][g] for g in grp]
    c["decay"] = [jnp.exp(jnp.where(tri, acs[:, h:h + 1] - acs_t[h:h + 1, :], -jnp.inf)) for h in heads]
    c["m"] = [c["cb"][h // HG] * c["decay"][h] for h in heads]
    c["m_b"] = [c["m"][h].astype(bf16) for h in heads]
    c["dte_x"] = [_lane_expand(c["dte"], g) for g in grp]
    c["xdte_b"] = [(c["xdt"][g] * c["dte_x"][g]).astype(bf16) for g in grp]
    return c


def _ssd_fwd(xact, z, dt, attn, dt_bias, a_log, d_skip, norm_g, layer):
    def body(xc_ref, z_ref, dt_ref, at_ref, dtb_ref, alog_ref, dsk_ref, ng_ref, mix_ref, hs_ref, y_ref, h_ref):
        n = pl.program_id(0)

        @pl.when(n == 0)
        def _():
            h_ref[...] = jnp.zeros_like(h_ref)

        hs_ref[...] = h_ref[...]
        c = _ssd_chunk_common(xc_ref, dt_ref, dtb_ref, alog_ref, lambda g: h_ref[pl.ds(GW * g, GW), :], layer)
        grp, heads = range(NGRP), range(NSSM)
        y_diag = [_dot(c["m_b"][h], c["xdt_b"][h // HG][:, HD * (h % HG):HD * (h % HG + 1)], NN_DIMS) for h in heads]
        new_st = [_dot(c["xdte_b"][g], c["bm_b"][g], TN_DIMS) for g in grp]
        for h in heads:
            y_ref[:, pl.ds(HD * h, HD)] = y_diag[h]
        dskip = dsk_ref[layer:layer + 1, :]
        for g in grp:
            cols = pl.ds(GW * g, GW)
            y_ref[:, cols] = y_ref[:, cols] + c["y_off"][g] + c["x"][g] * _lane_expand(dskip, g)
            h_ref[cols, :] = c["prev"][g] * _row_expand(c["cd"], g) + new_st[g]
        zv = z_ref[...]
        yz = y_ref[...] * (zv * _sigmoid(zv))
        mix_ref[:, 0:D_ATTN] = at_ref[...]
        for g in grp:
            yg = yz[:, GW * g:GW * (g + 1)]
            rs = lax.rsqrt(jnp.mean(yg * yg, axis=-1, keepdims=True) + EPS)
            mix_ref[:, D_ATTN + GW * g:D_ATTN + GW * (g + 1)] = (yg * rs * ng_ref[layer:layer + 1, GW * g:GW * (g + 1)]).astype(bf16)

    small = lambda shape: pl.BlockSpec(shape, lambda n: (0,) * len(shape))
    return pl.pallas_call(
        body, grid=(NBLK,),
        in_specs=[pl.BlockSpec((BLK, D_CONV), lambda n: (n, 0)), pl.BlockSpec((BLK, D_SSM), lambda n: (n, 0)),
                  pl.BlockSpec((BLK, 128), lambda n: (n, 0)), pl.BlockSpec((BLK, D_ATTN), lambda n: (n, 0)),
                  small((DEPTH, NSSM)), small((DEPTH, NSSM)), small((DEPTH, NSSM)), small((DEPTH, D_SSM))],
        out_specs=[pl.BlockSpec((BLK, D), lambda n: (n, 0)), pl.BlockSpec((None, NSSM * HD, NSTATE), lambda n: (n, 0, 0)),
                   pl.BlockSpec((BLK, D_SSM), lambda n: (n, 0))],
        out_shape=[SDS((S, D), bf16), SDS((NBLK, NSSM * HD, NSTATE), f32), SDS((S, D_SSM), f32)],
        scratch_shapes=[pltpu.VMEM((NSSM * HD, NSTATE), f32)],
        name="ssd_fwd", compiler_params=_cparams(1),
    )(xact, z, dt, attn, dt_bias, a_log, d_skip, norm_g)


def _ssd_bwd(xact, z, dt, dmix, hs, y, dt_bias, a_log, d_skip, norm_g, layer):
    def body(xc_ref, z_ref, dt_ref, do_ref, hs_ref, y_ref, dtb_ref, alog_ref, dsk_ref, ng_ref,
             dz_ref, dx_ref, ddt_ref, dsm_ref, dh_ref, dy_ref):
        i = pl.program_id(0)

        @pl.when(i == 0)
        def _():
            dh_ref[...] = jnp.zeros_like(dh_ref)
            dsm_ref[...] = jnp.zeros_like(dsm_ref)

        c = _ssd_chunk_common(xc_ref, dt_ref, dtb_ref, alog_ref, lambda g: hs_ref[pl.ds(GW * g, GW), :], layer)
        raw, dtv, a = c["raw"], c["dtv"], c["a"]
        grp, heads = range(NGRP), range(NSSM)
        dskip = dsk_ref[layer:layer + 1, :]
        lane8 = lax.broadcasted_iota(jnp.int32, (1, NSSM), 1)
        sub8 = lax.broadcasted_iota(jnp.int32, (NSSM, 1), 0)

        zv = z_ref[...]
        sz = _sigmoid(zv)
        gz = zv * sz
        yv = y_ref[...]
        yz = yv * gz
        for g in grp:
            sl = slice(GW * g, GW * (g + 1))
            yg = yz[:, sl]
            rs = lax.rsqrt(jnp.mean(yg * yg, axis=-1, keepdims=True) + EPS)
            yhat = yg * rs
            dog = do_ref[:, sl]
            w = dog * ng_ref[layer:layer + 1, sl]
            dyz = rs * (w - yhat * jnp.mean(yhat * w, axis=-1, keepdims=True))
            dsm_ref[0:1, sl] += jnp.sum(dog * yhat, axis=0, keepdims=True)
            dy_ref[:, sl] = dyz * gz[:, sl]
            dz_ref[:, sl] = dyz * yv[:, sl] * (sz[:, sl] * (1.0 + zv[:, sl] * (1.0 - sz[:, sl])))

        dy = [dy_ref[:, pl.ds(GW * g, GW)] for g in grp]
        dy_b = [dy[g].astype(bf16) for g in grp]
        hl = lambda h: slice(HD * (h % HG), HD * (h % HG + 1))
        dt_off_b = [(dy[g] * c["e"][g]).astype(bf16) for g in grp]
        dcm = [_dot(dt_off_b[g], c["prev_b"][g], NN_DIMS) for g in grp]
        dprev = [_dot(dt_off_b[g], c["cm_b"][g], TN_DIMS) for g in grp]
        yoff_rs = [_head_rowsums(dy[g] * c["y_off"][g]) for g in grp]
        dhn = [dh_ref[pl.ds(GW * g, GW), :] for g in grp]
        dhn_b = [dhn[g].astype(bf16) for g in grp]
        dprev = [dprev[g] + dhn[g] * _row_expand(c["cd"], g) for g in grp]
        dhn_prev = [dhn[g] * c["prev"][g] for g in grp]
        u = [_dot(c["bm_b"][g], dhn_b[g], NT_DIMS) for g in grp]
        dbm = [_dot(c["xdte_b"][g], dhn_b[g], NN_DIMS) for g in grp]
        ddte_rs = [_head_rowsums(c["xdt"][g] * u[g]) for g in grp]
        dm = [_dot(dy_b[h // HG][:, hl(h)], c["xdt_b"][h // HG][:, hl(h)], NT_DIMS) for h in heads]
        dxdt_in = [_dot(c["m_b"][h], dy_b[h // HG][:, hl(h)], TN_DIMS) for h in heads]
        dseg = [dm[h] * c["m"][h] for h in heads]
        dmd = [dm[h] * c["decay"][h] for h in heads]
        for h in heads:
            dx_ref[:, pl.ds(HD * h, HD)] = dxdt_in[h]

        dacs = jnp.zeros((BLK, NSSM), f32)
        dacs_cols = jnp.zeros((NSSM, BLK), f32)
        dlast = jnp.zeros((1, NSSM), f32)
        ddtv = jnp.zeros((BLK, NSSM), f32)
        ddsk = jnp.zeros((1, NSSM), f32)
        for g in grp:
            cols = pl.ds(GW * g, GW)
            dxdt = dx_ref[:, cols] + u[g] * c["dte_x"][g]
            dx_ref[:, cols] = dy[g] * _lane_expand(dskip, g) + dxdt * c["dt"][g]
            ddtv_rs = _head_rowsums(dxdt * c["x"][g])
            ddsk_rs = _head_rowsums(dy[g] * c["x"][g])
            dcb = dmd[HG * g]
            for r in range(1, HG):
                dcb = dcb + dmd[HG * g + r]
            dcb_b = dcb.astype(bf16)
            dx_ref[:, pl.ds(D_SSM + NSTATE * g, NSTATE)] = dbm[g] + _dot(dcb_b, c["cm_b"][g], TN_DIMS)
            dx_ref[:, pl.ds(D_SSM + NGRP * NSTATE + NSTATE * g, NSTATE)] = dcm[g] + _dot(dcb_b, c["bm_b"][g], NN_DIMS)
            dh_ref[cols, :] = dprev[g]
            for r in range(HG):
                h = HG * g + r
                oh = (lane8 == h).astype(f32)
                tmp = ddte_rs[g][r] * c["dte"][:, h:h + 1]
                dacs = dacs + oh * (jnp.sum(dseg[h], axis=1, keepdims=True) + yoff_rs[g][r] - tmp)
                dacs_cols = dacs_cols + (sub8 == h).astype(f32) * jnp.sum(dseg[h], axis=0, keepdims=True)
                dlast = dlast + oh * (_sum11(dhn_prev[g][HD * r:HD * (r + 1), :]) * c["cd"][:, h:h + 1] + _sum11(tmp))
                ddtv = ddtv + oh * ddtv_rs[r]
                ddsk = ddsk + oh * _sum11(ddsk_rs[r])

        row = lax.broadcasted_iota(jnp.int32, (BLK, 1), 0)
        dacs = dacs - dacs_cols.T + jnp.where(row == BLK - 1, dlast, 0.0)
        dda = lax.dot_general(_tri().astype(f32), dacs, TN_DIMS, preferred_element_type=f32, precision=HIGHEST)
        ddtv = ddtv + dda * a
        da = jnp.sum(dda * dtv, axis=0, keepdims=True)
        draw = ddtv * _sigmoid(raw)
        ddt_ref[...] = jnp.zeros_like(ddt_ref)
        ddt_ref[:, 0:NSSM] = draw
        dsm_ref[1:2, 0:NSSM] += jnp.sum(draw, axis=0, keepdims=True)
        dsm_ref[2:3, 0:NSSM] += da * a
        dsm_ref[3:4, 0:NSSM] += ddsk

    rev = lambda i: NBLK - 1 - i
    small = lambda shape: pl.BlockSpec(shape, lambda i: (0,) * len(shape))
    return pl.pallas_call(
        body, grid=(NBLK,),
        in_specs=[pl.BlockSpec((BLK, D_CONV), lambda i: (rev(i), 0)), pl.BlockSpec((BLK, D_SSM), lambda i: (rev(i), 0)),
                  pl.BlockSpec((BLK, 128), lambda i: (rev(i), 0)), pl.BlockSpec((BLK, D_SSM), lambda i: (rev(i), 1)),
                  pl.BlockSpec((None, NSSM * HD, NSTATE), lambda i: (rev(i), 0, 0)), pl.BlockSpec((BLK, D_SSM), lambda i: (rev(i), 0)),
                  small((DEPTH, NSSM)), small((DEPTH, NSSM)), small((DEPTH, NSSM)), small((DEPTH, D_SSM))],
        out_specs=[pl.BlockSpec((BLK, D_SSM), lambda i: (rev(i), 0)), pl.BlockSpec((BLK, D_CONV), lambda i: (rev(i), 0)),
                   pl.BlockSpec((BLK, 128), lambda i: (rev(i), 0)), small((8, D_SSM))],
        out_shape=[SDS((S, D_SSM), f32), SDS((S, D_CONV), f32), SDS((S, 128), f32), SDS((8, D_SSM), f32)],
        scratch_shapes=[pltpu.VMEM((NSSM * HD, NSTATE), f32), pltpu.VMEM((BLK, D_SSM), f32)],
        name="ssd_bwd", compiler_params=_cparams(1),
    )(xact, z, dt, dmix, hs, y, dt_bias, a_log, d_skip, norm_g)


def _my_place():
    return lax.axis_index("x"), lax.axis_index("y"), lax.axis_index("c")


def _dev_index(px, py, pc):
    return 4 * px + 2 * py + pc


def _slab2(kind, ref, idx):
    if kind == "stack":
        return ref.at[idx]
    if kind == "rows128":
        return ref.at[pl.ds(pl.multiple_of(idx * 128, 128), 128), :]
    if kind == "rows512":
        return ref.at[pl.ds(pl.multiple_of(idx * 512, 512), 512), :]
    return ref.at[:, pl.ds(pl.multiple_of(idx * 512, 512), 512)]


def _slab_shape(kind, full_shape):
    if kind == "stack":
        return tuple(full_shape[1:])
    if kind == "rows128":
        return (128, full_shape[1])
    if kind == "rows512":
        return (512, full_shape[1])
    return (full_shape[0], 512)


KIND = dict(w_in="stack", w_out="rows128", w_up="cols512", w_down="rows512", conv_w="stack")
FULL_SHAPE = dict(w_in=(N_DEV, D, D_IN // N_DEV), w_out=(D, D), w_up=(D, D_FF), w_down=(D_FF, D))
HBM_SPEC = pl.BlockSpec(memory_space=pltpu.HBM)
SEM_SPEC = pl.BlockSpec(memory_space=pltpu.SEMAPHORE)
SIDE_EFFECT = pltpu.SideEffectType.DATAFLOW_SIDE_EFFECTING


def _peers_all():
    x, y, c = _my_place()
    return [(x ^ ((r >> 2) & 1), y ^ ((r >> 1) & 1), c ^ (r & 1)) for r in range(1, N_DEV)]


def _split_start(name, bufs, n_copies, plan, deps=()):
    nb = len(bufs)

    def body(*refs):
        ins = refs[:nb]
        send_sems, recv_sems = refs[nb + len(deps)], refs[nb + len(deps) + 1]
        token = refs[-1]
        for i, (src, dst, dev) in enumerate(plan(ins)):
            pltpu.make_async_remote_copy(src_ref=src, dst_ref=dst, send_sem=send_sems.at[i], recv_sem=recv_sems.at[i],
                                         device_id=dev, device_id_type=MESH).start()
        token[...] = jnp.zeros_like(token)

    outs = pl.pallas_call(
        body, name=name,
        out_shape=(pltpu.SemaphoreType.DMA((n_copies,)), pltpu.SemaphoreType.DMA((n_copies,)),
                   *[pltpu.HBM(b.shape, b.dtype) for b in bufs], SDS((8, 128), f32)),
        in_specs=[HBM_SPEC] * nb + [ANY_SPEC] * len(deps),
        out_specs=(SEM_SPEC, SEM_SPEC, *[HBM_SPEC] * nb, pl.BlockSpec(memory_space=pltpu.VMEM)),
        input_output_aliases={i: 2 + i for i in range(nb)},
        compiler_params=pltpu.CompilerParams(has_side_effects=SIDE_EFFECT),
    )(*[pltpu.with_memory_space_constraint(b, pltpu.HBM) for b in bufs], *deps)
    return dict(send=outs[0], recv=outs[1], bufs=list(outs[2:2 + nb]), token=outs[-1], plan=plan, n=n_copies)


def _split_wait(name, started, after):
    bufs = started["bufs"]
    nb = len(bufs)
    plan = started["plan"]

    def body(*refs):
        ins = refs[:nb]
        send_sems, recv_sems = refs[nb], refs[nb + 1]
        for i, (src, dst, dev) in enumerate(plan(ins)):
            cp = pltpu.make_async_remote_copy(src_ref=src, dst_ref=dst, send_sem=send_sems.at[i], recv_sem=recv_sems.at[i],
                                              device_id=dev, device_id_type=MESH)
            cp.wait_send()
            cp.wait_recv()

    outs = pl.pallas_call(
        body, name=name, out_shape=tuple(pltpu.HBM(b.shape, b.dtype) for b in bufs),
        in_specs=[HBM_SPEC] * nb + [SEM_SPEC, SEM_SPEC] + [ANY_SPEC] * len(after), out_specs=(HBM_SPEC,) * nb,
        input_output_aliases={i: i for i in range(nb)},
        compiler_params=pltpu.CompilerParams(has_side_effects=SIDE_EFFECT),
    )(*bufs, started["send"], started["recv"], *after)
    return list(outs)


def _gather_start(name, names, fulls, deps):
    n_t = len(names)

    def plan(refs):
        x, y, c = _my_place()
        my_idx = _dev_index(x, y, c)
        targets = [(x, y, 1 - c), (1 - x, y, c), (x, 1 - y, c), (1 - x, 1 - y, c)]
        slabs = [_slab2(KIND[names[t]], refs[t], my_idx) for t in range(n_t)]
        return [(slabs[t], slabs[t], dev) for t in range(n_t) for dev in targets]

    return _split_start(name, list(fulls), 4 * n_t, plan, deps)


def _gather_finish(name, names, started, after):
    n_t = len(names)
    fulls = _split_wait(name + "_wait", started, after)
    slab_shapes = [SDS(_slab_shape(KIND[n], f.shape), f.dtype) for n, f in zip(names, fulls)]

    def body(*refs):
        ins = refs[:n_t]
        outs = refs[n_t:2 * n_t]
        stage = refs[2 * n_t:3 * n_t]
        load_sems, send_sems, recv_sems = refs[3 * n_t:]
        x, y, c = _my_place()
        chips = [(1 - x, y), (x, 1 - y), (1 - x, 1 - y)]
        pairs = [(t, j) for t in range(n_t) for j in range(3)]
        loads = [pltpu.make_async_copy(_slab2(KIND[names[t]], ins[t], _dev_index(*chips[j], c)), stage[t].at[j], load_sems.at[t, j])
                 for t, j in pairs]
        for cp in loads:
            cp.start()

        def copy(t, j, core):
            return pltpu.make_async_remote_copy(
                src_ref=stage[t].at[j], dst_ref=_slab2(KIND[names[t]], outs[t], _dev_index(*chips[j], core)),
                send_sem=send_sems.at[t, j], recv_sem=recv_sems.at[t, j], device_id=(x, y, 1 - c), device_id_type=MESH)

        sends = [copy(t, j, c) for t, j in pairs]
        for ld, cp in zip(loads, sends):
            ld.wait()
            cp.start()
        for t, j in pairs:
            copy(t, j, 1 - c).wait_recv()
        for cp in sends:
            cp.wait_send()

    return pl.pallas_call(
        body, in_specs=[ANY_SPEC] * n_t, out_specs=[ANY_SPEC] * n_t, out_shape=[SDS(b.shape, b.dtype) for b in fulls],
        input_output_aliases={t: t for t in range(n_t)},
        scratch_shapes=[pltpu.VMEM((3,) + s.shape, s.dtype) for s in slab_shapes]
        + [pltpu.SemaphoreType.DMA((n_t, 3)), pltpu.SemaphoreType.DMA((n_t, 3)), pltpu.SemaphoreType.DMA((n_t, 3))],
        name=name + "_pass", compiler_params=pltpu.CompilerParams(vmem_limit_bytes=VMEM_LIMIT),
    )(*fulls)


def _exchange_start(name, names, grads, deps):
    n_t = len(names)
    lands = [lax.empty((N_DEV,) + _slab_shape(KIND[n], g.shape), g.dtype) for n, g in zip(names, grads)]

    def plan(refs):
        my_idx = _dev_index(*_my_place())
        return [(_slab2(KIND[names[t]], refs[t], _dev_index(*peer)), refs[n_t + t].at[my_idx], peer)
                for t in range(n_t) for peer in _peers_all()]

    return _split_start(name, list(grads) + lands, 7 * n_t, plan, deps)


def _small_exchange_start(part, deps):
    land = lax.empty((N_DEV,) + part.shape, part.dtype)

    def plan(refs):
        my_idx = _dev_index(*_my_place())
        return [(refs[0], refs[1].at[my_idx], peer) for peer in _peers_all()]

    return _split_start("small_exchange", [part, land], N_DEV - 1, plan, deps)


def _w_in_assemble(stacked):
    tr = 256
    sh = D_IN // N_DEV

    def body(i_ref, o_ref):
        for j in range(N_DEV):
            o_ref[:, sh * j:sh * (j + 1)] = i_ref[j]
        o_ref[:, D_IN:D_IN_PAD] = jnp.zeros((tr, D_IN_PAD - D_IN), bf16)

    return pl.pallas_call(
        body, grid=(D // tr,), in_specs=[pl.BlockSpec((N_DEV, tr, sh), lambda i: (0, i, 0))],
        out_specs=pl.BlockSpec((None, tr, D_IN_PAD), lambda i: (0, i, 0)), out_shape=SDS((1, D, D_IN_PAD), bf16),
        name="w_in_assemble", compiler_params=_cparams(1),
    )(stacked)


def _w_in_slabs(dw_in):
    tr = 256
    sh = D_IN // N_DEV

    def body(i_ref, o_ref):
        for j in range(N_DEV):
            o_ref[j] = i_ref[:, sh * j:sh * (j + 1)]

    return pl.pallas_call(
        body, grid=(D // tr,), in_specs=[pl.BlockSpec((tr, D_IN_PAD), lambda i: (i, 0))],
        out_specs=pl.BlockSpec((N_DEV, tr, sh), lambda i: (0, i, 0)), out_shape=SDS((N_DEV, D, sh), bf16),
        name="w_in_slabs", compiler_params=_cparams(1),
    )(dw_in)


SMALL_NAMES = ("mix_norm_g", "mlp_norm_g", "conv_b", "ssm_norm_g", "q_gain", "k_gain", "sinks", "dt_bias", "a_log", "d_skip",
               "rel_bias", "conv_w")
MISC_LANES = dict(q_gain=(LANE_QG, HD), k_gain=(LANE_KG, HD), sinks=(LANE_SINK, NQ), dt_bias=(LANE_DTB, NSSM),
                  a_log=(LANE_ALOG, NSSM), d_skip=(LANE_DSKIP, NSSM))


def _pack_small_grads(smalls, drel_t, loss):
    def body(*refs):
        o_ref = refs[-1]
        drel_ref, loss_ref = refs[-3], refs[-2]
        o_ref[...] = jnp.zeros_like(o_ref)
        for l in range(DEPTH):
            mixg, mlpg, convb, convw, ssd, attn = refs[6 * l:6 * l + 6]
            o_ref[ROW_MIXG + l:ROW_MIXG + l + 1, :] = mixg[...]
            o_ref[ROW_MLPG + l:ROW_MLPG + l + 1, :] = mlpg[...]
            o_ref[ROW_CONVB + l:ROW_CONVB + l + 1, :] = convb[0:1, :]
            o_ref[ROW_SSMG + l:ROW_SSMG + l + 1, 0:D_SSM] = ssd[0:1, :]
            o_ref[ROW_CONVW + 4 * l:ROW_CONVW + 4 * l + 4, :] = convw[0:4, :]
            row = slice(ROW_MISC + l, ROW_MISC + l + 1)
            o_ref[row, LANE_QG:LANE_QG + HD] = attn[0:1, 0:HD]
            o_ref[row, LANE_KG:LANE_KG + HD] = attn[1:2, 0:HD]
            o_ref[row, LANE_SINK:LANE_SINK + NQ] = attn[2:3, 0:NQ]
            o_ref[row, LANE_DTB:LANE_DTB + NSSM] = ssd[1:2, 0:NSSM]
            o_ref[row, LANE_ALOG:LANE_ALOG + NSSM] = ssd[2:3, 0:NSSM]
            o_ref[row, LANE_DSKIP:LANE_DSKIP + NSSM] = ssd[3:4, 0:NSSM]
        o_ref[ROW_RELB:ROW_RELB + NQ, 0:N_BUCKETS] = drel_ref[...]
        o_ref[ROW_LOSS:ROW_LOSS + 1, 0:1] = loss_ref[0:1, 0:1]

    args = []
    for sm in smalls:
        args += [sm["mix_norm_g"], sm["mlp_norm_g"], sm["conv_b"], sm["conv_w"], sm["ssd"], sm["attn"]]
    args += [drel_t, loss]
    return pl.pallas_call(body, out_shape=SDS((SMALL_ROWS, D), f32), name="pack_small_grads")(*args)


def _adamw_small(part, land, w, m, v):
    n = len(SMALL_NAMES)

    def grad_of(name, g_ref):
        if name == "mix_norm_g":
            return g_ref[ROW_MIXG:ROW_MIXG + DEPTH, :]
        if name == "mlp_norm_g":
            return g_ref[ROW_MLPG:ROW_MLPG + DEPTH, :]
        if name == "conv_b":
            return g_ref[ROW_CONVB:ROW_CONVB + DEPTH, :]
        if name == "ssm_norm_g":
            return g_ref[ROW_SSMG:ROW_SSMG + DEPTH, 0:D_SSM]
        if name == "rel_bias":
            return g_ref[ROW_RELB:ROW_RELB + NQ, 0:N_BUCKETS].T
        lane, width = MISC_LANES[name]
        return g_ref[ROW_MISC:ROW_MISC + DEPTH, lane:lane + width]

    def body(part_ref, land_ref, *refs):
        ws, ms, vs = refs[:n], refs[n:2 * n], refs[2 * n:3 * n]
        loss_ref = refs[3 * n]
        outs = refs[3 * n + 1:-1]
        g_ref = refs[-1]
        me = _dev_index(*_my_place())
        for p in range(N_DEV):
            term = jnp.where(me == p, part_ref[...], land_ref[p])
            if p == 0:
                g_ref[...] = term
            else:
                g_ref[...] += term
        loss_ref[...] = g_ref[ROW_LOSS:ROW_LOSS + 1, 0:128]
        my_cols = pl.ds(pl.multiple_of(me * 128, 128), 128)
        for k, name in enumerate(SMALL_NAMES):
            g_out, d_out, m_out, v_out = outs[4 * k:4 * k + 4]
            if name == "conv_w":
                for l in range(DEPTH):
                    g = g_ref[ROW_CONVW + 4 * l:ROW_CONVW + 4 * l + 4, my_cols]
                    delta, m_new, v_new = _adamw_math(ws[k][l], ms[k][l], vs[k][l], g)
                    g_out[l], d_out[l], m_out[l], v_out[l] = g, delta, m_new, v_new
            else:
                g = grad_of(name, g_ref)
                delta, m_new, v_new = _adamw_math(ws[k][...], ms[k][...], vs[k][...], g)
                g_out[...], d_out[...], m_out[...], v_out[...] = g, delta, m_new, v_new

    ws = [w[name] for name in SMALL_NAMES]
    out_shape = [SDS((1, 128), f32)]
    for a in ws:
        out_shape += [SDS(a.shape, f32)] * 4
    return pl.pallas_call(body, out_shape=out_shape, name="adamw_small", scratch_shapes=[pltpu.VMEM((SMALL_ROWS, D), f32)])(
        part, land, *ws, *[m[name] for name in SMALL_NAMES], *[v[name] for name in SMALL_NAMES])


def _plain(tm, tn):
    return pl.BlockSpec((tm, tn), lambda i, j, k: (i, j))


def _rowblk(tm, width):
    return pl.BlockSpec((tm, width), lambda i, j, k: (i, 0))


def _store_epi(dtype):
    def epi(acc, i, j, ex, outs):
        outs[0][...] = acc.astype(dtype)
    return epi


def _layer_fwd(l, x, p, get_weights, bias, deps):
    wts = get_weights(l, "in", [x])
    h1 = _rms_fwd("rms_mix", x, p["mix_norm_g"], l, deps)

    def inproj_epi(acc, i, j, ex, outs):
        outs[0][...] = acc[:, 0:768]
        outs[1][...] = acc[:, 768:1280]
        outs[2][...] = acc[:, 1280:2304]
        outs[3][...] = acc[:, 2304:2432]

    tm = 256
    qkv, z, xbc, dt = _matmul(
        "in_proj", "nn", h1, wts["w_in"], tm=tm, tn=D_IN_PAD, tk=D,
        out_shape=[SDS((S, 768), f32), SDS((S, 512), f32), SDS((S, 1024), f32), SDS((S, 128), f32)],
        out_specs=[_rowblk(tm, 768), _rowblk(tm, 512), _rowblk(tm, 1024), _rowblk(tm, 128)], epilogue=inproj_epi)
    attn = _attn_fwd(qkv, p["q_gain"], p["k_gain"], p["sinks"], bias, l)
    xact = _conv_fwd(xbc, wts["conv_w"], p["conv_b"], l)
    mix, hs, y_ssd = _ssd_fwd(xact, z, dt, attn, p["dt_bias"], p["a_log"], p["d_skip"], p["ssm_norm_g"], l)
    wts = dict(wts, **get_weights(l, "rest", [mix]))

    def resid_epi(acc, i, j, ex, outs):
        outs[0][...] = ex[0][...] + acc

    x_mid = _matmul("out_proj", "nn", mix, wts["w_out"], tm=256, tn=D, tk=D, out_shape=SDS((S, D), f32),
                    out_specs=_plain(256, D), epilogue=resid_epi, extras=(x,), extra_specs=(_plain(256, D),))
    h2 = _rms_fwd("rms_mlp", x_mid, p["mlp_norm_g"], l)

    def up_epi(acc, i, j, ex, outs):
        r = jnp.maximum(acc, 0.0)
        outs[0][...] = r.astype(bf16)
        outs[1][...] = (r * r).astype(bf16)

    r_act, a_act = _matmul("mlp_up", "nn", h2, wts["w_up"], tm=256, tn=D_FF, tk=D,
                           out_shape=[SDS((S, D_FF), bf16)] * 2, out_specs=[_plain(256, D_FF)] * 2, epilogue=up_epi)
    x_out = _matmul("mlp_down", "nn", a_act, wts["w_down"], tm=256, tn=D, tk=D_FF, out_shape=SDS((S, D), f32),
                    out_specs=_plain(256, D), epilogue=resid_epi, extras=(x_mid,), extra_specs=(_plain(256, D),))
    saved = dict(x=x, h1=h1, qkv=qkv, z=z, xbc=xbc, dt=dt, xact=xact, mix=mix, hs=hs, y_ssd=y_ssd, x_mid=x_mid, h2=h2, r=r_act,
                 a=a_act, wts=wts)
    return x_out, saved


def _layer_bwd(l, dx_out, sv, p, bias, deps, send):
    wts = sv["wts"]

    def du_epi(acc, i, j, ex, outs):
        outs[0][...] = (acc * (2.0 * ex[0][...].astype(f32))).astype(bf16)

    du = _matmul("mlp_da", "nt", dx_out, wts["w_down"], tm=256, tn=D_FF, tk=D, out_shape=SDS((S, D_FF), bf16),
                 out_specs=_plain(256, D_FF), epilogue=du_epi, extras=(sv["r"],), extra_specs=(_plain(256, D_FF),), deps=deps)
    dw_down = _matmul("dw_down", "tn", sv["a"], dx_out, tm=1024, tn=D, tk=S, out_shape=SDS((D_FF, D), bf16),
                      out_specs=_plain(1024, D), epilogue=_store_epi(bf16))
    dw_up = _matmul("dw_up", "tn", sv["h2"], du, tm=D, tn=1024, tk=S, out_shape=SDS((D, D_FF), bf16),
                    out_specs=_plain(D, 1024), epilogue=_store_epi(bf16))
    deps = send(l, dict(w_down=dw_down, w_up=dw_up))
    gfull = pl.BlockSpec((DEPTH, D), lambda i, j, k: (0, 0))
    grow = pl.BlockSpec((1, D), lambda i, j, k: (0, 0))
    dx_mid, dg_mlp = _matmul(
        "mlp_dh", "nt", du, wts["w_up"], tm=256, tn=D, tk=D_FF, out_shape=[SDS((S, D), f32), SDS((1, D), f32)],
        out_specs=[_plain(256, D), grow], epilogue=_rms_bwd_epilogue(l),
        extras=(sv["x_mid"], p["mlp_norm_g"], dx_out), extra_specs=(_plain(256, D), gfull, _plain(256, D)), deps=deps)
    dmix = _matmul("out_proj_da", "nt", dx_mid, wts["w_out"], tm=256, tn=D, tk=D, out_shape=SDS((S, D), f32),
                   out_specs=_plain(256, D), epilogue=_store_epi(f32))
    dw_out = _matmul("dw_out", "tn", sv["mix"], dx_mid, tm=D, tn=512, tk=512, out_shape=SDS((D, D), bf16),
                     out_specs=_plain(D, 512), epilogue=_store_epi(bf16))
    dz, dxact, ddt, dsm_ssd = _ssd_bwd(sv["xact"], sv["z"], sv["dt"], dmix, sv["hs"], sv["y_ssd"], p["dt_bias"], p["a_log"],
                                       p["d_skip"], p["ssm_norm_g"], l)
    dxbc, dconv_w, dconv_b = _conv_bwd(sv["xbc"], dxact, wts["conv_w"], p["conv_b"], l)
    dqkv, dbias, dsm_attn = _attn_bwd(sv["qkv"], dmix, p["q_gain"], p["k_gain"], p["sinks"], bias, l)
    dproj = _pack_dproj(dqkv, dz, dxbc, ddt)
    dw_in = _matmul("dw_in", "tn", sv["h1"], dproj, tm=D, tn=640, tk=S, out_shape=SDS((D, D_IN_PAD), bf16),
                    out_specs=_plain(D, 640), epilogue=_store_epi(bf16))
    deps = send(l, dict(w_out=dw_out, w_in=_w_in_slabs(dw_in)))
    dx, dg_mix = _matmul(
        "in_proj_dh", "nt", dproj, wts["w_in"], tm=256, tn=D, tk=D_IN_PAD, out_shape=[SDS((S, D), f32), SDS((1, D), f32)],
        out_specs=[_plain(256, D), grow], epilogue=_rms_bwd_epilogue(l),
        extras=(sv["x"], p["mix_norm_g"], dx_mid), extra_specs=(_plain(256, D), gfull, _plain(256, D)), deps=deps)
    small = dict(mix_norm_g=dg_mix, mlp_norm_g=dg_mlp, conv_w=dconv_w, conv_b=dconv_b, ssd=dsm_ssd, attn=dsm_attn, dbias=dbias)
    return dx, small, deps


def _local_step(x, tgt, p, get_weights, send, deps):
    onehot_t = jnp.asarray(_onehot_buckets())
    bias = _bias_build(p["rel_bias"].T, onehot_t).reshape(NQ, BLK, 2 * BLK)
    saved = []
    h = x
    for l in range(DEPTH):
        h, sv = _layer_fwd(l, h, p, get_weights, bias, deps if l == 0 else ())
        saved.append(sv)
    dx, loss = _loss_kernel(h, tgt)
    smalls = [None] * DEPTH
    deps = ()
    for l in reversed(range(DEPTH)):
        dx, smalls[l], deps = _layer_bwd(l, dx, saved[l], p, bias, deps, send)
    drel_t = _bias_grad(smalls[0]["dbias"].reshape(NQ, -1), smalls[1]["dbias"].reshape(NQ, -1), onehot_t)
    return dx, _pack_small_grads(smalls, drel_t, loss), deps


WEIGHT_ORDER = ("mix_norm_g", "w_in", "q_gain", "k_gain", "sinks", "rel_bias", "conv_w", "conv_b", "dt_bias", "a_log", "d_skip",
                "ssm_norm_g", "w_out", "mlp_norm_g", "w_up", "w_down")


def kernel(x, mix_norm_g, w_in, q_gain, k_gain, sinks, rel_bias, conv_w, conv_b, dt_bias, a_log, d_skip, ssm_norm_g, w_out, mlp_norm_g, w_up, w_down, loss_target, m_mix_norm_g, m_w_in, m_q_gain, m_k_gain, m_sinks, m_rel_bias, m_conv_w, m_conv_b, m_dt_bias, m_a_log, m_d_skip, m_ssm_norm_g, m_w_out, m_mlp_norm_g, m_w_up, m_w_down, v_mix_norm_g, v_w_in, v_q_gain, v_k_gain, v_sinks, v_rel_bias, v_conv_w, v_conv_b, v_dt_bias, v_a_log, v_d_skip, v_ssm_norm_g, v_w_out, v_mlp_norm_g, v_w_up, v_w_down):
    w = dict(mix_norm_g=mix_norm_g, w_in=w_in, q_gain=q_gain, k_gain=k_gain, sinks=sinks, rel_bias=rel_bias, conv_w=conv_w,
             conv_b=conv_b, dt_bias=dt_bias, a_log=a_log, d_skip=d_skip, ssm_norm_g=ssm_norm_g, w_out=w_out,
             mlp_norm_g=mlp_norm_g, w_up=w_up, w_down=w_down)
    m = dict(mix_norm_g=m_mix_norm_g, w_in=m_w_in, q_gain=m_q_gain, k_gain=m_k_gain, sinks=m_sinks, rel_bias=m_rel_bias,
             conv_w=m_conv_w, conv_b=m_conv_b, dt_bias=m_dt_bias, a_log=m_a_log, d_skip=m_d_skip, ssm_norm_g=m_ssm_norm_g,
             w_out=m_w_out, mlp_norm_g=m_mlp_norm_g, w_up=m_w_up, w_down=m_w_down)
    v = dict(mix_norm_g=v_mix_norm_g, w_in=v_w_in, q_gain=v_q_gain, k_gain=v_k_gain, sinks=v_sinks, rel_bias=v_rel_bias,
             conv_w=v_conv_w, conv_b=v_conv_b, dt_bias=v_dt_bias, a_log=v_a_log, d_skip=v_d_skip, ssm_norm_g=v_ssm_norm_g,
             w_out=v_w_out, mlp_norm_g=v_mlp_norm_g, w_up=v_w_up, w_down=v_w_down)
    big = ("w_in", "w_out", "w_up", "w_down")

    my_idx = _dev_index(*_my_place()).astype(jnp.int32).reshape(1)

    fulls = {n: _cast_to_full("cast_" + n, w[n], KIND[n], FULL_SHAPE[n], my_idx, bf16) for n in big}
    conv_full = _cast_to_full("cast_conv_w", conv_w.reshape(1, DEPTH * 4, 128), "stack", (N_DEV, DEPTH * 4, 128), my_idx, f32)[0]
    rest = ["w_out", "w_up", "w_down"]
    g0 = _gather_start("gather0", ["w_in", "conv_w"], [fulls["w_in"][0], conv_full], ())
    g1 = _gather_start("gather1", rest, [fulls[n][0] for n in rest], (g0["token"],))
    g2 = _gather_start("gather2", list(big), [fulls[n][1] for n in big], (g1["token"],))
    held = {}

    def get_weights(l, part, after):
        if l == 0 and part == "in":
            full_in, full_conv = _gather_finish("gather0", ["w_in", "conv_w"], g0, list(after) + [g2["token"]])
            held["conv_w"] = jnp.transpose(full_conv.reshape(N_DEV, DEPTH, 4, 128), (1, 2, 0, 3)).reshape(DEPTH, 4, D_CONV)
            return dict(w_in=_w_in_assemble(full_in), conv_w=held["conv_w"])
        if l == 0:
            full = _gather_finish("gather1", rest, g1, after)
            return {n: f[None] for n, f in zip(rest, full)}
        if part == "in":
            held["layer1"] = _gather_finish("gather2", list(big), g2, after)
            return dict(w_in=_w_in_assemble(held["layer1"][0]), conv_w=held["conv_w"])
        return {n: f[None] for n, f in zip(rest, held["layer1"][1:])}

    pending = []

    def send(l, grads):
        names = list(grads)
        started = _exchange_start("exchange%d_%s" % (l, names[0]), names, [grads[n] for n in names], ())
        pending.append((l, names, started))
        return (started["token"],)

    dx, small_part, _ = _local_step(x.reshape(S, D), loss_target.reshape(S, D), w, get_weights, send, (g2["token"],))

    small = _small_exchange_start(small_part, ())
    tiles = dict(w_in=256, w_out=128, w_up=256, w_down=256)
    flat = lambda a: a.reshape(a.shape[0] * a.shape[1], a.shape[2])
    outs_of = {n: None for n in big}
    after = [dx, small["token"]]
    for l, names, started in pending:
        bufs = _split_wait("exchange%d_%s_wait" % (l, names[0]), started, after)
        for t, n in enumerate(names):
            outs_of[n] = _adamw_layer("adamw_%s%d" % (n, l), KIND[n], l, flat(w[n]), flat(m[n]), flat(v[n]),
                                      bufs[len(names) + t], bufs[t], my_idx, outs_of[n], tiles[n])
        after = [outs_of[names[-1]][0]]
    res = {n: [o.reshape(w[n].shape) for o in outs_of[n]] for n in big}
    small_part, small_land = _split_wait("small_exchange_wait", small, after)
    small_outs = _adamw_small(small_part, small_land, w, m, v)
    loss = small_outs[0][0, 0]
    for k, name in enumerate(SMALL_NAMES):
        res[name] = small_outs[1 + 4 * k:5 + 4 * k]

    result = [loss, dx.reshape(1, S, D)]
    for k in range(4):
        result += [res[name][k] for name in WEIGHT_ORDER]
    return tuple(result)
```

```python
import functools
import math

import numpy as np
import jax
import jax.numpy as jnp
from jax import lax
from jax.experimental import pallas as pl
from jax.experimental.pallas import tpu as pltpu

f32 = jnp.float32
bf16 = jnp.bfloat16
SDS = jax.ShapeDtypeStruct
MESH = pl.DeviceIdType.MESH
HIGHEST = lax.Precision.HIGHEST

S = 2048
D = 1024
DEPTH = 2
BLK = 128
NBLK = S // BLK
HD = 64
NQ = 8
NKV = 2
NSSM = 8
NGRP = 2
NSTATE = 128
D_ATTN = 512
D_SSM = 512
D_CONV = 1024
D_FF = 4096
D_IN = 2312
D_IN_PAD = 2560
N_BUCKETS = 32
EPS = 1e-6
N_DEV = 8
VMEM_LIMIT = 48 * 1024 * 1024

ADAM_LR = 0.001
ADAM_B1 = 0.9
ADAM_B2 = 0.999
ADAM_EPS = 1e-08
ADAM_WD = 0.01
ADAM_STEP = 10

NT_DIMS = (((1,), (1,)), ((), ()))
TN_DIMS = (((0,), (0,)), ((), ()))
NN_DIMS = (((1,), (0,)), ((), ()))

ROW_MIXG = 0
ROW_MLPG = 2
ROW_CONVB = 4
ROW_SSMG = 6
ROW_MISC = 8
ROW_RELB = 10
ROW_CONVW = 18
ROW_LOSS = 26
SMALL_ROWS = 32
LANE_QG, LANE_KG, LANE_SINK, LANE_DTB, LANE_ALOG, LANE_DSKIP = 0, 64, 128, 256, 384, 512


def _dot(a, b, dims):
    return lax.dot_general(a, b, dims, preferred_element_type=f32)


def _cparams(n_axes):
    return pltpu.CompilerParams(dimension_semantics=("arbitrary",) * n_axes, vmem_limit_bytes=VMEM_LIMIT)


def _sum11(v):
    return jnp.sum(jnp.sum(v, axis=1, keepdims=True), axis=0, keepdims=True)


def _sigmoid(v):
    return 1.0 / (1.0 + jnp.exp(-v))


ANY_SPEC = pl.BlockSpec(memory_space=pl.ANY)


def _matmul(name, mode, a, b, *, layer=0, tm, tn, tk, out_shape, out_specs, epilogue, extras=(), extra_specs=(), deps=(),
            prologue=None):
    extras = tuple(extras) + tuple(deps)
    extra_specs = tuple(extra_specs) + (ANY_SPEC,) * len(deps)
    if mode == "tn":
        t_dim, m_dim = a.shape
        n_dim = b.shape[1]
        grid = (m_dim // tm, n_dim // tn, t_dim // tk)
        a_spec = pl.BlockSpec((tk, tm), lambda i, j, k: (k, i))
        b_spec = pl.BlockSpec((tk, tn), lambda i, j, k: (k, j))
        dims = TN_DIMS
    elif mode == "nn":
        m_dim, k_dim = a.shape
        n_dim = b.shape[-1]
        grid = (m_dim // tm, n_dim // tn, k_dim // tk)
        a_spec = pl.BlockSpec((tm, tk), lambda i, j, k: (i, k))
        b_spec = pl.BlockSpec((None, tk, tn), lambda i, j, k: (layer, k, j))
        dims = NN_DIMS
    else:
        m_dim, k_dim = a.shape
        n_dim = b.shape[-2]
        grid = (m_dim // tm, n_dim // tn, k_dim // tk)
        a_spec = pl.BlockSpec((tm, tk), lambda i, j, k: (i, k))
        b_spec = pl.BlockSpec((None, tn, tk), lambda i, j, k: (layer, j, k))
        dims = NT_DIMS
    nk = grid[2]
    n_ex = len(extras)

    def body(a_ref, b_ref, *rest):
        ex = rest[:n_ex - len(deps)]
        outs = rest[n_ex:-1]
        acc = rest[-1]
        i = pl.program_id(0)
        j = pl.program_id(1)
        k = pl.program_id(2)
        lhs = a_ref[...].astype(bf16) if prologue is None else prologue(a_ref, ex, outs)
        part = _dot(lhs, b_ref[...].astype(bf16), dims)
        if nk == 1:
            epilogue(part, i, j, ex, outs)
        else:
            @pl.when(k == 0)
            def _():
                acc[...] = part

            @pl.when(k > 0)
            def _():
                acc[...] += part

            @pl.when(k == nk - 1)
            def _():
                epilogue(acc[...], i, j, ex, outs)

    return pl.pallas_call(
        body, grid=grid, in_specs=[a_spec, b_spec, *extra_specs], out_specs=out_specs, out_shape=out_shape,
        scratch_shapes=[pltpu.VMEM((tm, tn) if nk > 1 else (8, 128), f32)], name=name, compiler_params=_cparams(3),
    )(a, b, *extras)


def _rms_bwd_epilogue(layer):
    def epi(acc, i, j, ex, outs):
        x_ref, g_ref, dres_ref = ex
        dx_ref, dg_ref = outs
        xv = x_ref[...]
        r = lax.rsqrt(jnp.mean(xv * xv, axis=-1, keepdims=True) + EPS)
        xhat = xv * r
        w = acc * g_ref[layer:layer + 1, :]
        dx_ref[...] = dres_ref[...] + r * (w - xhat * jnp.mean(xhat * w, axis=-1, keepdims=True))
        dg = jnp.sum(acc * xhat, axis=0, keepdims=True)

        @pl.when(i == 0)
        def _():
            dg_ref[...] = dg

        @pl.when(i > 0)
        def _():
            dg_ref[...] += dg
    return epi


def _pack_dproj(dqkv, dz, dxbc, ddt):
    tr = 256

    def body(a_ref, b_ref, c_ref, d_ref, o_ref):
        o_ref[:, 0:768] = a_ref[...].astype(bf16)
        o_ref[:, 768:1280] = b_ref[...].astype(bf16)
        o_ref[:, 1280:2304] = c_ref[...].astype(bf16)
        o_ref[:, 2304:2432] = d_ref[...].astype(bf16)
        o_ref[:, 2432:D_IN_PAD] = jnp.zeros((tr, D_IN_PAD - 2432), bf16)

    return pl.pallas_call(
        body, grid=(S // tr,),
        in_specs=[pl.BlockSpec((tr, 768), lambda i: (i, 0)), pl.BlockSpec((tr, 512), lambda i: (i, 0)),
                  pl.BlockSpec((tr, 1024), lambda i: (i, 0)), pl.BlockSpec((tr, 128), lambda i: (i, 0))],
        out_specs=pl.BlockSpec((tr, D_IN_PAD), lambda i: (i, 0)), out_shape=SDS((S, D_IN_PAD), bf16),
        name="pack_dproj", compiler_params=_cparams(1),
    )(dqkv, dz, dxbc, ddt)


def _own_slab_spec(kind, tr, cols, nblk):
    if kind == "stack":
        return pl.BlockSpec((None, tr, cols), lambda i, idx: (idx[0], i, 0))
    if kind == "cols512":
        return pl.BlockSpec((tr, cols), lambda i, idx: (i, idx[0]))
    return pl.BlockSpec((tr, cols), lambda i, idx: (idx[0] * nblk + i, 0))


def _cast_to_full(name, w, kind, full_shape, my_idx, dtype):
    n_layers, rows, cols = w.shape
    tr = min(rows, 256)
    nblk = rows // tr

    def body(idx_ref, w_ref, *o_refs):
        for l in range(n_layers):
            o_refs[l][...] = w_ref[l].astype(dtype)

    grid_spec = pltpu.PrefetchScalarGridSpec(
        num_scalar_prefetch=1, grid=(nblk,), in_specs=[pl.BlockSpec((n_layers, tr, cols), lambda i, idx: (0, i, 0))],
        out_specs=[_own_slab_spec(kind, tr, cols, nblk)] * n_layers)
    return pl.pallas_call(body, grid_spec=grid_spec, out_shape=[SDS(full_shape, dtype)] * n_layers, name=name,
                          compiler_params=_cparams(1))(my_idx, w)


def _adamw_math(w, m, v, g):
    m_new = ADAM_B1 * m + (1.0 - ADAM_B1) * g
    v_new = ADAM_B2 * v + (1.0 - ADAM_B2) * (g * g)
    m_hat = m_new / (1.0 - ADAM_B1 ** ADAM_STEP)
    v_hat = v_new / (1.0 - ADAM_B2 ** ADAM_STEP)
    delta = -ADAM_LR * (m_hat / (jnp.sqrt(v_hat) + ADAM_EPS) + ADAM_WD * w)
    return delta, m_new, v_new


def _adamw_layer(name, kind, layer, w, m, v, land, g_full, my_idx, prev, tr):
    rows2, cols = w.shape
    rows = rows2 // DEPTH
    nblk = rows // tr
    own_spec = _own_slab_spec(kind, tr, cols, nblk)
    n_prev = 0 if prev is None else 4

    def body(idx_ref, w_ref, m_ref, v_ref, land_ref, own_ref, *rest):
        g_ref, d_ref, mo_ref, vo_ref = rest[n_prev:]
        me = idx_ref[0]
        g = None
        for p in range(N_DEV):
            part = jnp.where(me == p, own_ref[...], land_ref[p]).astype(f32)
            g = part if g is None else g + part
        delta, m_new, v_new = _adamw_math(w_ref[...], m_ref[...], v_ref[...], g)
        g_ref[...] = g
        d_ref[...] = delta
        mo_ref[...] = m_new
        vo_ref[...] = v_new

    blk = pl.BlockSpec((tr, cols), lambda i, idx: (layer * nblk + i, 0))
    grid_spec = pltpu.PrefetchScalarGridSpec(
        num_scalar_prefetch=1, grid=(nblk,),
        in_specs=[blk, blk, blk, pl.BlockSpec((N_DEV, tr, cols), lambda i, idx: (0, i, 0)), own_spec] + [ANY_SPEC] * n_prev,
        out_specs=[blk, blk, blk, blk])
    aliases = {} if prev is None else {6 + k: k for k in range(4)}
    return pl.pallas_call(
        body, grid_spec=grid_spec, out_shape=[SDS((rows2, cols), f32)] * 4, name=name, input_output_aliases=aliases,
        compiler_params=_cparams(1),
    )(my_idx, w, m, v, land, g_full, *([] if prev is None else prev))


def _bucket_table():
    qi = np.arange(BLK)[:, None]
    kj = np.arange(2 * BLK)[None, :]
    dist = qi + BLK - kj
    dcl = np.clip(dist, 0, None)
    max_exact = N_BUCKETS // 2
    d_f = np.maximum(dcl, 1).astype(np.float32)
    large = max_exact + (np.log(d_f / np.float32(max_exact)) / np.float32(math.log(128 / max_exact))
                         * np.float32(N_BUCKETS - max_exact)).astype(np.int32)
    large = np.minimum(large, N_BUCKETS - 1)
    bucket = np.where(dcl < max_exact, dcl, large)
    in_window = (dist >= 0) & (dist < BLK)
    return bucket.astype(np.int32), in_window


def _onehot_buckets():
    bucket, _ = _bucket_table()
    oh = (bucket.reshape(-1)[None, :] == np.arange(N_BUCKETS)[:, None]).astype(np.float32)
    return oh


def _bias_build(rel_bias_t, onehot_t):
    def body(r_ref, o_ref, out_ref):
        out_ref[...] = jnp.dot(r_ref[...], o_ref[...], preferred_element_type=f32, precision=HIGHEST)

    tn = 4096
    return pl.pallas_call(
        body, grid=(BLK * 2 * BLK // tn,),
        in_specs=[pl.BlockSpec((NQ, N_BUCKETS), lambda i: (0, 0)), pl.BlockSpec((N_BUCKETS, tn), lambda i: (0, i))],
        out_specs=pl.BlockSpec((NQ, tn), lambda i: (0, i)), out_shape=SDS((NQ, BLK * 2 * BLK), f32), name="bias_build",
        compiler_params=_cparams(1),
    )(rel_bias_t, onehot_t)


def _bias_grad(dbias0, dbias1, onehot_t):
    tn = 4096
    nsteps = BLK * 2 * BLK // tn

    def body(a_ref, b_ref, o_ref, out_ref):
        part = lax.dot_general(a_ref[...] + b_ref[...], o_ref[...], NT_DIMS, preferred_element_type=f32, precision=HIGHEST)

        @pl.when(pl.program_id(0) == 0)
        def _():
            out_ref[...] = part

        @pl.when(pl.program_id(0) > 0)
        def _():
            out_ref[...] += part

    return pl.pallas_call(
        body, grid=(nsteps,),
        in_specs=[pl.BlockSpec((NQ, tn), lambda i: (0, i)), pl.BlockSpec((NQ, tn), lambda i: (0, i)),
                  pl.BlockSpec((N_BUCKETS, tn), lambda i: (0, i))],
        out_specs=pl.BlockSpec((NQ, N_BUCKETS), lambda i: (0, 0)), out_shape=SDS((NQ, N_BUCKETS), f32), name="bias_grad",
        compiler_params=_cparams(1),
    )(dbias0, dbias1, onehot_t)


def _attn_mask(n):
    qi = lax.broadcasted_iota(jnp.int32, (BLK, 2 * BLK), 0)
    kj = lax.broadcasted_iota(jnp.int32, (BLK, 2 * BLK), 1)
    dist = qi + BLK - kj
    first_key = jnp.where(n > 0, 0, BLK)
    return (dist >= 0) & (dist < BLK) & (kj >= first_key)


def _head_norm(t, gain):
    r = lax.rsqrt(jnp.mean(t * t, axis=-1, keepdims=True) + EPS)
    that = t * r
    return that, r, that * gain


def _softmax_with_sink(s, sink):
    m = jnp.maximum(jnp.max(s, axis=-1, keepdims=True), sink)
    p = jnp.exp(s - m)
    psink = jnp.exp(sink - m)
    inv = 1.0 / (jnp.sum(p, axis=-1, keepdims=True) + psink)
    return p * inv, psink * inv


GQ = NQ // NKV


def _group_rows(x_ref, sk_ref, layer, j):
    heads = [GQ * j + g for g in range(GQ)]
    xs = jnp.concatenate([x_ref[:, pl.ds(HD * h, HD)] for h in heads], axis=0)
    sink = jnp.concatenate([jnp.broadcast_to(sk_ref[layer:layer + 1, h:h + 1], (BLK, 1)) for h in heads], axis=0)
    return xs, sink


def _attn_fwd(qkv, q_gain, k_gain, sinks, bias, layer):
    def body(q_ref, kc_ref, kp_ref, vc_ref, vp_ref, qg_ref, kg_ref, sk_ref, bias_ref, o_ref):
        n = pl.program_id(0)
        mask = jnp.tile(_attn_mask(n), (GQ, 1))
        qg = qg_ref[layer:layer + 1, :]
        kg = kg_ref[layer:layer + 1, :]
        grp = range(NKV)
        kbs = [jnp.concatenate([kp_ref[:, pl.ds(HD * j, HD)], kc_ref[:, pl.ds(HD * j, HD)]], axis=0) for j in grp]
        vbs = [jnp.concatenate([vp_ref[:, pl.ds(HD * j, HD)], vc_ref[:, pl.ds(HD * j, HD)]], axis=0).astype(bf16) for j in grp]
        kn_b = [_head_norm(kbs[j], kg)[2].astype(bf16) for j in grp]
        rows = [_group_rows(q_ref, sk_ref, layer, j) for j in grp]
        qn_b = [_head_norm(rows[j][0], qg)[2].astype(bf16) for j in grp]
        ss = [_dot(qn_b[j], kn_b[j], NT_DIMS) * (HD ** -0.5) + bias_ref[GQ * j:GQ * (j + 1)].reshape(GQ * BLK, 2 * BLK) for j in grp]
        ps = [_softmax_with_sink(jnp.where(mask, ss[j], -jnp.inf), rows[j][1])[0] for j in grp]
        outs = [_dot(ps[j].astype(bf16), vbs[j], NN_DIMS).astype(bf16) for j in grp]
        for j in grp:
            for g in range(GQ):
                o_ref[:, pl.ds(HD * (GQ * j + g), HD)] = outs[j][BLK * g:BLK * (g + 1), :]

    prev = lambda n: jnp.maximum(n - 1, 0)
    small = lambda shape: pl.BlockSpec(shape, lambda n: (0,) * len(shape))
    return pl.pallas_call(
        body, grid=(NBLK,),
        in_specs=[pl.BlockSpec((BLK, D_ATTN), lambda n: (n, 0)),
                  pl.BlockSpec((BLK, 128), lambda n: (n, 4)), pl.BlockSpec((BLK, 128), lambda n: (prev(n), 4)),
                  pl.BlockSpec((BLK, 128), lambda n: (n, 5)), pl.BlockSpec((BLK, 128), lambda n: (prev(n), 5)),
                  small((DEPTH, HD)), small((DEPTH, HD)), small((DEPTH, NQ)), small((NQ, BLK, 2 * BLK))],
        out_specs=pl.BlockSpec((BLK, D_ATTN), lambda n: (n, 0)), out_shape=SDS((S, D_ATTN), bf16),
        name="attn_fwd", compiler_params=_cparams(1),
    )(qkv, qkv, qkv, qkv, qkv, q_gain, k_gain, sinks, bias)


def _attn_bwd(qkv, dmix, q_gain, k_gain, sinks, bias, layer):
    def body(q_ref, kc_ref, kp_ref, vc_ref, vp_ref, do_ref, qg_ref, kg_ref, sk_ref, bias_ref,
             dqkv_ref, dbias_ref, dsm_ref, carry):
        i = pl.program_id(0)
        n = NBLK - 1 - i
        mask = jnp.tile(_attn_mask(n), (GQ, 1))
        qg = qg_ref[layer:layer + 1, :]
        kg = kg_ref[layer:layer + 1, :]
        lane = lax.broadcasted_iota(jnp.int32, (1, 128), 1)

        @pl.when(i == 0)
        def _():
            carry[...] = jnp.zeros_like(carry)
            dbias_ref[...] = jnp.zeros_like(dbias_ref)
            dsm_ref[...] = jnp.zeros_like(dsm_ref)

        grp = range(NKV)
        kbs = [jnp.concatenate([kp_ref[:, pl.ds(HD * j, HD)], kc_ref[:, pl.ds(HD * j, HD)]], axis=0) for j in grp]
        vbs = [jnp.concatenate([vp_ref[:, pl.ds(HD * j, HD)], vc_ref[:, pl.ds(HD * j, HD)]], axis=0).astype(bf16) for j in grp]
        knorm = [_head_norm(kbs[j], kg) for j in grp]
        kn_b = [knorm[j][2].astype(bf16) for j in grp]
        rows = [_group_rows(q_ref, sk_ref, layer, j) for j in grp]
        qnorm = [_head_norm(rows[j][0], qg) for j in grp]
        qn_b = [qnorm[j][2].astype(bf16) for j in grp]
        ss = [_dot(qn_b[j], kn_b[j], NT_DIMS) * (HD ** -0.5) + bias_ref[GQ * j:GQ * (j + 1)].reshape(GQ * BLK, 2 * BLK) for j in grp]
        sm = [_softmax_with_sink(jnp.where(mask, ss[j], -jnp.inf), rows[j][1]) for j in grp]
        do_b = [jnp.concatenate([do_ref[:, pl.ds(HD * (GQ * j + g), HD)] for g in range(GQ)], axis=0).astype(bf16) for j in grp]
        dps = [_dot(do_b[j], vbs[j], NT_DIMS) for j in grp]
        deltas = [jnp.sum(sm[j][0] * dps[j], axis=-1, keepdims=True) for j in grp]
        dss = [sm[j][0] * (dps[j] - deltas[j]) for j in grp]
        ds_b = [(dss[j] * (HD ** -0.5)).astype(bf16) for j in grp]
        dqn = [_dot(ds_b[j], kn_b[j], NN_DIMS) for j in grp]
        dkn = [_dot(ds_b[j], qn_b[j], TN_DIMS) for j in grp]
        dvs = [_dot(sm[j][0].astype(bf16), do_b[j], TN_DIMS) for j in grp]
        dqg = jnp.zeros((1, HD), f32)
        dkg = jnp.zeros((1, HD), f32)
        dsink = jnp.zeros((1, 128), f32)
        for j in grp:
            dbias_ref[GQ * j:GQ * (j + 1)] += dss[j].reshape(GQ, BLK, 2 * BLK)
            dsk = sm[j][1] * deltas[j]
            for g in range(GQ):
                dsink = dsink + jnp.where(lane == GQ * j + g, -_sum11(dsk[BLK * g:BLK * (g + 1), :]), 0.0)
            qhat, rq, _ = qnorm[j]
            w = dqn[j] * qg
            dq = rq * (w - qhat * jnp.mean(qhat * w, axis=-1, keepdims=True))
            for g in range(GQ):
                dqkv_ref[:, pl.ds(HD * (GQ * j + g), HD)] = dq[BLK * g:BLK * (g + 1), :]
            dqg = dqg + jnp.sum(dqn[j] * qhat, axis=0, keepdims=True)
            khat, rk, _ = knorm[j]
            w = dkn[j] * kg
            dk = rk * (w - khat * jnp.mean(khat * w, axis=-1, keepdims=True))
            dkg = dkg + jnp.sum(dkn[j] * khat, axis=0, keepdims=True)
            dqkv_ref[:, pl.ds(D_ATTN + HD * j, HD)] = dk[BLK:, :] + carry[:, pl.ds(HD * j, HD)]
            dqkv_ref[:, pl.ds(D_ATTN + 128 + HD * j, HD)] = dvs[j][BLK:, :] + carry[:, pl.ds(128 + HD * j, HD)]
            carry[:, pl.ds(HD * j, HD)] = dk[:BLK, :]
            carry[:, pl.ds(128 + HD * j, HD)] = dvs[j][:BLK, :]
        dsm_ref[0:1, 0:HD] += dqg
        dsm_ref[1:2, 0:HD] += dkg
        dsm_ref[2:3, :] += dsink

    rev = lambda i: NBLK - 1 - i
    prev = lambda i: jnp.maximum(NBLK - 2 - i, 0)
    small = lambda shape: pl.BlockSpec(shape, lambda i: (0,) * len(shape))
    return pl.pallas_call(
        body, grid=(NBLK,),
        in_specs=[pl.BlockSpec((BLK, D_ATTN), lambda i: (rev(i), 0)),
                  pl.BlockSpec((BLK, 128), lambda i: (rev(i), 4)), pl.BlockSpec((BLK, 128), lambda i: (prev(i), 4)),
                  pl.BlockSpec((BLK, 128), lambda i: (rev(i), 5)), pl.BlockSpec((BLK, 128), lambda i: (prev(i), 5)),
                  pl.BlockSpec((BLK, D_ATTN), lambda i: (rev(i), 0)),
                  small((DEPTH, HD)), small((DEPTH, HD)), small((DEPTH, NQ)), small((NQ, BLK, 2 * BLK))],
        out_specs=[pl.BlockSpec((BLK, 768), lambda i: (rev(i), 0)), small((NQ, BLK, 2 * BLK)), small((8, 128))],
        out_shape=[SDS((S, 768), f32), SDS((NQ, BLK, 2 * BLK), f32), SDS((8, 128), f32)],
        scratch_shapes=[pltpu.VMEM((BLK, 256), f32)], name="attn_bwd", compiler_params=_cparams(1),
    )(qkv, qkv, qkv, qkv, qkv, dmix, q_gain, k_gain, sinks, bias)


CONV_TC = 128


def _shift_down(u, s):
    if s == 0:
        return u
    rows = lax.broadcasted_iota(jnp.int32, u.shape, 0)
    return jnp.where(rows >= s, pltpu.roll(u, s, 0), 0.0)


def _shift_up(u, s):
    if s == 0:
        return u
    rows = lax.broadcasted_iota(jnp.int32, u.shape, 0)
    return jnp.where(rows < u.shape[0] - s, pltpu.roll(u, u.shape[0] - s, 0), 0.0)


def _conv_specs():
    return [pl.BlockSpec((S, CONV_TC), lambda c: (0, c)),
            pl.BlockSpec((None, 4, CONV_TC), lambda c: (0, 0, c)),
            pl.BlockSpec((DEPTH, CONV_TC), lambda c: (0, c))]


def _conv_pre(u, w_ref, b_ref, layer):
    pre = b_ref[layer:layer + 1, :] + w_ref[3:4, :] * u
    for k in range(3):
        pre = pre + w_ref[k:k + 1, :] * _shift_down(u, 3 - k)
    return pre


def _conv_fwd(xbc, conv_w, conv_b, layer):
    def body(u_ref, w_ref, b_ref, o_ref):
        pre = _conv_pre(u_ref[...], w_ref, b_ref, layer)
        o_ref[...] = pre * _sigmoid(pre)

    specs = _conv_specs()
    specs[1] = pl.BlockSpec((None, 4, CONV_TC), lambda c: (layer, 0, c))
    return pl.pallas_call(
        body, grid=(D_CONV // CONV_TC,), in_specs=specs, out_specs=pl.BlockSpec((S, CONV_TC), lambda c: (0, c)),
        out_shape=SDS((S, D_CONV), f32), name="conv_fwd", compiler_params=_cparams(1),
    )(xbc, conv_w, conv_b)


def _conv_bwd(xbc, dact, conv_w, conv_b, layer):
    def body(u_ref, w_ref, b_ref, da_ref, du_ref, dw_ref, db_ref):
        u = u_ref[...]
        pre = _conv_pre(u, w_ref, b_ref, layer)
        sg = _sigmoid(pre)
        dpre = da_ref[...] * (sg * (1.0 + pre * (1.0 - sg)))
        du = w_ref[3:4, :] * dpre
        for k in range(3):
            du = du + w_ref[k:k + 1, :] * _shift_up(dpre, 3 - k)
        du_ref[...] = du
        db_ref[...] = jnp.broadcast_to(jnp.sum(dpre, axis=0, keepdims=True), db_ref.shape)
        dw_ref[...] = jnp.zeros_like(dw_ref)
        for k in range(4):
            dw_ref[k:k + 1, :] = jnp.sum(dpre * _shift_down(u, 3 - k), axis=0, keepdims=True)

    specs = _conv_specs()
    specs[1] = pl.BlockSpec((None, 4, CONV_TC), lambda c: (layer, 0, c))
    col = pl.BlockSpec((S, CONV_TC), lambda c: (0, c))
    row8 = pl.BlockSpec((8, CONV_TC), lambda c: (0, c))
    return pl.pallas_call(
        body, grid=(D_CONV // CONV_TC,), in_specs=[*specs, col], out_specs=[col, row8, row8],
        out_shape=[SDS((S, D_CONV), f32), SDS((8, D_CONV), f32), SDS((8, D_CONV), f32)], name="conv_bwd",
        compiler_params=_cparams(1),
    )(xbc, conv_w, conv_b, dact)


def _tri():
    return (lax.broadcasted_iota(jnp.int32, (BLK, BLK), 0) >= lax.broadcasted_iota(jnp.int32, (BLK, BLK), 1))


def _ssd_scalars(dt_ref, dtb_ref, alog_ref, layer):
    raw = dt_ref[:, 0:NSSM] + dtb_ref[layer:layer + 1, :]
    dtv = jnp.maximum(raw, 0.0) + jnp.log(1.0 + jnp.exp(-jnp.abs(raw)))
    a = -jnp.exp(alog_ref[layer:layer + 1, :])
    acs = jnp.dot(_tri().astype(f32), dtv * a, preferred_element_type=f32, precision=HIGHEST)
    return raw, dtv, a, acs


HG = NSSM // NGRP
GW = HG * HD


def _lane_expand(cols, g):
    lane_head = lax.broadcasted_iota(jnp.int32, (1, GW), 1) // HD
    out = cols[:, HG * g + HG - 1:HG * g + HG]
    for r in range(HG - 2, -1, -1):
        out = jnp.where(lane_head == r, cols[:, HG * g + r:HG * g + r + 1], out)
    return out


def _row_expand(vals, g):
    row_head = lax.broadcasted_iota(jnp.int32, (GW, 1), 0) // HD
    out = vals[:, HG * g + HG - 1:HG * g + HG]
    for r in range(HG - 2, -1, -1):
        out = jnp.where(row_head == r, vals[:, HG * g + r:HG * g + r + 1], out)
    return out


def _head_rowsums(a):
    sel = (lax.broadcasted_iota(jnp.int32, (GW, HG), 0) // HD == lax.broadcasted_iota(jnp.int32, (GW, HG), 1)).astype(bf16)
    hi = a.astype(bf16)
    lo = (a - hi.astype(f32)).astype(bf16)
    sums = _dot(hi, sel, NN_DIMS) + _dot(lo, sel, NN_DIMS)
    return [sums[:, r:r + 1] for r in range(HG)]


def _ssd_chunk_common(xc_ref, dt_ref, dtb_ref, alog_ref, h_rows, layer):
    raw, dtv, a, acs = _ssd_scalars(dt_ref, dtb_ref, alog_ref, layer)
    acs_t = acs.T
    last = acs[BLK - 1:BLK, :]
    c = dict(raw=raw, dtv=dtv, a=a, acs=acs, last=last, dte=jnp.exp(last - acs), e_all=jnp.exp(acs), cd=jnp.exp(last))
    grp, heads, tri = range(NGRP), range(NSSM), _tri()
    c["bm"] = [xc_ref[:, pl.ds(D_SSM + NSTATE * g, NSTATE)] for g in grp]
    c["bm_b"] = [c["bm"][g].astype(bf16) for g in grp]
    c["cm_b"] = [xc_ref[:, pl.ds(D_SSM + NGRP * NSTATE + NSTATE * g, NSTATE)].astype(bf16) for g in grp]
    c["cb"] = [_dot(c["cm_b"][g], c["bm_b"][g], NT_DIMS) for g in grp]
    c["x"] = [xc_ref[:, pl.ds(GW * g, GW)] for g in grp]
    c["dt"] = [_lane_expand(dtv, g) for g in grp]
    c["xdt"] = [c["x"][g] * c["dt"][g] for g in grp]
    c["xdt_b"] = [c["xdt"][g].astype(bf16) for g in grp]
    c["prev"] = [h_rows(g) for g in grp]
    c["prev_b"] = [c["prev"][g].astype(bf16) for g in grp]
    c["e"] = [_lane_expand(c["e_all"], g) for g in grp]
    c["y_off"] = [_dot(c["cm_b"][g], c["prev_b"][g], NT_DIMS) * c["e"][g] for g in grp]
    c["decay"] = [jnp.exp(jnp.where(tri, acs[:, h:h + 1] - acs_t[h:h + 1, :], -jnp.inf)) for h in heads]
    c["m"] = [c["cb"][h // HG] * c["decay"][h] for h in heads]
    c["m_b"] = [c["m"][h].astype(bf16) for h in heads]
    c["dte_x"] = [_lane_expand(c["dte"], g) for g in grp]
    c["xdte_b"] = [(c["xdt"][g] * c["dte_x"][g]).astype(bf16) for g in grp]
    return c


def _ssd_fwd(xact, z, dt, attn, dt_bias, a_log, d_skip, norm_g, layer):
    def body(xc_ref, z_ref, dt_ref, at_ref, dtb_ref, alog_ref, dsk_ref, ng_ref, mix_ref, hs_ref, y_ref, h_ref):
        n = pl.program_id(0)

        @pl.when(n == 0)
        def _():
            h_ref[...] = jnp.zeros_like(h_ref)

        hs_ref[...] = h_ref[...]
        c = _ssd_chunk_common(xc_ref, dt_ref, dtb_ref, alog_ref, lambda g: h_ref[pl.ds(GW * g, GW), :], layer)
        grp, heads = range(NGRP), range(NSSM)
        y_diag = [_dot(c["m_b"][h], c["xdt_b"][h // HG][:, HD * (h % HG):HD * (h % HG + 1)], NN_DIMS) for h in heads]
        new_st = [_dot(c["xdte_b"][g], c["bm_b"][g], TN_DIMS) for g in grp]
        for h in heads:
            y_ref[:, pl.ds(HD * h, HD)] = y_diag[h]
        dskip = dsk_ref[layer:layer + 1, :]
        for g in grp:
            cols = pl.ds(GW * g, GW)
            y_ref[:, cols] = y_ref[:, cols] + c["y_off"][g] + c["x"][g] * _lane_expand(dskip, g)
            h_ref[cols, :] = c["prev"][g] * _row_expand(c["cd"], g) + new_st[g]
        zv = z_ref[...]
        yz = y_ref[...] * (zv * _sigmoid(zv))
        mix_ref[:, 0:D_ATTN] = at_ref[...]
        for g in grp:
            yg = yz[:, GW * g:GW * (g + 1)]
            rs = lax.rsqrt(jnp.mean(yg * yg, axis=-1, keepdims=True) + EPS)
            mix_ref[:, D_ATTN + GW * g:D_ATTN + GW * (g + 1)] = (yg * rs * ng_ref[layer:layer + 1, GW * g:GW * (g + 1)]).astype(bf16)

    small = lambda shape: pl.BlockSpec(shape, lambda n: (0,) * len(shape))
    return pl.pallas_call(
        body, grid=(NBLK,),
        in_specs=[pl.BlockSpec((BLK, D_CONV), lambda n: (n, 0)), pl.BlockSpec((BLK, D_SSM), lambda n: (n, 0)),
                  pl.BlockSpec((BLK, 128), lambda n: (n, 0)), pl.BlockSpec((BLK, D_ATTN), lambda n: (n, 0)),
                  small((DEPTH, NSSM)), small((DEPTH, NSSM)), small((DEPTH, NSSM)), small((DEPTH, D_SSM))],
        out_specs=[pl.BlockSpec((BLK, D), lambda n: (n, 0)), pl.BlockSpec((None, NSSM * HD, NSTATE), lambda n: (n, 0, 0)),
                   pl.BlockSpec((BLK, D_SSM), lambda n: (n, 0))],
        out_shape=[SDS((S, D), bf16), SDS((NBLK, NSSM * HD, NSTATE), f32), SDS((S, D_SSM), f32)],
        scratch_shapes=[pltpu.VMEM((NSSM * HD, NSTATE), f32)],
        name="ssd_fwd", compiler_params=_cparams(1),
    )(xact, z, dt, attn, dt_bias, a_log, d_skip, norm_g)


def _ssd_bwd(xact, z, dt, dmix, hs, y, dt_bias, a_log, d_skip, norm_g, layer):
    def body(xc_ref, z_ref, dt_ref, do_ref, hs_ref, y_ref, dtb_ref, alog_ref, dsk_ref, ng_ref,
             dz_ref, dx_ref, ddt_ref, dsm_ref, dh_ref, dy_ref):
        i = pl.program_id(0)

        @pl.when(i == 0)
        def _():
            dh_ref[...] = jnp.zeros_like(dh_ref)
            dsm_ref[...] = jnp.zeros_like(dsm_ref)

        c = _ssd_chunk_common(xc_ref, dt_ref, dtb_ref, alog_ref, lambda g: hs_ref[pl.ds(GW * g, GW), :], layer)
        raw, dtv, a = c["raw"], c["dtv"], c["a"]
        grp, heads = range(NGRP), range(NSSM)
        dskip = dsk_ref[layer:layer + 1, :]
        lane8 = lax.broadcasted_iota(jnp.int32, (1, NSSM), 1)
        sub8 = lax.broadcasted_iota(jnp.int32, (NSSM, 1), 0)

        zv = z_ref[...]
        sz = _sigmoid(zv)
        gz = zv * sz
        yv = y_ref[...]
        yz = yv * gz
        for g in grp:
            sl = slice(GW * g, GW * (g + 1))
            yg = yz[:, sl]
            rs = lax.rsqrt(jnp.mean(yg * yg, axis=-1, keepdims=True) + EPS)
            yhat = yg * rs
            dog = do_ref[:, sl]
            w = dog * ng_ref[layer:layer + 1, sl]
            dyz = rs * (w - yhat * jnp.mean(yhat * w, axis=-1, keepdims=True))
            dsm_ref[0:1, sl] += jnp.sum(dog * yhat, axis=0, keepdims=True)
            dy_ref[:, sl] = dyz * gz[:, sl]
            dz_ref[:, sl] = dyz * yv[:, sl] * (sz[:, sl] * (1.0 + zv[:, sl] * (1.0 - sz[:, sl])))

        dy = [dy_ref[:, pl.ds(GW * g, GW)] for g in grp]
        dy_b = [dy[g].astype(bf16) for g in grp]
        hl = lambda h: slice(HD * (h % HG), HD * (h % HG + 1))
        dt_off_b = [(dy[g] * c["e"][g]).astype(bf16) for g in grp]
        dcm = [_dot(dt_off_b[g], c["prev_b"][g], NN_DIMS) for g in grp]
        dprev = [_dot(dt_off_b[g], c["cm_b"][g], TN_DIMS) for g in grp]
        yoff_rs = [_head_rowsums(dy[g] * c["y_off"][g]) for g in grp]
        dhn = [dh_ref[pl.ds(GW * g, GW), :] for g in grp]
        dhn_b = [dhn[g].astype(bf16) for g in grp]
        dprev = [dprev[g] + dhn[g] * _row_expand(c["cd"], g) for g in grp]
        dhn_prev = [dhn[g] * c["prev"][g] for g in grp]
        u = [_dot(c["bm_b"][g], dhn_b[g], NT_DIMS) for g in grp]
        dbm = [_dot(c["xdte_b"][g], dhn_b[g], NN_DIMS) for g in grp]
        ddte_rs = [_head_rowsums(c["xdt"][g] * u[g]) for g in grp]
        dm = [_dot(dy_b[h // HG][:, hl(h)], c["xdt_b"][h // HG][:, hl(h)], NT_DIMS) for h in heads]
        dxdt_in = [_dot(c["m_b"][h], dy_b[h // HG][:, hl(h)], TN_DIMS) for h in heads]
        dseg = [dm[h] * c["m"][h] for h in heads]
        dmd = [dm[h] * c["decay"][h] for h in heads]
        for h in heads:
            dx_ref[:, pl.ds(HD * h, HD)] = dxdt_in[h]

        dacs = jnp.zeros((BLK, NSSM), f32)
        dacs_cols = jnp.zeros((NSSM, BLK), f32)
        dlast = jnp.zeros((1, NSSM), f32)
        ddtv = jnp.zeros((BLK, NSSM), f32)
        ddsk = jnp.zeros((1, NSSM), f32)
        for g in grp:
            cols = pl.ds(GW * g, GW)
            dxdt = dx_ref[:, cols] + u[g] * c["dte_x"][g]
            dx_ref[:, cols] = dy[g] * _lane_expand(dskip, g) + dxdt * c["dt"][g]
            ddtv_rs = _head_rowsums(dxdt * c["x"][g])
            ddsk_rs = _head_rowsums(dy[g] * c["x"][g])
            dcb = dmd[HG * g]
            for r in range(1, HG):
                dcb = dcb + dmd[HG * g + r]
            dcb_b = dcb.astype(bf16)
            dx_ref[:, pl.ds(D_SSM + NSTATE * g, NSTATE)] = dbm[g] + _dot(dcb_b, c["cm_b"][g], TN_DIMS)
            dx_ref[:, pl.ds(D_SSM + NGRP * NSTATE + NSTATE * g, NSTATE)] = dcm[g] + _dot(dcb_b, c["bm_b"][g], NN_DIMS)
            dh_ref[cols, :] = dprev[g]
            for r in range(HG):
                h = HG * g + r
                oh = (lane8 == h).astype(f32)
                tmp = ddte_rs[g][r] * c["dte"][:, h:h + 1]
                dacs = dacs + oh * (jnp.sum(dseg[h], axis=1, keepdims=True) + yoff_rs[g][r] - tmp)
                dacs_cols = dacs_cols + (sub8 == h).astype(f32) * jnp.sum(dseg[h], axis=0, keepdims=True)
                dlast = dlast + oh * (_sum11(dhn_prev[g][HD * r:HD * (r + 1), :]) * c["cd"][:, h:h + 1] + _sum11(tmp))
                ddtv = ddtv + oh * ddtv_rs[r]
                ddsk = ddsk + oh * _sum11(ddsk_rs[r])

        row = lax.broadcasted_iota(jnp.int32, (BLK, 1), 0)
        dacs = dacs - dacs_cols.T + jnp.where(row == BLK - 1, dlast, 0.0)
        dda = lax.dot_general(_tri().astype(f32), dacs, TN_DIMS, preferred_element_type=f32, precision=HIGHEST)
        ddtv = ddtv + dda * a
        da = jnp.sum(dda * dtv, axis=0, keepdims=True)
        draw = ddtv * _sigmoid(raw)
        ddt_ref[...] = jnp.zeros_like(ddt_ref)
        ddt_ref[:, 0:NSSM] = draw
        dsm_ref[1:2, 0:NSSM] += jnp.sum(draw, axis=0, keepdims=True)
        dsm_ref[2:3, 0:NSSM] += da * a
        dsm_ref[3:4, 0:NSSM] += ddsk

    rev = lambda i: NBLK - 1 - i
    small = lambda shape: pl.BlockSpec(shape, lambda i: (0,) * len(shape))
    return pl.pallas_call(
        body, grid=(NBLK,),
        in_specs=[pl.BlockSpec((BLK, D_CONV), lambda i: (rev(i), 0)), pl.BlockSpec((BLK, D_SSM), lambda i: (rev(i), 0)),
                  pl.BlockSpec((BLK, 128), lambda i: (rev(i), 0)), pl.BlockSpec((BLK, D_SSM), lambda i: (rev(i), 1)),
                  pl.BlockSpec((None, NSSM * HD, NSTATE), lambda i: (rev(i), 0, 0)), pl.BlockSpec((BLK, D_SSM), lambda i: (rev(i), 0)),
                  small((DEPTH, NSSM)), small((DEPTH, NSSM)), small((DEPTH, NSSM)), small((DEPTH, D_SSM))],
        out_specs=[pl.BlockSpec((BLK, D_SSM), lambda i: (rev(i), 0)), pl.BlockSpec((BLK, D_CONV), lambda i: (rev(i), 0)),
                   pl.BlockSpec((BLK, 128), lambda i: (rev(i), 0)), small((8, D_SSM))],
        out_shape=[SDS((S, D_SSM), f32), SDS((S, D_CONV), f32), SDS((S, 128), f32), SDS((8, D_SSM), f32)],
        scratch_shapes=[pltpu.VMEM((NSSM * HD, NSTATE), f32), pltpu.VMEM((BLK, D_SSM), f32)],
        name="ssd_bwd", compiler_params=_cparams(1),
    )(xact, z, dt, dmix, hs, y, dt_bias, a_log, d_skip, norm_g)


def _my_place():
    return lax.axis_index("x"), lax.axis_index("y"), lax.axis_index("c")


def _dev_index(px, py, pc):
    return 4 * px + 2 * py + pc


def _slab2(kind, ref, idx):
    if kind == "stack":
        return ref.at[idx]
    if kind == "rows128":
        return ref.at[pl.ds(pl.multiple_of(idx * 128, 128), 128), :]
    if kind == "rows512":
        return ref.at[pl.ds(pl.multiple_of(idx * 512, 512), 512), :]
    return ref.at[:, pl.ds(pl.multiple_of(idx * 512, 512), 512)]


def _slab_shape(kind, full_shape):
    if kind == "stack":
        return tuple(full_shape[1:])
    if kind == "rows128":
        return (128, full_shape[1])
    if kind == "rows512":
        return (512, full_shape[1])
    return (full_shape[0], 512)


KIND = dict(w_in="stack", w_out="rows128", w_up="cols512", w_down="rows512", conv_w="stack")
FULL_SHAPE = dict(w_in=(N_DEV, D, D_IN // N_DEV), w_out=(D, D), w_up=(D, D_FF), w_down=(D_FF, D))
HBM_SPEC = pl.BlockSpec(memory_space=pltpu.HBM)
SEM_SPEC = pl.BlockSpec(memory_space=pltpu.SEMAPHORE)
SIDE_EFFECT = pltpu.SideEffectType.DATAFLOW_SIDE_EFFECTING


def _peers_all():
    x, y, c = _my_place()
    return [(x ^ ((r >> 2) & 1), y ^ ((r >> 1) & 1), c ^ (r & 1)) for r in range(1, N_DEV)]


def _split_start(name, bufs, n_copies, plan, deps=()):
    nb = len(bufs)

    def body(*refs):
        ins = refs[:nb]
        send_sems, recv_sems = refs[nb + len(deps)], refs[nb + len(deps) + 1]
        token = refs[-1]
        for i, (src, dst, dev) in enumerate(plan(ins)):
            pltpu.make_async_remote_copy(src_ref=src, dst_ref=dst, send_sem=send_sems.at[i], recv_sem=recv_sems.at[i],
                                         device_id=dev, device_id_type=MESH).start()
        token[...] = jnp.zeros_like(token)

    outs = pl.pallas_call(
        body, name=name,
        out_shape=(pltpu.SemaphoreType.DMA((n_copies,)), pltpu.SemaphoreType.DMA((n_copies,)),
                   *[pltpu.HBM(b.shape, b.dtype) for b in bufs], SDS((8, 128), f32)),
        in_specs=[HBM_SPEC] * nb + [ANY_SPEC] * len(deps),
        out_specs=(SEM_SPEC, SEM_SPEC, *[HBM_SPEC] * nb, pl.BlockSpec(memory_space=pltpu.VMEM)),
        input_output_aliases={i: 2 + i for i in range(nb)},
        compiler_params=pltpu.CompilerParams(has_side_effects=SIDE_EFFECT),
    )(*[pltpu.with_memory_space_constraint(b, pltpu.HBM) for b in bufs], *deps)
    return dict(send=outs[0], recv=outs[1], bufs=list(outs[2:2 + nb]), token=outs[-1], plan=plan, n=n_copies)


def _split_wait(name, started, after):
    bufs = started["bufs"]
    nb = len(bufs)
    plan = started["plan"]

    def body(*refs):
        ins = refs[:nb]
        send_sems, recv_sems = refs[nb], refs[nb + 1]
        for i, (src, dst, dev) in enumerate(plan(ins)):
            cp = pltpu.make_async_remote_copy(src_ref=src, dst_ref=dst, send_sem=send_sems.at[i], recv_sem=recv_sems.at[i],
                                              device_id=dev, device_id_type=MESH)
            cp.wait_send()
            cp.wait_recv()

    outs = pl.pallas_call(
        body, name=name, out_shape=tuple(pltpu.HBM(b.shape, b.dtype) for b in bufs),
        in_specs=[HBM_SPEC] * nb + [SEM_SPEC, SEM_SPEC] + [ANY_SPEC] * len(after), out_specs=(HBM_SPEC,) * nb,
        input_output_aliases={i: i for i in range(nb)},
        compiler_params=pltpu.CompilerParams(has_side_effects=SIDE_EFFECT),
    )(*bufs, started["send"], started["recv"], *after)
    return list(outs)


def _gather_start(name, names, fulls, deps):
    n_t = len(names)

    def plan(refs):
        x, y, c = _my_place()
        my_idx = _dev_index(x, y, c)
        targets = [(x, y, 1 - c), (1 - x, y, c), (x, 1 - y, c), (1 - x, 1 - y, c)]
        slabs = [_slab2(KIND[names[t]], refs[t], my_idx) for t in range(n_t)]
        return [(slabs[t], slabs[t], dev) for t in range(n_t) for dev in targets]

    return _split_start(name, list(fulls), 4 * n_t, plan, deps)


def _gather_finish(name, names, started, after):
    n_t = len(names)
    fulls = _split_wait(name + "_wait", started, after)
    slab_shapes = [SDS(_slab_shape(KIND[n], f.shape), f.dtype) for n, f in zip(names, fulls)]

    def body(*refs):
        ins = refs[:n_t]
        outs = refs[n_t:2 * n_t]
        stage = refs[2 * n_t:3 * n_t]
        load_sems, send_sems, recv_sems = refs[3 * n_t:]
        x, y, c = _my_place()
        chips = [(1 - x, y), (x, 1 - y), (1 - x, 1 - y)]
        pairs = [(t, j) for t in range(n_t) for j in range(3)]
        loads = [pltpu.make_async_copy(_slab2(KIND[names[t]], ins[t], _dev_index(*chips[j], c)), stage[t].at[j], load_sems.at[t, j])
                 for t, j in pairs]
        for cp in loads:
            cp.start()

        def copy(t, j, core):
            return pltpu.make_async_remote_copy(
                src_ref=stage[t].at[j], dst_ref=_slab2(KIND[names[t]], outs[t], _dev_index(*chips[j], core)),
                send_sem=send_sems.at[t, j], recv_sem=recv_sems.at[t, j], device_id=(x, y, 1 - c), device_id_type=MESH)

        sends = [copy(t, j, c) for t, j in pairs]
        for ld, cp in zip(loads, sends):
            ld.wait()
            cp.start()
        for t, j in pairs:
            copy(t, j, 1 - c).wait_recv()
        for cp in sends:
            cp.wait_send()

    return pl.pallas_call(
        body, in_specs=[ANY_SPEC] * n_t, out_specs=[ANY_SPEC] * n_t, out_shape=[SDS(b.shape, b.dtype) for b in fulls],
        input_output_aliases={t: t for t in range(n_t)},
        scratch_shapes=[pltpu.VMEM((3,) + s.shape, s.dtype) for s in slab_shapes]
        + [pltpu.SemaphoreType.DMA((n_t, 3)), pltpu.SemaphoreType.DMA((n_t, 3)), pltpu.SemaphoreType.DMA((n_t, 3))],
        name=name + "_pass", compiler_params=pltpu.CompilerParams(vmem_limit_bytes=VMEM_LIMIT),
    )(*fulls)


def _exchange_start(name, names, grads, deps):
    n_t = len(names)
    lands = [lax.empty((N_DEV,) + _slab_shape(KIND[n], g.shape), g.dtype) for n, g in zip(names, grads)]

    def plan(refs):
        my_idx = _dev_index(*_my_place())
        return [(_slab2(KIND[names[t]], refs[t], _dev_index(*peer)), refs[n_t + t].at[my_idx], peer)
                for t in range(n_t) for peer in _peers_all()]

    return _split_start(name, list(grads) + lands, 7 * n_t, plan, deps)


def _small_exchange_start(part, deps):
    land = lax.empty((N_DEV,) + part.shape, part.dtype)

    def plan(refs):
        my_idx = _dev_index(*_my_place())
        return [(refs[0], refs[1].at[my_idx], peer) for peer in _peers_all()]

    return _split_start("small_exchange", [part, land], N_DEV - 1, plan, deps)


def _w_in_assemble(stacked):
    tr = 256
    sh = D_IN // N_DEV

    def body(i_ref, o_ref):
        for j in range(N_DEV):
            o_ref[:, sh * j:sh * (j + 1)] = i_ref[j]
        o_ref[:, D_IN:D_IN_PAD] = jnp.zeros((tr, D_IN_PAD - D_IN), bf16)

    return pl.pallas_call(
        body, grid=(D // tr,), in_specs=[pl.BlockSpec((N_DEV, tr, sh), lambda i: (0, i, 0))],
        out_specs=pl.BlockSpec((None, tr, D_IN_PAD), lambda i: (0, i, 0)), out_shape=SDS((1, D, D_IN_PAD), bf16),
        name="w_in_assemble", compiler_params=_cparams(1),
    )(stacked)


def _w_in_slabs(dw_in):
    tr = 256
    sh = D_IN // N_DEV

    def body(i_ref, o_ref):
        for j in range(N_DEV):
            o_ref[j] = i_ref[:, sh * j:sh * (j + 1)]

    return pl.pallas_call(
        body, grid=(D // tr,), in_specs=[pl.BlockSpec((tr, D_IN_PAD), lambda i: (i, 0))],
        out_specs=pl.BlockSpec((N_DEV, tr, sh), lambda i: (0, i, 0)), out_shape=SDS((N_DEV, D, sh), bf16),
        name="w_in_slabs", compiler_params=_cparams(1),
    )(dw_in)


SMALL_NAMES = ("mix_norm_g", "mlp_norm_g", "conv_b", "ssm_norm_g", "q_gain", "k_gain", "sinks", "dt_bias", "a_log", "d_skip",
               "rel_bias", "conv_w")
MISC_LANES = dict(q_gain=(LANE_QG, HD), k_gain=(LANE_KG, HD), sinks=(LANE_SINK, NQ), dt_bias=(LANE_DTB, NSSM),
                  a_log=(LANE_ALOG, NSSM), d_skip=(LANE_DSKIP, NSSM))


def _pack_small_grads(smalls, drel_t, loss):
    def body(*refs):
        o_ref = refs[-1]
        drel_ref, loss_ref = refs[-3], refs[-2]
        o_ref[...] = jnp.zeros_like(o_ref)
        for l in range(DEPTH):
            mixg, mlpg, convb, convw, ssd, attn = refs[6 * l:6 * l + 6]
            o_ref[ROW_MIXG + l:ROW_MIXG + l + 1, :] = mixg[...]
            o_ref[ROW_MLPG + l:ROW_MLPG + l + 1, :] = mlpg[...]
            o_ref[ROW_CONVB + l:ROW_CONVB + l + 1, :] = convb[0:1, :]
            o_ref[ROW_SSMG + l:ROW_SSMG + l + 1, 0:D_SSM] = ssd[0:1, :]
            o_ref[ROW_CONVW + 4 * l:ROW_CONVW + 4 * l + 4, :] = convw[0:4, :]
            row = slice(ROW_MISC + l, ROW_MISC + l + 1)
            o_ref[row, LANE_QG:LANE_QG + HD] = attn[0:1, 0:HD]
            o_ref[row, LANE_KG:LANE_KG + HD] = attn[1:2, 0:HD]
            o_ref[row, LANE_SINK:LANE_SINK + NQ] = attn[2:3, 0:NQ]
            o_ref[row, LANE_DTB:LANE_DTB + NSSM] = ssd[1:2, 0:NSSM]
            o_ref[row, LANE_ALOG:LANE_ALOG + NSSM] = ssd[2:3, 0:NSSM]
            o_ref[row, LANE_DSKIP:LANE_DSKIP + NSSM] = ssd[3:4, 0:NSSM]
        o_ref[ROW_RELB:ROW_RELB + NQ, 0:N_BUCKETS] = drel_ref[...]
        o_ref[ROW_LOSS:ROW_LOSS + 1, 0:1] = loss_ref[0:1, 0:1]

    args = []
    for sm in smalls:
        args += [sm["mix_norm_g"], sm["mlp_norm_g"], sm["conv_b"], sm["conv_w"], sm["ssd"], sm["attn"]]
    args += [drel_t, loss]
    return pl.pallas_call(body, out_shape=SDS((SMALL_ROWS, D), f32), name="pack_small_grads")(*args)


def _adamw_small(part, land, w, m, v):
    n = len(SMALL_NAMES)

    def grad_of(name, g_ref):
        if name == "mix_norm_g":
            return g_ref[ROW_MIXG:ROW_MIXG + DEPTH, :]
        if name == "mlp_norm_g":
            return g_ref[ROW_MLPG:ROW_MLPG + DEPTH, :]
        if name == "conv_b":
            return g_ref[ROW_CONVB:ROW_CONVB + DEPTH, :]
        if name == "ssm_norm_g":
            return g_ref[ROW_SSMG:ROW_SSMG + DEPTH, 0:D_SSM]
        if name == "rel_bias":
            return g_ref[ROW_RELB:ROW_RELB + NQ, 0:N_BUCKETS].T
        lane, width = MISC_LANES[name]
        return g_ref[ROW_MISC:ROW_MISC + DEPTH, lane:lane + width]

    def body(part_ref, land_ref, *refs):
        ws, ms, vs = refs[:n], refs[n:2 * n], refs[2 * n:3 * n]
        loss_ref = refs[3 * n]
        outs = refs[3 * n + 1:-1]
        g_ref = refs[-1]
        me = _dev_index(*_my_place())
        for p in range(N_DEV):
            term = jnp.where(me == p, part_ref[...], land_ref[p])
            if p == 0:
                g_ref[...] = term
            else:
                g_ref[...] += term
        loss_ref[...] = g_ref[ROW_LOSS:ROW_LOSS + 1, 0:128]
        my_cols = pl.ds(pl.multiple_of(me * 128, 128), 128)
        for k, name in enumerate(SMALL_NAMES):
            g_out, d_out, m_out, v_out = outs[4 * k:4 * k + 4]
            if name == "conv_w":
                for l in range(DEPTH):
                    g = g_ref[ROW_CONVW + 4 * l:ROW_CONVW + 4 * l + 4, my_cols]
                    delta, m_new, v_new = _adamw_math(ws[k][l], ms[k][l], vs[k][l], g)
                    g_out[l], d_out[l], m_out[l], v_out[l] = g, delta, m_new, v_new
            else:
                g = grad_of(name, g_ref)
                delta, m_new, v_new = _adamw_math(ws[k][...], ms[k][...], vs[k][...], g)
                g_out[...], d_out[...], m_out[...], v_out[...] = g, delta, m_new, v_new

    ws = [w[name] for name in SMALL_NAMES]
    out_shape = [SDS((1, 128), f32)]
    for a in ws:
        out_shape += [SDS(a.shape, f32)] * 4
    return pl.pallas_call(body, out_shape=out_shape, name="adamw_small", scratch_shapes=[pltpu.VMEM((SMALL_ROWS, D), f32)])(
        part, land, *ws, *[m[name] for name in SMALL_NAMES], *[v[name] for name in SMALL_NAMES])


def _plain(tm, tn):
    return pl.BlockSpec((tm, tn), lambda i, j, k: (i, j))


def _rowblk(tm, width):
    return pl.BlockSpec((tm, width), lambda i, j, k: (i, 0))


def _store_epi(dtype):
    def epi(acc, i, j, ex, outs):
        outs[0][...] = acc.astype(dtype)
    return epi


def _rms_prologue(layer):
    def pro(a_ref, ex, outs):
        xv = a_ref[...]
        r = lax.rsqrt(jnp.mean(xv * xv, axis=-1, keepdims=True) + EPS)
        h = (xv * r * ex[0][layer:layer + 1, :]).astype(bf16)
        outs[-1][...] = h
        return h
    return pro


def _layer_fwd(l, x, p, get_weights, bias, tgt=None):
    wts = get_weights(l, "in", [x, bias])
    gfull = pl.BlockSpec((DEPTH, D), lambda i, j, k: (0, 0))
    tm = 256

    def inproj_epi(acc, i, j, ex, outs):
        outs[0][...] = acc[:, 0:768]
        outs[1][...] = acc[:, 768:1280]
        outs[2][...] = acc[:, 1280:2304]
        outs[3][...] = acc[:, 2304:2432]

    qkv, z, xbc, dt, h1 = _matmul(
        "in_proj", "nn", x, wts["w_in"], tm=tm, tn=D_IN_PAD, tk=D, prologue=_rms_prologue(l),
        extras=(p["mix_norm_g"],), extra_specs=(gfull,),
        out_shape=[SDS((S, 768), f32), SDS((S, 512), f32), SDS((S, 1024), f32), SDS((S, 128), f32), SDS((S, D), bf16)],
        out_specs=[_rowblk(tm, 768), _rowblk(tm, 512), _rowblk(tm, 1024), _rowblk(tm, 128), _rowblk(tm, D)], epilogue=inproj_epi)
    attn = _attn_fwd(qkv, p["q_gain"], p["k_gain"], p["sinks"], bias, l)
    xact = _conv_fwd(xbc, wts["conv_w"], p["conv_b"], l)
    mix, hs, y_ssd = _ssd_fwd(xact, z, dt, attn, p["dt_bias"], p["a_log"], p["d_skip"], p["ssm_norm_g"], l)
    wts = dict(wts, **get_weights(l, "rest", [mix]))

    def resid_epi(acc, i, j, ex, outs):
        outs[0][...] = ex[0][...] + acc

    x_mid = _matmul("out_proj", "nn", mix, wts["w_out"], tm=tm, tn=D, tk=D, out_shape=SDS((S, D), f32),
                    out_specs=_plain(tm, D), epilogue=resid_epi, extras=(x,), extra_specs=(_plain(tm, D),))

    def up_epi(acc, i, j, ex, outs):
        r = jnp.maximum(acc, 0.0)
        outs[0][...] = (r * r).astype(bf16)

    a_act, h2 = _matmul("mlp_up", "nn", x_mid, wts["w_up"], tm=tm, tn=D_FF, tk=D, prologue=_rms_prologue(l),
                        extras=(p["mlp_norm_g"],), extra_specs=(gfull,),
                        out_shape=[SDS((S, D_FF), bf16), SDS((S, D), bf16)], out_specs=[_plain(tm, D_FF), _rowblk(tm, D)],
                        epilogue=up_epi)
    saved = dict(x=x, h1=h1, qkv=qkv, z=z, xbc=xbc, dt=dt, xact=xact, mix=mix, hs=hs, y_ssd=y_ssd, x_mid=x_mid, h2=h2,
                 a=a_act, wts=wts)
    if tgt is None:
        x_out = _matmul("mlp_down", "nn", a_act, wts["w_down"], tm=tm, tn=D, tk=D_FF, out_shape=SDS((S, D), f32),
                        out_specs=_plain(tm, D), epilogue=resid_epi, extras=(x_mid,), extra_specs=(_plain(tm, D),))
        return x_out, saved

    def loss_epi(acc, i, j, ex, outs):
        err = ex[0][...] + acc - ex[1][...]
        outs[0][...] = err * (1.0 / D)
        part = 0.5 * jnp.sum(jnp.mean(err * err, axis=-1, keepdims=True), axis=0, keepdims=True)

        @pl.when(i == 0)
        def _():
            outs[1][...] = jnp.zeros_like(outs[1])

        outs[1][...] += jnp.broadcast_to(part, outs[1].shape)

    head = _matmul("mlp_down_loss", "nn", a_act, wts["w_down"], tm=tm, tn=D, tk=D_FF, out_shape=[SDS((S, D), f32), SDS((1, 128), f32)],
                   out_specs=[_plain(tm, D), pl.BlockSpec((1, 128), lambda i, j, k: (0, 0))], epilogue=loss_epi,
                   extras=(x_mid, tgt), extra_specs=(_plain(tm, D), _plain(tm, D)))
    return head, saved


def _layer_bwd(l, dx_out, sv, p, bias, deps, send):
    wts = sv["wts"]

    def du_epi(acc, i, j, ex, outs):
        outs[0][...] = (acc * (2.0 * jnp.sqrt(ex[0][...].astype(f32)))).astype(bf16)

    du = _matmul("mlp_da", "nt", dx_out, wts["w_down"], tm=256, tn=D_FF, tk=D, out_shape=SDS((S, D_FF), bf16),
                 out_specs=_plain(256, D_FF), epilogue=du_epi, extras=(sv["a"],), extra_specs=(_plain(256, D_FF),), deps=deps)
    dw_down = _matmul("dw_down", "tn", sv["a"], dx_out, tm=1024, tn=D, tk=S, out_shape=SDS((D_FF, D), bf16),
                      out_specs=_plain(1024, D), epilogue=_store_epi(bf16))
    dw_up = _matmul("dw_up", "tn", sv["h2"], du, tm=D, tn=1024, tk=S, out_shape=SDS((D, D_FF), bf16),
                    out_specs=_plain(D, 1024), epilogue=_store_epi(bf16))
    deps = send(l, dict(w_down=dw_down, w_up=dw_up))
    gfull = pl.BlockSpec((DEPTH, D), lambda i, j, k: (0, 0))
    grow = pl.BlockSpec((1, D), lambda i, j, k: (0, 0))
    dx_mid, dg_mlp = _matmul(
        "mlp_dh", "nt", du, wts["w_up"], tm=256, tn=D, tk=D_FF, out_shape=[SDS((S, D), f32), SDS((1, D), f32)],
        out_specs=[_plain(256, D), grow], epilogue=_rms_bwd_epilogue(l),
        extras=(sv["x_mid"], p["mlp_norm_g"], dx_out), extra_specs=(_plain(256, D), gfull, _plain(256, D)), deps=deps)
    dmix = _matmul("out_proj_da", "nt", dx_mid, wts["w_out"], tm=256, tn=D, tk=D, out_shape=SDS((S, D), f32),
                   out_specs=_plain(256, D), epilogue=_store_epi(f32))
    dw_out = _matmul("dw_out", "tn", sv["mix"], dx_mid, tm=D, tn=512, tk=512, out_shape=SDS((D, D), bf16),
                     out_specs=_plain(D, 512), epilogue=_store_epi(bf16))
    dz, dxact, ddt, dsm_ssd = _ssd_bwd(sv["xact"], sv["z"], sv["dt"], dmix, sv["hs"], sv["y_ssd"], p["dt_bias"], p["a_log"],
                                       p["d_skip"], p["ssm_norm_g"], l)
    dxbc, dconv_w, dconv_b = _conv_bwd(sv["xbc"], dxact, wts["conv_w"], p["conv_b"], l)
    dqkv, dbias, dsm_attn = _attn_bwd(sv["qkv"], dmix, p["q_gain"], p["k_gain"], p["sinks"], bias, l)
    dproj = _pack_dproj(dqkv, dz, dxbc, ddt)
    dw_in = _matmul("dw_in", "tn", sv["h1"], dproj, tm=D, tn=640, tk=S, out_shape=SDS((D, D_IN_PAD), bf16),
                    out_specs=_plain(D, 640), epilogue=_store_epi(bf16))
    deps = send(l, dict(w_out=dw_out, w_in=_w_in_slabs(dw_in)))
    dx, dg_mix = _matmul(
        "in_proj_dh", "nt", dproj, wts["w_in"], tm=256, tn=D, tk=D_IN_PAD, out_shape=[SDS((S, D), f32), SDS((1, D), f32)],
        out_specs=[_plain(256, D), grow], epilogue=_rms_bwd_epilogue(l),
        extras=(sv["x"], p["mix_norm_g"], dx_mid), extra_specs=(_plain(256, D), gfull, _plain(256, D)), deps=deps)
    small = dict(mix_norm_g=dg_mix, mlp_norm_g=dg_mlp, conv_w=dconv_w, conv_b=dconv_b, ssd=dsm_ssd, attn=dsm_attn, dbias=dbias)
    return dx, small, deps


def _local_step(x, tgt, p, get_weights, send):
    onehot_t = jnp.asarray(_onehot_buckets())
    bias = _bias_build(p["rel_bias"].T, onehot_t).reshape(NQ, BLK, 2 * BLK)
    saved = []
    h = x
    for l in range(DEPTH):
        h, sv = _layer_fwd(l, h, p, get_weights, bias, tgt if l == DEPTH - 1 else None)
        saved.append(sv)
    dx, loss = h
    smalls = [None] * DEPTH
    deps = ()
    for l in reversed(range(DEPTH)):
        dx, smalls[l], deps = _layer_bwd(l, dx, saved[l], p, bias, deps, send)
    drel_t = _bias_grad(smalls[0]["dbias"].reshape(NQ, -1), smalls[1]["dbias"].reshape(NQ, -1), onehot_t)
    return dx, _pack_small_grads(smalls, drel_t, loss)


WEIGHT_ORDER = ("mix_norm_g", "w_in", "q_gain", "k_gain", "sinks", "rel_bias", "conv_w", "conv_b", "dt_bias", "a_log", "d_skip",
                "ssm_norm_g", "w_out", "mlp_norm_g", "w_up", "w_down")


def kernel(x, mix_norm_g, w_in, q_gain, k_gain, sinks, rel_bias, conv_w, conv_b, dt_bias, a_log, d_skip, ssm_norm_g, w_out, mlp_norm_g, w_up, w_down, loss_target, m_mix_norm_g, m_w_in, m_q_gain, m_k_gain, m_sinks, m_rel_bias, m_conv_w, m_conv_b, m_dt_bias, m_a_log, m_d_skip, m_ssm_norm_g, m_w_out, m_mlp_norm_g, m_w_up, m_w_down, v_mix_norm_g, v_w_in, v_q_gain, v_k_gain, v_sinks, v_rel_bias, v_conv_w, v_conv_b, v_dt_bias, v_a_log, v_d_skip, v_ssm_norm_g, v_w_out, v_mlp_norm_g, v_w_up, v_w_down):
    w = dict(mix_norm_g=mix_norm_g, w_in=w_in, q_gain=q_gain, k_gain=k_gain, sinks=sinks, rel_bias=rel_bias, conv_w=conv_w,
             conv_b=conv_b, dt_bias=dt_bias, a_log=a_log, d_skip=d_skip, ssm_norm_g=ssm_norm_g, w_out=w_out,
             mlp_norm_g=mlp_norm_g, w_up=w_up, w_down=w_down)
    m = dict(mix_norm_g=m_mix_norm_g, w_in=m_w_in, q_gain=m_q_gain, k_gain=m_k_gain, sinks=m_sinks, rel_bias=m_rel_bias,
             conv_w=m_conv_w, conv_b=m_conv_b, dt_bias=m_dt_bias, a_log=m_a_log, d_skip=m_d_skip, ssm_norm_g=m_ssm_norm_g,
             w_out=m_w_out, mlp_norm_g=m_mlp_norm_g, w_up=m_w_up, w_down=m_w_down)
    v = dict(mix_norm_g=v_mix_norm_g, w_in=v_w_in, q_gain=v_q_gain, k_gain=v_k_gain, sinks=v_sinks, rel_bias=v_rel_bias,
             conv_w=v_conv_w, conv_b=v_conv_b, dt_bias=v_dt_bias, a_log=v_a_log, d_skip=v_d_skip, ssm_norm_g=v_ssm_norm_g,
             w_out=v_w_out, mlp_norm_g=v_mlp_norm_g, w_up=v_w_up, w_down=v_w_down)
    big = ("w_in", "w_out", "w_up", "w_down")

    my_idx = _dev_index(*_my_place()).astype(jnp.int32).reshape(1)

    fulls = {n: _cast_to_full("cast_" + n, w[n], KIND[n], FULL_SHAPE[n], my_idx, bf16) for n in big}
    conv_full = _cast_to_full("cast_conv_w", conv_w.reshape(1, DEPTH * 4, 128), "stack", (N_DEV, DEPTH * 4, 128), my_idx, f32)[0]
    rest = ["w_out", "w_up", "w_down"]
    g0 = _gather_start("gather0", ["w_in", "conv_w"], [fulls["w_in"][0], conv_full], ())
    g1 = _gather_start("gather1", rest, [fulls[n][0] for n in rest], (g0["token"],))
    g2 = _gather_start("gather2", ["w_in"], [fulls["w_in"][1]], (g1["token"],))
    g3 = _gather_start("gather3", rest, [fulls[n][1] for n in rest], (g2["token"],))
    held = {}

    def get_weights(l, part, after):
        if l == 0 and part == "in":
            full_in, full_conv = _gather_finish("gather0", ["w_in", "conv_w"], g0, list(after) + [g3["token"]])
            held["conv_w"] = jnp.transpose(full_conv.reshape(N_DEV, DEPTH, 4, 128), (1, 2, 0, 3)).reshape(DEPTH, 4, D_CONV)
            return dict(w_in=_w_in_assemble(full_in), conv_w=held["conv_w"])
        if part == "in":
            return dict(w_in=_w_in_assemble(_gather_finish("gather2", ["w_in"], g2, after)[0]), conv_w=held["conv_w"])
        full = _gather_finish("gather1" if l == 0 else "gather3", rest, g1 if l == 0 else g3, after)
        return {n: f[None] for n, f in zip(rest, full)}

    pending = []

    def send(l, grads):
        names = list(grads)
        started = _exchange_start("exchange%d_%s" % (l, names[0]), names, [grads[n] for n in names], ())
        pending.append((l, names, started))
        return (started["token"],)

    dx, small_part = _local_step(x.reshape(S, D), loss_target.reshape(S, D), w, get_weights, send)

    small = _small_exchange_start(small_part, ())
    tiles = dict(w_in=256, w_out=128, w_up=256, w_down=256)
    flat = lambda a: a.reshape(a.shape[0] * a.shape[1], a.shape[2])
    outs_of = {n: None for n in big}
    after = [dx, small["token"]]
    for l, names, started in pending:
        bufs = _split_wait("exchange%d_%s_wait" % (l, names[0]), started, after)
        for t, n in enumerate(names):
            outs_of[n] = _adamw_layer("adamw_%s%d" % (n, l), KIND[n], l, flat(w[n]), flat(m[n]), flat(v[n]),
                                      bufs[len(names) + t], bufs[t], my_idx, outs_of[n], tiles[n])
        after = [outs_of[names[-1]][0]]
    res = {n: [o.reshape(w[n].shape) for o in outs_of[n]] for n in big}
    small_part, small_land = _split_wait("small_exchange_wait", small, after)
    small_outs = _adamw_small(small_part, small_land, w, m, v)
    loss = small_outs[0][0, 0]
    for k, name in enumerate(SMALL_NAMES):
        res[name] = small_outs[1 + 4 * k:5 + 4 * k]

    result = [loss, dx.reshape(1, S, D)]
    for k in range(4):
        result += [res[name][k] for name in WEIGHT_ORDER]
    return tuple(result)
```

```python
import functools
import math

import numpy as np
import jax
import jax.numpy as jnp
from jax import lax
from jax.experimental import pallas as pl
from jax.experimental.pallas import tpu as pltpu

f32 = jnp.float32
bf16 = jnp.bfloat16
SDS = jax.ShapeDtypeStruct
MESH = pl.DeviceIdType.MESH
HIGHEST = lax.Precision.HIGHEST

S = 2048
D = 1024
DEPTH = 2
BLK = 128
NBLK = S // BLK
HD = 64
NQ = 8
NKV = 2
NSSM = 8
NGRP = 2
NSTATE = 128
D_ATTN = 512
D_SSM = 512
D_CONV = 1024
D_FF = 4096
D_IN = 2312
D_IN_PAD = 2560
COL_QKV, COL_Z, COL_DT, COL_XBC = 0, 768, 1280, 1536
IN_SEGMENTS = ((0, 1280, 0), (1280, 2304, COL_XBC), (2304, 2312, COL_DT))
N_BUCKETS = 32
EPS = 1e-6
N_DEV = 8
VMEM_LIMIT = 48 * 1024 * 1024

ADAM_LR = 0.001
ADAM_B1 = 0.9
ADAM_B2 = 0.999
ADAM_EPS = 1e-08
ADAM_WD = 0.01
ADAM_STEP = 10

NT_DIMS = (((1,), (1,)), ((), ()))
TN_DIMS = (((0,), (0,)), ((), ()))
NN_DIMS = (((1,), (0,)), ((), ()))

ROW_MIXG = 0
ROW_MLPG = 2
ROW_CONVB = 4
ROW_SSMG = 6
ROW_MISC = 8
ROW_RELB = 10
ROW_CONVW = 18
ROW_LOSS = 26
SMALL_ROWS = 32
LANE_QG, LANE_KG, LANE_SINK, LANE_DTB, LANE_ALOG, LANE_DSKIP = 0, 64, 128, 256, 384, 512


def _dot(a, b, dims):
    return lax.dot_general(a, b, dims, preferred_element_type=f32)


def _cparams(n_axes):
    return pltpu.CompilerParams(dimension_semantics=("arbitrary",) * n_axes, vmem_limit_bytes=VMEM_LIMIT)


def _sum11(v):
    return jnp.sum(jnp.sum(v, axis=1, keepdims=True), axis=0, keepdims=True)


def _sigmoid(v):
    return 1.0 / (1.0 + jnp.exp(-v))


ANY_SPEC = pl.BlockSpec(memory_space=pl.ANY)


def _matmul(name, mode, a, b, *, layer=0, tm, tn, tk, out_shape, out_specs, epilogue, extras=(), extra_specs=(), deps=(),
            prologue=None):
    extras = tuple(extras) + tuple(deps)
    extra_specs = tuple(extra_specs) + (ANY_SPEC,) * len(deps)
    if mode == "tn":
        t_dim, m_dim = a.shape
        n_dim = b.shape[1]
        grid = (m_dim // tm, n_dim // tn, t_dim // tk)
        a_spec = pl.BlockSpec((tk, tm), lambda i, j, k: (k, i))
        b_spec = pl.BlockSpec((tk, tn), lambda i, j, k: (k, j))
        dims = TN_DIMS
    elif mode == "nn":
        m_dim, k_dim = a.shape
        n_dim = b.shape[-1]
        grid = (m_dim // tm, n_dim // tn, k_dim // tk)
        a_spec = pl.BlockSpec((tm, tk), lambda i, j, k: (i, k))
        b_spec = pl.BlockSpec((None, tk, tn), lambda i, j, k: (layer, k, j))
        dims = NN_DIMS
    else:
        m_dim, k_dim = a.shape
        n_dim = b.shape[-2]
        grid = (m_dim // tm, n_dim // tn, k_dim // tk)
        a_spec = pl.BlockSpec((tm, tk), lambda i, j, k: (i, k))
        b_spec = pl.BlockSpec((None, tn, tk), lambda i, j, k: (layer, j, k))
        dims = NT_DIMS
    nk = grid[2]
    n_ex = len(extras)

    def body(a_ref, b_ref, *rest):
        ex = rest[:n_ex - len(deps)]
        outs = rest[n_ex:-1]
        acc = rest[-1]
        i = pl.program_id(0)
        j = pl.program_id(1)
        k = pl.program_id(2)
        lhs = a_ref[...].astype(bf16) if prologue is None else prologue(a_ref, ex, outs)
        part = _dot(lhs, b_ref[...].astype(bf16), dims)
        if nk == 1:
            epilogue(part, i, j, ex, outs)
        else:
            @pl.when(k == 0)
            def _():
                acc[...] = part

            @pl.when(k > 0)
            def _():
                acc[...] += part

            @pl.when(k == nk - 1)
            def _():
                epilogue(acc[...], i, j, ex, outs)

    return pl.pallas_call(
        body, grid=grid, in_specs=[a_spec, b_spec, *extra_specs], out_specs=out_specs, out_shape=out_shape,
        scratch_shapes=[pltpu.VMEM((tm, tn) if nk > 1 else (8, 128), f32)], name=name, compiler_params=_cparams(3),
    )(a, b, *extras)


def _rms_bwd_epilogue(layer):
    def epi(acc, i, j, ex, outs):
        x_ref, g_ref, dres_ref = ex
        dx_ref, dg_ref = outs
        xv = x_ref[...]
        r = lax.rsqrt(jnp.mean(xv * xv, axis=-1, keepdims=True) + EPS)
        xhat = xv * r
        w = acc * g_ref[layer:layer + 1, :]
        dx_ref[...] = dres_ref[...] + r * (w - xhat * jnp.mean(xhat * w, axis=-1, keepdims=True))
        dg = jnp.sum(acc * xhat, axis=0, keepdims=True)

        @pl.when(i == 0)
        def _():
            dg_ref[...] = dg

        @pl.when(i > 0)
        def _():
            dg_ref[...] += dg
    return epi


def _own_slab_spec(kind, tr, cols, nblk):
    if kind == "stack":
        return pl.BlockSpec((None, tr, cols), lambda i, idx: (idx[0], i, 0))
    if kind == "cols512":
        return pl.BlockSpec((tr, cols), lambda i, idx: (i, idx[0]))
    return pl.BlockSpec((tr, cols), lambda i, idx: (idx[0] * nblk + i, 0))


def _cast_to_full(name, w, kind, full_shape, my_idx, dtype):
    n_layers, rows, cols = w.shape
    tr = min(rows, 256)
    nblk = rows // tr

    def body(idx_ref, w_ref, *o_refs):
        for l in range(n_layers):
            o_refs[l][...] = w_ref[l].astype(dtype)

    grid_spec = pltpu.PrefetchScalarGridSpec(
        num_scalar_prefetch=1, grid=(nblk,), in_specs=[pl.BlockSpec((n_layers, tr, cols), lambda i, idx: (0, i, 0))],
        out_specs=[_own_slab_spec(kind, tr, cols, nblk)] * n_layers)
    return pl.pallas_call(body, grid_spec=grid_spec, out_shape=[SDS(full_shape, dtype)] * n_layers, name=name,
                          compiler_params=_cparams(1))(my_idx, w)


def _adamw_math(w, m, v, g):
    m_new = ADAM_B1 * m + (1.0 - ADAM_B1) * g
    v_new = ADAM_B2 * v + (1.0 - ADAM_B2) * (g * g)
    m_hat = m_new / (1.0 - ADAM_B1 ** ADAM_STEP)
    v_hat = v_new / (1.0 - ADAM_B2 ** ADAM_STEP)
    delta = -ADAM_LR * (m_hat / (jnp.sqrt(v_hat) + ADAM_EPS) + ADAM_WD * w)
    return delta, m_new, v_new


def _adamw_layer(name, kind, layer, w, m, v, land, g_full, my_idx, prev, tr):
    rows2, cols = w.shape
    rows = rows2 // DEPTH
    nblk = rows // tr
    own_spec = _own_slab_spec(kind, tr, cols, nblk)
    n_prev = 0 if prev is None else 4

    def body(idx_ref, w_ref, m_ref, v_ref, land_ref, own_ref, *rest):
        g_ref, d_ref, mo_ref, vo_ref = rest[n_prev:]
        me = idx_ref[0]
        g = None
        for p in range(N_DEV):
            part = jnp.where(me == p, own_ref[...], land_ref[p]).astype(f32)
            g = part if g is None else g + part
        delta, m_new, v_new = _adamw_math(w_ref[...], m_ref[...], v_ref[...], g)
        g_ref[...] = g
        d_ref[...] = delta
        mo_ref[...] = m_new
        vo_ref[...] = v_new

    blk = pl.BlockSpec((tr, cols), lambda i, idx: (layer * nblk + i, 0))
    grid_spec = pltpu.PrefetchScalarGridSpec(
        num_scalar_prefetch=1, grid=(nblk,),
        in_specs=[blk, blk, blk, pl.BlockSpec((N_DEV, tr, cols), lambda i, idx: (0, i, 0)), own_spec] + [ANY_SPEC] * n_prev,
        out_specs=[blk, blk, blk, blk])
    aliases = {} if prev is None else {6 + k: k for k in range(4)}
    return pl.pallas_call(
        body, grid_spec=grid_spec, out_shape=[SDS((rows2, cols), f32)] * 4, name=name, input_output_aliases=aliases,
        compiler_params=_cparams(1),
    )(my_idx, w, m, v, land, g_full, *([] if prev is None else prev))


def _bucket_table():
    qi = np.arange(BLK)[:, None]
    kj = np.arange(2 * BLK)[None, :]
    dist = qi + BLK - kj
    dcl = np.clip(dist, 0, None)
    max_exact = N_BUCKETS // 2
    d_f = np.maximum(dcl, 1).astype(np.float32)
    large = max_exact + (np.log(d_f / np.float32(max_exact)) / np.float32(math.log(128 / max_exact))
                         * np.float32(N_BUCKETS - max_exact)).astype(np.int32)
    large = np.minimum(large, N_BUCKETS - 1)
    bucket = np.where(dcl < max_exact, dcl, large)
    in_window = (dist >= 0) & (dist < BLK)
    return bucket.astype(np.int32), in_window


def _onehot_buckets():
    bucket, _ = _bucket_table()
    oh = (bucket.reshape(-1)[None, :] == np.arange(N_BUCKETS)[:, None]).astype(np.float32)
    return oh


def _bias_build(rel_bias_t, onehot_t):
    def body(r_ref, o_ref, out_ref):
        out_ref[...] = jnp.dot(r_ref[...], o_ref[...], preferred_element_type=f32, precision=HIGHEST)

    tn = 4096
    return pl.pallas_call(
        body, grid=(BLK * 2 * BLK // tn,),
        in_specs=[pl.BlockSpec((NQ, N_BUCKETS), lambda i: (0, 0)), pl.BlockSpec((N_BUCKETS, tn), lambda i: (0, i))],
        out_specs=pl.BlockSpec((NQ, tn), lambda i: (0, i)), out_shape=SDS((NQ, BLK * 2 * BLK), f32), name="bias_build",
        compiler_params=_cparams(1),
    )(rel_bias_t, onehot_t)


def _bias_grad(dbias0, dbias1, onehot_t):
    tn = 4096
    nsteps = BLK * 2 * BLK // tn

    def body(a_ref, b_ref, o_ref, out_ref):
        part = lax.dot_general(a_ref[...] + b_ref[...], o_ref[...], NT_DIMS, preferred_element_type=f32, precision=HIGHEST)

        @pl.when(pl.program_id(0) == 0)
        def _():
            out_ref[...] = part

        @pl.when(pl.program_id(0) > 0)
        def _():
            out_ref[...] += part

    return pl.pallas_call(
        body, grid=(nsteps,),
        in_specs=[pl.BlockSpec((NQ, tn), lambda i: (0, i)), pl.BlockSpec((NQ, tn), lambda i: (0, i)),
                  pl.BlockSpec((N_BUCKETS, tn), lambda i: (0, i))],
        out_specs=pl.BlockSpec((NQ, N_BUCKETS), lambda i: (0, 0)), out_shape=SDS((NQ, N_BUCKETS), f32), name="bias_grad",
        compiler_params=_cparams(1),
    )(dbias0, dbias1, onehot_t)


def _attn_mask(n):
    qi = lax.broadcasted_iota(jnp.int32, (BLK, 2 * BLK), 0)
    kj = lax.broadcasted_iota(jnp.int32, (BLK, 2 * BLK), 1)
    dist = qi + BLK - kj
    first_key = jnp.where(n > 0, 0, BLK)
    return (dist >= 0) & (dist < BLK) & (kj >= first_key)


def _head_norm(t, gain):
    r = lax.rsqrt(jnp.mean(t * t, axis=-1, keepdims=True) + EPS)
    that = t * r
    return that, r, that * gain


def _softmax_with_sink(s, sink):
    m = jnp.maximum(jnp.max(s, axis=-1, keepdims=True), sink)
    p = jnp.exp(s - m)
    psink = jnp.exp(sink - m)
    inv = 1.0 / (jnp.sum(p, axis=-1, keepdims=True) + psink)
    return p * inv, psink * inv


GQ = NQ // NKV


def _group_rows(x_ref, sk_ref, layer, j):
    heads = [GQ * j + g for g in range(GQ)]
    xs = jnp.concatenate([x_ref[:, pl.ds(HD * h, HD)] for h in heads], axis=0)
    sink = jnp.concatenate([jnp.broadcast_to(sk_ref[layer:layer + 1, h:h + 1], (BLK, 1)) for h in heads], axis=0)
    return xs, sink


def _attn_fwd(qkv, q_gain, k_gain, sinks, bias, layer):
    def body(q_ref, kc_ref, kp_ref, vc_ref, vp_ref, qg_ref, kg_ref, sk_ref, bias_ref, o_ref):
        n = pl.program_id(0)
        mask = jnp.tile(_attn_mask(n), (GQ, 1))
        qg = qg_ref[layer:layer + 1, :]
        kg = kg_ref[layer:layer + 1, :]
        grp = range(NKV)
        kbs = [jnp.concatenate([kp_ref[:, pl.ds(HD * j, HD)], kc_ref[:, pl.ds(HD * j, HD)]], axis=0) for j in grp]
        vbs = [jnp.concatenate([vp_ref[:, pl.ds(HD * j, HD)], vc_ref[:, pl.ds(HD * j, HD)]], axis=0).astype(bf16) for j in grp]
        kn_b = [_head_norm(kbs[j], kg)[2].astype(bf16) for j in grp]
        rows = [_group_rows(q_ref, sk_ref, layer, j) for j in grp]
        qn_b = [_head_norm(rows[j][0], qg)[2].astype(bf16) for j in grp]
        ss = [_dot(qn_b[j], kn_b[j], NT_DIMS) * (HD ** -0.5) + bias_ref[GQ * j:GQ * (j + 1)].reshape(GQ * BLK, 2 * BLK) for j in grp]
        ps = [_softmax_with_sink(jnp.where(mask, ss[j], -jnp.inf), rows[j][1])[0] for j in grp]
        outs = [_dot(ps[j].astype(bf16), vbs[j], NN_DIMS).astype(bf16) for j in grp]
        for j in grp:
            for g in range(GQ):
                o_ref[:, pl.ds(HD * (GQ * j + g), HD)] = outs[j][BLK * g:BLK * (g + 1), :]

    prev = lambda n: jnp.maximum(n - 1, 0)
    small = lambda shape: pl.BlockSpec(shape, lambda n: (0,) * len(shape))
    return pl.pallas_call(
        body, grid=(NBLK,),
        in_specs=[pl.BlockSpec((BLK, D_ATTN), lambda n: (n, 0)),
                  pl.BlockSpec((BLK, 128), lambda n: (n, 4)), pl.BlockSpec((BLK, 128), lambda n: (prev(n), 4)),
                  pl.BlockSpec((BLK, 128), lambda n: (n, 5)), pl.BlockSpec((BLK, 128), lambda n: (prev(n), 5)),
                  small((DEPTH, HD)), small((DEPTH, HD)), small((DEPTH, NQ)), small((NQ, BLK, 2 * BLK))],
        out_specs=pl.BlockSpec((BLK, D_ATTN), lambda n: (n, 0)), out_shape=SDS((S, D_ATTN), bf16),
        name="attn_fwd", compiler_params=_cparams(1),
    )(qkv, qkv, qkv, qkv, qkv, q_gain, k_gain, sinks, bias)


def _attn_bwd(qkv, dmix, q_gain, k_gain, sinks, bias, layer):
    def body(q_ref, kc_ref, kp_ref, vc_ref, vp_ref, do_ref, qg_ref, kg_ref, sk_ref, bias_ref,
             dqkv_ref, dbias_ref, dsm_ref, carry):
        i = pl.program_id(0)
        n = NBLK - 1 - i
        mask = jnp.tile(_attn_mask(n), (GQ, 1))
        qg = qg_ref[layer:layer + 1, :]
        kg = kg_ref[layer:layer + 1, :]
        lane = lax.broadcasted_iota(jnp.int32, (1, 128), 1)

        @pl.when(i == 0)
        def _():
            carry[...] = jnp.zeros_like(carry)
            dbias_ref[...] = jnp.zeros_like(dbias_ref)
            dsm_ref[...] = jnp.zeros_like(dsm_ref)

        grp = range(NKV)
        kbs = [jnp.concatenate([kp_ref[:, pl.ds(HD * j, HD)], kc_ref[:, pl.ds(HD * j, HD)]], axis=0) for j in grp]
        vbs = [jnp.concatenate([vp_ref[:, pl.ds(HD * j, HD)], vc_ref[:, pl.ds(HD * j, HD)]], axis=0).astype(bf16) for j in grp]
        knorm = [_head_norm(kbs[j], kg) for j in grp]
        kn_b = [knorm[j][2].astype(bf16) for j in grp]
        rows = [_group_rows(q_ref, sk_ref, layer, j) for j in grp]
        qnorm = [_head_norm(rows[j][0], qg) for j in grp]
        qn_b = [qnorm[j][2].astype(bf16) for j in grp]
        ss = [_dot(qn_b[j], kn_b[j], NT_DIMS) * (HD ** -0.5) + bias_ref[GQ * j:GQ * (j + 1)].reshape(GQ * BLK, 2 * BLK) for j in grp]
        sm = [_softmax_with_sink(jnp.where(mask, ss[j], -jnp.inf), rows[j][1]) for j in grp]
        do_b = [jnp.concatenate([do_ref[:, pl.ds(HD * (GQ * j + g), HD)] for g in range(GQ)], axis=0).astype(bf16) for j in grp]
        dps = [_dot(do_b[j], vbs[j], NT_DIMS) for j in grp]
        deltas = [jnp.sum(sm[j][0] * dps[j], axis=-1, keepdims=True) for j in grp]
        dss = [sm[j][0] * (dps[j] - deltas[j]) for j in grp]
        ds_b = [(dss[j] * (HD ** -0.5)).astype(bf16) for j in grp]
        dqn = [_dot(ds_b[j], kn_b[j], NN_DIMS) for j in grp]
        dkn = [_dot(ds_b[j], qn_b[j], TN_DIMS) for j in grp]
        dvs = [_dot(sm[j][0].astype(bf16), do_b[j], TN_DIMS) for j in grp]
        dqg = jnp.zeros((1, HD), f32)
        dkg = jnp.zeros((1, HD), f32)
        dsink = jnp.zeros((1, 128), f32)
        for j in grp:
            dbias_ref[GQ * j:GQ * (j + 1)] += dss[j].reshape(GQ, BLK, 2 * BLK)
            dsk = sm[j][1] * deltas[j]
            for g in range(GQ):
                dsink = dsink + jnp.where(lane == GQ * j + g, -_sum11(dsk[BLK * g:BLK * (g + 1), :]), 0.0)
            qhat, rq, _ = qnorm[j]
            w = dqn[j] * qg
            dq = rq * (w - qhat * jnp.mean(qhat * w, axis=-1, keepdims=True))
            for g in range(GQ):
                dqkv_ref[:, pl.ds(HD * (GQ * j + g), HD)] = dq[BLK * g:BLK * (g + 1), :].astype(bf16)
            dqg = dqg + jnp.sum(dqn[j] * qhat, axis=0, keepdims=True)
            khat, rk, _ = knorm[j]
            w = dkn[j] * kg
            dk = rk * (w - khat * jnp.mean(khat * w, axis=-1, keepdims=True))
            dkg = dkg + jnp.sum(dkn[j] * khat, axis=0, keepdims=True)
            dqkv_ref[:, pl.ds(D_ATTN + HD * j, HD)] = (dk[BLK:, :] + carry[:, pl.ds(HD * j, HD)]).astype(bf16)
            dqkv_ref[:, pl.ds(D_ATTN + 128 + HD * j, HD)] = (dvs[j][BLK:, :] + carry[:, pl.ds(128 + HD * j, HD)]).astype(bf16)
            carry[:, pl.ds(HD * j, HD)] = dk[:BLK, :]
            carry[:, pl.ds(128 + HD * j, HD)] = dvs[j][:BLK, :]
        dsm_ref[0:1, 0:HD] += dqg
        dsm_ref[1:2, 0:HD] += dkg
        dsm_ref[2:3, :] += dsink

    rev = lambda i: NBLK - 1 - i
    prev = lambda i: jnp.maximum(NBLK - 2 - i, 0)
    small = lambda shape: pl.BlockSpec(shape, lambda i: (0,) * len(shape))
    return pl.pallas_call(
        body, grid=(NBLK,),
        in_specs=[pl.BlockSpec((BLK, D_ATTN), lambda i: (rev(i), 0)),
                  pl.BlockSpec((BLK, 128), lambda i: (rev(i), 4)), pl.BlockSpec((BLK, 128), lambda i: (prev(i), 4)),
                  pl.BlockSpec((BLK, 128), lambda i: (rev(i), 5)), pl.BlockSpec((BLK, 128), lambda i: (prev(i), 5)),
                  pl.BlockSpec((BLK, D_ATTN), lambda i: (rev(i), 0)),
                  small((DEPTH, HD)), small((DEPTH, HD)), small((DEPTH, NQ)), small((NQ, BLK, 2 * BLK))],
        out_specs=[pl.BlockSpec((BLK, 768), lambda i: (rev(i), COL_QKV // 768)), small((NQ, BLK, 2 * BLK)), small((8, 128))],
        out_shape=[SDS((S, D_IN_PAD), bf16), SDS((NQ, BLK, 2 * BLK), f32), SDS((8, 128), f32)],
        scratch_shapes=[pltpu.VMEM((BLK, 256), f32)], name="attn_bwd", compiler_params=_cparams(1),
    )(qkv, qkv, qkv, qkv, qkv, dmix, q_gain, k_gain, sinks, bias)


CONV_TC = 128


def _shift_down(u, s):
    if s == 0:
        return u
    rows = lax.broadcasted_iota(jnp.int32, u.shape, 0)
    return jnp.where(rows >= s, pltpu.roll(u, s, 0), 0.0)


def _shift_up(u, s):
    if s == 0:
        return u
    rows = lax.broadcasted_iota(jnp.int32, u.shape, 0)
    return jnp.where(rows < u.shape[0] - s, pltpu.roll(u, u.shape[0] - s, 0), 0.0)


def _conv_specs():
    return [pl.BlockSpec((S, CONV_TC), lambda c: (0, c)),
            pl.BlockSpec((None, 4, CONV_TC), lambda c: (0, 0, c)),
            pl.BlockSpec((DEPTH, CONV_TC), lambda c: (0, c))]


def _conv_pre(u, w_ref, b_ref, layer):
    pre = b_ref[layer:layer + 1, :] + w_ref[3:4, :] * u
    for k in range(3):
        pre = pre + w_ref[k:k + 1, :] * _shift_down(u, 3 - k)
    return pre


def _conv_fwd(xbc, conv_w, conv_b, layer):
    def body(u_ref, w_ref, b_ref, o_ref):
        pre = _conv_pre(u_ref[...], w_ref, b_ref, layer)
        o_ref[...] = pre * _sigmoid(pre)

    specs = _conv_specs()
    specs[1] = pl.BlockSpec((None, 4, CONV_TC), lambda c: (layer, 0, c))
    return pl.pallas_call(
        body, grid=(D_CONV // CONV_TC,), in_specs=specs, out_specs=pl.BlockSpec((S, CONV_TC), lambda c: (0, c)),
        out_shape=SDS((S, D_CONV), f32), name="conv_fwd", compiler_params=_cparams(1),
    )(xbc, conv_w, conv_b)


def _conv_bwd(xbc, dact, conv_w, conv_b, dproj, layer):
    def body(u_ref, w_ref, b_ref, da_ref, dproj_in, du_ref, dw_ref, db_ref):
        u = u_ref[...]
        pre = _conv_pre(u, w_ref, b_ref, layer)
        sg = _sigmoid(pre)
        dpre = da_ref[...] * (sg * (1.0 + pre * (1.0 - sg)))
        du = w_ref[3:4, :] * dpre
        for k in range(3):
            du = du + w_ref[k:k + 1, :] * _shift_up(dpre, 3 - k)
        du_ref[...] = du.astype(bf16)
        db_ref[...] = jnp.broadcast_to(jnp.sum(dpre, axis=0, keepdims=True), db_ref.shape)
        dw_ref[...] = jnp.zeros_like(dw_ref)
        for k in range(4):
            dw_ref[k:k + 1, :] = jnp.sum(dpre * _shift_down(u, 3 - k), axis=0, keepdims=True)

    specs = _conv_specs()
    specs[1] = pl.BlockSpec((None, 4, CONV_TC), lambda c: (layer, 0, c))
    col = pl.BlockSpec((S, CONV_TC), lambda c: (0, c))
    row8 = pl.BlockSpec((8, CONV_TC), lambda c: (0, c))
    return pl.pallas_call(
        body, grid=(D_CONV // CONV_TC,), in_specs=[*specs, col, ANY_SPEC],
        out_specs=[pl.BlockSpec((S, CONV_TC), lambda c: (0, COL_XBC // CONV_TC + c)), row8, row8],
        out_shape=[SDS((S, D_IN_PAD), bf16), SDS((8, D_CONV), f32), SDS((8, D_CONV), f32)], name="conv_bwd",
        input_output_aliases={4: 0}, compiler_params=_cparams(1),
    )(xbc, conv_w, conv_b, dact, dproj)


def _tri():
    return (lax.broadcasted_iota(jnp.int32, (BLK, BLK), 0) >= lax.broadcasted_iota(jnp.int32, (BLK, BLK), 1))


def _ssd_scalars(dt_ref, dtb_ref, alog_ref, layer):
    raw = dt_ref[:, 0:NSSM] + dtb_ref[layer:layer + 1, :]
    dtv = jnp.maximum(raw, 0.0) + jnp.log(1.0 + jnp.exp(-jnp.abs(raw)))
    a = -jnp.exp(alog_ref[layer:layer + 1, :])
    acs = jnp.dot(_tri().astype(f32), dtv * a, preferred_element_type=f32, precision=HIGHEST)
    return raw, dtv, a, acs


HG = NSSM // NGRP
GW = HG * HD


def _lane_expand(cols, g):
    lane_head = lax.broadcasted_iota(jnp.int32, (1, GW), 1) // HD
    out = cols[:, HG * g + HG - 1:HG * g + HG]
    for r in range(HG - 2, -1, -1):
        out = jnp.where(lane_head == r, cols[:, HG * g + r:HG * g + r + 1], out)
    return out


def _row_expand(vals, g):
    row_head = lax.broadcasted_iota(jnp.int32, (GW, 1), 0) // HD
    out = vals[:, HG * g + HG - 1:HG * g + HG]
    for r in range(HG - 2, -1, -1):
        out = jnp.where(row_head == r, vals[:, HG * g + r:HG * g + r + 1], out)
    return out


def _head_rowsums(a):
    sel = (lax.broadcasted_iota(jnp.int32, (GW, HG), 0) // HD == lax.broadcasted_iota(jnp.int32, (GW, HG), 1)).astype(bf16)
    hi = a.astype(bf16)
    lo = (a - hi.astype(f32)).astype(bf16)
    sums = _dot(hi, sel, NN_DIMS) + _dot(lo, sel, NN_DIMS)
    return [sums[:, r:r + 1] for r in range(HG)]


def _ssd_chunk_common(xc_ref, dt_ref, dtb_ref, alog_ref, h_rows, layer):
    raw, dtv, a, acs = _ssd_scalars(dt_ref, dtb_ref, alog_ref, layer)
    acs_t = acs.T
    last = acs[BLK - 1:BLK, :]
    c = dict(raw=raw, dtv=dtv, a=a, acs=acs, last=last, dte=jnp.exp(last - acs), e_all=jnp.exp(acs), cd=jnp.exp(last))
    grp, heads, tri = range(NGRP), range(NSSM), _tri()
    c["bm"] = [xc_ref[:, pl.ds(D_SSM + NSTATE * g, NSTATE)] for g in grp]
    c["bm_b"] = [c["bm"][g].astype(bf16) for g in grp]
    c["cm_b"] = [xc_ref[:, pl.ds(D_SSM + NGRP * NSTATE + NSTATE * g, NSTATE)].astype(bf16) for g in grp]
    c["cb"] = [_dot(c["cm_b"][g], c["bm_b"][g], NT_DIMS) for g in grp]
    c["x"] = [xc_ref[:, pl.ds(GW * g, GW)] for g in grp]
    c["dt"] = [_lane_expand(dtv, g) for g in grp]
    c["xdt"] = [c["x"][g] * c["dt"][g] for g in grp]
    c["xdt_b"] = [c["xdt"][g].astype(bf16) for g in grp]
    c["prev"] = [h_rows(g) for g in grp]
    c["prev_b"] = [c["prev"][g].astype(bf16) for g in grp]
    c["e"] = [_lane_expand(c["e_all"], g) for g in grp]
    c["y_off"] = [_dot(c["cm_b"][g], c["prev_b"][g], NT_DIMS) * c["e"][g] for g in grp]
    c["decay"] = [jnp.exp(jnp.where(tri, acs[:, h:h + 1] - acs_t[h:h + 1, :], -jnp.inf)) for h in heads]
    c["m"] = [c["cb"][h // HG] * c["decay"][h] for h in heads]
    c["m_b"] = [c["m"][h].astype(bf16) for h in heads]
    c["dte_x"] = [_lane_expand(c["dte"], g) for g in grp]
    c["xdte_b"] = [(c["xdt"][g] * c["dte_x"][g]).astype(bf16) for g in grp]
    return c


def _ssd_fwd(xact, z, dt, attn, dt_bias, a_log, d_skip, norm_g, layer):
    def body(xc_ref, z_ref, dt_ref, at_ref, dtb_ref, alog_ref, dsk_ref, ng_ref, mix_ref, hs_ref, y_ref, h_ref):
        n = pl.program_id(0)

        @pl.when(n == 0)
        def _():
            h_ref[...] = jnp.zeros_like(h_ref)

        hs_ref[...] = h_ref[...]
        c = _ssd_chunk_common(xc_ref, dt_ref, dtb_ref, alog_ref, lambda g: h_ref[pl.ds(GW * g, GW), :], layer)
        grp, heads = range(NGRP), range(NSSM)
        y_diag = [_dot(c["m_b"][h], c["xdt_b"][h // HG][:, HD * (h % HG):HD * (h % HG + 1)], NN_DIMS) for h in heads]
        new_st = [_dot(c["xdte_b"][g], c["bm_b"][g], TN_DIMS) for g in grp]
        for h in heads:
            y_ref[:, pl.ds(HD * h, HD)] = y_diag[h]
        dskip = dsk_ref[layer:layer + 1, :]
        for g in grp:
            cols = pl.ds(GW * g, GW)
            y_ref[:, cols] = y_ref[:, cols] + c["y_off"][g] + c["x"][g] * _lane_expand(dskip, g)
            h_ref[cols, :] = c["prev"][g] * _row_expand(c["cd"], g) + new_st[g]
        zv = z_ref[...]
        yz = y_ref[...] * (zv * _sigmoid(zv))
        mix_ref[:, 0:D_ATTN] = at_ref[...]
        for g in grp:
            yg = yz[:, GW * g:GW * (g + 1)]
            rs = lax.rsqrt(jnp.mean(yg * yg, axis=-1, keepdims=True) + EPS)
            mix_ref[:, D_ATTN + GW * g:D_ATTN + GW * (g + 1)] = (yg * rs * ng_ref[layer:layer + 1, GW * g:GW * (g + 1)]).astype(bf16)

    small = lambda shape: pl.BlockSpec(shape, lambda n: (0,) * len(shape))
    return pl.pallas_call(
        body, grid=(NBLK,),
        in_specs=[pl.BlockSpec((BLK, D_CONV), lambda n: (n, 0)), pl.BlockSpec((BLK, D_SSM), lambda n: (n, 0)),
                  pl.BlockSpec((BLK, 128), lambda n: (n, 0)), pl.BlockSpec((BLK, D_ATTN), lambda n: (n, 0)),
                  small((DEPTH, NSSM)), small((DEPTH, NSSM)), small((DEPTH, NSSM)), small((DEPTH, D_SSM))],
        out_specs=[pl.BlockSpec((BLK, D), lambda n: (n, 0)), pl.BlockSpec((None, NSSM * HD, NSTATE), lambda n: (n, 0, 0)),
                   pl.BlockSpec((BLK, D_SSM), lambda n: (n, 0))],
        out_shape=[SDS((S, D), bf16), SDS((NBLK, NSSM * HD, NSTATE), f32), SDS((S, D_SSM), f32)],
        scratch_shapes=[pltpu.VMEM((NSSM * HD, NSTATE), f32)],
        name="ssd_fwd", compiler_params=_cparams(1),
    )(xact, z, dt, attn, dt_bias, a_log, d_skip, norm_g)


def _ssd_bwd(xact, z, dt, dmix, hs, y, dt_bias, a_log, d_skip, norm_g, dproj, layer):
    def body(xc_ref, z_ref, dt_ref, do_ref, hs_ref, y_ref, dtb_ref, alog_ref, dsk_ref, ng_ref, dproj_in,
             dzdt_ref, dx_ref, dsm_ref, dh_ref, dy_ref):
        i = pl.program_id(0)

        @pl.when(i == 0)
        def _():
            dh_ref[...] = jnp.zeros_like(dh_ref)
            dsm_ref[...] = jnp.zeros_like(dsm_ref)

        c = _ssd_chunk_common(xc_ref, dt_ref, dtb_ref, alog_ref, lambda g: hs_ref[pl.ds(GW * g, GW), :], layer)
        raw, dtv, a = c["raw"], c["dtv"], c["a"]
        grp, heads = range(NGRP), range(NSSM)
        dskip = dsk_ref[layer:layer + 1, :]
        lane8 = lax.broadcasted_iota(jnp.int32, (1, NSSM), 1)
        sub8 = lax.broadcasted_iota(jnp.int32, (NSSM, 1), 0)

        zv = z_ref[...]
        sz = _sigmoid(zv)
        gz = zv * sz
        yv = y_ref[...]
        yz = yv * gz
        for g in grp:
            sl = slice(GW * g, GW * (g + 1))
            yg = yz[:, sl]
            rs = lax.rsqrt(jnp.mean(yg * yg, axis=-1, keepdims=True) + EPS)
            yhat = yg * rs
            dog = do_ref[:, sl]
            w = dog * ng_ref[layer:layer + 1, sl]
            dyz = rs * (w - yhat * jnp.mean(yhat * w, axis=-1, keepdims=True))
            dsm_ref[0:1, sl] += jnp.sum(dog * yhat, axis=0, keepdims=True)
            dy_ref[:, sl] = dyz * gz[:, sl]
            dzdt_ref[:, sl] = (dyz * yv[:, sl] * (sz[:, sl] * (1.0 + zv[:, sl] * (1.0 - sz[:, sl])))).astype(bf16)

        dy = [dy_ref[:, pl.ds(GW * g, GW)] for g in grp]
        dy_b = [dy[g].astype(bf16) for g in grp]
        hl = lambda h: slice(HD * (h % HG), HD * (h % HG + 1))
        dt_off_b = [(dy[g] * c["e"][g]).astype(bf16) for g in grp]
        dcm = [_dot(dt_off_b[g], c["prev_b"][g], NN_DIMS) for g in grp]
        dprev = [_dot(dt_off_b[g], c["cm_b"][g], TN_DIMS) for g in grp]
        yoff_rs = [_head_rowsums(dy[g] * c["y_off"][g]) for g in grp]
        dhn = [dh_ref[pl.ds(GW * g, GW), :] for g in grp]
        dhn_b = [dhn[g].astype(bf16) for g in grp]
        dprev = [dprev[g] + dhn[g] * _row_expand(c["cd"], g) for g in grp]
        dhn_prev = [dhn[g] * c["prev"][g] for g in grp]
        u = [_dot(c["bm_b"][g], dhn_b[g], NT_DIMS) for g in grp]
        dbm = [_dot(c["xdte_b"][g], dhn_b[g], NN_DIMS) for g in grp]
        ddte_rs = [_head_rowsums(c["xdt"][g] * u[g]) for g in grp]
        dm = [_dot(dy_b[h // HG][:, hl(h)], c["xdt_b"][h // HG][:, hl(h)], NT_DIMS) for h in heads]
        dxdt_in = [_dot(c["m_b"][h], dy_b[h // HG][:, hl(h)], TN_DIMS) for h in heads]
        dseg = [dm[h] * c["m"][h] for h in heads]
        dmd = [dm[h] * c["decay"][h] for h in heads]
        for h in heads:
            dx_ref[:, pl.ds(HD * h, HD)] = dxdt_in[h]

        dacs = jnp.zeros((BLK, NSSM), f32)
        dacs_cols = jnp.zeros((NSSM, BLK), f32)
        dlast = jnp.zeros((1, NSSM), f32)
        ddtv = jnp.zeros((BLK, NSSM), f32)
        ddsk = jnp.zeros((1, NSSM), f32)
        for g in grp:
            cols = pl.ds(GW * g, GW)
            dxdt = dx_ref[:, cols] + u[g] * c["dte_x"][g]
            dx_ref[:, cols] = dy[g] * _lane_expand(dskip, g) + dxdt * c["dt"][g]
            ddtv_rs = _head_rowsums(dxdt * c["x"][g])
            ddsk_rs = _head_rowsums(dy[g] * c["x"][g])
            dcb = dmd[HG * g]
            for r in range(1, HG):
                dcb = dcb + dmd[HG * g + r]
            dcb_b = dcb.astype(bf16)
            dx_ref[:, pl.ds(D_SSM + NSTATE * g, NSTATE)] = dbm[g] + _dot(dcb_b, c["cm_b"][g], TN_DIMS)
            dx_ref[:, pl.ds(D_SSM + NGRP * NSTATE + NSTATE * g, NSTATE)] = dcm[g] + _dot(dcb_b, c["bm_b"][g], NN_DIMS)
            dh_ref[cols, :] = dprev[g]
            for r in range(HG):
                h = HG * g + r
                oh = (lane8 == h).astype(f32)
                tmp = ddte_rs[g][r] * c["dte"][:, h:h + 1]
                dacs = dacs + oh * (jnp.sum(dseg[h], axis=1, keepdims=True) + yoff_rs[g][r] - tmp)
                dacs_cols = dacs_cols + (sub8 == h).astype(f32) * jnp.sum(dseg[h], axis=0, keepdims=True)
                dlast = dlast + oh * (_sum11(dhn_prev[g][HD * r:HD * (r + 1), :]) * c["cd"][:, h:h + 1] + _sum11(tmp))
                ddtv = ddtv + oh * ddtv_rs[r]
                ddsk = ddsk + oh * _sum11(ddsk_rs[r])

        row = lax.broadcasted_iota(jnp.int32, (BLK, 1), 0)
        dacs = dacs - dacs_cols.T + jnp.where(row == BLK - 1, dlast, 0.0)
        dda = lax.dot_general(_tri().astype(f32), dacs, TN_DIMS, preferred_element_type=f32, precision=HIGHEST)
        ddtv = ddtv + dda * a
        da = jnp.sum(dda * dtv, axis=0, keepdims=True)
        draw = ddtv * _sigmoid(raw)
        dzdt_ref[:, D_SSM:] = jnp.zeros((BLK, COL_XBC - COL_DT), bf16)
        dzdt_ref[:, D_SSM:D_SSM + NSSM] = draw.astype(bf16)
        dsm_ref[1:2, 0:NSSM] += jnp.sum(draw, axis=0, keepdims=True)
        dsm_ref[2:3, 0:NSSM] += da * a
        dsm_ref[3:4, 0:NSSM] += ddsk

    rev = lambda i: NBLK - 1 - i
    small = lambda shape: pl.BlockSpec(shape, lambda i: (0,) * len(shape))
    return pl.pallas_call(
        body, grid=(NBLK,),
        in_specs=[pl.BlockSpec((BLK, D_CONV), lambda i: (rev(i), 0)), pl.BlockSpec((BLK, D_SSM), lambda i: (rev(i), 0)),
                  pl.BlockSpec((BLK, 128), lambda i: (rev(i), 0)), pl.BlockSpec((BLK, D_SSM), lambda i: (rev(i), 1)),
                  pl.BlockSpec((None, NSSM * HD, NSTATE), lambda i: (rev(i), 0, 0)), pl.BlockSpec((BLK, D_SSM), lambda i: (rev(i), 0)),
                  small((DEPTH, NSSM)), small((DEPTH, NSSM)), small((DEPTH, NSSM)), small((DEPTH, D_SSM)), ANY_SPEC],
        out_specs=[pl.BlockSpec((BLK, COL_XBC - COL_Z), lambda i: (rev(i), COL_Z // (COL_XBC - COL_Z))),
                   pl.BlockSpec((BLK, D_CONV), lambda i: (rev(i), 0)), small((8, D_SSM))],
        out_shape=[SDS((S, D_IN_PAD), bf16), SDS((S, D_CONV), f32), SDS((8, D_SSM), f32)],
        scratch_shapes=[pltpu.VMEM((NSSM * HD, NSTATE), f32), pltpu.VMEM((BLK, D_SSM), f32)],
        name="ssd_bwd", input_output_aliases={10: 0}, compiler_params=_cparams(1),
    )(xact, z, dt, dmix, hs, y, dt_bias, a_log, d_skip, norm_g, dproj)


def _my_place():
    return lax.axis_index("x"), lax.axis_index("y"), lax.axis_index("c")


def _dev_index(px, py, pc):
    return 4 * px + 2 * py + pc


def _slab2(kind, ref, idx):
    if kind == "stack":
        return ref.at[idx]
    if kind == "rows128":
        return ref.at[pl.ds(pl.multiple_of(idx * 128, 128), 128), :]
    if kind == "rows512":
        return ref.at[pl.ds(pl.multiple_of(idx * 512, 512), 512), :]
    return ref.at[:, pl.ds(pl.multiple_of(idx * 512, 512), 512)]


def _slab_shape(kind, full_shape):
    if kind == "stack":
        return tuple(full_shape[1:])
    if kind == "rows128":
        return (128, full_shape[1])
    if kind == "rows512":
        return (512, full_shape[1])
    return (full_shape[0], 512)


KIND = dict(w_in="stack", w_out="rows128", w_up="cols512", w_down="rows512", conv_w="stack")
FULL_SHAPE = dict(w_in=(N_DEV, D, D_IN // N_DEV), w_out=(D, D), w_up=(D, D_FF), w_down=(D_FF, D))
HBM_SPEC = pl.BlockSpec(memory_space=pltpu.HBM)
SEM_SPEC = pl.BlockSpec(memory_space=pltpu.SEMAPHORE)
SIDE_EFFECT = pltpu.SideEffectType.DATAFLOW_SIDE_EFFECTING


def _peers_all():
    x, y, c = _my_place()
    return [(x ^ ((r >> 2) & 1), y ^ ((r >> 1) & 1), c ^ (r & 1)) for r in range(1, N_DEV)]


def _split_start(name, bufs, n_copies, plan, deps=()):
    nb = len(bufs)

    def body(*refs):
        ins = refs[:nb]
        send_sems, recv_sems = refs[nb + len(deps)], refs[nb + len(deps) + 1]
        token = refs[-1]
        for i, (src, dst, dev) in enumerate(plan(ins)):
            pltpu.make_async_remote_copy(src_ref=src, dst_ref=dst, send_sem=send_sems.at[i], recv_sem=recv_sems.at[i],
                                         device_id=dev, device_id_type=MESH).start()
        token[...] = jnp.zeros_like(token)

    outs = pl.pallas_call(
        body, name=name,
        out_shape=(pltpu.SemaphoreType.DMA((n_copies,)), pltpu.SemaphoreType.DMA((n_copies,)),
                   *[pltpu.HBM(b.shape, b.dtype) for b in bufs], SDS((8, 128), f32)),
        in_specs=[HBM_SPEC] * nb + [ANY_SPEC] * len(deps),
        out_specs=(SEM_SPEC, SEM_SPEC, *[HBM_SPEC] * nb, pl.BlockSpec(memory_space=pltpu.VMEM)),
        input_output_aliases={i: 2 + i for i in range(nb)},
        compiler_params=pltpu.CompilerParams(has_side_effects=SIDE_EFFECT),
    )(*[pltpu.with_memory_space_constraint(b, pltpu.HBM) for b in bufs], *deps)
    return dict(send=outs[0], recv=outs[1], bufs=list(outs[2:2 + nb]), token=outs[-1], plan=plan, n=n_copies)


def _split_wait(name, started, after):
    bufs = started["bufs"]
    nb = len(bufs)
    plan = started["plan"]

    def body(*refs):
        ins = refs[:nb]
        send_sems, recv_sems = refs[nb], refs[nb + 1]
        for i, (src, dst, dev) in enumerate(plan(ins)):
            cp = pltpu.make_async_remote_copy(src_ref=src, dst_ref=dst, send_sem=send_sems.at[i], recv_sem=recv_sems.at[i],
                                              device_id=dev, device_id_type=MESH)
            cp.wait_send()
            cp.wait_recv()

    outs = pl.pallas_call(
        body, name=name, out_shape=tuple(pltpu.HBM(b.shape, b.dtype) for b in bufs),
        in_specs=[HBM_SPEC] * nb + [SEM_SPEC, SEM_SPEC] + [ANY_SPEC] * len(after), out_specs=(HBM_SPEC,) * nb,
        input_output_aliases={i: i for i in range(nb)},
        compiler_params=pltpu.CompilerParams(has_side_effects=SIDE_EFFECT),
    )(*bufs, started["send"], started["recv"], *after)
    return list(outs)


def _gather_start(name, names, fulls, deps):
    n_t = len(names)

    def plan(refs):
        x, y, c = _my_place()
        my_idx = _dev_index(x, y, c)
        targets = [(x, y, 1 - c), (1 - x, y, c), (x, 1 - y, c), (1 - x, 1 - y, c)]
        slabs = [_slab2(KIND[names[t]], refs[t], my_idx) for t in range(n_t)]
        return [(slabs[t], slabs[t], dev) for t in range(n_t) for dev in targets]

    return _split_start(name, list(fulls), 4 * n_t, plan, deps)


def _gather_finish(name, names, started, after):
    n_t = len(names)
    fulls = _split_wait(name + "_wait", started, after)
    slab_shapes = [SDS(_slab_shape(KIND[n], f.shape), f.dtype) for n, f in zip(names, fulls)]

    def body(*refs):
        ins = refs[:n_t]
        outs = refs[n_t:2 * n_t]
        stage = refs[2 * n_t:3 * n_t]
        load_sems, send_sems, recv_sems = refs[3 * n_t:]
        x, y, c = _my_place()
        chips = [(1 - x, y), (x, 1 - y), (1 - x, 1 - y)]
        pairs = [(t, j) for t in range(n_t) for j in range(3)]
        loads = [pltpu.make_async_copy(_slab2(KIND[names[t]], ins[t], _dev_index(*chips[j], c)), stage[t].at[j], load_sems.at[t, j])
                 for t, j in pairs]
        for cp in loads:
            cp.start()

        def copy(t, j, core):
            return pltpu.make_async_remote_copy(
                src_ref=stage[t].at[j], dst_ref=_slab2(KIND[names[t]], outs[t], _dev_index(*chips[j], core)),
                send_sem=send_sems.at[t, j], recv_sem=recv_sems.at[t, j], device_id=(x, y, 1 - c), device_id_type=MESH)

        sends = [copy(t, j, c) for t, j in pairs]
        for ld, cp in zip(loads, sends):
            ld.wait()
            cp.start()
        for t, j in pairs:
            copy(t, j, 1 - c).wait_recv()
        for cp in sends:
            cp.wait_send()

    return pl.pallas_call(
        body, in_specs=[ANY_SPEC] * n_t, out_specs=[ANY_SPEC] * n_t, out_shape=[SDS(b.shape, b.dtype) for b in fulls],
        input_output_aliases={t: t for t in range(n_t)},
        scratch_shapes=[pltpu.VMEM((3,) + s.shape, s.dtype) for s in slab_shapes]
        + [pltpu.SemaphoreType.DMA((n_t, 3)), pltpu.SemaphoreType.DMA((n_t, 3)), pltpu.SemaphoreType.DMA((n_t, 3))],
        name=name + "_pass", compiler_params=pltpu.CompilerParams(vmem_limit_bytes=VMEM_LIMIT),
    )(*fulls)


def _exchange_start(name, names, grads, deps):
    n_t = len(names)
    lands = [lax.empty((N_DEV,) + _slab_shape(KIND[n], g.shape), g.dtype) for n, g in zip(names, grads)]

    def plan(refs):
        my_idx = _dev_index(*_my_place())
        return [(_slab2(KIND[names[t]], refs[t], _dev_index(*peer)), refs[n_t + t].at[my_idx], peer)
                for t in range(n_t) for peer in _peers_all()]

    return _split_start(name, list(grads) + lands, 7 * n_t, plan, deps)


def _small_exchange_start(part, deps):
    land = lax.empty((N_DEV,) + part.shape, part.dtype)

    def plan(refs):
        my_idx = _dev_index(*_my_place())
        return [(refs[0], refs[1].at[my_idx], peer) for peer in _peers_all()]

    return _split_start("small_exchange", [part, land], N_DEV - 1, plan, deps)


def _slab_pieces():
    sh = D_IN // N_DEV
    out = []
    for j in range(N_DEV):
        for first, end, dst in IN_SEGMENTS:
            lo, hi = max(first, sh * j), min(end, sh * (j + 1))
            if lo < hi:
                out.append((j, lo - sh * j, hi - sh * j, dst + lo - first))
    return out


def _w_in_assemble(stacked):
    tr = 256
    sh = D_IN // N_DEV

    def body(i_ref, o_ref):
        o_ref[:, COL_DT:COL_XBC] = jnp.zeros((tr, COL_XBC - COL_DT), bf16)
        for j, lo, hi, dst in _slab_pieces():
            o_ref[:, dst:dst + hi - lo] = i_ref[j, :, lo:hi]

    return pl.pallas_call(
        body, grid=(D // tr,), in_specs=[pl.BlockSpec((N_DEV, tr, sh), lambda i: (0, i, 0))],
        out_specs=pl.BlockSpec((None, tr, D_IN_PAD), lambda i: (0, i, 0)), out_shape=SDS((1, D, D_IN_PAD), bf16),
        name="w_in_assemble", compiler_params=_cparams(1),
    )(stacked)


def _w_in_slabs(dw_in):
    tr = 256
    sh = D_IN // N_DEV

    def body(i_ref, o_ref):
        for j, lo, hi, src in _slab_pieces():
            o_ref[j, :, lo:hi] = i_ref[:, src:src + hi - lo]

    return pl.pallas_call(
        body, grid=(D // tr,), in_specs=[pl.BlockSpec((tr, D_IN_PAD), lambda i: (i, 0))],
        out_specs=pl.BlockSpec((N_DEV, tr, sh), lambda i: (0, i, 0)), out_shape=SDS((N_DEV, D, sh), bf16),
        name="w_in_slabs", compiler_params=_cparams(1),
    )(dw_in)


SMALL_NAMES = ("mix_norm_g", "mlp_norm_g", "conv_b", "ssm_norm_g", "q_gain", "k_gain", "sinks", "dt_bias", "a_log", "d_skip",
               "rel_bias", "conv_w")
MISC_LANES = dict(q_gain=(LANE_QG, HD), k_gain=(LANE_KG, HD), sinks=(LANE_SINK, NQ), dt_bias=(LANE_DTB, NSSM),
                  a_log=(LANE_ALOG, NSSM), d_skip=(LANE_DSKIP, NSSM))


def _pack_small_grads(smalls, drel_t, loss):
    def body(*refs):
        o_ref = refs[-1]
        drel_ref, loss_ref = refs[-3], refs[-2]
        o_ref[...] = jnp.zeros_like(o_ref)
        for l in range(DEPTH):
            mixg, mlpg, convb, convw, ssd, attn = refs[6 * l:6 * l + 6]
            o_ref[ROW_MIXG + l:ROW_MIXG + l + 1, :] = mixg[...]
            o_ref[ROW_MLPG + l:ROW_MLPG + l + 1, :] = mlpg[...]
            o_ref[ROW_CONVB + l:ROW_CONVB + l + 1, :] = convb[0:1, :]
            o_ref[ROW_SSMG + l:ROW_SSMG + l + 1, 0:D_SSM] = ssd[0:1, :]
            o_ref[ROW_CONVW + 4 * l:ROW_CONVW + 4 * l + 4, :] = convw[0:4, :]
            row = slice(ROW_MISC + l, ROW_MISC + l + 1)
            o_ref[row, LANE_QG:LANE_QG + HD] = attn[0:1, 0:HD]
            o_ref[row, LANE_KG:LANE_KG + HD] = attn[1:2, 0:HD]
            o_ref[row, LANE_SINK:LANE_SINK + NQ] = attn[2:3, 0:NQ]
            o_ref[row, LANE_DTB:LANE_DTB + NSSM] = ssd[1:2, 0:NSSM]
            o_ref[row, LANE_ALOG:LANE_ALOG + NSSM] = ssd[2:3, 0:NSSM]
            o_ref[row, LANE_DSKIP:LANE_DSKIP + NSSM] = ssd[3:4, 0:NSSM]
        o_ref[ROW_RELB:ROW_RELB + NQ, 0:N_BUCKETS] = drel_ref[...]
        o_ref[ROW_LOSS:ROW_LOSS + 1, 0:1] = loss_ref[0:1, 0:1]

    args = []
    for sm in smalls:
        args += [sm["mix_norm_g"], sm["mlp_norm_g"], sm["conv_b"], sm["conv_w"], sm["ssd"], sm["attn"]]
    args += [drel_t, loss]
    return pl.pallas_call(body, out_shape=SDS((SMALL_ROWS, D), f32), name="pack_small_grads")(*args)


def _adamw_small(part, land, w, m, v):
    n = len(SMALL_NAMES)

    def grad_of(name, g_ref):
        if name == "mix_norm_g":
            return g_ref[ROW_MIXG:ROW_MIXG + DEPTH, :]
        if name == "mlp_norm_g":
            return g_ref[ROW_MLPG:ROW_MLPG + DEPTH, :]
        if name == "conv_b":
            return g_ref[ROW_CONVB:ROW_CONVB + DEPTH, :]
        if name == "ssm_norm_g":
            return g_ref[ROW_SSMG:ROW_SSMG + DEPTH, 0:D_SSM]
        if name == "rel_bias":
            return g_ref[ROW_RELB:ROW_RELB + NQ, 0:N_BUCKETS].T
        lane, width = MISC_LANES[name]
        return g_ref[ROW_MISC:ROW_MISC + DEPTH, lane:lane + width]

    def body(part_ref, land_ref, *refs):
        ws, ms, vs = refs[:n], refs[n:2 * n], refs[2 * n:3 * n]
        loss_ref = refs[3 * n]
        outs = refs[3 * n + 1:-1]
        g_ref = refs[-1]
        me = _dev_index(*_my_place())
        for p in range(N_DEV):
            term = jnp.where(me == p, part_ref[...], land_ref[p])
            if p == 0:
                g_ref[...] = term
            else:
                g_ref[...] += term
        loss_ref[...] = g_ref[ROW_LOSS:ROW_LOSS + 1, 0:128]
        my_cols = pl.ds(pl.multiple_of(me * 128, 128), 128)
        for k, name in enumerate(SMALL_NAMES):
            g_out, d_out, m_out, v_out = outs[4 * k:4 * k + 4]
            if name == "conv_w":
                for l in range(DEPTH):
                    g = g_ref[ROW_CONVW + 4 * l:ROW_CONVW + 4 * l + 4, my_cols]
                    delta, m_new, v_new = _adamw_math(ws[k][l], ms[k][l], vs[k][l], g)
                    g_out[l], d_out[l], m_out[l], v_out[l] = g, delta, m_new, v_new
            else:
                g = grad_of(name, g_ref)
                delta, m_new, v_new = _adamw_math(ws[k][...], ms[k][...], vs[k][...], g)
                g_out[...], d_out[...], m_out[...], v_out[...] = g, delta, m_new, v_new

    ws = [w[name] for name in SMALL_NAMES]
    out_shape = [SDS((1, 128), f32)]
    for a in ws:
        out_shape += [SDS(a.shape, f32)] * 4
    return pl.pallas_call(body, out_shape=out_shape, name="adamw_small", scratch_shapes=[pltpu.VMEM((SMALL_ROWS, D), f32)])(
        part, land, *ws, *[m[name] for name in SMALL_NAMES], *[v[name] for name in SMALL_NAMES])


def _plain(tm, tn):
    return pl.BlockSpec((tm, tn), lambda i, j, k: (i, j))


def _rowblk(tm, width):
    return pl.BlockSpec((tm, width), lambda i, j, k: (i, 0))


def _store_epi(dtype):
    def epi(acc, i, j, ex, outs):
        outs[0][...] = acc.astype(dtype)
    return epi


def _rms_prologue(layer):
    def pro(a_ref, ex, outs):
        xv = a_ref[...]
        r = lax.rsqrt(jnp.mean(xv * xv, axis=-1, keepdims=True) + EPS)
        h = (xv * r * ex[0][layer:layer + 1, :]).astype(bf16)
        outs[-1][...] = h
        return h
    return pro


def _layer_fwd(l, x, p, get_weights, bias, tgt=None):
    wts = get_weights(l, "in", [x, bias])
    gfull = pl.BlockSpec((DEPTH, D), lambda i, j, k: (0, 0))
    tm = 256

    def inproj_epi(acc, i, j, ex, outs):
        outs[0][...] = acc[:, COL_QKV:COL_Z]
        outs[1][...] = acc[:, COL_Z:COL_DT]
        outs[2][...] = acc[:, COL_XBC:D_IN_PAD]
        outs[3][...] = acc[:, COL_DT:COL_DT + 128]

    qkv, z, xbc, dt, h1 = _matmul(
        "in_proj", "nn", x, wts["w_in"], tm=tm, tn=D_IN_PAD, tk=D, prologue=_rms_prologue(l),
        extras=(p["mix_norm_g"],), extra_specs=(gfull,),
        out_shape=[SDS((S, 768), f32), SDS((S, 512), f32), SDS((S, 1024), f32), SDS((S, 128), f32), SDS((S, D), bf16)],
        out_specs=[_rowblk(tm, 768), _rowblk(tm, 512), _rowblk(tm, 1024), _rowblk(tm, 128), _rowblk(tm, D)], epilogue=inproj_epi)
    attn = _attn_fwd(qkv, p["q_gain"], p["k_gain"], p["sinks"], bias, l)
    xact = _conv_fwd(xbc, wts["conv_w"], p["conv_b"], l)
    mix, hs, y_ssd = _ssd_fwd(xact, z, dt, attn, p["dt_bias"], p["a_log"], p["d_skip"], p["ssm_norm_g"], l)
    wts = dict(wts, **get_weights(l, "rest", [mix]))

    def resid_epi(acc, i, j, ex, outs):
        outs[0][...] = ex[0][...] + acc

    x_mid = _matmul("out_proj", "nn", mix, wts["w_out"], tm=tm, tn=D, tk=D, out_shape=SDS((S, D), f32),
                    out_specs=_plain(tm, D), epilogue=resid_epi, extras=(x,), extra_specs=(_plain(tm, D),))

    def up_epi(acc, i, j, ex, outs):
        r = jnp.maximum(acc, 0.0)
        outs[0][...] = (r * r).astype(bf16)
        outs[1][...] = r.astype(bf16)

    a_act, r_act, h2 = _matmul("mlp_up", "nn", x_mid, wts["w_up"], tm=tm, tn=D_FF, tk=D, prologue=_rms_prologue(l),
                               extras=(p["mlp_norm_g"],), extra_specs=(gfull,),
                               out_shape=[SDS((S, D_FF), bf16), SDS((S, D_FF), bf16), SDS((S, D), bf16)],
                               out_specs=[_plain(tm, D_FF), _plain(tm, D_FF), _rowblk(tm, D)], epilogue=up_epi)
    saved = dict(x=x, h1=h1, qkv=qkv, z=z, xbc=xbc, dt=dt, xact=xact, mix=mix, hs=hs, y_ssd=y_ssd, x_mid=x_mid, h2=h2,
                 a=a_act, r=r_act, wts=wts)
    if tgt is None:
        x_out = _matmul("mlp_down", "nn", a_act, wts["w_down"], tm=tm, tn=D, tk=D_FF, out_shape=SDS((S, D), f32),
                        out_specs=_plain(tm, D), epilogue=resid_epi, extras=(x_mid,), extra_specs=(_plain(tm, D),))
        return x_out, saved

    def loss_epi(acc, i, j, ex, outs):
        err = ex[0][...] + acc - ex[1][...]
        outs[0][...] = err * (1.0 / D)
        part = 0.5 * jnp.sum(jnp.mean(err * err, axis=-1, keepdims=True), axis=0, keepdims=True)

        @pl.when(i == 0)
        def _():
            outs[1][...] = jnp.zeros_like(outs[1])

        outs[1][...] += jnp.broadcast_to(part, outs[1].shape)

    head = _matmul("mlp_down_loss", "nn", a_act, wts["w_down"], tm=tm, tn=D, tk=D_FF, out_shape=[SDS((S, D), f32), SDS((1, 128), f32)],
                   out_specs=[_plain(tm, D), pl.BlockSpec((1, 128), lambda i, j, k: (0, 0))], epilogue=loss_epi,
                   extras=(x_mid, tgt), extra_specs=(_plain(tm, D), _plain(tm, D)))
    return head, saved


def _layer_bwd(l, dx_out, sv, p, bias, deps, send):
    wts = sv["wts"]

    def du_epi(acc, i, j, ex, outs):
        outs[0][...] = (acc * (2.0 * ex[0][...].astype(f32))).astype(bf16)

    du = _matmul("mlp_da", "nt", dx_out, wts["w_down"], tm=256, tn=D_FF, tk=D, out_shape=SDS((S, D_FF), bf16),
                 out_specs=_plain(256, D_FF), epilogue=du_epi, extras=(sv["r"],), extra_specs=(_plain(256, D_FF),), deps=deps)
    dw_down = _matmul("dw_down", "tn", sv["a"], dx_out, tm=1024, tn=D, tk=S, out_shape=SDS((D_FF, D), bf16),
                      out_specs=_plain(1024, D), epilogue=_store_epi(bf16))
    dw_up = _matmul("dw_up", "tn", sv["h2"], du, tm=D, tn=1024, tk=S, out_shape=SDS((D, D_FF), bf16),
                    out_specs=_plain(D, 1024), epilogue=_store_epi(bf16))
    deps = send(l, dict(w_down=dw_down, w_up=dw_up))
    gfull = pl.BlockSpec((DEPTH, D), lambda i, j, k: (0, 0))
    grow = pl.BlockSpec((1, D), lambda i, j, k: (0, 0))
    dx_mid, dg_mlp = _matmul(
        "mlp_dh", "nt", du, wts["w_up"], tm=256, tn=D, tk=D_FF, out_shape=[SDS((S, D), f32), SDS((1, D), f32)],
        out_specs=[_plain(256, D), grow], epilogue=_rms_bwd_epilogue(l),
        extras=(sv["x_mid"], p["mlp_norm_g"], dx_out), extra_specs=(_plain(256, D), gfull, _plain(256, D)), deps=deps)
    dmix = _matmul("out_proj_da", "nt", dx_mid, wts["w_out"], tm=256, tn=D, tk=D, out_shape=SDS((S, D), f32),
                   out_specs=_plain(256, D), epilogue=_store_epi(f32))
    dw_out = _matmul("dw_out", "tn", sv["mix"], dx_mid, tm=D, tn=512, tk=512, out_shape=SDS((D, D), bf16),
                     out_specs=_plain(D, 512), epilogue=_store_epi(bf16))
    dproj, dbias, dsm_attn = _attn_bwd(sv["qkv"], dmix, p["q_gain"], p["k_gain"], p["sinks"], bias, l)
    dproj, dxact, dsm_ssd = _ssd_bwd(sv["xact"], sv["z"], sv["dt"], dmix, sv["hs"], sv["y_ssd"], p["dt_bias"], p["a_log"],
                                     p["d_skip"], p["ssm_norm_g"], dproj, l)
    dproj, dconv_w, dconv_b = _conv_bwd(sv["xbc"], dxact, wts["conv_w"], p["conv_b"], dproj, l)
    dw_in = _matmul("dw_in", "tn", sv["h1"], dproj, tm=D, tn=640, tk=S, out_shape=SDS((D, D_IN_PAD), bf16),
                    out_specs=_plain(D, 640), epilogue=_store_epi(bf16))
    deps = send(l, dict(w_out=dw_out, w_in=_w_in_slabs(dw_in)))
    dx, dg_mix = _matmul(
        "in_proj_dh", "nt", dproj, wts["w_in"], tm=256, tn=D, tk=D_IN_PAD, out_shape=[SDS((S, D), f32), SDS((1, D), f32)],
        out_specs=[_plain(256, D), grow], epilogue=_rms_bwd_epilogue(l),
        extras=(sv["x"], p["mix_norm_g"], dx_mid), extra_specs=(_plain(256, D), gfull, _plain(256, D)), deps=deps)
    small = dict(mix_norm_g=dg_mix, mlp_norm_g=dg_mlp, conv_w=dconv_w, conv_b=dconv_b, ssd=dsm_ssd, attn=dsm_attn, dbias=dbias)
    return dx, small, deps


def _local_step(x, tgt, p, get_weights, send):
    onehot_t = jnp.asarray(_onehot_buckets())
    bias = _bias_build(p["rel_bias"].T, onehot_t).reshape(NQ, BLK, 2 * BLK)
    saved = []
    h = x
    for l in range(DEPTH):
        h, sv = _layer_fwd(l, h, p, get_weights, bias, tgt if l == DEPTH - 1 else None)
        saved.append(sv)
    dx, loss = h
    smalls = [None] * DEPTH
    deps = ()
    for l in reversed(range(DEPTH)):
        dx, smalls[l], deps = _layer_bwd(l, dx, saved[l], p, bias, deps, send)
    drel_t = _bias_grad(smalls[0]["dbias"].reshape(NQ, -1), smalls[1]["dbias"].reshape(NQ, -1), onehot_t)
    return dx, _pack_small_grads(smalls, drel_t, loss)


WEIGHT_ORDER = ("mix_norm_g", "w_in", "q_gain", "k_gain", "sinks", "rel_bias", "conv_w", "conv_b", "dt_bias", "a_log", "d_skip",
                "ssm_norm_g", "w_out", "mlp_norm_g", "w_up", "w_down")


def kernel(x, mix_norm_g, w_in, q_gain, k_gain, sinks, rel_bias, conv_w, conv_b, dt_bias, a_log, d_skip, ssm_norm_g, w_out, mlp_norm_g, w_up, w_down, loss_target, m_mix_norm_g, m_w_in, m_q_gain, m_k_gain, m_sinks, m_rel_bias, m_conv_w, m_conv_b, m_dt_bias, m_a_log, m_d_skip, m_ssm_norm_g, m_w_out, m_mlp_norm_g, m_w_up, m_w_down, v_mix_norm_g, v_w_in, v_q_gain, v_k_gain, v_sinks, v_rel_bias, v_conv_w, v_conv_b, v_dt_bias, v_a_log, v_d_skip, v_ssm_norm_g, v_w_out, v_mlp_norm_g, v_w_up, v_w_down):
    w = dict(mix_norm_g=mix_norm_g, w_in=w_in, q_gain=q_gain, k_gain=k_gain, sinks=sinks, rel_bias=rel_bias, conv_w=conv_w,
             conv_b=conv_b, dt_bias=dt_bias, a_log=a_log, d_skip=d_skip, ssm_norm_g=ssm_norm_g, w_out=w_out,
             mlp_norm_g=mlp_norm_g, w_up=w_up, w_down=w_down)
    m = dict(mix_norm_g=m_mix_norm_g, w_in=m_w_in, q_gain=m_q_gain, k_gain=m_k_gain, sinks=m_sinks, rel_bias=m_rel_bias,
             conv_w=m_conv_w, conv_b=m_conv_b, dt_bias=m_dt_bias, a_log=m_a_log, d_skip=m_d_skip, ssm_norm_g=m_ssm_norm_g,
             w_out=m_w_out, mlp_norm_g=m_mlp_norm_g, w_up=m_w_up, w_down=m_w_down)
    v = dict(mix_norm_g=v_mix_norm_g, w_in=v_w_in, q_gain=v_q_gain, k_gain=v_k_gain, sinks=v_sinks, rel_bias=v_rel_bias,
             conv_w=v_conv_w, conv_b=v_conv_b, dt_bias=v_dt_bias, a_log=v_a_log, d_skip=v_d_skip, ssm_norm_g=v_ssm_norm_g,
             w_out=v_w_out, mlp_norm_g=v_mlp_norm_g, w_up=v_w_up, w_down=v_w_down)
    big = ("w_in", "w_out", "w_up", "w_down")

    my_idx = _dev_index(*_my_place()).astype(jnp.int32).reshape(1)

    fulls = {n: _cast_to_full("cast_" + n, w[n], KIND[n], FULL_SHAPE[n], my_idx, bf16) for n in big}
    conv_full = _cast_to_full("cast_conv_w", conv_w.reshape(1, DEPTH * 4, 128), "stack", (N_DEV, DEPTH * 4, 128), my_idx, f32)[0]
    rest = ["w_out", "w_up", "w_down"]
    g0 = _gather_start("gather0", ["w_in", "conv_w"], [fulls["w_in"][0], conv_full], ())
    g1 = _gather_start("gather1", rest, [fulls[n][0] for n in rest], (g0["token"],))
    g2 = _gather_start("gather2", ["w_in"], [fulls["w_in"][1]], (g1["token"],))
    g3 = _gather_start("gather3", rest, [fulls[n][1] for n in rest], (g2["token"],))
    held = {}
    flat = lambda a: a.reshape(a.shape[0] * a.shape[1], a.shape[2])
    adam_in = {n: (flat(w[n]), flat(m[n]), flat(v[n])) for n in big}

    def get_weights(l, part, after):
        if l == 0 and part == "in":
            full_in, full_conv = _gather_finish("gather0", ["w_in", "conv_w"], g0,
                                                list(after) + [g3["token"], adam_in["w_in"][1], adam_in["w_in"][2]])
            held["conv_w"] = jnp.transpose(full_conv.reshape(N_DEV, DEPTH, 4, 128), (1, 2, 0, 3)).reshape(DEPTH, 4, D_CONV)
            return dict(w_in=_w_in_assemble(full_in), conv_w=held["conv_w"])
        if part == "in":
            return dict(w_in=_w_in_assemble(_gather_finish("gather2", ["w_in"], g2, after)[0]), conv_w=held["conv_w"])
        full = _gather_finish("gather1" if l == 0 else "gather3", rest, g1 if l == 0 else g3, after)
        return {n: f[None] for n, f in zip(rest, full)}

    pending = []

    def send(l, grads):
        names = list(grads)
        started = _exchange_start("exchange%d_%s" % (l, names[0]), names, [grads[n] for n in names], ())
        pending.append((l, names, started))
        return (started["token"],)

    dx, small_part = _local_step(x.reshape(S, D), loss_target.reshape(S, D), w, get_weights, send)

    small = _small_exchange_start(small_part, ())
    tiles = dict(w_in=256, w_out=128, w_up=256, w_down=256)
    outs_of = {n: None for n in big}
    after = [dx, small["token"]]
    for l, names, started in pending:
        bufs = _split_wait("exchange%d_%s_wait" % (l, names[0]), started, after)
        for t, n in enumerate(names):
            outs_of[n] = _adamw_layer("adamw_%s%d" % (n, l), KIND[n], l, *adam_in[n],
                                      bufs[len(names) + t], bufs[t], my_idx, outs_of[n], tiles[n])
        after = [outs_of[names[-1]][0]]
    res = {n: [o.reshape(w[n].shape) for o in outs_of[n]] for n in big}
    small_part, small_land = _split_wait("small_exchange_wait", small, after)
    small_outs = _adamw_small(small_part, small_land, w, m, v)
    loss = small_outs[0][0, 0]
    for k, name in enumerate(SMALL_NAMES):
        res[name] = small_outs[1 + 4 * k:5 + 4 * k]

    result = [loss, dx.reshape(1, S, D)]
    for k in range(4):
        result += [res[name][k] for name in WEIGHT_ORDER]
    return tuple(result)
```

```python
import functools
import math

import numpy as np
import jax
import jax.numpy as jnp
from jax import lax
from jax.experimental import pallas as pl
from jax.experimental.pallas import tpu as pltpu

f32 = jnp.float32
bf16 = jnp.bfloat16
SDS = jax.ShapeDtypeStruct
MESH = pl.DeviceIdType.MESH
HIGHEST = lax.Precision.HIGHEST

S = 2048
D = 1024
DEPTH = 2
BLK = 128
NBLK = S // BLK
HD = 64
NQ = 8
NKV = 2
NSSM = 8
NGRP = 2
NSTATE = 128
D_ATTN = 512
D_SSM = 512
D_CONV = 1024
D_FF = 4096
D_IN = 2312
D_IN_PAD = 2560
COL_QKV, COL_Z, COL_DT, COL_XBC = 0, 768, 1280, 1536
IN_SEGMENTS = ((0, 1280, 0), (1280, 2304, COL_XBC), (2304, 2312, COL_DT))
N_BUCKETS = 32
EPS = 1e-6
N_DEV = 8
VMEM_LIMIT = 48 * 1024 * 1024

ADAM_LR = 0.001
ADAM_B1 = 0.9
ADAM_B2 = 0.999
ADAM_EPS = 1e-08
ADAM_WD = 0.01
ADAM_STEP = 10

NT_DIMS = (((1,), (1,)), ((), ()))
TN_DIMS = (((0,), (0,)), ((), ()))
NN_DIMS = (((1,), (0,)), ((), ()))

ROW_MIXG = 0
ROW_MLPG = 2
ROW_CONVB = 4
ROW_SSMG = 6
ROW_MISC = 8
ROW_RELB = 10
ROW_CONVW = 18
ROW_LOSS = 26
SMALL_ROWS = 32
LANE_QG, LANE_KG, LANE_SINK, LANE_DTB, LANE_ALOG, LANE_DSKIP = 0, 64, 128, 256, 384, 512


def _dot(a, b, dims):
    return lax.dot_general(a, b, dims, preferred_element_type=f32)


def _cparams(n_axes):
    return pltpu.CompilerParams(dimension_semantics=("arbitrary",) * n_axes, vmem_limit_bytes=VMEM_LIMIT)


def _sum11(v):
    return jnp.sum(jnp.sum(v, axis=1, keepdims=True), axis=0, keepdims=True)


def _sigmoid(v):
    return 1.0 / (1.0 + jnp.exp(-v))


ANY_SPEC = pl.BlockSpec(memory_space=pl.ANY)


def _matmul(name, mode, a, b, *, layer=0, tm, tn, tk, out_shape, out_specs, epilogue, extras=(), extra_specs=(), deps=(),
            prologue=None):
    extras = tuple(extras) + tuple(deps)
    extra_specs = tuple(extra_specs) + (ANY_SPEC,) * len(deps)
    if mode == "tn":
        t_dim, m_dim = a.shape
        n_dim = b.shape[1]
        grid = (m_dim // tm, n_dim // tn, t_dim // tk)
        a_spec = pl.BlockSpec((tk, tm), lambda i, j, k: (k, i))
        b_spec = pl.BlockSpec((tk, tn), lambda i, j, k: (k, j))
        dims = TN_DIMS
    elif mode == "nn":
        m_dim, k_dim = a.shape
        n_dim = b.shape[-1]
        grid = (m_dim // tm, n_dim // tn, k_dim // tk)
        a_spec = pl.BlockSpec((tm, tk), lambda i, j, k: (i, k))
        b_spec = pl.BlockSpec((None, tk, tn), lambda i, j, k: (layer, k, j))
        dims = NN_DIMS
    else:
        m_dim, k_dim = a.shape
        n_dim = b.shape[-2]
        grid = (m_dim // tm, n_dim // tn, k_dim // tk)
        a_spec = pl.BlockSpec((tm, tk), lambda i, j, k: (i, k))
        b_spec = pl.BlockSpec((None, tn, tk), lambda i, j, k: (layer, j, k))
        dims = NT_DIMS
    nk = grid[2]
    n_ex = len(extras)

    def body(a_ref, b_ref, *rest):
        ex = rest[:n_ex - len(deps)]
        outs = rest[n_ex:-1]
        acc = rest[-1]
        i = pl.program_id(0)
        j = pl.program_id(1)
        k = pl.program_id(2)
        lhs = a_ref[...].astype(bf16) if prologue is None else prologue(a_ref, ex, outs)
        part = _dot(lhs, b_ref[...].astype(bf16), dims)
        if nk == 1:
            epilogue(part, i, j, ex, outs)
        else:
            @pl.when(k == 0)
            def _():
                acc[...] = part

            @pl.when(k > 0)
            def _():
                acc[...] += part

            @pl.when(k == nk - 1)
            def _():
                epilogue(acc[...], i, j, ex, outs)

    return pl.pallas_call(
        body, grid=grid, in_specs=[a_spec, b_spec, *extra_specs], out_specs=out_specs, out_shape=out_shape,
        scratch_shapes=[pltpu.VMEM((tm, tn) if nk > 1 else (8, 128), f32)], name=name, compiler_params=_cparams(3),
    )(a, b, *extras)


def _rms_bwd_epilogue(layer):
    def epi(acc, i, j, ex, outs):
        x_ref, g_ref, dres_ref = ex
        dx_ref, dg_ref = outs
        xv = x_ref[...]
        r = lax.rsqrt(jnp.mean(xv * xv, axis=-1, keepdims=True) + EPS)
        xhat = xv * r
        w = acc * g_ref[layer:layer + 1, :]
        dx_ref[...] = dres_ref[...] + r * (w - xhat * jnp.mean(xhat * w, axis=-1, keepdims=True))
        dg = jnp.sum(acc * xhat, axis=0, keepdims=True)

        @pl.when(i == 0)
        def _():
            dg_ref[...] = dg

        @pl.when(i > 0)
        def _():
            dg_ref[...] += dg
    return epi


def _own_slab_spec(kind, tr, cols, nblk):
    if kind == "stack":
        return pl.BlockSpec((None, tr, cols), lambda i, idx: (idx[0], i, 0))
    if kind == "cols512":
        return pl.BlockSpec((tr, cols), lambda i, idx: (i, idx[0]))
    return pl.BlockSpec((tr, cols), lambda i, idx: (idx[0] * nblk + i, 0))


def _cast_to_full(name, w, kind, full_shape, my_idx, dtype):
    n_layers, rows, cols = w.shape
    tr = min(rows, 256)
    nblk = rows // tr

    def body(idx_ref, w_ref, *o_refs):
        for l in range(n_layers):
            o_refs[l][...] = w_ref[l].astype(dtype)

    grid_spec = pltpu.PrefetchScalarGridSpec(
        num_scalar_prefetch=1, grid=(nblk,), in_specs=[pl.BlockSpec((n_layers, tr, cols), lambda i, idx: (0, i, 0))],
        out_specs=[_own_slab_spec(kind, tr, cols, nblk)] * n_layers)
    return pl.pallas_call(body, grid_spec=grid_spec, out_shape=[SDS(full_shape, dtype)] * n_layers, name=name,
                          compiler_params=_cparams(1))(my_idx, w)


def _adamw_math(w, m, v, g):
    m_new = ADAM_B1 * m + (1.0 - ADAM_B1) * g
    v_new = ADAM_B2 * v + (1.0 - ADAM_B2) * (g * g)
    m_hat = m_new / (1.0 - ADAM_B1 ** ADAM_STEP)
    v_hat = v_new / (1.0 - ADAM_B2 ** ADAM_STEP)
    delta = -ADAM_LR * (m_hat / (jnp.sqrt(v_hat) + ADAM_EPS) + ADAM_WD * w)
    return delta, m_new, v_new


def _adamw_layer(name, kind, layer, w, m, v, land, g_full, my_idx, prev, tr):
    rows2, cols = w.shape
    rows = rows2 // DEPTH
    nblk = rows // tr
    own_spec = _own_slab_spec(kind, tr, cols, nblk)
    n_prev = 0 if prev is None else 4

    def body(idx_ref, w_ref, m_ref, v_ref, land_ref, own_ref, *rest):
        g_ref, d_ref, mo_ref, vo_ref = rest[n_prev:]
        me = idx_ref[0]
        g = None
        for p in range(N_DEV):
            part = jnp.where(me == p, own_ref[...], land_ref[p]).astype(f32)
            g = part if g is None else g + part
        delta, m_new, v_new = _adamw_math(w_ref[...], m_ref[...], v_ref[...], g)
        g_ref[...] = g
        d_ref[...] = delta
        mo_ref[...] = m_new
        vo_ref[...] = v_new

    blk = pl.BlockSpec((tr, cols), lambda i, idx: (layer * nblk + i, 0))
    grid_spec = pltpu.PrefetchScalarGridSpec(
        num_scalar_prefetch=1, grid=(nblk,),
        in_specs=[blk, blk, blk, pl.BlockSpec((N_DEV, tr, cols), lambda i, idx: (0, i, 0)), own_spec] + [ANY_SPEC] * n_prev,
        out_specs=[blk, blk, blk, blk])
    aliases = {} if prev is None else {6 + k: k for k in range(4)}
    return pl.pallas_call(
        body, grid_spec=grid_spec, out_shape=[SDS((rows2, cols), f32)] * 4, name=name, input_output_aliases=aliases,
        compiler_params=_cparams(1),
    )(my_idx, w, m, v, land, g_full, *([] if prev is None else prev))


def _bucket_table():
    qi = np.arange(BLK)[:, None]
    kj = np.arange(2 * BLK)[None, :]
    dist = qi + BLK - kj
    dcl = np.clip(dist, 0, None)
    max_exact = N_BUCKETS // 2
    d_f = np.maximum(dcl, 1).astype(np.float32)
    large = max_exact + (np.log(d_f / np.float32(max_exact)) / np.float32(math.log(128 / max_exact))
                         * np.float32(N_BUCKETS - max_exact)).astype(np.int32)
    large = np.minimum(large, N_BUCKETS - 1)
    bucket = np.where(dcl < max_exact, dcl, large)
    in_window = (dist >= 0) & (dist < BLK)
    return bucket.astype(np.int32), in_window


def _onehot_buckets():
    bucket, _ = _bucket_table()
    oh = (bucket.reshape(-1)[None, :] == np.arange(N_BUCKETS)[:, None]).astype(np.float32)
    return oh


def _bias_build(rel_bias_t, onehot_t):
    def body(r_ref, o_ref, out_ref):
        out_ref[...] = jnp.dot(r_ref[...], o_ref[...], preferred_element_type=f32, precision=HIGHEST)

    tn = 4096
    return pl.pallas_call(
        body, grid=(BLK * 2 * BLK // tn,),
        in_specs=[pl.BlockSpec((NQ, N_BUCKETS), lambda i: (0, 0)), pl.BlockSpec((N_BUCKETS, tn), lambda i: (0, i))],
        out_specs=pl.BlockSpec((NQ, tn), lambda i: (0, i)), out_shape=SDS((NQ, BLK * 2 * BLK), f32), name="bias_build",
        compiler_params=_cparams(1),
    )(rel_bias_t, onehot_t)


def _bias_grad(dbias0, dbias1, onehot_t):
    tn = 4096
    nsteps = BLK * 2 * BLK // tn

    def body(a_ref, b_ref, o_ref, out_ref):
        part = lax.dot_general(a_ref[...] + b_ref[...], o_ref[...], NT_DIMS, preferred_element_type=f32, precision=HIGHEST)

        @pl.when(pl.program_id(0) == 0)
        def _():
            out_ref[...] = part

        @pl.when(pl.program_id(0) > 0)
        def _():
            out_ref[...] += part

    return pl.pallas_call(
        body, grid=(nsteps,),
        in_specs=[pl.BlockSpec((NQ, tn), lambda i: (0, i)), pl.BlockSpec((NQ, tn), lambda i: (0, i)),
                  pl.BlockSpec((N_BUCKETS, tn), lambda i: (0, i))],
        out_specs=pl.BlockSpec((NQ, N_BUCKETS), lambda i: (0, 0)), out_shape=SDS((NQ, N_BUCKETS), f32), name="bias_grad",
        compiler_params=_cparams(1),
    )(dbias0, dbias1, onehot_t)


def _attn_mask(n):
    qi = lax.broadcasted_iota(jnp.int32, (BLK, 2 * BLK), 0)
    kj = lax.broadcasted_iota(jnp.int32, (BLK, 2 * BLK), 1)
    dist = qi + BLK - kj
    first_key = jnp.where(n > 0, 0, BLK)
    return (dist >= 0) & (dist < BLK) & (kj >= first_key)


def _head_norm(t, gain):
    r = lax.rsqrt(jnp.mean(t * t, axis=-1, keepdims=True) + EPS)
    that = t * r
    return that, r, that * gain


def _softmax_with_sink(s, sink):
    m = jnp.maximum(jnp.max(s, axis=-1, keepdims=True), sink)
    p = jnp.exp(s - m)
    psink = jnp.exp(sink - m)
    inv = 1.0 / (jnp.sum(p, axis=-1, keepdims=True) + psink)
    return p * inv, psink * inv


GQ = NQ // NKV


def _group_rows(x_ref, sk_ref, layer, j):
    heads = [GQ * j + g for g in range(GQ)]
    xs = jnp.concatenate([x_ref[:, pl.ds(HD * h, HD)] for h in heads], axis=0)
    sink = jnp.concatenate([jnp.broadcast_to(sk_ref[layer:layer + 1, h:h + 1], (BLK, 1)) for h in heads], axis=0)
    return xs, sink


def _attn_fwd(qkv, q_gain, k_gain, sinks, bias, layer):
    def body(q_ref, kc_ref, kp_ref, vc_ref, vp_ref, qg_ref, kg_ref, sk_ref, bias_ref, o_ref):
        n = pl.program_id(0)
        mask = jnp.tile(_attn_mask(n), (GQ, 1))
        qg = qg_ref[layer:layer + 1, :]
        kg = kg_ref[layer:layer + 1, :]
        grp = range(NKV)
        kbs = [jnp.concatenate([kp_ref[:, pl.ds(HD * j, HD)], kc_ref[:, pl.ds(HD * j, HD)]], axis=0) for j in grp]
        vbs = [jnp.concatenate([vp_ref[:, pl.ds(HD * j, HD)], vc_ref[:, pl.ds(HD * j, HD)]], axis=0).astype(bf16) for j in grp]
        kn_b = [_head_norm(kbs[j], kg)[2].astype(bf16) for j in grp]
        rows = [_group_rows(q_ref, sk_ref, layer, j) for j in grp]
        qn_b = [_head_norm(rows[j][0], qg)[2].astype(bf16) for j in grp]
        ss = [_dot(qn_b[j], kn_b[j], NT_DIMS) * (HD ** -0.5) + bias_ref[GQ * j:GQ * (j + 1)].reshape(GQ * BLK, 2 * BLK) for j in grp]
        ps = [_softmax_with_sink(jnp.where(mask, ss[j], -jnp.inf), rows[j][1])[0] for j in grp]
        outs = [_dot(ps[j].astype(bf16), vbs[j], NN_DIMS).astype(bf16) for j in grp]
        for j in grp:
            for g in range(GQ):
                o_ref[:, pl.ds(HD * (GQ * j + g), HD)] = outs[j][BLK * g:BLK * (g + 1), :]

    prev = lambda n: jnp.maximum(n - 1, 0)
    small = lambda shape: pl.BlockSpec(shape, lambda n: (0,) * len(shape))
    return pl.pallas_call(
        body, grid=(NBLK,),
        in_specs=[pl.BlockSpec((BLK, D_ATTN), lambda n: (n, 0)),
                  pl.BlockSpec((BLK, 128), lambda n: (n, 4)), pl.BlockSpec((BLK, 128), lambda n: (prev(n), 4)),
                  pl.BlockSpec((BLK, 128), lambda n: (n, 5)), pl.BlockSpec((BLK, 128), lambda n: (prev(n), 5)),
                  small((DEPTH, HD)), small((DEPTH, HD)), small((DEPTH, NQ)), small((NQ, BLK, 2 * BLK))],
        out_specs=pl.BlockSpec((BLK, D_ATTN), lambda n: (n, 0)), out_shape=SDS((S, D_ATTN), bf16),
        name="attn_fwd", compiler_params=_cparams(1),
    )(qkv, qkv, qkv, qkv, qkv, q_gain, k_gain, sinks, bias)


def _attn_bwd(qkv, dmix, q_gain, k_gain, sinks, bias, layer):
    def body(q_ref, kc_ref, kp_ref, vc_ref, vp_ref, do_ref, qg_ref, kg_ref, sk_ref, bias_ref,
             dqkv_ref, dbias_ref, dsm_ref, carry):
        i = pl.program_id(0)
        n = NBLK - 1 - i
        mask = jnp.tile(_attn_mask(n), (GQ, 1))
        qg = qg_ref[layer:layer + 1, :]
        kg = kg_ref[layer:layer + 1, :]
        lane = lax.broadcasted_iota(jnp.int32, (1, 128), 1)

        @pl.when(i == 0)
        def _():
            carry[...] = jnp.zeros_like(carry)
            dbias_ref[...] = jnp.zeros_like(dbias_ref)
            dsm_ref[...] = jnp.zeros_like(dsm_ref)

        grp = range(NKV)
        kbs = [jnp.concatenate([kp_ref[:, pl.ds(HD * j, HD)], kc_ref[:, pl.ds(HD * j, HD)]], axis=0) for j in grp]
        vbs = [jnp.concatenate([vp_ref[:, pl.ds(HD * j, HD)], vc_ref[:, pl.ds(HD * j, HD)]], axis=0).astype(bf16) for j in grp]
        knorm = [_head_norm(kbs[j], kg) for j in grp]
        kn_b = [knorm[j][2].astype(bf16) for j in grp]
        rows = [_group_rows(q_ref, sk_ref, layer, j) for j in grp]
        qnorm = [_head_norm(rows[j][0], qg) for j in grp]
        qn_b = [qnorm[j][2].astype(bf16) for j in grp]
        ss = [_dot(qn_b[j], kn_b[j], NT_DIMS) * (HD ** -0.5) + bias_ref[GQ * j:GQ * (j + 1)].reshape(GQ * BLK, 2 * BLK) for j in grp]
        sm = [_softmax_with_sink(jnp.where(mask, ss[j], -jnp.inf), rows[j][1]) for j in grp]
        do_b = [jnp.concatenate([do_ref[:, pl.ds(HD * (GQ * j + g), HD)] for g in range(GQ)], axis=0).astype(bf16) for j in grp]
        dps = [_dot(do_b[j], vbs[j], NT_DIMS) for j in grp]
        deltas = [jnp.sum(sm[j][0] * dps[j], axis=-1, keepdims=True) for j in grp]
        dss = [sm[j][0] * (dps[j] - deltas[j]) for j in grp]
        ds_b = [(dss[j] * (HD ** -0.5)).astype(bf16) for j in grp]
        dqn = [_dot(ds_b[j], kn_b[j], NN_DIMS) for j in grp]
        dkn = [_dot(ds_b[j], qn_b[j], TN_DIMS) for j in grp]
        dvs = [_dot(sm[j][0].astype(bf16), do_b[j], TN_DIMS) for j in grp]
        dqg = jnp.zeros((1, HD), f32)
        dkg = jnp.zeros((1, HD), f32)
        dsink = jnp.zeros((1, 128), f32)
        for j in grp:
            dbias_ref[GQ * j:GQ * (j + 1)] += dss[j].reshape(GQ, BLK, 2 * BLK)
            dsk = sm[j][1] * deltas[j]
            for g in range(GQ):
                dsink = dsink + jnp.where(lane == GQ * j + g, -_sum11(dsk[BLK * g:BLK * (g + 1), :]), 0.0)
            qhat, rq, _ = qnorm[j]
            w = dqn[j] * qg
            dq = rq * (w - qhat * jnp.mean(qhat * w, axis=-1, keepdims=True))
            for g in range(GQ):
                dqkv_ref[:, pl.ds(HD * (GQ * j + g), HD)] = dq[BLK * g:BLK * (g + 1), :].astype(bf16)
            dqg = dqg + jnp.sum(dqn[j] * qhat, axis=0, keepdims=True)
            khat, rk, _ = knorm[j]
            w = dkn[j] * kg
            dk = rk * (w - khat * jnp.mean(khat * w, axis=-1, keepdims=True))
            dkg = dkg + jnp.sum(dkn[j] * khat, axis=0, keepdims=True)
            dqkv_ref[:, pl.ds(D_ATTN + HD * j, HD)] = (dk[BLK:, :] + carry[:, pl.ds(HD * j, HD)]).astype(bf16)
            dqkv_ref[:, pl.ds(D_ATTN + 128 + HD * j, HD)] = (dvs[j][BLK:, :] + carry[:, pl.ds(128 + HD * j, HD)]).astype(bf16)
            carry[:, pl.ds(HD * j, HD)] = dk[:BLK, :]
            carry[:, pl.ds(128 + HD * j, HD)] = dvs[j][:BLK, :]
        dsm_ref[0:1, 0:HD] += dqg
        dsm_ref[1:2, 0:HD] += dkg
        dsm_ref[2:3, :] += dsink

    rev = lambda i: NBLK - 1 - i
    prev = lambda i: jnp.maximum(NBLK - 2 - i, 0)
    small = lambda shape: pl.BlockSpec(shape, lambda i: (0,) * len(shape))
    return pl.pallas_call(
        body, grid=(NBLK,),
        in_specs=[pl.BlockSpec((BLK, D_ATTN), lambda i: (rev(i), 0)),
                  pl.BlockSpec((BLK, 128), lambda i: (rev(i), 4)), pl.BlockSpec((BLK, 128), lambda i: (prev(i), 4)),
                  pl.BlockSpec((BLK, 128), lambda i: (rev(i), 5)), pl.BlockSpec((BLK, 128), lambda i: (prev(i), 5)),
                  pl.BlockSpec((BLK, D_ATTN), lambda i: (rev(i), 0)),
                  small((DEPTH, HD)), small((DEPTH, HD)), small((DEPTH, NQ)), small((NQ, BLK, 2 * BLK))],
        out_specs=[pl.BlockSpec((BLK, 768), lambda i: (rev(i), COL_QKV // 768)), small((NQ, BLK, 2 * BLK)), small((8, 128))],
        out_shape=[SDS((S, D_IN_PAD), bf16), SDS((NQ, BLK, 2 * BLK), f32), SDS((8, 128), f32)],
        scratch_shapes=[pltpu.VMEM((BLK, 256), f32)], name="attn_bwd", compiler_params=_cparams(1),
    )(qkv, qkv, qkv, qkv, qkv, dmix, q_gain, k_gain, sinks, bias)


CONV_TC = 128


def _shift_down(u, s):
    if s == 0:
        return u
    rows = lax.broadcasted_iota(jnp.int32, u.shape, 0)
    return jnp.where(rows >= s, pltpu.roll(u, s, 0), 0.0)


def _shift_up(u, s):
    if s == 0:
        return u
    rows = lax.broadcasted_iota(jnp.int32, u.shape, 0)
    return jnp.where(rows < u.shape[0] - s, pltpu.roll(u, u.shape[0] - s, 0), 0.0)


def _conv_specs():
    return [pl.BlockSpec((S, CONV_TC), lambda c: (0, c)),
            pl.BlockSpec((None, 4, CONV_TC), lambda c: (0, 0, c)),
            pl.BlockSpec((DEPTH, CONV_TC), lambda c: (0, c))]


def _conv_pre(u, w_ref, b_ref, layer):
    pre = b_ref[layer:layer + 1, :] + w_ref[3:4, :] * u
    for k in range(3):
        pre = pre + w_ref[k:k + 1, :] * _shift_down(u, 3 - k)
    return pre


def _conv_fwd(xbc, conv_w, conv_b, layer):
    def body(u_ref, w_ref, b_ref, o_ref):
        pre = _conv_pre(u_ref[...], w_ref, b_ref, layer)
        o_ref[...] = pre * _sigmoid(pre)

    specs = _conv_specs()
    specs[1] = pl.BlockSpec((None, 4, CONV_TC), lambda c: (layer, 0, c))
    return pl.pallas_call(
        body, grid=(D_CONV // CONV_TC,), in_specs=specs, out_specs=pl.BlockSpec((S, CONV_TC), lambda c: (0, c)),
        out_shape=SDS((S, D_CONV), f32), name="conv_fwd", compiler_params=_cparams(1),
    )(xbc, conv_w, conv_b)


def _conv_bwd(xbc, dact, conv_w, conv_b, dproj, layer):
    def body(u_ref, w_ref, b_ref, da_ref, dproj_in, du_ref, dw_ref, db_ref):
        u = u_ref[...]
        pre = _conv_pre(u, w_ref, b_ref, layer)
        sg = _sigmoid(pre)
        dpre = da_ref[...] * (sg * (1.0 + pre * (1.0 - sg)))
        du = w_ref[3:4, :] * dpre
        for k in range(3):
            du = du + w_ref[k:k + 1, :] * _shift_up(dpre, 3 - k)
        du_ref[...] = du.astype(bf16)
        db_ref[...] = jnp.broadcast_to(jnp.sum(dpre, axis=0, keepdims=True), db_ref.shape)
        dw_ref[...] = jnp.zeros_like(dw_ref)
        for k in range(4):
            dw_ref[k:k + 1, :] = jnp.sum(dpre * _shift_down(u, 3 - k), axis=0, keepdims=True)

    specs = _conv_specs()
    specs[1] = pl.BlockSpec((None, 4, CONV_TC), lambda c: (layer, 0, c))
    col = pl.BlockSpec((S, CONV_TC), lambda c: (0, c))
    row8 = pl.BlockSpec((8, CONV_TC), lambda c: (0, c))
    return pl.pallas_call(
        body, grid=(D_CONV // CONV_TC,), in_specs=[*specs, col, ANY_SPEC],
        out_specs=[pl.BlockSpec((S, CONV_TC), lambda c: (0, COL_XBC // CONV_TC + c)), row8, row8],
        out_shape=[SDS((S, D_IN_PAD), bf16), SDS((8, D_CONV), f32), SDS((8, D_CONV), f32)], name="conv_bwd",
        input_output_aliases={4: 0}, compiler_params=_cparams(1),
    )(xbc, conv_w, conv_b, dact, dproj)


def _tri():
    return (lax.broadcasted_iota(jnp.int32, (BLK, BLK), 0) >= lax.broadcasted_iota(jnp.int32, (BLK, BLK), 1))


def _ssd_scalars(dt_ref, dtb_ref, alog_ref, layer):
    raw = dt_ref[:, 0:NSSM] + dtb_ref[layer:layer + 1, :]
    dtv = jnp.maximum(raw, 0.0) + jnp.log(1.0 + jnp.exp(-jnp.abs(raw)))
    a = -jnp.exp(alog_ref[layer:layer + 1, :])
    acs = jnp.dot(_tri().astype(f32), dtv * a, preferred_element_type=f32, precision=HIGHEST)
    return raw, dtv, a, acs


HG = NSSM // NGRP
GW = HG * HD


def _lane_expand(cols, g):
    lane_head = lax.broadcasted_iota(jnp.int32, (1, GW), 1) // HD
    out = cols[:, HG * g + HG - 1:HG * g + HG]
    for r in range(HG - 2, -1, -1):
        out = jnp.where(lane_head == r, cols[:, HG * g + r:HG * g + r + 1], out)
    return out


def _row_expand(vals, g):
    row_head = lax.broadcasted_iota(jnp.int32, (GW, 1), 0) // HD
    out = vals[:, HG * g + HG - 1:HG * g + HG]
    for r in range(HG - 2, -1, -1):
        out = jnp.where(row_head == r, vals[:, HG * g + r:HG * g + r + 1], out)
    return out


def _head_rowsums(a):
    sel = (lax.broadcasted_iota(jnp.int32, (GW, HG), 0) // HD == lax.broadcasted_iota(jnp.int32, (GW, HG), 1)).astype(bf16)
    hi = a.astype(bf16)
    lo = (a - hi.astype(f32)).astype(bf16)
    sums = _dot(hi, sel, NN_DIMS) + _dot(lo, sel, NN_DIMS)
    return [sums[:, r:r + 1] for r in range(HG)]


def _ssd_chunk_common(xc_ref, dt_ref, dtb_ref, alog_ref, h_rows, layer):
    raw, dtv, a, acs = _ssd_scalars(dt_ref, dtb_ref, alog_ref, layer)
    acs_t = acs.T
    last = acs[BLK - 1:BLK, :]
    c = dict(raw=raw, dtv=dtv, a=a, acs=acs, last=last, dte=jnp.exp(last - acs), e_all=jnp.exp(acs), cd=jnp.exp(last))
    grp, heads, tri = range(NGRP), range(NSSM), _tri()
    c["bm"] = [xc_ref[:, pl.ds(D_SSM + NSTATE * g, NSTATE)] for g in grp]
    c["bm_b"] = [c["bm"][g].astype(bf16) for g in grp]
    c["cm_b"] = [xc_ref[:, pl.ds(D_SSM + NGRP * NSTATE + NSTATE * g, NSTATE)].astype(bf16) for g in grp]
    c["cb"] = [_dot(c["cm_b"][g], c["bm_b"][g], NT_DIMS) for g in grp]
    c["x"] = [xc_ref[:, pl.ds(GW * g, GW)] for g in grp]
    c["dt"] = [_lane_expand(dtv, g) for g in grp]
    c["xdt"] = [c["x"][g] * c["dt"][g] for g in grp]
    c["xdt_b"] = [c["xdt"][g].astype(bf16) for g in grp]
    c["prev"] = [h_rows(g) for g in grp]
    c["prev_b"] = [c["prev"][g].astype(bf16) for g in grp]
    c["e"] = [_lane_expand(c["e_all"], g) for g in grp]
    c["y_off"] = [_dot(c["cm_b"][g], c["prev_b"][g], NT_DIMS) * c["e"][g] for g in grp]
    c["decay"] = [jnp.exp(jnp.where(tri, acs[:, h:h + 1] - acs_t[h:h + 1, :], -jnp.inf)) for h in heads]
    c["m"] = [c["cb"][h // HG] * c["decay"][h] for h in heads]
    c["m_b"] = [c["m"][h].astype(bf16) for h in heads]
    c["dte_x"] = [_lane_expand(c["dte"], g) for g in grp]
    c["xdte_b"] = [(c["xdt"][g] * c["dte_x"][g]).astype(bf16) for g in grp]
    return c


def _ssd_fwd(xact, z, dt, attn, dt_bias, a_log, d_skip, norm_g, layer):
    def body(xc_ref, z_ref, dt_ref, at_ref, dtb_ref, alog_ref, dsk_ref, ng_ref, mix_ref, hs_ref, y_ref, h_ref):
        n = pl.program_id(0)

        @pl.when(n == 0)
        def _():
            h_ref[...] = jnp.zeros_like(h_ref)

        hs_ref[...] = h_ref[...]
        c = _ssd_chunk_common(xc_ref, dt_ref, dtb_ref, alog_ref, lambda g: h_ref[pl.ds(GW * g, GW), :], layer)
        grp, heads = range(NGRP), range(NSSM)
        y_diag = [_dot(c["m_b"][h], c["xdt_b"][h // HG][:, HD * (h % HG):HD * (h % HG + 1)], NN_DIMS) for h in heads]
        new_st = [_dot(c["xdte_b"][g], c["bm_b"][g], TN_DIMS) for g in grp]
        for h in heads:
            y_ref[:, pl.ds(HD * h, HD)] = y_diag[h]
        dskip = dsk_ref[layer:layer + 1, :]
        for g in grp:
            cols = pl.ds(GW * g, GW)
            y_ref[:, cols] = y_ref[:, cols] + c["y_off"][g] + c["x"][g] * _lane_expand(dskip, g)
            h_ref[cols, :] = c["prev"][g] * _row_expand(c["cd"], g) + new_st[g]
        zv = z_ref[...]
        yz = y_ref[...] * (zv * _sigmoid(zv))
        mix_ref[:, 0:D_ATTN] = at_ref[...]
        for g in grp:
            yg = yz[:, GW * g:GW * (g + 1)]
            rs = lax.rsqrt(jnp.mean(yg * yg, axis=-1, keepdims=True) + EPS)
            mix_ref[:, D_ATTN + GW * g:D_ATTN + GW * (g + 1)] = (yg * rs * ng_ref[layer:layer + 1, GW * g:GW * (g + 1)]).astype(bf16)

    small = lambda shape: pl.BlockSpec(shape, lambda n: (0,) * len(shape))
    return pl.pallas_call(
        body, grid=(NBLK,),
        in_specs=[pl.BlockSpec((BLK, D_CONV), lambda n: (n, 0)), pl.BlockSpec((BLK, D_SSM), lambda n: (n, 0)),
                  pl.BlockSpec((BLK, 128), lambda n: (n, 0)), pl.BlockSpec((BLK, D_ATTN), lambda n: (n, 0)),
                  small((DEPTH, NSSM)), small((DEPTH, NSSM)), small((DEPTH, NSSM)), small((DEPTH, D_SSM))],
        out_specs=[pl.BlockSpec((BLK, D), lambda n: (n, 0)), pl.BlockSpec((None, NSSM * HD, NSTATE), lambda n: (n, 0, 0)),
                   pl.BlockSpec((BLK, D_SSM), lambda n: (n, 0))],
        out_shape=[SDS((S, D), bf16), SDS((NBLK, NSSM * HD, NSTATE), f32), SDS((S, D_SSM), f32)],
        scratch_shapes=[pltpu.VMEM((NSSM * HD, NSTATE), f32)],
        name="ssd_fwd", compiler_params=_cparams(1),
    )(xact, z, dt, attn, dt_bias, a_log, d_skip, norm_g)


def _ssd_bwd(xact, z, dt, dmix, hs, y, dt_bias, a_log, d_skip, norm_g, dproj, layer):
    def body(xc_ref, z_ref, dt_ref, do_ref, hs_ref, y_ref, dtb_ref, alog_ref, dsk_ref, ng_ref, dproj_in,
             dzdt_ref, dx_ref, dsm_ref, dh_ref, dy_ref):
        i = pl.program_id(0)

        @pl.when(i == 0)
        def _():
            dh_ref[...] = jnp.zeros_like(dh_ref)
            dsm_ref[...] = jnp.zeros_like(dsm_ref)

        c = _ssd_chunk_common(xc_ref, dt_ref, dtb_ref, alog_ref, lambda g: hs_ref[pl.ds(GW * g, GW), :], layer)
        raw, dtv, a = c["raw"], c["dtv"], c["a"]
        grp, heads = range(NGRP), range(NSSM)
        dskip = dsk_ref[layer:layer + 1, :]
        lane8 = lax.broadcasted_iota(jnp.int32, (1, NSSM), 1)
        sub8 = lax.broadcasted_iota(jnp.int32, (NSSM, 1), 0)

        zv = z_ref[...]
        sz = _sigmoid(zv)
        gz = zv * sz
        yv = y_ref[...]
        yz = yv * gz
        for g in grp:
            sl = slice(GW * g, GW * (g + 1))
            yg = yz[:, sl]
            rs = lax.rsqrt(jnp.mean(yg * yg, axis=-1, keepdims=True) + EPS)
            yhat = yg * rs
            dog = do_ref[:, sl]
            w = dog * ng_ref[layer:layer + 1, sl]
            dyz = rs * (w - yhat * jnp.mean(yhat * w, axis=-1, keepdims=True))
            dsm_ref[0:1, sl] += jnp.sum(dog * yhat, axis=0, keepdims=True)
            dy_ref[:, sl] = dyz * gz[:, sl]
            dzdt_ref[:, sl] = (dyz * yv[:, sl] * (sz[:, sl] * (1.0 + zv[:, sl] * (1.0 - sz[:, sl])))).astype(bf16)

        dy = [dy_ref[:, pl.ds(GW * g, GW)] for g in grp]
        dy_b = [dy[g].astype(bf16) for g in grp]
        hl = lambda h: slice(HD * (h % HG), HD * (h % HG + 1))
        dt_off_b = [(dy[g] * c["e"][g]).astype(bf16) for g in grp]
        dcm = [_dot(dt_off_b[g], c["prev_b"][g], NN_DIMS) for g in grp]
        dprev = [_dot(dt_off_b[g], c["cm_b"][g], TN_DIMS) for g in grp]
        yoff_rs = [_head_rowsums(dy[g] * c["y_off"][g]) for g in grp]
        dhn = [dh_ref[pl.ds(GW * g, GW), :] for g in grp]
        dhn_b = [dhn[g].astype(bf16) for g in grp]
        dprev = [dprev[g] + dhn[g] * _row_expand(c["cd"], g) for g in grp]
        dhn_prev = [dhn[g] * c["prev"][g] for g in grp]
        u = [_dot(c["bm_b"][g], dhn_b[g], NT_DIMS) for g in grp]
        dbm = [_dot(c["xdte_b"][g], dhn_b[g], NN_DIMS) for g in grp]
        ddte_rs = [_head_rowsums(c["xdt"][g] * u[g]) for g in grp]
        dm = [_dot(dy_b[h // HG][:, hl(h)], c["xdt_b"][h // HG][:, hl(h)], NT_DIMS) for h in heads]
        dxdt_in = [_dot(c["m_b"][h], dy_b[h // HG][:, hl(h)], TN_DIMS) for h in heads]
        dseg = [dm[h] * c["m"][h] for h in heads]
        dmd = [dm[h] * c["decay"][h] for h in heads]
        for h in heads:
            dx_ref[:, pl.ds(HD * h, HD)] = dxdt_in[h]

        dacs = jnp.zeros((BLK, NSSM), f32)
        dacs_cols = jnp.zeros((NSSM, BLK), f32)
        dlast = jnp.zeros((1, NSSM), f32)
        ddtv = jnp.zeros((BLK, NSSM), f32)
        ddsk = jnp.zeros((1, NSSM), f32)
        for g in grp:
            cols = pl.ds(GW * g, GW)
            dxdt = dx_ref[:, cols] + u[g] * c["dte_x"][g]
            dx_ref[:, cols] = dy[g] * _lane_expand(dskip, g) + dxdt * c["dt"][g]
            ddtv_rs = _head_rowsums(dxdt * c["x"][g])
            ddsk_rs = _head_rowsums(dy[g] * c["x"][g])
            dcb = dmd[HG * g]
            for r in range(1, HG):
                dcb = dcb + dmd[HG * g + r]
            dcb_b = dcb.astype(bf16)
            dx_ref[:, pl.ds(D_SSM + NSTATE * g, NSTATE)] = dbm[g] + _dot(dcb_b, c["cm_b"][g], TN_DIMS)
            dx_ref[:, pl.ds(D_SSM + NGRP * NSTATE + NSTATE * g, NSTATE)] = dcm[g] + _dot(dcb_b, c["bm_b"][g], NN_DIMS)
            dh_ref[cols, :] = dprev[g]
            for r in range(HG):
                h = HG * g + r
                oh = (lane8 == h).astype(f32)
                tmp = ddte_rs[g][r] * c["dte"][:, h:h + 1]
                dacs = dacs + oh * (jnp.sum(dseg[h], axis=1, keepdims=True) + yoff_rs[g][r] - tmp)
                dacs_cols = dacs_cols + (sub8 == h).astype(f32) * jnp.sum(dseg[h], axis=0, keepdims=True)
                dlast = dlast + oh * (_sum11(dhn_prev[g][HD * r:HD * (r + 1), :]) * c["cd"][:, h:h + 1] + _sum11(tmp))
                ddtv = ddtv + oh * ddtv_rs[r]
                ddsk = ddsk + oh * _sum11(ddsk_rs[r])

        row = lax.broadcasted_iota(jnp.int32, (BLK, 1), 0)
        dacs = dacs - dacs_cols.T + jnp.where(row == BLK - 1, dlast, 0.0)
        dda = lax.dot_general(_tri().astype(f32), dacs, TN_DIMS, preferred_element_type=f32, precision=HIGHEST)
        ddtv = ddtv + dda * a
        da = jnp.sum(dda * dtv, axis=0, keepdims=True)
        draw = ddtv * _sigmoid(raw)
        dzdt_ref[:, D_SSM:] = jnp.zeros((BLK, COL_XBC - COL_DT), bf16)
        dzdt_ref[:, D_SSM:D_SSM + NSSM] = draw.astype(bf16)
        dsm_ref[1:2, 0:NSSM] += jnp.sum(draw, axis=0, keepdims=True)
        dsm_ref[2:3, 0:NSSM] += da * a
        dsm_ref[3:4, 0:NSSM] += ddsk

    rev = lambda i: NBLK - 1 - i
    small = lambda shape: pl.BlockSpec(shape, lambda i: (0,) * len(shape))
    return pl.pallas_call(
        body, grid=(NBLK,),
        in_specs=[pl.BlockSpec((BLK, D_CONV), lambda i: (rev(i), 0)), pl.BlockSpec((BLK, D_SSM), lambda i: (rev(i), 0)),
                  pl.BlockSpec((BLK, 128), lambda i: (rev(i), 0)), pl.BlockSpec((BLK, D_SSM), lambda i: (rev(i), 1)),
                  pl.BlockSpec((None, NSSM * HD, NSTATE), lambda i: (rev(i), 0, 0)), pl.BlockSpec((BLK, D_SSM), lambda i: (rev(i), 0)),
                  small((DEPTH, NSSM)), small((DEPTH, NSSM)), small((DEPTH, NSSM)), small((DEPTH, D_SSM)), ANY_SPEC],
        out_specs=[pl.BlockSpec((BLK, COL_XBC - COL_Z), lambda i: (rev(i), COL_Z // (COL_XBC - COL_Z))),
                   pl.BlockSpec((BLK, D_CONV), lambda i: (rev(i), 0)), small((8, D_SSM))],
        out_shape=[SDS((S, D_IN_PAD), bf16), SDS((S, D_CONV), f32), SDS((8, D_SSM), f32)],
        scratch_shapes=[pltpu.VMEM((NSSM * HD, NSTATE), f32), pltpu.VMEM((BLK, D_SSM), f32)],
        name="ssd_bwd", input_output_aliases={10: 0}, compiler_params=_cparams(1),
    )(xact, z, dt, dmix, hs, y, dt_bias, a_log, d_skip, norm_g, dproj)


def _my_place():
    return lax.axis_index("x"), lax.axis_index("y"), lax.axis_index("c")


def _dev_index(px, py, pc):
    return 4 * px + 2 * py + pc


def _slab2(kind, ref, idx):
    if kind == "stack":
        return ref.at[idx]
    if kind == "rows128":
        return ref.at[pl.ds(pl.multiple_of(idx * 128, 128), 128), :]
    if kind == "rows512":
        return ref.at[pl.ds(pl.multiple_of(idx * 512, 512), 512), :]
    return ref.at[:, pl.ds(pl.multiple_of(idx * 512, 512), 512)]


def _slab_shape(kind, full_shape):
    if kind == "stack":
        return tuple(full_shape[1:])
    if kind == "rows128":
        return (128, full_shape[1])
    if kind == "rows512":
        return (512, full_shape[1])
    return (full_shape[0], 512)


KIND = dict(w_in="stack", w_out="rows128", w_up="cols512", w_down="rows512", conv_w="stack")
FULL_SHAPE = dict(w_in=(N_DEV, D, D_IN // N_DEV), w_out=(D, D), w_up=(D, D_FF), w_down=(D_FF, D))
HBM_SPEC = pl.BlockSpec(memory_space=pltpu.HBM)
SEM_SPEC = pl.BlockSpec(memory_space=pltpu.SEMAPHORE)
SIDE_EFFECT = pltpu.SideEffectType.DATAFLOW_SIDE_EFFECTING


def _peers_all():
    x, y, c = _my_place()
    return [(x ^ ((r >> 2) & 1), y ^ ((r >> 1) & 1), c ^ (r & 1)) for r in range(1, N_DEV)]


def _split_start(name, bufs, n_copies, plan, deps=()):
    nb = len(bufs)

    def body(*refs):
        ins = refs[:nb]
        send_sems, recv_sems = refs[nb + len(deps)], refs[nb + len(deps) + 1]
        token = refs[-1]
        for i, (src, dst, dev) in enumerate(plan(ins)):
            pltpu.make_async_remote_copy(src_ref=src, dst_ref=dst, send_sem=send_sems.at[i], recv_sem=recv_sems.at[i],
                                         device_id=dev, device_id_type=MESH).start()
        token[...] = jnp.zeros_like(token)

    outs = pl.pallas_call(
        body, name=name,
        out_shape=(pltpu.SemaphoreType.DMA((n_copies,)), pltpu.SemaphoreType.DMA((n_copies,)),
                   *[pltpu.HBM(b.shape, b.dtype) for b in bufs], SDS((8, 128), f32)),
        in_specs=[HBM_SPEC] * nb + [ANY_SPEC] * len(deps),
        out_specs=(SEM_SPEC, SEM_SPEC, *[HBM_SPEC] * nb, pl.BlockSpec(memory_space=pltpu.VMEM)),
        input_output_aliases={i: 2 + i for i in range(nb)},
        compiler_params=pltpu.CompilerParams(has_side_effects=SIDE_EFFECT),
    )(*[pltpu.with_memory_space_constraint(b, pltpu.HBM) for b in bufs], *deps)
    return dict(send=outs[0], recv=outs[1], bufs=list(outs[2:2 + nb]), token=outs[-1], plan=plan, n=n_copies)


def _split_wait(name, started, after):
    bufs = started["bufs"]
    nb = len(bufs)
    plan = started["plan"]

    def body(*refs):
        ins = refs[:nb]
        send_sems, recv_sems = refs[nb], refs[nb + 1]
        for i, (src, dst, dev) in enumerate(plan(ins)):
            cp = pltpu.make_async_remote_copy(src_ref=src, dst_ref=dst, send_sem=send_sems.at[i], recv_sem=recv_sems.at[i],
                                              device_id=dev, device_id_type=MESH)
            cp.wait_send()
            cp.wait_recv()

    outs = pl.pallas_call(
        body, name=name, out_shape=tuple(pltpu.HBM(b.shape, b.dtype) for b in bufs),
        in_specs=[HBM_SPEC] * nb + [SEM_SPEC, SEM_SPEC] + [ANY_SPEC] * len(after), out_specs=(HBM_SPEC,) * nb,
        input_output_aliases={i: i for i in range(nb)},
        compiler_params=pltpu.CompilerParams(has_side_effects=SIDE_EFFECT),
    )(*bufs, started["send"], started["recv"], *after)
    return list(outs)


def _gather_start(name, names, fulls, deps):
    n_t = len(names)

    def plan(refs):
        x, y, c = _my_place()
        my_idx = _dev_index(x, y, c)
        targets = [(x, y, 1 - c), (1 - x, y, c), (x, 1 - y, c), (1 - x, 1 - y, c)]
        slabs = [_slab2(KIND[names[t]], refs[t], my_idx) for t in range(n_t)]
        return [(slabs[t], slabs[t], dev) for t in range(n_t) for dev in targets]

    return _split_start(name, list(fulls), 4 * n_t, plan, deps)


def _gather_finish(name, names, started, after):
    n_t = len(names)
    fulls = _split_wait(name + "_wait", started, after)
    slab_shapes = [SDS(_slab_shape(KIND[n], f.shape), f.dtype) for n, f in zip(names, fulls)]

    def body(*refs):
        ins = refs[:n_t]
        outs = refs[n_t:2 * n_t]
        stage = refs[2 * n_t:3 * n_t]
        load_sems, send_sems, recv_sems = refs[3 * n_t:]
        x, y, c = _my_place()
        chips = [(1 - x, y), (x, 1 - y), (1 - x, 1 - y)]
        pairs = [(t, j) for t in range(n_t) for j in range(3)]
        loads = [pltpu.make_async_copy(_slab2(KIND[names[t]], ins[t], _dev_index(*chips[j], c)), stage[t].at[j], load_sems.at[t, j])
                 for t, j in pairs]
        for cp in loads:
            cp.start()

        def copy(t, j, core):
            return pltpu.make_async_remote_copy(
                src_ref=stage[t].at[j], dst_ref=_slab2(KIND[names[t]], outs[t], _dev_index(*chips[j], core)),
                send_sem=send_sems.at[t, j], recv_sem=recv_sems.at[t, j], device_id=(x, y, 1 - c), device_id_type=MESH)

        sends = [copy(t, j, c) for t, j in pairs]
        for ld, cp in zip(loads, sends):
            ld.wait()
            cp.start()
        for t, j in pairs:
            copy(t, j, 1 - c).wait_recv()
        for cp in sends:
            cp.wait_send()

    return pl.pallas_call(
        body, in_specs=[ANY_SPEC] * n_t, out_specs=[ANY_SPEC] * n_t, out_shape=[SDS(b.shape, b.dtype) for b in fulls],
        input_output_aliases={t: t for t in range(n_t)},
        scratch_shapes=[pltpu.VMEM((3,) + s.shape, s.dtype) for s in slab_shapes]
        + [pltpu.SemaphoreType.DMA((n_t, 3)), pltpu.SemaphoreType.DMA((n_t, 3)), pltpu.SemaphoreType.DMA((n_t, 3))],
        name=name + "_pass", compiler_params=pltpu.CompilerParams(vmem_limit_bytes=VMEM_LIMIT),
    )(*fulls)


def _exchange_start(name, names, grads, deps):
    n_t = len(names)
    lands = [lax.empty((N_DEV,) + _slab_shape(KIND[n], g.shape), g.dtype) for n, g in zip(names, grads)]

    def plan(refs):
        my_idx = _dev_index(*_my_place())
        return [(_slab2(KIND[names[t]], refs[t], _dev_index(*peer)), refs[n_t + t].at[my_idx], peer)
                for t in range(n_t) for peer in _peers_all()]

    return _split_start(name, list(grads) + lands, 7 * n_t, plan, deps)


def _small_exchange_start(part, deps):
    land = lax.empty((N_DEV,) + part.shape, part.dtype)

    def plan(refs):
        my_idx = _dev_index(*_my_place())
        return [(refs[0], refs[1].at[my_idx], peer) for peer in _peers_all()]

    return _split_start("small_exchange", [part, land], N_DEV - 1, plan, deps)


def _slab_pieces():
    sh = D_IN // N_DEV
    out = []
    for j in range(N_DEV):
        for first, end, dst in IN_SEGMENTS:
            lo, hi = max(first, sh * j), min(end, sh * (j + 1))
            if lo < hi:
                out.append((j, lo - sh * j, hi - sh * j, dst + lo - first))
    return out


def _w_in_assemble(stacked):
    tr = 256
    sh = D_IN // N_DEV

    def body(i_ref, o_ref):
        o_ref[:, COL_DT:COL_XBC] = jnp.zeros((tr, COL_XBC - COL_DT), bf16)
        for j, lo, hi, dst in _slab_pieces():
            o_ref[:, dst:dst + hi - lo] = i_ref[j, :, lo:hi]

    return pl.pallas_call(
        body, grid=(D // tr,), in_specs=[pl.BlockSpec((N_DEV, tr, sh), lambda i: (0, i, 0))],
        out_specs=pl.BlockSpec((None, tr, D_IN_PAD), lambda i: (0, i, 0)), out_shape=SDS((1, D, D_IN_PAD), bf16),
        name="w_in_assemble", compiler_params=_cparams(1),
    )(stacked)


def _w_in_slabs(dw_in):
    tr = 256
    sh = D_IN // N_DEV

    def body(i_ref, o_ref):
        for j, lo, hi, src in _slab_pieces():
            o_ref[j, :, lo:hi] = i_ref[:, src:src + hi - lo]

    return pl.pallas_call(
        body, grid=(D // tr,), in_specs=[pl.BlockSpec((tr, D_IN_PAD), lambda i: (i, 0))],
        out_specs=pl.BlockSpec((N_DEV, tr, sh), lambda i: (0, i, 0)), out_shape=SDS((N_DEV, D, sh), bf16),
        name="w_in_slabs", compiler_params=_cparams(1),
    )(dw_in)


SMALL_NAMES = ("mix_norm_g", "mlp_norm_g", "conv_b", "ssm_norm_g", "q_gain", "k_gain", "sinks", "dt_bias", "a_log", "d_skip",
               "rel_bias", "conv_w")
MISC_LANES = dict(q_gain=(LANE_QG, HD), k_gain=(LANE_KG, HD), sinks=(LANE_SINK, NQ), dt_bias=(LANE_DTB, NSSM),
                  a_log=(LANE_ALOG, NSSM), d_skip=(LANE_DSKIP, NSSM))


def _pack_small_grads(smalls, drel_t, loss):
    def body(*refs):
        o_ref = refs[-1]
        drel_ref, loss_ref = refs[-3], refs[-2]
        o_ref[...] = jnp.zeros_like(o_ref)
        for l in range(DEPTH):
            mixg, mlpg, convb, convw, ssd, attn = refs[6 * l:6 * l + 6]
            o_ref[ROW_MIXG + l:ROW_MIXG + l + 1, :] = mixg[...]
            o_ref[ROW_MLPG + l:ROW_MLPG + l + 1, :] = mlpg[...]
            o_ref[ROW_CONVB + l:ROW_CONVB + l + 1, :] = convb[0:1, :]
            o_ref[ROW_SSMG + l:ROW_SSMG + l + 1, 0:D_SSM] = ssd[0:1, :]
            o_ref[ROW_CONVW + 4 * l:ROW_CONVW + 4 * l + 4, :] = convw[0:4, :]
            row = slice(ROW_MISC + l, ROW_MISC + l + 1)
            o_ref[row, LANE_QG:LANE_QG + HD] = attn[0:1, 0:HD]
            o_ref[row, LANE_KG:LANE_KG + HD] = attn[1:2, 0:HD]
            o_ref[row, LANE_SINK:LANE_SINK + NQ] = attn[2:3, 0:NQ]
            o_ref[row, LANE_DTB:LANE_DTB + NSSM] = ssd[1:2, 0:NSSM]
            o_ref[row, LANE_ALOG:LANE_ALOG + NSSM] = ssd[2:3, 0:NSSM]
            o_ref[row, LANE_DSKIP:LANE_DSKIP + NSSM] = ssd[3:4, 0:NSSM]
        o_ref[ROW_RELB:ROW_RELB + NQ, 0:N_BUCKETS] = drel_ref[...]
        o_ref[ROW_LOSS:ROW_LOSS + 1, 0:1] = loss_ref[0:1, 0:1]

    args = []
    for sm in smalls:
        args += [sm["mix_norm_g"], sm["mlp_norm_g"], sm["conv_b"], sm["conv_w"], sm["ssd"], sm["attn"]]
    args += [drel_t, loss]
    return pl.pallas_call(body, out_shape=SDS((SMALL_ROWS, D), f32), name="pack_small_grads")(*args)


def _adamw_small(part, land, w, m, v):
    n = len(SMALL_NAMES)

    def grad_of(name, g_ref):
        if name == "mix_norm_g":
            return g_ref[ROW_MIXG:ROW_MIXG + DEPTH, :]
        if name == "mlp_norm_g":
            return g_ref[ROW_MLPG:ROW_MLPG + DEPTH, :]
        if name == "conv_b":
            return g_ref[ROW_CONVB:ROW_CONVB + DEPTH, :]
        if name == "ssm_norm_g":
            return g_ref[ROW_SSMG:ROW_SSMG + DEPTH, 0:D_SSM]
        if name == "rel_bias":
            return g_ref[ROW_RELB:ROW_RELB + NQ, 0:N_BUCKETS].T
        lane, width = MISC_LANES[name]
        return g_ref[ROW_MISC:ROW_MISC + DEPTH, lane:lane + width]

    def body(part_ref, land_ref, *refs):
        ws, ms, vs = refs[:n], refs[n:2 * n], refs[2 * n:3 * n]
        loss_ref = refs[3 * n]
        outs = refs[3 * n + 1:-1]
        g_ref = refs[-1]
        me = _dev_index(*_my_place())
        for p in range(N_DEV):
            term = jnp.where(me == p, part_ref[...], land_ref[p])
            if p == 0:
                g_ref[...] = term
            else:
                g_ref[...] += term
        loss_ref[...] = g_ref[ROW_LOSS:ROW_LOSS + 1, 0:128]
        my_cols = pl.ds(pl.multiple_of(me * 128, 128), 128)
        for k, name in enumerate(SMALL_NAMES):
            g_out, d_out, m_out, v_out = outs[4 * k:4 * k + 4]
            if name == "conv_w":
                for l in range(DEPTH):
                    g = g_ref[ROW_CONVW + 4 * l:ROW_CONVW + 4 * l + 4, my_cols]
                    delta, m_new, v_new = _adamw_math(ws[k][l], ms[k][l], vs[k][l], g)
                    g_out[l], d_out[l], m_out[l], v_out[l] = g, delta, m_new, v_new
            else:
                g = grad_of(name, g_ref)
                delta, m_new, v_new = _adamw_math(ws[k][...], ms[k][...], vs[k][...], g)
                g_out[...], d_out[...], m_out[...], v_out[...] = g, delta, m_new, v_new

    ws = [w[name] for name in SMALL_NAMES]
    out_shape = [SDS((1, 128), f32)]
    for a in ws:
        out_shape += [SDS(a.shape, f32)] * 4
    return pl.pallas_call(body, out_shape=out_shape, name="adamw_small", scratch_shapes=[pltpu.VMEM((SMALL_ROWS, D), f32)])(
        part, land, *ws, *[m[name] for name in SMALL_NAMES], *[v[name] for name in SMALL_NAMES])


def _plain(tm, tn):
    return pl.BlockSpec((tm, tn), lambda i, j, k: (i, j))


def _rowblk(tm, width):
    return pl.BlockSpec((tm, width), lambda i, j, k: (i, 0))


def _store_epi(dtype):
    def epi(acc, i, j, ex, outs):
        outs[0][...] = acc.astype(dtype)
    return epi


def _rms_prologue(layer):
    def pro(a_ref, ex, outs):
        xv = a_ref[...]
        r = lax.rsqrt(jnp.mean(xv * xv, axis=-1, keepdims=True) + EPS)
        h = (xv * r * ex[0][layer:layer + 1, :]).astype(bf16)
        outs[-1][...] = h
        return h
    return pro


MLP_TM = 256
MLP_VMEM = 56 * 1024 * 1024


def _resident(shape):
    return pl.BlockSpec((None,) + shape, lambda i: (0, 0, 0), pipeline_mode=pl.Buffered(1))


def _mlp_fwd(layer, x_mid, g, w_up, w_down, tgt=None):
    tm = MLP_TM
    with_loss = tgt is not None

    def body(x_ref, g_ref, wu_ref, wd_ref, *rest):
        a_ref, r_ref, h_ref = rest[with_loss:with_loss + 3]
        i = pl.program_id(0)
        xv = x_ref[...]
        h = (xv * lax.rsqrt(jnp.mean(xv * xv, axis=-1, keepdims=True) + EPS) * g_ref[layer:layer + 1, :]).astype(bf16)
        h_ref[...] = h
        r = jnp.maximum(_dot(h, wu_ref[...], NN_DIMS), 0.0)
        a = (r * r).astype(bf16)
        a_ref[...] = a
        r_ref[...] = r.astype(bf16)
        y = xv + _dot(a, wd_ref[...], NN_DIMS)
        if not with_loss:
            rest[3][...] = y
            return
        err = y - rest[0][...]
        rest[4][...] = err * (1.0 / D)
        part = 0.5 * jnp.sum(jnp.mean(err * err, axis=-1, keepdims=True), axis=0, keepdims=True)

        @pl.when(i == 0)
        def _():
            rest[5][...] = jnp.zeros_like(rest[5])

        rest[5][...] += jnp.broadcast_to(part, rest[5].shape)

    row = lambda width: pl.BlockSpec((tm, width), lambda i: (i, 0))
    in_specs = [row(D), pl.BlockSpec((DEPTH, D), lambda i: (0, 0)), _resident((D, D_FF)), _resident((D_FF, D))]
    out_specs = [row(D_FF), row(D_FF), row(D), row(D)]
    out_shape = [SDS((S, D_FF), bf16), SDS((S, D_FF), bf16), SDS((S, D), bf16), SDS((S, D), f32)]
    args = [x_mid, g, w_up, w_down]
    if with_loss:
        in_specs.append(row(D))
        args.append(tgt)
        out_specs.append(pl.BlockSpec((1, 128), lambda i: (0, 0)))
        out_shape.append(SDS((1, 128), f32))
    return pl.pallas_call(
        body, grid=(S // tm,), in_specs=in_specs, out_specs=out_specs, out_shape=out_shape,
        name="mlp_fwd_loss" if with_loss else "mlp_fwd",
        compiler_params=pltpu.CompilerParams(dimension_semantics=("arbitrary",), vmem_limit_bytes=MLP_VMEM),
    )(*args)


def _mlp_bwd_act(layer, dx_out, r_act, x_mid, g, w_down, w_up, deps):
    tm = MLP_TM

    def body(dxo_ref, r_ref, xm_ref, g_ref, wd_ref, wu_ref, *rest):
        du_ref, dx_ref, dg_ref = rest[len(deps):]
        dxo = dxo_ref[...]
        du = (_dot(dxo.astype(bf16), wd_ref[...], NT_DIMS) * (2.0 * r_ref[...].astype(f32))).astype(bf16)
        du_ref[...] = du
        dh = _dot(du, wu_ref[...], NT_DIMS)
        _rms_bwd_epilogue(layer)(dh, pl.program_id(0), 0, (xm_ref, g_ref, dxo_ref), (dx_ref, dg_ref))

    row = lambda width: pl.BlockSpec((tm, width), lambda i: (i, 0))
    return pl.pallas_call(
        body, grid=(S // tm,),
        in_specs=[row(D), row(D_FF), row(D), pl.BlockSpec((DEPTH, D), lambda i: (0, 0)), _resident((D_FF, D)), _resident((D, D_FF))]
        + [ANY_SPEC] * len(deps),
        out_specs=[row(D_FF), row(D), pl.BlockSpec((1, D), lambda i: (0, 0))],
        out_shape=[SDS((S, D_FF), bf16), SDS((S, D), f32), SDS((1, D), f32)], name="mlp_bwd_act",
        compiler_params=pltpu.CompilerParams(dimension_semantics=("arbitrary",), vmem_limit_bytes=MLP_VMEM),
    )(dx_out, r_act, x_mid, g, w_down, w_up, *deps)


def _layer_fwd(l, x, p, get_weights, bias, tgt=None):
    wts = get_weights(l, "in", [x, bias])
    gfull = pl.BlockSpec((DEPTH, D), lambda i, j, k: (0, 0))
    tm = 256

    def inproj_epi(acc, i, j, ex, outs):
        outs[0][...] = acc[:, COL_QKV:COL_Z]
        outs[1][...] = acc[:, COL_Z:COL_DT]
        outs[2][...] = acc[:, COL_XBC:D_IN_PAD]
        outs[3][...] = acc[:, COL_DT:COL_DT + 128]

    qkv, z, xbc, dt, h1 = _matmul(
        "in_proj", "nn", x, wts["w_in"], tm=tm, tn=D_IN_PAD, tk=D, prologue=_rms_prologue(l),
        extras=(p["mix_norm_g"],), extra_specs=(gfull,),
        out_shape=[SDS((S, 768), f32), SDS((S, 512), f32), SDS((S, 1024), f32), SDS((S, 128), f32), SDS((S, D), bf16)],
        out_specs=[_rowblk(tm, 768), _rowblk(tm, 512), _rowblk(tm, 1024), _rowblk(tm, 128), _rowblk(tm, D)], epilogue=inproj_epi)
    attn = _attn_fwd(qkv, p["q_gain"], p["k_gain"], p["sinks"], bias, l)
    xact = _conv_fwd(xbc, wts["conv_w"], p["conv_b"], l)
    mix, hs, y_ssd = _ssd_fwd(xact, z, dt, attn, p["dt_bias"], p["a_log"], p["d_skip"], p["ssm_norm_g"], l)
    wts = dict(wts, **get_weights(l, "rest", [mix]))

    def resid_epi(acc, i, j, ex, outs):
        outs[0][...] = ex[0][...] + acc

    x_mid = _matmul("out_proj", "nn", mix, wts["w_out"], tm=tm, tn=D, tk=D, out_shape=SDS((S, D), f32),
                    out_specs=_plain(tm, D), epilogue=resid_epi, extras=(x,), extra_specs=(_plain(tm, D),))

    a_act, r_act, h2, *result = _mlp_fwd(l, x_mid, p["mlp_norm_g"], wts["w_up"], wts["w_down"], tgt)
    saved = dict(x=x, h1=h1, qkv=qkv, z=z, xbc=xbc, dt=dt, xact=xact, mix=mix, hs=hs, y_ssd=y_ssd, x_mid=x_mid, h2=h2,
                 a=a_act, r=r_act, wts=wts)
    return (result[0] if tgt is None else tuple(result)), saved


def _layer_bwd(l, dx_out, sv, p, bias, deps, send):
    wts = sv["wts"]

    du, dx_mid, dg_mlp = _mlp_bwd_act(l, dx_out, sv["r"], sv["x_mid"], p["mlp_norm_g"], wts["w_down"], wts["w_up"], deps)
    dw_down = _matmul("dw_down", "tn", sv["a"], dx_out, tm=1024, tn=D, tk=S, out_shape=SDS((D_FF, D), bf16),
                      out_specs=_plain(1024, D), epilogue=_store_epi(bf16))
    dw_up = _matmul("dw_up", "tn", sv["h2"], du, tm=D, tn=1024, tk=S, out_shape=SDS((D, D_FF), bf16),
                    out_specs=_plain(D, 1024), epilogue=_store_epi(bf16))
    dw_out = _matmul("dw_out", "tn", sv["mix"], dx_mid, tm=D, tn=512, tk=512, out_shape=SDS((D, D), bf16),
                     out_specs=_plain(D, 512), epilogue=_store_epi(bf16))
    deps = send(l, dict(w_down=dw_down, w_up=dw_up, w_out=dw_out))
    gfull = pl.BlockSpec((DEPTH, D), lambda i, j, k: (0, 0))
    grow = pl.BlockSpec((1, D), lambda i, j, k: (0, 0))
    dmix = _matmul("out_proj_da", "nt", dx_mid, wts["w_out"], tm=256, tn=D, tk=D, out_shape=SDS((S, D), f32),
                   out_specs=_plain(256, D), epilogue=_store_epi(f32), deps=deps)
    dproj, dbias, dsm_attn = _attn_bwd(sv["qkv"], dmix, p["q_gain"], p["k_gain"], p["sinks"], bias, l)
    dproj, dxact, dsm_ssd = _ssd_bwd(sv["xact"], sv["z"], sv["dt"], dmix, sv["hs"], sv["y_ssd"], p["dt_bias"], p["a_log"],
                                     p["d_skip"], p["ssm_norm_g"], dproj, l)
    dproj, dconv_w, dconv_b = _conv_bwd(sv["xbc"], dxact, wts["conv_w"], p["conv_b"], dproj, l)
    dw_in = _matmul("dw_in", "tn", sv["h1"], dproj, tm=D, tn=640, tk=S, out_shape=SDS((D, D_IN_PAD), bf16),
                    out_specs=_plain(D, 640), epilogue=_store_epi(bf16))
    deps = send(l, dict(w_in=_w_in_slabs(dw_in)))
    dx, dg_mix = _matmul(
        "in_proj_dh", "nt", dproj, wts["w_in"], tm=256, tn=D, tk=D_IN_PAD, out_shape=[SDS((S, D), f32), SDS((1, D), f32)],
        out_specs=[_plain(256, D), grow], epilogue=_rms_bwd_epilogue(l),
        extras=(sv["x"], p["mix_norm_g"], dx_mid), extra_specs=(_plain(256, D), gfull, _plain(256, D)), deps=deps)
    small = dict(mix_norm_g=dg_mix, mlp_norm_g=dg_mlp, conv_w=dconv_w, conv_b=dconv_b, ssd=dsm_ssd, attn=dsm_attn, dbias=dbias)
    return dx, small, deps


def _local_step(x, tgt, p, get_weights, send):
    onehot_t = jnp.asarray(_onehot_buckets())
    bias = _bias_build(p["rel_bias"].T, onehot_t).reshape(NQ, BLK, 2 * BLK)
    saved = []
    h = x
    for l in range(DEPTH):
        h, sv = _layer_fwd(l, h, p, get_weights, bias, tgt if l == DEPTH - 1 else None)
        saved.append(sv)
    dx, loss = h
    smalls = [None] * DEPTH
    deps = ()
    for l in reversed(range(DEPTH)):
        dx, smalls[l], deps = _layer_bwd(l, dx, saved[l], p, bias, deps, send)
    drel_t = _bias_grad(smalls[0]["dbias"].reshape(NQ, -1), smalls[1]["dbias"].reshape(NQ, -1), onehot_t)
    return dx, _pack_small_grads(smalls, drel_t, loss)


WEIGHT_ORDER = ("mix_norm_g", "w_in", "q_gain", "k_gain", "sinks", "rel_bias", "conv_w", "conv_b", "dt_bias", "a_log", "d_skip",
                "ssm_norm_g", "w_out", "mlp_norm_g", "w_up", "w_down")


def kernel(x, mix_norm_g, w_in, q_gain, k_gain, sinks, rel_bias, conv_w, conv_b, dt_bias, a_log, d_skip, ssm_norm_g, w_out, mlp_norm_g, w_up, w_down, loss_target, m_mix_norm_g, m_w_in, m_q_gain, m_k_gain, m_sinks, m_rel_bias, m_conv_w, m_conv_b, m_dt_bias, m_a_log, m_d_skip, m_ssm_norm_g, m_w_out, m_mlp_norm_g, m_w_up, m_w_down, v_mix_norm_g, v_w_in, v_q_gain, v_k_gain, v_sinks, v_rel_bias, v_conv_w, v_conv_b, v_dt_bias, v_a_log, v_d_skip, v_ssm_norm_g, v_w_out, v_mlp_norm_g, v_w_up, v_w_down):
    w = dict(mix_norm_g=mix_norm_g, w_in=w_in, q_gain=q_gain, k_gain=k_gain, sinks=sinks, rel_bias=rel_bias, conv_w=conv_w,
             conv_b=conv_b, dt_bias=dt_bias, a_log=a_log, d_skip=d_skip, ssm_norm_g=ssm_norm_g, w_out=w_out,
             mlp_norm_g=mlp_norm_g, w_up=w_up, w_down=w_down)
    m = dict(mix_norm_g=m_mix_norm_g, w_in=m_w_in, q_gain=m_q_gain, k_gain=m_k_gain, sinks=m_sinks, rel_bias=m_rel_bias,
             conv_w=m_conv_w, conv_b=m_conv_b, dt_bias=m_dt_bias, a_log=m_a_log, d_skip=m_d_skip, ssm_norm_g=m_ssm_norm_g,
             w_out=m_w_out, mlp_norm_g=m_mlp_norm_g, w_up=m_w_up, w_down=m_w_down)
    v = dict(mix_norm_g=v_mix_norm_g, w_in=v_w_in, q_gain=v_q_gain, k_gain=v_k_gain, sinks=v_sinks, rel_bias=v_rel_bias,
             conv_w=v_conv_w, conv_b=v_conv_b, dt_bias=v_dt_bias, a_log=v_a_log, d_skip=v_d_skip, ssm_norm_g=v_ssm_norm_g,
             w_out=v_w_out, mlp_norm_g=v_mlp_norm_g, w_up=v_w_up, w_down=v_w_down)
    big = ("w_in", "w_out", "w_up", "w_down")

    my_idx = _dev_index(*_my_place()).astype(jnp.int32).reshape(1)

    fulls = {n: _cast_to_full("cast_" + n, w[n], KIND[n], FULL_SHAPE[n], my_idx, bf16) for n in big}
    conv_full = _cast_to_full("cast_conv_w", conv_w.reshape(1, DEPTH * 4, 128), "stack", (N_DEV, DEPTH * 4, 128), my_idx, f32)[0]
    rest = ["w_out", "w_up", "w_down"]
    g0 = _gather_start("gather0", ["w_in", "conv_w"], [fulls["w_in"][0], conv_full], ())
    g1 = _gather_start("gather1", rest, [fulls[n][0] for n in rest], (g0["token"],))
    g2 = _gather_start("gather2", ["w_in"], [fulls["w_in"][1]], (g1["token"],))
    g3 = _gather_start("gather3", rest, [fulls[n][1] for n in rest], (g2["token"],))
    held = {}
    flat = lambda a: a.reshape(a.shape[0] * a.shape[1], a.shape[2])
    adam_in = {n: (flat(w[n]), flat(m[n]), flat(v[n])) for n in big}

    def get_weights(l, part, after):
        if l == 0 and part == "in":
            full_in, full_conv = _gather_finish("gather0", ["w_in", "conv_w"], g0,
                                                list(after) + [g3["token"], adam_in["w_in"][1], adam_in["w_in"][2]])
            held["conv_w"] = jnp.transpose(full_conv.reshape(N_DEV, DEPTH, 4, 128), (1, 2, 0, 3)).reshape(DEPTH, 4, D_CONV)
            return dict(w_in=_w_in_assemble(full_in), conv_w=held["conv_w"])
        if part == "in":
            return dict(w_in=_w_in_assemble(_gather_finish("gather2", ["w_in"], g2, after)[0]), conv_w=held["conv_w"])
        full = _gather_finish("gather1" if l == 0 else "gather3", rest, g1 if l == 0 else g3, after)
        return {n: f[None] for n, f in zip(rest, full)}

    pending = []

    def send(l, grads):
        names = list(grads)
        started = _exchange_start("exchange%d_%s" % (l, names[0]), names, [grads[n] for n in names], ())
        pending.append((l, names, started))
        return (started["token"],)

    dx, small_part = _local_step(x.reshape(S, D), loss_target.reshape(S, D), w, get_weights, send)

    small = _small_exchange_start(small_part, ())
    tiles = dict(w_in=256, w_out=128, w_up=256, w_down=256)
    outs_of = {n: None for n in big}
    after = [dx, small["token"]]
    for l, names, started in pending:
        bufs = _split_wait("exchange%d_%s_wait" % (l, names[0]), started, after)
        for t, n in enumerate(names):
            outs_of[n] = _adamw_layer("adamw_%s%d" % (n, l), KIND[n], l, *adam_in[n],
                                      bufs[len(names) + t], bufs[t], my_idx, outs_of[n], tiles[n])
        after = [outs_of[names[-1]][0]]
    res = {n: [o.reshape(w[n].shape) for o in outs_of[n]] for n in big}
    small_part, small_land = _split_wait("small_exchange_wait", small, after)
    small_outs = _adamw_small(small_part, small_land, w, m, v)
    loss = small_outs[0][0, 0]
    for k, name in enumerate(SMALL_NAMES):
        res[name] = small_outs[1 + 4 * k:5 + 4 * k]

    result = [loss, dx.reshape(1, S, D)]
    for k in range(4):
        result += [res[name][k] for name in WEIGHT_ORDER]
    return tuple(result)
```

```python
import functools
import math

import numpy as np
import jax
import jax.numpy as jnp
from jax import lax
from jax.experimental import pallas as pl
from jax.experimental.pallas import tpu as pltpu

f32 = jnp.float32
bf16 = jnp.bfloat16
SDS = jax.ShapeDtypeStruct
MESH = pl.DeviceIdType.MESH
HIGHEST = lax.Precision.HIGHEST

S = 2048
D = 1024
DEPTH = 2
BLK = 128
NBLK = S // BLK
HD = 64
NQ = 8
NKV = 2
NSSM = 8
NGRP = 2
NSTATE = 128
D_ATTN = 512
D_SSM = 512
D_CONV = 1024
D_FF = 4096
D_IN = 2312
D_IN_PAD = 2560
COL_QKV, COL_Z, COL_DT, COL_XBC = 0, 768, 1280, 1536
IN_SEGMENTS = ((0, 1280, 0), (1280, 2304, COL_XBC), (2304, 2312, COL_DT))
N_BUCKETS = 32
EPS = 1e-6
N_DEV = 8
VMEM_LIMIT = 48 * 1024 * 1024

ADAM_LR = 0.001
ADAM_B1 = 0.9
ADAM_B2 = 0.999
ADAM_EPS = 1e-08
ADAM_WD = 0.01
ADAM_STEP = 10

NT_DIMS = (((1,), (1,)), ((), ()))
TN_DIMS = (((0,), (0,)), ((), ()))
NN_DIMS = (((1,), (0,)), ((), ()))

ROW_MIXG = 0
ROW_MLPG = 2
ROW_CONVB = 4
ROW_SSMG = 6
ROW_MISC = 8
ROW_RELB = 10
ROW_CONVW = 18
ROW_LOSS = 26
SMALL_ROWS = 32
LANE_QG, LANE_KG, LANE_SINK, LANE_DTB, LANE_ALOG, LANE_DSKIP = 0, 64, 128, 256, 384, 512


def _dot(a, b, dims):
    return lax.dot_general(a, b, dims, preferred_element_type=f32)


def _cparams(n_axes):
    return pltpu.CompilerParams(dimension_semantics=("arbitrary",) * n_axes, vmem_limit_bytes=VMEM_LIMIT)


def _sum11(v):
    return jnp.sum(jnp.sum(v, axis=1, keepdims=True), axis=0, keepdims=True)


def _sigmoid(v):
    return 1.0 / (1.0 + jnp.exp(-v))


ANY_SPEC = pl.BlockSpec(memory_space=pl.ANY)


def _in_hbm(args):
    return [pltpu.with_memory_space_constraint(a, pltpu.HBM) if a.size >= 65536 else a for a in args]


def _out_hbm(out_shape):
    one = lambda s: pltpu.HBM(s.shape, s.dtype) if math.prod(s.shape) >= 65536 else s
    return [one(s) for s in out_shape] if isinstance(out_shape, (list, tuple)) else one(out_shape)


def _matmul(name, mode, a, b, *, layer=0, tm, tn, tk, out_shape, out_specs, epilogue, extras=(), extra_specs=(), deps=(),
            prologue=None):
    extras = tuple(extras) + tuple(deps)
    extra_specs = tuple(extra_specs) + (ANY_SPEC,) * len(deps)
    if mode == "tn":
        t_dim, m_dim = a.shape
        n_dim = b.shape[1]
        grid = (m_dim // tm, n_dim // tn, t_dim // tk)
        a_spec = pl.BlockSpec((tk, tm), lambda i, j, k: (k, i))
        b_spec = pl.BlockSpec((tk, tn), lambda i, j, k: (k, j))
        dims = TN_DIMS
    elif mode == "nn":
        m_dim, k_dim = a.shape
        n_dim = b.shape[-1]
        grid = (m_dim // tm, n_dim // tn, k_dim // tk)
        a_spec = pl.BlockSpec((tm, tk), lambda i, j, k: (i, k))
        b_spec = pl.BlockSpec((None, tk, tn), lambda i, j, k: (layer, k, j))
        dims = NN_DIMS
    else:
        m_dim, k_dim = a.shape
        n_dim = b.shape[-2]
        grid = (m_dim // tm, n_dim // tn, k_dim // tk)
        a_spec = pl.BlockSpec((tm, tk), lambda i, j, k: (i, k))
        b_spec = pl.BlockSpec((None, tn, tk), lambda i, j, k: (layer, j, k))
        dims = NT_DIMS
    nk = grid[2]
    n_ex = len(extras)

    def body(a_ref, b_ref, *rest):
        ex = rest[:n_ex - len(deps)]
        outs = rest[n_ex:-1]
        acc = rest[-1]
        i = pl.program_id(0)
        j = pl.program_id(1)
        k = pl.program_id(2)
        lhs = a_ref[...].astype(bf16) if prologue is None else prologue(a_ref, ex, outs)
        part = _dot(lhs, b_ref[...].astype(bf16), dims)
        if nk == 1:
            epilogue(part, i, j, ex, outs)
        else:
            @pl.when(k == 0)
            def _():
                acc[...] = part

            @pl.when(k > 0)
            def _():
                acc[...] += part

            @pl.when(k == nk - 1)
            def _():
                epilogue(acc[...], i, j, ex, outs)

    return pl.pallas_call(
        body, grid=grid, in_specs=[a_spec, b_spec, *extra_specs], out_specs=out_specs, out_shape=_out_hbm(out_shape),
        scratch_shapes=[pltpu.VMEM((tm, tn) if nk > 1 else (8, 128), f32)], name=name, compiler_params=_cparams(3),
    )(*_in_hbm([a, b, *extras]))


def _rms_bwd_epilogue(layer):
    def epi(acc, i, j, ex, outs):
        x_ref, g_ref, dres_ref = ex
        dx_ref, dg_ref = outs
        xv = x_ref[...]
        r = lax.rsqrt(jnp.mean(xv * xv, axis=-1, keepdims=True) + EPS)
        xhat = xv * r
        w = acc * g_ref[layer:layer + 1, :]
        dx_ref[...] = dres_ref[...] + r * (w - xhat * jnp.mean(xhat * w, axis=-1, keepdims=True))
        dg = jnp.sum(acc * xhat, axis=0, keepdims=True)

        @pl.when(i == 0)
        def _():
            dg_ref[...] = dg

        @pl.when(i > 0)
        def _():
            dg_ref[...] += dg
    return epi


def _own_slab_spec(kind, tr, cols, nblk):
    if kind == "stack":
        return pl.BlockSpec((None, tr, cols), lambda i, idx: (idx[0], i, 0))
    if kind == "cols512":
        return pl.BlockSpec((tr, cols), lambda i, idx: (i, idx[0]))
    return pl.BlockSpec((tr, cols), lambda i, idx: (idx[0] * nblk + i, 0))


def _cast_to_full(name, w, kind, full_shape, my_idx, dtype):
    n_layers, rows, cols = w.shape
    tr = min(rows, 256)
    nblk = rows // tr

    def body(idx_ref, w_ref, *o_refs):
        for l in range(n_layers):
            o_refs[l][...] = w_ref[l].astype(dtype)

    grid_spec = pltpu.PrefetchScalarGridSpec(
        num_scalar_prefetch=1, grid=(nblk,), in_specs=[pl.BlockSpec((n_layers, tr, cols), lambda i, idx: (0, i, 0))],
        out_specs=[_own_slab_spec(kind, tr, cols, nblk)] * n_layers)
    return pl.pallas_call(body, grid_spec=grid_spec, out_shape=_out_hbm([SDS(full_shape, dtype)] * n_layers), name=name,
                          compiler_params=_cparams(1))(*_in_hbm([my_idx, w]))


def _adamw_math(w, m, v, g):
    m_new = ADAM_B1 * m + (1.0 - ADAM_B1) * g
    v_new = ADAM_B2 * v + (1.0 - ADAM_B2) * (g * g)
    m_hat = m_new / (1.0 - ADAM_B1 ** ADAM_STEP)
    v_hat = v_new / (1.0 - ADAM_B2 ** ADAM_STEP)
    delta = -ADAM_LR * (m_hat / (jnp.sqrt(v_hat) + ADAM_EPS) + ADAM_WD * w)
    return delta, m_new, v_new


def _adamw_layer(name, kind, layer, w, m, v, land, g_full, my_idx, prev, tr):
    rows2, cols = w.shape
    rows = rows2 // DEPTH
    nblk = rows // tr
    own_spec = _own_slab_spec(kind, tr, cols, nblk)
    n_prev = 0 if prev is None else 4

    def body(idx_ref, w_ref, m_ref, v_ref, land_ref, own_ref, *rest):
        g_ref, d_ref, mo_ref, vo_ref = rest[n_prev:]
        me = idx_ref[0]
        g = None
        for p in range(N_DEV):
            part = jnp.where(me == p, own_ref[...], land_ref[p]).astype(f32)
            g = part if g is None else g + part
        delta, m_new, v_new = _adamw_math(w_ref[...], m_ref[...], v_ref[...], g)
        g_ref[...] = g
        d_ref[...] = delta
        mo_ref[...] = m_new
        vo_ref[...] = v_new

    blk = pl.BlockSpec((tr, cols), lambda i, idx: (layer * nblk + i, 0))
    grid_spec = pltpu.PrefetchScalarGridSpec(
        num_scalar_prefetch=1, grid=(nblk,),
        in_specs=[blk, blk, blk, pl.BlockSpec((N_DEV, tr, cols), lambda i, idx: (0, i, 0)), own_spec] + [ANY_SPEC] * n_prev,
        out_specs=[blk, blk, blk, blk])
    aliases = {} if prev is None else {6 + k: k for k in range(4)}
    return pl.pallas_call(
        body, grid_spec=grid_spec, out_shape=_out_hbm([SDS((rows2, cols), f32)] * 4), name=name, input_output_aliases=aliases,
        compiler_params=_cparams(1),
    )(*_in_hbm([my_idx, w, m, v, land, g_full, *([] if prev is None else prev)]))


def _bucket_table():
    qi = np.arange(BLK)[:, None]
    kj = np.arange(2 * BLK)[None, :]
    dist = qi + BLK - kj
    dcl = np.clip(dist, 0, None)
    max_exact = N_BUCKETS // 2
    d_f = np.maximum(dcl, 1).astype(np.float32)
    large = max_exact + (np.log(d_f / np.float32(max_exact)) / np.float32(math.log(128 / max_exact))
                         * np.float32(N_BUCKETS - max_exact)).astype(np.int32)
    large = np.minimum(large, N_BUCKETS - 1)
    bucket = np.where(dcl < max_exact, dcl, large)
    in_window = (dist >= 0) & (dist < BLK)
    return bucket.astype(np.int32), in_window


def _onehot_buckets():
    bucket, _ = _bucket_table()
    oh = (bucket.reshape(-1)[None, :] == np.arange(N_BUCKETS)[:, None]).astype(np.float32)
    return oh


def _bias_build(rel_bias_t, onehot_t):
    def body(r_ref, o_ref, out_ref):
        out_ref[...] = jnp.dot(r_ref[...], o_ref[...], preferred_element_type=f32, precision=HIGHEST)

    tn = 4096
    return pl.pallas_call(
        body, grid=(BLK * 2 * BLK // tn,),
        in_specs=[pl.BlockSpec((NQ, N_BUCKETS), lambda i: (0, 0)), pl.BlockSpec((N_BUCKETS, tn), lambda i: (0, i))],
        out_specs=pl.BlockSpec((NQ, tn), lambda i: (0, i)), out_shape=SDS((NQ, BLK * 2 * BLK), f32), name="bias_build",
        compiler_params=_cparams(1),
    )(rel_bias_t, onehot_t)


def _bias_grad(dbias0, dbias1, onehot_t):
    tn = 4096
    nsteps = BLK * 2 * BLK // tn

    def body(a_ref, b_ref, o_ref, out_ref):
        part = lax.dot_general(a_ref[...] + b_ref[...], o_ref[...], NT_DIMS, preferred_element_type=f32, precision=HIGHEST)

        @pl.when(pl.program_id(0) == 0)
        def _():
            out_ref[...] = part

        @pl.when(pl.program_id(0) > 0)
        def _():
            out_ref[...] += part

    return pl.pallas_call(
        body, grid=(nsteps,),
        in_specs=[pl.BlockSpec((NQ, tn), lambda i: (0, i)), pl.BlockSpec((NQ, tn), lambda i: (0, i)),
                  pl.BlockSpec((N_BUCKETS, tn), lambda i: (0, i))],
        out_specs=pl.BlockSpec((NQ, N_BUCKETS), lambda i: (0, 0)), out_shape=SDS((NQ, N_BUCKETS), f32), name="bias_grad",
        compiler_params=_cparams(1),
    )(dbias0, dbias1, onehot_t)


def _attn_mask(n):
    qi = lax.broadcasted_iota(jnp.int32, (BLK, 2 * BLK), 0)
    kj = lax.broadcasted_iota(jnp.int32, (BLK, 2 * BLK), 1)
    dist = qi + BLK - kj
    first_key = jnp.where(n > 0, 0, BLK)
    return (dist >= 0) & (dist < BLK) & (kj >= first_key)


def _row_mean(a):
    return jnp.mean(a, axis=-1, keepdims=True)


def _head_norm(t, gain):
    r = lax.rsqrt(_row_mean(t * t) + EPS)
    that = t * r
    return that, r, that * gain


def _softmax_with_sink(s, sink):
    m = jnp.maximum(jnp.max(s, axis=-1, keepdims=True), sink)
    p = jnp.exp(s - m)
    psink = jnp.exp(sink - m)
    inv = 1.0 / (jnp.sum(p, axis=-1, keepdims=True) + psink)
    return p * inv, psink * inv


GQ = NQ // NKV


def _group_rows(x_ref, sk_ref, layer, j):
    heads = [GQ * j + g for g in range(GQ)]
    xs = jnp.concatenate([x_ref[:, pl.ds(HD * h, HD)] for h in heads], axis=0)
    sink = jnp.concatenate([jnp.broadcast_to(sk_ref[layer:layer + 1, h:h + 1], (BLK, 1)) for h in heads], axis=0)
    return xs, sink


def _attn_fwd(qkv, q_gain, k_gain, sinks, bias, layer):
    def body(q_ref, kc_ref, kp_ref, vc_ref, vp_ref, qg_ref, kg_ref, sk_ref, bias_ref, o_ref):
        n = pl.program_id(0)
        mask = jnp.tile(_attn_mask(n), (GQ, 1))
        qg = qg_ref[layer:layer + 1, :]
        kg = kg_ref[layer:layer + 1, :]
        grp = range(NKV)
        kbs = [jnp.concatenate([kp_ref[:, pl.ds(HD * j, HD)], kc_ref[:, pl.ds(HD * j, HD)]], axis=0) for j in grp]
        vbs = [jnp.concatenate([vp_ref[:, pl.ds(HD * j, HD)], vc_ref[:, pl.ds(HD * j, HD)]], axis=0).astype(bf16) for j in grp]
        kn_b = [_head_norm(kbs[j], kg)[2].astype(bf16) for j in grp]
        rows = [_group_rows(q_ref, sk_ref, layer, j) for j in grp]
        qn_b = [_head_norm(rows[j][0], qg)[2].astype(bf16) for j in grp]
        ss = [_dot(qn_b[j], kn_b[j], NT_DIMS) * (HD ** -0.5) + bias_ref[GQ * j:GQ * (j + 1)].reshape(GQ * BLK, 2 * BLK) for j in grp]
        ps = [_softmax_with_sink(jnp.where(mask, ss[j], -jnp.inf), rows[j][1])[0] for j in grp]
        outs = [_dot(ps[j].astype(bf16), vbs[j], NN_DIMS).astype(bf16) for j in grp]
        for j in grp:
            for g in range(GQ):
                o_ref[:, pl.ds(HD * (GQ * j + g), HD)] = outs[j][BLK * g:BLK * (g + 1), :]

    prev = lambda n: jnp.maximum(n - 1, 0)
    small = lambda shape: pl.BlockSpec(shape, lambda n: (0,) * len(shape))
    return pl.pallas_call(
        body, grid=(NBLK,),
        in_specs=[pl.BlockSpec((BLK, D_ATTN), lambda n: (n, 0)),
                  pl.BlockSpec((BLK, 128), lambda n: (n, 4)), pl.BlockSpec((BLK, 128), lambda n: (prev(n), 4)),
                  pl.BlockSpec((BLK, 128), lambda n: (n, 5)), pl.BlockSpec((BLK, 128), lambda n: (prev(n), 5)),
                  small((DEPTH, HD)), small((DEPTH, HD)), small((DEPTH, NQ)), small((NQ, BLK, 2 * BLK))],
        out_specs=pl.BlockSpec((BLK, D_ATTN), lambda n: (n, 0)), out_shape=_out_hbm(SDS((S, D_ATTN), bf16)),
        name="attn_fwd", compiler_params=_cparams(1),
    )(*_in_hbm([qkv, qkv, qkv, qkv, qkv, q_gain, k_gain, sinks, bias]))


def _attn_bwd(qkv, dmix, q_gain, k_gain, sinks, bias, layer):
    def body(q_ref, kc_ref, kp_ref, vc_ref, vp_ref, do_ref, qg_ref, kg_ref, sk_ref, bias_ref,
             dqkv_ref, dbias_ref, dsm_ref, carry):
        i = pl.program_id(0)
        n = NBLK - 1 - i
        mask = jnp.tile(_attn_mask(n), (GQ, 1))
        qg = qg_ref[layer:layer + 1, :]
        kg = kg_ref[layer:layer + 1, :]
        lane = lax.broadcasted_iota(jnp.int32, (1, 128), 1)

        @pl.when(i == 0)
        def _():
            carry[...] = jnp.zeros_like(carry)
            dbias_ref[...] = jnp.zeros_like(dbias_ref)
            dsm_ref[...] = jnp.zeros_like(dsm_ref)

        grp = range(NKV)
        kbs = [jnp.concatenate([kp_ref[:, pl.ds(HD * j, HD)], kc_ref[:, pl.ds(HD * j, HD)]], axis=0) for j in grp]
        vbs = [jnp.concatenate([vp_ref[:, pl.ds(HD * j, HD)], vc_ref[:, pl.ds(HD * j, HD)]], axis=0).astype(bf16) for j in grp]
        knorm = [_head_norm(kbs[j], kg) for j in grp]
        kn_b = [knorm[j][2].astype(bf16) for j in grp]
        rows = [_group_rows(q_ref, sk_ref, layer, j) for j in grp]
        qnorm = [_head_norm(rows[j][0], qg) for j in grp]
        qn_b = [qnorm[j][2].astype(bf16) for j in grp]
        ss = [_dot(qn_b[j], kn_b[j], NT_DIMS) * (HD ** -0.5) + bias_ref[GQ * j:GQ * (j + 1)].reshape(GQ * BLK, 2 * BLK) for j in grp]
        sm = [_softmax_with_sink(jnp.where(mask, ss[j], -jnp.inf), rows[j][1]) for j in grp]
        do_b = [jnp.concatenate([do_ref[:, pl.ds(HD * (GQ * j + g), HD)] for g in range(GQ)], axis=0).astype(bf16) for j in grp]
        dps = [_dot(do_b[j], vbs[j], NT_DIMS) for j in grp]
        deltas = [jnp.sum(sm[j][0] * dps[j], axis=-1, keepdims=True) for j in grp]
        dss = [sm[j][0] * (dps[j] - deltas[j]) for j in grp]
        ds_b = [(dss[j] * (HD ** -0.5)).astype(bf16) for j in grp]
        dqn = [_dot(ds_b[j], kn_b[j], NN_DIMS) for j in grp]
        dkn = [_dot(ds_b[j], qn_b[j], TN_DIMS) for j in grp]
        dvs = [_dot(sm[j][0].astype(bf16), do_b[j], TN_DIMS) for j in grp]
        dqg = jnp.zeros((1, HD), f32)
        dkg = jnp.zeros((1, HD), f32)
        dsink = jnp.zeros((1, 128), f32)
        for j in grp:
            dbias_ref[GQ * j:GQ * (j + 1)] += dss[j].reshape(GQ, BLK, 2 * BLK)
            dsk = sm[j][1] * deltas[j]
            for g in range(GQ):
                dsink = dsink + jnp.where(lane == GQ * j + g, -_sum11(dsk[BLK * g:BLK * (g + 1), :]), 0.0)
            qhat, rq, _ = qnorm[j]
            w = dqn[j] * qg
            dq = rq * (w - qhat * _row_mean(qhat * w))
            for g in range(GQ):
                dqkv_ref[:, pl.ds(HD * (GQ * j + g), HD)] = dq[BLK * g:BLK * (g + 1), :].astype(bf16)
            dqg = dqg + jnp.sum(dqn[j] * qhat, axis=0, keepdims=True)
            khat, rk, _ = knorm[j]
            w = dkn[j] * kg
            dk = rk * (w - khat * _row_mean(khat * w))
            dkg = dkg + jnp.sum(dkn[j] * khat, axis=0, keepdims=True)
            dqkv_ref[:, pl.ds(D_ATTN + HD * j, HD)] = (dk[BLK:, :] + carry[:, pl.ds(HD * j, HD)]).astype(bf16)
            dqkv_ref[:, pl.ds(D_ATTN + 128 + HD * j, HD)] = (dvs[j][BLK:, :] + carry[:, pl.ds(128 + HD * j, HD)]).astype(bf16)
            carry[:, pl.ds(HD * j, HD)] = dk[:BLK, :]
            carry[:, pl.ds(128 + HD * j, HD)] = dvs[j][:BLK, :]
        dsm_ref[0:1, 0:HD] += dqg
        dsm_ref[1:2, 0:HD] += dkg
        dsm_ref[2:3, :] += dsink

    rev = lambda i: NBLK - 1 - i
    prev = lambda i: jnp.maximum(NBLK - 2 - i, 0)
    small = lambda shape: pl.BlockSpec(shape, lambda i: (0,) * len(shape))
    return pl.pallas_call(
        body, grid=(NBLK,),
        in_specs=[pl.BlockSpec((BLK, D_ATTN), lambda i: (rev(i), 0)),
                  pl.BlockSpec((BLK, 128), lambda i: (rev(i), 4)), pl.BlockSpec((BLK, 128), lambda i: (prev(i), 4)),
                  pl.BlockSpec((BLK, 128), lambda i: (rev(i), 5)), pl.BlockSpec((BLK, 128), lambda i: (prev(i), 5)),
                  pl.BlockSpec((BLK, D_ATTN), lambda i: (rev(i), 0)),
                  small((DEPTH, HD)), small((DEPTH, HD)), small((DEPTH, NQ)), small((NQ, BLK, 2 * BLK))],
        out_specs=[pl.BlockSpec((BLK, 768), lambda i: (rev(i), COL_QKV // 768)), small((NQ, BLK, 2 * BLK)), small((8, 128))],
        out_shape=_out_hbm([SDS((S, D_IN_PAD), bf16), SDS((NQ, BLK, 2 * BLK), f32), SDS((8, 128), f32)]),
        scratch_shapes=[pltpu.VMEM((BLK, 256), f32)], name="attn_bwd", compiler_params=_cparams(1),
    )(*_in_hbm([qkv, qkv, qkv, qkv, qkv, dmix, q_gain, k_gain, sinks, bias]))


CONV_TC = 128


def _shift_down(u, s):
    if s == 0:
        return u
    rows = lax.broadcasted_iota(jnp.int32, u.shape, 0)
    return jnp.where(rows >= s, pltpu.roll(u, s, 0), 0.0)


def _shift_up(u, s):
    if s == 0:
        return u
    rows = lax.broadcasted_iota(jnp.int32, u.shape, 0)
    return jnp.where(rows < u.shape[0] - s, pltpu.roll(u, u.shape[0] - s, 0), 0.0)


def _conv_specs():
    return [pl.BlockSpec((S, CONV_TC), lambda c: (0, c)),
            pl.BlockSpec((None, 4, CONV_TC), lambda c: (0, 0, c)),
            pl.BlockSpec((DEPTH, CONV_TC), lambda c: (0, c))]


def _conv_pre(u, w_ref, b_ref, layer):
    pre = b_ref[layer:layer + 1, :] + w_ref[3:4, :] * u
    for k in range(3):
        pre = pre + w_ref[k:k + 1, :] * _shift_down(u, 3 - k)
    return pre


def _conv_fwd(xbc, conv_w, conv_b, layer):
    def body(u_ref, w_ref, b_ref, o_ref):
        pre = _conv_pre(u_ref[...], w_ref, b_ref, layer)
        o_ref[...] = pre * _sigmoid(pre)

    specs = _conv_specs()
    specs[1] = pl.BlockSpec((None, 4, CONV_TC), lambda c: (layer, 0, c))
    return pl.pallas_call(
        body, grid=(D_CONV // CONV_TC,), in_specs=specs, out_specs=pl.BlockSpec((S, CONV_TC), lambda c: (0, c)),
        out_shape=_out_hbm(SDS((S, D_CONV), f32)), name="conv_fwd", compiler_params=_cparams(1),
    )(*_in_hbm([xbc, conv_w, conv_b]))


def _conv_bwd(xbc, dact, conv_w, conv_b, dproj, layer):
    def body(u_ref, w_ref, b_ref, da_ref, dproj_in, du_ref, dw_ref, db_ref):
        u = u_ref[...]
        pre = _conv_pre(u, w_ref, b_ref, layer)
        sg = _sigmoid(pre)
        dpre = da_ref[...] * (sg * (1.0 + pre * (1.0 - sg)))
        du = w_ref[3:4, :] * dpre
        for k in range(3):
            du = du + w_ref[k:k + 1, :] * _shift_up(dpre, 3 - k)
        du_ref[...] = du.astype(bf16)
        db_ref[...] = jnp.broadcast_to(jnp.sum(dpre, axis=0, keepdims=True), db_ref.shape)
        dw_ref[...] = jnp.zeros_like(dw_ref)
        for k in range(4):
            dw_ref[k:k + 1, :] = jnp.sum(dpre * _shift_down(u, 3 - k), axis=0, keepdims=True)

    specs = _conv_specs()
    specs[1] = pl.BlockSpec((None, 4, CONV_TC), lambda c: (layer, 0, c))
    col = pl.BlockSpec((S, CONV_TC), lambda c: (0, c))
    row8 = pl.BlockSpec((8, CONV_TC), lambda c: (0, c))
    return pl.pallas_call(
        body, grid=(D_CONV // CONV_TC,), in_specs=[*specs, col, ANY_SPEC],
        out_specs=[pl.BlockSpec((S, CONV_TC), lambda c: (0, COL_XBC // CONV_TC + c)), row8, row8],
        out_shape=_out_hbm([SDS((S, D_IN_PAD), bf16), SDS((8, D_CONV), f32), SDS((8, D_CONV), f32)]), name="conv_bwd",
        input_output_aliases={4: 0}, compiler_params=_cparams(1),
    )(*_in_hbm([xbc, conv_w, conv_b, dact, dproj]))


def _tri():
    return (lax.broadcasted_iota(jnp.int32, (BLK, BLK), 0) >= lax.broadcasted_iota(jnp.int32, (BLK, BLK), 1))


def _ssd_scalars(dt_ref, dtb_ref, alog_ref, layer):
    raw = dt_ref[:, 0:NSSM] + dtb_ref[layer:layer + 1, :]
    dtv = jnp.maximum(raw, 0.0) + jnp.log(1.0 + jnp.exp(-jnp.abs(raw)))
    a = -jnp.exp(alog_ref[layer:layer + 1, :])
    acs = jnp.dot(_tri().astype(f32), dtv * a, preferred_element_type=f32, precision=HIGHEST)
    return raw, dtv, a, acs


HG = NSSM // NGRP
GW = HG * HD


def _lane_expand(cols, g):
    lane_head = lax.broadcasted_iota(jnp.int32, (1, GW), 1) // HD
    out = cols[:, HG * g + HG - 1:HG * g + HG]
    for r in range(HG - 2, -1, -1):
        out = jnp.where(lane_head == r, cols[:, HG * g + r:HG * g + r + 1], out)
    return out


def _row_expand(vals, g):
    row_head = lax.broadcasted_iota(jnp.int32, (GW, 1), 0) // HD
    out = vals[:, HG * g + HG - 1:HG * g + HG]
    for r in range(HG - 2, -1, -1):
        out = jnp.where(row_head == r, vals[:, HG * g + r:HG * g + r + 1], out)
    return out


def _head_rowsums(a):
    sel = (lax.broadcasted_iota(jnp.int32, (GW, HG), 0) // HD == lax.broadcasted_iota(jnp.int32, (GW, HG), 1)).astype(bf16)
    hi = a.astype(bf16)
    lo = (a - hi.astype(f32)).astype(bf16)
    sums = _dot(hi, sel, NN_DIMS) + _dot(lo, sel, NN_DIMS)
    return [sums[:, r:r + 1] for r in range(HG)]


def _ssd_chunk_common(xc_ref, dt_ref, dtb_ref, alog_ref, h_rows, layer):
    raw, dtv, a, acs = _ssd_scalars(dt_ref, dtb_ref, alog_ref, layer)
    acs_t = acs.T
    last = acs[BLK - 1:BLK, :]
    c = dict(raw=raw, dtv=dtv, a=a, acs=acs, last=last, dte=jnp.exp(last - acs), e_all=jnp.exp(acs), cd=jnp.exp(last))
    grp, heads, tri = range(NGRP), range(NSSM), _tri()
    c["bm"] = [xc_ref[:, pl.ds(D_SSM + NSTATE * g, NSTATE)] for g in grp]
    c["bm_b"] = [c["bm"][g].astype(bf16) for g in grp]
    c["cm_b"] = [xc_ref[:, pl.ds(D_SSM + NGRP * NSTATE + NSTATE * g, NSTATE)].astype(bf16) for g in grp]
    c["cb"] = [_dot(c["cm_b"][g], c["bm_b"][g], NT_DIMS) for g in grp]
    c["x"] = [xc_ref[:, pl.ds(GW * g, GW)] for g in grp]
    c["dt"] = [_lane_expand(dtv, g) for g in grp]
    c["xdt"] = [c["x"][g] * c["dt"][g] for g in grp]
    c["xdt_b"] = [c["xdt"][g].astype(bf16) for g in grp]
    c["prev"] = [h_rows(g) for g in grp]
    c["prev_b"] = [c["prev"][g].astype(bf16) for g in grp]
    c["e"] = [_lane_expand(c["e_all"], g) for g in grp]
    c["y_off"] = [_dot(c["cm_b"][g], c["prev_b"][g], NT_DIMS) * c["e"][g] for g in grp]
    c["decay"] = [jnp.exp(jnp.where(tri, acs[:, h:h + 1] - acs_t[h:h + 1, :], -jnp.inf)) for h in heads]
    c["m"] = [c["cb"][h // HG] * c["decay"][h] for h in heads]
    c["m_b"] = [c["m"][h].astype(bf16) for h in heads]
    c["dte_x"] = [_lane_expand(c["dte"], g) for g in grp]
    c["xdte_b"] = [(c["xdt"][g] * c["dte_x"][g]).astype(bf16) for g in grp]
    return c


def _ssd_fwd(xact, z, dt, attn, dt_bias, a_log, d_skip, norm_g, layer):
    def body(xc_ref, z_ref, dt_ref, at_ref, dtb_ref, alog_ref, dsk_ref, ng_ref, mix_ref, hs_ref, y_ref, h_ref):
        n = pl.program_id(0)

        @pl.when(n == 0)
        def _():
            h_ref[...] = jnp.zeros_like(h_ref)

        hs_ref[...] = h_ref[...]
        c = _ssd_chunk_common(xc_ref, dt_ref, dtb_ref, alog_ref, lambda g: h_ref[pl.ds(GW * g, GW), :], layer)
        grp, heads = range(NGRP), range(NSSM)
        y_diag = [_dot(c["m_b"][h], c["xdt_b"][h // HG][:, HD * (h % HG):HD * (h % HG + 1)], NN_DIMS) for h in heads]
        new_st = [_dot(c["xdte_b"][g], c["bm_b"][g], TN_DIMS) for g in grp]
        for h in heads:
            y_ref[:, pl.ds(HD * h, HD)] = y_diag[h]
        dskip = dsk_ref[layer:layer + 1, :]
        for g in grp:
            cols = pl.ds(GW * g, GW)
            y_ref[:, cols] = y_ref[:, cols] + c["y_off"][g] + c["x"][g] * _lane_expand(dskip, g)
            h_ref[cols, :] = c["prev"][g] * _row_expand(c["cd"], g) + new_st[g]
        zv = z_ref[...]
        yz = y_ref[...] * (zv * _sigmoid(zv))
        mix_ref[:, 0:D_ATTN] = at_ref[...]
        for g in grp:
            yg = yz[:, GW * g:GW * (g + 1)]
            rs = lax.rsqrt(jnp.mean(yg * yg, axis=-1, keepdims=True) + EPS)
            mix_ref[:, D_ATTN + GW * g:D_ATTN + GW * (g + 1)] = (yg * rs * ng_ref[layer:layer + 1, GW * g:GW * (g + 1)]).astype(bf16)

    small = lambda shape: pl.BlockSpec(shape, lambda n: (0,) * len(shape))
    return pl.pallas_call(
        body, grid=(NBLK,),
        in_specs=[pl.BlockSpec((BLK, D_CONV), lambda n: (n, 0)), pl.BlockSpec((BLK, D_SSM), lambda n: (n, 0)),
                  pl.BlockSpec((BLK, 128), lambda n: (n, 0)), pl.BlockSpec((BLK, D_ATTN), lambda n: (n, 0)),
                  small((DEPTH, NSSM)), small((DEPTH, NSSM)), small((DEPTH, NSSM)), small((DEPTH, D_SSM))],
        out_specs=[pl.BlockSpec((BLK, D), lambda n: (n, 0)), pl.BlockSpec((None, NSSM * HD, NSTATE), lambda n: (n, 0, 0)),
                   pl.BlockSpec((BLK, D_SSM), lambda n: (n, 0))],
        out_shape=_out_hbm([SDS((S, D), bf16), SDS((NBLK, NSSM * HD, NSTATE), f32), SDS((S, D_SSM), f32)]),
        scratch_shapes=[pltpu.VMEM((NSSM * HD, NSTATE), f32)],
        name="ssd_fwd", compiler_params=_cparams(1),
    )(*_in_hbm([xact, z, dt, attn, dt_bias, a_log, d_skip, norm_g]))


def _ssd_bwd(xact, z, dt, dmix, hs, y, dt_bias, a_log, d_skip, norm_g, dproj, layer):
    def body(xc_ref, z_ref, dt_ref, do_ref, hs_ref, y_ref, dtb_ref, alog_ref, dsk_ref, ng_ref, dproj_in,
             dzdt_ref, dx_ref, dsm_ref, dh_ref, dy_ref):
        i = pl.program_id(0)

        @pl.when(i == 0)
        def _():
            dh_ref[...] = jnp.zeros_like(dh_ref)
            dsm_ref[...] = jnp.zeros_like(dsm_ref)

        c = _ssd_chunk_common(xc_ref, dt_ref, dtb_ref, alog_ref, lambda g: hs_ref[pl.ds(GW * g, GW), :], layer)
        raw, dtv, a = c["raw"], c["dtv"], c["a"]
        grp, heads = range(NGRP), range(NSSM)
        dskip = dsk_ref[layer:layer + 1, :]
        lane8 = lax.broadcasted_iota(jnp.int32, (1, NSSM), 1)
        sub8 = lax.broadcasted_iota(jnp.int32, (NSSM, 1), 0)

        zv = z_ref[...]
        sz = _sigmoid(zv)
        gz = zv * sz
        yv = y_ref[...]
        yz = yv * gz
        for g in grp:
            sl = slice(GW * g, GW * (g + 1))
            yg = yz[:, sl]
            rs = lax.rsqrt(jnp.mean(yg * yg, axis=-1, keepdims=True) + EPS)
            yhat = yg * rs
            dog = do_ref[:, sl]
            w = dog * ng_ref[layer:layer + 1, sl]
            dyz = rs * (w - yhat * jnp.mean(yhat * w, axis=-1, keepdims=True))
            dsm_ref[0:1, sl] += jnp.sum(dog * yhat, axis=0, keepdims=True)
            dy_ref[:, sl] = dyz * gz[:, sl]
            dzdt_ref[:, sl] = (dyz * yv[:, sl] * (sz[:, sl] * (1.0 + zv[:, sl] * (1.0 - sz[:, sl])))).astype(bf16)

        dy = [dy_ref[:, pl.ds(GW * g, GW)] for g in grp]
        dy_b = [dy[g].astype(bf16) for g in grp]
        hl = lambda h: slice(HD * (h % HG), HD * (h % HG + 1))
        dt_off_b = [(dy[g] * c["e"][g]).astype(bf16) for g in grp]
        dcm = [_dot(dt_off_b[g], c["prev_b"][g], NN_DIMS) for g in grp]
        dprev = [_dot(dt_off_b[g], c["cm_b"][g], TN_DIMS) for g in grp]
        yoff_rs = [_head_rowsums(dy[g] * c["y_off"][g]) for g in grp]
        dhn = [dh_ref[pl.ds(GW * g, GW), :] for g in grp]
        dhn_b = [dhn[g].astype(bf16) for g in grp]
        dprev = [dprev[g] + dhn[g] * _row_expand(c["cd"], g) for g in grp]
        dhn_prev = [dhn[g] * c["prev"][g] for g in grp]
        u = [_dot(c["bm_b"][g], dhn_b[g], NT_DIMS) for g in grp]
        dbm = [_dot(c["xdte_b"][g], dhn_b[g], NN_DIMS) for g in grp]
        ddte_rs = [_head_rowsums(c["xdt"][g] * u[g]) for g in grp]
        dm = [_dot(dy_b[h // HG][:, hl(h)], c["xdt_b"][h // HG][:, hl(h)], NT_DIMS) for h in heads]
        dxdt_in = [_dot(c["m_b"][h], dy_b[h // HG][:, hl(h)], TN_DIMS) for h in heads]
        dseg = [dm[h] * c["m"][h] for h in heads]
        dmd = [dm[h] * c["decay"][h] for h in heads]
        for h in heads:
            dx_ref[:, pl.ds(HD * h, HD)] = dxdt_in[h]

        dacs = jnp.zeros((BLK, NSSM), f32)
        dacs_cols = jnp.zeros((NSSM, BLK), f32)
        dlast = jnp.zeros((1, NSSM), f32)
        ddtv = jnp.zeros((BLK, NSSM), f32)
        ddsk = jnp.zeros((1, NSSM), f32)
        for g in grp:
            cols = pl.ds(GW * g, GW)
            dxdt = dx_ref[:, cols] + u[g] * c["dte_x"][g]
            dx_ref[:, cols] = dy[g] * _lane_expand(dskip, g) + dxdt * c["dt"][g]
            ddtv_rs = _head_rowsums(dxdt * c["x"][g])
            ddsk_rs = _head_rowsums(dy[g] * c["x"][g])
            dcb = dmd[HG * g]
            for r in range(1, HG):
                dcb = dcb + dmd[HG * g + r]
            dcb_b = dcb.astype(bf16)
            dx_ref[:, pl.ds(D_SSM + NSTATE * g, NSTATE)] = dbm[g] + _dot(dcb_b, c["cm_b"][g], TN_DIMS)
            dx_ref[:, pl.ds(D_SSM + NGRP * NSTATE + NSTATE * g, NSTATE)] = dcm[g] + _dot(dcb_b, c["bm_b"][g], NN_DIMS)
            dh_ref[cols, :] = dprev[g]
            for r in range(HG):
                h = HG * g + r
                oh = (lane8 == h).astype(f32)
                tmp = ddte_rs[g][r] * c["dte"][:, h:h + 1]
                dacs = dacs + oh * (jnp.sum(dseg[h], axis=1, keepdims=True) + yoff_rs[g][r] - tmp)
                dacs_cols = dacs_cols + (sub8 == h).astype(f32) * jnp.sum(dseg[h], axis=0, keepdims=True)
                dlast = dlast + oh * (_sum11(dhn_prev[g][HD * r:HD * (r + 1), :]) * c["cd"][:, h:h + 1] + _sum11(tmp))
                ddtv = ddtv + oh * ddtv_rs[r]
                ddsk = ddsk + oh * _sum11(ddsk_rs[r])

        row = lax.broadcasted_iota(jnp.int32, (BLK, 1), 0)
        dacs = dacs - dacs_cols.T + jnp.where(row == BLK - 1, dlast, 0.0)
        dda = lax.dot_general(_tri().astype(f32), dacs, TN_DIMS, preferred_element_type=f32, precision=HIGHEST)
        ddtv = ddtv + dda * a
        da = jnp.sum(dda * dtv, axis=0, keepdims=True)
        draw = ddtv * _sigmoid(raw)
        dzdt_ref[:, D_SSM:] = jnp.zeros((BLK, COL_XBC - COL_DT), bf16)
        dzdt_ref[:, D_SSM:D_SSM + NSSM] = draw.astype(bf16)
        dsm_ref[1:2, 0:NSSM] += jnp.sum(draw, axis=0, keepdims=True)
        dsm_ref[2:3, 0:NSSM] += da * a
        dsm_ref[3:4, 0:NSSM] += ddsk

    rev = lambda i: NBLK - 1 - i
    small = lambda shape: pl.BlockSpec(shape, lambda i: (0,) * len(shape))
    return pl.pallas_call(
        body, grid=(NBLK,),
        in_specs=[pl.BlockSpec((BLK, D_CONV), lambda i: (rev(i), 0)), pl.BlockSpec((BLK, D_SSM), lambda i: (rev(i), 0)),
                  pl.BlockSpec((BLK, 128), lambda i: (rev(i), 0)), pl.BlockSpec((BLK, D_SSM), lambda i: (rev(i), 1)),
                  pl.BlockSpec((None, NSSM * HD, NSTATE), lambda i: (rev(i), 0, 0)), pl.BlockSpec((BLK, D_SSM), lambda i: (rev(i), 0)),
                  small((DEPTH, NSSM)), small((DEPTH, NSSM)), small((DEPTH, NSSM)), small((DEPTH, D_SSM)), ANY_SPEC],
        out_specs=[pl.BlockSpec((BLK, COL_XBC - COL_Z), lambda i: (rev(i), COL_Z // (COL_XBC - COL_Z))),
                   pl.BlockSpec((BLK, D_CONV), lambda i: (rev(i), 0)), small((8, D_SSM))],
        out_shape=_out_hbm([SDS((S, D_IN_PAD), bf16), SDS((S, D_CONV), f32), SDS((8, D_SSM), f32)]),
        scratch_shapes=[pltpu.VMEM((NSSM * HD, NSTATE), f32), pltpu.VMEM((BLK, D_SSM), f32)],
        name="ssd_bwd", input_output_aliases={10: 0}, compiler_params=_cparams(1),
    )(*_in_hbm([xact, z, dt, dmix, hs, y, dt_bias, a_log, d_skip, norm_g, dproj]))


def _my_place():
    return lax.axis_index("x"), lax.axis_index("y"), lax.axis_index("c")


def _dev_index(px, py, pc):
    return 4 * px + 2 * py + pc


def _slab2(kind, ref, idx):
    if kind == "stack":
        return ref.at[idx]
    if kind == "rows128":
        return ref.at[pl.ds(pl.multiple_of(idx * 128, 128), 128), :]
    if kind == "rows512":
        return ref.at[pl.ds(pl.multiple_of(idx * 512, 512), 512), :]
    return ref.at[:, pl.ds(pl.multiple_of(idx * 512, 512), 512)]


def _slab_shape(kind, full_shape):
    if kind == "stack":
        return tuple(full_shape[1:])
    if kind == "rows128":
        return (128, full_shape[1])
    if kind == "rows512":
        return (512, full_shape[1])
    return (full_shape[0], 512)


KIND = dict(w_in="stack", w_out="rows128", w_up="cols512", w_down="rows512", conv_w="stack")
FULL_SHAPE = dict(w_in=(N_DEV, D, D_IN // N_DEV), w_out=(D, D), w_up=(D, D_FF), w_down=(D_FF, D))
HBM_SPEC = pl.BlockSpec(memory_space=pltpu.HBM)
SEM_SPEC = pl.BlockSpec(memory_space=pltpu.SEMAPHORE)
SIDE_EFFECT = pltpu.SideEffectType.DATAFLOW_SIDE_EFFECTING


def _peers_all():
    x, y, c = _my_place()
    return [(x ^ ((r >> 2) & 1), y ^ ((r >> 1) & 1), c ^ (r & 1)) for r in range(1, N_DEV)]


def _split_start(name, bufs, n_copies, plan, deps=()):
    nb = len(bufs)

    def body(*refs):
        ins = refs[:nb]
        send_sems, recv_sems = refs[nb + len(deps)], refs[nb + len(deps) + 1]
        token = refs[-1]
        for i, (src, dst, dev) in enumerate(plan(ins)):
            pltpu.make_async_remote_copy(src_ref=src, dst_ref=dst, send_sem=send_sems.at[i], recv_sem=recv_sems.at[i],
                                         device_id=dev, device_id_type=MESH).start()
        token[...] = jnp.zeros_like(token)

    outs = pl.pallas_call(
        body, name=name,
        out_shape=(pltpu.SemaphoreType.DMA((n_copies,)), pltpu.SemaphoreType.DMA((n_copies,)),
                   *[pltpu.HBM(b.shape, b.dtype) for b in bufs], SDS((8, 128), f32)),
        in_specs=[HBM_SPEC] * nb + [ANY_SPEC] * len(deps),
        out_specs=(SEM_SPEC, SEM_SPEC, *[HBM_SPEC] * nb, pl.BlockSpec(memory_space=pltpu.VMEM)),
        input_output_aliases={i: 2 + i for i in range(nb)},
        compiler_params=pltpu.CompilerParams(has_side_effects=SIDE_EFFECT),
    )(*[pltpu.with_memory_space_constraint(b, pltpu.HBM) for b in bufs], *deps)
    return dict(send=outs[0], recv=outs[1], bufs=list(outs[2:2 + nb]), token=outs[-1], plan=plan, n=n_copies)


def _split_wait(name, started, after):
    bufs = started["bufs"]
    nb = len(bufs)
    plan = started["plan"]

    def body(*refs):
        ins = refs[:nb]
        send_sems, recv_sems = refs[nb], refs[nb + 1]
        for i, (src, dst, dev) in enumerate(plan(ins)):
            cp = pltpu.make_async_remote_copy(src_ref=src, dst_ref=dst, send_sem=send_sems.at[i], recv_sem=recv_sems.at[i],
                                              device_id=dev, device_id_type=MESH)
            cp.wait_send()
            cp.wait_recv()

    outs = pl.pallas_call(
        body, name=name, out_shape=tuple(pltpu.HBM(b.shape, b.dtype) for b in bufs),
        in_specs=[HBM_SPEC] * nb + [SEM_SPEC, SEM_SPEC] + [ANY_SPEC] * len(after), out_specs=(HBM_SPEC,) * nb,
        input_output_aliases={i: i for i in range(nb)},
        compiler_params=pltpu.CompilerParams(has_side_effects=SIDE_EFFECT),
    )(*bufs, started["send"], started["recv"], *after)
    return list(outs)


def _gather_start(name, names, fulls, deps):
    n_t = len(names)

    def plan(refs):
        x, y, c = _my_place()
        my_idx = _dev_index(x, y, c)
        targets = [(x, y, 1 - c), (1 - x, y, c), (x, 1 - y, c), (1 - x, 1 - y, c)]
        slabs = [_slab2(KIND[names[t]], refs[t], my_idx) for t in range(n_t)]
        return [(slabs[t], slabs[t], dev) for t in range(n_t) for dev in targets]

    return _split_start(name, list(fulls), 4 * n_t, plan, deps)


def _gather_finish(name, names, started, after):
    n_t = len(names)
    fulls = _split_wait(name + "_wait", started, after)
    slab_shapes = [SDS(_slab_shape(KIND[n], f.shape), f.dtype) for n, f in zip(names, fulls)]

    def body(*refs):
        ins = refs[:n_t]
        outs = refs[n_t:2 * n_t]
        stage = refs[2 * n_t:3 * n_t]
        load_sems, send_sems, recv_sems = refs[3 * n_t:]
        x, y, c = _my_place()
        chips = [(1 - x, y), (x, 1 - y), (1 - x, 1 - y)]
        pairs = [(t, j) for t in range(n_t) for j in range(3)]
        loads = [pltpu.make_async_copy(_slab2(KIND[names[t]], ins[t], _dev_index(*chips[j], c)), stage[t].at[j], load_sems.at[t, j])
                 for t, j in pairs]
        for cp in loads:
            cp.start()

        def copy(t, j, core):
            return pltpu.make_async_remote_copy(
                src_ref=stage[t].at[j], dst_ref=_slab2(KIND[names[t]], outs[t], _dev_index(*chips[j], core)),
                send_sem=send_sems.at[t, j], recv_sem=recv_sems.at[t, j], device_id=(x, y, 1 - c), device_id_type=MESH)

        sends = [copy(t, j, c) for t, j in pairs]
        for ld, cp in zip(loads, sends):
            ld.wait()
            cp.start()
        for t, j in pairs:
            copy(t, j, 1 - c).wait_recv()
        for cp in sends:
            cp.wait_send()

    return pl.pallas_call(
        body, in_specs=[ANY_SPEC] * n_t, out_specs=[ANY_SPEC] * n_t, out_shape=[SDS(b.shape, b.dtype) for b in fulls],
        input_output_aliases={t: t for t in range(n_t)},
        scratch_shapes=[pltpu.VMEM((3,) + s.shape, s.dtype) for s in slab_shapes]
        + [pltpu.SemaphoreType.DMA((n_t, 3)), pltpu.SemaphoreType.DMA((n_t, 3)), pltpu.SemaphoreType.DMA((n_t, 3))],
        name=name + "_pass", compiler_params=pltpu.CompilerParams(vmem_limit_bytes=VMEM_LIMIT),
    )(*fulls)


def _exchange_start(name, names, grads, deps):
    n_t = len(names)
    lands = [lax.empty((N_DEV,) + _slab_shape(KIND[n], g.shape), g.dtype) for n, g in zip(names, grads)]

    def plan(refs):
        my_idx = _dev_index(*_my_place())
        return [(_slab2(KIND[names[t]], refs[t], _dev_index(*peer)), refs[n_t + t].at[my_idx], peer)
                for t in range(n_t) for peer in _peers_all()]

    return _split_start(name, list(grads) + lands, 7 * n_t, plan, deps)


def _small_exchange_start(part, deps):
    land = lax.empty((N_DEV,) + part.shape, part.dtype)

    def plan(refs):
        my_idx = _dev_index(*_my_place())
        return [(refs[0], refs[1].at[my_idx], peer) for peer in _peers_all()]

    return _split_start("small_exchange", [part, land], N_DEV - 1, plan, deps)


def _slab_pieces():
    sh = D_IN // N_DEV
    out = []
    for j in range(N_DEV):
        for first, end, dst in IN_SEGMENTS:
            lo, hi = max(first, sh * j), min(end, sh * (j + 1))
            if lo < hi:
                out.append((j, lo - sh * j, hi - sh * j, dst + lo - first))
    return out


def _w_in_assemble(stacked):
    tr = 256
    sh = D_IN // N_DEV

    def body(i_ref, o_ref):
        o_ref[:, COL_DT:COL_XBC] = jnp.zeros((tr, COL_XBC - COL_DT), bf16)
        for j, lo, hi, dst in _slab_pieces():
            o_ref[:, dst:dst + hi - lo] = i_ref[j, :, lo:hi]

    return pl.pallas_call(
        body, grid=(D // tr,), in_specs=[pl.BlockSpec((N_DEV, tr, sh), lambda i: (0, i, 0))],
        out_specs=pl.BlockSpec((None, tr, D_IN_PAD), lambda i: (0, i, 0)), out_shape=_out_hbm(SDS((1, D, D_IN_PAD), bf16)),
        name="w_in_assemble", compiler_params=_cparams(1),
    )(*_in_hbm([stacked]))


def _w_in_slabs(dw_in):
    tr = 256
    sh = D_IN // N_DEV

    def body(i_ref, o_ref):
        for j, lo, hi, src in _slab_pieces():
            o_ref[j, :, lo:hi] = i_ref[:, src:src + hi - lo]

    return pl.pallas_call(
        body, grid=(D // tr,), in_specs=[pl.BlockSpec((tr, D_IN_PAD), lambda i: (i, 0))],
        out_specs=pl.BlockSpec((N_DEV, tr, sh), lambda i: (0, i, 0)), out_shape=_out_hbm(SDS((N_DEV, D, sh), bf16)),
        name="w_in_slabs", compiler_params=_cparams(1),
    )(*_in_hbm([dw_in]))


SMALL_NAMES = ("mix_norm_g", "mlp_norm_g", "conv_b", "ssm_norm_g", "q_gain", "k_gain", "sinks", "dt_bias", "a_log", "d_skip",
               "rel_bias", "conv_w")
MISC_LANES = dict(q_gain=(LANE_QG, HD), k_gain=(LANE_KG, HD), sinks=(LANE_SINK, NQ), dt_bias=(LANE_DTB, NSSM),
                  a_log=(LANE_ALOG, NSSM), d_skip=(LANE_DSKIP, NSSM))


def _pack_small_grads(smalls, drel_t, loss):
    def body(*refs):
        o_ref = refs[-1]
        drel_ref, loss_ref = refs[-3], refs[-2]
        o_ref[...] = jnp.zeros_like(o_ref)
        for l in range(DEPTH):
            mixg, mlpg, convb, convw, ssd, attn = refs[6 * l:6 * l + 6]
            o_ref[ROW_MIXG + l:ROW_MIXG + l + 1, :] = mixg[...]
            o_ref[ROW_MLPG + l:ROW_MLPG + l + 1, :] = mlpg[...]
            o_ref[ROW_CONVB + l:ROW_CONVB + l + 1, :] = convb[0:1, :]
            o_ref[ROW_SSMG + l:ROW_SSMG + l + 1, 0:D_SSM] = ssd[0:1, :]
            o_ref[ROW_CONVW + 4 * l:ROW_CONVW + 4 * l + 4, :] = convw[0:4, :]
            row = slice(ROW_MISC + l, ROW_MISC + l + 1)
            o_ref[row, LANE_QG:LANE_QG + HD] = attn[0:1, 0:HD]
            o_ref[row, LANE_KG:LANE_KG + HD] = attn[1:2, 0:HD]
            o_ref[row, LANE_SINK:LANE_SINK + NQ] = attn[2:3, 0:NQ]
            o_ref[row, LANE_DTB:LANE_DTB + NSSM] = ssd[1:2, 0:NSSM]
            o_ref[row, LANE_ALOG:LANE_ALOG + NSSM] = ssd[2:3, 0:NSSM]
            o_ref[row, LANE_DSKIP:LANE_DSKIP + NSSM] = ssd[3:4, 0:NSSM]
        o_ref[ROW_RELB:ROW_RELB + NQ, 0:N_BUCKETS] = drel_ref[...]
        o_ref[ROW_LOSS:ROW_LOSS + 1, 0:1] = loss_ref[0:1, 0:1]

    args = []
    for sm in smalls:
        args += [sm["mix_norm_g"], sm["mlp_norm_g"], sm["conv_b"], sm["conv_w"], sm["ssd"], sm["attn"]]
    args += [drel_t, loss]
    return pl.pallas_call(body, out_shape=SDS((SMALL_ROWS, D), f32), name="pack_small_grads")(*args)


def _adamw_small(part, land, w, m, v):
    n = len(SMALL_NAMES)

    def grad_of(name, g_ref):
        if name == "mix_norm_g":
            return g_ref[ROW_MIXG:ROW_MIXG + DEPTH, :]
        if name == "mlp_norm_g":
            return g_ref[ROW_MLPG:ROW_MLPG + DEPTH, :]
        if name == "conv_b":
            return g_ref[ROW_CONVB:ROW_CONVB + DEPTH, :]
        if name == "ssm_norm_g":
            return g_ref[ROW_SSMG:ROW_SSMG + DEPTH, 0:D_SSM]
        if name == "rel_bias":
            return g_ref[ROW_RELB:ROW_RELB + NQ, 0:N_BUCKETS].T
        lane, width = MISC_LANES[name]
        return g_ref[ROW_MISC:ROW_MISC + DEPTH, lane:lane + width]

    def body(part_ref, land_ref, *refs):
        ws, ms, vs = refs[:n], refs[n:2 * n], refs[2 * n:3 * n]
        loss_ref = refs[3 * n]
        outs = refs[3 * n + 1:-1]
        g_ref = refs[-1]
        me = _dev_index(*_my_place())
        for p in range(N_DEV):
            term = jnp.where(me == p, part_ref[...], land_ref[p])
            if p == 0:
                g_ref[...] = term
            else:
                g_ref[...] += term
        loss_ref[...] = g_ref[ROW_LOSS:ROW_LOSS + 1, 0:128]
        my_cols = pl.ds(pl.multiple_of(me * 128, 128), 128)
        for k, name in enumerate(SMALL_NAMES):
            g_out, d_out, m_out, v_out = outs[4 * k:4 * k + 4]
            if name == "conv_w":
                for l in range(DEPTH):
                    g = g_ref[ROW_CONVW + 4 * l:ROW_CONVW + 4 * l + 4, my_cols]
                    delta, m_new, v_new = _adamw_math(ws[k][l], ms[k][l], vs[k][l], g)
                    g_out[l], d_out[l], m_out[l], v_out[l] = g, delta, m_new, v_new
            else:
                g = grad_of(name, g_ref)
                delta, m_new, v_new = _adamw_math(ws[k][...], ms[k][...], vs[k][...], g)
                g_out[...], d_out[...], m_out[...], v_out[...] = g, delta, m_new, v_new

    ws = [w[name] for name in SMALL_NAMES]
    out_shape = [SDS((1, 128), f32)]
    for a in ws:
        out_shape += [SDS(a.shape, f32)] * 4
    return pl.pallas_call(body, out_shape=out_shape, name="adamw_small", scratch_shapes=[pltpu.VMEM((SMALL_ROWS, D), f32)])(
        part, land, *ws, *[m[name] for name in SMALL_NAMES], *[v[name] for name in SMALL_NAMES])


def _plain(tm, tn):
    return pl.BlockSpec((tm, tn), lambda i, j, k: (i, j))


def _rowblk(tm, width):
    return pl.BlockSpec((tm, width), lambda i, j, k: (i, 0))


def _store_epi(dtype):
    def epi(acc, i, j, ex, outs):
        outs[0][...] = acc.astype(dtype)
    return epi


def _rms_prologue(layer):
    def pro(a_ref, ex, outs):
        xv = a_ref[...]
        r = lax.rsqrt(jnp.mean(xv * xv, axis=-1, keepdims=True) + EPS)
        h = (xv * r * ex[0][layer:layer + 1, :]).astype(bf16)
        outs[-1][...] = h
        return h
    return pro


MLP_TM = 256
MLP_VMEM = 56 * 1024 * 1024


def _resident(shape):
    return pl.BlockSpec((None,) + shape, lambda i: (0, 0, 0), pipeline_mode=pl.Buffered(1))


def _mlp_fwd(layer, x_mid, g, w_up, w_down, tgt=None):
    tm = MLP_TM
    with_loss = tgt is not None

    def body(x_ref, g_ref, wu_ref, wd_ref, *rest):
        a_ref, r_ref, h_ref = rest[with_loss:with_loss + 3]
        i = pl.program_id(0)
        xv = x_ref[...]
        h = (xv * lax.rsqrt(jnp.mean(xv * xv, axis=-1, keepdims=True) + EPS) * g_ref[layer:layer + 1, :]).astype(bf16)
        h_ref[...] = h
        r = jnp.maximum(_dot(h, wu_ref[...], NN_DIMS), 0.0)
        a = (r * r).astype(bf16)
        a_ref[...] = a
        r_ref[...] = r.astype(bf16)
        y = xv + _dot(a, wd_ref[...], NN_DIMS)
        if not with_loss:
            rest[3][...] = y
            return
        err = y - rest[0][...]
        rest[4][...] = err * (1.0 / D)
        part = 0.5 * jnp.sum(jnp.mean(err * err, axis=-1, keepdims=True), axis=0, keepdims=True)

        @pl.when(i == 0)
        def _():
            rest[5][...] = jnp.zeros_like(rest[5])

        rest[5][...] += jnp.broadcast_to(part, rest[5].shape)

    row = lambda width: pl.BlockSpec((tm, width), lambda i: (i, 0))
    in_specs = [row(D), pl.BlockSpec((DEPTH, D), lambda i: (0, 0)), _resident((D, D_FF)), _resident((D_FF, D))]
    out_specs = [row(D_FF), row(D_FF), row(D), row(D)]
    out_shape = [SDS((S, D_FF), bf16), SDS((S, D_FF), bf16), SDS((S, D), bf16), SDS((S, D), f32)]
    args = [x_mid, g, w_up, w_down]
    if with_loss:
        in_specs.append(row(D))
        args.append(tgt)
        out_specs.append(pl.BlockSpec((1, 128), lambda i: (0, 0)))
        out_shape.append(SDS((1, 128), f32))
    return pl.pallas_call(
        body, grid=(S // tm,), in_specs=in_specs, out_specs=out_specs, out_shape=_out_hbm(out_shape),
        name="mlp_fwd_loss" if with_loss else "mlp_fwd",
        compiler_params=pltpu.CompilerParams(dimension_semantics=("arbitrary",), vmem_limit_bytes=MLP_VMEM),
    )(*_in_hbm(args))


def _mlp_bwd_act(layer, dx_out, r_act, x_mid, g, w_down, w_up, deps):
    tm = MLP_TM

    def body(dxo_ref, r_ref, xm_ref, g_ref, wd_ref, wu_ref, *rest):
        du_ref, dx_ref, dg_ref = rest[len(deps):]
        dxo = dxo_ref[...]
        du = (_dot(dxo.astype(bf16), wd_ref[...], NT_DIMS) * (2.0 * r_ref[...].astype(f32))).astype(bf16)
        du_ref[...] = du
        dh = _dot(du, wu_ref[...], NT_DIMS)
        _rms_bwd_epilogue(layer)(dh, pl.program_id(0), 0, (xm_ref, g_ref, dxo_ref), (dx_ref, dg_ref))

    row = lambda width: pl.BlockSpec((tm, width), lambda i: (i, 0))
    return pl.pallas_call(
        body, grid=(S // tm,),
        in_specs=[row(D), row(D_FF), row(D), pl.BlockSpec((DEPTH, D), lambda i: (0, 0)), _resident((D_FF, D)), _resident((D, D_FF))]
        + [ANY_SPEC] * len(deps),
        out_specs=[row(D_FF), row(D), pl.BlockSpec((1, D), lambda i: (0, 0))],
        out_shape=_out_hbm([SDS((S, D_FF), bf16), SDS((S, D), f32), SDS((1, D), f32)]), name="mlp_bwd_act",
        compiler_params=pltpu.CompilerParams(dimension_semantics=("arbitrary",), vmem_limit_bytes=MLP_VMEM),
    )(*_in_hbm([dx_out, r_act, x_mid, g, w_down, w_up, *deps]))


def _layer_fwd(l, x, p, get_weights, bias, tgt=None):
    wts = get_weights(l, "in", [x, bias])
    gfull = pl.BlockSpec((DEPTH, D), lambda i, j, k: (0, 0))
    tm = 256

    def inproj_epi(acc, i, j, ex, outs):
        outs[0][...] = acc[:, COL_QKV:COL_Z]
        outs[1][...] = acc[:, COL_Z:COL_DT]
        outs[2][...] = acc[:, COL_XBC:D_IN_PAD]
        outs[3][...] = acc[:, COL_DT:COL_DT + 128]

    qkv, z, xbc, dt, h1 = _matmul(
        "in_proj", "nn", x, wts["w_in"], tm=tm, tn=D_IN_PAD, tk=D, prologue=_rms_prologue(l),
        extras=(p["mix_norm_g"],), extra_specs=(gfull,),
        out_shape=[SDS((S, 768), f32), SDS((S, 512), f32), SDS((S, 1024), f32), SDS((S, 128), f32), SDS((S, D), bf16)],
        out_specs=[_rowblk(tm, 768), _rowblk(tm, 512), _rowblk(tm, 1024), _rowblk(tm, 128), _rowblk(tm, D)], epilogue=inproj_epi)
    attn = _attn_fwd(qkv, p["q_gain"], p["k_gain"], p["sinks"], bias, l)
    xact = _conv_fwd(xbc, wts["conv_w"], p["conv_b"], l)
    mix, hs, y_ssd = _ssd_fwd(xact, z, dt, attn, p["dt_bias"], p["a_log"], p["d_skip"], p["ssm_norm_g"], l)
    wts = dict(wts, **get_weights(l, "rest", [mix]))

    def resid_epi(acc, i, j, ex, outs):
        outs[0][...] = ex[0][...] + acc

    x_mid = _matmul("out_proj", "nn", mix, wts["w_out"], tm=tm, tn=D, tk=D, out_shape=SDS((S, D), f32),
                    out_specs=_plain(tm, D), epilogue=resid_epi, extras=(x,), extra_specs=(_plain(tm, D),))

    a_act, r_act, h2, *result = _mlp_fwd(l, x_mid, p["mlp_norm_g"], wts["w_up"], wts["w_down"], tgt)
    saved = dict(x=x, h1=h1, qkv=qkv, z=z, xbc=xbc, dt=dt, xact=xact, mix=mix, hs=hs, y_ssd=y_ssd, x_mid=x_mid, h2=h2,
                 a=a_act, r=r_act, wts=wts)
    return (result[0] if tgt is None else tuple(result)), saved


def _layer_bwd(l, dx_out, sv, p, bias, deps, send):
    wts = sv["wts"]

    du, dx_mid, dg_mlp = _mlp_bwd_act(l, dx_out, sv["r"], sv["x_mid"], p["mlp_norm_g"], wts["w_down"], wts["w_up"], deps)
    dw_down = _matmul("dw_down", "tn", sv["a"], dx_out, tm=1024, tn=D, tk=S, out_shape=SDS((D_FF, D), bf16),
                      out_specs=_plain(1024, D), epilogue=_store_epi(bf16))
    dw_up = _matmul("dw_up", "tn", sv["h2"], du, tm=D, tn=1024, tk=S, out_shape=SDS((D, D_FF), bf16),
                    out_specs=_plain(D, 1024), epilogue=_store_epi(bf16))
    dw_out = _matmul("dw_out", "tn", sv["mix"], dx_mid, tm=D, tn=512, tk=512, out_shape=SDS((D, D), bf16),
                     out_specs=_plain(D, 512), epilogue=_store_epi(bf16))
    deps = send(l, dict(w_down=dw_down, w_up=dw_up, w_out=dw_out))
    gfull = pl.BlockSpec((DEPTH, D), lambda i, j, k: (0, 0))
    grow = pl.BlockSpec((1, D), lambda i, j, k: (0, 0))
    dmix = _matmul("out_proj_da", "nt", dx_mid, wts["w_out"], tm=256, tn=D, tk=D, out_shape=SDS((S, D), f32),
                   out_specs=_plain(256, D), epilogue=_store_epi(f32), deps=deps)
    dproj, dbias, dsm_attn = _attn_bwd(sv["qkv"], dmix, p["q_gain"], p["k_gain"], p["sinks"], bias, l)
    dproj, dxact, dsm_ssd = _ssd_bwd(sv["xact"], sv["z"], sv["dt"], dmix, sv["hs"], sv["y_ssd"], p["dt_bias"], p["a_log"],
                                     p["d_skip"], p["ssm_norm_g"], dproj, l)
    dproj, dconv_w, dconv_b = _conv_bwd(sv["xbc"], dxact, wts["conv_w"], p["conv_b"], dproj, l)
    dw_in = _matmul("dw_in", "tn", sv["h1"], dproj, tm=D, tn=640, tk=S, out_shape=SDS((D, D_IN_PAD), bf16),
                    out_specs=_plain(D, 640), epilogue=_store_epi(bf16))
    deps = send(l, dict(w_in=_w_in_slabs(dw_in)))
    dx, dg_mix = _matmul(
        "in_proj_dh", "nt", dproj, wts["w_in"], tm=256, tn=D, tk=D_IN_PAD, out_shape=[SDS((S, D), f32), SDS((1, D), f32)],
        out_specs=[_plain(256, D), grow], epilogue=_rms_bwd_epilogue(l),
        extras=(sv["x"], p["mix_norm_g"], dx_mid), extra_specs=(_plain(256, D), gfull, _plain(256, D)), deps=deps)
    small = dict(mix_norm_g=dg_mix, mlp_norm_g=dg_mlp, conv_w=dconv_w, conv_b=dconv_b, ssd=dsm_ssd, attn=dsm_attn, dbias=dbias)
    return dx, small, deps


def _local_step(x, tgt, p, get_weights, send):
    onehot_t = jnp.asarray(_onehot_buckets())
    bias = _bias_build(p["rel_bias"].T, onehot_t).reshape(NQ, BLK, 2 * BLK)
    saved = []
    h = x
    for l in range(DEPTH):
        h, sv = _layer_fwd(l, h, p, get_weights, bias, tgt if l == DEPTH - 1 else None)
        saved.append(sv)
    dx, loss = h
    smalls = [None] * DEPTH
    deps = ()
    for l in reversed(range(DEPTH)):
        dx, smalls[l], deps = _layer_bwd(l, dx, saved[l], p, bias, deps, send)
    drel_t = _bias_grad(smalls[0]["dbias"].reshape(NQ, -1), smalls[1]["dbias"].reshape(NQ, -1), onehot_t)
    return dx, _pack_small_grads(smalls, drel_t, loss)


WEIGHT_ORDER = ("mix_norm_g", "w_in", "q_gain", "k_gain", "sinks", "rel_bias", "conv_w", "conv_b", "dt_bias", "a_log", "d_skip",
                "ssm_norm_g", "w_out", "mlp_norm_g", "w_up", "w_down")


def kernel(x, mix_norm_g, w_in, q_gain, k_gain, sinks, rel_bias, conv_w, conv_b, dt_bias, a_log, d_skip, ssm_norm_g, w_out, mlp_norm_g, w_up, w_down, loss_target, m_mix_norm_g, m_w_in, m_q_gain, m_k_gain, m_sinks, m_rel_bias, m_conv_w, m_conv_b, m_dt_bias, m_a_log, m_d_skip, m_ssm_norm_g, m_w_out, m_mlp_norm_g, m_w_up, m_w_down, v_mix_norm_g, v_w_in, v_q_gain, v_k_gain, v_sinks, v_rel_bias, v_conv_w, v_conv_b, v_dt_bias, v_a_log, v_d_skip, v_ssm_norm_g, v_w_out, v_mlp_norm_g, v_w_up, v_w_down):
    w = dict(mix_norm_g=mix_norm_g, w_in=w_in, q_gain=q_gain, k_gain=k_gain, sinks=sinks, rel_bias=rel_bias, conv_w=conv_w,
             conv_b=conv_b, dt_bias=dt_bias, a_log=a_log, d_skip=d_skip, ssm_norm_g=ssm_norm_g, w_out=w_out,
             mlp_norm_g=mlp_norm_g, w_up=w_up, w_down=w_down)
    m = dict(mix_norm_g=m_mix_norm_g, w_in=m_w_in, q_gain=m_q_gain, k_gain=m_k_gain, sinks=m_sinks, rel_bias=m_rel_bias,
             conv_w=m_conv_w, conv_b=m_conv_b, dt_bias=m_dt_bias, a_log=m_a_log, d_skip=m_d_skip, ssm_norm_g=m_ssm_norm_g,
             w_out=m_w_out, mlp_norm_g=m_mlp_norm_g, w_up=m_w_up, w_down=m_w_down)
    v = dict(mix_norm_g=v_mix_norm_g, w_in=v_w_in, q_gain=v_q_gain, k_gain=v_k_gain, sinks=v_sinks, rel_bias=v_rel_bias,
             conv_w=v_conv_w, conv_b=v_conv_b, dt_bias=v_dt_bias, a_log=v_a_log, d_skip=v_d_skip, ssm_norm_g=v_ssm_norm_g,
             w_out=v_w_out, mlp_norm_g=v_mlp_norm_g, w_up=v_w_up, w_down=v_w_down)
    big = ("w_in", "w_out", "w_up", "w_down")

    my_idx = _dev_index(*_my_place()).astype(jnp.int32).reshape(1)

    fulls = {n: _cast_to_full("cast_" + n, w[n], KIND[n], FULL_SHAPE[n], my_idx, bf16) for n in big}
    conv_full = _cast_to_full("cast_conv_w", conv_w.reshape(1, DEPTH * 4, 128), "stack", (N_DEV, DEPTH * 4, 128), my_idx, f32)[0]
    rest = ["w_out", "w_up", "w_down"]
    g0 = _gather_start("gather0", ["w_in", "conv_w"], [fulls["w_in"][0], conv_full], ())
    g1 = _gather_start("gather1", rest, [fulls[n][0] for n in rest], (g0["token"],))
    g2 = _gather_start("gather2", ["w_in"], [fulls["w_in"][1]], (g1["token"],))
    g3 = _gather_start("gather3", rest, [fulls[n][1] for n in rest], (g2["token"],))
    held = {}
    flat = lambda a: a.reshape(a.shape[0] * a.shape[1], a.shape[2])
    adam_in = {n: (flat(w[n]), flat(m[n]), flat(v[n])) for n in big}

    def get_weights(l, part, after):
        if l == 0 and part == "in":
            full_in, full_conv = _gather_finish("gather0", ["w_in", "conv_w"], g0,
                                                list(after) + [g3["token"], adam_in["w_in"][1], adam_in["w_in"][2]])
            held["conv_w"] = jnp.transpose(full_conv.reshape(N_DEV, DEPTH, 4, 128), (1, 2, 0, 3)).reshape(DEPTH, 4, D_CONV)
            return dict(w_in=_w_in_assemble(full_in), conv_w=held["conv_w"])
        if part == "in":
            return dict(w_in=_w_in_assemble(_gather_finish("gather2", ["w_in"], g2, after)[0]), conv_w=held["conv_w"])
        full = _gather_finish("gather1" if l == 0 else "gather3", rest, g1 if l == 0 else g3, after)
        return {n: f[None] for n, f in zip(rest, full)}

    pending = []

    def send(l, grads):
        names = list(grads)
        started = _exchange_start("exchange%d_%s" % (l, names[0]), names, [grads[n] for n in names], ())
        pending.append((l, names, started))
        return (started["token"],)

    dx, small_part = _local_step(x.reshape(S, D), loss_target.reshape(S, D), w, get_weights, send)

    small = _small_exchange_start(small_part, ())
    tiles = dict(w_in=256, w_out=128, w_up=256, w_down=256)
    outs_of = {n: None for n in big}
    after = [dx, small["token"]]
    for l, names, started in pending:
        bufs = _split_wait("exchange%d_%s_wait" % (l, names[0]), started, after)
        for t, n in enumerate(names):
            outs_of[n] = _adamw_layer("adamw_%s%d" % (n, l), KIND[n], l, *adam_in[n],
                                      bufs[len(names) + t], bufs[t], my_idx, outs_of[n], tiles[n])
        after = [outs_of[names[-1]][0]]
    res = {n: [o.reshape(w[n].shape) for o in outs_of[n]] for n in big}
    small_part, small_land = _split_wait("small_exchange_wait", small, after)
    small_outs = _adamw_small(small_part, small_land, w, m, v)
    loss = small_outs[0][0, 0]
    for k, name in enumerate(SMALL_NAMES):
        res[name] = small_outs[1 + 4 * k:5 + 4 * k]

    result = [loss, dx.reshape(1, S, D)]
    for k in range(4):
        result += [res[name][k] for name in WEIGHT_ORDER]
    return tuple(result)
```

```python
import functools
import math

import numpy as np
import jax
import jax.numpy as jnp
from jax import lax
from jax.experimental import pallas as pl
from jax.experimental.pallas import tpu as pltpu

f32 = jnp.float32
bf16 = jnp.bfloat16
SDS = jax.ShapeDtypeStruct
MESH = pl.DeviceIdType.MESH
HIGHEST = lax.Precision.HIGHEST

S = 2048
D = 1024
DEPTH = 2
BLK = 128
NBLK = S // BLK
HD = 64
NQ = 8
NKV = 2
NSSM = 8
NGRP = 2
NSTATE = 128
D_ATTN = 512
D_SSM = 512
D_CONV = 1024
D_FF = 4096
D_IN = 2312
D_IN_PAD = 2560
COL_QKV, COL_Z, COL_DT, COL_XBC = 0, 768, 1280, 1536
IN_SEGMENTS = ((0, 1280, 0), (1280, 2304, COL_XBC), (2304, 2312, COL_DT))
N_BUCKETS = 32
EPS = 1e-6
N_DEV = 8
VMEM_LIMIT = 48 * 1024 * 1024

ADAM_LR = 0.001
ADAM_B1 = 0.9
ADAM_B2 = 0.999
ADAM_EPS = 1e-08
ADAM_WD = 0.01
ADAM_STEP = 10

NT_DIMS = (((1,), (1,)), ((), ()))
TN_DIMS = (((0,), (0,)), ((), ()))
NN_DIMS = (((1,), (0,)), ((), ()))

ROW_MIXG = 0
ROW_MLPG = 2
ROW_CONVB = 4
ROW_SSMG = 6
ROW_MISC = 8
ROW_RELB = 10
ROW_CONVW = 18
ROW_LOSS = 26
SMALL_ROWS = 32
LANE_QG, LANE_KG, LANE_SINK, LANE_DTB, LANE_ALOG, LANE_DSKIP = 0, 64, 128, 256, 384, 512


def _dot(a, b, dims):
    return lax.dot_general(a, b, dims, preferred_element_type=f32)


def _cparams(n_axes):
    return pltpu.CompilerParams(dimension_semantics=("arbitrary",) * n_axes, vmem_limit_bytes=VMEM_LIMIT)


def _sum11(v):
    return jnp.sum(jnp.sum(v, axis=1, keepdims=True), axis=0, keepdims=True)


def _sigmoid(v):
    return 1.0 / (1.0 + jnp.exp(-v))


ANY_SPEC = pl.BlockSpec(memory_space=pl.ANY)


def _in_hbm(args):
    return [pltpu.with_memory_space_constraint(a, pltpu.HBM) if a.size >= 65536 else a for a in args]


def _out_hbm(out_shape):
    one = lambda s: pltpu.HBM(s.shape, s.dtype) if math.prod(s.shape) >= 65536 else s
    return [one(s) for s in out_shape] if isinstance(out_shape, (list, tuple)) else one(out_shape)


def _matmul(name, mode, a, b, *, layer=0, tm, tn, tk, out_shape, out_specs, epilogue, extras=(), extra_specs=(), deps=(),
            prologue=None):
    extras = tuple(extras) + tuple(deps)
    extra_specs = tuple(extra_specs) + (ANY_SPEC,) * len(deps)
    if mode == "tn":
        t_dim, m_dim = a.shape
        n_dim = b.shape[1]
        grid = (m_dim // tm, n_dim // tn, t_dim // tk)
        a_spec = pl.BlockSpec((tk, tm), lambda i, j, k: (k, i))
        b_spec = pl.BlockSpec((tk, tn), lambda i, j, k: (k, j))
        dims = TN_DIMS
    elif mode == "nn":
        m_dim, k_dim = a.shape
        n_dim = b.shape[-1]
        grid = (m_dim // tm, n_dim // tn, k_dim // tk)
        a_spec = pl.BlockSpec((tm, tk), lambda i, j, k: (i, k))
        b_spec = pl.BlockSpec((None, tk, tn), lambda i, j, k: (layer, k, j))
        dims = NN_DIMS
    else:
        m_dim, k_dim = a.shape
        n_dim = b.shape[-2]
        grid = (m_dim // tm, n_dim // tn, k_dim // tk)
        a_spec = pl.BlockSpec((tm, tk), lambda i, j, k: (i, k))
        b_spec = pl.BlockSpec((None, tn, tk), lambda i, j, k: (layer, j, k))
        dims = NT_DIMS
    nk = grid[2]
    n_ex = len(extras)

    def body(a_ref, b_ref, *rest):
        ex = rest[:n_ex - len(deps)]
        outs = rest[n_ex:-1]
        acc = rest[-1]
        i = pl.program_id(0)
        j = pl.program_id(1)
        k = pl.program_id(2)
        lhs = a_ref[...].astype(bf16) if prologue is None else prologue(a_ref, ex, outs)
        part = _dot(lhs, b_ref[...].astype(bf16), dims)
        if nk == 1:
            epilogue(part, i, j, ex, outs)
        else:
            @pl.when(k == 0)
            def _():
                acc[...] = part

            @pl.when(k > 0)
            def _():
                acc[...] += part

            @pl.when(k == nk - 1)
            def _():
                epilogue(acc[...], i, j, ex, outs)

    return pl.pallas_call(
        body, grid=grid, in_specs=[a_spec, b_spec, *extra_specs], out_specs=out_specs, out_shape=_out_hbm(out_shape),
        scratch_shapes=[pltpu.VMEM((tm, tn) if nk > 1 else (8, 128), f32)], name=name, compiler_params=_cparams(3),
    )(*_in_hbm([a, b, *extras]))


def _rms_bwd_epilogue(layer):
    def epi(acc, i, j, ex, outs):
        x_ref, g_ref, dres_ref = ex
        dx_ref, dg_ref = outs
        xv = x_ref[...]
        r = lax.rsqrt(jnp.mean(xv * xv, axis=-1, keepdims=True) + EPS)
        xhat = xv * r
        w = acc * g_ref[layer:layer + 1, :]
        dx_ref[...] = dres_ref[...] + r * (w - xhat * jnp.mean(xhat * w, axis=-1, keepdims=True))
        dg = jnp.sum(acc * xhat, axis=0, keepdims=True)

        @pl.when(i == 0)
        def _():
            dg_ref[...] = dg

        @pl.when(i > 0)
        def _():
            dg_ref[...] += dg
    return epi


def _own_slab_spec(kind, tr, cols, nblk):
    if kind == "stack":
        return pl.BlockSpec((None, tr, cols), lambda i, idx: (idx[0], i, 0))
    if kind == "cols512":
        return pl.BlockSpec((tr, cols), lambda i, idx: (i, idx[0]))
    return pl.BlockSpec((tr, cols), lambda i, idx: (idx[0] * nblk + i, 0))


def _cast_to_full(name, w, kind, full_shape, my_idx, dtype):
    n_layers, rows, cols = w.shape
    tr = min(rows, 256)
    nblk = rows // tr

    def body(idx_ref, w_ref, *o_refs):
        for l in range(n_layers):
            o_refs[l][...] = w_ref[l].astype(dtype)

    grid_spec = pltpu.PrefetchScalarGridSpec(
        num_scalar_prefetch=1, grid=(nblk,), in_specs=[pl.BlockSpec((n_layers, tr, cols), lambda i, idx: (0, i, 0))],
        out_specs=[_own_slab_spec(kind, tr, cols, nblk)] * n_layers)
    return pl.pallas_call(body, grid_spec=grid_spec, out_shape=_out_hbm([SDS(full_shape, dtype)] * n_layers), name=name,
                          compiler_params=_cparams(1))(*_in_hbm([my_idx, w]))


def _adamw_math(w, m, v, g):
    m_new = ADAM_B1 * m + (1.0 - ADAM_B1) * g
    v_new = ADAM_B2 * v + (1.0 - ADAM_B2) * (g * g)
    m_hat = m_new / (1.0 - ADAM_B1 ** ADAM_STEP)
    v_hat = v_new / (1.0 - ADAM_B2 ** ADAM_STEP)
    delta = -ADAM_LR * (m_hat / (jnp.sqrt(v_hat) + ADAM_EPS) + ADAM_WD * w)
    return delta, m_new, v_new


def _adamw_layer(name, kind, layer, w, m, v, land, g_full, my_idx, prev, tr):
    rows2, cols = w.shape
    rows = rows2 // DEPTH
    nblk = rows // tr
    own_spec = _own_slab_spec(kind, tr, cols, nblk)
    n_prev = 0 if prev is None else 4

    def body(idx_ref, w_ref, m_ref, v_ref, land_ref, own_ref, *rest):
        g_ref, d_ref, mo_ref, vo_ref = rest[n_prev:]
        me = idx_ref[0]
        g = None
        for p in range(N_DEV):
            part = jnp.where(me == p, own_ref[...], land_ref[p]).astype(f32)
            g = part if g is None else g + part
        delta, m_new, v_new = _adamw_math(w_ref[...], m_ref[...], v_ref[...], g)
        g_ref[...] = g
        d_ref[...] = delta
        mo_ref[...] = m_new
        vo_ref[...] = v_new

    blk = pl.BlockSpec((tr, cols), lambda i, idx: (layer * nblk + i, 0))
    grid_spec = pltpu.PrefetchScalarGridSpec(
        num_scalar_prefetch=1, grid=(nblk,),
        in_specs=[blk, blk, blk, pl.BlockSpec((N_DEV, tr, cols), lambda i, idx: (0, i, 0)), own_spec] + [ANY_SPEC] * n_prev,
        out_specs=[blk, blk, blk, blk])
    aliases = {} if prev is None else {6 + k: k for k in range(4)}
    return pl.pallas_call(
        body, grid_spec=grid_spec, out_shape=_out_hbm([SDS((rows2, cols), f32)] * 4), name=name, input_output_aliases=aliases,
        compiler_params=_cparams(1),
    )(*_in_hbm([my_idx, w, m, v, land, g_full, *([] if prev is None else prev)]))


def _bucket_table():
    qi = np.arange(BLK)[:, None]
    kj = np.arange(2 * BLK)[None, :]
    dist = qi + BLK - kj
    dcl = np.clip(dist, 0, None)
    max_exact = N_BUCKETS // 2
    d_f = np.maximum(dcl, 1).astype(np.float32)
    large = max_exact + (np.log(d_f / np.float32(max_exact)) / np.float32(math.log(128 / max_exact))
                         * np.float32(N_BUCKETS - max_exact)).astype(np.int32)
    large = np.minimum(large, N_BUCKETS - 1)
    bucket = np.where(dcl < max_exact, dcl, large)
    in_window = (dist >= 0) & (dist < BLK)
    return bucket.astype(np.int32), in_window


def _onehot_buckets():
    bucket, _ = _bucket_table()
    oh = (bucket.reshape(-1)[None, :] == np.arange(N_BUCKETS)[:, None]).astype(np.float32)
    return oh


def _bias_build(rel_bias_t, onehot_t):
    def body(r_ref, o_ref, out_ref):
        out_ref[...] = jnp.dot(r_ref[...], o_ref[...], preferred_element_type=f32, precision=HIGHEST)

    tn = 4096
    return pl.pallas_call(
        body, grid=(BLK * 2 * BLK // tn,),
        in_specs=[pl.BlockSpec((NQ, N_BUCKETS), lambda i: (0, 0)), pl.BlockSpec((N_BUCKETS, tn), lambda i: (0, i))],
        out_specs=pl.BlockSpec((NQ, tn), lambda i: (0, i)), out_shape=SDS((NQ, BLK * 2 * BLK), f32), name="bias_build",
        compiler_params=_cparams(1),
    )(rel_bias_t, onehot_t)


def _bias_grad(dbias0, dbias1, onehot_t):
    tn = 4096
    nsteps = BLK * 2 * BLK // tn

    def body(a_ref, b_ref, o_ref, out_ref):
        part = lax.dot_general(a_ref[...] + b_ref[...], o_ref[...], NT_DIMS, preferred_element_type=f32, precision=HIGHEST)

        @pl.when(pl.program_id(0) == 0)
        def _():
            out_ref[...] = part

        @pl.when(pl.program_id(0) > 0)
        def _():
            out_ref[...] += part

    return pl.pallas_call(
        body, grid=(nsteps,),
        in_specs=[pl.BlockSpec((NQ, tn), lambda i: (0, i)), pl.BlockSpec((NQ, tn), lambda i: (0, i)),
                  pl.BlockSpec((N_BUCKETS, tn), lambda i: (0, i))],
        out_specs=pl.BlockSpec((NQ, N_BUCKETS), lambda i: (0, 0)), out_shape=SDS((NQ, N_BUCKETS), f32), name="bias_grad",
        compiler_params=_cparams(1),
    )(dbias0, dbias1, onehot_t)


def _attn_mask(n):
    qi = lax.broadcasted_iota(jnp.int32, (BLK, 2 * BLK), 0)
    kj = lax.broadcasted_iota(jnp.int32, (BLK, 2 * BLK), 1)
    dist = qi + BLK - kj
    first_key = jnp.where(n > 0, 0, BLK)
    return (dist >= 0) & (dist < BLK) & (kj >= first_key)


def _row_mean(a):
    return jnp.mean(a, axis=-1, keepdims=True)


def _head_norm(t, gain):
    r = lax.rsqrt(_row_mean(t * t) + EPS)
    that = t * r
    return that, r, that * gain


def _softmax_with_sink(s, sink):
    m = jnp.maximum(jnp.max(s, axis=-1, keepdims=True), sink)
    p = jnp.exp(s - m)
    psink = jnp.exp(sink - m)
    inv = 1.0 / (jnp.sum(p, axis=-1, keepdims=True) + psink)
    return p * inv, psink * inv


GQ = NQ // NKV


def _attn_fwd(qkv, q_gain, k_gain, sinks, bias, layer):
    def body(q_ref, kc_ref, kp_ref, vc_ref, vp_ref, qg_ref, kg_ref, sk_ref, bias_ref, o_ref):
        m = pl.program_id(0)
        qg = qg_ref[layer:layer + 1, :]
        kg = kg_ref[layer:layer + 1, :]
        grp = range(NKV)
        chains = [(b, j) for b in range(2) for j in grp]
        masks = [jnp.tile(_attn_mask(2 * m + b), (GQ, 1)) for b in range(2)]
        kblk = [[kp_ref[:, pl.ds(HD * j, HD)], kc_ref[0:BLK, pl.ds(HD * j, HD)], kc_ref[BLK:, pl.ds(HD * j, HD)]] for j in grp]
        vblk = [[vp_ref[:, pl.ds(HD * j, HD)].astype(bf16), vc_ref[0:BLK, pl.ds(HD * j, HD)].astype(bf16),
                 vc_ref[BLK:, pl.ds(HD * j, HD)].astype(bf16)] for j in grp]
        knb = [[_head_norm(kblk[j][t], kg)[2].astype(bf16) for t in range(3)] for j in grp]
        kn_b = {(b, j): jnp.concatenate([knb[j][b], knb[j][b + 1]], axis=0) for b, j in chains}
        vbs = {(b, j): jnp.concatenate([vblk[j][b], vblk[j][b + 1]], axis=0) for b, j in chains}
        rows = {}
        for b, j in chains:
            heads = [GQ * j + g for g in range(GQ)]
            rows[b, j] = (jnp.concatenate([q_ref[pl.ds(BLK * b, BLK), pl.ds(HD * h, HD)] for h in heads], axis=0),
                          jnp.concatenate([jnp.broadcast_to(sk_ref[layer:layer + 1, h:h + 1], (BLK, 1)) for h in heads], axis=0))
        qn_b = {c: _head_norm(rows[c][0], qg)[2].astype(bf16) for c in chains}
        ss = {(b, j): _dot(qn_b[b, j], kn_b[b, j], NT_DIMS) * (HD ** -0.5) + bias_ref[GQ * j:GQ * (j + 1)].reshape(GQ * BLK, 2 * BLK)
              for b, j in chains}
        ps = {(b, j): _softmax_with_sink(jnp.where(masks[b], ss[b, j], -jnp.inf), rows[b, j][1])[0] for b, j in chains}
        outs = {c: _dot(ps[c].astype(bf16), vbs[c], NN_DIMS).astype(bf16) for c in chains}
        for b, j in chains:
            for g in range(GQ):
                o_ref[pl.ds(BLK * b, BLK), pl.ds(HD * (GQ * j + g), HD)] = outs[b, j][BLK * g:BLK * (g + 1), :]

    prev = lambda m: jnp.maximum(2 * m - 1, 0)
    small = lambda shape: pl.BlockSpec(shape, lambda m: (0,) * len(shape))
    return pl.pallas_call(
        body, grid=(NBLK // 2,),
        in_specs=[pl.BlockSpec((2 * BLK, D_ATTN), lambda m: (m, 0)),
                  pl.BlockSpec((2 * BLK, 128), lambda m: (m, 4)), pl.BlockSpec((BLK, 128), lambda m: (prev(m), 4)),
                  pl.BlockSpec((2 * BLK, 128), lambda m: (m, 5)), pl.BlockSpec((BLK, 128), lambda m: (prev(m), 5)),
                  small((DEPTH, HD)), small((DEPTH, HD)), small((DEPTH, NQ)), small((NQ, BLK, 2 * BLK))],
        out_specs=pl.BlockSpec((2 * BLK, D_ATTN), lambda m: (m, 0)), out_shape=_out_hbm(SDS((S, D_ATTN), bf16)),
        name="attn_fwd", compiler_params=_cparams(1),
    )(*_in_hbm([qkv, qkv, qkv, qkv, qkv, q_gain, k_gain, sinks, bias]))


def _attn_bwd(qkv, dmix, q_gain, k_gain, sinks, bias, layer):
    def body(q_ref, kc_ref, kp_ref, vc_ref, vp_ref, do_ref, qg_ref, kg_ref, sk_ref, bias_ref,
             dqkv_ref, dbias_ref, dsm_ref, carry):
        i = pl.program_id(0)
        m = NBLK // 2 - 1 - i
        qg = qg_ref[layer:layer + 1, :]
        kg = kg_ref[layer:layer + 1, :]
        lane = lax.broadcasted_iota(jnp.int32, (1, 128), 1)

        @pl.when(i == 0)
        def _():
            carry[...] = jnp.zeros_like(carry)
            dbias_ref[...] = jnp.zeros_like(dbias_ref)
            dsm_ref[...] = jnp.zeros_like(dsm_ref)

        grp = range(NKV)
        chains = [(b, j) for b in range(2) for j in grp]
        masks = [jnp.tile(_attn_mask(2 * m + b), (GQ, 1)) for b in range(2)]
        kblk = [[kp_ref[:, pl.ds(HD * j, HD)], kc_ref[0:BLK, pl.ds(HD * j, HD)], kc_ref[BLK:, pl.ds(HD * j, HD)]] for j in grp]
        vblk = [[vp_ref[:, pl.ds(HD * j, HD)].astype(bf16), vc_ref[0:BLK, pl.ds(HD * j, HD)].astype(bf16),
                 vc_ref[BLK:, pl.ds(HD * j, HD)].astype(bf16)] for j in grp]
        knorm = [[_head_norm(kblk[j][t], kg) for t in range(3)] for j in grp]
        kn_b = {(b, j): jnp.concatenate([knorm[j][b][2].astype(bf16), knorm[j][b + 1][2].astype(bf16)], axis=0) for b, j in chains}
        vbs = {(b, j): jnp.concatenate([vblk[j][b], vblk[j][b + 1]], axis=0) for b, j in chains}
        rows, do_b = {}, {}
        for b, j in chains:
            heads = [GQ * j + g for g in range(GQ)]
            qrows = pl.ds(BLK * b, BLK)
            rows[b, j] = (jnp.concatenate([q_ref[qrows, pl.ds(HD * h, HD)] for h in heads], axis=0),
                          jnp.concatenate([jnp.broadcast_to(sk_ref[layer:layer + 1, h:h + 1], (BLK, 1)) for h in heads], axis=0))
            do_b[b, j] = jnp.concatenate([do_ref[qrows, pl.ds(HD * h, HD)] for h in heads], axis=0).astype(bf16)
        qnorm = {c: _head_norm(rows[c][0], qg) for c in chains}
        qn_b = {c: qnorm[c][2].astype(bf16) for c in chains}
        ss = {(b, j): _dot(qn_b[b, j], kn_b[b, j], NT_DIMS) * (HD ** -0.5) + bias_ref[GQ * j:GQ * (j + 1)].reshape(GQ * BLK, 2 * BLK)
              for b, j in chains}
        sm = {(b, j): _softmax_with_sink(jnp.where(masks[b], ss[b, j], -jnp.inf), rows[b, j][1]) for b, j in chains}
        dps = {c: _dot(do_b[c], vbs[c], NT_DIMS) for c in chains}
        deltas = {c: jnp.sum(sm[c][0] * dps[c], axis=-1, keepdims=True) for c in chains}
        dss = {c: sm[c][0] * (dps[c] - deltas[c]) for c in chains}
        ds_b = {c: (dss[c] * (HD ** -0.5)).astype(bf16) for c in chains}
        dqn = {c: _dot(ds_b[c], kn_b[c], NN_DIMS) for c in chains}
        dkn = {c: _dot(ds_b[c], qn_b[c], TN_DIMS) for c in chains}
        dvs = {c: _dot(sm[c][0].astype(bf16), do_b[c], TN_DIMS) for c in chains}
        dqg = jnp.zeros((1, HD), f32)
        dkg = jnp.zeros((1, HD), f32)
        dsink = jnp.zeros((1, 128), f32)
        for b, j in chains:
            dbias_ref[GQ * j:GQ * (j + 1)] += dss[b, j].reshape(GQ, BLK, 2 * BLK)
            dsk = sm[b, j][1] * deltas[b, j]
            for g in range(GQ):
                dsink = dsink + jnp.where(lane == GQ * j + g, -_sum11(dsk[BLK * g:BLK * (g + 1), :]), 0.0)
            qhat, rq, _ = qnorm[b, j]
            w = dqn[b, j] * qg
            dq = rq * (w - qhat * _row_mean(qhat * w))
            for g in range(GQ):
                dqkv_ref[pl.ds(BLK * b, BLK), pl.ds(HD * (GQ * j + g), HD)] = dq[BLK * g:BLK * (g + 1), :].astype(bf16)
            dqg = dqg + jnp.sum(dqn[b, j] * qhat, axis=0, keepdims=True)
        for j in grp:
            dkn_t = [dkn[0, j][:BLK, :], dkn[0, j][BLK:, :] + dkn[1, j][:BLK, :], dkn[1, j][BLK:, :]]
            dv_t = [dvs[0, j][:BLK, :], dvs[0, j][BLK:, :] + dvs[1, j][:BLK, :], dvs[1, j][BLK:, :]]
            dk_t = []
            for t in range(3):
                khat, rk, _ = knorm[j][t]
                w = dkn_t[t] * kg
                dk_t.append(rk * (w - khat * _row_mean(khat * w)))
                dkg = dkg + jnp.sum(dkn_t[t] * khat, axis=0, keepdims=True)
            kcols, vcols = pl.ds(D_ATTN + HD * j, HD), pl.ds(D_ATTN + 128 + HD * j, HD)
            dqkv_ref[BLK:, kcols] = (dk_t[2] + carry[:, pl.ds(HD * j, HD)]).astype(bf16)
            dqkv_ref[BLK:, vcols] = (dv_t[2] + carry[:, pl.ds(128 + HD * j, HD)]).astype(bf16)
            dqkv_ref[0:BLK, kcols] = dk_t[1].astype(bf16)
            dqkv_ref[0:BLK, vcols] = dv_t[1].astype(bf16)
            carry[:, pl.ds(HD * j, HD)] = dk_t[0]
            carry[:, pl.ds(128 + HD * j, HD)] = dv_t[0]
        dsm_ref[0:1, 0:HD] += dqg
        dsm_ref[1:2, 0:HD] += dkg
        dsm_ref[2:3, :] += dsink

    rev = lambda i: NBLK // 2 - 1 - i
    prev = lambda i: jnp.maximum(NBLK - 3 - 2 * i, 0)
    small = lambda shape: pl.BlockSpec(shape, lambda i: (0,) * len(shape))
    return pl.pallas_call(
        body, grid=(NBLK // 2,),
        in_specs=[pl.BlockSpec((2 * BLK, D_ATTN), lambda i: (rev(i), 0)),
                  pl.BlockSpec((2 * BLK, 128), lambda i: (rev(i), 4)), pl.BlockSpec((BLK, 128), lambda i: (prev(i), 4)),
                  pl.BlockSpec((2 * BLK, 128), lambda i: (rev(i), 5)), pl.BlockSpec((BLK, 128), lambda i: (prev(i), 5)),
                  pl.BlockSpec((2 * BLK, D_ATTN), lambda i: (rev(i), 0)),
                  small((DEPTH, HD)), small((DEPTH, HD)), small((DEPTH, NQ)), small((NQ, BLK, 2 * BLK))],
        out_specs=[pl.BlockSpec((2 * BLK, 768), lambda i: (rev(i), COL_QKV // 768)), small((NQ, BLK, 2 * BLK)), small((8, 128))],
        out_shape=_out_hbm([SDS((S, D_IN_PAD), bf16), SDS((NQ, BLK, 2 * BLK), f32), SDS((8, 128), f32)]),
        scratch_shapes=[pltpu.VMEM((BLK, 256), f32)], name="attn_bwd", compiler_params=_cparams(1),
    )(*_in_hbm([qkv, qkv, qkv, qkv, qkv, dmix, q_gain, k_gain, sinks, bias]))


CONV_TC = 128


def _shift_down(u, s):
    if s == 0:
        return u
    rows = lax.broadcasted_iota(jnp.int32, u.shape, 0)
    return jnp.where(rows >= s, pltpu.roll(u, s, 0), 0.0)


def _shift_up(u, s):
    if s == 0:
        return u
    rows = lax.broadcasted_iota(jnp.int32, u.shape, 0)
    return jnp.where(rows < u.shape[0] - s, pltpu.roll(u, u.shape[0] - s, 0), 0.0)


def _conv_specs():
    return [pl.BlockSpec((S, CONV_TC), lambda c: (0, c)),
            pl.BlockSpec((None, 4, CONV_TC), lambda c: (0, 0, c)),
            pl.BlockSpec((DEPTH, CONV_TC), lambda c: (0, c))]


def _conv_pre(u, w_ref, b_ref, layer):
    pre = b_ref[layer:layer + 1, :] + w_ref[3:4, :] * u
    for k in range(3):
        pre = pre + w_ref[k:k + 1, :] * _shift_down(u, 3 - k)
    return pre


def _conv_fwd(xbc, conv_w, conv_b, layer):
    def body(u_ref, w_ref, b_ref, o_ref):
        pre = _conv_pre(u_ref[...], w_ref, b_ref, layer)
        o_ref[...] = pre * _sigmoid(pre)

    specs = _conv_specs()
    specs[1] = pl.BlockSpec((None, 4, CONV_TC), lambda c: (layer, 0, c))
    return pl.pallas_call(
        body, grid=(D_CONV // CONV_TC,), in_specs=specs, out_specs=pl.BlockSpec((S, CONV_TC), lambda c: (0, c)),
        out_shape=_out_hbm(SDS((S, D_CONV), f32)), name="conv_fwd", compiler_params=_cparams(1),
    )(*_in_hbm([xbc, conv_w, conv_b]))


def _conv_bwd(xbc, dact, conv_w, conv_b, dproj, layer):
    def body(u_ref, w_ref, b_ref, da_ref, dproj_in, du_ref, dw_ref, db_ref):
        u = u_ref[...]
        pre = _conv_pre(u, w_ref, b_ref, layer)
        sg = _sigmoid(pre)
        dpre = da_ref[...] * (sg * (1.0 + pre * (1.0 - sg)))
        du = w_ref[3:4, :] * dpre
        for k in range(3):
            du = du + w_ref[k:k + 1, :] * _shift_up(dpre, 3 - k)
        du_ref[...] = du.astype(bf16)
        db_ref[...] = jnp.broadcast_to(jnp.sum(dpre, axis=0, keepdims=True), db_ref.shape)
        dw_ref[...] = jnp.zeros_like(dw_ref)
        for k in range(4):
            dw_ref[k:k + 1, :] = jnp.sum(dpre * _shift_down(u, 3 - k), axis=0, keepdims=True)

    specs = _conv_specs()
    specs[1] = pl.BlockSpec((None, 4, CONV_TC), lambda c: (layer, 0, c))
    col = pl.BlockSpec((S, CONV_TC), lambda c: (0, c))
    row8 = pl.BlockSpec((8, CONV_TC), lambda c: (0, c))
    return pl.pallas_call(
        body, grid=(D_CONV // CONV_TC,), in_specs=[*specs, col, ANY_SPEC],
        out_specs=[pl.BlockSpec((S, CONV_TC), lambda c: (0, COL_XBC // CONV_TC + c)), row8, row8],
        out_shape=_out_hbm([SDS((S, D_IN_PAD), bf16), SDS((8, D_CONV), f32), SDS((8, D_CONV), f32)]), name="conv_bwd",
        input_output_aliases={4: 0}, compiler_params=_cparams(1),
    )(*_in_hbm([xbc, conv_w, conv_b, dact, dproj]))


def _tri():
    return (lax.broadcasted_iota(jnp.int32, (BLK, BLK), 0) >= lax.broadcasted_iota(jnp.int32, (BLK, BLK), 1))


def _ssd_scalars(dt_ref, dtb_ref, alog_ref, layer):
    raw = dt_ref[:, 0:NSSM] + dtb_ref[layer:layer + 1, :]
    dtv = jnp.maximum(raw, 0.0) + jnp.log(1.0 + jnp.exp(-jnp.abs(raw)))
    a = -jnp.exp(alog_ref[layer:layer + 1, :])
    acs = jnp.dot(_tri().astype(f32), dtv * a, preferred_element_type=f32, precision=HIGHEST)
    return raw, dtv, a, acs


HG = NSSM // NGRP
GW = HG * HD


def _lane_expand(cols, g):
    lane_head = lax.broadcasted_iota(jnp.int32, (1, GW), 1) // HD
    out = cols[:, HG * g + HG - 1:HG * g + HG]
    for r in range(HG - 2, -1, -1):
        out = jnp.where(lane_head == r, cols[:, HG * g + r:HG * g + r + 1], out)
    return out


def _row_expand(vals, g):
    row_head = lax.broadcasted_iota(jnp.int32, (GW, 1), 0) // HD
    out = vals[:, HG * g + HG - 1:HG * g + HG]
    for r in range(HG - 2, -1, -1):
        out = jnp.where(row_head == r, vals[:, HG * g + r:HG * g + r + 1], out)
    return out


def _head_rowsums(a):
    sel = (lax.broadcasted_iota(jnp.int32, (GW, HG), 0) // HD == lax.broadcasted_iota(jnp.int32, (GW, HG), 1)).astype(bf16)
    hi = a.astype(bf16)
    lo = (a - hi.astype(f32)).astype(bf16)
    sums = _dot(hi, sel, NN_DIMS) + _dot(lo, sel, NN_DIMS)
    return [sums[:, r:r + 1] for r in range(HG)]


def _ssd_chunk_common(xc_ref, dt_ref, dtb_ref, alog_ref, h_rows, layer):
    raw, dtv, a, acs = _ssd_scalars(dt_ref, dtb_ref, alog_ref, layer)
    acs_t = acs.T
    last = acs[BLK - 1:BLK, :]
    c = dict(raw=raw, dtv=dtv, a=a, acs=acs, last=last, dte=jnp.exp(last - acs), e_all=jnp.exp(acs), cd=jnp.exp(last))
    grp, heads, tri = range(NGRP), range(NSSM), _tri()
    c["bm"] = [xc_ref[:, pl.ds(D_SSM + NSTATE * g, NSTATE)] for g in grp]
    c["bm_b"] = [c["bm"][g].astype(bf16) for g in grp]
    c["cm_b"] = [xc_ref[:, pl.ds(D_SSM + NGRP * NSTATE + NSTATE * g, NSTATE)].astype(bf16) for g in grp]
    c["cb"] = [_dot(c["cm_b"][g], c["bm_b"][g], NT_DIMS) for g in grp]
    c["x"] = [xc_ref[:, pl.ds(GW * g, GW)] for g in grp]
    c["dt"] = [_lane_expand(dtv, g) for g in grp]
    c["xdt"] = [c["x"][g] * c["dt"][g] for g in grp]
    c["xdt_b"] = [c["xdt"][g].astype(bf16) for g in grp]
    c["prev"] = [h_rows(g) for g in grp]
    c["prev_b"] = [c["prev"][g].astype(bf16) for g in grp]
    c["e"] = [_lane_expand(c["e_all"], g) for g in grp]
    c["y_off"] = [_dot(c["cm_b"][g], c["prev_b"][g], NT_DIMS) * c["e"][g] for g in grp]
    c["decay"] = [jnp.exp(jnp.where(tri, acs[:, h:h + 1] - acs_t[h:h + 1, :], -jnp.inf)) for h in heads]
    c["m"] = [c["cb"][h // HG] * c["decay"][h] for h in heads]
    c["m_b"] = [c["m"][h].astype(bf16) for h in heads]
    c["dte_x"] = [_lane_expand(c["dte"], g) for g in grp]
    c["xdte_b"] = [(c["xdt"][g] * c["dte_x"][g]).astype(bf16) for g in grp]
    return c


def _ssd_fwd(xact, z, dt, attn, dt_bias, a_log, d_skip, norm_g, layer):
    def body(xc_ref, z_ref, dt_ref, at_ref, dtb_ref, alog_ref, dsk_ref, ng_ref, mix_ref, hs_ref, y_ref, h_ref):
        n = pl.program_id(0)

        @pl.when(n == 0)
        def _():
            h_ref[...] = jnp.zeros_like(h_ref)

        hs_ref[...] = h_ref[...]
        c = _ssd_chunk_common(xc_ref, dt_ref, dtb_ref, alog_ref, lambda g: h_ref[pl.ds(GW * g, GW), :], layer)
        grp, heads = range(NGRP), range(NSSM)
        y_diag = [_dot(c["m_b"][h], c["xdt_b"][h // HG][:, HD * (h % HG):HD * (h % HG + 1)], NN_DIMS) for h in heads]
        new_st = [_dot(c["xdte_b"][g], c["bm_b"][g], TN_DIMS) for g in grp]
        for h in heads:
            y_ref[:, pl.ds(HD * h, HD)] = y_diag[h]
        dskip = dsk_ref[layer:layer + 1, :]
        for g in grp:
            cols = pl.ds(GW * g, GW)
            y_ref[:, cols] = y_ref[:, cols] + c["y_off"][g] + c["x"][g] * _lane_expand(dskip, g)
            h_ref[cols, :] = c["prev"][g] * _row_expand(c["cd"], g) + new_st[g]
        zv = z_ref[...]
        yz = y_ref[...] * (zv * _sigmoid(zv))
        mix_ref[:, 0:D_ATTN] = at_ref[...]
        for g in grp:
            yg = yz[:, GW * g:GW * (g + 1)]
            rs = lax.rsqrt(jnp.mean(yg * yg, axis=-1, keepdims=True) + EPS)
            mix_ref[:, D_ATTN + GW * g:D_ATTN + GW * (g + 1)] = (yg * rs * ng_ref[layer:layer + 1, GW * g:GW * (g + 1)]).astype(bf16)

    small = lambda shape: pl.BlockSpec(shape, lambda n: (0,) * len(shape))
    return pl.pallas_call(
        body, grid=(NBLK,),
        in_specs=[pl.BlockSpec((BLK, D_CONV), lambda n: (n, 0)), pl.BlockSpec((BLK, D_SSM), lambda n: (n, 0)),
                  pl.BlockSpec((BLK, 128), lambda n: (n, 0)), pl.BlockSpec((BLK, D_ATTN), lambda n: (n, 0)),
                  small((DEPTH, NSSM)), small((DEPTH, NSSM)), small((DEPTH, NSSM)), small((DEPTH, D_SSM))],
        out_specs=[pl.BlockSpec((BLK, D), lambda n: (n, 0)), pl.BlockSpec((None, NSSM * HD, NSTATE), lambda n: (n, 0, 0)),
                   pl.BlockSpec((BLK, D_SSM), lambda n: (n, 0))],
        out_shape=_out_hbm([SDS((S, D), bf16), SDS((NBLK, NSSM * HD, NSTATE), f32), SDS((S, D_SSM), f32)]),
        scratch_shapes=[pltpu.VMEM((NSSM * HD, NSTATE), f32)],
        name="ssd_fwd", compiler_params=_cparams(1),
    )(*_in_hbm([xact, z, dt, attn, dt_bias, a_log, d_skip, norm_g]))


def _ssd_bwd(xact, z, dt, dmix, hs, y, dt_bias, a_log, d_skip, norm_g, dproj, layer):
    def body(xc_ref, z_ref, dt_ref, do_ref, hs_ref, y_ref, dtb_ref, alog_ref, dsk_ref, ng_ref, dproj_in,
             dzdt_ref, dx_ref, dsm_ref, dh_ref, dy_ref):
        i = pl.program_id(0)

        @pl.when(i == 0)
        def _():
            dh_ref[...] = jnp.zeros_like(dh_ref)
            dsm_ref[...] = jnp.zeros_like(dsm_ref)

        c = _ssd_chunk_common(xc_ref, dt_ref, dtb_ref, alog_ref, lambda g: hs_ref[pl.ds(GW * g, GW), :], layer)
        raw, dtv, a = c["raw"], c["dtv"], c["a"]
        grp, heads = range(NGRP), range(NSSM)
        dskip = dsk_ref[layer:layer + 1, :]
        lane8 = lax.broadcasted_iota(jnp.int32, (1, NSSM), 1)
        sub8 = lax.broadcasted_iota(jnp.int32, (NSSM, 1), 0)

        zv = z_ref[...]
        sz = _sigmoid(zv)
        gz = zv * sz
        yv = y_ref[...]
        yz = yv * gz
        for g in grp:
            sl = slice(GW * g, GW * (g + 1))
            yg = yz[:, sl]
            rs = lax.rsqrt(jnp.mean(yg * yg, axis=-1, keepdims=True) + EPS)
            yhat = yg * rs
            dog = do_ref[:, sl]
            w = dog * ng_ref[layer:layer + 1, sl]
            dyz = rs * (w - yhat * jnp.mean(yhat * w, axis=-1, keepdims=True))
            dsm_ref[0:1, sl] += jnp.sum(dog * yhat, axis=0, keepdims=True)
            dy_ref[:, sl] = dyz * gz[:, sl]
            dzdt_ref[:, sl] = (dyz * yv[:, sl] * (sz[:, sl] * (1.0 + zv[:, sl] * (1.0 - sz[:, sl])))).astype(bf16)

        dy = [dy_ref[:, pl.ds(GW * g, GW)] for g in grp]
        dy_b = [dy[g].astype(bf16) for g in grp]
        hl = lambda h: slice(HD * (h % HG), HD * (h % HG + 1))
        dt_off_b = [(dy[g] * c["e"][g]).astype(bf16) for g in grp]
        dcm = [_dot(dt_off_b[g], c["prev_b"][g], NN_DIMS) for g in grp]
        dprev = [_dot(dt_off_b[g], c["cm_b"][g], TN_DIMS) for g in grp]
        yoff_rs = [_head_rowsums(dy[g] * c["y_off"][g]) for g in grp]
        dhn = [dh_ref[pl.ds(GW * g, GW), :] for g in grp]
        dhn_b = [dhn[g].astype(bf16) for g in grp]
        dprev = [dprev[g] + dhn[g] * _row_expand(c["cd"], g) for g in grp]
        dhn_prev = [dhn[g] * c["prev"][g] for g in grp]
        u = [_dot(c["bm_b"][g], dhn_b[g], NT_DIMS) for g in grp]
        dbm = [_dot(c["xdte_b"][g], dhn_b[g], NN_DIMS) for g in grp]
        ddte_rs = [_head_rowsums(c["xdt"][g] * u[g]) for g in grp]
        dm = [_dot(dy_b[h // HG][:, hl(h)], c["xdt_b"][h // HG][:, hl(h)], NT_DIMS) for h in heads]
        dxdt_in = [_dot(c["m_b"][h], dy_b[h // HG][:, hl(h)], TN_DIMS) for h in heads]
        dseg = [dm[h] * c["m"][h] for h in heads]
        dmd = [dm[h] * c["decay"][h] for h in heads]
        for h in heads:
            dx_ref[:, pl.ds(HD * h, HD)] = dxdt_in[h]

        dacs = jnp.zeros((BLK, NSSM), f32)
        dacs_cols = jnp.zeros((NSSM, BLK), f32)
        dlast = jnp.zeros((1, NSSM), f32)
        ddtv = jnp.zeros((BLK, NSSM), f32)
        ddsk = jnp.zeros((1, NSSM), f32)
        for g in grp:
            cols = pl.ds(GW * g, GW)
            dxdt = dx_ref[:, cols] + u[g] * c["dte_x"][g]
            dx_ref[:, cols] = dy[g] * _lane_expand(dskip, g) + dxdt * c["dt"][g]
            ddtv_rs = _head_rowsums(dxdt * c["x"][g])
            ddsk_rs = _head_rowsums(dy[g] * c["x"][g])
            dcb = dmd[HG * g]
            for r in range(1, HG):
                dcb = dcb + dmd[HG * g + r]
            dcb_b = dcb.astype(bf16)
            dx_ref[:, pl.ds(D_SSM + NSTATE * g, NSTATE)] = dbm[g] + _dot(dcb_b, c["cm_b"][g], TN_DIMS)
            dx_ref[:, pl.ds(D_SSM + NGRP * NSTATE + NSTATE * g, NSTATE)] = dcm[g] + _dot(dcb_b, c["bm_b"][g], NN_DIMS)
            dh_ref[cols, :] = dprev[g]
            for r in range(HG):
                h = HG * g + r
                oh = (lane8 == h).astype(f32)
                tmp = ddte_rs[g][r] * c["dte"][:, h:h + 1]
                dacs = dacs + oh * (jnp.sum(dseg[h], axis=1, keepdims=True) + yoff_rs[g][r] - tmp)
                dacs_cols = dacs_cols + (sub8 == h).astype(f32) * jnp.sum(dseg[h], axis=0, keepdims=True)
                dlast = dlast + oh * (_sum11(dhn_prev[g][HD * r:HD * (r + 1), :]) * c["cd"][:, h:h + 1] + _sum11(tmp))
                ddtv = ddtv + oh * ddtv_rs[r]
                ddsk = ddsk + oh * _sum11(ddsk_rs[r])

        row = lax.broadcasted_iota(jnp.int32, (BLK, 1), 0)
        dacs = dacs - dacs_cols.T + jnp.where(row == BLK - 1, dlast, 0.0)
        dda = lax.dot_general(_tri().astype(f32), dacs, TN_DIMS, preferred_element_type=f32, precision=HIGHEST)
        ddtv = ddtv + dda * a
        da = jnp.sum(dda * dtv, axis=0, keepdims=True)
        draw = ddtv * _sigmoid(raw)
        dzdt_ref[:, D_SSM:] = jnp.zeros((BLK, COL_XBC - COL_DT), bf16)
        dzdt_ref[:, D_SSM:D_SSM + NSSM] = draw.astype(bf16)
        dsm_ref[1:2, 0:NSSM] += jnp.sum(draw, axis=0, keepdims=True)
        dsm_ref[2:3, 0:NSSM] += da * a
        dsm_ref[3:4, 0:NSSM] += ddsk

    rev = lambda i: NBLK - 1 - i
    small = lambda shape: pl.BlockSpec(shape, lambda i: (0,) * len(shape))
    return pl.pallas_call(
        body, grid=(NBLK,),
        in_specs=[pl.BlockSpec((BLK, D_CONV), lambda i: (rev(i), 0)), pl.BlockSpec((BLK, D_SSM), lambda i: (rev(i), 0)),
                  pl.BlockSpec((BLK, 128), lambda i: (rev(i), 0)), pl.BlockSpec((BLK, D_SSM), lambda i: (rev(i), 1)),
                  pl.BlockSpec((None, NSSM * HD, NSTATE), lambda i: (rev(i), 0, 0)), pl.BlockSpec((BLK, D_SSM), lambda i: (rev(i), 0)),
                  small((DEPTH, NSSM)), small((DEPTH, NSSM)), small((DEPTH, NSSM)), small((DEPTH, D_SSM)), ANY_SPEC],
        out_specs=[pl.BlockSpec((BLK, COL_XBC - COL_Z), lambda i: (rev(i), COL_Z // (COL_XBC - COL_Z))),
                   pl.BlockSpec((BLK, D_CONV), lambda i: (rev(i), 0)), small((8, D_SSM))],
        out_shape=_out_hbm([SDS((S, D_IN_PAD), bf16), SDS((S, D_CONV), f32), SDS((8, D_SSM), f32)]),
        scratch_shapes=[pltpu.VMEM((NSSM * HD, NSTATE), f32), pltpu.VMEM((BLK, D_SSM), f32)],
        name="ssd_bwd", input_output_aliases={10: 0}, compiler_params=_cparams(1),
    )(*_in_hbm([xact, z, dt, dmix, hs, y, dt_bias, a_log, d_skip, norm_g, dproj]))


def _my_place():
    return lax.axis_index("x"), lax.axis_index("y"), lax.axis_index("c")


def _dev_index(px, py, pc):
    return 4 * px + 2 * py + pc


def _slab2(kind, ref, idx):
    if kind == "stack":
        return ref.at[idx]
    if kind == "rows128":
        return ref.at[pl.ds(pl.multiple_of(idx * 128, 128), 128), :]
    if kind == "rows512":
        return ref.at[pl.ds(pl.multiple_of(idx * 512, 512), 512), :]
    return ref.at[:, pl.ds(pl.multiple_of(idx * 512, 512), 512)]


def _slab_shape(kind, full_shape):
    if kind == "stack":
        return tuple(full_shape[1:])
    if kind == "rows128":
        return (128, full_shape[1])
    if kind == "rows512":
        return (512, full_shape[1])
    return (full_shape[0], 512)


KIND = dict(w_in="stack", w_out="rows128", w_up="cols512", w_down="rows512", conv_w="stack")
FULL_SHAPE = dict(w_in=(N_DEV, D, D_IN // N_DEV), w_out=(D, D), w_up=(D, D_FF), w_down=(D_FF, D))
HBM_SPEC = pl.BlockSpec(memory_space=pltpu.HBM)
SEM_SPEC = pl.BlockSpec(memory_space=pltpu.SEMAPHORE)
SIDE_EFFECT = pltpu.SideEffectType.DATAFLOW_SIDE_EFFECTING


def _peers_all():
    x, y, c = _my_place()
    return [(x ^ ((r >> 2) & 1), y ^ ((r >> 1) & 1), c ^ (r & 1)) for r in range(1, N_DEV)]


def _split_start(name, bufs, n_copies, plan, deps=()):
    nb = len(bufs)

    def body(*refs):
        ins = refs[:nb]
        send_sems, recv_sems = refs[nb + len(deps)], refs[nb + len(deps) + 1]
        token = refs[-1]
        for i, (src, dst, dev) in enumerate(plan(ins)):
            pltpu.make_async_remote_copy(src_ref=src, dst_ref=dst, send_sem=send_sems.at[i], recv_sem=recv_sems.at[i],
                                         device_id=dev, device_id_type=MESH).start()
        token[...] = jnp.zeros_like(token)

    outs = pl.pallas_call(
        body, name=name,
        out_shape=(pltpu.SemaphoreType.DMA((n_copies,)), pltpu.SemaphoreType.DMA((n_copies,)),
                   *[pltpu.HBM(b.shape, b.dtype) for b in bufs], SDS((8, 128), f32)),
        in_specs=[HBM_SPEC] * nb + [ANY_SPEC] * len(deps),
        out_specs=(SEM_SPEC, SEM_SPEC, *[HBM_SPEC] * nb, pl.BlockSpec(memory_space=pltpu.VMEM)),
        input_output_aliases={i: 2 + i for i in range(nb)},
        compiler_params=pltpu.CompilerParams(has_side_effects=SIDE_EFFECT),
    )(*[pltpu.with_memory_space_constraint(b, pltpu.HBM) for b in bufs], *deps)
    return dict(send=outs[0], recv=outs[1], bufs=list(outs[2:2 + nb]), token=outs[-1], plan=plan, n=n_copies)


def _split_wait(name, started, after):
    bufs = started["bufs"]
    nb = len(bufs)
    plan = started["plan"]

    def body(*refs):
        ins = refs[:nb]
        send_sems, recv_sems = refs[nb], refs[nb + 1]
        for i, (src, dst, dev) in enumerate(plan(ins)):
            cp = pltpu.make_async_remote_copy(src_ref=src, dst_ref=dst, send_sem=send_sems.at[i], recv_sem=recv_sems.at[i],
                                              device_id=dev, device_id_type=MESH)
            cp.wait_send()
            cp.wait_recv()

    outs = pl.pallas_call(
        body, name=name, out_shape=tuple(pltpu.HBM(b.shape, b.dtype) for b in bufs),
        in_specs=[HBM_SPEC] * nb + [SEM_SPEC, SEM_SPEC] + [ANY_SPEC] * len(after), out_specs=(HBM_SPEC,) * nb,
        input_output_aliases={i: i for i in range(nb)},
        compiler_params=pltpu.CompilerParams(has_side_effects=SIDE_EFFECT),
    )(*bufs, started["send"], started["recv"], *after)
    return list(outs)


def _gather_start(name, names, fulls, deps):
    n_t = len(names)

    def plan(refs):
        x, y, c = _my_place()
        my_idx = _dev_index(x, y, c)
        targets = [(x, y, 1 - c), (1 - x, y, c), (x, 1 - y, c), (1 - x, 1 - y, c)]
        slabs = [_slab2(KIND[names[t]], refs[t], my_idx) for t in range(n_t)]
        return [(slabs[t], slabs[t], dev) for t in range(n_t) for dev in targets]

    return _split_start(name, list(fulls), 4 * n_t, plan, deps)


def _gather_finish(name, names, started, after):
    n_t = len(names)
    fulls = _split_wait(name + "_wait", started, after)
    slab_shapes = [SDS(_slab_shape(KIND[n], f.shape), f.dtype) for n, f in zip(names, fulls)]

    def body(*refs):
        ins = refs[:n_t]
        outs = refs[n_t:2 * n_t]
        stage = refs[2 * n_t:3 * n_t]
        load_sems, send_sems, recv_sems = refs[3 * n_t:]
        x, y, c = _my_place()
        chips = [(1 - x, y), (x, 1 - y), (1 - x, 1 - y)]
        pairs = [(t, j) for t in range(n_t) for j in range(3)]
        loads = [pltpu.make_async_copy(_slab2(KIND[names[t]], ins[t], _dev_index(*chips[j], c)), stage[t].at[j], load_sems.at[t, j])
                 for t, j in pairs]
        for cp in loads:
            cp.start()

        def copy(t, j, core):
            return pltpu.make_async_remote_copy(
                src_ref=stage[t].at[j], dst_ref=_slab2(KIND[names[t]], outs[t], _dev_index(*chips[j], core)),
                send_sem=send_sems.at[t, j], recv_sem=recv_sems.at[t, j], device_id=(x, y, 1 - c), device_id_type=MESH)

        sends = [copy(t, j, c) for t, j in pairs]
        for ld, cp in zip(loads, sends):
            ld.wait()
            cp.start()
        for t, j in pairs:
            copy(t, j, 1 - c).wait_recv()
        for cp in sends:
            cp.wait_send()

    return pl.pallas_call(
        body, in_specs=[ANY_SPEC] * n_t, out_specs=[ANY_SPEC] * n_t, out_shape=[SDS(b.shape, b.dtype) for b in fulls],
        input_output_aliases={t: t for t in range(n_t)},
        scratch_shapes=[pltpu.VMEM((3,) + s.shape, s.dtype) for s in slab_shapes]
        + [pltpu.SemaphoreType.DMA((n_t, 3)), pltpu.SemaphoreType.DMA((n_t, 3)), pltpu.SemaphoreType.DMA((n_t, 3))],
        name=name + "_pass", compiler_params=pltpu.CompilerParams(vmem_limit_bytes=VMEM_LIMIT),
    )(*fulls)


def _exchange_start(name, names, grads, deps):
    n_t = len(names)
    lands = [lax.empty((N_DEV,) + _slab_shape(KIND[n], g.shape), g.dtype) for n, g in zip(names, grads)]

    def plan(refs):
        my_idx = _dev_index(*_my_place())
        return [(_slab2(KIND[names[t]], refs[t], _dev_index(*peer)), refs[n_t + t].at[my_idx], peer)
                for t in range(n_t) for peer in _peers_all()]

    return _split_start(name, list(grads) + lands, 7 * n_t, plan, deps)


def _small_exchange_start(part, deps):
    land = lax.empty((N_DEV,) + part.shape, part.dtype)

    def plan(refs):
        my_idx = _dev_index(*_my_place())
        return [(refs[0], refs[1].at[my_idx], peer) for peer in _peers_all()]

    return _split_start("small_exchange", [part, land], N_DEV - 1, plan, deps)


def _slab_pieces():
    sh = D_IN // N_DEV
    out = []
    for j in range(N_DEV):
        for first, end, dst in IN_SEGMENTS:
            lo, hi = max(first, sh * j), min(end, sh * (j + 1))
            if lo < hi:
                out.append((j, lo - sh * j, hi - sh * j, dst + lo - first))
    return out


def _w_in_assemble(stacked):
    tr = 256
    sh = D_IN // N_DEV

    def body(i_ref, o_ref):
        o_ref[:, COL_DT:COL_XBC] = jnp.zeros((tr, COL_XBC - COL_DT), bf16)
        for j, lo, hi, dst in _slab_pieces():
            o_ref[:, dst:dst + hi - lo] = i_ref[j, :, lo:hi]

    return pl.pallas_call(
        body, grid=(D // tr,), in_specs=[pl.BlockSpec((N_DEV, tr, sh), lambda i: (0, i, 0))],
        out_specs=pl.BlockSpec((None, tr, D_IN_PAD), lambda i: (0, i, 0)), out_shape=_out_hbm(SDS((1, D, D_IN_PAD), bf16)),
        name="w_in_assemble", compiler_params=_cparams(1),
    )(*_in_hbm([stacked]))


def _w_in_slabs(dw_in):
    tr = 256
    sh = D_IN // N_DEV

    def body(i_ref, o_ref):
        for j, lo, hi, src in _slab_pieces():
            o_ref[j, :, lo:hi] = i_ref[:, src:src + hi - lo]

    return pl.pallas_call(
        body, grid=(D // tr,), in_specs=[pl.BlockSpec((tr, D_IN_PAD), lambda i: (i, 0))],
        out_specs=pl.BlockSpec((N_DEV, tr, sh), lambda i: (0, i, 0)), out_shape=_out_hbm(SDS((N_DEV, D, sh), bf16)),
        name="w_in_slabs", compiler_params=_cparams(1),
    )(*_in_hbm([dw_in]))


SMALL_NAMES = ("mix_norm_g", "mlp_norm_g", "conv_b", "ssm_norm_g", "q_gain", "k_gain", "sinks", "dt_bias", "a_log", "d_skip",
               "rel_bias", "conv_w")
MISC_LANES = dict(q_gain=(LANE_QG, HD), k_gain=(LANE_KG, HD), sinks=(LANE_SINK, NQ), dt_bias=(LANE_DTB, NSSM),
                  a_log=(LANE_ALOG, NSSM), d_skip=(LANE_DSKIP, NSSM))


def _pack_small_grads(smalls, drel_t, loss):
    def body(*refs):
        o_ref = refs[-1]
        drel_ref, loss_ref = refs[-3], refs[-2]
        o_ref[...] = jnp.zeros_like(o_ref)
        for l in range(DEPTH):
            mixg, mlpg, convb, convw, ssd, attn = refs[6 * l:6 * l + 6]
            o_ref[ROW_MIXG + l:ROW_MIXG + l + 1, :] = mixg[...]
            o_ref[ROW_MLPG + l:ROW_MLPG + l + 1, :] = mlpg[...]
            o_ref[ROW_CONVB + l:ROW_CONVB + l + 1, :] = convb[0:1, :]
            o_ref[ROW_SSMG + l:ROW_SSMG + l + 1, 0:D_SSM] = ssd[0:1, :]
            o_ref[ROW_CONVW + 4 * l:ROW_CONVW + 4 * l + 4, :] = convw[0:4, :]
            row = slice(ROW_MISC + l, ROW_MISC + l + 1)
            o_ref[row, LANE_QG:LANE_QG + HD] = attn[0:1, 0:HD]
            o_ref[row, LANE_KG:LANE_KG + HD] = attn[1:2, 0:HD]
            o_ref[row, LANE_SINK:LANE_SINK + NQ] = attn[2:3, 0:NQ]
            o_ref[row, LANE_DTB:LANE_DTB + NSSM] = ssd[1:2, 0:NSSM]
            o_ref[row, LANE_ALOG:LANE_ALOG + NSSM] = ssd[2:3, 0:NSSM]
            o_ref[row, LANE_DSKIP:LANE_DSKIP + NSSM] = ssd[3:4, 0:NSSM]
        o_ref[ROW_RELB:ROW_RELB + NQ, 0:N_BUCKETS] = drel_ref[...]
        o_ref[ROW_LOSS:ROW_LOSS + 1, 0:1] = loss_ref[0:1, 0:1]

    args = []
    for sm in smalls:
        args += [sm["mix_norm_g"], sm["mlp_norm_g"], sm["conv_b"], sm["conv_w"], sm["ssd"], sm["attn"]]
    args += [drel_t, loss]
    return pl.pallas_call(body, out_shape=SDS((SMALL_ROWS, D), f32), name="pack_small_grads")(*args)


def _adamw_small(part, land, w, m, v):
    n = len(SMALL_NAMES)

    def grad_of(name, g_ref):
        if name == "mix_norm_g":
            return g_ref[ROW_MIXG:ROW_MIXG + DEPTH, :]
        if name == "mlp_norm_g":
            return g_ref[ROW_MLPG:ROW_MLPG + DEPTH, :]
        if name == "conv_b":
            return g_ref[ROW_CONVB:ROW_CONVB + DEPTH, :]
        if name == "ssm_norm_g":
            return g_ref[ROW_SSMG:ROW_SSMG + DEPTH, 0:D_SSM]
        if name == "rel_bias":
            return g_ref[ROW_RELB:ROW_RELB + NQ, 0:N_BUCKETS].T
        lane, width = MISC_LANES[name]
        return g_ref[ROW_MISC:ROW_MISC + DEPTH, lane:lane + width]

    def body(part_ref, land_ref, *refs):
        ws, ms, vs = refs[:n], refs[n:2 * n], refs[2 * n:3 * n]
        loss_ref = refs[3 * n]
        outs = refs[3 * n + 1:-1]
        g_ref = refs[-1]
        me = _dev_index(*_my_place())
        for p in range(N_DEV):
            term = jnp.where(me == p, part_ref[...], land_ref[p])
            if p == 0:
                g_ref[...] = term
            else:
                g_ref[...] += term
        loss_ref[...] = g_ref[ROW_LOSS:ROW_LOSS + 1, 0:128]
        my_cols = pl.ds(pl.multiple_of(me * 128, 128), 128)
        for k, name in enumerate(SMALL_NAMES):
            g_out, d_out, m_out, v_out = outs[4 * k:4 * k + 4]
            if name == "conv_w":
                for l in range(DEPTH):
                    g = g_ref[ROW_CONVW + 4 * l:ROW_CONVW + 4 * l + 4, my_cols]
                    delta, m_new, v_new = _adamw_math(ws[k][l], ms[k][l], vs[k][l], g)
                    g_out[l], d_out[l], m_out[l], v_out[l] = g, delta, m_new, v_new
            else:
                g = grad_of(name, g_ref)
                delta, m_new, v_new = _adamw_math(ws[k][...], ms[k][...], vs[k][...], g)
                g_out[...], d_out[...], m_out[...], v_out[...] = g, delta, m_new, v_new

    ws = [w[name] for name in SMALL_NAMES]
    out_shape = [SDS((1, 128), f32)]
    for a in ws:
        out_shape += [SDS(a.shape, f32)] * 4
    return pl.pallas_call(body, out_shape=out_shape, name="adamw_small", scratch_shapes=[pltpu.VMEM((SMALL_ROWS, D), f32)])(
        part, land, *ws, *[m[name] for name in SMALL_NAMES], *[v[name] for name in SMALL_NAMES])


def _plain(tm, tn):
    return pl.BlockSpec((tm, tn), lambda i, j, k: (i, j))


def _rowblk(tm, width):
    return pl.BlockSpec((tm, width), lambda i, j, k: (i, 0))


def _store_epi(dtype):
    def epi(acc, i, j, ex, outs):
        outs[0][...] = acc.astype(dtype)
    return epi


def _rms_prologue(layer):
    def pro(a_ref, ex, outs):
        xv = a_ref[...]
        r = lax.rsqrt(jnp.mean(xv * xv, axis=-1, keepdims=True) + EPS)
        h = (xv * r * ex[0][layer:layer + 1, :]).astype(bf16)
        outs[-1][...] = h
        return h
    return pro


MLP_TM = 256
MLP_VMEM = 56 * 1024 * 1024


def _resident(shape):
    return pl.BlockSpec((None,) + shape, lambda i: (0, 0, 0), pipeline_mode=pl.Buffered(1))


def _mlp_fwd(layer, x_mid, g, w_up, w_down, tgt=None):
    tm = MLP_TM
    with_loss = tgt is not None

    def body(x_ref, g_ref, wu_ref, wd_ref, *rest):
        a_ref, r_ref, h_ref = rest[with_loss:with_loss + 3]
        i = pl.program_id(0)
        xv = x_ref[...]
        h = (xv * lax.rsqrt(jnp.mean(xv * xv, axis=-1, keepdims=True) + EPS) * g_ref[layer:layer + 1, :]).astype(bf16)
        h_ref[...] = h
        r = jnp.maximum(_dot(h, wu_ref[...], NN_DIMS), 0.0)
        a = (r * r).astype(bf16)
        a_ref[...] = a
        r_ref[...] = r.astype(bf16)
        y = xv + _dot(a, wd_ref[...], NN_DIMS)
        if not with_loss:
            rest[3][...] = y
            return
        err = y - rest[0][...]
        rest[4][...] = err * (1.0 / D)
        part = 0.5 * jnp.sum(jnp.mean(err * err, axis=-1, keepdims=True), axis=0, keepdims=True)

        @pl.when(i == 0)
        def _():
            rest[5][...] = jnp.zeros_like(rest[5])

        rest[5][...] += jnp.broadcast_to(part, rest[5].shape)

    row = lambda width: pl.BlockSpec((tm, width), lambda i: (i, 0))
    in_specs = [row(D), pl.BlockSpec((DEPTH, D), lambda i: (0, 0)), _resident((D, D_FF)), _resident((D_FF, D))]
    out_specs = [row(D_FF), row(D_FF), row(D), row(D)]
    out_shape = [SDS((S, D_FF), bf16), SDS((S, D_FF), bf16), SDS((S, D), bf16), SDS((S, D), f32)]
    args = [x_mid, g, w_up, w_down]
    if with_loss:
        in_specs.append(row(D))
        args.append(tgt)
        out_specs.append(pl.BlockSpec((1, 128), lambda i: (0, 0)))
        out_shape.append(SDS((1, 128), f32))
    return pl.pallas_call(
        body, grid=(S // tm,), in_specs=in_specs, out_specs=out_specs, out_shape=_out_hbm(out_shape),
        name="mlp_fwd_loss" if with_loss else "mlp_fwd",
        compiler_params=pltpu.CompilerParams(dimension_semantics=("arbitrary",), vmem_limit_bytes=MLP_VMEM),
    )(*_in_hbm(args))


def _mlp_bwd_act(layer, dx_out, r_act, x_mid, g, w_down, w_up, deps):
    tm = MLP_TM

    def body(dxo_ref, r_ref, xm_ref, g_ref, wd_ref, wu_ref, *rest):
        du_ref, dx_ref, dg_ref = rest[len(deps):]
        dxo = dxo_ref[...]
        du = (_dot(dxo.astype(bf16), wd_ref[...], NT_DIMS) * (2.0 * r_ref[...].astype(f32))).astype(bf16)
        du_ref[...] = du
        dh = _dot(du, wu_ref[...], NT_DIMS)
        _rms_bwd_epilogue(layer)(dh, pl.program_id(0), 0, (xm_ref, g_ref, dxo_ref), (dx_ref, dg_ref))

    row = lambda width: pl.BlockSpec((tm, width), lambda i: (i, 0))
    return pl.pallas_call(
        body, grid=(S // tm,),
        in_specs=[row(D), row(D_FF), row(D), pl.BlockSpec((DEPTH, D), lambda i: (0, 0)), _resident((D_FF, D)), _resident((D, D_FF))]
        + [ANY_SPEC] * len(deps),
        out_specs=[row(D_FF), row(D), pl.BlockSpec((1, D), lambda i: (0, 0))],
        out_shape=_out_hbm([SDS((S, D_FF), bf16), SDS((S, D), f32), SDS((1, D), f32)]), name="mlp_bwd_act",
        compiler_params=pltpu.CompilerParams(dimension_semantics=("arbitrary",), vmem_limit_bytes=MLP_VMEM),
    )(*_in_hbm([dx_out, r_act, x_mid, g, w_down, w_up, *deps]))


def _layer_fwd(l, x, p, get_weights, bias, tgt=None):
    wts = get_weights(l, "in", [x, bias])
    gfull = pl.BlockSpec((DEPTH, D), lambda i, j, k: (0, 0))
    tm = 256

    def inproj_epi(acc, i, j, ex, outs):
        outs[0][...] = acc[:, COL_QKV:COL_Z]
        outs[1][...] = acc[:, COL_Z:COL_DT]
        outs[2][...] = acc[:, COL_XBC:D_IN_PAD]
        outs[3][...] = acc[:, COL_DT:COL_DT + 128]

    qkv, z, xbc, dt, h1 = _matmul(
        "in_proj", "nn", x, wts["w_in"], tm=tm, tn=D_IN_PAD, tk=D, prologue=_rms_prologue(l),
        extras=(p["mix_norm_g"],), extra_specs=(gfull,),
        out_shape=[SDS((S, 768), f32), SDS((S, 512), f32), SDS((S, 1024), f32), SDS((S, 128), f32), SDS((S, D), bf16)],
        out_specs=[_rowblk(tm, 768), _rowblk(tm, 512), _rowblk(tm, 1024), _rowblk(tm, 128), _rowblk(tm, D)], epilogue=inproj_epi)
    attn = _attn_fwd(qkv, p["q_gain"], p["k_gain"], p["sinks"], bias, l)
    xact = _conv_fwd(xbc, wts["conv_w"], p["conv_b"], l)
    mix, hs, y_ssd = _ssd_fwd(xact, z, dt, attn, p["dt_bias"], p["a_log"], p["d_skip"], p["ssm_norm_g"], l)
    wts = dict(wts, **get_weights(l, "rest", [mix]))

    def resid_epi(acc, i, j, ex, outs):
        outs[0][...] = ex[0][...] + acc

    x_mid = _matmul("out_proj", "nn", mix, wts["w_out"], tm=tm, tn=D, tk=D, out_shape=SDS((S, D), f32),
                    out_specs=_plain(tm, D), epilogue=resid_epi, extras=(x,), extra_specs=(_plain(tm, D),))

    a_act, r_act, h2, *result = _mlp_fwd(l, x_mid, p["mlp_norm_g"], wts["w_up"], wts["w_down"], tgt)
    saved = dict(x=x, h1=h1, qkv=qkv, z=z, xbc=xbc, dt=dt, xact=xact, mix=mix, hs=hs, y_ssd=y_ssd, x_mid=x_mid, h2=h2,
                 a=a_act, r=r_act, wts=wts)
    return (result[0] if tgt is None else tuple(result)), saved


def _layer_bwd(l, dx_out, sv, p, bias, deps, send):
    wts = sv["wts"]

    du, dx_mid, dg_mlp = _mlp_bwd_act(l, dx_out, sv["r"], sv["x_mid"], p["mlp_norm_g"], wts["w_down"], wts["w_up"], deps)
    dw_down = _matmul("dw_down", "tn", sv["a"], dx_out, tm=1024, tn=D, tk=S, out_shape=SDS((D_FF, D), bf16),
                      out_specs=_plain(1024, D), epilogue=_store_epi(bf16))
    dw_up = _matmul("dw_up", "tn", sv["h2"], du, tm=D, tn=1024, tk=S, out_shape=SDS((D, D_FF), bf16),
                    out_specs=_plain(D, 1024), epilogue=_store_epi(bf16))
    dw_out = _matmul("dw_out", "tn", sv["mix"], dx_mid, tm=D, tn=512, tk=512, out_shape=SDS((D, D), bf16),
                     out_specs=_plain(D, 512), epilogue=_store_epi(bf16))
    deps = send(l, dict(w_down=dw_down, w_up=dw_up, w_out=dw_out))
    gfull = pl.BlockSpec((DEPTH, D), lambda i, j, k: (0, 0))
    grow = pl.BlockSpec((1, D), lambda i, j, k: (0, 0))
    dmix = _matmul("out_proj_da", "nt", dx_mid, wts["w_out"], tm=256, tn=D, tk=D, out_shape=SDS((S, D), f32),
                   out_specs=_plain(256, D), epilogue=_store_epi(f32), deps=deps)
    dproj, dbias, dsm_attn = _attn_bwd(sv["qkv"], dmix, p["q_gain"], p["k_gain"], p["sinks"], bias, l)
    dproj, dxact, dsm_ssd = _ssd_bwd(sv["xact"], sv["z"], sv["dt"], dmix, sv["hs"], sv["y_ssd"], p["dt_bias"], p["a_log"],
                                     p["d_skip"], p["ssm_norm_g"], dproj, l)
    dproj, dconv_w, dconv_b = _conv_bwd(sv["xbc"], dxact, wts["conv_w"], p["conv_b"], dproj, l)
    dw_in = _matmul("dw_in", "tn", sv["h1"], dproj, tm=D, tn=640, tk=S, out_shape=SDS((D, D_IN_PAD), bf16),
                    out_specs=_plain(D, 640), epilogue=_store_epi(bf16))
    deps = send(l, dict(w_in=_w_in_slabs(dw_in)))
    dx, dg_mix = _matmul(
        "in_proj_dh", "nt", dproj, wts["w_in"], tm=256, tn=D, tk=D_IN_PAD, out_shape=[SDS((S, D), f32), SDS((1, D), f32)],
        out_specs=[_plain(256, D), grow], epilogue=_rms_bwd_epilogue(l),
        extras=(sv["x"], p["mix_norm_g"], dx_mid), extra_specs=(_plain(256, D), gfull, _plain(256, D)), deps=deps)
    small = dict(mix_norm_g=dg_mix, mlp_norm_g=dg_mlp, conv_w=dconv_w, conv_b=dconv_b, ssd=dsm_ssd, attn=dsm_attn, dbias=dbias)
    return dx, small, deps


def _local_step(x, tgt, p, get_weights, send):
    onehot_t = jnp.asarray(_onehot_buckets())
    bias = _bias_build(p["rel_bias"].T, onehot_t).reshape(NQ, BLK, 2 * BLK)
    saved = []
    h = x
    for l in range(DEPTH):
        h, sv = _layer_fwd(l, h, p, get_weights, bias, tgt if l == DEPTH - 1 else None)
        saved.append(sv)
    dx, loss = h
    smalls = [None] * DEPTH
    deps = ()
    for l in reversed(range(DEPTH)):
        dx, smalls[l], deps = _layer_bwd(l, dx, saved[l], p, bias, deps, send)
    drel_t = _bias_grad(smalls[0]["dbias"].reshape(NQ, -1), smalls[1]["dbias"].reshape(NQ, -1), onehot_t)
    return dx, _pack_small_grads(smalls, drel_t, loss)


WEIGHT_ORDER = ("mix_norm_g", "w_in", "q_gain", "k_gain", "sinks", "rel_bias", "conv_w", "conv_b", "dt_bias", "a_log", "d_skip",
                "ssm_norm_g", "w_out", "mlp_norm_g", "w_up", "w_down")


def kernel(x, mix_norm_g, w_in, q_gain, k_gain, sinks, rel_bias, conv_w, conv_b, dt_bias, a_log, d_skip, ssm_norm_g, w_out, mlp_norm_g, w_up, w_down, loss_target, m_mix_norm_g, m_w_in, m_q_gain, m_k_gain, m_sinks, m_rel_bias, m_conv_w, m_conv_b, m_dt_bias, m_a_log, m_d_skip, m_ssm_norm_g, m_w_out, m_mlp_norm_g, m_w_up, m_w_down, v_mix_norm_g, v_w_in, v_q_gain, v_k_gain, v_sinks, v_rel_bias, v_conv_w, v_conv_b, v_dt_bias, v_a_log, v_d_skip, v_ssm_norm_g, v_w_out, v_mlp_norm_g, v_w_up, v_w_down):
    w = dict(mix_norm_g=mix_norm_g, w_in=w_in, q_gain=q_gain, k_gain=k_gain, sinks=sinks, rel_bias=rel_bias, conv_w=conv_w,
             conv_b=conv_b, dt_bias=dt_bias, a_log=a_log, d_skip=d_skip, ssm_norm_g=ssm_norm_g, w_out=w_out,
             mlp_norm_g=mlp_norm_g, w_up=w_up, w_down=w_down)
    m = dict(mix_norm_g=m_mix_norm_g, w_in=m_w_in, q_gain=m_q_gain, k_gain=m_k_gain, sinks=m_sinks, rel_bias=m_rel_bias,
             conv_w=m_conv_w, conv_b=m_conv_b, dt_bias=m_dt_bias, a_log=m_a_log, d_skip=m_d_skip, ssm_norm_g=m_ssm_norm_g,
             w_out=m_w_out, mlp_norm_g=m_mlp_norm_g, w_up=m_w_up, w_down=m_w_down)
    v = dict(mix_norm_g=v_mix_norm_g, w_in=v_w_in, q_gain=v_q_gain, k_gain=v_k_gain, sinks=v_sinks, rel_bias=v_rel_bias,
             conv_w=v_conv_w, conv_b=v_conv_b, dt_bias=v_dt_bias, a_log=v_a_log, d_skip=v_d_skip, ssm_norm_g=v_ssm_norm_g,
             w_out=v_w_out, mlp_norm_g=v_mlp_norm_g, w_up=v_w_up, w_down=v_w_down)
    big = ("w_in", "w_out", "w_up", "w_down")

    my_idx = _dev_index(*_my_place()).astype(jnp.int32).reshape(1)

    fulls = {n: _cast_to_full("cast_" + n, w[n], KIND[n], FULL_SHAPE[n], my_idx, bf16) for n in big}
    conv_full = _cast_to_full("cast_conv_w", conv_w.reshape(1, DEPTH * 4, 128), "stack", (N_DEV, DEPTH * 4, 128), my_idx, f32)[0]
    rest = ["w_out", "w_up", "w_down"]
    g0 = _gather_start("gather0", ["w_in", "conv_w"], [fulls["w_in"][0], conv_full], ())
    g1 = _gather_start("gather1", rest, [fulls[n][0] for n in rest], (g0["token"],))
    g2 = _gather_start("gather2", ["w_in"], [fulls["w_in"][1]], (g1["token"],))
    g3 = _gather_start("gather3", rest, [fulls[n][1] for n in rest], (g2["token"],))
    held = {}
    flat = lambda a: a.reshape(a.shape[0] * a.shape[1], a.shape[2])
    adam_in = {n: (flat(w[n]), flat(m[n]), flat(v[n])) for n in big}

    def get_weights(l, part, after):
        if l == 0 and part == "in":
            full_in, full_conv = _gather_finish("gather0", ["w_in", "conv_w"], g0,
                                                list(after) + [g3["token"], adam_in["w_in"][1], adam_in["w_in"][2]])
            held["conv_w"] = jnp.transpose(full_conv.reshape(N_DEV, DEPTH, 4, 128), (1, 2, 0, 3)).reshape(DEPTH, 4, D_CONV)
            return dict(w_in=_w_in_assemble(full_in), conv_w=held["conv_w"])
        if part == "in":
            return dict(w_in=_w_in_assemble(_gather_finish("gather2", ["w_in"], g2, after)[0]), conv_w=held["conv_w"])
        full = _gather_finish("gather1" if l == 0 else "gather3", rest, g1 if l == 0 else g3, after)
        return {n: f[None] for n, f in zip(rest, full)}

    pending = []

    def send(l, grads):
        names = list(grads)
        started = _exchange_start("exchange%d_%s" % (l, names[0]), names, [grads[n] for n in names], ())
        pending.append((l, names, started))
        return (started["token"],)

    dx, small_part = _local_step(x.reshape(S, D), loss_target.reshape(S, D), w, get_weights, send)

    small = _small_exchange_start(small_part, ())
    tiles = dict(w_in=256, w_out=128, w_up=256, w_down=256)
    outs_of = {n: None for n in big}
    after = [dx, small["token"]]
    for l, names, started in pending:
        bufs = _split_wait("exchange%d_%s_wait" % (l, names[0]), started, after)
        for t, n in enumerate(names):
            outs_of[n] = _adamw_layer("adamw_%s%d" % (n, l), KIND[n], l, *adam_in[n],
                                      bufs[len(names) + t], bufs[t], my_idx, outs_of[n], tiles[n])
        after = [outs_of[names[-1]][0]]
    res = {n: [o.reshape(w[n].shape) for o in outs_of[n]] for n in big}
    small_part, small_land = _split_wait("small_exchange_wait", small, after)
    small_outs = _adamw_small(small_part, small_land, w, m, v)
    loss = small_outs[0][0, 0]
    for k, name in enumerate(SMALL_NAMES):
        res[name] = small_outs[1 + 4 * k:5 + 4 * k]

    result = [loss, dx.reshape(1, S, D)]
    for k in range(4):
        result += [res[name][k] for name in WEIGHT_ORDER]
    return tuple(result)
```

```python
import functools
import math

import numpy as np
import jax
import jax.numpy as jnp
from jax import lax
from jax.experimental import pallas as pl
from jax.experimental.pallas import tpu as pltpu

f32 = jnp.float32
bf16 = jnp.bfloat16
SDS = jax.ShapeDtypeStruct
MESH = pl.DeviceIdType.MESH
HIGHEST = lax.Precision.HIGHEST

S = 2048
D = 1024
DEPTH = 2
BLK = 128
NBLK = S // BLK
HD = 64
NQ = 8
NKV = 2
NSSM = 8
NGRP = 2
NSTATE = 128
D_ATTN = 512
D_SSM = 512
D_CONV = 1024
D_FF = 4096
D_IN = 2312
D_IN_PAD = 2560
COL_QKV, COL_Z, COL_DT, COL_XBC = 0, 768, 1280, 1536
IN_SEGMENTS = ((0, 1280, 0), (1280, 2304, COL_XBC), (2304, 2312, COL_DT))
N_BUCKETS = 32
EPS = 1e-6
N_DEV = 8
VMEM_LIMIT = 48 * 1024 * 1024

ADAM_LR = 0.001
ADAM_B1 = 0.9
ADAM_B2 = 0.999
ADAM_EPS = 1e-08
ADAM_WD = 0.01
ADAM_STEP = 10

NT_DIMS = (((1,), (1,)), ((), ()))
TN_DIMS = (((0,), (0,)), ((), ()))
NN_DIMS = (((1,), (0,)), ((), ()))

ROW_MIXG = 0
ROW_MLPG = 2
ROW_CONVB = 4
ROW_SSMG = 6
ROW_MISC = 8
ROW_RELB = 10
ROW_CONVW = 18
ROW_LOSS = 26
SMALL_ROWS = 32
LANE_QG, LANE_KG, LANE_SINK, LANE_DTB, LANE_ALOG, LANE_DSKIP = 0, 64, 128, 256, 384, 512


def _dot(a, b, dims):
    return lax.dot_general(a, b, dims, preferred_element_type=f32)


def _cparams(n_axes):
    return pltpu.CompilerParams(dimension_semantics=("arbitrary",) * n_axes, vmem_limit_bytes=VMEM_LIMIT)


def _sum11(v):
    return jnp.sum(jnp.sum(v, axis=1, keepdims=True), axis=0, keepdims=True)


def _sigmoid(v):
    return 1.0 / (1.0 + jnp.exp(-v))


ANY_SPEC = pl.BlockSpec(memory_space=pl.ANY)


def _in_hbm(args):
    return [pltpu.with_memory_space_constraint(a, pltpu.HBM) if a.size >= 65536 else a for a in args]


def _out_hbm(out_shape):
    one = lambda s: pltpu.HBM(s.shape, s.dtype) if math.prod(s.shape) >= 65536 else s
    return [one(s) for s in out_shape] if isinstance(out_shape, (list, tuple)) else one(out_shape)


def _matmul(name, mode, a, b, *, layer=0, tm, tn, tk, out_shape, out_specs, epilogue, extras=(), extra_specs=(), deps=(),
            prologue=None):
    extras = tuple(extras) + tuple(deps)
    extra_specs = tuple(extra_specs) + (ANY_SPEC,) * len(deps)
    if mode == "tn":
        t_dim, m_dim = a.shape
        n_dim = b.shape[1]
        grid = (m_dim // tm, n_dim // tn, t_dim // tk)
        a_spec = pl.BlockSpec((tk, tm), lambda i, j, k: (k, i))
        b_spec = pl.BlockSpec((tk, tn), lambda i, j, k: (k, j))
        dims = TN_DIMS
    elif mode == "nn":
        m_dim, k_dim = a.shape
        n_dim = b.shape[-1]
        grid = (m_dim // tm, n_dim // tn, k_dim // tk)
        a_spec = pl.BlockSpec((tm, tk), lambda i, j, k: (i, k))
        b_spec = pl.BlockSpec((None, tk, tn), lambda i, j, k: (layer, k, j))
        dims = NN_DIMS
    else:
        m_dim, k_dim = a.shape
        n_dim = b.shape[-2]
        grid = (m_dim // tm, n_dim // tn, k_dim // tk)
        a_spec = pl.BlockSpec((tm, tk), lambda i, j, k: (i, k))
        b_spec = pl.BlockSpec((None, tn, tk), lambda i, j, k: (layer, j, k))
        dims = NT_DIMS
    nk = grid[2]
    n_ex = len(extras)

    def body(a_ref, b_ref, *rest):
        ex = rest[:n_ex - len(deps)]
        outs = rest[n_ex:-1]
        acc = rest[-1]
        i = pl.program_id(0)
        j = pl.program_id(1)
        k = pl.program_id(2)
        lhs = a_ref[...].astype(bf16) if prologue is None else prologue(a_ref, ex, outs)
        part = _dot(lhs, b_ref[...].astype(bf16), dims)
        if nk == 1:
            epilogue(part, i, j, ex, outs)
        else:
            @pl.when(k == 0)
            def _():
                acc[...] = part

            @pl.when(k > 0)
            def _():
                acc[...] += part

            @pl.when(k == nk - 1)
            def _():
                epilogue(acc[...], i, j, ex, outs)

    return pl.pallas_call(
        body, grid=grid, in_specs=[a_spec, b_spec, *extra_specs], out_specs=out_specs, out_shape=_out_hbm(out_shape),
        scratch_shapes=[pltpu.VMEM((tm, tn) if nk > 1 else (8, 128), f32)], name=name, compiler_params=_cparams(3),
    )(*_in_hbm([a, b, *extras]))


def _rms_bwd_epilogue(layer):
    def epi(acc, i, j, ex, outs):
        x_ref, g_ref, dres_ref = ex
        dx_ref, dg_ref = outs
        xv = x_ref[...]
        r = lax.rsqrt(jnp.mean(xv * xv, axis=-1, keepdims=True) + EPS)
        xhat = xv * r
        w = acc * g_ref[layer:layer + 1, :]
        dx_ref[...] = dres_ref[...] + r * (w - xhat * jnp.mean(xhat * w, axis=-1, keepdims=True))
        dg = jnp.sum(acc * xhat, axis=0, keepdims=True)

        @pl.when(i == 0)
        def _():
            dg_ref[...] = dg

        @pl.when(i > 0)
        def _():
            dg_ref[...] += dg
    return epi


def _own_slab_spec(kind, tr, cols, nblk):
    if kind == "stack":
        return pl.BlockSpec((None, tr, cols), lambda i, idx: (idx[0], i, 0))
    if kind == "cols512":
        return pl.BlockSpec((tr, cols), lambda i, idx: (i, idx[0]))
    return pl.BlockSpec((tr, cols), lambda i, idx: (idx[0] * nblk + i, 0))


def _cast_to_full(name, w, kind, full_shape, my_idx, dtype):
    n_layers, rows, cols = w.shape
    tr = min(rows, 256)
    nblk = rows // tr

    def body(idx_ref, w_ref, *o_refs):
        for l in range(n_layers):
            o_refs[l][...] = w_ref[l].astype(dtype)

    grid_spec = pltpu.PrefetchScalarGridSpec(
        num_scalar_prefetch=1, grid=(nblk,), in_specs=[pl.BlockSpec((n_layers, tr, cols), lambda i, idx: (0, i, 0))],
        out_specs=[_own_slab_spec(kind, tr, cols, nblk)] * n_layers)
    return pl.pallas_call(body, grid_spec=grid_spec, out_shape=_out_hbm([SDS(full_shape, dtype)] * n_layers), name=name,
                          compiler_params=_cparams(1))(*_in_hbm([my_idx, w]))


def _adamw_math(w, m, v, g):
    m_new = ADAM_B1 * m + (1.0 - ADAM_B1) * g
    v_new = ADAM_B2 * v + (1.0 - ADAM_B2) * (g * g)
    m_hat = m_new / (1.0 - ADAM_B1 ** ADAM_STEP)
    v_hat = v_new / (1.0 - ADAM_B2 ** ADAM_STEP)
    delta = -ADAM_LR * (m_hat / (jnp.sqrt(v_hat) + ADAM_EPS) + ADAM_WD * w)
    return delta, m_new, v_new


def _adamw_layer(name, kind, layer, w, m, v, land, g_full, my_idx, prev, tr):
    rows2, cols = w.shape
    rows = rows2 // DEPTH
    nblk = rows // tr
    own_spec = _own_slab_spec(kind, tr, cols, nblk)
    n_prev = 0 if prev is None else 4

    def body(idx_ref, w_ref, m_ref, v_ref, land_ref, own_ref, *rest):
        g_ref, d_ref, mo_ref, vo_ref = rest[n_prev:]
        me = idx_ref[0]
        g = None
        for p in range(N_DEV):
            part = jnp.where(me == p, own_ref[...], land_ref[p]).astype(f32)
            g = part if g is None else g + part
        delta, m_new, v_new = _adamw_math(w_ref[...], m_ref[...], v_ref[...], g)
        g_ref[...] = g
        d_ref[...] = delta
        mo_ref[...] = m_new
        vo_ref[...] = v_new

    blk = pl.BlockSpec((tr, cols), lambda i, idx: (layer * nblk + i, 0))
    grid_spec = pltpu.PrefetchScalarGridSpec(
        num_scalar_prefetch=1, grid=(nblk,),
        in_specs=[blk, blk, blk, pl.BlockSpec((N_DEV, tr, cols), lambda i, idx: (0, i, 0)), own_spec] + [ANY_SPEC] * n_prev,
        out_specs=[blk, blk, blk, blk])
    aliases = {} if prev is None else {6 + k: k for k in range(4)}
    return pl.pallas_call(
        body, grid_spec=grid_spec, out_shape=_out_hbm([SDS((rows2, cols), f32)] * 4), name=name, input_output_aliases=aliases,
        compiler_params=_cparams(1),
    )(*_in_hbm([my_idx, w, m, v, land, g_full, *([] if prev is None else prev)]))


def _bucket_table():
    qi = np.arange(BLK)[:, None]
    kj = np.arange(2 * BLK)[None, :]
    dist = qi + BLK - kj
    dcl = np.clip(dist, 0, None)
    max_exact = N_BUCKETS // 2
    d_f = np.maximum(dcl, 1).astype(np.float32)
    large = max_exact + (np.log(d_f / np.float32(max_exact)) / np.float32(math.log(128 / max_exact))
                         * np.float32(N_BUCKETS - max_exact)).astype(np.int32)
    large = np.minimum(large, N_BUCKETS - 1)
    bucket = np.where(dcl < max_exact, dcl, large)
    in_window = (dist >= 0) & (dist < BLK)
    return bucket.astype(np.int32), in_window


def _onehot_buckets():
    bucket, _ = _bucket_table()
    oh = (bucket.reshape(-1)[None, :] == np.arange(N_BUCKETS)[:, None]).astype(np.float32)
    return oh


def _bias_build(rel_bias_t, onehot_t):
    def body(r_ref, o_ref, out_ref):
        out_ref[...] = jnp.dot(r_ref[...], o_ref[...], preferred_element_type=f32, precision=HIGHEST)

    tn = 4096
    return pl.pallas_call(
        body, grid=(BLK * 2 * BLK // tn,),
        in_specs=[pl.BlockSpec((NQ, N_BUCKETS), lambda i: (0, 0)), pl.BlockSpec((N_BUCKETS, tn), lambda i: (0, i))],
        out_specs=pl.BlockSpec((NQ, tn), lambda i: (0, i)), out_shape=SDS((NQ, BLK * 2 * BLK), f32), name="bias_build",
        compiler_params=_cparams(1),
    )(rel_bias_t, onehot_t)


def _bias_grad(dbias0, dbias1, onehot_t):
    tn = 4096
    nsteps = BLK * 2 * BLK // tn

    def body(a_ref, b_ref, o_ref, out_ref):
        part = lax.dot_general(a_ref[...] + b_ref[...], o_ref[...], NT_DIMS, preferred_element_type=f32, precision=HIGHEST)

        @pl.when(pl.program_id(0) == 0)
        def _():
            out_ref[...] = part

        @pl.when(pl.program_id(0) > 0)
        def _():
            out_ref[...] += part

    return pl.pallas_call(
        body, grid=(nsteps,),
        in_specs=[pl.BlockSpec((NQ, tn), lambda i: (0, i)), pl.BlockSpec((NQ, tn), lambda i: (0, i)),
                  pl.BlockSpec((N_BUCKETS, tn), lambda i: (0, i))],
        out_specs=pl.BlockSpec((NQ, N_BUCKETS), lambda i: (0, 0)), out_shape=SDS((NQ, N_BUCKETS), f32), name="bias_grad",
        compiler_params=_cparams(1),
    )(dbias0, dbias1, onehot_t)


def _attn_mask(n):
    qi = lax.broadcasted_iota(jnp.int32, (BLK, 2 * BLK), 0)
    kj = lax.broadcasted_iota(jnp.int32, (BLK, 2 * BLK), 1)
    dist = qi + BLK - kj
    first_key = jnp.where(n > 0, 0, BLK)
    return (dist >= 0) & (dist < BLK) & (kj >= first_key)


def _row_mean(a):
    return jnp.mean(a, axis=-1, keepdims=True)


def _head_norm(t, gain):
    r = lax.rsqrt(_row_mean(t * t) + EPS)
    that = t * r
    return that, r, that * gain


def _softmax_with_sink(s, sink):
    m = jnp.maximum(jnp.max(s, axis=-1, keepdims=True), sink)
    p = jnp.exp(s - m)
    psink = jnp.exp(sink - m)
    inv = 1.0 / (jnp.sum(p, axis=-1, keepdims=True) + psink)
    return p * inv, psink * inv


GQ = NQ // NKV


def _attn_fwd(qkv, q_gain, k_gain, sinks, bias, layer):
    def body(q_ref, kc_ref, kp_ref, vc_ref, vp_ref, qg_ref, kg_ref, sk_ref, bias_ref, o_ref):
        m = pl.program_id(0)
        qg = qg_ref[layer:layer + 1, :]
        kg = kg_ref[layer:layer + 1, :]
        grp = range(NKV)
        chains = [(b, j) for b in range(2) for j in grp]
        masks = [jnp.tile(_attn_mask(2 * m + b), (GQ, 1)) for b in range(2)]
        kblk = [[kp_ref[:, pl.ds(HD * j, HD)], kc_ref[0:BLK, pl.ds(HD * j, HD)], kc_ref[BLK:, pl.ds(HD * j, HD)]] for j in grp]
        vblk = [[vp_ref[:, pl.ds(HD * j, HD)].astype(bf16), vc_ref[0:BLK, pl.ds(HD * j, HD)].astype(bf16),
                 vc_ref[BLK:, pl.ds(HD * j, HD)].astype(bf16)] for j in grp]
        knb = [[_head_norm(kblk[j][t], kg)[2].astype(bf16) for t in range(3)] for j in grp]
        kn_b = {(b, j): jnp.concatenate([knb[j][b], knb[j][b + 1]], axis=0) for b, j in chains}
        vbs = {(b, j): jnp.concatenate([vblk[j][b], vblk[j][b + 1]], axis=0) for b, j in chains}
        rows = {}
        for b, j in chains:
            heads = [GQ * j + g for g in range(GQ)]
            rows[b, j] = (jnp.concatenate([q_ref[pl.ds(BLK * b, BLK), pl.ds(HD * h, HD)] for h in heads], axis=0),
                          jnp.concatenate([jnp.broadcast_to(sk_ref[layer:layer + 1, h:h + 1], (BLK, 1)) for h in heads], axis=0))
        qn_b = {c: _head_norm(rows[c][0], qg)[2].astype(bf16) for c in chains}
        ss = {(b, j): _dot(qn_b[b, j], kn_b[b, j], NT_DIMS) * (HD ** -0.5) + bias_ref[GQ * j:GQ * (j + 1)].reshape(GQ * BLK, 2 * BLK)
              for b, j in chains}
        ps = {(b, j): _softmax_with_sink(jnp.where(masks[b], ss[b, j], -jnp.inf), rows[b, j][1])[0] for b, j in chains}
        outs = {c: _dot(ps[c].astype(bf16), vbs[c], NN_DIMS).astype(bf16) for c in chains}
        for b, j in chains:
            for g in range(GQ):
                o_ref[pl.ds(BLK * b, BLK), pl.ds(HD * (GQ * j + g), HD)] = outs[b, j][BLK * g:BLK * (g + 1), :]

    prev = lambda m: jnp.maximum(2 * m - 1, 0)
    small = lambda shape: pl.BlockSpec(shape, lambda m: (0,) * len(shape))
    return pl.pallas_call(
        body, grid=(NBLK // 2,),
        in_specs=[pl.BlockSpec((2 * BLK, D_ATTN), lambda m: (m, 0)),
                  pl.BlockSpec((2 * BLK, 128), lambda m: (m, 4)), pl.BlockSpec((BLK, 128), lambda m: (prev(m), 4)),
                  pl.BlockSpec((2 * BLK, 128), lambda m: (m, 5)), pl.BlockSpec((BLK, 128), lambda m: (prev(m), 5)),
                  small((DEPTH, HD)), small((DEPTH, HD)), small((DEPTH, NQ)), small((NQ, BLK, 2 * BLK))],
        out_specs=pl.BlockSpec((2 * BLK, D_ATTN), lambda m: (m, 0)), out_shape=_out_hbm(SDS((S, D_ATTN), bf16)),
        name="attn_fwd", compiler_params=_cparams(1),
    )(*_in_hbm([qkv, qkv, qkv, qkv, qkv, q_gain, k_gain, sinks, bias]))


def _attn_bwd(qkv, dmix, q_gain, k_gain, sinks, bias, layer, deps=()):
    def body(q_ref, kc_ref, kp_ref, vc_ref, vp_ref, do_ref, qg_ref, kg_ref, sk_ref, bias_ref, *rest):
        dqkv_ref, dbias_ref, dsm_ref, carry = rest[len(deps):]
        i = pl.program_id(0)
        m = NBLK // 2 - 1 - i
        qg = qg_ref[layer:layer + 1, :]
        kg = kg_ref[layer:layer + 1, :]
        lane = lax.broadcasted_iota(jnp.int32, (1, 128), 1)

        @pl.when(i == 0)
        def _():
            carry[...] = jnp.zeros_like(carry)
            dbias_ref[...] = jnp.zeros_like(dbias_ref)
            dsm_ref[...] = jnp.zeros_like(dsm_ref)

        grp = range(NKV)
        chains = [(b, j) for b in range(2) for j in grp]
        masks = [jnp.tile(_attn_mask(2 * m + b), (GQ, 1)) for b in range(2)]
        kblk = [[kp_ref[:, pl.ds(HD * j, HD)], kc_ref[0:BLK, pl.ds(HD * j, HD)], kc_ref[BLK:, pl.ds(HD * j, HD)]] for j in grp]
        vblk = [[vp_ref[:, pl.ds(HD * j, HD)].astype(bf16), vc_ref[0:BLK, pl.ds(HD * j, HD)].astype(bf16),
                 vc_ref[BLK:, pl.ds(HD * j, HD)].astype(bf16)] for j in grp]
        knorm = [[_head_norm(kblk[j][t], kg) for t in range(3)] for j in grp]
        kn_b = {(b, j): jnp.concatenate([knorm[j][b][2].astype(bf16), knorm[j][b + 1][2].astype(bf16)], axis=0) for b, j in chains}
        vbs = {(b, j): jnp.concatenate([vblk[j][b], vblk[j][b + 1]], axis=0) for b, j in chains}
        rows, do_b = {}, {}
        for b, j in chains:
            heads = [GQ * j + g for g in range(GQ)]
            qrows = pl.ds(BLK * b, BLK)
            rows[b, j] = (jnp.concatenate([q_ref[qrows, pl.ds(HD * h, HD)] for h in heads], axis=0),
                          jnp.concatenate([jnp.broadcast_to(sk_ref[layer:layer + 1, h:h + 1], (BLK, 1)) for h in heads], axis=0))
            do_b[b, j] = jnp.concatenate([do_ref[qrows, pl.ds(HD * h, HD)] for h in heads], axis=0).astype(bf16)
        qnorm = {c: _head_norm(rows[c][0], qg) for c in chains}
        qn_b = {c: qnorm[c][2].astype(bf16) for c in chains}
        ss = {(b, j): _dot(qn_b[b, j], kn_b[b, j], NT_DIMS) * (HD ** -0.5) + bias_ref[GQ * j:GQ * (j + 1)].reshape(GQ * BLK, 2 * BLK)
              for b, j in chains}
        sm = {(b, j): _softmax_with_sink(jnp.where(masks[b], ss[b, j], -jnp.inf), rows[b, j][1]) for b, j in chains}
        dps = {c: _dot(do_b[c], vbs[c], NT_DIMS) for c in chains}
        deltas = {c: jnp.sum(sm[c][0] * dps[c], axis=-1, keepdims=True) for c in chains}
        dss = {c: sm[c][0] * (dps[c] - deltas[c]) for c in chains}
        ds_b = {c: (dss[c] * (HD ** -0.5)).astype(bf16) for c in chains}
        dqn = {c: _dot(ds_b[c], kn_b[c], NN_DIMS) for c in chains}
        dkn = {c: _dot(ds_b[c], qn_b[c], TN_DIMS) for c in chains}
        dvs = {c: _dot(sm[c][0].astype(bf16), do_b[c], TN_DIMS) for c in chains}
        dqg = jnp.zeros((1, HD), f32)
        dkg = jnp.zeros((1, HD), f32)
        dsink = jnp.zeros((1, 128), f32)
        for b, j in chains:
            dbias_ref[GQ * j:GQ * (j + 1)] += dss[b, j].reshape(GQ, BLK, 2 * BLK)
            dsk = sm[b, j][1] * deltas[b, j]
            for g in range(GQ):
                dsink = dsink + jnp.where(lane == GQ * j + g, -_sum11(dsk[BLK * g:BLK * (g + 1), :]), 0.0)
            qhat, rq, _ = qnorm[b, j]
            w = dqn[b, j] * qg
            dq = rq * (w - qhat * _row_mean(qhat * w))
            for g in range(GQ):
                dqkv_ref[pl.ds(BLK * b, BLK), pl.ds(HD * (GQ * j + g), HD)] = dq[BLK * g:BLK * (g + 1), :].astype(bf16)
            dqg = dqg + jnp.sum(dqn[b, j] * qhat, axis=0, keepdims=True)
        for j in grp:
            dkn_t = [dkn[0, j][:BLK, :], dkn[0, j][BLK:, :] + dkn[1, j][:BLK, :], dkn[1, j][BLK:, :]]
            dv_t = [dvs[0, j][:BLK, :], dvs[0, j][BLK:, :] + dvs[1, j][:BLK, :], dvs[1, j][BLK:, :]]
            dk_t = []
            for t in range(3):
                khat, rk, _ = knorm[j][t]
                w = dkn_t[t] * kg
                dk_t.append(rk * (w - khat * _row_mean(khat * w)))
                dkg = dkg + jnp.sum(dkn_t[t] * khat, axis=0, keepdims=True)
            kcols, vcols = pl.ds(D_ATTN + HD * j, HD), pl.ds(D_ATTN + 128 + HD * j, HD)
            dqkv_ref[BLK:, kcols] = (dk_t[2] + carry[:, pl.ds(HD * j, HD)]).astype(bf16)
            dqkv_ref[BLK:, vcols] = (dv_t[2] + carry[:, pl.ds(128 + HD * j, HD)]).astype(bf16)
            dqkv_ref[0:BLK, kcols] = dk_t[1].astype(bf16)
            dqkv_ref[0:BLK, vcols] = dv_t[1].astype(bf16)
            carry[:, pl.ds(HD * j, HD)] = dk_t[0]
            carry[:, pl.ds(128 + HD * j, HD)] = dv_t[0]
        dsm_ref[0:1, 0:HD] += dqg
        dsm_ref[1:2, 0:HD] += dkg
        dsm_ref[2:3, :] += dsink

    rev = lambda i: NBLK // 2 - 1 - i
    prev = lambda i: jnp.maximum(NBLK - 3 - 2 * i, 0)
    small = lambda shape: pl.BlockSpec(shape, lambda i: (0,) * len(shape))
    return pl.pallas_call(
        body, grid=(NBLK // 2,),
        in_specs=[pl.BlockSpec((2 * BLK, D_ATTN), lambda i: (rev(i), 0)),
                  pl.BlockSpec((2 * BLK, 128), lambda i: (rev(i), 4)), pl.BlockSpec((BLK, 128), lambda i: (prev(i), 4)),
                  pl.BlockSpec((2 * BLK, 128), lambda i: (rev(i), 5)), pl.BlockSpec((BLK, 128), lambda i: (prev(i), 5)),
                  pl.BlockSpec((2 * BLK, D_ATTN), lambda i: (rev(i), 0)),
                  small((DEPTH, HD)), small((DEPTH, HD)), small((DEPTH, NQ)), small((NQ, BLK, 2 * BLK))] + [ANY_SPEC] * len(deps),
        out_specs=[pl.BlockSpec((2 * BLK, 768), lambda i: (rev(i), COL_QKV // 768)), small((NQ, BLK, 2 * BLK)), small((8, 128))],
        out_shape=_out_hbm([SDS((S, D_IN_PAD), bf16), SDS((NQ, BLK, 2 * BLK), f32), SDS((8, 128), f32)]),
        scratch_shapes=[pltpu.VMEM((BLK, 256), f32)], name="attn_bwd", compiler_params=_cparams(1),
    )(*_in_hbm([qkv, qkv, qkv, qkv, qkv, dmix, q_gain, k_gain, sinks, bias, *deps]))


CONV_TC = 128


def _shift_down(u, s):
    if s == 0:
        return u
    rows = lax.broadcasted_iota(jnp.int32, u.shape, 0)
    return jnp.where(rows >= s, pltpu.roll(u, s, 0), 0.0)


def _shift_up(u, s):
    if s == 0:
        return u
    rows = lax.broadcasted_iota(jnp.int32, u.shape, 0)
    return jnp.where(rows < u.shape[0] - s, pltpu.roll(u, u.shape[0] - s, 0), 0.0)


def _conv_specs():
    return [pl.BlockSpec((S, CONV_TC), lambda c: (0, c)),
            pl.BlockSpec((None, 4, CONV_TC), lambda c: (0, 0, c)),
            pl.BlockSpec((DEPTH, CONV_TC), lambda c: (0, c))]


def _conv_pre(u, w_ref, b_ref, layer):
    pre = b_ref[layer:layer + 1, :] + w_ref[3:4, :] * u
    for k in range(3):
        pre = pre + w_ref[k:k + 1, :] * _shift_down(u, 3 - k)
    return pre


def _conv_fwd(xbc, conv_w, conv_b, layer):
    def body(u_ref, w_ref, b_ref, o_ref):
        pre = _conv_pre(u_ref[...], w_ref, b_ref, layer)
        o_ref[...] = pre * _sigmoid(pre)

    specs = _conv_specs()
    specs[1] = pl.BlockSpec((None, 4, CONV_TC), lambda c: (layer, 0, c))
    return pl.pallas_call(
        body, grid=(D_CONV // CONV_TC,), in_specs=specs, out_specs=pl.BlockSpec((S, CONV_TC), lambda c: (0, c)),
        out_shape=_out_hbm(SDS((S, D_CONV), f32)), name="conv_fwd", compiler_params=_cparams(1),
    )(*_in_hbm([xbc, conv_w, conv_b]))


def _conv_bwd(xbc, dact, conv_w, conv_b, dproj, layer):
    def body(u_ref, w_ref, b_ref, da_ref, dproj_in, du_ref, dw_ref, db_ref):
        u = u_ref[...]
        pre = _conv_pre(u, w_ref, b_ref, layer)
        sg = _sigmoid(pre)
        dpre = da_ref[...] * (sg * (1.0 + pre * (1.0 - sg)))
        du = w_ref[3:4, :] * dpre
        for k in range(3):
            du = du + w_ref[k:k + 1, :] * _shift_up(dpre, 3 - k)
        du_ref[...] = du.astype(bf16)
        db_ref[...] = jnp.broadcast_to(jnp.sum(dpre, axis=0, keepdims=True), db_ref.shape)
        dw_ref[...] = jnp.zeros_like(dw_ref)
        for k in range(4):
            dw_ref[k:k + 1, :] = jnp.sum(dpre * _shift_down(u, 3 - k), axis=0, keepdims=True)

    specs = _conv_specs()
    specs[1] = pl.BlockSpec((None, 4, CONV_TC), lambda c: (layer, 0, c))
    col = pl.BlockSpec((S, CONV_TC), lambda c: (0, c))
    row8 = pl.BlockSpec((8, CONV_TC), lambda c: (0, c))
    return pl.pallas_call(
        body, grid=(D_CONV // CONV_TC,), in_specs=[*specs, col, ANY_SPEC],
        out_specs=[pl.BlockSpec((S, CONV_TC), lambda c: (0, COL_XBC // CONV_TC + c)), row8, row8],
        out_shape=_out_hbm([SDS((S, D_IN_PAD), bf16), SDS((8, D_CONV), f32), SDS((8, D_CONV), f32)]), name="conv_bwd",
        input_output_aliases={4: 0}, compiler_params=_cparams(1),
    )(*_in_hbm([xbc, conv_w, conv_b, dact, dproj]))


def _tri():
    return (lax.broadcasted_iota(jnp.int32, (BLK, BLK), 0) >= lax.broadcasted_iota(jnp.int32, (BLK, BLK), 1))


def _ssd_scalars(dt_ref, dtb_ref, alog_ref, layer):
    raw = dt_ref[:, 0:NSSM] + dtb_ref[layer:layer + 1, :]
    dtv = jnp.maximum(raw, 0.0) + jnp.log(1.0 + jnp.exp(-jnp.abs(raw)))
    a = -jnp.exp(alog_ref[layer:layer + 1, :])
    acs = jnp.dot(_tri().astype(f32), dtv * a, preferred_element_type=f32, precision=HIGHEST)
    return raw, dtv, a, acs


HG = NSSM // NGRP
GW = HG * HD


def _lane_expand(cols, g):
    lane_head = lax.broadcasted_iota(jnp.int32, (1, GW), 1) // HD
    out = cols[:, HG * g + HG - 1:HG * g + HG]
    for r in range(HG - 2, -1, -1):
        out = jnp.where(lane_head == r, cols[:, HG * g + r:HG * g + r + 1], out)
    return out


def _row_expand(vals, g):
    row_head = lax.broadcasted_iota(jnp.int32, (GW, 1), 0) // HD
    out = vals[:, HG * g + HG - 1:HG * g + HG]
    for r in range(HG - 2, -1, -1):
        out = jnp.where(row_head == r, vals[:, HG * g + r:HG * g + r + 1], out)
    return out


def _head_rowsums(a):
    sel = (lax.broadcasted_iota(jnp.int32, (GW, HG), 0) // HD == lax.broadcasted_iota(jnp.int32, (GW, HG), 1)).astype(bf16)
    hi = a.astype(bf16)
    lo = (a - hi.astype(f32)).astype(bf16)
    sums = _dot(hi, sel, NN_DIMS) + _dot(lo, sel, NN_DIMS)
    return [sums[:, r:r + 1] for r in range(HG)]


def _ssd_chunk_common(xc_ref, dt_ref, dtb_ref, alog_ref, h_rows, layer):
    raw, dtv, a, acs = _ssd_scalars(dt_ref, dtb_ref, alog_ref, layer)
    acs_t = acs.T
    last = acs[BLK - 1:BLK, :]
    c = dict(raw=raw, dtv=dtv, a=a, acs=acs, last=last, dte=jnp.exp(last - acs), e_all=jnp.exp(acs), cd=jnp.exp(last))
    grp, heads, tri = range(NGRP), range(NSSM), _tri()
    c["bm"] = [xc_ref[:, pl.ds(D_SSM + NSTATE * g, NSTATE)] for g in grp]
    c["bm_b"] = [c["bm"][g].astype(bf16) for g in grp]
    c["cm_b"] = [xc_ref[:, pl.ds(D_SSM + NGRP * NSTATE + NSTATE * g, NSTATE)].astype(bf16) for g in grp]
    c["cb"] = [_dot(c["cm_b"][g], c["bm_b"][g], NT_DIMS) for g in grp]
    c["x"] = [xc_ref[:, pl.ds(GW * g, GW)] for g in grp]
    c["dt"] = [_lane_expand(dtv, g) for g in grp]
    c["xdt"] = [c["x"][g] * c["dt"][g] for g in grp]
    c["xdt_b"] = [c["xdt"][g].astype(bf16) for g in grp]
    c["prev"] = [h_rows(g) for g in grp]
    c["prev_b"] = [c["prev"][g].astype(bf16) for g in grp]
    c["e"] = [_lane_expand(c["e_all"], g) for g in grp]
    c["y_off"] = [_dot(c["cm_b"][g], c["prev_b"][g], NT_DIMS) * c["e"][g] for g in grp]
    c["decay"] = [jnp.exp(jnp.where(tri, acs[:, h:h + 1] - acs_t[h:h + 1, :], -jnp.inf)) for h in heads]
    c["m"] = [c["cb"][h // HG] * c["decay"][h] for h in heads]
    c["m_b"] = [c["m"][h].astype(bf16) for h in heads]
    c["dte_x"] = [_lane_expand(c["dte"], g) for g in grp]
    c["xdte_b"] = [(c["xdt"][g] * c["dte_x"][g]).astype(bf16) for g in grp]
    return c


def _ssd_fwd(xact, z, dt, attn, dt_bias, a_log, d_skip, norm_g, layer):
    def body(xc_ref, z_ref, dt_ref, at_ref, dtb_ref, alog_ref, dsk_ref, ng_ref, mix_ref, hs_ref, y_ref, h_ref):
        n = pl.program_id(0)

        @pl.when(n == 0)
        def _():
            h_ref[...] = jnp.zeros_like(h_ref)

        hs_ref[...] = h_ref[...]
        c = _ssd_chunk_common(xc_ref, dt_ref, dtb_ref, alog_ref, lambda g: h_ref[pl.ds(GW * g, GW), :], layer)
        grp, heads = range(NGRP), range(NSSM)
        y_diag = [_dot(c["m_b"][h], c["xdt_b"][h // HG][:, HD * (h % HG):HD * (h % HG + 1)], NN_DIMS) for h in heads]
        new_st = [_dot(c["xdte_b"][g], c["bm_b"][g], TN_DIMS) for g in grp]
        for h in heads:
            y_ref[:, pl.ds(HD * h, HD)] = y_diag[h]
        dskip = dsk_ref[layer:layer + 1, :]
        for g in grp:
            cols = pl.ds(GW * g, GW)
            y_ref[:, cols] = y_ref[:, cols] + c["y_off"][g] + c["x"][g] * _lane_expand(dskip, g)
            h_ref[cols, :] = c["prev"][g] * _row_expand(c["cd"], g) + new_st[g]
        zv = z_ref[...]
        yz = y_ref[...] * (zv * _sigmoid(zv))
        mix_ref[:, 0:D_ATTN] = at_ref[...]
        for g in grp:
            yg = yz[:, GW * g:GW * (g + 1)]
            rs = lax.rsqrt(jnp.mean(yg * yg, axis=-1, keepdims=True) + EPS)
            mix_ref[:, D_ATTN + GW * g:D_ATTN + GW * (g + 1)] = (yg * rs * ng_ref[layer:layer + 1, GW * g:GW * (g + 1)]).astype(bf16)

    small = lambda shape: pl.BlockSpec(shape, lambda n: (0,) * len(shape))
    return pl.pallas_call(
        body, grid=(NBLK,),
        in_specs=[pl.BlockSpec((BLK, D_CONV), lambda n: (n, 0)), pl.BlockSpec((BLK, D_SSM), lambda n: (n, 0)),
                  pl.BlockSpec((BLK, 128), lambda n: (n, 0)), pl.BlockSpec((BLK, D_ATTN), lambda n: (n, 0)),
                  small((DEPTH, NSSM)), small((DEPTH, NSSM)), small((DEPTH, NSSM)), small((DEPTH, D_SSM))],
        out_specs=[pl.BlockSpec((BLK, D), lambda n: (n, 0)), pl.BlockSpec((None, NSSM * HD, NSTATE), lambda n: (n, 0, 0)),
                   pl.BlockSpec((BLK, D_SSM), lambda n: (n, 0))],
        out_shape=_out_hbm([SDS((S, D), bf16), SDS((NBLK, NSSM * HD, NSTATE), f32), SDS((S, D_SSM), f32)]),
        scratch_shapes=[pltpu.VMEM((NSSM * HD, NSTATE), f32)],
        name="ssd_fwd", compiler_params=_cparams(1),
    )(*_in_hbm([xact, z, dt, attn, dt_bias, a_log, d_skip, norm_g]))


def _ssd_bwd(xact, z, dt, dmix, hs, y, dt_bias, a_log, d_skip, norm_g, dproj, layer):
    def body(xc_ref, z_ref, dt_ref, do_ref, hs_ref, y_ref, dtb_ref, alog_ref, dsk_ref, ng_ref, dproj_in,
             dzdt_ref, dx_ref, dsm_ref, dh_ref, dy_ref):
        i = pl.program_id(0)

        @pl.when(i == 0)
        def _():
            dh_ref[...] = jnp.zeros_like(dh_ref)
            dsm_ref[...] = jnp.zeros_like(dsm_ref)

        c = _ssd_chunk_common(xc_ref, dt_ref, dtb_ref, alog_ref, lambda g: hs_ref[pl.ds(GW * g, GW), :], layer)
        raw, dtv, a = c["raw"], c["dtv"], c["a"]
        grp, heads = range(NGRP), range(NSSM)
        dskip = dsk_ref[layer:layer + 1, :]
        lane8 = lax.broadcasted_iota(jnp.int32, (1, NSSM), 1)
        sub8 = lax.broadcasted_iota(jnp.int32, (NSSM, 1), 0)

        zv = z_ref[...]
        sz = _sigmoid(zv)
        gz = zv * sz
        yv = y_ref[...]
        yz = yv * gz
        for g in grp:
            sl = slice(GW * g, GW * (g + 1))
            yg = yz[:, sl]
            rs = lax.rsqrt(jnp.mean(yg * yg, axis=-1, keepdims=True) + EPS)
            yhat = yg * rs
            dog = do_ref[:, sl]
            w = dog * ng_ref[layer:layer + 1, sl]
            dyz = rs * (w - yhat * jnp.mean(yhat * w, axis=-1, keepdims=True))
            dsm_ref[0:1, sl] += jnp.sum(dog * yhat, axis=0, keepdims=True)
            dy_ref[:, sl] = dyz * gz[:, sl]
            dzdt_ref[:, sl] = (dyz * yv[:, sl] * (sz[:, sl] * (1.0 + zv[:, sl] * (1.0 - sz[:, sl])))).astype(bf16)

        dy = [dy_ref[:, pl.ds(GW * g, GW)] for g in grp]
        dy_b = [dy[g].astype(bf16) for g in grp]
        hl = lambda h: slice(HD * (h % HG), HD * (h % HG + 1))
        dt_off_b = [(dy[g] * c["e"][g]).astype(bf16) for g in grp]
        dcm = [_dot(dt_off_b[g], c["prev_b"][g], NN_DIMS) for g in grp]
        dprev = [_dot(dt_off_b[g], c["cm_b"][g], TN_DIMS) for g in grp]
        yoff_rs = [_head_rowsums(dy[g] * c["y_off"][g]) for g in grp]
        dhn = [dh_ref[pl.ds(GW * g, GW), :] for g in grp]
        dhn_b = [dhn[g].astype(bf16) for g in grp]
        dprev = [dprev[g] + dhn[g] * _row_expand(c["cd"], g) for g in grp]
        dhn_prev = [dhn[g] * c["prev"][g] for g in grp]
        u = [_dot(c["bm_b"][g], dhn_b[g], NT_DIMS) for g in grp]
        dbm = [_dot(c["xdte_b"][g], dhn_b[g], NN_DIMS) for g in grp]
        ddte_rs = [_head_rowsums(c["xdt"][g] * u[g]) for g in grp]
        dm = [_dot(dy_b[h // HG][:, hl(h)], c["xdt_b"][h // HG][:, hl(h)], NT_DIMS) for h in heads]
        dxdt_in = [_dot(c["m_b"][h], dy_b[h // HG][:, hl(h)], TN_DIMS) for h in heads]
        dseg = [dm[h] * c["m"][h] for h in heads]
        dmd = [dm[h] * c["decay"][h] for h in heads]
        for h in heads:
            dx_ref[:, pl.ds(HD * h, HD)] = dxdt_in[h]

        dacs = jnp.zeros((BLK, NSSM), f32)
        dacs_cols = jnp.zeros((NSSM, BLK), f32)
        dlast = jnp.zeros((1, NSSM), f32)
        ddtv = jnp.zeros((BLK, NSSM), f32)
        ddsk = jnp.zeros((1, NSSM), f32)
        for g in grp:
            cols = pl.ds(GW * g, GW)
            dxdt = dx_ref[:, cols] + u[g] * c["dte_x"][g]
            dx_ref[:, cols] = dy[g] * _lane_expand(dskip, g) + dxdt * c["dt"][g]
            ddtv_rs = _head_rowsums(dxdt * c["x"][g])
            ddsk_rs = _head_rowsums(dy[g] * c["x"][g])
            dcb = dmd[HG * g]
            for r in range(1, HG):
                dcb = dcb + dmd[HG * g + r]
            dcb_b = dcb.astype(bf16)
            dx_ref[:, pl.ds(D_SSM + NSTATE * g, NSTATE)] = dbm[g] + _dot(dcb_b, c["cm_b"][g], TN_DIMS)
            dx_ref[:, pl.ds(D_SSM + NGRP * NSTATE + NSTATE * g, NSTATE)] = dcm[g] + _dot(dcb_b, c["bm_b"][g], NN_DIMS)
            dh_ref[cols, :] = dprev[g]
            for r in range(HG):
                h = HG * g + r
                oh = (lane8 == h).astype(f32)
                tmp = ddte_rs[g][r] * c["dte"][:, h:h + 1]
                dacs = dacs + oh * (jnp.sum(dseg[h], axis=1, keepdims=True) + yoff_rs[g][r] - tmp)
                dacs_cols = dacs_cols + (sub8 == h).astype(f32) * jnp.sum(dseg[h], axis=0, keepdims=True)
                dlast = dlast + oh * (_sum11(dhn_prev[g][HD * r:HD * (r + 1), :]) * c["cd"][:, h:h + 1] + _sum11(tmp))
                ddtv = ddtv + oh * ddtv_rs[r]
                ddsk = ddsk + oh * _sum11(ddsk_rs[r])

        row = lax.broadcasted_iota(jnp.int32, (BLK, 1), 0)
        dacs = dacs - dacs_cols.T + jnp.where(row == BLK - 1, dlast, 0.0)
        dda = lax.dot_general(_tri().astype(f32), dacs, TN_DIMS, preferred_element_type=f32, precision=HIGHEST)
        ddtv = ddtv + dda * a
        da = jnp.sum(dda * dtv, axis=0, keepdims=True)
        draw = ddtv * _sigmoid(raw)
        dzdt_ref[:, D_SSM:] = jnp.zeros((BLK, COL_XBC - COL_DT), bf16)
        dzdt_ref[:, D_SSM:D_SSM + NSSM] = draw.astype(bf16)
        dsm_ref[1:2, 0:NSSM] += jnp.sum(draw, axis=0, keepdims=True)
        dsm_ref[2:3, 0:NSSM] += da * a
        dsm_ref[3:4, 0:NSSM] += ddsk

    rev = lambda i: NBLK - 1 - i
    small = lambda shape: pl.BlockSpec(shape, lambda i: (0,) * len(shape))
    return pl.pallas_call(
        body, grid=(NBLK,),
        in_specs=[pl.BlockSpec((BLK, D_CONV), lambda i: (rev(i), 0)), pl.BlockSpec((BLK, D_SSM), lambda i: (rev(i), 0)),
                  pl.BlockSpec((BLK, 128), lambda i: (rev(i), 0)), pl.BlockSpec((BLK, D_SSM), lambda i: (rev(i), 1)),
                  pl.BlockSpec((None, NSSM * HD, NSTATE), lambda i: (rev(i), 0, 0)), pl.BlockSpec((BLK, D_SSM), lambda i: (rev(i), 0)),
                  small((DEPTH, NSSM)), small((DEPTH, NSSM)), small((DEPTH, NSSM)), small((DEPTH, D_SSM)), ANY_SPEC],
        out_specs=[pl.BlockSpec((BLK, COL_XBC - COL_Z), lambda i: (rev(i), COL_Z // (COL_XBC - COL_Z))),
                   pl.BlockSpec((BLK, D_CONV), lambda i: (rev(i), 0)), small((8, D_SSM))],
        out_shape=_out_hbm([SDS((S, D_IN_PAD), bf16), SDS((S, D_CONV), f32), SDS((8, D_SSM), f32)]),
        scratch_shapes=[pltpu.VMEM((NSSM * HD, NSTATE), f32), pltpu.VMEM((BLK, D_SSM), f32)],
        name="ssd_bwd", input_output_aliases={10: 0}, compiler_params=_cparams(1),
    )(*_in_hbm([xact, z, dt, dmix, hs, y, dt_bias, a_log, d_skip, norm_g, dproj]))


def _my_place():
    return lax.axis_index("x"), lax.axis_index("y"), lax.axis_index("c")


def _dev_index(px, py, pc):
    return 4 * px + 2 * py + pc


def _slab2(kind, ref, idx):
    if kind == "stack":
        return ref.at[idx]
    if kind == "rows128":
        return ref.at[pl.ds(pl.multiple_of(idx * 128, 128), 128), :]
    if kind == "rows512":
        return ref.at[pl.ds(pl.multiple_of(idx * 512, 512), 512), :]
    return ref.at[:, pl.ds(pl.multiple_of(idx * 512, 512), 512)]


def _slab_shape(kind, full_shape):
    if kind == "stack":
        return tuple(full_shape[1:])
    if kind == "rows128":
        return (128, full_shape[1])
    if kind == "rows512":
        return (512, full_shape[1])
    return (full_shape[0], 512)


KIND = dict(w_in="stack", w_out="rows128", w_up="cols512", w_down="rows512", conv_w="stack")
FULL_SHAPE = dict(w_in=(N_DEV, D, D_IN // N_DEV), w_out=(D, D), w_up=(D, D_FF), w_down=(D_FF, D))
HBM_SPEC = pl.BlockSpec(memory_space=pltpu.HBM)
SEM_SPEC = pl.BlockSpec(memory_space=pltpu.SEMAPHORE)
SIDE_EFFECT = pltpu.SideEffectType.DATAFLOW_SIDE_EFFECTING


def _peers_all():
    x, y, c = _my_place()
    return [(x ^ ((r >> 2) & 1), y ^ ((r >> 1) & 1), c ^ (r & 1)) for r in range(1, N_DEV)]


def _split_start(name, bufs, n_copies, plan, deps=()):
    nb = len(bufs)

    def body(*refs):
        ins = refs[:nb]
        send_sems, recv_sems = refs[nb + len(deps)], refs[nb + len(deps) + 1]
        token = refs[-1]
        for i, (src, dst, dev) in enumerate(plan(ins)):
            pltpu.make_async_remote_copy(src_ref=src, dst_ref=dst, send_sem=send_sems.at[i], recv_sem=recv_sems.at[i],
                                         device_id=dev, device_id_type=MESH).start()
        token[...] = jnp.zeros_like(token)

    outs = pl.pallas_call(
        body, name=name,
        out_shape=(pltpu.SemaphoreType.DMA((n_copies,)), pltpu.SemaphoreType.DMA((n_copies,)),
                   *[pltpu.HBM(b.shape, b.dtype) for b in bufs], SDS((8, 128), f32)),
        in_specs=[HBM_SPEC] * nb + [ANY_SPEC] * len(deps),
        out_specs=(SEM_SPEC, SEM_SPEC, *[HBM_SPEC] * nb, pl.BlockSpec(memory_space=pltpu.VMEM)),
        input_output_aliases={i: 2 + i for i in range(nb)},
        compiler_params=pltpu.CompilerParams(has_side_effects=SIDE_EFFECT),
    )(*[pltpu.with_memory_space_constraint(b, pltpu.HBM) for b in bufs], *deps)
    return dict(send=outs[0], recv=outs[1], bufs=list(outs[2:2 + nb]), token=outs[-1], plan=plan, n=n_copies)


def _split_wait(name, started, after):
    bufs = started["bufs"]
    nb = len(bufs)
    plan = started["plan"]

    def body(*refs):
        ins = refs[:nb]
        send_sems, recv_sems = refs[nb], refs[nb + 1]
        for i, (src, dst, dev) in enumerate(plan(ins)):
            cp = pltpu.make_async_remote_copy(src_ref=src, dst_ref=dst, send_sem=send_sems.at[i], recv_sem=recv_sems.at[i],
                                              device_id=dev, device_id_type=MESH)
            cp.wait_send()
            cp.wait_recv()

    outs = pl.pallas_call(
        body, name=name, out_shape=tuple(pltpu.HBM(b.shape, b.dtype) for b in bufs),
        in_specs=[HBM_SPEC] * nb + [SEM_SPEC, SEM_SPEC] + [ANY_SPEC] * len(after), out_specs=(HBM_SPEC,) * nb,
        input_output_aliases={i: i for i in range(nb)},
        compiler_params=pltpu.CompilerParams(has_side_effects=SIDE_EFFECT),
    )(*bufs, started["send"], started["recv"], *after)
    return list(outs)


def _gather_start(name, names, fulls, deps):
    n_t = len(names)

    def plan(refs):
        x, y, c = _my_place()
        my_idx = _dev_index(x, y, c)
        targets = [(x, y, 1 - c), (1 - x, y, c), (x, 1 - y, c), (1 - x, 1 - y, c)]
        slabs = [_slab2(KIND[names[t]], refs[t], my_idx) for t in range(n_t)]
        return [(slabs[t], slabs[t], dev) for t in range(n_t) for dev in targets]

    return _split_start(name, list(fulls), 4 * n_t, plan, deps)


def _gather_finish(name, names, started, after):
    n_t = len(names)
    fulls = _split_wait(name + "_wait", started, after)
    slab_shapes = [SDS(_slab_shape(KIND[n], f.shape), f.dtype) for n, f in zip(names, fulls)]

    def body(*refs):
        ins = refs[:n_t]
        outs = refs[n_t:2 * n_t]
        stage = refs[2 * n_t:3 * n_t]
        load_sems, send_sems, recv_sems = refs[3 * n_t:]
        x, y, c = _my_place()
        chips = [(1 - x, y), (x, 1 - y), (1 - x, 1 - y)]
        pairs = [(t, j) for t in range(n_t) for j in range(3)]
        loads = [pltpu.make_async_copy(_slab2(KIND[names[t]], ins[t], _dev_index(*chips[j], c)), stage[t].at[j], load_sems.at[t, j])
                 for t, j in pairs]
        for cp in loads:
            cp.start()

        def copy(t, j, core):
            return pltpu.make_async_remote_copy(
                src_ref=stage[t].at[j], dst_ref=_slab2(KIND[names[t]], outs[t], _dev_index(*chips[j], core)),
                send_sem=send_sems.at[t, j], recv_sem=recv_sems.at[t, j], device_id=(x, y, 1 - c), device_id_type=MESH)

        sends = [copy(t, j, c) for t, j in pairs]
        for ld, cp in zip(loads, sends):
            ld.wait()
            cp.start()
        for t, j in pairs:
            copy(t, j, 1 - c).wait_recv()
        for cp in sends:
            cp.wait_send()

    return pl.pallas_call(
        body, in_specs=[ANY_SPEC] * n_t, out_specs=[ANY_SPEC] * n_t, out_shape=[SDS(b.shape, b.dtype) for b in fulls],
        input_output_aliases={t: t for t in range(n_t)},
        scratch_shapes=[pltpu.VMEM((3,) + s.shape, s.dtype) for s in slab_shapes]
        + [pltpu.SemaphoreType.DMA((n_t, 3)), pltpu.SemaphoreType.DMA((n_t, 3)), pltpu.SemaphoreType.DMA((n_t, 3))],
        name=name + "_pass", compiler_params=pltpu.CompilerParams(vmem_limit_bytes=VMEM_LIMIT),
    )(*fulls)


def _exchange_start(name, names, grads, deps):
    n_t = len(names)
    lands = [lax.empty((N_DEV,) + _slab_shape(KIND[n], g.shape), g.dtype) for n, g in zip(names, grads)]

    def plan(refs):
        my_idx = _dev_index(*_my_place())
        return [(_slab2(KIND[names[t]], refs[t], _dev_index(*peer)), refs[n_t + t].at[my_idx], peer)
                for t in range(n_t) for peer in _peers_all()]

    return _split_start(name, list(grads) + lands, 7 * n_t, plan, deps)


def _small_exchange_start(part, deps):
    land = lax.empty((N_DEV,) + part.shape, part.dtype)

    def plan(refs):
        my_idx = _dev_index(*_my_place())
        return [(refs[0], refs[1].at[my_idx], peer) for peer in _peers_all()]

    return _split_start("small_exchange", [part, land], N_DEV - 1, plan, deps)


def _slab_pieces():
    sh = D_IN // N_DEV
    out = []
    for j in range(N_DEV):
        for first, end, dst in IN_SEGMENTS:
            lo, hi = max(first, sh * j), min(end, sh * (j + 1))
            if lo < hi:
                out.append((j, lo - sh * j, hi - sh * j, dst + lo - first))
    return out


def _w_in_assemble(stacked):
    tr = 256
    sh = D_IN // N_DEV

    def body(i_ref, o_ref):
        o_ref[:, COL_DT:COL_XBC] = jnp.zeros((tr, COL_XBC - COL_DT), bf16)
        for j, lo, hi, dst in _slab_pieces():
            o_ref[:, dst:dst + hi - lo] = i_ref[j, :, lo:hi]

    return pl.pallas_call(
        body, grid=(D // tr,), in_specs=[pl.BlockSpec((N_DEV, tr, sh), lambda i: (0, i, 0))],
        out_specs=pl.BlockSpec((None, tr, D_IN_PAD), lambda i: (0, i, 0)), out_shape=_out_hbm(SDS((1, D, D_IN_PAD), bf16)),
        name="w_in_assemble", compiler_params=_cparams(1),
    )(*_in_hbm([stacked]))


def _w_in_slabs(dw_in):
    tr = 256
    sh = D_IN // N_DEV

    def body(i_ref, o_ref):
        for j, lo, hi, src in _slab_pieces():
            o_ref[j, :, lo:hi] = i_ref[:, src:src + hi - lo]

    return pl.pallas_call(
        body, grid=(D // tr,), in_specs=[pl.BlockSpec((tr, D_IN_PAD), lambda i: (i, 0))],
        out_specs=pl.BlockSpec((N_DEV, tr, sh), lambda i: (0, i, 0)), out_shape=_out_hbm(SDS((N_DEV, D, sh), bf16)),
        name="w_in_slabs", compiler_params=_cparams(1),
    )(*_in_hbm([dw_in]))


SMALL_NAMES = ("mix_norm_g", "mlp_norm_g", "conv_b", "ssm_norm_g", "q_gain", "k_gain", "sinks", "dt_bias", "a_log", "d_skip",
               "rel_bias", "conv_w")
MISC_LANES = dict(q_gain=(LANE_QG, HD), k_gain=(LANE_KG, HD), sinks=(LANE_SINK, NQ), dt_bias=(LANE_DTB, NSSM),
                  a_log=(LANE_ALOG, NSSM), d_skip=(LANE_DSKIP, NSSM))


def _pack_small_grads(smalls, drel_t, loss):
    def body(*refs):
        o_ref = refs[-1]
        drel_ref, loss_ref = refs[-3], refs[-2]
        o_ref[...] = jnp.zeros_like(o_ref)
        for l in range(DEPTH):
            mixg, mlpg, convb, convw, ssd, attn = refs[6 * l:6 * l + 6]
            o_ref[ROW_MIXG + l:ROW_MIXG + l + 1, :] = mixg[...]
            o_ref[ROW_MLPG + l:ROW_MLPG + l + 1, :] = mlpg[...]
            o_ref[ROW_CONVB + l:ROW_CONVB + l + 1, :] = convb[0:1, :]
            o_ref[ROW_SSMG + l:ROW_SSMG + l + 1, 0:D_SSM] = ssd[0:1, :]
            o_ref[ROW_CONVW + 4 * l:ROW_CONVW + 4 * l + 4, :] = convw[0:4, :]
            row = slice(ROW_MISC + l, ROW_MISC + l + 1)
            o_ref[row, LANE_QG:LANE_QG + HD] = attn[0:1, 0:HD]
            o_ref[row, LANE_KG:LANE_KG + HD] = attn[1:2, 0:HD]
            o_ref[row, LANE_SINK:LANE_SINK + NQ] = attn[2:3, 0:NQ]
            o_ref[row, LANE_DTB:LANE_DTB + NSSM] = ssd[1:2, 0:NSSM]
            o_ref[row, LANE_ALOG:LANE_ALOG + NSSM] = ssd[2:3, 0:NSSM]
            o_ref[row, LANE_DSKIP:LANE_DSKIP + NSSM] = ssd[3:4, 0:NSSM]
        o_ref[ROW_RELB:ROW_RELB + NQ, 0:N_BUCKETS] = drel_ref[...]
        o_ref[ROW_LOSS:ROW_LOSS + 1, 0:1] = loss_ref[0:1, 0:1]

    args = []
    for sm in smalls:
        args += [sm["mix_norm_g"], sm["mlp_norm_g"], sm["conv_b"], sm["conv_w"], sm["ssd"], sm["attn"]]
    args += [drel_t, loss]
    return pl.pallas_call(body, out_shape=SDS((SMALL_ROWS, D), f32), name="pack_small_grads")(*args)


def _adamw_small(part, land, w, m, v):
    n = len(SMALL_NAMES)

    def grad_of(name, g_ref):
        if name == "mix_norm_g":
            return g_ref[ROW_MIXG:ROW_MIXG + DEPTH, :]
        if name == "mlp_norm_g":
            return g_ref[ROW_MLPG:ROW_MLPG + DEPTH, :]
        if name == "conv_b":
            return g_ref[ROW_CONVB:ROW_CONVB + DEPTH, :]
        if name == "ssm_norm_g":
            return g_ref[ROW_SSMG:ROW_SSMG + DEPTH, 0:D_SSM]
        if name == "rel_bias":
            return g_ref[ROW_RELB:ROW_RELB + NQ, 0:N_BUCKETS].T
        lane, width = MISC_LANES[name]
        return g_ref[ROW_MISC:ROW_MISC + DEPTH, lane:lane + width]

    def body(part_ref, land_ref, *refs):
        ws, ms, vs = refs[:n], refs[n:2 * n], refs[2 * n:3 * n]
        loss_ref = refs[3 * n]
        outs = refs[3 * n + 1:-1]
        g_ref = refs[-1]
        me = _dev_index(*_my_place())
        for p in range(N_DEV):
            term = jnp.where(me == p, part_ref[...], land_ref[p])
            if p == 0:
                g_ref[...] = term
            else:
                g_ref[...] += term
        loss_ref[...] = g_ref[ROW_LOSS:ROW_LOSS + 1, 0:128]
        my_cols = pl.ds(pl.multiple_of(me * 128, 128), 128)
        for k, name in enumerate(SMALL_NAMES):
            g_out, d_out, m_out, v_out = outs[4 * k:4 * k + 4]
            if name == "conv_w":
                for l in range(DEPTH):
                    g = g_ref[ROW_CONVW + 4 * l:ROW_CONVW + 4 * l + 4, my_cols]
                    delta, m_new, v_new = _adamw_math(ws[k][l], ms[k][l], vs[k][l], g)
                    g_out[l], d_out[l], m_out[l], v_out[l] = g, delta, m_new, v_new
            else:
                g = grad_of(name, g_ref)
                delta, m_new, v_new = _adamw_math(ws[k][...], ms[k][...], vs[k][...], g)
                g_out[...], d_out[...], m_out[...], v_out[...] = g, delta, m_new, v_new

    ws = [w[name] for name in SMALL_NAMES]
    out_shape = [SDS((1, 128), f32)]
    for a in ws:
        out_shape += [SDS(a.shape, f32)] * 4
    return pl.pallas_call(body, out_shape=out_shape, name="adamw_small", scratch_shapes=[pltpu.VMEM((SMALL_ROWS, D), f32)])(
        part, land, *ws, *[m[name] for name in SMALL_NAMES], *[v[name] for name in SMALL_NAMES])


def _plain(tm, tn):
    return pl.BlockSpec((tm, tn), lambda i, j, k: (i, j))


def _rowblk(tm, width):
    return pl.BlockSpec((tm, width), lambda i, j, k: (i, 0))


def _store_epi(dtype):
    def epi(acc, i, j, ex, outs):
        outs[0][...] = acc.astype(dtype)
    return epi


def _rms_prologue(layer):
    def pro(a_ref, ex, outs):
        xv = a_ref[...]
        r = lax.rsqrt(jnp.mean(xv * xv, axis=-1, keepdims=True) + EPS)
        h = (xv * r * ex[0][layer:layer + 1, :]).astype(bf16)
        outs[-1][...] = h
        return h
    return pro


MLP_TM = 256
MLP_VMEM = 56 * 1024 * 1024


def _resident(shape):
    return pl.BlockSpec((None,) + shape, lambda i: (0, 0, 0), pipeline_mode=pl.Buffered(1))


def _mlp_fwd(layer, x, mix, g, w_out, w_up, w_down, tgt=None):
    tm = MLP_TM
    with_loss = tgt is not None

    def body(x_ref, mix_ref, g_ref, wo_ref, wu_ref, wd_ref, *rest):
        xm_ref, a_ref, r_ref, h_ref = rest[with_loss:with_loss + 4]
        rest = rest[:with_loss] + rest[with_loss + 1:]
        i = pl.program_id(0)
        xv = x_ref[...] + _dot(mix_ref[...], wo_ref[...], NN_DIMS)
        xm_ref[...] = xv
        h = (xv * lax.rsqrt(jnp.mean(xv * xv, axis=-1, keepdims=True) + EPS) * g_ref[layer:layer + 1, :]).astype(bf16)
        h_ref[...] = h
        r = jnp.maximum(_dot(h, wu_ref[...], NN_DIMS), 0.0)
        a = (r * r).astype(bf16)
        a_ref[...] = a
        r_ref[...] = r.astype(bf16)
        y = xv + _dot(a, wd_ref[...], NN_DIMS)
        if not with_loss:
            rest[3][...] = y
            return
        err = y - rest[0][...]
        rest[4][...] = err * (1.0 / D)
        part = 0.5 * jnp.sum(jnp.mean(err * err, axis=-1, keepdims=True), axis=0, keepdims=True)

        @pl.when(i == 0)
        def _():
            rest[5][...] = jnp.zeros_like(rest[5])

        rest[5][...] += jnp.broadcast_to(part, rest[5].shape)

    row = lambda width: pl.BlockSpec((tm, width), lambda i: (i, 0))
    in_specs = [row(D), row(D), pl.BlockSpec((DEPTH, D), lambda i: (0, 0)), _resident((D, D)), _resident((D, D_FF)),
                _resident((D_FF, D))]
    out_specs = [row(D), row(D_FF), row(D_FF), row(D), row(D)]
    out_shape = [SDS((S, D), f32), SDS((S, D_FF), bf16), SDS((S, D_FF), bf16), SDS((S, D), bf16), SDS((S, D), f32)]
    args = [x, mix, g, w_out, w_up, w_down]
    if with_loss:
        in_specs.append(row(D))
        args.append(tgt)
        out_specs.append(pl.BlockSpec((1, 128), lambda i: (0, 0)))
        out_shape.append(SDS((1, 128), f32))
    return pl.pallas_call(
        body, grid=(S // tm,), in_specs=in_specs, out_specs=out_specs, out_shape=_out_hbm(out_shape),
        name="mlp_fwd_loss" if with_loss else "mlp_fwd",
        compiler_params=pltpu.CompilerParams(dimension_semantics=("arbitrary",), vmem_limit_bytes=MLP_VMEM),
    )(*_in_hbm(args))


def _mlp_bwd_act(layer, dx_out, r_act, x_mid, g, w_down, w_up, w_out, deps):
    tm = MLP_TM

    def body(dxo_ref, r_ref, xm_ref, g_ref, wd_ref, wu_ref, wo_ref, *rest):
        du_ref, dx_ref, dg_ref, dmix_ref = rest[len(deps):]
        dxo = dxo_ref[...]
        du = (_dot(dxo.astype(bf16), wd_ref[...], NT_DIMS) * (2.0 * r_ref[...].astype(f32))).astype(bf16)
        du_ref[...] = du
        dh = _dot(du, wu_ref[...], NT_DIMS)
        _rms_bwd_epilogue(layer)(dh, pl.program_id(0), 0, (xm_ref, g_ref, dxo_ref), (dx_ref, dg_ref))
        dmix_ref[...] = _dot(dx_ref[...].astype(bf16), wo_ref[...], NT_DIMS)

    row = lambda width: pl.BlockSpec((tm, width), lambda i: (i, 0))
    return pl.pallas_call(
        body, grid=(S // tm,),
        in_specs=[row(D), row(D_FF), row(D), pl.BlockSpec((DEPTH, D), lambda i: (0, 0)), _resident((D_FF, D)), _resident((D, D_FF)),
                  _resident((D, D))] + [ANY_SPEC] * len(deps),
        out_specs=[row(D_FF), row(D), pl.BlockSpec((1, D), lambda i: (0, 0)), row(D)],
        out_shape=_out_hbm([SDS((S, D_FF), bf16), SDS((S, D), f32), SDS((1, D), f32), SDS((S, D), f32)]), name="mlp_bwd_act",
        compiler_params=pltpu.CompilerParams(dimension_semantics=("arbitrary",), vmem_limit_bytes=MLP_VMEM),
    )(*_in_hbm([dx_out, r_act, x_mid, g, w_down, w_up, w_out, *deps]))


def _layer_fwd(l, x, p, get_weights, bias, tgt=None):
    wts = get_weights(l, "in", [x, bias])
    gfull = pl.BlockSpec((DEPTH, D), lambda i, j, k: (0, 0))
    tm = 256

    def inproj_epi(acc, i, j, ex, outs):
        outs[0][...] = acc[:, COL_QKV:COL_Z]
        outs[1][...] = acc[:, COL_Z:COL_DT]
        outs[2][...] = acc[:, COL_XBC:D_IN_PAD]
        outs[3][...] = acc[:, COL_DT:COL_DT + 128]

    qkv, z, xbc, dt, h1 = _matmul(
        "in_proj", "nn", x, wts["w_in"], tm=tm, tn=D_IN_PAD, tk=D, prologue=_rms_prologue(l),
        extras=(p["mix_norm_g"],), extra_specs=(gfull,),
        out_shape=[SDS((S, 768), f32), SDS((S, 512), f32), SDS((S, 1024), f32), SDS((S, 128), f32), SDS((S, D), bf16)],
        out_specs=[_rowblk(tm, 768), _rowblk(tm, 512), _rowblk(tm, 1024), _rowblk(tm, 128), _rowblk(tm, D)], epilogue=inproj_epi)
    attn = _attn_fwd(qkv, p["q_gain"], p["k_gain"], p["sinks"], bias, l)
    xact = _conv_fwd(xbc, wts["conv_w"], p["conv_b"], l)
    mix, hs, y_ssd = _ssd_fwd(xact, z, dt, attn, p["dt_bias"], p["a_log"], p["d_skip"], p["ssm_norm_g"], l)
    wts = dict(wts, **get_weights(l, "rest", [mix]))

    x_mid, a_act, r_act, h2, *result = _mlp_fwd(l, x, mix, p["mlp_norm_g"], wts["w_out"], wts["w_up"], wts["w_down"], tgt)
    saved = dict(x=x, h1=h1, qkv=qkv, z=z, xbc=xbc, dt=dt, xact=xact, mix=mix, hs=hs, y_ssd=y_ssd, x_mid=x_mid, h2=h2,
                 a=a_act, r=r_act, wts=wts)
    return (result[0] if tgt is None else tuple(result)), saved


def _layer_bwd(l, dx_out, sv, p, bias, deps, send):
    wts = sv["wts"]

    du, dx_mid, dg_mlp, dmix = _mlp_bwd_act(l, dx_out, sv["r"], sv["x_mid"], p["mlp_norm_g"], wts["w_down"], wts["w_up"],
                                            wts["w_out"], deps)
    dw_down = _matmul("dw_down", "tn", sv["a"], dx_out, tm=1024, tn=D, tk=S, out_shape=SDS((D_FF, D), bf16),
                      out_specs=_plain(1024, D), epilogue=_store_epi(bf16))
    dw_up = _matmul("dw_up", "tn", sv["h2"], du, tm=D, tn=1024, tk=S, out_shape=SDS((D, D_FF), bf16),
                    out_specs=_plain(D, 1024), epilogue=_store_epi(bf16))
    dw_out = _matmul("dw_out", "tn", sv["mix"], dx_mid, tm=D, tn=512, tk=512, out_shape=SDS((D, D), bf16),
                     out_specs=_plain(D, 512), epilogue=_store_epi(bf16))
    deps = send(l, dict(w_down=dw_down, w_up=dw_up, w_out=dw_out))
    gfull = pl.BlockSpec((DEPTH, D), lambda i, j, k: (0, 0))
    grow = pl.BlockSpec((1, D), lambda i, j, k: (0, 0))
    dproj, dbias, dsm_attn = _attn_bwd(sv["qkv"], dmix, p["q_gain"], p["k_gain"], p["sinks"], bias, l, deps)
    dproj, dxact, dsm_ssd = _ssd_bwd(sv["xact"], sv["z"], sv["dt"], dmix, sv["hs"], sv["y_ssd"], p["dt_bias"], p["a_log"],
                                     p["d_skip"], p["ssm_norm_g"], dproj, l)
    dproj, dconv_w, dconv_b = _conv_bwd(sv["xbc"], dxact, wts["conv_w"], p["conv_b"], dproj, l)
    dw_in = _matmul("dw_in", "tn", sv["h1"], dproj, tm=D, tn=640, tk=S, out_shape=SDS((D, D_IN_PAD), bf16),
                    out_specs=_plain(D, 640), epilogue=_store_epi(bf16))
    deps = send(l, dict(w_in=_w_in_slabs(dw_in)))
    dx, dg_mix = _matmul(
        "in_proj_dh", "nt", dproj, wts["w_in"], tm=256, tn=D, tk=D_IN_PAD, out_shape=[SDS((S, D), f32), SDS((1, D), f32)],
        out_specs=[_plain(256, D), grow], epilogue=_rms_bwd_epilogue(l),
        extras=(sv["x"], p["mix_norm_g"], dx_mid), extra_specs=(_plain(256, D), gfull, _plain(256, D)), deps=deps)
    small = dict(mix_norm_g=dg_mix, mlp_norm_g=dg_mlp, conv_w=dconv_w, conv_b=dconv_b, ssd=dsm_ssd, attn=dsm_attn, dbias=dbias)
    return dx, small, deps


def _local_step(x, tgt, p, get_weights, send):
    onehot_t = jnp.asarray(_onehot_buckets())
    bias = _bias_build(p["rel_bias"].T, onehot_t).reshape(NQ, BLK, 2 * BLK)
    saved = []
    h = x
    for l in range(DEPTH):
        h, sv = _layer_fwd(l, h, p, get_weights, bias, tgt if l == DEPTH - 1 else None)
        saved.append(sv)
    dx, loss = h
    smalls = [None] * DEPTH
    deps = ()
    for l in reversed(range(DEPTH)):
        dx, smalls[l], deps = _layer_bwd(l, dx, saved[l], p, bias, deps, send)
    drel_t = _bias_grad(smalls[0]["dbias"].reshape(NQ, -1), smalls[1]["dbias"].reshape(NQ, -1), onehot_t)
    return dx, _pack_small_grads(smalls, drel_t, loss)


WEIGHT_ORDER = ("mix_norm_g", "w_in", "q_gain", "k_gain", "sinks", "rel_bias", "conv_w", "conv_b", "dt_bias", "a_log", "d_skip",
                "ssm_norm_g", "w_out", "mlp_norm_g", "w_up", "w_down")


def kernel(x, mix_norm_g, w_in, q_gain, k_gain, sinks, rel_bias, conv_w, conv_b, dt_bias, a_log, d_skip, ssm_norm_g, w_out, mlp_norm_g, w_up, w_down, loss_target, m_mix_norm_g, m_w_in, m_q_gain, m_k_gain, m_sinks, m_rel_bias, m_conv_w, m_conv_b, m_dt_bias, m_a_log, m_d_skip, m_ssm_norm_g, m_w_out, m_mlp_norm_g, m_w_up, m_w_down, v_mix_norm_g, v_w_in, v_q_gain, v_k_gain, v_sinks, v_rel_bias, v_conv_w, v_conv_b, v_dt_bias, v_a_log, v_d_skip, v_ssm_norm_g, v_w_out, v_mlp_norm_g, v_w_up, v_w_down):
    w = dict(mix_norm_g=mix_norm_g, w_in=w_in, q_gain=q_gain, k_gain=k_gain, sinks=sinks, rel_bias=rel_bias, conv_w=conv_w,
             conv_b=conv_b, dt_bias=dt_bias, a_log=a_log, d_skip=d_skip, ssm_norm_g=ssm_norm_g, w_out=w_out,
             mlp_norm_g=mlp_norm_g, w_up=w_up, w_down=w_down)
    m = dict(mix_norm_g=m_mix_norm_g, w_in=m_w_in, q_gain=m_q_gain, k_gain=m_k_gain, sinks=m_sinks, rel_bias=m_rel_bias,
             conv_w=m_conv_w, conv_b=m_conv_b, dt_bias=m_dt_bias, a_log=m_a_log, d_skip=m_d_skip, ssm_norm_g=m_ssm_norm_g,
             w_out=m_w_out, mlp_norm_g=m_mlp_norm_g, w_up=m_w_up, w_down=m_w_down)
    v = dict(mix_norm_g=v_mix_norm_g, w_in=v_w_in, q_gain=v_q_gain, k_gain=v_k_gain, sinks=v_sinks, rel_bias=v_rel_bias,
             conv_w=v_conv_w, conv_b=v_conv_b, dt_bias=v_dt_bias, a_log=v_a_log, d_skip=v_d_skip, ssm_norm_g=v_ssm_norm_g,
             w_out=v_w_out, mlp_norm_g=v_mlp_norm_g, w_up=v_w_up, w_down=v_w_down)
    big = ("w_in", "w_out", "w_up", "w_down")

    my_idx = _dev_index(*_my_place()).astype(jnp.int32).reshape(1)

    fulls = {n: _cast_to_full("cast_" + n, w[n], KIND[n], FULL_SHAPE[n], my_idx, bf16) for n in big}
    conv_full = _cast_to_full("cast_conv_w", conv_w.reshape(1, DEPTH * 4, 128), "stack", (N_DEV, DEPTH * 4, 128), my_idx, f32)[0]
    rest = ["w_out", "w_up", "w_down"]
    g0 = _gather_start("gather0", ["w_in", "conv_w"], [fulls["w_in"][0], conv_full], ())
    g1 = _gather_start("gather1", rest, [fulls[n][0] for n in rest], (g0["token"],))
    g2 = _gather_start("gather2", ["w_in"], [fulls["w_in"][1]], (g1["token"],))
    g3 = _gather_start("gather3", rest, [fulls[n][1] for n in rest], (g2["token"],))
    held = {}
    flat = lambda a: a.reshape(a.shape[0] * a.shape[1], a.shape[2])
    adam_in = {n: (flat(w[n]), flat(m[n]), flat(v[n])) for n in big}

    def get_weights(l, part, after):
        if l == 0 and part == "in":
            full_in, full_conv = _gather_finish("gather0", ["w_in", "conv_w"], g0,
                                                list(after) + [g3["token"], adam_in["w_in"][1], adam_in["w_in"][2]])
            held["conv_w"] = jnp.transpose(full_conv.reshape(N_DEV, DEPTH, 4, 128), (1, 2, 0, 3)).reshape(DEPTH, 4, D_CONV)
            return dict(w_in=_w_in_assemble(full_in), conv_w=held["conv_w"])
        if part == "in":
            return dict(w_in=_w_in_assemble(_gather_finish("gather2", ["w_in"], g2, after)[0]), conv_w=held["conv_w"])
        full = _gather_finish("gather1" if l == 0 else "gather3", rest, g1 if l == 0 else g3, after)
        return {n: f[None] for n, f in zip(rest, full)}

    pending = []

    def send(l, grads):
        names = list(grads)
        started = _exchange_start("exchange%d_%s" % (l, names[0]), names, [grads[n] for n in names], ())
        pending.append((l, names, started))
        return (started["token"],)

    dx, small_part = _local_step(x.reshape(S, D), loss_target.reshape(S, D), w, get_weights, send)

    small = _small_exchange_start(small_part, ())
    tiles = dict(w_in=256, w_out=128, w_up=256, w_down=256)
    outs_of = {n: None for n in big}
    after = [dx, small["token"]]
    for l, names, started in pending:
        bufs = _split_wait("exchange%d_%s_wait" % (l, names[0]), started, after)
        for t, n in enumerate(names):
            outs_of[n] = _adamw_layer("adamw_%s%d" % (n, l), KIND[n], l, *adam_in[n],
                                      bufs[len(names) + t], bufs[t], my_idx, outs_of[n], tiles[n])
        after = [outs_of[names[-1]][0]]
    res = {n: [o.reshape(w[n].shape) for o in outs_of[n]] for n in big}
    small_part, small_land = _split_wait("small_exchange_wait", small, after)
    small_outs = _adamw_small(small_part, small_land, w, m, v)
    loss = small_outs[0][0, 0]
    for k, name in enumerate(SMALL_NAMES):
        res[name] = small_outs[1 + 4 * k:5 + 4 * k]

    result = [loss, dx.reshape(1, S, D)]
    for k in range(4):
        result += [res[name][k] for name in WEIGHT_ORDER]
    return tuple(result)
```

```python
import functools
import math

import numpy as np
import jax
import jax.numpy as jnp
from jax import lax
from jax.experimental import pallas as pl
from jax.experimental.pallas import tpu as pltpu

f32 = jnp.float32
bf16 = jnp.bfloat16
SDS = jax.ShapeDtypeStruct
MESH = pl.DeviceIdType.MESH
HIGHEST = lax.Precision.HIGHEST

S = 2048
D = 1024
DEPTH = 2
BLK = 128
NBLK = S // BLK
HD = 64
NQ = 8
NKV = 2
NSSM = 8
NGRP = 2
NSTATE = 128
D_ATTN = 512
D_SSM = 512
D_CONV = 1024
D_FF = 4096
D_IN = 2312
D_IN_PAD = 2560
COL_QKV, COL_Z, COL_DT, COL_XBC = 0, 768, 1280, 1536
IN_SEGMENTS = ((0, 1280, 0), (1280, 2304, COL_XBC), (2304, 2312, COL_DT))
N_BUCKETS = 32
EPS = 1e-6
N_DEV = 8
VMEM_LIMIT = 48 * 1024 * 1024

ADAM_LR = 0.001
ADAM_B1 = 0.9
ADAM_B2 = 0.999
ADAM_EPS = 1e-08
ADAM_WD = 0.01
ADAM_STEP = 10

NT_DIMS = (((1,), (1,)), ((), ()))
TN_DIMS = (((0,), (0,)), ((), ()))
NN_DIMS = (((1,), (0,)), ((), ()))

ROW_MIXG = 0
ROW_MLPG = 2
ROW_CONVB = 4
ROW_SSMG = 6
ROW_MISC = 8
ROW_RELB = 10
ROW_CONVW = 18
ROW_LOSS = 26
SMALL_ROWS = 32
LANE_QG, LANE_KG, LANE_SINK, LANE_DTB, LANE_ALOG, LANE_DSKIP = 0, 64, 128, 256, 384, 512


def _dot(a, b, dims):
    return lax.dot_general(a, b, dims, preferred_element_type=f32)


def _cparams(n_axes):
    return pltpu.CompilerParams(dimension_semantics=("arbitrary",) * n_axes, vmem_limit_bytes=VMEM_LIMIT)


def _sum11(v):
    return jnp.sum(jnp.sum(v, axis=1, keepdims=True), axis=0, keepdims=True)


def _sigmoid(v):
    return 1.0 / (1.0 + jnp.exp(-v))


ANY_SPEC = pl.BlockSpec(memory_space=pl.ANY)


def _in_hbm(args):
    return [pltpu.with_memory_space_constraint(a, pltpu.HBM) if a.size >= 65536 else a for a in args]


def _out_hbm(out_shape):
    one = lambda s: pltpu.HBM(s.shape, s.dtype) if math.prod(s.shape) >= 65536 else s
    return [one(s) for s in out_shape] if isinstance(out_shape, (list, tuple)) else one(out_shape)


def _matmul(name, mode, a, b, *, layer=0, tm, tn, tk, out_shape, out_specs, epilogue, extras=(), extra_specs=(), deps=(),
            prologue=None):
    extras = tuple(extras) + tuple(deps)
    extra_specs = tuple(extra_specs) + (ANY_SPEC,) * len(deps)
    if mode == "tn":
        t_dim, m_dim = a.shape
        n_dim = b.shape[1]
        grid = (m_dim // tm, n_dim // tn, t_dim // tk)
        a_spec = pl.BlockSpec((tk, tm), lambda i, j, k: (k, i))
        b_spec = pl.BlockSpec((tk, tn), lambda i, j, k: (k, j))
        dims = TN_DIMS
    elif mode == "nn":
        m_dim, k_dim = a.shape
        n_dim = b.shape[-1]
        grid = (m_dim // tm, n_dim // tn, k_dim // tk)
        a_spec = pl.BlockSpec((tm, tk), lambda i, j, k: (i, k))
        b_spec = pl.BlockSpec((None, tk, tn), lambda i, j, k: (layer, k, j))
        dims = NN_DIMS
    else:
        m_dim, k_dim = a.shape
        n_dim = b.shape[-2]
        grid = (m_dim // tm, n_dim // tn, k_dim // tk)
        a_spec = pl.BlockSpec((tm, tk), lambda i, j, k: (i, k))
        b_spec = pl.BlockSpec((None, tn, tk), lambda i, j, k: (layer, j, k))
        dims = NT_DIMS
    nk = grid[2]
    n_ex = len(extras)

    def body(a_ref, b_ref, *rest):
        ex = rest[:n_ex - len(deps)]
        outs = rest[n_ex:-1]
        acc = rest[-1]
        i = pl.program_id(0)
        j = pl.program_id(1)
        k = pl.program_id(2)
        lhs = a_ref[...].astype(bf16) if prologue is None else prologue(a_ref, ex, outs)
        part = _dot(lhs, b_ref[...].astype(bf16), dims)
        if nk == 1:
            epilogue(part, i, j, ex, outs)
        else:
            @pl.when(k == 0)
            def _():
                acc[...] = part

            @pl.when(k > 0)
            def _():
                acc[...] += part

            @pl.when(k == nk - 1)
            def _():
                epilogue(acc[...], i, j, ex, outs)

    return pl.pallas_call(
        body, grid=grid, in_specs=[a_spec, b_spec, *extra_specs], out_specs=out_specs, out_shape=_out_hbm(out_shape),
        scratch_shapes=[pltpu.VMEM((tm, tn) if nk > 1 else (8, 128), f32)], name=name, compiler_params=_cparams(3),
    )(*_in_hbm([a, b, *extras]))


def _rms_bwd_epilogue(layer):
    def epi(acc, i, j, ex, outs):
        x_ref, g_ref, dres_ref = ex
        dx_ref, dg_ref = outs
        xv = x_ref[...]
        r = lax.rsqrt(jnp.mean(xv * xv, axis=-1, keepdims=True) + EPS)
        xhat = xv * r
        w = acc * g_ref[layer:layer + 1, :]
        dx_ref[...] = dres_ref[...] + r * (w - xhat * jnp.mean(xhat * w, axis=-1, keepdims=True))
        dg = jnp.sum(acc * xhat, axis=0, keepdims=True)

        @pl.when(i == 0)
        def _():
            dg_ref[...] = dg

        @pl.when(i > 0)
        def _():
            dg_ref[...] += dg
    return epi


def _own_slab_spec(kind, tr, cols, nblk):
    if kind == "stack":
        return pl.BlockSpec((None, tr, cols), lambda i, idx: (idx[0], i, 0))
    if kind == "cols512":
        return pl.BlockSpec((tr, cols), lambda i, idx: (i, idx[0]))
    return pl.BlockSpec((tr, cols), lambda i, idx: (idx[0] * nblk + i, 0))


def _cast_to_full(name, w, kind, full_shape, my_idx, dtype):
    n_layers, rows, cols = w.shape
    tr = min(rows, 256)
    nblk = rows // tr

    def body(idx_ref, w_ref, *o_refs):
        for l in range(n_layers):
            o_refs[l][...] = w_ref[l].astype(dtype)

    grid_spec = pltpu.PrefetchScalarGridSpec(
        num_scalar_prefetch=1, grid=(nblk,), in_specs=[pl.BlockSpec((n_layers, tr, cols), lambda i, idx: (0, i, 0))],
        out_specs=[_own_slab_spec(kind, tr, cols, nblk)] * n_layers)
    return pl.pallas_call(body, grid_spec=grid_spec, out_shape=_out_hbm([SDS(full_shape, dtype)] * n_layers), name=name,
                          compiler_params=_cparams(1))(*_in_hbm([my_idx, w]))


def _adamw_math(w, m, v, g):
    m_new = ADAM_B1 * m + (1.0 - ADAM_B1) * g
    v_new = ADAM_B2 * v + (1.0 - ADAM_B2) * (g * g)
    m_hat = m_new / (1.0 - ADAM_B1 ** ADAM_STEP)
    v_hat = v_new / (1.0 - ADAM_B2 ** ADAM_STEP)
    delta = -ADAM_LR * (m_hat / (jnp.sqrt(v_hat) + ADAM_EPS) + ADAM_WD * w)
    return delta, m_new, v_new


def _adamw_layer(name, kind, layer, w, m, v, land, g_full, my_idx, prev, tr):
    rows2, cols = w.shape
    rows = rows2 // DEPTH
    nblk = rows // tr
    own_spec = _own_slab_spec(kind, tr, cols, nblk)
    n_prev = 0 if prev is None else 4

    def body(idx_ref, w_ref, m_ref, v_ref, land_ref, own_ref, *rest):
        g_ref, d_ref, mo_ref, vo_ref = rest[n_prev:]
        me = idx_ref[0]
        g = None
        for p in range(N_DEV):
            part = jnp.where(me == p, own_ref[...], land_ref[p]).astype(f32)
            g = part if g is None else g + part
        delta, m_new, v_new = _adamw_math(w_ref[...], m_ref[...], v_ref[...], g)
        g_ref[...] = g
        d_ref[...] = delta
        mo_ref[...] = m_new
        vo_ref[...] = v_new

    blk = pl.BlockSpec((tr, cols), lambda i, idx: (layer * nblk + i, 0))
    grid_spec = pltpu.PrefetchScalarGridSpec(
        num_scalar_prefetch=1, grid=(nblk,),
        in_specs=[blk, blk, blk, pl.BlockSpec((N_DEV, tr, cols), lambda i, idx: (0, i, 0)), own_spec] + [ANY_SPEC] * n_prev,
        out_specs=[blk, blk, blk, blk])
    aliases = {} if prev is None else {6 + k: k for k in range(4)}
    return pl.pallas_call(
        body, grid_spec=grid_spec, out_shape=_out_hbm([SDS((rows2, cols), f32)] * 4), name=name, input_output_aliases=aliases,
        compiler_params=_cparams(1),
    )(*_in_hbm([my_idx, w, m, v, land, g_full, *([] if prev is None else prev)]))


def _bucket_table():
    qi = np.arange(BLK)[:, None]
    kj = np.arange(2 * BLK)[None, :]
    dist = qi + BLK - kj
    dcl = np.clip(dist, 0, None)
    max_exact = N_BUCKETS // 2
    d_f = np.maximum(dcl, 1).astype(np.float32)
    large = max_exact + (np.log(d_f / np.float32(max_exact)) / np.float32(math.log(128 / max_exact))
                         * np.float32(N_BUCKETS - max_exact)).astype(np.int32)
    large = np.minimum(large, N_BUCKETS - 1)
    bucket = np.where(dcl < max_exact, dcl, large)
    in_window = (dist >= 0) & (dist < BLK)
    return bucket.astype(np.int32), in_window


def _onehot_buckets():
    bucket, _ = _bucket_table()
    oh = (bucket.reshape(-1)[None, :] == np.arange(N_BUCKETS)[:, None]).astype(np.float32)
    return oh


def _bias_build(rel_bias_t, onehot_t):
    def body(r_ref, o_ref, out_ref):
        out_ref[...] = jnp.dot(r_ref[...], o_ref[...], preferred_element_type=f32, precision=HIGHEST)

    tn = 4096
    return pl.pallas_call(
        body, grid=(BLK * 2 * BLK // tn,),
        in_specs=[pl.BlockSpec((NQ, N_BUCKETS), lambda i: (0, 0)), pl.BlockSpec((N_BUCKETS, tn), lambda i: (0, i))],
        out_specs=pl.BlockSpec((NQ, tn), lambda i: (0, i)), out_shape=SDS((NQ, BLK * 2 * BLK), f32), name="bias_build",
        compiler_params=_cparams(1),
    )(rel_bias_t, onehot_t)


def _bias_grad(dbias0, dbias1, onehot_t):
    tn = 4096
    nsteps = BLK * 2 * BLK // tn

    def body(a_ref, b_ref, o_ref, out_ref):
        part = lax.dot_general(a_ref[...] + b_ref[...], o_ref[...], NT_DIMS, preferred_element_type=f32, precision=HIGHEST)

        @pl.when(pl.program_id(0) == 0)
        def _():
            out_ref[...] = part

        @pl.when(pl.program_id(0) > 0)
        def _():
            out_ref[...] += part

    return pl.pallas_call(
        body, grid=(nsteps,),
        in_specs=[pl.BlockSpec((NQ, tn), lambda i: (0, i)), pl.BlockSpec((NQ, tn), lambda i: (0, i)),
                  pl.BlockSpec((N_BUCKETS, tn), lambda i: (0, i))],
        out_specs=pl.BlockSpec((NQ, N_BUCKETS), lambda i: (0, 0)), out_shape=SDS((NQ, N_BUCKETS), f32), name="bias_grad",
        compiler_params=_cparams(1),
    )(dbias0, dbias1, onehot_t)


def _attn_mask(n):
    qi = lax.broadcasted_iota(jnp.int32, (BLK, 2 * BLK), 0)
    kj = lax.broadcasted_iota(jnp.int32, (BLK, 2 * BLK), 1)
    dist = qi + BLK - kj
    first_key = jnp.where(n > 0, 0, BLK)
    return (dist >= 0) & (dist < BLK) & (kj >= first_key)


def _row_mean(a):
    return jnp.mean(a, axis=-1, keepdims=True)


def _head_norm(t, gain):
    r = lax.rsqrt(_row_mean(t * t) + EPS)
    that = t * r
    return that, r, that * gain


def _softmax_with_sink(s, sink):
    m = jnp.maximum(jnp.max(s, axis=-1, keepdims=True), sink)
    p = jnp.exp(s - m)
    psink = jnp.exp(sink - m)
    inv = 1.0 / (jnp.sum(p, axis=-1, keepdims=True) + psink)
    return p * inv, psink * inv


GQ = NQ // NKV


def _attn_fwd(qkv, q_gain, k_gain, sinks, bias, layer):
    def body(q_ref, kc_ref, kp_ref, vc_ref, vp_ref, qg_ref, kg_ref, sk_ref, bias_ref, o_ref):
        m = pl.program_id(0)
        qg = qg_ref[layer:layer + 1, :]
        kg = kg_ref[layer:layer + 1, :]
        grp = range(NKV)
        chains = [(b, j) for b in range(2) for j in grp]
        masks = [jnp.tile(_attn_mask(2 * m + b), (GQ, 1)) for b in range(2)]
        kblk = [[kp_ref[:, pl.ds(HD * j, HD)], kc_ref[0:BLK, pl.ds(HD * j, HD)], kc_ref[BLK:, pl.ds(HD * j, HD)]] for j in grp]
        vblk = [[vp_ref[:, pl.ds(HD * j, HD)].astype(bf16), vc_ref[0:BLK, pl.ds(HD * j, HD)].astype(bf16),
                 vc_ref[BLK:, pl.ds(HD * j, HD)].astype(bf16)] for j in grp]
        knb = [[_head_norm(kblk[j][t], kg)[2].astype(bf16) for t in range(3)] for j in grp]
        kn_b = {(b, j): jnp.concatenate([knb[j][b], knb[j][b + 1]], axis=0) for b, j in chains}
        vbs = {(b, j): jnp.concatenate([vblk[j][b], vblk[j][b + 1]], axis=0) for b, j in chains}
        rows = {}
        for b, j in chains:
            heads = [GQ * j + g for g in range(GQ)]
            rows[b, j] = (jnp.concatenate([q_ref[pl.ds(BLK * b, BLK), pl.ds(HD * h, HD)] for h in heads], axis=0),
                          jnp.concatenate([jnp.broadcast_to(sk_ref[layer:layer + 1, h:h + 1], (BLK, 1)) for h in heads], axis=0))
        qn_b = {c: _head_norm(rows[c][0], qg)[2].astype(bf16) for c in chains}
        ss = {(b, j): _dot(qn_b[b, j], kn_b[b, j], NT_DIMS) * (HD ** -0.5) + bias_ref[GQ * j:GQ * (j + 1)].reshape(GQ * BLK, 2 * BLK)
              for b, j in chains}
        ps = {(b, j): _softmax_with_sink(jnp.where(masks[b], ss[b, j], -jnp.inf), rows[b, j][1])[0] for b, j in chains}
        outs = {c: _dot(ps[c].astype(bf16), vbs[c], NN_DIMS).astype(bf16) for c in chains}
        for b, j in chains:
            for g in range(GQ):
                o_ref[pl.ds(BLK * b, BLK), pl.ds(HD * (GQ * j + g), HD)] = outs[b, j][BLK * g:BLK * (g + 1), :]

    prev = lambda m: jnp.maximum(2 * m - 1, 0)
    small = lambda shape: pl.BlockSpec(shape, lambda m: (0,) * len(shape))
    return pl.pallas_call(
        body, grid=(NBLK // 2,),
        in_specs=[pl.BlockSpec((2 * BLK, D_ATTN), lambda m: (m, 0)),
                  pl.BlockSpec((2 * BLK, 128), lambda m: (m, 4)), pl.BlockSpec((BLK, 128), lambda m: (prev(m), 4)),
                  pl.BlockSpec((2 * BLK, 128), lambda m: (m, 5)), pl.BlockSpec((BLK, 128), lambda m: (prev(m), 5)),
                  small((DEPTH, HD)), small((DEPTH, HD)), small((DEPTH, NQ)), small((NQ, BLK, 2 * BLK))],
        out_specs=pl.BlockSpec((2 * BLK, D_ATTN), lambda m: (m, 0)), out_shape=_out_hbm(SDS((S, D_ATTN), bf16)),
        name="attn_fwd", compiler_params=_cparams(1),
    )(*_in_hbm([qkv, qkv, qkv, qkv, qkv, q_gain, k_gain, sinks, bias]))


def _attn_bwd(qkv, dmix, q_gain, k_gain, sinks, bias, layer, deps=()):
    def body(q_ref, kc_ref, kp_ref, vc_ref, vp_ref, do_ref, qg_ref, kg_ref, sk_ref, bias_ref, *rest):
        dqkv_ref, dbias_ref, dsm_ref, carry = rest[len(deps):]
        i = pl.program_id(0)
        m = NBLK // 2 - 1 - i
        qg = qg_ref[layer:layer + 1, :]
        kg = kg_ref[layer:layer + 1, :]
        lane = lax.broadcasted_iota(jnp.int32, (1, 128), 1)

        @pl.when(i == 0)
        def _():
            carry[...] = jnp.zeros_like(carry)
            dbias_ref[...] = jnp.zeros_like(dbias_ref)
            dsm_ref[...] = jnp.zeros_like(dsm_ref)

        grp = range(NKV)
        chains = [(b, j) for b in range(2) for j in grp]
        masks = [jnp.tile(_attn_mask(2 * m + b), (GQ, 1)) for b in range(2)]
        kblk = [[kp_ref[:, pl.ds(HD * j, HD)], kc_ref[0:BLK, pl.ds(HD * j, HD)], kc_ref[BLK:, pl.ds(HD * j, HD)]] for j in grp]
        vblk = [[vp_ref[:, pl.ds(HD * j, HD)].astype(bf16), vc_ref[0:BLK, pl.ds(HD * j, HD)].astype(bf16),
                 vc_ref[BLK:, pl.ds(HD * j, HD)].astype(bf16)] for j in grp]
        knorm = [[_head_norm(kblk[j][t], kg) for t in range(3)] for j in grp]
        kn_b = {(b, j): jnp.concatenate([knorm[j][b][2].astype(bf16), knorm[j][b + 1][2].astype(bf16)], axis=0) for b, j in chains}
        vbs = {(b, j): jnp.concatenate([vblk[j][b], vblk[j][b + 1]], axis=0) for b, j in chains}
        rows, do_b = {}, {}
        for b, j in chains:
            heads = [GQ * j + g for g in range(GQ)]
            qrows = pl.ds(BLK * b, BLK)
            rows[b, j] = (jnp.concatenate([q_ref[qrows, pl.ds(HD * h, HD)] for h in heads], axis=0),
                          jnp.concatenate([jnp.broadcast_to(sk_ref[layer:layer + 1, h:h + 1], (BLK, 1)) for h in heads], axis=0))
            do_b[b, j] = jnp.concatenate([do_ref[qrows, pl.ds(HD * h, HD)] for h in heads], axis=0).astype(bf16)
        qnorm = {c: _head_norm(rows[c][0], qg) for c in chains}
        qn_b = {c: qnorm[c][2].astype(bf16) for c in chains}
        ss = {(b, j): _dot(qn_b[b, j], kn_b[b, j], NT_DIMS) * (HD ** -0.5) + bias_ref[GQ * j:GQ * (j + 1)].reshape(GQ * BLK, 2 * BLK)
              for b, j in chains}
        sm = {(b, j): _softmax_with_sink(jnp.where(masks[b], ss[b, j], -jnp.inf), rows[b, j][1]) for b, j in chains}
        dps = {c: _dot(do_b[c], vbs[c], NT_DIMS) for c in chains}
        deltas = {c: jnp.sum(sm[c][0] * dps[c], axis=-1, keepdims=True) for c in chains}
        dss = {c: sm[c][0] * (dps[c] - deltas[c]) for c in chains}
        ds_b = {c: (dss[c] * (HD ** -0.5)).astype(bf16) for c in chains}
        dqn = {c: _dot(ds_b[c], kn_b[c], NN_DIMS) for c in chains}
        dkn = {c: _dot(ds_b[c], qn_b[c], TN_DIMS) for c in chains}
        dvs = {c: _dot(sm[c][0].astype(bf16), do_b[c], TN_DIMS) for c in chains}
        dqg = jnp.zeros((1, HD), f32)
        dkg = jnp.zeros((1, HD), f32)
        dsink = jnp.zeros((1, 128), f32)
        for b, j in chains:
            dbias_ref[GQ * j:GQ * (j + 1)] += dss[b, j].reshape(GQ, BLK, 2 * BLK)
            dsk = sm[b, j][1] * deltas[b, j]
            for g in range(GQ):
                dsink = dsink + jnp.where(lane == GQ * j + g, -_sum11(dsk[BLK * g:BLK * (g + 1), :]), 0.0)
            qhat, rq, _ = qnorm[b, j]
            w = dqn[b, j] * qg
            dq = rq * (w - qhat * _row_mean(qhat * w))
            for g in range(GQ):
                dqkv_ref[pl.ds(BLK * b, BLK), pl.ds(HD * (GQ * j + g), HD)] = dq[BLK * g:BLK * (g + 1), :].astype(bf16)
            dqg = dqg + jnp.sum(dqn[b, j] * qhat, axis=0, keepdims=True)
        for j in grp:
            dkn_t = [dkn[0, j][:BLK, :], dkn[0, j][BLK:, :] + dkn[1, j][:BLK, :], dkn[1, j][BLK:, :]]
            dv_t = [dvs[0, j][:BLK, :], dvs[0, j][BLK:, :] + dvs[1, j][:BLK, :], dvs[1, j][BLK:, :]]
            dk_t = []
            for t in range(3):
                khat, rk, _ = knorm[j][t]
                w = dkn_t[t] * kg
                dk_t.append(rk * (w - khat * _row_mean(khat * w)))
                dkg = dkg + jnp.sum(dkn_t[t] * khat, axis=0, keepdims=True)
            kcols, vcols = pl.ds(D_ATTN + HD * j, HD), pl.ds(D_ATTN + 128 + HD * j, HD)
            dqkv_ref[BLK:, kcols] = (dk_t[2] + carry[:, pl.ds(HD * j, HD)]).astype(bf16)
            dqkv_ref[BLK:, vcols] = (dv_t[2] + carry[:, pl.ds(128 + HD * j, HD)]).astype(bf16)
            dqkv_ref[0:BLK, kcols] = dk_t[1].astype(bf16)
            dqkv_ref[0:BLK, vcols] = dv_t[1].astype(bf16)
            carry[:, pl.ds(HD * j, HD)] = dk_t[0]
            carry[:, pl.ds(128 + HD * j, HD)] = dv_t[0]
        dsm_ref[0:1, 0:HD] += dqg
        dsm_ref[1:2, 0:HD] += dkg
        dsm_ref[2:3, :] += dsink

    rev = lambda i: NBLK // 2 - 1 - i
    prev = lambda i: jnp.maximum(NBLK - 3 - 2 * i, 0)
    small = lambda shape: pl.BlockSpec(shape, lambda i: (0,) * len(shape))
    return pl.pallas_call(
        body, grid=(NBLK // 2,),
        in_specs=[pl.BlockSpec((2 * BLK, D_ATTN), lambda i: (rev(i), 0)),
                  pl.BlockSpec((2 * BLK, 128), lambda i: (rev(i), 4)), pl.BlockSpec((BLK, 128), lambda i: (prev(i), 4)),
                  pl.BlockSpec((2 * BLK, 128), lambda i: (rev(i), 5)), pl.BlockSpec((BLK, 128), lambda i: (prev(i), 5)),
                  pl.BlockSpec((2 * BLK, D_ATTN), lambda i: (rev(i), 0)),
                  small((DEPTH, HD)), small((DEPTH, HD)), small((DEPTH, NQ)), small((NQ, BLK, 2 * BLK))] + [ANY_SPEC] * len(deps),
        out_specs=[pl.BlockSpec((2 * BLK, 768), lambda i: (rev(i), COL_QKV // 768)), small((NQ, BLK, 2 * BLK)), small((8, 128))],
        out_shape=_out_hbm([SDS((S, D_IN_PAD), bf16), SDS((NQ, BLK, 2 * BLK), f32), SDS((8, 128), f32)]),
        scratch_shapes=[pltpu.VMEM((BLK, 256), f32)], name="attn_bwd", compiler_params=_cparams(1),
    )(*_in_hbm([qkv, qkv, qkv, qkv, qkv, dmix, q_gain, k_gain, sinks, bias, *deps]))


CONV_TC = 128


def _shift_down(u, s):
    if s == 0:
        return u
    rows = lax.broadcasted_iota(jnp.int32, u.shape, 0)
    return jnp.where(rows >= s, pltpu.roll(u, s, 0), 0.0)


def _shift_up(u, s):
    if s == 0:
        return u
    rows = lax.broadcasted_iota(jnp.int32, u.shape, 0)
    return jnp.where(rows < u.shape[0] - s, pltpu.roll(u, u.shape[0] - s, 0), 0.0)


def _conv_specs():
    return [pl.BlockSpec((S, CONV_TC), lambda c: (0, c)),
            pl.BlockSpec((None, 4, CONV_TC), lambda c: (0, 0, c)),
            pl.BlockSpec((DEPTH, CONV_TC), lambda c: (0, c))]


def _conv_pre(u, w_ref, b_ref, layer):
    pre = b_ref[layer:layer + 1, :] + w_ref[3:4, :] * u
    for k in range(3):
        pre = pre + w_ref[k:k + 1, :] * _shift_down(u, 3 - k)
    return pre


def _conv_fwd(xbc, conv_w, conv_b, layer):
    def body(u_ref, w_ref, b_ref, o_ref):
        pre = _conv_pre(u_ref[...], w_ref, b_ref, layer)
        o_ref[...] = pre * _sigmoid(pre)

    specs = _conv_specs()
    specs[1] = pl.BlockSpec((None, 4, CONV_TC), lambda c: (layer, 0, c))
    return pl.pallas_call(
        body, grid=(D_CONV // CONV_TC,), in_specs=specs, out_specs=pl.BlockSpec((S, CONV_TC), lambda c: (0, c)),
        out_shape=_out_hbm(SDS((S, D_CONV), f32)), name="conv_fwd", compiler_params=_cparams(1),
    )(*_in_hbm([xbc, conv_w, conv_b]))


def _conv_bwd(xbc, dact, conv_w, conv_b, dproj, layer):
    def body(u_ref, w_ref, b_ref, da_ref, dproj_in, du_ref, dw_ref, db_ref):
        u = u_ref[...]
        pre = _conv_pre(u, w_ref, b_ref, layer)
        sg = _sigmoid(pre)
        dpre = da_ref[...] * (sg * (1.0 + pre * (1.0 - sg)))
        du = w_ref[3:4, :] * dpre
        for k in range(3):
            du = du + w_ref[k:k + 1, :] * _shift_up(dpre, 3 - k)
        du_ref[...] = du.astype(bf16)
        db_ref[...] = jnp.broadcast_to(jnp.sum(dpre, axis=0, keepdims=True), db_ref.shape)
        dw_ref[...] = jnp.zeros_like(dw_ref)
        for k in range(4):
            dw_ref[k:k + 1, :] = jnp.sum(dpre * _shift_down(u, 3 - k), axis=0, keepdims=True)

    specs = _conv_specs()
    specs[1] = pl.BlockSpec((None, 4, CONV_TC), lambda c: (layer, 0, c))
    col = pl.BlockSpec((S, CONV_TC), lambda c: (0, c))
    row8 = pl.BlockSpec((8, CONV_TC), lambda c: (0, c))
    return pl.pallas_call(
        body, grid=(D_CONV // CONV_TC,), in_specs=[*specs, col, ANY_SPEC],
        out_specs=[pl.BlockSpec((S, CONV_TC), lambda c: (0, COL_XBC // CONV_TC + c)), row8, row8],
        out_shape=_out_hbm([SDS((S, D_IN_PAD), bf16), SDS((8, D_CONV), f32), SDS((8, D_CONV), f32)]), name="conv_bwd",
        input_output_aliases={4: 0}, compiler_params=_cparams(1),
    )(*_in_hbm([xbc, conv_w, conv_b, dact, dproj]))


def _tri():
    return (lax.broadcasted_iota(jnp.int32, (BLK, BLK), 0) >= lax.broadcasted_iota(jnp.int32, (BLK, BLK), 1))


def _ssd_scalars(dt_ref, dtb_ref, alog_ref, layer):
    raw = dt_ref[:, 0:NSSM] + dtb_ref[layer:layer + 1, :]
    dtv = jnp.maximum(raw, 0.0) + jnp.log(1.0 + jnp.exp(-jnp.abs(raw)))
    a = -jnp.exp(alog_ref[layer:layer + 1, :])
    acs = jnp.dot(_tri().astype(f32), dtv * a, preferred_element_type=f32, precision=HIGHEST)
    return raw, dtv, a, acs


HG = NSSM // NGRP
GW = HG * HD


def _lane_expand(cols, g):
    lane_head = lax.broadcasted_iota(jnp.int32, (1, GW), 1) // HD
    out = cols[:, HG * g + HG - 1:HG * g + HG]
    for r in range(HG - 2, -1, -1):
        out = jnp.where(lane_head == r, cols[:, HG * g + r:HG * g + r + 1], out)
    return out


def _row_expand(vals, g):
    row_head = lax.broadcasted_iota(jnp.int32, (GW, 1), 0) // HD
    out = vals[:, HG * g + HG - 1:HG * g + HG]
    for r in range(HG - 2, -1, -1):
        out = jnp.where(row_head == r, vals[:, HG * g + r:HG * g + r + 1], out)
    return out


def _head_rowsums(a, g):
    sel = (lax.broadcasted_iota(jnp.int32, (GW, NSSM), 0) // HD + HG * g == lax.broadcasted_iota(jnp.int32, (GW, NSSM), 1)).astype(bf16)
    hi = a.astype(bf16)
    lo = (a - hi.astype(f32)).astype(bf16)
    return _dot(hi, sel, NN_DIMS) + _dot(lo, sel, NN_DIMS)


def _head_blocksums(v, g):
    sel = (lax.broadcasted_iota(jnp.int32, (GW, NSSM), 0) // HD + HG * g == lax.broadcasted_iota(jnp.int32, (GW, NSSM), 1)).astype(bf16)
    hi = v.astype(bf16)
    lo = (v - hi.astype(f32)).astype(bf16)
    return _dot(hi, sel, TN_DIMS) + _dot(lo, sel, TN_DIMS)


def _ssd_chunk_common(xc_ref, dt_ref, dtb_ref, alog_ref, h_rows, layer):
    raw, dtv, a, acs = _ssd_scalars(dt_ref, dtb_ref, alog_ref, layer)
    acs_t = acs.T
    last = acs[BLK - 1:BLK, :]
    c = dict(raw=raw, dtv=dtv, a=a, acs=acs, last=last, dte=jnp.exp(last - acs), e_all=jnp.exp(acs), cd=jnp.exp(last))
    grp, heads, tri = range(NGRP), range(NSSM), _tri()
    c["bm"] = [xc_ref[:, pl.ds(D_SSM + NSTATE * g, NSTATE)] for g in grp]
    c["bm_b"] = [c["bm"][g].astype(bf16) for g in grp]
    c["cm_b"] = [xc_ref[:, pl.ds(D_SSM + NGRP * NSTATE + NSTATE * g, NSTATE)].astype(bf16) for g in grp]
    c["cb"] = [_dot(c["cm_b"][g], c["bm_b"][g], NT_DIMS) for g in grp]
    c["x"] = [xc_ref[:, pl.ds(GW * g, GW)] for g in grp]
    c["dt"] = [_lane_expand(dtv, g) for g in grp]
    c["xdt"] = [c["x"][g] * c["dt"][g] for g in grp]
    c["xdt_b"] = [c["xdt"][g].astype(bf16) for g in grp]
    c["prev"] = [h_rows(g) for g in grp]
    c["prev_b"] = [c["prev"][g].astype(bf16) for g in grp]
    c["e"] = [_lane_expand(c["e_all"], g) for g in grp]
    c["y_off"] = [_dot(c["cm_b"][g], c["prev_b"][g], NT_DIMS) * c["e"][g] for g in grp]
    c["decay"] = [jnp.exp(jnp.where(tri, acs[:, h:h + 1] - acs_t[h:h + 1, :], -jnp.inf)) for h in heads]
    c["m"] = [c["cb"][h // HG] * c["decay"][h] for h in heads]
    c["m_b"] = [c["m"][h].astype(bf16) for h in heads]
    c["dte_x"] = [_lane_expand(c["dte"], g) for g in grp]
    c["xdte_b"] = [(c["xdt"][g] * c["dte_x"][g]).astype(bf16) for g in grp]
    return c


def _ssd_fwd(xact, z, dt, attn, dt_bias, a_log, d_skip, norm_g, layer):
    def body(xc_ref, z_ref, dt_ref, at_ref, dtb_ref, alog_ref, dsk_ref, ng_ref, mix_ref, hs_ref, y_ref, h_ref):
        n = pl.program_id(0)

        @pl.when(n == 0)
        def _():
            h_ref[...] = jnp.zeros_like(h_ref)

        hs_ref[...] = h_ref[...]
        c = _ssd_chunk_common(xc_ref, dt_ref, dtb_ref, alog_ref, lambda g: h_ref[pl.ds(GW * g, GW), :], layer)
        grp, heads = range(NGRP), range(NSSM)
        y_diag = [_dot(c["m_b"][h], c["xdt_b"][h // HG][:, HD * (h % HG):HD * (h % HG + 1)], NN_DIMS) for h in heads]
        new_st = [_dot(c["xdte_b"][g], c["bm_b"][g], TN_DIMS) for g in grp]
        for h in heads:
            y_ref[:, pl.ds(HD * h, HD)] = y_diag[h]
        dskip = dsk_ref[layer:layer + 1, :]
        for g in grp:
            cols = pl.ds(GW * g, GW)
            y_ref[:, cols] = y_ref[:, cols] + c["y_off"][g] + c["x"][g] * _lane_expand(dskip, g)
            h_ref[cols, :] = c["prev"][g] * _row_expand(c["cd"], g) + new_st[g]
        zv = z_ref[...]
        yz = y_ref[...] * (zv * _sigmoid(zv))
        mix_ref[:, 0:D_ATTN] = at_ref[...]
        for g in grp:
            yg = yz[:, GW * g:GW * (g + 1)]
            rs = lax.rsqrt(jnp.mean(yg * yg, axis=-1, keepdims=True) + EPS)
            mix_ref[:, D_ATTN + GW * g:D_ATTN + GW * (g + 1)] = (yg * rs * ng_ref[layer:layer + 1, GW * g:GW * (g + 1)]).astype(bf16)

    small = lambda shape: pl.BlockSpec(shape, lambda n: (0,) * len(shape))
    return pl.pallas_call(
        body, grid=(NBLK,),
        in_specs=[pl.BlockSpec((BLK, D_CONV), lambda n: (n, 0)), pl.BlockSpec((BLK, D_SSM), lambda n: (n, 0)),
                  pl.BlockSpec((BLK, 128), lambda n: (n, 0)), pl.BlockSpec((BLK, D_ATTN), lambda n: (n, 0)),
                  small((DEPTH, NSSM)), small((DEPTH, NSSM)), small((DEPTH, NSSM)), small((DEPTH, D_SSM))],
        out_specs=[pl.BlockSpec((BLK, D), lambda n: (n, 0)), pl.BlockSpec((None, NSSM * HD, NSTATE), lambda n: (n, 0, 0)),
                   pl.BlockSpec((BLK, D_SSM), lambda n: (n, 0))],
        out_shape=_out_hbm([SDS((S, D), bf16), SDS((NBLK, NSSM * HD, NSTATE), f32), SDS((S, D_SSM), f32)]),
        scratch_shapes=[pltpu.VMEM((NSSM * HD, NSTATE), f32)],
        name="ssd_fwd", compiler_params=_cparams(1),
    )(*_in_hbm([xact, z, dt, attn, dt_bias, a_log, d_skip, norm_g]))


def _ssd_bwd(xact, z, dt, dmix, hs, y, dt_bias, a_log, d_skip, norm_g, dproj, layer):
    def body(xc_ref, z_ref, dt_ref, do_ref, hs_ref, y_ref, dtb_ref, alog_ref, dsk_ref, ng_ref, dproj_in,
             dzdt_ref, dx_ref, dsm_ref, dh_ref, dy_ref):
        i = pl.program_id(0)

        @pl.when(i == 0)
        def _():
            dh_ref[...] = jnp.zeros_like(dh_ref)
            dsm_ref[...] = jnp.zeros_like(dsm_ref)

        c = _ssd_chunk_common(xc_ref, dt_ref, dtb_ref, alog_ref, lambda g: hs_ref[pl.ds(GW * g, GW), :], layer)
        raw, dtv, a = c["raw"], c["dtv"], c["a"]
        grp, heads = range(NGRP), range(NSSM)
        dskip = dsk_ref[layer:layer + 1, :]
        lane8 = lax.broadcasted_iota(jnp.int32, (1, NSSM), 1)
        sub8 = lax.broadcasted_iota(jnp.int32, (NSSM, 1), 0)

        zv = z_ref[...]
        sz = _sigmoid(zv)
        gz = zv * sz
        yv = y_ref[...]
        yz = yv * gz
        for g in grp:
            sl = slice(GW * g, GW * (g + 1))
            yg = yz[:, sl]
            rs = lax.rsqrt(jnp.mean(yg * yg, axis=-1, keepdims=True) + EPS)
            yhat = yg * rs
            dog = do_ref[:, sl]
            w = dog * ng_ref[layer:layer + 1, sl]
            dyz = rs * (w - yhat * jnp.mean(yhat * w, axis=-1, keepdims=True))
            dsm_ref[0:1, sl] += jnp.sum(dog * yhat, axis=0, keepdims=True)
            dy_ref[:, sl] = dyz * gz[:, sl]
            dzdt_ref[:, sl] = (dyz * yv[:, sl] * (sz[:, sl] * (1.0 + zv[:, sl] * (1.0 - sz[:, sl])))).astype(bf16)

        dy = [dy_ref[:, pl.ds(GW * g, GW)] for g in grp]
        dy_b = [dy[g].astype(bf16) for g in grp]
        hl = lambda h: slice(HD * (h % HG), HD * (h % HG + 1))
        dt_off_b = [(dy[g] * c["e"][g]).astype(bf16) for g in grp]
        dcm = [_dot(dt_off_b[g], c["prev_b"][g], NN_DIMS) for g in grp]
        dprev = [_dot(dt_off_b[g], c["cm_b"][g], TN_DIMS) for g in grp]
        yoff_rs = [_head_rowsums(dy[g] * c["y_off"][g], g) for g in grp]
        dhn = [dh_ref[pl.ds(GW * g, GW), :] for g in grp]
        dhn_b = [dhn[g].astype(bf16) for g in grp]
        dprev = [dprev[g] + dhn[g] * _row_expand(c["cd"], g) for g in grp]
        dhn_prev = [dhn[g] * c["prev"][g] for g in grp]
        u = [_dot(c["bm_b"][g], dhn_b[g], NT_DIMS) for g in grp]
        dbm = [_dot(c["xdte_b"][g], dhn_b[g], NN_DIMS) for g in grp]
        ddte_rs = [_head_rowsums(c["xdt"][g] * u[g], g) for g in grp]
        dm = [_dot(dy_b[h // HG][:, hl(h)], c["xdt_b"][h // HG][:, hl(h)], NT_DIMS) for h in heads]
        dxdt_in = [_dot(c["m_b"][h], dy_b[h // HG][:, hl(h)], TN_DIMS) for h in heads]
        dseg = [dm[h] * c["m"][h] for h in heads]
        dmd = [dm[h] * c["decay"][h] for h in heads]
        for h in heads:
            dx_ref[:, pl.ds(HD * h, HD)] = dxdt_in[h]

        tmp = (ddte_rs[0] + ddte_rs[1]) * c["dte"]
        dacs = yoff_rs[0] + yoff_rs[1] - tmp
        dacs_cols = jnp.zeros((NSSM, BLK), f32)
        ddtv = jnp.zeros((BLK, NSSM), f32)
        ddsk = jnp.zeros((BLK, NSSM), f32)
        hp = jnp.zeros((1, NSSM), f32)
        for g in grp:
            cols = pl.ds(GW * g, GW)
            dxdt = dx_ref[:, cols] + u[g] * c["dte_x"][g]
            dx_ref[:, cols] = dy[g] * _lane_expand(dskip, g) + dxdt * c["dt"][g]
            ddtv = ddtv + _head_rowsums(dxdt * c["x"][g], g)
            ddsk = ddsk + _head_rowsums(dy[g] * c["x"][g], g)
            dcb = dmd[HG * g]
            for r in range(1, HG):
                dcb = dcb + dmd[HG * g + r]
            dcb_b = dcb.astype(bf16)
            dx_ref[:, pl.ds(D_SSM + NSTATE * g, NSTATE)] = dbm[g] + _dot(dcb_b, c["cm_b"][g], TN_DIMS)
            dx_ref[:, pl.ds(D_SSM + NGRP * NSTATE + NSTATE * g, NSTATE)] = dcm[g] + _dot(dcb_b, c["bm_b"][g], NN_DIMS)
            dh_ref[cols, :] = dprev[g]
            hp = hp + _head_blocksums(jnp.sum(dhn_prev[g], axis=1, keepdims=True), g)
            for r in range(HG):
                h = HG * g + r
                dacs = dacs + (lane8 == h).astype(f32) * jnp.sum(dseg[h], axis=1, keepdims=True)
                dacs_cols = dacs_cols + (sub8 == h).astype(f32) * jnp.sum(dseg[h], axis=0, keepdims=True)
        dlast = hp * c["cd"] + jnp.sum(tmp, axis=0, keepdims=True)
        ddsk = jnp.sum(ddsk, axis=0, keepdims=True)

        row = lax.broadcasted_iota(jnp.int32, (BLK, 1), 0)
        dacs = dacs - dacs_cols.T + jnp.where(row == BLK - 1, dlast, 0.0)
        dda = lax.dot_general(_tri().astype(f32), dacs, TN_DIMS, preferred_element_type=f32, precision=HIGHEST)
        ddtv = ddtv + dda * a
        da = jnp.sum(dda * dtv, axis=0, keepdims=True)
        draw = ddtv * _sigmoid(raw)
        dzdt_ref[:, D_SSM:] = jnp.zeros((BLK, COL_XBC - COL_DT), bf16)
        dzdt_ref[:, D_SSM:D_SSM + NSSM] = draw.astype(bf16)
        dsm_ref[1:2, 0:NSSM] += jnp.sum(draw, axis=0, keepdims=True)
        dsm_ref[2:3, 0:NSSM] += da * a
        dsm_ref[3:4, 0:NSSM] += ddsk

    rev = lambda i: NBLK - 1 - i
    small = lambda shape: pl.BlockSpec(shape, lambda i: (0,) * len(shape))
    return pl.pallas_call(
        body, grid=(NBLK,),
        in_specs=[pl.BlockSpec((BLK, D_CONV), lambda i: (rev(i), 0)), pl.BlockSpec((BLK, D_SSM), lambda i: (rev(i), 0)),
                  pl.BlockSpec((BLK, 128), lambda i: (rev(i), 0)), pl.BlockSpec((BLK, D_SSM), lambda i: (rev(i), 1)),
                  pl.BlockSpec((None, NSSM * HD, NSTATE), lambda i: (rev(i), 0, 0)), pl.BlockSpec((BLK, D_SSM), lambda i: (rev(i), 0)),
                  small((DEPTH, NSSM)), small((DEPTH, NSSM)), small((DEPTH, NSSM)), small((DEPTH, D_SSM)), ANY_SPEC],
        out_specs=[pl.BlockSpec((BLK, COL_XBC - COL_Z), lambda i: (rev(i), COL_Z // (COL_XBC - COL_Z))),
                   pl.BlockSpec((BLK, D_CONV), lambda i: (rev(i), 0)), small((8, D_SSM))],
        out_shape=_out_hbm([SDS((S, D_IN_PAD), bf16), SDS((S, D_CONV), f32), SDS((8, D_SSM), f32)]),
        scratch_shapes=[pltpu.VMEM((NSSM * HD, NSTATE), f32), pltpu.VMEM((BLK, D_SSM), f32)],
        name="ssd_bwd", input_output_aliases={10: 0}, compiler_params=_cparams(1),
    )(*_in_hbm([xact, z, dt, dmix, hs, y, dt_bias, a_log, d_skip, norm_g, dproj]))


def _my_place():
    return lax.axis_index("x"), lax.axis_index("y"), lax.axis_index("c")


def _dev_index(px, py, pc):
    return 4 * px + 2 * py + pc


def _slab2(kind, ref, idx):
    if kind == "stack":
        return ref.at[idx]
    if kind == "rows128":
        return ref.at[pl.ds(pl.multiple_of(idx * 128, 128), 128), :]
    if kind == "rows512":
        return ref.at[pl.ds(pl.multiple_of(idx * 512, 512), 512), :]
    return ref.at[:, pl.ds(pl.multiple_of(idx * 512, 512), 512)]


def _slab_shape(kind, full_shape):
    if kind == "stack":
        return tuple(full_shape[1:])
    if kind == "rows128":
        return (128, full_shape[1])
    if kind == "rows512":
        return (512, full_shape[1])
    return (full_shape[0], 512)


KIND = dict(w_in="stack", w_out="rows128", w_up="cols512", w_down="rows512", conv_w="stack")
FULL_SHAPE = dict(w_in=(N_DEV, D, D_IN // N_DEV), w_out=(D, D), w_up=(D, D_FF), w_down=(D_FF, D))
HBM_SPEC = pl.BlockSpec(memory_space=pltpu.HBM)
SEM_SPEC = pl.BlockSpec(memory_space=pltpu.SEMAPHORE)
SIDE_EFFECT = pltpu.SideEffectType.DATAFLOW_SIDE_EFFECTING


def _peers_all():
    x, y, c = _my_place()
    return [(x ^ ((r >> 2) & 1), y ^ ((r >> 1) & 1), c ^ (r & 1)) for r in range(1, N_DEV)]


def _split_start(name, bufs, n_copies, plan, deps=()):
    nb = len(bufs)

    def body(*refs):
        ins = refs[:nb]
        send_sems, recv_sems = refs[nb + len(deps)], refs[nb + len(deps) + 1]
        token = refs[-1]
        for i, (src, dst, dev) in enumerate(plan(ins)):
            pltpu.make_async_remote_copy(src_ref=src, dst_ref=dst, send_sem=send_sems.at[i], recv_sem=recv_sems.at[i],
                                         device_id=dev, device_id_type=MESH).start()
        token[...] = jnp.zeros_like(token)

    outs = pl.pallas_call(
        body, name=name,
        out_shape=(pltpu.SemaphoreType.DMA((n_copies,)), pltpu.SemaphoreType.DMA((n_copies,)),
                   *[pltpu.HBM(b.shape, b.dtype) for b in bufs], SDS((8, 128), f32)),
        in_specs=[HBM_SPEC] * nb + [ANY_SPEC] * len(deps),
        out_specs=(SEM_SPEC, SEM_SPEC, *[HBM_SPEC] * nb, pl.BlockSpec(memory_space=pltpu.VMEM)),
        input_output_aliases={i: 2 + i for i in range(nb)},
        compiler_params=pltpu.CompilerParams(has_side_effects=SIDE_EFFECT),
    )(*[pltpu.with_memory_space_constraint(b, pltpu.HBM) for b in bufs], *deps)
    return dict(send=outs[0], recv=outs[1], bufs=list(outs[2:2 + nb]), token=outs[-1], plan=plan, n=n_copies)


def _split_wait(name, started, after):
    bufs = started["bufs"]
    nb = len(bufs)
    plan = started["plan"]

    def body(*refs):
        ins = refs[:nb]
        send_sems, recv_sems = refs[nb], refs[nb + 1]
        for i, (src, dst, dev) in enumerate(plan(ins)):
            cp = pltpu.make_async_remote_copy(src_ref=src, dst_ref=dst, send_sem=send_sems.at[i], recv_sem=recv_sems.at[i],
                                              device_id=dev, device_id_type=MESH)
            cp.wait_send()
            cp.wait_recv()

    outs = pl.pallas_call(
        body, name=name, out_shape=tuple(pltpu.HBM(b.shape, b.dtype) for b in bufs),
        in_specs=[HBM_SPEC] * nb + [SEM_SPEC, SEM_SPEC] + [ANY_SPEC] * len(after), out_specs=(HBM_SPEC,) * nb,
        input_output_aliases={i: i for i in range(nb)},
        compiler_params=pltpu.CompilerParams(has_side_effects=SIDE_EFFECT),
    )(*bufs, started["send"], started["recv"], *after)
    return list(outs)


def _gather_start(name, names, fulls, deps):
    n_t = len(names)

    def plan(refs):
        x, y, c = _my_place()
        my_idx = _dev_index(x, y, c)
        targets = [(x, y, 1 - c), (1 - x, y, c), (x, 1 - y, c), (1 - x, 1 - y, c)]
        slabs = [_slab2(KIND[names[t]], refs[t], my_idx) for t in range(n_t)]
        return [(slabs[t], slabs[t], dev) for t in range(n_t) for dev in targets]

    return _split_start(name, list(fulls), 4 * n_t, plan, deps)


def _gather_finish(name, names, started, after):
    n_t = len(names)
    fulls = _split_wait(name + "_wait", started, after)
    slab_shapes = [SDS(_slab_shape(KIND[n], f.shape), f.dtype) for n, f in zip(names, fulls)]

    def body(*refs):
        ins = refs[:n_t]
        outs = refs[n_t:2 * n_t]
        stage = refs[2 * n_t:3 * n_t]
        load_sems, send_sems, recv_sems = refs[3 * n_t:]
        x, y, c = _my_place()
        chips = [(1 - x, y), (x, 1 - y), (1 - x, 1 - y)]
        pairs = [(t, j) for t in range(n_t) for j in range(3)]
        loads = [pltpu.make_async_copy(_slab2(KIND[names[t]], ins[t], _dev_index(*chips[j], c)), stage[t].at[j], load_sems.at[t, j])
                 for t, j in pairs]
        for cp in loads:
            cp.start()

        def copy(t, j, core):
            return pltpu.make_async_remote_copy(
                src_ref=stage[t].at[j], dst_ref=_slab2(KIND[names[t]], outs[t], _dev_index(*chips[j], core)),
                send_sem=send_sems.at[t, j], recv_sem=recv_sems.at[t, j], device_id=(x, y, 1 - c), device_id_type=MESH)

        sends = [copy(t, j, c) for t, j in pairs]
        for ld, cp in zip(loads, sends):
            ld.wait()
            cp.start()
        for t, j in pairs:
            copy(t, j, 1 - c).wait_recv()
        for cp in sends:
            cp.wait_send()

    return pl.pallas_call(
        body, in_specs=[ANY_SPEC] * n_t, out_specs=[ANY_SPEC] * n_t, out_shape=[SDS(b.shape, b.dtype) for b in fulls],
        input_output_aliases={t: t for t in range(n_t)},
        scratch_shapes=[pltpu.VMEM((3,) + s.shape, s.dtype) for s in slab_shapes]
        + [pltpu.SemaphoreType.DMA((n_t, 3)), pltpu.SemaphoreType.DMA((n_t, 3)), pltpu.SemaphoreType.DMA((n_t, 3))],
        name=name + "_pass", compiler_params=pltpu.CompilerParams(vmem_limit_bytes=VMEM_LIMIT),
    )(*fulls)


def _exchange_start(name, names, grads, deps):
    n_t = len(names)
    lands = [lax.empty((N_DEV,) + _slab_shape(KIND[n], g.shape), g.dtype) for n, g in zip(names, grads)]

    def plan(refs):
        my_idx = _dev_index(*_my_place())
        return [(_slab2(KIND[names[t]], refs[t], _dev_index(*peer)), refs[n_t + t].at[my_idx], peer)
                for t in range(n_t) for peer in _peers_all()]

    return _split_start(name, list(grads) + lands, 7 * n_t, plan, deps)


def _small_exchange_start(part, deps):
    land = lax.empty((N_DEV,) + part.shape, part.dtype)

    def plan(refs):
        my_idx = _dev_index(*_my_place())
        return [(refs[0], refs[1].at[my_idx], peer) for peer in _peers_all()]

    return _split_start("small_exchange", [part, land], N_DEV - 1, plan, deps)


def _slab_pieces():
    sh = D_IN // N_DEV
    out = []
    for j in range(N_DEV):
        for first, end, dst in IN_SEGMENTS:
            lo, hi = max(first, sh * j), min(end, sh * (j + 1))
            if lo < hi:
                out.append((j, lo - sh * j, hi - sh * j, dst + lo - first))
    return out


def _w_in_assemble(stacked):
    tr = 256
    sh = D_IN // N_DEV

    def body(i_ref, o_ref):
        o_ref[:, COL_DT:COL_XBC] = jnp.zeros((tr, COL_XBC - COL_DT), bf16)
        for j, lo, hi, dst in _slab_pieces():
            o_ref[:, dst:dst + hi - lo] = i_ref[j, :, lo:hi]

    return pl.pallas_call(
        body, grid=(D // tr,), in_specs=[pl.BlockSpec((N_DEV, tr, sh), lambda i: (0, i, 0))],
        out_specs=pl.BlockSpec((None, tr, D_IN_PAD), lambda i: (0, i, 0)), out_shape=_out_hbm(SDS((1, D, D_IN_PAD), bf16)),
        name="w_in_assemble", compiler_params=_cparams(1),
    )(*_in_hbm([stacked]))


def _w_in_slabs(dw_in):
    tr = 256
    sh = D_IN // N_DEV

    def body(i_ref, o_ref):
        for j, lo, hi, src in _slab_pieces():
            o_ref[j, :, lo:hi] = i_ref[:, src:src + hi - lo]

    return pl.pallas_call(
        body, grid=(D // tr,), in_specs=[pl.BlockSpec((tr, D_IN_PAD), lambda i: (i, 0))],
        out_specs=pl.BlockSpec((N_DEV, tr, sh), lambda i: (0, i, 0)), out_shape=_out_hbm(SDS((N_DEV, D, sh), bf16)),
        name="w_in_slabs", compiler_params=_cparams(1),
    )(*_in_hbm([dw_in]))


SMALL_NAMES = ("mix_norm_g", "mlp_norm_g", "conv_b", "ssm_norm_g", "q_gain", "k_gain", "sinks", "dt_bias", "a_log", "d_skip",
               "rel_bias", "conv_w")
MISC_LANES = dict(q_gain=(LANE_QG, HD), k_gain=(LANE_KG, HD), sinks=(LANE_SINK, NQ), dt_bias=(LANE_DTB, NSSM),
                  a_log=(LANE_ALOG, NSSM), d_skip=(LANE_DSKIP, NSSM))


def _pack_small_grads(smalls, drel_t, loss):
    def body(*refs):
        o_ref = refs[-1]
        drel_ref, loss_ref = refs[-3], refs[-2]
        o_ref[...] = jnp.zeros_like(o_ref)
        for l in range(DEPTH):
            mixg, mlpg, convb, convw, ssd, attn = refs[6 * l:6 * l + 6]
            o_ref[ROW_MIXG + l:ROW_MIXG + l + 1, :] = mixg[...]
            o_ref[ROW_MLPG + l:ROW_MLPG + l + 1, :] = mlpg[...]
            o_ref[ROW_CONVB + l:ROW_CONVB + l + 1, :] = convb[0:1, :]
            o_ref[ROW_SSMG + l:ROW_SSMG + l + 1, 0:D_SSM] = ssd[0:1, :]
            o_ref[ROW_CONVW + 4 * l:ROW_CONVW + 4 * l + 4, :] = convw[0:4, :]
            row = slice(ROW_MISC + l, ROW_MISC + l + 1)
            o_ref[row, LANE_QG:LANE_QG + HD] = attn[0:1, 0:HD]
            o_ref[row, LANE_KG:LANE_KG + HD] = attn[1:2, 0:HD]
            o_ref[row, LANE_SINK:LANE_SINK + NQ] = attn[2:3, 0:NQ]
            o_ref[row, LANE_DTB:LANE_DTB + NSSM] = ssd[1:2, 0:NSSM]
            o_ref[row, LANE_ALOG:LANE_ALOG + NSSM] = ssd[2:3, 0:NSSM]
            o_ref[row, LANE_DSKIP:LANE_DSKIP + NSSM] = ssd[3:4, 0:NSSM]
        o_ref[ROW_RELB:ROW_RELB + NQ, 0:N_BUCKETS] = drel_ref[...]
        o_ref[ROW_LOSS:ROW_LOSS + 1, 0:1] = loss_ref[0:1, 0:1]

    args = []
    for sm in smalls:
        args += [sm["mix_norm_g"], sm["mlp_norm_g"], sm["conv_b"], sm["conv_w"], sm["ssd"], sm["attn"]]
    args += [drel_t, loss]
    return pl.pallas_call(body, out_shape=SDS((SMALL_ROWS, D), f32), name="pack_small_grads")(*args)


def _adamw_small(part, land, w, m, v):
    n = len(SMALL_NAMES)

    def grad_of(name, g_ref):
        if name == "mix_norm_g":
            return g_ref[ROW_MIXG:ROW_MIXG + DEPTH, :]
        if name == "mlp_norm_g":
            return g_ref[ROW_MLPG:ROW_MLPG + DEPTH, :]
        if name == "conv_b":
            return g_ref[ROW_CONVB:ROW_CONVB + DEPTH, :]
        if name == "ssm_norm_g":
            return g_ref[ROW_SSMG:ROW_SSMG + DEPTH, 0:D_SSM]
        if name == "rel_bias":
            return g_ref[ROW_RELB:ROW_RELB + NQ, 0:N_BUCKETS].T
        lane, width = MISC_LANES[name]
        return g_ref[ROW_MISC:ROW_MISC + DEPTH, lane:lane + width]

    def body(part_ref, land_ref, *refs):
        ws, ms, vs = refs[:n], refs[n:2 * n], refs[2 * n:3 * n]
        loss_ref = refs[3 * n]
        outs = refs[3 * n + 1:-1]
        g_ref = refs[-1]
        me = _dev_index(*_my_place())
        for p in range(N_DEV):
            term = jnp.where(me == p, part_ref[...], land_ref[p])
            if p == 0:
                g_ref[...] = term
            else:
                g_ref[...] += term
        loss_ref[...] = g_ref[ROW_LOSS:ROW_LOSS + 1, 0:128]
        my_cols = pl.ds(pl.multiple_of(me * 128, 128), 128)
        for k, name in enumerate(SMALL_NAMES):
            g_out, d_out, m_out, v_out = outs[4 * k:4 * k + 4]
            if name == "conv_w":
                for l in range(DEPTH):
                    g = g_ref[ROW_CONVW + 4 * l:ROW_CONVW + 4 * l + 4, my_cols]
                    delta, m_new, v_new = _adamw_math(ws[k][l], ms[k][l], vs[k][l], g)
                    g_out[l], d_out[l], m_out[l], v_out[l] = g, delta, m_new, v_new
            else:
                g = grad_of(name, g_ref)
                delta, m_new, v_new = _adamw_math(ws[k][...], ms[k][...], vs[k][...], g)
                g_out[...], d_out[...], m_out[...], v_out[...] = g, delta, m_new, v_new

    ws = [w[name] for name in SMALL_NAMES]
    out_shape = [SDS((1, 128), f32)]
    for a in ws:
        out_shape += [SDS(a.shape, f32)] * 4
    return pl.pallas_call(body, out_shape=out_shape, name="adamw_small", scratch_shapes=[pltpu.VMEM((SMALL_ROWS, D), f32)])(
        part, land, *ws, *[m[name] for name in SMALL_NAMES], *[v[name] for name in SMALL_NAMES])


def _plain(tm, tn):
    return pl.BlockSpec((tm, tn), lambda i, j, k: (i, j))


def _rowblk(tm, width):
    return pl.BlockSpec((tm, width), lambda i, j, k: (i, 0))


def _store_epi(dtype):
    def epi(acc, i, j, ex, outs):
        outs[0][...] = acc.astype(dtype)
    return epi


def _rms_prologue(layer):
    def pro(a_ref, ex, outs):
        xv = a_ref[...]
        r = lax.rsqrt(jnp.mean(xv * xv, axis=-1, keepdims=True) + EPS)
        h = (xv * r * ex[0][layer:layer + 1, :]).astype(bf16)
        outs[-1][...] = h
        return h
    return pro


MLP_TM = 256
MLP_VMEM = 56 * 1024 * 1024


def _resident(shape):
    return pl.BlockSpec((None,) + shape, lambda i: (0, 0, 0), pipeline_mode=pl.Buffered(1))


def _mlp_fwd(layer, x, mix, g, w_out, w_up, w_down, tgt=None):
    tm = MLP_TM
    with_loss = tgt is not None

    def body(x_ref, mix_ref, g_ref, wo_ref, wu_ref, wd_ref, *rest):
        xm_ref, a_ref, r_ref, h_ref = rest[with_loss:with_loss + 4]
        rest = rest[:with_loss] + rest[with_loss + 1:]
        i = pl.program_id(0)
        xv = x_ref[...] + _dot(mix_ref[...], wo_ref[...], NN_DIMS)
        xm_ref[...] = xv
        h = (xv * lax.rsqrt(jnp.mean(xv * xv, axis=-1, keepdims=True) + EPS) * g_ref[layer:layer + 1, :]).astype(bf16)
        h_ref[...] = h
        r = jnp.maximum(_dot(h, wu_ref[...], NN_DIMS), 0.0)
        a = (r * r).astype(bf16)
        a_ref[...] = a
        r_ref[...] = r.astype(bf16)
        y = xv + _dot(a, wd_ref[...], NN_DIMS)
        if not with_loss:
            rest[3][...] = y
            return
        err = y - rest[0][...]
        rest[4][...] = err * (1.0 / D)
        part = 0.5 * jnp.sum(jnp.mean(err * err, axis=-1, keepdims=True), axis=0, keepdims=True)

        @pl.when(i == 0)
        def _():
            rest[5][...] = jnp.zeros_like(rest[5])

        rest[5][...] += jnp.broadcast_to(part, rest[5].shape)

    row = lambda width: pl.BlockSpec((tm, width), lambda i: (i, 0))
    in_specs = [row(D), row(D), pl.BlockSpec((DEPTH, D), lambda i: (0, 0)), _resident((D, D)), _resident((D, D_FF)),
                _resident((D_FF, D))]
    out_specs = [row(D), row(D_FF), row(D_FF), row(D), row(D)]
    out_shape = [SDS((S, D), f32), SDS((S, D_FF), bf16), SDS((S, D_FF), bf16), SDS((S, D), bf16), SDS((S, D), f32)]
    args = [x, mix, g, w_out, w_up, w_down]
    if with_loss:
        in_specs.append(row(D))
        args.append(tgt)
        out_specs.append(pl.BlockSpec((1, 128), lambda i: (0, 0)))
        out_shape.append(SDS((1, 128), f32))
    return pl.pallas_call(
        body, grid=(S // tm,), in_specs=in_specs, out_specs=out_specs, out_shape=_out_hbm(out_shape),
        name="mlp_fwd_loss" if with_loss else "mlp_fwd",
        compiler_params=pltpu.CompilerParams(dimension_semantics=("arbitrary",), vmem_limit_bytes=MLP_VMEM),
    )(*_in_hbm(args))


def _mlp_bwd_act(layer, dx_out, r_act, x_mid, g, w_down, w_up, w_out, deps):
    tm = MLP_TM

    def body(dxo_ref, r_ref, xm_ref, g_ref, wd_ref, wu_ref, wo_ref, *rest):
        du_ref, dx_ref, dg_ref, dmix_ref = rest[len(deps):]
        dxo = dxo_ref[...]
        du = (_dot(dxo.astype(bf16), wd_ref[...], NT_DIMS) * (2.0 * r_ref[...].astype(f32))).astype(bf16)
        du_ref[...] = du
        dh = _dot(du, wu_ref[...], NT_DIMS)
        _rms_bwd_epilogue(layer)(dh, pl.program_id(0), 0, (xm_ref, g_ref, dxo_ref), (dx_ref, dg_ref))
        dmix_ref[...] = _dot(dx_ref[...].astype(bf16), wo_ref[...], NT_DIMS)

    row = lambda width: pl.BlockSpec((tm, width), lambda i: (i, 0))
    return pl.pallas_call(
        body, grid=(S // tm,),
        in_specs=[row(D), row(D_FF), row(D), pl.BlockSpec((DEPTH, D), lambda i: (0, 0)), _resident((D_FF, D)), _resident((D, D_FF)),
                  _resident((D, D))] + [ANY_SPEC] * len(deps),
        out_specs=[row(D_FF), row(D), pl.BlockSpec((1, D), lambda i: (0, 0)), row(D)],
        out_shape=_out_hbm([SDS((S, D_FF), bf16), SDS((S, D), f32), SDS((1, D), f32), SDS((S, D), f32)]), name="mlp_bwd_act",
        compiler_params=pltpu.CompilerParams(dimension_semantics=("arbitrary",), vmem_limit_bytes=MLP_VMEM),
    )(*_in_hbm([dx_out, r_act, x_mid, g, w_down, w_up, w_out, *deps]))


def _layer_fwd(l, x, p, get_weights, bias, tgt=None):
    wts = get_weights(l, "in", [x, bias])
    gfull = pl.BlockSpec((DEPTH, D), lambda i, j, k: (0, 0))
    tm = 256

    def inproj_epi(acc, i, j, ex, outs):
        outs[0][...] = acc[:, COL_QKV:COL_Z]
        outs[1][...] = acc[:, COL_Z:COL_DT]
        outs[2][...] = acc[:, COL_XBC:D_IN_PAD]
        outs[3][...] = acc[:, COL_DT:COL_DT + 128]

    qkv, z, xbc, dt, h1 = _matmul(
        "in_proj", "nn", x, wts["w_in"], tm=tm, tn=D_IN_PAD, tk=D, prologue=_rms_prologue(l),
        extras=(p["mix_norm_g"],), extra_specs=(gfull,),
        out_shape=[SDS((S, 768), f32), SDS((S, 512), f32), SDS((S, 1024), f32), SDS((S, 128), f32), SDS((S, D), bf16)],
        out_specs=[_rowblk(tm, 768), _rowblk(tm, 512), _rowblk(tm, 1024), _rowblk(tm, 128), _rowblk(tm, D)], epilogue=inproj_epi)
    attn = _attn_fwd(qkv, p["q_gain"], p["k_gain"], p["sinks"], bias, l)
    xact = _conv_fwd(xbc, wts["conv_w"], p["conv_b"], l)
    mix, hs, y_ssd = _ssd_fwd(xact, z, dt, attn, p["dt_bias"], p["a_log"], p["d_skip"], p["ssm_norm_g"], l)
    wts = dict(wts, **get_weights(l, "rest", [mix]))

    x_mid, a_act, r_act, h2, *result = _mlp_fwd(l, x, mix, p["mlp_norm_g"], wts["w_out"], wts["w_up"], wts["w_down"], tgt)
    saved = dict(x=x, h1=h1, qkv=qkv, z=z, xbc=xbc, dt=dt, xact=xact, mix=mix, hs=hs, y_ssd=y_ssd, x_mid=x_mid, h2=h2,
                 a=a_act, r=r_act, wts=wts)
    return (result[0] if tgt is None else tuple(result)), saved


def _layer_bwd(l, dx_out, sv, p, bias, deps, send):
    wts = sv["wts"]

    dw_down = _matmul("dw_down", "tn", sv["a"], dx_out, tm=1024, tn=D, tk=S, out_shape=SDS((D_FF, D), bf16),
                      out_specs=_plain(1024, D), epilogue=_store_epi(bf16), deps=deps)
    deps = send(l, dict(w_down=dw_down))
    du, dx_mid, dg_mlp, dmix = _mlp_bwd_act(l, dx_out, sv["r"], sv["x_mid"], p["mlp_norm_g"], wts["w_down"], wts["w_up"],
                                            wts["w_out"], deps)
    dw_up = _matmul("dw_up", "tn", sv["h2"], du, tm=D, tn=1024, tk=S, out_shape=SDS((D, D_FF), bf16),
                    out_specs=_plain(D, 1024), epilogue=_store_epi(bf16))
    dw_out = _matmul("dw_out", "tn", sv["mix"], dx_mid, tm=D, tn=512, tk=512, out_shape=SDS((D, D), bf16),
                     out_specs=_plain(D, 512), epilogue=_store_epi(bf16))
    deps = send(l, dict(w_up=dw_up, w_out=dw_out))
    gfull = pl.BlockSpec((DEPTH, D), lambda i, j, k: (0, 0))
    grow = pl.BlockSpec((1, D), lambda i, j, k: (0, 0))
    dproj, dbias, dsm_attn = _attn_bwd(sv["qkv"], dmix, p["q_gain"], p["k_gain"], p["sinks"], bias, l, deps)
    dproj, dxact, dsm_ssd = _ssd_bwd(sv["xact"], sv["z"], sv["dt"], dmix, sv["hs"], sv["y_ssd"], p["dt_bias"], p["a_log"],
                                     p["d_skip"], p["ssm_norm_g"], dproj, l)
    dproj, dconv_w, dconv_b = _conv_bwd(sv["xbc"], dxact, wts["conv_w"], p["conv_b"], dproj, l)
    dw_in = _matmul("dw_in", "tn", sv["h1"], dproj, tm=D, tn=640, tk=S, out_shape=SDS((D, D_IN_PAD), bf16),
                    out_specs=_plain(D, 640), epilogue=_store_epi(bf16))
    deps = send(l, dict(w_in=_w_in_slabs(dw_in)))
    dx, dg_mix = _matmul(
        "in_proj_dh", "nt", dproj, wts["w_in"], tm=256, tn=D, tk=D_IN_PAD, out_shape=[SDS((S, D), f32), SDS((1, D), f32)],
        out_specs=[_plain(256, D), grow], epilogue=_rms_bwd_epilogue(l),
        extras=(sv["x"], p["mix_norm_g"], dx_mid), extra_specs=(_plain(256, D), gfull, _plain(256, D)), deps=deps)
    small = dict(mix_norm_g=dg_mix, mlp_norm_g=dg_mlp, conv_w=dconv_w, conv_b=dconv_b, ssd=dsm_ssd, attn=dsm_attn, dbias=dbias)
    return dx, small, deps


def _local_step(x, tgt, p, get_weights, send):
    onehot_t = jnp.asarray(_onehot_buckets())
    bias = _bias_build(p["rel_bias"].T, onehot_t).reshape(NQ, BLK, 2 * BLK)
    saved = []
    h = x
    for l in range(DEPTH):
        h, sv = _layer_fwd(l, h, p, get_weights, bias, tgt if l == DEPTH - 1 else None)
        saved.append(sv)
    dx, loss = h
    smalls = [None] * DEPTH
    deps = ()
    for l in reversed(range(DEPTH)):
        dx, smalls[l], deps = _layer_bwd(l, dx, saved[l], p, bias, deps, send)
    drel_t = _bias_grad(smalls[0]["dbias"].reshape(NQ, -1), smalls[1]["dbias"].reshape(NQ, -1), onehot_t)
    return dx, _pack_small_grads(smalls, drel_t, loss)


WEIGHT_ORDER = ("mix_norm_g", "w_in", "q_gain", "k_gain", "sinks", "rel_bias", "conv_w", "conv_b", "dt_bias", "a_log", "d_skip",
                "ssm_norm_g", "w_out", "mlp_norm_g", "w_up", "w_down")


def kernel(x, mix_norm_g, w_in, q_gain, k_gain, sinks, rel_bias, conv_w, conv_b, dt_bias, a_log, d_skip, ssm_norm_g, w_out, mlp_norm_g, w_up, w_down, loss_target, m_mix_norm_g, m_w_in, m_q_gain, m_k_gain, m_sinks, m_rel_bias, m_conv_w, m_conv_b, m_dt_bias, m_a_log, m_d_skip, m_ssm_norm_g, m_w_out, m_mlp_norm_g, m_w_up, m_w_down, v_mix_norm_g, v_w_in, v_q_gain, v_k_gain, v_sinks, v_rel_bias, v_conv_w, v_conv_b, v_dt_bias, v_a_log, v_d_skip, v_ssm_norm_g, v_w_out, v_mlp_norm_g, v_w_up, v_w_down):
    w = dict(mix_norm_g=mix_norm_g, w_in=w_in, q_gain=q_gain, k_gain=k_gain, sinks=sinks, rel_bias=rel_bias, conv_w=conv_w,
             conv_b=conv_b, dt_bias=dt_bias, a_log=a_log, d_skip=d_skip, ssm_norm_g=ssm_norm_g, w_out=w_out,
             mlp_norm_g=mlp_norm_g, w_up=w_up, w_down=w_down)
    m = dict(mix_norm_g=m_mix_norm_g, w_in=m_w_in, q_gain=m_q_gain, k_gain=m_k_gain, sinks=m_sinks, rel_bias=m_rel_bias,
             conv_w=m_conv_w, conv_b=m_conv_b, dt_bias=m_dt_bias, a_log=m_a_log, d_skip=m_d_skip, ssm_norm_g=m_ssm_norm_g,
             w_out=m_w_out, mlp_norm_g=m_mlp_norm_g, w_up=m_w_up, w_down=m_w_down)
    v = dict(mix_norm_g=v_mix_norm_g, w_in=v_w_in, q_gain=v_q_gain, k_gain=v_k_gain, sinks=v_sinks, rel_bias=v_rel_bias,
             conv_w=v_conv_w, conv_b=v_conv_b, dt_bias=v_dt_bias, a_log=v_a_log, d_skip=v_d_skip, ssm_norm_g=v_ssm_norm_g,
             w_out=v_w_out, mlp_norm_g=v_mlp_norm_g, w_up=v_w_up, w_down=v_w_down)
    big = ("w_in", "w_out", "w_up", "w_down")

    my_idx = _dev_index(*_my_place()).astype(jnp.int32).reshape(1)

    fulls = {n: _cast_to_full("cast_" + n, w[n], KIND[n], FULL_SHAPE[n], my_idx, bf16) for n in big}
    conv_full = _cast_to_full("cast_conv_w", conv_w.reshape(1, DEPTH * 4, 128), "stack", (N_DEV, DEPTH * 4, 128), my_idx, f32)[0]
    rest = ["w_out", "w_up", "w_down"]
    g0 = _gather_start("gather0", ["w_in", "conv_w"], [fulls["w_in"][0], conv_full], ())
    g1 = _gather_start("gather1", rest, [fulls[n][0] for n in rest], (g0["token"],))
    g2 = _gather_start("gather2", ["w_in"], [fulls["w_in"][1]], (g1["token"],))
    g3 = _gather_start("gather3", rest, [fulls[n][1] for n in rest], (g2["token"],))
    held = {}
    flat = lambda a: a.reshape(a.shape[0] * a.shape[1], a.shape[2])
    adam_in = {n: (flat(w[n]), flat(m[n]), flat(v[n])) for n in big}

    def get_weights(l, part, after):
        if l == 0 and part == "in":
            full_in, full_conv = _gather_finish("gather0", ["w_in", "conv_w"], g0,
                                                list(after) + [g3["token"], adam_in["w_in"][1], adam_in["w_in"][2]])
            held["conv_w"] = jnp.transpose(full_conv.reshape(N_DEV, DEPTH, 4, 128), (1, 2, 0, 3)).reshape(DEPTH, 4, D_CONV)
            return dict(w_in=_w_in_assemble(full_in), conv_w=held["conv_w"])
        if part == "in":
            return dict(w_in=_w_in_assemble(_gather_finish("gather2", ["w_in"], g2, after)[0]), conv_w=held["conv_w"])
        full = _gather_finish("gather1" if l == 0 else "gather3", rest, g1 if l == 0 else g3, after)
        return {n: f[None] for n, f in zip(rest, full)}

    pending = []

    def send(l, grads):
        names = list(grads)
        started = _exchange_start("exchange%d_%s" % (l, names[0]), names, [grads[n] for n in names], ())
        pending.append((l, names, started))
        return (started["token"],)

    dx, small_part = _local_step(x.reshape(S, D), loss_target.reshape(S, D), w, get_weights, send)

    small = _small_exchange_start(small_part, ())
    tiles = dict(w_in=256, w_out=128, w_up=256, w_down=256)
    outs_of = {n: None for n in big}
    after = [dx, small["token"]]
    for l, names, started in pending:
        bufs = _split_wait("exchange%d_%s_wait" % (l, names[0]), started, after)
        for t, n in enumerate(names):
            outs_of[n] = _adamw_layer("adamw_%s%d" % (n, l), KIND[n], l, *adam_in[n],
                                      bufs[len(names) + t], bufs[t], my_idx, outs_of[n], tiles[n])
        after = [outs_of[names[-1]][0]]
    res = {n: [o.reshape(w[n].shape) for o in outs_of[n]] for n in big}
    small_part, small_land = _split_wait("small_exchange_wait", small, after)
    small_outs = _adamw_small(small_part, small_land, w, m, v)
    loss = small_outs[0][0, 0]
    for k, name in enumerate(SMALL_NAMES):
        res[name] = small_outs[1 + 4 * k:5 + 4 * k]

    result = [loss, dx.reshape(1, S, D)]
    for k in range(4):
        result += [res[name][k] for name in WEIGHT_ORDER]
    return tuple(result)
```

```python
import functools
import math

import numpy as np
import jax
import jax.numpy as jnp
from jax import lax
from jax.experimental import pallas as pl
from jax.experimental.pallas import tpu as pltpu

f32 = jnp.float32
bf16 = jnp.bfloat16
SDS = jax.ShapeDtypeStruct
MESH = pl.DeviceIdType.MESH
HIGHEST = lax.Precision.HIGHEST

S = 2048
D = 1024
DEPTH = 2
BLK = 128
NBLK = S // BLK
HD = 64
NQ = 8
NKV = 2
NSSM = 8
NGRP = 2
NSTATE = 128
D_ATTN = 512
D_SSM = 512
D_CONV = 1024
D_FF = 4096
D_IN = 2312
D_IN_PAD = 2560
COL_QKV, COL_Z, COL_DT, COL_XBC = 0, 768, 1280, 1536
IN_SEGMENTS = ((0, 1280, 0), (1280, 2304, COL_XBC), (2304, 2312, COL_DT))
N_BUCKETS = 32
EPS = 1e-6
N_DEV = 8
VMEM_LIMIT = 48 * 1024 * 1024

ADAM_LR = 0.001
ADAM_B1 = 0.9
ADAM_B2 = 0.999
ADAM_EPS = 1e-08
ADAM_WD = 0.01
ADAM_STEP = 10

NT_DIMS = (((1,), (1,)), ((), ()))
TN_DIMS = (((0,), (0,)), ((), ()))
NN_DIMS = (((1,), (0,)), ((), ()))

ROW_MIXG = 0
ROW_MLPG = 2
ROW_CONVB = 4
ROW_SSMG = 6
ROW_MISC = 8
ROW_RELB = 10
ROW_CONVW = 18
ROW_LOSS = 26
SMALL_ROWS = 32
LANE_QG, LANE_KG, LANE_SINK, LANE_DTB, LANE_ALOG, LANE_DSKIP = 0, 64, 128, 256, 384, 512


def _dot(a, b, dims):
    return lax.dot_general(a, b, dims, preferred_element_type=f32)


def _cparams(n_axes):
    return pltpu.CompilerParams(dimension_semantics=("arbitrary",) * n_axes, vmem_limit_bytes=VMEM_LIMIT)


def _sum11(v):
    return jnp.sum(jnp.sum(v, axis=1, keepdims=True), axis=0, keepdims=True)


def _sigmoid(v):
    return 1.0 / (1.0 + jnp.exp(-v))


ANY_SPEC = pl.BlockSpec(memory_space=pl.ANY)


def _in_hbm(args):
    return [pltpu.with_memory_space_constraint(a, pltpu.HBM) if a.size >= 65536 else a for a in args]


def _out_hbm(out_shape):
    one = lambda s: pltpu.HBM(s.shape, s.dtype) if math.prod(s.shape) >= 65536 else s
    return [one(s) for s in out_shape] if isinstance(out_shape, (list, tuple)) else one(out_shape)


def _matmul(name, mode, a, b, *, layer=0, tm, tn, tk, out_shape, out_specs, epilogue, extras=(), extra_specs=(), deps=(),
            prologue=None):
    extras = tuple(extras) + tuple(deps)
    extra_specs = tuple(extra_specs) + (ANY_SPEC,) * len(deps)
    if mode == "tn":
        t_dim, m_dim = a.shape
        n_dim = b.shape[1]
        grid = (m_dim // tm, n_dim // tn, t_dim // tk)
        a_spec = pl.BlockSpec((tk, tm), lambda i, j, k: (k, i))
        b_spec = pl.BlockSpec((tk, tn), lambda i, j, k: (k, j))
        dims = TN_DIMS
    elif mode == "nn":
        m_dim, k_dim = a.shape
        n_dim = b.shape[-1]
        grid = (m_dim // tm, n_dim // tn, k_dim // tk)
        a_spec = pl.BlockSpec((tm, tk), lambda i, j, k: (i, k))
        b_spec = pl.BlockSpec((None, tk, tn), lambda i, j, k: (layer, k, j))
        dims = NN_DIMS
    else:
        m_dim, k_dim = a.shape
        n_dim = b.shape[-2]
        grid = (m_dim // tm, n_dim // tn, k_dim // tk)
        a_spec = pl.BlockSpec((tm, tk), lambda i, j, k: (i, k))
        b_spec = pl.BlockSpec((None, tn, tk), lambda i, j, k: (layer, j, k))
        dims = NT_DIMS
    nk = grid[2]
    n_ex = len(extras)

    def body(a_ref, b_ref, *rest):
        ex = rest[:n_ex - len(deps)]
        outs = rest[n_ex:-1]
        acc = rest[-1]
        i = pl.program_id(0)
        j = pl.program_id(1)
        k = pl.program_id(2)
        lhs = a_ref[...].astype(bf16) if prologue is None else prologue(a_ref, ex, outs)
        part = _dot(lhs, b_ref[...].astype(bf16), dims)
        if nk == 1:
            epilogue(part, i, j, ex, outs)
        else:
            @pl.when(k == 0)
            def _():
                acc[...] = part

            @pl.when(k > 0)
            def _():
                acc[...] += part

            @pl.when(k == nk - 1)
            def _():
                epilogue(acc[...], i, j, ex, outs)

    return pl.pallas_call(
        body, grid=grid, in_specs=[a_spec, b_spec, *extra_specs], out_specs=out_specs, out_shape=_out_hbm(out_shape),
        scratch_shapes=[pltpu.VMEM((tm, tn) if nk > 1 else (8, 128), f32)], name=name, compiler_params=_cparams(3),
    )(*_in_hbm([a]), b, *_in_hbm(extras))


def _rms_bwd_epilogue(layer):
    def epi(acc, i, j, ex, outs):
        x_ref, g_ref, dres_ref = ex
        dx_ref, dg_ref = outs
        xv = x_ref[...]
        r = lax.rsqrt(jnp.mean(xv * xv, axis=-1, keepdims=True) + EPS)
        xhat = xv * r
        w = acc * g_ref[layer:layer + 1, :]
        dx_ref[...] = dres_ref[...] + r * (w - xhat * jnp.mean(xhat * w, axis=-1, keepdims=True))
        dg = jnp.sum(acc * xhat, axis=0, keepdims=True)

        @pl.when(i == 0)
        def _():
            dg_ref[...] = dg

        @pl.when(i > 0)
        def _():
            dg_ref[...] += dg
    return epi


def _own_slab_spec(kind, tr, cols, nblk):
    if kind == "stack":
        return pl.BlockSpec((None, tr, cols), lambda i, idx: (idx[0], i, 0))
    if kind == "cols512":
        return pl.BlockSpec((tr, cols), lambda i, idx: (i, idx[0]))
    return pl.BlockSpec((tr, cols), lambda i, idx: (idx[0] * nblk + i, 0))


def _cast_to_full(name, w, kind, full_shape, my_idx, dtype):
    n_layers, rows, cols = w.shape
    tr = min(rows, 256)
    nblk = rows // tr

    def body(idx_ref, w_ref, *o_refs):
        for l in range(n_layers):
            o_refs[l][...] = w_ref[l].astype(dtype)

    grid_spec = pltpu.PrefetchScalarGridSpec(
        num_scalar_prefetch=1, grid=(nblk,), in_specs=[pl.BlockSpec((n_layers, tr, cols), lambda i, idx: (0, i, 0))],
        out_specs=[_own_slab_spec(kind, tr, cols, nblk)] * n_layers)
    return pl.pallas_call(body, grid_spec=grid_spec, out_shape=_out_hbm([SDS(full_shape, dtype)] * n_layers), name=name,
                          compiler_params=_cparams(1))(*_in_hbm([my_idx, w]))


def _adamw_math(w, m, v, g):
    m_new = ADAM_B1 * m + (1.0 - ADAM_B1) * g
    v_new = ADAM_B2 * v + (1.0 - ADAM_B2) * (g * g)
    m_hat = m_new / (1.0 - ADAM_B1 ** ADAM_STEP)
    v_hat = v_new / (1.0 - ADAM_B2 ** ADAM_STEP)
    delta = -ADAM_LR * (m_hat / (jnp.sqrt(v_hat) + ADAM_EPS) + ADAM_WD * w)
    return delta, m_new, v_new


def _adamw_layer(name, kind, layer, w, m, v, land, g_full, my_idx, prev, tr):
    rows2, cols = w.shape
    rows = rows2 // DEPTH
    nblk = rows // tr
    own_spec = _own_slab_spec(kind, tr, cols, nblk)
    n_prev = 0 if prev is None else 4

    def body(idx_ref, w_ref, m_ref, v_ref, land_ref, own_ref, *rest):
        g_ref, d_ref, mo_ref, vo_ref = rest[n_prev:]
        me = idx_ref[0]
        g = None
        for p in range(N_DEV):
            part = jnp.where(me == p, own_ref[...], land_ref[p]).astype(f32)
            g = part if g is None else g + part
        delta, m_new, v_new = _adamw_math(w_ref[...], m_ref[...], v_ref[...], g)
        g_ref[...] = g
        d_ref[...] = delta
        mo_ref[...] = m_new
        vo_ref[...] = v_new

    blk = pl.BlockSpec((tr, cols), lambda i, idx: (layer * nblk + i, 0))
    grid_spec = pltpu.PrefetchScalarGridSpec(
        num_scalar_prefetch=1, grid=(nblk,),
        in_specs=[blk, blk, blk, pl.BlockSpec((N_DEV, tr, cols), lambda i, idx: (0, i, 0)), own_spec] + [ANY_SPEC] * n_prev,
        out_specs=[blk, blk, blk, blk])
    aliases = {} if prev is None else {6 + k: k for k in range(4)}
    return pl.pallas_call(
        body, grid_spec=grid_spec, out_shape=_out_hbm([SDS((rows2, cols), f32)] * 4), name=name, input_output_aliases=aliases,
        compiler_params=_cparams(1),
    )(*_in_hbm([my_idx, w, m, v, land, g_full, *([] if prev is None else prev)]))


def _bucket_table():
    qi = np.arange(BLK)[:, None]
    kj = np.arange(2 * BLK)[None, :]
    dist = qi + BLK - kj
    dcl = np.clip(dist, 0, None)
    max_exact = N_BUCKETS // 2
    d_f = np.maximum(dcl, 1).astype(np.float32)
    large = max_exact + (np.log(d_f / np.float32(max_exact)) / np.float32(math.log(128 / max_exact))
                         * np.float32(N_BUCKETS - max_exact)).astype(np.int32)
    large = np.minimum(large, N_BUCKETS - 1)
    bucket = np.where(dcl < max_exact, dcl, large)
    in_window = (dist >= 0) & (dist < BLK)
    return bucket.astype(np.int32), in_window


def _onehot_buckets():
    bucket, _ = _bucket_table()
    oh = (bucket.reshape(-1)[None, :] == np.arange(N_BUCKETS)[:, None]).astype(np.float32)
    return oh


def _bias_build(rel_bias_t, onehot_t):
    def body(r_ref, o_ref, out_ref):
        out_ref[...] = jnp.dot(r_ref[...], o_ref[...], preferred_element_type=f32, precision=HIGHEST)

    tn = 4096
    return pl.pallas_call(
        body, grid=(BLK * 2 * BLK // tn,),
        in_specs=[pl.BlockSpec((NQ, N_BUCKETS), lambda i: (0, 0)), pl.BlockSpec((N_BUCKETS, tn), lambda i: (0, i))],
        out_specs=pl.BlockSpec((NQ, tn), lambda i: (0, i)), out_shape=SDS((NQ, BLK * 2 * BLK), f32), name="bias_build",
        compiler_params=_cparams(1),
    )(rel_bias_t, onehot_t)


def _bias_grad(dbias0, dbias1, onehot_t):
    tn = 4096
    nsteps = BLK * 2 * BLK // tn

    def body(a_ref, b_ref, o_ref, out_ref):
        part = lax.dot_general(a_ref[...] + b_ref[...], o_ref[...], NT_DIMS, preferred_element_type=f32, precision=HIGHEST)

        @pl.when(pl.program_id(0) == 0)
        def _():
            out_ref[...] = part

        @pl.when(pl.program_id(0) > 0)
        def _():
            out_ref[...] += part

    return pl.pallas_call(
        body, grid=(nsteps,),
        in_specs=[pl.BlockSpec((NQ, tn), lambda i: (0, i)), pl.BlockSpec((NQ, tn), lambda i: (0, i)),
                  pl.BlockSpec((N_BUCKETS, tn), lambda i: (0, i))],
        out_specs=pl.BlockSpec((NQ, N_BUCKETS), lambda i: (0, 0)), out_shape=SDS((NQ, N_BUCKETS), f32), name="bias_grad",
        compiler_params=_cparams(1),
    )(dbias0, dbias1, onehot_t)


def _attn_mask(n):
    qi = lax.broadcasted_iota(jnp.int32, (BLK, 2 * BLK), 0)
    kj = lax.broadcasted_iota(jnp.int32, (BLK, 2 * BLK), 1)
    dist = qi + BLK - kj
    first_key = jnp.where(n > 0, 0, BLK)
    return (dist >= 0) & (dist < BLK) & (kj >= first_key)


def _row_mean(a):
    return jnp.mean(a, axis=-1, keepdims=True)


def _head_norm(t, gain):
    r = lax.rsqrt(_row_mean(t * t) + EPS)
    that = t * r
    return that, r, that * gain


def _softmax_with_sink(s, sink):
    m = jnp.maximum(jnp.max(s, axis=-1, keepdims=True), sink)
    p = jnp.exp(s - m)
    psink = jnp.exp(sink - m)
    inv = 1.0 / (jnp.sum(p, axis=-1, keepdims=True) + psink)
    return p * inv, psink * inv


GQ = NQ // NKV


def _attn_fwd(qkv, q_gain, k_gain, sinks, bias, layer):
    def body(q_ref, kc_ref, kp_ref, vc_ref, vp_ref, qg_ref, kg_ref, sk_ref, bias_ref, o_ref):
        m = pl.program_id(0)
        qg = qg_ref[layer:layer + 1, :]
        kg = kg_ref[layer:layer + 1, :]
        grp = range(NKV)
        chains = [(b, j) for b in range(2) for j in grp]
        masks = [jnp.tile(_attn_mask(2 * m + b), (GQ, 1)) for b in range(2)]
        kblk = [[kp_ref[:, pl.ds(HD * j, HD)], kc_ref[0:BLK, pl.ds(HD * j, HD)], kc_ref[BLK:, pl.ds(HD * j, HD)]] for j in grp]
        vblk = [[vp_ref[:, pl.ds(HD * j, HD)].astype(bf16), vc_ref[0:BLK, pl.ds(HD * j, HD)].astype(bf16),
                 vc_ref[BLK:, pl.ds(HD * j, HD)].astype(bf16)] for j in grp]
        knb = [[_head_norm(kblk[j][t], kg)[2].astype(bf16) for t in range(3)] for j in grp]
        kn_b = {(b, j): jnp.concatenate([knb[j][b], knb[j][b + 1]], axis=0) for b, j in chains}
        vbs = {(b, j): jnp.concatenate([vblk[j][b], vblk[j][b + 1]], axis=0) for b, j in chains}
        rows = {}
        for b, j in chains:
            heads = [GQ * j + g for g in range(GQ)]
            rows[b, j] = (jnp.concatenate([q_ref[pl.ds(BLK * b, BLK), pl.ds(HD * h, HD)] for h in heads], axis=0),
                          jnp.concatenate([jnp.broadcast_to(sk_ref[layer:layer + 1, h:h + 1], (BLK, 1)) for h in heads], axis=0))
        qn_b = {c: _head_norm(rows[c][0], qg)[2].astype(bf16) for c in chains}
        ss = {(b, j): _dot(qn_b[b, j], kn_b[b, j], NT_DIMS) * (HD ** -0.5) + bias_ref[GQ * j:GQ * (j + 1)].reshape(GQ * BLK, 2 * BLK)
              for b, j in chains}
        ps = {(b, j): _softmax_with_sink(jnp.where(masks[b], ss[b, j], -jnp.inf), rows[b, j][1])[0] for b, j in chains}
        outs = {c: _dot(ps[c].astype(bf16), vbs[c], NN_DIMS).astype(bf16) for c in chains}
        for b, j in chains:
            for g in range(GQ):
                o_ref[pl.ds(BLK * b, BLK), pl.ds(HD * (GQ * j + g), HD)] = outs[b, j][BLK * g:BLK * (g + 1), :]

    prev = lambda m: jnp.maximum(2 * m - 1, 0)
    small = lambda shape: pl.BlockSpec(shape, lambda m: (0,) * len(shape))
    return pl.pallas_call(
        body, grid=(NBLK // 2,),
        in_specs=[pl.BlockSpec((2 * BLK, D_ATTN), lambda m: (m, 0)),
                  pl.BlockSpec((2 * BLK, 128), lambda m: (m, 4)), pl.BlockSpec((BLK, 128), lambda m: (prev(m), 4)),
                  pl.BlockSpec((2 * BLK, 128), lambda m: (m, 5)), pl.BlockSpec((BLK, 128), lambda m: (prev(m), 5)),
                  small((DEPTH, HD)), small((DEPTH, HD)), small((DEPTH, NQ)), small((NQ, BLK, 2 * BLK))],
        out_specs=pl.BlockSpec((2 * BLK, D_ATTN), lambda m: (m, 0)), out_shape=_out_hbm(SDS((S, D_ATTN), bf16)),
        name="attn_fwd", compiler_params=_cparams(1),
    )(*_in_hbm([qkv, qkv, qkv, qkv, qkv, q_gain, k_gain, sinks, bias]))


def _attn_bwd(qkv, dmix, q_gain, k_gain, sinks, bias, layer, deps=()):
    def body(q_ref, kc_ref, kp_ref, vc_ref, vp_ref, do_ref, qg_ref, kg_ref, sk_ref, bias_ref, *rest):
        dqkv_ref, dbias_ref, dsm_ref, carry = rest[len(deps):]
        i = pl.program_id(0)
        m = NBLK // 2 - 1 - i
        qg = qg_ref[layer:layer + 1, :]
        kg = kg_ref[layer:layer + 1, :]
        lane = lax.broadcasted_iota(jnp.int32, (1, 128), 1)

        @pl.when(i == 0)
        def _():
            carry[...] = jnp.zeros_like(carry)
            dbias_ref[...] = jnp.zeros_like(dbias_ref)
            dsm_ref[...] = jnp.zeros_like(dsm_ref)

        grp = range(NKV)
        chains = [(b, j) for b in range(2) for j in grp]
        masks = [jnp.tile(_attn_mask(2 * m + b), (GQ, 1)) for b in range(2)]
        kblk = [[kp_ref[:, pl.ds(HD * j, HD)], kc_ref[0:BLK, pl.ds(HD * j, HD)], kc_ref[BLK:, pl.ds(HD * j, HD)]] for j in grp]
        vblk = [[vp_ref[:, pl.ds(HD * j, HD)].astype(bf16), vc_ref[0:BLK, pl.ds(HD * j, HD)].astype(bf16),
                 vc_ref[BLK:, pl.ds(HD * j, HD)].astype(bf16)] for j in grp]
        knorm = [[_head_norm(kblk[j][t], kg) for t in range(3)] for j in grp]
        kn_b = {(b, j): jnp.concatenate([knorm[j][b][2].astype(bf16), knorm[j][b + 1][2].astype(bf16)], axis=0) for b, j in chains}
        vbs = {(b, j): jnp.concatenate([vblk[j][b], vblk[j][b + 1]], axis=0) for b, j in chains}
        rows, do_b = {}, {}
        for b, j in chains:
            heads = [GQ * j + g for g in range(GQ)]
            qrows = pl.ds(BLK * b, BLK)
            rows[b, j] = (jnp.concatenate([q_ref[qrows, pl.ds(HD * h, HD)] for h in heads], axis=0),
                          jnp.concatenate([jnp.broadcast_to(sk_ref[layer:layer + 1, h:h + 1], (BLK, 1)) for h in heads], axis=0))
            do_b[b, j] = jnp.concatenate([do_ref[qrows, pl.ds(HD * h, HD)] for h in heads], axis=0).astype(bf16)
        qnorm = {c: _head_norm(rows[c][0], qg) for c in chains}
        qn_b = {c: qnorm[c][2].astype(bf16) for c in chains}
        ss = {(b, j): _dot(qn_b[b, j], kn_b[b, j], NT_DIMS) * (HD ** -0.5) + bias_ref[GQ * j:GQ * (j + 1)].reshape(GQ * BLK, 2 * BLK)
              for b, j in chains}
        sm = {(b, j): _softmax_with_sink(jnp.where(masks[b], ss[b, j], -jnp.inf), rows[b, j][1]) for b, j in chains}
        dps = {c: _dot(do_b[c], vbs[c], NT_DIMS) for c in chains}
        deltas = {c: jnp.sum(sm[c][0] * dps[c], axis=-1, keepdims=True) for c in chains}
        dss = {c: sm[c][0] * (dps[c] - deltas[c]) for c in chains}
        ds_b = {c: (dss[c] * (HD ** -0.5)).astype(bf16) for c in chains}
        dqn = {c: _dot(ds_b[c], kn_b[c], NN_DIMS) for c in chains}
        dkn = {c: _dot(ds_b[c], qn_b[c], TN_DIMS) for c in chains}
        dvs = {c: _dot(sm[c][0].astype(bf16), do_b[c], TN_DIMS) for c in chains}
        dqg = jnp.zeros((1, HD), f32)
        dkg = jnp.zeros((1, HD), f32)
        dsink = jnp.zeros((1, 128), f32)
        for b, j in chains:
            dbias_ref[GQ * j:GQ * (j + 1)] += dss[b, j].reshape(GQ, BLK, 2 * BLK)
            dsk = sm[b, j][1] * deltas[b, j]
            for g in range(GQ):
                dsink = dsink + jnp.where(lane == GQ * j + g, -_sum11(dsk[BLK * g:BLK * (g + 1), :]), 0.0)
            qhat, rq, _ = qnorm[b, j]
            w = dqn[b, j] * qg
            dq = rq * (w - qhat * _row_mean(qhat * w))
            for g in range(GQ):
                dqkv_ref[pl.ds(BLK * b, BLK), pl.ds(HD * (GQ * j + g), HD)] = dq[BLK * g:BLK * (g + 1), :].astype(bf16)
            dqg = dqg + jnp.sum(dqn[b, j] * qhat, axis=0, keepdims=True)
        for j in grp:
            dkn_t = [dkn[0, j][:BLK, :], dkn[0, j][BLK:, :] + dkn[1, j][:BLK, :], dkn[1, j][BLK:, :]]
            dv_t = [dvs[0, j][:BLK, :], dvs[0, j][BLK:, :] + dvs[1, j][:BLK, :], dvs[1, j][BLK:, :]]
            dk_t = []
            for t in range(3):
                khat, rk, _ = knorm[j][t]
                w = dkn_t[t] * kg
                dk_t.append(rk * (w - khat * _row_mean(khat * w)))
                dkg = dkg + jnp.sum(dkn_t[t] * khat, axis=0, keepdims=True)
            kcols, vcols = pl.ds(D_ATTN + HD * j, HD), pl.ds(D_ATTN + 128 + HD * j, HD)
            dqkv_ref[BLK:, kcols] = (dk_t[2] + carry[:, pl.ds(HD * j, HD)]).astype(bf16)
            dqkv_ref[BLK:, vcols] = (dv_t[2] + carry[:, pl.ds(128 + HD * j, HD)]).astype(bf16)
            dqkv_ref[0:BLK, kcols] = dk_t[1].astype(bf16)
            dqkv_ref[0:BLK, vcols] = dv_t[1].astype(bf16)
            carry[:, pl.ds(HD * j, HD)] = dk_t[0]
            carry[:, pl.ds(128 + HD * j, HD)] = dv_t[0]
        dsm_ref[0:1, 0:HD] += dqg
        dsm_ref[1:2, 0:HD] += dkg
        dsm_ref[2:3, :] += dsink

    rev = lambda i: NBLK // 2 - 1 - i
    prev = lambda i: jnp.maximum(NBLK - 3 - 2 * i, 0)
    small = lambda shape: pl.BlockSpec(shape, lambda i: (0,) * len(shape))
    return pl.pallas_call(
        body, grid=(NBLK // 2,),
        in_specs=[pl.BlockSpec((2 * BLK, D_ATTN), lambda i: (rev(i), 0)),
                  pl.BlockSpec((2 * BLK, 128), lambda i: (rev(i), 4)), pl.BlockSpec((BLK, 128), lambda i: (prev(i), 4)),
                  pl.BlockSpec((2 * BLK, 128), lambda i: (rev(i), 5)), pl.BlockSpec((BLK, 128), lambda i: (prev(i), 5)),
                  pl.BlockSpec((2 * BLK, D_ATTN), lambda i: (rev(i), 0)),
                  small((DEPTH, HD)), small((DEPTH, HD)), small((DEPTH, NQ)), small((NQ, BLK, 2 * BLK))] + [ANY_SPEC] * len(deps),
        out_specs=[pl.BlockSpec((2 * BLK, 768), lambda i: (rev(i), COL_QKV // 768)), small((NQ, BLK, 2 * BLK)), small((8, 128))],
        out_shape=_out_hbm([SDS((S, D_IN_PAD), bf16), SDS((NQ, BLK, 2 * BLK), f32), SDS((8, 128), f32)]),
        scratch_shapes=[pltpu.VMEM((BLK, 256), f32)], name="attn_bwd", compiler_params=_cparams(1),
    )(*_in_hbm([qkv, qkv, qkv, qkv, qkv, dmix, q_gain, k_gain, sinks, bias, *deps]))


CONV_TC = 256


def _shift_down(u, s):
    if s == 0:
        return u
    rows = lax.broadcasted_iota(jnp.int32, u.shape, 0)
    return jnp.where(rows >= s, pltpu.roll(u, s, 0), 0.0)


def _shift_up(u, s):
    if s == 0:
        return u
    rows = lax.broadcasted_iota(jnp.int32, u.shape, 0)
    return jnp.where(rows < u.shape[0] - s, pltpu.roll(u, u.shape[0] - s, 0), 0.0)


def _conv_specs():
    return [pl.BlockSpec((S, CONV_TC), lambda c: (0, c)),
            pl.BlockSpec((None, 4, CONV_TC), lambda c: (0, 0, c)),
            pl.BlockSpec((DEPTH, CONV_TC), lambda c: (0, c))]


def _conv_pre(u, w_ref, b_ref, layer):
    pre = b_ref[layer:layer + 1, :] + w_ref[3:4, :] * u
    for k in range(3):
        pre = pre + w_ref[k:k + 1, :] * _shift_down(u, 3 - k)
    return pre


def _conv_fwd(xbc, conv_w, conv_b, layer):
    def body(u_ref, w_ref, b_ref, o_ref):
        pre = _conv_pre(u_ref[...], w_ref, b_ref, layer)
        o_ref[...] = pre * _sigmoid(pre)

    specs = _conv_specs()
    specs[1] = pl.BlockSpec((None, 4, CONV_TC), lambda c: (layer, 0, c))
    return pl.pallas_call(
        body, grid=(D_CONV // CONV_TC,), in_specs=specs, out_specs=pl.BlockSpec((S, CONV_TC), lambda c: (0, c)),
        out_shape=_out_hbm(SDS((S, D_CONV), f32)), name="conv_fwd", compiler_params=_cparams(1),
    )(*_in_hbm([xbc, conv_w, conv_b]))


def _conv_bwd(xbc, dact, conv_w, conv_b, dproj, layer):
    def body(u_ref, w_ref, b_ref, da_ref, dproj_in, du_ref, dw_ref, db_ref):
        u = u_ref[...]
        pre = _conv_pre(u, w_ref, b_ref, layer)
        sg = _sigmoid(pre)
        dpre = da_ref[...] * (sg * (1.0 + pre * (1.0 - sg)))
        du = w_ref[3:4, :] * dpre
        for k in range(3):
            du = du + w_ref[k:k + 1, :] * _shift_up(dpre, 3 - k)
        du_ref[...] = du.astype(bf16)
        db_ref[...] = jnp.broadcast_to(jnp.sum(dpre, axis=0, keepdims=True), db_ref.shape)
        dw_ref[...] = jnp.zeros_like(dw_ref)
        for k in range(4):
            dw_ref[k:k + 1, :] = jnp.sum(dpre * _shift_down(u, 3 - k), axis=0, keepdims=True)

    specs = _conv_specs()
    specs[1] = pl.BlockSpec((None, 4, CONV_TC), lambda c: (layer, 0, c))
    col = pl.BlockSpec((S, CONV_TC), lambda c: (0, c))
    row8 = pl.BlockSpec((8, CONV_TC), lambda c: (0, c))
    return pl.pallas_call(
        body, grid=(D_CONV // CONV_TC,), in_specs=[*specs, col, ANY_SPEC],
        out_specs=[pl.BlockSpec((S, CONV_TC), lambda c: (0, COL_XBC // CONV_TC + c)), row8, row8],
        out_shape=_out_hbm([SDS((S, D_IN_PAD), bf16), SDS((8, D_CONV), f32), SDS((8, D_CONV), f32)]), name="conv_bwd",
        input_output_aliases={4: 0}, compiler_params=_cparams(1),
    )(*_in_hbm([xbc, conv_w, conv_b, dact, dproj]))


def _tri():
    return (lax.broadcasted_iota(jnp.int32, (BLK, BLK), 0) >= lax.broadcasted_iota(jnp.int32, (BLK, BLK), 1))


def _ssd_scalars(dt_ref, dtb_ref, alog_ref, layer):
    raw = dt_ref[:, 0:NSSM] + dtb_ref[layer:layer + 1, :]
    dtv = jnp.maximum(raw, 0.0) + jnp.log(1.0 + jnp.exp(-jnp.abs(raw)))
    a = -jnp.exp(alog_ref[layer:layer + 1, :])
    acs = jnp.dot(_tri().astype(f32), dtv * a, preferred_element_type=f32, precision=HIGHEST)
    return raw, dtv, a, acs


HG = NSSM // NGRP
GW = HG * HD


def _lane_expand(cols, g):
    lane_head = lax.broadcasted_iota(jnp.int32, (1, GW), 1) // HD
    out = cols[:, HG * g + HG - 1:HG * g + HG]
    for r in range(HG - 2, -1, -1):
        out = jnp.where(lane_head == r, cols[:, HG * g + r:HG * g + r + 1], out)
    return out


def _row_expand(vals, g):
    row_head = lax.broadcasted_iota(jnp.int32, (GW, 1), 0) // HD
    out = vals[:, HG * g + HG - 1:HG * g + HG]
    for r in range(HG - 2, -1, -1):
        out = jnp.where(row_head == r, vals[:, HG * g + r:HG * g + r + 1], out)
    return out


def _head_rowsums(a, g):
    sel = (lax.broadcasted_iota(jnp.int32, (GW, NSSM), 0) // HD + HG * g == lax.broadcasted_iota(jnp.int32, (GW, NSSM), 1)).astype(bf16)
    hi = a.astype(bf16)
    lo = (a - hi.astype(f32)).astype(bf16)
    return _dot(hi, sel, NN_DIMS) + _dot(lo, sel, NN_DIMS)


def _head_blocksums(v, g):
    sel = (lax.broadcasted_iota(jnp.int32, (GW, NSSM), 0) // HD + HG * g == lax.broadcasted_iota(jnp.int32, (GW, NSSM), 1)).astype(bf16)
    hi = v.astype(bf16)
    lo = (v - hi.astype(f32)).astype(bf16)
    return _dot(hi, sel, TN_DIMS) + _dot(lo, sel, TN_DIMS)


def _ssd_chunk_common(xc_ref, dt_ref, dtb_ref, alog_ref, h_rows, layer):
    raw, dtv, a, acs = _ssd_scalars(dt_ref, dtb_ref, alog_ref, layer)
    acs_t = acs.T
    last = acs[BLK - 1:BLK, :]
    c = dict(raw=raw, dtv=dtv, a=a, acs=acs, last=last, dte=jnp.exp(last - acs), e_all=jnp.exp(acs), cd=jnp.exp(last))
    grp, heads, tri = range(NGRP), range(NSSM), _tri()
    c["bm"] = [xc_ref[:, pl.ds(D_SSM + NSTATE * g, NSTATE)] for g in grp]
    c["bm_b"] = [c["bm"][g].astype(bf16) for g in grp]
    c["cm_b"] = [xc_ref[:, pl.ds(D_SSM + NGRP * NSTATE + NSTATE * g, NSTATE)].astype(bf16) for g in grp]
    c["cb"] = [_dot(c["cm_b"][g], c["bm_b"][g], NT_DIMS) for g in grp]
    c["x"] = [xc_ref[:, pl.ds(GW * g, GW)] for g in grp]
    c["dt"] = [_lane_expand(dtv, g) for g in grp]
    c["xdt"] = [c["x"][g] * c["dt"][g] for g in grp]
    c["xdt_b"] = [c["xdt"][g].astype(bf16) for g in grp]
    c["prev"] = [h_rows(g) for g in grp]
    c["prev_b"] = [c["prev"][g].astype(bf16) for g in grp]
    c["e"] = [_lane_expand(c["e_all"], g) for g in grp]
    c["y_off"] = [_dot(c["cm_b"][g], c["prev_b"][g], NT_DIMS) * c["e"][g] for g in grp]
    c["decay"] = [jnp.exp(jnp.where(tri, acs[:, h:h + 1] - acs_t[h:h + 1, :], -jnp.inf)) for h in heads]
    c["m"] = [c["cb"][h // HG] * c["decay"][h] for h in heads]
    c["m_b"] = [c["m"][h].astype(bf16) for h in heads]
    c["dte_x"] = [_lane_expand(c["dte"], g) for g in grp]
    c["xdte_b"] = [(c["xdt"][g] * c["dte_x"][g]).astype(bf16) for g in grp]
    return c


def _ssd_fwd(xact, z, dt, attn, dt_bias, a_log, d_skip, norm_g, layer):
    def body(xc_ref, z_ref, dt_ref, at_ref, dtb_ref, alog_ref, dsk_ref, ng_ref, mix_ref, hs_ref, y_ref, h_ref):
        n = pl.program_id(0)

        @pl.when(n == 0)
        def _():
            h_ref[...] = jnp.zeros_like(h_ref)

        hs_ref[...] = h_ref[...]
        c = _ssd_chunk_common(xc_ref, dt_ref, dtb_ref, alog_ref, lambda g: h_ref[pl.ds(GW * g, GW), :], layer)
        grp, heads = range(NGRP), range(NSSM)
        y_diag = [_dot(c["m_b"][h], c["xdt_b"][h // HG][:, HD * (h % HG):HD * (h % HG + 1)], NN_DIMS) for h in heads]
        new_st = [_dot(c["xdte_b"][g], c["bm_b"][g], TN_DIMS) for g in grp]
        for h in heads:
            y_ref[:, pl.ds(HD * h, HD)] = y_diag[h]
        dskip = dsk_ref[layer:layer + 1, :]
        for g in grp:
            cols = pl.ds(GW * g, GW)
            y_ref[:, cols] = y_ref[:, cols] + c["y_off"][g] + c["x"][g] * _lane_expand(dskip, g)
            h_ref[cols, :] = c["prev"][g] * _row_expand(c["cd"], g) + new_st[g]
        zv = z_ref[...]
        yz = y_ref[...] * (zv * _sigmoid(zv))
        mix_ref[:, 0:D_ATTN] = at_ref[...]
        for g in grp:
            yg = yz[:, GW * g:GW * (g + 1)]
            rs = lax.rsqrt(jnp.mean(yg * yg, axis=-1, keepdims=True) + EPS)
            mix_ref[:, D_ATTN + GW * g:D_ATTN + GW * (g + 1)] = (yg * rs * ng_ref[layer:layer + 1, GW * g:GW * (g + 1)]).astype(bf16)

    small = lambda shape: pl.BlockSpec(shape, lambda n: (0,) * len(shape))
    return pl.pallas_call(
        body, grid=(NBLK,),
        in_specs=[pl.BlockSpec((BLK, D_CONV), lambda n: (n, 0)), pl.BlockSpec((BLK, D_SSM), lambda n: (n, 0)),
                  pl.BlockSpec((BLK, 128), lambda n: (n, 0)), pl.BlockSpec((BLK, D_ATTN), lambda n: (n, 0)),
                  small((DEPTH, NSSM)), small((DEPTH, NSSM)), small((DEPTH, NSSM)), small((DEPTH, D_SSM))],
        out_specs=[pl.BlockSpec((BLK, D), lambda n: (n, 0)), pl.BlockSpec((None, NSSM * HD, NSTATE), lambda n: (n, 0, 0)),
                   pl.BlockSpec((BLK, D_SSM), lambda n: (n, 0))],
        out_shape=_out_hbm([SDS((S, D), bf16), SDS((NBLK, NSSM * HD, NSTATE), f32), SDS((S, D_SSM), f32)]),
        scratch_shapes=[pltpu.VMEM((NSSM * HD, NSTATE), f32)],
        name="ssd_fwd", compiler_params=_cparams(1),
    )(*_in_hbm([xact, z, dt, attn, dt_bias, a_log, d_skip, norm_g]))


def _ssd_bwd(xact, z, dt, dmix, hs, y, dt_bias, a_log, d_skip, norm_g, dproj, layer):
    def body(xc_ref, z_ref, dt_ref, do_ref, hs_ref, y_ref, dtb_ref, alog_ref, dsk_ref, ng_ref, dproj_in,
             dzdt_ref, dx_ref, dsm_ref, dh_ref, dy_ref):
        i = pl.program_id(0)

        @pl.when(i == 0)
        def _():
            dh_ref[...] = jnp.zeros_like(dh_ref)
            dsm_ref[...] = jnp.zeros_like(dsm_ref)

        c = _ssd_chunk_common(xc_ref, dt_ref, dtb_ref, alog_ref, lambda g: hs_ref[pl.ds(GW * g, GW), :], layer)
        raw, dtv, a = c["raw"], c["dtv"], c["a"]
        grp, heads = range(NGRP), range(NSSM)
        dskip = dsk_ref[layer:layer + 1, :]
        lane8 = lax.broadcasted_iota(jnp.int32, (1, NSSM), 1)
        sub8 = lax.broadcasted_iota(jnp.int32, (NSSM, 1), 0)

        zv = z_ref[...]
        sz = _sigmoid(zv)
        gz = zv * sz
        yv = y_ref[...]
        yz = yv * gz
        for g in grp:
            sl = slice(GW * g, GW * (g + 1))
            yg = yz[:, sl]
            rs = lax.rsqrt(jnp.mean(yg * yg, axis=-1, keepdims=True) + EPS)
            yhat = yg * rs
            dog = do_ref[:, sl]
            w = dog * ng_ref[layer:layer + 1, sl]
            dyz = rs * (w - yhat * jnp.mean(yhat * w, axis=-1, keepdims=True))
            dsm_ref[0:1, sl] += jnp.sum(dog * yhat, axis=0, keepdims=True)
            dy_ref[:, sl] = dyz * gz[:, sl]
            dzdt_ref[:, sl] = (dyz * yv[:, sl] * (sz[:, sl] * (1.0 + zv[:, sl] * (1.0 - sz[:, sl])))).astype(bf16)

        dy = [dy_ref[:, pl.ds(GW * g, GW)] for g in grp]
        dy_b = [dy[g].astype(bf16) for g in grp]
        hl = lambda h: slice(HD * (h % HG), HD * (h % HG + 1))
        dt_off_b = [(dy[g] * c["e"][g]).astype(bf16) for g in grp]
        dcm = [_dot(dt_off_b[g], c["prev_b"][g], NN_DIMS) for g in grp]
        dprev = [_dot(dt_off_b[g], c["cm_b"][g], TN_DIMS) for g in grp]
        yoff_rs = [_head_rowsums(dy[g] * c["y_off"][g], g) for g in grp]
        dhn = [dh_ref[pl.ds(GW * g, GW), :] for g in grp]
        dhn_b = [dhn[g].astype(bf16) for g in grp]
        dprev = [dprev[g] + dhn[g] * _row_expand(c["cd"], g) for g in grp]
        dhn_prev = [dhn[g] * c["prev"][g] for g in grp]
        u = [_dot(c["bm_b"][g], dhn_b[g], NT_DIMS) for g in grp]
        dbm = [_dot(c["xdte_b"][g], dhn_b[g], NN_DIMS) for g in grp]
        ddte_rs = [_head_rowsums(c["xdt"][g] * u[g], g) for g in grp]
        dm = [_dot(dy_b[h // HG][:, hl(h)], c["xdt_b"][h // HG][:, hl(h)], NT_DIMS) for h in heads]
        dxdt_in = [_dot(c["m_b"][h], dy_b[h // HG][:, hl(h)], TN_DIMS) for h in heads]
        dseg = [dm[h] * c["m"][h] for h in heads]
        dmd = [dm[h] * c["decay"][h] for h in heads]
        for h in heads:
            dx_ref[:, pl.ds(HD * h, HD)] = dxdt_in[h]

        tmp = (ddte_rs[0] + ddte_rs[1]) * c["dte"]
        dacs = yoff_rs[0] + yoff_rs[1] - tmp
        dacs_cols = jnp.zeros((NSSM, BLK), f32)
        ddtv = jnp.zeros((BLK, NSSM), f32)
        ddsk = jnp.zeros((BLK, NSSM), f32)
        hp = jnp.zeros((1, NSSM), f32)
        for g in grp:
            cols = pl.ds(GW * g, GW)
            dxdt = dx_ref[:, cols] + u[g] * c["dte_x"][g]
            dx_ref[:, cols] = dy[g] * _lane_expand(dskip, g) + dxdt * c["dt"][g]
            ddtv = ddtv + _head_rowsums(dxdt * c["x"][g], g)
            ddsk = ddsk + _head_rowsums(dy[g] * c["x"][g], g)
            dcb = dmd[HG * g]
            for r in range(1, HG):
                dcb = dcb + dmd[HG * g + r]
            dcb_b = dcb.astype(bf16)
            dx_ref[:, pl.ds(D_SSM + NSTATE * g, NSTATE)] = dbm[g] + _dot(dcb_b, c["cm_b"][g], TN_DIMS)
            dx_ref[:, pl.ds(D_SSM + NGRP * NSTATE + NSTATE * g, NSTATE)] = dcm[g] + _dot(dcb_b, c["bm_b"][g], NN_DIMS)
            dh_ref[cols, :] = dprev[g]
            hp = hp + _head_blocksums(jnp.sum(dhn_prev[g], axis=1, keepdims=True), g)
            for r in range(HG):
                h = HG * g + r
                dacs = dacs + (lane8 == h).astype(f32) * jnp.sum(dseg[h], axis=1, keepdims=True)
                dacs_cols = dacs_cols + (sub8 == h).astype(f32) * jnp.sum(dseg[h], axis=0, keepdims=True)
        dlast = hp * c["cd"] + jnp.sum(tmp, axis=0, keepdims=True)
        ddsk = jnp.sum(ddsk, axis=0, keepdims=True)

        row = lax.broadcasted_iota(jnp.int32, (BLK, 1), 0)
        dacs = dacs - dacs_cols.T + jnp.where(row == BLK - 1, dlast, 0.0)
        dda = lax.dot_general(_tri().astype(f32), dacs, TN_DIMS, preferred_element_type=f32, precision=HIGHEST)
        ddtv = ddtv + dda * a
        da = jnp.sum(dda * dtv, axis=0, keepdims=True)
        draw = ddtv * _sigmoid(raw)
        dzdt_ref[:, D_SSM:] = jnp.zeros((BLK, COL_XBC - COL_DT), bf16)
        dzdt_ref[:, D_SSM:D_SSM + NSSM] = draw.astype(bf16)
        dsm_ref[1:2, 0:NSSM] += jnp.sum(draw, axis=0, keepdims=True)
        dsm_ref[2:3, 0:NSSM] += da * a
        dsm_ref[3:4, 0:NSSM] += ddsk

    rev = lambda i: NBLK - 1 - i
    small = lambda shape: pl.BlockSpec(shape, lambda i: (0,) * len(shape))
    return pl.pallas_call(
        body, grid=(NBLK,),
        in_specs=[pl.BlockSpec((BLK, D_CONV), lambda i: (rev(i), 0)), pl.BlockSpec((BLK, D_SSM), lambda i: (rev(i), 0)),
                  pl.BlockSpec((BLK, 128), lambda i: (rev(i), 0)), pl.BlockSpec((BLK, D_SSM), lambda i: (rev(i), 1)),
                  pl.BlockSpec((None, NSSM * HD, NSTATE), lambda i: (rev(i), 0, 0)), pl.BlockSpec((BLK, D_SSM), lambda i: (rev(i), 0)),
                  small((DEPTH, NSSM)), small((DEPTH, NSSM)), small((DEPTH, NSSM)), small((DEPTH, D_SSM)), ANY_SPEC],
        out_specs=[pl.BlockSpec((BLK, COL_XBC - COL_Z), lambda i: (rev(i), COL_Z // (COL_XBC - COL_Z))),
                   pl.BlockSpec((BLK, D_CONV), lambda i: (rev(i), 0)), small((8, D_SSM))],
        out_shape=_out_hbm([SDS((S, D_IN_PAD), bf16), SDS((S, D_CONV), f32), SDS((8, D_SSM), f32)]),
        scratch_shapes=[pltpu.VMEM((NSSM * HD, NSTATE), f32), pltpu.VMEM((BLK, D_SSM), f32)],
        name="ssd_bwd", input_output_aliases={10: 0}, compiler_params=_cparams(1),
    )(*_in_hbm([xact, z, dt, dmix, hs, y, dt_bias, a_log, d_skip, norm_g, dproj]))


def _my_place():
    return lax.axis_index("x"), lax.axis_index("y"), lax.axis_index("c")


def _dev_index(px, py, pc):
    return 4 * px + 2 * py + pc


def _slab2(kind, ref, idx):
    if kind == "stack":
        return ref.at[idx]
    if kind == "rows128":
        return ref.at[pl.ds(pl.multiple_of(idx * 128, 128), 128), :]
    if kind == "rows512":
        return ref.at[pl.ds(pl.multiple_of(idx * 512, 512), 512), :]
    return ref.at[:, pl.ds(pl.multiple_of(idx * 512, 512), 512)]


def _slab_shape(kind, full_shape):
    if kind == "stack":
        return tuple(full_shape[1:])
    if kind == "rows128":
        return (128, full_shape[1])
    if kind == "rows512":
        return (512, full_shape[1])
    return (full_shape[0], 512)


KIND = dict(w_in="stack", w_out="rows128", w_up="cols512", w_down="rows512", conv_w="stack")
FULL_SHAPE = dict(w_in=(N_DEV, D, D_IN // N_DEV), w_out=(D, D), w_up=(D, D_FF), w_down=(D_FF, D))
HBM_SPEC = pl.BlockSpec(memory_space=pltpu.HBM)
SEM_SPEC = pl.BlockSpec(memory_space=pltpu.SEMAPHORE)
SIDE_EFFECT = pltpu.SideEffectType.DATAFLOW_SIDE_EFFECTING


def _peers_all():
    x, y, c = _my_place()
    return [(x ^ ((r >> 2) & 1), y ^ ((r >> 1) & 1), c ^ (r & 1)) for r in range(1, N_DEV)]


def _split_start(name, bufs, n_copies, plan, deps=()):
    nb = len(bufs)

    def body(*refs):
        ins = refs[:nb]
        send_sems, recv_sems = refs[nb + len(deps)], refs[nb + len(deps) + 1]
        token = refs[-1]
        for i, (src, dst, dev) in enumerate(plan(ins)):
            pltpu.make_async_remote_copy(src_ref=src, dst_ref=dst, send_sem=send_sems.at[i], recv_sem=recv_sems.at[i],
                                         device_id=dev, device_id_type=MESH).start()
        token[...] = jnp.zeros_like(token)

    outs = pl.pallas_call(
        body, name=name,
        out_shape=(pltpu.SemaphoreType.DMA((n_copies,)), pltpu.SemaphoreType.DMA((n_copies,)),
                   *[pltpu.HBM(b.shape, b.dtype) for b in bufs], SDS((8, 128), f32)),
        in_specs=[HBM_SPEC] * nb + [ANY_SPEC] * len(deps),
        out_specs=(SEM_SPEC, SEM_SPEC, *[HBM_SPEC] * nb, pl.BlockSpec(memory_space=pltpu.VMEM)),
        input_output_aliases={i: 2 + i for i in range(nb)},
        compiler_params=pltpu.CompilerParams(has_side_effects=SIDE_EFFECT),
    )(*[pltpu.with_memory_space_constraint(b, pltpu.HBM) for b in bufs], *deps)
    return dict(send=outs[0], recv=outs[1], bufs=list(outs[2:2 + nb]), token=outs[-1], plan=plan, n=n_copies)


def _split_wait(name, started, after):
    bufs = started["bufs"]
    nb = len(bufs)
    plan = started["plan"]

    def body(*refs):
        ins = refs[:nb]
        send_sems, recv_sems = refs[nb], refs[nb + 1]
        for i, (src, dst, dev) in enumerate(plan(ins)):
            cp = pltpu.make_async_remote_copy(src_ref=src, dst_ref=dst, send_sem=send_sems.at[i], recv_sem=recv_sems.at[i],
                                              device_id=dev, device_id_type=MESH)
            cp.wait_send()
            cp.wait_recv()

    outs = pl.pallas_call(
        body, name=name, out_shape=tuple(pltpu.HBM(b.shape, b.dtype) for b in bufs),
        in_specs=[HBM_SPEC] * nb + [SEM_SPEC, SEM_SPEC] + [ANY_SPEC] * len(after), out_specs=(HBM_SPEC,) * nb,
        input_output_aliases={i: i for i in range(nb)},
        compiler_params=pltpu.CompilerParams(has_side_effects=SIDE_EFFECT),
    )(*bufs, started["send"], started["recv"], *after)
    return list(outs)


def _gather_start(name, names, fulls, deps):
    n_t = len(names)

    def plan(refs):
        x, y, c = _my_place()
        my_idx = _dev_index(x, y, c)
        targets = [(x, y, 1 - c), (1 - x, y, c), (x, 1 - y, c), (1 - x, 1 - y, c)]
        slabs = [_slab2(KIND[names[t]], refs[t], my_idx) for t in range(n_t)]
        return [(slabs[t], slabs[t], dev) for t in range(n_t) for dev in targets]

    return _split_start(name, list(fulls), 4 * n_t, plan, deps)


def _gather_finish(name, names, started, after):
    n_t = len(names)
    fulls = _split_wait(name + "_wait", started, after)
    slab_shapes = [SDS(_slab_shape(KIND[n], f.shape), f.dtype) for n, f in zip(names, fulls)]

    def body(*refs):
        ins = refs[:n_t]
        outs = refs[n_t:2 * n_t]
        stage = refs[2 * n_t:3 * n_t]
        load_sems, send_sems, recv_sems = refs[3 * n_t:]
        x, y, c = _my_place()
        chips = [(1 - x, y), (x, 1 - y), (1 - x, 1 - y)]
        pairs = [(t, j) for t in range(n_t) for j in range(3)]
        loads = [pltpu.make_async_copy(_slab2(KIND[names[t]], ins[t], _dev_index(*chips[j], c)), stage[t].at[j], load_sems.at[t, j])
                 for t, j in pairs]
        for cp in loads:
            cp.start()

        def copy(t, j, core):
            return pltpu.make_async_remote_copy(
                src_ref=stage[t].at[j], dst_ref=_slab2(KIND[names[t]], outs[t], _dev_index(*chips[j], core)),
                send_sem=send_sems.at[t, j], recv_sem=recv_sems.at[t, j], device_id=(x, y, 1 - c), device_id_type=MESH)

        sends = [copy(t, j, c) for t, j in pairs]
        for ld, cp in zip(loads, sends):
            ld.wait()
            cp.start()
        for t, j in pairs:
            copy(t, j, 1 - c).wait_recv()
        for cp in sends:
            cp.wait_send()

    return pl.pallas_call(
        body, in_specs=[ANY_SPEC] * n_t, out_specs=[ANY_SPEC] * n_t, out_shape=[SDS(b.shape, b.dtype) for b in fulls],
        input_output_aliases={t: t for t in range(n_t)},
        scratch_shapes=[pltpu.VMEM((3,) + s.shape, s.dtype) for s in slab_shapes]
        + [pltpu.SemaphoreType.DMA((n_t, 3)), pltpu.SemaphoreType.DMA((n_t, 3)), pltpu.SemaphoreType.DMA((n_t, 3))],
        name=name + "_pass", compiler_params=pltpu.CompilerParams(vmem_limit_bytes=VMEM_LIMIT),
    )(*fulls)


def _exchange_start(name, names, grads, deps):
    n_t = len(names)
    lands = [lax.empty((N_DEV,) + _slab_shape(KIND[n], g.shape), g.dtype) for n, g in zip(names, grads)]

    def plan(refs):
        my_idx = _dev_index(*_my_place())
        return [(_slab2(KIND[names[t]], refs[t], _dev_index(*peer)), refs[n_t + t].at[my_idx], peer)
                for t in range(n_t) for peer in _peers_all()]

    return _split_start(name, list(grads) + lands, 7 * n_t, plan, deps)


def _small_exchange_start(part, deps):
    land = lax.empty((N_DEV,) + part.shape, part.dtype)

    def plan(refs):
        my_idx = _dev_index(*_my_place())
        return [(refs[0], refs[1].at[my_idx], peer) for peer in _peers_all()]

    return _split_start("small_exchange", [part, land], N_DEV - 1, plan, deps)


def _slab_pieces():
    sh = D_IN // N_DEV
    out = []
    for j in range(N_DEV):
        for first, end, dst in IN_SEGMENTS:
            lo, hi = max(first, sh * j), min(end, sh * (j + 1))
            if lo < hi:
                out.append((j, lo - sh * j, hi - sh * j, dst + lo - first))
    return out


def _w_in_assemble(stacked):
    tr = 256
    sh = D_IN // N_DEV

    def body(i_ref, o_ref):
        o_ref[:, COL_DT:COL_XBC] = jnp.zeros((tr, COL_XBC - COL_DT), bf16)
        for j, lo, hi, dst in _slab_pieces():
            o_ref[:, dst:dst + hi - lo] = i_ref[j, :, lo:hi]

    return pl.pallas_call(
        body, grid=(D // tr,), in_specs=[pl.BlockSpec((N_DEV, tr, sh), lambda i: (0, i, 0))],
        out_specs=pl.BlockSpec((None, tr, D_IN_PAD), lambda i: (0, i, 0)), out_shape=_out_hbm(SDS((1, D, D_IN_PAD), bf16)),
        name="w_in_assemble", compiler_params=_cparams(1),
    )(*_in_hbm([stacked]))


def _w_in_slabs(dw_in):
    tr = 256
    sh = D_IN // N_DEV

    def body(i_ref, o_ref):
        for j, lo, hi, src in _slab_pieces():
            o_ref[j, :, lo:hi] = i_ref[:, src:src + hi - lo]

    return pl.pallas_call(
        body, grid=(D // tr,), in_specs=[pl.BlockSpec((tr, D_IN_PAD), lambda i: (i, 0))],
        out_specs=pl.BlockSpec((N_DEV, tr, sh), lambda i: (0, i, 0)), out_shape=_out_hbm(SDS((N_DEV, D, sh), bf16)),
        name="w_in_slabs", compiler_params=_cparams(1),
    )(*_in_hbm([dw_in]))


SMALL_NAMES = ("mix_norm_g", "mlp_norm_g", "conv_b", "ssm_norm_g", "q_gain", "k_gain", "sinks", "dt_bias", "a_log", "d_skip",
               "rel_bias", "conv_w")
MISC_LANES = dict(q_gain=(LANE_QG, HD), k_gain=(LANE_KG, HD), sinks=(LANE_SINK, NQ), dt_bias=(LANE_DTB, NSSM),
                  a_log=(LANE_ALOG, NSSM), d_skip=(LANE_DSKIP, NSSM))


def _pack_small_grads(smalls, drel_t, loss):
    def body(*refs):
        o_ref = refs[-1]
        drel_ref, loss_ref = refs[-3], refs[-2]
        o_ref[...] = jnp.zeros_like(o_ref)
        for l in range(DEPTH):
            mixg, mlpg, convb, convw, ssd, attn = refs[6 * l:6 * l + 6]
            o_ref[ROW_MIXG + l:ROW_MIXG + l + 1, :] = mixg[...]
            o_ref[ROW_MLPG + l:ROW_MLPG + l + 1, :] = mlpg[...]
            o_ref[ROW_CONVB + l:ROW_CONVB + l + 1, :] = convb[0:1, :]
            o_ref[ROW_SSMG + l:ROW_SSMG + l + 1, 0:D_SSM] = ssd[0:1, :]
            o_ref[ROW_CONVW + 4 * l:ROW_CONVW + 4 * l + 4, :] = convw[0:4, :]
            row = slice(ROW_MISC + l, ROW_MISC + l + 1)
            o_ref[row, LANE_QG:LANE_QG + HD] = attn[0:1, 0:HD]
            o_ref[row, LANE_KG:LANE_KG + HD] = attn[1:2, 0:HD]
            o_ref[row, LANE_SINK:LANE_SINK + NQ] = attn[2:3, 0:NQ]
            o_ref[row, LANE_DTB:LANE_DTB + NSSM] = ssd[1:2, 0:NSSM]
            o_ref[row, LANE_ALOG:LANE_ALOG + NSSM] = ssd[2:3, 0:NSSM]
            o_ref[row, LANE_DSKIP:LANE_DSKIP + NSSM] = ssd[3:4, 0:NSSM]
        o_ref[ROW_RELB:ROW_RELB + NQ, 0:N_BUCKETS] = drel_ref[...]
        o_ref[ROW_LOSS:ROW_LOSS + 1, 0:1] = loss_ref[0:1, 0:1]

    args = []
    for sm in smalls:
        args += [sm["mix_norm_g"], sm["mlp_norm_g"], sm["conv_b"], sm["conv_w"], sm["ssd"], sm["attn"]]
    args += [drel_t, loss]
    return pl.pallas_call(body, out_shape=SDS((SMALL_ROWS, D), f32), name="pack_small_grads")(*args)


def _adamw_small(part, land, w, m, v):
    n = len(SMALL_NAMES)

    def grad_of(name, g_ref):
        if name == "mix_norm_g":
            return g_ref[ROW_MIXG:ROW_MIXG + DEPTH, :]
        if name == "mlp_norm_g":
            return g_ref[ROW_MLPG:ROW_MLPG + DEPTH, :]
        if name == "conv_b":
            return g_ref[ROW_CONVB:ROW_CONVB + DEPTH, :]
        if name == "ssm_norm_g":
            return g_ref[ROW_SSMG:ROW_SSMG + DEPTH, 0:D_SSM]
        if name == "rel_bias":
            return g_ref[ROW_RELB:ROW_RELB + NQ, 0:N_BUCKETS].T
        lane, width = MISC_LANES[name]
        return g_ref[ROW_MISC:ROW_MISC + DEPTH, lane:lane + width]

    def body(part_ref, land_ref, *refs):
        ws, ms, vs = refs[:n], refs[n:2 * n], refs[2 * n:3 * n]
        loss_ref = refs[3 * n]
        outs = refs[3 * n + 1:-1]
        g_ref = refs[-1]
        me = _dev_index(*_my_place())
        for p in range(N_DEV):
            term = jnp.where(me == p, part_ref[...], land_ref[p])
            if p == 0:
                g_ref[...] = term
            else:
                g_ref[...] += term
        loss_ref[...] = g_ref[ROW_LOSS:ROW_LOSS + 1, 0:128]
        my_cols = pl.ds(pl.multiple_of(me * 128, 128), 128)
        for k, name in enumerate(SMALL_NAMES):
            g_out, d_out, m_out, v_out = outs[4 * k:4 * k + 4]
            if name == "conv_w":
                for l in range(DEPTH):
                    g = g_ref[ROW_CONVW + 4 * l:ROW_CONVW + 4 * l + 4, my_cols]
                    delta, m_new, v_new = _adamw_math(ws[k][l], ms[k][l], vs[k][l], g)
                    g_out[l], d_out[l], m_out[l], v_out[l] = g, delta, m_new, v_new
            else:
                g = grad_of(name, g_ref)
                delta, m_new, v_new = _adamw_math(ws[k][...], ms[k][...], vs[k][...], g)
                g_out[...], d_out[...], m_out[...], v_out[...] = g, delta, m_new, v_new

    ws = [w[name] for name in SMALL_NAMES]
    out_shape = [SDS((1, 128), f32)]
    for a in ws:
        out_shape += [SDS(a.shape, f32)] * 4
    return pl.pallas_call(body, out_shape=out_shape, name="adamw_small", scratch_shapes=[pltpu.VMEM((SMALL_ROWS, D), f32)])(
        part, land, *ws, *[m[name] for name in SMALL_NAMES], *[v[name] for name in SMALL_NAMES])


def _plain(tm, tn):
    return pl.BlockSpec((tm, tn), lambda i, j, k: (i, j))


def _rowblk(tm, width):
    return pl.BlockSpec((tm, width), lambda i, j, k: (i, 0))


def _store_epi(dtype):
    def epi(acc, i, j, ex, outs):
        outs[0][...] = acc.astype(dtype)
    return epi


def _rms_prologue(layer):
    def pro(a_ref, ex, outs):
        xv = a_ref[...]
        r = lax.rsqrt(jnp.mean(xv * xv, axis=-1, keepdims=True) + EPS)
        h = (xv * r * ex[0][layer:layer + 1, :]).astype(bf16)
        outs[-1][...] = h
        return h
    return pro


MLP_TM = 256
MLP_VMEM = 56 * 1024 * 1024


def _resident(shape):
    return pl.BlockSpec((None,) + shape, lambda i: (0, 0, 0), pipeline_mode=pl.Buffered(1))


def _mlp_fwd(layer, x, mix, g, w_out, w_up, w_down, tgt=None):
    tm = MLP_TM
    with_loss = tgt is not None

    def body(x_ref, mix_ref, g_ref, wo_ref, wu_ref, wd_ref, *rest):
        xm_ref, a_ref, r_ref, h_ref = rest[with_loss:with_loss + 4]
        rest = rest[:with_loss] + rest[with_loss + 1:]
        i = pl.program_id(0)
        xv = x_ref[...] + _dot(mix_ref[...], wo_ref[...], NN_DIMS)
        xm_ref[...] = xv
        h = (xv * lax.rsqrt(jnp.mean(xv * xv, axis=-1, keepdims=True) + EPS) * g_ref[layer:layer + 1, :]).astype(bf16)
        h_ref[...] = h
        r = jnp.maximum(_dot(h, wu_ref[...], NN_DIMS), 0.0)
        a = (r * r).astype(bf16)
        a_ref[...] = a
        r_ref[...] = r.astype(bf16)
        y = xv + _dot(a, wd_ref[...], NN_DIMS)
        if not with_loss:
            rest[3][...] = y
            return
        err = y - rest[0][...]
        rest[4][...] = err * (1.0 / D)
        part = 0.5 * jnp.sum(jnp.mean(err * err, axis=-1, keepdims=True), axis=0, keepdims=True)

        @pl.when(i == 0)
        def _():
            rest[5][...] = jnp.zeros_like(rest[5])

        rest[5][...] += jnp.broadcast_to(part, rest[5].shape)

    row = lambda width: pl.BlockSpec((tm, width), lambda i: (i, 0))
    in_specs = [row(D), row(D), pl.BlockSpec((DEPTH, D), lambda i: (0, 0)), _resident((D, D)), _resident((D, D_FF)),
                _resident((D_FF, D))]
    out_specs = [row(D), row(D_FF), row(D_FF), row(D), row(D)]
    out_shape = [SDS((S, D), f32), SDS((S, D_FF), bf16), SDS((S, D_FF), bf16), SDS((S, D), bf16), SDS((S, D), f32)]
    args = [x, mix, g, w_out, w_up, w_down]
    if with_loss:
        in_specs.append(row(D))
        args.append(tgt)
        out_specs.append(pl.BlockSpec((1, 128), lambda i: (0, 0)))
        out_shape.append(SDS((1, 128), f32))
    return pl.pallas_call(
        body, grid=(S // tm,), in_specs=in_specs, out_specs=out_specs, out_shape=_out_hbm(out_shape),
        name="mlp_fwd_loss" if with_loss else "mlp_fwd",
        compiler_params=pltpu.CompilerParams(dimension_semantics=("arbitrary",), vmem_limit_bytes=MLP_VMEM),
    )(*_in_hbm(args[:3]), *args[3:6], *_in_hbm(args[6:]))


def _mlp_bwd_act(layer, dx_out, r_act, x_mid, g, w_down, w_up, w_out, deps):
    tm = MLP_TM

    def body(dxo_ref, r_ref, xm_ref, g_ref, wd_ref, wu_ref, wo_ref, *rest):
        du_ref, dx_ref, dg_ref, dmix_ref = rest[len(deps):]
        dxo = dxo_ref[...]
        du = (_dot(dxo.astype(bf16), wd_ref[...], NT_DIMS) * (2.0 * r_ref[...].astype(f32))).astype(bf16)
        du_ref[...] = du
        dh = _dot(du, wu_ref[...], NT_DIMS)
        _rms_bwd_epilogue(layer)(dh, pl.program_id(0), 0, (xm_ref, g_ref, dxo_ref), (dx_ref, dg_ref))
        dmix_ref[...] = _dot(dx_ref[...].astype(bf16), wo_ref[...], NT_DIMS)

    row = lambda width: pl.BlockSpec((tm, width), lambda i: (i, 0))
    return pl.pallas_call(
        body, grid=(S // tm,),
        in_specs=[row(D), row(D_FF), row(D), pl.BlockSpec((DEPTH, D), lambda i: (0, 0)), _resident((D_FF, D)), _resident((D, D_FF)),
                  _resident((D, D))] + [ANY_SPEC] * len(deps),
        out_specs=[row(D_FF), row(D), pl.BlockSpec((1, D), lambda i: (0, 0)), row(D)],
        out_shape=_out_hbm([SDS((S, D_FF), bf16), SDS((S, D), f32), SDS((1, D), f32), SDS((S, D), f32)]), name="mlp_bwd_act",
        compiler_params=pltpu.CompilerParams(dimension_semantics=("arbitrary",), vmem_limit_bytes=MLP_VMEM),
    )(*_in_hbm([dx_out, r_act, x_mid, g]), w_down, w_up, w_out, *_in_hbm(deps))


def _layer_fwd(l, x, p, get_weights, bias, tgt=None):
    wts = get_weights(l, "in", [x, bias])
    gfull = pl.BlockSpec((DEPTH, D), lambda i, j, k: (0, 0))
    tm = 256

    def inproj_epi(acc, i, j, ex, outs):
        outs[0][...] = acc[:, COL_QKV:COL_Z]
        outs[1][...] = acc[:, COL_Z:COL_DT]
        outs[2][...] = acc[:, COL_XBC:D_IN_PAD]
        outs[3][...] = acc[:, COL_DT:COL_DT + 128]

    qkv, z, xbc, dt, h1 = _matmul(
        "in_proj", "nn", x, wts["w_in"], tm=tm, tn=D_IN_PAD, tk=D, prologue=_rms_prologue(l),
        extras=(p["mix_norm_g"],), extra_specs=(gfull,),
        out_shape=[SDS((S, 768), f32), SDS((S, 512), f32), SDS((S, 1024), f32), SDS((S, 128), f32), SDS((S, D), bf16)],
        out_specs=[_rowblk(tm, 768), _rowblk(tm, 512), _rowblk(tm, 1024), _rowblk(tm, 128), _rowblk(tm, D)], epilogue=inproj_epi)
    attn = _attn_fwd(qkv, p["q_gain"], p["k_gain"], p["sinks"], bias, l)
    xact = _conv_fwd(xbc, wts["conv_w"], p["conv_b"], l)
    mix, hs, y_ssd = _ssd_fwd(xact, z, dt, attn, p["dt_bias"], p["a_log"], p["d_skip"], p["ssm_norm_g"], l)
    wts = dict(wts, **get_weights(l, "rest", [mix]))

    x_mid, a_act, r_act, h2, *result = _mlp_fwd(l, x, mix, p["mlp_norm_g"], wts["w_out"], wts["w_up"], wts["w_down"], tgt)
    saved = dict(x=x, h1=h1, qkv=qkv, z=z, xbc=xbc, dt=dt, xact=xact, mix=mix, hs=hs, y_ssd=y_ssd, x_mid=x_mid, h2=h2,
                 a=a_act, r=r_act, wts=wts)
    return (result[0] if tgt is None else tuple(result)), saved


def _layer_bwd(l, dx_out, sv, p, bias, deps, send):
    wts = sv["wts"]

    dw_down = _matmul("dw_down", "tn", sv["a"], dx_out, tm=1024, tn=D, tk=S, out_shape=SDS((D_FF, D), bf16),
                      out_specs=_plain(1024, D), epilogue=_store_epi(bf16), deps=deps)
    deps = send(l, dict(w_down=dw_down))
    du, dx_mid, dg_mlp, dmix = _mlp_bwd_act(l, dx_out, sv["r"], sv["x_mid"], p["mlp_norm_g"], wts["w_down"], wts["w_up"],
                                            wts["w_out"], deps)
    dw_up = _matmul("dw_up", "tn", sv["h2"], du, tm=D, tn=1024, tk=S, out_shape=SDS((D, D_FF), bf16),
                    out_specs=_plain(D, 1024), epilogue=_store_epi(bf16))
    dw_out = _matmul("dw_out", "tn", sv["mix"], dx_mid, tm=D, tn=512, tk=512, out_shape=SDS((D, D), bf16),
                     out_specs=_plain(D, 512), epilogue=_store_epi(bf16))
    deps = send(l, dict(w_up=dw_up, w_out=dw_out))
    gfull = pl.BlockSpec((DEPTH, D), lambda i, j, k: (0, 0))
    grow = pl.BlockSpec((1, D), lambda i, j, k: (0, 0))
    dproj, dbias, dsm_attn = _attn_bwd(sv["qkv"], dmix, p["q_gain"], p["k_gain"], p["sinks"], bias, l, deps)
    dproj, dxact, dsm_ssd = _ssd_bwd(sv["xact"], sv["z"], sv["dt"], dmix, sv["hs"], sv["y_ssd"], p["dt_bias"], p["a_log"],
                                     p["d_skip"], p["ssm_norm_g"], dproj, l)
    dproj, dconv_w, dconv_b = _conv_bwd(sv["xbc"], dxact, wts["conv_w"], p["conv_b"], dproj, l)
    dw_in = _matmul("dw_in", "tn", sv["h1"], dproj, tm=D, tn=640, tk=S, out_shape=SDS((D, D_IN_PAD), bf16),
                    out_specs=_plain(D, 640), epilogue=_store_epi(bf16))
    deps = send(l, dict(w_in=_w_in_slabs(dw_in)))
    dx, dg_mix = _matmul(
        "in_proj_dh", "nt", dproj, wts["w_in"], tm=256, tn=D, tk=D_IN_PAD, out_shape=[SDS((S, D), f32), SDS((1, D), f32)],
        out_specs=[_plain(256, D), grow], epilogue=_rms_bwd_epilogue(l),
        extras=(sv["x"], p["mix_norm_g"], dx_mid), extra_specs=(_plain(256, D), gfull, _plain(256, D)), deps=deps)
    small = dict(mix_norm_g=dg_mix, mlp_norm_g=dg_mlp, conv_w=dconv_w, conv_b=dconv_b, ssd=dsm_ssd, attn=dsm_attn, dbias=dbias)
    return dx, small, deps


def _local_step(x, tgt, p, get_weights, send):
    onehot_t = jnp.asarray(_onehot_buckets())
    bias = _bias_build(p["rel_bias"].T, onehot_t).reshape(NQ, BLK, 2 * BLK)
    saved = []
    h = x
    for l in range(DEPTH):
        h, sv = _layer_fwd(l, h, p, get_weights, bias, tgt if l == DEPTH - 1 else None)
        saved.append(sv)
    dx, loss = h
    smalls = [None] * DEPTH
    deps = ()
    for l in reversed(range(DEPTH)):
        dx, smalls[l], deps = _layer_bwd(l, dx, saved[l], p, bias, deps, send)
    drel_t = _bias_grad(smalls[0]["dbias"].reshape(NQ, -1), smalls[1]["dbias"].reshape(NQ, -1), onehot_t)
    return dx, _pack_small_grads(smalls, drel_t, loss)


WEIGHT_ORDER = ("mix_norm_g", "w_in", "q_gain", "k_gain", "sinks", "rel_bias", "conv_w", "conv_b", "dt_bias", "a_log", "d_skip",
                "ssm_norm_g", "w_out", "mlp_norm_g", "w_up", "w_down")


def kernel(x, mix_norm_g, w_in, q_gain, k_gain, sinks, rel_bias, conv_w, conv_b, dt_bias, a_log, d_skip, ssm_norm_g, w_out, mlp_norm_g, w_up, w_down, loss_target, m_mix_norm_g, m_w_in, m_q_gain, m_k_gain, m_sinks, m_rel_bias, m_conv_w, m_conv_b, m_dt_bias, m_a_log, m_d_skip, m_ssm_norm_g, m_w_out, m_mlp_norm_g, m_w_up, m_w_down, v_mix_norm_g, v_w_in, v_q_gain, v_k_gain, v_sinks, v_rel_bias, v_conv_w, v_conv_b, v_dt_bias, v_a_log, v_d_skip, v_ssm_norm_g, v_w_out, v_mlp_norm_g, v_w_up, v_w_down):
    w = dict(mix_norm_g=mix_norm_g, w_in=w_in, q_gain=q_gain, k_gain=k_gain, sinks=sinks, rel_bias=rel_bias, conv_w=conv_w,
             conv_b=conv_b, dt_bias=dt_bias, a_log=a_log, d_skip=d_skip, ssm_norm_g=ssm_norm_g, w_out=w_out,
             mlp_norm_g=mlp_norm_g, w_up=w_up, w_down=w_down)
    m = dict(mix_norm_g=m_mix_norm_g, w_in=m_w_in, q_gain=m_q_gain, k_gain=m_k_gain, sinks=m_sinks, rel_bias=m_rel_bias,
             conv_w=m_conv_w, conv_b=m_conv_b, dt_bias=m_dt_bias, a_log=m_a_log, d_skip=m_d_skip, ssm_norm_g=m_ssm_norm_g,
             w_out=m_w_out, mlp_norm_g=m_mlp_norm_g, w_up=m_w_up, w_down=m_w_down)
    v = dict(mix_norm_g=v_mix_norm_g, w_in=v_w_in, q_gain=v_q_gain, k_gain=v_k_gain, sinks=v_sinks, rel_bias=v_rel_bias,
             conv_w=v_conv_w, conv_b=v_conv_b, dt_bias=v_dt_bias, a_log=v_a_log, d_skip=v_d_skip, ssm_norm_g=v_ssm_norm_g,
             w_out=v_w_out, mlp_norm_g=v_mlp_norm_g, w_up=v_w_up, w_down=v_w_down)
    big = ("w_in", "w_out", "w_up", "w_down")

    my_idx = _dev_index(*_my_place()).astype(jnp.int32).reshape(1)

    fulls = {n: _cast_to_full("cast_" + n, w[n], KIND[n], FULL_SHAPE[n], my_idx, bf16) for n in big}
    conv_full = _cast_to_full("cast_conv_w", conv_w.reshape(1, DEPTH * 4, 128), "stack", (N_DEV, DEPTH * 4, 128), my_idx, f32)[0]
    rest = ["w_out", "w_up", "w_down"]
    g0 = _gather_start("gather0", ["w_in", "conv_w"], [fulls["w_in"][0], conv_full], ())
    g1 = _gather_start("gather1", rest, [fulls[n][0] for n in rest], (g0["token"],))
    g2 = _gather_start("gather2", ["w_in"], [fulls["w_in"][1]], (g1["token"],))
    g3 = _gather_start("gather3", rest, [fulls[n][1] for n in rest], (g2["token"],))
    held = {}
    flat = lambda a: a.reshape(a.shape[0] * a.shape[1], a.shape[2])
    adam_in = {n: (flat(w[n]), flat(m[n]), flat(v[n])) for n in big}

    def get_weights(l, part, after):
        if l == 0 and part == "in":
            full_in, full_conv = _gather_finish("gather0", ["w_in", "conv_w"], g0,
                                                list(after) + [g3["token"], adam_in["w_in"][1], adam_in["w_in"][2]])
            held["conv_w"] = jnp.transpose(full_conv.reshape(N_DEV, DEPTH, 4, 128), (1, 2, 0, 3)).reshape(DEPTH, 4, D_CONV)
            return dict(w_in=_w_in_assemble(full_in), conv_w=held["conv_w"])
        if part == "in":
            return dict(w_in=_w_in_assemble(_gather_finish("gather2", ["w_in"], g2, after)[0]), conv_w=held["conv_w"])
        full = _gather_finish("gather1" if l == 0 else "gather3", rest, g1 if l == 0 else g3, after)
        return {n: f[None] for n, f in zip(rest, full)}

    pending = []

    def send(l, grads):
        names = list(grads)
        started = _exchange_start("exchange%d_%s" % (l, names[0]), names, [grads[n] for n in names], ())
        pending.append((l, names, started))
        return (started["token"],)

    dx, small_part = _local_step(x.reshape(S, D), loss_target.reshape(S, D), w, get_weights, send)

    small = _small_exchange_start(small_part, ())
    tiles = dict(w_in=256, w_out=128, w_up=256, w_down=256)
    outs_of = {n: None for n in big}
    after = [dx, small["token"]]
    for l, names, started in pending:
        bufs = _split_wait("exchange%d_%s_wait" % (l, names[0]), started, after)
        for t, n in enumerate(names):
            outs_of[n] = _adamw_layer("adamw_%s%d" % (n, l), KIND[n], l, *adam_in[n],
                                      bufs[len(names) + t], bufs[t], my_idx, outs_of[n], tiles[n])
        after = [outs_of[names[-1]][0]]
    res = {n: [o.reshape(w[n].shape) for o in outs_of[n]] for n in big}
    small_part, small_land = _split_wait("small_exchange_wait", small, after)
    small_outs = _adamw_small(small_part, small_land, w, m, v)
    loss = small_outs[0][0, 0]
    for k, name in enumerate(SMALL_NAMES):
        res[name] = small_outs[1 + 4 * k:5 + 4 * k]

    result = [loss, dx.reshape(1, S, D)]
    for k in range(4):
        result += [res[name][k] for name in WEIGHT_ORDER]
    return tuple(result)
```

```python
import functools
import math

import numpy as np
import jax
import jax.numpy as jnp
from jax import lax
from jax.experimental import pallas as pl
from jax.experimental.pallas import tpu as pltpu

f32 = jnp.float32
bf16 = jnp.bfloat16
SDS = jax.ShapeDtypeStruct
MESH = pl.DeviceIdType.MESH
HIGHEST = lax.Precision.HIGHEST

S = 2048
D = 1024
DEPTH = 2
BLK = 128
NBLK = S // BLK
HD = 64
NQ = 8
NKV = 2
NSSM = 8
NGRP = 2
NSTATE = 128
D_ATTN = 512
D_SSM = 512
D_CONV = 1024
D_FF = 4096
D_IN = 2312
D_IN_PAD = 2560
COL_QKV, COL_Z, COL_DT, COL_XBC = 0, 768, 1280, 1536
IN_SEGMENTS = ((0, 1280, 0), (1280, 2304, COL_XBC), (2304, 2312, COL_DT))
N_BUCKETS = 32
EPS = 1e-6
N_DEV = 8
VMEM_LIMIT = 48 * 1024 * 1024

ADAM_LR = 0.001
ADAM_B1 = 0.9
ADAM_B2 = 0.999
ADAM_EPS = 1e-08
ADAM_WD = 0.01
ADAM_STEP = 10

NT_DIMS = (((1,), (1,)), ((), ()))
TN_DIMS = (((0,), (0,)), ((), ()))
NN_DIMS = (((1,), (0,)), ((), ()))

ROW_MIXG = 0
ROW_MLPG = 2
ROW_CONVB = 4
ROW_SSMG = 6
ROW_MISC = 8
ROW_RELB = 10
ROW_CONVW = 18
ROW_LOSS = 26
SMALL_ROWS = 32
LANE_QG, LANE_KG, LANE_SINK, LANE_DTB, LANE_ALOG, LANE_DSKIP = 0, 64, 128, 256, 384, 512


def _dot(a, b, dims):
    return lax.dot_general(a, b, dims, preferred_element_type=f32)


def _cparams(n_axes):
    return pltpu.CompilerParams(dimension_semantics=("arbitrary",) * n_axes, vmem_limit_bytes=VMEM_LIMIT)


def _sum11(v):
    return jnp.sum(jnp.sum(v, axis=1, keepdims=True), axis=0, keepdims=True)


def _sigmoid(v):
    return 1.0 / (1.0 + jnp.exp(-v))


ANY_SPEC = pl.BlockSpec(memory_space=pl.ANY)


def _in_hbm(args):
    return [pltpu.with_memory_space_constraint(a, pltpu.HBM) if a.size >= 65536 else a for a in args]


def _out_hbm(out_shape):
    one = lambda s: pltpu.HBM(s.shape, s.dtype) if math.prod(s.shape) >= 65536 else s
    return [one(s) for s in out_shape] if isinstance(out_shape, (list, tuple)) else one(out_shape)


def _matmul(name, mode, a, b, *, layer=0, tm, tn, tk, out_shape, out_specs, epilogue, extras=(), extra_specs=(), deps=(),
            prologue=None):
    extras = tuple(extras) + tuple(deps)
    extra_specs = tuple(extra_specs) + (ANY_SPEC,) * len(deps)
    if mode == "tn":
        t_dim, m_dim = a.shape
        n_dim = b.shape[1]
        grid = (m_dim // tm, n_dim // tn, t_dim // tk)
        a_spec = pl.BlockSpec((tk, tm), lambda i, j, k: (k, i))
        b_spec = pl.BlockSpec((tk, tn), lambda i, j, k: (k, j))
        dims = TN_DIMS
    elif mode == "nn":
        m_dim, k_dim = a.shape
        n_dim = b.shape[-1]
        grid = (m_dim // tm, n_dim // tn, k_dim // tk)
        a_spec = pl.BlockSpec((tm, tk), lambda i, j, k: (i, k))
        b_spec = pl.BlockSpec((None, tk, tn), lambda i, j, k: (layer, k, j))
        dims = NN_DIMS
    else:
        m_dim, k_dim = a.shape
        n_dim = b.shape[-2]
        grid = (m_dim // tm, n_dim // tn, k_dim // tk)
        a_spec = pl.BlockSpec((tm, tk), lambda i, j, k: (i, k))
        b_spec = pl.BlockSpec((None, tn, tk), lambda i, j, k: (layer, j, k))
        dims = NT_DIMS
    nk = grid[2]
    n_ex = len(extras)

    def body(a_ref, b_ref, *rest):
        ex = rest[:n_ex - len(deps)]
        outs = rest[n_ex:-1]
        acc = rest[-1]
        i = pl.program_id(0)
        j = pl.program_id(1)
        k = pl.program_id(2)
        lhs = a_ref[...].astype(bf16) if prologue is None else prologue(a_ref, ex, outs)
        part = _dot(lhs, b_ref[...].astype(bf16), dims)
        if nk == 1:
            epilogue(part, i, j, ex, outs)
        else:
            @pl.when(k == 0)
            def _():
                acc[...] = part

            @pl.when(k > 0)
            def _():
                acc[...] += part

            @pl.when(k == nk - 1)
            def _():
                epilogue(acc[...], i, j, ex, outs)

    return pl.pallas_call(
        body, grid=grid, in_specs=[a_spec, b_spec, *extra_specs], out_specs=out_specs, out_shape=_out_hbm(out_shape),
        scratch_shapes=[pltpu.VMEM((tm, tn) if nk > 1 else (8, 128), f32)], name=name, compiler_params=_cparams(3),
    )(*_in_hbm([a]), b, *_in_hbm(extras))


def _rms_bwd_epilogue(layer):
    def epi(acc, i, j, ex, outs):
        x_ref, g_ref, dres_ref = ex
        dx_ref, dg_ref = outs
        xv = x_ref[...]
        r = lax.rsqrt(jnp.mean(xv * xv, axis=-1, keepdims=True) + EPS)
        xhat = xv * r
        w = acc * g_ref[layer:layer + 1, :]
        dx_ref[...] = dres_ref[...] + r * (w - xhat * jnp.mean(xhat * w, axis=-1, keepdims=True))
        dg = jnp.sum(acc * xhat, axis=0, keepdims=True)

        @pl.when(i == 0)
        def _():
            dg_ref[...] = dg

        @pl.when(i > 0)
        def _():
            dg_ref[...] += dg
    return epi


def _own_slab_spec(kind, tr, cols, nblk):
    if kind == "stack":
        return pl.BlockSpec((None, tr, cols), lambda i, idx: (idx[0], i, 0))
    if kind == "cols512":
        return pl.BlockSpec((tr, cols), lambda i, idx: (i, idx[0]))
    return pl.BlockSpec((tr, cols), lambda i, idx: (idx[0] * nblk + i, 0))


def _cast_to_full(name, w, kind, full_shape, my_idx, dtype):
    n_layers, rows, cols = w.shape
    tr = min(rows, 256)
    nblk = rows // tr

    def body(idx_ref, w_ref, *o_refs):
        for l in range(n_layers):
            o_refs[l][...] = w_ref[l].astype(dtype)

    grid_spec = pltpu.PrefetchScalarGridSpec(
        num_scalar_prefetch=1, grid=(nblk,), in_specs=[pl.BlockSpec((n_layers, tr, cols), lambda i, idx: (0, i, 0))],
        out_specs=[_own_slab_spec(kind, tr, cols, nblk)] * n_layers)
    return pl.pallas_call(body, grid_spec=grid_spec, out_shape=_out_hbm([SDS(full_shape, dtype)] * n_layers), name=name,
                          compiler_params=_cparams(1))(*_in_hbm([my_idx, w]))


def _adamw_math(w, m, v, g):
    m_new = ADAM_B1 * m + (1.0 - ADAM_B1) * g
    v_new = ADAM_B2 * v + (1.0 - ADAM_B2) * (g * g)
    m_hat = m_new / (1.0 - ADAM_B1 ** ADAM_STEP)
    v_hat = v_new / (1.0 - ADAM_B2 ** ADAM_STEP)
    delta = -ADAM_LR * (m_hat / (jnp.sqrt(v_hat) + ADAM_EPS) + ADAM_WD * w)
    return delta, m_new, v_new


def _adamw_layer(name, kind, layer, w, m, v, land, g_full, my_idx, prev, tr):
    rows2, cols = w.shape
    rows = rows2 // DEPTH
    nblk = rows // tr
    own_spec = _own_slab_spec(kind, tr, cols, nblk)
    n_prev = 0 if prev is None else 4

    def body(idx_ref, w_ref, m_ref, v_ref, land_ref, own_ref, *rest):
        g_ref, d_ref, mo_ref, vo_ref = rest[n_prev:]
        me = idx_ref[0]
        g = None
        for p in range(N_DEV):
            part = jnp.where(me == p, own_ref[...], land_ref[p]).astype(f32)
            g = part if g is None else g + part
        delta, m_new, v_new = _adamw_math(w_ref[...], m_ref[...], v_ref[...], g)
        g_ref[...] = g
        d_ref[...] = delta
        mo_ref[...] = m_new
        vo_ref[...] = v_new

    blk = pl.BlockSpec((tr, cols), lambda i, idx: (layer * nblk + i, 0))
    grid_spec = pltpu.PrefetchScalarGridSpec(
        num_scalar_prefetch=1, grid=(nblk,),
        in_specs=[blk, blk, blk, pl.BlockSpec((N_DEV, tr, cols), lambda i, idx: (0, i, 0)), own_spec] + [ANY_SPEC] * n_prev,
        out_specs=[blk, blk, blk, blk])
    aliases = {} if prev is None else {6 + k: k for k in range(4)}
    return pl.pallas_call(
        body, grid_spec=grid_spec, out_shape=_out_hbm([SDS((rows2, cols), f32)] * 4), name=name, input_output_aliases=aliases,
        compiler_params=_cparams(1),
    )(*_in_hbm([my_idx, w, m, v, land, g_full, *([] if prev is None else prev)]))


def _bucket_table():
    qi = np.arange(BLK)[:, None]
    kj = np.arange(2 * BLK)[None, :]
    dist = qi + BLK - kj
    dcl = np.clip(dist, 0, None)
    max_exact = N_BUCKETS // 2
    d_f = np.maximum(dcl, 1).astype(np.float32)
    large = max_exact + (np.log(d_f / np.float32(max_exact)) / np.float32(math.log(128 / max_exact))
                         * np.float32(N_BUCKETS - max_exact)).astype(np.int32)
    large = np.minimum(large, N_BUCKETS - 1)
    bucket = np.where(dcl < max_exact, dcl, large)
    in_window = (dist >= 0) & (dist < BLK)
    return bucket.astype(np.int32), in_window


def _onehot_buckets():
    bucket, _ = _bucket_table()
    oh = (bucket.reshape(-1)[None, :] == np.arange(N_BUCKETS)[:, None]).astype(np.float32)
    return oh


def _bias_build(rel_bias_t, onehot_t):
    def body(r_ref, o_ref, out_ref):
        out_ref[...] = jnp.dot(r_ref[...], o_ref[...], preferred_element_type=f32, precision=HIGHEST)

    tn = 4096
    return pl.pallas_call(
        body, grid=(BLK * 2 * BLK // tn,),
        in_specs=[pl.BlockSpec((NQ, N_BUCKETS), lambda i: (0, 0)), pl.BlockSpec((N_BUCKETS, tn), lambda i: (0, i))],
        out_specs=pl.BlockSpec((NQ, tn), lambda i: (0, i)), out_shape=SDS((NQ, BLK * 2 * BLK), f32), name="bias_build",
        compiler_params=_cparams(1),
    )(rel_bias_t, onehot_t)


def _bias_grad(dbias0, dbias1, onehot_t):
    tn = 4096
    nsteps = BLK * 2 * BLK // tn

    def body(a_ref, b_ref, o_ref, out_ref):
        part = lax.dot_general(a_ref[...] + b_ref[...], o_ref[...], NT_DIMS, preferred_element_type=f32, precision=HIGHEST)

        @pl.when(pl.program_id(0) == 0)
        def _():
            out_ref[...] = part

        @pl.when(pl.program_id(0) > 0)
        def _():
            out_ref[...] += part

    return pl.pallas_call(
        body, grid=(nsteps,),
        in_specs=[pl.BlockSpec((NQ, tn), lambda i: (0, i)), pl.BlockSpec((NQ, tn), lambda i: (0, i)),
                  pl.BlockSpec((N_BUCKETS, tn), lambda i: (0, i))],
        out_specs=pl.BlockSpec((NQ, N_BUCKETS), lambda i: (0, 0)), out_shape=SDS((NQ, N_BUCKETS), f32), name="bias_grad",
        compiler_params=_cparams(1),
    )(dbias0, dbias1, onehot_t)


def _attn_mask(n):
    qi = lax.broadcasted_iota(jnp.int32, (BLK, 2 * BLK), 0)
    kj = lax.broadcasted_iota(jnp.int32, (BLK, 2 * BLK), 1)
    dist = qi + BLK - kj
    first_key = jnp.where(n > 0, 0, BLK)
    return (dist >= 0) & (dist < BLK) & (kj >= first_key)


def _row_mean(a):
    return jnp.mean(a, axis=-1, keepdims=True)


def _head_norm(t, gain):
    r = lax.rsqrt(_row_mean(t * t) + EPS)
    that = t * r
    return that, r, that * gain


def _softmax_with_sink(s, sink):
    m = jnp.maximum(jnp.max(s, axis=-1, keepdims=True), sink)
    p = jnp.exp(s - m)
    psink = jnp.exp(sink - m)
    inv = 1.0 / (jnp.sum(p, axis=-1, keepdims=True) + psink)
    return p * inv, psink * inv


GQ = NQ // NKV


def _attn_fwd(qkv, q_gain, k_gain, sinks, bias, layer):
    def body(q_ref, kc_ref, kp_ref, vc_ref, vp_ref, qg_ref, kg_ref, sk_ref, bias_ref, o_ref):
        m = pl.program_id(0)
        qg = qg_ref[layer:layer + 1, :]
        kg = kg_ref[layer:layer + 1, :]
        grp = range(NKV)
        chains = [(b, j) for b in range(2) for j in grp]
        masks = [jnp.tile(_attn_mask(2 * m + b), (GQ, 1)) for b in range(2)]
        kblk = [[kp_ref[:, pl.ds(HD * j, HD)].astype(f32), kc_ref[0:BLK, pl.ds(HD * j, HD)].astype(f32),
                 kc_ref[BLK:, pl.ds(HD * j, HD)].astype(f32)] for j in grp]
        vblk = [[vp_ref[:, pl.ds(HD * j, HD)].astype(bf16), vc_ref[0:BLK, pl.ds(HD * j, HD)].astype(bf16),
                 vc_ref[BLK:, pl.ds(HD * j, HD)].astype(bf16)] for j in grp]
        knb = [[_head_norm(kblk[j][t], kg)[2].astype(bf16) for t in range(3)] for j in grp]
        kn_b = {(b, j): jnp.concatenate([knb[j][b], knb[j][b + 1]], axis=0) for b, j in chains}
        vbs = {(b, j): jnp.concatenate([vblk[j][b], vblk[j][b + 1]], axis=0) for b, j in chains}
        rows = {}
        for b, j in chains:
            heads = [GQ * j + g for g in range(GQ)]
            rows[b, j] = (jnp.concatenate([q_ref[pl.ds(BLK * b, BLK), pl.ds(HD * h, HD)] for h in heads], axis=0).astype(f32),
                          jnp.concatenate([jnp.broadcast_to(sk_ref[layer:layer + 1, h:h + 1], (BLK, 1)) for h in heads], axis=0))
        qn_b = {c: _head_norm(rows[c][0], qg)[2].astype(bf16) for c in chains}
        ss = {(b, j): _dot(qn_b[b, j], kn_b[b, j], NT_DIMS) * (HD ** -0.5) + bias_ref[GQ * j:GQ * (j + 1)].reshape(GQ * BLK, 2 * BLK)
              for b, j in chains}
        ps = {(b, j): _softmax_with_sink(jnp.where(masks[b], ss[b, j], -jnp.inf), rows[b, j][1])[0] for b, j in chains}
        outs = {c: _dot(ps[c].astype(bf16), vbs[c], NN_DIMS).astype(bf16) for c in chains}
        for b, j in chains:
            for g in range(GQ):
                o_ref[pl.ds(BLK * b, BLK), pl.ds(HD * (GQ * j + g), HD)] = outs[b, j][BLK * g:BLK * (g + 1), :]

    prev = lambda m: jnp.maximum(2 * m - 1, 0)
    small = lambda shape: pl.BlockSpec(shape, lambda m: (0,) * len(shape))
    return pl.pallas_call(
        body, grid=(NBLK // 2,),
        in_specs=[pl.BlockSpec((2 * BLK, D_ATTN), lambda m: (m, 0)),
                  pl.BlockSpec((2 * BLK, 128), lambda m: (m, 4)), pl.BlockSpec((BLK, 128), lambda m: (prev(m), 4)),
                  pl.BlockSpec((2 * BLK, 128), lambda m: (m, 5)), pl.BlockSpec((BLK, 128), lambda m: (prev(m), 5)),
                  small((DEPTH, HD)), small((DEPTH, HD)), small((DEPTH, NQ)), small((NQ, BLK, 2 * BLK))],
        out_specs=pl.BlockSpec((2 * BLK, D_ATTN), lambda m: (m, 0)), out_shape=_out_hbm(SDS((S, D_ATTN), bf16)),
        name="attn_fwd", compiler_params=_cparams(1),
    )(*_in_hbm([qkv, qkv, qkv, qkv, qkv, q_gain, k_gain, sinks, bias]))


def _attn_bwd(qkv, dmix, q_gain, k_gain, sinks, bias, layer, deps=()):
    def body(q_ref, kc_ref, kp_ref, vc_ref, vp_ref, do_ref, qg_ref, kg_ref, sk_ref, bias_ref, *rest):
        dqkv_ref, dbias_ref, dsm_ref, carry = rest[len(deps):]
        i = pl.program_id(0)
        m = NBLK // 2 - 1 - i
        qg = qg_ref[layer:layer + 1, :]
        kg = kg_ref[layer:layer + 1, :]
        lane = lax.broadcasted_iota(jnp.int32, (1, 128), 1)

        @pl.when(i == 0)
        def _():
            carry[...] = jnp.zeros_like(carry)
            dbias_ref[...] = jnp.zeros_like(dbias_ref)
            dsm_ref[...] = jnp.zeros_like(dsm_ref)

        grp = range(NKV)
        chains = [(b, j) for b in range(2) for j in grp]
        masks = [jnp.tile(_attn_mask(2 * m + b), (GQ, 1)) for b in range(2)]
        kblk = [[kp_ref[:, pl.ds(HD * j, HD)].astype(f32), kc_ref[0:BLK, pl.ds(HD * j, HD)].astype(f32),
                 kc_ref[BLK:, pl.ds(HD * j, HD)].astype(f32)] for j in grp]
        vblk = [[vp_ref[:, pl.ds(HD * j, HD)].astype(bf16), vc_ref[0:BLK, pl.ds(HD * j, HD)].astype(bf16),
                 vc_ref[BLK:, pl.ds(HD * j, HD)].astype(bf16)] for j in grp]
        knorm = [[_head_norm(kblk[j][t], kg) for t in range(3)] for j in grp]
        kn_b = {(b, j): jnp.concatenate([knorm[j][b][2].astype(bf16), knorm[j][b + 1][2].astype(bf16)], axis=0) for b, j in chains}
        vbs = {(b, j): jnp.concatenate([vblk[j][b], vblk[j][b + 1]], axis=0) for b, j in chains}
        rows, do_b = {}, {}
        for b, j in chains:
            heads = [GQ * j + g for g in range(GQ)]
            qrows = pl.ds(BLK * b, BLK)
            rows[b, j] = (jnp.concatenate([q_ref[qrows, pl.ds(HD * h, HD)] for h in heads], axis=0).astype(f32),
                          jnp.concatenate([jnp.broadcast_to(sk_ref[layer:layer + 1, h:h + 1], (BLK, 1)) for h in heads], axis=0))
            do_b[b, j] = jnp.concatenate([do_ref[qrows, pl.ds(HD * h, HD)] for h in heads], axis=0).astype(bf16)
        qnorm = {c: _head_norm(rows[c][0], qg) for c in chains}
        qn_b = {c: qnorm[c][2].astype(bf16) for c in chains}
        ss = {(b, j): _dot(qn_b[b, j], kn_b[b, j], NT_DIMS) * (HD ** -0.5) + bias_ref[GQ * j:GQ * (j + 1)].reshape(GQ * BLK, 2 * BLK)
              for b, j in chains}
        sm = {(b, j): _softmax_with_sink(jnp.where(masks[b], ss[b, j], -jnp.inf), rows[b, j][1]) for b, j in chains}
        dps = {c: _dot(do_b[c], vbs[c], NT_DIMS) for c in chains}
        deltas = {c: jnp.sum(sm[c][0] * dps[c], axis=-1, keepdims=True) for c in chains}
        dss = {c: sm[c][0] * (dps[c] - deltas[c]) for c in chains}
        ds_b = {c: (dss[c] * (HD ** -0.5)).astype(bf16) for c in chains}
        dqn = {c: _dot(ds_b[c], kn_b[c], NN_DIMS) for c in chains}
        dkn = {c: _dot(ds_b[c], qn_b[c], TN_DIMS) for c in chains}
        dvs = {c: _dot(sm[c][0].astype(bf16), do_b[c], TN_DIMS) for c in chains}
        dqg = jnp.zeros((1, HD), f32)
        dkg = jnp.zeros((1, HD), f32)
        dsink = jnp.zeros((1, 128), f32)
        for b, j in chains:
            dbias_ref[GQ * j:GQ * (j + 1)] += dss[b, j].reshape(GQ, BLK, 2 * BLK)
            dsk = sm[b, j][1] * deltas[b, j]
            for g in range(GQ):
                dsink = dsink + jnp.where(lane == GQ * j + g, -_sum11(dsk[BLK * g:BLK * (g + 1), :]), 0.0)
            qhat, rq, _ = qnorm[b, j]
            w = dqn[b, j] * qg
            dq = rq * (w - qhat * _row_mean(qhat * w))
            for g in range(GQ):
                dqkv_ref[pl.ds(BLK * b, BLK), pl.ds(HD * (GQ * j + g), HD)] = dq[BLK * g:BLK * (g + 1), :].astype(bf16)
            dqg = dqg + jnp.sum(dqn[b, j] * qhat, axis=0, keepdims=True)
        for j in grp:
            dkn_t = [dkn[0, j][:BLK, :], dkn[0, j][BLK:, :] + dkn[1, j][:BLK, :], dkn[1, j][BLK:, :]]
            dv_t = [dvs[0, j][:BLK, :], dvs[0, j][BLK:, :] + dvs[1, j][:BLK, :], dvs[1, j][BLK:, :]]
            dk_t = []
            for t in range(3):
                khat, rk, _ = knorm[j][t]
                w = dkn_t[t] * kg
                dk_t.append(rk * (w - khat * _row_mean(khat * w)))
                dkg = dkg + jnp.sum(dkn_t[t] * khat, axis=0, keepdims=True)
            kcols, vcols = pl.ds(D_ATTN + HD * j, HD), pl.ds(D_ATTN + 128 + HD * j, HD)
            dqkv_ref[BLK:, kcols] = (dk_t[2] + carry[:, pl.ds(HD * j, HD)]).astype(bf16)
            dqkv_ref[BLK:, vcols] = (dv_t[2] + carry[:, pl.ds(128 + HD * j, HD)]).astype(bf16)
            dqkv_ref[0:BLK, kcols] = dk_t[1].astype(bf16)
            dqkv_ref[0:BLK, vcols] = dv_t[1].astype(bf16)
            carry[:, pl.ds(HD * j, HD)] = dk_t[0]
            carry[:, pl.ds(128 + HD * j, HD)] = dv_t[0]
        dsm_ref[0:1, 0:HD] += dqg
        dsm_ref[1:2, 0:HD] += dkg
        dsm_ref[2:3, :] += dsink

    rev = lambda i: NBLK // 2 - 1 - i
    prev = lambda i: jnp.maximum(NBLK - 3 - 2 * i, 0)
    small = lambda shape: pl.BlockSpec(shape, lambda i: (0,) * len(shape))
    return pl.pallas_call(
        body, grid=(NBLK // 2,),
        in_specs=[pl.BlockSpec((2 * BLK, D_ATTN), lambda i: (rev(i), 0)),
                  pl.BlockSpec((2 * BLK, 128), lambda i: (rev(i), 4)), pl.BlockSpec((BLK, 128), lambda i: (prev(i), 4)),
                  pl.BlockSpec((2 * BLK, 128), lambda i: (rev(i), 5)), pl.BlockSpec((BLK, 128), lambda i: (prev(i), 5)),
                  pl.BlockSpec((2 * BLK, D_ATTN), lambda i: (rev(i), 0)),
                  small((DEPTH, HD)), small((DEPTH, HD)), small((DEPTH, NQ)), small((NQ, BLK, 2 * BLK))] + [ANY_SPEC] * len(deps),
        out_specs=[pl.BlockSpec((2 * BLK, 768), lambda i: (rev(i), COL_QKV // 768)), small((NQ, BLK, 2 * BLK)), small((8, 128))],
        out_shape=_out_hbm([SDS((S, D_IN_PAD), bf16), SDS((NQ, BLK, 2 * BLK), f32), SDS((8, 128), f32)]),
        scratch_shapes=[pltpu.VMEM((BLK, 256), f32)], name="attn_bwd", compiler_params=_cparams(1),
    )(*_in_hbm([qkv, qkv, qkv, qkv, qkv, dmix, q_gain, k_gain, sinks, bias, *deps]))


CONV_TC = 256


def _shift_down(u, s):
    if s == 0:
        return u
    rows = lax.broadcasted_iota(jnp.int32, u.shape, 0)
    return jnp.where(rows >= s, pltpu.roll(u, s, 0), 0.0)


def _shift_up(u, s):
    if s == 0:
        return u
    rows = lax.broadcasted_iota(jnp.int32, u.shape, 0)
    return jnp.where(rows < u.shape[0] - s, pltpu.roll(u, u.shape[0] - s, 0), 0.0)


def _conv_specs():
    return [pl.BlockSpec((S, CONV_TC), lambda c: (0, c)),
            pl.BlockSpec((None, 4, CONV_TC), lambda c: (0, 0, c)),
            pl.BlockSpec((DEPTH, CONV_TC), lambda c: (0, c))]


def _conv_pre(u, w_ref, b_ref, layer):
    pre = b_ref[layer:layer + 1, :] + w_ref[3:4, :] * u
    for k in range(3):
        pre = pre + w_ref[k:k + 1, :] * _shift_down(u, 3 - k)
    return pre


def _conv_fwd(xbc, conv_w, conv_b, layer):
    def body(u_ref, w_ref, b_ref, o_ref):
        pre = _conv_pre(u_ref[...].astype(f32), w_ref, b_ref, layer)
        o_ref[...] = pre * _sigmoid(pre)

    specs = _conv_specs()
    specs[1] = pl.BlockSpec((None, 4, CONV_TC), lambda c: (layer, 0, c))
    return pl.pallas_call(
        body, grid=(D_CONV // CONV_TC,), in_specs=specs, out_specs=pl.BlockSpec((S, CONV_TC), lambda c: (0, c)),
        out_shape=_out_hbm(SDS((S, D_CONV), f32)), name="conv_fwd", compiler_params=_cparams(1),
    )(*_in_hbm([xbc, conv_w, conv_b]))


def _conv_bwd(xbc, dact, conv_w, conv_b, dproj, layer):
    def body(u_ref, w_ref, b_ref, da_ref, dproj_in, du_ref, dw_ref, db_ref):
        u = u_ref[...].astype(f32)
        pre = _conv_pre(u, w_ref, b_ref, layer)
        sg = _sigmoid(pre)
        dpre = da_ref[...] * (sg * (1.0 + pre * (1.0 - sg)))
        du = w_ref[3:4, :] * dpre
        for k in range(3):
            du = du + w_ref[k:k + 1, :] * _shift_up(dpre, 3 - k)
        du_ref[...] = du.astype(bf16)
        db_ref[...] = jnp.broadcast_to(jnp.sum(dpre, axis=0, keepdims=True), db_ref.shape)
        dw_ref[...] = jnp.zeros_like(dw_ref)
        for k in range(4):
            dw_ref[k:k + 1, :] = jnp.sum(dpre * _shift_down(u, 3 - k), axis=0, keepdims=True)

    specs = _conv_specs()
    specs[1] = pl.BlockSpec((None, 4, CONV_TC), lambda c: (layer, 0, c))
    col = pl.BlockSpec((S, CONV_TC), lambda c: (0, c))
    row8 = pl.BlockSpec((8, CONV_TC), lambda c: (0, c))
    return pl.pallas_call(
        body, grid=(D_CONV // CONV_TC,), in_specs=[*specs, col, ANY_SPEC],
        out_specs=[pl.BlockSpec((S, CONV_TC), lambda c: (0, COL_XBC // CONV_TC + c)), row8, row8],
        out_shape=_out_hbm([SDS((S, D_IN_PAD), bf16), SDS((8, D_CONV), f32), SDS((8, D_CONV), f32)]), name="conv_bwd",
        input_output_aliases={4: 0}, compiler_params=_cparams(1),
    )(*_in_hbm([xbc, conv_w, conv_b, dact, dproj]))


def _tri():
    return (lax.broadcasted_iota(jnp.int32, (BLK, BLK), 0) >= lax.broadcasted_iota(jnp.int32, (BLK, BLK), 1))


def _ssd_scalars(dt_ref, dtb_ref, alog_ref, layer):
    raw = dt_ref[:, 0:NSSM] + dtb_ref[layer:layer + 1, :]
    dtv = jnp.maximum(raw, 0.0) + jnp.log(1.0 + jnp.exp(-jnp.abs(raw)))
    a = -jnp.exp(alog_ref[layer:layer + 1, :])
    acs = jnp.dot(_tri().astype(f32), dtv * a, preferred_element_type=f32, precision=HIGHEST)
    return raw, dtv, a, acs


HG = NSSM // NGRP
GW = HG * HD


def _lane_expand(cols, g):
    lane_head = lax.broadcasted_iota(jnp.int32, (1, GW), 1) // HD
    out = cols[:, HG * g + HG - 1:HG * g + HG]
    for r in range(HG - 2, -1, -1):
        out = jnp.where(lane_head == r, cols[:, HG * g + r:HG * g + r + 1], out)
    return out


def _row_expand(vals, g):
    row_head = lax.broadcasted_iota(jnp.int32, (GW, 1), 0) // HD
    out = vals[:, HG * g + HG - 1:HG * g + HG]
    for r in range(HG - 2, -1, -1):
        out = jnp.where(row_head == r, vals[:, HG * g + r:HG * g + r + 1], out)
    return out


def _head_rowsums(a, g):
    sel = (lax.broadcasted_iota(jnp.int32, (GW, NSSM), 0) // HD + HG * g == lax.broadcasted_iota(jnp.int32, (GW, NSSM), 1)).astype(bf16)
    hi = a.astype(bf16)
    lo = (a - hi.astype(f32)).astype(bf16)
    return _dot(hi, sel, NN_DIMS) + _dot(lo, sel, NN_DIMS)


def _head_blocksums(v, g):
    sel = (lax.broadcasted_iota(jnp.int32, (GW, NSSM), 0) // HD + HG * g == lax.broadcasted_iota(jnp.int32, (GW, NSSM), 1)).astype(bf16)
    hi = v.astype(bf16)
    lo = (v - hi.astype(f32)).astype(bf16)
    return _dot(hi, sel, TN_DIMS) + _dot(lo, sel, TN_DIMS)


def _ssd_chunk_common(xc_ref, dt_ref, dtb_ref, alog_ref, h_rows, layer):
    raw, dtv, a, acs = _ssd_scalars(dt_ref, dtb_ref, alog_ref, layer)
    acs_t = acs.T
    last = acs[BLK - 1:BLK, :]
    c = dict(raw=raw, dtv=dtv, a=a, acs=acs, last=last, dte=jnp.exp(last - acs), e_all=jnp.exp(acs), cd=jnp.exp(last))
    grp, heads, tri = range(NGRP), range(NSSM), _tri()
    c["bm"] = [xc_ref[:, pl.ds(D_SSM + NSTATE * g, NSTATE)] for g in grp]
    c["bm_b"] = [c["bm"][g].astype(bf16) for g in grp]
    c["cm_b"] = [xc_ref[:, pl.ds(D_SSM + NGRP * NSTATE + NSTATE * g, NSTATE)].astype(bf16) for g in grp]
    c["cb"] = [_dot(c["cm_b"][g], c["bm_b"][g], NT_DIMS) for g in grp]
    c["x"] = [xc_ref[:, pl.ds(GW * g, GW)] for g in grp]
    c["dt"] = [_lane_expand(dtv, g) for g in grp]
    c["xdt"] = [c["x"][g] * c["dt"][g] for g in grp]
    c["xdt_b"] = [c["xdt"][g].astype(bf16) for g in grp]
    c["prev"] = [h_rows(g) for g in grp]
    c["prev_b"] = [c["prev"][g].astype(bf16) for g in grp]
    c["e"] = [_lane_expand(c["e_all"], g) for g in grp]
    c["y_off"] = [_dot(c["cm_b"][g], c["prev_b"][g], NT_DIMS) * c["e"][g] for g in grp]
    c["decay"] = [jnp.exp(jnp.where(tri, acs[:, h:h + 1] - acs_t[h:h + 1, :], -jnp.inf)) for h in heads]
    c["m"] = [c["cb"][h // HG] * c["decay"][h] for h in heads]
    c["m_b"] = [c["m"][h].astype(bf16) for h in heads]
    c["dte_x"] = [_lane_expand(c["dte"], g) for g in grp]
    c["xdte_b"] = [(c["xdt"][g] * c["dte_x"][g]).astype(bf16) for g in grp]
    return c


def _ssd_fwd(xact, z, dt, attn, dt_bias, a_log, d_skip, norm_g, layer):
    def body(xc_ref, z_ref, dt_ref, at_ref, dtb_ref, alog_ref, dsk_ref, ng_ref, mix_ref, hs_ref, y_ref, h_ref):
        n = pl.program_id(0)

        @pl.when(n == 0)
        def _():
            h_ref[...] = jnp.zeros_like(h_ref)

        hs_ref[...] = h_ref[...]
        c = _ssd_chunk_common(xc_ref, dt_ref, dtb_ref, alog_ref, lambda g: h_ref[pl.ds(GW * g, GW), :], layer)
        grp, heads = range(NGRP), range(NSSM)
        y_diag = [_dot(c["m_b"][h], c["xdt_b"][h // HG][:, HD * (h % HG):HD * (h % HG + 1)], NN_DIMS) for h in heads]
        new_st = [_dot(c["xdte_b"][g], c["bm_b"][g], TN_DIMS) for g in grp]
        for h in heads:
            y_ref[:, pl.ds(HD * h, HD)] = y_diag[h]
        dskip = dsk_ref[layer:layer + 1, :]
        for g in grp:
            cols = pl.ds(GW * g, GW)
            y_ref[:, cols] = y_ref[:, cols] + c["y_off"][g] + c["x"][g] * _lane_expand(dskip, g)
            h_ref[cols, :] = c["prev"][g] * _row_expand(c["cd"], g) + new_st[g]
        zv = z_ref[...].astype(f32)
        yz = y_ref[...] * (zv * _sigmoid(zv))
        mix_ref[:, 0:D_ATTN] = at_ref[...]
        for g in grp:
            yg = yz[:, GW * g:GW * (g + 1)]
            rs = lax.rsqrt(jnp.mean(yg * yg, axis=-1, keepdims=True) + EPS)
            mix_ref[:, D_ATTN + GW * g:D_ATTN + GW * (g + 1)] = (yg * rs * ng_ref[layer:layer + 1, GW * g:GW * (g + 1)]).astype(bf16)

    small = lambda shape: pl.BlockSpec(shape, lambda n: (0,) * len(shape))
    return pl.pallas_call(
        body, grid=(NBLK,),
        in_specs=[pl.BlockSpec((BLK, D_CONV), lambda n: (n, 0)), pl.BlockSpec((BLK, D_SSM), lambda n: (n, 0)),
                  pl.BlockSpec((BLK, 128), lambda n: (n, 0)), pl.BlockSpec((BLK, D_ATTN), lambda n: (n, 0)),
                  small((DEPTH, NSSM)), small((DEPTH, NSSM)), small((DEPTH, NSSM)), small((DEPTH, D_SSM))],
        out_specs=[pl.BlockSpec((BLK, D), lambda n: (n, 0)), pl.BlockSpec((None, NSSM * HD, NSTATE), lambda n: (n, 0, 0)),
                   pl.BlockSpec((BLK, D_SSM), lambda n: (n, 0))],
        out_shape=_out_hbm([SDS((S, D), bf16), SDS((NBLK, NSSM * HD, NSTATE), f32), SDS((S, D_SSM), f32)]),
        scratch_shapes=[pltpu.VMEM((NSSM * HD, NSTATE), f32)],
        name="ssd_fwd", compiler_params=_cparams(1),
    )(*_in_hbm([xact, z, dt, attn, dt_bias, a_log, d_skip, norm_g]))


def _ssd_bwd(xact, z, dt, dmix, hs, y, dt_bias, a_log, d_skip, norm_g, dproj, layer):
    def body(xc_ref, z_ref, dt_ref, do_ref, hs_ref, y_ref, dtb_ref, alog_ref, dsk_ref, ng_ref, dproj_in,
             dzdt_ref, dx_ref, dsm_ref, dh_ref, dy_ref):
        i = pl.program_id(0)

        @pl.when(i == 0)
        def _():
            dh_ref[...] = jnp.zeros_like(dh_ref)
            dsm_ref[...] = jnp.zeros_like(dsm_ref)

        c = _ssd_chunk_common(xc_ref, dt_ref, dtb_ref, alog_ref, lambda g: hs_ref[pl.ds(GW * g, GW), :], layer)
        raw, dtv, a = c["raw"], c["dtv"], c["a"]
        grp, heads = range(NGRP), range(NSSM)
        dskip = dsk_ref[layer:layer + 1, :]
        lane8 = lax.broadcasted_iota(jnp.int32, (1, NSSM), 1)
        sub8 = lax.broadcasted_iota(jnp.int32, (NSSM, 1), 0)

        zv = z_ref[...].astype(f32)
        sz = _sigmoid(zv)
        gz = zv * sz
        yv = y_ref[...]
        yz = yv * gz
        for g in grp:
            sl = slice(GW * g, GW * (g + 1))
            yg = yz[:, sl]
            rs = lax.rsqrt(jnp.mean(yg * yg, axis=-1, keepdims=True) + EPS)
            yhat = yg * rs
            dog = do_ref[:, sl]
            w = dog * ng_ref[layer:layer + 1, sl]
            dyz = rs * (w - yhat * jnp.mean(yhat * w, axis=-1, keepdims=True))
            dsm_ref[0:1, sl] += jnp.sum(dog * yhat, axis=0, keepdims=True)
            dy_ref[:, sl] = dyz * gz[:, sl]
            dzdt_ref[:, sl] = (dyz * yv[:, sl] * (sz[:, sl] * (1.0 + zv[:, sl] * (1.0 - sz[:, sl])))).astype(bf16)

        dy = [dy_ref[:, pl.ds(GW * g, GW)] for g in grp]
        dy_b = [dy[g].astype(bf16) for g in grp]
        hl = lambda h: slice(HD * (h % HG), HD * (h % HG + 1))
        dt_off_b = [(dy[g] * c["e"][g]).astype(bf16) for g in grp]
        dcm = [_dot(dt_off_b[g], c["prev_b"][g], NN_DIMS) for g in grp]
        dprev = [_dot(dt_off_b[g], c["cm_b"][g], TN_DIMS) for g in grp]
        yoff_rs = [_head_rowsums(dy[g] * c["y_off"][g], g) for g in grp]
        dhn = [dh_ref[pl.ds(GW * g, GW), :] for g in grp]
        dhn_b = [dhn[g].astype(bf16) for g in grp]
        dprev = [dprev[g] + dhn[g] * _row_expand(c["cd"], g) for g in grp]
        dhn_prev = [dhn[g] * c["prev"][g] for g in grp]
        u = [_dot(c["bm_b"][g], dhn_b[g], NT_DIMS) for g in grp]
        dbm = [_dot(c["xdte_b"][g], dhn_b[g], NN_DIMS) for g in grp]
        ddte_rs = [_head_rowsums(c["xdt"][g] * u[g], g) for g in grp]
        dm = [_dot(dy_b[h // HG][:, hl(h)], c["xdt_b"][h // HG][:, hl(h)], NT_DIMS) for h in heads]
        dxdt_in = [_dot(c["m_b"][h], dy_b[h // HG][:, hl(h)], TN_DIMS) for h in heads]
        dseg = [dm[h] * c["m"][h] for h in heads]
        dmd = [dm[h] * c["decay"][h] for h in heads]
        for h in heads:
            dx_ref[:, pl.ds(HD * h, HD)] = dxdt_in[h]

        tmp = (ddte_rs[0] + ddte_rs[1]) * c["dte"]
        dacs = yoff_rs[0] + yoff_rs[1] - tmp
        dacs_cols = jnp.zeros((NSSM, BLK), f32)
        ddtv = jnp.zeros((BLK, NSSM), f32)
        ddsk = jnp.zeros((BLK, NSSM), f32)
        hp = jnp.zeros((1, NSSM), f32)
        for g in grp:
            cols = pl.ds(GW * g, GW)
            dxdt = dx_ref[:, cols] + u[g] * c["dte_x"][g]
            dx_ref[:, cols] = dy[g] * _lane_expand(dskip, g) + dxdt * c["dt"][g]
            ddtv = ddtv + _head_rowsums(dxdt * c["x"][g], g)
            ddsk = ddsk + _head_rowsums(dy[g] * c["x"][g], g)
            dcb = dmd[HG * g]
            for r in range(1, HG):
                dcb = dcb + dmd[HG * g + r]
            dcb_b = dcb.astype(bf16)
            dx_ref[:, pl.ds(D_SSM + NSTATE * g, NSTATE)] = dbm[g] + _dot(dcb_b, c["cm_b"][g], TN_DIMS)
            dx_ref[:, pl.ds(D_SSM + NGRP * NSTATE + NSTATE * g, NSTATE)] = dcm[g] + _dot(dcb_b, c["bm_b"][g], NN_DIMS)
            dh_ref[cols, :] = dprev[g]
            hp = hp + _head_blocksums(jnp.sum(dhn_prev[g], axis=1, keepdims=True), g)
            for r in range(HG):
                h = HG * g + r
                dacs = dacs + (lane8 == h).astype(f32) * jnp.sum(dseg[h], axis=1, keepdims=True)
                dacs_cols = dacs_cols + (sub8 == h).astype(f32) * jnp.sum(dseg[h], axis=0, keepdims=True)
        dlast = hp * c["cd"] + jnp.sum(tmp, axis=0, keepdims=True)
        ddsk = jnp.sum(ddsk, axis=0, keepdims=True)

        row = lax.broadcasted_iota(jnp.int32, (BLK, 1), 0)
        dacs = dacs - dacs_cols.T + jnp.where(row == BLK - 1, dlast, 0.0)
        dda = lax.dot_general(_tri().astype(f32), dacs, TN_DIMS, preferred_element_type=f32, precision=HIGHEST)
        ddtv = ddtv + dda * a
        da = jnp.sum(dda * dtv, axis=0, keepdims=True)
        draw = ddtv * _sigmoid(raw)
        dzdt_ref[:, D_SSM:] = jnp.zeros((BLK, COL_XBC - COL_DT), bf16)
        dzdt_ref[:, D_SSM:D_SSM + NSSM] = draw.astype(bf16)
        dsm_ref[1:2, 0:NSSM] += jnp.sum(draw, axis=0, keepdims=True)
        dsm_ref[2:3, 0:NSSM] += da * a
        dsm_ref[3:4, 0:NSSM] += ddsk

    rev = lambda i: NBLK - 1 - i
    small = lambda shape: pl.BlockSpec(shape, lambda i: (0,) * len(shape))
    return pl.pallas_call(
        body, grid=(NBLK,),
        in_specs=[pl.BlockSpec((BLK, D_CONV), lambda i: (rev(i), 0)), pl.BlockSpec((BLK, D_SSM), lambda i: (rev(i), 0)),
                  pl.BlockSpec((BLK, 128), lambda i: (rev(i), 0)), pl.BlockSpec((BLK, D_SSM), lambda i: (rev(i), 1)),
                  pl.BlockSpec((None, NSSM * HD, NSTATE), lambda i: (rev(i), 0, 0)), pl.BlockSpec((BLK, D_SSM), lambda i: (rev(i), 0)),
                  small((DEPTH, NSSM)), small((DEPTH, NSSM)), small((DEPTH, NSSM)), small((DEPTH, D_SSM)), ANY_SPEC],
        out_specs=[pl.BlockSpec((BLK, COL_XBC - COL_Z), lambda i: (rev(i), COL_Z // (COL_XBC - COL_Z))),
                   pl.BlockSpec((BLK, D_CONV), lambda i: (rev(i), 0)), small((8, D_SSM))],
        out_shape=_out_hbm([SDS((S, D_IN_PAD), bf16), SDS((S, D_CONV), f32), SDS((8, D_SSM), f32)]),
        scratch_shapes=[pltpu.VMEM((NSSM * HD, NSTATE), f32), pltpu.VMEM((BLK, D_SSM), f32)],
        name="ssd_bwd", input_output_aliases={10: 0}, compiler_params=_cparams(1),
    )(*_in_hbm([xact, z, dt, dmix, hs, y, dt_bias, a_log, d_skip, norm_g, dproj]))


def _my_place():
    return lax.axis_index("x"), lax.axis_index("y"), lax.axis_index("c")


def _dev_index(px, py, pc):
    return 4 * px + 2 * py + pc


def _slab2(kind, ref, idx):
    if kind == "stack":
        return ref.at[idx]
    if kind == "rows128":
        return ref.at[pl.ds(pl.multiple_of(idx * 128, 128), 128), :]
    if kind == "rows512":
        return ref.at[pl.ds(pl.multiple_of(idx * 512, 512), 512), :]
    return ref.at[:, pl.ds(pl.multiple_of(idx * 512, 512), 512)]


def _slab_shape(kind, full_shape):
    if kind == "stack":
        return tuple(full_shape[1:])
    if kind == "rows128":
        return (128, full_shape[1])
    if kind == "rows512":
        return (512, full_shape[1])
    return (full_shape[0], 512)


KIND = dict(w_in="stack", w_out="rows128", w_up="cols512", w_down="rows512", conv_w="stack")
FULL_SHAPE = dict(w_in=(N_DEV, D, D_IN // N_DEV), w_out=(D, D), w_up=(D, D_FF), w_down=(D_FF, D))
HBM_SPEC = pl.BlockSpec(memory_space=pltpu.HBM)
SEM_SPEC = pl.BlockSpec(memory_space=pltpu.SEMAPHORE)
SIDE_EFFECT = pltpu.SideEffectType.DATAFLOW_SIDE_EFFECTING


def _peers_all():
    x, y, c = _my_place()
    return [(x ^ ((r >> 2) & 1), y ^ ((r >> 1) & 1), c ^ (r & 1)) for r in range(1, N_DEV)]


def _split_start(name, bufs, n_copies, plan, deps=()):
    nb = len(bufs)

    def body(*refs):
        ins = refs[:nb]
        send_sems, recv_sems = refs[nb + len(deps)], refs[nb + len(deps) + 1]
        token = refs[-1]
        for i, (src, dst, dev) in enumerate(plan(ins)):
            pltpu.make_async_remote_copy(src_ref=src, dst_ref=dst, send_sem=send_sems.at[i], recv_sem=recv_sems.at[i],
                                         device_id=dev, device_id_type=MESH).start()
        token[...] = jnp.zeros_like(token)

    outs = pl.pallas_call(
        body, name=name,
        out_shape=(pltpu.SemaphoreType.DMA((n_copies,)), pltpu.SemaphoreType.DMA((n_copies,)),
                   *[pltpu.HBM(b.shape, b.dtype) for b in bufs], SDS((8, 128), f32)),
        in_specs=[HBM_SPEC] * nb + [ANY_SPEC] * len(deps),
        out_specs=(SEM_SPEC, SEM_SPEC, *[HBM_SPEC] * nb, pl.BlockSpec(memory_space=pltpu.VMEM)),
        input_output_aliases={i: 2 + i for i in range(nb)},
        compiler_params=pltpu.CompilerParams(has_side_effects=SIDE_EFFECT),
    )(*[pltpu.with_memory_space_constraint(b, pltpu.HBM) for b in bufs], *deps)
    return dict(send=outs[0], recv=outs[1], bufs=list(outs[2:2 + nb]), token=outs[-1], plan=plan, n=n_copies)


def _split_wait(name, started, after):
    bufs = started["bufs"]
    nb = len(bufs)
    plan = started["plan"]

    def body(*refs):
        ins = refs[:nb]
        send_sems, recv_sems = refs[nb], refs[nb + 1]
        for i, (src, dst, dev) in enumerate(plan(ins)):
            cp = pltpu.make_async_remote_copy(src_ref=src, dst_ref=dst, send_sem=send_sems.at[i], recv_sem=recv_sems.at[i],
                                              device_id=dev, device_id_type=MESH)
            cp.wait_send()
            cp.wait_recv()

    outs = pl.pallas_call(
        body, name=name, out_shape=tuple(pltpu.HBM(b.shape, b.dtype) for b in bufs),
        in_specs=[HBM_SPEC] * nb + [SEM_SPEC, SEM_SPEC] + [ANY_SPEC] * len(after), out_specs=(HBM_SPEC,) * nb,
        input_output_aliases={i: i for i in range(nb)},
        compiler_params=pltpu.CompilerParams(has_side_effects=SIDE_EFFECT),
    )(*bufs, started["send"], started["recv"], *after)
    return list(outs)


def _gather_start(name, names, fulls, deps):
    n_t = len(names)

    def plan(refs):
        x, y, c = _my_place()
        my_idx = _dev_index(x, y, c)
        targets = [(x, y, 1 - c), (1 - x, y, c), (x, 1 - y, c), (1 - x, 1 - y, c)]
        slabs = [_slab2(KIND[names[t]], refs[t], my_idx) for t in range(n_t)]
        return [(slabs[t], slabs[t], dev) for t in range(n_t) for dev in targets]

    return _split_start(name, list(fulls), 4 * n_t, plan, deps)


def _gather_finish(name, names, started, after):
    n_t = len(names)
    fulls = _split_wait(name + "_wait", started, after)
    slab_shapes = [SDS(_slab_shape(KIND[n], f.shape), f.dtype) for n, f in zip(names, fulls)]

    def body(*refs):
        ins = refs[:n_t]
        outs = refs[n_t:2 * n_t]
        stage = refs[2 * n_t:3 * n_t]
        load_sems, send_sems, recv_sems = refs[3 * n_t:]
        x, y, c = _my_place()
        chips = [(1 - x, y), (x, 1 - y), (1 - x, 1 - y)]
        pairs = [(t, j) for t in range(n_t) for j in range(3)]
        loads = [pltpu.make_async_copy(_slab2(KIND[names[t]], ins[t], _dev_index(*chips[j], c)), stage[t].at[j], load_sems.at[t, j])
                 for t, j in pairs]
        for cp in loads:
            cp.start()

        def copy(t, j, core):
            return pltpu.make_async_remote_copy(
                src_ref=stage[t].at[j], dst_ref=_slab2(KIND[names[t]], outs[t], _dev_index(*chips[j], core)),
                send_sem=send_sems.at[t, j], recv_sem=recv_sems.at[t, j], device_id=(x, y, 1 - c), device_id_type=MESH)

        sends = [copy(t, j, c) for t, j in pairs]
        for ld, cp in zip(loads, sends):
            ld.wait()
            cp.start()
        for t, j in pairs:
            copy(t, j, 1 - c).wait_recv()
        for cp in sends:
            cp.wait_send()

    return pl.pallas_call(
        body, in_specs=[ANY_SPEC] * n_t, out_specs=[ANY_SPEC] * n_t, out_shape=[SDS(b.shape, b.dtype) for b in fulls],
        input_output_aliases={t: t for t in range(n_t)},
        scratch_shapes=[pltpu.VMEM((3,) + s.shape, s.dtype) for s in slab_shapes]
        + [pltpu.SemaphoreType.DMA((n_t, 3)), pltpu.SemaphoreType.DMA((n_t, 3)), pltpu.SemaphoreType.DMA((n_t, 3))],
        name=name + "_pass", compiler_params=pltpu.CompilerParams(vmem_limit_bytes=VMEM_LIMIT),
    )(*fulls)


def _exchange_start(name, names, grads, deps):
    n_t = len(names)
    lands = [lax.empty((N_DEV,) + _slab_shape(KIND[n], g.shape), g.dtype) for n, g in zip(names, grads)]

    def plan(refs):
        my_idx = _dev_index(*_my_place())
        return [(_slab2(KIND[names[t]], refs[t], _dev_index(*peer)), refs[n_t + t].at[my_idx], peer)
                for t in range(n_t) for peer in _peers_all()]

    return _split_start(name, list(grads) + lands, 7 * n_t, plan, deps)


def _small_exchange_start(part, deps):
    land = lax.empty((N_DEV,) + part.shape, part.dtype)

    def plan(refs):
        my_idx = _dev_index(*_my_place())
        return [(refs[0], refs[1].at[my_idx], peer) for peer in _peers_all()]

    return _split_start("small_exchange", [part, land], N_DEV - 1, plan, deps)


def _slab_pieces():
    sh = D_IN // N_DEV
    out = []
    for j in range(N_DEV):
        for first, end, dst in IN_SEGMENTS:
            lo, hi = max(first, sh * j), min(end, sh * (j + 1))
            if lo < hi:
                out.append((j, lo - sh * j, hi - sh * j, dst + lo - first))
    return out


def _w_in_assemble(stacked):
    tr = 256
    sh = D_IN // N_DEV

    def body(i_ref, o_ref):
        o_ref[:, COL_DT:COL_XBC] = jnp.zeros((tr, COL_XBC - COL_DT), bf16)
        for j, lo, hi, dst in _slab_pieces():
            o_ref[:, dst:dst + hi - lo] = i_ref[j, :, lo:hi]

    return pl.pallas_call(
        body, grid=(D // tr,), in_specs=[pl.BlockSpec((N_DEV, tr, sh), lambda i: (0, i, 0))],
        out_specs=pl.BlockSpec((None, tr, D_IN_PAD), lambda i: (0, i, 0)), out_shape=_out_hbm(SDS((1, D, D_IN_PAD), bf16)),
        name="w_in_assemble", compiler_params=_cparams(1),
    )(*_in_hbm([stacked]))


def _w_in_slabs(dw_in):
    tr = 256
    sh = D_IN // N_DEV

    def body(i_ref, o_ref):
        for j, lo, hi, src in _slab_pieces():
            o_ref[j, :, lo:hi] = i_ref[:, src:src + hi - lo]

    return pl.pallas_call(
        body, grid=(D // tr,), in_specs=[pl.BlockSpec((tr, D_IN_PAD), lambda i: (i, 0))],
        out_specs=pl.BlockSpec((N_DEV, tr, sh), lambda i: (0, i, 0)), out_shape=_out_hbm(SDS((N_DEV, D, sh), bf16)),
        name="w_in_slabs", compiler_params=_cparams(1),
    )(*_in_hbm([dw_in]))


SMALL_NAMES = ("mix_norm_g", "mlp_norm_g", "conv_b", "ssm_norm_g", "q_gain", "k_gain", "sinks", "dt_bias", "a_log", "d_skip",
               "rel_bias", "conv_w")
MISC_LANES = dict(q_gain=(LANE_QG, HD), k_gain=(LANE_KG, HD), sinks=(LANE_SINK, NQ), dt_bias=(LANE_DTB, NSSM),
                  a_log=(LANE_ALOG, NSSM), d_skip=(LANE_DSKIP, NSSM))


def _pack_small_grads(smalls, drel_t, loss):
    def body(*refs):
        o_ref = refs[-1]
        drel_ref, loss_ref = refs[-3], refs[-2]
        o_ref[...] = jnp.zeros_like(o_ref)
        for l in range(DEPTH):
            mixg, mlpg, convb, convw, ssd, attn = refs[6 * l:6 * l + 6]
            o_ref[ROW_MIXG + l:ROW_MIXG + l + 1, :] = mixg[...]
            o_ref[ROW_MLPG + l:ROW_MLPG + l + 1, :] = mlpg[...]
            o_ref[ROW_CONVB + l:ROW_CONVB + l + 1, :] = convb[0:1, :]
            o_ref[ROW_SSMG + l:ROW_SSMG + l + 1, 0:D_SSM] = ssd[0:1, :]
            o_ref[ROW_CONVW + 4 * l:ROW_CONVW + 4 * l + 4, :] = convw[0:4, :]
            row = slice(ROW_MISC + l, ROW_MISC + l + 1)
            o_ref[row, LANE_QG:LANE_QG + HD] = attn[0:1, 0:HD]
            o_ref[row, LANE_KG:LANE_KG + HD] = attn[1:2, 0:HD]
            o_ref[row, LANE_SINK:LANE_SINK + NQ] = attn[2:3, 0:NQ]
            o_ref[row, LANE_DTB:LANE_DTB + NSSM] = ssd[1:2, 0:NSSM]
            o_ref[row, LANE_ALOG:LANE_ALOG + NSSM] = ssd[2:3, 0:NSSM]
            o_ref[row, LANE_DSKIP:LANE_DSKIP + NSSM] = ssd[3:4, 0:NSSM]
        o_ref[ROW_RELB:ROW_RELB + NQ, 0:N_BUCKETS] = drel_ref[...]
        o_ref[ROW_LOSS:ROW_LOSS + 1, 0:1] = loss_ref[0:1, 0:1]

    args = []
    for sm in smalls:
        args += [sm["mix_norm_g"], sm["mlp_norm_g"], sm["conv_b"], sm["conv_w"], sm["ssd"], sm["attn"]]
    args += [drel_t, loss]
    return pl.pallas_call(body, out_shape=SDS((SMALL_ROWS, D), f32), name="pack_small_grads")(*args)


def _adamw_small(part, land, w, m, v):
    n = len(SMALL_NAMES)

    def grad_of(name, g_ref):
        if name == "mix_norm_g":
            return g_ref[ROW_MIXG:ROW_MIXG + DEPTH, :]
        if name == "mlp_norm_g":
            return g_ref[ROW_MLPG:ROW_MLPG + DEPTH, :]
        if name == "conv_b":
            return g_ref[ROW_CONVB:ROW_CONVB + DEPTH, :]
        if name == "ssm_norm_g":
            return g_ref[ROW_SSMG:ROW_SSMG + DEPTH, 0:D_SSM]
        if name == "rel_bias":
            return g_ref[ROW_RELB:ROW_RELB + NQ, 0:N_BUCKETS].T
        lane, width = MISC_LANES[name]
        return g_ref[ROW_MISC:ROW_MISC + DEPTH, lane:lane + width]

    def body(part_ref, land_ref, *refs):
        ws, ms, vs = refs[:n], refs[n:2 * n], refs[2 * n:3 * n]
        loss_ref = refs[3 * n]
        outs = refs[3 * n + 1:-1]
        g_ref = refs[-1]
        me = _dev_index(*_my_place())
        for p in range(N_DEV):
            term = jnp.where(me == p, part_ref[...], land_ref[p])
            if p == 0:
                g_ref[...] = term
            else:
                g_ref[...] += term
        loss_ref[...] = g_ref[ROW_LOSS:ROW_LOSS + 1, 0:128]
        my_cols = pl.ds(pl.multiple_of(me * 128, 128), 128)
        for k, name in enumerate(SMALL_NAMES):
            g_out, d_out, m_out, v_out = outs[4 * k:4 * k + 4]
            if name == "conv_w":
                for l in range(DEPTH):
                    g = g_ref[ROW_CONVW + 4 * l:ROW_CONVW + 4 * l + 4, my_cols]
                    delta, m_new, v_new = _adamw_math(ws[k][l], ms[k][l], vs[k][l], g)
                    g_out[l], d_out[l], m_out[l], v_out[l] = g, delta, m_new, v_new
            else:
                g = grad_of(name, g_ref)
                delta, m_new, v_new = _adamw_math(ws[k][...], ms[k][...], vs[k][...], g)
                g_out[...], d_out[...], m_out[...], v_out[...] = g, delta, m_new, v_new

    ws = [w[name] for name in SMALL_NAMES]
    out_shape = [SDS((1, 128), f32)]
    for a in ws:
        out_shape += [SDS(a.shape, f32)] * 4
    return pl.pallas_call(body, out_shape=out_shape, name="adamw_small", scratch_shapes=[pltpu.VMEM((SMALL_ROWS, D), f32)])(
        part, land, *ws, *[m[name] for name in SMALL_NAMES], *[v[name] for name in SMALL_NAMES])


def _plain(tm, tn):
    return pl.BlockSpec((tm, tn), lambda i, j, k: (i, j))


def _rowblk(tm, width):
    return pl.BlockSpec((tm, width), lambda i, j, k: (i, 0))


def _store_epi(dtype):
    def epi(acc, i, j, ex, outs):
        outs[0][...] = acc.astype(dtype)
    return epi


def _rms_prologue(layer):
    def pro(a_ref, ex, outs):
        xv = a_ref[...]
        r = lax.rsqrt(jnp.mean(xv * xv, axis=-1, keepdims=True) + EPS)
        h = (xv * r * ex[0][layer:layer + 1, :]).astype(bf16)
        outs[-1][...] = h
        return h
    return pro


MLP_TM = 256
MLP_VMEM = 56 * 1024 * 1024


def _resident(shape):
    return pl.BlockSpec((None,) + shape, lambda i: (0, 0, 0), pipeline_mode=pl.Buffered(1))


def _mlp_fwd(layer, x, mix, g, w_out, w_up, w_down, tgt=None):
    tm = MLP_TM
    with_loss = tgt is not None

    def body(x_ref, mix_ref, g_ref, wo_ref, wu_ref, wd_ref, *rest):
        xm_ref, a_ref, r_ref, h_ref = rest[with_loss:with_loss + 4]
        rest = rest[:with_loss] + rest[with_loss + 1:]
        i = pl.program_id(0)
        xv = x_ref[...] + _dot(mix_ref[...], wo_ref[...], NN_DIMS)
        xm_ref[...] = xv
        h = (xv * lax.rsqrt(jnp.mean(xv * xv, axis=-1, keepdims=True) + EPS) * g_ref[layer:layer + 1, :]).astype(bf16)
        h_ref[...] = h
        r = jnp.maximum(_dot(h, wu_ref[...], NN_DIMS), 0.0)
        a = (r * r).astype(bf16)
        a_ref[...] = a
        r_ref[...] = r.astype(bf16)
        y = xv + _dot(a, wd_ref[...], NN_DIMS)
        if not with_loss:
            rest[3][...] = y
            return
        err = y - rest[0][...]
        rest[4][...] = err * (1.0 / D)
        part = 0.5 * jnp.sum(jnp.mean(err * err, axis=-1, keepdims=True), axis=0, keepdims=True)

        @pl.when(i == 0)
        def _():
            rest[5][...] = jnp.zeros_like(rest[5])

        rest[5][...] += jnp.broadcast_to(part, rest[5].shape)

    row = lambda width: pl.BlockSpec((tm, width), lambda i: (i, 0))
    in_specs = [row(D), row(D), pl.BlockSpec((DEPTH, D), lambda i: (0, 0)), _resident((D, D)), _resident((D, D_FF)),
                _resident((D_FF, D))]
    out_specs = [row(D), row(D_FF), row(D_FF), row(D), row(D)]
    out_shape = [SDS((S, D), f32), SDS((S, D_FF), bf16), SDS((S, D_FF), bf16), SDS((S, D), bf16), SDS((S, D), f32)]
    args = [x, mix, g, w_out, w_up, w_down]
    if with_loss:
        in_specs.append(row(D))
        args.append(tgt)
        out_specs.append(pl.BlockSpec((1, 128), lambda i: (0, 0)))
        out_shape.append(SDS((1, 128), f32))
    return pl.pallas_call(
        body, grid=(S // tm,), in_specs=in_specs, out_specs=out_specs, out_shape=_out_hbm(out_shape),
        name="mlp_fwd_loss" if with_loss else "mlp_fwd",
        compiler_params=pltpu.CompilerParams(dimension_semantics=("arbitrary",), vmem_limit_bytes=MLP_VMEM),
    )(*_in_hbm(args[:3]), *args[3:6], *_in_hbm(args[6:]))


def _mlp_bwd_act(layer, dx_out, r_act, x_mid, g, w_down, w_up, w_out, deps):
    tm = MLP_TM

    def body(dxo_ref, r_ref, xm_ref, g_ref, wd_ref, wu_ref, wo_ref, *rest):
        du_ref, dx_ref, dg_ref, dmix_ref = rest[len(deps):]
        dxo = dxo_ref[...]
        du = (_dot(dxo.astype(bf16), wd_ref[...], NT_DIMS) * (2.0 * r_ref[...].astype(f32))).astype(bf16)
        du_ref[...] = du
        dh = _dot(du, wu_ref[...], NT_DIMS)
        _rms_bwd_epilogue(layer)(dh, pl.program_id(0), 0, (xm_ref, g_ref, dxo_ref), (dx_ref, dg_ref))
        dmix_ref[...] = _dot(dx_ref[...].astype(bf16), wo_ref[...], NT_DIMS)

    row = lambda width: pl.BlockSpec((tm, width), lambda i: (i, 0))
    return pl.pallas_call(
        body, grid=(S // tm,),
        in_specs=[row(D), row(D_FF), row(D), pl.BlockSpec((DEPTH, D), lambda i: (0, 0)), _resident((D_FF, D)), _resident((D, D_FF)),
                  _resident((D, D))] + [ANY_SPEC] * len(deps),
        out_specs=[row(D_FF), row(D), pl.BlockSpec((1, D), lambda i: (0, 0)), row(D)],
        out_shape=_out_hbm([SDS((S, D_FF), bf16), SDS((S, D), f32), SDS((1, D), f32), SDS((S, D), f32)]), name="mlp_bwd_act",
        compiler_params=pltpu.CompilerParams(dimension_semantics=("arbitrary",), vmem_limit_bytes=MLP_VMEM),
    )(*_in_hbm([dx_out, r_act, x_mid, g]), w_down, w_up, w_out, *_in_hbm(deps))


def _layer_fwd(l, x, p, get_weights, bias, tgt=None):
    wts = get_weights(l, "in", [x, bias])
    gfull = pl.BlockSpec((DEPTH, D), lambda i, j, k: (0, 0))
    tm = 256

    def inproj_epi(acc, i, j, ex, outs):
        outs[0][...] = acc[:, COL_QKV:COL_Z].astype(bf16)
        outs[1][...] = acc[:, COL_Z:COL_DT].astype(bf16)
        outs[2][...] = acc[:, COL_XBC:D_IN_PAD].astype(bf16)
        outs[3][...] = acc[:, COL_DT:COL_DT + 128]

    qkv, z, xbc, dt, h1 = _matmul(
        "in_proj", "nn", x, wts["w_in"], tm=tm, tn=D_IN_PAD, tk=D, prologue=_rms_prologue(l),
        extras=(p["mix_norm_g"],), extra_specs=(gfull,),
        out_shape=[SDS((S, 768), bf16), SDS((S, 512), bf16), SDS((S, 1024), bf16), SDS((S, 128), f32), SDS((S, D), bf16)],
        out_specs=[_rowblk(tm, 768), _rowblk(tm, 512), _rowblk(tm, 1024), _rowblk(tm, 128), _rowblk(tm, D)], epilogue=inproj_epi)
    attn = _attn_fwd(qkv, p["q_gain"], p["k_gain"], p["sinks"], bias, l)
    xact = _conv_fwd(xbc, wts["conv_w"], p["conv_b"], l)
    mix, hs, y_ssd = _ssd_fwd(xact, z, dt, attn, p["dt_bias"], p["a_log"], p["d_skip"], p["ssm_norm_g"], l)
    wts = dict(wts, **get_weights(l, "rest", [mix]))

    x_mid, a_act, r_act, h2, *result = _mlp_fwd(l, x, mix, p["mlp_norm_g"], wts["w_out"], wts["w_up"], wts["w_down"], tgt)
    saved = dict(x=x, h1=h1, qkv=qkv, z=z, xbc=xbc, dt=dt, xact=xact, mix=mix, hs=hs, y_ssd=y_ssd, x_mid=x_mid, h2=h2,
                 a=a_act, r=r_act, wts=wts)
    return (result[0] if tgt is None else tuple(result)), saved


def _layer_bwd(l, dx_out, sv, p, bias, deps, send):
    wts = sv["wts"]

    dw_down = _matmul("dw_down", "tn", sv["a"], dx_out, tm=1024, tn=D, tk=S, out_shape=SDS((D_FF, D), bf16),
                      out_specs=_plain(1024, D), epilogue=_store_epi(bf16), deps=deps)
    deps = send(l, dict(w_down=dw_down))
    du, dx_mid, dg_mlp, dmix = _mlp_bwd_act(l, dx_out, sv["r"], sv["x_mid"], p["mlp_norm_g"], wts["w_down"], wts["w_up"],
                                            wts["w_out"], deps)
    dw_up = _matmul("dw_up", "tn", sv["h2"], du, tm=D, tn=1024, tk=S, out_shape=SDS((D, D_FF), bf16),
                    out_specs=_plain(D, 1024), epilogue=_store_epi(bf16))
    dw_out = _matmul("dw_out", "tn", sv["mix"], dx_mid, tm=D, tn=512, tk=512, out_shape=SDS((D, D), bf16),
                     out_specs=_plain(D, 512), epilogue=_store_epi(bf16))
    deps = send(l, dict(w_up=dw_up, w_out=dw_out))
    gfull = pl.BlockSpec((DEPTH, D), lambda i, j, k: (0, 0))
    grow = pl.BlockSpec((1, D), lambda i, j, k: (0, 0))
    dproj, dbias, dsm_attn = _attn_bwd(sv["qkv"], dmix, p["q_gain"], p["k_gain"], p["sinks"], bias, l, deps)
    dproj, dxact, dsm_ssd = _ssd_bwd(sv["xact"], sv["z"], sv["dt"], dmix, sv["hs"], sv["y_ssd"], p["dt_bias"], p["a_log"],
                                     p["d_skip"], p["ssm_norm_g"], dproj, l)
    dproj, dconv_w, dconv_b = _conv_bwd(sv["xbc"], dxact, wts["conv_w"], p["conv_b"], dproj, l)
    dw_in = _matmul("dw_in", "tn", sv["h1"], dproj, tm=D, tn=640, tk=S, out_shape=SDS((D, D_IN_PAD), bf16),
                    out_specs=_plain(D, 640), epilogue=_store_epi(bf16))
    deps = send(l, dict(w_in=_w_in_slabs(dw_in)))
    dx, dg_mix = _matmul(
        "in_proj_dh", "nt", dproj, wts["w_in"], tm=256, tn=D, tk=D_IN_PAD, out_shape=[SDS((S, D), f32), SDS((1, D), f32)],
        out_specs=[_plain(256, D), grow], epilogue=_rms_bwd_epilogue(l),
        extras=(sv["x"], p["mix_norm_g"], dx_mid), extra_specs=(_plain(256, D), gfull, _plain(256, D)), deps=deps)
    small = dict(mix_norm_g=dg_mix, mlp_norm_g=dg_mlp, conv_w=dconv_w, conv_b=dconv_b, ssd=dsm_ssd, attn=dsm_attn, dbias=dbias)
    return dx, small, deps


def _local_step(x, tgt, p, get_weights, send):
    onehot_t = jnp.asarray(_onehot_buckets())
    bias = _bias_build(p["rel_bias"].T, onehot_t).reshape(NQ, BLK, 2 * BLK)
    saved = []
    h = x
    for l in range(DEPTH):
        h, sv = _layer_fwd(l, h, p, get_weights, bias, tgt if l == DEPTH - 1 else None)
        saved.append(sv)
    dx, loss = h
    smalls = [None] * DEPTH
    deps = ()
    for l in reversed(range(DEPTH)):
        dx, smalls[l], deps = _layer_bwd(l, dx, saved[l], p, bias, deps, send)
    drel_t = _bias_grad(smalls[0]["dbias"].reshape(NQ, -1), smalls[1]["dbias"].reshape(NQ, -1), onehot_t)
    return dx, _pack_small_grads(smalls, drel_t, loss)


WEIGHT_ORDER = ("mix_norm_g", "w_in", "q_gain", "k_gain", "sinks", "rel_bias", "conv_w", "conv_b", "dt_bias", "a_log", "d_skip",
                "ssm_norm_g", "w_out", "mlp_norm_g", "w_up", "w_down")


def kernel(x, mix_norm_g, w_in, q_gain, k_gain, sinks, rel_bias, conv_w, conv_b, dt_bias, a_log, d_skip, ssm_norm_g, w_out, mlp_norm_g, w_up, w_down, loss_target, m_mix_norm_g, m_w_in, m_q_gain, m_k_gain, m_sinks, m_rel_bias, m_conv_w, m_conv_b, m_dt_bias, m_a_log, m_d_skip, m_ssm_norm_g, m_w_out, m_mlp_norm_g, m_w_up, m_w_down, v_mix_norm_g, v_w_in, v_q_gain, v_k_gain, v_sinks, v_rel_bias, v_conv_w, v_conv_b, v_dt_bias, v_a_log, v_d_skip, v_ssm_norm_g, v_w_out, v_mlp_norm_g, v_w_up, v_w_down):
    w = dict(mix_norm_g=mix_norm_g, w_in=w_in, q_gain=q_gain, k_gain=k_gain, sinks=sinks, rel_bias=rel_bias, conv_w=conv_w,
             conv_b=conv_b, dt_bias=dt_bias, a_log=a_log, d_skip=d_skip, ssm_norm_g=ssm_norm_g, w_out=w_out,
             mlp_norm_g=mlp_norm_g, w_up=w_up, w_down=w_down)
    m = dict(mix_norm_g=m_mix_norm_g, w_in=m_w_in, q_gain=m_q_gain, k_gain=m_k_gain, sinks=m_sinks, rel_bias=m_rel_bias,
             conv_w=m_conv_w, conv_b=m_conv_b, dt_bias=m_dt_bias, a_log=m_a_log, d_skip=m_d_skip, ssm_norm_g=m_ssm_norm_g,
             w_out=m_w_out, mlp_norm_g=m_mlp_norm_g, w_up=m_w_up, w_down=m_w_down)
    v = dict(mix_norm_g=v_mix_norm_g, w_in=v_w_in, q_gain=v_q_gain, k_gain=v_k_gain, sinks=v_sinks, rel_bias=v_rel_bias,
             conv_w=v_conv_w, conv_b=v_conv_b, dt_bias=v_dt_bias, a_log=v_a_log, d_skip=v_d_skip, ssm_norm_g=v_ssm_norm_g,
             w_out=v_w_out, mlp_norm_g=v_mlp_norm_g, w_up=v_w_up, w_down=v_w_down)
    big = ("w_in", "w_out", "w_up", "w_down")

    my_idx = _dev_index(*_my_place()).astype(jnp.int32).reshape(1)

    fulls = {n: _cast_to_full("cast_" + n, w[n], KIND[n], FULL_SHAPE[n], my_idx, bf16) for n in big}
    conv_full = _cast_to_full("cast_conv_w", conv_w.reshape(1, DEPTH * 4, 128), "stack", (N_DEV, DEPTH * 4, 128), my_idx, f32)[0]
    rest = ["w_out", "w_up", "w_down"]
    g0 = _gather_start("gather0", ["w_in", "conv_w"], [fulls["w_in"][0], conv_full], ())
    g1 = _gather_start("gather1", rest, [fulls[n][0] for n in rest], (g0["token"],))
    g2 = _gather_start("gather2", ["w_in"], [fulls["w_in"][1]], (g1["token"],))
    g3 = _gather_start("gather3", rest, [fulls[n][1] for n in rest], (g2["token"],))
    held = {}
    flat = lambda a: a.reshape(a.shape[0] * a.shape[1], a.shape[2])
    adam_in = {n: (flat(w[n]), flat(m[n]), flat(v[n])) for n in big}

    def get_weights(l, part, after):
        if l == 0 and part == "in":
            full_in, full_conv = _gather_finish("gather0", ["w_in", "conv_w"], g0,
                                                list(after) + [g3["token"], adam_in["w_in"][1], adam_in["w_in"][2]])
            held["conv_w"] = jnp.transpose(full_conv.reshape(N_DEV, DEPTH, 4, 128), (1, 2, 0, 3)).reshape(DEPTH, 4, D_CONV)
            return dict(w_in=_w_in_assemble(full_in), conv_w=held["conv_w"])
        if part == "in":
            return dict(w_in=_w_in_assemble(_gather_finish("gather2", ["w_in"], g2, after)[0]), conv_w=held["conv_w"])
        full = _gather_finish("gather1" if l == 0 else "gather3", rest, g1 if l == 0 else g3, after)
        return {n: f[None] for n, f in zip(rest, full)}

    pending = []

    def send(l, grads):
        names = list(grads)
        started = _exchange_start("exchange%d_%s" % (l, names[0]), names, [grads[n] for n in names], ())
        pending.append((l, names, started))
        return (started["token"],)

    dx, small_part = _local_step(x.reshape(S, D), loss_target.reshape(S, D), w, get_weights, send)

    small = _small_exchange_start(small_part, ())
    tiles = dict(w_in=256, w_out=128, w_up=256, w_down=256)
    outs_of = {n: None for n in big}
    after = [dx, small["token"]]
    for l, names, started in pending:
        bufs = _split_wait("exchange%d_%s_wait" % (l, names[0]), started, after)
        for t, n in enumerate(names):
            outs_of[n] = _adamw_layer("adamw_%s%d" % (n, l), KIND[n], l, *adam_in[n],
                                      bufs[len(names) + t], bufs[t], my_idx, outs_of[n], tiles[n])
        after = [outs_of[names[-1]][0]]
    res = {n: [o.reshape(w[n].shape) for o in outs_of[n]] for n in big}
    small_part, small_land = _split_wait("small_exchange_wait", small, after)
    small_outs = _adamw_small(small_part, small_land, w, m, v)
    loss = small_outs[0][0, 0]
    for k, name in enumerate(SMALL_NAMES):
        res[name] = small_outs[1 + 4 * k:5 + 4 * k]

    result = [loss, dx.reshape(1, S, D)]
    for k in range(4):
        result += [res[name][k] for name in WEIGHT_ORDER]
    return tuple(result)
```

```python
import functools
import math

import numpy as np
import jax
import jax.numpy as jnp
from jax import lax
from jax.experimental import pallas as pl
from jax.experimental.pallas import tpu as pltpu

f32 = jnp.float32
bf16 = jnp.bfloat16
SDS = jax.ShapeDtypeStruct
MESH = pl.DeviceIdType.MESH
HIGHEST = lax.Precision.HIGHEST

S = 2048
D = 1024
DEPTH = 2
BLK = 128
NBLK = S // BLK
HD = 64
NQ = 8
NKV = 2
NSSM = 8
NGRP = 2
NSTATE = 128
D_ATTN = 512
D_SSM = 512
D_CONV = 1024
D_FF = 4096
D_IN = 2312
D_IN_PAD = 2560
COL_QKV, COL_Z, COL_DT, COL_XBC = 0, 768, 1280, 1536
IN_SEGMENTS = ((0, 1280, 0), (1280, 2304, COL_XBC), (2304, 2312, COL_DT))
N_BUCKETS = 32
EPS = 1e-6
N_DEV = 8
VMEM_LIMIT = 48 * 1024 * 1024

ADAM_LR = 0.001
ADAM_B1 = 0.9
ADAM_B2 = 0.999
ADAM_EPS = 1e-08
ADAM_WD = 0.01
ADAM_STEP = 10

NT_DIMS = (((1,), (1,)), ((), ()))
TN_DIMS = (((0,), (0,)), ((), ()))
NN_DIMS = (((1,), (0,)), ((), ()))

ROW_MIXG = 0
ROW_MLPG = 2
ROW_CONVB = 4
ROW_SSMG = 6
ROW_MISC = 8
ROW_RELB = 10
ROW_CONVW = 18
ROW_LOSS = 26
SMALL_ROWS = 32
LANE_QG, LANE_KG, LANE_SINK, LANE_DTB, LANE_ALOG, LANE_DSKIP = 0, 64, 128, 256, 384, 512


def _dot(a, b, dims):
    return lax.dot_general(a, b, dims, preferred_element_type=f32)


def _cparams(n_axes):
    return pltpu.CompilerParams(dimension_semantics=("arbitrary",) * n_axes, vmem_limit_bytes=VMEM_LIMIT)


def _sum11(v):
    return jnp.sum(jnp.sum(v, axis=1, keepdims=True), axis=0, keepdims=True)


def _sigmoid(v):
    return 1.0 / (1.0 + jnp.exp(-v))


ANY_SPEC = pl.BlockSpec(memory_space=pl.ANY)


def _in_hbm(args):
    return [pltpu.with_memory_space_constraint(a, pltpu.HBM) if a.size >= 65536 else a for a in args]


def _out_hbm(out_shape):
    one = lambda s: pltpu.HBM(s.shape, s.dtype) if math.prod(s.shape) >= 65536 else s
    return [one(s) for s in out_shape] if isinstance(out_shape, (list, tuple)) else one(out_shape)


def _matmul(name, mode, a, b, *, layer=0, tm, tn, tk, out_shape, out_specs, epilogue, extras=(), extra_specs=(), deps=(),
            prologue=None):
    extras = tuple(extras) + tuple(deps)
    extra_specs = tuple(extra_specs) + (ANY_SPEC,) * len(deps)
    if mode == "tn":
        t_dim, m_dim = a.shape
        n_dim = b.shape[1]
        grid = (m_dim // tm, n_dim // tn, t_dim // tk)
        a_spec = pl.BlockSpec((tk, tm), lambda i, j, k: (k, i))
        b_spec = pl.BlockSpec((tk, tn), lambda i, j, k: (k, j))
        dims = TN_DIMS
    elif mode == "nn":
        m_dim, k_dim = a.shape
        n_dim = b.shape[-1]
        grid = (m_dim // tm, n_dim // tn, k_dim // tk)
        a_spec = pl.BlockSpec((tm, tk), lambda i, j, k: (i, k))
        b_spec = pl.BlockSpec((None, tk, tn), lambda i, j, k: (layer, k, j))
        dims = NN_DIMS
    else:
        m_dim, k_dim = a.shape
        n_dim = b.shape[-2]
        grid = (m_dim // tm, n_dim // tn, k_dim // tk)
        a_spec = pl.BlockSpec((tm, tk), lambda i, j, k: (i, k))
        b_spec = pl.BlockSpec((None, tn, tk), lambda i, j, k: (layer, j, k))
        dims = NT_DIMS
    nk = grid[2]
    n_ex = len(extras)

    def body(a_ref, b_ref, *rest):
        ex = rest[:n_ex - len(deps)]
        outs = rest[n_ex:-1]
        acc = rest[-1]
        i = pl.program_id(0)
        j = pl.program_id(1)
        k = pl.program_id(2)
        lhs = a_ref[...].astype(bf16) if prologue is None else prologue(a_ref, ex, outs)
        part = _dot(lhs, b_ref[...].astype(bf16), dims)
        if nk == 1:
            epilogue(part, i, j, ex, outs)
        else:
            @pl.when(k == 0)
            def _():
                acc[...] = part

            @pl.when(k > 0)
            def _():
                acc[...] += part

            @pl.when(k == nk - 1)
            def _():
                epilogue(acc[...], i, j, ex, outs)

    return pl.pallas_call(
        body, grid=grid, in_specs=[a_spec, b_spec, *extra_specs], out_specs=out_specs, out_shape=_out_hbm(out_shape),
        scratch_shapes=[pltpu.VMEM((tm, tn) if nk > 1 else (8, 128), f32)], name=name, compiler_params=_cparams(3),
    )(*_in_hbm([a]), b, *_in_hbm(extras))


def _rms_bwd_epilogue(layer):
    def epi(acc, i, j, ex, outs):
        x_ref, g_ref, dres_ref = ex
        dx_ref, dg_ref = outs
        xv = x_ref[...]
        r = lax.rsqrt(jnp.mean(xv * xv, axis=-1, keepdims=True) + EPS)
        xhat = xv * r
        w = acc * g_ref[layer:layer + 1, :]
        dx_ref[...] = dres_ref[...] + r * (w - xhat * jnp.mean(xhat * w, axis=-1, keepdims=True))
        dg = jnp.sum(acc * xhat, axis=0, keepdims=True)

        @pl.when(i == 0)
        def _():
            dg_ref[...] = dg

        @pl.when(i > 0)
        def _():
            dg_ref[...] += dg
    return epi


def _own_slab_spec(kind, tr, cols, nblk):
    if kind == "stack":
        return pl.BlockSpec((None, tr, cols), lambda i, idx: (idx[0], i, 0))
    if kind == "cols512":
        return pl.BlockSpec((tr, cols), lambda i, idx: (i, idx[0]))
    return pl.BlockSpec((tr, cols), lambda i, idx: (idx[0] * nblk + i, 0))


def _cast_to_full(name, w, kind, full_shape, my_idx, dtype):
    n_layers, rows, cols = w.shape
    tr = min(rows, 256)
    nblk = rows // tr

    def body(idx_ref, w_ref, *o_refs):
        for l in range(n_layers):
            o_refs[l][...] = w_ref[l].astype(dtype)

    grid_spec = pltpu.PrefetchScalarGridSpec(
        num_scalar_prefetch=1, grid=(nblk,), in_specs=[pl.BlockSpec((n_layers, tr, cols), lambda i, idx: (0, i, 0))],
        out_specs=[_own_slab_spec(kind, tr, cols, nblk)] * n_layers)
    return pl.pallas_call(body, grid_spec=grid_spec, out_shape=_out_hbm([SDS(full_shape, dtype)] * n_layers), name=name,
                          compiler_params=_cparams(1))(*_in_hbm([my_idx, w]))


def _adamw_math(w, m, v, g):
    m_new = ADAM_B1 * m + (1.0 - ADAM_B1) * g
    v_new = ADAM_B2 * v + (1.0 - ADAM_B2) * (g * g)
    m_hat = m_new / (1.0 - ADAM_B1 ** ADAM_STEP)
    v_hat = v_new / (1.0 - ADAM_B2 ** ADAM_STEP)
    delta = -ADAM_LR * (m_hat / (jnp.sqrt(v_hat) + ADAM_EPS) + ADAM_WD * w)
    return delta, m_new, v_new


def _adamw_layer(name, kind, layer, w, m, v, land, g_full, my_idx, prev, tr):
    rows2, cols = w.shape
    rows = rows2 // DEPTH
    nblk = rows // tr
    own_spec = _own_slab_spec(kind, tr, cols, nblk)
    n_prev = 0 if prev is None else 4

    def body(idx_ref, w_ref, m_ref, v_ref, land_ref, own_ref, *rest):
        g_ref, d_ref, mo_ref, vo_ref = rest[n_prev:]
        me = idx_ref[0]
        g = None
        for p in range(N_DEV):
            part = jnp.where(me == p, own_ref[...], land_ref[p]).astype(f32)
            g = part if g is None else g + part
        delta, m_new, v_new = _adamw_math(w_ref[...], m_ref[...], v_ref[...], g)
        g_ref[...] = g
        d_ref[...] = delta
        mo_ref[...] = m_new
        vo_ref[...] = v_new

    blk = pl.BlockSpec((tr, cols), lambda i, idx: (layer * nblk + i, 0))
    grid_spec = pltpu.PrefetchScalarGridSpec(
        num_scalar_prefetch=1, grid=(nblk,),
        in_specs=[blk, blk, blk, pl.BlockSpec((N_DEV, tr, cols), lambda i, idx: (0, i, 0)), own_spec] + [ANY_SPEC] * n_prev,
        out_specs=[blk, blk, blk, blk])
    aliases = {} if prev is None else {6 + k: k for k in range(4)}
    return pl.pallas_call(
        body, grid_spec=grid_spec, out_shape=_out_hbm([SDS((rows2, cols), f32)] * 4), name=name, input_output_aliases=aliases,
        compiler_params=_cparams(1),
    )(*_in_hbm([my_idx, w, m, v, land, g_full, *([] if prev is None else prev)]))


def _bucket_table():
    qi = np.arange(BLK)[:, None]
    kj = np.arange(2 * BLK)[None, :]
    dist = qi + BLK - kj
    dcl = np.clip(dist, 0, None)
    max_exact = N_BUCKETS // 2
    d_f = np.maximum(dcl, 1).astype(np.float32)
    large = max_exact + (np.log(d_f / np.float32(max_exact)) / np.float32(math.log(128 / max_exact))
                         * np.float32(N_BUCKETS - max_exact)).astype(np.int32)
    large = np.minimum(large, N_BUCKETS - 1)
    bucket = np.where(dcl < max_exact, dcl, large)
    in_window = (dist >= 0) & (dist < BLK)
    return bucket.astype(np.int32), in_window


def _onehot_buckets():
    bucket, _ = _bucket_table()
    oh = (bucket.reshape(-1)[None, :] == np.arange(N_BUCKETS)[:, None]).astype(np.float32)
    return oh


def _bias_build(rel_bias_t, onehot_t):
    def body(r_ref, o_ref, out_ref):
        out_ref[...] = jnp.dot(r_ref[...], o_ref[...], preferred_element_type=f32, precision=HIGHEST)

    tn = 4096
    return pl.pallas_call(
        body, grid=(BLK * 2 * BLK // tn,),
        in_specs=[pl.BlockSpec((NQ, N_BUCKETS), lambda i: (0, 0)), pl.BlockSpec((N_BUCKETS, tn), lambda i: (0, i))],
        out_specs=pl.BlockSpec((NQ, tn), lambda i: (0, i)), out_shape=SDS((NQ, BLK * 2 * BLK), f32), name="bias_build",
        compiler_params=_cparams(1),
    )(rel_bias_t, onehot_t)


def _bias_grad(dbias0, dbias1, onehot_t):
    tn = 4096
    nsteps = BLK * 2 * BLK // tn

    def body(a_ref, b_ref, o_ref, out_ref):
        part = lax.dot_general(a_ref[...] + b_ref[...], o_ref[...], NT_DIMS, preferred_element_type=f32, precision=HIGHEST)

        @pl.when(pl.program_id(0) == 0)
        def _():
            out_ref[...] = part

        @pl.when(pl.program_id(0) > 0)
        def _():
            out_ref[...] += part

    return pl.pallas_call(
        body, grid=(nsteps,),
        in_specs=[pl.BlockSpec((NQ, tn), lambda i: (0, i)), pl.BlockSpec((NQ, tn), lambda i: (0, i)),
                  pl.BlockSpec((N_BUCKETS, tn), lambda i: (0, i))],
        out_specs=pl.BlockSpec((NQ, N_BUCKETS), lambda i: (0, 0)), out_shape=SDS((NQ, N_BUCKETS), f32), name="bias_grad",
        compiler_params=_cparams(1),
    )(dbias0, dbias1, onehot_t)


def _attn_mask(n):
    qi = lax.broadcasted_iota(jnp.int32, (BLK, 2 * BLK), 0)
    kj = lax.broadcasted_iota(jnp.int32, (BLK, 2 * BLK), 1)
    dist = qi + BLK - kj
    first_key = jnp.where(n > 0, 0, BLK)
    return (dist >= 0) & (dist < BLK) & (kj >= first_key)


def _row_mean(a):
    return jnp.mean(a, axis=-1, keepdims=True)


def _head_norm(t, gain):
    r = lax.rsqrt(_row_mean(t * t) + EPS)
    that = t * r
    return that, r, that * gain


def _softmax_with_sink(s, sink):
    m = jnp.maximum(jnp.max(s, axis=-1, keepdims=True), sink)
    p = jnp.exp(s - m)
    psink = jnp.exp(sink - m)
    inv = 1.0 / (jnp.sum(p, axis=-1, keepdims=True) + psink)
    return p * inv, psink * inv


GQ = NQ // NKV


def _attn_fwd(qkv, q_gain, k_gain, sinks, bias, layer):
    def body(q_ref, kc_ref, kp_ref, vc_ref, vp_ref, qg_ref, kg_ref, sk_ref, bias_ref, o_ref):
        m = pl.program_id(0)
        qg = qg_ref[layer:layer + 1, :]
        kg = kg_ref[layer:layer + 1, :]
        grp = range(NKV)
        chains = [(b, j) for b in range(2) for j in grp]
        masks = [jnp.tile(_attn_mask(2 * m + b), (GQ, 1)) for b in range(2)]
        kblk = [[kp_ref[:, pl.ds(HD * j, HD)].astype(f32), kc_ref[0:BLK, pl.ds(HD * j, HD)].astype(f32),
                 kc_ref[BLK:, pl.ds(HD * j, HD)].astype(f32)] for j in grp]
        vblk = [[vp_ref[:, pl.ds(HD * j, HD)].astype(bf16), vc_ref[0:BLK, pl.ds(HD * j, HD)].astype(bf16),
                 vc_ref[BLK:, pl.ds(HD * j, HD)].astype(bf16)] for j in grp]
        knb = [[_head_norm(kblk[j][t], kg)[2].astype(bf16) for t in range(3)] for j in grp]
        kn_b = {(b, j): jnp.concatenate([knb[j][b], knb[j][b + 1]], axis=0) for b, j in chains}
        vbs = {(b, j): jnp.concatenate([vblk[j][b], vblk[j][b + 1]], axis=0) for b, j in chains}
        rows = {}
        for b, j in chains:
            heads = [GQ * j + g for g in range(GQ)]
            rows[b, j] = (jnp.concatenate([q_ref[pl.ds(BLK * b, BLK), pl.ds(HD * h, HD)] for h in heads], axis=0).astype(f32),
                          jnp.concatenate([jnp.broadcast_to(sk_ref[layer:layer + 1, h:h + 1], (BLK, 1)) for h in heads], axis=0))
        qn_b = {c: _head_norm(rows[c][0], qg)[2].astype(bf16) for c in chains}
        ss = {(b, j): _dot(qn_b[b, j], kn_b[b, j], NT_DIMS) * (HD ** -0.5) + bias_ref[GQ * j:GQ * (j + 1)].reshape(GQ * BLK, 2 * BLK)
              for b, j in chains}
        ps = {(b, j): _softmax_with_sink(jnp.where(masks[b], ss[b, j], -jnp.inf), rows[b, j][1])[0] for b, j in chains}
        outs = {c: _dot(ps[c].astype(bf16), vbs[c], NN_DIMS).astype(bf16) for c in chains}
        for b, j in chains:
            for g in range(GQ):
                o_ref[pl.ds(BLK * b, BLK), pl.ds(HD * (GQ * j + g), HD)] = outs[b, j][BLK * g:BLK * (g + 1), :]

    prev = lambda m: jnp.maximum(2 * m - 1, 0)
    small = lambda shape: pl.BlockSpec(shape, lambda m: (0,) * len(shape))
    return pl.pallas_call(
        body, grid=(NBLK // 2,),
        in_specs=[pl.BlockSpec((2 * BLK, D_ATTN), lambda m: (m, 0)),
                  pl.BlockSpec((2 * BLK, 128), lambda m: (m, 4)), pl.BlockSpec((BLK, 128), lambda m: (prev(m), 4)),
                  pl.BlockSpec((2 * BLK, 128), lambda m: (m, 5)), pl.BlockSpec((BLK, 128), lambda m: (prev(m), 5)),
                  small((DEPTH, HD)), small((DEPTH, HD)), small((DEPTH, NQ)), small((NQ, BLK, 2 * BLK))],
        out_specs=pl.BlockSpec((2 * BLK, D_ATTN), lambda m: (m, 0)), out_shape=_out_hbm(SDS((S, D_ATTN), bf16)),
        name="attn_fwd", compiler_params=_cparams(1),
    )(*_in_hbm([qkv, qkv, qkv, qkv, qkv, q_gain, k_gain, sinks, bias]))


def _attn_bwd(qkv, dmix, q_gain, k_gain, sinks, bias, layer, deps=()):
    def body(q_ref, kc_ref, kp_ref, vc_ref, vp_ref, do_ref, qg_ref, kg_ref, sk_ref, bias_ref, *rest):
        dqkv_ref, dbias_ref, dsm_ref, carry = rest[len(deps):]
        i = pl.program_id(0)
        m = NBLK // 2 - 1 - i
        qg = qg_ref[layer:layer + 1, :]
        kg = kg_ref[layer:layer + 1, :]
        lane = lax.broadcasted_iota(jnp.int32, (1, 128), 1)

        @pl.when(i == 0)
        def _():
            carry[...] = jnp.zeros_like(carry)
            dbias_ref[...] = jnp.zeros_like(dbias_ref)
            dsm_ref[...] = jnp.zeros_like(dsm_ref)

        grp = range(NKV)
        chains = [(b, j) for b in range(2) for j in grp]
        masks = [jnp.tile(_attn_mask(2 * m + b), (GQ, 1)) for b in range(2)]
        kblk = [[kp_ref[:, pl.ds(HD * j, HD)].astype(f32), kc_ref[0:BLK, pl.ds(HD * j, HD)].astype(f32),
                 kc_ref[BLK:, pl.ds(HD * j, HD)].astype(f32)] for j in grp]
        vblk = [[vp_ref[:, pl.ds(HD * j, HD)].astype(bf16), vc_ref[0:BLK, pl.ds(HD * j, HD)].astype(bf16),
                 vc_ref[BLK:, pl.ds(HD * j, HD)].astype(bf16)] for j in grp]
        knorm = [[_head_norm(kblk[j][t], kg) for t in range(3)] for j in grp]
        kn_b = {(b, j): jnp.concatenate([knorm[j][b][2].astype(bf16), knorm[j][b + 1][2].astype(bf16)], axis=0) for b, j in chains}
        vbs = {(b, j): jnp.concatenate([vblk[j][b], vblk[j][b + 1]], axis=0) for b, j in chains}
        rows, do_b = {}, {}
        for b, j in chains:
            heads = [GQ * j + g for g in range(GQ)]
            qrows = pl.ds(BLK * b, BLK)
            rows[b, j] = (jnp.concatenate([q_ref[qrows, pl.ds(HD * h, HD)] for h in heads], axis=0).astype(f32),
                          jnp.concatenate([jnp.broadcast_to(sk_ref[layer:layer + 1, h:h + 1], (BLK, 1)) for h in heads], axis=0))
            do_b[b, j] = jnp.concatenate([do_ref[qrows, pl.ds(HD * h, HD)] for h in heads], axis=0).astype(bf16)
        qnorm = {c: _head_norm(rows[c][0], qg) for c in chains}
        qn_b = {c: qnorm[c][2].astype(bf16) for c in chains}
        ss = {(b, j): _dot(qn_b[b, j], kn_b[b, j], NT_DIMS) * (HD ** -0.5) + bias_ref[GQ * j:GQ * (j + 1)].reshape(GQ * BLK, 2 * BLK)
              for b, j in chains}
        sm = {(b, j): _softmax_with_sink(jnp.where(masks[b], ss[b, j], -jnp.inf), rows[b, j][1]) for b, j in chains}
        dps = {c: _dot(do_b[c], vbs[c], NT_DIMS) for c in chains}
        deltas = {c: jnp.sum(sm[c][0] * dps[c], axis=-1, keepdims=True) for c in chains}
        dss = {c: sm[c][0] * (dps[c] - deltas[c]) for c in chains}
        ds_b = {c: (dss[c] * (HD ** -0.5)).astype(bf16) for c in chains}
        dqn = {c: _dot(ds_b[c], kn_b[c], NN_DIMS) for c in chains}
        dkn = {c: _dot(ds_b[c], qn_b[c], TN_DIMS) for c in chains}
        dvs = {c: _dot(sm[c][0].astype(bf16), do_b[c], TN_DIMS) for c in chains}
        dqg = jnp.zeros((1, HD), f32)
        dkg = jnp.zeros((1, HD), f32)
        dsink = jnp.zeros((1, 128), f32)
        for b, j in chains:
            dbias_ref[GQ * j:GQ * (j + 1)] += dss[b, j].reshape(GQ, BLK, 2 * BLK)
            dsk = sm[b, j][1] * deltas[b, j]
            for g in range(GQ):
                dsink = dsink + jnp.where(lane == GQ * j + g, -_sum11(dsk[BLK * g:BLK * (g + 1), :]), 0.0)
            qhat, rq, _ = qnorm[b, j]
            w = dqn[b, j] * qg
            dq = rq * (w - qhat * _row_mean(qhat * w))
            for g in range(GQ):
                dqkv_ref[pl.ds(BLK * b, BLK), pl.ds(HD * (GQ * j + g), HD)] = dq[BLK * g:BLK * (g + 1), :].astype(bf16)
            dqg = dqg + jnp.sum(dqn[b, j] * qhat, axis=0, keepdims=True)
        for j in grp:
            dkn_t = [dkn[0, j][:BLK, :], dkn[0, j][BLK:, :] + dkn[1, j][:BLK, :], dkn[1, j][BLK:, :]]
            dv_t = [dvs[0, j][:BLK, :], dvs[0, j][BLK:, :] + dvs[1, j][:BLK, :], dvs[1, j][BLK:, :]]
            dk_t = []
            for t in range(3):
                khat, rk, _ = knorm[j][t]
                w = dkn_t[t] * kg
                dk_t.append(rk * (w - khat * _row_mean(khat * w)))
                dkg = dkg + jnp.sum(dkn_t[t] * khat, axis=0, keepdims=True)
            kcols, vcols = pl.ds(D_ATTN + HD * j, HD), pl.ds(D_ATTN + 128 + HD * j, HD)
            dqkv_ref[BLK:, kcols] = (dk_t[2] + carry[:, pl.ds(HD * j, HD)]).astype(bf16)
            dqkv_ref[BLK:, vcols] = (dv_t[2] + carry[:, pl.ds(128 + HD * j, HD)]).astype(bf16)
            dqkv_ref[0:BLK, kcols] = dk_t[1].astype(bf16)
            dqkv_ref[0:BLK, vcols] = dv_t[1].astype(bf16)
            carry[:, pl.ds(HD * j, HD)] = dk_t[0]
            carry[:, pl.ds(128 + HD * j, HD)] = dv_t[0]
        dsm_ref[0:1, 0:HD] += dqg
        dsm_ref[1:2, 0:HD] += dkg
        dsm_ref[2:3, :] += dsink

    rev = lambda i: NBLK // 2 - 1 - i
    prev = lambda i: jnp.maximum(NBLK - 3 - 2 * i, 0)
    small = lambda shape: pl.BlockSpec(shape, lambda i: (0,) * len(shape))
    return pl.pallas_call(
        body, grid=(NBLK // 2,),
        in_specs=[pl.BlockSpec((2 * BLK, D_ATTN), lambda i: (rev(i), 0)),
                  pl.BlockSpec((2 * BLK, 128), lambda i: (rev(i), 4)), pl.BlockSpec((BLK, 128), lambda i: (prev(i), 4)),
                  pl.BlockSpec((2 * BLK, 128), lambda i: (rev(i), 5)), pl.BlockSpec((BLK, 128), lambda i: (prev(i), 5)),
                  pl.BlockSpec((2 * BLK, D_ATTN), lambda i: (rev(i), 0)),
                  small((DEPTH, HD)), small((DEPTH, HD)), small((DEPTH, NQ)), small((NQ, BLK, 2 * BLK))] + [ANY_SPEC] * len(deps),
        out_specs=[pl.BlockSpec((2 * BLK, 768), lambda i: (rev(i), COL_QKV // 768)), small((NQ, BLK, 2 * BLK)), small((8, 128))],
        out_shape=_out_hbm([SDS((S, D_IN_PAD), bf16), SDS((NQ, BLK, 2 * BLK), f32), SDS((8, 128), f32)]),
        scratch_shapes=[pltpu.VMEM((BLK, 256), f32)], name="attn_bwd", compiler_params=_cparams(1),
    )(*_in_hbm([qkv, qkv, qkv, qkv, qkv, dmix, q_gain, k_gain, sinks, bias, *deps]))


CONV_TC = 256


def _shift_down(u, s):
    if s == 0:
        return u
    rows = lax.broadcasted_iota(jnp.int32, u.shape, 0)
    return jnp.where(rows >= s, pltpu.roll(u, s, 0), 0.0)


def _shift_up(u, s):
    if s == 0:
        return u
    rows = lax.broadcasted_iota(jnp.int32, u.shape, 0)
    return jnp.where(rows < u.shape[0] - s, pltpu.roll(u, u.shape[0] - s, 0), 0.0)


def _conv_specs():
    return [pl.BlockSpec((S, CONV_TC), lambda c: (0, c)),
            pl.BlockSpec((None, 4, CONV_TC), lambda c: (0, 0, c)),
            pl.BlockSpec((DEPTH, CONV_TC), lambda c: (0, c))]


def _conv_pre(u, w_ref, b_ref, layer):
    pre = b_ref[layer:layer + 1, :] + w_ref[3:4, :] * u
    for k in range(3):
        pre = pre + w_ref[k:k + 1, :] * _shift_down(u, 3 - k)
    return pre


def _conv_fwd(xbc, conv_w, conv_b, layer):
    def body(u_ref, w_ref, b_ref, o_ref):
        pre = _conv_pre(u_ref[...].astype(f32), w_ref, b_ref, layer)
        o_ref[...] = pre * _sigmoid(pre)

    specs = _conv_specs()
    specs[1] = pl.BlockSpec((None, 4, CONV_TC), lambda c: (layer, 0, c))
    return pl.pallas_call(
        body, grid=(D_CONV // CONV_TC,), in_specs=specs, out_specs=pl.BlockSpec((S, CONV_TC), lambda c: (0, c)),
        out_shape=_out_hbm(SDS((S, D_CONV), f32)), name="conv_fwd", compiler_params=_cparams(1),
    )(*_in_hbm([xbc, conv_w, conv_b]))


def _conv_bwd(xbc, dact, conv_w, conv_b, dproj, layer):
    def body(u_ref, w_ref, b_ref, da_ref, dproj_in, du_ref, dw_ref, db_ref):
        u = u_ref[...].astype(f32)
        pre = _conv_pre(u, w_ref, b_ref, layer)
        sg = _sigmoid(pre)
        dpre = da_ref[...] * (sg * (1.0 + pre * (1.0 - sg)))
        du = w_ref[3:4, :] * dpre
        for k in range(3):
            du = du + w_ref[k:k + 1, :] * _shift_up(dpre, 3 - k)
        du_ref[...] = du.astype(bf16)
        db_ref[...] = jnp.broadcast_to(jnp.sum(dpre, axis=0, keepdims=True), db_ref.shape)
        dw_ref[...] = jnp.zeros_like(dw_ref)
        for k in range(4):
            dw_ref[k:k + 1, :] = jnp.sum(dpre * _shift_down(u, 3 - k), axis=0, keepdims=True)

    specs = _conv_specs()
    specs[1] = pl.BlockSpec((None, 4, CONV_TC), lambda c: (layer, 0, c))
    col = pl.BlockSpec((S, CONV_TC), lambda c: (0, c))
    row8 = pl.BlockSpec((8, CONV_TC), lambda c: (0, c))
    return pl.pallas_call(
        body, grid=(D_CONV // CONV_TC,), in_specs=[*specs, col, ANY_SPEC],
        out_specs=[pl.BlockSpec((S, CONV_TC), lambda c: (0, COL_XBC // CONV_TC + c)), row8, row8],
        out_shape=_out_hbm([SDS((S, D_IN_PAD), bf16), SDS((8, D_CONV), f32), SDS((8, D_CONV), f32)]), name="conv_bwd",
        input_output_aliases={4: 0}, compiler_params=_cparams(1),
    )(*_in_hbm([xbc, conv_w, conv_b, dact, dproj]))


def _tri():
    return (lax.broadcasted_iota(jnp.int32, (BLK, BLK), 0) >= lax.broadcasted_iota(jnp.int32, (BLK, BLK), 1))


def _ssd_scalars(dt_ref, dtb_ref, alog_ref, layer):
    raw = dt_ref[:, 0:NSSM] + dtb_ref[layer:layer + 1, :]
    dtv = jnp.maximum(raw, 0.0) + jnp.log(1.0 + jnp.exp(-jnp.abs(raw)))
    a = -jnp.exp(alog_ref[layer:layer + 1, :])
    acs = jnp.dot(_tri().astype(f32), dtv * a, preferred_element_type=f32, precision=HIGHEST)
    return raw, dtv, a, acs


HG = NSSM // NGRP
GW = HG * HD


def _lane_expand(cols, g):
    lane_head = lax.broadcasted_iota(jnp.int32, (1, GW), 1) // HD
    out = cols[:, HG * g + HG - 1:HG * g + HG]
    for r in range(HG - 2, -1, -1):
        out = jnp.where(lane_head == r, cols[:, HG * g + r:HG * g + r + 1], out)
    return out


def _row_expand(vals, g):
    row_head = lax.broadcasted_iota(jnp.int32, (GW, 1), 0) // HD
    out = vals[:, HG * g + HG - 1:HG * g + HG]
    for r in range(HG - 2, -1, -1):
        out = jnp.where(row_head == r, vals[:, HG * g + r:HG * g + r + 1], out)
    return out


def _head_rowsums(a, g):
    sel = (lax.broadcasted_iota(jnp.int32, (GW, NSSM), 0) // HD + HG * g == lax.broadcasted_iota(jnp.int32, (GW, NSSM), 1)).astype(bf16)
    hi = a.astype(bf16)
    lo = (a - hi.astype(f32)).astype(bf16)
    return _dot(hi, sel, NN_DIMS) + _dot(lo, sel, NN_DIMS)


def _head_blocksums(v, g):
    sel = (lax.broadcasted_iota(jnp.int32, (GW, NSSM), 0) // HD + HG * g == lax.broadcasted_iota(jnp.int32, (GW, NSSM), 1)).astype(bf16)
    hi = v.astype(bf16)
    lo = (v - hi.astype(f32)).astype(bf16)
    return _dot(hi, sel, TN_DIMS) + _dot(lo, sel, TN_DIMS)


def _ssd_chunk_common(xc_ref, dt_ref, dtb_ref, alog_ref, h_rows, layer):
    raw, dtv, a, acs = _ssd_scalars(dt_ref, dtb_ref, alog_ref, layer)
    acs_t = acs.T
    last = acs[BLK - 1:BLK, :]
    c = dict(raw=raw, dtv=dtv, a=a, acs=acs, last=last, dte=jnp.exp(last - acs), e_all=jnp.exp(acs), cd=jnp.exp(last))
    grp, heads, tri = range(NGRP), range(NSSM), _tri()
    c["bm"] = [xc_ref[:, pl.ds(D_SSM + NSTATE * g, NSTATE)] for g in grp]
    c["bm_b"] = [c["bm"][g].astype(bf16) for g in grp]
    c["cm_b"] = [xc_ref[:, pl.ds(D_SSM + NGRP * NSTATE + NSTATE * g, NSTATE)].astype(bf16) for g in grp]
    c["cb"] = [_dot(c["cm_b"][g], c["bm_b"][g], NT_DIMS) for g in grp]
    c["x"] = [xc_ref[:, pl.ds(GW * g, GW)] for g in grp]
    c["dt"] = [_lane_expand(dtv, g) for g in grp]
    c["xdt"] = [c["x"][g] * c["dt"][g] for g in grp]
    c["xdt_b"] = [c["xdt"][g].astype(bf16) for g in grp]
    c["prev"] = [h_rows(g) for g in grp]
    c["prev_b"] = [c["prev"][g].astype(bf16) for g in grp]
    c["e"] = [_lane_expand(c["e_all"], g) for g in grp]
    c["y_off"] = [_dot(c["cm_b"][g], c["prev_b"][g], NT_DIMS) * c["e"][g] for g in grp]
    c["decay"] = [jnp.exp(jnp.where(tri, acs[:, h:h + 1] - acs_t[h:h + 1, :], -jnp.inf)) for h in heads]
    c["m"] = [c["cb"][h // HG] * c["decay"][h] for h in heads]
    c["m_b"] = [c["m"][h].astype(bf16) for h in heads]
    c["dte_x"] = [_lane_expand(c["dte"], g) for g in grp]
    c["xdte_b"] = [(c["xdt"][g] * c["dte_x"][g]).astype(bf16) for g in grp]
    return c


def _ssd_fwd(xact, z, dt, attn, dt_bias, a_log, d_skip, norm_g, layer):
    def body(xc_ref, z_ref, dt_ref, at_ref, dtb_ref, alog_ref, dsk_ref, ng_ref, mix_ref, hs_ref, y_ref, h_ref):
        n = pl.program_id(0)

        @pl.when(n == 0)
        def _():
            h_ref[...] = jnp.zeros_like(h_ref)

        hs_ref[...] = h_ref[...]
        c = _ssd_chunk_common(xc_ref, dt_ref, dtb_ref, alog_ref, lambda g: h_ref[pl.ds(GW * g, GW), :], layer)
        grp, heads = range(NGRP), range(NSSM)
        y_diag = [_dot(c["m_b"][h], c["xdt_b"][h // HG][:, HD * (h % HG):HD * (h % HG + 1)], NN_DIMS) for h in heads]
        new_st = [_dot(c["xdte_b"][g], c["bm_b"][g], TN_DIMS) for g in grp]
        for h in heads:
            y_ref[:, pl.ds(HD * h, HD)] = y_diag[h]
        dskip = dsk_ref[layer:layer + 1, :]
        for g in grp:
            cols = pl.ds(GW * g, GW)
            y_ref[:, cols] = y_ref[:, cols] + c["y_off"][g] + c["x"][g] * _lane_expand(dskip, g)
            h_ref[cols, :] = c["prev"][g] * _row_expand(c["cd"], g) + new_st[g]
        zv = z_ref[...].astype(f32)
        yz = y_ref[...] * (zv * _sigmoid(zv))
        mix_ref[:, 0:D_ATTN] = at_ref[...]
        for g in grp:
            yg = yz[:, GW * g:GW * (g + 1)]
            rs = lax.rsqrt(jnp.mean(yg * yg, axis=-1, keepdims=True) + EPS)
            mix_ref[:, D_ATTN + GW * g:D_ATTN + GW * (g + 1)] = (yg * rs * ng_ref[layer:layer + 1, GW * g:GW * (g + 1)]).astype(bf16)

    small = lambda shape: pl.BlockSpec(shape, lambda n: (0,) * len(shape))
    return pl.pallas_call(
        body, grid=(NBLK,),
        in_specs=[pl.BlockSpec((BLK, D_CONV), lambda n: (n, 0)), pl.BlockSpec((BLK, D_SSM), lambda n: (n, 0)),
                  pl.BlockSpec((BLK, 128), lambda n: (n, 0)), pl.BlockSpec((BLK, D_ATTN), lambda n: (n, 0)),
                  small((DEPTH, NSSM)), small((DEPTH, NSSM)), small((DEPTH, NSSM)), small((DEPTH, D_SSM))],
        out_specs=[pl.BlockSpec((BLK, D), lambda n: (n, 0)), pl.BlockSpec((None, NSSM * HD, NSTATE), lambda n: (n, 0, 0)),
                   pl.BlockSpec((BLK, D_SSM), lambda n: (n, 0))],
        out_shape=_out_hbm([SDS((S, D), bf16), SDS((NBLK, NSSM * HD, NSTATE), f32), SDS((S, D_SSM), f32)]),
        scratch_shapes=[pltpu.VMEM((NSSM * HD, NSTATE), f32)],
        name="ssd_fwd", compiler_params=_cparams(1),
    )(*_in_hbm([xact, z, dt, attn, dt_bias, a_log, d_skip, norm_g]))


def _ssd_bwd(xact, z, dt, dmix, hs, y, dt_bias, a_log, d_skip, norm_g, dproj, layer):
    def body(xc_ref, z_ref, dt_ref, do_ref, hs_ref, y_ref, dtb_ref, alog_ref, dsk_ref, ng_ref, dproj_in,
             dzdt_ref, dx_ref, dsm_ref, dh_ref, dy_ref):
        i = pl.program_id(0)

        @pl.when(i == 0)
        def _():
            dh_ref[...] = jnp.zeros_like(dh_ref)
            dsm_ref[...] = jnp.zeros_like(dsm_ref)

        c = _ssd_chunk_common(xc_ref, dt_ref, dtb_ref, alog_ref, lambda g: hs_ref[pl.ds(GW * g, GW), :], layer)
        raw, dtv, a = c["raw"], c["dtv"], c["a"]
        grp, heads = range(NGRP), range(NSSM)
        dskip = dsk_ref[layer:layer + 1, :]
        lane8 = lax.broadcasted_iota(jnp.int32, (1, NSSM), 1)
        sub8 = lax.broadcasted_iota(jnp.int32, (NSSM, 1), 0)

        zv = z_ref[...].astype(f32)
        sz = _sigmoid(zv)
        gz = zv * sz
        yv = y_ref[...]
        yz = yv * gz
        for g in grp:
            sl = slice(GW * g, GW * (g + 1))
            yg = yz[:, sl]
            rs = lax.rsqrt(jnp.mean(yg * yg, axis=-1, keepdims=True) + EPS)
            yhat = yg * rs
            dog = do_ref[:, sl]
            w = dog * ng_ref[layer:layer + 1, sl]
            dyz = rs * (w - yhat * jnp.mean(yhat * w, axis=-1, keepdims=True))
            dsm_ref[0:1, sl] += jnp.sum(dog * yhat, axis=0, keepdims=True)
            dy_ref[:, sl] = dyz * gz[:, sl]
            dzdt_ref[:, sl] = (dyz * yv[:, sl] * (sz[:, sl] * (1.0 + zv[:, sl] * (1.0 - sz[:, sl])))).astype(bf16)

        dy = [dy_ref[:, pl.ds(GW * g, GW)] for g in grp]
        dy_b = [dy[g].astype(bf16) for g in grp]
        hl = lambda h: slice(HD * (h % HG), HD * (h % HG + 1))
        dt_off_b = [(dy[g] * c["e"][g]).astype(bf16) for g in grp]
        dcm = [_dot(dt_off_b[g], c["prev_b"][g], NN_DIMS) for g in grp]
        dprev = [_dot(dt_off_b[g], c["cm_b"][g], TN_DIMS) for g in grp]
        yoff_rs = [_head_rowsums(dy[g] * c["y_off"][g], g) for g in grp]
        dhn = [dh_ref[pl.ds(GW * g, GW), :] for g in grp]
        dhn_b = [dhn[g].astype(bf16) for g in grp]
        dprev = [dprev[g] + dhn[g] * _row_expand(c["cd"], g) for g in grp]
        dhn_prev = [dhn[g] * c["prev"][g] for g in grp]
        u = [_dot(c["bm_b"][g], dhn_b[g], NT_DIMS) for g in grp]
        dbm = [_dot(c["xdte_b"][g], dhn_b[g], NN_DIMS) for g in grp]
        ddte_rs = [_head_rowsums(c["xdt"][g] * u[g], g) for g in grp]
        dm = [_dot(dy_b[h // HG][:, hl(h)], c["xdt_b"][h // HG][:, hl(h)], NT_DIMS) for h in heads]
        dxdt_in = [_dot(c["m_b"][h], dy_b[h // HG][:, hl(h)], TN_DIMS) for h in heads]
        dseg = [dm[h] * c["m"][h] for h in heads]
        dmd = [dm[h] * c["decay"][h] for h in heads]
        for h in heads:
            dx_ref[:, pl.ds(HD * h, HD)] = dxdt_in[h]

        tmp = (ddte_rs[0] + ddte_rs[1]) * c["dte"]
        dacs = yoff_rs[0] + yoff_rs[1] - tmp
        dacs_cols = jnp.zeros((NSSM, BLK), f32)
        ddtv = jnp.zeros((BLK, NSSM), f32)
        ddsk = jnp.zeros((BLK, NSSM), f32)
        hp = jnp.zeros((1, NSSM), f32)
        for g in grp:
            cols = pl.ds(GW * g, GW)
            dxdt = dx_ref[:, cols] + u[g] * c["dte_x"][g]
            dx_ref[:, cols] = dy[g] * _lane_expand(dskip, g) + dxdt * c["dt"][g]
            ddtv = ddtv + _head_rowsums(dxdt * c["x"][g], g)
            ddsk = ddsk + _head_rowsums(dy[g] * c["x"][g], g)
            dcb = dmd[HG * g]
            for r in range(1, HG):
                dcb = dcb + dmd[HG * g + r]
            dcb_b = dcb.astype(bf16)
            dx_ref[:, pl.ds(D_SSM + NSTATE * g, NSTATE)] = dbm[g] + _dot(dcb_b, c["cm_b"][g], TN_DIMS)
            dx_ref[:, pl.ds(D_SSM + NGRP * NSTATE + NSTATE * g, NSTATE)] = dcm[g] + _dot(dcb_b, c["bm_b"][g], NN_DIMS)
            dh_ref[cols, :] = dprev[g]
            hp = hp + _head_blocksums(jnp.sum(dhn_prev[g], axis=1, keepdims=True), g)
            for r in range(HG):
                h = HG * g + r
                dacs = dacs + (lane8 == h).astype(f32) * jnp.sum(dseg[h], axis=1, keepdims=True)
                dacs_cols = dacs_cols + (sub8 == h).astype(f32) * jnp.sum(dseg[h], axis=0, keepdims=True)
        dlast = hp * c["cd"] + jnp.sum(tmp, axis=0, keepdims=True)
        ddsk = jnp.sum(ddsk, axis=0, keepdims=True)

        row = lax.broadcasted_iota(jnp.int32, (BLK, 1), 0)
        dacs = dacs - dacs_cols.T + jnp.where(row == BLK - 1, dlast, 0.0)
        dda = lax.dot_general(_tri().astype(f32), dacs, TN_DIMS, preferred_element_type=f32, precision=HIGHEST)
        ddtv = ddtv + dda * a
        da = jnp.sum(dda * dtv, axis=0, keepdims=True)
        draw = ddtv * _sigmoid(raw)
        dzdt_ref[:, D_SSM:] = jnp.zeros((BLK, COL_XBC - COL_DT), bf16)
        dzdt_ref[:, D_SSM:D_SSM + NSSM] = draw.astype(bf16)
        dsm_ref[1:2, 0:NSSM] += jnp.sum(draw, axis=0, keepdims=True)
        dsm_ref[2:3, 0:NSSM] += da * a
        dsm_ref[3:4, 0:NSSM] += ddsk

    rev = lambda i: NBLK - 1 - i
    small = lambda shape: pl.BlockSpec(shape, lambda i: (0,) * len(shape))
    return pl.pallas_call(
        body, grid=(NBLK,),
        in_specs=[pl.BlockSpec((BLK, D_CONV), lambda i: (rev(i), 0)), pl.BlockSpec((BLK, D_SSM), lambda i: (rev(i), 0)),
                  pl.BlockSpec((BLK, 128), lambda i: (rev(i), 0)), pl.BlockSpec((BLK, D_SSM), lambda i: (rev(i), 1)),
                  pl.BlockSpec((None, NSSM * HD, NSTATE), lambda i: (rev(i), 0, 0)), pl.BlockSpec((BLK, D_SSM), lambda i: (rev(i), 0)),
                  small((DEPTH, NSSM)), small((DEPTH, NSSM)), small((DEPTH, NSSM)), small((DEPTH, D_SSM)), ANY_SPEC],
        out_specs=[pl.BlockSpec((BLK, COL_XBC - COL_Z), lambda i: (rev(i), COL_Z // (COL_XBC - COL_Z))),
                   pl.BlockSpec((BLK, D_CONV), lambda i: (rev(i), 0)), small((8, D_SSM))],
        out_shape=_out_hbm([SDS((S, D_IN_PAD), bf16), SDS((S, D_CONV), f32), SDS((8, D_SSM), f32)]),
        scratch_shapes=[pltpu.VMEM((NSSM * HD, NSTATE), f32), pltpu.VMEM((BLK, D_SSM), f32)],
        name="ssd_bwd", input_output_aliases={10: 0}, compiler_params=_cparams(1),
    )(*_in_hbm([xact, z, dt, dmix, hs, y, dt_bias, a_log, d_skip, norm_g, dproj]))


def _my_place():
    return lax.axis_index("x"), lax.axis_index("y"), lax.axis_index("c")


def _dev_index(px, py, pc):
    return 4 * px + 2 * py + pc


def _slab2(kind, ref, idx):
    if kind == "stack":
        return ref.at[idx]
    if kind == "rows128":
        return ref.at[pl.ds(pl.multiple_of(idx * 128, 128), 128), :]
    if kind == "rows512":
        return ref.at[pl.ds(pl.multiple_of(idx * 512, 512), 512), :]
    return ref.at[:, pl.ds(pl.multiple_of(idx * 512, 512), 512)]


def _slab_shape(kind, full_shape):
    if kind == "stack":
        return tuple(full_shape[1:])
    if kind == "rows128":
        return (128, full_shape[1])
    if kind == "rows512":
        return (512, full_shape[1])
    return (full_shape[0], 512)


KIND = dict(w_in="stack", w_out="rows128", w_up="cols512", w_down="rows512", conv_w="stack")
FULL_SHAPE = dict(w_in=(N_DEV, D, D_IN // N_DEV), w_out=(D, D), w_up=(D, D_FF), w_down=(D_FF, D))
HBM_SPEC = pl.BlockSpec(memory_space=pltpu.HBM)
SEM_SPEC = pl.BlockSpec(memory_space=pltpu.SEMAPHORE)
SIDE_EFFECT = pltpu.SideEffectType.DATAFLOW_SIDE_EFFECTING


def _peers_all():
    x, y, c = _my_place()
    return [(x ^ ((r >> 2) & 1), y ^ ((r >> 1) & 1), c ^ (r & 1)) for r in range(1, N_DEV)]


def _split_start(name, bufs, n_copies, plan, deps=()):
    nb = len(bufs)

    def body(*refs):
        ins = refs[:nb]
        send_sems, recv_sems = refs[nb + len(deps)], refs[nb + len(deps) + 1]
        token = refs[-1]
        for i, (src, dst, dev) in enumerate(plan(ins)):
            pltpu.make_async_remote_copy(src_ref=src, dst_ref=dst, send_sem=send_sems.at[i], recv_sem=recv_sems.at[i],
                                         device_id=dev, device_id_type=MESH).start()
        token[...] = jnp.zeros_like(token)

    outs = pl.pallas_call(
        body, name=name,
        out_shape=(pltpu.SemaphoreType.DMA((n_copies,)), pltpu.SemaphoreType.DMA((n_copies,)),
                   *[pltpu.HBM(b.shape, b.dtype) for b in bufs], SDS((8, 128), f32)),
        in_specs=[HBM_SPEC] * nb + [ANY_SPEC] * len(deps),
        out_specs=(SEM_SPEC, SEM_SPEC, *[HBM_SPEC] * nb, pl.BlockSpec(memory_space=pltpu.VMEM)),
        input_output_aliases={i: 2 + i for i in range(nb)},
        compiler_params=pltpu.CompilerParams(has_side_effects=SIDE_EFFECT),
    )(*[pltpu.with_memory_space_constraint(b, pltpu.HBM) for b in bufs], *deps)
    return dict(send=outs[0], recv=outs[1], bufs=list(outs[2:2 + nb]), token=outs[-1], plan=plan, n=n_copies)


def _split_wait(name, started, after):
    bufs = started["bufs"]
    nb = len(bufs)
    plan = started["plan"]

    def body(*refs):
        ins = refs[:nb]
        send_sems, recv_sems = refs[nb], refs[nb + 1]
        for i, (src, dst, dev) in enumerate(plan(ins)):
            cp = pltpu.make_async_remote_copy(src_ref=src, dst_ref=dst, send_sem=send_sems.at[i], recv_sem=recv_sems.at[i],
                                              device_id=dev, device_id_type=MESH)
            cp.wait_send()
            cp.wait_recv()

    outs = pl.pallas_call(
        body, name=name, out_shape=tuple(pltpu.HBM(b.shape, b.dtype) for b in bufs),
        in_specs=[HBM_SPEC] * nb + [SEM_SPEC, SEM_SPEC] + [ANY_SPEC] * len(after), out_specs=(HBM_SPEC,) * nb,
        input_output_aliases={i: i for i in range(nb)},
        compiler_params=pltpu.CompilerParams(has_side_effects=SIDE_EFFECT),
    )(*bufs, started["send"], started["recv"], *after)
    return list(outs)


def _gather_start(name, names, fulls, deps):
    n_t = len(names)

    def plan(refs):
        x, y, c = _my_place()
        my_idx = _dev_index(x, y, c)
        targets = [(x, y, 1 - c), (1 - x, y, c), (x, 1 - y, c), (1 - x, 1 - y, c)]
        slabs = [_slab2(KIND[names[t]], refs[t], my_idx) for t in range(n_t)]
        return [(slabs[t], slabs[t], dev) for t in range(n_t) for dev in targets]

    return _split_start(name, list(fulls), 4 * n_t, plan, deps)


def _gather_finish(name, names, started, after):
    n_t = len(names)
    fulls = _split_wait(name + "_wait", started, after)
    slab_shapes = [SDS(_slab_shape(KIND[n], f.shape), f.dtype) for n, f in zip(names, fulls)]

    def body(*refs):
        ins = refs[:n_t]
        outs = refs[n_t:2 * n_t]
        stage = refs[2 * n_t:3 * n_t]
        load_sems, send_sems, recv_sems = refs[3 * n_t:]
        x, y, c = _my_place()
        chips = [(1 - x, y), (x, 1 - y), (1 - x, 1 - y)]
        pairs = [(t, j) for t in range(n_t) for j in range(3)]
        loads = [pltpu.make_async_copy(_slab2(KIND[names[t]], ins[t], _dev_index(*chips[j], c)), stage[t].at[j], load_sems.at[t, j])
                 for t, j in pairs]
        for cp in loads:
            cp.start()

        def copy(t, j, core):
            return pltpu.make_async_remote_copy(
                src_ref=stage[t].at[j], dst_ref=_slab2(KIND[names[t]], outs[t], _dev_index(*chips[j], core)),
                send_sem=send_sems.at[t, j], recv_sem=recv_sems.at[t, j], device_id=(x, y, 1 - c), device_id_type=MESH)

        sends = [copy(t, j, c) for t, j in pairs]
        for ld, cp in zip(loads, sends):
            ld.wait()
            cp.start()
        for t, j in pairs:
            copy(t, j, 1 - c).wait_recv()
        for cp in sends:
            cp.wait_send()

    return pl.pallas_call(
        body, in_specs=[ANY_SPEC] * n_t, out_specs=[ANY_SPEC] * n_t, out_shape=[SDS(b.shape, b.dtype) for b in fulls],
        input_output_aliases={t: t for t in range(n_t)},
        scratch_shapes=[pltpu.VMEM((3,) + s.shape, s.dtype) for s in slab_shapes]
        + [pltpu.SemaphoreType.DMA((n_t, 3)), pltpu.SemaphoreType.DMA((n_t, 3)), pltpu.SemaphoreType.DMA((n_t, 3))],
        name=name + "_pass", compiler_params=pltpu.CompilerParams(vmem_limit_bytes=VMEM_LIMIT),
    )(*fulls)


def _exchange_start(name, names, grads, deps):
    n_t = len(names)
    lands = [lax.empty((N_DEV,) + _slab_shape(KIND[n], g.shape), g.dtype) for n, g in zip(names, grads)]

    def plan(refs):
        my_idx = _dev_index(*_my_place())
        return [(_slab2(KIND[names[t]], refs[t], _dev_index(*peer)), refs[n_t + t].at[my_idx], peer)
                for t in range(n_t) for peer in _peers_all()]

    return _split_start(name, list(grads) + lands, 7 * n_t, plan, deps)


def _small_exchange_start(part, deps):
    land = lax.empty((N_DEV,) + part.shape, part.dtype)

    def plan(refs):
        my_idx = _dev_index(*_my_place())
        return [(refs[0], refs[1].at[my_idx], peer) for peer in _peers_all()]

    return _split_start("small_exchange", [part, land], N_DEV - 1, plan, deps)


def _slab_pieces():
    sh = D_IN // N_DEV
    out = []
    for j in range(N_DEV):
        for first, end, dst in IN_SEGMENTS:
            lo, hi = max(first, sh * j), min(end, sh * (j + 1))
            if lo < hi:
                out.append((j, lo - sh * j, hi - sh * j, dst + lo - first))
    return out


def _w_in_assemble(stacked):
    tr = 256
    sh = D_IN // N_DEV

    def body(i_ref, o_ref):
        o_ref[:, COL_DT:COL_XBC] = jnp.zeros((tr, COL_XBC - COL_DT), bf16)
        for j, lo, hi, dst in _slab_pieces():
            o_ref[:, dst:dst + hi - lo] = i_ref[j, :, lo:hi]

    return pl.pallas_call(
        body, grid=(D // tr,), in_specs=[pl.BlockSpec((N_DEV, tr, sh), lambda i: (0, i, 0))],
        out_specs=pl.BlockSpec((None, tr, D_IN_PAD), lambda i: (0, i, 0)), out_shape=_out_hbm(SDS((1, D, D_IN_PAD), bf16)),
        name="w_in_assemble", compiler_params=_cparams(1),
    )(*_in_hbm([stacked]))


def _w_in_slabs(dw_in):
    tr = 256
    sh = D_IN // N_DEV

    def body(i_ref, o_ref):
        for j, lo, hi, src in _slab_pieces():
            o_ref[j, :, lo:hi] = i_ref[:, src:src + hi - lo]

    return pl.pallas_call(
        body, grid=(D // tr,), in_specs=[pl.BlockSpec((tr, D_IN_PAD), lambda i: (i, 0))],
        out_specs=pl.BlockSpec((N_DEV, tr, sh), lambda i: (0, i, 0)), out_shape=_out_hbm(SDS((N_DEV, D, sh), bf16)),
        name="w_in_slabs", compiler_params=_cparams(1),
    )(*_in_hbm([dw_in]))


SMALL_NAMES = ("mix_norm_g", "mlp_norm_g", "conv_b", "ssm_norm_g", "q_gain", "k_gain", "sinks", "dt_bias", "a_log", "d_skip",
               "rel_bias", "conv_w")
MISC_LANES = dict(q_gain=(LANE_QG, HD), k_gain=(LANE_KG, HD), sinks=(LANE_SINK, NQ), dt_bias=(LANE_DTB, NSSM),
                  a_log=(LANE_ALOG, NSSM), d_skip=(LANE_DSKIP, NSSM))


def _pack_small_grads(smalls, drel_t, loss):
    def body(*refs):
        o_ref = refs[-1]
        drel_ref, loss_ref = refs[-3], refs[-2]
        o_ref[...] = jnp.zeros_like(o_ref)
        for l in range(DEPTH):
            mixg, mlpg, convb, convw, ssd, attn = refs[6 * l:6 * l + 6]
            o_ref[ROW_MIXG + l:ROW_MIXG + l + 1, :] = mixg[...]
            o_ref[ROW_MLPG + l:ROW_MLPG + l + 1, :] = mlpg[...]
            o_ref[ROW_CONVB + l:ROW_CONVB + l + 1, :] = convb[0:1, :]
            o_ref[ROW_SSMG + l:ROW_SSMG + l + 1, 0:D_SSM] = ssd[0:1, :]
            o_ref[ROW_CONVW + 4 * l:ROW_CONVW + 4 * l + 4, :] = convw[0:4, :]
            row = slice(ROW_MISC + l, ROW_MISC + l + 1)
            o_ref[row, LANE_QG:LANE_QG + HD] = attn[0:1, 0:HD]
            o_ref[row, LANE_KG:LANE_KG + HD] = attn[1:2, 0:HD]
            o_ref[row, LANE_SINK:LANE_SINK + NQ] = attn[2:3, 0:NQ]
            o_ref[row, LANE_DTB:LANE_DTB + NSSM] = ssd[1:2, 0:NSSM]
            o_ref[row, LANE_ALOG:LANE_ALOG + NSSM] = ssd[2:3, 0:NSSM]
            o_ref[row, LANE_DSKIP:LANE_DSKIP + NSSM] = ssd[3:4, 0:NSSM]
        o_ref[ROW_RELB:ROW_RELB + NQ, 0:N_BUCKETS] = drel_ref[...]
        o_ref[ROW_LOSS:ROW_LOSS + 1, 0:1] = loss_ref[0:1, 0:1]

    args = []
    for sm in smalls:
        args += [sm["mix_norm_g"], sm["mlp_norm_g"], sm["conv_b"], sm["conv_w"], sm["ssd"], sm["attn"]]
    args += [drel_t, loss]
    return pl.pallas_call(body, out_shape=SDS((SMALL_ROWS, D), f32), name="pack_small_grads")(*args)


def _adamw_small(part, land, w, m, v):
    n = len(SMALL_NAMES)

    def grad_of(name, g_ref):
        if name == "mix_norm_g":
            return g_ref[ROW_MIXG:ROW_MIXG + DEPTH, :]
        if name == "mlp_norm_g":
            return g_ref[ROW_MLPG:ROW_MLPG + DEPTH, :]
        if name == "conv_b":
            return g_ref[ROW_CONVB:ROW_CONVB + DEPTH, :]
        if name == "ssm_norm_g":
            return g_ref[ROW_SSMG:ROW_SSMG + DEPTH, 0:D_SSM]
        if name == "rel_bias":
            return g_ref[ROW_RELB:ROW_RELB + NQ, 0:N_BUCKETS].T
        lane, width = MISC_LANES[name]
        return g_ref[ROW_MISC:ROW_MISC + DEPTH, lane:lane + width]

    def body(part_ref, land_ref, *refs):
        ws, ms, vs = refs[:n], refs[n:2 * n], refs[2 * n:3 * n]
        loss_ref = refs[3 * n]
        outs = refs[3 * n + 1:-1]
        g_ref = refs[-1]
        me = _dev_index(*_my_place())
        for p in range(N_DEV):
            term = jnp.where(me == p, part_ref[...], land_ref[p])
            if p == 0:
                g_ref[...] = term
            else:
                g_ref[...] += term
        loss_ref[...] = g_ref[ROW_LOSS:ROW_LOSS + 1, 0:128]
        my_cols = pl.ds(pl.multiple_of(me * 128, 128), 128)
        for k, name in enumerate(SMALL_NAMES):
            g_out, d_out, m_out, v_out = outs[4 * k:4 * k + 4]
            if name == "conv_w":
                for l in range(DEPTH):
                    g = g_ref[ROW_CONVW + 4 * l:ROW_CONVW + 4 * l + 4, my_cols]
                    delta, m_new, v_new = _adamw_math(ws[k][l], ms[k][l], vs[k][l], g)
                    g_out[l], d_out[l], m_out[l], v_out[l] = g, delta, m_new, v_new
            else:
                g = grad_of(name, g_ref)
                delta, m_new, v_new = _adamw_math(ws[k][...], ms[k][...], vs[k][...], g)
                g_out[...], d_out[...], m_out[...], v_out[...] = g, delta, m_new, v_new

    ws = [w[name] for name in SMALL_NAMES]
    out_shape = [SDS((1, 128), f32)]
    for a in ws:
        out_shape += [SDS(a.shape, f32)] * 4
    return pl.pallas_call(body, out_shape=out_shape, name="adamw_small", scratch_shapes=[pltpu.VMEM((SMALL_ROWS, D), f32)])(
        part, land, *ws, *[m[name] for name in SMALL_NAMES], *[v[name] for name in SMALL_NAMES])


def _plain(tm, tn):
    return pl.BlockSpec((tm, tn), lambda i, j, k: (i, j))


def _rowblk(tm, width):
    return pl.BlockSpec((tm, width), lambda i, j, k: (i, 0))


def _store_epi(dtype):
    def epi(acc, i, j, ex, outs):
        outs[0][...] = acc.astype(dtype)
    return epi


def _rms_prologue(layer):
    def pro(a_ref, ex, outs):
        xv = a_ref[...]
        r = lax.rsqrt(jnp.mean(xv * xv, axis=-1, keepdims=True) + EPS)
        h = (xv * r * ex[0][layer:layer + 1, :]).astype(bf16)
        outs[-1][...] = h
        return h
    return pro


MLP_TM = 256
MLP_VMEM = 56 * 1024 * 1024


def _load_once(i, hbm_refs, vmem_refs, sems):
    copies = [pltpu.make_async_copy(h.at[0], v, sems.at[k]) for k, (h, v) in enumerate(zip(hbm_refs, vmem_refs))]

    @pl.when(i == 0)
    def _():
        for cp in copies:
            cp.start()

    def wait(k):
        @pl.when(i == 0)
        def _():
            copies[k].wait()
    return wait


def _mlp_fwd(layer, x, mix, g, w_out, w_up, w_down, tgt=None):
    tm = MLP_TM
    with_loss = tgt is not None

    def body(x_ref, mix_ref, g_ref, wo_hbm, wu_hbm, wd_hbm, *rest):
        wo_ref, wu_ref, wd_ref, sems = rest[-4:]
        rest = rest[:-4]
        xm_ref, a_ref, r_ref, h_ref = rest[with_loss:with_loss + 4]
        rest = rest[:with_loss] + rest[with_loss + 1:]
        i = pl.program_id(0)
        wait = _load_once(i, (wo_hbm, wu_hbm, wd_hbm), (wo_ref, wu_ref, wd_ref), sems)
        wait(0)
        xv = x_ref[...] + _dot(mix_ref[...], wo_ref[...], NN_DIMS)
        xm_ref[...] = xv
        h = (xv * lax.rsqrt(jnp.mean(xv * xv, axis=-1, keepdims=True) + EPS) * g_ref[layer:layer + 1, :]).astype(bf16)
        h_ref[...] = h
        wait(1)
        r = jnp.maximum(_dot(h, wu_ref[...], NN_DIMS), 0.0)
        a = (r * r).astype(bf16)
        a_ref[...] = a
        r_ref[...] = r.astype(bf16)
        wait(2)
        y = xv + _dot(a, wd_ref[...], NN_DIMS)
        if not with_loss:
            rest[3][...] = y
            return
        err = y - rest[0][...]
        rest[4][...] = err * (1.0 / D)
        part = 0.5 * jnp.sum(jnp.mean(err * err, axis=-1, keepdims=True), axis=0, keepdims=True)

        @pl.when(i == 0)
        def _():
            rest[5][...] = jnp.zeros_like(rest[5])

        rest[5][...] += jnp.broadcast_to(part, rest[5].shape)

    row = lambda width: pl.BlockSpec((tm, width), lambda i: (i, 0))
    in_specs = [row(D), row(D), pl.BlockSpec((DEPTH, D), lambda i: (0, 0)), ANY_SPEC, ANY_SPEC, ANY_SPEC]
    out_specs = [row(D), row(D_FF), row(D_FF), row(D), row(D)]
    out_shape = [SDS((S, D), f32), SDS((S, D_FF), bf16), SDS((S, D_FF), bf16), SDS((S, D), bf16), SDS((S, D), f32)]
    args = [x, mix, g, w_out, w_up, w_down]
    if with_loss:
        in_specs.append(row(D))
        args.append(tgt)
        out_specs.append(pl.BlockSpec((1, 128), lambda i: (0, 0)))
        out_shape.append(SDS((1, 128), f32))
    return pl.pallas_call(
        body, grid=(S // tm,), in_specs=in_specs, out_specs=out_specs, out_shape=_out_hbm(out_shape),
        scratch_shapes=[pltpu.VMEM((D, D), bf16), pltpu.VMEM((D, D_FF), bf16), pltpu.VMEM((D_FF, D), bf16),
                        pltpu.SemaphoreType.DMA((3,))],
        name="mlp_fwd_loss" if with_loss else "mlp_fwd",
        compiler_params=pltpu.CompilerParams(dimension_semantics=("arbitrary",), vmem_limit_bytes=MLP_VMEM),
    )(*_in_hbm(args[:3]), *args[3:6], *_in_hbm(args[6:]))


def _mlp_bwd_act(layer, dx_out, r_act, x_mid, g, w_down, w_up, w_out, deps):
    tm = MLP_TM

    def body(dxo_ref, r_ref, xm_ref, g_ref, wd_hbm, wu_hbm, wo_hbm, *rest):
        du_ref, dx_ref, dg_ref, dmix_ref, wd_ref, wu_ref, wo_ref, sems = rest[len(deps):]
        i = pl.program_id(0)
        wait = _load_once(i, (wd_hbm, wu_hbm, wo_hbm), (wd_ref, wu_ref, wo_ref), sems)
        dxo = dxo_ref[...]
        wait(0)
        du = (_dot(dxo.astype(bf16), wd_ref[...], NT_DIMS) * (2.0 * r_ref[...].astype(f32))).astype(bf16)
        du_ref[...] = du
        wait(1)
        dh = _dot(du, wu_ref[...], NT_DIMS)
        _rms_bwd_epilogue(layer)(dh, i, 0, (xm_ref, g_ref, dxo_ref), (dx_ref, dg_ref))
        wait(2)
        dmix_ref[...] = _dot(dx_ref[...].astype(bf16), wo_ref[...], NT_DIMS)

    row = lambda width: pl.BlockSpec((tm, width), lambda i: (i, 0))
    return pl.pallas_call(
        body, grid=(S // tm,),
        in_specs=[row(D), row(D_FF), row(D), pl.BlockSpec((DEPTH, D), lambda i: (0, 0)), ANY_SPEC, ANY_SPEC, ANY_SPEC]
        + [ANY_SPEC] * len(deps),
        out_specs=[row(D_FF), row(D), pl.BlockSpec((1, D), lambda i: (0, 0)), row(D)],
        scratch_shapes=[pltpu.VMEM((D_FF, D), bf16), pltpu.VMEM((D, D_FF), bf16), pltpu.VMEM((D, D), bf16),
                        pltpu.SemaphoreType.DMA((3,))],
        out_shape=_out_hbm([SDS((S, D_FF), bf16), SDS((S, D), f32), SDS((1, D), f32), SDS((S, D), f32)]), name="mlp_bwd_act",
        compiler_params=pltpu.CompilerParams(dimension_semantics=("arbitrary",), vmem_limit_bytes=MLP_VMEM),
    )(*_in_hbm([dx_out, r_act, x_mid, g]), w_down, w_up, w_out, *_in_hbm(deps))


def _layer_fwd(l, x, p, get_weights, bias, tgt=None):
    wts = get_weights(l, "in", [x, bias])
    gfull = pl.BlockSpec((DEPTH, D), lambda i, j, k: (0, 0))
    tm = 256

    def inproj_epi(acc, i, j, ex, outs):
        outs[0][...] = acc[:, COL_QKV:COL_Z].astype(bf16)
        outs[1][...] = acc[:, COL_Z:COL_DT].astype(bf16)
        outs[2][...] = acc[:, COL_XBC:D_IN_PAD].astype(bf16)
        outs[3][...] = acc[:, COL_DT:COL_DT + 128]

    qkv, z, xbc, dt, h1 = _matmul(
        "in_proj", "nn", x, wts["w_in"], tm=tm, tn=D_IN_PAD, tk=D, prologue=_rms_prologue(l),
        extras=(p["mix_norm_g"],), extra_specs=(gfull,),
        out_shape=[SDS((S, 768), bf16), SDS((S, 512), bf16), SDS((S, 1024), bf16), SDS((S, 128), f32), SDS((S, D), bf16)],
        out_specs=[_rowblk(tm, 768), _rowblk(tm, 512), _rowblk(tm, 1024), _rowblk(tm, 128), _rowblk(tm, D)], epilogue=inproj_epi)
    attn = _attn_fwd(qkv, p["q_gain"], p["k_gain"], p["sinks"], bias, l)
    xact = _conv_fwd(xbc, wts["conv_w"], p["conv_b"], l)
    mix, hs, y_ssd = _ssd_fwd(xact, z, dt, attn, p["dt_bias"], p["a_log"], p["d_skip"], p["ssm_norm_g"], l)
    wts = dict(wts, **get_weights(l, "rest", [mix]))

    x_mid, a_act, r_act, h2, *result = _mlp_fwd(l, x, mix, p["mlp_norm_g"], wts["w_out"], wts["w_up"], wts["w_down"], tgt)
    saved = dict(x=x, h1=h1, qkv=qkv, z=z, xbc=xbc, dt=dt, xact=xact, mix=mix, hs=hs, y_ssd=y_ssd, x_mid=x_mid, h2=h2,
                 a=a_act, r=r_act, wts=wts)
    return (result[0] if tgt is None else tuple(result)), saved


def _layer_bwd(l, dx_out, sv, p, bias, deps, send):
    wts = sv["wts"]

    dw_down = _matmul("dw_down", "tn", sv["a"], dx_out, tm=1024, tn=D, tk=S, out_shape=SDS((D_FF, D), bf16),
                      out_specs=_plain(1024, D), epilogue=_store_epi(bf16), deps=deps)
    deps = send(l, dict(w_down=dw_down))
    du, dx_mid, dg_mlp, dmix = _mlp_bwd_act(l, dx_out, sv["r"], sv["x_mid"], p["mlp_norm_g"], wts["w_down"], wts["w_up"],
                                            wts["w_out"], deps)
    dw_up = _matmul("dw_up", "tn", sv["h2"], du, tm=D, tn=1024, tk=S, out_shape=SDS((D, D_FF), bf16),
                    out_specs=_plain(D, 1024), epilogue=_store_epi(bf16))
    dw_out = _matmul("dw_out", "tn", sv["mix"], dx_mid, tm=D, tn=512, tk=512, out_shape=SDS((D, D), bf16),
                     out_specs=_plain(D, 512), epilogue=_store_epi(bf16))
    deps = send(l, dict(w_up=dw_up, w_out=dw_out))
    gfull = pl.BlockSpec((DEPTH, D), lambda i, j, k: (0, 0))
    grow = pl.BlockSpec((1, D), lambda i, j, k: (0, 0))
    dproj, dbias, dsm_attn = _attn_bwd(sv["qkv"], dmix, p["q_gain"], p["k_gain"], p["sinks"], bias, l, deps)
    dproj, dxact, dsm_ssd = _ssd_bwd(sv["xact"], sv["z"], sv["dt"], dmix, sv["hs"], sv["y_ssd"], p["dt_bias"], p["a_log"],
                                     p["d_skip"], p["ssm_norm_g"], dproj, l)
    dproj, dconv_w, dconv_b = _conv_bwd(sv["xbc"], dxact, wts["conv_w"], p["conv_b"], dproj, l)
    dw_in = _matmul("dw_in", "tn", sv["h1"], dproj, tm=D, tn=640, tk=S, out_shape=SDS((D, D_IN_PAD), bf16),
                    out_specs=_plain(D, 640), epilogue=_store_epi(bf16))
    deps = send(l, dict(w_in=_w_in_slabs(dw_in)))
    dx, dg_mix = _matmul(
        "in_proj_dh", "nt", dproj, wts["w_in"], tm=256, tn=D, tk=D_IN_PAD, out_shape=[SDS((S, D), f32), SDS((1, D), f32)],
        out_specs=[_plain(256, D), grow], epilogue=_rms_bwd_epilogue(l),
        extras=(sv["x"], p["mix_norm_g"], dx_mid), extra_specs=(_plain(256, D), gfull, _plain(256, D)), deps=deps)
    small = dict(mix_norm_g=dg_mix, mlp_norm_g=dg_mlp, conv_w=dconv_w, conv_b=dconv_b, ssd=dsm_ssd, attn=dsm_attn, dbias=dbias)
    return dx, small, deps


def _local_step(x, tgt, p, get_weights, send):
    onehot_t = jnp.asarray(_onehot_buckets())
    bias = _bias_build(p["rel_bias"].T, onehot_t).reshape(NQ, BLK, 2 * BLK)
    saved = []
    h = x
    for l in range(DEPTH):
        h, sv = _layer_fwd(l, h, p, get_weights, bias, tgt if l == DEPTH - 1 else None)
        saved.append(sv)
    dx, loss = h
    smalls = [None] * DEPTH
    deps = ()
    for l in reversed(range(DEPTH)):
        dx, smalls[l], deps = _layer_bwd(l, dx, saved[l], p, bias, deps, send)
    drel_t = _bias_grad(smalls[0]["dbias"].reshape(NQ, -1), smalls[1]["dbias"].reshape(NQ, -1), onehot_t)
    return dx, _pack_small_grads(smalls, drel_t, loss)


WEIGHT_ORDER = ("mix_norm_g", "w_in", "q_gain", "k_gain", "sinks", "rel_bias", "conv_w", "conv_b", "dt_bias", "a_log", "d_skip",
                "ssm_norm_g", "w_out", "mlp_norm_g", "w_up", "w_down")


def kernel(x, mix_norm_g, w_in, q_gain, k_gain, sinks, rel_bias, conv_w, conv_b, dt_bias, a_log, d_skip, ssm_norm_g, w_out, mlp_norm_g, w_up, w_down, loss_target, m_mix_norm_g, m_w_in, m_q_gain, m_k_gain, m_sinks, m_rel_bias, m_conv_w, m_conv_b, m_dt_bias, m_a_log, m_d_skip, m_ssm_norm_g, m_w_out, m_mlp_norm_g, m_w_up, m_w_down, v_mix_norm_g, v_w_in, v_q_gain, v_k_gain, v_sinks, v_rel_bias, v_conv_w, v_conv_b, v_dt_bias, v_a_log, v_d_skip, v_ssm_norm_g, v_w_out, v_mlp_norm_g, v_w_up, v_w_down):
    w = dict(mix_norm_g=mix_norm_g, w_in=w_in, q_gain=q_gain, k_gain=k_gain, sinks=sinks, rel_bias=rel_bias, conv_w=conv_w,
             conv_b=conv_b, dt_bias=dt_bias, a_log=a_log, d_skip=d_skip, ssm_norm_g=ssm_norm_g, w_out=w_out,
             mlp_norm_g=mlp_norm_g, w_up=w_up, w_down=w_down)
    m = dict(mix_norm_g=m_mix_norm_g, w_in=m_w_in, q_gain=m_q_gain, k_gain=m_k_gain, sinks=m_sinks, rel_bias=m_rel_bias,
             conv_w=m_conv_w, conv_b=m_conv_b, dt_bias=m_dt_bias, a_log=m_a_log, d_skip=m_d_skip, ssm_norm_g=m_ssm_norm_g,
             w_out=m_w_out, mlp_norm_g=m_mlp_norm_g, w_up=m_w_up, w_down=m_w_down)
    v = dict(mix_norm_g=v_mix_norm_g, w_in=v_w_in, q_gain=v_q_gain, k_gain=v_k_gain, sinks=v_sinks, rel_bias=v_rel_bias,
             conv_w=v_conv_w, conv_b=v_conv_b, dt_bias=v_dt_bias, a_log=v_a_log, d_skip=v_d_skip, ssm_norm_g=v_ssm_norm_g,
             w_out=v_w_out, mlp_norm_g=v_mlp_norm_g, w_up=v_w_up, w_down=v_w_down)
    big = ("w_in", "w_out", "w_up", "w_down")

    my_idx = _dev_index(*_my_place()).astype(jnp.int32).reshape(1)

    fulls = {n: _cast_to_full("cast_" + n, w[n], KIND[n], FULL_SHAPE[n], my_idx, bf16) for n in big}
    conv_full = _cast_to_full("cast_conv_w", conv_w.reshape(1, DEPTH * 4, 128), "stack", (N_DEV, DEPTH * 4, 128), my_idx, f32)[0]
    rest = ["w_out", "w_up", "w_down"]
    g0 = _gather_start("gather0", ["w_in", "conv_w"], [fulls["w_in"][0], conv_full], ())
    g1 = _gather_start("gather1", rest, [fulls[n][0] for n in rest], (g0["token"],))
    g2 = _gather_start("gather2", ["w_in"], [fulls["w_in"][1]], (g1["token"],))
    g3 = _gather_start("gather3", rest, [fulls[n][1] for n in rest], (g2["token"],))
    held = {}
    flat = lambda a: a.reshape(a.shape[0] * a.shape[1], a.shape[2])
    adam_in = {n: (flat(w[n]), flat(m[n]), flat(v[n])) for n in big}

    def get_weights(l, part, after):
        if l == 0 and part == "in":
            full_in, full_conv = _gather_finish("gather0", ["w_in", "conv_w"], g0,
                                                list(after) + [g3["token"], adam_in["w_in"][1], adam_in["w_in"][2]])
            held["conv_w"] = jnp.transpose(full_conv.reshape(N_DEV, DEPTH, 4, 128), (1, 2, 0, 3)).reshape(DEPTH, 4, D_CONV)
            return dict(w_in=_w_in_assemble(full_in), conv_w=held["conv_w"])
        if part == "in":
            return dict(w_in=_w_in_assemble(_gather_finish("gather2", ["w_in"], g2, after)[0]), conv_w=held["conv_w"])
        full = _gather_finish("gather1" if l == 0 else "gather3", rest, g1 if l == 0 else g3, after)
        return {n: f[None] for n, f in zip(rest, full)}

    pending = []

    def send(l, grads):
        names = list(grads)
        started = _exchange_start("exchange%d_%s" % (l, names[0]), names, [grads[n] for n in names], ())
        pending.append((l, names, started))
        return (started["token"],)

    dx, small_part = _local_step(x.reshape(S, D), loss_target.reshape(S, D), w, get_weights, send)

    small = _small_exchange_start(small_part, ())
    tiles = dict(w_in=256, w_out=128, w_up=256, w_down=256)
    outs_of = {n: None for n in big}
    after = [dx, small["token"]]
    for l, names, started in pending:
        bufs = _split_wait("exchange%d_%s_wait" % (l, names[0]), started, after)
        for t, n in enumerate(names):
            outs_of[n] = _adamw_layer("adamw_%s%d" % (n, l), KIND[n], l, *adam_in[n],
                                      bufs[len(names) + t], bufs[t], my_idx, outs_of[n], tiles[n])
        after = [outs_of[names[-1]][0]]
    res = {n: [o.reshape(w[n].shape) for o in outs_of[n]] for n in big}
    small_part, small_land = _split_wait("small_exchange_wait", small, after)
    small_outs = _adamw_small(small_part, small_land, w, m, v)
    loss = small_outs[0][0, 0]
    for k, name in enumerate(SMALL_NAMES):
        res[name] = small_outs[1 + 4 * k:5 + 4 * k]

    result = [loss, dx.reshape(1, S, D)]
    for k in range(4):
        result += [res[name][k] for name in WEIGHT_ORDER]
    return tuple(result)
```

```python
import functools
import math

import numpy as np
import jax
import jax.numpy as jnp
from jax import lax
from jax.experimental import pallas as pl
from jax.experimental.pallas import tpu as pltpu

f32 = jnp.float32
bf16 = jnp.bfloat16
SDS = jax.ShapeDtypeStruct
MESH = pl.DeviceIdType.MESH
HIGHEST = lax.Precision.HIGHEST

S = 2048
D = 1024
DEPTH = 2
BLK = 128
NBLK = S // BLK
HD = 64
NQ = 8
NKV = 2
NSSM = 8
NGRP = 2
NSTATE = 128
D_ATTN = 512
D_SSM = 512
D_CONV = 1024
D_FF = 4096
D_IN = 2312
D_IN_PAD = 2560
COL_QKV, COL_Z, COL_DT, COL_XBC = 0, 768, 1280, 1536
IN_SEGMENTS = ((0, 1280, 0), (1280, 2304, COL_XBC), (2304, 2312, COL_DT))
N_BUCKETS = 32
EPS = 1e-6
N_DEV = 8
VMEM_LIMIT = 48 * 1024 * 1024

ADAM_LR = 0.001
ADAM_B1 = 0.9
ADAM_B2 = 0.999
ADAM_EPS = 1e-08
ADAM_WD = 0.01
ADAM_STEP = 10

NT_DIMS = (((1,), (1,)), ((), ()))
TN_DIMS = (((0,), (0,)), ((), ()))
NN_DIMS = (((1,), (0,)), ((), ()))

ROW_MIXG = 0
ROW_MLPG = 2
ROW_CONVB = 4
ROW_SSMG = 6
ROW_MISC = 8
ROW_RELB = 10
ROW_CONVW = 18
ROW_LOSS = 26
SMALL_ROWS = 32
LANE_QG, LANE_KG, LANE_SINK, LANE_DTB, LANE_ALOG, LANE_DSKIP = 0, 64, 128, 256, 384, 512


def _dot(a, b, dims):
    return lax.dot_general(a, b, dims, preferred_element_type=f32)


def _cparams(n_axes):
    return pltpu.CompilerParams(dimension_semantics=("arbitrary",) * n_axes, vmem_limit_bytes=VMEM_LIMIT)


def _sum11(v):
    return jnp.sum(jnp.sum(v, axis=1, keepdims=True), axis=0, keepdims=True)


def _sigmoid(v):
    return 1.0 / (1.0 + jnp.exp(-v))


ANY_SPEC = pl.BlockSpec(memory_space=pl.ANY)


def _in_hbm(args):
    return [pltpu.with_memory_space_constraint(a, pltpu.HBM) if a.size >= 65536 else a for a in args]


def _out_hbm(out_shape):
    one = lambda s: pltpu.HBM(s.shape, s.dtype) if math.prod(s.shape) >= 65536 else s
    return [one(s) for s in out_shape] if isinstance(out_shape, (list, tuple)) else one(out_shape)


def _matmul(name, mode, a, b, *, layer=0, tm, tn, tk, out_shape, out_specs, epilogue, extras=(), extra_specs=(), deps=(),
            prologue=None):
    extras = tuple(extras) + tuple(deps)
    extra_specs = tuple(extra_specs) + (ANY_SPEC,) * len(deps)
    if mode == "tn":
        t_dim, m_dim = a.shape
        n_dim = b.shape[1]
        grid = (m_dim // tm, n_dim // tn, t_dim // tk)
        a_spec = pl.BlockSpec((tk, tm), lambda i, j, k: (k, i))
        b_spec = pl.BlockSpec((tk, tn), lambda i, j, k: (k, j))
        dims = TN_DIMS
    elif mode == "nn":
        m_dim, k_dim = a.shape
        n_dim = b.shape[-1]
        grid = (m_dim // tm, n_dim // tn, k_dim // tk)
        a_spec = pl.BlockSpec((tm, tk), lambda i, j, k: (i, k))
        b_spec = pl.BlockSpec((None, tk, tn), lambda i, j, k: (layer, k, j))
        dims = NN_DIMS
    else:
        m_dim, k_dim = a.shape
        n_dim = b.shape[-2]
        grid = (m_dim // tm, n_dim // tn, k_dim // tk)
        a_spec = pl.BlockSpec((tm, tk), lambda i, j, k: (i, k))
        b_spec = pl.BlockSpec((None, tn, tk), lambda i, j, k: (layer, j, k))
        dims = NT_DIMS
    nk = grid[2]
    n_ex = len(extras)

    def body(a_ref, b_ref, *rest):
        ex = rest[:n_ex - len(deps)]
        outs = rest[n_ex:-1]
        acc = rest[-1]
        i = pl.program_id(0)
        j = pl.program_id(1)
        k = pl.program_id(2)
        lhs = a_ref[...].astype(bf16) if prologue is None else prologue(a_ref, ex, outs)
        part = _dot(lhs, b_ref[...].astype(bf16), dims)
        if nk == 1:
            epilogue(part, i, j, ex, outs)
        else:
            @pl.when(k == 0)
            def _():
                acc[...] = part

            @pl.when(k > 0)
            def _():
                acc[...] += part

            @pl.when(k == nk - 1)
            def _():
                epilogue(acc[...], i, j, ex, outs)

    return pl.pallas_call(
        body, grid=grid, in_specs=[a_spec, b_spec, *extra_specs], out_specs=out_specs, out_shape=_out_hbm(out_shape),
        scratch_shapes=[pltpu.VMEM((tm, tn) if nk > 1 else (8, 128), f32)], name=name, compiler_params=_cparams(3),
    )(*_in_hbm([a]), b, *_in_hbm(extras))


def _rms_bwd_epilogue(layer):
    def epi(acc, i, j, ex, outs):
        x_ref, g_ref, dres_ref = ex
        dx_ref, dg_ref = outs
        xv = x_ref[...]
        r = lax.rsqrt(jnp.mean(xv * xv, axis=-1, keepdims=True) + EPS)
        xhat = xv * r
        w = acc * g_ref[layer:layer + 1, :]
        dx_ref[...] = dres_ref[...] + r * (w - xhat * jnp.mean(xhat * w, axis=-1, keepdims=True))
        dg = jnp.sum(acc * xhat, axis=0, keepdims=True)

        @pl.when(i == 0)
        def _():
            dg_ref[...] = dg

        @pl.when(i > 0)
        def _():
            dg_ref[...] += dg
    return epi


def _own_slab_spec(kind, tr, cols, nblk):
    if kind == "stack":
        return pl.BlockSpec((None, tr, cols), lambda i, idx: (idx[0], i, 0))
    if kind == "cols512":
        return pl.BlockSpec((tr, cols), lambda i, idx: (i, idx[0]))
    return pl.BlockSpec((tr, cols), lambda i, idx: (idx[0] * nblk + i, 0))


def _cast_to_full(name, w, kind, full_shape, my_idx, dtype):
    n_layers, rows, cols = w.shape
    tr = min(rows, 256)
    nblk = rows // tr

    def body(idx_ref, w_ref, *o_refs):
        for l in range(n_layers):
            o_refs[l][...] = w_ref[l].astype(dtype)

    grid_spec = pltpu.PrefetchScalarGridSpec(
        num_scalar_prefetch=1, grid=(nblk,), in_specs=[pl.BlockSpec((n_layers, tr, cols), lambda i, idx: (0, i, 0))],
        out_specs=[_own_slab_spec(kind, tr, cols, nblk)] * n_layers)
    return pl.pallas_call(body, grid_spec=grid_spec, out_shape=_out_hbm([SDS(full_shape, dtype)] * n_layers), name=name,
                          compiler_params=_cparams(1))(*_in_hbm([my_idx, w]))


def _adamw_math(w, m, v, g):
    m_new = ADAM_B1 * m + (1.0 - ADAM_B1) * g
    v_new = ADAM_B2 * v + (1.0 - ADAM_B2) * (g * g)
    m_hat = m_new / (1.0 - ADAM_B1 ** ADAM_STEP)
    v_hat = v_new / (1.0 - ADAM_B2 ** ADAM_STEP)
    delta = -ADAM_LR * (m_hat / (jnp.sqrt(v_hat) + ADAM_EPS) + ADAM_WD * w)
    return delta, m_new, v_new


def _adamw_layer(name, kind, layer, w, m, v, land, g_full, my_idx, prev, tr):
    rows2, cols = w.shape
    rows = rows2 // DEPTH
    nblk = rows // tr
    own_spec = _own_slab_spec(kind, tr, cols, nblk)
    n_prev = 0 if prev is None else 4

    def body(idx_ref, w_ref, m_ref, v_ref, land_ref, own_ref, *rest):
        g_ref, d_ref, mo_ref, vo_ref = rest[n_prev:]
        me = idx_ref[0]
        g = None
        for p in range(N_DEV):
            part = jnp.where(me == p, own_ref[...], land_ref[p]).astype(f32)
            g = part if g is None else g + part
        delta, m_new, v_new = _adamw_math(w_ref[...], m_ref[...], v_ref[...], g)
        g_ref[...] = g
        d_ref[...] = delta
        mo_ref[...] = m_new
        vo_ref[...] = v_new

    blk = pl.BlockSpec((tr, cols), lambda i, idx: (layer * nblk + i, 0))
    grid_spec = pltpu.PrefetchScalarGridSpec(
        num_scalar_prefetch=1, grid=(nblk,),
        in_specs=[blk, blk, blk, pl.BlockSpec((N_DEV, tr, cols), lambda i, idx: (0, i, 0)), own_spec] + [ANY_SPEC] * n_prev,
        out_specs=[blk, blk, blk, blk])
    aliases = {} if prev is None else {6 + k: k for k in range(4)}
    return pl.pallas_call(
        body, grid_spec=grid_spec, out_shape=_out_hbm([SDS((rows2, cols), f32)] * 4), name=name, input_output_aliases=aliases,
        compiler_params=_cparams(1),
    )(*_in_hbm([my_idx, w, m, v, land, g_full, *([] if prev is None else prev)]))


def _bucket_table():
    qi = np.arange(BLK)[:, None]
    kj = np.arange(2 * BLK)[None, :]
    dist = qi + BLK - kj
    dcl = np.clip(dist, 0, None)
    max_exact = N_BUCKETS // 2
    d_f = np.maximum(dcl, 1).astype(np.float32)
    large = max_exact + (np.log(d_f / np.float32(max_exact)) / np.float32(math.log(128 / max_exact))
                         * np.float32(N_BUCKETS - max_exact)).astype(np.int32)
    large = np.minimum(large, N_BUCKETS - 1)
    bucket = np.where(dcl < max_exact, dcl, large)
    in_window = (dist >= 0) & (dist < BLK)
    return bucket.astype(np.int32), in_window


def _onehot_buckets():
    bucket, _ = _bucket_table()
    oh = (bucket.reshape(-1)[None, :] == np.arange(N_BUCKETS)[:, None]).astype(np.float32)
    return oh


def _bias_build(rel_bias_t, onehot_t):
    def body(r_ref, o_ref, out_ref):
        r = r_ref[...]
        hi = r.astype(bf16)
        r1 = r - hi.astype(f32)
        mid = r1.astype(bf16)
        lo = (r1 - mid.astype(f32)).astype(bf16)
        oh = o_ref[...]
        out_ref[...] = _dot(hi, oh, NN_DIMS) + _dot(mid, oh, NN_DIMS) + _dot(lo, oh, NN_DIMS)

    tn = 4096
    return pl.pallas_call(
        body, grid=(BLK * 2 * BLK // tn,),
        in_specs=[pl.BlockSpec((NQ, N_BUCKETS), lambda i: (0, 0)), pl.BlockSpec((N_BUCKETS, tn), lambda i: (0, i))],
        out_specs=pl.BlockSpec((NQ, tn), lambda i: (0, i)), out_shape=SDS((NQ, BLK * 2 * BLK), f32), name="bias_build",
        compiler_params=_cparams(1),
    )(rel_bias_t, onehot_t)


def _bias_grad(dbias0, dbias1, onehot_t):
    tn = 4096
    nsteps = BLK * 2 * BLK // tn

    def body(a_ref, b_ref, o_ref, out_ref):
        g = a_ref[...] + b_ref[...]
        hi = g.astype(bf16)
        lo = (g - hi.astype(f32)).astype(bf16)
        part = _dot(hi, o_ref[...], NT_DIMS) + _dot(lo, o_ref[...], NT_DIMS)

        @pl.when(pl.program_id(0) == 0)
        def _():
            out_ref[...] = part

        @pl.when(pl.program_id(0) > 0)
        def _():
            out_ref[...] += part

    return pl.pallas_call(
        body, grid=(nsteps,),
        in_specs=[pl.BlockSpec((NQ, tn), lambda i: (0, i)), pl.BlockSpec((NQ, tn), lambda i: (0, i)),
                  pl.BlockSpec((N_BUCKETS, tn), lambda i: (0, i))],
        out_specs=pl.BlockSpec((NQ, N_BUCKETS), lambda i: (0, 0)), out_shape=SDS((NQ, N_BUCKETS), f32), name="bias_grad",
        compiler_params=_cparams(1),
    )(dbias0, dbias1, onehot_t)


def _attn_mask(n):
    qi = lax.broadcasted_iota(jnp.int32, (BLK, 2 * BLK), 0)
    kj = lax.broadcasted_iota(jnp.int32, (BLK, 2 * BLK), 1)
    dist = qi + BLK - kj
    first_key = jnp.where(n > 0, 0, BLK)
    return (dist >= 0) & (dist < BLK) & (kj >= first_key)


def _row_mean(a):
    return jnp.mean(a, axis=-1, keepdims=True)


def _head_norm(t, gain):
    r = lax.rsqrt(_row_mean(t * t) + EPS)
    that = t * r
    return that, r, that * gain


def _softmax_with_sink(s, sink):
    m = jnp.maximum(jnp.max(s, axis=-1, keepdims=True), sink)
    p = jnp.exp(s - m)
    psink = jnp.exp(sink - m)
    inv = 1.0 / (jnp.sum(p, axis=-1, keepdims=True) + psink)
    return p * inv, psink * inv


GQ = NQ // NKV


def _attn_fwd(qkv, q_gain, k_gain, sinks, bias, layer):
    def body(q_ref, kc_ref, kp_ref, vc_ref, vp_ref, qg_ref, kg_ref, sk_ref, bias_ref, o_ref):
        m = pl.program_id(0)
        qg = qg_ref[layer:layer + 1, :]
        kg = kg_ref[layer:layer + 1, :]
        grp = range(NKV)
        chains = [(b, j) for b in range(2) for j in grp]
        masks = [jnp.tile(_attn_mask(2 * m + b), (GQ, 1)) for b in range(2)]
        kblk = [[kp_ref[:, pl.ds(HD * j, HD)].astype(f32), kc_ref[0:BLK, pl.ds(HD * j, HD)].astype(f32),
                 kc_ref[BLK:, pl.ds(HD * j, HD)].astype(f32)] for j in grp]
        vblk = [[vp_ref[:, pl.ds(HD * j, HD)].astype(bf16), vc_ref[0:BLK, pl.ds(HD * j, HD)].astype(bf16),
                 vc_ref[BLK:, pl.ds(HD * j, HD)].astype(bf16)] for j in grp]
        knb = [[_head_norm(kblk[j][t], kg)[2].astype(bf16) for t in range(3)] for j in grp]
        kn_b = {(b, j): jnp.concatenate([knb[j][b], knb[j][b + 1]], axis=0) for b, j in chains}
        vbs = {(b, j): jnp.concatenate([vblk[j][b], vblk[j][b + 1]], axis=0) for b, j in chains}
        rows = {}
        for b, j in chains:
            heads = [GQ * j + g for g in range(GQ)]
            rows[b, j] = (jnp.concatenate([q_ref[pl.ds(BLK * b, BLK), pl.ds(HD * h, HD)] for h in heads], axis=0).astype(f32),
                          jnp.concatenate([jnp.broadcast_to(sk_ref[layer:layer + 1, h:h + 1], (BLK, 1)) for h in heads], axis=0))
        qn_b = {c: _head_norm(rows[c][0], qg)[2].astype(bf16) for c in chains}
        ss = {(b, j): _dot(qn_b[b, j], kn_b[b, j], NT_DIMS) * (HD ** -0.5) + bias_ref[GQ * j:GQ * (j + 1)].reshape(GQ * BLK, 2 * BLK)
              for b, j in chains}
        ps = {(b, j): _softmax_with_sink(jnp.where(masks[b], ss[b, j], -jnp.inf), rows[b, j][1])[0] for b, j in chains}
        outs = {c: _dot(ps[c].astype(bf16), vbs[c], NN_DIMS).astype(bf16) for c in chains}
        for b, j in chains:
            for g in range(GQ):
                o_ref[pl.ds(BLK * b, BLK), pl.ds(HD * (GQ * j + g), HD)] = outs[b, j][BLK * g:BLK * (g + 1), :]

    prev = lambda m: jnp.maximum(2 * m - 1, 0)
    small = lambda shape: pl.BlockSpec(shape, lambda m: (0,) * len(shape))
    return pl.pallas_call(
        body, grid=(NBLK // 2,),
        in_specs=[pl.BlockSpec((2 * BLK, D_ATTN), lambda m: (m, 0)),
                  pl.BlockSpec((2 * BLK, 128), lambda m: (m, 4)), pl.BlockSpec((BLK, 128), lambda m: (prev(m), 4)),
                  pl.BlockSpec((2 * BLK, 128), lambda m: (m, 5)), pl.BlockSpec((BLK, 128), lambda m: (prev(m), 5)),
                  small((DEPTH, HD)), small((DEPTH, HD)), small((DEPTH, NQ)), small((NQ, BLK, 2 * BLK))],
        out_specs=pl.BlockSpec((2 * BLK, D_ATTN), lambda m: (m, 0)), out_shape=_out_hbm(SDS((S, D_ATTN), bf16)),
        name="attn_fwd", compiler_params=_cparams(1),
    )(*_in_hbm([qkv, qkv, qkv, qkv, qkv, q_gain, k_gain, sinks, bias]))


def _attn_bwd(qkv, dmix, q_gain, k_gain, sinks, bias, layer, deps=()):
    def body(q_ref, kc_ref, kp_ref, vc_ref, vp_ref, do_ref, qg_ref, kg_ref, sk_ref, bias_ref, *rest):
        dqkv_ref, dbias_ref, dsm_ref, carry = rest[len(deps):]
        i = pl.program_id(0)
        m = NBLK // 2 - 1 - i
        qg = qg_ref[layer:layer + 1, :]
        kg = kg_ref[layer:layer + 1, :]
        lane = lax.broadcasted_iota(jnp.int32, (1, 128), 1)

        @pl.when(i == 0)
        def _():
            carry[...] = jnp.zeros_like(carry)
            dbias_ref[...] = jnp.zeros_like(dbias_ref)
            dsm_ref[...] = jnp.zeros_like(dsm_ref)

        grp = range(NKV)
        chains = [(b, j) for b in range(2) for j in grp]
        masks = [jnp.tile(_attn_mask(2 * m + b), (GQ, 1)) for b in range(2)]
        kblk = [[kp_ref[:, pl.ds(HD * j, HD)].astype(f32), kc_ref[0:BLK, pl.ds(HD * j, HD)].astype(f32),
                 kc_ref[BLK:, pl.ds(HD * j, HD)].astype(f32)] for j in grp]
        vblk = [[vp_ref[:, pl.ds(HD * j, HD)].astype(bf16), vc_ref[0:BLK, pl.ds(HD * j, HD)].astype(bf16),
                 vc_ref[BLK:, pl.ds(HD * j, HD)].astype(bf16)] for j in grp]
        knorm = [[_head_norm(kblk[j][t], kg) for t in range(3)] for j in grp]
        kn_b = {(b, j): jnp.concatenate([knorm[j][b][2].astype(bf16), knorm[j][b + 1][2].astype(bf16)], axis=0) for b, j in chains}
        vbs = {(b, j): jnp.concatenate([vblk[j][b], vblk[j][b + 1]], axis=0) for b, j in chains}
        rows, do_b = {}, {}
        for b, j in chains:
            heads = [GQ * j + g for g in range(GQ)]
            qrows = pl.ds(BLK * b, BLK)
            rows[b, j] = (jnp.concatenate([q_ref[qrows, pl.ds(HD * h, HD)] for h in heads], axis=0).astype(f32),
                          jnp.concatenate([jnp.broadcast_to(sk_ref[layer:layer + 1, h:h + 1], (BLK, 1)) for h in heads], axis=0))
            do_b[b, j] = jnp.concatenate([do_ref[qrows, pl.ds(HD * h, HD)] for h in heads], axis=0).astype(bf16)
        qnorm = {c: _head_norm(rows[c][0], qg) for c in chains}
        qn_b = {c: qnorm[c][2].astype(bf16) for c in chains}
        ss = {(b, j): _dot(qn_b[b, j], kn_b[b, j], NT_DIMS) * (HD ** -0.5) + bias_ref[GQ * j:GQ * (j + 1)].reshape(GQ * BLK, 2 * BLK)
              for b, j in chains}
        sm = {(b, j): _softmax_with_sink(jnp.where(masks[b], ss[b, j], -jnp.inf), rows[b, j][1]) for b, j in chains}
        dps = {c: _dot(do_b[c], vbs[c], NT_DIMS) for c in chains}
        deltas = {c: jnp.sum(sm[c][0] * dps[c], axis=-1, keepdims=True) for c in chains}
        dss = {c: sm[c][0] * (dps[c] - deltas[c]) for c in chains}
        ds_b = {c: (dss[c] * (HD ** -0.5)).astype(bf16) for c in chains}
        dqn = {c: _dot(ds_b[c], kn_b[c], NN_DIMS) for c in chains}
        dkn = {c: _dot(ds_b[c], qn_b[c], TN_DIMS) for c in chains}
        dvs = {c: _dot(sm[c][0].astype(bf16), do_b[c], TN_DIMS) for c in chains}
        dqg = jnp.zeros((1, HD), f32)
        dkg = jnp.zeros((1, HD), f32)
        dsink = jnp.zeros((1, 128), f32)
        for b, j in chains:
            dbias_ref[GQ * j:GQ * (j + 1)] += dss[b, j].reshape(GQ, BLK, 2 * BLK)
            dsk = sm[b, j][1] * deltas[b, j]
            for g in range(GQ):
                dsink = dsink + jnp.where(lane == GQ * j + g, -_sum11(dsk[BLK * g:BLK * (g + 1), :]), 0.0)
            qhat, rq, _ = qnorm[b, j]
            w = dqn[b, j] * qg
            dq = rq * (w - qhat * _row_mean(qhat * w))
            for g in range(GQ):
                dqkv_ref[pl.ds(BLK * b, BLK), pl.ds(HD * (GQ * j + g), HD)] = dq[BLK * g:BLK * (g + 1), :].astype(bf16)
            dqg = dqg + jnp.sum(dqn[b, j] * qhat, axis=0, keepdims=True)
        for j in grp:
            dkn_t = [dkn[0, j][:BLK, :], dkn[0, j][BLK:, :] + dkn[1, j][:BLK, :], dkn[1, j][BLK:, :]]
            dv_t = [dvs[0, j][:BLK, :], dvs[0, j][BLK:, :] + dvs[1, j][:BLK, :], dvs[1, j][BLK:, :]]
            dk_t = []
            for t in range(3):
                khat, rk, _ = knorm[j][t]
                w = dkn_t[t] * kg
                dk_t.append(rk * (w - khat * _row_mean(khat * w)))
                dkg = dkg + jnp.sum(dkn_t[t] * khat, axis=0, keepdims=True)
            kcols, vcols = pl.ds(D_ATTN + HD * j, HD), pl.ds(D_ATTN + 128 + HD * j, HD)
            dqkv_ref[BLK:, kcols] = (dk_t[2] + carry[:, pl.ds(HD * j, HD)]).astype(bf16)
            dqkv_ref[BLK:, vcols] = (dv_t[2] + carry[:, pl.ds(128 + HD * j, HD)]).astype(bf16)
            dqkv_ref[0:BLK, kcols] = dk_t[1].astype(bf16)
            dqkv_ref[0:BLK, vcols] = dv_t[1].astype(bf16)
            carry[:, pl.ds(HD * j, HD)] = dk_t[0]
            carry[:, pl.ds(128 + HD * j, HD)] = dv_t[0]
        dsm_ref[0:1, 0:HD] += dqg
        dsm_ref[1:2, 0:HD] += dkg
        dsm_ref[2:3, :] += dsink

    rev = lambda i: NBLK // 2 - 1 - i
    prev = lambda i: jnp.maximum(NBLK - 3 - 2 * i, 0)
    small = lambda shape: pl.BlockSpec(shape, lambda i: (0,) * len(shape))
    return pl.pallas_call(
        body, grid=(NBLK // 2,),
        in_specs=[pl.BlockSpec((2 * BLK, D_ATTN), lambda i: (rev(i), 0)),
                  pl.BlockSpec((2 * BLK, 128), lambda i: (rev(i), 4)), pl.BlockSpec((BLK, 128), lambda i: (prev(i), 4)),
                  pl.BlockSpec((2 * BLK, 128), lambda i: (rev(i), 5)), pl.BlockSpec((BLK, 128), lambda i: (prev(i), 5)),
                  pl.BlockSpec((2 * BLK, D_ATTN), lambda i: (rev(i), 0)),
                  small((DEPTH, HD)), small((DEPTH, HD)), small((DEPTH, NQ)), small((NQ, BLK, 2 * BLK))] + [ANY_SPEC] * len(deps),
        out_specs=[pl.BlockSpec((2 * BLK, 768), lambda i: (rev(i), COL_QKV // 768)), small((NQ, BLK, 2 * BLK)), small((8, 128))],
        out_shape=_out_hbm([SDS((S, D_IN_PAD), bf16), SDS((NQ, BLK, 2 * BLK), f32), SDS((8, 128), f32)]),
        scratch_shapes=[pltpu.VMEM((BLK, 256), f32)], name="attn_bwd", compiler_params=_cparams(1),
    )(*_in_hbm([qkv, qkv, qkv, qkv, qkv, dmix, q_gain, k_gain, sinks, bias, *deps]))


CONV_TC = 256


def _shift_down(u, s):
    if s == 0:
        return u
    rows = lax.broadcasted_iota(jnp.int32, u.shape, 0)
    return jnp.where(rows >= s, pltpu.roll(u, s, 0), 0.0)


def _shift_up(u, s):
    if s == 0:
        return u
    rows = lax.broadcasted_iota(jnp.int32, u.shape, 0)
    return jnp.where(rows < u.shape[0] - s, pltpu.roll(u, u.shape[0] - s, 0), 0.0)


def _conv_specs():
    return [pl.BlockSpec((S, CONV_TC), lambda c: (0, c)),
            pl.BlockSpec((None, 4, CONV_TC), lambda c: (0, 0, c)),
            pl.BlockSpec((DEPTH, CONV_TC), lambda c: (0, c))]


def _conv_pre(u, w_ref, b_ref, layer):
    pre = b_ref[layer:layer + 1, :] + w_ref[3:4, :] * u
    for k in range(3):
        pre = pre + w_ref[k:k + 1, :] * _shift_down(u, 3 - k)
    return pre


def _conv_fwd(xbc, conv_w, conv_b, layer):
    def body(u_ref, w_ref, b_ref, o_ref):
        pre = _conv_pre(u_ref[...].astype(f32), w_ref, b_ref, layer)
        o_ref[...] = pre * _sigmoid(pre)

    specs = _conv_specs()
    specs[1] = pl.BlockSpec((None, 4, CONV_TC), lambda c: (layer, 0, c))
    return pl.pallas_call(
        body, grid=(D_CONV // CONV_TC,), in_specs=specs, out_specs=pl.BlockSpec((S, CONV_TC), lambda c: (0, c)),
        out_shape=_out_hbm(SDS((S, D_CONV), f32)), name="conv_fwd", compiler_params=_cparams(1),
    )(*_in_hbm([xbc, conv_w, conv_b]))


def _conv_bwd(xbc, dact, conv_w, conv_b, dproj, layer):
    def body(u_ref, w_ref, b_ref, da_ref, dproj_in, du_ref, dw_ref, db_ref):
        u = u_ref[...].astype(f32)
        pre = _conv_pre(u, w_ref, b_ref, layer)
        sg = _sigmoid(pre)
        dpre = da_ref[...] * (sg * (1.0 + pre * (1.0 - sg)))
        du = w_ref[3:4, :] * dpre
        for k in range(3):
            du = du + w_ref[k:k + 1, :] * _shift_up(dpre, 3 - k)
        du_ref[...] = du.astype(bf16)
        db_ref[...] = jnp.broadcast_to(jnp.sum(dpre, axis=0, keepdims=True), db_ref.shape)
        dw_ref[...] = jnp.zeros_like(dw_ref)
        for k in range(4):
            dw_ref[k:k + 1, :] = jnp.sum(dpre * _shift_down(u, 3 - k), axis=0, keepdims=True)

    specs = _conv_specs()
    specs[1] = pl.BlockSpec((None, 4, CONV_TC), lambda c: (layer, 0, c))
    col = pl.BlockSpec((S, CONV_TC), lambda c: (0, c))
    row8 = pl.BlockSpec((8, CONV_TC), lambda c: (0, c))
    return pl.pallas_call(
        body, grid=(D_CONV // CONV_TC,), in_specs=[*specs, col, ANY_SPEC],
        out_specs=[pl.BlockSpec((S, CONV_TC), lambda c: (0, COL_XBC // CONV_TC + c)), row8, row8],
        out_shape=_out_hbm([SDS((S, D_IN_PAD), bf16), SDS((8, D_CONV), f32), SDS((8, D_CONV), f32)]), name="conv_bwd",
        input_output_aliases={4: 0}, compiler_params=_cparams(1),
    )(*_in_hbm([xbc, conv_w, conv_b, dact, dproj]))


def _tri():
    return (lax.broadcasted_iota(jnp.int32, (BLK, BLK), 0) >= lax.broadcasted_iota(jnp.int32, (BLK, BLK), 1))


def _ssd_scalars(dt_ref, dtb_ref, alog_ref, layer):
    raw = dt_ref[:, 0:NSSM] + dtb_ref[layer:layer + 1, :]
    dtv = jnp.maximum(raw, 0.0) + jnp.log(1.0 + jnp.exp(-jnp.abs(raw)))
    a = -jnp.exp(alog_ref[layer:layer + 1, :])
    acs = jnp.dot(_tri().astype(f32), dtv * a, preferred_element_type=f32, precision=HIGHEST)
    return raw, dtv, a, acs


HG = NSSM // NGRP
GW = HG * HD


def _lane_expand(cols, g):
    lane_head = lax.broadcasted_iota(jnp.int32, (1, GW), 1) // HD
    out = cols[:, HG * g + HG - 1:HG * g + HG]
    for r in range(HG - 2, -1, -1):
        out = jnp.where(lane_head == r, cols[:, HG * g + r:HG * g + r + 1], out)
    return out


def _row_expand(vals, g):
    row_head = lax.broadcasted_iota(jnp.int32, (GW, 1), 0) // HD
    out = vals[:, HG * g + HG - 1:HG * g + HG]
    for r in range(HG - 2, -1, -1):
        out = jnp.where(row_head == r, vals[:, HG * g + r:HG * g + r + 1], out)
    return out


def _head_rowsums(a, g):
    sel = (lax.broadcasted_iota(jnp.int32, (GW, NSSM), 0) // HD + HG * g == lax.broadcasted_iota(jnp.int32, (GW, NSSM), 1)).astype(bf16)
    hi = a.astype(bf16)
    lo = (a - hi.astype(f32)).astype(bf16)
    return _dot(hi, sel, NN_DIMS) + _dot(lo, sel, NN_DIMS)


def _head_blocksums(v, g):
    sel = (lax.broadcasted_iota(jnp.int32, (GW, NSSM), 0) // HD + HG * g == lax.broadcasted_iota(jnp.int32, (GW, NSSM), 1)).astype(bf16)
    hi = v.astype(bf16)
    lo = (v - hi.astype(f32)).astype(bf16)
    return _dot(hi, sel, TN_DIMS) + _dot(lo, sel, TN_DIMS)


def _ssd_chunk_common(xc_ref, dt_ref, dtb_ref, alog_ref, h_rows, layer):
    raw, dtv, a, acs = _ssd_scalars(dt_ref, dtb_ref, alog_ref, layer)
    acs_t = acs.T
    last = acs[BLK - 1:BLK, :]
    c = dict(raw=raw, dtv=dtv, a=a, acs=acs, last=last, dte=jnp.exp(last - acs), e_all=jnp.exp(acs), cd=jnp.exp(last))
    grp, heads, tri = range(NGRP), range(NSSM), _tri()
    c["bm"] = [xc_ref[:, pl.ds(D_SSM + NSTATE * g, NSTATE)] for g in grp]
    c["bm_b"] = [c["bm"][g].astype(bf16) for g in grp]
    c["cm_b"] = [xc_ref[:, pl.ds(D_SSM + NGRP * NSTATE + NSTATE * g, NSTATE)].astype(bf16) for g in grp]
    c["cb"] = [_dot(c["cm_b"][g], c["bm_b"][g], NT_DIMS) for g in grp]
    c["x"] = [xc_ref[:, pl.ds(GW * g, GW)] for g in grp]
    c["dt"] = [_lane_expand(dtv, g) for g in grp]
    c["xdt"] = [c["x"][g] * c["dt"][g] for g in grp]
    c["xdt_b"] = [c["xdt"][g].astype(bf16) for g in grp]
    c["prev"] = [h_rows(g) for g in grp]
    c["prev_b"] = [c["prev"][g].astype(bf16) for g in grp]
    c["e"] = [_lane_expand(c["e_all"], g) for g in grp]
    c["y_off"] = [_dot(c["cm_b"][g], c["prev_b"][g], NT_DIMS) * c["e"][g] for g in grp]
    c["decay"] = [jnp.exp(jnp.where(tri, acs[:, h:h + 1] - acs_t[h:h + 1, :], -jnp.inf)) for h in heads]
    c["m"] = [c["cb"][h // HG] * c["decay"][h] for h in heads]
    c["m_b"] = [c["m"][h].astype(bf16) for h in heads]
    c["dte_x"] = [_lane_expand(c["dte"], g) for g in grp]
    c["xdte_b"] = [(c["xdt"][g] * c["dte_x"][g]).astype(bf16) for g in grp]
    return c


def _ssd_fwd(xact, z, dt, attn, dt_bias, a_log, d_skip, norm_g, layer):
    def body(xc_ref, z_ref, dt_ref, at_ref, dtb_ref, alog_ref, dsk_ref, ng_ref, mix_ref, hs_ref, y_ref, h_ref):
        n = pl.program_id(0)

        @pl.when(n == 0)
        def _():
            h_ref[...] = jnp.zeros_like(h_ref)

        hs_ref[...] = h_ref[...]
        c = _ssd_chunk_common(xc_ref, dt_ref, dtb_ref, alog_ref, lambda g: h_ref[pl.ds(GW * g, GW), :], layer)
        grp, heads = range(NGRP), range(NSSM)
        y_diag = [_dot(c["m_b"][h], c["xdt_b"][h // HG][:, HD * (h % HG):HD * (h % HG + 1)], NN_DIMS) for h in heads]
        new_st = [_dot(c["xdte_b"][g], c["bm_b"][g], TN_DIMS) for g in grp]
        for h in heads:
            y_ref[:, pl.ds(HD * h, HD)] = y_diag[h]
        dskip = dsk_ref[layer:layer + 1, :]
        for g in grp:
            cols = pl.ds(GW * g, GW)
            y_ref[:, cols] = y_ref[:, cols] + c["y_off"][g] + c["x"][g] * _lane_expand(dskip, g)
            h_ref[cols, :] = c["prev"][g] * _row_expand(c["cd"], g) + new_st[g]
        zv = z_ref[...].astype(f32)
        yz = y_ref[...] * (zv * _sigmoid(zv))
        mix_ref[:, 0:D_ATTN] = at_ref[...]
        for g in grp:
            yg = yz[:, GW * g:GW * (g + 1)]
            rs = lax.rsqrt(jnp.mean(yg * yg, axis=-1, keepdims=True) + EPS)
            mix_ref[:, D_ATTN + GW * g:D_ATTN + GW * (g + 1)] = (yg * rs * ng_ref[layer:layer + 1, GW * g:GW * (g + 1)]).astype(bf16)

    small = lambda shape: pl.BlockSpec(shape, lambda n: (0,) * len(shape))
    return pl.pallas_call(
        body, grid=(NBLK,),
        in_specs=[pl.BlockSpec((BLK, D_CONV), lambda n: (n, 0)), pl.BlockSpec((BLK, D_SSM), lambda n: (n, 0)),
                  pl.BlockSpec((BLK, 128), lambda n: (n, 0)), pl.BlockSpec((BLK, D_ATTN), lambda n: (n, 0)),
                  small((DEPTH, NSSM)), small((DEPTH, NSSM)), small((DEPTH, NSSM)), small((DEPTH, D_SSM))],
        out_specs=[pl.BlockSpec((BLK, D), lambda n: (n, 0)), pl.BlockSpec((None, NSSM * HD, NSTATE), lambda n: (n, 0, 0)),
                   pl.BlockSpec((BLK, D_SSM), lambda n: (n, 0))],
        out_shape=_out_hbm([SDS((S, D), bf16), SDS((NBLK, NSSM * HD, NSTATE), f32), SDS((S, D_SSM), f32)]),
        scratch_shapes=[pltpu.VMEM((NSSM * HD, NSTATE), f32)],
        name="ssd_fwd", compiler_params=_cparams(1),
    )(*_in_hbm([xact, z, dt, attn, dt_bias, a_log, d_skip, norm_g]))


def _ssd_bwd(xact, z, dt, dmix, hs, y, dt_bias, a_log, d_skip, norm_g, dproj, layer):
    def body(xc_ref, z_ref, dt_ref, do_ref, hs_ref, y_ref, dtb_ref, alog_ref, dsk_ref, ng_ref, dproj_in,
             dzdt_ref, dx_ref, dsm_ref, dh_ref, dy_ref):
        i = pl.program_id(0)

        @pl.when(i == 0)
        def _():
            dh_ref[...] = jnp.zeros_like(dh_ref)
            dsm_ref[...] = jnp.zeros_like(dsm_ref)

        c = _ssd_chunk_common(xc_ref, dt_ref, dtb_ref, alog_ref, lambda g: hs_ref[pl.ds(GW * g, GW), :], layer)
        raw, dtv, a = c["raw"], c["dtv"], c["a"]
        grp, heads = range(NGRP), range(NSSM)
        dskip = dsk_ref[layer:layer + 1, :]
        lane8 = lax.broadcasted_iota(jnp.int32, (1, NSSM), 1)
        sub8 = lax.broadcasted_iota(jnp.int32, (NSSM, 1), 0)

        zv = z_ref[...].astype(f32)
        sz = _sigmoid(zv)
        gz = zv * sz
        yv = y_ref[...]
        yz = yv * gz
        for g in grp:
            sl = slice(GW * g, GW * (g + 1))
            yg = yz[:, sl]
            rs = lax.rsqrt(jnp.mean(yg * yg, axis=-1, keepdims=True) + EPS)
            yhat = yg * rs
            dog = do_ref[:, sl]
            w = dog * ng_ref[layer:layer + 1, sl]
            dyz = rs * (w - yhat * jnp.mean(yhat * w, axis=-1, keepdims=True))
            dsm_ref[0:1, sl] += jnp.sum(dog * yhat, axis=0, keepdims=True)
            dy_ref[:, sl] = dyz * gz[:, sl]
            dzdt_ref[:, sl] = (dyz * yv[:, sl] * (sz[:, sl] * (1.0 + zv[:, sl] * (1.0 - sz[:, sl])))).astype(bf16)

        dy = [dy_ref[:, pl.ds(GW * g, GW)] for g in grp]
        dy_b = [dy[g].astype(bf16) for g in grp]
        hl = lambda h: slice(HD * (h % HG), HD * (h % HG + 1))
        dt_off_b = [(dy[g] * c["e"][g]).astype(bf16) for g in grp]
        dcm = [_dot(dt_off_b[g], c["prev_b"][g], NN_DIMS) for g in grp]
        dprev = [_dot(dt_off_b[g], c["cm_b"][g], TN_DIMS) for g in grp]
        yoff_rs = [_head_rowsums(dy[g] * c["y_off"][g], g) for g in grp]
        dhn = [dh_ref[pl.ds(GW * g, GW), :] for g in grp]
        dhn_b = [dhn[g].astype(bf16) for g in grp]
        dprev = [dprev[g] + dhn[g] * _row_expand(c["cd"], g) for g in grp]
        dhn_prev = [dhn[g] * c["prev"][g] for g in grp]
        u = [_dot(c["bm_b"][g], dhn_b[g], NT_DIMS) for g in grp]
        dbm = [_dot(c["xdte_b"][g], dhn_b[g], NN_DIMS) for g in grp]
        ddte_rs = [_head_rowsums(c["xdt"][g] * u[g], g) for g in grp]
        dm = [_dot(dy_b[h // HG][:, hl(h)], c["xdt_b"][h // HG][:, hl(h)], NT_DIMS) for h in heads]
        dxdt_in = [_dot(c["m_b"][h], dy_b[h // HG][:, hl(h)], TN_DIMS) for h in heads]
        dseg = [dm[h] * c["m"][h] for h in heads]
        dmd = [dm[h] * c["decay"][h] for h in heads]
        for h in heads:
            dx_ref[:, pl.ds(HD * h, HD)] = dxdt_in[h]

        tmp = (ddte_rs[0] + ddte_rs[1]) * c["dte"]
        dacs = yoff_rs[0] + yoff_rs[1] - tmp
        dacs_cols = jnp.zeros((NSSM, BLK), f32)
        ddtv = jnp.zeros((BLK, NSSM), f32)
        ddsk = jnp.zeros((BLK, NSSM), f32)
        hp = jnp.zeros((1, NSSM), f32)
        for g in grp:
            cols = pl.ds(GW * g, GW)
            dxdt = dx_ref[:, cols] + u[g] * c["dte_x"][g]
            dx_ref[:, cols] = dy[g] * _lane_expand(dskip, g) + dxdt * c["dt"][g]
            ddtv = ddtv + _head_rowsums(dxdt * c["x"][g], g)
            ddsk = ddsk + _head_rowsums(dy[g] * c["x"][g], g)
            dcb = dmd[HG * g]
            for r in range(1, HG):
                dcb = dcb + dmd[HG * g + r]
            dcb_b = dcb.astype(bf16)
            dx_ref[:, pl.ds(D_SSM + NSTATE * g, NSTATE)] = dbm[g] + _dot(dcb_b, c["cm_b"][g], TN_DIMS)
            dx_ref[:, pl.ds(D_SSM + NGRP * NSTATE + NSTATE * g, NSTATE)] = dcm[g] + _dot(dcb_b, c["bm_b"][g], NN_DIMS)
            dh_ref[cols, :] = dprev[g]
            hp = hp + _head_blocksums(jnp.sum(dhn_prev[g], axis=1, keepdims=True), g)
            for r in range(HG):
                h = HG * g + r
                dacs = dacs + (lane8 == h).astype(f32) * jnp.sum(dseg[h], axis=1, keepdims=True)
                dacs_cols = dacs_cols + (sub8 == h).astype(f32) * jnp.sum(dseg[h], axis=0, keepdims=True)
        dlast = hp * c["cd"] + jnp.sum(tmp, axis=0, keepdims=True)
        ddsk = jnp.sum(ddsk, axis=0, keepdims=True)

        row = lax.broadcasted_iota(jnp.int32, (BLK, 1), 0)
        dacs = dacs - dacs_cols.T + jnp.where(row == BLK - 1, dlast, 0.0)
        dda = lax.dot_general(_tri().astype(f32), dacs, TN_DIMS, preferred_element_type=f32, precision=HIGHEST)
        ddtv = ddtv + dda * a
        da = jnp.sum(dda * dtv, axis=0, keepdims=True)
        draw = ddtv * _sigmoid(raw)
        dzdt_ref[:, D_SSM:] = jnp.zeros((BLK, COL_XBC - COL_DT), bf16)
        dzdt_ref[:, D_SSM:D_SSM + NSSM] = draw.astype(bf16)
        dsm_ref[1:2, 0:NSSM] += jnp.sum(draw, axis=0, keepdims=True)
        dsm_ref[2:3, 0:NSSM] += da * a
        dsm_ref[3:4, 0:NSSM] += ddsk

    rev = lambda i: NBLK - 1 - i
    small = lambda shape: pl.BlockSpec(shape, lambda i: (0,) * len(shape))
    return pl.pallas_call(
        body, grid=(NBLK,),
        in_specs=[pl.BlockSpec((BLK, D_CONV), lambda i: (rev(i), 0)), pl.BlockSpec((BLK, D_SSM), lambda i: (rev(i), 0)),
                  pl.BlockSpec((BLK, 128), lambda i: (rev(i), 0)), pl.BlockSpec((BLK, D_SSM), lambda i: (rev(i), 1)),
                  pl.BlockSpec((None, NSSM * HD, NSTATE), lambda i: (rev(i), 0, 0)), pl.BlockSpec((BLK, D_SSM), lambda i: (rev(i), 0)),
                  small((DEPTH, NSSM)), small((DEPTH, NSSM)), small((DEPTH, NSSM)), small((DEPTH, D_SSM)), ANY_SPEC],
        out_specs=[pl.BlockSpec((BLK, COL_XBC - COL_Z), lambda i: (rev(i), COL_Z // (COL_XBC - COL_Z))),
                   pl.BlockSpec((BLK, D_CONV), lambda i: (rev(i), 0)), small((8, D_SSM))],
        out_shape=_out_hbm([SDS((S, D_IN_PAD), bf16), SDS((S, D_CONV), f32), SDS((8, D_SSM), f32)]),
        scratch_shapes=[pltpu.VMEM((NSSM * HD, NSTATE), f32), pltpu.VMEM((BLK, D_SSM), f32)],
        name="ssd_bwd", input_output_aliases={10: 0}, compiler_params=_cparams(1),
    )(*_in_hbm([xact, z, dt, dmix, hs, y, dt_bias, a_log, d_skip, norm_g, dproj]))


def _my_place():
    return lax.axis_index("x"), lax.axis_index("y"), lax.axis_index("c")


def _dev_index(px, py, pc):
    return 4 * px + 2 * py + pc


def _slab2(kind, ref, idx):
    if kind == "stack":
        return ref.at[idx]
    if kind == "rows128":
        return ref.at[pl.ds(pl.multiple_of(idx * 128, 128), 128), :]
    if kind == "rows512":
        return ref.at[pl.ds(pl.multiple_of(idx * 512, 512), 512), :]
    return ref.at[:, pl.ds(pl.multiple_of(idx * 512, 512), 512)]


def _slab_shape(kind, full_shape):
    if kind == "stack":
        return tuple(full_shape[1:])
    if kind == "rows128":
        return (128, full_shape[1])
    if kind == "rows512":
        return (512, full_shape[1])
    return (full_shape[0], 512)


KIND = dict(w_in="stack", w_out="rows128", w_up="cols512", w_down="rows512", conv_w="stack")
FULL_SHAPE = dict(w_in=(N_DEV, D, D_IN // N_DEV), w_out=(D, D), w_up=(D, D_FF), w_down=(D_FF, D))
HBM_SPEC = pl.BlockSpec(memory_space=pltpu.HBM)
SEM_SPEC = pl.BlockSpec(memory_space=pltpu.SEMAPHORE)
SIDE_EFFECT = pltpu.SideEffectType.DATAFLOW_SIDE_EFFECTING


def _peers_all():
    x, y, c = _my_place()
    return [(x ^ ((r >> 2) & 1), y ^ ((r >> 1) & 1), c ^ (r & 1)) for r in range(1, N_DEV)]


def _split_start(name, bufs, n_copies, plan, deps=()):
    nb = len(bufs)

    def body(*refs):
        ins = refs[:nb]
        send_sems, recv_sems = refs[nb + len(deps)], refs[nb + len(deps) + 1]
        token = refs[-1]
        for i, (src, dst, dev) in enumerate(plan(ins)):
            pltpu.make_async_remote_copy(src_ref=src, dst_ref=dst, send_sem=send_sems.at[i], recv_sem=recv_sems.at[i],
                                         device_id=dev, device_id_type=MESH).start()
        token[...] = jnp.zeros_like(token)

    outs = pl.pallas_call(
        body, name=name,
        out_shape=(pltpu.SemaphoreType.DMA((n_copies,)), pltpu.SemaphoreType.DMA((n_copies,)),
                   *[pltpu.HBM(b.shape, b.dtype) for b in bufs], SDS((8, 128), f32)),
        in_specs=[HBM_SPEC] * nb + [ANY_SPEC] * len(deps),
        out_specs=(SEM_SPEC, SEM_SPEC, *[HBM_SPEC] * nb, pl.BlockSpec(memory_space=pltpu.VMEM)),
        input_output_aliases={i: 2 + i for i in range(nb)},
        compiler_params=pltpu.CompilerParams(has_side_effects=SIDE_EFFECT),
    )(*[pltpu.with_memory_space_constraint(b, pltpu.HBM) for b in bufs], *deps)
    return dict(send=outs[0], recv=outs[1], bufs=list(outs[2:2 + nb]), token=outs[-1], plan=plan, n=n_copies)


def _split_wait(name, started, after):
    bufs = started["bufs"]
    nb = len(bufs)
    plan = started["plan"]

    def body(*refs):
        ins = refs[:nb]
        send_sems, recv_sems = refs[nb], refs[nb + 1]
        for i, (src, dst, dev) in enumerate(plan(ins)):
            cp = pltpu.make_async_remote_copy(src_ref=src, dst_ref=dst, send_sem=send_sems.at[i], recv_sem=recv_sems.at[i],
                                              device_id=dev, device_id_type=MESH)
            cp.wait_send()
            cp.wait_recv()

    outs = pl.pallas_call(
        body, name=name, out_shape=tuple(pltpu.HBM(b.shape, b.dtype) for b in bufs),
        in_specs=[HBM_SPEC] * nb + [SEM_SPEC, SEM_SPEC] + [ANY_SPEC] * len(after), out_specs=(HBM_SPEC,) * nb,
        input_output_aliases={i: i for i in range(nb)},
        compiler_params=pltpu.CompilerParams(has_side_effects=SIDE_EFFECT),
    )(*bufs, started["send"], started["recv"], *after)
    return list(outs)


def _gather_start(name, names, fulls, deps):
    n_t = len(names)

    def plan(refs):
        x, y, c = _my_place()
        my_idx = _dev_index(x, y, c)
        targets = [(x, y, 1 - c), (1 - x, y, c), (x, 1 - y, c), (1 - x, 1 - y, c)]
        slabs = [_slab2(KIND[names[t]], refs[t], my_idx) for t in range(n_t)]
        return [(slabs[t], slabs[t], dev) for t in range(n_t) for dev in targets]

    return _split_start(name, list(fulls), 4 * n_t, plan, deps)


def _gather_finish(name, names, started, after):
    n_t = len(names)
    fulls = _split_wait(name + "_wait", started, after)
    slab_shapes = [SDS(_slab_shape(KIND[n], f.shape), f.dtype) for n, f in zip(names, fulls)]

    def body(*refs):
        ins = refs[:n_t]
        outs = refs[n_t:2 * n_t]
        stage = refs[2 * n_t:3 * n_t]
        load_sems, send_sems, recv_sems = refs[3 * n_t:]
        x, y, c = _my_place()
        chips = [(1 - x, y), (x, 1 - y), (1 - x, 1 - y)]
        pairs = [(t, j) for t in range(n_t) for j in range(3)]
        loads = [pltpu.make_async_copy(_slab2(KIND[names[t]], ins[t], _dev_index(*chips[j], c)), stage[t].at[j], load_sems.at[t, j])
                 for t, j in pairs]
        for cp in loads:
            cp.start()

        def copy(t, j, core):
            return pltpu.make_async_remote_copy(
                src_ref=stage[t].at[j], dst_ref=_slab2(KIND[names[t]], outs[t], _dev_index(*chips[j], core)),
                send_sem=send_sems.at[t, j], recv_sem=recv_sems.at[t, j], device_id=(x, y, 1 - c), device_id_type=MESH)

        sends = [copy(t, j, c) for t, j in pairs]
        for ld, cp in zip(loads, sends):
            ld.wait()
            cp.start()
        for t, j in pairs:
            copy(t, j, 1 - c).wait_recv()
        for cp in sends:
            cp.wait_send()

    return pl.pallas_call(
        body, in_specs=[ANY_SPEC] * n_t, out_specs=[ANY_SPEC] * n_t, out_shape=[SDS(b.shape, b.dtype) for b in fulls],
        input_output_aliases={t: t for t in range(n_t)},
        scratch_shapes=[pltpu.VMEM((3,) + s.shape, s.dtype) for s in slab_shapes]
        + [pltpu.SemaphoreType.DMA((n_t, 3)), pltpu.SemaphoreType.DMA((n_t, 3)), pltpu.SemaphoreType.DMA((n_t, 3))],
        name=name + "_pass", compiler_params=pltpu.CompilerParams(vmem_limit_bytes=VMEM_LIMIT),
    )(*fulls)


def _exchange_start(name, names, grads, deps):
    n_t = len(names)
    lands = [lax.empty((N_DEV,) + _slab_shape(KIND[n], g.shape), g.dtype) for n, g in zip(names, grads)]

    def plan(refs):
        my_idx = _dev_index(*_my_place())
        return [(_slab2(KIND[names[t]], refs[t], _dev_index(*peer)), refs[n_t + t].at[my_idx], peer)
                for t in range(n_t) for peer in _peers_all()]

    return _split_start(name, list(grads) + lands, 7 * n_t, plan, deps)


def _small_exchange_start(part, deps):
    land = lax.empty((N_DEV,) + part.shape, part.dtype)

    def plan(refs):
        my_idx = _dev_index(*_my_place())
        return [(refs[0], refs[1].at[my_idx], peer) for peer in _peers_all()]

    return _split_start("small_exchange", [part, land], N_DEV - 1, plan, deps)


def _slab_pieces():
    sh = D_IN // N_DEV
    out = []
    for j in range(N_DEV):
        for first, end, dst in IN_SEGMENTS:
            lo, hi = max(first, sh * j), min(end, sh * (j + 1))
            if lo < hi:
                out.append((j, lo - sh * j, hi - sh * j, dst + lo - first))
    return out


def _w_in_assemble(stacked):
    tr = 256
    sh = D_IN // N_DEV

    def body(i_ref, o_ref):
        o_ref[:, COL_DT:COL_XBC] = jnp.zeros((tr, COL_XBC - COL_DT), bf16)
        for j, lo, hi, dst in _slab_pieces():
            o_ref[:, dst:dst + hi - lo] = i_ref[j, :, lo:hi]

    return pl.pallas_call(
        body, grid=(D // tr,), in_specs=[pl.BlockSpec((N_DEV, tr, sh), lambda i: (0, i, 0))],
        out_specs=pl.BlockSpec((None, tr, D_IN_PAD), lambda i: (0, i, 0)), out_shape=_out_hbm(SDS((1, D, D_IN_PAD), bf16)),
        name="w_in_assemble", compiler_params=_cparams(1),
    )(*_in_hbm([stacked]))


def _w_in_slabs(dw_in):
    tr = 256
    sh = D_IN // N_DEV

    def body(i_ref, o_ref):
        for j, lo, hi, src in _slab_pieces():
            o_ref[j, :, lo:hi] = i_ref[:, src:src + hi - lo]

    return pl.pallas_call(
        body, grid=(D // tr,), in_specs=[pl.BlockSpec((tr, D_IN_PAD), lambda i: (i, 0))],
        out_specs=pl.BlockSpec((N_DEV, tr, sh), lambda i: (0, i, 0)), out_shape=_out_hbm(SDS((N_DEV, D, sh), bf16)),
        name="w_in_slabs", compiler_params=_cparams(1),
    )(*_in_hbm([dw_in]))


SMALL_NAMES = ("mix_norm_g", "mlp_norm_g", "conv_b", "ssm_norm_g", "q_gain", "k_gain", "sinks", "dt_bias", "a_log", "d_skip",
               "rel_bias", "conv_w")
MISC_LANES = dict(q_gain=(LANE_QG, HD), k_gain=(LANE_KG, HD), sinks=(LANE_SINK, NQ), dt_bias=(LANE_DTB, NSSM),
                  a_log=(LANE_ALOG, NSSM), d_skip=(LANE_DSKIP, NSSM))


def _pack_small_grads(smalls, drel_t, loss):
    def body(*refs):
        o_ref = refs[-1]
        drel_ref, loss_ref = refs[-3], refs[-2]
        o_ref[...] = jnp.zeros_like(o_ref)
        for l in range(DEPTH):
            mixg, mlpg, convb, convw, ssd, attn = refs[6 * l:6 * l + 6]
            o_ref[ROW_MIXG + l:ROW_MIXG + l + 1, :] = mixg[...]
            o_ref[ROW_MLPG + l:ROW_MLPG + l + 1, :] = mlpg[...]
            o_ref[ROW_CONVB + l:ROW_CONVB + l + 1, :] = convb[0:1, :]
            o_ref[ROW_SSMG + l:ROW_SSMG + l + 1, 0:D_SSM] = ssd[0:1, :]
            o_ref[ROW_CONVW + 4 * l:ROW_CONVW + 4 * l + 4, :] = convw[0:4, :]
            row = slice(ROW_MISC + l, ROW_MISC + l + 1)
            o_ref[row, LANE_QG:LANE_QG + HD] = attn[0:1, 0:HD]
            o_ref[row, LANE_KG:LANE_KG + HD] = attn[1:2, 0:HD]
            o_ref[row, LANE_SINK:LANE_SINK + NQ] = attn[2:3, 0:NQ]
            o_ref[row, LANE_DTB:LANE_DTB + NSSM] = ssd[1:2, 0:NSSM]
            o_ref[row, LANE_ALOG:LANE_ALOG + NSSM] = ssd[2:3, 0:NSSM]
            o_ref[row, LANE_DSKIP:LANE_DSKIP + NSSM] = ssd[3:4, 0:NSSM]
        o_ref[ROW_RELB:ROW_RELB + NQ, 0:N_BUCKETS] = drel_ref[...]
        o_ref[ROW_LOSS:ROW_LOSS + 1, 0:1] = loss_ref[0:1, 0:1]

    args = []
    for sm in smalls:
        args += [sm["mix_norm_g"], sm["mlp_norm_g"], sm["conv_b"], sm["conv_w"], sm["ssd"], sm["attn"]]
    args += [drel_t, loss]
    return pl.pallas_call(body, out_shape=SDS((SMALL_ROWS, D), f32), name="pack_small_grads")(*args)


def _adamw_small(part, land, w, m, v):
    n = len(SMALL_NAMES)

    def grad_of(name, g_ref):
        if name == "mix_norm_g":
            return g_ref[ROW_MIXG:ROW_MIXG + DEPTH, :]
        if name == "mlp_norm_g":
            return g_ref[ROW_MLPG:ROW_MLPG + DEPTH, :]
        if name == "conv_b":
            return g_ref[ROW_CONVB:ROW_CONVB + DEPTH, :]
        if name == "ssm_norm_g":
            return g_ref[ROW_SSMG:ROW_SSMG + DEPTH, 0:D_SSM]
        if name == "rel_bias":
            return g_ref[ROW_RELB:ROW_RELB + NQ, 0:N_BUCKETS].T
        lane, width = MISC_LANES[name]
        return g_ref[ROW_MISC:ROW_MISC + DEPTH, lane:lane + width]

    def body(part_ref, land_ref, *refs):
        ws, ms, vs = refs[:n], refs[n:2 * n], refs[2 * n:3 * n]
        loss_ref = refs[3 * n]
        outs = refs[3 * n + 1:-1]
        g_ref = refs[-1]
        me = _dev_index(*_my_place())
        for p in range(N_DEV):
            term = jnp.where(me == p, part_ref[...], land_ref[p])
            if p == 0:
                g_ref[...] = term
            else:
                g_ref[...] += term
        loss_ref[...] = g_ref[ROW_LOSS:ROW_LOSS + 1, 0:128]
        my_cols = pl.ds(pl.multiple_of(me * 128, 128), 128)
        for k, name in enumerate(SMALL_NAMES):
            g_out, d_out, m_out, v_out = outs[4 * k:4 * k + 4]
            if name == "conv_w":
                for l in range(DEPTH):
                    g = g_ref[ROW_CONVW + 4 * l:ROW_CONVW + 4 * l + 4, my_cols]
                    delta, m_new, v_new = _adamw_math(ws[k][l], ms[k][l], vs[k][l], g)
                    g_out[l], d_out[l], m_out[l], v_out[l] = g, delta, m_new, v_new
            else:
                g = grad_of(name, g_ref)
                delta, m_new, v_new = _adamw_math(ws[k][...], ms[k][...], vs[k][...], g)
                g_out[...], d_out[...], m_out[...], v_out[...] = g, delta, m_new, v_new

    ws = [w[name] for name in SMALL_NAMES]
    out_shape = [SDS((1, 128), f32)]
    for a in ws:
        out_shape += [SDS(a.shape, f32)] * 4
    return pl.pallas_call(body, out_shape=out_shape, name="adamw_small", scratch_shapes=[pltpu.VMEM((SMALL_ROWS, D), f32)])(
        part, land, *ws, *[m[name] for name in SMALL_NAMES], *[v[name] for name in SMALL_NAMES])


def _plain(tm, tn):
    return pl.BlockSpec((tm, tn), lambda i, j, k: (i, j))


def _rowblk(tm, width):
    return pl.BlockSpec((tm, width), lambda i, j, k: (i, 0))


def _store_epi(dtype):
    def epi(acc, i, j, ex, outs):
        outs[0][...] = acc.astype(dtype)
    return epi


def _rms_prologue(layer):
    def pro(a_ref, ex, outs):
        xv = a_ref[...]
        r = lax.rsqrt(jnp.mean(xv * xv, axis=-1, keepdims=True) + EPS)
        h = (xv * r * ex[0][layer:layer + 1, :]).astype(bf16)
        outs[-1][...] = h
        return h
    return pro


MLP_TM = 256
MLP_VMEM = 56 * 1024 * 1024


def _resident(shape):
    return pl.BlockSpec((None,) + shape, lambda i: (0, 0, 0), pipeline_mode=pl.Buffered(1))


def _mlp_fwd(layer, x, mix, g, w_out, w_up, w_down, tgt=None):
    tm = MLP_TM
    with_loss = tgt is not None

    def body(x_ref, mix_ref, g_ref, wo_ref, wu_ref, wd_ref, *rest):
        xm_ref, a_ref, r_ref, h_ref = rest[with_loss:with_loss + 4]
        rest = rest[:with_loss] + rest[with_loss + 1:]
        i = pl.program_id(0)
        xv = x_ref[...] + _dot(mix_ref[...], wo_ref[...], NN_DIMS)
        xm_ref[...] = xv
        h = (xv * lax.rsqrt(jnp.mean(xv * xv, axis=-1, keepdims=True) + EPS) * g_ref[layer:layer + 1, :]).astype(bf16)
        h_ref[...] = h
        r = jnp.maximum(_dot(h, wu_ref[...], NN_DIMS), 0.0)
        a = (r * r).astype(bf16)
        a_ref[...] = a
        r_ref[...] = r.astype(bf16)
        y = xv + _dot(a, wd_ref[...], NN_DIMS)
        if not with_loss:
            rest[3][...] = y
            return
        err = y - rest[0][...]
        rest[4][...] = err * (1.0 / D)
        part = 0.5 * jnp.sum(jnp.mean(err * err, axis=-1, keepdims=True), axis=0, keepdims=True)

        @pl.when(i == 0)
        def _():
            rest[5][...] = jnp.zeros_like(rest[5])

        rest[5][...] += jnp.broadcast_to(part, rest[5].shape)

    row = lambda width: pl.BlockSpec((tm, width), lambda i: (i, 0))
    in_specs = [row(D), row(D), pl.BlockSpec((DEPTH, D), lambda i: (0, 0)), _resident((D, D)), _resident((D, D_FF)),
                _resident((D_FF, D))]
    out_specs = [row(D), row(D_FF), row(D_FF), row(D), row(D)]
    out_shape = [SDS((S, D), f32), SDS((S, D_FF), bf16), SDS((S, D_FF), bf16), SDS((S, D), bf16), SDS((S, D), f32)]
    args = [x, mix, g, w_out, w_up, w_down]
    if with_loss:
        in_specs.append(row(D))
        args.append(tgt)
        out_specs.append(pl.BlockSpec((1, 128), lambda i: (0, 0)))
        out_shape.append(SDS((1, 128), f32))
    return pl.pallas_call(
        body, grid=(S // tm,), in_specs=in_specs, out_specs=out_specs, out_shape=_out_hbm(out_shape),
        name="mlp_fwd_loss" if with_loss else "mlp_fwd",
        compiler_params=pltpu.CompilerParams(dimension_semantics=("arbitrary",), vmem_limit_bytes=MLP_VMEM),
    )(*_in_hbm(args[:3]), *args[3:6], *_in_hbm(args[6:]))


def _mlp_bwd_act(layer, dx_out, r_act, x_mid, g, w_down, w_up, w_out, deps):
    tm = MLP_TM

    def body(dxo_ref, r_ref, xm_ref, g_ref, wd_ref, wu_ref, wo_ref, *rest):
        du_ref, dx_ref, dg_ref, dmix_ref = rest[len(deps):]
        dxo = dxo_ref[...]
        du = (_dot(dxo.astype(bf16), wd_ref[...], NT_DIMS) * (2.0 * r_ref[...].astype(f32))).astype(bf16)
        du_ref[...] = du
        dh = _dot(du, wu_ref[...], NT_DIMS)
        _rms_bwd_epilogue(layer)(dh, pl.program_id(0), 0, (xm_ref, g_ref, dxo_ref), (dx_ref, dg_ref))
        dmix_ref[...] = _dot(dx_ref[...].astype(bf16), wo_ref[...], NT_DIMS)

    row = lambda width: pl.BlockSpec((tm, width), lambda i: (i, 0))
    return pl.pallas_call(
        body, grid=(S // tm,),
        in_specs=[row(D), row(D_FF), row(D), pl.BlockSpec((DEPTH, D), lambda i: (0, 0)), _resident((D_FF, D)), _resident((D, D_FF)),
                  _resident((D, D))] + [ANY_SPEC] * len(deps),
        out_specs=[row(D_FF), row(D), pl.BlockSpec((1, D), lambda i: (0, 0)), row(D)],
        out_shape=_out_hbm([SDS((S, D_FF), bf16), SDS((S, D), f32), SDS((1, D), f32), SDS((S, D), f32)]), name="mlp_bwd_act",
        compiler_params=pltpu.CompilerParams(dimension_semantics=("arbitrary",), vmem_limit_bytes=MLP_VMEM),
    )(*_in_hbm([dx_out, r_act, x_mid, g]), w_down, w_up, w_out, *_in_hbm(deps))


def _layer_fwd(l, x, p, get_weights, bias, tgt=None):
    wts = get_weights(l, "in", [x, bias])
    gfull = pl.BlockSpec((DEPTH, D), lambda i, j, k: (0, 0))
    tm = 512

    def inproj_epi(acc, i, j, ex, outs):
        outs[0][...] = acc[:, COL_QKV:COL_Z].astype(bf16)
        outs[1][...] = acc[:, COL_Z:COL_DT].astype(bf16)
        outs[2][...] = acc[:, COL_XBC:D_IN_PAD].astype(bf16)
        outs[3][...] = acc[:, COL_DT:COL_DT + 128]

    qkv, z, xbc, dt, h1 = _matmul(
        "in_proj", "nn", x, wts["w_in"], tm=tm, tn=D_IN_PAD, tk=D, prologue=_rms_prologue(l),
        extras=(p["mix_norm_g"],), extra_specs=(gfull,),
        out_shape=[SDS((S, 768), bf16), SDS((S, 512), bf16), SDS((S, 1024), bf16), SDS((S, 128), f32), SDS((S, D), bf16)],
        out_specs=[_rowblk(tm, 768), _rowblk(tm, 512), _rowblk(tm, 1024), _rowblk(tm, 128), _rowblk(tm, D)], epilogue=inproj_epi)
    attn = _attn_fwd(qkv, p["q_gain"], p["k_gain"], p["sinks"], bias, l)
    xact = _conv_fwd(xbc, wts["conv_w"], p["conv_b"], l)
    mix, hs, y_ssd = _ssd_fwd(xact, z, dt, attn, p["dt_bias"], p["a_log"], p["d_skip"], p["ssm_norm_g"], l)
    wts = dict(wts, **get_weights(l, "rest", [mix]))

    x_mid, a_act, r_act, h2, *result = _mlp_fwd(l, x, mix, p["mlp_norm_g"], wts["w_out"], wts["w_up"], wts["w_down"], tgt)
    saved = dict(x=x, h1=h1, qkv=qkv, z=z, xbc=xbc, dt=dt, xact=xact, mix=mix, hs=hs, y_ssd=y_ssd, x_mid=x_mid, h2=h2,
                 a=a_act, r=r_act, wts=wts)
    return (result[0] if tgt is None else tuple(result)), saved


def _layer_bwd(l, dx_out, sv, p, bias, deps, send):
    wts = sv["wts"]

    dw_down = _matmul("dw_down", "tn", sv["a"], dx_out, tm=1024, tn=D, tk=S, out_shape=SDS((D_FF, D), bf16),
                      out_specs=_plain(1024, D), epilogue=_store_epi(bf16), deps=deps)
    deps = send(l, dict(w_down=dw_down))
    du, dx_mid, dg_mlp, dmix = _mlp_bwd_act(l, dx_out, sv["r"], sv["x_mid"], p["mlp_norm_g"], wts["w_down"], wts["w_up"],
                                            wts["w_out"], deps)
    dw_up = _matmul("dw_up", "tn", sv["h2"], du, tm=D, tn=1024, tk=S, out_shape=SDS((D, D_FF), bf16),
                    out_specs=_plain(D, 1024), epilogue=_store_epi(bf16))
    dw_out = _matmul("dw_out", "tn", sv["mix"], dx_mid, tm=D, tn=512, tk=S, out_shape=SDS((D, D), bf16),
                     out_specs=_plain(D, 512), epilogue=_store_epi(bf16))
    deps = send(l, dict(w_up=dw_up, w_out=dw_out))
    gfull = pl.BlockSpec((DEPTH, D), lambda i, j, k: (0, 0))
    grow = pl.BlockSpec((1, D), lambda i, j, k: (0, 0))
    dproj, dbias, dsm_attn = _attn_bwd(sv["qkv"], dmix, p["q_gain"], p["k_gain"], p["sinks"], bias, l, deps)
    dproj, dxact, dsm_ssd = _ssd_bwd(sv["xact"], sv["z"], sv["dt"], dmix, sv["hs"], sv["y_ssd"], p["dt_bias"], p["a_log"],
                                     p["d_skip"], p["ssm_norm_g"], dproj, l)
    dproj, dconv_w, dconv_b = _conv_bwd(sv["xbc"], dxact, wts["conv_w"], p["conv_b"], dproj, l)
    dw_in = _matmul("dw_in", "tn", sv["h1"], dproj, tm=D, tn=640, tk=S, out_shape=SDS((D, D_IN_PAD), bf16),
                    out_specs=_plain(D, 640), epilogue=_store_epi(bf16))
    deps = send(l, dict(w_in=_w_in_slabs(dw_in)))
    dx, dg_mix = _matmul(
        "in_proj_dh", "nt", dproj, wts["w_in"], tm=512, tn=D, tk=D_IN_PAD, out_shape=[SDS((S, D), f32), SDS((1, D), f32)],
        out_specs=[_plain(512, D), grow], epilogue=_rms_bwd_epilogue(l),
        extras=(sv["x"], p["mix_norm_g"], dx_mid), extra_specs=(_plain(512, D), gfull, _plain(512, D)), deps=deps)
    small = dict(mix_norm_g=dg_mix, mlp_norm_g=dg_mlp, conv_w=dconv_w, conv_b=dconv_b, ssd=dsm_ssd, attn=dsm_attn, dbias=dbias)
    return dx, small, deps


def _local_step(x, tgt, p, get_weights, send):
    onehot_t = jnp.asarray(_onehot_buckets(), dtype=bf16)
    bias = _bias_build(p["rel_bias"].T, onehot_t).reshape(NQ, BLK, 2 * BLK)
    saved = []
    h = x
    for l in range(DEPTH):
        h, sv = _layer_fwd(l, h, p, get_weights, bias, tgt if l == DEPTH - 1 else None)
        saved.append(sv)
    dx, loss = h
    smalls = [None] * DEPTH
    deps = ()
    for l in reversed(range(DEPTH)):
        dx, smalls[l], deps = _layer_bwd(l, dx, saved[l], p, bias, deps, send)
    drel_t = _bias_grad(smalls[0]["dbias"].reshape(NQ, -1), smalls[1]["dbias"].reshape(NQ, -1), onehot_t)
    return dx, _pack_small_grads(smalls, drel_t, loss)


WEIGHT_ORDER = ("mix_norm_g", "w_in", "q_gain", "k_gain", "sinks", "rel_bias", "conv_w", "conv_b", "dt_bias", "a_log", "d_skip",
                "ssm_norm_g", "w_out", "mlp_norm_g", "w_up", "w_down")


def kernel(x, mix_norm_g, w_in, q_gain, k_gain, sinks, rel_bias, conv_w, conv_b, dt_bias, a_log, d_skip, ssm_norm_g, w_out, mlp_norm_g, w_up, w_down, loss_target, m_mix_norm_g, m_w_in, m_q_gain, m_k_gain, m_sinks, m_rel_bias, m_conv_w, m_conv_b, m_dt_bias, m_a_log, m_d_skip, m_ssm_norm_g, m_w_out, m_mlp_norm_g, m_w_up, m_w_down, v_mix_norm_g, v_w_in, v_q_gain, v_k_gain, v_sinks, v_rel_bias, v_conv_w, v_conv_b, v_dt_bias, v_a_log, v_d_skip, v_ssm_norm_g, v_w_out, v_mlp_norm_g, v_w_up, v_w_down):
    w = dict(mix_norm_g=mix_norm_g, w_in=w_in, q_gain=q_gain, k_gain=k_gain, sinks=sinks, rel_bias=rel_bias, conv_w=conv_w,
             conv_b=conv_b, dt_bias=dt_bias, a_log=a_log, d_skip=d_skip, ssm_norm_g=ssm_norm_g, w_out=w_out,
             mlp_norm_g=mlp_norm_g, w_up=w_up, w_down=w_down)
    m = dict(mix_norm_g=m_mix_norm_g, w_in=m_w_in, q_gain=m_q_gain, k_gain=m_k_gain, sinks=m_sinks, rel_bias=m_rel_bias,
             conv_w=m_conv_w, conv_b=m_conv_b, dt_bias=m_dt_bias, a_log=m_a_log, d_skip=m_d_skip, ssm_norm_g=m_ssm_norm_g,
             w_out=m_w_out, mlp_norm_g=m_mlp_norm_g, w_up=m_w_up, w_down=m_w_down)
    v = dict(mix_norm_g=v_mix_norm_g, w_in=v_w_in, q_gain=v_q_gain, k_gain=v_k_gain, sinks=v_sinks, rel_bias=v_rel_bias,
             conv_w=v_conv_w, conv_b=v_conv_b, dt_bias=v_dt_bias, a_log=v_a_log, d_skip=v_d_skip, ssm_norm_g=v_ssm_norm_g,
             w_out=v_w_out, mlp_norm_g=v_mlp_norm_g, w_up=v_w_up, w_down=v_w_down)
    big = ("w_in", "w_out", "w_up", "w_down")

    my_idx = _dev_index(*_my_place()).astype(jnp.int32).reshape(1)

    fulls = {n: _cast_to_full("cast_" + n, w[n], KIND[n], FULL_SHAPE[n], my_idx, bf16) for n in big}
    conv_full = _cast_to_full("cast_conv_w", conv_w.reshape(1, DEPTH * 4, 128), "stack", (N_DEV, DEPTH * 4, 128), my_idx, f32)[0]
    rest = ["w_out", "w_up", "w_down"]
    g0 = _gather_start("gather0", ["w_in", "conv_w"], [fulls["w_in"][0], conv_full], ())
    g1 = _gather_start("gather1", rest, [fulls[n][0] for n in rest], (g0["token"],))
    g2 = _gather_start("gather2", ["w_in"], [fulls["w_in"][1]], (g1["token"],))
    g3 = _gather_start("gather3", rest, [fulls[n][1] for n in rest], (g2["token"],))
    held = {}
    flat = lambda a: a.reshape(a.shape[0] * a.shape[1], a.shape[2])
    adam_in = {n: (flat(w[n]), flat(m[n]), flat(v[n])) for n in big}

    def get_weights(l, part, after):
        if l == 0 and part == "in":
            full_in, full_conv = _gather_finish("gather0", ["w_in", "conv_w"], g0,
                                                list(after) + [g3["token"], adam_in["w_in"][1], adam_in["w_in"][2]])
            held["conv_w"] = jnp.transpose(full_conv.reshape(N_DEV, DEPTH, 4, 128), (1, 2, 0, 3)).reshape(DEPTH, 4, D_CONV)
            return dict(w_in=_w_in_assemble(full_in), conv_w=held["conv_w"])
        if part == "in":
            return dict(w_in=_w_in_assemble(_gather_finish("gather2", ["w_in"], g2, after)[0]), conv_w=held["conv_w"])
        full = _gather_finish("gather1" if l == 0 else "gather3", rest, g1 if l == 0 else g3, after)
        return {n: f[None] for n, f in zip(rest, full)}

    pending = []

    def send(l, grads):
        names = list(grads)
        started = _exchange_start("exchange%d_%s" % (l, names[0]), names, [grads[n] for n in names], ())
        pending.append((l, names, started))
        return (started["token"],)

    dx, small_part = _local_step(x.reshape(S, D), loss_target.reshape(S, D), w, get_weights, send)

    small = _small_exchange_start(small_part, ())
    tiles = dict(w_in=256, w_out=128, w_up=256, w_down=256)
    outs_of = {n: None for n in big}
    after = [dx, small["token"]]
    for l, names, started in pending:
        bufs = _split_wait("exchange%d_%s_wait" % (l, names[0]), started, after)
        for t, n in enumerate(names):
            outs_of[n] = _adamw_layer("adamw_%s%d" % (n, l), KIND[n], l, *adam_in[n],
                                      bufs[len(names) + t], bufs[t], my_idx, outs_of[n], tiles[n])
        after = [outs_of[names[-1]][0]]
    res = {n: [o.reshape(w[n].shape) for o in outs_of[n]] for n in big}
    small_part, small_land = _split_wait("small_exchange_wait", small, after)
    small_outs = _adamw_small(small_part, small_land, w, m, v)
    loss = small_outs[0][0, 0]
    for k, name in enumerate(SMALL_NAMES):
        res[name] = small_outs[1 + 4 * k:5 + 4 * k]

    result = [loss, dx.reshape(1, S, D)]
    for k in range(4):
        result += [res[name][k] for name in WEIGHT_ORDER]
    return tuple(result)
```

```python
import functools
import math

import numpy as np
import jax
import jax.numpy as jnp
from jax import lax
from jax.experimental import pallas as pl
from jax.experimental.pallas import tpu as pltpu

f32 = jnp.float32
bf16 = jnp.bfloat16
SDS = jax.ShapeDtypeStruct
MESH = pl.DeviceIdType.MESH
HIGHEST = lax.Precision.HIGHEST

S = 2048
D = 1024
DEPTH = 2
BLK = 128
NBLK = S // BLK
HD = 64
NQ = 8
NKV = 2
NSSM = 8
NGRP = 2
NSTATE = 128
D_ATTN = 512
D_SSM = 512
D_CONV = 1024
D_FF = 4096
D_IN = 2312
D_IN_PAD = 2560
COL_QKV, COL_Z, COL_DT, COL_XBC = 0, 768, 1280, 1536
IN_SEGMENTS = ((0, 1280, 0), (1280, 2304, COL_XBC), (2304, 2312, COL_DT))
N_BUCKETS = 32
EPS = 1e-6
N_DEV = 8
VMEM_LIMIT = 48 * 1024 * 1024

ADAM_LR = 0.001
ADAM_B1 = 0.9
ADAM_B2 = 0.999
ADAM_EPS = 1e-08
ADAM_WD = 0.01
ADAM_STEP = 10

NT_DIMS = (((1,), (1,)), ((), ()))
TN_DIMS = (((0,), (0,)), ((), ()))
NN_DIMS = (((1,), (0,)), ((), ()))

ROW_MIXG = 0
ROW_MLPG = 2
ROW_CONVB = 4
ROW_SSMG = 6
ROW_MISC = 8
ROW_RELB = 10
ROW_CONVW = 18
ROW_LOSS = 26
SMALL_ROWS = 32
LANE_QG, LANE_KG, LANE_SINK, LANE_DTB, LANE_ALOG, LANE_DSKIP = 0, 64, 128, 256, 384, 512


def _dot(a, b, dims):
    return lax.dot_general(a, b, dims, preferred_element_type=f32)


def _cparams(n_axes):
    return pltpu.CompilerParams(dimension_semantics=("arbitrary",) * n_axes, vmem_limit_bytes=VMEM_LIMIT)


def _sum11(v):
    return jnp.sum(jnp.sum(v, axis=1, keepdims=True), axis=0, keepdims=True)


def _sigmoid(v):
    return 1.0 / (1.0 + jnp.exp(-v))


ANY_SPEC = pl.BlockSpec(memory_space=pl.ANY)


def _in_hbm(args):
    return [pltpu.with_memory_space_constraint(a, pltpu.HBM) if a.size >= 65536 else a for a in args]


def _out_hbm(out_shape):
    one = lambda s: pltpu.HBM(s.shape, s.dtype) if math.prod(s.shape) >= 65536 else s
    return [one(s) for s in out_shape] if isinstance(out_shape, (list, tuple)) else one(out_shape)


def _matmul(name, mode, a, b, *, layer=0, tm, tn, tk, out_shape, out_specs, epilogue, extras=(), extra_specs=(), deps=(),
            prologue=None):
    extras = tuple(extras) + tuple(deps)
    extra_specs = tuple(extra_specs) + (ANY_SPEC,) * len(deps)
    if mode == "tn":
        t_dim, m_dim = a.shape
        n_dim = b.shape[1]
        grid = (m_dim // tm, n_dim // tn, t_dim // tk)
        a_spec = pl.BlockSpec((tk, tm), lambda i, j, k: (k, i))
        b_spec = pl.BlockSpec((tk, tn), lambda i, j, k: (k, j))
        dims = TN_DIMS
    elif mode == "nn":
        m_dim, k_dim = a.shape
        n_dim = b.shape[-1]
        grid = (m_dim // tm, n_dim // tn, k_dim // tk)
        a_spec = pl.BlockSpec((tm, tk), lambda i, j, k: (i, k))
        b_spec = pl.BlockSpec((None, tk, tn), lambda i, j, k: (layer, k, j))
        dims = NN_DIMS
    else:
        m_dim, k_dim = a.shape
        n_dim = b.shape[-2]
        grid = (m_dim // tm, n_dim // tn, k_dim // tk)
        a_spec = pl.BlockSpec((tm, tk), lambda i, j, k: (i, k))
        b_spec = pl.BlockSpec((None, tn, tk), lambda i, j, k: (layer, j, k))
        dims = NT_DIMS
    nk = grid[2]
    n_ex = len(extras)

    def body(a_ref, b_ref, *rest):
        ex = rest[:n_ex - len(deps)]
        outs = rest[n_ex:-1]
        acc = rest[-1]
        i = pl.program_id(0)
        j = pl.program_id(1)
        k = pl.program_id(2)
        lhs = a_ref[...].astype(bf16) if prologue is None else prologue(a_ref, ex, outs)
        part = _dot(lhs, b_ref[...].astype(bf16), dims)
        if nk == 1:
            epilogue(part, i, j, ex, outs)
        else:
            @pl.when(k == 0)
            def _():
                acc[...] = part

            @pl.when(k > 0)
            def _():
                acc[...] += part

            @pl.when(k == nk - 1)
            def _():
                epilogue(acc[...], i, j, ex, outs)

    return pl.pallas_call(
        body, grid=grid, in_specs=[a_spec, b_spec, *extra_specs], out_specs=out_specs, out_shape=_out_hbm(out_shape),
        scratch_shapes=[pltpu.VMEM((tm, tn) if nk > 1 else (8, 128), f32)], name=name, compiler_params=_cparams(3),
    )(*_in_hbm([a]), b, *_in_hbm(extras))


def _rms_bwd_epilogue(layer):
    def epi(acc, i, j, ex, outs):
        x_ref, g_ref, dres_ref = ex
        dx_ref, dg_ref = outs
        xv = x_ref[...]
        r = lax.rsqrt(jnp.mean(xv * xv, axis=-1, keepdims=True) + EPS)
        xhat = xv * r
        w = acc * g_ref[layer:layer + 1, :]
        dx_ref[...] = dres_ref[...] + r * (w - xhat * jnp.mean(xhat * w, axis=-1, keepdims=True))
        dg = jnp.sum(acc * xhat, axis=0, keepdims=True)

        @pl.when(i == 0)
        def _():
            dg_ref[...] = dg

        @pl.when(i > 0)
        def _():
            dg_ref[...] += dg
    return epi


def _own_slab_spec(kind, tr, cols, nblk):
    if kind == "stack":
        return pl.BlockSpec((None, tr, cols), lambda i, idx: (idx[0], i, 0))
    if kind == "cols512":
        return pl.BlockSpec((tr, cols), lambda i, idx: (i, idx[0]))
    return pl.BlockSpec((tr, cols), lambda i, idx: (idx[0] * nblk + i, 0))


def _cast_to_full(name, w, kind, full_shape, my_idx, dtype):
    n_layers, rows, cols = w.shape
    tr = min(rows, 256)
    nblk = rows // tr

    def body(idx_ref, w_ref, *o_refs):
        for l in range(n_layers):
            o_refs[l][...] = w_ref[l].astype(dtype)

    grid_spec = pltpu.PrefetchScalarGridSpec(
        num_scalar_prefetch=1, grid=(nblk,), in_specs=[pl.BlockSpec((n_layers, tr, cols), lambda i, idx: (0, i, 0))],
        out_specs=[_own_slab_spec(kind, tr, cols, nblk)] * n_layers)
    return pl.pallas_call(body, grid_spec=grid_spec, out_shape=_out_hbm([SDS(full_shape, dtype)] * n_layers), name=name,
                          compiler_params=_cparams(1))(*_in_hbm([my_idx, w]))


def _adamw_math(w, m, v, g):
    m_new = ADAM_B1 * m + (1.0 - ADAM_B1) * g
    v_new = ADAM_B2 * v + (1.0 - ADAM_B2) * (g * g)
    m_hat = m_new / (1.0 - ADAM_B1 ** ADAM_STEP)
    v_hat = v_new / (1.0 - ADAM_B2 ** ADAM_STEP)
    delta = -ADAM_LR * (m_hat / (jnp.sqrt(v_hat) + ADAM_EPS) + ADAM_WD * w)
    return delta, m_new, v_new


def _adamw_layer(name, kind, layer, w, m, v, land, g_full, my_idx, prev, tr):
    rows2, cols = w.shape
    rows = rows2 // DEPTH
    nblk = rows // tr
    own_spec = _own_slab_spec(kind, tr, cols, nblk)
    n_prev = 0 if prev is None else 4

    def body(idx_ref, w_ref, m_ref, v_ref, land_ref, own_ref, *rest):
        g_ref, d_ref, mo_ref, vo_ref = rest[n_prev:]
        me = idx_ref[0]
        g = None
        for p in range(N_DEV):
            part = jnp.where(me == p, own_ref[...], land_ref[p]).astype(f32)
            g = part if g is None else g + part
        delta, m_new, v_new = _adamw_math(w_ref[...], m_ref[...], v_ref[...], g)
        g_ref[...] = g
        d_ref[...] = delta
        mo_ref[...] = m_new
        vo_ref[...] = v_new

    blk = pl.BlockSpec((tr, cols), lambda i, idx: (layer * nblk + i, 0))
    grid_spec = pltpu.PrefetchScalarGridSpec(
        num_scalar_prefetch=1, grid=(nblk,),
        in_specs=[blk, blk, blk, pl.BlockSpec((N_DEV, tr, cols), lambda i, idx: (0, i, 0)), own_spec] + [ANY_SPEC] * n_prev,
        out_specs=[blk, blk, blk, blk])
    aliases = {} if prev is None else {6 + k: k for k in range(4)}
    return pl.pallas_call(
        body, grid_spec=grid_spec, out_shape=_out_hbm([SDS((rows2, cols), f32)] * 4), name=name, input_output_aliases=aliases,
        compiler_params=_cparams(1),
    )(*_in_hbm([my_idx, w, m, v, land, g_full, *([] if prev is None else prev)]))


def _bucket_table():
    qi = np.arange(BLK)[:, None]
    kj = np.arange(2 * BLK)[None, :]
    dist = qi + BLK - kj
    dcl = np.clip(dist, 0, None)
    max_exact = N_BUCKETS // 2
    d_f = np.maximum(dcl, 1).astype(np.float32)
    large = max_exact + (np.log(d_f / np.float32(max_exact)) / np.float32(math.log(128 / max_exact))
                         * np.float32(N_BUCKETS - max_exact)).astype(np.int32)
    large = np.minimum(large, N_BUCKETS - 1)
    bucket = np.where(dcl < max_exact, dcl, large)
    in_window = (dist >= 0) & (dist < BLK)
    return bucket.astype(np.int32), in_window


def _onehot_buckets():
    bucket, _ = _bucket_table()
    oh = (bucket.reshape(-1)[None, :] == np.arange(N_BUCKETS)[:, None]).astype(np.float32)
    return oh


def _bias_build(rel_bias_t, onehot_t):
    def body(r_ref, o_ref, out_ref):
        r = r_ref[...]
        hi = r.astype(bf16)
        r1 = r - hi.astype(f32)
        mid = r1.astype(bf16)
        lo = (r1 - mid.astype(f32)).astype(bf16)
        oh = o_ref[...]
        out_ref[...] = _dot(hi, oh, NN_DIMS) + _dot(mid, oh, NN_DIMS) + _dot(lo, oh, NN_DIMS)

    tn = 4096
    return pl.pallas_call(
        body, grid=(BLK * 2 * BLK // tn,),
        in_specs=[pl.BlockSpec((NQ, N_BUCKETS), lambda i: (0, 0)), pl.BlockSpec((N_BUCKETS, tn), lambda i: (0, i))],
        out_specs=pl.BlockSpec((NQ, tn), lambda i: (0, i)), out_shape=SDS((NQ, BLK * 2 * BLK), f32), name="bias_build",
        compiler_params=_cparams(1),
    )(rel_bias_t, onehot_t)


def _bias_grad(dbias0, dbias1, onehot_t):
    tn = 4096
    nsteps = BLK * 2 * BLK // tn

    def body(a_ref, b_ref, o_ref, out_ref):
        g = a_ref[...] + b_ref[...]
        hi = g.astype(bf16)
        lo = (g - hi.astype(f32)).astype(bf16)
        part = _dot(hi, o_ref[...], NT_DIMS) + _dot(lo, o_ref[...], NT_DIMS)

        @pl.when(pl.program_id(0) == 0)
        def _():
            out_ref[...] = part

        @pl.when(pl.program_id(0) > 0)
        def _():
            out_ref[...] += part

    return pl.pallas_call(
        body, grid=(nsteps,),
        in_specs=[pl.BlockSpec((NQ, tn), lambda i: (0, i)), pl.BlockSpec((NQ, tn), lambda i: (0, i)),
                  pl.BlockSpec((N_BUCKETS, tn), lambda i: (0, i))],
        out_specs=pl.BlockSpec((NQ, N_BUCKETS), lambda i: (0, 0)), out_shape=SDS((NQ, N_BUCKETS), f32), name="bias_grad",
        compiler_params=_cparams(1),
    )(dbias0, dbias1, onehot_t)


def _attn_mask(n):
    qi = lax.broadcasted_iota(jnp.int32, (BLK, 2 * BLK), 0)
    kj = lax.broadcasted_iota(jnp.int32, (BLK, 2 * BLK), 1)
    dist = qi + BLK - kj
    first_key = jnp.where(n > 0, 0, BLK)
    return (dist >= 0) & (dist < BLK) & (kj >= first_key)


def _row_mean(a):
    return jnp.mean(a, axis=-1, keepdims=True)


def _head_norm(t, gain):
    r = lax.rsqrt(_row_mean(t * t) + EPS)
    that = t * r
    return that, r, that * gain


def _softmax_with_sink(s, sink):
    m = jnp.maximum(jnp.max(s, axis=-1, keepdims=True), sink)
    p = jnp.exp(s - m)
    psink = jnp.exp(sink - m)
    inv = 1.0 / (jnp.sum(p, axis=-1, keepdims=True) + psink)
    return p * inv, psink * inv


GQ = NQ // NKV


def _attn_fwd(qkv, q_gain, k_gain, sinks, bias, layer):
    def body(q_ref, kc_ref, kp_ref, vc_ref, vp_ref, qg_ref, kg_ref, sk_ref, bias_ref, o_ref):
        m = pl.program_id(0)
        qg = qg_ref[layer:layer + 1, :]
        kg = kg_ref[layer:layer + 1, :]
        grp = range(NKV)
        chains = [(b, j) for b in range(2) for j in grp]
        masks = [jnp.tile(_attn_mask(2 * m + b), (GQ, 1)) for b in range(2)]
        kblk = [[kp_ref[:, pl.ds(HD * j, HD)].astype(f32), kc_ref[0:BLK, pl.ds(HD * j, HD)].astype(f32),
                 kc_ref[BLK:, pl.ds(HD * j, HD)].astype(f32)] for j in grp]
        vblk = [[vp_ref[:, pl.ds(HD * j, HD)].astype(bf16), vc_ref[0:BLK, pl.ds(HD * j, HD)].astype(bf16),
                 vc_ref[BLK:, pl.ds(HD * j, HD)].astype(bf16)] for j in grp]
        knb = [[_head_norm(kblk[j][t], kg)[2].astype(bf16) for t in range(3)] for j in grp]
        kn_b = {(b, j): jnp.concatenate([knb[j][b], knb[j][b + 1]], axis=0) for b, j in chains}
        vbs = {(b, j): jnp.concatenate([vblk[j][b], vblk[j][b + 1]], axis=0) for b, j in chains}
        rows = {}
        for b, j in chains:
            heads = [GQ * j + g for g in range(GQ)]
            rows[b, j] = (jnp.concatenate([q_ref[pl.ds(BLK * b, BLK), pl.ds(HD * h, HD)] for h in heads], axis=0).astype(f32),
                          jnp.concatenate([jnp.broadcast_to(sk_ref[layer:layer + 1, h:h + 1], (BLK, 1)) for h in heads], axis=0))
        qn_b = {c: _head_norm(rows[c][0], qg)[2].astype(bf16) for c in chains}
        ss = {(b, j): _dot(qn_b[b, j], kn_b[b, j], NT_DIMS) * (HD ** -0.5) + bias_ref[GQ * j:GQ * (j + 1)].reshape(GQ * BLK, 2 * BLK)
              for b, j in chains}
        ps = {(b, j): _softmax_with_sink(jnp.where(masks[b], ss[b, j], -jnp.inf), rows[b, j][1])[0] for b, j in chains}
        outs = {c: _dot(ps[c].astype(bf16), vbs[c], NN_DIMS).astype(bf16) for c in chains}
        for b, j in chains:
            for g in range(GQ):
                o_ref[pl.ds(BLK * b, BLK), pl.ds(HD * (GQ * j + g), HD)] = outs[b, j][BLK * g:BLK * (g + 1), :]

    prev = lambda m: jnp.maximum(2 * m - 1, 0)
    small = lambda shape: pl.BlockSpec(shape, lambda m: (0,) * len(shape))
    return pl.pallas_call(
        body, grid=(NBLK // 2,),
        in_specs=[pl.BlockSpec((2 * BLK, D_ATTN), lambda m: (m, 0)),
                  pl.BlockSpec((2 * BLK, 128), lambda m: (m, 4)), pl.BlockSpec((BLK, 128), lambda m: (prev(m), 4)),
                  pl.BlockSpec((2 * BLK, 128), lambda m: (m, 5)), pl.BlockSpec((BLK, 128), lambda m: (prev(m), 5)),
                  small((DEPTH, HD)), small((DEPTH, HD)), small((DEPTH, NQ)), small((NQ, BLK, 2 * BLK))],
        out_specs=pl.BlockSpec((2 * BLK, D_ATTN), lambda m: (m, 0)), out_shape=_out_hbm(SDS((S, D_ATTN), bf16)),
        name="attn_fwd", compiler_params=_cparams(1),
    )(*_in_hbm([qkv, qkv, qkv, qkv, qkv, q_gain, k_gain, sinks, bias]))


def _attn_bwd(qkv, dmix, q_gain, k_gain, sinks, bias, layer, deps=()):
    def body(q_ref, kc_ref, kp_ref, vc_ref, vp_ref, do_ref, qg_ref, kg_ref, sk_ref, bias_ref, *rest):
        dqkv_ref, dbias_ref, dsm_ref, carry = rest[len(deps):]
        i = pl.program_id(0)
        m = NBLK // 2 - 1 - i
        qg = qg_ref[layer:layer + 1, :]
        kg = kg_ref[layer:layer + 1, :]
        lane = lax.broadcasted_iota(jnp.int32, (1, 128), 1)

        @pl.when(i == 0)
        def _():
            carry[...] = jnp.zeros_like(carry)
            dbias_ref[...] = jnp.zeros_like(dbias_ref)
            dsm_ref[...] = jnp.zeros_like(dsm_ref)

        grp = range(NKV)
        chains = [(b, j) for b in range(2) for j in grp]
        masks = [jnp.tile(_attn_mask(2 * m + b), (GQ, 1)) for b in range(2)]
        kblk = [[kp_ref[:, pl.ds(HD * j, HD)].astype(f32), kc_ref[0:BLK, pl.ds(HD * j, HD)].astype(f32),
                 kc_ref[BLK:, pl.ds(HD * j, HD)].astype(f32)] for j in grp]
        vblk = [[vp_ref[:, pl.ds(HD * j, HD)].astype(bf16), vc_ref[0:BLK, pl.ds(HD * j, HD)].astype(bf16),
                 vc_ref[BLK:, pl.ds(HD * j, HD)].astype(bf16)] for j in grp]
        knorm = [[_head_norm(kblk[j][t], kg) for t in range(3)] for j in grp]
        kn_b = {(b, j): jnp.concatenate([knorm[j][b][2].astype(bf16), knorm[j][b + 1][2].astype(bf16)], axis=0) for b, j in chains}
        vbs = {(b, j): jnp.concatenate([vblk[j][b], vblk[j][b + 1]], axis=0) for b, j in chains}
        rows, do_b = {}, {}
        for b, j in chains:
            heads = [GQ * j + g for g in range(GQ)]
            qrows = pl.ds(BLK * b, BLK)
            rows[b, j] = (jnp.concatenate([q_ref[qrows, pl.ds(HD * h, HD)] for h in heads], axis=0).astype(f32),
                          jnp.concatenate([jnp.broadcast_to(sk_ref[layer:layer + 1, h:h + 1], (BLK, 1)) for h in heads], axis=0))
            do_b[b, j] = jnp.concatenate([do_ref[qrows, pl.ds(HD * h, HD)] for h in heads], axis=0).astype(bf16)
        qnorm = {c: _head_norm(rows[c][0], qg) for c in chains}
        qn_b = {c: qnorm[c][2].astype(bf16) for c in chains}
        ss = {(b, j): _dot(qn_b[b, j], kn_b[b, j], NT_DIMS) * (HD ** -0.5) + bias_ref[GQ * j:GQ * (j + 1)].reshape(GQ * BLK, 2 * BLK)
              for b, j in chains}
        sm = {(b, j): _softmax_with_sink(jnp.where(masks[b], ss[b, j], -jnp.inf), rows[b, j][1]) for b, j in chains}
        dps = {c: _dot(do_b[c], vbs[c], NT_DIMS) for c in chains}
        deltas = {c: jnp.sum(sm[c][0] * dps[c], axis=-1, keepdims=True) for c in chains}
        dss = {c: sm[c][0] * (dps[c] - deltas[c]) for c in chains}
        ds_b = {c: (dss[c] * (HD ** -0.5)).astype(bf16) for c in chains}
        dqn = {c: _dot(ds_b[c], kn_b[c], NN_DIMS) for c in chains}
        dkn = {c: _dot(ds_b[c], qn_b[c], TN_DIMS) for c in chains}
        dvs = {c: _dot(sm[c][0].astype(bf16), do_b[c], TN_DIMS) for c in chains}
        dqg = jnp.zeros((1, HD), f32)
        dkg = jnp.zeros((1, HD), f32)
        dsink = jnp.zeros((1, 128), f32)
        for b, j in chains:
            dbias_ref[GQ * j:GQ * (j + 1)] += dss[b, j].reshape(GQ, BLK, 2 * BLK)
            dsk = sm[b, j][1] * deltas[b, j]
            for g in range(GQ):
                dsink = dsink + jnp.where(lane == GQ * j + g, -_sum11(dsk[BLK * g:BLK * (g + 1), :]), 0.0)
            qhat, rq, _ = qnorm[b, j]
            w = dqn[b, j] * qg
            dq = rq * (w - qhat * _row_mean(qhat * w))
            for g in range(GQ):
                dqkv_ref[pl.ds(BLK * b, BLK), pl.ds(HD * (GQ * j + g), HD)] = dq[BLK * g:BLK * (g + 1), :].astype(bf16)
            dqg = dqg + jnp.sum(dqn[b, j] * qhat, axis=0, keepdims=True)
        for j in grp:
            dkn_t = [dkn[0, j][:BLK, :], dkn[0, j][BLK:, :] + dkn[1, j][:BLK, :], dkn[1, j][BLK:, :]]
            dv_t = [dvs[0, j][:BLK, :], dvs[0, j][BLK:, :] + dvs[1, j][:BLK, :], dvs[1, j][BLK:, :]]
            dk_t = []
            for t in range(3):
                khat, rk, _ = knorm[j][t]
                w = dkn_t[t] * kg
                dk_t.append(rk * (w - khat * _row_mean(khat * w)))
                dkg = dkg + jnp.sum(dkn_t[t] * khat, axis=0, keepdims=True)
            kcols, vcols = pl.ds(D_ATTN + HD * j, HD), pl.ds(D_ATTN + 128 + HD * j, HD)
            dqkv_ref[BLK:, kcols] = (dk_t[2] + carry[:, pl.ds(HD * j, HD)]).astype(bf16)
            dqkv_ref[BLK:, vcols] = (dv_t[2] + carry[:, pl.ds(128 + HD * j, HD)]).astype(bf16)
            dqkv_ref[0:BLK, kcols] = dk_t[1].astype(bf16)
            dqkv_ref[0:BLK, vcols] = dv_t[1].astype(bf16)
            carry[:, pl.ds(HD * j, HD)] = dk_t[0]
            carry[:, pl.ds(128 + HD * j, HD)] = dv_t[0]
        dsm_ref[0:1, 0:HD] += dqg
        dsm_ref[1:2, 0:HD] += dkg
        dsm_ref[2:3, :] += dsink

    rev = lambda i: NBLK // 2 - 1 - i
    prev = lambda i: jnp.maximum(NBLK - 3 - 2 * i, 0)
    small = lambda shape: pl.BlockSpec(shape, lambda i: (0,) * len(shape))
    return pl.pallas_call(
        body, grid=(NBLK // 2,),
        in_specs=[pl.BlockSpec((2 * BLK, D_ATTN), lambda i: (rev(i), 0)),
                  pl.BlockSpec((2 * BLK, 128), lambda i: (rev(i), 4)), pl.BlockSpec((BLK, 128), lambda i: (prev(i), 4)),
                  pl.BlockSpec((2 * BLK, 128), lambda i: (rev(i), 5)), pl.BlockSpec((BLK, 128), lambda i: (prev(i), 5)),
                  pl.BlockSpec((2 * BLK, D_ATTN), lambda i: (rev(i), 0)),
                  small((DEPTH, HD)), small((DEPTH, HD)), small((DEPTH, NQ)), small((NQ, BLK, 2 * BLK))] + [ANY_SPEC] * len(deps),
        out_specs=[pl.BlockSpec((2 * BLK, 768), lambda i: (rev(i), COL_QKV // 768)), small((NQ, BLK, 2 * BLK)), small((8, 128))],
        out_shape=_out_hbm([SDS((S, D_IN_PAD), bf16), SDS((NQ, BLK, 2 * BLK), f32), SDS((8, 128), f32)]),
        scratch_shapes=[pltpu.VMEM((BLK, 256), f32)], name="attn_bwd", compiler_params=_cparams(1),
    )(*_in_hbm([qkv, qkv, qkv, qkv, qkv, dmix, q_gain, k_gain, sinks, bias, *deps]))


CONV_TC = 256


def _shift_down(u, s):
    if s == 0:
        return u
    rows = lax.broadcasted_iota(jnp.int32, u.shape, 0)
    return jnp.where(rows >= s, pltpu.roll(u, s, 0), 0.0)


def _shift_up(u, s):
    if s == 0:
        return u
    rows = lax.broadcasted_iota(jnp.int32, u.shape, 0)
    return jnp.where(rows < u.shape[0] - s, pltpu.roll(u, u.shape[0] - s, 0), 0.0)


def _conv_specs():
    return [pl.BlockSpec((S, CONV_TC), lambda c: (0, c)),
            pl.BlockSpec((None, 4, CONV_TC), lambda c: (0, 0, c)),
            pl.BlockSpec((DEPTH, CONV_TC), lambda c: (0, c))]


def _conv_pre(u, w_ref, b_ref, layer):
    pre = b_ref[layer:layer + 1, :] + w_ref[3:4, :] * u
    for k in range(3):
        pre = pre + w_ref[k:k + 1, :] * _shift_down(u, 3 - k)
    return pre


def _conv_fwd(xbc, conv_w, conv_b, layer):
    def body(u_ref, w_ref, b_ref, o_ref):
        pre = _conv_pre(u_ref[...].astype(f32), w_ref, b_ref, layer)
        o_ref[...] = pre * _sigmoid(pre)

    specs = _conv_specs()
    specs[1] = pl.BlockSpec((None, 4, CONV_TC), lambda c: (layer, 0, c))
    return pl.pallas_call(
        body, grid=(D_CONV // CONV_TC,), in_specs=specs, out_specs=pl.BlockSpec((S, CONV_TC), lambda c: (0, c)),
        out_shape=_out_hbm(SDS((S, D_CONV), f32)), name="conv_fwd", compiler_params=_cparams(1),
    )(*_in_hbm([xbc, conv_w, conv_b]))


def _conv_bwd(xbc, dact, conv_w, conv_b, dproj, layer):
    def body(u_ref, w_ref, b_ref, da_ref, dproj_in, du_ref, dw_ref, db_ref):
        u = u_ref[...].astype(f32)
        pre = _conv_pre(u, w_ref, b_ref, layer)
        sg = _sigmoid(pre)
        dpre = da_ref[...] * (sg * (1.0 + pre * (1.0 - sg)))
        du = w_ref[3:4, :] * dpre
        for k in range(3):
            du = du + w_ref[k:k + 1, :] * _shift_up(dpre, 3 - k)
        du_ref[...] = du.astype(bf16)
        db_ref[...] = jnp.broadcast_to(jnp.sum(dpre, axis=0, keepdims=True), db_ref.shape)
        dw_ref[...] = jnp.zeros_like(dw_ref)
        for k in range(4):
            dw_ref[k:k + 1, :] = jnp.sum(dpre * _shift_down(u, 3 - k), axis=0, keepdims=True)

    specs = _conv_specs()
    specs[1] = pl.BlockSpec((None, 4, CONV_TC), lambda c: (layer, 0, c))
    col = pl.BlockSpec((S, CONV_TC), lambda c: (0, c))
    row8 = pl.BlockSpec((8, CONV_TC), lambda c: (0, c))
    return pl.pallas_call(
        body, grid=(D_CONV // CONV_TC,), in_specs=[*specs, col, ANY_SPEC],
        out_specs=[pl.BlockSpec((S, CONV_TC), lambda c: (0, COL_XBC // CONV_TC + c)), row8, row8],
        out_shape=_out_hbm([SDS((S, D_IN_PAD), bf16), SDS((8, D_CONV), f32), SDS((8, D_CONV), f32)]), name="conv_bwd",
        input_output_aliases={4: 0}, compiler_params=_cparams(1),
    )(*_in_hbm([xbc, conv_w, conv_b, dact, dproj]))


def _tri():
    return (lax.broadcasted_iota(jnp.int32, (BLK, BLK), 0) >= lax.broadcasted_iota(jnp.int32, (BLK, BLK), 1))


def _ssd_scalars(dt_ref, dtb_ref, alog_ref, layer):
    raw = dt_ref[:, 0:NSSM] + dtb_ref[layer:layer + 1, :]
    dtv = jnp.maximum(raw, 0.0) + jnp.log(1.0 + jnp.exp(-jnp.abs(raw)))
    a = -jnp.exp(alog_ref[layer:layer + 1, :])
    acs = jnp.dot(_tri().astype(f32), dtv * a, preferred_element_type=f32, precision=HIGHEST)
    return raw, dtv, a, acs


HG = NSSM // NGRP
GW = HG * HD


def _lane_expand(cols, g):
    lane_head = lax.broadcasted_iota(jnp.int32, (1, GW), 1) // HD
    out = cols[:, HG * g + HG - 1:HG * g + HG]
    for r in range(HG - 2, -1, -1):
        out = jnp.where(lane_head == r, cols[:, HG * g + r:HG * g + r + 1], out)
    return out


def _row_expand(vals, g):
    row_head = lax.broadcasted_iota(jnp.int32, (GW, 1), 0) // HD
    out = vals[:, HG * g + HG - 1:HG * g + HG]
    for r in range(HG - 2, -1, -1):
        out = jnp.where(row_head == r, vals[:, HG * g + r:HG * g + r + 1], out)
    return out


def _head_rowsums(a, g):
    sel = (lax.broadcasted_iota(jnp.int32, (GW, NSSM), 0) // HD + HG * g == lax.broadcasted_iota(jnp.int32, (GW, NSSM), 1)).astype(bf16)
    hi = a.astype(bf16)
    lo = (a - hi.astype(f32)).astype(bf16)
    return _dot(hi, sel, NN_DIMS) + _dot(lo, sel, NN_DIMS)


def _head_blocksums(v, g):
    sel = (lax.broadcasted_iota(jnp.int32, (GW, NSSM), 0) // HD + HG * g == lax.broadcasted_iota(jnp.int32, (GW, NSSM), 1)).astype(bf16)
    hi = v.astype(bf16)
    lo = (v - hi.astype(f32)).astype(bf16)
    return _dot(hi, sel, TN_DIMS) + _dot(lo, sel, TN_DIMS)


def _ssd_chunk_common(xc_ref, dt_ref, dtb_ref, alog_ref, h_rows, layer):
    raw, dtv, a, acs = _ssd_scalars(dt_ref, dtb_ref, alog_ref, layer)
    acs_t = acs.T
    last = acs[BLK - 1:BLK, :]
    c = dict(raw=raw, dtv=dtv, a=a, acs=acs, last=last, dte=jnp.exp(last - acs), e_all=jnp.exp(acs), cd=jnp.exp(last))
    grp, heads, tri = range(NGRP), range(NSSM), _tri()
    c["bm"] = [xc_ref[:, pl.ds(D_SSM + NSTATE * g, NSTATE)] for g in grp]
    c["bm_b"] = [c["bm"][g].astype(bf16) for g in grp]
    c["cm_b"] = [xc_ref[:, pl.ds(D_SSM + NGRP * NSTATE + NSTATE * g, NSTATE)].astype(bf16) for g in grp]
    c["cb"] = [_dot(c["cm_b"][g], c["bm_b"][g], NT_DIMS) for g in grp]
    c["x"] = [xc_ref[:, pl.ds(GW * g, GW)] for g in grp]
    c["dt"] = [_lane_expand(dtv, g) for g in grp]
    c["xdt"] = [c["x"][g] * c["dt"][g] for g in grp]
    c["xdt_b"] = [c["xdt"][g].astype(bf16) for g in grp]
    c["prev"] = [h_rows(g) for g in grp]
    c["prev_b"] = [c["prev"][g].astype(bf16) for g in grp]
    c["e"] = [_lane_expand(c["e_all"], g) for g in grp]
    c["y_off"] = [_dot(c["cm_b"][g], c["prev_b"][g], NT_DIMS) * c["e"][g] for g in grp]
    c["decay"] = [jnp.exp(jnp.where(tri, acs[:, h:h + 1] - acs_t[h:h + 1, :], -jnp.inf)) for h in heads]
    c["m"] = [c["cb"][h // HG] * c["decay"][h] for h in heads]
    c["m_b"] = [c["m"][h].astype(bf16) for h in heads]
    c["dte_x"] = [_lane_expand(c["dte"], g) for g in grp]
    c["xdte_b"] = [(c["xdt"][g] * c["dte_x"][g]).astype(bf16) for g in grp]
    return c


def _ssd_fwd(xact, z, dt, attn, dt_bias, a_log, d_skip, norm_g, layer):
    def body(xc_ref, z_ref, dt_ref, at_ref, dtb_ref, alog_ref, dsk_ref, ng_ref, mix_ref, hs_ref, y_ref, h_ref):
        n = pl.program_id(0)

        @pl.when(n == 0)
        def _():
            h_ref[...] = jnp.zeros_like(h_ref)

        hs_ref[...] = h_ref[...]
        c = _ssd_chunk_common(xc_ref, dt_ref, dtb_ref, alog_ref, lambda g: h_ref[pl.ds(GW * g, GW), :], layer)
        grp, heads = range(NGRP), range(NSSM)
        y_diag = [_dot(c["m_b"][h], c["xdt_b"][h // HG][:, HD * (h % HG):HD * (h % HG + 1)], NN_DIMS) for h in heads]
        new_st = [_dot(c["xdte_b"][g], c["bm_b"][g], TN_DIMS) for g in grp]
        for h in heads:
            y_ref[:, pl.ds(HD * h, HD)] = y_diag[h]
        dskip = dsk_ref[layer:layer + 1, :]
        for g in grp:
            cols = pl.ds(GW * g, GW)
            y_ref[:, cols] = y_ref[:, cols] + c["y_off"][g] + c["x"][g] * _lane_expand(dskip, g)
            h_ref[cols, :] = c["prev"][g] * _row_expand(c["cd"], g) + new_st[g]
        zv = z_ref[...].astype(f32)
        yz = y_ref[...] * (zv * _sigmoid(zv))
        mix_ref[:, 0:D_ATTN] = at_ref[...]
        for g in grp:
            yg = yz[:, GW * g:GW * (g + 1)]
            rs = lax.rsqrt(jnp.mean(yg * yg, axis=-1, keepdims=True) + EPS)
            mix_ref[:, D_ATTN + GW * g:D_ATTN + GW * (g + 1)] = (yg * rs * ng_ref[layer:layer + 1, GW * g:GW * (g + 1)]).astype(bf16)

    small = lambda shape: pl.BlockSpec(shape, lambda n: (0,) * len(shape))
    return pl.pallas_call(
        body, grid=(NBLK,),
        in_specs=[pl.BlockSpec((BLK, D_CONV), lambda n: (n, 0)), pl.BlockSpec((BLK, D_SSM), lambda n: (n, 0)),
                  pl.BlockSpec((BLK, 128), lambda n: (n, 0)), pl.BlockSpec((BLK, D_ATTN), lambda n: (n, 0)),
                  small((DEPTH, NSSM)), small((DEPTH, NSSM)), small((DEPTH, NSSM)), small((DEPTH, D_SSM))],
        out_specs=[pl.BlockSpec((BLK, D), lambda n: (n, 0)), pl.BlockSpec((None, NSSM * HD, NSTATE), lambda n: (n, 0, 0)),
                   pl.BlockSpec((BLK, D_SSM), lambda n: (n, 0))],
        out_shape=_out_hbm([SDS((S, D), bf16), SDS((NBLK, NSSM * HD, NSTATE), f32), SDS((S, D_SSM), f32)]),
        scratch_shapes=[pltpu.VMEM((NSSM * HD, NSTATE), f32)],
        name="ssd_fwd", compiler_params=_cparams(1),
    )(*_in_hbm([xact, z, dt, attn, dt_bias, a_log, d_skip, norm_g]))


def _ssd_bwd(xact, z, dt, dmix, hs, y, dt_bias, a_log, d_skip, norm_g, dproj, layer):
    def body(xc_ref, z_ref, dt_ref, do_ref, hs_ref, y_ref, dtb_ref, alog_ref, dsk_ref, ng_ref, dproj_in,
             dzdt_ref, dx_ref, dsm_ref, dh_ref, dy_ref):
        i = pl.program_id(0)

        @pl.when(i == 0)
        def _():
            dh_ref[...] = jnp.zeros_like(dh_ref)
            dsm_ref[...] = jnp.zeros_like(dsm_ref)

        c = _ssd_chunk_common(xc_ref, dt_ref, dtb_ref, alog_ref, lambda g: hs_ref[pl.ds(GW * g, GW), :], layer)
        raw, dtv, a = c["raw"], c["dtv"], c["a"]
        grp, heads = range(NGRP), range(NSSM)
        dskip = dsk_ref[layer:layer + 1, :]
        lane8 = lax.broadcasted_iota(jnp.int32, (1, NSSM), 1)
        sub8 = lax.broadcasted_iota(jnp.int32, (NSSM, 1), 0)

        zv = z_ref[...].astype(f32)
        sz = _sigmoid(zv)
        gz = zv * sz
        yv = y_ref[...]
        yz = yv * gz
        for g in grp:
            sl = slice(GW * g, GW * (g + 1))
            yg = yz[:, sl]
            rs = lax.rsqrt(jnp.mean(yg * yg, axis=-1, keepdims=True) + EPS)
            yhat = yg * rs
            dog = do_ref[:, sl]
            w = dog * ng_ref[layer:layer + 1, sl]
            dyz = rs * (w - yhat * jnp.mean(yhat * w, axis=-1, keepdims=True))
            dsm_ref[0:1, sl] += jnp.sum(dog * yhat, axis=0, keepdims=True)
            dy_ref[:, sl] = dyz * gz[:, sl]
            dzdt_ref[:, sl] = (dyz * yv[:, sl] * (sz[:, sl] * (1.0 + zv[:, sl] * (1.0 - sz[:, sl])))).astype(bf16)

        dy = [dy_ref[:, pl.ds(GW * g, GW)] for g in grp]
        dy_b = [dy[g].astype(bf16) for g in grp]
        hl = lambda h: slice(HD * (h % HG), HD * (h % HG + 1))
        dt_off_b = [(dy[g] * c["e"][g]).astype(bf16) for g in grp]
        dcm = [_dot(dt_off_b[g], c["prev_b"][g], NN_DIMS) for g in grp]
        dprev = [_dot(dt_off_b[g], c["cm_b"][g], TN_DIMS) for g in grp]
        yoff_rs = [_head_rowsums(dy[g] * c["y_off"][g], g) for g in grp]
        dhn = [dh_ref[pl.ds(GW * g, GW), :] for g in grp]
        dhn_b = [dhn[g].astype(bf16) for g in grp]
        dprev = [dprev[g] + dhn[g] * _row_expand(c["cd"], g) for g in grp]
        dhn_prev = [dhn[g] * c["prev"][g] for g in grp]
        u = [_dot(c["bm_b"][g], dhn_b[g], NT_DIMS) for g in grp]
        dbm = [_dot(c["xdte_b"][g], dhn_b[g], NN_DIMS) for g in grp]
        ddte_rs = [_head_rowsums(c["xdt"][g] * u[g], g) for g in grp]
        dm = [_dot(dy_b[h // HG][:, hl(h)], c["xdt_b"][h // HG][:, hl(h)], NT_DIMS) for h in heads]
        dxdt_in = [_dot(c["m_b"][h], dy_b[h // HG][:, hl(h)], TN_DIMS) for h in heads]
        dseg = [dm[h] * c["m"][h] for h in heads]
        dmd = [dm[h] * c["decay"][h] for h in heads]
        for h in heads:
            dx_ref[:, pl.ds(HD * h, HD)] = dxdt_in[h]

        tmp = (ddte_rs[0] + ddte_rs[1]) * c["dte"]
        dacs = yoff_rs[0] + yoff_rs[1] - tmp
        dacs_cols = jnp.zeros((NSSM, BLK), f32)
        ddtv = jnp.zeros((BLK, NSSM), f32)
        ddsk = jnp.zeros((BLK, NSSM), f32)
        hp = jnp.zeros((1, NSSM), f32)
        for g in grp:
            cols = pl.ds(GW * g, GW)
            dxdt = dx_ref[:, cols] + u[g] * c["dte_x"][g]
            dx_ref[:, cols] = dy[g] * _lane_expand(dskip, g) + dxdt * c["dt"][g]
            ddtv = ddtv + _head_rowsums(dxdt * c["x"][g], g)
            ddsk = ddsk + _head_rowsums(dy[g] * c["x"][g], g)
            dcb = dmd[HG * g]
            for r in range(1, HG):
                dcb = dcb + dmd[HG * g + r]
            dcb_b = dcb.astype(bf16)
            dx_ref[:, pl.ds(D_SSM + NSTATE * g, NSTATE)] = dbm[g] + _dot(dcb_b, c["cm_b"][g], TN_DIMS)
            dx_ref[:, pl.ds(D_SSM + NGRP * NSTATE + NSTATE * g, NSTATE)] = dcm[g] + _dot(dcb_b, c["bm_b"][g], NN_DIMS)
            dh_ref[cols, :] = dprev[g]
            hp = hp + _head_blocksums(jnp.sum(dhn_prev[g], axis=1, keepdims=True), g)
            for r in range(HG):
                h = HG * g + r
                dacs = dacs + (lane8 == h).astype(f32) * jnp.sum(dseg[h], axis=1, keepdims=True)
                dacs_cols = dacs_cols + (sub8 == h).astype(f32) * jnp.sum(dseg[h], axis=0, keepdims=True)
        dlast = hp * c["cd"] + jnp.sum(tmp, axis=0, keepdims=True)
        ddsk = jnp.sum(ddsk, axis=0, keepdims=True)

        row = lax.broadcasted_iota(jnp.int32, (BLK, 1), 0)
        dacs = dacs - dacs_cols.T + jnp.where(row == BLK - 1, dlast, 0.0)
        dda = lax.dot_general(_tri().astype(f32), dacs, TN_DIMS, preferred_element_type=f32, precision=HIGHEST)
        ddtv = ddtv + dda * a
        da = jnp.sum(dda * dtv, axis=0, keepdims=True)
        draw = ddtv * _sigmoid(raw)
        dzdt_ref[:, D_SSM:] = jnp.zeros((BLK, COL_XBC - COL_DT), bf16)
        dzdt_ref[:, D_SSM:D_SSM + NSSM] = draw.astype(bf16)
        dsm_ref[1:2, 0:NSSM] += jnp.sum(draw, axis=0, keepdims=True)
        dsm_ref[2:3, 0:NSSM] += da * a
        dsm_ref[3:4, 0:NSSM] += ddsk

    rev = lambda i: NBLK - 1 - i
    small = lambda shape: pl.BlockSpec(shape, lambda i: (0,) * len(shape))
    return pl.pallas_call(
        body, grid=(NBLK,),
        in_specs=[pl.BlockSpec((BLK, D_CONV), lambda i: (rev(i), 0)), pl.BlockSpec((BLK, D_SSM), lambda i: (rev(i), 0)),
                  pl.BlockSpec((BLK, 128), lambda i: (rev(i), 0)), pl.BlockSpec((BLK, D_SSM), lambda i: (rev(i), 1)),
                  pl.BlockSpec((None, NSSM * HD, NSTATE), lambda i: (rev(i), 0, 0)), pl.BlockSpec((BLK, D_SSM), lambda i: (rev(i), 0)),
                  small((DEPTH, NSSM)), small((DEPTH, NSSM)), small((DEPTH, NSSM)), small((DEPTH, D_SSM)), ANY_SPEC],
        out_specs=[pl.BlockSpec((BLK, COL_XBC - COL_Z), lambda i: (rev(i), COL_Z // (COL_XBC - COL_Z))),
                   pl.BlockSpec((BLK, D_CONV), lambda i: (rev(i), 0)), small((8, D_SSM))],
        out_shape=_out_hbm([SDS((S, D_IN_PAD), bf16), SDS((S, D_CONV), f32), SDS((8, D_SSM), f32)]),
        scratch_shapes=[pltpu.VMEM((NSSM * HD, NSTATE), f32), pltpu.VMEM((BLK, D_SSM), f32)],
        name="ssd_bwd", input_output_aliases={10: 0}, compiler_params=_cparams(1),
    )(*_in_hbm([xact, z, dt, dmix, hs, y, dt_bias, a_log, d_skip, norm_g, dproj]))


def _my_place():
    return lax.axis_index("x"), lax.axis_index("y"), lax.axis_index("c")


def _dev_index(px, py, pc):
    return 4 * px + 2 * py + pc


def _slab2(kind, ref, idx):
    if kind == "stack":
        return ref.at[idx]
    if kind == "rows128":
        return ref.at[pl.ds(pl.multiple_of(idx * 128, 128), 128), :]
    if kind == "rows512":
        return ref.at[pl.ds(pl.multiple_of(idx * 512, 512), 512), :]
    return ref.at[:, pl.ds(pl.multiple_of(idx * 512, 512), 512)]


def _slab_shape(kind, full_shape):
    if kind == "stack":
        return tuple(full_shape[1:])
    if kind == "rows128":
        return (128, full_shape[1])
    if kind == "rows512":
        return (512, full_shape[1])
    return (full_shape[0], 512)


KIND = dict(w_in="stack", w_out="rows128", w_up="cols512", w_down="rows512", conv_w="stack")
FULL_SHAPE = dict(w_in=(N_DEV, D, D_IN // N_DEV), w_out=(D, D), w_up=(D, D_FF), w_down=(D_FF, D))
HBM_SPEC = pl.BlockSpec(memory_space=pltpu.HBM)
SEM_SPEC = pl.BlockSpec(memory_space=pltpu.SEMAPHORE)
SIDE_EFFECT = pltpu.SideEffectType.DATAFLOW_SIDE_EFFECTING


def _peers_all():
    x, y, c = _my_place()
    return [(x ^ ((r >> 2) & 1), y ^ ((r >> 1) & 1), c ^ (r & 1)) for r in range(1, N_DEV)]


def _split_start(name, bufs, n_copies, plan, deps=()):
    nb = len(bufs)

    def body(*refs):
        ins = refs[:nb]
        send_sems, recv_sems = refs[nb + len(deps)], refs[nb + len(deps) + 1]
        token = refs[-1]
        for i, (src, dst, dev) in enumerate(plan(ins)):
            pltpu.make_async_remote_copy(src_ref=src, dst_ref=dst, send_sem=send_sems.at[i], recv_sem=recv_sems.at[i],
                                         device_id=dev, device_id_type=MESH).start()
        token[...] = jnp.zeros_like(token)

    outs = pl.pallas_call(
        body, name=name,
        out_shape=(pltpu.SemaphoreType.DMA((n_copies,)), pltpu.SemaphoreType.DMA((n_copies,)),
                   *[pltpu.HBM(b.shape, b.dtype) for b in bufs], SDS((8, 128), f32)),
        in_specs=[HBM_SPEC] * nb + [ANY_SPEC] * len(deps),
        out_specs=(SEM_SPEC, SEM_SPEC, *[HBM_SPEC] * nb, pl.BlockSpec(memory_space=pltpu.VMEM)),
        input_output_aliases={i: 2 + i for i in range(nb)},
        compiler_params=pltpu.CompilerParams(has_side_effects=SIDE_EFFECT),
    )(*[pltpu.with_memory_space_constraint(b, pltpu.HBM) for b in bufs], *deps)
    return dict(send=outs[0], recv=outs[1], bufs=list(outs[2:2 + nb]), token=outs[-1], plan=plan, n=n_copies)


def _split_wait(name, started, after):
    bufs = started["bufs"]
    nb = len(bufs)
    plan = started["plan"]

    def body(*refs):
        ins = refs[:nb]
        send_sems, recv_sems = refs[nb], refs[nb + 1]
        for i, (src, dst, dev) in enumerate(plan(ins)):
            cp = pltpu.make_async_remote_copy(src_ref=src, dst_ref=dst, send_sem=send_sems.at[i], recv_sem=recv_sems.at[i],
                                              device_id=dev, device_id_type=MESH)
            cp.wait_send()
            cp.wait_recv()

    outs = pl.pallas_call(
        body, name=name, out_shape=tuple(pltpu.HBM(b.shape, b.dtype) for b in bufs),
        in_specs=[HBM_SPEC] * nb + [SEM_SPEC, SEM_SPEC] + [ANY_SPEC] * len(after), out_specs=(HBM_SPEC,) * nb,
        input_output_aliases={i: i for i in range(nb)},
        compiler_params=pltpu.CompilerParams(has_side_effects=SIDE_EFFECT),
    )(*bufs, started["send"], started["recv"], *after)
    return list(outs)


def _gather_start(name, names, fulls, deps):
    n_t = len(names)

    def plan(refs):
        x, y, c = _my_place()
        my_idx = _dev_index(x, y, c)
        targets = [(x, y, 1 - c), (1 - x, y, c), (x, 1 - y, c), (1 - x, 1 - y, c)]
        slabs = [_slab2(KIND[names[t]], refs[t], my_idx) for t in range(n_t)]
        return [(slabs[t], slabs[t], dev) for t in range(n_t) for dev in targets]

    return _split_start(name, list(fulls), 4 * n_t, plan, deps)


def _gather_finish(name, names, started, after):
    n_t = len(names)
    fulls = _split_wait(name + "_wait", started, after)
    slab_shapes = [SDS(_slab_shape(KIND[n], f.shape), f.dtype) for n, f in zip(names, fulls)]

    def body(*refs):
        ins = refs[:n_t]
        outs = refs[n_t:2 * n_t]
        stage = refs[2 * n_t:3 * n_t]
        load_sems, send_sems, recv_sems = refs[3 * n_t:]
        x, y, c = _my_place()
        chips = [(1 - x, y), (x, 1 - y), (1 - x, 1 - y)]
        pairs = [(t, j) for t in range(n_t) for j in range(3)]
        loads = [pltpu.make_async_copy(_slab2(KIND[names[t]], ins[t], _dev_index(*chips[j], c)), stage[t].at[j], load_sems.at[t, j])
                 for t, j in pairs]
        for cp in loads:
            cp.start()

        def copy(t, j, core):
            return pltpu.make_async_remote_copy(
                src_ref=stage[t].at[j], dst_ref=_slab2(KIND[names[t]], outs[t], _dev_index(*chips[j], core)),
                send_sem=send_sems.at[t, j], recv_sem=recv_sems.at[t, j], device_id=(x, y, 1 - c), device_id_type=MESH)

        sends = [copy(t, j, c) for t, j in pairs]
        for ld, cp in zip(loads, sends):
            ld.wait()
            cp.start()
        for t, j in pairs:
            copy(t, j, 1 - c).wait_recv()
        for cp in sends:
            cp.wait_send()

    return pl.pallas_call(
        body, in_specs=[ANY_SPEC] * n_t, out_specs=[ANY_SPEC] * n_t, out_shape=[SDS(b.shape, b.dtype) for b in fulls],
        input_output_aliases={t: t for t in range(n_t)},
        scratch_shapes=[pltpu.VMEM((3,) + s.shape, s.dtype) for s in slab_shapes]
        + [pltpu.SemaphoreType.DMA((n_t, 3)), pltpu.SemaphoreType.DMA((n_t, 3)), pltpu.SemaphoreType.DMA((n_t, 3))],
        name=name + "_pass", compiler_params=pltpu.CompilerParams(vmem_limit_bytes=VMEM_LIMIT),
    )(*fulls)


def _exchange_start(name, names, grads, deps):
    n_t = len(names)
    lands = [lax.empty((N_DEV,) + _slab_shape(KIND[n], g.shape), g.dtype) for n, g in zip(names, grads)]

    def plan(refs):
        my_idx = _dev_index(*_my_place())
        return [(_slab2(KIND[names[t]], refs[t], _dev_index(*peer)), refs[n_t + t].at[my_idx], peer)
                for t in range(n_t) for peer in _peers_all()]

    return _split_start(name, list(grads) + lands, 7 * n_t, plan, deps)


def _small_exchange_start(part, deps):
    land = lax.empty((N_DEV,) + part.shape, part.dtype)

    def plan(refs):
        my_idx = _dev_index(*_my_place())
        return [(refs[0], refs[1].at[my_idx], peer) for peer in _peers_all()]

    return _split_start("small_exchange", [part, land], N_DEV - 1, plan, deps)


def _slab_pieces():
    sh = D_IN // N_DEV
    out = []
    for j in range(N_DEV):
        for first, end, dst in IN_SEGMENTS:
            lo, hi = max(first, sh * j), min(end, sh * (j + 1))
            if lo < hi:
                out.append((j, lo - sh * j, hi - sh * j, dst + lo - first))
    return out


def _w_in_assemble(stacked):
    tr = 256
    sh = D_IN // N_DEV

    def body(i_ref, o_ref):
        o_ref[:, COL_DT:COL_XBC] = jnp.zeros((tr, COL_XBC - COL_DT), bf16)
        for j, lo, hi, dst in _slab_pieces():
            o_ref[:, dst:dst + hi - lo] = i_ref[j, :, lo:hi]

    return pl.pallas_call(
        body, grid=(D // tr,), in_specs=[pl.BlockSpec((N_DEV, tr, sh), lambda i: (0, i, 0))],
        out_specs=pl.BlockSpec((None, tr, D_IN_PAD), lambda i: (0, i, 0)), out_shape=_out_hbm(SDS((1, D, D_IN_PAD), bf16)),
        name="w_in_assemble", compiler_params=_cparams(1),
    )(*_in_hbm([stacked]))


def _w_in_slabs(dw_in):
    tr = 256
    sh = D_IN // N_DEV

    def body(i_ref, o_ref):
        for j, lo, hi, src in _slab_pieces():
            o_ref[j, :, lo:hi] = i_ref[:, src:src + hi - lo]

    return pl.pallas_call(
        body, grid=(D // tr,), in_specs=[pl.BlockSpec((tr, D_IN_PAD), lambda i: (i, 0))],
        out_specs=pl.BlockSpec((N_DEV, tr, sh), lambda i: (0, i, 0)), out_shape=_out_hbm(SDS((N_DEV, D, sh), bf16)),
        name="w_in_slabs", compiler_params=_cparams(1),
    )(*_in_hbm([dw_in]))


SMALL_NAMES = ("mix_norm_g", "mlp_norm_g", "conv_b", "ssm_norm_g", "q_gain", "k_gain", "sinks", "dt_bias", "a_log", "d_skip",
               "rel_bias", "conv_w")
MISC_LANES = dict(q_gain=(LANE_QG, HD), k_gain=(LANE_KG, HD), sinks=(LANE_SINK, NQ), dt_bias=(LANE_DTB, NSSM),
                  a_log=(LANE_ALOG, NSSM), d_skip=(LANE_DSKIP, NSSM))


def _pack_small_grads(smalls, drel_t, loss):
    def body(*refs):
        o_ref = refs[-1]
        drel_ref, loss_ref = refs[-3], refs[-2]
        o_ref[...] = jnp.zeros_like(o_ref)
        for l in range(DEPTH):
            mixg, mlpg, convb, convw, ssd, attn = refs[6 * l:6 * l + 6]
            o_ref[ROW_MIXG + l:ROW_MIXG + l + 1, :] = mixg[...]
            o_ref[ROW_MLPG + l:ROW_MLPG + l + 1, :] = mlpg[...]
            o_ref[ROW_CONVB + l:ROW_CONVB + l + 1, :] = convb[0:1, :]
            o_ref[ROW_SSMG + l:ROW_SSMG + l + 1, 0:D_SSM] = ssd[0:1, :]
            o_ref[ROW_CONVW + 4 * l:ROW_CONVW + 4 * l + 4, :] = convw[0:4, :]
            row = slice(ROW_MISC + l, ROW_MISC + l + 1)
            o_ref[row, LANE_QG:LANE_QG + HD] = attn[0:1, 0:HD]
            o_ref[row, LANE_KG:LANE_KG + HD] = attn[1:2, 0:HD]
            o_ref[row, LANE_SINK:LANE_SINK + NQ] = attn[2:3, 0:NQ]
            o_ref[row, LANE_DTB:LANE_DTB + NSSM] = ssd[1:2, 0:NSSM]
            o_ref[row, LANE_ALOG:LANE_ALOG + NSSM] = ssd[2:3, 0:NSSM]
            o_ref[row, LANE_DSKIP:LANE_DSKIP + NSSM] = ssd[3:4, 0:NSSM]
        o_ref[ROW_RELB:ROW_RELB + NQ, 0:N_BUCKETS] = drel_ref[...]
        o_ref[ROW_LOSS:ROW_LOSS + 1, 0:1] = loss_ref[0:1, 0:1]

    args = []
    for sm in smalls:
        args += [sm["mix_norm_g"], sm["mlp_norm_g"], sm["conv_b"], sm["conv_w"], sm["ssd"], sm["attn"]]
    args += [drel_t, loss]
    return pl.pallas_call(body, out_shape=SDS((SMALL_ROWS, D), f32), name="pack_small_grads")(*args)


def _adamw_small(part, land, w, m, v):
    n = len(SMALL_NAMES)

    def grad_of(name, g_ref):
        if name == "mix_norm_g":
            return g_ref[ROW_MIXG:ROW_MIXG + DEPTH, :]
        if name == "mlp_norm_g":
            return g_ref[ROW_MLPG:ROW_MLPG + DEPTH, :]
        if name == "conv_b":
            return g_ref[ROW_CONVB:ROW_CONVB + DEPTH, :]
        if name == "ssm_norm_g":
            return g_ref[ROW_SSMG:ROW_SSMG + DEPTH, 0:D_SSM]
        if name == "rel_bias":
            return g_ref[ROW_RELB:ROW_RELB + NQ, 0:N_BUCKETS].T
        lane, width = MISC_LANES[name]
        return g_ref[ROW_MISC:ROW_MISC + DEPTH, lane:lane + width]

    def body(part_ref, land_ref, *refs):
        ws, ms, vs = refs[:n], refs[n:2 * n], refs[2 * n:3 * n]
        loss_ref = refs[3 * n]
        outs = refs[3 * n + 1:-1]
        g_ref = refs[-1]
        me = _dev_index(*_my_place())
        for p in range(N_DEV):
            term = jnp.where(me == p, part_ref[...], land_ref[p])
            if p == 0:
                g_ref[...] = term
            else:
                g_ref[...] += term
        loss_ref[...] = g_ref[ROW_LOSS:ROW_LOSS + 1, 0:128]
        my_cols = pl.ds(pl.multiple_of(me * 128, 128), 128)
        for k, name in enumerate(SMALL_NAMES):
            g_out, d_out, m_out, v_out = outs[4 * k:4 * k + 4]
            if name == "conv_w":
                for l in range(DEPTH):
                    g = g_ref[ROW_CONVW + 4 * l:ROW_CONVW + 4 * l + 4, my_cols]
                    delta, m_new, v_new = _adamw_math(ws[k][l], ms[k][l], vs[k][l], g)
                    g_out[l], d_out[l], m_out[l], v_out[l] = g, delta, m_new, v_new
            else:
                g = grad_of(name, g_ref)
                delta, m_new, v_new = _adamw_math(ws[k][...], ms[k][...], vs[k][...], g)
                g_out[...], d_out[...], m_out[...], v_out[...] = g, delta, m_new, v_new

    ws = [w[name] for name in SMALL_NAMES]
    out_shape = [SDS((1, 128), f32)]
    for a in ws:
        out_shape += [SDS(a.shape, f32)] * 4
    return pl.pallas_call(body, out_shape=out_shape, name="adamw_small", scratch_shapes=[pltpu.VMEM((SMALL_ROWS, D), f32)])(
        part, land, *ws, *[m[name] for name in SMALL_NAMES], *[v[name] for name in SMALL_NAMES])


def _plain(tm, tn):
    return pl.BlockSpec((tm, tn), lambda i, j, k: (i, j))


def _rowblk(tm, width):
    return pl.BlockSpec((tm, width), lambda i, j, k: (i, 0))


def _store_epi(dtype):
    def epi(acc, i, j, ex, outs):
        outs[0][...] = acc.astype(dtype)
    return epi


def _rms_prologue(layer):
    def pro(a_ref, ex, outs):
        xv = a_ref[...]
        r = lax.rsqrt(jnp.mean(xv * xv, axis=-1, keepdims=True) + EPS)
        h = (xv * r * ex[0][layer:layer + 1, :]).astype(bf16)
        outs[-1][...] = h
        return h
    return pro


MLP_TM = 256
MLP_VMEM = 56 * 1024 * 1024


def _resident(shape):
    return pl.BlockSpec((None,) + shape, lambda i: (0, 0, 0), pipeline_mode=pl.Buffered(1))


def _mlp_fwd(layer, x, mix, g, w_out, w_up, w_down, tgt=None):
    tm = MLP_TM
    with_loss = tgt is not None

    def body(x_ref, mix_ref, g_ref, wo_ref, wu_ref, wd_ref, *rest):
        xm_ref, a_ref, r_ref, h_ref = rest[with_loss:with_loss + 4]
        rest = rest[:with_loss] + rest[with_loss + 1:]
        i = pl.program_id(0)
        xv = x_ref[...] + _dot(mix_ref[...], wo_ref[...], NN_DIMS)
        xm_ref[...] = xv
        h = (xv * lax.rsqrt(jnp.mean(xv * xv, axis=-1, keepdims=True) + EPS) * g_ref[layer:layer + 1, :]).astype(bf16)
        h_ref[...] = h
        r = jnp.maximum(_dot(h, wu_ref[...], NN_DIMS), 0.0)
        a = (r * r).astype(bf16)
        a_ref[...] = a
        r_ref[...] = r.astype(bf16)
        y = xv + _dot(a, wd_ref[...], NN_DIMS)
        if not with_loss:
            rest[3][...] = y
            return
        err = y - rest[0][...]
        rest[4][...] = err * (1.0 / D)
        part = 0.5 * jnp.sum(jnp.mean(err * err, axis=-1, keepdims=True), axis=0, keepdims=True)

        @pl.when(i == 0)
        def _():
            rest[5][...] = jnp.zeros_like(rest[5])

        rest[5][...] += jnp.broadcast_to(part, rest[5].shape)

    row = lambda width: pl.BlockSpec((tm, width), lambda i: (i, 0))
    in_specs = [row(D), row(D), pl.BlockSpec((DEPTH, D), lambda i: (0, 0)), _resident((D, D)), _resident((D, D_FF)),
                _resident((D_FF, D))]
    out_specs = [row(D), row(D_FF), row(D_FF), row(D), row(D)]
    out_shape = [SDS((S, D), f32), SDS((S, D_FF), bf16), SDS((S, D_FF), bf16), SDS((S, D), bf16), SDS((S, D), f32)]
    args = [x, mix, g, w_out, w_up, w_down]
    if with_loss:
        in_specs.append(row(D))
        args.append(tgt)
        out_specs.append(pl.BlockSpec((1, 128), lambda i: (0, 0)))
        out_shape.append(SDS((1, 128), f32))
    return pl.pallas_call(
        body, grid=(S // tm,), in_specs=in_specs, out_specs=out_specs, out_shape=_out_hbm(out_shape),
        name="mlp_fwd_loss" if with_loss else "mlp_fwd",
        compiler_params=pltpu.CompilerParams(dimension_semantics=("arbitrary",), vmem_limit_bytes=MLP_VMEM),
    )(*_in_hbm(args[:3]), *args[3:6], *_in_hbm(args[6:]))


def _mlp_bwd_act(layer, dx_out, r_act, x_mid, g, w_down, w_up, w_out, deps):
    tm = MLP_TM

    def body(dxo_ref, r_ref, xm_ref, g_ref, wd_ref, wu_ref, wo_ref, *rest):
        du_ref, dx_ref, dg_ref, dmix_ref = rest[len(deps):]
        dxo = dxo_ref[...]
        du = (_dot(dxo.astype(bf16), wd_ref[...], NT_DIMS) * (2.0 * r_ref[...].astype(f32))).astype(bf16)
        du_ref[...] = du
        dh = _dot(du, wu_ref[...], NT_DIMS)
        _rms_bwd_epilogue(layer)(dh, pl.program_id(0), 0, (xm_ref, g_ref, dxo_ref), (dx_ref, dg_ref))
        dmix_ref[...] = _dot(dx_ref[...].astype(bf16), wo_ref[...], NT_DIMS)

    row = lambda width: pl.BlockSpec((tm, width), lambda i: (i, 0))
    return pl.pallas_call(
        body, grid=(S // tm,),
        in_specs=[row(D), row(D_FF), row(D), pl.BlockSpec((DEPTH, D), lambda i: (0, 0)), _resident((D_FF, D)), _resident((D, D_FF)),
                  _resident((D, D))] + [ANY_SPEC] * len(deps),
        out_specs=[row(D_FF), row(D), pl.BlockSpec((1, D), lambda i: (0, 0)), row(D)],
        out_shape=_out_hbm([SDS((S, D_FF), bf16), SDS((S, D), f32), SDS((1, D), f32), SDS((S, D), f32)]), name="mlp_bwd_act",
        compiler_params=pltpu.CompilerParams(dimension_semantics=("arbitrary",), vmem_limit_bytes=MLP_VMEM),
    )(*_in_hbm([dx_out, r_act, x_mid, g]), w_down, w_up, w_out, *_in_hbm(deps))


def _layer_fwd(l, x, p, get_weights, bias, tgt=None):
    wts = get_weights(l, "in", [x, bias])
    gfull = pl.BlockSpec((DEPTH, D), lambda i, j, k: (0, 0))
    tm = 512

    def inproj_epi(acc, i, j, ex, outs):
        outs[0][...] = acc[:, COL_QKV:COL_Z].astype(bf16)
        outs[1][...] = acc[:, COL_Z:COL_DT].astype(bf16)
        outs[2][...] = acc[:, COL_XBC:D_IN_PAD].astype(bf16)
        outs[3][...] = acc[:, COL_DT:COL_DT + 128]

    qkv, z, xbc, dt, h1 = _matmul(
        "in_proj", "nn", x, wts["w_in"], tm=tm, tn=D_IN_PAD, tk=D, prologue=_rms_prologue(l),
        extras=(p["mix_norm_g"],), extra_specs=(gfull,),
        out_shape=[SDS((S, 768), bf16), SDS((S, 512), bf16), SDS((S, 1024), bf16), SDS((S, 128), f32), SDS((S, D), bf16)],
        out_specs=[_rowblk(tm, 768), _rowblk(tm, 512), _rowblk(tm, 1024), _rowblk(tm, 128), _rowblk(tm, D)], epilogue=inproj_epi)
    attn = _attn_fwd(qkv, p["q_gain"], p["k_gain"], p["sinks"], bias, l)
    xact = _conv_fwd(xbc, wts["conv_w"], p["conv_b"], l)
    mix, hs, y_ssd = _ssd_fwd(xact, z, dt, attn, p["dt_bias"], p["a_log"], p["d_skip"], p["ssm_norm_g"], l)
    wts = dict(wts, **get_weights(l, "rest", [mix]))

    x_mid, a_act, r_act, h2, *result = _mlp_fwd(l, x, mix, p["mlp_norm_g"], wts["w_out"], wts["w_up"], wts["w_down"], tgt)
    saved = dict(x=x, h1=h1, qkv=qkv, z=z, xbc=xbc, dt=dt, xact=xact, mix=mix, hs=hs, y_ssd=y_ssd, x_mid=x_mid, h2=h2,
                 a=a_act, r=r_act, wts=wts)
    return (result[0] if tgt is None else tuple(result)), saved


def _layer_bwd(l, dx_out, sv, p, bias, deps, send):
    wts = sv["wts"]

    dw_down = _matmul("dw_down", "tn", sv["a"], dx_out, tm=1024, tn=D, tk=S, out_shape=SDS((D_FF, D), bf16),
                      out_specs=_plain(1024, D), epilogue=_store_epi(bf16), deps=deps)
    deps = send(l, dict(w_down=dw_down))
    du, dx_mid, dg_mlp, dmix = _mlp_bwd_act(l, dx_out, sv["r"], sv["x_mid"], p["mlp_norm_g"], wts["w_down"], wts["w_up"],
                                            wts["w_out"], deps)
    dw_up = _matmul("dw_up", "tn", sv["h2"], du, tm=D, tn=1024, tk=S, out_shape=SDS((D, D_FF), bf16),
                    out_specs=_plain(D, 1024), epilogue=_store_epi(bf16))
    dw_out = _matmul("dw_out", "tn", sv["mix"], dx_mid, tm=D, tn=512, tk=S, out_shape=SDS((D, D), bf16),
                     out_specs=_plain(D, 512), epilogue=_store_epi(bf16))
    deps = send(l, dict(w_up=dw_up, w_out=dw_out))
    gfull = pl.BlockSpec((DEPTH, D), lambda i, j, k: (0, 0))
    grow = pl.BlockSpec((1, D), lambda i, j, k: (0, 0))
    dproj, dbias, dsm_attn = _attn_bwd(sv["qkv"], dmix, p["q_gain"], p["k_gain"], p["sinks"], bias, l, deps)
    dproj, dxact, dsm_ssd = _ssd_bwd(sv["xact"], sv["z"], sv["dt"], dmix, sv["hs"], sv["y_ssd"], p["dt_bias"], p["a_log"],
                                     p["d_skip"], p["ssm_norm_g"], dproj, l)
    dproj, dconv_w, dconv_b = _conv_bwd(sv["xbc"], dxact, wts["conv_w"], p["conv_b"], dproj, l)
    dw_in = _matmul("dw_in", "tn", sv["h1"], dproj, tm=D, tn=1280, tk=S, out_shape=SDS((D, D_IN_PAD), bf16),
                    out_specs=_plain(D, 1280), epilogue=_store_epi(bf16))
    deps = send(l, dict(w_in=_w_in_slabs(dw_in)))
    dx, dg_mix = _matmul(
        "in_proj_dh", "nt", dproj, wts["w_in"], tm=512, tn=D, tk=D_IN_PAD, out_shape=[SDS((S, D), f32), SDS((1, D), f32)],
        out_specs=[_plain(512, D), grow], epilogue=_rms_bwd_epilogue(l),
        extras=(sv["x"], p["mix_norm_g"], dx_mid), extra_specs=(_plain(512, D), gfull, _plain(512, D)), deps=deps)
    small = dict(mix_norm_g=dg_mix, mlp_norm_g=dg_mlp, conv_w=dconv_w, conv_b=dconv_b, ssd=dsm_ssd, attn=dsm_attn, dbias=dbias)
    return dx, small, deps


def _local_step(x, tgt, p, get_weights, send):
    onehot_t = jnp.asarray(_onehot_buckets(), dtype=bf16)
    bias = _bias_build(p["rel_bias"].T, onehot_t).reshape(NQ, BLK, 2 * BLK)
    saved = []
    h = x
    for l in range(DEPTH):
        h, sv = _layer_fwd(l, h, p, get_weights, bias, tgt if l == DEPTH - 1 else None)
        saved.append(sv)
    dx, loss = h
    smalls = [None] * DEPTH
    deps = ()
    for l in reversed(range(DEPTH)):
        dx, smalls[l], deps = _layer_bwd(l, dx, saved[l], p, bias, deps, send)
    drel_t = _bias_grad(smalls[0]["dbias"].reshape(NQ, -1), smalls[1]["dbias"].reshape(NQ, -1), onehot_t)
    return dx, _pack_small_grads(smalls, drel_t, loss)


WEIGHT_ORDER = ("mix_norm_g", "w_in", "q_gain", "k_gain", "sinks", "rel_bias", "conv_w", "conv_b", "dt_bias", "a_log", "d_skip",
                "ssm_norm_g", "w_out", "mlp_norm_g", "w_up", "w_down")


def kernel(x, mix_norm_g, w_in, q_gain, k_gain, sinks, rel_bias, conv_w, conv_b, dt_bias, a_log, d_skip, ssm_norm_g, w_out, mlp_norm_g, w_up, w_down, loss_target, m_mix_norm_g, m_w_in, m_q_gain, m_k_gain, m_sinks, m_rel_bias, m_conv_w, m_conv_b, m_dt_bias, m_a_log, m_d_skip, m_ssm_norm_g, m_w_out, m_mlp_norm_g, m_w_up, m_w_down, v_mix_norm_g, v_w_in, v_q_gain, v_k_gain, v_sinks, v_rel_bias, v_conv_w, v_conv_b, v_dt_bias, v_a_log, v_d_skip, v_ssm_norm_g, v_w_out, v_mlp_norm_g, v_w_up, v_w_down):
    w = dict(mix_norm_g=mix_norm_g, w_in=w_in, q_gain=q_gain, k_gain=k_gain, sinks=sinks, rel_bias=rel_bias, conv_w=conv_w,
             conv_b=conv_b, dt_bias=dt_bias, a_log=a_log, d_skip=d_skip, ssm_norm_g=ssm_norm_g, w_out=w_out,
             mlp_norm_g=mlp_norm_g, w_up=w_up, w_down=w_down)
    m = dict(mix_norm_g=m_mix_norm_g, w_in=m_w_in, q_gain=m_q_gain, k_gain=m_k_gain, sinks=m_sinks, rel_bias=m_rel_bias,
             conv_w=m_conv_w, conv_b=m_conv_b, dt_bias=m_dt_bias, a_log=m_a_log, d_skip=m_d_skip, ssm_norm_g=m_ssm_norm_g,
             w_out=m_w_out, mlp_norm_g=m_mlp_norm_g, w_up=m_w_up, w_down=m_w_down)
    v = dict(mix_norm_g=v_mix_norm_g, w_in=v_w_in, q_gain=v_q_gain, k_gain=v_k_gain, sinks=v_sinks, rel_bias=v_rel_bias,
             conv_w=v_conv_w, conv_b=v_conv_b, dt_bias=v_dt_bias, a_log=v_a_log, d_skip=v_d_skip, ssm_norm_g=v_ssm_norm_g,
             w_out=v_w_out, mlp_norm_g=v_mlp_norm_g, w_up=v_w_up, w_down=v_w_down)
    big = ("w_in", "w_out", "w_up", "w_down")

    my_idx = _dev_index(*_my_place()).astype(jnp.int32).reshape(1)

    fulls = {n: _cast_to_full("cast_" + n, w[n], KIND[n], FULL_SHAPE[n], my_idx, bf16) for n in big}
    conv_full = _cast_to_full("cast_conv_w", conv_w.reshape(1, DEPTH * 4, 128), "stack", (N_DEV, DEPTH * 4, 128), my_idx, f32)[0]
    rest = ["w_out", "w_up", "w_down"]
    g0 = _gather_start("gather0", ["w_in", "conv_w"], [fulls["w_in"][0], conv_full], ())
    g1 = _gather_start("gather1", rest, [fulls[n][0] for n in rest], (g0["token"],))
    g2 = _gather_start("gather2", ["w_in"], [fulls["w_in"][1]], (g1["token"],))
    g3 = _gather_start("gather3", rest, [fulls[n][1] for n in rest], (g2["token"],))
    held = {}
    flat = lambda a: a.reshape(a.shape[0] * a.shape[1], a.shape[2])
    adam_in = {n: (flat(w[n]), flat(m[n]), flat(v[n])) for n in big}

    def get_weights(l, part, after):
        if l == 0 and part == "in":
            full_in, full_conv = _gather_finish("gather0", ["w_in", "conv_w"], g0,
                                                list(after) + [g3["token"], adam_in["w_in"][1], adam_in["w_in"][2]])
            held["conv_w"] = jnp.transpose(full_conv.reshape(N_DEV, DEPTH, 4, 128), (1, 2, 0, 3)).reshape(DEPTH, 4, D_CONV)
            return dict(w_in=_w_in_assemble(full_in), conv_w=held["conv_w"])
        if part == "in":
            return dict(w_in=_w_in_assemble(_gather_finish("gather2", ["w_in"], g2, after)[0]), conv_w=held["conv_w"])
        full = _gather_finish("gather1" if l == 0 else "gather3", rest, g1 if l == 0 else g3, after)
        return {n: f[None] for n, f in zip(rest, full)}

    pending = []

    def send(l, grads):
        names = list(grads)
        started = _exchange_start("exchange%d_%s" % (l, names[0]), names, [grads[n] for n in names], ())
        pending.append((l, names, started))
        return (started["token"],)

    dx, small_part = _local_step(x.reshape(S, D), loss_target.reshape(S, D), w, get_weights, send)

    small = _small_exchange_start(small_part, ())
    tiles = dict(w_in=512, w_out=128, w_up=512, w_down=256)
    outs_of = {n: None for n in big}
    after = [dx, small["token"]]
    for l, names, started in pending:
        bufs = _split_wait("exchange%d_%s_wait" % (l, names[0]), started, after)
        for t, n in enumerate(names):
            outs_of[n] = _adamw_layer("adamw_%s%d" % (n, l), KIND[n], l, *adam_in[n],
                                      bufs[len(names) + t], bufs[t], my_idx, outs_of[n], tiles[n])
        after = [outs_of[names[-1]][0]]
    res = {n: [o.reshape(w[n].shape) for o in outs_of[n]] for n in big}
    small_part, small_land = _split_wait("small_exchange_wait", small, after)
    small_outs = _adamw_small(small_part, small_land, w, m, v)
    loss = small_outs[0][0, 0]
    for k, name in enumerate(SMALL_NAMES):
        res[name] = small_outs[1 + 4 * k:5 + 4 * k]

    result = [loss, dx.reshape(1, S, D)]
    for k in range(4):
        result += [res[name][k] for name in WEIGHT_ORDER]
    return tuple(result)
```

```python
import functools
import math

import numpy as np
import jax
import jax.numpy as jnp
from jax import lax
from jax.experimental import pallas as pl
from jax.experimental.pallas import tpu as pltpu

f32 = jnp.float32
bf16 = jnp.bfloat16
SDS = jax.ShapeDtypeStruct
MESH = pl.DeviceIdType.MESH
HIGHEST = lax.Precision.HIGHEST

S = 2048
D = 1024
DEPTH = 2
BLK = 128
NBLK = S // BLK
HD = 64
NQ = 8
NKV = 2
NSSM = 8
NGRP = 2
NSTATE = 128
D_ATTN = 512
D_SSM = 512
D_CONV = 1024
D_FF = 4096
D_IN = 2312
D_IN_PAD = 2560
COL_QKV, COL_Z, COL_DT, COL_XBC = 0, 768, 1280, 1536
IN_SEGMENTS = ((0, 1280, 0), (1280, 2304, COL_XBC), (2304, 2312, COL_DT))
N_BUCKETS = 32
EPS = 1e-6
N_DEV = 8
VMEM_LIMIT = 48 * 1024 * 1024

ADAM_LR = 0.001
ADAM_B1 = 0.9
ADAM_B2 = 0.999
ADAM_EPS = 1e-08
ADAM_WD = 0.01
ADAM_STEP = 10

NT_DIMS = (((1,), (1,)), ((), ()))
TN_DIMS = (((0,), (0,)), ((), ()))
NN_DIMS = (((1,), (0,)), ((), ()))

ROW_MIXG = 0
ROW_MLPG = 2
ROW_CONVB = 4
ROW_SSMG = 6
ROW_MISC = 8
ROW_RELB = 10
ROW_CONVW = 18
ROW_LOSS = 26
SMALL_ROWS = 32
LANE_QG, LANE_KG, LANE_SINK, LANE_DTB, LANE_ALOG, LANE_DSKIP = 0, 64, 128, 256, 384, 512


def _dot(a, b, dims):
    return lax.dot_general(a, b, dims, preferred_element_type=f32)


def _cparams(n_axes):
    return pltpu.CompilerParams(dimension_semantics=("arbitrary",) * n_axes, vmem_limit_bytes=VMEM_LIMIT)


def _sum11(v):
    return jnp.sum(jnp.sum(v, axis=1, keepdims=True), axis=0, keepdims=True)


def _sigmoid(v):
    return 1.0 / (1.0 + jnp.exp(-v))


ANY_SPEC = pl.BlockSpec(memory_space=pl.ANY)


def _in_hbm(args):
    return [pltpu.with_memory_space_constraint(a, pltpu.HBM) if a.size >= 65536 else a for a in args]


def _out_hbm(out_shape):
    one = lambda s: pltpu.HBM(s.shape, s.dtype) if math.prod(s.shape) >= 65536 else s
    return [one(s) for s in out_shape] if isinstance(out_shape, (list, tuple)) else one(out_shape)


def _matmul(name, mode, a, b, *, layer=0, tm, tn, tk, out_shape, out_specs, epilogue, extras=(), extra_specs=(), deps=(),
            prologue=None):
    extras = tuple(extras) + tuple(deps)
    extra_specs = tuple(extra_specs) + (ANY_SPEC,) * len(deps)
    if mode == "tn":
        t_dim, m_dim = a.shape
        n_dim = b.shape[1]
        grid = (m_dim // tm, n_dim // tn, t_dim // tk)
        a_spec = pl.BlockSpec((tk, tm), lambda i, j, k: (k, i))
        b_spec = pl.BlockSpec((tk, tn), lambda i, j, k: (k, j))
        dims = TN_DIMS
    elif mode == "nn":
        m_dim, k_dim = a.shape
        n_dim = b.shape[-1]
        grid = (m_dim // tm, n_dim // tn, k_dim // tk)
        a_spec = pl.BlockSpec((tm, tk), lambda i, j, k: (i, k))
        b_spec = pl.BlockSpec((None, tk, tn), lambda i, j, k: (layer, k, j))
        dims = NN_DIMS
    else:
        m_dim, k_dim = a.shape
        n_dim = b.shape[-2]
        grid = (m_dim // tm, n_dim // tn, k_dim // tk)
        a_spec = pl.BlockSpec((tm, tk), lambda i, j, k: (i, k))
        b_spec = pl.BlockSpec((None, tn, tk), lambda i, j, k: (layer, j, k))
        dims = NT_DIMS
    nk = grid[2]
    n_ex = len(extras)

    def body(a_ref, b_ref, *rest):
        ex = rest[:n_ex - len(deps)]
        outs = rest[n_ex:-1]
        acc = rest[-1]
        i = pl.program_id(0)
        j = pl.program_id(1)
        k = pl.program_id(2)
        lhs = a_ref[...].astype(bf16) if prologue is None else prologue(a_ref, ex, outs)
        part = _dot(lhs, b_ref[...].astype(bf16), dims)
        if nk == 1:
            epilogue(part, i, j, ex, outs)
        else:
            @pl.when(k == 0)
            def _():
                acc[...] = part

            @pl.when(k > 0)
            def _():
                acc[...] += part

            @pl.when(k == nk - 1)
            def _():
                epilogue(acc[...], i, j, ex, outs)

    return pl.pallas_call(
        body, grid=grid, in_specs=[a_spec, b_spec, *extra_specs], out_specs=out_specs, out_shape=_out_hbm(out_shape),
        scratch_shapes=[pltpu.VMEM((tm, tn) if nk > 1 else (8, 128), f32)], name=name, compiler_params=_cparams(3),
    )(*_in_hbm([a]), b, *_in_hbm(extras))


def _rms_bwd_epilogue(layer):
    def epi(acc, i, j, ex, outs):
        x_ref, g_ref, dres_ref = ex
        dx_ref, dg_ref = outs
        xv = x_ref[...]
        r = lax.rsqrt(jnp.mean(xv * xv, axis=-1, keepdims=True) + EPS)
        xhat = xv * r
        w = acc * g_ref[layer:layer + 1, :]
        dx_ref[...] = dres_ref[...] + r * (w - xhat * jnp.mean(xhat * w, axis=-1, keepdims=True))
        dg = jnp.sum(acc * xhat, axis=0, keepdims=True)

        @pl.when(i == 0)
        def _():
            dg_ref[...] = dg

        @pl.when(i > 0)
        def _():
            dg_ref[...] += dg
    return epi


def _own_slab_spec(kind, tr, cols, nblk):
    if kind == "stack":
        return pl.BlockSpec((None, tr, cols), lambda i, idx: (idx[0], i, 0))
    if kind == "cols512":
        return pl.BlockSpec((tr, cols), lambda i, idx: (i, idx[0]))
    return pl.BlockSpec((tr, cols), lambda i, idx: (idx[0] * nblk + i, 0))


def _cast_to_full(name, w, kind, full_shape, my_idx, dtype):
    n_layers, rows, cols = w.shape
    tr = min(rows, 512)
    nblk = rows // tr

    def body(idx_ref, w_ref, *o_refs):
        for l in range(n_layers):
            o_refs[l][...] = w_ref[l].astype(dtype)

    grid_spec = pltpu.PrefetchScalarGridSpec(
        num_scalar_prefetch=1, grid=(nblk,), in_specs=[pl.BlockSpec((n_layers, tr, cols), lambda i, idx: (0, i, 0))],
        out_specs=[_own_slab_spec(kind, tr, cols, nblk)] * n_layers)
    return pl.pallas_call(body, grid_spec=grid_spec, out_shape=_out_hbm([SDS(full_shape, dtype)] * n_layers), name=name,
                          compiler_params=_cparams(1))(*_in_hbm([my_idx, w]))


def _adamw_math(w, m, v, g):
    m_new = ADAM_B1 * m + (1.0 - ADAM_B1) * g
    v_new = ADAM_B2 * v + (1.0 - ADAM_B2) * (g * g)
    m_hat = m_new / (1.0 - ADAM_B1 ** ADAM_STEP)
    v_hat = v_new / (1.0 - ADAM_B2 ** ADAM_STEP)
    delta = -ADAM_LR * (m_hat / (jnp.sqrt(v_hat) + ADAM_EPS) + ADAM_WD * w)
    return delta, m_new, v_new


def _adamw_layer(name, kind, layer, w, m, v, land, g_full, my_idx, prev, tr):
    rows2, cols = w.shape
    rows = rows2 // DEPTH
    nblk = rows // tr
    own_spec = _own_slab_spec(kind, tr, cols, nblk)
    n_prev = 0 if prev is None else 4

    def body(idx_ref, w_ref, m_ref, v_ref, land_ref, own_ref, *rest):
        g_ref, d_ref, mo_ref, vo_ref = rest[n_prev:]
        me = idx_ref[0]
        g = None
        for p in range(N_DEV):
            part = jnp.where(me == p, own_ref[...], land_ref[p]).astype(f32)
            g = part if g is None else g + part
        delta, m_new, v_new = _adamw_math(w_ref[...], m_ref[...], v_ref[...], g)
        g_ref[...] = g
        d_ref[...] = delta
        mo_ref[...] = m_new
        vo_ref[...] = v_new

    blk = pl.BlockSpec((tr, cols), lambda i, idx: (layer * nblk + i, 0))
    grid_spec = pltpu.PrefetchScalarGridSpec(
        num_scalar_prefetch=1, grid=(nblk,),
        in_specs=[blk, blk, blk, pl.BlockSpec((N_DEV, tr, cols), lambda i, idx: (0, i, 0)), own_spec] + [ANY_SPEC] * n_prev,
        out_specs=[blk, blk, blk, blk])
    aliases = {} if prev is None else {6 + k: k for k in range(4)}
    return pl.pallas_call(
        body, grid_spec=grid_spec, out_shape=_out_hbm([SDS((rows2, cols), f32)] * 4), name=name, input_output_aliases=aliases,
        compiler_params=_cparams(1),
    )(*_in_hbm([my_idx, w, m, v, land, g_full, *([] if prev is None else prev)]))


def _bucket_table():
    qi = np.arange(BLK)[:, None]
    kj = np.arange(2 * BLK)[None, :]
    dist = qi + BLK - kj
    dcl = np.clip(dist, 0, None)
    max_exact = N_BUCKETS // 2
    d_f = np.maximum(dcl, 1).astype(np.float32)
    large = max_exact + (np.log(d_f / np.float32(max_exact)) / np.float32(math.log(128 / max_exact))
                         * np.float32(N_BUCKETS - max_exact)).astype(np.int32)
    large = np.minimum(large, N_BUCKETS - 1)
    bucket = np.where(dcl < max_exact, dcl, large)
    in_window = (dist >= 0) & (dist < BLK)
    return bucket.astype(np.int32), in_window


def _onehot_buckets():
    bucket, _ = _bucket_table()
    oh = (bucket.reshape(-1)[None, :] == np.arange(N_BUCKETS)[:, None]).astype(np.float32)
    return oh


def _bias_build(rel_bias_t, onehot_t):
    def body(r_ref, o_ref, out_ref):
        r = r_ref[...]
        hi = r.astype(bf16)
        r1 = r - hi.astype(f32)
        mid = r1.astype(bf16)
        lo = (r1 - mid.astype(f32)).astype(bf16)
        oh = o_ref[...]
        out_ref[...] = _dot(hi, oh, NN_DIMS) + _dot(mid, oh, NN_DIMS) + _dot(lo, oh, NN_DIMS)

    tn = 4096
    return pl.pallas_call(
        body, grid=(BLK * 2 * BLK // tn,),
        in_specs=[pl.BlockSpec((NQ, N_BUCKETS), lambda i: (0, 0)), pl.BlockSpec((N_BUCKETS, tn), lambda i: (0, i))],
        out_specs=pl.BlockSpec((NQ, tn), lambda i: (0, i)), out_shape=SDS((NQ, BLK * 2 * BLK), f32), name="bias_build",
        compiler_params=_cparams(1),
    )(rel_bias_t, onehot_t)


def _bias_grad(dbias0, dbias1, onehot_t):
    tn = 4096
    nsteps = BLK * 2 * BLK // tn

    def body(a_ref, b_ref, o_ref, out_ref):
        g = a_ref[...] + b_ref[...]
        hi = g.astype(bf16)
        lo = (g - hi.astype(f32)).astype(bf16)
        part = _dot(hi, o_ref[...], NT_DIMS) + _dot(lo, o_ref[...], NT_DIMS)

        @pl.when(pl.program_id(0) == 0)
        def _():
            out_ref[...] = part

        @pl.when(pl.program_id(0) > 0)
        def _():
            out_ref[...] += part

    return pl.pallas_call(
        body, grid=(nsteps,),
        in_specs=[pl.BlockSpec((NQ, tn), lambda i: (0, i)), pl.BlockSpec((NQ, tn), lambda i: (0, i)),
                  pl.BlockSpec((N_BUCKETS, tn), lambda i: (0, i))],
        out_specs=pl.BlockSpec((NQ, N_BUCKETS), lambda i: (0, 0)), out_shape=SDS((NQ, N_BUCKETS), f32), name="bias_grad",
        compiler_params=_cparams(1),
    )(dbias0, dbias1, onehot_t)


def _attn_mask(n):
    qi = lax.broadcasted_iota(jnp.int32, (BLK, 2 * BLK), 0)
    kj = lax.broadcasted_iota(jnp.int32, (BLK, 2 * BLK), 1)
    dist = qi + BLK - kj
    first_key = jnp.where(n > 0, 0, BLK)
    return (dist >= 0) & (dist < BLK) & (kj >= first_key)


def _row_mean(a):
    return jnp.mean(a, axis=-1, keepdims=True)


def _head_norm(t, gain):
    r = lax.rsqrt(_row_mean(t * t) + EPS)
    that = t * r
    return that, r, that * gain


def _softmax_with_sink(s, sink):
    m = jnp.maximum(jnp.max(s, axis=-1, keepdims=True), sink)
    p = jnp.exp(s - m)
    psink = jnp.exp(sink - m)
    inv = 1.0 / (jnp.sum(p, axis=-1, keepdims=True) + psink)
    return p * inv, psink * inv


GQ = NQ // NKV


def _attn_fwd(qkv, q_gain, k_gain, sinks, bias, layer):
    def body(q_ref, kc_ref, kp_ref, vc_ref, vp_ref, qg_ref, kg_ref, sk_ref, bias_ref, o_ref):
        m = pl.program_id(0)
        qg = qg_ref[layer:layer + 1, :]
        kg = kg_ref[layer:layer + 1, :]
        grp = range(NKV)
        chains = [(b, j) for b in range(2) for j in grp]
        masks = [jnp.tile(_attn_mask(2 * m + b), (GQ, 1)) for b in range(2)]
        kblk = [[kp_ref[:, pl.ds(HD * j, HD)].astype(f32), kc_ref[0:BLK, pl.ds(HD * j, HD)].astype(f32),
                 kc_ref[BLK:, pl.ds(HD * j, HD)].astype(f32)] for j in grp]
        vblk = [[vp_ref[:, pl.ds(HD * j, HD)].astype(bf16), vc_ref[0:BLK, pl.ds(HD * j, HD)].astype(bf16),
                 vc_ref[BLK:, pl.ds(HD * j, HD)].astype(bf16)] for j in grp]
        knb = [[_head_norm(kblk[j][t], kg)[2].astype(bf16) for t in range(3)] for j in grp]
        kn_b = {(b, j): jnp.concatenate([knb[j][b], knb[j][b + 1]], axis=0) for b, j in chains}
        vbs = {(b, j): jnp.concatenate([vblk[j][b], vblk[j][b + 1]], axis=0) for b, j in chains}
        rows = {}
        for b, j in chains:
            heads = [GQ * j + g for g in range(GQ)]
            rows[b, j] = (jnp.concatenate([q_ref[pl.ds(BLK * b, BLK), pl.ds(HD * h, HD)] for h in heads], axis=0).astype(f32),
                          jnp.concatenate([jnp.broadcast_to(sk_ref[layer:layer + 1, h:h + 1], (BLK, 1)) for h in heads], axis=0))
        qn_b = {c: _head_norm(rows[c][0], qg)[2].astype(bf16) for c in chains}
        ss = {(b, j): _dot(qn_b[b, j], kn_b[b, j], NT_DIMS) * (HD ** -0.5) + bias_ref[GQ * j:GQ * (j + 1)].reshape(GQ * BLK, 2 * BLK)
              for b, j in chains}
        ps = {(b, j): _softmax_with_sink(jnp.where(masks[b], ss[b, j], -jnp.inf), rows[b, j][1])[0] for b, j in chains}
        outs = {c: _dot(ps[c].astype(bf16), vbs[c], NN_DIMS).astype(bf16) for c in chains}
        for b, j in chains:
            for g in range(GQ):
                o_ref[pl.ds(BLK * b, BLK), pl.ds(HD * (GQ * j + g), HD)] = outs[b, j][BLK * g:BLK * (g + 1), :]

    prev = lambda m: jnp.maximum(2 * m - 1, 0)
    small = lambda shape: pl.BlockSpec(shape, lambda m: (0,) * len(shape))
    return pl.pallas_call(
        body, grid=(NBLK // 2,),
        in_specs=[pl.BlockSpec((2 * BLK, D_ATTN), lambda m: (m, 0)),
                  pl.BlockSpec((2 * BLK, 128), lambda m: (m, 4)), pl.BlockSpec((BLK, 128), lambda m: (prev(m), 4)),
                  pl.BlockSpec((2 * BLK, 128), lambda m: (m, 5)), pl.BlockSpec((BLK, 128), lambda m: (prev(m), 5)),
                  small((DEPTH, HD)), small((DEPTH, HD)), small((DEPTH, NQ)), small((NQ, BLK, 2 * BLK))],
        out_specs=pl.BlockSpec((2 * BLK, D_ATTN), lambda m: (m, 0)), out_shape=_out_hbm(SDS((S, D_ATTN), bf16)),
        name="attn_fwd", compiler_params=_cparams(1),
    )(*_in_hbm([qkv, qkv, qkv, qkv, qkv, q_gain, k_gain, sinks, bias]))


def _attn_bwd(qkv, dmix, q_gain, k_gain, sinks, bias, layer, deps=()):
    def body(q_ref, kc_ref, kp_ref, vc_ref, vp_ref, do_ref, qg_ref, kg_ref, sk_ref, bias_ref, *rest):
        dqkv_ref, dbias_ref, dsm_ref, carry = rest[len(deps):]
        i = pl.program_id(0)
        m = NBLK // 2 - 1 - i
        qg = qg_ref[layer:layer + 1, :]
        kg = kg_ref[layer:layer + 1, :]
        lane = lax.broadcasted_iota(jnp.int32, (1, 128), 1)

        @pl.when(i == 0)
        def _():
            carry[...] = jnp.zeros_like(carry)
            dbias_ref[...] = jnp.zeros_like(dbias_ref)
            dsm_ref[...] = jnp.zeros_like(dsm_ref)

        grp = range(NKV)
        chains = [(b, j) for b in range(2) for j in grp]
        masks = [jnp.tile(_attn_mask(2 * m + b), (GQ, 1)) for b in range(2)]
        kblk = [[kp_ref[:, pl.ds(HD * j, HD)].astype(f32), kc_ref[0:BLK, pl.ds(HD * j, HD)].astype(f32),
                 kc_ref[BLK:, pl.ds(HD * j, HD)].astype(f32)] for j in grp]
        vblk = [[vp_ref[:, pl.ds(HD * j, HD)].astype(bf16), vc_ref[0:BLK, pl.ds(HD * j, HD)].astype(bf16),
                 vc_ref[BLK:, pl.ds(HD * j, HD)].astype(bf16)] for j in grp]
        knorm = [[_head_norm(kblk[j][t], kg) for t in range(3)] for j in grp]
        kn_b = {(b, j): jnp.concatenate([knorm[j][b][2].astype(bf16), knorm[j][b + 1][2].astype(bf16)], axis=0) for b, j in chains}
        vbs = {(b, j): jnp.concatenate([vblk[j][b], vblk[j][b + 1]], axis=0) for b, j in chains}
        rows, do_b = {}, {}
        for b, j in chains:
            heads = [GQ * j + g for g in range(GQ)]
            qrows = pl.ds(BLK * b, BLK)
            rows[b, j] = (jnp.concatenate([q_ref[qrows, pl.ds(HD * h, HD)] for h in heads], axis=0).astype(f32),
                          jnp.concatenate([jnp.broadcast_to(sk_ref[layer:layer + 1, h:h + 1], (BLK, 1)) for h in heads], axis=0))
            do_b[b, j] = jnp.concatenate([do_ref[qrows, pl.ds(HD * h, HD)] for h in heads], axis=0).astype(bf16)
        qnorm = {c: _head_norm(rows[c][0], qg) for c in chains}
        qn_b = {c: qnorm[c][2].astype(bf16) for c in chains}
        ss = {(b, j): _dot(qn_b[b, j], kn_b[b, j], NT_DIMS) * (HD ** -0.5) + bias_ref[GQ * j:GQ * (j + 1)].reshape(GQ * BLK, 2 * BLK)
              for b, j in chains}
        sm = {(b, j): _softmax_with_sink(jnp.where(masks[b], ss[b, j], -jnp.inf), rows[b, j][1]) for b, j in chains}
        dps = {c: _dot(do_b[c], vbs[c], NT_DIMS) for c in chains}
        deltas = {c: jnp.sum(sm[c][0] * dps[c], axis=-1, keepdims=True) for c in chains}
        dss = {c: sm[c][0] * (dps[c] - deltas[c]) for c in chains}
        ds_b = {c: (dss[c] * (HD ** -0.5)).astype(bf16) for c in chains}
        dqn = {c: _dot(ds_b[c], kn_b[c], NN_DIMS) for c in chains}
        dkn = {c: _dot(ds_b[c], qn_b[c], TN_DIMS) for c in chains}
        dvs = {c: _dot(sm[c][0].astype(bf16), do_b[c], TN_DIMS) for c in chains}
        dqg = jnp.zeros((1, HD), f32)
        dkg = jnp.zeros((1, HD), f32)
        dsink = jnp.zeros((1, 128), f32)
        for b, j in chains:
            dbias_ref[GQ * j:GQ * (j + 1)] += dss[b, j].reshape(GQ, BLK, 2 * BLK)
            dsk = sm[b, j][1] * deltas[b, j]
            for g in range(GQ):
                dsink = dsink + jnp.where(lane == GQ * j + g, -_sum11(dsk[BLK * g:BLK * (g + 1), :]), 0.0)
            qhat, rq, _ = qnorm[b, j]
            w = dqn[b, j] * qg
            dq = rq * (w - qhat * _row_mean(qhat * w))
            for g in range(GQ):
                dqkv_ref[pl.ds(BLK * b, BLK), pl.ds(HD * (GQ * j + g), HD)] = dq[BLK * g:BLK * (g + 1), :].astype(bf16)
            dqg = dqg + jnp.sum(dqn[b, j] * qhat, axis=0, keepdims=True)
        for j in grp:
            dkn_t = [dkn[0, j][:BLK, :], dkn[0, j][BLK:, :] + dkn[1, j][:BLK, :], dkn[1, j][BLK:, :]]
            dv_t = [dvs[0, j][:BLK, :], dvs[0, j][BLK:, :] + dvs[1, j][:BLK, :], dvs[1, j][BLK:, :]]
            dk_t = []
            for t in range(3):
                khat, rk, _ = knorm[j][t]
                w = dkn_t[t] * kg
                dk_t.append(rk * (w - khat * _row_mean(khat * w)))
                dkg = dkg + jnp.sum(dkn_t[t] * khat, axis=0, keepdims=True)
            kcols, vcols = pl.ds(D_ATTN + HD * j, HD), pl.ds(D_ATTN + 128 + HD * j, HD)
            dqkv_ref[BLK:, kcols] = (dk_t[2] + carry[:, pl.ds(HD * j, HD)]).astype(bf16)
            dqkv_ref[BLK:, vcols] = (dv_t[2] + carry[:, pl.ds(128 + HD * j, HD)]).astype(bf16)
            dqkv_ref[0:BLK, kcols] = dk_t[1].astype(bf16)
            dqkv_ref[0:BLK, vcols] = dv_t[1].astype(bf16)
            carry[:, pl.ds(HD * j, HD)] = dk_t[0]
            carry[:, pl.ds(128 + HD * j, HD)] = dv_t[0]
        dsm_ref[0:1, 0:HD] += dqg
        dsm_ref[1:2, 0:HD] += dkg
        dsm_ref[2:3, :] += dsink

    rev = lambda i: NBLK // 2 - 1 - i
    prev = lambda i: jnp.maximum(NBLK - 3 - 2 * i, 0)
    small = lambda shape: pl.BlockSpec(shape, lambda i: (0,) * len(shape))
    return pl.pallas_call(
        body, grid=(NBLK // 2,),
        in_specs=[pl.BlockSpec((2 * BLK, D_ATTN), lambda i: (rev(i), 0)),
                  pl.BlockSpec((2 * BLK, 128), lambda i: (rev(i), 4)), pl.BlockSpec((BLK, 128), lambda i: (prev(i), 4)),
                  pl.BlockSpec((2 * BLK, 128), lambda i: (rev(i), 5)), pl.BlockSpec((BLK, 128), lambda i: (prev(i), 5)),
                  pl.BlockSpec((2 * BLK, D_ATTN), lambda i: (rev(i), 0)),
                  small((DEPTH, HD)), small((DEPTH, HD)), small((DEPTH, NQ)), small((NQ, BLK, 2 * BLK))] + [ANY_SPEC] * len(deps),
        out_specs=[pl.BlockSpec((2 * BLK, 768), lambda i: (rev(i), COL_QKV // 768)), small((NQ, BLK, 2 * BLK)), small((8, 128))],
        out_shape=_out_hbm([SDS((S, D_IN_PAD), bf16), SDS((NQ, BLK, 2 * BLK), f32), SDS((8, 128), f32)]),
        scratch_shapes=[pltpu.VMEM((BLK, 256), f32)], name="attn_bwd", compiler_params=_cparams(1),
    )(*_in_hbm([qkv, qkv, qkv, qkv, qkv, dmix, q_gain, k_gain, sinks, bias, *deps]))


CONV_TC = 256


def _shift_down(u, s):
    if s == 0:
        return u
    rows = lax.broadcasted_iota(jnp.int32, u.shape, 0)
    return jnp.where(rows >= s, pltpu.roll(u, s, 0), 0.0)


def _shift_up(u, s):
    if s == 0:
        return u
    rows = lax.broadcasted_iota(jnp.int32, u.shape, 0)
    return jnp.where(rows < u.shape[0] - s, pltpu.roll(u, u.shape[0] - s, 0), 0.0)


def _conv_specs():
    return [pl.BlockSpec((S, CONV_TC), lambda c: (0, c)),
            pl.BlockSpec((None, 4, CONV_TC), lambda c: (0, 0, c)),
            pl.BlockSpec((DEPTH, CONV_TC), lambda c: (0, c))]


def _conv_pre(u, w_ref, b_ref, layer):
    pre = b_ref[layer:layer + 1, :] + w_ref[3:4, :] * u
    for k in range(3):
        pre = pre + w_ref[k:k + 1, :] * _shift_down(u, 3 - k)
    return pre


def _conv_fwd(xbc, conv_w, conv_b, layer):
    def body(u_ref, w_ref, b_ref, o_ref):
        pre = _conv_pre(u_ref[...].astype(f32), w_ref, b_ref, layer)
        o_ref[...] = pre * _sigmoid(pre)

    specs = _conv_specs()
    specs[1] = pl.BlockSpec((None, 4, CONV_TC), lambda c: (layer, 0, c))
    return pl.pallas_call(
        body, grid=(D_CONV // CONV_TC,), in_specs=specs, out_specs=pl.BlockSpec((S, CONV_TC), lambda c: (0, c)),
        out_shape=_out_hbm(SDS((S, D_CONV), f32)), name="conv_fwd", compiler_params=_cparams(1),
    )(*_in_hbm([xbc, conv_w, conv_b]))


def _conv_bwd(xbc, dact, conv_w, conv_b, dproj, layer):
    def body(u_ref, w_ref, b_ref, da_ref, dproj_in, du_ref, dw_ref, db_ref):
        u = u_ref[...].astype(f32)
        pre = _conv_pre(u, w_ref, b_ref, layer)
        sg = _sigmoid(pre)
        dpre = da_ref[...] * (sg * (1.0 + pre * (1.0 - sg)))
        du = w_ref[3:4, :] * dpre
        for k in range(3):
            du = du + w_ref[k:k + 1, :] * _shift_up(dpre, 3 - k)
        du_ref[...] = du.astype(bf16)
        db_ref[...] = jnp.broadcast_to(jnp.sum(dpre, axis=0, keepdims=True), db_ref.shape)
        dw_ref[...] = jnp.zeros_like(dw_ref)
        for k in range(4):
            dw_ref[k:k + 1, :] = jnp.sum(dpre * _shift_down(u, 3 - k), axis=0, keepdims=True)

    specs = _conv_specs()
    specs[1] = pl.BlockSpec((None, 4, CONV_TC), lambda c: (layer, 0, c))
    col = pl.BlockSpec((S, CONV_TC), lambda c: (0, c))
    row8 = pl.BlockSpec((8, CONV_TC), lambda c: (0, c))
    return pl.pallas_call(
        body, grid=(D_CONV // CONV_TC,), in_specs=[*specs, col, ANY_SPEC],
        out_specs=[pl.BlockSpec((S, CONV_TC), lambda c: (0, COL_XBC // CONV_TC + c)), row8, row8],
        out_shape=_out_hbm([SDS((S, D_IN_PAD), bf16), SDS((8, D_CONV), f32), SDS((8, D_CONV), f32)]), name="conv_bwd",
        input_output_aliases={4: 0}, compiler_params=_cparams(1),
    )(*_in_hbm([xbc, conv_w, conv_b, dact, dproj]))


def _tri():
    return (lax.broadcasted_iota(jnp.int32, (BLK, BLK), 0) >= lax.broadcasted_iota(jnp.int32, (BLK, BLK), 1))


def _ssd_scalars(dt_ref, dtb_ref, alog_ref, layer):
    raw = dt_ref[:, 0:NSSM] + dtb_ref[layer:layer + 1, :]
    dtv = jnp.maximum(raw, 0.0) + jnp.log(1.0 + jnp.exp(-jnp.abs(raw)))
    a = -jnp.exp(alog_ref[layer:layer + 1, :])
    acs = jnp.dot(_tri().astype(f32), dtv * a, preferred_element_type=f32, precision=HIGHEST)
    return raw, dtv, a, acs


HG = NSSM // NGRP
GW = HG * HD


def _lane_expand(cols, g):
    lane_head = lax.broadcasted_iota(jnp.int32, (1, GW), 1) // HD
    out = cols[:, HG * g + HG - 1:HG * g + HG]
    for r in range(HG - 2, -1, -1):
        out = jnp.where(lane_head == r, cols[:, HG * g + r:HG * g + r + 1], out)
    return out


def _row_expand(vals, g):
    row_head = lax.broadcasted_iota(jnp.int32, (GW, 1), 0) // HD
    out = vals[:, HG * g + HG - 1:HG * g + HG]
    for r in range(HG - 2, -1, -1):
        out = jnp.where(row_head == r, vals[:, HG * g + r:HG * g + r + 1], out)
    return out


def _head_rowsums(a, g):
    sel = (lax.broadcasted_iota(jnp.int32, (GW, NSSM), 0) // HD + HG * g == lax.broadcasted_iota(jnp.int32, (GW, NSSM), 1)).astype(bf16)
    hi = a.astype(bf16)
    lo = (a - hi.astype(f32)).astype(bf16)
    return _dot(hi, sel, NN_DIMS) + _dot(lo, sel, NN_DIMS)


def _head_blocksums(v, g):
    sel = (lax.broadcasted_iota(jnp.int32, (GW, NSSM), 0) // HD + HG * g == lax.broadcasted_iota(jnp.int32, (GW, NSSM), 1)).astype(bf16)
    hi = v.astype(bf16)
    lo = (v - hi.astype(f32)).astype(bf16)
    return _dot(hi, sel, TN_DIMS) + _dot(lo, sel, TN_DIMS)


def _ssd_chunk_common(xc_ref, dt_ref, dtb_ref, alog_ref, h_rows, layer):
    raw, dtv, a, acs = _ssd_scalars(dt_ref, dtb_ref, alog_ref, layer)
    acs_t = acs.T
    last = acs[BLK - 1:BLK, :]
    c = dict(raw=raw, dtv=dtv, a=a, acs=acs, last=last, dte=jnp.exp(last - acs), e_all=jnp.exp(acs), cd=jnp.exp(last))
    grp, heads, tri = range(NGRP), range(NSSM), _tri()
    c["bm"] = [xc_ref[:, pl.ds(D_SSM + NSTATE * g, NSTATE)] for g in grp]
    c["bm_b"] = [c["bm"][g].astype(bf16) for g in grp]
    c["cm_b"] = [xc_ref[:, pl.ds(D_SSM + NGRP * NSTATE + NSTATE * g, NSTATE)].astype(bf16) for g in grp]
    c["cb"] = [_dot(c["cm_b"][g], c["bm_b"][g], NT_DIMS) for g in grp]
    c["x"] = [xc_ref[:, pl.ds(GW * g, GW)] for g in grp]
    c["dt"] = [_lane_expand(dtv, g) for g in grp]
    c["xdt"] = [c["x"][g] * c["dt"][g] for g in grp]
    c["xdt_b"] = [c["xdt"][g].astype(bf16) for g in grp]
    c["prev"] = [h_rows(g) for g in grp]
    c["prev_b"] = [c["prev"][g].astype(bf16) for g in grp]
    c["e"] = [_lane_expand(c["e_all"], g) for g in grp]
    c["y_off"] = [_dot(c["cm_b"][g], c["prev_b"][g], NT_DIMS) * c["e"][g] for g in grp]
    c["decay"] = [jnp.exp(jnp.where(tri, acs[:, h:h + 1] - acs_t[h:h + 1, :], -jnp.inf)) for h in heads]
    c["m"] = [c["cb"][h // HG] * c["decay"][h] for h in heads]
    c["m_b"] = [c["m"][h].astype(bf16) for h in heads]
    c["dte_x"] = [_lane_expand(c["dte"], g) for g in grp]
    c["xdte_b"] = [(c["xdt"][g] * c["dte_x"][g]).astype(bf16) for g in grp]
    return c


def _ssd_fwd(xact, z, dt, attn, dt_bias, a_log, d_skip, norm_g, layer):
    def body(xc_ref, z_ref, dt_ref, at_ref, dtb_ref, alog_ref, dsk_ref, ng_ref, mix_ref, hs_ref, y_ref, h_ref):
        n = pl.program_id(0)

        @pl.when(n == 0)
        def _():
            h_ref[...] = jnp.zeros_like(h_ref)

        hs_ref[...] = h_ref[...]
        c = _ssd_chunk_common(xc_ref, dt_ref, dtb_ref, alog_ref, lambda g: h_ref[pl.ds(GW * g, GW), :], layer)
        grp, heads = range(NGRP), range(NSSM)
        y_diag = [_dot(c["m_b"][h], c["xdt_b"][h // HG][:, HD * (h % HG):HD * (h % HG + 1)], NN_DIMS) for h in heads]
        new_st = [_dot(c["xdte_b"][g], c["bm_b"][g], TN_DIMS) for g in grp]
        for h in heads:
            y_ref[:, pl.ds(HD * h, HD)] = y_diag[h]
        dskip = dsk_ref[layer:layer + 1, :]
        for g in grp:
            cols = pl.ds(GW * g, GW)
            y_ref[:, cols] = y_ref[:, cols] + c["y_off"][g] + c["x"][g] * _lane_expand(dskip, g)
            h_ref[cols, :] = c["prev"][g] * _row_expand(c["cd"], g) + new_st[g]
        zv = z_ref[...].astype(f32)
        yz = y_ref[...] * (zv * _sigmoid(zv))
        mix_ref[:, 0:D_ATTN] = at_ref[...]
        for g in grp:
            yg = yz[:, GW * g:GW * (g + 1)]
            rs = lax.rsqrt(jnp.mean(yg * yg, axis=-1, keepdims=True) + EPS)
            mix_ref[:, D_ATTN + GW * g:D_ATTN + GW * (g + 1)] = (yg * rs * ng_ref[layer:layer + 1, GW * g:GW * (g + 1)]).astype(bf16)

    small = lambda shape: pl.BlockSpec(shape, lambda n: (0,) * len(shape))
    return pl.pallas_call(
        body, grid=(NBLK,),
        in_specs=[pl.BlockSpec((BLK, D_CONV), lambda n: (n, 0)), pl.BlockSpec((BLK, D_SSM), lambda n: (n, 0)),
                  pl.BlockSpec((BLK, 128), lambda n: (n, 0)), pl.BlockSpec((BLK, D_ATTN), lambda n: (n, 0)),
                  small((DEPTH, NSSM)), small((DEPTH, NSSM)), small((DEPTH, NSSM)), small((DEPTH, D_SSM))],
        out_specs=[pl.BlockSpec((BLK, D), lambda n: (n, 0)), pl.BlockSpec((None, NSSM * HD, NSTATE), lambda n: (n, 0, 0)),
                   pl.BlockSpec((BLK, D_SSM), lambda n: (n, 0))],
        out_shape=_out_hbm([SDS((S, D), bf16), SDS((NBLK, NSSM * HD, NSTATE), f32), SDS((S, D_SSM), f32)]),
        scratch_shapes=[pltpu.VMEM((NSSM * HD, NSTATE), f32)],
        name="ssd_fwd", compiler_params=_cparams(1),
    )(*_in_hbm([xact, z, dt, attn, dt_bias, a_log, d_skip, norm_g]))


def _ssd_bwd(xact, z, dt, dmix, hs, y, dt_bias, a_log, d_skip, norm_g, dproj, layer):
    def body(xc_ref, z_ref, dt_ref, do_ref, hs_ref, y_ref, dtb_ref, alog_ref, dsk_ref, ng_ref, dproj_in,
             dzdt_ref, dx_ref, dsm_ref, dh_ref, dy_ref):
        i = pl.program_id(0)

        @pl.when(i == 0)
        def _():
            dh_ref[...] = jnp.zeros_like(dh_ref)
            dsm_ref[...] = jnp.zeros_like(dsm_ref)

        c = _ssd_chunk_common(xc_ref, dt_ref, dtb_ref, alog_ref, lambda g: hs_ref[pl.ds(GW * g, GW), :], layer)
        raw, dtv, a = c["raw"], c["dtv"], c["a"]
        grp, heads = range(NGRP), range(NSSM)
        dskip = dsk_ref[layer:layer + 1, :]
        lane8 = lax.broadcasted_iota(jnp.int32, (1, NSSM), 1)
        sub8 = lax.broadcasted_iota(jnp.int32, (NSSM, 1), 0)

        zv = z_ref[...].astype(f32)
        sz = _sigmoid(zv)
        gz = zv * sz
        yv = y_ref[...]
        yz = yv * gz
        for g in grp:
            sl = slice(GW * g, GW * (g + 1))
            yg = yz[:, sl]
            rs = lax.rsqrt(jnp.mean(yg * yg, axis=-1, keepdims=True) + EPS)
            yhat = yg * rs
            dog = do_ref[:, sl]
            w = dog * ng_ref[layer:layer + 1, sl]
            dyz = rs * (w - yhat * jnp.mean(yhat * w, axis=-1, keepdims=True))
            dsm_ref[0:1, sl] += jnp.sum(dog * yhat, axis=0, keepdims=True)
            dy_ref[:, sl] = dyz * gz[:, sl]
            dzdt_ref[:, sl] = (dyz * yv[:, sl] * (sz[:, sl] * (1.0 + zv[:, sl] * (1.0 - sz[:, sl])))).astype(bf16)

        dy = [dy_ref[:, pl.ds(GW * g, GW)] for g in grp]
        dy_b = [dy[g].astype(bf16) for g in grp]
        hl = lambda h: slice(HD * (h % HG), HD * (h % HG + 1))
        dt_off_b = [(dy[g] * c["e"][g]).astype(bf16) for g in grp]
        dcm = [_dot(dt_off_b[g], c["prev_b"][g], NN_DIMS) for g in grp]
        dprev = [_dot(dt_off_b[g], c["cm_b"][g], TN_DIMS) for g in grp]
        yoff_rs = [_head_rowsums(dy[g] * c["y_off"][g], g) for g in grp]
        dhn = [dh_ref[pl.ds(GW * g, GW), :] for g in grp]
        dhn_b = [dhn[g].astype(bf16) for g in grp]
        dprev = [dprev[g] + dhn[g] * _row_expand(c["cd"], g) for g in grp]
        dhn_prev = [dhn[g] * c["prev"][g] for g in grp]
        u = [_dot(c["bm_b"][g], dhn_b[g], NT_DIMS) for g in grp]
        dbm = [_dot(c["xdte_b"][g], dhn_b[g], NN_DIMS) for g in grp]
        ddte_rs = [_head_rowsums(c["xdt"][g] * u[g], g) for g in grp]
        dm = [_dot(dy_b[h // HG][:, hl(h)], c["xdt_b"][h // HG][:, hl(h)], NT_DIMS) for h in heads]
        dxdt_in = [_dot(c["m_b"][h], dy_b[h // HG][:, hl(h)], TN_DIMS) for h in heads]
        dseg = [dm[h] * c["m"][h] for h in heads]
        dmd = [dm[h] * c["decay"][h] for h in heads]
        for h in heads:
            dx_ref[:, pl.ds(HD * h, HD)] = dxdt_in[h]

        tmp = (ddte_rs[0] + ddte_rs[1]) * c["dte"]
        dacs = yoff_rs[0] + yoff_rs[1] - tmp
        dacs_cols = jnp.zeros((NSSM, BLK), f32)
        ddtv = jnp.zeros((BLK, NSSM), f32)
        ddsk = jnp.zeros((BLK, NSSM), f32)
        hp = jnp.zeros((1, NSSM), f32)
        for g in grp:
            cols = pl.ds(GW * g, GW)
            dxdt = dx_ref[:, cols] + u[g] * c["dte_x"][g]
            dx_ref[:, cols] = dy[g] * _lane_expand(dskip, g) + dxdt * c["dt"][g]
            ddtv = ddtv + _head_rowsums(dxdt * c["x"][g], g)
            ddsk = ddsk + _head_rowsums(dy[g] * c["x"][g], g)
            dcb = dmd[HG * g]
            for r in range(1, HG):
                dcb = dcb + dmd[HG * g + r]
            dcb_b = dcb.astype(bf16)
            dx_ref[:, pl.ds(D_SSM + NSTATE * g, NSTATE)] = dbm[g] + _dot(dcb_b, c["cm_b"][g], TN_DIMS)
            dx_ref[:, pl.ds(D_SSM + NGRP * NSTATE + NSTATE * g, NSTATE)] = dcm[g] + _dot(dcb_b, c["bm_b"][g], NN_DIMS)
            dh_ref[cols, :] = dprev[g]
            hp = hp + _head_blocksums(jnp.sum(dhn_prev[g], axis=1, keepdims=True), g)
            for r in range(HG):
                h = HG * g + r
                dacs = dacs + (lane8 == h).astype(f32) * jnp.sum(dseg[h], axis=1, keepdims=True)
                dacs_cols = dacs_cols + (sub8 == h).astype(f32) * jnp.sum(dseg[h], axis=0, keepdims=True)
        dlast = hp * c["cd"] + jnp.sum(tmp, axis=0, keepdims=True)
        ddsk = jnp.sum(ddsk, axis=0, keepdims=True)

        row = lax.broadcasted_iota(jnp.int32, (BLK, 1), 0)
        dacs = dacs - dacs_cols.T + jnp.where(row == BLK - 1, dlast, 0.0)
        dda = lax.dot_general(_tri().astype(f32), dacs, TN_DIMS, preferred_element_type=f32, precision=HIGHEST)
        ddtv = ddtv + dda * a
        da = jnp.sum(dda * dtv, axis=0, keepdims=True)
        draw = ddtv * _sigmoid(raw)
        dzdt_ref[:, D_SSM:] = jnp.zeros((BLK, COL_XBC - COL_DT), bf16)
        dzdt_ref[:, D_SSM:D_SSM + NSSM] = draw.astype(bf16)
        dsm_ref[1:2, 0:NSSM] += jnp.sum(draw, axis=0, keepdims=True)
        dsm_ref[2:3, 0:NSSM] += da * a
        dsm_ref[3:4, 0:NSSM] += ddsk

    rev = lambda i: NBLK - 1 - i
    small = lambda shape: pl.BlockSpec(shape, lambda i: (0,) * len(shape))
    return pl.pallas_call(
        body, grid=(NBLK,),
        in_specs=[pl.BlockSpec((BLK, D_CONV), lambda i: (rev(i), 0)), pl.BlockSpec((BLK, D_SSM), lambda i: (rev(i), 0)),
                  pl.BlockSpec((BLK, 128), lambda i: (rev(i), 0)), pl.BlockSpec((BLK, D_SSM), lambda i: (rev(i), 1)),
                  pl.BlockSpec((None, NSSM * HD, NSTATE), lambda i: (rev(i), 0, 0)), pl.BlockSpec((BLK, D_SSM), lambda i: (rev(i), 0)),
                  small((DEPTH, NSSM)), small((DEPTH, NSSM)), small((DEPTH, NSSM)), small((DEPTH, D_SSM)), ANY_SPEC],
        out_specs=[pl.BlockSpec((BLK, COL_XBC - COL_Z), lambda i: (rev(i), COL_Z // (COL_XBC - COL_Z))),
                   pl.BlockSpec((BLK, D_CONV), lambda i: (rev(i), 0)), small((8, D_SSM))],
        out_shape=_out_hbm([SDS((S, D_IN_PAD), bf16), SDS((S, D_CONV), f32), SDS((8, D_SSM), f32)]),
        scratch_shapes=[pltpu.VMEM((NSSM * HD, NSTATE), f32), pltpu.VMEM((BLK, D_SSM), f32)],
        name="ssd_bwd", input_output_aliases={10: 0}, compiler_params=_cparams(1),
    )(*_in_hbm([xact, z, dt, dmix, hs, y, dt_bias, a_log, d_skip, norm_g, dproj]))


def _my_place():
    return lax.axis_index("x"), lax.axis_index("y"), lax.axis_index("c")


def _dev_index(px, py, pc):
    return 4 * px + 2 * py + pc


def _slab2(kind, ref, idx):
    if kind == "stack":
        return ref.at[idx]
    if kind == "rows128":
        return ref.at[pl.ds(pl.multiple_of(idx * 128, 128), 128), :]
    if kind == "rows512":
        return ref.at[pl.ds(pl.multiple_of(idx * 512, 512), 512), :]
    return ref.at[:, pl.ds(pl.multiple_of(idx * 512, 512), 512)]


def _slab_shape(kind, full_shape):
    if kind == "stack":
        return tuple(full_shape[1:])
    if kind == "rows128":
        return (128, full_shape[1])
    if kind == "rows512":
        return (512, full_shape[1])
    return (full_shape[0], 512)


KIND = dict(w_in="stack", w_out="rows128", w_up="cols512", w_down="rows512", conv_w="stack")
FULL_SHAPE = dict(w_in=(N_DEV, D, D_IN // N_DEV), w_out=(D, D), w_up=(D, D_FF), w_down=(D_FF, D))
HBM_SPEC = pl.BlockSpec(memory_space=pltpu.HBM)
SEM_SPEC = pl.BlockSpec(memory_space=pltpu.SEMAPHORE)
SIDE_EFFECT = pltpu.SideEffectType.DATAFLOW_SIDE_EFFECTING


def _peers_all():
    x, y, c = _my_place()
    return [(x ^ ((r >> 2) & 1), y ^ ((r >> 1) & 1), c ^ (r & 1)) for r in range(1, N_DEV)]


def _split_start(name, bufs, n_copies, plan, deps=()):
    nb = len(bufs)

    def body(*refs):
        ins = refs[:nb]
        send_sems, recv_sems = refs[nb + len(deps)], refs[nb + len(deps) + 1]
        token = refs[-1]
        for i, (src, dst, dev) in enumerate(plan(ins)):
            pltpu.make_async_remote_copy(src_ref=src, dst_ref=dst, send_sem=send_sems.at[i], recv_sem=recv_sems.at[i],
                                         device_id=dev, device_id_type=MESH).start()
        token[...] = jnp.zeros_like(token)

    outs = pl.pallas_call(
        body, name=name,
        out_shape=(pltpu.SemaphoreType.DMA((n_copies,)), pltpu.SemaphoreType.DMA((n_copies,)),
                   *[pltpu.HBM(b.shape, b.dtype) for b in bufs], SDS((8, 128), f32)),
        in_specs=[HBM_SPEC] * nb + [ANY_SPEC] * len(deps),
        out_specs=(SEM_SPEC, SEM_SPEC, *[HBM_SPEC] * nb, pl.BlockSpec(memory_space=pltpu.VMEM)),
        input_output_aliases={i: 2 + i for i in range(nb)},
        compiler_params=pltpu.CompilerParams(has_side_effects=SIDE_EFFECT),
    )(*[pltpu.with_memory_space_constraint(b, pltpu.HBM) for b in bufs], *deps)
    return dict(send=outs[0], recv=outs[1], bufs=list(outs[2:2 + nb]), token=outs[-1], plan=plan, n=n_copies)


def _split_wait(name, started, after):
    bufs = started["bufs"]
    nb = len(bufs)
    plan = started["plan"]

    def body(*refs):
        ins = refs[:nb]
        send_sems, recv_sems = refs[nb], refs[nb + 1]
        for i, (src, dst, dev) in enumerate(plan(ins)):
            cp = pltpu.make_async_remote_copy(src_ref=src, dst_ref=dst, send_sem=send_sems.at[i], recv_sem=recv_sems.at[i],
                                              device_id=dev, device_id_type=MESH)
            cp.wait_send()
            cp.wait_recv()

    outs = pl.pallas_call(
        body, name=name, out_shape=tuple(pltpu.HBM(b.shape, b.dtype) for b in bufs),
        in_specs=[HBM_SPEC] * nb + [SEM_SPEC, SEM_SPEC] + [ANY_SPEC] * len(after), out_specs=(HBM_SPEC,) * nb,
        input_output_aliases={i: i for i in range(nb)},
        compiler_params=pltpu.CompilerParams(has_side_effects=SIDE_EFFECT),
    )(*bufs, started["send"], started["recv"], *after)
    return list(outs)


def _gather_start(name, names, fulls, deps):
    n_t = len(names)

    def plan(refs):
        x, y, c = _my_place()
        my_idx = _dev_index(x, y, c)
        targets = [(x, y, 1 - c), (1 - x, y, c), (x, 1 - y, c), (1 - x, 1 - y, c)]
        slabs = [_slab2(KIND[names[t]], refs[t], my_idx) for t in range(n_t)]
        return [(slabs[t], slabs[t], dev) for t in range(n_t) for dev in targets]

    return _split_start(name, list(fulls), 4 * n_t, plan, deps)


def _gather_finish(name, names, started, after):
    n_t = len(names)
    fulls = _split_wait(name + "_wait", started, after)
    slab_shapes = [SDS(_slab_shape(KIND[n], f.shape), f.dtype) for n, f in zip(names, fulls)]

    def body(*refs):
        ins = refs[:n_t]
        outs = refs[n_t:2 * n_t]
        stage = refs[2 * n_t:3 * n_t]
        load_sems, send_sems, recv_sems = refs[3 * n_t:]
        x, y, c = _my_place()
        chips = [(1 - x, y), (x, 1 - y), (1 - x, 1 - y)]
        pairs = [(t, j) for t in range(n_t) for j in range(3)]
        loads = [pltpu.make_async_copy(_slab2(KIND[names[t]], ins[t], _dev_index(*chips[j], c)), stage[t].at[j], load_sems.at[t, j])
                 for t, j in pairs]
        for cp in loads:
            cp.start()

        def copy(t, j, core):
            return pltpu.make_async_remote_copy(
                src_ref=stage[t].at[j], dst_ref=_slab2(KIND[names[t]], outs[t], _dev_index(*chips[j], core)),
                send_sem=send_sems.at[t, j], recv_sem=recv_sems.at[t, j], device_id=(x, y, 1 - c), device_id_type=MESH)

        sends = [copy(t, j, c) for t, j in pairs]
        for ld, cp in zip(loads, sends):
            ld.wait()
            cp.start()
        for t, j in pairs:
            copy(t, j, 1 - c).wait_recv()
        for cp in sends:
            cp.wait_send()

    return pl.pallas_call(
        body, in_specs=[ANY_SPEC] * n_t, out_specs=[ANY_SPEC] * n_t, out_shape=[SDS(b.shape, b.dtype) for b in fulls],
        input_output_aliases={t: t for t in range(n_t)},
        scratch_shapes=[pltpu.VMEM((3,) + s.shape, s.dtype) for s in slab_shapes]
        + [pltpu.SemaphoreType.DMA((n_t, 3)), pltpu.SemaphoreType.DMA((n_t, 3)), pltpu.SemaphoreType.DMA((n_t, 3))],
        name=name + "_pass", compiler_params=pltpu.CompilerParams(vmem_limit_bytes=VMEM_LIMIT),
    )(*fulls)


def _exchange_start(name, names, grads, deps):
    n_t = len(names)
    lands = [lax.empty((N_DEV,) + _slab_shape(KIND[n], g.shape), g.dtype) for n, g in zip(names, grads)]

    def plan(refs):
        my_idx = _dev_index(*_my_place())
        return [(_slab2(KIND[names[t]], refs[t], _dev_index(*peer)), refs[n_t + t].at[my_idx], peer)
                for t in range(n_t) for peer in _peers_all()]

    return _split_start(name, list(grads) + lands, 7 * n_t, plan, deps)


def _small_exchange_start(part, deps):
    land = lax.empty((N_DEV,) + part.shape, part.dtype)

    def plan(refs):
        my_idx = _dev_index(*_my_place())
        return [(refs[0], refs[1].at[my_idx], peer) for peer in _peers_all()]

    return _split_start("small_exchange", [part, land], N_DEV - 1, plan, deps)


def _slab_pieces():
    sh = D_IN // N_DEV
    out = []
    for j in range(N_DEV):
        for first, end, dst in IN_SEGMENTS:
            lo, hi = max(first, sh * j), min(end, sh * (j + 1))
            if lo < hi:
                out.append((j, lo - sh * j, hi - sh * j, dst + lo - first))
    return out


def _w_in_assemble(stacked):
    tr = 512
    sh = D_IN // N_DEV

    def body(i_ref, o_ref):
        o_ref[:, COL_DT:COL_XBC] = jnp.zeros((tr, COL_XBC - COL_DT), bf16)
        for j, lo, hi, dst in _slab_pieces():
            o_ref[:, dst:dst + hi - lo] = i_ref[j, :, lo:hi]

    return pl.pallas_call(
        body, grid=(D // tr,), in_specs=[pl.BlockSpec((N_DEV, tr, sh), lambda i: (0, i, 0))],
        out_specs=pl.BlockSpec((None, tr, D_IN_PAD), lambda i: (0, i, 0)), out_shape=_out_hbm(SDS((1, D, D_IN_PAD), bf16)),
        name="w_in_assemble", compiler_params=_cparams(1),
    )(*_in_hbm([stacked]))


def _w_in_slabs(dw_in):
    tr = 512
    sh = D_IN // N_DEV

    def body(i_ref, o_ref):
        for j, lo, hi, src in _slab_pieces():
            o_ref[j, :, lo:hi] = i_ref[:, src:src + hi - lo]

    return pl.pallas_call(
        body, grid=(D // tr,), in_specs=[pl.BlockSpec((tr, D_IN_PAD), lambda i: (i, 0))],
        out_specs=pl.BlockSpec((N_DEV, tr, sh), lambda i: (0, i, 0)), out_shape=_out_hbm(SDS((N_DEV, D, sh), bf16)),
        name="w_in_slabs", compiler_params=_cparams(1),
    )(*_in_hbm([dw_in]))


SMALL_NAMES = ("mix_norm_g", "mlp_norm_g", "conv_b", "ssm_norm_g", "q_gain", "k_gain", "sinks", "dt_bias", "a_log", "d_skip",
               "rel_bias", "conv_w")
MISC_LANES = dict(q_gain=(LANE_QG, HD), k_gain=(LANE_KG, HD), sinks=(LANE_SINK, NQ), dt_bias=(LANE_DTB, NSSM),
                  a_log=(LANE_ALOG, NSSM), d_skip=(LANE_DSKIP, NSSM))


def _pack_small_grads(smalls, drel_t, loss):
    def body(*refs):
        o_ref = refs[-1]
        drel_ref, loss_ref = refs[-3], refs[-2]
        o_ref[...] = jnp.zeros_like(o_ref)
        for l in range(DEPTH):
            mixg, mlpg, convb, convw, ssd, attn = refs[6 * l:6 * l + 6]
            o_ref[ROW_MIXG + l:ROW_MIXG + l + 1, :] = mixg[...]
            o_ref[ROW_MLPG + l:ROW_MLPG + l + 1, :] = mlpg[...]
            o_ref[ROW_CONVB + l:ROW_CONVB + l + 1, :] = convb[0:1, :]
            o_ref[ROW_SSMG + l:ROW_SSMG + l + 1, 0:D_SSM] = ssd[0:1, :]
            o_ref[ROW_CONVW + 4 * l:ROW_CONVW + 4 * l + 4, :] = convw[0:4, :]
            row = slice(ROW_MISC + l, ROW_MISC + l + 1)
            o_ref[row, LANE_QG:LANE_QG + HD] = attn[0:1, 0:HD]
            o_ref[row, LANE_KG:LANE_KG + HD] = attn[1:2, 0:HD]
            o_ref[row, LANE_SINK:LANE_SINK + NQ] = attn[2:3, 0:NQ]
            o_ref[row, LANE_DTB:LANE_DTB + NSSM] = ssd[1:2, 0:NSSM]
            o_ref[row, LANE_ALOG:LANE_ALOG + NSSM] = ssd[2:3, 0:NSSM]
            o_ref[row, LANE_DSKIP:LANE_DSKIP + NSSM] = ssd[3:4, 0:NSSM]
        o_ref[ROW_RELB:ROW_RELB + NQ, 0:N_BUCKETS] = drel_ref[...]
        o_ref[ROW_LOSS:ROW_LOSS + 1, 0:1] = loss_ref[0:1, 0:1]

    args = []
    for sm in smalls:
        args += [sm["mix_norm_g"], sm["mlp_norm_g"], sm["conv_b"], sm["conv_w"], sm["ssd"], sm["attn"]]
    args += [drel_t, loss]
    return pl.pallas_call(body, out_shape=SDS((SMALL_ROWS, D), f32), name="pack_small_grads")(*args)


def _adamw_small(part, land, w, m, v):
    n = len(SMALL_NAMES)

    def grad_of(name, g_ref):
        if name == "mix_norm_g":
            return g_ref[ROW_MIXG:ROW_MIXG + DEPTH, :]
        if name == "mlp_norm_g":
            return g_ref[ROW_MLPG:ROW_MLPG + DEPTH, :]
        if name == "conv_b":
            return g_ref[ROW_CONVB:ROW_CONVB + DEPTH, :]
        if name == "ssm_norm_g":
            return g_ref[ROW_SSMG:ROW_SSMG + DEPTH, 0:D_SSM]
        if name == "rel_bias":
            return g_ref[ROW_RELB:ROW_RELB + NQ, 0:N_BUCKETS].T
        lane, width = MISC_LANES[name]
        return g_ref[ROW_MISC:ROW_MISC + DEPTH, lane:lane + width]

    def body(part_ref, land_ref, *refs):
        ws, ms, vs = refs[:n], refs[n:2 * n], refs[2 * n:3 * n]
        loss_ref = refs[3 * n]
        outs = refs[3 * n + 1:-1]
        g_ref = refs[-1]
        me = _dev_index(*_my_place())
        for p in range(N_DEV):
            term = jnp.where(me == p, part_ref[...], land_ref[p])
            if p == 0:
                g_ref[...] = term
            else:
                g_ref[...] += term
        loss_ref[...] = g_ref[ROW_LOSS:ROW_LOSS + 1, 0:128]
        my_cols = pl.ds(pl.multiple_of(me * 128, 128), 128)
        for k, name in enumerate(SMALL_NAMES):
            g_out, d_out, m_out, v_out = outs[4 * k:4 * k + 4]
            if name == "conv_w":
                for l in range(DEPTH):
                    g = g_ref[ROW_CONVW + 4 * l:ROW_CONVW + 4 * l + 4, my_cols]
                    delta, m_new, v_new = _adamw_math(ws[k][l], ms[k][l], vs[k][l], g)
                    g_out[l], d_out[l], m_out[l], v_out[l] = g, delta, m_new, v_new
            else:
                g = grad_of(name, g_ref)
                delta, m_new, v_new = _adamw_math(ws[k][...], ms[k][...], vs[k][...], g)
                g_out[...], d_out[...], m_out[...], v_out[...] = g, delta, m_new, v_new

    ws = [w[name] for name in SMALL_NAMES]
    out_shape = [SDS((1, 128), f32)]
    for a in ws:
        out_shape += [SDS(a.shape, f32)] * 4
    return pl.pallas_call(body, out_shape=out_shape, name="adamw_small", scratch_shapes=[pltpu.VMEM((SMALL_ROWS, D), f32)])(
        part, land, *ws, *[m[name] for name in SMALL_NAMES], *[v[name] for name in SMALL_NAMES])


def _plain(tm, tn):
    return pl.BlockSpec((tm, tn), lambda i, j, k: (i, j))


def _rowblk(tm, width):
    return pl.BlockSpec((tm, width), lambda i, j, k: (i, 0))


def _store_epi(dtype):
    def epi(acc, i, j, ex, outs):
        outs[0][...] = acc.astype(dtype)
    return epi


def _rms_prologue(layer):
    def pro(a_ref, ex, outs):
        xv = a_ref[...]
        r = lax.rsqrt(jnp.mean(xv * xv, axis=-1, keepdims=True) + EPS)
        h = (xv * r * ex[0][layer:layer + 1, :]).astype(bf16)
        outs[-1][...] = h
        return h
    return pro


MLP_TM = 256
MLP_VMEM = 56 * 1024 * 1024


def _resident(shape):
    return pl.BlockSpec((None,) + shape, lambda i: (0, 0, 0), pipeline_mode=pl.Buffered(1))


def _mlp_fwd(layer, x, mix, g, w_out, w_up, w_down, tgt=None):
    tm = MLP_TM
    with_loss = tgt is not None

    def body(x_ref, mix_ref, g_ref, wo_ref, wu_ref, wd_ref, *rest):
        xm_ref, a_ref, r_ref, h_ref = rest[with_loss:with_loss + 4]
        rest = rest[:with_loss] + rest[with_loss + 1:]
        i = pl.program_id(0)
        xv = x_ref[...] + _dot(mix_ref[...], wo_ref[...], NN_DIMS)
        xm_ref[...] = xv
        h = (xv * lax.rsqrt(jnp.mean(xv * xv, axis=-1, keepdims=True) + EPS) * g_ref[layer:layer + 1, :]).astype(bf16)
        h_ref[...] = h
        r = jnp.maximum(_dot(h, wu_ref[...], NN_DIMS), 0.0)
        a = (r * r).astype(bf16)
        a_ref[...] = a
        r_ref[...] = r.astype(bf16)
        y = xv + _dot(a, wd_ref[...], NN_DIMS)
        if not with_loss:
            rest[3][...] = y
            return
        err = y - rest[0][...]
        rest[4][...] = err * (1.0 / D)
        part = 0.5 * jnp.sum(jnp.mean(err * err, axis=-1, keepdims=True), axis=0, keepdims=True)

        @pl.when(i == 0)
        def _():
            rest[5][...] = jnp.zeros_like(rest[5])

        rest[5][...] += jnp.broadcast_to(part, rest[5].shape)

    row = lambda width: pl.BlockSpec((tm, width), lambda i: (i, 0))
    in_specs = [row(D), row(D), pl.BlockSpec((DEPTH, D), lambda i: (0, 0)), _resident((D, D)), _resident((D, D_FF)),
                _resident((D_FF, D))]
    out_specs = [row(D), row(D_FF), row(D_FF), row(D), row(D)]
    out_shape = [SDS((S, D), f32), SDS((S, D_FF), bf16), SDS((S, D_FF), bf16), SDS((S, D), bf16), SDS((S, D), f32)]
    args = [x, mix, g, w_out, w_up, w_down]
    if with_loss:
        in_specs.append(row(D))
        args.append(tgt)
        out_specs.append(pl.BlockSpec((1, 128), lambda i: (0, 0)))
        out_shape.append(SDS((1, 128), f32))
    return pl.pallas_call(
        body, grid=(S // tm,), in_specs=in_specs, out_specs=out_specs, out_shape=_out_hbm(out_shape),
        name="mlp_fwd_loss" if with_loss else "mlp_fwd",
        compiler_params=pltpu.CompilerParams(dimension_semantics=("arbitrary",), vmem_limit_bytes=MLP_VMEM),
    )(*_in_hbm(args[:3]), *args[3:6], *_in_hbm(args[6:]))


def _mlp_bwd_act(layer, dx_out, r_act, x_mid, g, w_down, w_up, w_out, deps):
    tm = MLP_TM

    def body(dxo_ref, r_ref, xm_ref, g_ref, wd_ref, wu_ref, wo_ref, *rest):
        du_ref, dx_ref, dg_ref, dmix_ref = rest[len(deps):]
        dxo = dxo_ref[...]
        du = (_dot(dxo.astype(bf16), wd_ref[...], NT_DIMS) * (2.0 * r_ref[...].astype(f32))).astype(bf16)
        du_ref[...] = du
        dh = _dot(du, wu_ref[...], NT_DIMS)
        _rms_bwd_epilogue(layer)(dh, pl.program_id(0), 0, (xm_ref, g_ref, dxo_ref), (dx_ref, dg_ref))
        dmix_ref[...] = _dot(dx_ref[...].astype(bf16), wo_ref[...], NT_DIMS)

    row = lambda width: pl.BlockSpec((tm, width), lambda i: (i, 0))
    return pl.pallas_call(
        body, grid=(S // tm,),
        in_specs=[row(D), row(D_FF), row(D), pl.BlockSpec((DEPTH, D), lambda i: (0, 0)), _resident((D_FF, D)), _resident((D, D_FF)),
                  _resident((D, D))] + [ANY_SPEC] * len(deps),
        out_specs=[row(D_FF), row(D), pl.BlockSpec((1, D), lambda i: (0, 0)), row(D)],
        out_shape=_out_hbm([SDS((S, D_FF), bf16), SDS((S, D), f32), SDS((1, D), f32), SDS((S, D), f32)]), name="mlp_bwd_act",
        compiler_params=pltpu.CompilerParams(dimension_semantics=("arbitrary",), vmem_limit_bytes=MLP_VMEM),
    )(*_in_hbm([dx_out, r_act, x_mid, g]), w_down, w_up, w_out, *_in_hbm(deps))


def _layer_fwd(l, x, p, get_weights, bias, tgt=None):
    wts = get_weights(l, "in", [x, bias])
    gfull = pl.BlockSpec((DEPTH, D), lambda i, j, k: (0, 0))
    tm = 512

    def inproj_epi(acc, i, j, ex, outs):
        outs[0][...] = acc[:, COL_QKV:COL_Z].astype(bf16)
        outs[1][...] = acc[:, COL_Z:COL_DT].astype(bf16)
        outs[2][...] = acc[:, COL_XBC:D_IN_PAD].astype(bf16)
        outs[3][...] = acc[:, COL_DT:COL_DT + 128]

    qkv, z, xbc, dt, h1 = _matmul(
        "in_proj", "nn", x, wts["w_in"], tm=tm, tn=D_IN_PAD, tk=D, prologue=_rms_prologue(l),
        extras=(p["mix_norm_g"],), extra_specs=(gfull,),
        out_shape=[SDS((S, 768), bf16), SDS((S, 512), bf16), SDS((S, 1024), bf16), SDS((S, 128), f32), SDS((S, D), bf16)],
        out_specs=[_rowblk(tm, 768), _rowblk(tm, 512), _rowblk(tm, 1024), _rowblk(tm, 128), _rowblk(tm, D)], epilogue=inproj_epi)
    attn = _attn_fwd(qkv, p["q_gain"], p["k_gain"], p["sinks"], bias, l)
    xact = _conv_fwd(xbc, wts["conv_w"], p["conv_b"], l)
    mix, hs, y_ssd = _ssd_fwd(xact, z, dt, attn, p["dt_bias"], p["a_log"], p["d_skip"], p["ssm_norm_g"], l)
    wts = dict(wts, **get_weights(l, "rest", [mix]))

    x_mid, a_act, r_act, h2, *result = _mlp_fwd(l, x, mix, p["mlp_norm_g"], wts["w_out"], wts["w_up"], wts["w_down"], tgt)
    saved = dict(x=x, h1=h1, qkv=qkv, z=z, xbc=xbc, dt=dt, xact=xact, mix=mix, hs=hs, y_ssd=y_ssd, x_mid=x_mid, h2=h2,
                 a=a_act, r=r_act, wts=wts)
    return (result[0] if tgt is None else tuple(result)), saved


def _layer_bwd(l, dx_out, sv, p, bias, deps, send):
    wts = sv["wts"]

    dw_down = _matmul("dw_down", "tn", sv["a"], dx_out, tm=1024, tn=D, tk=S, out_shape=SDS((D_FF, D), bf16),
                      out_specs=_plain(1024, D), epilogue=_store_epi(bf16), deps=deps)
    deps = send(l, dict(w_down=dw_down))
    du, dx_mid, dg_mlp, dmix = _mlp_bwd_act(l, dx_out, sv["r"], sv["x_mid"], p["mlp_norm_g"], wts["w_down"], wts["w_up"],
                                            wts["w_out"], deps)
    dw_up = _matmul("dw_up", "tn", sv["h2"], du, tm=D, tn=2048, tk=S, out_shape=SDS((D, D_FF), bf16),
                    out_specs=_plain(D, 2048), epilogue=_store_epi(bf16))
    dw_out = _matmul("dw_out", "tn", sv["mix"], dx_mid, tm=D, tn=512, tk=S, out_shape=SDS((D, D), bf16),
                     out_specs=_plain(D, 512), epilogue=_store_epi(bf16))
    deps = send(l, dict(w_up=dw_up, w_out=dw_out))
    gfull = pl.BlockSpec((DEPTH, D), lambda i, j, k: (0, 0))
    grow = pl.BlockSpec((1, D), lambda i, j, k: (0, 0))
    dproj, dbias, dsm_attn = _attn_bwd(sv["qkv"], dmix, p["q_gain"], p["k_gain"], p["sinks"], bias, l, deps)
    dproj, dxact, dsm_ssd = _ssd_bwd(sv["xact"], sv["z"], sv["dt"], dmix, sv["hs"], sv["y_ssd"], p["dt_bias"], p["a_log"],
                                     p["d_skip"], p["ssm_norm_g"], dproj, l)
    dproj, dconv_w, dconv_b = _conv_bwd(sv["xbc"], dxact, wts["conv_w"], p["conv_b"], dproj, l)
    dw_in = _matmul("dw_in", "tn", sv["h1"], dproj, tm=D, tn=1280, tk=S, out_shape=SDS((D, D_IN_PAD), bf16),
                    out_specs=_plain(D, 1280), epilogue=_store_epi(bf16))
    deps = send(l, dict(w_in=_w_in_slabs(dw_in)))
    dx, dg_mix = _matmul(
        "in_proj_dh", "nt", dproj, wts["w_in"], tm=512, tn=D, tk=D_IN_PAD, out_shape=[SDS((S, D), f32), SDS((1, D), f32)],
        out_specs=[_plain(512, D), grow], epilogue=_rms_bwd_epilogue(l),
        extras=(sv["x"], p["mix_norm_g"], dx_mid), extra_specs=(_plain(512, D), gfull, _plain(512, D)), deps=deps)
    small = dict(mix_norm_g=dg_mix, mlp_norm_g=dg_mlp, conv_w=dconv_w, conv_b=dconv_b, ssd=dsm_ssd, attn=dsm_attn, dbias=dbias)
    return dx, small, deps


def _local_step(x, tgt, p, get_weights, send):
    onehot_t = jnp.asarray(_onehot_buckets(), dtype=bf16)
    bias = _bias_build(p["rel_bias"].T, onehot_t).reshape(NQ, BLK, 2 * BLK)
    saved = []
    h = x
    for l in range(DEPTH):
        h, sv = _layer_fwd(l, h, p, get_weights, bias, tgt if l == DEPTH - 1 else None)
        saved.append(sv)
    dx, loss = h
    smalls = [None] * DEPTH
    deps = ()
    for l in reversed(range(DEPTH)):
        dx, smalls[l], deps = _layer_bwd(l, dx, saved[l], p, bias, deps, send)
    drel_t = _bias_grad(smalls[0]["dbias"].reshape(NQ, -1), smalls[1]["dbias"].reshape(NQ, -1), onehot_t)
    return dx, _pack_small_grads(smalls, drel_t, loss)


WEIGHT_ORDER = ("mix_norm_g", "w_in", "q_gain", "k_gain", "sinks", "rel_bias", "conv_w", "conv_b", "dt_bias", "a_log", "d_skip",
                "ssm_norm_g", "w_out", "mlp_norm_g", "w_up", "w_down")


def kernel(x, mix_norm_g, w_in, q_gain, k_gain, sinks, rel_bias, conv_w, conv_b, dt_bias, a_log, d_skip, ssm_norm_g, w_out, mlp_norm_g, w_up, w_down, loss_target, m_mix_norm_g, m_w_in, m_q_gain, m_k_gain, m_sinks, m_rel_bias, m_conv_w, m_conv_b, m_dt_bias, m_a_log, m_d_skip, m_ssm_norm_g, m_w_out, m_mlp_norm_g, m_w_up, m_w_down, v_mix_norm_g, v_w_in, v_q_gain, v_k_gain, v_sinks, v_rel_bias, v_conv_w, v_conv_b, v_dt_bias, v_a_log, v_d_skip, v_ssm_norm_g, v_w_out, v_mlp_norm_g, v_w_up, v_w_down):
    w = dict(mix_norm_g=mix_norm_g, w_in=w_in, q_gain=q_gain, k_gain=k_gain, sinks=sinks, rel_bias=rel_bias, conv_w=conv_w,
             conv_b=conv_b, dt_bias=dt_bias, a_log=a_log, d_skip=d_skip, ssm_norm_g=ssm_norm_g, w_out=w_out,
             mlp_norm_g=mlp_norm_g, w_up=w_up, w_down=w_down)
    m = dict(mix_norm_g=m_mix_norm_g, w_in=m_w_in, q_gain=m_q_gain, k_gain=m_k_gain, sinks=m_sinks, rel_bias=m_rel_bias,
             conv_w=m_conv_w, conv_b=m_conv_b, dt_bias=m_dt_bias, a_log=m_a_log, d_skip=m_d_skip, ssm_norm_g=m_ssm_norm_g,
             w_out=m_w_out, mlp_norm_g=m_mlp_norm_g, w_up=m_w_up, w_down=m_w_down)
    v = dict(mix_norm_g=v_mix_norm_g, w_in=v_w_in, q_gain=v_q_gain, k_gain=v_k_gain, sinks=v_sinks, rel_bias=v_rel_bias,
             conv_w=v_conv_w, conv_b=v_conv_b, dt_bias=v_dt_bias, a_log=v_a_log, d_skip=v_d_skip, ssm_norm_g=v_ssm_norm_g,
             w_out=v_w_out, mlp_norm_g=v_mlp_norm_g, w_up=v_w_up, w_down=v_w_down)
    big = ("w_in", "w_out", "w_up", "w_down")

    my_idx = _dev_index(*_my_place()).astype(jnp.int32).reshape(1)

    fulls = {n: _cast_to_full("cast_" + n, w[n], KIND[n], FULL_SHAPE[n], my_idx, bf16) for n in big}
    conv_full = _cast_to_full("cast_conv_w", conv_w.reshape(1, DEPTH * 4, 128), "stack", (N_DEV, DEPTH * 4, 128), my_idx, f32)[0]
    rest = ["w_out", "w_up", "w_down"]
    g0 = _gather_start("gather0", ["w_in", "conv_w"], [fulls["w_in"][0], conv_full], ())
    g1 = _gather_start("gather1", rest, [fulls[n][0] for n in rest], (g0["token"],))
    g2 = _gather_start("gather2", ["w_in"], [fulls["w_in"][1]], (g1["token"],))
    g3 = _gather_start("gather3", rest, [fulls[n][1] for n in rest], (g2["token"],))
    held = {}
    flat = lambda a: a.reshape(a.shape[0] * a.shape[1], a.shape[2])
    adam_in = {n: (flat(w[n]), flat(m[n]), flat(v[n])) for n in big}

    def get_weights(l, part, after):
        if l == 0 and part == "in":
            full_in, full_conv = _gather_finish("gather0", ["w_in", "conv_w"], g0,
                                                list(after) + [g3["token"], adam_in["w_in"][1], adam_in["w_in"][2]])
            held["conv_w"] = jnp.transpose(full_conv.reshape(N_DEV, DEPTH, 4, 128), (1, 2, 0, 3)).reshape(DEPTH, 4, D_CONV)
            return dict(w_in=_w_in_assemble(full_in), conv_w=held["conv_w"])
        if part == "in":
            return dict(w_in=_w_in_assemble(_gather_finish("gather2", ["w_in"], g2, after)[0]), conv_w=held["conv_w"])
        full = _gather_finish("gather1" if l == 0 else "gather3", rest, g1 if l == 0 else g3, after)
        return {n: f[None] for n, f in zip(rest, full)}

    pending = []

    def send(l, grads):
        names = list(grads)
        started = _exchange_start("exchange%d_%s" % (l, names[0]), names, [grads[n] for n in names], ())
        pending.append((l, names, started))
        return (started["token"],)

    dx, small_part = _local_step(x.reshape(S, D), loss_target.reshape(S, D), w, get_weights, send)

    small = _small_exchange_start(small_part, ())
    tiles = dict(w_in=512, w_out=128, w_up=512, w_down=256)
    outs_of = {n: None for n in big}
    after = [dx, small["token"]]
    for l, names, started in pending:
        bufs = _split_wait("exchange%d_%s_wait" % (l, names[0]), started, after)
        for t, n in enumerate(names):
            outs_of[n] = _adamw_layer("adamw_%s%d" % (n, l), KIND[n], l, *adam_in[n],
                                      bufs[len(names) + t], bufs[t], my_idx, outs_of[n], tiles[n])
        after = [outs_of[names[-1]][0]]
    res = {n: [o.reshape(w[n].shape) for o in outs_of[n]] for n in big}
    small_part, small_land = _split_wait("small_exchange_wait", small, after)
    small_outs = _adamw_small(small_part, small_land, w, m, v)
    loss = small_outs[0][0, 0]
    for k, name in enumerate(SMALL_NAMES):
        res[name] = small_outs[1 + 4 * k:5 + 4 * k]

    result = [loss, dx.reshape(1, S, D)]
    for k in range(4):
        result += [res[name][k] for name in WEIGHT_ORDER]
    return tuple(result)
```

```python
import functools
import math

import numpy as np
import jax
import jax.numpy as jnp
from jax import lax
from jax.experimental import pallas as pl
from jax.experimental.pallas import tpu as pltpu

f32 = jnp.float32
bf16 = jnp.bfloat16
SDS = jax.ShapeDtypeStruct
MESH = pl.DeviceIdType.MESH
HIGHEST = lax.Precision.HIGHEST

S = 2048
D = 1024
DEPTH = 2
BLK = 128
NBLK = S // BLK
HD = 64
NQ = 8
NKV = 2
NSSM = 8
NGRP = 2
NSTATE = 128
D_ATTN = 512
D_SSM = 512
D_CONV = 1024
D_FF = 4096
D_IN = 2312
D_IN_PAD = 2560
COL_QKV, COL_Z, COL_DT, COL_XBC = 0, 768, 1280, 1536
IN_SEGMENTS = ((0, 1280, 0), (1280, 2304, COL_XBC), (2304, 2312, COL_DT))
N_BUCKETS = 32
EPS = 1e-6
N_DEV = 8
VMEM_LIMIT = 48 * 1024 * 1024

ADAM_LR = 0.001
ADAM_B1 = 0.9
ADAM_B2 = 0.999
ADAM_EPS = 1e-08
ADAM_WD = 0.01
ADAM_STEP = 10

NT_DIMS = (((1,), (1,)), ((), ()))
TN_DIMS = (((0,), (0,)), ((), ()))
NN_DIMS = (((1,), (0,)), ((), ()))

ROW_MIXG = 0
ROW_MLPG = 2
ROW_CONVB = 4
ROW_SSMG = 6
ROW_MISC = 8
ROW_RELB = 10
ROW_CONVW = 18
ROW_LOSS = 26
SMALL_ROWS = 32
LANE_QG, LANE_KG, LANE_SINK, LANE_DTB, LANE_ALOG, LANE_DSKIP = 0, 64, 128, 256, 384, 512


def _dot(a, b, dims):
    return lax.dot_general(a, b, dims, preferred_element_type=f32)


def _cparams(n_axes):
    return pltpu.CompilerParams(dimension_semantics=("arbitrary",) * n_axes, vmem_limit_bytes=VMEM_LIMIT)


def _sum11(v):
    return jnp.sum(jnp.sum(v, axis=1, keepdims=True), axis=0, keepdims=True)


def _sigmoid(v):
    return 1.0 / (1.0 + jnp.exp(-v))


ANY_SPEC = pl.BlockSpec(memory_space=pl.ANY)


def _in_hbm(args):
    return [pltpu.with_memory_space_constraint(a, pltpu.HBM) if a.size >= 65536 else a for a in args]


def _out_hbm(out_shape):
    one = lambda s: pltpu.HBM(s.shape, s.dtype) if math.prod(s.shape) >= 65536 else s
    return [one(s) for s in out_shape] if isinstance(out_shape, (list, tuple)) else one(out_shape)


def _matmul(name, mode, a, b, *, layer=0, tm, tn, tk, out_shape, out_specs, epilogue, extras=(), extra_specs=(), deps=(),
            prologue=None):
    extras = tuple(extras) + tuple(deps)
    extra_specs = tuple(extra_specs) + (ANY_SPEC,) * len(deps)
    if mode == "tn":
        t_dim, m_dim = a.shape
        n_dim = b.shape[1]
        grid = (m_dim // tm, n_dim // tn, t_dim // tk)
        a_spec = pl.BlockSpec((tk, tm), lambda i, j, k: (k, i))
        b_spec = pl.BlockSpec((tk, tn), lambda i, j, k: (k, j))
        dims = TN_DIMS
    elif mode == "nn":
        m_dim, k_dim = a.shape
        n_dim = b.shape[-1]
        grid = (m_dim // tm, n_dim // tn, k_dim // tk)
        a_spec = pl.BlockSpec((tm, tk), lambda i, j, k: (i, k))
        b_spec = pl.BlockSpec((None, tk, tn), lambda i, j, k: (layer, k, j))
        dims = NN_DIMS
    else:
        m_dim, k_dim = a.shape
        n_dim = b.shape[-2]
        grid = (m_dim // tm, n_dim // tn, k_dim // tk)
        a_spec = pl.BlockSpec((tm, tk), lambda i, j, k: (i, k))
        b_spec = pl.BlockSpec((None, tn, tk), lambda i, j, k: (layer, j, k))
        dims = NT_DIMS
    nk = grid[2]
    n_ex = len(extras)

    def body(a_ref, b_ref, *rest):
        ex = rest[:n_ex - len(deps)]
        outs = rest[n_ex:-1]
        acc = rest[-1]
        i = pl.program_id(0)
        j = pl.program_id(1)
        k = pl.program_id(2)
        lhs = a_ref[...].astype(bf16) if prologue is None else prologue(a_ref, ex, outs)
        part = _dot(lhs, b_ref[...].astype(bf16), dims)
        if nk == 1:
            epilogue(part, i, j, ex, outs)
        else:
            @pl.when(k == 0)
            def _():
                acc[...] = part

            @pl.when(k > 0)
            def _():
                acc[...] += part

            @pl.when(k == nk - 1)
            def _():
                epilogue(acc[...], i, j, ex, outs)

    return pl.pallas_call(
        body, grid=grid, in_specs=[a_spec, b_spec, *extra_specs], out_specs=out_specs, out_shape=_out_hbm(out_shape),
        scratch_shapes=[pltpu.VMEM((tm, tn) if nk > 1 else (8, 128), f32)], name=name, compiler_params=_cparams(3),
    )(*_in_hbm([a]), b, *_in_hbm(extras))


def _rms_bwd_epilogue(layer):
    def epi(acc, i, j, ex, outs):
        x_ref, g_ref, dres_ref = ex
        dx_ref, dg_ref = outs
        xv = x_ref[...]
        r = lax.rsqrt(jnp.mean(xv * xv, axis=-1, keepdims=True) + EPS)
        xhat = xv * r
        w = acc * g_ref[layer:layer + 1, :]
        dx_ref[...] = dres_ref[...] + r * (w - xhat * jnp.mean(xhat * w, axis=-1, keepdims=True))
        dg = jnp.sum(acc * xhat, axis=0, keepdims=True)

        @pl.when(i == 0)
        def _():
            dg_ref[...] = dg

        @pl.when(i > 0)
        def _():
            dg_ref[...] += dg
    return epi


def _own_slab_spec(kind, tr, cols, nblk):
    if kind == "stack":
        return pl.BlockSpec((None, tr, cols), lambda i, idx: (idx[0], i, 0))
    if kind == "cols512":
        return pl.BlockSpec((tr, cols), lambda i, idx: (i, idx[0]))
    return pl.BlockSpec((tr, cols), lambda i, idx: (idx[0] * nblk + i, 0))


def _cast_to_full(name, w, kind, full_shape, my_idx, dtype):
    n_layers, rows, cols = w.shape
    tr = min(rows, 256)
    nblk = rows // tr

    def body(idx_ref, w_ref, *o_refs):
        for l in range(n_layers):
            o_refs[l][...] = w_ref[l].astype(dtype)

    grid_spec = pltpu.PrefetchScalarGridSpec(
        num_scalar_prefetch=1, grid=(nblk,), in_specs=[pl.BlockSpec((n_layers, tr, cols), lambda i, idx: (0, i, 0))],
        out_specs=[_own_slab_spec(kind, tr, cols, nblk)] * n_layers)
    return pl.pallas_call(body, grid_spec=grid_spec, out_shape=_out_hbm([SDS(full_shape, dtype)] * n_layers), name=name,
                          compiler_params=_cparams(1))(*_in_hbm([my_idx, w]))


def _adamw_math(w, m, v, g):
    m_new = ADAM_B1 * m + (1.0 - ADAM_B1) * g
    v_new = ADAM_B2 * v + (1.0 - ADAM_B2) * (g * g)
    m_hat = m_new / (1.0 - ADAM_B1 ** ADAM_STEP)
    v_hat = v_new / (1.0 - ADAM_B2 ** ADAM_STEP)
    delta = -ADAM_LR * (m_hat / (jnp.sqrt(v_hat) + ADAM_EPS) + ADAM_WD * w)
    return delta, m_new, v_new


def _adamw_layer(name, kind, layer, w, m, v, land, g_full, my_idx, prev, tr):
    rows2, cols = w.shape
    rows = rows2 // DEPTH
    nblk = rows // tr
    own_spec = _own_slab_spec(kind, tr, cols, nblk)
    n_prev = 0 if prev is None else 4

    def body(idx_ref, w_ref, m_ref, v_ref, land_ref, own_ref, *rest):
        g_ref, d_ref, mo_ref, vo_ref = rest[n_prev:]
        me = idx_ref[0]
        g = None
        for p in range(N_DEV):
            part = jnp.where(me == p, own_ref[...], land_ref[p]).astype(f32)
            g = part if g is None else g + part
        delta, m_new, v_new = _adamw_math(w_ref[...], m_ref[...], v_ref[...], g)
        g_ref[...] = g
        d_ref[...] = delta
        mo_ref[...] = m_new
        vo_ref[...] = v_new

    blk = pl.BlockSpec((tr, cols), lambda i, idx: (layer * nblk + i, 0))
    grid_spec = pltpu.PrefetchScalarGridSpec(
        num_scalar_prefetch=1, grid=(nblk,),
        in_specs=[blk, blk, blk, pl.BlockSpec((N_DEV, tr, cols), lambda i, idx: (0, i, 0)), own_spec] + [ANY_SPEC] * n_prev,
        out_specs=[blk, blk, blk, blk])
    aliases = {} if prev is None else {6 + k: k for k in range(4)}
    return pl.pallas_call(
        body, grid_spec=grid_spec, out_shape=_out_hbm([SDS((rows2, cols), f32)] * 4), name=name, input_output_aliases=aliases,
        compiler_params=_cparams(1),
    )(*_in_hbm([my_idx, w, m, v, land, g_full, *([] if prev is None else prev)]))


def _bucket_table():
    qi = np.arange(BLK)[:, None]
    kj = np.arange(2 * BLK)[None, :]
    dist = qi + BLK - kj
    dcl = np.clip(dist, 0, None)
    max_exact = N_BUCKETS // 2
    d_f = np.maximum(dcl, 1).astype(np.float32)
    large = max_exact + (np.log(d_f / np.float32(max_exact)) / np.float32(math.log(128 / max_exact))
                         * np.float32(N_BUCKETS - max_exact)).astype(np.int32)
    large = np.minimum(large, N_BUCKETS - 1)
    bucket = np.where(dcl < max_exact, dcl, large)
    in_window = (dist >= 0) & (dist < BLK)
    return bucket.astype(np.int32), in_window


def _onehot_buckets():
    bucket, _ = _bucket_table()
    oh = (bucket.reshape(-1)[None, :] == np.arange(N_BUCKETS)[:, None]).astype(np.float32)
    return oh


def _bias_build(rel_bias_t, onehot_t):
    def body(r_ref, o_ref, out_ref):
        r = r_ref[...]
        hi = r.astype(bf16)
        r1 = r - hi.astype(f32)
        mid = r1.astype(bf16)
        lo = (r1 - mid.astype(f32)).astype(bf16)
        oh = o_ref[...]
        out_ref[...] = _dot(hi, oh, NN_DIMS) + _dot(mid, oh, NN_DIMS) + _dot(lo, oh, NN_DIMS)

    tn = 4096
    return pl.pallas_call(
        body, grid=(BLK * 2 * BLK // tn,),
        in_specs=[pl.BlockSpec((NQ, N_BUCKETS), lambda i: (0, 0)), pl.BlockSpec((N_BUCKETS, tn), lambda i: (0, i))],
        out_specs=pl.BlockSpec((NQ, tn), lambda i: (0, i)), out_shape=SDS((NQ, BLK * 2 * BLK), f32), name="bias_build",
        compiler_params=_cparams(1),
    )(rel_bias_t, onehot_t)


def _bias_grad(dbias0, dbias1, onehot_t):
    tn = 4096
    nsteps = BLK * 2 * BLK // tn

    def body(a_ref, b_ref, o_ref, out_ref):
        g = a_ref[...] + b_ref[...]
        hi = g.astype(bf16)
        lo = (g - hi.astype(f32)).astype(bf16)
        part = _dot(hi, o_ref[...], NT_DIMS) + _dot(lo, o_ref[...], NT_DIMS)

        @pl.when(pl.program_id(0) == 0)
        def _():
            out_ref[...] = part

        @pl.when(pl.program_id(0) > 0)
        def _():
            out_ref[...] += part

    return pl.pallas_call(
        body, grid=(nsteps,),
        in_specs=[pl.BlockSpec((NQ, tn), lambda i: (0, i)), pl.BlockSpec((NQ, tn), lambda i: (0, i)),
                  pl.BlockSpec((N_BUCKETS, tn), lambda i: (0, i))],
        out_specs=pl.BlockSpec((NQ, N_BUCKETS), lambda i: (0, 0)), out_shape=SDS((NQ, N_BUCKETS), f32), name="bias_grad",
        compiler_params=_cparams(1),
    )(dbias0, dbias1, onehot_t)


def _attn_mask(n):
    qi = lax.broadcasted_iota(jnp.int32, (BLK, 2 * BLK), 0)
    kj = lax.broadcasted_iota(jnp.int32, (BLK, 2 * BLK), 1)
    dist = qi + BLK - kj
    first_key = jnp.where(n > 0, 0, BLK)
    return (dist >= 0) & (dist < BLK) & (kj >= first_key)


def _row_mean(a):
    return jnp.mean(a, axis=-1, keepdims=True)


def _head_norm(t, gain):
    r = lax.rsqrt(_row_mean(t * t) + EPS)
    that = t * r
    return that, r, that * gain


def _softmax_with_sink(s, sink):
    m = jnp.maximum(jnp.max(s, axis=-1, keepdims=True), sink)
    p = jnp.exp(s - m)
    psink = jnp.exp(sink - m)
    inv = 1.0 / (jnp.sum(p, axis=-1, keepdims=True) + psink)
    return p * inv, psink * inv


GQ = NQ // NKV


def _attn_fwd(qkv, q_gain, k_gain, sinks, bias, layer):
    def body(q_ref, kc_ref, kp_ref, vc_ref, vp_ref, qg_ref, kg_ref, sk_ref, bias_ref, o_ref):
        m = pl.program_id(0)
        qg = qg_ref[layer:layer + 1, :]
        kg = kg_ref[layer:layer + 1, :]
        grp = range(NKV)
        chains = [(b, j) for b in range(2) for j in grp]
        masks = [jnp.tile(_attn_mask(2 * m + b), (GQ, 1)) for b in range(2)]
        kblk = [[kp_ref[:, pl.ds(HD * j, HD)].astype(f32), kc_ref[0:BLK, pl.ds(HD * j, HD)].astype(f32),
                 kc_ref[BLK:, pl.ds(HD * j, HD)].astype(f32)] for j in grp]
        vblk = [[vp_ref[:, pl.ds(HD * j, HD)].astype(bf16), vc_ref[0:BLK, pl.ds(HD * j, HD)].astype(bf16),
                 vc_ref[BLK:, pl.ds(HD * j, HD)].astype(bf16)] for j in grp]
        knb = [[_head_norm(kblk[j][t], kg)[2].astype(bf16) for t in range(3)] for j in grp]
        kn_b = {(b, j): jnp.concatenate([knb[j][b], knb[j][b + 1]], axis=0) for b, j in chains}
        vbs = {(b, j): jnp.concatenate([vblk[j][b], vblk[j][b + 1]], axis=0) for b, j in chains}
        rows = {}
        for b, j in chains:
            heads = [GQ * j + g for g in range(GQ)]
            rows[b, j] = (jnp.concatenate([q_ref[pl.ds(BLK * b, BLK), pl.ds(HD * h, HD)] for h in heads], axis=0).astype(f32),
                          jnp.concatenate([jnp.broadcast_to(sk_ref[layer:layer + 1, h:h + 1], (BLK, 1)) for h in heads], axis=0))
        qn_b = {c: _head_norm(rows[c][0], qg)[2].astype(bf16) for c in chains}
        ss = {(b, j): _dot(qn_b[b, j], kn_b[b, j], NT_DIMS) * (HD ** -0.5) + bias_ref[GQ * j:GQ * (j + 1)].reshape(GQ * BLK, 2 * BLK)
              for b, j in chains}
        ps = {(b, j): _softmax_with_sink(jnp.where(masks[b], ss[b, j], -jnp.inf), rows[b, j][1])[0] for b, j in chains}
        outs = {c: _dot(ps[c].astype(bf16), vbs[c], NN_DIMS).astype(bf16) for c in chains}
        for b, j in chains:
            for g in range(GQ):
                o_ref[pl.ds(BLK * b, BLK), pl.ds(HD * (GQ * j + g), HD)] = outs[b, j][BLK * g:BLK * (g + 1), :]

    prev = lambda m: jnp.maximum(2 * m - 1, 0)
    small = lambda shape: pl.BlockSpec(shape, lambda m: (0,) * len(shape))
    return pl.pallas_call(
        body, grid=(NBLK // 2,),
        in_specs=[pl.BlockSpec((2 * BLK, D_ATTN), lambda m: (m, 0)),
                  pl.BlockSpec((2 * BLK, 128), lambda m: (m, 4)), pl.BlockSpec((BLK, 128), lambda m: (prev(m), 4)),
                  pl.BlockSpec((2 * BLK, 128), lambda m: (m, 5)), pl.BlockSpec((BLK, 128), lambda m: (prev(m), 5)),
                  small((DEPTH, HD)), small((DEPTH, HD)), small((DEPTH, NQ)), small((NQ, BLK, 2 * BLK))],
        out_specs=pl.BlockSpec((2 * BLK, D_ATTN), lambda m: (m, 0)), out_shape=_out_hbm(SDS((S, D_ATTN), bf16)),
        name="attn_fwd", compiler_params=_cparams(1),
    )(*_in_hbm([qkv, qkv, qkv, qkv, qkv, q_gain, k_gain, sinks, bias]))


def _attn_bwd(qkv, dmix, q_gain, k_gain, sinks, bias, layer, deps=()):
    def body(q_ref, kc_ref, kp_ref, vc_ref, vp_ref, do_ref, qg_ref, kg_ref, sk_ref, bias_ref, *rest):
        dqkv_ref, dbias_ref, dsm_ref, carry = rest[len(deps):]
        i = pl.program_id(0)
        m = NBLK // 2 - 1 - i
        qg = qg_ref[layer:layer + 1, :]
        kg = kg_ref[layer:layer + 1, :]
        lane = lax.broadcasted_iota(jnp.int32, (1, 128), 1)

        @pl.when(i == 0)
        def _():
            carry[...] = jnp.zeros_like(carry)
            dbias_ref[...] = jnp.zeros_like(dbias_ref)
            dsm_ref[...] = jnp.zeros_like(dsm_ref)

        grp = range(NKV)
        chains = [(b, j) for b in range(2) for j in grp]
        masks = [jnp.tile(_attn_mask(2 * m + b), (GQ, 1)) for b in range(2)]
        kblk = [[kp_ref[:, pl.ds(HD * j, HD)].astype(f32), kc_ref[0:BLK, pl.ds(HD * j, HD)].astype(f32),
                 kc_ref[BLK:, pl.ds(HD * j, HD)].astype(f32)] for j in grp]
        vblk = [[vp_ref[:, pl.ds(HD * j, HD)].astype(bf16), vc_ref[0:BLK, pl.ds(HD * j, HD)].astype(bf16),
                 vc_ref[BLK:, pl.ds(HD * j, HD)].astype(bf16)] for j in grp]
        knorm = [[_head_norm(kblk[j][t], kg) for t in range(3)] for j in grp]
        kn_b = {(b, j): jnp.concatenate([knorm[j][b][2].astype(bf16), knorm[j][b + 1][2].astype(bf16)], axis=0) for b, j in chains}
        vbs = {(b, j): jnp.concatenate([vblk[j][b], vblk[j][b + 1]], axis=0) for b, j in chains}
        rows, do_b = {}, {}
        for b, j in chains:
            heads = [GQ * j + g for g in range(GQ)]
            qrows = pl.ds(BLK * b, BLK)
            rows[b, j] = (jnp.concatenate([q_ref[qrows, pl.ds(HD * h, HD)] for h in heads], axis=0).astype(f32),
                          jnp.concatenate([jnp.broadcast_to(sk_ref[layer:layer + 1, h:h + 1], (BLK, 1)) for h in heads], axis=0))
            do_b[b, j] = jnp.concatenate([do_ref[qrows, pl.ds(HD * h, HD)] for h in heads], axis=0).astype(bf16)
        qnorm = {c: _head_norm(rows[c][0], qg) for c in chains}
        qn_b = {c: qnorm[c][2].astype(bf16) for c in chains}
        ss = {(b, j): _dot(qn_b[b, j], kn_b[b, j], NT_DIMS) * (HD ** -0.5) + bias_ref[GQ * j:GQ * (j + 1)].reshape(GQ * BLK, 2 * BLK)
              for b, j in chains}
        sm = {(b, j): _softmax_with_sink(jnp.where(masks[b], ss[b, j], -jnp.inf), rows[b, j][1]) for b, j in chains}
        dps = {c: _dot(do_b[c], vbs[c], NT_DIMS) for c in chains}
        deltas = {c: jnp.sum(sm[c][0] * dps[c], axis=-1, keepdims=True) for c in chains}
        dss = {c: sm[c][0] * (dps[c] - deltas[c]) for c in chains}
        ds_b = {c: (dss[c] * (HD ** -0.5)).astype(bf16) for c in chains}
        dqn = {c: _dot(ds_b[c], kn_b[c], NN_DIMS) for c in chains}
        dkn = {c: _dot(ds_b[c], qn_b[c], TN_DIMS) for c in chains}
        dvs = {c: _dot(sm[c][0].astype(bf16), do_b[c], TN_DIMS) for c in chains}
        dqg = jnp.zeros((1, HD), f32)
        dkg = jnp.zeros((1, HD), f32)
        dsink = jnp.zeros((1, 128), f32)
        for b, j in chains:
            dbias_ref[GQ * j:GQ * (j + 1)] += dss[b, j].reshape(GQ, BLK, 2 * BLK)
            dsk = sm[b, j][1] * deltas[b, j]
            for g in range(GQ):
                dsink = dsink + jnp.where(lane == GQ * j + g, -_sum11(dsk[BLK * g:BLK * (g + 1), :]), 0.0)
            qhat, rq, _ = qnorm[b, j]
            w = dqn[b, j] * qg
            dq = rq * (w - qhat * _row_mean(qhat * w))
            for g in range(GQ):
                dqkv_ref[pl.ds(BLK * b, BLK), pl.ds(HD * (GQ * j + g), HD)] = dq[BLK * g:BLK * (g + 1), :].astype(bf16)
            dqg = dqg + jnp.sum(dqn[b, j] * qhat, axis=0, keepdims=True)
        for j in grp:
            dkn_t = [dkn[0, j][:BLK, :], dkn[0, j][BLK:, :] + dkn[1, j][:BLK, :], dkn[1, j][BLK:, :]]
            dv_t = [dvs[0, j][:BLK, :], dvs[0, j][BLK:, :] + dvs[1, j][:BLK, :], dvs[1, j][BLK:, :]]
            dk_t = []
            for t in range(3):
                khat, rk, _ = knorm[j][t]
                w = dkn_t[t] * kg
                dk_t.append(rk * (w - khat * _row_mean(khat * w)))
                dkg = dkg + jnp.sum(dkn_t[t] * khat, axis=0, keepdims=True)
            kcols, vcols = pl.ds(D_ATTN + HD * j, HD), pl.ds(D_ATTN + 128 + HD * j, HD)
            dqkv_ref[BLK:, kcols] = (dk_t[2] + carry[:, pl.ds(HD * j, HD)]).astype(bf16)
            dqkv_ref[BLK:, vcols] = (dv_t[2] + carry[:, pl.ds(128 + HD * j, HD)]).astype(bf16)
            dqkv_ref[0:BLK, kcols] = dk_t[1].astype(bf16)
            dqkv_ref[0:BLK, vcols] = dv_t[1].astype(bf16)
            carry[:, pl.ds(HD * j, HD)] = dk_t[0]
            carry[:, pl.ds(128 + HD * j, HD)] = dv_t[0]
        dsm_ref[0:1, 0:HD] += dqg
        dsm_ref[1:2, 0:HD] += dkg
        dsm_ref[2:3, :] += dsink

    rev = lambda i: NBLK // 2 - 1 - i
    prev = lambda i: jnp.maximum(NBLK - 3 - 2 * i, 0)
    small = lambda shape: pl.BlockSpec(shape, lambda i: (0,) * len(shape))
    return pl.pallas_call(
        body, grid=(NBLK // 2,),
        in_specs=[pl.BlockSpec((2 * BLK, D_ATTN), lambda i: (rev(i), 0)),
                  pl.BlockSpec((2 * BLK, 128), lambda i: (rev(i), 4)), pl.BlockSpec((BLK, 128), lambda i: (prev(i), 4)),
                  pl.BlockSpec((2 * BLK, 128), lambda i: (rev(i), 5)), pl.BlockSpec((BLK, 128), lambda i: (prev(i), 5)),
                  pl.BlockSpec((2 * BLK, D_ATTN), lambda i: (rev(i), 0)),
                  small((DEPTH, HD)), small((DEPTH, HD)), small((DEPTH, NQ)), small((NQ, BLK, 2 * BLK))] + [ANY_SPEC] * len(deps),
        out_specs=[pl.BlockSpec((2 * BLK, 768), lambda i: (rev(i), COL_QKV // 768)), small((NQ, BLK, 2 * BLK)), small((8, 128))],
        out_shape=_out_hbm([SDS((S, D_IN_PAD), bf16), SDS((NQ, BLK, 2 * BLK), f32), SDS((8, 128), f32)]),
        scratch_shapes=[pltpu.VMEM((BLK, 256), f32)], name="attn_bwd", compiler_params=_cparams(1),
    )(*_in_hbm([qkv, qkv, qkv, qkv, qkv, dmix, q_gain, k_gain, sinks, bias, *deps]))


CONV_TC = 256


def _shift_down(u, s):
    if s == 0:
        return u
    rows = lax.broadcasted_iota(jnp.int32, u.shape, 0)
    return jnp.where(rows >= s, pltpu.roll(u, s, 0), 0.0)


def _shift_up(u, s):
    if s == 0:
        return u
    rows = lax.broadcasted_iota(jnp.int32, u.shape, 0)
    return jnp.where(rows < u.shape[0] - s, pltpu.roll(u, u.shape[0] - s, 0), 0.0)


def _conv_specs():
    return [pl.BlockSpec((S, CONV_TC), lambda c: (0, c)),
            pl.BlockSpec((None, 4, CONV_TC), lambda c: (0, 0, c)),
            pl.BlockSpec((DEPTH, CONV_TC), lambda c: (0, c))]


def _conv_pre(u, w_ref, b_ref, layer):
    pre = b_ref[layer:layer + 1, :] + w_ref[3:4, :] * u
    for k in range(3):
        pre = pre + w_ref[k:k + 1, :] * _shift_down(u, 3 - k)
    return pre


def _conv_fwd(xbc, conv_w, conv_b, layer):
    def body(u_ref, w_ref, b_ref, o_ref):
        pre = _conv_pre(u_ref[...].astype(f32), w_ref, b_ref, layer)
        o_ref[...] = pre * _sigmoid(pre)

    specs = _conv_specs()
    specs[1] = pl.BlockSpec((None, 4, CONV_TC), lambda c: (layer, 0, c))
    return pl.pallas_call(
        body, grid=(D_CONV // CONV_TC,), in_specs=specs, out_specs=pl.BlockSpec((S, CONV_TC), lambda c: (0, c)),
        out_shape=_out_hbm(SDS((S, D_CONV), f32)), name="conv_fwd", compiler_params=_cparams(1),
    )(*_in_hbm([xbc, conv_w, conv_b]))


def _conv_bwd(xbc, dact, conv_w, conv_b, dproj, layer):
    def body(u_ref, w_ref, b_ref, da_ref, dproj_in, du_ref, dw_ref, db_ref):
        u = u_ref[...].astype(f32)
        pre = _conv_pre(u, w_ref, b_ref, layer)
        sg = _sigmoid(pre)
        dpre = da_ref[...] * (sg * (1.0 + pre * (1.0 - sg)))
        du = w_ref[3:4, :] * dpre
        for k in range(3):
            du = du + w_ref[k:k + 1, :] * _shift_up(dpre, 3 - k)
        du_ref[...] = du.astype(bf16)
        db_ref[...] = jnp.broadcast_to(jnp.sum(dpre, axis=0, keepdims=True), db_ref.shape)
        dw_ref[...] = jnp.zeros_like(dw_ref)
        for k in range(4):
            dw_ref[k:k + 1, :] = jnp.sum(dpre * _shift_down(u, 3 - k), axis=0, keepdims=True)

    specs = _conv_specs()
    specs[1] = pl.BlockSpec((None, 4, CONV_TC), lambda c: (layer, 0, c))
    col = pl.BlockSpec((S, CONV_TC), lambda c: (0, c))
    row8 = pl.BlockSpec((8, CONV_TC), lambda c: (0, c))
    return pl.pallas_call(
        body, grid=(D_CONV // CONV_TC,), in_specs=[*specs, col, ANY_SPEC],
        out_specs=[pl.BlockSpec((S, CONV_TC), lambda c: (0, COL_XBC // CONV_TC + c)), row8, row8],
        out_shape=_out_hbm([SDS((S, D_IN_PAD), bf16), SDS((8, D_CONV), f32), SDS((8, D_CONV), f32)]), name="conv_bwd",
        input_output_aliases={4: 0}, compiler_params=_cparams(1),
    )(*_in_hbm([xbc, conv_w, conv_b, dact, dproj]))


def _tri():
    return (lax.broadcasted_iota(jnp.int32, (BLK, BLK), 0) >= lax.broadcasted_iota(jnp.int32, (BLK, BLK), 1))


def _ssd_scalars(dt_ref, dtb_ref, alog_ref, layer):
    raw = dt_ref[:, 0:NSSM] + dtb_ref[layer:layer + 1, :]
    dtv = jnp.maximum(raw, 0.0) + jnp.log(1.0 + jnp.exp(-jnp.abs(raw)))
    a = -jnp.exp(alog_ref[layer:layer + 1, :])
    acs = jnp.dot(_tri().astype(f32), dtv * a, preferred_element_type=f32, precision=HIGHEST)
    return raw, dtv, a, acs


HG = NSSM // NGRP
GW = HG * HD


def _lane_expand(cols, g):
    lane_head = lax.broadcasted_iota(jnp.int32, (1, GW), 1) // HD
    out = cols[:, HG * g + HG - 1:HG * g + HG]
    for r in range(HG - 2, -1, -1):
        out = jnp.where(lane_head == r, cols[:, HG * g + r:HG * g + r + 1], out)
    return out


def _row_expand(vals, g):
    row_head = lax.broadcasted_iota(jnp.int32, (GW, 1), 0) // HD
    out = vals[:, HG * g + HG - 1:HG * g + HG]
    for r in range(HG - 2, -1, -1):
        out = jnp.where(row_head == r, vals[:, HG * g + r:HG * g + r + 1], out)
    return out


def _head_rowsums(a, g):
    sel = (lax.broadcasted_iota(jnp.int32, (GW, NSSM), 0) // HD + HG * g == lax.broadcasted_iota(jnp.int32, (GW, NSSM), 1)).astype(bf16)
    hi = a.astype(bf16)
    lo = (a - hi.astype(f32)).astype(bf16)
    return _dot(hi, sel, NN_DIMS) + _dot(lo, sel, NN_DIMS)


def _head_blocksums(v, g):
    sel = (lax.broadcasted_iota(jnp.int32, (GW, NSSM), 0) // HD + HG * g == lax.broadcasted_iota(jnp.int32, (GW, NSSM), 1)).astype(bf16)
    hi = v.astype(bf16)
    lo = (v - hi.astype(f32)).astype(bf16)
    return _dot(hi, sel, TN_DIMS) + _dot(lo, sel, TN_DIMS)


def _ssd_chunk_common(xc_ref, dt_ref, dtb_ref, alog_ref, h_rows, layer):
    raw, dtv, a, acs = _ssd_scalars(dt_ref, dtb_ref, alog_ref, layer)
    acs_t = acs.T
    last = acs[BLK - 1:BLK, :]
    c = dict(raw=raw, dtv=dtv, a=a, acs=acs, last=last, dte=jnp.exp(last - acs), e_all=jnp.exp(acs), cd=jnp.exp(last))
    grp, heads, tri = range(NGRP), range(NSSM), _tri()
    c["bm"] = [xc_ref[:, pl.ds(D_SSM + NSTATE * g, NSTATE)] for g in grp]
    c["bm_b"] = [c["bm"][g].astype(bf16) for g in grp]
    c["cm_b"] = [xc_ref[:, pl.ds(D_SSM + NGRP * NSTATE + NSTATE * g, NSTATE)].astype(bf16) for g in grp]
    c["cb"] = [_dot(c["cm_b"][g], c["bm_b"][g], NT_DIMS) for g in grp]
    c["x"] = [xc_ref[:, pl.ds(GW * g, GW)] for g in grp]
    c["dt"] = [_lane_expand(dtv, g) for g in grp]
    c["xdt"] = [c["x"][g] * c["dt"][g] for g in grp]
    c["xdt_b"] = [c["xdt"][g].astype(bf16) for g in grp]
    c["prev"] = [h_rows(g) for g in grp]
    c["prev_b"] = [c["prev"][g].astype(bf16) for g in grp]
    c["e"] = [_lane_expand(c["e_all"], g) for g in grp]
    c["y_off"] = [_dot(c["cm_b"][g], c["prev_b"][g], NT_DIMS) * c["e"][g] for g in grp]
    c["decay"] = [jnp.exp(jnp.where(tri, acs[:, h:h + 1] - acs_t[h:h + 1, :], -jnp.inf)) for h in heads]
    c["m"] = [c["cb"][h // HG] * c["decay"][h] for h in heads]
    c["m_b"] = [c["m"][h].astype(bf16) for h in heads]
    c["dte_x"] = [_lane_expand(c["dte"], g) for g in grp]
    c["xdte_b"] = [(c["xdt"][g] * c["dte_x"][g]).astype(bf16) for g in grp]
    return c


def _ssd_fwd(xact, z, dt, attn, dt_bias, a_log, d_skip, norm_g, layer):
    def body(xc_ref, z_ref, dt_ref, at_ref, dtb_ref, alog_ref, dsk_ref, ng_ref, mix_ref, hs_ref, y_ref, h_ref):
        n = pl.program_id(0)

        @pl.when(n == 0)
        def _():
            h_ref[...] = jnp.zeros_like(h_ref)

        hs_ref[...] = h_ref[...]
        c = _ssd_chunk_common(xc_ref, dt_ref, dtb_ref, alog_ref, lambda g: h_ref[pl.ds(GW * g, GW), :], layer)
        grp, heads = range(NGRP), range(NSSM)
        y_diag = [_dot(c["m_b"][h], c["xdt_b"][h // HG][:, HD * (h % HG):HD * (h % HG + 1)], NN_DIMS) for h in heads]
        new_st = [_dot(c["xdte_b"][g], c["bm_b"][g], TN_DIMS) for g in grp]
        for h in heads:
            y_ref[:, pl.ds(HD * h, HD)] = y_diag[h]
        dskip = dsk_ref[layer:layer + 1, :]
        for g in grp:
            cols = pl.ds(GW * g, GW)
            y_ref[:, cols] = y_ref[:, cols] + c["y_off"][g] + c["x"][g] * _lane_expand(dskip, g)
            h_ref[cols, :] = c["prev"][g] * _row_expand(c["cd"], g) + new_st[g]
        zv = z_ref[...].astype(f32)
        yz = y_ref[...] * (zv * _sigmoid(zv))
        mix_ref[:, 0:D_ATTN] = at_ref[...]
        for g in grp:
            yg = yz[:, GW * g:GW * (g + 1)]
            rs = lax.rsqrt(jnp.mean(yg * yg, axis=-1, keepdims=True) + EPS)
            mix_ref[:, D_ATTN + GW * g:D_ATTN + GW * (g + 1)] = (yg * rs * ng_ref[layer:layer + 1, GW * g:GW * (g + 1)]).astype(bf16)

    small = lambda shape: pl.BlockSpec(shape, lambda n: (0,) * len(shape))
    return pl.pallas_call(
        body, grid=(NBLK,),
        in_specs=[pl.BlockSpec((BLK, D_CONV), lambda n: (n, 0)), pl.BlockSpec((BLK, D_SSM), lambda n: (n, 0)),
                  pl.BlockSpec((BLK, 128), lambda n: (n, 0)), pl.BlockSpec((BLK, D_ATTN), lambda n: (n, 0)),
                  small((DEPTH, NSSM)), small((DEPTH, NSSM)), small((DEPTH, NSSM)), small((DEPTH, D_SSM))],
        out_specs=[pl.BlockSpec((BLK, D), lambda n: (n, 0)), pl.BlockSpec((None, NSSM * HD, NSTATE), lambda n: (n, 0, 0)),
                   pl.BlockSpec((BLK, D_SSM), lambda n: (n, 0))],
        out_shape=_out_hbm([SDS((S, D), bf16), SDS((NBLK, NSSM * HD, NSTATE), f32), SDS((S, D_SSM), f32)]),
        scratch_shapes=[pltpu.VMEM((NSSM * HD, NSTATE), f32)],
        name="ssd_fwd", compiler_params=_cparams(1),
    )(*_in_hbm([xact, z, dt, attn, dt_bias, a_log, d_skip, norm_g]))


def _ssd_bwd(xact, z, dt, dmix, hs, y, dt_bias, a_log, d_skip, norm_g, dproj, layer):
    def body(xc_ref, z_ref, dt_ref, do_ref, hs_ref, y_ref, dtb_ref, alog_ref, dsk_ref, ng_ref, dproj_in,
             dzdt_ref, dx_ref, dsm_ref, dh_ref, dy_ref):
        i = pl.program_id(0)

        @pl.when(i == 0)
        def _():
            dh_ref[...] = jnp.zeros_like(dh_ref)
            dsm_ref[...] = jnp.zeros_like(dsm_ref)

        c = _ssd_chunk_common(xc_ref, dt_ref, dtb_ref, alog_ref, lambda g: hs_ref[pl.ds(GW * g, GW), :], layer)
        raw, dtv, a = c["raw"], c["dtv"], c["a"]
        grp, heads = range(NGRP), range(NSSM)
        dskip = dsk_ref[layer:layer + 1, :]
        lane8 = lax.broadcasted_iota(jnp.int32, (1, NSSM), 1)
        sub8 = lax.broadcasted_iota(jnp.int32, (NSSM, 1), 0)

        zv = z_ref[...].astype(f32)
        sz = _sigmoid(zv)
        gz = zv * sz
        yv = y_ref[...]
        yz = yv * gz
        for g in grp:
            sl = slice(GW * g, GW * (g + 1))
            yg = yz[:, sl]
            rs = lax.rsqrt(jnp.mean(yg * yg, axis=-1, keepdims=True) + EPS)
            yhat = yg * rs
            dog = do_ref[:, sl]
            w = dog * ng_ref[layer:layer + 1, sl]
            dyz = rs * (w - yhat * jnp.mean(yhat * w, axis=-1, keepdims=True))
            dsm_ref[0:1, sl] += jnp.sum(dog * yhat, axis=0, keepdims=True)
            dy_ref[:, sl] = dyz * gz[:, sl]
            dzdt_ref[:, sl] = (dyz * yv[:, sl] * (sz[:, sl] * (1.0 + zv[:, sl] * (1.0 - sz[:, sl])))).astype(bf16)

        dy = [dy_ref[:, pl.ds(GW * g, GW)] for g in grp]
        dy_b = [dy[g].astype(bf16) for g in grp]
        hl = lambda h: slice(HD * (h % HG), HD * (h % HG + 1))
        dt_off_b = [(dy[g] * c["e"][g]).astype(bf16) for g in grp]
        dcm = [_dot(dt_off_b[g], c["prev_b"][g], NN_DIMS) for g in grp]
        dprev = [_dot(dt_off_b[g], c["cm_b"][g], TN_DIMS) for g in grp]
        yoff_rs = [_head_rowsums(dy[g] * c["y_off"][g], g) for g in grp]
        dhn = [dh_ref[pl.ds(GW * g, GW), :] for g in grp]
        dhn_b = [dhn[g].astype(bf16) for g in grp]
        dprev = [dprev[g] + dhn[g] * _row_expand(c["cd"], g) for g in grp]
        dhn_prev = [dhn[g] * c["prev"][g] for g in grp]
        u = [_dot(c["bm_b"][g], dhn_b[g], NT_DIMS) for g in grp]
        dbm = [_dot(c["xdte_b"][g], dhn_b[g], NN_DIMS) for g in grp]
        ddte_rs = [_head_rowsums(c["xdt"][g] * u[g], g) for g in grp]
        dm = [_dot(dy_b[h // HG][:, hl(h)], c["xdt_b"][h // HG][:, hl(h)], NT_DIMS) for h in heads]
        dxdt_in = [_dot(c["m_b"][h], dy_b[h // HG][:, hl(h)], TN_DIMS) for h in heads]
        dseg = [dm[h] * c["m"][h] for h in heads]
        dmd = [dm[h] * c["decay"][h] for h in heads]
        for h in heads:
            dx_ref[:, pl.ds(HD * h, HD)] = dxdt_in[h]

        tmp = (ddte_rs[0] + ddte_rs[1]) * c["dte"]
        dacs = yoff_rs[0] + yoff_rs[1] - tmp
        dacs_cols = jnp.zeros((NSSM, BLK), f32)
        ddtv = jnp.zeros((BLK, NSSM), f32)
        ddsk = jnp.zeros((BLK, NSSM), f32)
        hp = jnp.zeros((1, NSSM), f32)
        for g in grp:
            cols = pl.ds(GW * g, GW)
            dxdt = dx_ref[:, cols] + u[g] * c["dte_x"][g]
            dx_ref[:, cols] = dy[g] * _lane_expand(dskip, g) + dxdt * c["dt"][g]
            ddtv = ddtv + _head_rowsums(dxdt * c["x"][g], g)
            ddsk = ddsk + _head_rowsums(dy[g] * c["x"][g], g)
            dcb = dmd[HG * g]
            for r in range(1, HG):
                dcb = dcb + dmd[HG * g + r]
            dcb_b = dcb.astype(bf16)
            dx_ref[:, pl.ds(D_SSM + NSTATE * g, NSTATE)] = dbm[g] + _dot(dcb_b, c["cm_b"][g], TN_DIMS)
            dx_ref[:, pl.ds(D_SSM + NGRP * NSTATE + NSTATE * g, NSTATE)] = dcm[g] + _dot(dcb_b, c["bm_b"][g], NN_DIMS)
            dh_ref[cols, :] = dprev[g]
            hp = hp + _head_blocksums(jnp.sum(dhn_prev[g], axis=1, keepdims=True), g)
            for r in range(HG):
                h = HG * g + r
                dacs = dacs + (lane8 == h).astype(f32) * jnp.sum(dseg[h], axis=1, keepdims=True)
                dacs_cols = dacs_cols + (sub8 == h).astype(f32) * jnp.sum(dseg[h], axis=0, keepdims=True)
        dlast = hp * c["cd"] + jnp.sum(tmp, axis=0, keepdims=True)
        ddsk = jnp.sum(ddsk, axis=0, keepdims=True)

        row = lax.broadcasted_iota(jnp.int32, (BLK, 1), 0)
        dacs = dacs - dacs_cols.T + jnp.where(row == BLK - 1, dlast, 0.0)
        dda = lax.dot_general(_tri().astype(f32), dacs, TN_DIMS, preferred_element_type=f32, precision=HIGHEST)
        ddtv = ddtv + dda * a
        da = jnp.sum(dda * dtv, axis=0, keepdims=True)
        draw = ddtv * _sigmoid(raw)
        dzdt_ref[:, D_SSM:] = jnp.zeros((BLK, COL_XBC - COL_DT), bf16)
        dzdt_ref[:, D_SSM:D_SSM + NSSM] = draw.astype(bf16)
        dsm_ref[1:2, 0:NSSM] += jnp.sum(draw, axis=0, keepdims=True)
        dsm_ref[2:3, 0:NSSM] += da * a
        dsm_ref[3:4, 0:NSSM] += ddsk

    rev = lambda i: NBLK - 1 - i
    small = lambda shape: pl.BlockSpec(shape, lambda i: (0,) * len(shape))
    return pl.pallas_call(
        body, grid=(NBLK,),
        in_specs=[pl.BlockSpec((BLK, D_CONV), lambda i: (rev(i), 0)), pl.BlockSpec((BLK, D_SSM), lambda i: (rev(i), 0)),
                  pl.BlockSpec((BLK, 128), lambda i: (rev(i), 0)), pl.BlockSpec((BLK, D_SSM), lambda i: (rev(i), 1)),
                  pl.BlockSpec((None, NSSM * HD, NSTATE), lambda i: (rev(i), 0, 0)), pl.BlockSpec((BLK, D_SSM), lambda i: (rev(i), 0)),
                  small((DEPTH, NSSM)), small((DEPTH, NSSM)), small((DEPTH, NSSM)), small((DEPTH, D_SSM)), ANY_SPEC],
        out_specs=[pl.BlockSpec((BLK, COL_XBC - COL_Z), lambda i: (rev(i), COL_Z // (COL_XBC - COL_Z))),
                   pl.BlockSpec((BLK, D_CONV), lambda i: (rev(i), 0)), small((8, D_SSM))],
        out_shape=_out_hbm([SDS((S, D_IN_PAD), bf16), SDS((S, D_CONV), f32), SDS((8, D_SSM), f32)]),
        scratch_shapes=[pltpu.VMEM((NSSM * HD, NSTATE), f32), pltpu.VMEM((BLK, D_SSM), f32)],
        name="ssd_bwd", input_output_aliases={10: 0}, compiler_params=_cparams(1),
    )(*_in_hbm([xact, z, dt, dmix, hs, y, dt_bias, a_log, d_skip, norm_g, dproj]))


def _my_place():
    return lax.axis_index("x"), lax.axis_index("y"), lax.axis_index("c")


def _dev_index(px, py, pc):
    return 4 * px + 2 * py + pc


def _slab2(kind, ref, idx):
    if kind == "stack":
        return ref.at[idx]
    if kind == "rows128":
        return ref.at[pl.ds(pl.multiple_of(idx * 128, 128), 128), :]
    if kind == "rows512":
        return ref.at[pl.ds(pl.multiple_of(idx * 512, 512), 512), :]
    return ref.at[:, pl.ds(pl.multiple_of(idx * 512, 512), 512)]


def _slab_shape(kind, full_shape):
    if kind == "stack":
        return tuple(full_shape[1:])
    if kind == "rows128":
        return (128, full_shape[1])
    if kind == "rows512":
        return (512, full_shape[1])
    return (full_shape[0], 512)


KIND = dict(w_in="stack", w_out="rows128", w_up="cols512", w_down="rows512", conv_w="stack")
FULL_SHAPE = dict(w_in=(N_DEV, D, D_IN // N_DEV), w_out=(D, D), w_up=(D, D_FF), w_down=(D_FF, D))
HBM_SPEC = pl.BlockSpec(memory_space=pltpu.HBM)
SEM_SPEC = pl.BlockSpec(memory_space=pltpu.SEMAPHORE)
SIDE_EFFECT = pltpu.SideEffectType.DATAFLOW_SIDE_EFFECTING


def _peers_all():
    x, y, c = _my_place()
    return [(x ^ ((r >> 2) & 1), y ^ ((r >> 1) & 1), c ^ (r & 1)) for r in range(1, N_DEV)]


def _split_start(name, bufs, n_copies, plan, deps=()):
    nb = len(bufs)

    def body(*refs):
        ins = refs[:nb]
        send_sems, recv_sems = refs[nb + len(deps)], refs[nb + len(deps) + 1]
        token = refs[-1]
        for i, (src, dst, dev) in enumerate(plan(ins)):
            pltpu.make_async_remote_copy(src_ref=src, dst_ref=dst, send_sem=send_sems.at[i], recv_sem=recv_sems.at[i],
                                         device_id=dev, device_id_type=MESH).start()
        token[...] = jnp.zeros_like(token)

    outs = pl.pallas_call(
        body, name=name,
        out_shape=(pltpu.SemaphoreType.DMA((n_copies,)), pltpu.SemaphoreType.DMA((n_copies,)),
                   *[pltpu.HBM(b.shape, b.dtype) for b in bufs], SDS((8, 128), f32)),
        in_specs=[HBM_SPEC] * nb + [ANY_SPEC] * len(deps),
        out_specs=(SEM_SPEC, SEM_SPEC, *[HBM_SPEC] * nb, pl.BlockSpec(memory_space=pltpu.VMEM)),
        input_output_aliases={i: 2 + i for i in range(nb)},
        compiler_params=pltpu.CompilerParams(has_side_effects=SIDE_EFFECT),
    )(*[pltpu.with_memory_space_constraint(b, pltpu.HBM) for b in bufs], *deps)
    return dict(send=outs[0], recv=outs[1], bufs=list(outs[2:2 + nb]), token=outs[-1], plan=plan, n=n_copies)


def _split_wait(name, started, after):
    bufs = started["bufs"]
    nb = len(bufs)
    plan = started["plan"]
    first = started.get("first", 0)

    def body(*refs):
        ins = refs[:nb]
        send_sems, recv_sems = refs[nb], refs[nb + 1]
        for i, (src, dst, dev) in enumerate(plan(ins)):
            cp = pltpu.make_async_remote_copy(src_ref=src, dst_ref=dst, send_sem=send_sems.at[first + i],
                                              recv_sem=recv_sems.at[first + i], device_id=dev, device_id_type=MESH)
            cp.wait_send()
            cp.wait_recv()

    outs = pl.pallas_call(
        body, name=name, out_shape=tuple(pltpu.HBM(b.shape, b.dtype) for b in bufs),
        in_specs=[HBM_SPEC] * nb + [SEM_SPEC, SEM_SPEC] + [ANY_SPEC] * len(after), out_specs=(HBM_SPEC,) * nb,
        input_output_aliases={i: i for i in range(nb)},
        compiler_params=pltpu.CompilerParams(has_side_effects=SIDE_EFFECT),
    )(*bufs, started["send"], started["recv"], *after)
    return list(outs)


def _gather_plan(names):
    def plan(refs):
        x, y, c = _my_place()
        my_idx = _dev_index(x, y, c)
        targets = [(x, y, 1 - c), (1 - x, y, c), (x, 1 - y, c), (1 - x, 1 - y, c)]
        slabs = [_slab2(KIND[n], refs[t], my_idx) for t, n in enumerate(names)]
        return [(slab, slab, dev) for slab in slabs for dev in targets]

    return plan


def _gather_start(name, names, fulls, deps):
    return _split_start(name, list(fulls), 4 * len(names), _gather_plan(names), deps)


def _gather_part(started, names, lo):
    return dict(send=started["send"], recv=started["recv"], bufs=started["bufs"][lo:lo + len(names)],
                plan=_gather_plan(names), first=4 * lo)


def _gather_finish(name, names, started, after):
    n_t = len(names)
    fulls = _split_wait(name + "_wait", started, after)
    slab_shapes = [SDS(_slab_shape(KIND[n], f.shape), f.dtype) for n, f in zip(names, fulls)]

    def body(*refs):
        ins = refs[:n_t]
        outs = refs[n_t:2 * n_t]
        stage = refs[2 * n_t:3 * n_t]
        load_sems, send_sems, recv_sems = refs[3 * n_t:]
        x, y, c = _my_place()
        chips = [(1 - x, y), (x, 1 - y), (1 - x, 1 - y)]
        pairs = [(t, j) for t in range(n_t) for j in range(3)]
        loads = [pltpu.make_async_copy(_slab2(KIND[names[t]], ins[t], _dev_index(*chips[j], c)), stage[t].at[j], load_sems.at[t, j])
                 for t, j in pairs]
        for cp in loads:
            cp.start()

        def copy(t, j, core):
            return pltpu.make_async_remote_copy(
                src_ref=stage[t].at[j], dst_ref=_slab2(KIND[names[t]], outs[t], _dev_index(*chips[j], core)),
                send_sem=send_sems.at[t, j], recv_sem=recv_sems.at[t, j], device_id=(x, y, 1 - c), device_id_type=MESH)

        sends = [copy(t, j, c) for t, j in pairs]
        for ld, cp in zip(loads, sends):
            ld.wait()
            cp.start()
        for t, j in pairs:
            copy(t, j, 1 - c).wait_recv()
        for cp in sends:
            cp.wait_send()

    return pl.pallas_call(
        body, in_specs=[ANY_SPEC] * n_t, out_specs=[ANY_SPEC] * n_t, out_shape=[SDS(b.shape, b.dtype) for b in fulls],
        input_output_aliases={t: t for t in range(n_t)},
        scratch_shapes=[pltpu.VMEM((3,) + s.shape, s.dtype) for s in slab_shapes]
        + [pltpu.SemaphoreType.DMA((n_t, 3)), pltpu.SemaphoreType.DMA((n_t, 3)), pltpu.SemaphoreType.DMA((n_t, 3))],
        name=name + "_pass", compiler_params=pltpu.CompilerParams(vmem_limit_bytes=VMEM_LIMIT),
    )(*fulls)


def _exchange_start(name, names, grads, deps):
    n_t = len(names)
    lands = [lax.empty((N_DEV,) + _slab_shape(KIND[n], g.shape), g.dtype) for n, g in zip(names, grads)]

    def plan(refs):
        my_idx = _dev_index(*_my_place())
        return [(_slab2(KIND[names[t]], refs[t], _dev_index(*peer)), refs[n_t + t].at[my_idx], peer)
                for t in range(n_t) for peer in _peers_all()]

    return _split_start(name, list(grads) + lands, 7 * n_t, plan, deps)


def _small_exchange_start(part, deps):
    land = lax.empty((N_DEV,) + part.shape, part.dtype)

    def plan(refs):
        my_idx = _dev_index(*_my_place())
        return [(refs[0], refs[1].at[my_idx], peer) for peer in _peers_all()]

    return _split_start("small_exchange", [part, land], N_DEV - 1, plan, deps)


def _slab_pieces():
    sh = D_IN // N_DEV
    out = []
    for j in range(N_DEV):
        for first, end, dst in IN_SEGMENTS:
            lo, hi = max(first, sh * j), min(end, sh * (j + 1))
            if lo < hi:
                out.append((j, lo - sh * j, hi - sh * j, dst + lo - first))
    return out


def _w_in_assemble(stacked):
    tr = 256
    sh = D_IN // N_DEV

    def body(i_ref, o_ref):
        o_ref[:, COL_DT:COL_XBC] = jnp.zeros((tr, COL_XBC - COL_DT), bf16)
        for j, lo, hi, dst in _slab_pieces():
            o_ref[:, dst:dst + hi - lo] = i_ref[j, :, lo:hi]

    return pl.pallas_call(
        body, grid=(D // tr,), in_specs=[pl.BlockSpec((N_DEV, tr, sh), lambda i: (0, i, 0))],
        out_specs=pl.BlockSpec((None, tr, D_IN_PAD), lambda i: (0, i, 0)), out_shape=_out_hbm(SDS((1, D, D_IN_PAD), bf16)),
        name="w_in_assemble", compiler_params=_cparams(1),
    )(*_in_hbm([stacked]))


def _w_in_slabs(dw_in):
    tr = 256
    sh = D_IN // N_DEV

    def body(i_ref, o_ref):
        for j, lo, hi, src in _slab_pieces():
            o_ref[j, :, lo:hi] = i_ref[:, src:src + hi - lo]

    return pl.pallas_call(
        body, grid=(D // tr,), in_specs=[pl.BlockSpec((tr, D_IN_PAD), lambda i: (i, 0))],
        out_specs=pl.BlockSpec((N_DEV, tr, sh), lambda i: (0, i, 0)), out_shape=_out_hbm(SDS((N_DEV, D, sh), bf16)),
        name="w_in_slabs", compiler_params=_cparams(1),
    )(*_in_hbm([dw_in]))


SMALL_NAMES = ("mix_norm_g", "mlp_norm_g", "conv_b", "ssm_norm_g", "q_gain", "k_gain", "sinks", "dt_bias", "a_log", "d_skip",
               "rel_bias", "conv_w")
MISC_LANES = dict(q_gain=(LANE_QG, HD), k_gain=(LANE_KG, HD), sinks=(LANE_SINK, NQ), dt_bias=(LANE_DTB, NSSM),
                  a_log=(LANE_ALOG, NSSM), d_skip=(LANE_DSKIP, NSSM))


def _pack_small_grads(smalls, drel_t, loss):
    def body(*refs):
        o_ref = refs[-1]
        drel_ref, loss_ref = refs[-3], refs[-2]
        o_ref[...] = jnp.zeros_like(o_ref)
        for l in range(DEPTH):
            mixg, mlpg, convb, convw, ssd, attn = refs[6 * l:6 * l + 6]
            o_ref[ROW_MIXG + l:ROW_MIXG + l + 1, :] = mixg[...]
            o_ref[ROW_MLPG + l:ROW_MLPG + l + 1, :] = mlpg[...]
            o_ref[ROW_CONVB + l:ROW_CONVB + l + 1, :] = convb[0:1, :]
            o_ref[ROW_SSMG + l:ROW_SSMG + l + 1, 0:D_SSM] = ssd[0:1, :]
            o_ref[ROW_CONVW + 4 * l:ROW_CONVW + 4 * l + 4, :] = convw[0:4, :]
            row = slice(ROW_MISC + l, ROW_MISC + l + 1)
            o_ref[row, LANE_QG:LANE_QG + HD] = attn[0:1, 0:HD]
            o_ref[row, LANE_KG:LANE_KG + HD] = attn[1:2, 0:HD]
            o_ref[row, LANE_SINK:LANE_SINK + NQ] = attn[2:3, 0:NQ]
            o_ref[row, LANE_DTB:LANE_DTB + NSSM] = ssd[1:2, 0:NSSM]
            o_ref[row, LANE_ALOG:LANE_ALOG + NSSM] = ssd[2:3, 0:NSSM]
            o_ref[row, LANE_DSKIP:LANE_DSKIP + NSSM] = ssd[3:4, 0:NSSM]
        o_ref[ROW_RELB:ROW_RELB + NQ, 0:N_BUCKETS] = drel_ref[...]
        o_ref[ROW_LOSS:ROW_LOSS + 1, 0:1] = loss_ref[0:1, 0:1]

    args = []
    for sm in smalls:
        args += [sm["mix_norm_g"], sm["mlp_norm_g"], sm["conv_b"], sm["conv_w"], sm["ssd"], sm["attn"]]
    args += [drel_t, loss]
    return pl.pallas_call(body, out_shape=SDS((SMALL_ROWS, D), f32), name="pack_small_grads")(*args)


def _adamw_small(part, land, w, m, v):
    n = len(SMALL_NAMES)

    def grad_of(name, g_ref):
        if name == "mix_norm_g":
            return g_ref[ROW_MIXG:ROW_MIXG + DEPTH, :]
        if name == "mlp_norm_g":
            return g_ref[ROW_MLPG:ROW_MLPG + DEPTH, :]
        if name == "conv_b":
            return g_ref[ROW_CONVB:ROW_CONVB + DEPTH, :]
        if name == "ssm_norm_g":
            return g_ref[ROW_SSMG:ROW_SSMG + DEPTH, 0:D_SSM]
        if name == "rel_bias":
            return g_ref[ROW_RELB:ROW_RELB + NQ, 0:N_BUCKETS].T
        lane, width = MISC_LANES[name]
        return g_ref[ROW_MISC:ROW_MISC + DEPTH, lane:lane + width]

    def body(part_ref, land_ref, *refs):
        ws, ms, vs = refs[:n], refs[n:2 * n], refs[2 * n:3 * n]
        loss_ref = refs[3 * n]
        outs = refs[3 * n + 1:-1]
        g_ref = refs[-1]
        me = _dev_index(*_my_place())
        for p in range(N_DEV):
            term = jnp.where(me == p, part_ref[...], land_ref[p])
            if p == 0:
                g_ref[...] = term
            else:
                g_ref[...] += term
        loss_ref[...] = g_ref[ROW_LOSS:ROW_LOSS + 1, 0:128]
        my_cols = pl.ds(pl.multiple_of(me * 128, 128), 128)
        for k, name in enumerate(SMALL_NAMES):
            g_out, d_out, m_out, v_out = outs[4 * k:4 * k + 4]
            if name == "conv_w":
                for l in range(DEPTH):
                    g = g_ref[ROW_CONVW + 4 * l:ROW_CONVW + 4 * l + 4, my_cols]
                    delta, m_new, v_new = _adamw_math(ws[k][l], ms[k][l], vs[k][l], g)
                    g_out[l], d_out[l], m_out[l], v_out[l] = g, delta, m_new, v_new
            else:
                g = grad_of(name, g_ref)
                delta, m_new, v_new = _adamw_math(ws[k][...], ms[k][...], vs[k][...], g)
                g_out[...], d_out[...], m_out[...], v_out[...] = g, delta, m_new, v_new

    ws = [w[name] for name in SMALL_NAMES]
    out_shape = [SDS((1, 128), f32)]
    for a in ws:
        out_shape += [SDS(a.shape, f32)] * 4
    return pl.pallas_call(body, out_shape=out_shape, name="adamw_small", scratch_shapes=[pltpu.VMEM((SMALL_ROWS, D), f32)])(
        part, land, *ws, *[m[name] for name in SMALL_NAMES], *[v[name] for name in SMALL_NAMES])


def _plain(tm, tn):
    return pl.BlockSpec((tm, tn), lambda i, j, k: (i, j))


def _rowblk(tm, width):
    return pl.BlockSpec((tm, width), lambda i, j, k: (i, 0))


def _store_epi(dtype):
    def epi(acc, i, j, ex, outs):
        outs[0][...] = acc.astype(dtype)
    return epi


def _rms_prologue(layer):
    def pro(a_ref, ex, outs):
        xv = a_ref[...]
        r = lax.rsqrt(jnp.mean(xv * xv, axis=-1, keepdims=True) + EPS)
        h = (xv * r * ex[0][layer:layer + 1, :]).astype(bf16)
        outs[-1][...] = h
        return h
    return pro


MLP_TM = 256
MLP_VMEM = 56 * 1024 * 1024


def _resident(shape):
    return pl.BlockSpec((None,) + shape, lambda i: (0, 0, 0), pipeline_mode=pl.Buffered(1))


def _mlp_fwd(layer, x, mix, g, w_out, w_up, w_down, tgt=None):
    tm = MLP_TM
    with_loss = tgt is not None

    def body(x_ref, mix_ref, g_ref, wo_ref, wu_ref, wd_ref, *rest):
        xm_ref, a_ref, r_ref, h_ref = rest[with_loss:with_loss + 4]
        rest = rest[:with_loss] + rest[with_loss + 1:]
        i = pl.program_id(0)
        xv = x_ref[...] + _dot(mix_ref[...], wo_ref[...], NN_DIMS)
        xm_ref[...] = xv
        h = (xv * lax.rsqrt(jnp.mean(xv * xv, axis=-1, keepdims=True) + EPS) * g_ref[layer:layer + 1, :]).astype(bf16)
        h_ref[...] = h
        r = jnp.maximum(_dot(h, wu_ref[...], NN_DIMS), 0.0)
        a = (r * r).astype(bf16)
        a_ref[...] = a
        r_ref[...] = r.astype(bf16)
        y = xv + _dot(a, wd_ref[...], NN_DIMS)
        if not with_loss:
            rest[3][...] = y
            return
        err = y - rest[0][...]
        rest[4][...] = err * (1.0 / D)
        part = 0.5 * jnp.sum(jnp.mean(err * err, axis=-1, keepdims=True), axis=0, keepdims=True)

        @pl.when(i == 0)
        def _():
            rest[5][...] = jnp.zeros_like(rest[5])

        rest[5][...] += jnp.broadcast_to(part, rest[5].shape)

    row = lambda width: pl.BlockSpec((tm, width), lambda i: (i, 0))
    in_specs = [row(D), row(D), pl.BlockSpec((DEPTH, D), lambda i: (0, 0)), _resident((D, D)), _resident((D, D_FF)),
                _resident((D_FF, D))]
    out_specs = [row(D), row(D_FF), row(D_FF), row(D), row(D)]
    out_shape = [SDS((S, D), f32), SDS((S, D_FF), bf16), SDS((S, D_FF), bf16), SDS((S, D), bf16), SDS((S, D), f32)]
    args = [x, mix, g, w_out, w_up, w_down]
    if with_loss:
        in_specs.append(row(D))
        args.append(tgt)
        out_specs.append(pl.BlockSpec((1, 128), lambda i: (0, 0)))
        out_shape.append(SDS((1, 128), f32))
    return pl.pallas_call(
        body, grid=(S // tm,), in_specs=in_specs, out_specs=out_specs, out_shape=_out_hbm(out_shape),
        name="mlp_fwd_loss" if with_loss else "mlp_fwd",
        compiler_params=pltpu.CompilerParams(dimension_semantics=("arbitrary",), vmem_limit_bytes=MLP_VMEM),
    )(*_in_hbm(args[:3]), *args[3:6], *_in_hbm(args[6:]))


def _mlp_bwd_act(layer, dx_out, r_act, x_mid, g, w_down, w_up, w_out, deps):
    tm = MLP_TM

    def body(dxo_ref, r_ref, xm_ref, g_ref, wd_ref, wu_ref, wo_ref, *rest):
        du_ref, dx_ref, dg_ref, dmix_ref = rest[len(deps):]
        dxo = dxo_ref[...]
        du = (_dot(dxo.astype(bf16), wd_ref[...], NT_DIMS) * (2.0 * r_ref[...].astype(f32))).astype(bf16)
        du_ref[...] = du
        dh = _dot(du, wu_ref[...], NT_DIMS)
        _rms_bwd_epilogue(layer)(dh, pl.program_id(0), 0, (xm_ref, g_ref, dxo_ref), (dx_ref, dg_ref))
        dmix_ref[...] = _dot(dx_ref[...].astype(bf16), wo_ref[...], NT_DIMS)

    row = lambda width: pl.BlockSpec((tm, width), lambda i: (i, 0))
    return pl.pallas_call(
        body, grid=(S // tm,),
        in_specs=[row(D), row(D_FF), row(D), pl.BlockSpec((DEPTH, D), lambda i: (0, 0)), _resident((D_FF, D)), _resident((D, D_FF)),
                  _resident((D, D))] + [ANY_SPEC] * len(deps),
        out_specs=[row(D_FF), row(D), pl.BlockSpec((1, D), lambda i: (0, 0)), row(D)],
        out_shape=_out_hbm([SDS((S, D_FF), bf16), SDS((S, D), f32), SDS((1, D), f32), SDS((S, D), f32)]), name="mlp_bwd_act",
        compiler_params=pltpu.CompilerParams(dimension_semantics=("arbitrary",), vmem_limit_bytes=MLP_VMEM),
    )(*_in_hbm([dx_out, r_act, x_mid, g]), w_down, w_up, w_out, *_in_hbm(deps))


def _layer_fwd(l, x, p, get_weights, bias, tgt=None):
    wts = get_weights(l, "in", [x, bias])
    gfull = pl.BlockSpec((DEPTH, D), lambda i, j, k: (0, 0))
    tm = 512

    def inproj_epi(acc, i, j, ex, outs):
        outs[0][...] = acc[:, COL_QKV:COL_Z].astype(bf16)
        outs[1][...] = acc[:, COL_Z:COL_DT].astype(bf16)
        outs[2][...] = acc[:, COL_XBC:D_IN_PAD].astype(bf16)
        outs[3][...] = acc[:, COL_DT:COL_DT + 128]

    qkv, z, xbc, dt, h1 = _matmul(
        "in_proj", "nn", x, wts["w_in"], tm=tm, tn=D_IN_PAD, tk=D, prologue=_rms_prologue(l),
        extras=(p["mix_norm_g"],), extra_specs=(gfull,),
        out_shape=[SDS((S, 768), bf16), SDS((S, 512), bf16), SDS((S, 1024), bf16), SDS((S, 128), f32), SDS((S, D), bf16)],
        out_specs=[_rowblk(tm, 768), _rowblk(tm, 512), _rowblk(tm, 1024), _rowblk(tm, 128), _rowblk(tm, D)], epilogue=inproj_epi)
    attn = _attn_fwd(qkv, p["q_gain"], p["k_gain"], p["sinks"], bias, l)
    xact = _conv_fwd(xbc, wts["conv_w"], p["conv_b"], l)
    mix, hs, y_ssd = _ssd_fwd(xact, z, dt, attn, p["dt_bias"], p["a_log"], p["d_skip"], p["ssm_norm_g"], l)
    wts = dict(wts, **get_weights(l, "rest", [mix]))

    x_mid, a_act, r_act, h2, *result = _mlp_fwd(l, x, mix, p["mlp_norm_g"], wts["w_out"], wts["w_up"], wts["w_down"], tgt)
    saved = dict(x=x, h1=h1, qkv=qkv, z=z, xbc=xbc, dt=dt, xact=xact, mix=mix, hs=hs, y_ssd=y_ssd, x_mid=x_mid, h2=h2,
                 a=a_act, r=r_act, wts=wts)
    return (result[0] if tgt is None else tuple(result)), saved


def _layer_bwd(l, dx_out, sv, p, bias, deps, send):
    wts = sv["wts"]

    dw_down = _matmul("dw_down", "tn", sv["a"], dx_out, tm=1024, tn=D, tk=S, out_shape=SDS((D_FF, D), bf16),
                      out_specs=_plain(1024, D), epilogue=_store_epi(bf16), deps=deps)
    deps = send(l, dict(w_down=dw_down))
    du, dx_mid, dg_mlp, dmix = _mlp_bwd_act(l, dx_out, sv["r"], sv["x_mid"], p["mlp_norm_g"], wts["w_down"], wts["w_up"],
                                            wts["w_out"], deps)
    dw_up = _matmul("dw_up", "tn", sv["h2"], du, tm=D, tn=1024, tk=S, out_shape=SDS((D, D_FF), bf16),
                    out_specs=_plain(D, 1024), epilogue=_store_epi(bf16))
    dw_out = _matmul("dw_out", "tn", sv["mix"], dx_mid, tm=D, tn=512, tk=S, out_shape=SDS((D, D), bf16),
                     out_specs=_plain(D, 512), epilogue=_store_epi(bf16))
    deps = send(l, dict(w_up=dw_up, w_out=dw_out))
    gfull = pl.BlockSpec((DEPTH, D), lambda i, j, k: (0, 0))
    grow = pl.BlockSpec((1, D), lambda i, j, k: (0, 0))
    dproj, dbias, dsm_attn = _attn_bwd(sv["qkv"], dmix, p["q_gain"], p["k_gain"], p["sinks"], bias, l, deps)
    dproj, dxact, dsm_ssd = _ssd_bwd(sv["xact"], sv["z"], sv["dt"], dmix, sv["hs"], sv["y_ssd"], p["dt_bias"], p["a_log"],
                                     p["d_skip"], p["ssm_norm_g"], dproj, l)
    dproj, dconv_w, dconv_b = _conv_bwd(sv["xbc"], dxact, wts["conv_w"], p["conv_b"], dproj, l)
    dw_in = _matmul("dw_in", "tn", sv["h1"], dproj, tm=D, tn=1280, tk=S, out_shape=SDS((D, D_IN_PAD), bf16),
                    out_specs=_plain(D, 1280), epilogue=_store_epi(bf16))
    deps = send(l, dict(w_in=_w_in_slabs(dw_in)))
    dx, dg_mix = _matmul(
        "in_proj_dh", "nt", dproj, wts["w_in"], tm=512, tn=D, tk=D_IN_PAD, out_shape=[SDS((S, D), f32), SDS((1, D), f32)],
        out_specs=[_plain(512, D), grow], epilogue=_rms_bwd_epilogue(l),
        extras=(sv["x"], p["mix_norm_g"], dx_mid), extra_specs=(_plain(512, D), gfull, _plain(512, D)), deps=deps)
    small = dict(mix_norm_g=dg_mix, mlp_norm_g=dg_mlp, conv_w=dconv_w, conv_b=dconv_b, ssd=dsm_ssd, attn=dsm_attn, dbias=dbias)
    return dx, small, deps


def _local_step(x, tgt, p, get_weights, send):
    onehot_t = jnp.asarray(_onehot_buckets(), dtype=bf16)
    bias = _bias_build(p["rel_bias"].T, onehot_t).reshape(NQ, BLK, 2 * BLK)
    saved = []
    h = x
    for l in range(DEPTH):
        h, sv = _layer_fwd(l, h, p, get_weights, bias, tgt if l == DEPTH - 1 else None)
        saved.append(sv)
    dx, loss = h
    smalls = [None] * DEPTH
    deps = ()
    for l in reversed(range(DEPTH)):
        dx, smalls[l], deps = _layer_bwd(l, dx, saved[l], p, bias, deps, send)
    drel_t = _bias_grad(smalls[0]["dbias"].reshape(NQ, -1), smalls[1]["dbias"].reshape(NQ, -1), onehot_t)
    return dx, _pack_small_grads(smalls, drel_t, loss)


WEIGHT_ORDER = ("mix_norm_g", "w_in", "q_gain", "k_gain", "sinks", "rel_bias", "conv_w", "conv_b", "dt_bias", "a_log", "d_skip",
                "ssm_norm_g", "w_out", "mlp_norm_g", "w_up", "w_down")


def kernel(x, mix_norm_g, w_in, q_gain, k_gain, sinks, rel_bias, conv_w, conv_b, dt_bias, a_log, d_skip, ssm_norm_g, w_out, mlp_norm_g, w_up, w_down, loss_target, m_mix_norm_g, m_w_in, m_q_gain, m_k_gain, m_sinks, m_rel_bias, m_conv_w, m_conv_b, m_dt_bias, m_a_log, m_d_skip, m_ssm_norm_g, m_w_out, m_mlp_norm_g, m_w_up, m_w_down, v_mix_norm_g, v_w_in, v_q_gain, v_k_gain, v_sinks, v_rel_bias, v_conv_w, v_conv_b, v_dt_bias, v_a_log, v_d_skip, v_ssm_norm_g, v_w_out, v_mlp_norm_g, v_w_up, v_w_down):
    w = dict(mix_norm_g=mix_norm_g, w_in=w_in, q_gain=q_gain, k_gain=k_gain, sinks=sinks, rel_bias=rel_bias, conv_w=conv_w,
             conv_b=conv_b, dt_bias=dt_bias, a_log=a_log, d_skip=d_skip, ssm_norm_g=ssm_norm_g, w_out=w_out,
             mlp_norm_g=mlp_norm_g, w_up=w_up, w_down=w_down)
    m = dict(mix_norm_g=m_mix_norm_g, w_in=m_w_in, q_gain=m_q_gain, k_gain=m_k_gain, sinks=m_sinks, rel_bias=m_rel_bias,
             conv_w=m_conv_w, conv_b=m_conv_b, dt_bias=m_dt_bias, a_log=m_a_log, d_skip=m_d_skip, ssm_norm_g=m_ssm_norm_g,
             w_out=m_w_out, mlp_norm_g=m_mlp_norm_g, w_up=m_w_up, w_down=m_w_down)
    v = dict(mix_norm_g=v_mix_norm_g, w_in=v_w_in, q_gain=v_q_gain, k_gain=v_k_gain, sinks=v_sinks, rel_bias=v_rel_bias,
             conv_w=v_conv_w, conv_b=v_conv_b, dt_bias=v_dt_bias, a_log=v_a_log, d_skip=v_d_skip, ssm_norm_g=v_ssm_norm_g,
             w_out=v_w_out, mlp_norm_g=v_mlp_norm_g, w_up=v_w_up, w_down=v_w_down)
    big = ("w_in", "w_out", "w_up", "w_down")

    my_idx = _dev_index(*_my_place()).astype(jnp.int32).reshape(1)

    fulls = {n: _cast_to_full("cast_" + n, w[n], KIND[n], FULL_SHAPE[n], my_idx, bf16) for n in big}
    conv_full = _cast_to_full("cast_conv_w", conv_w.reshape(1, DEPTH * 4, 128), "stack", (N_DEV, DEPTH * 4, 128), my_idx, f32)[0]
    rest = ["w_out", "w_up", "w_down"]
    g0 = _gather_start("gather0", ["w_in", "conv_w"], [fulls["w_in"][0], conv_full], ())
    later = _gather_start("gather_later", rest + ["w_in"] + rest,
                          [fulls[n][0] for n in rest] + [fulls["w_in"][1]] + [fulls[n][1] for n in rest], (g0["token"],))
    g1, g2, g3 = _gather_part(later, rest, 0), _gather_part(later, ["w_in"], 3), _gather_part(later, rest, 4)
    held = {}
    flat = lambda a: a.reshape(a.shape[0] * a.shape[1], a.shape[2])
    adam_in = {n: (flat(w[n]), flat(m[n]), flat(v[n])) for n in big}

    def get_weights(l, part, after):
        if l == 0 and part == "in":
            full_in, full_conv = _gather_finish("gather0", ["w_in", "conv_w"], g0,
                                                list(after) + [later["token"], adam_in["w_in"][1], adam_in["w_in"][2]])
            held["conv_w"] = jnp.transpose(full_conv.reshape(N_DEV, DEPTH, 4, 128), (1, 2, 0, 3)).reshape(DEPTH, 4, D_CONV)
            return dict(w_in=_w_in_assemble(full_in), conv_w=held["conv_w"])
        if part == "in":
            return dict(w_in=_w_in_assemble(_gather_finish("gather2", ["w_in"], g2, after)[0]), conv_w=held["conv_w"])
        full = _gather_finish("gather1" if l == 0 else "gather3", rest, g1 if l == 0 else g3, after)
        return {n: f[None] for n, f in zip(rest, full)}

    pending = []

    def send(l, grads):
        names = list(grads)
        started = _exchange_start("exchange%d_%s" % (l, names[0]), names, [grads[n] for n in names], ())
        pending.append((l, names, started))
        return (started["token"],)

    dx, small_part = _local_step(x.reshape(S, D), loss_target.reshape(S, D), w, get_weights, send)

    small = _small_exchange_start(small_part, ())
    tiles = dict(w_in=512, w_out=128, w_up=512, w_down=256)
    outs_of = {n: None for n in big}
    after = [dx, small["token"]]
    for l, names, started in pending:
        bufs = _split_wait("exchange%d_%s_wait" % (l, names[0]), started, after)
        for t, n in enumerate(names):
            outs_of[n] = _adamw_layer("adamw_%s%d" % (n, l), KIND[n], l, *adam_in[n],
                                      bufs[len(names) + t], bufs[t], my_idx, outs_of[n], tiles[n])
        after = [outs_of[names[-1]][0]]
    res = {n: [o.reshape(w[n].shape) for o in outs_of[n]] for n in big}
    small_part, small_land = _split_wait("small_exchange_wait", small, after)
    small_outs = _adamw_small(small_part, small_land, w, m, v)
    loss = small_outs[0][0, 0]
    for k, name in enumerate(SMALL_NAMES):
        res[name] = small_outs[1 + 4 * k:5 + 4 * k]

    result = [loss, dx.reshape(1, S, D)]
    for k in range(4):
        result += [res[name][k] for name in WEIGHT_ORDER]
    return tuple(result)
```

```python
import functools
import math

import numpy as np
import jax
import jax.numpy as jnp
from jax import lax
from jax.experimental import pallas as pl
from jax.experimental.pallas import tpu as pltpu

f32 = jnp.float32
bf16 = jnp.bfloat16
SDS = jax.ShapeDtypeStruct
MESH = pl.DeviceIdType.MESH
HIGHEST = lax.Precision.HIGHEST

S = 2048
D = 1024
DEPTH = 2
BLK = 128
NBLK = S // BLK
HD = 64
NQ = 8
NKV = 2
NSSM = 8
NGRP = 2
NSTATE = 128
D_ATTN = 512
D_SSM = 512
D_CONV = 1024
D_FF = 4096
D_IN = 2312
D_IN_PAD = 2560
COL_QKV, COL_Z, COL_DT, COL_XBC = 0, 768, 1280, 1536
IN_SEGMENTS = ((0, 1280, 0), (1280, 2304, COL_XBC), (2304, 2312, COL_DT))
N_BUCKETS = 32
EPS = 1e-6
N_DEV = 8
VMEM_LIMIT = 48 * 1024 * 1024

ADAM_LR = 0.001
ADAM_B1 = 0.9
ADAM_B2 = 0.999
ADAM_EPS = 1e-08
ADAM_WD = 0.01
ADAM_STEP = 10

NT_DIMS = (((1,), (1,)), ((), ()))
TN_DIMS = (((0,), (0,)), ((), ()))
NN_DIMS = (((1,), (0,)), ((), ()))

ROW_MIXG = 0
ROW_MLPG = 2
ROW_CONVB = 4
ROW_SSMG = 6
ROW_MISC = 8
ROW_RELB = 10
ROW_CONVW = 18
ROW_LOSS = 26
SMALL_ROWS = 32
LANE_QG, LANE_KG, LANE_SINK, LANE_DTB, LANE_ALOG, LANE_DSKIP = 0, 64, 128, 256, 384, 512


def _dot(a, b, dims):
    return lax.dot_general(a, b, dims, preferred_element_type=f32)


def _cparams(n_axes):
    return pltpu.CompilerParams(dimension_semantics=("arbitrary",) * n_axes, vmem_limit_bytes=VMEM_LIMIT)


def _sum11(v):
    return jnp.sum(jnp.sum(v, axis=1, keepdims=True), axis=0, keepdims=True)


def _sigmoid(v):
    return 1.0 / (1.0 + jnp.exp(-v))


ANY_SPEC = pl.BlockSpec(memory_space=pl.ANY)


def _in_hbm(args):
    return [pltpu.with_memory_space_constraint(a, pltpu.HBM) if a.size >= 65536 else a for a in args]


def _out_hbm(out_shape):
    one = lambda s: pltpu.HBM(s.shape, s.dtype) if math.prod(s.shape) >= 65536 else s
    return [one(s) for s in out_shape] if isinstance(out_shape, (list, tuple)) else one(out_shape)


def _matmul(name, mode, a, b, *, layer=0, tm, tn, tk, out_shape, out_specs, epilogue, extras=(), extra_specs=(), deps=(),
            prologue=None):
    extras = tuple(extras) + tuple(deps)
    extra_specs = tuple(extra_specs) + (ANY_SPEC,) * len(deps)
    if mode == "tn":
        t_dim, m_dim = a.shape
        n_dim = b.shape[1]
        grid = (m_dim // tm, n_dim // tn, t_dim // tk)
        a_spec = pl.BlockSpec((tk, tm), lambda i, j, k: (k, i))
        b_spec = pl.BlockSpec((tk, tn), lambda i, j, k: (k, j))
        dims = TN_DIMS
    elif mode == "nn":
        m_dim, k_dim = a.shape
        n_dim = b.shape[-1]
        grid = (m_dim // tm, n_dim // tn, k_dim // tk)
        a_spec = pl.BlockSpec((tm, tk), lambda i, j, k: (i, k))
        b_spec = pl.BlockSpec((None, tk, tn), lambda i, j, k: (layer, k, j))
        dims = NN_DIMS
    else:
        m_dim, k_dim = a.shape
        n_dim = b.shape[-2]
        grid = (m_dim // tm, n_dim // tn, k_dim // tk)
        a_spec = pl.BlockSpec((tm, tk), lambda i, j, k: (i, k))
        b_spec = pl.BlockSpec((None, tn, tk), lambda i, j, k: (layer, j, k))
        dims = NT_DIMS
    nk = grid[2]
    n_ex = len(extras)

    def body(a_ref, b_ref, *rest):
        ex = rest[:n_ex - len(deps)]
        outs = rest[n_ex:-1]
        acc = rest[-1]
        i = pl.program_id(0)
        j = pl.program_id(1)
        k = pl.program_id(2)
        lhs = a_ref[...].astype(bf16) if prologue is None else prologue(a_ref, ex, outs)
        part = _dot(lhs, b_ref[...].astype(bf16), dims)
        if nk == 1:
            epilogue(part, i, j, ex, outs)
        else:
            @pl.when(k == 0)
            def _():
                acc[...] = part

            @pl.when(k > 0)
            def _():
                acc[...] += part

            @pl.when(k == nk - 1)
            def _():
                epilogue(acc[...], i, j, ex, outs)

    return pl.pallas_call(
        body, grid=grid, in_specs=[a_spec, b_spec, *extra_specs], out_specs=out_specs, out_shape=_out_hbm(out_shape),
        scratch_shapes=[pltpu.VMEM((tm, tn) if nk > 1 else (8, 128), f32)], name=name, compiler_params=_cparams(3),
    )(*_in_hbm([a]), b, *_in_hbm(extras))


def _rms_bwd_epilogue(layer):
    def epi(acc, i, j, ex, outs):
        x_ref, g_ref, dres_ref = ex
        dx_ref, dg_ref = outs
        xv = x_ref[...]
        r = lax.rsqrt(jnp.mean(xv * xv, axis=-1, keepdims=True) + EPS)
        xhat = xv * r
        w = acc * g_ref[layer:layer + 1, :]
        dx_ref[...] = dres_ref[...] + r * (w - xhat * jnp.mean(xhat * w, axis=-1, keepdims=True))
        dg = jnp.sum(acc * xhat, axis=0, keepdims=True)

        @pl.when(i == 0)
        def _():
            dg_ref[...] = dg

        @pl.when(i > 0)
        def _():
            dg_ref[...] += dg
    return epi


def _own_slab_spec(kind, tr, cols, nblk):
    if kind == "stack":
        return pl.BlockSpec((None, tr, cols), lambda i, idx: (idx[0], i, 0))
    if kind == "cols512":
        return pl.BlockSpec((tr, cols), lambda i, idx: (i, idx[0]))
    return pl.BlockSpec((tr, cols), lambda i, idx: (idx[0] * nblk + i, 0))


def _cast_to_full(name, w, kind, full_shape, my_idx, dtype):
    n_layers, rows, cols = w.shape
    tr = min(rows, 256)
    nblk = rows // tr

    def body(idx_ref, w_ref, *o_refs):
        for l in range(n_layers):
            o_refs[l][...] = w_ref[l].astype(dtype)

    grid_spec = pltpu.PrefetchScalarGridSpec(
        num_scalar_prefetch=1, grid=(nblk,), in_specs=[pl.BlockSpec((n_layers, tr, cols), lambda i, idx: (0, i, 0))],
        out_specs=[_own_slab_spec(kind, tr, cols, nblk)] * n_layers)
    return pl.pallas_call(body, grid_spec=grid_spec, out_shape=_out_hbm([SDS(full_shape, dtype)] * n_layers), name=name,
                          compiler_params=_cparams(1))(*_in_hbm([my_idx, w]))


def _adamw_math(w, m, v, g):
    m_new = ADAM_B1 * m + (1.0 - ADAM_B1) * g
    v_new = ADAM_B2 * v + (1.0 - ADAM_B2) * (g * g)
    m_hat = m_new / (1.0 - ADAM_B1 ** ADAM_STEP)
    v_hat = v_new / (1.0 - ADAM_B2 ** ADAM_STEP)
    delta = -ADAM_LR * (m_hat / (jnp.sqrt(v_hat) + ADAM_EPS) + ADAM_WD * w)
    return delta, m_new, v_new


def _adamw_layer(name, kind, layer, w, m, v, land, g_full, my_idx, prev, tr):
    rows2, cols = w.shape
    rows = rows2 // DEPTH
    nblk = rows // tr
    own_spec = _own_slab_spec(kind, tr, cols, nblk)
    n_prev = 0 if prev is None else 4

    def body(idx_ref, w_ref, m_ref, v_ref, land_ref, own_ref, *rest):
        g_ref, d_ref, mo_ref, vo_ref = rest[n_prev:]
        me = idx_ref[0]
        g = None
        for p in range(N_DEV):
            part = jnp.where(me == p, own_ref[...], land_ref[p]).astype(f32)
            g = part if g is None else g + part
        delta, m_new, v_new = _adamw_math(w_ref[...], m_ref[...], v_ref[...], g)
        g_ref[...] = g
        d_ref[...] = delta
        mo_ref[...] = m_new
        vo_ref[...] = v_new

    blk = pl.BlockSpec((tr, cols), lambda i, idx: (layer * nblk + i, 0))
    grid_spec = pltpu.PrefetchScalarGridSpec(
        num_scalar_prefetch=1, grid=(nblk,),
        in_specs=[blk, blk, blk, pl.BlockSpec((N_DEV, tr, cols), lambda i, idx: (0, i, 0)), own_spec] + [ANY_SPEC] * n_prev,
        out_specs=[blk, blk, blk, blk])
    aliases = {} if prev is None else {6 + k: k for k in range(4)}
    return pl.pallas_call(
        body, grid_spec=grid_spec, out_shape=_out_hbm([SDS((rows2, cols), f32)] * 4), name=name, input_output_aliases=aliases,
        compiler_params=_cparams(1),
    )(*_in_hbm([my_idx, w, m, v, land, g_full, *([] if prev is None else prev)]))


def _bucket_table():
    qi = np.arange(BLK)[:, None]
    kj = np.arange(2 * BLK)[None, :]
    dist = qi + BLK - kj
    dcl = np.clip(dist, 0, None)
    max_exact = N_BUCKETS // 2
    d_f = np.maximum(dcl, 1).astype(np.float32)
    large = max_exact + (np.log(d_f / np.float32(max_exact)) / np.float32(math.log(128 / max_exact))
                         * np.float32(N_BUCKETS - max_exact)).astype(np.int32)
    large = np.minimum(large, N_BUCKETS - 1)
    bucket = np.where(dcl < max_exact, dcl, large)
    in_window = (dist >= 0) & (dist < BLK)
    return bucket.astype(np.int32), in_window


def _onehot_buckets():
    bucket, _ = _bucket_table()
    oh = (bucket.reshape(-1)[None, :] == np.arange(N_BUCKETS)[:, None]).astype(np.float32)
    return oh


def _bias_build(rel_bias_t, onehot_t):
    def body(r_ref, o_ref, out_ref):
        r = r_ref[...]
        hi = r.astype(bf16)
        r1 = r - hi.astype(f32)
        mid = r1.astype(bf16)
        lo = (r1 - mid.astype(f32)).astype(bf16)
        oh = o_ref[...]
        out_ref[...] = _dot(hi, oh, NN_DIMS) + _dot(mid, oh, NN_DIMS) + _dot(lo, oh, NN_DIMS)

    tn = 4096
    return pl.pallas_call(
        body, grid=(BLK * 2 * BLK // tn,),
        in_specs=[pl.BlockSpec((NQ, N_BUCKETS), lambda i: (0, 0)), pl.BlockSpec((N_BUCKETS, tn), lambda i: (0, i))],
        out_specs=pl.BlockSpec((NQ, tn), lambda i: (0, i)), out_shape=SDS((NQ, BLK * 2 * BLK), f32), name="bias_build",
        compiler_params=_cparams(1),
    )(rel_bias_t, onehot_t)


def _bias_grad(dbias0, dbias1, onehot_t):
    tn = 4096
    nsteps = BLK * 2 * BLK // tn

    def body(a_ref, b_ref, o_ref, out_ref):
        g = a_ref[...] + b_ref[...]
        hi = g.astype(bf16)
        lo = (g - hi.astype(f32)).astype(bf16)
        part = _dot(hi, o_ref[...], NT_DIMS) + _dot(lo, o_ref[...], NT_DIMS)

        @pl.when(pl.program_id(0) == 0)
        def _():
            out_ref[...] = part

        @pl.when(pl.program_id(0) > 0)
        def _():
            out_ref[...] += part

    return pl.pallas_call(
        body, grid=(nsteps,),
        in_specs=[pl.BlockSpec((NQ, tn), lambda i: (0, i)), pl.BlockSpec((NQ, tn), lambda i: (0, i)),
                  pl.BlockSpec((N_BUCKETS, tn), lambda i: (0, i))],
        out_specs=pl.BlockSpec((NQ, N_BUCKETS), lambda i: (0, 0)), out_shape=SDS((NQ, N_BUCKETS), f32), name="bias_grad",
        compiler_params=_cparams(1),
    )(dbias0, dbias1, onehot_t)


def _attn_mask(n):
    qi = lax.broadcasted_iota(jnp.int32, (BLK, 2 * BLK), 0)
    kj = lax.broadcasted_iota(jnp.int32, (BLK, 2 * BLK), 1)
    dist = qi + BLK - kj
    first_key = jnp.where(n > 0, 0, BLK)
    return (dist >= 0) & (dist < BLK) & (kj >= first_key)


def _row_mean(a):
    return jnp.mean(a, axis=-1, keepdims=True)


def _head_norm(t, gain):
    r = lax.rsqrt(_row_mean(t * t) + EPS)
    that = t * r
    return that, r, that * gain


def _softmax_with_sink(s, sink):
    m = jnp.maximum(jnp.max(s, axis=-1, keepdims=True), sink)
    p = jnp.exp(s - m)
    psink = jnp.exp(sink - m)
    inv = 1.0 / (jnp.sum(p, axis=-1, keepdims=True) + psink)
    return p * inv, psink * inv


GQ = NQ // NKV


def _attn_fwd(qkv, q_gain, k_gain, sinks, bias, layer):
    def body(q_ref, kc_ref, kp_ref, vc_ref, vp_ref, qg_ref, kg_ref, sk_ref, bias_ref, o_ref):
        m = pl.program_id(0)
        qg = qg_ref[layer:layer + 1, :]
        kg = kg_ref[layer:layer + 1, :]
        grp = range(NKV)
        chains = [(b, j) for b in range(2) for j in grp]
        masks = [jnp.tile(_attn_mask(2 * m + b), (GQ, 1)) for b in range(2)]
        kblk = [[kp_ref[:, pl.ds(HD * j, HD)].astype(f32), kc_ref[0:BLK, pl.ds(HD * j, HD)].astype(f32),
                 kc_ref[BLK:, pl.ds(HD * j, HD)].astype(f32)] for j in grp]
        vblk = [[vp_ref[:, pl.ds(HD * j, HD)].astype(bf16), vc_ref[0:BLK, pl.ds(HD * j, HD)].astype(bf16),
                 vc_ref[BLK:, pl.ds(HD * j, HD)].astype(bf16)] for j in grp]
        knb = [[_head_norm(kblk[j][t], kg)[2].astype(bf16) for t in range(3)] for j in grp]
        kn_b = {(b, j): jnp.concatenate([knb[j][b], knb[j][b + 1]], axis=0) for b, j in chains}
        vbs = {(b, j): jnp.concatenate([vblk[j][b], vblk[j][b + 1]], axis=0) for b, j in chains}
        rows = {}
        for b, j in chains:
            heads = [GQ * j + g for g in range(GQ)]
            rows[b, j] = (jnp.concatenate([q_ref[pl.ds(BLK * b, BLK), pl.ds(HD * h, HD)] for h in heads], axis=0).astype(f32),
                          jnp.concatenate([jnp.broadcast_to(sk_ref[layer:layer + 1, h:h + 1], (BLK, 1)) for h in heads], axis=0))
        qn_b = {c: _head_norm(rows[c][0], qg)[2].astype(bf16) for c in chains}
        ss = {(b, j): _dot(qn_b[b, j], kn_b[b, j], NT_DIMS) * (HD ** -0.5) + bias_ref[GQ * j:GQ * (j + 1)].reshape(GQ * BLK, 2 * BLK)
              for b, j in chains}
        ps = {(b, j): _softmax_with_sink(jnp.where(masks[b], ss[b, j], -jnp.inf), rows[b, j][1])[0] for b, j in chains}
        outs = {c: _dot(ps[c].astype(bf16), vbs[c], NN_DIMS).astype(bf16) for c in chains}
        for b, j in chains:
            for g in range(GQ):
                o_ref[pl.ds(BLK * b, BLK), pl.ds(HD * (GQ * j + g), HD)] = outs[b, j][BLK * g:BLK * (g + 1), :]

    prev = lambda m: jnp.maximum(2 * m - 1, 0)
    small = lambda shape: pl.BlockSpec(shape, lambda m: (0,) * len(shape))
    return pl.pallas_call(
        body, grid=(NBLK // 2,),
        in_specs=[pl.BlockSpec((2 * BLK, D_ATTN), lambda m: (m, 0)),
                  pl.BlockSpec((2 * BLK, 128), lambda m: (m, 4)), pl.BlockSpec((BLK, 128), lambda m: (prev(m), 4)),
                  pl.BlockSpec((2 * BLK, 128), lambda m: (m, 5)), pl.BlockSpec((BLK, 128), lambda m: (prev(m), 5)),
                  small((DEPTH, HD)), small((DEPTH, HD)), small((DEPTH, NQ)), small((NQ, BLK, 2 * BLK))],
        out_specs=pl.BlockSpec((2 * BLK, D_ATTN), lambda m: (m, 0)), out_shape=_out_hbm(SDS((S, D_ATTN), bf16)),
        name="attn_fwd", compiler_params=_cparams(1),
    )(*_in_hbm([qkv, qkv, qkv, qkv, qkv, q_gain, k_gain, sinks, bias]))


def _attn_bwd(qkv, dmix, q_gain, k_gain, sinks, bias, layer, deps=()):
    def body(q_ref, kc_ref, kp_ref, vc_ref, vp_ref, do_ref, qg_ref, kg_ref, sk_ref, bias_ref, *rest):
        dqkv_ref, dbias_ref, dsm_ref, carry = rest[len(deps):]
        i = pl.program_id(0)
        m = NBLK // 2 - 1 - i
        qg = qg_ref[layer:layer + 1, :]
        kg = kg_ref[layer:layer + 1, :]
        lane = lax.broadcasted_iota(jnp.int32, (1, 128), 1)

        @pl.when(i == 0)
        def _():
            carry[...] = jnp.zeros_like(carry)
            dbias_ref[...] = jnp.zeros_like(dbias_ref)
            dsm_ref[...] = jnp.zeros_like(dsm_ref)

        grp = range(NKV)
        chains = [(b, j) for b in range(2) for j in grp]
        masks = [jnp.tile(_attn_mask(2 * m + b), (GQ, 1)) for b in range(2)]
        kblk = [[kp_ref[:, pl.ds(HD * j, HD)].astype(f32), kc_ref[0:BLK, pl.ds(HD * j, HD)].astype(f32),
                 kc_ref[BLK:, pl.ds(HD * j, HD)].astype(f32)] for j in grp]
        vblk = [[vp_ref[:, pl.ds(HD * j, HD)].astype(bf16), vc_ref[0:BLK, pl.ds(HD * j, HD)].astype(bf16),
                 vc_ref[BLK:, pl.ds(HD * j, HD)].astype(bf16)] for j in grp]
        knorm = [[_head_norm(kblk[j][t], kg) for t in range(3)] for j in grp]
        kn_b = {(b, j): jnp.concatenate([knorm[j][b][2].astype(bf16), knorm[j][b + 1][2].astype(bf16)], axis=0) for b, j in chains}
        vbs = {(b, j): jnp.concatenate([vblk[j][b], vblk[j][b + 1]], axis=0) for b, j in chains}
        rows, do_b = {}, {}
        for b, j in chains:
            heads = [GQ * j + g for g in range(GQ)]
            qrows = pl.ds(BLK * b, BLK)
            rows[b, j] = (jnp.concatenate([q_ref[qrows, pl.ds(HD * h, HD)] for h in heads], axis=0).astype(f32),
                          jnp.concatenate([jnp.broadcast_to(sk_ref[layer:layer + 1, h:h + 1], (BLK, 1)) for h in heads], axis=0))
            do_b[b, j] = jnp.concatenate([do_ref[qrows, pl.ds(HD * h, HD)] for h in heads], axis=0).astype(bf16)
        qnorm = {c: _head_norm(rows[c][0], qg) for c in chains}
        qn_b = {c: qnorm[c][2].astype(bf16) for c in chains}
        ss = {(b, j): _dot(qn_b[b, j], kn_b[b, j], NT_DIMS) * (HD ** -0.5) + bias_ref[GQ * j:GQ * (j + 1)].reshape(GQ * BLK, 2 * BLK)
              for b, j in chains}
        sm = {(b, j): _softmax_with_sink(jnp.where(masks[b], ss[b, j], -jnp.inf), rows[b, j][1]) for b, j in chains}
        dps = {c: _dot(do_b[c], vbs[c], NT_DIMS) for c in chains}
        deltas = {c: jnp.sum(sm[c][0] * dps[c], axis=-1, keepdims=True) for c in chains}
        dss = {c: sm[c][0] * (dps[c] - deltas[c]) for c in chains}
        ds_b = {c: (dss[c] * (HD ** -0.5)).astype(bf16) for c in chains}
        dqn = {c: _dot(ds_b[c], kn_b[c], NN_DIMS) for c in chains}
        dkn = {c: _dot(ds_b[c], qn_b[c], TN_DIMS) for c in chains}
        dvs = {c: _dot(sm[c][0].astype(bf16), do_b[c], TN_DIMS) for c in chains}
        dqg = jnp.zeros((1, HD), f32)
        dkg = jnp.zeros((1, HD), f32)
        dsink = jnp.zeros((1, 128), f32)
        for b, j in chains:
            dbias_ref[GQ * j:GQ * (j + 1)] += dss[b, j].reshape(GQ, BLK, 2 * BLK)
            dsk = sm[b, j][1] * deltas[b, j]
            for g in range(GQ):
                dsink = dsink + jnp.where(lane == GQ * j + g, -_sum11(dsk[BLK * g:BLK * (g + 1), :]), 0.0)
            qhat, rq, _ = qnorm[b, j]
            w = dqn[b, j] * qg
            dq = rq * (w - qhat * _row_mean(qhat * w))
            for g in range(GQ):
                dqkv_ref[pl.ds(BLK * b, BLK), pl.ds(HD * (GQ * j + g), HD)] = dq[BLK * g:BLK * (g + 1), :].astype(bf16)
            dqg = dqg + jnp.sum(dqn[b, j] * qhat, axis=0, keepdims=True)
        for j in grp:
            dkn_t = [dkn[0, j][:BLK, :], dkn[0, j][BLK:, :] + dkn[1, j][:BLK, :], dkn[1, j][BLK:, :]]
            dv_t = [dvs[0, j][:BLK, :], dvs[0, j][BLK:, :] + dvs[1, j][:BLK, :], dvs[1, j][BLK:, :]]
            dk_t = []
            for t in range(3):
                khat, rk, _ = knorm[j][t]
                w = dkn_t[t] * kg
                dk_t.append(rk * (w - khat * _row_mean(khat * w)))
                dkg = dkg + jnp.sum(dkn_t[t] * khat, axis=0, keepdims=True)
            kcols, vcols = pl.ds(D_ATTN + HD * j, HD), pl.ds(D_ATTN + 128 + HD * j, HD)
            dqkv_ref[BLK:, kcols] = (dk_t[2] + carry[:, pl.ds(HD * j, HD)]).astype(bf16)
            dqkv_ref[BLK:, vcols] = (dv_t[2] + carry[:, pl.ds(128 + HD * j, HD)]).astype(bf16)
            dqkv_ref[0:BLK, kcols] = dk_t[1].astype(bf16)
            dqkv_ref[0:BLK, vcols] = dv_t[1].astype(bf16)
            carry[:, pl.ds(HD * j, HD)] = dk_t[0]
            carry[:, pl.ds(128 + HD * j, HD)] = dv_t[0]
        dsm_ref[0:1, 0:HD] += dqg
        dsm_ref[1:2, 0:HD] += dkg
        dsm_ref[2:3, :] += dsink

    rev = lambda i: NBLK // 2 - 1 - i
    prev = lambda i: jnp.maximum(NBLK - 3 - 2 * i, 0)
    small = lambda shape: pl.BlockSpec(shape, lambda i: (0,) * len(shape))
    return pl.pallas_call(
        body, grid=(NBLK // 2,),
        in_specs=[pl.BlockSpec((2 * BLK, D_ATTN), lambda i: (rev(i), 0)),
                  pl.BlockSpec((2 * BLK, 128), lambda i: (rev(i), 4)), pl.BlockSpec((BLK, 128), lambda i: (prev(i), 4)),
                  pl.BlockSpec((2 * BLK, 128), lambda i: (rev(i), 5)), pl.BlockSpec((BLK, 128), lambda i: (prev(i), 5)),
                  pl.BlockSpec((2 * BLK, D_ATTN), lambda i: (rev(i), 0)),
                  small((DEPTH, HD)), small((DEPTH, HD)), small((DEPTH, NQ)), small((NQ, BLK, 2 * BLK))] + [ANY_SPEC] * len(deps),
        out_specs=[pl.BlockSpec((2 * BLK, 768), lambda i: (rev(i), COL_QKV // 768)), small((NQ, BLK, 2 * BLK)), small((8, 128))],
        out_shape=_out_hbm([SDS((S, D_IN_PAD), bf16), SDS((NQ, BLK, 2 * BLK), f32), SDS((8, 128), f32)]),
        scratch_shapes=[pltpu.VMEM((BLK, 256), f32)], name="attn_bwd", compiler_params=_cparams(1),
    )(*_in_hbm([qkv, qkv, qkv, qkv, qkv, dmix, q_gain, k_gain, sinks, bias, *deps]))


CONV_TC = 256


def _shift_down(u, s):
    if s == 0:
        return u
    rows = lax.broadcasted_iota(jnp.int32, u.shape, 0)
    return jnp.where(rows >= s, pltpu.roll(u, s, 0), 0.0)


def _shift_up(u, s):
    if s == 0:
        return u
    rows = lax.broadcasted_iota(jnp.int32, u.shape, 0)
    return jnp.where(rows < u.shape[0] - s, pltpu.roll(u, u.shape[0] - s, 0), 0.0)


def _conv_specs():
    return [pl.BlockSpec((S, CONV_TC), lambda c: (0, c)),
            pl.BlockSpec((None, 4, CONV_TC), lambda c: (0, 0, c)),
            pl.BlockSpec((DEPTH, CONV_TC), lambda c: (0, c))]


def _conv_pre(u, w_ref, b_ref, layer):
    pre = b_ref[layer:layer + 1, :] + w_ref[3:4, :] * u
    for k in range(3):
        pre = pre + w_ref[k:k + 1, :] * _shift_down(u, 3 - k)
    return pre


def _conv_fwd(xbc, conv_w, conv_b, layer):
    def body(u_ref, w_ref, b_ref, o_ref):
        pre = _conv_pre(u_ref[...].astype(f32), w_ref, b_ref, layer)
        o_ref[...] = pre * _sigmoid(pre)

    specs = _conv_specs()
    specs[1] = pl.BlockSpec((None, 4, CONV_TC), lambda c: (layer, 0, c))
    return pl.pallas_call(
        body, grid=(D_CONV // CONV_TC,), in_specs=specs, out_specs=pl.BlockSpec((S, CONV_TC), lambda c: (0, c)),
        out_shape=_out_hbm(SDS((S, D_CONV), f32)), name="conv_fwd", compiler_params=_cparams(1),
    )(*_in_hbm([xbc, conv_w, conv_b]))


def _conv_bwd(xbc, dact, conv_w, conv_b, dproj, layer):
    def body(u_ref, w_ref, b_ref, da_ref, dproj_in, du_ref, dw_ref, db_ref):
        u = u_ref[...].astype(f32)
        pre = _conv_pre(u, w_ref, b_ref, layer)
        sg = _sigmoid(pre)
        dpre = da_ref[...] * (sg * (1.0 + pre * (1.0 - sg)))
        du = w_ref[3:4, :] * dpre
        for k in range(3):
            du = du + w_ref[k:k + 1, :] * _shift_up(dpre, 3 - k)
        du_ref[...] = du.astype(bf16)
        db_ref[...] = jnp.broadcast_to(jnp.sum(dpre, axis=0, keepdims=True), db_ref.shape)
        dw_ref[...] = jnp.zeros_like(dw_ref)
        for k in range(4):
            dw_ref[k:k + 1, :] = jnp.sum(dpre * _shift_down(u, 3 - k), axis=0, keepdims=True)

    specs = _conv_specs()
    specs[1] = pl.BlockSpec((None, 4, CONV_TC), lambda c: (layer, 0, c))
    col = pl.BlockSpec((S, CONV_TC), lambda c: (0, c))
    row8 = pl.BlockSpec((8, CONV_TC), lambda c: (0, c))
    return pl.pallas_call(
        body, grid=(D_CONV // CONV_TC,), in_specs=[*specs, col, ANY_SPEC],
        out_specs=[pl.BlockSpec((S, CONV_TC), lambda c: (0, COL_XBC // CONV_TC + c)), row8, row8],
        out_shape=_out_hbm([SDS((S, D_IN_PAD), bf16), SDS((8, D_CONV), f32), SDS((8, D_CONV), f32)]), name="conv_bwd",
        input_output_aliases={4: 0}, compiler_params=_cparams(1),
    )(*_in_hbm([xbc, conv_w, conv_b, dact, dproj]))


def _tri():
    return (lax.broadcasted_iota(jnp.int32, (BLK, BLK), 0) >= lax.broadcasted_iota(jnp.int32, (BLK, BLK), 1))


def _ssd_scalars(dt_ref, dtb_ref, alog_ref, layer):
    raw = dt_ref[:, 0:NSSM] + dtb_ref[layer:layer + 1, :]
    dtv = jnp.maximum(raw, 0.0) + jnp.log(1.0 + jnp.exp(-jnp.abs(raw)))
    a = -jnp.exp(alog_ref[layer:layer + 1, :])
    acs = jnp.dot(_tri().astype(f32), dtv * a, preferred_element_type=f32, precision=HIGHEST)
    return raw, dtv, a, acs


HG = NSSM // NGRP
GW = HG * HD


def _lane_expand(cols, g):
    lane_head = lax.broadcasted_iota(jnp.int32, (1, GW), 1) // HD
    out = cols[:, HG * g + HG - 1:HG * g + HG]
    for r in range(HG - 2, -1, -1):
        out = jnp.where(lane_head == r, cols[:, HG * g + r:HG * g + r + 1], out)
    return out


def _row_expand(vals, g):
    row_head = lax.broadcasted_iota(jnp.int32, (GW, 1), 0) // HD
    out = vals[:, HG * g + HG - 1:HG * g + HG]
    for r in range(HG - 2, -1, -1):
        out = jnp.where(row_head == r, vals[:, HG * g + r:HG * g + r + 1], out)
    return out


def _head_rowsums(a, g):
    sel = (lax.broadcasted_iota(jnp.int32, (GW, NSSM), 0) // HD + HG * g == lax.broadcasted_iota(jnp.int32, (GW, NSSM), 1)).astype(bf16)
    hi = a.astype(bf16)
    lo = (a - hi.astype(f32)).astype(bf16)
    return _dot(hi, sel, NN_DIMS) + _dot(lo, sel, NN_DIMS)


def _head_blocksums(v, g):
    sel = (lax.broadcasted_iota(jnp.int32, (GW, NSSM), 0) // HD + HG * g == lax.broadcasted_iota(jnp.int32, (GW, NSSM), 1)).astype(bf16)
    hi = v.astype(bf16)
    lo = (v - hi.astype(f32)).astype(bf16)
    return _dot(hi, sel, TN_DIMS) + _dot(lo, sel, TN_DIMS)


def _ssd_chunk_common(xc_ref, dt_ref, dtb_ref, alog_ref, h_rows, layer):
    raw, dtv, a, acs = _ssd_scalars(dt_ref, dtb_ref, alog_ref, layer)
    acs_t = acs.T
    last = acs[BLK - 1:BLK, :]
    c = dict(raw=raw, dtv=dtv, a=a, acs=acs, last=last, dte=jnp.exp(last - acs), e_all=jnp.exp(acs), cd=jnp.exp(last))
    grp, heads, tri = range(NGRP), range(NSSM), _tri()
    c["bm"] = [xc_ref[:, pl.ds(D_SSM + NSTATE * g, NSTATE)] for g in grp]
    c["bm_b"] = [c["bm"][g].astype(bf16) for g in grp]
    c["cm_b"] = [xc_ref[:, pl.ds(D_SSM + NGRP * NSTATE + NSTATE * g, NSTATE)].astype(bf16) for g in grp]
    c["cb"] = [_dot(c["cm_b"][g], c["bm_b"][g], NT_DIMS) for g in grp]
    c["x"] = [xc_ref[:, pl.ds(GW * g, GW)] for g in grp]
    c["dt"] = [_lane_expand(dtv, g) for g in grp]
    c["xdt"] = [c["x"][g] * c["dt"][g] for g in grp]
    c["xdt_b"] = [c["xdt"][g].astype(bf16) for g in grp]
    c["prev"] = [h_rows(g) for g in grp]
    c["prev_b"] = [c["prev"][g].astype(bf16) for g in grp]
    c["e"] = [_lane_expand(c["e_all"], g) for g in grp]
    c["y_off"] = [_dot(c["cm_b"][g], c["prev_b"][g], NT_DIMS) * c["e"][g] for g in grp]
    c["decay"] = [jnp.exp(jnp.where(tri, acs[:, h:h + 1] - acs_t[h:h + 1, :], -jnp.inf)) for h in heads]
    c["m"] = [c["cb"][h // HG] * c["decay"][h] for h in heads]
    c["m_b"] = [c["m"][h].astype(bf16) for h in heads]
    c["dte_x"] = [_lane_expand(c["dte"], g) for g in grp]
    c["xdte_b"] = [(c["xdt"][g] * c["dte_x"][g]).astype(bf16) for g in grp]
    return c


def _ssd_fwd(xact, z, dt, attn, dt_bias, a_log, d_skip, norm_g, layer):
    def body(xc_ref, z_ref, dt_ref, at_ref, dtb_ref, alog_ref, dsk_ref, ng_ref, mix_ref, hs_ref, y_ref, h_ref):
        n = pl.program_id(0)

        @pl.when(n == 0)
        def _():
            h_ref[...] = jnp.zeros_like(h_ref)

        hs_ref[...] = h_ref[...]
        c = _ssd_chunk_common(xc_ref, dt_ref, dtb_ref, alog_ref, lambda g: h_ref[pl.ds(GW * g, GW), :], layer)
        grp, heads = range(NGRP), range(NSSM)
        y_diag = [_dot(c["m_b"][h], c["xdt_b"][h // HG][:, HD * (h % HG):HD * (h % HG + 1)], NN_DIMS) for h in heads]
        new_st = [_dot(c["xdte_b"][g], c["bm_b"][g], TN_DIMS) for g in grp]
        for h in heads:
            y_ref[:, pl.ds(HD * h, HD)] = y_diag[h]
        dskip = dsk_ref[layer:layer + 1, :]
        for g in grp:
            cols = pl.ds(GW * g, GW)
            y_ref[:, cols] = y_ref[:, cols] + c["y_off"][g] + c["x"][g] * _lane_expand(dskip, g)
            h_ref[cols, :] = c["prev"][g] * _row_expand(c["cd"], g) + new_st[g]
        zv = z_ref[...].astype(f32)
        yz = y_ref[...] * (zv * _sigmoid(zv))
        mix_ref[:, 0:D_ATTN] = at_ref[...]
        for g in grp:
            yg = yz[:, GW * g:GW * (g + 1)]
            rs = lax.rsqrt(jnp.mean(yg * yg, axis=-1, keepdims=True) + EPS)
            mix_ref[:, D_ATTN + GW * g:D_ATTN + GW * (g + 1)] = (yg * rs * ng_ref[layer:layer + 1, GW * g:GW * (g + 1)]).astype(bf16)

    small = lambda shape: pl.BlockSpec(shape, lambda n: (0,) * len(shape))
    return pl.pallas_call(
        body, grid=(NBLK,),
        in_specs=[pl.BlockSpec((BLK, D_CONV), lambda n: (n, 0)), pl.BlockSpec((BLK, D_SSM), lambda n: (n, 0)),
                  pl.BlockSpec((BLK, 128), lambda n: (n, 0)), pl.BlockSpec((BLK, D_ATTN), lambda n: (n, 0)),
                  small((DEPTH, NSSM)), small((DEPTH, NSSM)), small((DEPTH, NSSM)), small((DEPTH, D_SSM))],
        out_specs=[pl.BlockSpec((BLK, D), lambda n: (n, 0)), pl.BlockSpec((None, NSSM * HD, NSTATE), lambda n: (n, 0, 0)),
                   pl.BlockSpec((BLK, D_SSM), lambda n: (n, 0))],
        out_shape=_out_hbm([SDS((S, D), bf16), SDS((NBLK, NSSM * HD, NSTATE), f32), SDS((S, D_SSM), f32)]),
        scratch_shapes=[pltpu.VMEM((NSSM * HD, NSTATE), f32)],
        name="ssd_fwd", compiler_params=_cparams(1),
    )(*_in_hbm([xact, z, dt, attn, dt_bias, a_log, d_skip, norm_g]))


def _ssd_bwd(xact, z, dt, dmix, hs, y, dt_bias, a_log, d_skip, norm_g, dproj, layer):
    def body(xc_ref, z_ref, dt_ref, do_ref, hs_ref, y_ref, dtb_ref, alog_ref, dsk_ref, ng_ref, dproj_in,
             dzdt_ref, dx_ref, dsm_ref, dh_ref, dy_ref):
        i = pl.program_id(0)

        @pl.when(i == 0)
        def _():
            dh_ref[...] = jnp.zeros_like(dh_ref)
            dsm_ref[...] = jnp.zeros_like(dsm_ref)

        c = _ssd_chunk_common(xc_ref, dt_ref, dtb_ref, alog_ref, lambda g: hs_ref[pl.ds(GW * g, GW), :], layer)
        raw, dtv, a = c["raw"], c["dtv"], c["a"]
        grp, heads = range(NGRP), range(NSSM)
        dskip = dsk_ref[layer:layer + 1, :]
        lane8 = lax.broadcasted_iota(jnp.int32, (1, NSSM), 1)
        sub8 = lax.broadcasted_iota(jnp.int32, (NSSM, 1), 0)

        zv = z_ref[...].astype(f32)
        sz = _sigmoid(zv)
        gz = zv * sz
        yv = y_ref[...]
        yz = yv * gz
        for g in grp:
            sl = slice(GW * g, GW * (g + 1))
            yg = yz[:, sl]
            rs = lax.rsqrt(jnp.mean(yg * yg, axis=-1, keepdims=True) + EPS)
            yhat = yg * rs
            dog = do_ref[:, sl]
            w = dog * ng_ref[layer:layer + 1, sl]
            dyz = rs * (w - yhat * jnp.mean(yhat * w, axis=-1, keepdims=True))
            dsm_ref[0:1, sl] += jnp.sum(dog * yhat, axis=0, keepdims=True)
            dy_ref[:, sl] = dyz * gz[:, sl]
            dzdt_ref[:, sl] = (dyz * yv[:, sl] * (sz[:, sl] * (1.0 + zv[:, sl] * (1.0 - sz[:, sl])))).astype(bf16)

        dy = [dy_ref[:, pl.ds(GW * g, GW)] for g in grp]
        dy_b = [dy[g].astype(bf16) for g in grp]
        hl = lambda h: slice(HD * (h % HG), HD * (h % HG + 1))
        dt_off_b = [(dy[g] * c["e"][g]).astype(bf16) for g in grp]
        dcm = [_dot(dt_off_b[g], c["prev_b"][g], NN_DIMS) for g in grp]
        dprev = [_dot(dt_off_b[g], c["cm_b"][g], TN_DIMS) for g in grp]
        yoff_rs = [_head_rowsums(dy[g] * c["y_off"][g], g) for g in grp]
        dhn = [dh_ref[pl.ds(GW * g, GW), :] for g in grp]
        dhn_b = [dhn[g].astype(bf16) for g in grp]
        dprev = [dprev[g] + dhn[g] * _row_expand(c["cd"], g) for g in grp]
        dhn_prev = [dhn[g] * c["prev"][g] for g in grp]
        u = [_dot(c["bm_b"][g], dhn_b[g], NT_DIMS) for g in grp]
        dbm = [_dot(c["xdte_b"][g], dhn_b[g], NN_DIMS) for g in grp]
        ddte_rs = [_head_rowsums(c["xdt"][g] * u[g], g) for g in grp]
        dm = [_dot(dy_b[h // HG][:, hl(h)], c["xdt_b"][h // HG][:, hl(h)], NT_DIMS) for h in heads]
        dxdt_in = [_dot(c["m_b"][h], dy_b[h // HG][:, hl(h)], TN_DIMS) for h in heads]
        dseg = [dm[h] * c["m"][h] for h in heads]
        dmd = [dm[h] * c["decay"][h] for h in heads]
        for h in heads:
            dx_ref[:, pl.ds(HD * h, HD)] = dxdt_in[h]

        tmp = (ddte_rs[0] + ddte_rs[1]) * c["dte"]
        dacs = yoff_rs[0] + yoff_rs[1] - tmp
        dacs_cols = jnp.zeros((NSSM, BLK), f32)
        ddtv = jnp.zeros((BLK, NSSM), f32)
        ddsk = jnp.zeros((BLK, NSSM), f32)
        hp = jnp.zeros((1, NSSM), f32)
        for g in grp:
            cols = pl.ds(GW * g, GW)
            dxdt = dx_ref[:, cols] + u[g] * c["dte_x"][g]
            dx_ref[:, cols] = dy[g] * _lane_expand(dskip, g) + dxdt * c["dt"][g]
            ddtv = ddtv + _head_rowsums(dxdt * c["x"][g], g)
            ddsk = ddsk + _head_rowsums(dy[g] * c["x"][g], g)
            dcb = dmd[HG * g]
            for r in range(1, HG):
                dcb = dcb + dmd[HG * g + r]
            dcb_b = dcb.astype(bf16)
            dx_ref[:, pl.ds(D_SSM + NSTATE * g, NSTATE)] = dbm[g] + _dot(dcb_b, c["cm_b"][g], TN_DIMS)
            dx_ref[:, pl.ds(D_SSM + NGRP * NSTATE + NSTATE * g, NSTATE)] = dcm[g] + _dot(dcb_b, c["bm_b"][g], NN_DIMS)
            dh_ref[cols, :] = dprev[g]
            hp = hp + _head_blocksums(jnp.sum(dhn_prev[g], axis=1, keepdims=True), g)
            for r in range(HG):
                h = HG * g + r
                dacs = dacs + (lane8 == h).astype(f32) * jnp.sum(dseg[h], axis=1, keepdims=True)
                dacs_cols = dacs_cols + (sub8 == h).astype(f32) * jnp.sum(dseg[h], axis=0, keepdims=True)
        dlast = hp * c["cd"] + jnp.sum(tmp, axis=0, keepdims=True)
        ddsk = jnp.sum(ddsk, axis=0, keepdims=True)

        row = lax.broadcasted_iota(jnp.int32, (BLK, 1), 0)
        dacs = dacs - dacs_cols.T + jnp.where(row == BLK - 1, dlast, 0.0)
        dda = lax.dot_general(_tri().astype(f32), dacs, TN_DIMS, preferred_element_type=f32, precision=HIGHEST)
        ddtv = ddtv + dda * a
        da = jnp.sum(dda * dtv, axis=0, keepdims=True)
        draw = ddtv * _sigmoid(raw)
        dzdt_ref[:, D_SSM:] = jnp.zeros((BLK, COL_XBC - COL_DT), bf16)
        dzdt_ref[:, D_SSM:D_SSM + NSSM] = draw.astype(bf16)
        dsm_ref[1:2, 0:NSSM] += jnp.sum(draw, axis=0, keepdims=True)
        dsm_ref[2:3, 0:NSSM] += da * a
        dsm_ref[3:4, 0:NSSM] += ddsk

    rev = lambda i: NBLK - 1 - i
    small = lambda shape: pl.BlockSpec(shape, lambda i: (0,) * len(shape))
    return pl.pallas_call(
        body, grid=(NBLK,),
        in_specs=[pl.BlockSpec((BLK, D_CONV), lambda i: (rev(i), 0)), pl.BlockSpec((BLK, D_SSM), lambda i: (rev(i), 0)),
                  pl.BlockSpec((BLK, 128), lambda i: (rev(i), 0)), pl.BlockSpec((BLK, D_SSM), lambda i: (rev(i), 1)),
                  pl.BlockSpec((None, NSSM * HD, NSTATE), lambda i: (rev(i), 0, 0)), pl.BlockSpec((BLK, D_SSM), lambda i: (rev(i), 0)),
                  small((DEPTH, NSSM)), small((DEPTH, NSSM)), small((DEPTH, NSSM)), small((DEPTH, D_SSM)), ANY_SPEC],
        out_specs=[pl.BlockSpec((BLK, COL_XBC - COL_Z), lambda i: (rev(i), COL_Z // (COL_XBC - COL_Z))),
                   pl.BlockSpec((BLK, D_CONV), lambda i: (rev(i), 0)), small((8, D_SSM))],
        out_shape=_out_hbm([SDS((S, D_IN_PAD), bf16), SDS((S, D_CONV), f32), SDS((8, D_SSM), f32)]),
        scratch_shapes=[pltpu.VMEM((NSSM * HD, NSTATE), f32), pltpu.VMEM((BLK, D_SSM), f32)],
        name="ssd_bwd", input_output_aliases={10: 0}, compiler_params=_cparams(1),
    )(*_in_hbm([xact, z, dt, dmix, hs, y, dt_bias, a_log, d_skip, norm_g, dproj]))


def _my_place():
    return lax.axis_index("x"), lax.axis_index("y"), lax.axis_index("c")


def _dev_index(px, py, pc):
    return 4 * px + 2 * py + pc


def _slab2(kind, ref, idx):
    if kind == "stack":
        return ref.at[idx]
    if kind == "rows128":
        return ref.at[pl.ds(pl.multiple_of(idx * 128, 128), 128), :]
    if kind == "rows512":
        return ref.at[pl.ds(pl.multiple_of(idx * 512, 512), 512), :]
    return ref.at[:, pl.ds(pl.multiple_of(idx * 512, 512), 512)]


def _slab_shape(kind, full_shape):
    if kind == "stack":
        return tuple(full_shape[1:])
    if kind == "rows128":
        return (128, full_shape[1])
    if kind == "rows512":
        return (512, full_shape[1])
    return (full_shape[0], 512)


KIND = dict(w_in="stack", w_out="rows128", w_up="cols512", w_down="rows512", conv_w="stack")
FULL_SHAPE = dict(w_in=(N_DEV, D, D_IN // N_DEV), w_out=(D, D), w_up=(D, D_FF), w_down=(D_FF, D))
HBM_SPEC = pl.BlockSpec(memory_space=pltpu.HBM)
SEM_SPEC = pl.BlockSpec(memory_space=pltpu.SEMAPHORE)
SIDE_EFFECT = pltpu.SideEffectType.DATAFLOW_SIDE_EFFECTING


def _peers_all():
    x, y, c = _my_place()
    return [(x ^ ((r >> 2) & 1), y ^ ((r >> 1) & 1), c ^ (r & 1)) for r in range(1, N_DEV)]


def _split_start(name, bufs, n_copies, plan, deps=()):
    nb = len(bufs)

    def body(*refs):
        ins = refs[:nb]
        send_sems, recv_sems = refs[nb + len(deps)], refs[nb + len(deps) + 1]
        token = refs[-1]
        for i, (src, dst, dev) in enumerate(plan(ins)):
            pltpu.make_async_remote_copy(src_ref=src, dst_ref=dst, send_sem=send_sems.at[i], recv_sem=recv_sems.at[i],
                                         device_id=dev, device_id_type=MESH).start()
        token[...] = jnp.zeros_like(token)

    outs = pl.pallas_call(
        body, name=name,
        out_shape=(pltpu.SemaphoreType.DMA((n_copies,)), pltpu.SemaphoreType.DMA((n_copies,)),
                   *[pltpu.HBM(b.shape, b.dtype) for b in bufs], SDS((8, 128), f32)),
        in_specs=[HBM_SPEC] * nb + [ANY_SPEC] * len(deps),
        out_specs=(SEM_SPEC, SEM_SPEC, *[HBM_SPEC] * nb, pl.BlockSpec(memory_space=pltpu.VMEM)),
        input_output_aliases={i: 2 + i for i in range(nb)},
        compiler_params=pltpu.CompilerParams(has_side_effects=SIDE_EFFECT),
    )(*[pltpu.with_memory_space_constraint(b, pltpu.HBM) for b in bufs], *deps)
    return dict(send=outs[0], recv=outs[1], bufs=list(outs[2:2 + nb]), token=outs[-1], plan=plan, n=n_copies)


def _split_wait(name, started, after):
    bufs = started["bufs"]
    nb = len(bufs)
    plan = started["plan"]
    first = started.get("first", 0)

    def body(*refs):
        ins = refs[:nb]
        send_sems, recv_sems = refs[nb], refs[nb + 1]
        for i, (src, dst, dev) in enumerate(plan(ins)):
            cp = pltpu.make_async_remote_copy(src_ref=src, dst_ref=dst, send_sem=send_sems.at[first + i],
                                              recv_sem=recv_sems.at[first + i], device_id=dev, device_id_type=MESH)
            cp.wait_send()
            cp.wait_recv()

    outs = pl.pallas_call(
        body, name=name, out_shape=tuple(pltpu.HBM(b.shape, b.dtype) for b in bufs),
        in_specs=[HBM_SPEC] * nb + [SEM_SPEC, SEM_SPEC] + [ANY_SPEC] * len(after), out_specs=(HBM_SPEC,) * nb,
        input_output_aliases={i: i for i in range(nb)},
        compiler_params=pltpu.CompilerParams(has_side_effects=SIDE_EFFECT),
    )(*bufs, started["send"], started["recv"], *after)
    return list(outs)


def _gather_plan(names):
    def plan(refs):
        x, y, c = _my_place()
        my_idx = _dev_index(x, y, c)
        targets = [(x, y, 1 - c), (1 - x, y, c), (x, 1 - y, c), (1 - x, 1 - y, c)]
        slabs = [_slab2(KIND[n], refs[t], my_idx) for t, n in enumerate(names)]
        return [(slab, slab, dev) for slab in slabs for dev in targets]

    return plan


def _gather_start(name, names, fulls, deps):
    return _split_start(name, list(fulls), 4 * len(names), _gather_plan(names), deps)


def _gather_part(started, names, lo):
    return dict(send=started["send"], recv=started["recv"], bufs=started["bufs"][lo:lo + len(names)],
                plan=_gather_plan(names), first=4 * lo)


def _gather_finish(name, names, started, after, between=None):
    n_t = len(names)
    fulls = _split_wait(name + "_wait", started, after)
    slab_shapes = [SDS(_slab_shape(KIND[n], f.shape), f.dtype) for n, f in zip(names, fulls)]
    deps = tuple(between(fulls)) if between else ()

    def body(*refs):
        ins = refs[:n_t]
        outs, stage = refs[n_t + len(deps):][:n_t], refs[2 * n_t + len(deps):][:n_t]
        load_sems, send_sems, recv_sems = refs[3 * n_t + len(deps):]
        x, y, c = _my_place()
        chips = [(1 - x, y), (x, 1 - y), (1 - x, 1 - y)]
        pairs = [(t, j) for t in range(n_t) for j in range(3)]
        loads = [pltpu.make_async_copy(_slab2(KIND[names[t]], ins[t], _dev_index(*chips[j], c)), stage[t].at[j], load_sems.at[t, j])
                 for t, j in pairs]
        for cp in loads:
            cp.start()

        def copy(t, j, core):
            return pltpu.make_async_remote_copy(
                src_ref=stage[t].at[j], dst_ref=_slab2(KIND[names[t]], outs[t], _dev_index(*chips[j], core)),
                send_sem=send_sems.at[t, j], recv_sem=recv_sems.at[t, j], device_id=(x, y, 1 - c), device_id_type=MESH)

        sends = [copy(t, j, c) for t, j in pairs]
        for ld, cp in zip(loads, sends):
            ld.wait()
            cp.start()
        for t, j in pairs:
            copy(t, j, 1 - c).wait_recv()
        for cp in sends:
            cp.wait_send()

    return pl.pallas_call(
        body, in_specs=[ANY_SPEC] * (n_t + len(deps)), out_specs=[ANY_SPEC] * n_t,
        out_shape=[SDS(b.shape, b.dtype) for b in fulls], input_output_aliases={t: t for t in range(n_t)},
        scratch_shapes=[pltpu.VMEM((3,) + s.shape, s.dtype) for s in slab_shapes]
        + [pltpu.SemaphoreType.DMA((n_t, 3)), pltpu.SemaphoreType.DMA((n_t, 3)), pltpu.SemaphoreType.DMA((n_t, 3))],
        name=name + "_pass", compiler_params=pltpu.CompilerParams(vmem_limit_bytes=VMEM_LIMIT),
    )(*fulls, *deps)


def _exchange_start(name, names, grads, deps):
    n_t = len(names)
    lands = [lax.empty((N_DEV,) + _slab_shape(KIND[n], g.shape), g.dtype) for n, g in zip(names, grads)]

    def plan(refs):
        my_idx = _dev_index(*_my_place())
        return [(_slab2(KIND[names[t]], refs[t], _dev_index(*peer)), refs[n_t + t].at[my_idx], peer)
                for t in range(n_t) for peer in _peers_all()]

    return _split_start(name, list(grads) + lands, 7 * n_t, plan, deps)


def _small_exchange_start(part, deps):
    land = lax.empty((N_DEV,) + part.shape, part.dtype)

    def plan(refs):
        my_idx = _dev_index(*_my_place())
        return [(refs[0], refs[1].at[my_idx], peer) for peer in _peers_all()]

    return _split_start("small_exchange", [part, land], N_DEV - 1, plan, deps)


def _slab_pieces():
    sh = D_IN // N_DEV
    out = []
    for j in range(N_DEV):
        for first, end, dst in IN_SEGMENTS:
            lo, hi = max(first, sh * j), min(end, sh * (j + 1))
            if lo < hi:
                out.append((j, lo - sh * j, hi - sh * j, dst + lo - first))
    return out


def _w_in_assemble(stacked):
    tr = 256
    sh = D_IN // N_DEV

    def body(i_ref, o_ref):
        o_ref[:, COL_DT:COL_XBC] = jnp.zeros((tr, COL_XBC - COL_DT), bf16)
        for j, lo, hi, dst in _slab_pieces():
            o_ref[:, dst:dst + hi - lo] = i_ref[j, :, lo:hi]

    return pl.pallas_call(
        body, grid=(D // tr,), in_specs=[pl.BlockSpec((N_DEV, tr, sh), lambda i: (0, i, 0))],
        out_specs=pl.BlockSpec((None, tr, D_IN_PAD), lambda i: (0, i, 0)), out_shape=_out_hbm(SDS((1, D, D_IN_PAD), bf16)),
        name="w_in_assemble", compiler_params=_cparams(1),
    )(*_in_hbm([stacked]))


def _w_in_slabs(dw_in):
    tr = 256
    sh = D_IN // N_DEV

    def body(i_ref, o_ref):
        for j, lo, hi, src in _slab_pieces():
            o_ref[j, :, lo:hi] = i_ref[:, src:src + hi - lo]

    return pl.pallas_call(
        body, grid=(D // tr,), in_specs=[pl.BlockSpec((tr, D_IN_PAD), lambda i: (i, 0))],
        out_specs=pl.BlockSpec((N_DEV, tr, sh), lambda i: (0, i, 0)), out_shape=_out_hbm(SDS((N_DEV, D, sh), bf16)),
        name="w_in_slabs", compiler_params=_cparams(1),
    )(*_in_hbm([dw_in]))


SMALL_NAMES = ("mix_norm_g", "mlp_norm_g", "conv_b", "ssm_norm_g", "q_gain", "k_gain", "sinks", "dt_bias", "a_log", "d_skip",
               "rel_bias", "conv_w")
MISC_LANES = dict(q_gain=(LANE_QG, HD), k_gain=(LANE_KG, HD), sinks=(LANE_SINK, NQ), dt_bias=(LANE_DTB, NSSM),
                  a_log=(LANE_ALOG, NSSM), d_skip=(LANE_DSKIP, NSSM))


def _pack_small_grads(smalls, drel_t, loss):
    def body(*refs):
        o_ref = refs[-1]
        drel_ref, loss_ref = refs[-3], refs[-2]
        o_ref[...] = jnp.zeros_like(o_ref)
        for l in range(DEPTH):
            mixg, mlpg, convb, convw, ssd, attn = refs[6 * l:6 * l + 6]
            o_ref[ROW_MIXG + l:ROW_MIXG + l + 1, :] = mixg[...]
            o_ref[ROW_MLPG + l:ROW_MLPG + l + 1, :] = mlpg[...]
            o_ref[ROW_CONVB + l:ROW_CONVB + l + 1, :] = convb[0:1, :]
            o_ref[ROW_SSMG + l:ROW_SSMG + l + 1, 0:D_SSM] = ssd[0:1, :]
            o_ref[ROW_CONVW + 4 * l:ROW_CONVW + 4 * l + 4, :] = convw[0:4, :]
            row = slice(ROW_MISC + l, ROW_MISC + l + 1)
            o_ref[row, LANE_QG:LANE_QG + HD] = attn[0:1, 0:HD]
            o_ref[row, LANE_KG:LANE_KG + HD] = attn[1:2, 0:HD]
            o_ref[row, LANE_SINK:LANE_SINK + NQ] = attn[2:3, 0:NQ]
            o_ref[row, LANE_DTB:LANE_DTB + NSSM] = ssd[1:2, 0:NSSM]
            o_ref[row, LANE_ALOG:LANE_ALOG + NSSM] = ssd[2:3, 0:NSSM]
            o_ref[row, LANE_DSKIP:LANE_DSKIP + NSSM] = ssd[3:4, 0:NSSM]
        o_ref[ROW_RELB:ROW_RELB + NQ, 0:N_BUCKETS] = drel_ref[...]
        o_ref[ROW_LOSS:ROW_LOSS + 1, 0:1] = loss_ref[0:1, 0:1]

    args = []
    for sm in smalls:
        args += [sm["mix_norm_g"], sm["mlp_norm_g"], sm["conv_b"], sm["conv_w"], sm["ssd"], sm["attn"]]
    args += [drel_t, loss]
    return pl.pallas_call(body, out_shape=SDS((SMALL_ROWS, D), f32), name="pack_small_grads")(*args)


def _adamw_small(part, land, w, m, v):
    n = len(SMALL_NAMES)

    def grad_of(name, g_ref):
        if name == "mix_norm_g":
            return g_ref[ROW_MIXG:ROW_MIXG + DEPTH, :]
        if name == "mlp_norm_g":
            return g_ref[ROW_MLPG:ROW_MLPG + DEPTH, :]
        if name == "conv_b":
            return g_ref[ROW_CONVB:ROW_CONVB + DEPTH, :]
        if name == "ssm_norm_g":
            return g_ref[ROW_SSMG:ROW_SSMG + DEPTH, 0:D_SSM]
        if name == "rel_bias":
            return g_ref[ROW_RELB:ROW_RELB + NQ, 0:N_BUCKETS].T
        lane, width = MISC_LANES[name]
        return g_ref[ROW_MISC:ROW_MISC + DEPTH, lane:lane + width]

    def body(part_ref, land_ref, *refs):
        ws, ms, vs = refs[:n], refs[n:2 * n], refs[2 * n:3 * n]
        loss_ref = refs[3 * n]
        outs = refs[3 * n + 1:-1]
        g_ref = refs[-1]
        me = _dev_index(*_my_place())
        for p in range(N_DEV):
            term = jnp.where(me == p, part_ref[...], land_ref[p])
            if p == 0:
                g_ref[...] = term
            else:
                g_ref[...] += term
        loss_ref[...] = g_ref[ROW_LOSS:ROW_LOSS + 1, 0:128]
        my_cols = pl.ds(pl.multiple_of(me * 128, 128), 128)
        for k, name in enumerate(SMALL_NAMES):
            g_out, d_out, m_out, v_out = outs[4 * k:4 * k + 4]
            if name == "conv_w":
                for l in range(DEPTH):
                    g = g_ref[ROW_CONVW + 4 * l:ROW_CONVW + 4 * l + 4, my_cols]
                    delta, m_new, v_new = _adamw_math(ws[k][l], ms[k][l], vs[k][l], g)
                    g_out[l], d_out[l], m_out[l], v_out[l] = g, delta, m_new, v_new
            else:
                g = grad_of(name, g_ref)
                delta, m_new, v_new = _adamw_math(ws[k][...], ms[k][...], vs[k][...], g)
                g_out[...], d_out[...], m_out[...], v_out[...] = g, delta, m_new, v_new

    ws = [w[name] for name in SMALL_NAMES]
    out_shape = [SDS((1, 128), f32)]
    for a in ws:
        out_shape += [SDS(a.shape, f32)] * 4
    return pl.pallas_call(body, out_shape=out_shape, name="adamw_small", scratch_shapes=[pltpu.VMEM((SMALL_ROWS, D), f32)])(
        part, land, *ws, *[m[name] for name in SMALL_NAMES], *[v[name] for name in SMALL_NAMES])


def _plain(tm, tn):
    return pl.BlockSpec((tm, tn), lambda i, j, k: (i, j))


def _rowblk(tm, width):
    return pl.BlockSpec((tm, width), lambda i, j, k: (i, 0))


def _store_epi(dtype):
    def epi(acc, i, j, ex, outs):
        outs[0][...] = acc.astype(dtype)
    return epi


def _rms_prologue(layer):
    def pro(a_ref, ex, outs):
        xv = a_ref[...]
        r = lax.rsqrt(jnp.mean(xv * xv, axis=-1, keepdims=True) + EPS)
        h = (xv * r * ex[0][layer:layer + 1, :]).astype(bf16)
        outs[-1][...] = h
        return h
    return pro


MLP_TM = 256
MLP_VMEM = 56 * 1024 * 1024


def _resident(shape):
    return pl.BlockSpec((None,) + shape, lambda i: (0, 0, 0), pipeline_mode=pl.Buffered(1))


def _mlp_fwd(layer, x, mix, g, w_out, w_up, w_down, tgt=None):
    tm = MLP_TM
    with_loss = tgt is not None

    def body(x_ref, mix_ref, g_ref, wo_ref, wu_ref, wd_ref, *rest):
        xm_ref, a_ref, r_ref, h_ref = rest[with_loss:with_loss + 4]
        rest = rest[:with_loss] + rest[with_loss + 1:]
        i = pl.program_id(0)
        xv = x_ref[...] + _dot(mix_ref[...], wo_ref[...], NN_DIMS)
        xm_ref[...] = xv
        h = (xv * lax.rsqrt(jnp.mean(xv * xv, axis=-1, keepdims=True) + EPS) * g_ref[layer:layer + 1, :]).astype(bf16)
        h_ref[...] = h
        r = jnp.maximum(_dot(h, wu_ref[...], NN_DIMS), 0.0)
        a = (r * r).astype(bf16)
        a_ref[...] = a
        r_ref[...] = r.astype(bf16)
        y = xv + _dot(a, wd_ref[...], NN_DIMS)
        if not with_loss:
            rest[3][...] = y
            return
        err = y - rest[0][...]
        rest[4][...] = err * (1.0 / D)
        part = 0.5 * jnp.sum(jnp.mean(err * err, axis=-1, keepdims=True), axis=0, keepdims=True)

        @pl.when(i == 0)
        def _():
            rest[5][...] = jnp.zeros_like(rest[5])

        rest[5][...] += jnp.broadcast_to(part, rest[5].shape)

    row = lambda width: pl.BlockSpec((tm, width), lambda i: (i, 0))
    in_specs = [row(D), row(D), pl.BlockSpec((DEPTH, D), lambda i: (0, 0)), _resident((D, D)), _resident((D, D_FF)),
                _resident((D_FF, D))]
    out_specs = [row(D), row(D_FF), row(D_FF), row(D), row(D)]
    out_shape = [SDS((S, D), f32), SDS((S, D_FF), bf16), SDS((S, D_FF), bf16), SDS((S, D), bf16), SDS((S, D), f32)]
    args = [x, mix, g, w_out, w_up, w_down]
    if with_loss:
        in_specs.append(row(D))
        args.append(tgt)
        out_specs.append(pl.BlockSpec((1, 128), lambda i: (0, 0)))
        out_shape.append(SDS((1, 128), f32))
    return pl.pallas_call(
        body, grid=(S // tm,), in_specs=in_specs, out_specs=out_specs, out_shape=_out_hbm(out_shape),
        name="mlp_fwd_loss" if with_loss else "mlp_fwd",
        compiler_params=pltpu.CompilerParams(dimension_semantics=("arbitrary",), vmem_limit_bytes=MLP_VMEM),
    )(*_in_hbm(args[:3]), *args[3:6], *_in_hbm(args[6:]))


def _mlp_bwd_act(layer, dx_out, r_act, x_mid, g, w_down, w_up, w_out, deps):
    tm = MLP_TM

    def body(dxo_ref, r_ref, xm_ref, g_ref, wd_ref, wu_ref, wo_ref, *rest):
        du_ref, dx_ref, dg_ref, dmix_ref = rest[len(deps):]
        dxo = dxo_ref[...]
        du = (_dot(dxo.astype(bf16), wd_ref[...], NT_DIMS) * (2.0 * r_ref[...].astype(f32))).astype(bf16)
        du_ref[...] = du
        dh = _dot(du, wu_ref[...], NT_DIMS)
        _rms_bwd_epilogue(layer)(dh, pl.program_id(0), 0, (xm_ref, g_ref, dxo_ref), (dx_ref, dg_ref))
        dmix_ref[...] = _dot(dx_ref[...].astype(bf16), wo_ref[...], NT_DIMS)

    row = lambda width: pl.BlockSpec((tm, width), lambda i: (i, 0))
    return pl.pallas_call(
        body, grid=(S // tm,),
        in_specs=[row(D), row(D_FF), row(D), pl.BlockSpec((DEPTH, D), lambda i: (0, 0)), _resident((D_FF, D)), _resident((D, D_FF)),
                  _resident((D, D))] + [ANY_SPEC] * len(deps),
        out_specs=[row(D_FF), row(D), pl.BlockSpec((1, D), lambda i: (0, 0)), row(D)],
        out_shape=_out_hbm([SDS((S, D_FF), bf16), SDS((S, D), f32), SDS((1, D), f32), SDS((S, D), f32)]), name="mlp_bwd_act",
        compiler_params=pltpu.CompilerParams(dimension_semantics=("arbitrary",), vmem_limit_bytes=MLP_VMEM),
    )(*_in_hbm([dx_out, r_act, x_mid, g]), w_down, w_up, w_out, *_in_hbm(deps))


def _layer_fwd(l, x, p, get_weights, bias, tgt=None):
    wts = get_weights(l, "in", [x, bias])
    gfull = pl.BlockSpec((DEPTH, D), lambda i, j, k: (0, 0))
    tm = 512

    def inproj_epi(acc, i, j, ex, outs):
        outs[0][...] = acc[:, COL_QKV:COL_Z].astype(bf16)
        outs[1][...] = acc[:, COL_Z:COL_DT].astype(bf16)
        outs[2][...] = acc[:, COL_XBC:D_IN_PAD].astype(bf16)
        outs[3][...] = acc[:, COL_DT:COL_DT + 128]

    qkv, z, xbc, dt, h1 = _matmul(
        "in_proj", "nn", x, wts["w_in"], tm=tm, tn=D_IN_PAD, tk=D, prologue=_rms_prologue(l),
        extras=(p["mix_norm_g"],), extra_specs=(gfull,),
        out_shape=[SDS((S, 768), bf16), SDS((S, 512), bf16), SDS((S, 1024), bf16), SDS((S, 128), f32), SDS((S, D), bf16)],
        out_specs=[_rowblk(tm, 768), _rowblk(tm, 512), _rowblk(tm, 1024), _rowblk(tm, 128), _rowblk(tm, D)], epilogue=inproj_epi)
    attn = _attn_fwd(qkv, p["q_gain"], p["k_gain"], p["sinks"], bias, l)
    xact = _conv_fwd(xbc, wts["conv_w"], p["conv_b"], l)
    mix, hs, y_ssd = _ssd_fwd(xact, z, dt, attn, p["dt_bias"], p["a_log"], p["d_skip"], p["ssm_norm_g"], l)
    wts = dict(wts, **get_weights(l, "rest", [mix]))

    x_mid, a_act, r_act, h2, *result = _mlp_fwd(l, x, mix, p["mlp_norm_g"], wts["w_out"], wts["w_up"], wts["w_down"], tgt)
    saved = dict(x=x, h1=h1, qkv=qkv, z=z, xbc=xbc, dt=dt, xact=xact, mix=mix, hs=hs, y_ssd=y_ssd, x_mid=x_mid, h2=h2,
                 a=a_act, r=r_act, wts=wts)
    return (result[0] if tgt is None else tuple(result)), saved


def _layer_bwd(l, dx_out, sv, p, bias, deps, send):
    wts = sv["wts"]

    dw_down = _matmul("dw_down", "tn", sv["a"], dx_out, tm=1024, tn=D, tk=S, out_shape=SDS((D_FF, D), bf16),
                      out_specs=_plain(1024, D), epilogue=_store_epi(bf16), deps=deps)
    deps = send(l, dict(w_down=dw_down))
    du, dx_mid, dg_mlp, dmix = _mlp_bwd_act(l, dx_out, sv["r"], sv["x_mid"], p["mlp_norm_g"], wts["w_down"], wts["w_up"],
                                            wts["w_out"], deps)
    dw_up = _matmul("dw_up", "tn", sv["h2"], du, tm=D, tn=1024, tk=S, out_shape=SDS((D, D_FF), bf16),
                    out_specs=_plain(D, 1024), epilogue=_store_epi(bf16))
    dw_out = _matmul("dw_out", "tn", sv["mix"], dx_mid, tm=D, tn=512, tk=S, out_shape=SDS((D, D), bf16),
                     out_specs=_plain(D, 512), epilogue=_store_epi(bf16))
    deps = send(l, dict(w_up=dw_up, w_out=dw_out))
    gfull = pl.BlockSpec((DEPTH, D), lambda i, j, k: (0, 0))
    grow = pl.BlockSpec((1, D), lambda i, j, k: (0, 0))
    dproj, dbias, dsm_attn = _attn_bwd(sv["qkv"], dmix, p["q_gain"], p["k_gain"], p["sinks"], bias, l, deps)
    dproj, dxact, dsm_ssd = _ssd_bwd(sv["xact"], sv["z"], sv["dt"], dmix, sv["hs"], sv["y_ssd"], p["dt_bias"], p["a_log"],
                                     p["d_skip"], p["ssm_norm_g"], dproj, l)
    dproj, dconv_w, dconv_b = _conv_bwd(sv["xbc"], dxact, wts["conv_w"], p["conv_b"], dproj, l)
    dw_in = _matmul("dw_in", "tn", sv["h1"], dproj, tm=D, tn=1280, tk=S, out_shape=SDS((D, D_IN_PAD), bf16),
                    out_specs=_plain(D, 1280), epilogue=_store_epi(bf16))
    deps = send(l, dict(w_in=_w_in_slabs(dw_in)))
    dx, dg_mix = _matmul(
        "in_proj_dh", "nt", dproj, wts["w_in"], tm=512, tn=D, tk=D_IN_PAD, out_shape=[SDS((S, D), f32), SDS((1, D), f32)],
        out_specs=[_plain(512, D), grow], epilogue=_rms_bwd_epilogue(l),
        extras=(sv["x"], p["mix_norm_g"], dx_mid), extra_specs=(_plain(512, D), gfull, _plain(512, D)), deps=deps)
    small = dict(mix_norm_g=dg_mix, mlp_norm_g=dg_mlp, conv_w=dconv_w, conv_b=dconv_b, ssd=dsm_ssd, attn=dsm_attn, dbias=dbias)
    return dx, small, deps


def _local_step(x, tgt, p, get_weights, send):
    onehot_t = jnp.asarray(_onehot_buckets(), dtype=bf16)
    bias = _bias_build(p["rel_bias"].T, onehot_t).reshape(NQ, BLK, 2 * BLK)
    saved = []
    h = x
    for l in range(DEPTH):
        h, sv = _layer_fwd(l, h, p, get_weights, bias, tgt if l == DEPTH - 1 else None)
        saved.append(sv)
    dx, loss = h
    smalls = [None] * DEPTH
    deps = ()
    for l in reversed(range(DEPTH)):
        dx, smalls[l], deps = _layer_bwd(l, dx, saved[l], p, bias, deps, send)
    drel_t = _bias_grad(smalls[0]["dbias"].reshape(NQ, -1), smalls[1]["dbias"].reshape(NQ, -1), onehot_t)
    return dx, _pack_small_grads(smalls, drel_t, loss)


WEIGHT_ORDER = ("mix_norm_g", "w_in", "q_gain", "k_gain", "sinks", "rel_bias", "conv_w", "conv_b", "dt_bias", "a_log", "d_skip",
                "ssm_norm_g", "w_out", "mlp_norm_g", "w_up", "w_down")


def kernel(x, mix_norm_g, w_in, q_gain, k_gain, sinks, rel_bias, conv_w, conv_b, dt_bias, a_log, d_skip, ssm_norm_g, w_out, mlp_norm_g, w_up, w_down, loss_target, m_mix_norm_g, m_w_in, m_q_gain, m_k_gain, m_sinks, m_rel_bias, m_conv_w, m_conv_b, m_dt_bias, m_a_log, m_d_skip, m_ssm_norm_g, m_w_out, m_mlp_norm_g, m_w_up, m_w_down, v_mix_norm_g, v_w_in, v_q_gain, v_k_gain, v_sinks, v_rel_bias, v_conv_w, v_conv_b, v_dt_bias, v_a_log, v_d_skip, v_ssm_norm_g, v_w_out, v_mlp_norm_g, v_w_up, v_w_down):
    w = dict(mix_norm_g=mix_norm_g, w_in=w_in, q_gain=q_gain, k_gain=k_gain, sinks=sinks, rel_bias=rel_bias, conv_w=conv_w,
             conv_b=conv_b, dt_bias=dt_bias, a_log=a_log, d_skip=d_skip, ssm_norm_g=ssm_norm_g, w_out=w_out,
             mlp_norm_g=mlp_norm_g, w_up=w_up, w_down=w_down)
    m = dict(mix_norm_g=m_mix_norm_g, w_in=m_w_in, q_gain=m_q_gain, k_gain=m_k_gain, sinks=m_sinks, rel_bias=m_rel_bias,
             conv_w=m_conv_w, conv_b=m_conv_b, dt_bias=m_dt_bias, a_log=m_a_log, d_skip=m_d_skip, ssm_norm_g=m_ssm_norm_g,
             w_out=m_w_out, mlp_norm_g=m_mlp_norm_g, w_up=m_w_up, w_down=m_w_down)
    v = dict(mix_norm_g=v_mix_norm_g, w_in=v_w_in, q_gain=v_q_gain, k_gain=v_k_gain, sinks=v_sinks, rel_bias=v_rel_bias,
             conv_w=v_conv_w, conv_b=v_conv_b, dt_bias=v_dt_bias, a_log=v_a_log, d_skip=v_d_skip, ssm_norm_g=v_ssm_norm_g,
             w_out=v_w_out, mlp_norm_g=v_mlp_norm_g, w_up=v_w_up, w_down=v_w_down)
    big = ("w_in", "w_out", "w_up", "w_down")

    my_idx = _dev_index(*_my_place()).astype(jnp.int32).reshape(1)

    fulls = {n: _cast_to_full("cast_" + n, w[n], KIND[n], FULL_SHAPE[n], my_idx, bf16) for n in big}
    conv_full = _cast_to_full("cast_conv_w", conv_w.reshape(1, DEPTH * 4, 128), "stack", (N_DEV, DEPTH * 4, 128), my_idx, f32)[0]
    rest = ["w_out", "w_up", "w_down"]
    g0 = _gather_start("gather0", ["w_in", "conv_w"], [fulls["w_in"][0], conv_full], ())
    later_fulls = [fulls[n][0] for n in rest] + [fulls["w_in"][1]] + [fulls[n][1] for n in rest]
    held = {}
    flat = lambda a: a.reshape(a.shape[0] * a.shape[1], a.shape[2])
    adam_in = {n: (flat(w[n]), flat(m[n]), flat(v[n])) for n in big}

    def start_later(arrived):
        later = _gather_start("gather_later", rest + ["w_in"] + rest, later_fulls, (arrived[0],))
        held["g1"], held["g2"], held["g3"] = _gather_part(later, rest, 0), _gather_part(later, ["w_in"], 3), _gather_part(later, rest, 4)
        return (later["token"],)

    def get_weights(l, part, after):
        if l == 0 and part == "in":
            full_in, full_conv = _gather_finish("gather0", ["w_in", "conv_w"], g0,
                                                list(after) + later_fulls + [adam_in["w_in"][1], adam_in["w_in"][2]], start_later)
            held["conv_w"] = jnp.transpose(full_conv.reshape(N_DEV, DEPTH, 4, 128), (1, 2, 0, 3)).reshape(DEPTH, 4, D_CONV)
            return dict(w_in=_w_in_assemble(full_in), conv_w=held["conv_w"])
        if part == "in":
            return dict(w_in=_w_in_assemble(_gather_finish("gather2", ["w_in"], held["g2"], after)[0]), conv_w=held["conv_w"])
        full = _gather_finish("gather1" if l == 0 else "gather3", rest, held["g1" if l == 0 else "g3"], after)
        return {n: f[None] for n, f in zip(rest, full)}

    pending = []

    def send(l, grads):
        names = list(grads)
        started = _exchange_start("exchange%d_%s" % (l, names[0]), names, [grads[n] for n in names], ())
        pending.append((l, names, started))
        return (started["token"],)

    dx, small_part = _local_step(x.reshape(S, D), loss_target.reshape(S, D), w, get_weights, send)

    small = _small_exchange_start(small_part, ())
    tiles = dict(w_in=512, w_out=128, w_up=512, w_down=256)
    outs_of = {n: None for n in big}
    after = [dx, small["token"]]
    for l, names, started in pending:
        bufs = _split_wait("exchange%d_%s_wait" % (l, names[0]), started, after)
        for t, n in enumerate(names):
            outs_of[n] = _adamw_layer("adamw_%s%d" % (n, l), KIND[n], l, *adam_in[n],
                                      bufs[len(names) + t], bufs[t], my_idx, outs_of[n], tiles[n])
        after = [outs_of[names[-1]][0]]
    res = {n: [o.reshape(w[n].shape) for o in outs_of[n]] for n in big}
    small_part, small_land = _split_wait("small_exchange_wait", small, after)
    small_outs = _adamw_small(small_part, small_land, w, m, v)
    loss = small_outs[0][0, 0]
    for k, name in enumerate(SMALL_NAMES):
        res[name] = small_outs[1 + 4 * k:5 + 4 * k]

    result = [loss, dx.reshape(1, S, D)]
    for k in range(4):
        result += [res[name][k] for name in WEIGHT_ORDER]
    return tuple(result)
```

```python
import functools
import math

import numpy as np
import jax
import jax.numpy as jnp
from jax import lax
from jax.experimental import pallas as pl
from jax.experimental.pallas import tpu as pltpu

f32 = jnp.float32
bf16 = jnp.bfloat16
SDS = jax.ShapeDtypeStruct
MESH = pl.DeviceIdType.MESH
HIGHEST = lax.Precision.HIGHEST

S = 2048
D = 1024
DEPTH = 2
BLK = 128
NBLK = S // BLK
HD = 64
NQ = 8
NKV = 2
NSSM = 8
NGRP = 2
NSTATE = 128
D_ATTN = 512
D_SSM = 512
D_CONV = 1024
D_FF = 4096
D_IN = 2312
D_IN_PAD = 2560
COL_QKV, COL_Z, COL_DT, COL_XBC = 0, 768, 1280, 1536
IN_SEGMENTS = ((0, 1280, 0), (1280, 2304, COL_XBC), (2304, 2312, COL_DT))
N_BUCKETS = 32
EPS = 1e-6
N_DEV = 8
VMEM_LIMIT = 48 * 1024 * 1024

ADAM_LR = 0.001
ADAM_B1 = 0.9
ADAM_B2 = 0.999
ADAM_EPS = 1e-08
ADAM_WD = 0.01
ADAM_STEP = 10

NT_DIMS = (((1,), (1,)), ((), ()))
TN_DIMS = (((0,), (0,)), ((), ()))
NN_DIMS = (((1,), (0,)), ((), ()))

ROW_MIXG = 0
ROW_MLPG = 2
ROW_CONVB = 4
ROW_SSMG = 6
ROW_MISC = 8
ROW_RELB = 10
ROW_CONVW = 18
ROW_LOSS = 26
SMALL_ROWS = 32
LANE_QG, LANE_KG, LANE_SINK, LANE_DTB, LANE_ALOG, LANE_DSKIP = 0, 64, 128, 256, 384, 512


def _dot(a, b, dims):
    return lax.dot_general(a, b, dims, preferred_element_type=f32)


def _cparams(n_axes):
    return pltpu.CompilerParams(dimension_semantics=("arbitrary",) * n_axes, vmem_limit_bytes=VMEM_LIMIT)


def _sum11(v):
    return jnp.sum(jnp.sum(v, axis=1, keepdims=True), axis=0, keepdims=True)


def _sigmoid(v):
    return 1.0 / (1.0 + jnp.exp(-v))


ANY_SPEC = pl.BlockSpec(memory_space=pl.ANY)


def _in_hbm(args):
    return [pltpu.with_memory_space_constraint(a, pltpu.HBM) if a.size >= 65536 else a for a in args]


def _out_hbm(out_shape):
    one = lambda s: pltpu.HBM(s.shape, s.dtype) if math.prod(s.shape) >= 65536 else s
    return [one(s) for s in out_shape] if isinstance(out_shape, (list, tuple)) else one(out_shape)


def _matmul(name, mode, a, b, *, layer=0, tm, tn, tk, out_shape, out_specs, epilogue, extras=(), extra_specs=(), deps=(),
            prologue=None):
    extras = tuple(extras) + tuple(deps)
    extra_specs = tuple(extra_specs) + (ANY_SPEC,) * len(deps)
    if mode == "tn":
        t_dim, m_dim = a.shape
        n_dim = b.shape[1]
        grid = (m_dim // tm, n_dim // tn, t_dim // tk)
        a_spec = pl.BlockSpec((tk, tm), lambda i, j, k: (k, i))
        b_spec = pl.BlockSpec((tk, tn), lambda i, j, k: (k, j))
        dims = TN_DIMS
    elif mode == "nn":
        m_dim, k_dim = a.shape
        n_dim = b.shape[-1]
        grid = (m_dim // tm, n_dim // tn, k_dim // tk)
        a_spec = pl.BlockSpec((tm, tk), lambda i, j, k: (i, k))
        b_spec = pl.BlockSpec((None, tk, tn), lambda i, j, k: (layer, k, j))
        dims = NN_DIMS
    else:
        m_dim, k_dim = a.shape
        n_dim = b.shape[-2]
        grid = (m_dim // tm, n_dim // tn, k_dim // tk)
        a_spec = pl.BlockSpec((tm, tk), lambda i, j, k: (i, k))
        b_spec = pl.BlockSpec((None, tn, tk), lambda i, j, k: (layer, j, k))
        dims = NT_DIMS
    nk = grid[2]
    n_ex = len(extras)

    def body(a_ref, b_ref, *rest):
        ex = rest[:n_ex - len(deps)]
        outs = rest[n_ex:-1]
        acc = rest[-1]
        i = pl.program_id(0)
        j = pl.program_id(1)
        k = pl.program_id(2)
        lhs = a_ref[...].astype(bf16) if prologue is None else prologue(a_ref, ex, outs)
        part = _dot(lhs, b_ref[...].astype(bf16), dims)
        if nk == 1:
            epilogue(part, i, j, ex, outs)
        else:
            @pl.when(k == 0)
            def _():
                acc[...] = part

            @pl.when(k > 0)
            def _():
                acc[...] += part

            @pl.when(k == nk - 1)
            def _():
                epilogue(acc[...], i, j, ex, outs)

    return pl.pallas_call(
        body, grid=grid, in_specs=[a_spec, b_spec, *extra_specs], out_specs=out_specs, out_shape=_out_hbm(out_shape),
        scratch_shapes=[pltpu.VMEM((tm, tn) if nk > 1 else (8, 128), f32)], name=name, compiler_params=_cparams(3),
    )(*_in_hbm([a]), b, *_in_hbm(extras))


def _rms_bwd_epilogue(layer):
    def epi(acc, i, j, ex, outs):
        x_ref, g_ref, dres_ref = ex
        dx_ref, dg_ref = outs
        xv = x_ref[...]
        r = lax.rsqrt(jnp.mean(xv * xv, axis=-1, keepdims=True) + EPS)
        xhat = xv * r
        w = acc * g_ref[layer:layer + 1, :]
        dx_ref[...] = dres_ref[...] + r * (w - xhat * jnp.mean(xhat * w, axis=-1, keepdims=True))
        dg = jnp.sum(acc * xhat, axis=0, keepdims=True)

        @pl.when(i == 0)
        def _():
            dg_ref[...] = dg

        @pl.when(i > 0)
        def _():
            dg_ref[...] += dg
    return epi


def _own_slab_spec(kind, tr, cols, nblk, blk_of=lambda i: i):
    if kind == "stack":
        return pl.BlockSpec((None, tr, cols), lambda i, idx: (idx[0], blk_of(i), 0))
    if kind == "cols512":
        return pl.BlockSpec((tr, cols), lambda i, idx: (blk_of(i), idx[0]))
    return pl.BlockSpec((tr, cols), lambda i, idx: (idx[0] * nblk + blk_of(i), 0))


def _cast_to_full(name, w, kind, full_shape, my_idx, dtype):
    n_layers, rows, cols = w.shape
    tr = min(rows, 256)
    nblk = rows // tr

    def body(idx_ref, w_ref, *o_refs):
        for l in range(n_layers):
            o_refs[l][...] = w_ref[l].astype(dtype)

    grid_spec = pltpu.PrefetchScalarGridSpec(
        num_scalar_prefetch=1, grid=(nblk,), in_specs=[pl.BlockSpec((n_layers, tr, cols), lambda i, idx: (0, i, 0))],
        out_specs=[_own_slab_spec(kind, tr, cols, nblk)] * n_layers)
    return pl.pallas_call(body, grid_spec=grid_spec, out_shape=_out_hbm([SDS(full_shape, dtype)] * n_layers), name=name,
                          compiler_params=_cparams(1))(*_in_hbm([my_idx, w]))


def _adamw_math(w, m, v, g):
    m_new = ADAM_B1 * m + (1.0 - ADAM_B1) * g
    v_new = ADAM_B2 * v + (1.0 - ADAM_B2) * (g * g)
    m_hat = m_new / (1.0 - ADAM_B1 ** ADAM_STEP)
    v_hat = v_new / (1.0 - ADAM_B2 ** ADAM_STEP)
    delta = -ADAM_LR * (m_hat / (jnp.sqrt(v_hat) + ADAM_EPS) + ADAM_WD * w)
    return delta, m_new, v_new


def _adamw_tensor(name, kind, w, m, v, lands, g_fulls, my_idx, tr):
    rows2, cols = w.shape
    rows = rows2 // DEPTH
    nblk = rows // tr
    turn = lambda l: DEPTH - 1 - l
    blk_of = lambda l: (lambda s: jnp.clip(s - turn(l) * nblk, 0, nblk - 1))

    def body(idx_ref, w_ref, m_ref, v_ref, *rest):
        land_refs, own_refs = rest[:DEPTH], rest[DEPTH:2 * DEPTH]
        g_ref, d_ref, mo_ref, vo_ref = rest[2 * DEPTH:]
        me = idx_ref[0]
        for l in range(DEPTH):
            @pl.when(pl.program_id(0) // nblk == turn(l))
            def _():
                g = None
                for p in range(N_DEV):
                    part = jnp.where(me == p, own_refs[l][...], land_refs[l][p]).astype(f32)
                    g = part if g is None else g + part
                delta, m_new, v_new = _adamw_math(w_ref[...], m_ref[...], v_ref[...], g)
                g_ref[...] = g
                d_ref[...] = delta
                mo_ref[...] = m_new
                vo_ref[...] = v_new

    blk = pl.BlockSpec((tr, cols), lambda s, idx: ((DEPTH - 1 - s // nblk) * nblk + s % nblk, 0))
    land_specs = [pl.BlockSpec((N_DEV, tr, cols), lambda s, idx, l=l: (0, blk_of(l)(s), 0)) for l in range(DEPTH)]
    own_specs = [_own_slab_spec(kind, tr, cols, nblk, blk_of(l)) for l in range(DEPTH)]
    grid_spec = pltpu.PrefetchScalarGridSpec(
        num_scalar_prefetch=1, grid=(DEPTH * nblk,), in_specs=[blk, blk, blk] + land_specs + own_specs,
        out_specs=[blk, blk, blk, blk])
    return pl.pallas_call(
        body, grid_spec=grid_spec, out_shape=_out_hbm([SDS((rows2, cols), f32)] * 4), name=name, compiler_params=_cparams(1),
    )(*_in_hbm([my_idx, w, m, v, *lands, *g_fulls]))


def _bucket_table():
    qi = np.arange(BLK)[:, None]
    kj = np.arange(2 * BLK)[None, :]
    dist = qi + BLK - kj
    dcl = np.clip(dist, 0, None)
    max_exact = N_BUCKETS // 2
    d_f = np.maximum(dcl, 1).astype(np.float32)
    large = max_exact + (np.log(d_f / np.float32(max_exact)) / np.float32(math.log(128 / max_exact))
                         * np.float32(N_BUCKETS - max_exact)).astype(np.int32)
    large = np.minimum(large, N_BUCKETS - 1)
    bucket = np.where(dcl < max_exact, dcl, large)
    in_window = (dist >= 0) & (dist < BLK)
    return bucket.astype(np.int32), in_window


def _onehot_buckets():
    bucket, _ = _bucket_table()
    oh = (bucket.reshape(-1)[None, :] == np.arange(N_BUCKETS)[:, None]).astype(np.float32)
    return oh


def _bias_build(rel_bias_t, onehot_t):
    def body(r_ref, o_ref, out_ref):
        r = r_ref[...]
        hi = r.astype(bf16)
        r1 = r - hi.astype(f32)
        mid = r1.astype(bf16)
        lo = (r1 - mid.astype(f32)).astype(bf16)
        oh = o_ref[...]
        out_ref[...] = _dot(hi, oh, NN_DIMS) + _dot(mid, oh, NN_DIMS) + _dot(lo, oh, NN_DIMS)

    tn = 4096
    return pl.pallas_call(
        body, grid=(BLK * 2 * BLK // tn,),
        in_specs=[pl.BlockSpec((NQ, N_BUCKETS), lambda i: (0, 0)), pl.BlockSpec((N_BUCKETS, tn), lambda i: (0, i))],
        out_specs=pl.BlockSpec((NQ, tn), lambda i: (0, i)), out_shape=SDS((NQ, BLK * 2 * BLK), f32), name="bias_build",
        compiler_params=_cparams(1),
    )(rel_bias_t, onehot_t)


def _bias_grad(dbias0, dbias1, onehot_t):
    tn = 4096
    nsteps = BLK * 2 * BLK // tn

    def body(a_ref, b_ref, o_ref, out_ref):
        g = a_ref[...] + b_ref[...]
        hi = g.astype(bf16)
        lo = (g - hi.astype(f32)).astype(bf16)
        part = _dot(hi, o_ref[...], NT_DIMS) + _dot(lo, o_ref[...], NT_DIMS)

        @pl.when(pl.program_id(0) == 0)
        def _():
            out_ref[...] = part

        @pl.when(pl.program_id(0) > 0)
        def _():
            out_ref[...] += part

    return pl.pallas_call(
        body, grid=(nsteps,),
        in_specs=[pl.BlockSpec((NQ, tn), lambda i: (0, i)), pl.BlockSpec((NQ, tn), lambda i: (0, i)),
                  pl.BlockSpec((N_BUCKETS, tn), lambda i: (0, i))],
        out_specs=pl.BlockSpec((NQ, N_BUCKETS), lambda i: (0, 0)), out_shape=SDS((NQ, N_BUCKETS), f32), name="bias_grad",
        compiler_params=_cparams(1),
    )(dbias0, dbias1, onehot_t)


def _attn_mask(n):
    qi = lax.broadcasted_iota(jnp.int32, (BLK, 2 * BLK), 0)
    kj = lax.broadcasted_iota(jnp.int32, (BLK, 2 * BLK), 1)
    dist = qi + BLK - kj
    first_key = jnp.where(n > 0, 0, BLK)
    return (dist >= 0) & (dist < BLK) & (kj >= first_key)


def _row_mean(a):
    return jnp.mean(a, axis=-1, keepdims=True)


def _head_norm(t, gain):
    r = lax.rsqrt(_row_mean(t * t) + EPS)
    that = t * r
    return that, r, that * gain


def _softmax_with_sink(s, sink):
    m = jnp.maximum(jnp.max(s, axis=-1, keepdims=True), sink)
    p = jnp.exp(s - m)
    psink = jnp.exp(sink - m)
    inv = 1.0 / (jnp.sum(p, axis=-1, keepdims=True) + psink)
    return p * inv, psink * inv


GQ = NQ // NKV


def _attn_fwd(qkv, q_gain, k_gain, sinks, bias, layer):
    def body(q_ref, kc_ref, kp_ref, vc_ref, vp_ref, qg_ref, kg_ref, sk_ref, bias_ref, o_ref):
        m = pl.program_id(0)
        qg = qg_ref[layer:layer + 1, :]
        kg = kg_ref[layer:layer + 1, :]
        grp = range(NKV)
        chains = [(b, j) for b in range(2) for j in grp]
        masks = [jnp.tile(_attn_mask(2 * m + b), (GQ, 1)) for b in range(2)]
        kblk = [[kp_ref[:, pl.ds(HD * j, HD)].astype(f32), kc_ref[0:BLK, pl.ds(HD * j, HD)].astype(f32),
                 kc_ref[BLK:, pl.ds(HD * j, HD)].astype(f32)] for j in grp]
        vblk = [[vp_ref[:, pl.ds(HD * j, HD)].astype(bf16), vc_ref[0:BLK, pl.ds(HD * j, HD)].astype(bf16),
                 vc_ref[BLK:, pl.ds(HD * j, HD)].astype(bf16)] for j in grp]
        knb = [[_head_norm(kblk[j][t], kg)[2].astype(bf16) for t in range(3)] for j in grp]
        kn_b = {(b, j): jnp.concatenate([knb[j][b], knb[j][b + 1]], axis=0) for b, j in chains}
        vbs = {(b, j): jnp.concatenate([vblk[j][b], vblk[j][b + 1]], axis=0) for b, j in chains}
        rows = {}
        for b, j in chains:
            heads = [GQ * j + g for g in range(GQ)]
            rows[b, j] = (jnp.concatenate([q_ref[pl.ds(BLK * b, BLK), pl.ds(HD * h, HD)] for h in heads], axis=0).astype(f32),
                          jnp.concatenate([jnp.broadcast_to(sk_ref[layer:layer + 1, h:h + 1], (BLK, 1)) for h in heads], axis=0))
        qn_b = {c: _head_norm(rows[c][0], qg)[2].astype(bf16) for c in chains}
        ss = {(b, j): _dot(qn_b[b, j], kn_b[b, j], NT_DIMS) * (HD ** -0.5) + bias_ref[GQ * j:GQ * (j + 1)].reshape(GQ * BLK, 2 * BLK)
              for b, j in chains}
        ps = {(b, j): _softmax_with_sink(jnp.where(masks[b], ss[b, j], -jnp.inf), rows[b, j][1])[0] for b, j in chains}
        outs = {c: _dot(ps[c].astype(bf16), vbs[c], NN_DIMS).astype(bf16) for c in chains}
        for b, j in chains:
            for g in range(GQ):
                o_ref[pl.ds(BLK * b, BLK), pl.ds(HD * (GQ * j + g), HD)] = outs[b, j][BLK * g:BLK * (g + 1), :]

    prev = lambda m: jnp.maximum(2 * m - 1, 0)
    small = lambda shape: pl.BlockSpec(shape, lambda m: (0,) * len(shape))
    return pl.pallas_call(
        body, grid=(NBLK // 2,),
        in_specs=[pl.BlockSpec((2 * BLK, D_ATTN), lambda m: (m, 0)),
                  pl.BlockSpec((2 * BLK, 128), lambda m: (m, 4)), pl.BlockSpec((BLK, 128), lambda m: (prev(m), 4)),
                  pl.BlockSpec((2 * BLK, 128), lambda m: (m, 5)), pl.BlockSpec((BLK, 128), lambda m: (prev(m), 5)),
                  small((DEPTH, HD)), small((DEPTH, HD)), small((DEPTH, NQ)), small((NQ, BLK, 2 * BLK))],
        out_specs=pl.BlockSpec((2 * BLK, D_ATTN), lambda m: (m, 0)), out_shape=_out_hbm(SDS((S, D_ATTN), bf16)),
        name="attn_fwd", compiler_params=_cparams(1),
    )(*_in_hbm([qkv, qkv, qkv, qkv, qkv, q_gain, k_gain, sinks, bias]))


def _attn_bwd(qkv, dmix, q_gain, k_gain, sinks, bias, layer, deps=()):
    def body(q_ref, kc_ref, kp_ref, vc_ref, vp_ref, do_ref, qg_ref, kg_ref, sk_ref, bias_ref, *rest):
        dqkv_ref, dbias_ref, dsm_ref, carry = rest[len(deps):]
        i = pl.program_id(0)
        m = NBLK // 2 - 1 - i
        qg = qg_ref[layer:layer + 1, :]
        kg = kg_ref[layer:layer + 1, :]
        lane = lax.broadcasted_iota(jnp.int32, (1, 128), 1)

        @pl.when(i == 0)
        def _():
            carry[...] = jnp.zeros_like(carry)
            dbias_ref[...] = jnp.zeros_like(dbias_ref)
            dsm_ref[...] = jnp.zeros_like(dsm_ref)

        grp = range(NKV)
        chains = [(b, j) for b in range(2) for j in grp]
        masks = [jnp.tile(_attn_mask(2 * m + b), (GQ, 1)) for b in range(2)]
        kblk = [[kp_ref[:, pl.ds(HD * j, HD)].astype(f32), kc_ref[0:BLK, pl.ds(HD * j, HD)].astype(f32),
                 kc_ref[BLK:, pl.ds(HD * j, HD)].astype(f32)] for j in grp]
        vblk = [[vp_ref[:, pl.ds(HD * j, HD)].astype(bf16), vc_ref[0:BLK, pl.ds(HD * j, HD)].astype(bf16),
                 vc_ref[BLK:, pl.ds(HD * j, HD)].astype(bf16)] for j in grp]
        knorm = [[_head_norm(kblk[j][t], kg) for t in range(3)] for j in grp]
        kn_b = {(b, j): jnp.concatenate([knorm[j][b][2].astype(bf16), knorm[j][b + 1][2].astype(bf16)], axis=0) for b, j in chains}
        vbs = {(b, j): jnp.concatenate([vblk[j][b], vblk[j][b + 1]], axis=0) for b, j in chains}
        rows, do_b = {}, {}
        for b, j in chains:
            heads = [GQ * j + g for g in range(GQ)]
            qrows = pl.ds(BLK * b, BLK)
            rows[b, j] = (jnp.concatenate([q_ref[qrows, pl.ds(HD * h, HD)] for h in heads], axis=0).astype(f32),
                          jnp.concatenate([jnp.broadcast_to(sk_ref[layer:layer + 1, h:h + 1], (BLK, 1)) for h in heads], axis=0))
            do_b[b, j] = jnp.concatenate([do_ref[qrows, pl.ds(HD * h, HD)] for h in heads], axis=0).astype(bf16)
        qnorm = {c: _head_norm(rows[c][0], qg) for c in chains}
        qn_b = {c: qnorm[c][2].astype(bf16) for c in chains}
        ss = {(b, j): _dot(qn_b[b, j], kn_b[b, j], NT_DIMS) * (HD ** -0.5) + bias_ref[GQ * j:GQ * (j + 1)].reshape(GQ * BLK, 2 * BLK)
              for b, j in chains}
        sm = {(b, j): _softmax_with_sink(jnp.where(masks[b], ss[b, j], -jnp.inf), rows[b, j][1]) for b, j in chains}
        dps = {c: _dot(do_b[c], vbs[c], NT_DIMS) for c in chains}
        deltas = {c: jnp.sum(sm[c][0] * dps[c], axis=-1, keepdims=True) for c in chains}
        dss = {c: sm[c][0] * (dps[c] - deltas[c]) for c in chains}
        ds_b = {c: (dss[c] * (HD ** -0.5)).astype(bf16) for c in chains}
        dqn = {c: _dot(ds_b[c], kn_b[c], NN_DIMS) for c in chains}
        dkn = {c: _dot(ds_b[c], qn_b[c], TN_DIMS) for c in chains}
        dvs = {c: _dot(sm[c][0].astype(bf16), do_b[c], TN_DIMS) for c in chains}
        dqg = jnp.zeros((1, HD), f32)
        dkg = jnp.zeros((1, HD), f32)
        dsink = jnp.zeros((1, 128), f32)
        for b, j in chains:
            dbias_ref[GQ * j:GQ * (j + 1)] += dss[b, j].reshape(GQ, BLK, 2 * BLK)
            dsk = sm[b, j][1] * deltas[b, j]
            for g in range(GQ):
                dsink = dsink + jnp.where(lane == GQ * j + g, -_sum11(dsk[BLK * g:BLK * (g + 1), :]), 0.0)
            qhat, rq, _ = qnorm[b, j]
            w = dqn[b, j] * qg
            dq = rq * (w - qhat * _row_mean(qhat * w))
            for g in range(GQ):
                dqkv_ref[pl.ds(BLK * b, BLK), pl.ds(HD * (GQ * j + g), HD)] = dq[BLK * g:BLK * (g + 1), :].astype(bf16)
            dqg = dqg + jnp.sum(dqn[b, j] * qhat, axis=0, keepdims=True)
        for j in grp:
            dkn_t = [dkn[0, j][:BLK, :], dkn[0, j][BLK:, :] + dkn[1, j][:BLK, :], dkn[1, j][BLK:, :]]
            dv_t = [dvs[0, j][:BLK, :], dvs[0, j][BLK:, :] + dvs[1, j][:BLK, :], dvs[1, j][BLK:, :]]
            dk_t = []
            for t in range(3):
                khat, rk, _ = knorm[j][t]
                w = dkn_t[t] * kg
                dk_t.append(rk * (w - khat * _row_mean(khat * w)))
                dkg = dkg + jnp.sum(dkn_t[t] * khat, axis=0, keepdims=True)
            kcols, vcols = pl.ds(D_ATTN + HD * j, HD), pl.ds(D_ATTN + 128 + HD * j, HD)
            dqkv_ref[BLK:, kcols] = (dk_t[2] + carry[:, pl.ds(HD * j, HD)]).astype(bf16)
            dqkv_ref[BLK:, vcols] = (dv_t[2] + carry[:, pl.ds(128 + HD * j, HD)]).astype(bf16)
            dqkv_ref[0:BLK, kcols] = dk_t[1].astype(bf16)
            dqkv_ref[0:BLK, vcols] = dv_t[1].astype(bf16)
            carry[:, pl.ds(HD * j, HD)] = dk_t[0]
            carry[:, pl.ds(128 + HD * j, HD)] = dv_t[0]
        dsm_ref[0:1, 0:HD] += dqg
        dsm_ref[1:2, 0:HD] += dkg
        dsm_ref[2:3, :] += dsink

    rev = lambda i: NBLK // 2 - 1 - i
    prev = lambda i: jnp.maximum(NBLK - 3 - 2 * i, 0)
    small = lambda shape: pl.BlockSpec(shape, lambda i: (0,) * len(shape))
    return pl.pallas_call(
        body, grid=(NBLK // 2,),
        in_specs=[pl.BlockSpec((2 * BLK, D_ATTN), lambda i: (rev(i), 0)),
                  pl.BlockSpec((2 * BLK, 128), lambda i: (rev(i), 4)), pl.BlockSpec((BLK, 128), lambda i: (prev(i), 4)),
                  pl.BlockSpec((2 * BLK, 128), lambda i: (rev(i), 5)), pl.BlockSpec((BLK, 128), lambda i: (prev(i), 5)),
                  pl.BlockSpec((2 * BLK, D_ATTN), lambda i: (rev(i), 0)),
                  small((DEPTH, HD)), small((DEPTH, HD)), small((DEPTH, NQ)), small((NQ, BLK, 2 * BLK))] + [ANY_SPEC] * len(deps),
        out_specs=[pl.BlockSpec((2 * BLK, 768), lambda i: (rev(i), COL_QKV // 768)), small((NQ, BLK, 2 * BLK)), small((8, 128))],
        out_shape=_out_hbm([SDS((S, D_IN_PAD), bf16), SDS((NQ, BLK, 2 * BLK), f32), SDS((8, 128), f32)]),
        scratch_shapes=[pltpu.VMEM((BLK, 256), f32)], name="attn_bwd", compiler_params=_cparams(1),
    )(*_in_hbm([qkv, qkv, qkv, qkv, qkv, dmix, q_gain, k_gain, sinks, bias, *deps]))


CONV_TC = 256


def _shift_down(u, s):
    if s == 0:
        return u
    rows = lax.broadcasted_iota(jnp.int32, u.shape, 0)
    return jnp.where(rows >= s, pltpu.roll(u, s, 0), 0.0)


def _shift_up(u, s):
    if s == 0:
        return u
    rows = lax.broadcasted_iota(jnp.int32, u.shape, 0)
    return jnp.where(rows < u.shape[0] - s, pltpu.roll(u, u.shape[0] - s, 0), 0.0)


def _conv_specs():
    return [pl.BlockSpec((S, CONV_TC), lambda c: (0, c)),
            pl.BlockSpec((None, 4, CONV_TC), lambda c: (0, 0, c)),
            pl.BlockSpec((DEPTH, CONV_TC), lambda c: (0, c))]


def _conv_pre(u, w_ref, b_ref, layer):
    pre = b_ref[layer:layer + 1, :] + w_ref[3:4, :] * u
    for k in range(3):
        pre = pre + w_ref[k:k + 1, :] * _shift_down(u, 3 - k)
    return pre


def _conv_fwd(xbc, conv_w, conv_b, layer):
    def body(u_ref, w_ref, b_ref, o_ref):
        pre = _conv_pre(u_ref[...].astype(f32), w_ref, b_ref, layer)
        o_ref[...] = pre * _sigmoid(pre)

    specs = _conv_specs()
    specs[1] = pl.BlockSpec((None, 4, CONV_TC), lambda c: (layer, 0, c))
    return pl.pallas_call(
        body, grid=(D_CONV // CONV_TC,), in_specs=specs, out_specs=pl.BlockSpec((S, CONV_TC), lambda c: (0, c)),
        out_shape=_out_hbm(SDS((S, D_CONV), f32)), name="conv_fwd", compiler_params=_cparams(1),
    )(*_in_hbm([xbc, conv_w, conv_b]))


def _conv_bwd(xbc, dact, conv_w, conv_b, dproj, layer):
    def body(u_ref, w_ref, b_ref, da_ref, dproj_in, du_ref, dw_ref, db_ref):
        u = u_ref[...].astype(f32)
        pre = _conv_pre(u, w_ref, b_ref, layer)
        sg = _sigmoid(pre)
        dpre = da_ref[...] * (sg * (1.0 + pre * (1.0 - sg)))
        du = w_ref[3:4, :] * dpre
        for k in range(3):
            du = du + w_ref[k:k + 1, :] * _shift_up(dpre, 3 - k)
        du_ref[...] = du.astype(bf16)
        db_ref[...] = jnp.broadcast_to(jnp.sum(dpre, axis=0, keepdims=True), db_ref.shape)
        dw_ref[...] = jnp.zeros_like(dw_ref)
        for k in range(4):
            dw_ref[k:k + 1, :] = jnp.sum(dpre * _shift_down(u, 3 - k), axis=0, keepdims=True)

    specs = _conv_specs()
    specs[1] = pl.BlockSpec((None, 4, CONV_TC), lambda c: (layer, 0, c))
    col = pl.BlockSpec((S, CONV_TC), lambda c: (0, c))
    row8 = pl.BlockSpec((8, CONV_TC), lambda c: (0, c))
    return pl.pallas_call(
        body, grid=(D_CONV // CONV_TC,), in_specs=[*specs, col, ANY_SPEC],
        out_specs=[pl.BlockSpec((S, CONV_TC), lambda c: (0, COL_XBC // CONV_TC + c)), row8, row8],
        out_shape=_out_hbm([SDS((S, D_IN_PAD), bf16), SDS((8, D_CONV), f32), SDS((8, D_CONV), f32)]), name="conv_bwd",
        input_output_aliases={4: 0}, compiler_params=_cparams(1),
    )(*_in_hbm([xbc, conv_w, conv_b, dact, dproj]))


def _tri():
    return (lax.broadcasted_iota(jnp.int32, (BLK, BLK), 0) >= lax.broadcasted_iota(jnp.int32, (BLK, BLK), 1))


def _ssd_scalars(dt_ref, dtb_ref, alog_ref, layer):
    raw = dt_ref[:, 0:NSSM] + dtb_ref[layer:layer + 1, :]
    dtv = jnp.maximum(raw, 0.0) + jnp.log(1.0 + jnp.exp(-jnp.abs(raw)))
    a = -jnp.exp(alog_ref[layer:layer + 1, :])
    acs = jnp.dot(_tri().astype(f32), dtv * a, preferred_element_type=f32, precision=HIGHEST)
    return raw, dtv, a, acs


HG = NSSM // NGRP
GW = HG * HD


def _lane_expand(cols, g):
    lane_head = lax.broadcasted_iota(jnp.int32, (1, GW), 1) // HD
    out = cols[:, HG * g + HG - 1:HG * g + HG]
    for r in range(HG - 2, -1, -1):
        out = jnp.where(lane_head == r, cols[:, HG * g + r:HG * g + r + 1], out)
    return out


def _row_expand(vals, g):
    row_head = lax.broadcasted_iota(jnp.int32, (GW, 1), 0) // HD
    out = vals[:, HG * g + HG - 1:HG * g + HG]
    for r in range(HG - 2, -1, -1):
        out = jnp.where(row_head == r, vals[:, HG * g + r:HG * g + r + 1], out)
    return out


def _head_rowsums(a, g):
    sel = (lax.broadcasted_iota(jnp.int32, (GW, NSSM), 0) // HD + HG * g == lax.broadcasted_iota(jnp.int32, (GW, NSSM), 1)).astype(bf16)
    hi = a.astype(bf16)
    lo = (a - hi.astype(f32)).astype(bf16)
    return _dot(hi, sel, NN_DIMS) + _dot(lo, sel, NN_DIMS)


def _head_blocksums(v, g):
    sel = (lax.broadcasted_iota(jnp.int32, (GW, NSSM), 0) // HD + HG * g == lax.broadcasted_iota(jnp.int32, (GW, NSSM), 1)).astype(bf16)
    hi = v.astype(bf16)
    lo = (v - hi.astype(f32)).astype(bf16)
    return _dot(hi, sel, TN_DIMS) + _dot(lo, sel, TN_DIMS)


def _ssd_chunk_common(xc_ref, dt_ref, dtb_ref, alog_ref, h_rows, layer):
    raw, dtv, a, acs = _ssd_scalars(dt_ref, dtb_ref, alog_ref, layer)
    acs_t = acs.T
    last = acs[BLK - 1:BLK, :]
    c = dict(raw=raw, dtv=dtv, a=a, acs=acs, last=last, dte=jnp.exp(last - acs), e_all=jnp.exp(acs), cd=jnp.exp(last))
    grp, heads, tri = range(NGRP), range(NSSM), _tri()
    c["bm"] = [xc_ref[:, pl.ds(D_SSM + NSTATE * g, NSTATE)] for g in grp]
    c["bm_b"] = [c["bm"][g].astype(bf16) for g in grp]
    c["cm_b"] = [xc_ref[:, pl.ds(D_SSM + NGRP * NSTATE + NSTATE * g, NSTATE)].astype(bf16) for g in grp]
    c["cb"] = [_dot(c["cm_b"][g], c["bm_b"][g], NT_DIMS) for g in grp]
    c["x"] = [xc_ref[:, pl.ds(GW * g, GW)] for g in grp]
    c["dt"] = [_lane_expand(dtv, g) for g in grp]
    c["xdt"] = [c["x"][g] * c["dt"][g] for g in grp]
    c["xdt_b"] = [c["xdt"][g].astype(bf16) for g in grp]
    c["prev"] = [h_rows(g) for g in grp]
    c["prev_b"] = [c["prev"][g].astype(bf16) for g in grp]
    c["e"] = [_lane_expand(c["e_all"], g) for g in grp]
    c["y_off"] = [_dot(c["cm_b"][g], c["prev_b"][g], NT_DIMS) * c["e"][g] for g in grp]
    c["decay"] = [jnp.exp(jnp.where(tri, acs[:, h:h + 1] - acs_t[h:h + 1, :], -jnp.inf)) for h in heads]
    c["m"] = [c["cb"][h // HG] * c["decay"][h] for h in heads]
    c["m_b"] = [c["m"][h].astype(bf16) for h in heads]
    c["dte_x"] = [_lane_expand(c["dte"], g) for g in grp]
    c["xdte_b"] = [(c["xdt"][g] * c["dte_x"][g]).astype(bf16) for g in grp]
    return c


def _ssd_fwd(xact, z, dt, attn, dt_bias, a_log, d_skip, norm_g, layer):
    def body(xc_ref, z_ref, dt_ref, at_ref, dtb_ref, alog_ref, dsk_ref, ng_ref, mix_ref, hs_ref, y_ref, h_ref):
        n = pl.program_id(0)

        @pl.when(n == 0)
        def _():
            h_ref[...] = jnp.zeros_like(h_ref)

        hs_ref[...] = h_ref[...]
        c = _ssd_chunk_common(xc_ref, dt_ref, dtb_ref, alog_ref, lambda g: h_ref[pl.ds(GW * g, GW), :], layer)
        grp, heads = range(NGRP), range(NSSM)
        y_diag = [_dot(c["m_b"][h], c["xdt_b"][h // HG][:, HD * (h % HG):HD * (h % HG + 1)], NN_DIMS) for h in heads]
        new_st = [_dot(c["xdte_b"][g], c["bm_b"][g], TN_DIMS) for g in grp]
        for h in heads:
            y_ref[:, pl.ds(HD * h, HD)] = y_diag[h]
        dskip = dsk_ref[layer:layer + 1, :]
        for g in grp:
            cols = pl.ds(GW * g, GW)
            y_ref[:, cols] = y_ref[:, cols] + c["y_off"][g] + c["x"][g] * _lane_expand(dskip, g)
            h_ref[cols, :] = c["prev"][g] * _row_expand(c["cd"], g) + new_st[g]
        zv = z_ref[...].astype(f32)
        yz = y_ref[...] * (zv * _sigmoid(zv))
        mix_ref[:, 0:D_ATTN] = at_ref[...]
        for g in grp:
            yg = yz[:, GW * g:GW * (g + 1)]
            rs = lax.rsqrt(jnp.mean(yg * yg, axis=-1, keepdims=True) + EPS)
            mix_ref[:, D_ATTN + GW * g:D_ATTN + GW * (g + 1)] = (yg * rs * ng_ref[layer:layer + 1, GW * g:GW * (g + 1)]).astype(bf16)

    small = lambda shape: pl.BlockSpec(shape, lambda n: (0,) * len(shape))
    return pl.pallas_call(
        body, grid=(NBLK,),
        in_specs=[pl.BlockSpec((BLK, D_CONV), lambda n: (n, 0)), pl.BlockSpec((BLK, D_SSM), lambda n: (n, 0)),
                  pl.BlockSpec((BLK, 128), lambda n: (n, 0)), pl.BlockSpec((BLK, D_ATTN), lambda n: (n, 0)),
                  small((DEPTH, NSSM)), small((DEPTH, NSSM)), small((DEPTH, NSSM)), small((DEPTH, D_SSM))],
        out_specs=[pl.BlockSpec((BLK, D), lambda n: (n, 0)), pl.BlockSpec((None, NSSM * HD, NSTATE), lambda n: (n, 0, 0)),
                   pl.BlockSpec((BLK, D_SSM), lambda n: (n, 0))],
        out_shape=_out_hbm([SDS((S, D), bf16), SDS((NBLK, NSSM * HD, NSTATE), f32), SDS((S, D_SSM), f32)]),
        scratch_shapes=[pltpu.VMEM((NSSM * HD, NSTATE), f32)],
        name="ssd_fwd", compiler_params=_cparams(1),
    )(*_in_hbm([xact, z, dt, attn, dt_bias, a_log, d_skip, norm_g]))


def _ssd_bwd(xact, z, dt, dmix, hs, y, dt_bias, a_log, d_skip, norm_g, dproj, layer):
    def body(xc_ref, z_ref, dt_ref, do_ref, hs_ref, y_ref, dtb_ref, alog_ref, dsk_ref, ng_ref, dproj_in,
             dzdt_ref, dx_ref, dsm_ref, dh_ref, dy_ref):
        i = pl.program_id(0)

        @pl.when(i == 0)
        def _():
            dh_ref[...] = jnp.zeros_like(dh_ref)
            dsm_ref[...] = jnp.zeros_like(dsm_ref)

        c = _ssd_chunk_common(xc_ref, dt_ref, dtb_ref, alog_ref, lambda g: hs_ref[pl.ds(GW * g, GW), :], layer)
        raw, dtv, a = c["raw"], c["dtv"], c["a"]
        grp, heads = range(NGRP), range(NSSM)
        dskip = dsk_ref[layer:layer + 1, :]
        lane8 = lax.broadcasted_iota(jnp.int32, (1, NSSM), 1)
        sub8 = lax.broadcasted_iota(jnp.int32, (NSSM, 1), 0)

        zv = z_ref[...].astype(f32)
        sz = _sigmoid(zv)
        gz = zv * sz
        yv = y_ref[...]
        yz = yv * gz
        for g in grp:
            sl = slice(GW * g, GW * (g + 1))
            yg = yz[:, sl]
            rs = lax.rsqrt(jnp.mean(yg * yg, axis=-1, keepdims=True) + EPS)
            yhat = yg * rs
            dog = do_ref[:, sl]
            w = dog * ng_ref[layer:layer + 1, sl]
            dyz = rs * (w - yhat * jnp.mean(yhat * w, axis=-1, keepdims=True))
            dsm_ref[0:1, sl] += jnp.sum(dog * yhat, axis=0, keepdims=True)
            dy_ref[:, sl] = dyz * gz[:, sl]
            dzdt_ref[:, sl] = (dyz * yv[:, sl] * (sz[:, sl] * (1.0 + zv[:, sl] * (1.0 - sz[:, sl])))).astype(bf16)

        dy = [dy_ref[:, pl.ds(GW * g, GW)] for g in grp]
        dy_b = [dy[g].astype(bf16) for g in grp]
        hl = lambda h: slice(HD * (h % HG), HD * (h % HG + 1))
        dt_off_b = [(dy[g] * c["e"][g]).astype(bf16) for g in grp]
        dcm = [_dot(dt_off_b[g], c["prev_b"][g], NN_DIMS) for g in grp]
        dprev = [_dot(dt_off_b[g], c["cm_b"][g], TN_DIMS) for g in grp]
        yoff_rs = [_head_rowsums(dy[g] * c["y_off"][g], g) for g in grp]
        dhn = [dh_ref[pl.ds(GW * g, GW), :] for g in grp]
        dhn_b = [dhn[g].astype(bf16) for g in grp]
        dprev = [dprev[g] + dhn[g] * _row_expand(c["cd"], g) for g in grp]
        dhn_prev = [dhn[g] * c["prev"][g] for g in grp]
        u = [_dot(c["bm_b"][g], dhn_b[g], NT_DIMS) for g in grp]
        dbm = [_dot(c["xdte_b"][g], dhn_b[g], NN_DIMS) for g in grp]
        ddte_rs = [_head_rowsums(c["xdt"][g] * u[g], g) for g in grp]
        dm = [_dot(dy_b[h // HG][:, hl(h)], c["xdt_b"][h // HG][:, hl(h)], NT_DIMS) for h in heads]
        dxdt_in = [_dot(c["m_b"][h], dy_b[h // HG][:, hl(h)], TN_DIMS) for h in heads]
        dseg = [dm[h] * c["m"][h] for h in heads]
        dmd = [dm[h] * c["decay"][h] for h in heads]
        for h in heads:
            dx_ref[:, pl.ds(HD * h, HD)] = dxdt_in[h]

        tmp = (ddte_rs[0] + ddte_rs[1]) * c["dte"]
        dacs = yoff_rs[0] + yoff_rs[1] - tmp
        dacs_cols = jnp.zeros((NSSM, BLK), f32)
        ddtv = jnp.zeros((BLK, NSSM), f32)
        ddsk = jnp.zeros((BLK, NSSM), f32)
        hp = jnp.zeros((1, NSSM), f32)
        for g in grp:
            cols = pl.ds(GW * g, GW)
            dxdt = dx_ref[:, cols] + u[g] * c["dte_x"][g]
            dx_ref[:, cols] = dy[g] * _lane_expand(dskip, g) + dxdt * c["dt"][g]
            ddtv = ddtv + _head_rowsums(dxdt * c["x"][g], g)
            ddsk = ddsk + _head_rowsums(dy[g] * c["x"][g], g)
            dcb = dmd[HG * g]
            for r in range(1, HG):
                dcb = dcb + dmd[HG * g + r]
            dcb_b = dcb.astype(bf16)
            dx_ref[:, pl.ds(D_SSM + NSTATE * g, NSTATE)] = dbm[g] + _dot(dcb_b, c["cm_b"][g], TN_DIMS)
            dx_ref[:, pl.ds(D_SSM + NGRP * NSTATE + NSTATE * g, NSTATE)] = dcm[g] + _dot(dcb_b, c["bm_b"][g], NN_DIMS)
            dh_ref[cols, :] = dprev[g]
            hp = hp + _head_blocksums(jnp.sum(dhn_prev[g], axis=1, keepdims=True), g)
            for r in range(HG):
                h = HG * g + r
                dacs = dacs + (lane8 == h).astype(f32) * jnp.sum(dseg[h], axis=1, keepdims=True)
                dacs_cols = dacs_cols + (sub8 == h).astype(f32) * jnp.sum(dseg[h], axis=0, keepdims=True)
        dlast = hp * c["cd"] + jnp.sum(tmp, axis=0, keepdims=True)
        ddsk = jnp.sum(ddsk, axis=0, keepdims=True)

        row = lax.broadcasted_iota(jnp.int32, (BLK, 1), 0)
        dacs = dacs - dacs_cols.T + jnp.where(row == BLK - 1, dlast, 0.0)
        dda = lax.dot_general(_tri().astype(f32), dacs, TN_DIMS, preferred_element_type=f32, precision=HIGHEST)
        ddtv = ddtv + dda * a
        da = jnp.sum(dda * dtv, axis=0, keepdims=True)
        draw = ddtv * _sigmoid(raw)
        dzdt_ref[:, D_SSM:] = jnp.zeros((BLK, COL_XBC - COL_DT), bf16)
        dzdt_ref[:, D_SSM:D_SSM + NSSM] = draw.astype(bf16)
        dsm_ref[1:2, 0:NSSM] += jnp.sum(draw, axis=0, keepdims=True)
        dsm_ref[2:3, 0:NSSM] += da * a
        dsm_ref[3:4, 0:NSSM] += ddsk

    rev = lambda i: NBLK - 1 - i
    small = lambda shape: pl.BlockSpec(shape, lambda i: (0,) * len(shape))
    return pl.pallas_call(
        body, grid=(NBLK,),
        in_specs=[pl.BlockSpec((BLK, D_CONV), lambda i: (rev(i), 0)), pl.BlockSpec((BLK, D_SSM), lambda i: (rev(i), 0)),
                  pl.BlockSpec((BLK, 128), lambda i: (rev(i), 0)), pl.BlockSpec((BLK, D_SSM), lambda i: (rev(i), 1)),
                  pl.BlockSpec((None, NSSM * HD, NSTATE), lambda i: (rev(i), 0, 0)), pl.BlockSpec((BLK, D_SSM), lambda i: (rev(i), 0)),
                  small((DEPTH, NSSM)), small((DEPTH, NSSM)), small((DEPTH, NSSM)), small((DEPTH, D_SSM)), ANY_SPEC],
        out_specs=[pl.BlockSpec((BLK, COL_XBC - COL_Z), lambda i: (rev(i), COL_Z // (COL_XBC - COL_Z))),
                   pl.BlockSpec((BLK, D_CONV), lambda i: (rev(i), 0)), small((8, D_SSM))],
        out_shape=_out_hbm([SDS((S, D_IN_PAD), bf16), SDS((S, D_CONV), f32), SDS((8, D_SSM), f32)]),
        scratch_shapes=[pltpu.VMEM((NSSM * HD, NSTATE), f32), pltpu.VMEM((BLK, D_SSM), f32)],
        name="ssd_bwd", input_output_aliases={10: 0}, compiler_params=_cparams(1),
    )(*_in_hbm([xact, z, dt, dmix, hs, y, dt_bias, a_log, d_skip, norm_g, dproj]))


def _my_place():
    return lax.axis_index("x"), lax.axis_index("y"), lax.axis_index("c")


def _dev_index(px, py, pc):
    return 4 * px + 2 * py + pc


def _slab2(kind, ref, idx):
    if kind == "stack":
        return ref.at[idx]
    if kind == "rows128":
        return ref.at[pl.ds(pl.multiple_of(idx * 128, 128), 128), :]
    if kind == "rows512":
        return ref.at[pl.ds(pl.multiple_of(idx * 512, 512), 512), :]
    return ref.at[:, pl.ds(pl.multiple_of(idx * 512, 512), 512)]


def _slab_shape(kind, full_shape):
    if kind == "stack":
        return tuple(full_shape[1:])
    if kind == "rows128":
        return (128, full_shape[1])
    if kind == "rows512":
        return (512, full_shape[1])
    return (full_shape[0], 512)


KIND = dict(w_in="stack", w_out="rows128", w_up="cols512", w_down="rows512", conv_w="stack")
FULL_SHAPE = dict(w_in=(N_DEV, D, D_IN // N_DEV), w_out=(D, D), w_up=(D, D_FF), w_down=(D_FF, D))
HBM_SPEC = pl.BlockSpec(memory_space=pltpu.HBM)
SEM_SPEC = pl.BlockSpec(memory_space=pltpu.SEMAPHORE)
SIDE_EFFECT = pltpu.SideEffectType.DATAFLOW_SIDE_EFFECTING


def _peers_all():
    x, y, c = _my_place()
    return [(x ^ ((r >> 2) & 1), y ^ ((r >> 1) & 1), c ^ (r & 1)) for r in range(1, N_DEV)]


def _split_start(name, bufs, n_copies, plan, deps=()):
    nb = len(bufs)

    def body(*refs):
        ins = refs[:nb]
        send_sems, recv_sems = refs[nb + len(deps)], refs[nb + len(deps) + 1]
        token = refs[-1]
        for i, (src, dst, dev) in enumerate(plan(ins)):
            pltpu.make_async_remote_copy(src_ref=src, dst_ref=dst, send_sem=send_sems.at[i], recv_sem=recv_sems.at[i],
                                         device_id=dev, device_id_type=MESH).start()
        token[...] = jnp.zeros_like(token)

    outs = pl.pallas_call(
        body, name=name,
        out_shape=(pltpu.SemaphoreType.DMA((n_copies,)), pltpu.SemaphoreType.DMA((n_copies,)),
                   *[pltpu.HBM(b.shape, b.dtype) for b in bufs], SDS((8, 128), f32)),
        in_specs=[HBM_SPEC] * nb + [ANY_SPEC] * len(deps),
        out_specs=(SEM_SPEC, SEM_SPEC, *[HBM_SPEC] * nb, pl.BlockSpec(memory_space=pltpu.VMEM)),
        input_output_aliases={i: 2 + i for i in range(nb)},
        compiler_params=pltpu.CompilerParams(has_side_effects=SIDE_EFFECT),
    )(*[pltpu.with_memory_space_constraint(b, pltpu.HBM) for b in bufs], *deps)
    return dict(send=outs[0], recv=outs[1], bufs=list(outs[2:2 + nb]), token=outs[-1], plan=plan, n=n_copies)


def _split_wait(name, started, after):
    bufs = started["bufs"]
    nb = len(bufs)
    plan = started["plan"]

    def body(*refs):
        ins = refs[:nb]
        send_sems, recv_sems = refs[nb], refs[nb + 1]
        for i, (src, dst, dev) in enumerate(plan(ins)):
            cp = pltpu.make_async_remote_copy(src_ref=src, dst_ref=dst, send_sem=send_sems.at[i], recv_sem=recv_sems.at[i],
                                              device_id=dev, device_id_type=MESH)
            cp.wait_send()
            cp.wait_recv()

    outs = pl.pallas_call(
        body, name=name, out_shape=tuple(pltpu.HBM(b.shape, b.dtype) for b in bufs),
        in_specs=[HBM_SPEC] * nb + [SEM_SPEC, SEM_SPEC] + [ANY_SPEC] * len(after), out_specs=(HBM_SPEC,) * nb,
        input_output_aliases={i: i for i in range(nb)},
        compiler_params=pltpu.CompilerParams(has_side_effects=SIDE_EFFECT),
    )(*bufs, started["send"], started["recv"], *after)
    return list(outs)


def _gather_start(name, names, fulls, deps):
    n_t = len(names)

    def plan(refs):
        x, y, c = _my_place()
        my_idx = _dev_index(x, y, c)
        targets = [(x, y, 1 - c), (1 - x, y, c), (x, 1 - y, c), (1 - x, 1 - y, c)]
        slabs = [_slab2(KIND[names[t]], refs[t], my_idx) for t in range(n_t)]
        return [(slabs[t], slabs[t], dev) for t in range(n_t) for dev in targets]

    return _split_start(name, list(fulls), 4 * n_t, plan, deps)


def _gather_finish(name, names, started, after):
    n_t = len(names)
    fulls = _split_wait(name + "_wait", started, after)
    slab_shapes = [SDS(_slab_shape(KIND[n], f.shape), f.dtype) for n, f in zip(names, fulls)]

    def body(*refs):
        ins = refs[:n_t]
        outs = refs[n_t:2 * n_t]
        stage = refs[2 * n_t:3 * n_t]
        load_sems, send_sems, recv_sems = refs[3 * n_t:]
        x, y, c = _my_place()
        chips = [(1 - x, y), (x, 1 - y), (1 - x, 1 - y)]
        pairs = [(t, j) for t in range(n_t) for j in range(3)]
        loads = [pltpu.make_async_copy(_slab2(KIND[names[t]], ins[t], _dev_index(*chips[j], c)), stage[t].at[j], load_sems.at[t, j])
                 for t, j in pairs]
        for cp in loads:
            cp.start()

        def copy(t, j, core):
            return pltpu.make_async_remote_copy(
                src_ref=stage[t].at[j], dst_ref=_slab2(KIND[names[t]], outs[t], _dev_index(*chips[j], core)),
                send_sem=send_sems.at[t, j], recv_sem=recv_sems.at[t, j], device_id=(x, y, 1 - c), device_id_type=MESH)

        sends = [copy(t, j, c) for t, j in pairs]
        for ld, cp in zip(loads, sends):
            ld.wait()
            cp.start()
        for t, j in pairs:
            copy(t, j, 1 - c).wait_recv()
        for cp in sends:
            cp.wait_send()

    return pl.pallas_call(
        body, in_specs=[ANY_SPEC] * n_t, out_specs=[ANY_SPEC] * n_t, out_shape=[SDS(b.shape, b.dtype) for b in fulls],
        input_output_aliases={t: t for t in range(n_t)},
        scratch_shapes=[pltpu.VMEM((3,) + s.shape, s.dtype) for s in slab_shapes]
        + [pltpu.SemaphoreType.DMA((n_t, 3)), pltpu.SemaphoreType.DMA((n_t, 3)), pltpu.SemaphoreType.DMA((n_t, 3))],
        name=name + "_pass", compiler_params=pltpu.CompilerParams(vmem_limit_bytes=VMEM_LIMIT),
    )(*fulls)


def _exchange_start(name, names, grads, deps):
    n_t = len(names)
    lands = [lax.empty((N_DEV,) + _slab_shape(KIND[n], g.shape), g.dtype) for n, g in zip(names, grads)]

    def plan(refs):
        my_idx = _dev_index(*_my_place())
        return [(_slab2(KIND[names[t]], refs[t], _dev_index(*peer)), refs[n_t + t].at[my_idx], peer)
                for t in range(n_t) for peer in _peers_all()]

    return _split_start(name, list(grads) + lands, 7 * n_t, plan, deps)


def _small_exchange_start(part, deps):
    land = lax.empty((N_DEV,) + part.shape, part.dtype)

    def plan(refs):
        my_idx = _dev_index(*_my_place())
        return [(refs[0], refs[1].at[my_idx], peer) for peer in _peers_all()]

    return _split_start("small_exchange", [part, land], N_DEV - 1, plan, deps)


def _slab_pieces():
    sh = D_IN // N_DEV
    out = []
    for j in range(N_DEV):
        for first, end, dst in IN_SEGMENTS:
            lo, hi = max(first, sh * j), min(end, sh * (j + 1))
            if lo < hi:
                out.append((j, lo - sh * j, hi - sh * j, dst + lo - first))
    return out


def _w_in_assemble(stacked):
    tr = 256
    sh = D_IN // N_DEV

    def body(i_ref, o_ref):
        o_ref[:, COL_DT:COL_XBC] = jnp.zeros((tr, COL_XBC - COL_DT), bf16)
        for j, lo, hi, dst in _slab_pieces():
            o_ref[:, dst:dst + hi - lo] = i_ref[j, :, lo:hi]

    return pl.pallas_call(
        body, grid=(D // tr,), in_specs=[pl.BlockSpec((N_DEV, tr, sh), lambda i: (0, i, 0))],
        out_specs=pl.BlockSpec((None, tr, D_IN_PAD), lambda i: (0, i, 0)), out_shape=_out_hbm(SDS((1, D, D_IN_PAD), bf16)),
        name="w_in_assemble", compiler_params=_cparams(1),
    )(*_in_hbm([stacked]))


def _w_in_slabs(dw_in):
    tr = 256
    sh = D_IN // N_DEV

    def body(i_ref, o_ref):
        for j, lo, hi, src in _slab_pieces():
            o_ref[j, :, lo:hi] = i_ref[:, src:src + hi - lo]

    return pl.pallas_call(
        body, grid=(D // tr,), in_specs=[pl.BlockSpec((tr, D_IN_PAD), lambda i: (i, 0))],
        out_specs=pl.BlockSpec((N_DEV, tr, sh), lambda i: (0, i, 0)), out_shape=_out_hbm(SDS((N_DEV, D, sh), bf16)),
        name="w_in_slabs", compiler_params=_cparams(1),
    )(*_in_hbm([dw_in]))


SMALL_NAMES = ("mix_norm_g", "mlp_norm_g", "conv_b", "ssm_norm_g", "q_gain", "k_gain", "sinks", "dt_bias", "a_log", "d_skip",
               "rel_bias", "conv_w")
MISC_LANES = dict(q_gain=(LANE_QG, HD), k_gain=(LANE_KG, HD), sinks=(LANE_SINK, NQ), dt_bias=(LANE_DTB, NSSM),
                  a_log=(LANE_ALOG, NSSM), d_skip=(LANE_DSKIP, NSSM))


def _pack_small_grads(smalls, drel_t, loss):
    def body(*refs):
        o_ref = refs[-1]
        drel_ref, loss_ref = refs[-3], refs[-2]
        o_ref[...] = jnp.zeros_like(o_ref)
        for l in range(DEPTH):
            mixg, mlpg, convb, convw, ssd, attn = refs[6 * l:6 * l + 6]
            o_ref[ROW_MIXG + l:ROW_MIXG + l + 1, :] = mixg[...]
            o_ref[ROW_MLPG + l:ROW_MLPG + l + 1, :] = mlpg[...]
            o_ref[ROW_CONVB + l:ROW_CONVB + l + 1, :] = convb[0:1, :]
            o_ref[ROW_SSMG + l:ROW_SSMG + l + 1, 0:D_SSM] = ssd[0:1, :]
            o_ref[ROW_CONVW + 4 * l:ROW_CONVW + 4 * l + 4, :] = convw[0:4, :]
            row = slice(ROW_MISC + l, ROW_MISC + l + 1)
            o_ref[row, LANE_QG:LANE_QG + HD] = attn[0:1, 0:HD]
            o_ref[row, LANE_KG:LANE_KG + HD] = attn[1:2, 0:HD]
            o_ref[row, LANE_SINK:LANE_SINK + NQ] = attn[2:3, 0:NQ]
            o_ref[row, LANE_DTB:LANE_DTB + NSSM] = ssd[1:2, 0:NSSM]
            o_ref[row, LANE_ALOG:LANE_ALOG + NSSM] = ssd[2:3, 0:NSSM]
            o_ref[row, LANE_DSKIP:LANE_DSKIP + NSSM] = ssd[3:4, 0:NSSM]
        o_ref[ROW_RELB:ROW_RELB + NQ, 0:N_BUCKETS] = drel_ref[...]
        o_ref[ROW_LOSS:ROW_LOSS + 1, 0:1] = loss_ref[0:1, 0:1]

    args = []
    for sm in smalls:
        args += [sm["mix_norm_g"], sm["mlp_norm_g"], sm["conv_b"], sm["conv_w"], sm["ssd"], sm["attn"]]
    args += [drel_t, loss]
    return pl.pallas_call(body, out_shape=SDS((SMALL_ROWS, D), f32), name="pack_small_grads")(*args)


def _adamw_small(part, land, w, m, v):
    n = len(SMALL_NAMES)

    def grad_of(name, g_ref):
        if name == "mix_norm_g":
            return g_ref[ROW_MIXG:ROW_MIXG + DEPTH, :]
        if name == "mlp_norm_g":
            return g_ref[ROW_MLPG:ROW_MLPG + DEPTH, :]
        if name == "conv_b":
            return g_ref[ROW_CONVB:ROW_CONVB + DEPTH, :]
        if name == "ssm_norm_g":
            return g_ref[ROW_SSMG:ROW_SSMG + DEPTH, 0:D_SSM]
        if name == "rel_bias":
            return g_ref[ROW_RELB:ROW_RELB + NQ, 0:N_BUCKETS].T
        lane, width = MISC_LANES[name]
        return g_ref[ROW_MISC:ROW_MISC + DEPTH, lane:lane + width]

    def body(part_ref, land_ref, *refs):
        ws, ms, vs = refs[:n], refs[n:2 * n], refs[2 * n:3 * n]
        loss_ref = refs[3 * n]
        outs = refs[3 * n + 1:-1]
        g_ref = refs[-1]
        me = _dev_index(*_my_place())
        for p in range(N_DEV):
            term = jnp.where(me == p, part_ref[...], land_ref[p])
            if p == 0:
                g_ref[...] = term
            else:
                g_ref[...] += term
        loss_ref[...] = g_ref[ROW_LOSS:ROW_LOSS + 1, 0:128]
        my_cols = pl.ds(pl.multiple_of(me * 128, 128), 128)
        for k, name in enumerate(SMALL_NAMES):
            g_out, d_out, m_out, v_out = outs[4 * k:4 * k + 4]
            if name == "conv_w":
                for l in range(DEPTH):
                    g = g_ref[ROW_CONVW + 4 * l:ROW_CONVW + 4 * l + 4, my_cols]
                    delta, m_new, v_new = _adamw_math(ws[k][l], ms[k][l], vs[k][l], g)
                    g_out[l], d_out[l], m_out[l], v_out[l] = g, delta, m_new, v_new
            else:
                g = grad_of(name, g_ref)
                delta, m_new, v_new = _adamw_math(ws[k][...], ms[k][...], vs[k][...], g)
                g_out[...], d_out[...], m_out[...], v_out[...] = g, delta, m_new, v_new

    ws = [w[name] for name in SMALL_NAMES]
    out_shape = [SDS((1, 128), f32)]
    for a in ws:
        out_shape += [SDS(a.shape, f32)] * 4
    return pl.pallas_call(body, out_shape=out_shape, name="adamw_small", scratch_shapes=[pltpu.VMEM((SMALL_ROWS, D), f32)])(
        part, land, *ws, *[m[name] for name in SMALL_NAMES], *[v[name] for name in SMALL_NAMES])


def _plain(tm, tn):
    return pl.BlockSpec((tm, tn), lambda i, j, k: (i, j))


def _rowblk(tm, width):
    return pl.BlockSpec((tm, width), lambda i, j, k: (i, 0))


def _store_epi(dtype):
    def epi(acc, i, j, ex, outs):
        outs[0][...] = acc.astype(dtype)
    return epi


def _rms_prologue(layer):
    def pro(a_ref, ex, outs):
        xv = a_ref[...]
        r = lax.rsqrt(jnp.mean(xv * xv, axis=-1, keepdims=True) + EPS)
        h = (xv * r * ex[0][layer:layer + 1, :]).astype(bf16)
        outs[-1][...] = h
        return h
    return pro


MLP_TM = 256
MLP_VMEM = 56 * 1024 * 1024


def _resident(shape):
    return pl.BlockSpec((None,) + shape, lambda i: (0, 0, 0), pipeline_mode=pl.Buffered(1))


def _mlp_fwd(layer, x, mix, g, w_out, w_up, w_down, tgt=None):
    tm = MLP_TM
    with_loss = tgt is not None

    def body(x_ref, mix_ref, g_ref, wo_ref, wu_ref, wd_ref, *rest):
        xm_ref, a_ref, r_ref, h_ref = rest[with_loss:with_loss + 4]
        rest = rest[:with_loss] + rest[with_loss + 1:]
        i = pl.program_id(0)
        xv = x_ref[...] + _dot(mix_ref[...], wo_ref[...], NN_DIMS)
        xm_ref[...] = xv
        h = (xv * lax.rsqrt(jnp.mean(xv * xv, axis=-1, keepdims=True) + EPS) * g_ref[layer:layer + 1, :]).astype(bf16)
        h_ref[...] = h
        r = jnp.maximum(_dot(h, wu_ref[...], NN_DIMS), 0.0)
        a = (r * r).astype(bf16)
        a_ref[...] = a
        r_ref[...] = r.astype(bf16)
        y = xv + _dot(a, wd_ref[...], NN_DIMS)
        if not with_loss:
            rest[3][...] = y
            return
        err = y - rest[0][...]
        rest[4][...] = err * (1.0 / D)
        part = 0.5 * jnp.sum(jnp.mean(err * err, axis=-1, keepdims=True), axis=0, keepdims=True)

        @pl.when(i == 0)
        def _():
            rest[5][...] = jnp.zeros_like(rest[5])

        rest[5][...] += jnp.broadcast_to(part, rest[5].shape)

    row = lambda width: pl.BlockSpec((tm, width), lambda i: (i, 0))
    in_specs = [row(D), row(D), pl.BlockSpec((DEPTH, D), lambda i: (0, 0)), _resident((D, D)), _resident((D, D_FF)),
                _resident((D_FF, D))]
    out_specs = [row(D), row(D_FF), row(D_FF), row(D), row(D)]
    out_shape = [SDS((S, D), f32), SDS((S, D_FF), bf16), SDS((S, D_FF), bf16), SDS((S, D), bf16), SDS((S, D), f32)]
    args = [x, mix, g, w_out, w_up, w_down]
    if with_loss:
        in_specs.append(row(D))
        args.append(tgt)
        out_specs.append(pl.BlockSpec((1, 128), lambda i: (0, 0)))
        out_shape.append(SDS((1, 128), f32))
    return pl.pallas_call(
        body, grid=(S // tm,), in_specs=in_specs, out_specs=out_specs, out_shape=_out_hbm(out_shape),
        name="mlp_fwd_loss" if with_loss else "mlp_fwd",
        compiler_params=pltpu.CompilerParams(dimension_semantics=("arbitrary",), vmem_limit_bytes=MLP_VMEM),
    )(*_in_hbm(args[:3]), *args[3:6], *_in_hbm(args[6:]))


def _mlp_bwd_act(layer, dx_out, r_act, x_mid, g, w_down, w_up, w_out, deps):
    tm = MLP_TM

    def body(dxo_ref, r_ref, xm_ref, g_ref, wd_ref, wu_ref, wo_ref, *rest):
        du_ref, dx_ref, dg_ref, dmix_ref = rest[len(deps):]
        dxo = dxo_ref[...]
        du = (_dot(dxo.astype(bf16), wd_ref[...], NT_DIMS) * (2.0 * r_ref[...].astype(f32))).astype(bf16)
        du_ref[...] = du
        dh = _dot(du, wu_ref[...], NT_DIMS)
        _rms_bwd_epilogue(layer)(dh, pl.program_id(0), 0, (xm_ref, g_ref, dxo_ref), (dx_ref, dg_ref))
        dmix_ref[...] = _dot(dx_ref[...].astype(bf16), wo_ref[...], NT_DIMS)

    row = lambda width: pl.BlockSpec((tm, width), lambda i: (i, 0))
    return pl.pallas_call(
        body, grid=(S // tm,),
        in_specs=[row(D), row(D_FF), row(D), pl.BlockSpec((DEPTH, D), lambda i: (0, 0)), _resident((D_FF, D)), _resident((D, D_FF)),
                  _resident((D, D))] + [ANY_SPEC] * len(deps),
        out_specs=[row(D_FF), row(D), pl.BlockSpec((1, D), lambda i: (0, 0)), row(D)],
        out_shape=_out_hbm([SDS((S, D_FF), bf16), SDS((S, D), f32), SDS((1, D), f32), SDS((S, D), f32)]), name="mlp_bwd_act",
        compiler_params=pltpu.CompilerParams(dimension_semantics=("arbitrary",), vmem_limit_bytes=MLP_VMEM),
    )(*_in_hbm([dx_out, r_act, x_mid, g]), w_down, w_up, w_out, *_in_hbm(deps))


def _layer_fwd(l, x, p, get_weights, bias, tgt=None):
    wts = get_weights(l, "in", [x, bias])
    gfull = pl.BlockSpec((DEPTH, D), lambda i, j, k: (0, 0))
    tm = 512

    def inproj_epi(acc, i, j, ex, outs):
        outs[0][...] = acc[:, COL_QKV:COL_Z].astype(bf16)
        outs[1][...] = acc[:, COL_Z:COL_DT].astype(bf16)
        outs[2][...] = acc[:, COL_XBC:D_IN_PAD].astype(bf16)
        outs[3][...] = acc[:, COL_DT:COL_DT + 128]

    qkv, z, xbc, dt, h1 = _matmul(
        "in_proj", "nn", x, wts["w_in"], tm=tm, tn=D_IN_PAD, tk=D, prologue=_rms_prologue(l),
        extras=(p["mix_norm_g"],), extra_specs=(gfull,),
        out_shape=[SDS((S, 768), bf16), SDS((S, 512), bf16), SDS((S, 1024), bf16), SDS((S, 128), f32), SDS((S, D), bf16)],
        out_specs=[_rowblk(tm, 768), _rowblk(tm, 512), _rowblk(tm, 1024), _rowblk(tm, 128), _rowblk(tm, D)], epilogue=inproj_epi)
    attn = _attn_fwd(qkv, p["q_gain"], p["k_gain"], p["sinks"], bias, l)
    xact = _conv_fwd(xbc, wts["conv_w"], p["conv_b"], l)
    mix, hs, y_ssd = _ssd_fwd(xact, z, dt, attn, p["dt_bias"], p["a_log"], p["d_skip"], p["ssm_norm_g"], l)
    wts = dict(wts, **get_weights(l, "rest", [mix]))

    x_mid, a_act, r_act, h2, *result = _mlp_fwd(l, x, mix, p["mlp_norm_g"], wts["w_out"], wts["w_up"], wts["w_down"], tgt)
    saved = dict(x=x, h1=h1, qkv=qkv, z=z, xbc=xbc, dt=dt, xact=xact, mix=mix, hs=hs, y_ssd=y_ssd, x_mid=x_mid, h2=h2,
                 a=a_act, r=r_act, wts=wts)
    return (result[0] if tgt is None else tuple(result)), saved


def _layer_bwd(l, dx_out, sv, p, bias, deps, send):
    wts = sv["wts"]

    dw_down = _matmul("dw_down", "tn", sv["a"], dx_out, tm=1024, tn=D, tk=S, out_shape=SDS((D_FF, D), bf16),
                      out_specs=_plain(1024, D), epilogue=_store_epi(bf16), deps=deps)
    deps = send(l, dict(w_down=dw_down))
    du, dx_mid, dg_mlp, dmix = _mlp_bwd_act(l, dx_out, sv["r"], sv["x_mid"], p["mlp_norm_g"], wts["w_down"], wts["w_up"],
                                            wts["w_out"], deps)
    dw_up = _matmul("dw_up", "tn", sv["h2"], du, tm=D, tn=1024, tk=S, out_shape=SDS((D, D_FF), bf16),
                    out_specs=_plain(D, 1024), epilogue=_store_epi(bf16))
    dw_out = _matmul("dw_out", "tn", sv["mix"], dx_mid, tm=D, tn=512, tk=S, out_shape=SDS((D, D), bf16),
                     out_specs=_plain(D, 512), epilogue=_store_epi(bf16))
    deps = send(l, dict(w_up=dw_up, w_out=dw_out))
    gfull = pl.BlockSpec((DEPTH, D), lambda i, j, k: (0, 0))
    grow = pl.BlockSpec((1, D), lambda i, j, k: (0, 0))
    dproj, dbias, dsm_attn = _attn_bwd(sv["qkv"], dmix, p["q_gain"], p["k_gain"], p["sinks"], bias, l, deps)
    dproj, dxact, dsm_ssd = _ssd_bwd(sv["xact"], sv["z"], sv["dt"], dmix, sv["hs"], sv["y_ssd"], p["dt_bias"], p["a_log"],
                                     p["d_skip"], p["ssm_norm_g"], dproj, l)
    dproj, dconv_w, dconv_b = _conv_bwd(sv["xbc"], dxact, wts["conv_w"], p["conv_b"], dproj, l)
    dw_in = _matmul("dw_in", "tn", sv["h1"], dproj, tm=D, tn=1280, tk=S, out_shape=SDS((D, D_IN_PAD), bf16),
                    out_specs=_plain(D, 1280), epilogue=_store_epi(bf16))
    deps = send(l, dict(w_in=_w_in_slabs(dw_in)))
    dx, dg_mix = _matmul(
        "in_proj_dh", "nt", dproj, wts["w_in"], tm=512, tn=D, tk=D_IN_PAD, out_shape=[SDS((S, D), f32), SDS((1, D), f32)],
        out_specs=[_plain(512, D), grow], epilogue=_rms_bwd_epilogue(l),
        extras=(sv["x"], p["mix_norm_g"], dx_mid), extra_specs=(_plain(512, D), gfull, _plain(512, D)), deps=deps)
    small = dict(mix_norm_g=dg_mix, mlp_norm_g=dg_mlp, conv_w=dconv_w, conv_b=dconv_b, ssd=dsm_ssd, attn=dsm_attn, dbias=dbias)
    return dx, small, deps


def _local_step(x, tgt, p, get_weights, send):
    onehot_t = jnp.asarray(_onehot_buckets(), dtype=bf16)
    bias = _bias_build(p["rel_bias"].T, onehot_t).reshape(NQ, BLK, 2 * BLK)
    saved = []
    h = x
    for l in range(DEPTH):
        h, sv = _layer_fwd(l, h, p, get_weights, bias, tgt if l == DEPTH - 1 else None)
        saved.append(sv)
    dx, loss = h
    smalls = [None] * DEPTH
    deps = ()
    for l in reversed(range(DEPTH)):
        dx, smalls[l], deps = _layer_bwd(l, dx, saved[l], p, bias, deps, send)
    drel_t = _bias_grad(smalls[0]["dbias"].reshape(NQ, -1), smalls[1]["dbias"].reshape(NQ, -1), onehot_t)
    return dx, _pack_small_grads(smalls, drel_t, loss)


WEIGHT_ORDER = ("mix_norm_g", "w_in", "q_gain", "k_gain", "sinks", "rel_bias", "conv_w", "conv_b", "dt_bias", "a_log", "d_skip",
                "ssm_norm_g", "w_out", "mlp_norm_g", "w_up", "w_down")


def kernel(x, mix_norm_g, w_in, q_gain, k_gain, sinks, rel_bias, conv_w, conv_b, dt_bias, a_log, d_skip, ssm_norm_g, w_out, mlp_norm_g, w_up, w_down, loss_target, m_mix_norm_g, m_w_in, m_q_gain, m_k_gain, m_sinks, m_rel_bias, m_conv_w, m_conv_b, m_dt_bias, m_a_log, m_d_skip, m_ssm_norm_g, m_w_out, m_mlp_norm_g, m_w_up, m_w_down, v_mix_norm_g, v_w_in, v_q_gain, v_k_gain, v_sinks, v_rel_bias, v_conv_w, v_conv_b, v_dt_bias, v_a_log, v_d_skip, v_ssm_norm_g, v_w_out, v_mlp_norm_g, v_w_up, v_w_down):
    w = dict(mix_norm_g=mix_norm_g, w_in=w_in, q_gain=q_gain, k_gain=k_gain, sinks=sinks, rel_bias=rel_bias, conv_w=conv_w,
             conv_b=conv_b, dt_bias=dt_bias, a_log=a_log, d_skip=d_skip, ssm_norm_g=ssm_norm_g, w_out=w_out,
             mlp_norm_g=mlp_norm_g, w_up=w_up, w_down=w_down)
    m = dict(mix_norm_g=m_mix_norm_g, w_in=m_w_in, q_gain=m_q_gain, k_gain=m_k_gain, sinks=m_sinks, rel_bias=m_rel_bias,
             conv_w=m_conv_w, conv_b=m_conv_b, dt_bias=m_dt_bias, a_log=m_a_log, d_skip=m_d_skip, ssm_norm_g=m_ssm_norm_g,
             w_out=m_w_out, mlp_norm_g=m_mlp_norm_g, w_up=m_w_up, w_down=m_w_down)
    v = dict(mix_norm_g=v_mix_norm_g, w_in=v_w_in, q_gain=v_q_gain, k_gain=v_k_gain, sinks=v_sinks, rel_bias=v_rel_bias,
             conv_w=v_conv_w, conv_b=v_conv_b, dt_bias=v_dt_bias, a_log=v_a_log, d_skip=v_d_skip, ssm_norm_g=v_ssm_norm_g,
             w_out=v_w_out, mlp_norm_g=v_mlp_norm_g, w_up=v_w_up, w_down=v_w_down)
    big = ("w_in", "w_out", "w_up", "w_down")

    my_idx = _dev_index(*_my_place()).astype(jnp.int32).reshape(1)

    fulls = {n: _cast_to_full("cast_" + n, w[n], KIND[n], FULL_SHAPE[n], my_idx, bf16) for n in big}
    conv_full = _cast_to_full("cast_conv_w", conv_w.reshape(1, DEPTH * 4, 128), "stack", (N_DEV, DEPTH * 4, 128), my_idx, f32)[0]
    rest = ["w_out", "w_up", "w_down"]
    g0 = _gather_start("gather0", ["w_in", "conv_w"], [fulls["w_in"][0], conv_full], ())
    g1 = _gather_start("gather1", rest, [fulls[n][0] for n in rest], (g0["token"],))
    g2 = _gather_start("gather2", ["w_in"], [fulls["w_in"][1]], (g1["token"],))
    g3 = _gather_start("gather3", rest, [fulls[n][1] for n in rest], (g2["token"],))
    held = {}
    flat = lambda a: a.reshape(a.shape[0] * a.shape[1], a.shape[2])
    adam_in = {n: (flat(w[n]), flat(m[n]), flat(v[n])) for n in big}

    def get_weights(l, part, after):
        if l == 0 and part == "in":
            full_in, full_conv = _gather_finish("gather0", ["w_in", "conv_w"], g0,
                                                list(after) + [g3["token"], adam_in["w_in"][1], adam_in["w_in"][2]])
            held["conv_w"] = jnp.transpose(full_conv.reshape(N_DEV, DEPTH, 4, 128), (1, 2, 0, 3)).reshape(DEPTH, 4, D_CONV)
            return dict(w_in=_w_in_assemble(full_in), conv_w=held["conv_w"])
        if part == "in":
            return dict(w_in=_w_in_assemble(_gather_finish("gather2", ["w_in"], g2, after)[0]), conv_w=held["conv_w"])
        full = _gather_finish("gather1" if l == 0 else "gather3", rest, g1 if l == 0 else g3, after)
        return {n: f[None] for n, f in zip(rest, full)}

    pending = []

    def send(l, grads):
        names = list(grads)
        started = _exchange_start("exchange%d_%s" % (l, names[0]), names, [grads[n] for n in names], ())
        pending.append((l, names, started))
        return (started["token"],)

    dx, small_part = _local_step(x.reshape(S, D), loss_target.reshape(S, D), w, get_weights, send)

    small = _small_exchange_start(small_part, ())
    tiles = dict(w_in=512, w_out=128, w_up=512, w_down=256)
    after = [dx, small["token"]]
    res = {}
    for names in (("w_down",), ("w_up", "w_out"), ("w_in",)):
        land, g_full = {}, {}
        for l, sent, started in pending:
            if tuple(sent) == names:
                bufs = _split_wait("exchange%d_%s_wait" % (l, names[0]), started, after)
                for t, n in enumerate(names):
                    g_full[l, n], land[l, n] = bufs[t], bufs[len(names) + t]
        for n in names:
            outs = _adamw_tensor("adamw_" + n, KIND[n], *adam_in[n], [land[l, n] for l in range(DEPTH)],
                                 [g_full[l, n] for l in range(DEPTH)], my_idx, tiles[n])
            res[n] = [o.reshape(w[n].shape) for o in outs]
            after = [outs[0]]
    small_part, small_land = _split_wait("small_exchange_wait", small, after)
    small_outs = _adamw_small(small_part, small_land, w, m, v)
    loss = small_outs[0][0, 0]
    for k, name in enumerate(SMALL_NAMES):
        res[name] = small_outs[1 + 4 * k:5 + 4 * k]

    result = [loss, dx.reshape(1, S, D)]
    for k in range(4):
        result += [res[name][k] for name in WEIGHT_ORDER]
    return tuple(result)
```

```python
import functools
import math

import numpy as np
import jax
import jax.numpy as jnp
from jax import lax
from jax.experimental import pallas as pl
from jax.experimental.pallas import tpu as pltpu

f32 = jnp.float32
bf16 = jnp.bfloat16
SDS = jax.ShapeDtypeStruct
MESH = pl.DeviceIdType.MESH
HIGHEST = lax.Precision.HIGHEST

S = 2048
D = 1024
DEPTH = 2
BLK = 128
NBLK = S // BLK
HD = 64
NQ = 8
NKV = 2
NSSM = 8
NGRP = 2
NSTATE = 128
D_ATTN = 512
D_SSM = 512
D_CONV = 1024
D_FF = 4096
D_IN = 2312
D_IN_PAD = 2560
COL_QKV, COL_Z, COL_DT, COL_XBC = 0, 768, 1280, 1536
IN_SEGMENTS = ((0, 1280, 0), (1280, 2304, COL_XBC), (2304, 2312, COL_DT))
N_BUCKETS = 32
EPS = 1e-6
N_DEV = 8
VMEM_LIMIT = 48 * 1024 * 1024

ADAM_LR = 0.001
ADAM_B1 = 0.9
ADAM_B2 = 0.999
ADAM_EPS = 1e-08
ADAM_WD = 0.01
ADAM_STEP = 10

NT_DIMS = (((1,), (1,)), ((), ()))
TN_DIMS = (((0,), (0,)), ((), ()))
NN_DIMS = (((1,), (0,)), ((), ()))

ROW_MIXG = 0
ROW_MLPG = 2
ROW_CONVB = 4
ROW_SSMG = 6
ROW_MISC = 8
ROW_RELB = 10
ROW_CONVW = 18
ROW_LOSS = 26
SMALL_ROWS = 32
LANE_QG, LANE_KG, LANE_SINK, LANE_DTB, LANE_ALOG, LANE_DSKIP = 0, 64, 128, 256, 384, 512


def _dot(a, b, dims):
    return lax.dot_general(a, b, dims, preferred_element_type=f32)


def _cparams(n_axes):
    return pltpu.CompilerParams(dimension_semantics=("arbitrary",) * n_axes, vmem_limit_bytes=VMEM_LIMIT)


def _sum11(v):
    return jnp.sum(jnp.sum(v, axis=1, keepdims=True), axis=0, keepdims=True)


def _sigmoid(v):
    return 1.0 / (1.0 + jnp.exp(-v))


ANY_SPEC = pl.BlockSpec(memory_space=pl.ANY)


def _in_hbm(args):
    return [pltpu.with_memory_space_constraint(a, pltpu.HBM) if a.size >= 65536 else a for a in args]


def _out_hbm(out_shape):
    one = lambda s: pltpu.HBM(s.shape, s.dtype) if math.prod(s.shape) >= 65536 else s
    return [one(s) for s in out_shape] if isinstance(out_shape, (list, tuple)) else one(out_shape)


def _matmul(name, mode, a, b, *, layer=0, tm, tn, tk, out_shape, out_specs, epilogue, extras=(), extra_specs=(), deps=(),
            prologue=None):
    extras = tuple(extras) + tuple(deps)
    extra_specs = tuple(extra_specs) + (ANY_SPEC,) * len(deps)
    if mode == "tn":
        t_dim, m_dim = a.shape
        n_dim = b.shape[1]
        grid = (m_dim // tm, n_dim // tn, t_dim // tk)
        a_spec = pl.BlockSpec((tk, tm), lambda i, j, k: (k, i))
        b_spec = pl.BlockSpec((tk, tn), lambda i, j, k: (k, j))
        dims = TN_DIMS
    elif mode == "nn":
        m_dim, k_dim = a.shape
        n_dim = b.shape[-1]
        grid = (m_dim // tm, n_dim // tn, k_dim // tk)
        a_spec = pl.BlockSpec((tm, tk), lambda i, j, k: (i, k))
        b_spec = pl.BlockSpec((None, tk, tn), lambda i, j, k: (layer, k, j))
        dims = NN_DIMS
    else:
        m_dim, k_dim = a.shape
        n_dim = b.shape[-2]
        grid = (m_dim // tm, n_dim // tn, k_dim // tk)
        a_spec = pl.BlockSpec((tm, tk), lambda i, j, k: (i, k))
        b_spec = pl.BlockSpec((None, tn, tk), lambda i, j, k: (layer, j, k))
        dims = NT_DIMS
    nk = grid[2]
    n_ex = len(extras)

    def body(a_ref, b_ref, *rest):
        ex = rest[:n_ex - len(deps)]
        outs = rest[n_ex:-1]
        acc = rest[-1]
        i = pl.program_id(0)
        j = pl.program_id(1)
        k = pl.program_id(2)
        lhs = a_ref[...].astype(bf16) if prologue is None else prologue(a_ref, ex, outs)
        part = _dot(lhs, b_ref[...].astype(bf16), dims)
        if nk == 1:
            epilogue(part, i, j, ex, outs)
        else:
            @pl.when(k == 0)
            def _():
                acc[...] = part

            @pl.when(k > 0)
            def _():
                acc[...] += part

            @pl.when(k == nk - 1)
            def _():
                epilogue(acc[...], i, j, ex, outs)

    return pl.pallas_call(
        body, grid=grid, in_specs=[a_spec, b_spec, *extra_specs], out_specs=out_specs, out_shape=_out_hbm(out_shape),
        scratch_shapes=[pltpu.VMEM((tm, tn) if nk > 1 else (8, 128), f32)], name=name, compiler_params=_cparams(3),
    )(*_in_hbm([a]), b, *_in_hbm(extras))


def _rms_bwd_epilogue(layer):
    def epi(acc, i, j, ex, outs):
        x_ref, g_ref, dres_ref = ex
        dx_ref, dg_ref = outs
        xv = x_ref[...]
        r = lax.rsqrt(jnp.mean(xv * xv, axis=-1, keepdims=True) + EPS)
        xhat = xv * r
        w = acc * g_ref[layer:layer + 1, :]
        dx_ref[...] = dres_ref[...] + r * (w - xhat * jnp.mean(xhat * w, axis=-1, keepdims=True))
        dg = jnp.sum(acc * xhat, axis=0, keepdims=True)

        @pl.when(i == 0)
        def _():
            dg_ref[...] = dg

        @pl.when(i > 0)
        def _():
            dg_ref[...] += dg
    return epi


def _own_slab_spec(kind, tr, cols, nblk):
    if kind == "stack":
        return pl.BlockSpec((None, tr, cols), lambda i, idx: (idx[0], i, 0))
    if kind == "cols512":
        return pl.BlockSpec((tr, cols), lambda i, idx: (i, idx[0]))
    return pl.BlockSpec((tr, cols), lambda i, idx: (idx[0] * nblk + i, 0))


def _cast_to_full(name, w, kind, full_shape, my_idx, dtype):
    n_layers, rows, cols = w.shape
    tr = min(rows, 256)
    nblk = rows // tr

    def body(idx_ref, w_ref, *o_refs):
        for l in range(n_layers):
            o_refs[l][...] = w_ref[l].astype(dtype)

    grid_spec = pltpu.PrefetchScalarGridSpec(
        num_scalar_prefetch=1, grid=(nblk,), in_specs=[pl.BlockSpec((n_layers, tr, cols), lambda i, idx: (0, i, 0))],
        out_specs=[_own_slab_spec(kind, tr, cols, nblk)] * n_layers)
    return pl.pallas_call(body, grid_spec=grid_spec, out_shape=_out_hbm([SDS(full_shape, dtype)] * n_layers), name=name,
                          compiler_params=_cparams(1))(*_in_hbm([my_idx, w]))


def _adamw_math(w, m, v, g):
    m_new = ADAM_B1 * m + (1.0 - ADAM_B1) * g
    v_new = ADAM_B2 * v + (1.0 - ADAM_B2) * (g * g)
    m_hat = m_new / (1.0 - ADAM_B1 ** ADAM_STEP)
    v_hat = v_new / (1.0 - ADAM_B2 ** ADAM_STEP)
    delta = -ADAM_LR * (m_hat / (jnp.sqrt(v_hat) + ADAM_EPS) + ADAM_WD * w)
    return delta, m_new, v_new


def _adamw_layer(name, kind, layer, w, m, v, land, g_full, my_idx, prev, tr):
    rows2, cols = w.shape
    rows = rows2 // DEPTH
    nblk = rows // tr
    own_spec = _own_slab_spec(kind, tr, cols, nblk)
    n_prev = 0 if prev is None else 4

    def body(idx_ref, w_ref, m_ref, v_ref, land_ref, own_ref, *rest):
        g_ref, d_ref, mo_ref, vo_ref = rest[n_prev:]
        me = idx_ref[0]
        g = None
        for p in range(N_DEV):
            part = jnp.where(me == p, own_ref[...], land_ref[p]).astype(f32)
            g = part if g is None else g + part
        delta, m_new, v_new = _adamw_math(w_ref[...], m_ref[...], v_ref[...], g)
        g_ref[...] = g
        d_ref[...] = delta
        mo_ref[...] = m_new
        vo_ref[...] = v_new

    blk = pl.BlockSpec((tr, cols), lambda i, idx: (layer * nblk + i, 0))
    grid_spec = pltpu.PrefetchScalarGridSpec(
        num_scalar_prefetch=1, grid=(nblk,),
        in_specs=[blk, blk, blk, pl.BlockSpec((N_DEV, tr, cols), lambda i, idx: (0, i, 0)), own_spec] + [ANY_SPEC] * n_prev,
        out_specs=[blk, blk, blk, blk])
    aliases = {} if prev is None else {6 + k: k for k in range(4)}
    return pl.pallas_call(
        body, grid_spec=grid_spec, out_shape=_out_hbm([SDS((rows2, cols), f32)] * 4), name=name, input_output_aliases=aliases,
        compiler_params=_cparams(1),
    )(*_in_hbm([my_idx, w, m, v, land, g_full, *([] if prev is None else prev)]))


def _bucket_table():
    qi = np.arange(BLK)[:, None]
    kj = np.arange(2 * BLK)[None, :]
    dist = qi + BLK - kj
    dcl = np.clip(dist, 0, None)
    max_exact = N_BUCKETS // 2
    d_f = np.maximum(dcl, 1).astype(np.float32)
    large = max_exact + (np.log(d_f / np.float32(max_exact)) / np.float32(math.log(128 / max_exact))
                         * np.float32(N_BUCKETS - max_exact)).astype(np.int32)
    large = np.minimum(large, N_BUCKETS - 1)
    bucket = np.where(dcl < max_exact, dcl, large)
    in_window = (dist >= 0) & (dist < BLK)
    return bucket.astype(np.int32), in_window


def _onehot_buckets():
    bucket, _ = _bucket_table()
    oh = (bucket.reshape(-1)[None, :] == np.arange(N_BUCKETS)[:, None]).astype(np.float32)
    return oh


def _bias_build(rel_bias_t, onehot_t):
    def body(r_ref, o_ref, out_ref):
        r = r_ref[...]
        hi = r.astype(bf16)
        r1 = r - hi.astype(f32)
        mid = r1.astype(bf16)
        lo = (r1 - mid.astype(f32)).astype(bf16)
        oh = o_ref[...]
        out_ref[...] = _dot(hi, oh, NN_DIMS) + _dot(mid, oh, NN_DIMS) + _dot(lo, oh, NN_DIMS)

    tn = 4096
    return pl.pallas_call(
        body, grid=(BLK * 2 * BLK // tn,),
        in_specs=[pl.BlockSpec((NQ, N_BUCKETS), lambda i: (0, 0)), pl.BlockSpec((N_BUCKETS, tn), lambda i: (0, i))],
        out_specs=pl.BlockSpec((NQ, tn), lambda i: (0, i)), out_shape=SDS((NQ, BLK * 2 * BLK), f32), name="bias_build",
        compiler_params=_cparams(1),
    )(rel_bias_t, onehot_t)


def _bias_grad(dbias0, dbias1, onehot_t):
    tn = 4096
    nsteps = BLK * 2 * BLK // tn

    def body(a_ref, b_ref, o_ref, out_ref):
        g = a_ref[...] + b_ref[...]
        hi = g.astype(bf16)
        lo = (g - hi.astype(f32)).astype(bf16)
        part = _dot(hi, o_ref[...], NT_DIMS) + _dot(lo, o_ref[...], NT_DIMS)

        @pl.when(pl.program_id(0) == 0)
        def _():
            out_ref[...] = part

        @pl.when(pl.program_id(0) > 0)
        def _():
            out_ref[...] += part

    return pl.pallas_call(
        body, grid=(nsteps,),
        in_specs=[pl.BlockSpec((NQ, tn), lambda i: (0, i)), pl.BlockSpec((NQ, tn), lambda i: (0, i)),
                  pl.BlockSpec((N_BUCKETS, tn), lambda i: (0, i))],
        out_specs=pl.BlockSpec((NQ, N_BUCKETS), lambda i: (0, 0)), out_shape=SDS((NQ, N_BUCKETS), f32), name="bias_grad",
        compiler_params=_cparams(1),
    )(dbias0, dbias1, onehot_t)


def _attn_mask(n):
    qi = lax.broadcasted_iota(jnp.int32, (BLK, 2 * BLK), 0)
    kj = lax.broadcasted_iota(jnp.int32, (BLK, 2 * BLK), 1)
    dist = qi + BLK - kj
    first_key = jnp.where(n > 0, 0, BLK)
    return (dist >= 0) & (dist < BLK) & (kj >= first_key)


def _row_mean(a):
    return jnp.mean(a, axis=-1, keepdims=True)


def _head_norm(t, gain):
    r = lax.rsqrt(_row_mean(t * t) + EPS)
    that = t * r
    return that, r, that * gain


def _softmax_with_sink(s, sink):
    m = jnp.maximum(jnp.max(s, axis=-1, keepdims=True), sink)
    p = jnp.exp(s - m)
    psink = jnp.exp(sink - m)
    inv = 1.0 / (jnp.sum(p, axis=-1, keepdims=True) + psink)
    return p * inv, psink * inv


GQ = NQ // NKV


def _attn_fwd(qkv, q_gain, k_gain, sinks, bias, layer):
    def body(q_ref, kc_ref, kp_ref, vc_ref, vp_ref, qg_ref, kg_ref, sk_ref, bias_ref, o_ref):
        m = pl.program_id(0)
        qg = qg_ref[layer:layer + 1, :]
        kg = kg_ref[layer:layer + 1, :]
        grp = range(NKV)
        chains = [(b, j) for b in range(2) for j in grp]
        masks = [jnp.tile(_attn_mask(2 * m + b), (GQ, 1)) for b in range(2)]
        kblk = [[kp_ref[:, pl.ds(HD * j, HD)].astype(f32), kc_ref[0:BLK, pl.ds(HD * j, HD)].astype(f32),
                 kc_ref[BLK:, pl.ds(HD * j, HD)].astype(f32)] for j in grp]
        vblk = [[vp_ref[:, pl.ds(HD * j, HD)].astype(bf16), vc_ref[0:BLK, pl.ds(HD * j, HD)].astype(bf16),
                 vc_ref[BLK:, pl.ds(HD * j, HD)].astype(bf16)] for j in grp]
        knb = [[_head_norm(kblk[j][t], kg)[2].astype(bf16) for t in range(3)] for j in grp]
        kn_b = {(b, j): jnp.concatenate([knb[j][b], knb[j][b + 1]], axis=0) for b, j in chains}
        vbs = {(b, j): jnp.concatenate([vblk[j][b], vblk[j][b + 1]], axis=0) for b, j in chains}
        rows = {}
        for b, j in chains:
            heads = [GQ * j + g for g in range(GQ)]
            rows[b, j] = (jnp.concatenate([q_ref[pl.ds(BLK * b, BLK), pl.ds(HD * h, HD)] for h in heads], axis=0).astype(f32),
                          jnp.concatenate([jnp.broadcast_to(sk_ref[layer:layer + 1, h:h + 1], (BLK, 1)) for h in heads], axis=0))
        qn_b = {c: _head_norm(rows[c][0], qg)[2].astype(bf16) for c in chains}
        ss = {(b, j): _dot(qn_b[b, j], kn_b[b, j], NT_DIMS) * (HD ** -0.5) + bias_ref[GQ * j:GQ * (j + 1)].reshape(GQ * BLK, 2 * BLK)
              for b, j in chains}
        ps = {(b, j): _softmax_with_sink(jnp.where(masks[b], ss[b, j], -jnp.inf), rows[b, j][1])[0] for b, j in chains}
        outs = {c: _dot(ps[c].astype(bf16), vbs[c], NN_DIMS).astype(bf16) for c in chains}
        for b, j in chains:
            for g in range(GQ):
                o_ref[pl.ds(BLK * b, BLK), pl.ds(HD * (GQ * j + g), HD)] = outs[b, j][BLK * g:BLK * (g + 1), :]

    prev = lambda m: jnp.maximum(2 * m - 1, 0)
    small = lambda shape: pl.BlockSpec(shape, lambda m: (0,) * len(shape))
    return pl.pallas_call(
        body, grid=(NBLK // 2,),
        in_specs=[pl.BlockSpec((2 * BLK, D_ATTN), lambda m: (m, 0)),
                  pl.BlockSpec((2 * BLK, 128), lambda m: (m, 4)), pl.BlockSpec((BLK, 128), lambda m: (prev(m), 4)),
                  pl.BlockSpec((2 * BLK, 128), lambda m: (m, 5)), pl.BlockSpec((BLK, 128), lambda m: (prev(m), 5)),
                  small((DEPTH, HD)), small((DEPTH, HD)), small((DEPTH, NQ)), small((NQ, BLK, 2 * BLK))],
        out_specs=pl.BlockSpec((2 * BLK, D_ATTN), lambda m: (m, 0)), out_shape=_out_hbm(SDS((S, D_ATTN), bf16)),
        name="attn_fwd", compiler_params=_cparams(1),
    )(*_in_hbm([qkv, qkv, qkv, qkv, qkv, q_gain, k_gain, sinks, bias]))


def _attn_bwd(qkv, dmix, q_gain, k_gain, sinks, bias, layer, deps=()):
    def body(q_ref, kc_ref, kp_ref, vc_ref, vp_ref, do_ref, qg_ref, kg_ref, sk_ref, bias_ref, *rest):
        dqkv_ref, dbias_ref, dsm_ref, carry = rest[len(deps):]
        i = pl.program_id(0)
        m = NBLK // 2 - 1 - i
        qg = qg_ref[layer:layer + 1, :]
        kg = kg_ref[layer:layer + 1, :]
        lane = lax.broadcasted_iota(jnp.int32, (1, 128), 1)

        @pl.when(i == 0)
        def _():
            carry[...] = jnp.zeros_like(carry)
            dbias_ref[...] = jnp.zeros_like(dbias_ref)
            dsm_ref[...] = jnp.zeros_like(dsm_ref)

        grp = range(NKV)
        chains = [(b, j) for b in range(2) for j in grp]
        masks = [jnp.tile(_attn_mask(2 * m + b), (GQ, 1)) for b in range(2)]
        kblk = [[kp_ref[:, pl.ds(HD * j, HD)].astype(f32), kc_ref[0:BLK, pl.ds(HD * j, HD)].astype(f32),
                 kc_ref[BLK:, pl.ds(HD * j, HD)].astype(f32)] for j in grp]
        vblk = [[vp_ref[:, pl.ds(HD * j, HD)].astype(bf16), vc_ref[0:BLK, pl.ds(HD * j, HD)].astype(bf16),
                 vc_ref[BLK:, pl.ds(HD * j, HD)].astype(bf16)] for j in grp]
        knorm = [[_head_norm(kblk[j][t], kg) for t in range(3)] for j in grp]
        kn_b = {(b, j): jnp.concatenate([knorm[j][b][2].astype(bf16), knorm[j][b + 1][2].astype(bf16)], axis=0) for b, j in chains}
        vbs = {(b, j): jnp.concatenate([vblk[j][b], vblk[j][b + 1]], axis=0) for b, j in chains}
        rows, do_b = {}, {}
        for b, j in chains:
            heads = [GQ * j + g for g in range(GQ)]
            qrows = pl.ds(BLK * b, BLK)
            rows[b, j] = (jnp.concatenate([q_ref[qrows, pl.ds(HD * h, HD)] for h in heads], axis=0).astype(f32),
                          jnp.concatenate([jnp.broadcast_to(sk_ref[layer:layer + 1, h:h + 1], (BLK, 1)) for h in heads], axis=0))
            do_b[b, j] = jnp.concatenate([do_ref[qrows, pl.ds(HD * h, HD)] for h in heads], axis=0).astype(bf16)
        qnorm = {c: _head_norm(rows[c][0], qg) for c in chains}
        qn_b = {c: qnorm[c][2].astype(bf16) for c in chains}
        ss = {(b, j): _dot(qn_b[b, j], kn_b[b, j], NT_DIMS) * (HD ** -0.5) + bias_ref[GQ * j:GQ * (j + 1)].reshape(GQ * BLK, 2 * BLK)
              for b, j in chains}
        sm = {(b, j): _softmax_with_sink(jnp.where(masks[b], ss[b, j], -jnp.inf), rows[b, j][1]) for b, j in chains}
        dps = {c: _dot(do_b[c], vbs[c], NT_DIMS) for c in chains}
        deltas = {c: jnp.sum(sm[c][0] * dps[c], axis=-1, keepdims=True) for c in chains}
        dss = {c: sm[c][0] * (dps[c] - deltas[c]) for c in chains}
        ds_b = {c: (dss[c] * (HD ** -0.5)).astype(bf16) for c in chains}
        dqn = {c: _dot(ds_b[c], kn_b[c], NN_DIMS) for c in chains}
        dkn = {c: _dot(ds_b[c], qn_b[c], TN_DIMS) for c in chains}
        dvs = {c: _dot(sm[c][0].astype(bf16), do_b[c], TN_DIMS) for c in chains}
        dqg = jnp.zeros((1, HD), f32)
        dkg = jnp.zeros((1, HD), f32)
        dsink = jnp.zeros((1, 128), f32)
        for b, j in chains:
            dbias_ref[GQ * j:GQ * (j + 1)] += dss[b, j].reshape(GQ, BLK, 2 * BLK)
            dsk = sm[b, j][1] * deltas[b, j]
            for g in range(GQ):
                dsink = dsink + jnp.where(lane == GQ * j + g, -_sum11(dsk[BLK * g:BLK * (g + 1), :]), 0.0)
            qhat, rq, _ = qnorm[b, j]
            w = dqn[b, j] * qg
            dq = rq * (w - qhat * _row_mean(qhat * w))
            for g in range(GQ):
                dqkv_ref[pl.ds(BLK * b, BLK), pl.ds(HD * (GQ * j + g), HD)] = dq[BLK * g:BLK * (g + 1), :].astype(bf16)
            dqg = dqg + jnp.sum(dqn[b, j] * qhat, axis=0, keepdims=True)
        for j in grp:
            dkn_t = [dkn[0, j][:BLK, :], dkn[0, j][BLK:, :] + dkn[1, j][:BLK, :], dkn[1, j][BLK:, :]]
            dv_t = [dvs[0, j][:BLK, :], dvs[0, j][BLK:, :] + dvs[1, j][:BLK, :], dvs[1, j][BLK:, :]]
            dk_t = []
            for t in range(3):
                khat, rk, _ = knorm[j][t]
                w = dkn_t[t] * kg
                dk_t.append(rk * (w - khat * _row_mean(khat * w)))
                dkg = dkg + jnp.sum(dkn_t[t] * khat, axis=0, keepdims=True)
            kcols, vcols = pl.ds(D_ATTN + HD * j, HD), pl.ds(D_ATTN + 128 + HD * j, HD)
            dqkv_ref[BLK:, kcols] = (dk_t[2] + carry[:, pl.ds(HD * j, HD)]).astype(bf16)
            dqkv_ref[BLK:, vcols] = (dv_t[2] + carry[:, pl.ds(128 + HD * j, HD)]).astype(bf16)
            dqkv_ref[0:BLK, kcols] = dk_t[1].astype(bf16)
            dqkv_ref[0:BLK, vcols] = dv_t[1].astype(bf16)
            carry[:, pl.ds(HD * j, HD)] = dk_t[0]
            carry[:, pl.ds(128 + HD * j, HD)] = dv_t[0]
        dsm_ref[0:1, 0:HD] += dqg
        dsm_ref[1:2, 0:HD] += dkg
        dsm_ref[2:3, :] += dsink

    rev = lambda i: NBLK // 2 - 1 - i
    prev = lambda i: jnp.maximum(NBLK - 3 - 2 * i, 0)
    small = lambda shape: pl.BlockSpec(shape, lambda i: (0,) * len(shape))
    return pl.pallas_call(
        body, grid=(NBLK // 2,),
        in_specs=[pl.BlockSpec((2 * BLK, D_ATTN), lambda i: (rev(i), 0)),
                  pl.BlockSpec((2 * BLK, 128), lambda i: (rev(i), 4)), pl.BlockSpec((BLK, 128), lambda i: (prev(i), 4)),
                  pl.BlockSpec((2 * BLK, 128), lambda i: (rev(i), 5)), pl.BlockSpec((BLK, 128), lambda i: (prev(i), 5)),
                  pl.BlockSpec((2 * BLK, D_ATTN), lambda i: (rev(i), 0)),
                  small((DEPTH, HD)), small((DEPTH, HD)), small((DEPTH, NQ)), small((NQ, BLK, 2 * BLK))] + [ANY_SPEC] * len(deps),
        out_specs=[pl.BlockSpec((2 * BLK, 768), lambda i: (rev(i), COL_QKV // 768)), small((NQ, BLK, 2 * BLK)), small((8, 128))],
        out_shape=_out_hbm([SDS((S, D_IN_PAD), bf16), SDS((NQ, BLK, 2 * BLK), f32), SDS((8, 128), f32)]),
        scratch_shapes=[pltpu.VMEM((BLK, 256), f32)], name="attn_bwd", compiler_params=_cparams(1),
    )(*_in_hbm([qkv, qkv, qkv, qkv, qkv, dmix, q_gain, k_gain, sinks, bias, *deps]))


CONV_TC = 256


def _shift_down(u, s):
    if s == 0:
        return u
    rows = lax.broadcasted_iota(jnp.int32, u.shape, 0)
    return jnp.where(rows >= s, pltpu.roll(u, s, 0), 0.0)


def _shift_up(u, s):
    if s == 0:
        return u
    rows = lax.broadcasted_iota(jnp.int32, u.shape, 0)
    return jnp.where(rows < u.shape[0] - s, pltpu.roll(u, u.shape[0] - s, 0), 0.0)


def _conv_specs():
    return [pl.BlockSpec((S, CONV_TC), lambda c: (0, c)),
            pl.BlockSpec((None, 4, CONV_TC), lambda c: (0, 0, c)),
            pl.BlockSpec((DEPTH, CONV_TC), lambda c: (0, c))]


def _conv_pre(u, w_ref, b_ref, layer):
    pre = b_ref[layer:layer + 1, :] + w_ref[3:4, :] * u
    for k in range(3):
        pre = pre + w_ref[k:k + 1, :] * _shift_down(u, 3 - k)
    return pre


def _conv_fwd(xbc, conv_w, conv_b, layer):
    def body(u_ref, w_ref, b_ref, o_ref):
        pre = _conv_pre(u_ref[...].astype(f32), w_ref, b_ref, layer)
        o_ref[...] = pre * _sigmoid(pre)

    specs = _conv_specs()
    specs[1] = pl.BlockSpec((None, 4, CONV_TC), lambda c: (layer, 0, c))
    return pl.pallas_call(
        body, grid=(D_CONV // CONV_TC,), in_specs=specs, out_specs=pl.BlockSpec((S, CONV_TC), lambda c: (0, c)),
        out_shape=_out_hbm(SDS((S, D_CONV), f32)), name="conv_fwd", compiler_params=_cparams(1),
    )(*_in_hbm([xbc, conv_w, conv_b]))


def _conv_bwd(xbc, dact, conv_w, conv_b, dproj, layer):
    def body(u_ref, w_ref, b_ref, da_ref, dproj_in, du_ref, dw_ref, db_ref):
        u = u_ref[...].astype(f32)
        pre = _conv_pre(u, w_ref, b_ref, layer)
        sg = _sigmoid(pre)
        dpre = da_ref[...] * (sg * (1.0 + pre * (1.0 - sg)))
        du = w_ref[3:4, :] * dpre
        for k in range(3):
            du = du + w_ref[k:k + 1, :] * _shift_up(dpre, 3 - k)
        du_ref[...] = du.astype(bf16)
        db_ref[...] = jnp.broadcast_to(jnp.sum(dpre, axis=0, keepdims=True), db_ref.shape)
        dw_ref[...] = jnp.zeros_like(dw_ref)
        for k in range(4):
            dw_ref[k:k + 1, :] = jnp.sum(dpre * _shift_down(u, 3 - k), axis=0, keepdims=True)

    specs = _conv_specs()
    specs[1] = pl.BlockSpec((None, 4, CONV_TC), lambda c: (layer, 0, c))
    col = pl.BlockSpec((S, CONV_TC), lambda c: (0, c))
    row8 = pl.BlockSpec((8, CONV_TC), lambda c: (0, c))
    return pl.pallas_call(
        body, grid=(D_CONV // CONV_TC,), in_specs=[*specs, col, ANY_SPEC],
        out_specs=[pl.BlockSpec((S, CONV_TC), lambda c: (0, COL_XBC // CONV_TC + c)), row8, row8],
        out_shape=_out_hbm([SDS((S, D_IN_PAD), bf16), SDS((8, D_CONV), f32), SDS((8, D_CONV), f32)]), name="conv_bwd",
        input_output_aliases={4: 0}, compiler_params=_cparams(1),
    )(*_in_hbm([xbc, conv_w, conv_b, dact, dproj]))


def _tri():
    return (lax.broadcasted_iota(jnp.int32, (BLK, BLK), 0) >= lax.broadcasted_iota(jnp.int32, (BLK, BLK), 1))


def _ssd_scalars(dt_ref, dtb_ref, alog_ref, layer):
    raw = dt_ref[:, 0:NSSM] + dtb_ref[layer:layer + 1, :]
    dtv = jnp.maximum(raw, 0.0) + jnp.log(1.0 + jnp.exp(-jnp.abs(raw)))
    a = -jnp.exp(alog_ref[layer:layer + 1, :])
    acs = jnp.dot(_tri().astype(f32), dtv * a, preferred_element_type=f32, precision=HIGHEST)
    return raw, dtv, a, acs


HG = NSSM // NGRP
GW = HG * HD


def _lane_expand(cols, g):
    lane_head = lax.broadcasted_iota(jnp.int32, (1, GW), 1) // HD
    out = cols[:, HG * g + HG - 1:HG * g + HG]
    for r in range(HG - 2, -1, -1):
        out = jnp.where(lane_head == r, cols[:, HG * g + r:HG * g + r + 1], out)
    return out


def _row_expand(vals, g):
    row_head = lax.broadcasted_iota(jnp.int32, (GW, 1), 0) // HD
    out = vals[:, HG * g + HG - 1:HG * g + HG]
    for r in range(HG - 2, -1, -1):
        out = jnp.where(row_head == r, vals[:, HG * g + r:HG * g + r + 1], out)
    return out


def _head_rowsums(a, g):
    sel = (lax.broadcasted_iota(jnp.int32, (GW, NSSM), 0) // HD + HG * g == lax.broadcasted_iota(jnp.int32, (GW, NSSM), 1)).astype(bf16)
    hi = a.astype(bf16)
    lo = (a - hi.astype(f32)).astype(bf16)
    return _dot(hi, sel, NN_DIMS) + _dot(lo, sel, NN_DIMS)


def _head_blocksums(v, g):
    sel = (lax.broadcasted_iota(jnp.int32, (GW, NSSM), 0) // HD + HG * g == lax.broadcasted_iota(jnp.int32, (GW, NSSM), 1)).astype(bf16)
    hi = v.astype(bf16)
    lo = (v - hi.astype(f32)).astype(bf16)
    return _dot(hi, sel, TN_DIMS) + _dot(lo, sel, TN_DIMS)


def _ssd_chunk_common(xc_ref, dt_ref, dtb_ref, alog_ref, h_rows, layer):
    raw, dtv, a, acs = _ssd_scalars(dt_ref, dtb_ref, alog_ref, layer)
    acs_t = acs.T
    last = acs[BLK - 1:BLK, :]
    c = dict(raw=raw, dtv=dtv, a=a, acs=acs, last=last, dte=jnp.exp(last - acs), e_all=jnp.exp(acs), cd=jnp.exp(last))
    grp, heads, tri = range(NGRP), range(NSSM), _tri()
    c["bm"] = [xc_ref[:, pl.ds(D_SSM + NSTATE * g, NSTATE)] for g in grp]
    c["bm_b"] = [c["bm"][g].astype(bf16) for g in grp]
    c["cm_b"] = [xc_ref[:, pl.ds(D_SSM + NGRP * NSTATE + NSTATE * g, NSTATE)].astype(bf16) for g in grp]
    c["cb"] = [_dot(c["cm_b"][g], c["bm_b"][g], NT_DIMS) for g in grp]
    c["x"] = [xc_ref[:, pl.ds(GW * g, GW)] for g in grp]
    c["dt"] = [_lane_expand(dtv, g) for g in grp]
    c["xdt"] = [c["x"][g] * c["dt"][g] for g in grp]
    c["xdt_b"] = [c["xdt"][g].astype(bf16) for g in grp]
    c["prev"] = [h_rows(g) for g in grp]
    c["prev_b"] = [c["prev"][g].astype(bf16) for g in grp]
    c["e"] = [_lane_expand(c["e_all"], g) for g in grp]
    c["y_off"] = [_dot(c["cm_b"][g], c["prev_b"][g], NT_DIMS) * c["e"][g] for g in grp]
    c["decay"] = [jnp.exp(jnp.where(tri, acs[:, h:h + 1] - acs_t[h:h + 1, :], -jnp.inf)) for h in heads]
    c["m"] = [c["cb"][h // HG] * c["decay"][h] for h in heads]
    c["m_b"] = [c["m"][h].astype(bf16) for h in heads]
    c["dte_x"] = [_lane_expand(c["dte"], g) for g in grp]
    c["xdte_b"] = [(c["xdt"][g] * c["dte_x"][g]).astype(bf16) for g in grp]
    return c


def _ssd_fwd(xact, z, dt, attn, dt_bias, a_log, d_skip, norm_g, layer):
    def body(xc_ref, z_ref, dt_ref, at_ref, dtb_ref, alog_ref, dsk_ref, ng_ref, mix_ref, hs_ref, y_ref, h_ref):
        n = pl.program_id(0)

        @pl.when(n == 0)
        def _():
            h_ref[...] = jnp.zeros_like(h_ref)

        hs_ref[...] = h_ref[...]
        c = _ssd_chunk_common(xc_ref, dt_ref, dtb_ref, alog_ref, lambda g: h_ref[pl.ds(GW * g, GW), :], layer)
        grp, heads = range(NGRP), range(NSSM)
        y_diag = [_dot(c["m_b"][h], c["xdt_b"][h // HG][:, HD * (h % HG):HD * (h % HG + 1)], NN_DIMS) for h in heads]
        new_st = [_dot(c["xdte_b"][g], c["bm_b"][g], TN_DIMS) for g in grp]
        for h in heads:
            y_ref[:, pl.ds(HD * h, HD)] = y_diag[h]
        dskip = dsk_ref[layer:layer + 1, :]
        for g in grp:
            cols = pl.ds(GW * g, GW)
            y_ref[:, cols] = y_ref[:, cols] + c["y_off"][g] + c["x"][g] * _lane_expand(dskip, g)
            h_ref[cols, :] = c["prev"][g] * _row_expand(c["cd"], g) + new_st[g]
        zv = z_ref[...].astype(f32)
        yz = y_ref[...] * (zv * _sigmoid(zv))
        mix_ref[:, 0:D_ATTN] = at_ref[...]
        for g in grp:
            yg = yz[:, GW * g:GW * (g + 1)]
            rs = lax.rsqrt(jnp.mean(yg * yg, axis=-1, keepdims=True) + EPS)
            mix_ref[:, D_ATTN + GW * g:D_ATTN + GW * (g + 1)] = (yg * rs * ng_ref[layer:layer + 1, GW * g:GW * (g + 1)]).astype(bf16)

    small = lambda shape: pl.BlockSpec(shape, lambda n: (0,) * len(shape))
    return pl.pallas_call(
        body, grid=(NBLK,),
        in_specs=[pl.BlockSpec((BLK, D_CONV), lambda n: (n, 0)), pl.BlockSpec((BLK, D_SSM), lambda n: (n, 0)),
                  pl.BlockSpec((BLK, 128), lambda n: (n, 0)), pl.BlockSpec((BLK, D_ATTN), lambda n: (n, 0)),
                  small((DEPTH, NSSM)), small((DEPTH, NSSM)), small((DEPTH, NSSM)), small((DEPTH, D_SSM))],
        out_specs=[pl.BlockSpec((BLK, D), lambda n: (n, 0)), pl.BlockSpec((None, NSSM * HD, NSTATE), lambda n: (n, 0, 0)),
                   pl.BlockSpec((BLK, D_SSM), lambda n: (n, 0))],
        out_shape=_out_hbm([SDS((S, D), bf16), SDS((NBLK, NSSM * HD, NSTATE), f32), SDS((S, D_SSM), f32)]),
        scratch_shapes=[pltpu.VMEM((NSSM * HD, NSTATE), f32)],
        name="ssd_fwd", compiler_params=_cparams(1),
    )(*_in_hbm([xact, z, dt, attn, dt_bias, a_log, d_skip, norm_g]))


def _ssd_bwd(xact, z, dt, dmix, hs, y, dt_bias, a_log, d_skip, norm_g, dproj, layer):
    def body(xc_ref, z_ref, dt_ref, do_ref, hs_ref, y_ref, dtb_ref, alog_ref, dsk_ref, ng_ref, dproj_in,
             dzdt_ref, dx_ref, dsm_ref, dh_ref, dy_ref):
        i = pl.program_id(0)

        @pl.when(i == 0)
        def _():
            dh_ref[...] = jnp.zeros_like(dh_ref)
            dsm_ref[...] = jnp.zeros_like(dsm_ref)

        c = _ssd_chunk_common(xc_ref, dt_ref, dtb_ref, alog_ref, lambda g: hs_ref[pl.ds(GW * g, GW), :], layer)
        raw, dtv, a = c["raw"], c["dtv"], c["a"]
        grp, heads = range(NGRP), range(NSSM)
        dskip = dsk_ref[layer:layer + 1, :]
        lane8 = lax.broadcasted_iota(jnp.int32, (1, NSSM), 1)
        sub8 = lax.broadcasted_iota(jnp.int32, (NSSM, 1), 0)

        zv = z_ref[...].astype(f32)
        sz = _sigmoid(zv)
        gz = zv * sz
        yv = y_ref[...]
        yz = yv * gz
        for g in grp:
            sl = slice(GW * g, GW * (g + 1))
            yg = yz[:, sl]
            rs = lax.rsqrt(jnp.mean(yg * yg, axis=-1, keepdims=True) + EPS)
            yhat = yg * rs
            dog = do_ref[:, sl]
            w = dog * ng_ref[layer:layer + 1, sl]
            dyz = rs * (w - yhat * jnp.mean(yhat * w, axis=-1, keepdims=True))
            dsm_ref[0:1, sl] += jnp.sum(dog * yhat, axis=0, keepdims=True)
            dy_ref[:, sl] = dyz * gz[:, sl]
            dzdt_ref[:, sl] = (dyz * yv[:, sl] * (sz[:, sl] * (1.0 + zv[:, sl] * (1.0 - sz[:, sl])))).astype(bf16)

        dy = [dy_ref[:, pl.ds(GW * g, GW)] for g in grp]
        dy_b = [dy[g].astype(bf16) for g in grp]
        hl = lambda h: slice(HD * (h % HG), HD * (h % HG + 1))
        dt_off_b = [(dy[g] * c["e"][g]).astype(bf16) for g in grp]
        dcm = [_dot(dt_off_b[g], c["prev_b"][g], NN_DIMS) for g in grp]
        dprev = [_dot(dt_off_b[g], c["cm_b"][g], TN_DIMS) for g in grp]
        yoff_rs = [_head_rowsums(dy[g] * c["y_off"][g], g) for g in grp]
        dhn = [dh_ref[pl.ds(GW * g, GW), :] for g in grp]
        dhn_b = [dhn[g].astype(bf16) for g in grp]
        dprev = [dprev[g] + dhn[g] * _row_expand(c["cd"], g) for g in grp]
        dhn_prev = [dhn[g] * c["prev"][g] for g in grp]
        u = [_dot(c["bm_b"][g], dhn_b[g], NT_DIMS) for g in grp]
        dbm = [_dot(c["xdte_b"][g], dhn_b[g], NN_DIMS) for g in grp]
        ddte_rs = [_head_rowsums(c["xdt"][g] * u[g], g) for g in grp]
        dm = [_dot(dy_b[h // HG][:, hl(h)], c["xdt_b"][h // HG][:, hl(h)], NT_DIMS) for h in heads]
        dxdt_in = [_dot(c["m_b"][h], dy_b[h // HG][:, hl(h)], TN_DIMS) for h in heads]
        dseg = [dm[h] * c["m"][h] for h in heads]
        dmd = [dm[h] * c["decay"][h] for h in heads]
        for h in heads:
            dx_ref[:, pl.ds(HD * h, HD)] = dxdt_in[h]

        tmp = (ddte_rs[0] + ddte_rs[1]) * c["dte"]
        dacs = yoff_rs[0] + yoff_rs[1] - tmp
        dacs_cols = jnp.zeros((NSSM, BLK), f32)
        ddtv = jnp.zeros((BLK, NSSM), f32)
        ddsk = jnp.zeros((BLK, NSSM), f32)
        hp = jnp.zeros((1, NSSM), f32)
        for g in grp:
            cols = pl.ds(GW * g, GW)
            dxdt = dx_ref[:, cols] + u[g] * c["dte_x"][g]
            dx_ref[:, cols] = dy[g] * _lane_expand(dskip, g) + dxdt * c["dt"][g]
            ddtv = ddtv + _head_rowsums(dxdt * c["x"][g], g)
            ddsk = ddsk + _head_rowsums(dy[g] * c["x"][g], g)
            dcb = dmd[HG * g]
            for r in range(1, HG):
                dcb = dcb + dmd[HG * g + r]
            dcb_b = dcb.astype(bf16)
            dx_ref[:, pl.ds(D_SSM + NSTATE * g, NSTATE)] = dbm[g] + _dot(dcb_b, c["cm_b"][g], TN_DIMS)
            dx_ref[:, pl.ds(D_SSM + NGRP * NSTATE + NSTATE * g, NSTATE)] = dcm[g] + _dot(dcb_b, c["bm_b"][g], NN_DIMS)
            dh_ref[cols, :] = dprev[g]
            hp = hp + _head_blocksums(jnp.sum(dhn_prev[g], axis=1, keepdims=True), g)
            for r in range(HG):
                h = HG * g + r
                dacs = dacs + (lane8 == h).astype(f32) * jnp.sum(dseg[h], axis=1, keepdims=True)
                dacs_cols = dacs_cols + (sub8 == h).astype(f32) * jnp.sum(dseg[h], axis=0, keepdims=True)
        dlast = hp * c["cd"] + jnp.sum(tmp, axis=0, keepdims=True)
        ddsk = jnp.sum(ddsk, axis=0, keepdims=True)

        row = lax.broadcasted_iota(jnp.int32, (BLK, 1), 0)
        dacs = dacs - dacs_cols.T + jnp.where(row == BLK - 1, dlast, 0.0)
        dda = lax.dot_general(_tri().astype(f32), dacs, TN_DIMS, preferred_element_type=f32, precision=HIGHEST)
        ddtv = ddtv + dda * a
        da = jnp.sum(dda * dtv, axis=0, keepdims=True)
        draw = ddtv * _sigmoid(raw)
        dzdt_ref[:, D_SSM:] = jnp.zeros((BLK, COL_XBC - COL_DT), bf16)
        dzdt_ref[:, D_SSM:D_SSM + NSSM] = draw.astype(bf16)
        dsm_ref[1:2, 0:NSSM] += jnp.sum(draw, axis=0, keepdims=True)
        dsm_ref[2:3, 0:NSSM] += da * a
        dsm_ref[3:4, 0:NSSM] += ddsk

    rev = lambda i: NBLK - 1 - i
    small = lambda shape: pl.BlockSpec(shape, lambda i: (0,) * len(shape))
    return pl.pallas_call(
        body, grid=(NBLK,),
        in_specs=[pl.BlockSpec((BLK, D_CONV), lambda i: (rev(i), 0)), pl.BlockSpec((BLK, D_SSM), lambda i: (rev(i), 0)),
                  pl.BlockSpec((BLK, 128), lambda i: (rev(i), 0)), pl.BlockSpec((BLK, D_SSM), lambda i: (rev(i), 1)),
                  pl.BlockSpec((None, NSSM * HD, NSTATE), lambda i: (rev(i), 0, 0)), pl.BlockSpec((BLK, D_SSM), lambda i: (rev(i), 0)),
                  small((DEPTH, NSSM)), small((DEPTH, NSSM)), small((DEPTH, NSSM)), small((DEPTH, D_SSM)), ANY_SPEC],
        out_specs=[pl.BlockSpec((BLK, COL_XBC - COL_Z), lambda i: (rev(i), COL_Z // (COL_XBC - COL_Z))),
                   pl.BlockSpec((BLK, D_CONV), lambda i: (rev(i), 0)), small((8, D_SSM))],
        out_shape=_out_hbm([SDS((S, D_IN_PAD), bf16), SDS((S, D_CONV), f32), SDS((8, D_SSM), f32)]),
        scratch_shapes=[pltpu.VMEM((NSSM * HD, NSTATE), f32), pltpu.VMEM((BLK, D_SSM), f32)],
        name="ssd_bwd", input_output_aliases={10: 0}, compiler_params=_cparams(1),
    )(*_in_hbm([xact, z, dt, dmix, hs, y, dt_bias, a_log, d_skip, norm_g, dproj]))


def _my_place():
    return lax.axis_index("x"), lax.axis_index("y"), lax.axis_index("c")


def _dev_index(px, py, pc):
    return 4 * px + 2 * py + pc


def _slab2(kind, ref, idx):
    if kind == "stack":
        return ref.at[idx]
    if kind == "rows128":
        return ref.at[pl.ds(pl.multiple_of(idx * 128, 128), 128), :]
    if kind == "rows512":
        return ref.at[pl.ds(pl.multiple_of(idx * 512, 512), 512), :]
    return ref.at[:, pl.ds(pl.multiple_of(idx * 512, 512), 512)]


def _slab_shape(kind, full_shape):
    if kind == "stack":
        return tuple(full_shape[1:])
    if kind == "rows128":
        return (128, full_shape[1])
    if kind == "rows512":
        return (512, full_shape[1])
    return (full_shape[0], 512)


KIND = dict(w_in="stack", w_out="rows128", w_up="cols512", w_down="rows512", conv_w="stack")
FULL_SHAPE = dict(w_in=(N_DEV, D, D_IN // N_DEV), w_out=(D, D), w_up=(D, D_FF), w_down=(D_FF, D))
HBM_SPEC = pl.BlockSpec(memory_space=pltpu.HBM)
SEM_SPEC = pl.BlockSpec(memory_space=pltpu.SEMAPHORE)
SIDE_EFFECT = pltpu.SideEffectType.DATAFLOW_SIDE_EFFECTING


def _peers_all():
    x, y, c = _my_place()
    return [(x ^ ((r >> 2) & 1), y ^ ((r >> 1) & 1), c ^ (r & 1)) for r in range(1, N_DEV)]


def _split_start(name, bufs, n_copies, plan, deps=()):
    nb = len(bufs)

    def body(*refs):
        ins = refs[:nb]
        send_sems, recv_sems = refs[nb + len(deps)], refs[nb + len(deps) + 1]
        token = refs[-1]
        for i, (src, dst, dev) in enumerate(plan(ins)):
            pltpu.make_async_remote_copy(src_ref=src, dst_ref=dst, send_sem=send_sems.at[i], recv_sem=recv_sems.at[i],
                                         device_id=dev, device_id_type=MESH).start()
        token[...] = jnp.zeros_like(token)

    outs = pl.pallas_call(
        body, name=name,
        out_shape=(pltpu.SemaphoreType.DMA((n_copies,)), pltpu.SemaphoreType.DMA((n_copies,)),
                   *[pltpu.HBM(b.shape, b.dtype) for b in bufs], SDS((8, 128), f32)),
        in_specs=[HBM_SPEC] * nb + [ANY_SPEC] * len(deps),
        out_specs=(SEM_SPEC, SEM_SPEC, *[HBM_SPEC] * nb, pl.BlockSpec(memory_space=pltpu.VMEM)),
        input_output_aliases={i: 2 + i for i in range(nb)},
        compiler_params=pltpu.CompilerParams(has_side_effects=SIDE_EFFECT),
    )(*[pltpu.with_memory_space_constraint(b, pltpu.HBM) for b in bufs], *deps)
    return dict(send=outs[0], recv=outs[1], bufs=list(outs[2:2 + nb]), token=outs[-1], plan=plan, n=n_copies)


def _split_wait(name, started, after):
    bufs = started["bufs"]
    nb = len(bufs)
    plan = started["plan"]

    def body(*refs):
        ins = refs[:nb]
        send_sems, recv_sems = refs[nb], refs[nb + 1]
        for i, (src, dst, dev) in enumerate(plan(ins)):
            cp = pltpu.make_async_remote_copy(src_ref=src, dst_ref=dst, send_sem=send_sems.at[i], recv_sem=recv_sems.at[i],
                                              device_id=dev, device_id_type=MESH)
            cp.wait_send()
            cp.wait_recv()

    outs = pl.pallas_call(
        body, name=name, out_shape=tuple(pltpu.HBM(b.shape, b.dtype) for b in bufs),
        in_specs=[HBM_SPEC] * nb + [SEM_SPEC, SEM_SPEC] + [ANY_SPEC] * len(after), out_specs=(HBM_SPEC,) * nb,
        input_output_aliases={i: i for i in range(nb)},
        compiler_params=pltpu.CompilerParams(has_side_effects=SIDE_EFFECT),
    )(*bufs, started["send"], started["recv"], *after)
    return list(outs)


def _gather_start(name, names, fulls, deps):
    n_t = len(names)

    def plan(refs):
        x, y, c = _my_place()
        my_idx = _dev_index(x, y, c)
        targets = [(x, y, 1 - c), (1 - x, y, c), (x, 1 - y, c), (1 - x, 1 - y, c)]
        slabs = [_slab2(KIND[names[t]], refs[t], my_idx) for t in range(n_t)]
        return [(slabs[t], slabs[t], dev) for t in range(n_t) for dev in targets]

    return _split_start(name, list(fulls), 4 * n_t, plan, deps)


def _gather_finish(name, names, started, after):
    n_t = len(names)
    fulls = _split_wait(name + "_wait", started, after)
    slab_shapes = [SDS(_slab_shape(KIND[n], f.shape), f.dtype) for n, f in zip(names, fulls)]

    def body(*refs):
        ins = refs[:n_t]
        outs = refs[n_t:2 * n_t]
        stage = refs[2 * n_t:3 * n_t]
        load_sems, send_sems, recv_sems = refs[3 * n_t:]
        x, y, c = _my_place()
        chips = [(1 - x, y), (x, 1 - y), (1 - x, 1 - y)]
        pairs = [(t, j) for t in range(n_t) for j in range(3)]
        loads = [pltpu.make_async_copy(_slab2(KIND[names[t]], ins[t], _dev_index(*chips[j], c)), stage[t].at[j], load_sems.at[t, j])
                 for t, j in pairs]
        for cp in loads:
            cp.start()

        def copy(t, j, core):
            return pltpu.make_async_remote_copy(
                src_ref=stage[t].at[j], dst_ref=_slab2(KIND[names[t]], outs[t], _dev_index(*chips[j], core)),
                send_sem=send_sems.at[t, j], recv_sem=recv_sems.at[t, j], device_id=(x, y, 1 - c), device_id_type=MESH)

        sends = [copy(t, j, c) for t, j in pairs]
        for ld, cp in zip(loads, sends):
            ld.wait()
            cp.start()
        for t, j in pairs:
            copy(t, j, 1 - c).wait_recv()
        for cp in sends:
            cp.wait_send()

    return pl.pallas_call(
        body, in_specs=[ANY_SPEC] * n_t, out_specs=[ANY_SPEC] * n_t, out_shape=[SDS(b.shape, b.dtype) for b in fulls],
        input_output_aliases={t: t for t in range(n_t)},
        scratch_shapes=[pltpu.VMEM((3,) + s.shape, s.dtype) for s in slab_shapes]
        + [pltpu.SemaphoreType.DMA((n_t, 3)), pltpu.SemaphoreType.DMA((n_t, 3)), pltpu.SemaphoreType.DMA((n_t, 3))],
        name=name + "_pass", compiler_params=pltpu.CompilerParams(vmem_limit_bytes=VMEM_LIMIT),
    )(*fulls)


def _exchange_start(name, names, grads, deps):
    n_t = len(names)
    lands = [lax.empty((N_DEV,) + _slab_shape(KIND[n], g.shape), g.dtype) for n, g in zip(names, grads)]

    def plan(refs):
        my_idx = _dev_index(*_my_place())
        return [(_slab2(KIND[names[t]], refs[t], _dev_index(*peer)), refs[n_t + t].at[my_idx], peer)
                for t in range(n_t) for peer in _peers_all()]

    return _split_start(name, list(grads) + lands, 7 * n_t, plan, deps)


def _small_exchange_start(part, deps):
    land = lax.empty((N_DEV,) + part.shape, part.dtype)

    def plan(refs):
        my_idx = _dev_index(*_my_place())
        return [(refs[0], refs[1].at[my_idx], peer) for peer in _peers_all()]

    return _split_start("small_exchange", [part, land], N_DEV - 1, plan, deps)


def _slab_pieces():
    sh = D_IN // N_DEV
    out = []
    for j in range(N_DEV):
        for first, end, dst in IN_SEGMENTS:
            lo, hi = max(first, sh * j), min(end, sh * (j + 1))
            if lo < hi:
                out.append((j, lo - sh * j, hi - sh * j, dst + lo - first))
    return out


def _w_in_assemble(stacked):
    tr = 256
    sh = D_IN // N_DEV

    def body(i_ref, o_ref):
        o_ref[:, COL_DT:COL_XBC] = jnp.zeros((tr, COL_XBC - COL_DT), bf16)
        for j, lo, hi, dst in _slab_pieces():
            o_ref[:, dst:dst + hi - lo] = i_ref[j, :, lo:hi]

    return pl.pallas_call(
        body, grid=(D // tr,), in_specs=[pl.BlockSpec((N_DEV, tr, sh), lambda i: (0, i, 0))],
        out_specs=pl.BlockSpec((None, tr, D_IN_PAD), lambda i: (0, i, 0)), out_shape=SDS((1, D, D_IN_PAD), bf16),
        name="w_in_assemble", compiler_params=_cparams(1),
    )(*_in_hbm([stacked]))


def _w_in_slabs(dw_in):
    tr = 256
    sh = D_IN // N_DEV

    def body(i_ref, o_ref):
        for j, lo, hi, src in _slab_pieces():
            o_ref[j, :, lo:hi] = i_ref[:, src:src + hi - lo]

    return pl.pallas_call(
        body, grid=(D // tr,), in_specs=[pl.BlockSpec((tr, D_IN_PAD), lambda i: (i, 0))],
        out_specs=pl.BlockSpec((N_DEV, tr, sh), lambda i: (0, i, 0)), out_shape=_out_hbm(SDS((N_DEV, D, sh), bf16)),
        name="w_in_slabs", compiler_params=_cparams(1),
    )(*_in_hbm([dw_in]))


SMALL_NAMES = ("mix_norm_g", "mlp_norm_g", "conv_b", "ssm_norm_g", "q_gain", "k_gain", "sinks", "dt_bias", "a_log", "d_skip",
               "rel_bias", "conv_w")
MISC_LANES = dict(q_gain=(LANE_QG, HD), k_gain=(LANE_KG, HD), sinks=(LANE_SINK, NQ), dt_bias=(LANE_DTB, NSSM),
                  a_log=(LANE_ALOG, NSSM), d_skip=(LANE_DSKIP, NSSM))


def _pack_small_grads(smalls, drel_t, loss):
    def body(*refs):
        o_ref = refs[-1]
        drel_ref, loss_ref = refs[-3], refs[-2]
        o_ref[...] = jnp.zeros_like(o_ref)
        for l in range(DEPTH):
            mixg, mlpg, convb, convw, ssd, attn = refs[6 * l:6 * l + 6]
            o_ref[ROW_MIXG + l:ROW_MIXG + l + 1, :] = mixg[...]
            o_ref[ROW_MLPG + l:ROW_MLPG + l + 1, :] = mlpg[...]
            o_ref[ROW_CONVB + l:ROW_CONVB + l + 1, :] = convb[0:1, :]
            o_ref[ROW_SSMG + l:ROW_SSMG + l + 1, 0:D_SSM] = ssd[0:1, :]
            o_ref[ROW_CONVW + 4 * l:ROW_CONVW + 4 * l + 4, :] = convw[0:4, :]
            row = slice(ROW_MISC + l, ROW_MISC + l + 1)
            o_ref[row, LANE_QG:LANE_QG + HD] = attn[0:1, 0:HD]
            o_ref[row, LANE_KG:LANE_KG + HD] = attn[1:2, 0:HD]
            o_ref[row, LANE_SINK:LANE_SINK + NQ] = attn[2:3, 0:NQ]
            o_ref[row, LANE_DTB:LANE_DTB + NSSM] = ssd[1:2, 0:NSSM]
            o_ref[row, LANE_ALOG:LANE_ALOG + NSSM] = ssd[2:3, 0:NSSM]
            o_ref[row, LANE_DSKIP:LANE_DSKIP + NSSM] = ssd[3:4, 0:NSSM]
        o_ref[ROW_RELB:ROW_RELB + NQ, 0:N_BUCKETS] = drel_ref[...]
        o_ref[ROW_LOSS:ROW_LOSS + 1, 0:1] = loss_ref[0:1, 0:1]

    args = []
    for sm in smalls:
        args += [sm["mix_norm_g"], sm["mlp_norm_g"], sm["conv_b"], sm["conv_w"], sm["ssd"], sm["attn"]]
    args += [drel_t, loss]
    return pl.pallas_call(body, out_shape=SDS((SMALL_ROWS, D), f32), name="pack_small_grads")(*args)


def _adamw_small(part, land, w, m, v):
    n = len(SMALL_NAMES)

    def grad_of(name, g_ref):
        if name == "mix_norm_g":
            return g_ref[ROW_MIXG:ROW_MIXG + DEPTH, :]
        if name == "mlp_norm_g":
            return g_ref[ROW_MLPG:ROW_MLPG + DEPTH, :]
        if name == "conv_b":
            return g_ref[ROW_CONVB:ROW_CONVB + DEPTH, :]
        if name == "ssm_norm_g":
            return g_ref[ROW_SSMG:ROW_SSMG + DEPTH, 0:D_SSM]
        if name == "rel_bias":
            return g_ref[ROW_RELB:ROW_RELB + NQ, 0:N_BUCKETS].T
        lane, width = MISC_LANES[name]
        return g_ref[ROW_MISC:ROW_MISC + DEPTH, lane:lane + width]

    def body(part_ref, land_ref, *refs):
        ws, ms, vs = refs[:n], refs[n:2 * n], refs[2 * n:3 * n]
        loss_ref = refs[3 * n]
        outs = refs[3 * n + 1:-1]
        g_ref = refs[-1]
        me = _dev_index(*_my_place())
        for p in range(N_DEV):
            term = jnp.where(me == p, part_ref[...], land_ref[p])
            if p == 0:
                g_ref[...] = term
            else:
                g_ref[...] += term
        loss_ref[...] = g_ref[ROW_LOSS:ROW_LOSS + 1, 0:128]
        my_cols = pl.ds(pl.multiple_of(me * 128, 128), 128)
        for k, name in enumerate(SMALL_NAMES):
            g_out, d_out, m_out, v_out = outs[4 * k:4 * k + 4]
            if name == "conv_w":
                for l in range(DEPTH):
                    g = g_ref[ROW_CONVW + 4 * l:ROW_CONVW + 4 * l + 4, my_cols]
                    delta, m_new, v_new = _adamw_math(ws[k][l], ms[k][l], vs[k][l], g)
                    g_out[l], d_out[l], m_out[l], v_out[l] = g, delta, m_new, v_new
            else:
                g = grad_of(name, g_ref)
                delta, m_new, v_new = _adamw_math(ws[k][...], ms[k][...], vs[k][...], g)
                g_out[...], d_out[...], m_out[...], v_out[...] = g, delta, m_new, v_new

    ws = [w[name] for name in SMALL_NAMES]
    out_shape = [SDS((1, 128), f32)]
    for a in ws:
        out_shape += [SDS(a.shape, f32)] * 4
    return pl.pallas_call(body, out_shape=out_shape, name="adamw_small", scratch_shapes=[pltpu.VMEM((SMALL_ROWS, D), f32)])(
        part, land, *ws, *[m[name] for name in SMALL_NAMES], *[v[name] for name in SMALL_NAMES])


def _plain(tm, tn):
    return pl.BlockSpec((tm, tn), lambda i, j, k: (i, j))


def _rowblk(tm, width):
    return pl.BlockSpec((tm, width), lambda i, j, k: (i, 0))


def _store_epi(dtype):
    def epi(acc, i, j, ex, outs):
        outs[0][...] = acc.astype(dtype)
    return epi


def _rms_prologue(layer):
    def pro(a_ref, ex, outs):
        xv = a_ref[...]
        r = lax.rsqrt(jnp.mean(xv * xv, axis=-1, keepdims=True) + EPS)
        h = (xv * r * ex[0][layer:layer + 1, :]).astype(bf16)
        outs[-1][...] = h
        return h
    return pro


MLP_TM = 256
MLP_VMEM = 56 * 1024 * 1024


def _resident(shape):
    return pl.BlockSpec((None,) + shape, lambda i: (0, 0, 0), pipeline_mode=pl.Buffered(1))


def _mlp_fwd(layer, x, mix, g, w_out, w_up, w_down, tgt=None):
    tm = MLP_TM
    with_loss = tgt is not None

    def body(x_ref, mix_ref, g_ref, wo_ref, wu_ref, wd_ref, *rest):
        xm_ref, a_ref, r_ref, h_ref = rest[with_loss:with_loss + 4]
        rest = rest[:with_loss] + rest[with_loss + 1:]
        i = pl.program_id(0)
        xv = x_ref[...] + _dot(mix_ref[...], wo_ref[...], NN_DIMS)
        xm_ref[...] = xv
        h = (xv * lax.rsqrt(jnp.mean(xv * xv, axis=-1, keepdims=True) + EPS) * g_ref[layer:layer + 1, :]).astype(bf16)
        h_ref[...] = h
        r = jnp.maximum(_dot(h, wu_ref[...], NN_DIMS), 0.0)
        a = (r * r).astype(bf16)
        a_ref[...] = a
        r_ref[...] = r.astype(bf16)
        y = xv + _dot(a, wd_ref[...], NN_DIMS)
        if not with_loss:
            rest[3][...] = y
            return
        err = y - rest[0][...]
        rest[4][...] = err * (1.0 / D)
        part = 0.5 * jnp.sum(jnp.mean(err * err, axis=-1, keepdims=True), axis=0, keepdims=True)

        @pl.when(i == 0)
        def _():
            rest[5][...] = jnp.zeros_like(rest[5])

        rest[5][...] += jnp.broadcast_to(part, rest[5].shape)

    row = lambda width: pl.BlockSpec((tm, width), lambda i: (i, 0))
    in_specs = [row(D), row(D), pl.BlockSpec((DEPTH, D), lambda i: (0, 0)), _resident((D, D)), _resident((D, D_FF)),
                _resident((D_FF, D))]
    out_specs = [row(D), row(D_FF), row(D_FF), row(D), row(D)]
    out_shape = [SDS((S, D), f32), SDS((S, D_FF), bf16), SDS((S, D_FF), bf16), SDS((S, D), bf16), SDS((S, D), f32)]
    args = [x, mix, g, w_out, w_up, w_down]
    if with_loss:
        in_specs.append(row(D))
        args.append(tgt)
        out_specs.append(pl.BlockSpec((1, 128), lambda i: (0, 0)))
        out_shape.append(SDS((1, 128), f32))
    return pl.pallas_call(
        body, grid=(S // tm,), in_specs=in_specs, out_specs=out_specs, out_shape=_out_hbm(out_shape),
        name="mlp_fwd_loss" if with_loss else "mlp_fwd",
        compiler_params=pltpu.CompilerParams(dimension_semantics=("arbitrary",), vmem_limit_bytes=MLP_VMEM),
    )(*_in_hbm(args[:3]), *args[3:6], *_in_hbm(args[6:]))


def _mlp_bwd_act(layer, dx_out, r_act, x_mid, g, w_down, w_up, w_out, deps):
    tm = MLP_TM

    def body(dxo_ref, r_ref, xm_ref, g_ref, wd_ref, wu_ref, wo_ref, *rest):
        du_ref, dx_ref, dg_ref, dmix_ref = rest[len(deps):]
        dxo = dxo_ref[...]
        du = (_dot(dxo.astype(bf16), wd_ref[...], NT_DIMS) * (2.0 * r_ref[...].astype(f32))).astype(bf16)
        du_ref[...] = du
        dh = _dot(du, wu_ref[...], NT_DIMS)
        _rms_bwd_epilogue(layer)(dh, pl.program_id(0), 0, (xm_ref, g_ref, dxo_ref), (dx_ref, dg_ref))
        dmix_ref[...] = _dot(dx_ref[...].astype(bf16), wo_ref[...], NT_DIMS)

    row = lambda width: pl.BlockSpec((tm, width), lambda i: (i, 0))
    return pl.pallas_call(
        body, grid=(S // tm,),
        in_specs=[row(D), row(D_FF), row(D), pl.BlockSpec((DEPTH, D), lambda i: (0, 0)), _resident((D_FF, D)), _resident((D, D_FF)),
                  _resident((D, D))] + [ANY_SPEC] * len(deps),
        out_specs=[row(D_FF), row(D), pl.BlockSpec((1, D), lambda i: (0, 0)), row(D)],
        out_shape=_out_hbm([SDS((S, D_FF), bf16), SDS((S, D), f32), SDS((1, D), f32), SDS((S, D), f32)]), name="mlp_bwd_act",
        compiler_params=pltpu.CompilerParams(dimension_semantics=("arbitrary",), vmem_limit_bytes=MLP_VMEM),
    )(*_in_hbm([dx_out, r_act, x_mid, g]), w_down, w_up, w_out, *_in_hbm(deps))


def _layer_fwd(l, x, p, get_weights, bias, tgt=None):
    wts = get_weights(l, "in", [x, bias])
    gfull = pl.BlockSpec((DEPTH, D), lambda i, j, k: (0, 0))
    tm = 512

    def inproj_epi(acc, i, j, ex, outs):
        outs[0][...] = acc[:, COL_QKV:COL_Z].astype(bf16)
        outs[1][...] = acc[:, COL_Z:COL_DT].astype(bf16)
        outs[2][...] = acc[:, COL_XBC:D_IN_PAD].astype(bf16)
        outs[3][...] = acc[:, COL_DT:COL_DT + 128]

    qkv, z, xbc, dt, h1 = _matmul(
        "in_proj", "nn", x, wts["w_in"], tm=tm, tn=D_IN_PAD, tk=D, prologue=_rms_prologue(l),
        extras=(p["mix_norm_g"],), extra_specs=(gfull,),
        out_shape=[SDS((S, 768), bf16), SDS((S, 512), bf16), SDS((S, 1024), bf16), SDS((S, 128), f32), SDS((S, D), bf16)],
        out_specs=[_rowblk(tm, 768), _rowblk(tm, 512), _rowblk(tm, 1024), _rowblk(tm, 128), _rowblk(tm, D)], epilogue=inproj_epi)
    attn = _attn_fwd(qkv, p["q_gain"], p["k_gain"], p["sinks"], bias, l)
    xact = _conv_fwd(xbc, wts["conv_w"], p["conv_b"], l)
    mix, hs, y_ssd = _ssd_fwd(xact, z, dt, attn, p["dt_bias"], p["a_log"], p["d_skip"], p["ssm_norm_g"], l)
    wts = dict(wts, **get_weights(l, "rest", [mix]))

    x_mid, a_act, r_act, h2, *result = _mlp_fwd(l, x, mix, p["mlp_norm_g"], wts["w_out"], wts["w_up"], wts["w_down"], tgt)
    saved = dict(x=x, h1=h1, qkv=qkv, z=z, xbc=xbc, dt=dt, xact=xact, mix=mix, hs=hs, y_ssd=y_ssd, x_mid=x_mid, h2=h2,
                 a=a_act, r=r_act, wts=wts)
    return (result[0] if tgt is None else tuple(result)), saved


def _layer_bwd(l, dx_out, sv, p, bias, deps, send):
    wts = sv["wts"]

    dw_down = _matmul("dw_down", "tn", sv["a"], dx_out, tm=512, tn=D, tk=S, out_shape=SDS((D_FF, D), bf16),
                      out_specs=_plain(512, D), epilogue=_store_epi(bf16), deps=deps)
    deps = send(l, dict(w_down=dw_down))
    du, dx_mid, dg_mlp, dmix = _mlp_bwd_act(l, dx_out, sv["r"], sv["x_mid"], p["mlp_norm_g"], wts["w_down"], wts["w_up"],
                                            wts["w_out"], deps)
    dw_up = _matmul("dw_up", "tn", sv["h2"], du, tm=D, tn=512, tk=S, out_shape=SDS((D, D_FF), bf16),
                    out_specs=_plain(D, 512), epilogue=_store_epi(bf16))
    dw_out = _matmul("dw_out", "tn", sv["mix"], dx_mid, tm=D, tn=512, tk=S, out_shape=SDS((D, D), bf16),
                     out_specs=_plain(D, 512), epilogue=_store_epi(bf16))
    deps = send(l, dict(w_up=dw_up, w_out=dw_out))
    gfull = pl.BlockSpec((DEPTH, D), lambda i, j, k: (0, 0))
    grow = pl.BlockSpec((1, D), lambda i, j, k: (0, 0))
    dproj, dbias, dsm_attn = _attn_bwd(sv["qkv"], dmix, p["q_gain"], p["k_gain"], p["sinks"], bias, l, deps)
    dproj, dxact, dsm_ssd = _ssd_bwd(sv["xact"], sv["z"], sv["dt"], dmix, sv["hs"], sv["y_ssd"], p["dt_bias"], p["a_log"],
                                     p["d_skip"], p["ssm_norm_g"], dproj, l)
    dproj, dconv_w, dconv_b = _conv_bwd(sv["xbc"], dxact, wts["conv_w"], p["conv_b"], dproj, l)
    dw_in = _matmul("dw_in", "tn", sv["h1"], dproj, tm=D, tn=1280, tk=S, out_shape=SDS((D, D_IN_PAD), bf16),
                    out_specs=_plain(D, 1280), epilogue=_store_epi(bf16))
    deps = send(l, dict(w_in=_w_in_slabs(dw_in)))
    dx, dg_mix = _matmul(
        "in_proj_dh", "nt", dproj, wts["w_in"], tm=512, tn=D, tk=D_IN_PAD, out_shape=[SDS((S, D), f32), SDS((1, D), f32)],
        out_specs=[_plain(512, D), grow], epilogue=_rms_bwd_epilogue(l),
        extras=(sv["x"], p["mix_norm_g"], dx_mid), extra_specs=(_plain(512, D), gfull, _plain(512, D)), deps=deps)
    small = dict(mix_norm_g=dg_mix, mlp_norm_g=dg_mlp, conv_w=dconv_w, conv_b=dconv_b, ssd=dsm_ssd, attn=dsm_attn, dbias=dbias)
    return dx, small, deps


def _local_step(x, tgt, p, get_weights, send):
    onehot_t = jnp.asarray(_onehot_buckets(), dtype=bf16)
    bias = _bias_build(p["rel_bias"].T, onehot_t).reshape(NQ, BLK, 2 * BLK)
    saved = []
    h = x
    for l in range(DEPTH):
        h, sv = _layer_fwd(l, h, p, get_weights, bias, tgt if l == DEPTH - 1 else None)
        saved.append(sv)
    dx, loss = h
    smalls = [None] * DEPTH
    deps = ()
    for l in reversed(range(DEPTH)):
        dx, smalls[l], deps = _layer_bwd(l, dx, saved[l], p, bias, deps, send)
    drel_t = _bias_grad(smalls[0]["dbias"].reshape(NQ, -1), smalls[1]["dbias"].reshape(NQ, -1), onehot_t)
    return dx, _pack_small_grads(smalls, drel_t, loss)


WEIGHT_ORDER = ("mix_norm_g", "w_in", "q_gain", "k_gain", "sinks", "rel_bias", "conv_w", "conv_b", "dt_bias", "a_log", "d_skip",
                "ssm_norm_g", "w_out", "mlp_norm_g", "w_up", "w_down")


def kernel(x, mix_norm_g, w_in, q_gain, k_gain, sinks, rel_bias, conv_w, conv_b, dt_bias, a_log, d_skip, ssm_norm_g, w_out, mlp_norm_g, w_up, w_down, loss_target, m_mix_norm_g, m_w_in, m_q_gain, m_k_gain, m_sinks, m_rel_bias, m_conv_w, m_conv_b, m_dt_bias, m_a_log, m_d_skip, m_ssm_norm_g, m_w_out, m_mlp_norm_g, m_w_up, m_w_down, v_mix_norm_g, v_w_in, v_q_gain, v_k_gain, v_sinks, v_rel_bias, v_conv_w, v_conv_b, v_dt_bias, v_a_log, v_d_skip, v_ssm_norm_g, v_w_out, v_mlp_norm_g, v_w_up, v_w_down):
    w = dict(mix_norm_g=mix_norm_g, w_in=w_in, q_gain=q_gain, k_gain=k_gain, sinks=sinks, rel_bias=rel_bias, conv_w=conv_w,
             conv_b=conv_b, dt_bias=dt_bias, a_log=a_log, d_skip=d_skip, ssm_norm_g=ssm_norm_g, w_out=w_out,
             mlp_norm_g=mlp_norm_g, w_up=w_up, w_down=w_down)
    m = dict(mix_norm_g=m_mix_norm_g, w_in=m_w_in, q_gain=m_q_gain, k_gain=m_k_gain, sinks=m_sinks, rel_bias=m_rel_bias,
             conv_w=m_conv_w, conv_b=m_conv_b, dt_bias=m_dt_bias, a_log=m_a_log, d_skip=m_d_skip, ssm_norm_g=m_ssm_norm_g,
             w_out=m_w_out, mlp_norm_g=m_mlp_norm_g, w_up=m_w_up, w_down=m_w_down)
    v = dict(mix_norm_g=v_mix_norm_g, w_in=v_w_in, q_gain=v_q_gain, k_gain=v_k_gain, sinks=v_sinks, rel_bias=v_rel_bias,
             conv_w=v_conv_w, conv_b=v_conv_b, dt_bias=v_dt_bias, a_log=v_a_log, d_skip=v_d_skip, ssm_norm_g=v_ssm_norm_g,
             w_out=v_w_out, mlp_norm_g=v_mlp_norm_g, w_up=v_w_up, w_down=v_w_down)
    big = ("w_in", "w_out", "w_up", "w_down")

    my_idx = _dev_index(*_my_place()).astype(jnp.int32).reshape(1)

    fulls = {n: _cast_to_full("cast_" + n, w[n], KIND[n], FULL_SHAPE[n], my_idx, bf16) for n in big}
    conv_full = _cast_to_full("cast_conv_w", conv_w.reshape(1, DEPTH * 4, 128), "stack", (N_DEV, DEPTH * 4, 128), my_idx, f32)[0]
    rest = ["w_out", "w_up", "w_down"]
    g0 = _gather_start("gather0", ["w_in", "conv_w"], [fulls["w_in"][0], conv_full], ())
    g1 = _gather_start("gather1", rest, [fulls[n][0] for n in rest], (g0["token"],))
    g2 = _gather_start("gather2", ["w_in"], [fulls["w_in"][1]], (g1["token"],))
    g3 = _gather_start("gather3", rest, [fulls[n][1] for n in rest], (g2["token"],))
    held = {}
    flat = lambda a: a.reshape(a.shape[0] * a.shape[1], a.shape[2])
    adam_in = {n: (flat(w[n]), flat(m[n]), flat(v[n])) for n in big}

    def get_weights(l, part, after):
        if l == 0 and part == "in":
            full_in, full_conv = _gather_finish("gather0", ["w_in", "conv_w"], g0,
                                                list(after) + [g3["token"], adam_in["w_in"][1], adam_in["w_in"][2]])
            held["conv_w"] = jnp.transpose(full_conv.reshape(N_DEV, DEPTH, 4, 128), (1, 2, 0, 3)).reshape(DEPTH, 4, D_CONV)
            return dict(w_in=_w_in_assemble(full_in), conv_w=held["conv_w"])
        if part == "in":
            return dict(w_in=_w_in_assemble(_gather_finish("gather2", ["w_in"], g2, after)[0]), conv_w=held["conv_w"])
        full = _gather_finish("gather1" if l == 0 else "gather3", rest, g1 if l == 0 else g3, after)
        return {n: f[None] for n, f in zip(rest, full)}

    pending = []

    def send(l, grads):
        names = list(grads)
        started = _exchange_start("exchange%d_%s" % (l, names[0]), names, [grads[n] for n in names], ())
        pending.append((l, names, started))
        return (started["token"],)

    dx, small_part = _local_step(x.reshape(S, D), loss_target.reshape(S, D), w, get_weights, send)

    small = _small_exchange_start(small_part, ())
    tiles = dict(w_in=512, w_out=128, w_up=512, w_down=256)
    outs_of = {n: None for n in big}
    after = [dx, small["token"]]
    for l, names, started in pending:
        bufs = _split_wait("exchange%d_%s_wait" % (l, names[0]), started, after)
        for t, n in enumerate(names):
            outs_of[n] = _adamw_layer("adamw_%s%d" % (n, l), KIND[n], l, *adam_in[n],
                                      bufs[len(names) + t], bufs[t], my_idx, outs_of[n], tiles[n])
        after = [outs_of[names[-1]][0]]
    res = {n: [o.reshape(w[n].shape) for o in outs_of[n]] for n in big}
    small_part, small_land = _split_wait("small_exchange_wait", small, after)
    small_outs = _adamw_small(small_part, small_land, w, m, v)
    loss = small_outs[0][0, 0]
    for k, name in enumerate(SMALL_NAMES):
        res[name] = small_outs[1 + 4 * k:5 + 4 * k]

    result = [loss, dx.reshape(1, S, D)]
    for k in range(4):
        result += [res[name][k] for name in WEIGHT_ORDER]
    return tuple(result)
```

```python
import functools
import math

import numpy as np
import jax
import jax.numpy as jnp
from jax import lax
from jax.experimental import pallas as pl
from jax.experimental.pallas import tpu as pltpu

f32 = jnp.float32
bf16 = jnp.bfloat16
SDS = jax.ShapeDtypeStruct
MESH = pl.DeviceIdType.MESH
HIGHEST = lax.Precision.HIGHEST

S = 2048
D = 1024
DEPTH = 2
BLK = 128
NBLK = S // BLK
HD = 64
NQ = 8
NKV = 2
NSSM = 8
NGRP = 2
NSTATE = 128
D_ATTN = 512
D_SSM = 512
D_CONV = 1024
D_FF = 4096
D_IN = 2312
D_IN_PAD = 2560
COL_QKV, COL_Z, COL_DT, COL_XBC = 0, 768, 1280, 1536
IN_SEGMENTS = ((0, 1280, 0), (1280, 2304, COL_XBC), (2304, 2312, COL_DT))
N_BUCKETS = 32
EPS = 1e-6
N_DEV = 8
VMEM_LIMIT = 48 * 1024 * 1024

ADAM_LR = 0.001
ADAM_B1 = 0.9
ADAM_B2 = 0.999
ADAM_EPS = 1e-08
ADAM_WD = 0.01
ADAM_STEP = 10

NT_DIMS = (((1,), (1,)), ((), ()))
TN_DIMS = (((0,), (0,)), ((), ()))
NN_DIMS = (((1,), (0,)), ((), ()))

ROW_MIXG = 0
ROW_MLPG = 2
ROW_CONVB = 4
ROW_SSMG = 6
ROW_MISC = 8
ROW_RELB = 10
ROW_CONVW = 18
ROW_LOSS = 26
SMALL_ROWS = 32
LANE_QG, LANE_KG, LANE_SINK, LANE_DTB, LANE_ALOG, LANE_DSKIP = 0, 64, 128, 256, 384, 512


def _dot(a, b, dims):
    return lax.dot_general(a, b, dims, preferred_element_type=f32)


def _cparams(n_axes):
    return pltpu.CompilerParams(dimension_semantics=("arbitrary",) * n_axes, vmem_limit_bytes=VMEM_LIMIT)


def _sum11(v):
    return jnp.sum(jnp.sum(v, axis=1, keepdims=True), axis=0, keepdims=True)


def _sigmoid(v):
    return 1.0 / (1.0 + jnp.exp(-v))


ANY_SPEC = pl.BlockSpec(memory_space=pl.ANY)


def _in_hbm(args):
    return [pltpu.with_memory_space_constraint(a, pltpu.HBM) if a.size >= 65536 else a for a in args]


def _out_hbm(out_shape):
    one = lambda s: pltpu.HBM(s.shape, s.dtype) if math.prod(s.shape) >= 65536 else s
    return [one(s) for s in out_shape] if isinstance(out_shape, (list, tuple)) else one(out_shape)


def _matmul(name, mode, a, b, *, layer=0, tm, tn, tk, out_shape, out_specs, epilogue, extras=(), extra_specs=(), deps=(),
            prologue=None):
    extras = tuple(extras) + tuple(deps)
    extra_specs = tuple(extra_specs) + (ANY_SPEC,) * len(deps)
    if mode == "tn":
        t_dim, m_dim = a.shape
        n_dim = b.shape[1]
        grid = (m_dim // tm, n_dim // tn, t_dim // tk)
        a_spec = pl.BlockSpec((tk, tm), lambda i, j, k: (k, i))
        b_spec = pl.BlockSpec((tk, tn), lambda i, j, k: (k, j))
        dims = TN_DIMS
    elif mode == "nn":
        m_dim, k_dim = a.shape
        n_dim = b.shape[-1]
        grid = (m_dim // tm, n_dim // tn, k_dim // tk)
        a_spec = pl.BlockSpec((tm, tk), lambda i, j, k: (i, k))
        b_spec = pl.BlockSpec((None, tk, tn), lambda i, j, k: (layer, k, j))
        dims = NN_DIMS
    else:
        m_dim, k_dim = a.shape
        n_dim = b.shape[-2]
        grid = (m_dim // tm, n_dim // tn, k_dim // tk)
        a_spec = pl.BlockSpec((tm, tk), lambda i, j, k: (i, k))
        b_spec = pl.BlockSpec((None, tn, tk), lambda i, j, k: (layer, j, k))
        dims = NT_DIMS
    nk = grid[2]
    n_ex = len(extras)

    def body(a_ref, b_ref, *rest):
        ex = rest[:n_ex - len(deps)]
        outs = rest[n_ex:-1]
        acc = rest[-1]
        i = pl.program_id(0)
        j = pl.program_id(1)
        k = pl.program_id(2)
        lhs = a_ref[...].astype(bf16) if prologue is None else prologue(a_ref, ex, outs)
        part = _dot(lhs, b_ref[...].astype(bf16), dims)
        if nk == 1:
            epilogue(part, i, j, ex, outs)
        else:
            @pl.when(k == 0)
            def _():
                acc[...] = part

            @pl.when(k > 0)
            def _():
                acc[...] += part

            @pl.when(k == nk - 1)
            def _():
                epilogue(acc[...], i, j, ex, outs)

    return pl.pallas_call(
        body, grid=grid, in_specs=[a_spec, b_spec, *extra_specs], out_specs=out_specs, out_shape=_out_hbm(out_shape),
        scratch_shapes=[pltpu.VMEM((tm, tn) if nk > 1 else (8, 128), f32)], name=name, compiler_params=_cparams(3),
    )(*_in_hbm([a]), b, *_in_hbm(extras))


def _rms_bwd_epilogue(layer):
    def epi(acc, i, j, ex, outs):
        x_ref, g_ref, dres_ref = ex
        dx_ref, dg_ref = outs
        xv = x_ref[...]
        r = lax.rsqrt(jnp.mean(xv * xv, axis=-1, keepdims=True) + EPS)
        xhat = xv * r
        w = acc * g_ref[layer:layer + 1, :]
        dx_ref[...] = dres_ref[...] + r * (w - xhat * jnp.mean(xhat * w, axis=-1, keepdims=True))
        dg = jnp.sum(acc * xhat, axis=0, keepdims=True)

        @pl.when(i == 0)
        def _():
            dg_ref[...] = dg

        @pl.when(i > 0)
        def _():
            dg_ref[...] += dg
    return epi


def _own_slab_spec(kind, tr, cols, nblk):
    if kind == "stack":
        return pl.BlockSpec((None, tr, cols), lambda i, idx: (idx[0], i, 0))
    if kind == "cols512":
        return pl.BlockSpec((tr, cols), lambda i, idx: (i, idx[0]))
    return pl.BlockSpec((tr, cols), lambda i, idx: (idx[0] * nblk + i, 0))


def _cast_to_full(name, w, kind, full_shape, my_idx, dtype):
    n_layers, rows, cols = w.shape
    tr = min(rows, 256)
    nblk = rows // tr

    def body(idx_ref, w_ref, *o_refs):
        for l in range(n_layers):
            o_refs[l][...] = w_ref[l].astype(dtype)

    grid_spec = pltpu.PrefetchScalarGridSpec(
        num_scalar_prefetch=1, grid=(nblk,), in_specs=[pl.BlockSpec((n_layers, tr, cols), lambda i, idx: (0, i, 0))],
        out_specs=[_own_slab_spec(kind, tr, cols, nblk)] * n_layers)
    return pl.pallas_call(body, grid_spec=grid_spec, out_shape=_out_hbm([SDS(full_shape, dtype)] * n_layers), name=name,
                          compiler_params=_cparams(1))(*_in_hbm([my_idx, w]))


def _adamw_math(w, m, v, g):
    m_new = ADAM_B1 * m + (1.0 - ADAM_B1) * g
    v_new = ADAM_B2 * v + (1.0 - ADAM_B2) * (g * g)
    m_hat = m_new / (1.0 - ADAM_B1 ** ADAM_STEP)
    v_hat = v_new / (1.0 - ADAM_B2 ** ADAM_STEP)
    delta = -ADAM_LR * (m_hat / (jnp.sqrt(v_hat) + ADAM_EPS) + ADAM_WD * w)
    return delta, m_new, v_new


def _adamw_layer(name, kind, layer, w, m, v, land, g_full, my_idx, prev, tr):
    rows2, cols = w.shape
    rows = rows2 // DEPTH
    nblk = rows // tr
    own_spec = _own_slab_spec(kind, tr, cols, nblk)
    n_prev = 0 if prev is None else 4

    def body(idx_ref, w_ref, m_ref, v_ref, land_ref, own_ref, *rest):
        g_ref, d_ref, mo_ref, vo_ref = rest[n_prev:]
        me = idx_ref[0]
        g = None
        for p in range(N_DEV):
            part = jnp.where(me == p, own_ref[...], land_ref[p]).astype(f32)
            g = part if g is None else g + part
        delta, m_new, v_new = _adamw_math(w_ref[...], m_ref[...], v_ref[...], g)
        g_ref[...] = g
        d_ref[...] = delta
        mo_ref[...] = m_new
        vo_ref[...] = v_new

    blk = pl.BlockSpec((tr, cols), lambda i, idx: (layer * nblk + i, 0))
    grid_spec = pltpu.PrefetchScalarGridSpec(
        num_scalar_prefetch=1, grid=(nblk,),
        in_specs=[blk, blk, blk, pl.BlockSpec((N_DEV, tr, cols), lambda i, idx: (0, i, 0)), own_spec] + [ANY_SPEC] * n_prev,
        out_specs=[blk, blk, blk, blk])
    aliases = {} if prev is None else {6 + k: k for k in range(4)}
    return pl.pallas_call(
        body, grid_spec=grid_spec, out_shape=_out_hbm([SDS((rows2, cols), f32)] * 4), name=name, input_output_aliases=aliases,
        compiler_params=_cparams(1),
    )(*_in_hbm([my_idx, w, m, v, land, g_full, *([] if prev is None else prev)]))


def _bucket_table():
    qi = np.arange(BLK)[:, None]
    kj = np.arange(2 * BLK)[None, :]
    dist = qi + BLK - kj
    dcl = np.clip(dist, 0, None)
    max_exact = N_BUCKETS // 2
    d_f = np.maximum(dcl, 1).astype(np.float32)
    large = max_exact + (np.log(d_f / np.float32(max_exact)) / np.float32(math.log(128 / max_exact))
                         * np.float32(N_BUCKETS - max_exact)).astype(np.int32)
    large = np.minimum(large, N_BUCKETS - 1)
    bucket = np.where(dcl < max_exact, dcl, large)
    in_window = (dist >= 0) & (dist < BLK)
    return bucket.astype(np.int32), in_window


def _onehot_buckets():
    bucket, _ = _bucket_table()
    oh = (bucket.reshape(-1)[None, :] == np.arange(N_BUCKETS)[:, None]).astype(np.float32)
    return oh


def _bias_build(rel_bias_t, onehot_t):
    def body(r_ref, o_ref, out_ref):
        r = r_ref[...]
        hi = r.astype(bf16)
        r1 = r - hi.astype(f32)
        mid = r1.astype(bf16)
        lo = (r1 - mid.astype(f32)).astype(bf16)
        oh = o_ref[...]
        out_ref[...] = _dot(hi, oh, NN_DIMS) + _dot(mid, oh, NN_DIMS) + _dot(lo, oh, NN_DIMS)

    tn = 4096
    return pl.pallas_call(
        body, grid=(BLK * 2 * BLK // tn,),
        in_specs=[pl.BlockSpec((NQ, N_BUCKETS), lambda i: (0, 0)), pl.BlockSpec((N_BUCKETS, tn), lambda i: (0, i))],
        out_specs=pl.BlockSpec((NQ, tn), lambda i: (0, i)), out_shape=SDS((NQ, BLK * 2 * BLK), f32), name="bias_build",
        compiler_params=_cparams(1),
    )(rel_bias_t, onehot_t)


def _bias_grad(dbias0, dbias1, onehot_t):
    tn = 4096
    nsteps = BLK * 2 * BLK // tn

    def body(a_ref, b_ref, o_ref, out_ref):
        g = a_ref[...] + b_ref[...]
        hi = g.astype(bf16)
        lo = (g - hi.astype(f32)).astype(bf16)
        part = _dot(hi, o_ref[...], NT_DIMS) + _dot(lo, o_ref[...], NT_DIMS)

        @pl.when(pl.program_id(0) == 0)
        def _():
            out_ref[...] = part

        @pl.when(pl.program_id(0) > 0)
        def _():
            out_ref[...] += part

    return pl.pallas_call(
        body, grid=(nsteps,),
        in_specs=[pl.BlockSpec((NQ, tn), lambda i: (0, i)), pl.BlockSpec((NQ, tn), lambda i: (0, i)),
                  pl.BlockSpec((N_BUCKETS, tn), lambda i: (0, i))],
        out_specs=pl.BlockSpec((NQ, N_BUCKETS), lambda i: (0, 0)), out_shape=SDS((NQ, N_BUCKETS), f32), name="bias_grad",
        compiler_params=_cparams(1),
    )(dbias0, dbias1, onehot_t)


def _attn_mask(n):
    qi = lax.broadcasted_iota(jnp.int32, (BLK, 2 * BLK), 0)
    kj = lax.broadcasted_iota(jnp.int32, (BLK, 2 * BLK), 1)
    dist = qi + BLK - kj
    first_key = jnp.where(n > 0, 0, BLK)
    return (dist >= 0) & (dist < BLK) & (kj >= first_key)


def _row_mean(a):
    return jnp.mean(a, axis=-1, keepdims=True)


def _head_norm(t, gain):
    r = lax.rsqrt(_row_mean(t * t) + EPS)
    that = t * r
    return that, r, that * gain


def _softmax_with_sink(s, sink):
    m = jnp.maximum(jnp.max(s, axis=-1, keepdims=True), sink)
    p = jnp.exp(s - m)
    psink = jnp.exp(sink - m)
    inv = 1.0 / (jnp.sum(p, axis=-1, keepdims=True) + psink)
    return p * inv, psink * inv


GQ = NQ // NKV


def _attn_fwd(qkv, q_gain, k_gain, sinks, bias, layer):
    def body(q_ref, kc_ref, kp_ref, vc_ref, vp_ref, qg_ref, kg_ref, sk_ref, bias_ref, o_ref):
        m = pl.program_id(0)
        qg = qg_ref[layer:layer + 1, :]
        kg = kg_ref[layer:layer + 1, :]
        grp = range(NKV)
        chains = [(b, j) for b in range(2) for j in grp]
        masks = [jnp.tile(_attn_mask(2 * m + b), (GQ, 1)) for b in range(2)]
        kblk = [[kp_ref[:, pl.ds(HD * j, HD)].astype(f32), kc_ref[0:BLK, pl.ds(HD * j, HD)].astype(f32),
                 kc_ref[BLK:, pl.ds(HD * j, HD)].astype(f32)] for j in grp]
        vblk = [[vp_ref[:, pl.ds(HD * j, HD)].astype(bf16), vc_ref[0:BLK, pl.ds(HD * j, HD)].astype(bf16),
                 vc_ref[BLK:, pl.ds(HD * j, HD)].astype(bf16)] for j in grp]
        knb = [[_head_norm(kblk[j][t], kg)[2].astype(bf16) for t in range(3)] for j in grp]
        kn_b = {(b, j): jnp.concatenate([knb[j][b], knb[j][b + 1]], axis=0) for b, j in chains}
        vbs = {(b, j): jnp.concatenate([vblk[j][b], vblk[j][b + 1]], axis=0) for b, j in chains}
        rows = {}
        for b, j in chains:
            heads = [GQ * j + g for g in range(GQ)]
            rows[b, j] = (jnp.concatenate([q_ref[pl.ds(BLK * b, BLK), pl.ds(HD * h, HD)] for h in heads], axis=0).astype(f32),
                          jnp.concatenate([jnp.broadcast_to(sk_ref[layer:layer + 1, h:h + 1], (BLK, 1)) for h in heads], axis=0))
        qn_b = {c: _head_norm(rows[c][0], qg)[2].astype(bf16) for c in chains}
        ss = {(b, j): _dot(qn_b[b, j], kn_b[b, j], NT_DIMS) * (HD ** -0.5) + bias_ref[GQ * j:GQ * (j + 1)].reshape(GQ * BLK, 2 * BLK)
              for b, j in chains}
        ps = {(b, j): _softmax_with_sink(jnp.where(masks[b], ss[b, j], -jnp.inf), rows[b, j][1])[0] for b, j in chains}
        outs = {c: _dot(ps[c].astype(bf16), vbs[c], NN_DIMS).astype(bf16) for c in chains}
        for b, j in chains:
            for g in range(GQ):
                o_ref[pl.ds(BLK * b, BLK), pl.ds(HD * (GQ * j + g), HD)] = outs[b, j][BLK * g:BLK * (g + 1), :]

    prev = lambda m: jnp.maximum(2 * m - 1, 0)
    small = lambda shape: pl.BlockSpec(shape, lambda m: (0,) * len(shape))
    return pl.pallas_call(
        body, grid=(NBLK // 2,),
        in_specs=[pl.BlockSpec((2 * BLK, D_ATTN), lambda m: (m, 0)),
                  pl.BlockSpec((2 * BLK, 128), lambda m: (m, 4)), pl.BlockSpec((BLK, 128), lambda m: (prev(m), 4)),
                  pl.BlockSpec((2 * BLK, 128), lambda m: (m, 5)), pl.BlockSpec((BLK, 128), lambda m: (prev(m), 5)),
                  small((DEPTH, HD)), small((DEPTH, HD)), small((DEPTH, NQ)), small((NQ, BLK, 2 * BLK))],
        out_specs=pl.BlockSpec((2 * BLK, D_ATTN), lambda m: (m, 0)), out_shape=_out_hbm(SDS((S, D_ATTN), bf16)),
        name="attn_fwd", compiler_params=_cparams(1),
    )(*_in_hbm([qkv, qkv, qkv, qkv, qkv, q_gain, k_gain, sinks, bias]))


def _attn_bwd(qkv, dmix, q_gain, k_gain, sinks, bias, layer, deps=()):
    def body(q_ref, kc_ref, kp_ref, vc_ref, vp_ref, do_ref, qg_ref, kg_ref, sk_ref, bias_ref, *rest):
        dqkv_ref, dbias_ref, dsm_ref, carry = rest[len(deps):]
        i = pl.program_id(0)
        m = NBLK // 2 - 1 - i
        qg = qg_ref[layer:layer + 1, :]
        kg = kg_ref[layer:layer + 1, :]
        lane = lax.broadcasted_iota(jnp.int32, (1, 128), 1)

        @pl.when(i == 0)
        def _():
            carry[...] = jnp.zeros_like(carry)
            dbias_ref[...] = jnp.zeros_like(dbias_ref)
            dsm_ref[...] = jnp.zeros_like(dsm_ref)

        grp = range(NKV)
        chains = [(b, j) for b in range(2) for j in grp]
        masks = [jnp.tile(_attn_mask(2 * m + b), (GQ, 1)) for b in range(2)]
        kblk = [[kp_ref[:, pl.ds(HD * j, HD)].astype(f32), kc_ref[0:BLK, pl.ds(HD * j, HD)].astype(f32),
                 kc_ref[BLK:, pl.ds(HD * j, HD)].astype(f32)] for j in grp]
        vblk = [[vp_ref[:, pl.ds(HD * j, HD)].astype(bf16), vc_ref[0:BLK, pl.ds(HD * j, HD)].astype(bf16),
                 vc_ref[BLK:, pl.ds(HD * j, HD)].astype(bf16)] for j in grp]
        knorm = [[_head_norm(kblk[j][t], kg) for t in range(3)] for j in grp]
        kn_b = {(b, j): jnp.concatenate([knorm[j][b][2].astype(bf16), knorm[j][b + 1][2].astype(bf16)], axis=0) for b, j in chains}
        vbs = {(b, j): jnp.concatenate([vblk[j][b], vblk[j][b + 1]], axis=0) for b, j in chains}
        rows, do_b = {}, {}
        for b, j in chains:
            heads = [GQ * j + g for g in range(GQ)]
            qrows = pl.ds(BLK * b, BLK)
            rows[b, j] = (jnp.concatenate([q_ref[qrows, pl.ds(HD * h, HD)] for h in heads], axis=0).astype(f32),
                          jnp.concatenate([jnp.broadcast_to(sk_ref[layer:layer + 1, h:h + 1], (BLK, 1)) for h in heads], axis=0))
            do_b[b, j] = jnp.concatenate([do_ref[qrows, pl.ds(HD * h, HD)] for h in heads], axis=0).astype(bf16)
        qnorm = {c: _head_norm(rows[c][0], qg) for c in chains}
        qn_b = {c: qnorm[c][2].astype(bf16) for c in chains}
        ss = {(b, j): _dot(qn_b[b, j], kn_b[b, j], NT_DIMS) * (HD ** -0.5) + bias_ref[GQ * j:GQ * (j + 1)].reshape(GQ * BLK, 2 * BLK)
              for b, j in chains}
        sm = {(b, j): _softmax_with_sink(jnp.where(masks[b], ss[b, j], -jnp.inf), rows[b, j][1]) for b, j in chains}
        dps = {c: _dot(do_b[c], vbs[c], NT_DIMS) for c in chains}
        deltas = {c: jnp.sum(sm[c][0] * dps[c], axis=-1, keepdims=True) for c in chains}
        dss = {c: sm[c][0] * (dps[c] - deltas[c]) for c in chains}
        ds_b = {c: (dss[c] * (HD ** -0.5)).astype(bf16) for c in chains}
        dqn = {c: _dot(ds_b[c], kn_b[c], NN_DIMS) for c in chains}
        dkn = {c: _dot(ds_b[c], qn_b[c], TN_DIMS) for c in chains}
        dvs = {c: _dot(sm[c][0].astype(bf16), do_b[c], TN_DIMS) for c in chains}
        dqg = jnp.zeros((1, HD), f32)
        dkg = jnp.zeros((1, HD), f32)
        dsink = jnp.zeros((1, 128), f32)
        for b, j in chains:
            dbias_ref[GQ * j:GQ * (j + 1)] += dss[b, j].reshape(GQ, BLK, 2 * BLK)
            dsk = sm[b, j][1] * deltas[b, j]
            for g in range(GQ):
                dsink = dsink + jnp.where(lane == GQ * j + g, -_sum11(dsk[BLK * g:BLK * (g + 1), :]), 0.0)
            qhat, rq, _ = qnorm[b, j]
            w = dqn[b, j] * qg
            dq = rq * (w - qhat * _row_mean(qhat * w))
            for g in range(GQ):
                dqkv_ref[pl.ds(BLK * b, BLK), pl.ds(HD * (GQ * j + g), HD)] = dq[BLK * g:BLK * (g + 1), :].astype(bf16)
            dqg = dqg + jnp.sum(dqn[b, j] * qhat, axis=0, keepdims=True)
        for j in grp:
            dkn_t = [dkn[0, j][:BLK, :], dkn[0, j][BLK:, :] + dkn[1, j][:BLK, :], dkn[1, j][BLK:, :]]
            dv_t = [dvs[0, j][:BLK, :], dvs[0, j][BLK:, :] + dvs[1, j][:BLK, :], dvs[1, j][BLK:, :]]
            dk_t = []
            for t in range(3):
                khat, rk, _ = knorm[j][t]
                w = dkn_t[t] * kg
                dk_t.append(rk * (w - khat * _row_mean(khat * w)))
                dkg = dkg + jnp.sum(dkn_t[t] * khat, axis=0, keepdims=True)
            kcols, vcols = pl.ds(D_ATTN + HD * j, HD), pl.ds(D_ATTN + 128 + HD * j, HD)
            dqkv_ref[BLK:, kcols] = (dk_t[2] + carry[:, pl.ds(HD * j, HD)]).astype(bf16)
            dqkv_ref[BLK:, vcols] = (dv_t[2] + carry[:, pl.ds(128 + HD * j, HD)]).astype(bf16)
            dqkv_ref[0:BLK, kcols] = dk_t[1].astype(bf16)
            dqkv_ref[0:BLK, vcols] = dv_t[1].astype(bf16)
            carry[:, pl.ds(HD * j, HD)] = dk_t[0]
            carry[:, pl.ds(128 + HD * j, HD)] = dv_t[0]
        dsm_ref[0:1, 0:HD] += dqg
        dsm_ref[1:2, 0:HD] += dkg
        dsm_ref[2:3, :] += dsink

    rev = lambda i: NBLK // 2 - 1 - i
    prev = lambda i: jnp.maximum(NBLK - 3 - 2 * i, 0)
    small = lambda shape: pl.BlockSpec(shape, lambda i: (0,) * len(shape))
    return pl.pallas_call(
        body, grid=(NBLK // 2,),
        in_specs=[pl.BlockSpec((2 * BLK, D_ATTN), lambda i: (rev(i), 0)),
                  pl.BlockSpec((2 * BLK, 128), lambda i: (rev(i), 4)), pl.BlockSpec((BLK, 128), lambda i: (prev(i), 4)),
                  pl.BlockSpec((2 * BLK, 128), lambda i: (rev(i), 5)), pl.BlockSpec((BLK, 128), lambda i: (prev(i), 5)),
                  pl.BlockSpec((2 * BLK, D_ATTN), lambda i: (rev(i), 0)),
                  small((DEPTH, HD)), small((DEPTH, HD)), small((DEPTH, NQ)), small((NQ, BLK, 2 * BLK))] + [ANY_SPEC] * len(deps),
        out_specs=[pl.BlockSpec((2 * BLK, 768), lambda i: (rev(i), COL_QKV // 768)), small((NQ, BLK, 2 * BLK)), small((8, 128))],
        out_shape=_out_hbm([SDS((S, D_IN_PAD), bf16), SDS((NQ, BLK, 2 * BLK), f32), SDS((8, 128), f32)]),
        scratch_shapes=[pltpu.VMEM((BLK, 256), f32)], name="attn_bwd", compiler_params=_cparams(1),
    )(*_in_hbm([qkv, qkv, qkv, qkv, qkv, dmix, q_gain, k_gain, sinks, bias, *deps]))


CONV_TC = 256


def _shift_down(u, s):
    if s == 0:
        return u
    rows = lax.broadcasted_iota(jnp.int32, u.shape, 0)
    return jnp.where(rows >= s, pltpu.roll(u, s, 0), 0.0)


def _shift_up(u, s):
    if s == 0:
        return u
    rows = lax.broadcasted_iota(jnp.int32, u.shape, 0)
    return jnp.where(rows < u.shape[0] - s, pltpu.roll(u, u.shape[0] - s, 0), 0.0)


def _conv_specs():
    return [pl.BlockSpec((S, CONV_TC), lambda c: (0, c)),
            pl.BlockSpec((None, 4, CONV_TC), lambda c: (0, 0, c)),
            pl.BlockSpec((DEPTH, CONV_TC), lambda c: (0, c))]


def _conv_pre(u, w_ref, b_ref, layer):
    pre = b_ref[layer:layer + 1, :] + w_ref[3:4, :] * u
    for k in range(3):
        pre = pre + w_ref[k:k + 1, :] * _shift_down(u, 3 - k)
    return pre


def _conv_fwd(xbc, conv_w, conv_b, layer):
    def body(u_ref, w_ref, b_ref, o_ref):
        pre = _conv_pre(u_ref[...].astype(f32), w_ref, b_ref, layer)
        o_ref[...] = pre * _sigmoid(pre)

    specs = _conv_specs()
    specs[1] = pl.BlockSpec((None, 4, CONV_TC), lambda c: (layer, 0, c))
    return pl.pallas_call(
        body, grid=(D_CONV // CONV_TC,), in_specs=specs, out_specs=pl.BlockSpec((S, CONV_TC), lambda c: (0, c)),
        out_shape=_out_hbm(SDS((S, D_CONV), f32)), name="conv_fwd", compiler_params=_cparams(1),
    )(*_in_hbm([xbc, conv_w, conv_b]))


def _conv_bwd(xbc, dact, conv_w, conv_b, dproj, layer):
    def body(u_ref, w_ref, b_ref, da_ref, dproj_in, du_ref, dw_ref, db_ref):
        u = u_ref[...].astype(f32)
        pre = _conv_pre(u, w_ref, b_ref, layer)
        sg = _sigmoid(pre)
        dpre = da_ref[...] * (sg * (1.0 + pre * (1.0 - sg)))
        du = w_ref[3:4, :] * dpre
        for k in range(3):
            du = du + w_ref[k:k + 1, :] * _shift_up(dpre, 3 - k)
        du_ref[...] = du.astype(bf16)
        db_ref[...] = jnp.broadcast_to(jnp.sum(dpre, axis=0, keepdims=True), db_ref.shape)
        dw_ref[...] = jnp.zeros_like(dw_ref)
        for k in range(4):
            dw_ref[k:k + 1, :] = jnp.sum(dpre * _shift_down(u, 3 - k), axis=0, keepdims=True)

    specs = _conv_specs()
    specs[1] = pl.BlockSpec((None, 4, CONV_TC), lambda c: (layer, 0, c))
    col = pl.BlockSpec((S, CONV_TC), lambda c: (0, c))
    row8 = pl.BlockSpec((8, CONV_TC), lambda c: (0, c))
    return pl.pallas_call(
        body, grid=(D_CONV // CONV_TC,), in_specs=[*specs, col, ANY_SPEC],
        out_specs=[pl.BlockSpec((S, CONV_TC), lambda c: (0, COL_XBC // CONV_TC + c)), row8, row8],
        out_shape=_out_hbm([SDS((S, D_IN_PAD), bf16), SDS((8, D_CONV), f32), SDS((8, D_CONV), f32)]), name="conv_bwd",
        input_output_aliases={4: 0}, compiler_params=_cparams(1),
    )(*_in_hbm([xbc, conv_w, conv_b, dact, dproj]))


def _tri():
    return (lax.broadcasted_iota(jnp.int32, (BLK, BLK), 0) >= lax.broadcasted_iota(jnp.int32, (BLK, BLK), 1))


def _ssd_scalars(dt_ref, dtb_ref, alog_ref, layer):
    raw = dt_ref[:, 0:NSSM] + dtb_ref[layer:layer + 1, :]
    dtv = jnp.maximum(raw, 0.0) + jnp.log(1.0 + jnp.exp(-jnp.abs(raw)))
    a = -jnp.exp(alog_ref[layer:layer + 1, :])
    acs = jnp.dot(_tri().astype(f32), dtv * a, preferred_element_type=f32, precision=HIGHEST)
    return raw, dtv, a, acs


HG = NSSM // NGRP
GW = HG * HD


def _lane_expand(cols, g):
    lane_head = lax.broadcasted_iota(jnp.int32, (1, GW), 1) // HD
    out = cols[:, HG * g + HG - 1:HG * g + HG]
    for r in range(HG - 2, -1, -1):
        out = jnp.where(lane_head == r, cols[:, HG * g + r:HG * g + r + 1], out)
    return out


def _row_expand(vals, g):
    row_head = lax.broadcasted_iota(jnp.int32, (GW, 1), 0) // HD
    out = vals[:, HG * g + HG - 1:HG * g + HG]
    for r in range(HG - 2, -1, -1):
        out = jnp.where(row_head == r, vals[:, HG * g + r:HG * g + r + 1], out)
    return out


def _head_rowsums(a, g):
    sel = (lax.broadcasted_iota(jnp.int32, (GW, NSSM), 0) // HD + HG * g == lax.broadcasted_iota(jnp.int32, (GW, NSSM), 1)).astype(bf16)
    hi = a.astype(bf16)
    lo = (a - hi.astype(f32)).astype(bf16)
    return _dot(hi, sel, NN_DIMS) + _dot(lo, sel, NN_DIMS)


def _head_blocksums(v, g):
    sel = (lax.broadcasted_iota(jnp.int32, (GW, NSSM), 0) // HD + HG * g == lax.broadcasted_iota(jnp.int32, (GW, NSSM), 1)).astype(bf16)
    hi = v.astype(bf16)
    lo = (v - hi.astype(f32)).astype(bf16)
    return _dot(hi, sel, TN_DIMS) + _dot(lo, sel, TN_DIMS)


def _ssd_chunk_common(xc_ref, dt_ref, dtb_ref, alog_ref, h_rows, layer):
    raw, dtv, a, acs = _ssd_scalars(dt_ref, dtb_ref, alog_ref, layer)
    acs_t = acs.T
    last = acs[BLK - 1:BLK, :]
    c = dict(raw=raw, dtv=dtv, a=a, acs=acs, last=last, dte=jnp.exp(last - acs), e_all=jnp.exp(acs), cd=jnp.exp(last))
    grp, heads, tri = range(NGRP), range(NSSM), _tri()
    c["bm"] = [xc_ref[:, pl.ds(D_SSM + NSTATE * g, NSTATE)] for g in grp]
    c["bm_b"] = [c["bm"][g].astype(bf16) for g in grp]
    c["cm_b"] = [xc_ref[:, pl.ds(D_SSM + NGRP * NSTATE + NSTATE * g, NSTATE)].astype(bf16) for g in grp]
    c["cb"] = [_dot(c["cm_b"][g], c["bm_b"][g], NT_DIMS) for g in grp]
    c["x"] = [xc_ref[:, pl.ds(GW * g, GW)] for g in grp]
    c["dt"] = [_lane_expand(dtv, g) for g in grp]
    c["xdt"] = [c["x"][g] * c["dt"][g] for g in grp]
    c["xdt_b"] = [c["xdt"][g].astype(bf16) for g in grp]
    c["prev"] = [h_rows(g) for g in grp]
    c["prev_b"] = [c["prev"][g].astype(bf16) for g in grp]
    c["e"] = [_lane_expand(c["e_all"], g) for g in grp]
    c["y_off"] = [_dot(c["cm_b"][g], c["prev_b"][g], NT_DIMS) * c["e"][g] for g in grp]
    c["decay"] = [jnp.exp(jnp.where(tri, acs[:, h:h + 1] - acs_t[h:h + 1, :], -jnp.inf)) for h in heads]
    c["m"] = [c["cb"][h // HG] * c["decay"][h] for h in heads]
    c["m_b"] = [c["m"][h].astype(bf16) for h in heads]
    c["dte_x"] = [_lane_expand(c["dte"], g) for g in grp]
    c["xdte_b"] = [(c["xdt"][g] * c["dte_x"][g]).astype(bf16) for g in grp]
    return c


def _ssd_fwd(xact, z, dt, attn, dt_bias, a_log, d_skip, norm_g, layer):
    def body(xc_ref, z_ref, dt_ref, at_ref, dtb_ref, alog_ref, dsk_ref, ng_ref, mix_ref, hs_ref, y_ref, h_ref):
        n = pl.program_id(0)

        @pl.when(n == 0)
        def _():
            h_ref[...] = jnp.zeros_like(h_ref)

        hs_ref[...] = h_ref[...]
        c = _ssd_chunk_common(xc_ref, dt_ref, dtb_ref, alog_ref, lambda g: h_ref[pl.ds(GW * g, GW), :], layer)
        grp, heads = range(NGRP), range(NSSM)
        y_diag = [_dot(c["m_b"][h], c["xdt_b"][h // HG][:, HD * (h % HG):HD * (h % HG + 1)], NN_DIMS) for h in heads]
        new_st = [_dot(c["xdte_b"][g], c["bm_b"][g], TN_DIMS) for g in grp]
        for h in heads:
            y_ref[:, pl.ds(HD * h, HD)] = y_diag[h]
        dskip = dsk_ref[layer:layer + 1, :]
        for g in grp:
            cols = pl.ds(GW * g, GW)
            y_ref[:, cols] = y_ref[:, cols] + c["y_off"][g] + c["x"][g] * _lane_expand(dskip, g)
            h_ref[cols, :] = c["prev"][g] * _row_expand(c["cd"], g) + new_st[g]
        zv = z_ref[...].astype(f32)
        yz = y_ref[...] * (zv * _sigmoid(zv))
        mix_ref[:, 0:D_ATTN] = at_ref[...]
        for g in grp:
            yg = yz[:, GW * g:GW * (g + 1)]
            rs = lax.rsqrt(jnp.mean(yg * yg, axis=-1, keepdims=True) + EPS)
            mix_ref[:, D_ATTN + GW * g:D_ATTN + GW * (g + 1)] = (yg * rs * ng_ref[layer:layer + 1, GW * g:GW * (g + 1)]).astype(bf16)

    small = lambda shape: pl.BlockSpec(shape, lambda n: (0,) * len(shape))
    return pl.pallas_call(
        body, grid=(NBLK,),
        in_specs=[pl.BlockSpec((BLK, D_CONV), lambda n: (n, 0)), pl.BlockSpec((BLK, D_SSM), lambda n: (n, 0)),
                  pl.BlockSpec((BLK, 128), lambda n: (n, 0)), pl.BlockSpec((BLK, D_ATTN), lambda n: (n, 0)),
                  small((DEPTH, NSSM)), small((DEPTH, NSSM)), small((DEPTH, NSSM)), small((DEPTH, D_SSM))],
        out_specs=[pl.BlockSpec((BLK, D), lambda n: (n, 0)), pl.BlockSpec((None, NSSM * HD, NSTATE), lambda n: (n, 0, 0)),
                   pl.BlockSpec((BLK, D_SSM), lambda n: (n, 0))],
        out_shape=_out_hbm([SDS((S, D), bf16), SDS((NBLK, NSSM * HD, NSTATE), f32), SDS((S, D_SSM), f32)]),
        scratch_shapes=[pltpu.VMEM((NSSM * HD, NSTATE), f32)],
        name="ssd_fwd", compiler_params=_cparams(1),
    )(*_in_hbm([xact, z, dt, attn, dt_bias, a_log, d_skip, norm_g]))


def _ssd_bwd(xact, z, dt, dmix, hs, y, dt_bias, a_log, d_skip, norm_g, dproj, layer):
    def body(xc_ref, z_ref, dt_ref, do_ref, hs_ref, y_ref, dtb_ref, alog_ref, dsk_ref, ng_ref, dproj_in,
             dzdt_ref, dx_ref, dsm_ref, dh_ref, dy_ref):
        i = pl.program_id(0)

        @pl.when(i == 0)
        def _():
            dh_ref[...] = jnp.zeros_like(dh_ref)
            dsm_ref[...] = jnp.zeros_like(dsm_ref)

        c = _ssd_chunk_common(xc_ref, dt_ref, dtb_ref, alog_ref, lambda g: hs_ref[pl.ds(GW * g, GW), :], layer)
        raw, dtv, a = c["raw"], c["dtv"], c["a"]
        grp, heads = range(NGRP), range(NSSM)
        dskip = dsk_ref[layer:layer + 1, :]
        lane8 = lax.broadcasted_iota(jnp.int32, (1, NSSM), 1)
        sub8 = lax.broadcasted_iota(jnp.int32, (NSSM, 1), 0)

        zv = z_ref[...].astype(f32)
        sz = _sigmoid(zv)
        gz = zv * sz
        yv = y_ref[...]
        yz = yv * gz
        for g in grp:
            sl = slice(GW * g, GW * (g + 1))
            yg = yz[:, sl]
            rs = lax.rsqrt(jnp.mean(yg * yg, axis=-1, keepdims=True) + EPS)
            yhat = yg * rs
            dog = do_ref[:, sl]
            w = dog * ng_ref[layer:layer + 1, sl]
            dyz = rs * (w - yhat * jnp.mean(yhat * w, axis=-1, keepdims=True))
            dsm_ref[0:1, sl] += jnp.sum(dog * yhat, axis=0, keepdims=True)
            dy_ref[:, sl] = dyz * gz[:, sl]
            dzdt_ref[:, sl] = (dyz * yv[:, sl] * (sz[:, sl] * (1.0 + zv[:, sl] * (1.0 - sz[:, sl])))).astype(bf16)

        dy = [dy_ref[:, pl.ds(GW * g, GW)] for g in grp]
        dy_b = [dy[g].astype(bf16) for g in grp]
        hl = lambda h: slice(HD * (h % HG), HD * (h % HG + 1))
        dt_off_b = [(dy[g] * c["e"][g]).astype(bf16) for g in grp]
        dcm = [_dot(dt_off_b[g], c["prev_b"][g], NN_DIMS) for g in grp]
        dprev = [_dot(dt_off_b[g], c["cm_b"][g], TN_DIMS) for g in grp]
        yoff_rs = [_head_rowsums(dy[g] * c["y_off"][g], g) for g in grp]
        dhn = [dh_ref[pl.ds(GW * g, GW), :] for g in grp]
        dhn_b = [dhn[g].astype(bf16) for g in grp]
        dprev = [dprev[g] + dhn[g] * _row_expand(c["cd"], g) for g in grp]
        dhn_prev = [dhn[g] * c["prev"][g] for g in grp]
        u = [_dot(c["bm_b"][g], dhn_b[g], NT_DIMS) for g in grp]
        dbm = [_dot(c["xdte_b"][g], dhn_b[g], NN_DIMS) for g in grp]
        ddte_rs = [_head_rowsums(c["xdt"][g] * u[g], g) for g in grp]
        dm = [_dot(dy_b[h // HG][:, hl(h)], c["xdt_b"][h // HG][:, hl(h)], NT_DIMS) for h in heads]
        dxdt_in = [_dot(c["m_b"][h], dy_b[h // HG][:, hl(h)], TN_DIMS) for h in heads]
        dseg = [dm[h] * c["m"][h] for h in heads]
        dmd = [dm[h] * c["decay"][h] for h in heads]
        for h in heads:
            dx_ref[:, pl.ds(HD * h, HD)] = dxdt_in[h]

        tmp = (ddte_rs[0] + ddte_rs[1]) * c["dte"]
        dacs = yoff_rs[0] + yoff_rs[1] - tmp
        dacs_cols = jnp.zeros((NSSM, BLK), f32)
        ddtv = jnp.zeros((BLK, NSSM), f32)
        ddsk = jnp.zeros((BLK, NSSM), f32)
        hp = jnp.zeros((1, NSSM), f32)
        for g in grp:
            cols = pl.ds(GW * g, GW)
            dxdt = dx_ref[:, cols] + u[g] * c["dte_x"][g]
            dx_ref[:, cols] = dy[g] * _lane_expand(dskip, g) + dxdt * c["dt"][g]
            ddtv = ddtv + _head_rowsums(dxdt * c["x"][g], g)
            ddsk = ddsk + _head_rowsums(dy[g] * c["x"][g], g)
            dcb = dmd[HG * g]
            for r in range(1, HG):
                dcb = dcb + dmd[HG * g + r]
            dcb_b = dcb.astype(bf16)
            dx_ref[:, pl.ds(D_SSM + NSTATE * g, NSTATE)] = dbm[g] + _dot(dcb_b, c["cm_b"][g], TN_DIMS)
            dx_ref[:, pl.ds(D_SSM + NGRP * NSTATE + NSTATE * g, NSTATE)] = dcm[g] + _dot(dcb_b, c["bm_b"][g], NN_DIMS)
            dh_ref[cols, :] = dprev[g]
            hp = hp + _head_blocksums(jnp.sum(dhn_prev[g], axis=1, keepdims=True), g)
            for r in range(HG):
                h = HG * g + r
                dacs = dacs + (lane8 == h).astype(f32) * jnp.sum(dseg[h], axis=1, keepdims=True)
                dacs_cols = dacs_cols + (sub8 == h).astype(f32) * jnp.sum(dseg[h], axis=0, keepdims=True)
        dlast = hp * c["cd"] + jnp.sum(tmp, axis=0, keepdims=True)
        ddsk = jnp.sum(ddsk, axis=0, keepdims=True)

        row = lax.broadcasted_iota(jnp.int32, (BLK, 1), 0)
        dacs = dacs - dacs_cols.T + jnp.where(row == BLK - 1, dlast, 0.0)
        dda = lax.dot_general(_tri().astype(f32), dacs, TN_DIMS, preferred_element_type=f32, precision=HIGHEST)
        ddtv = ddtv + dda * a
        da = jnp.sum(dda * dtv, axis=0, keepdims=True)
        draw = ddtv * _sigmoid(raw)
        dzdt_ref[:, D_SSM:] = jnp.zeros((BLK, COL_XBC - COL_DT), bf16)
        dzdt_ref[:, D_SSM:D_SSM + NSSM] = draw.astype(bf16)
        dsm_ref[1:2, 0:NSSM] += jnp.sum(draw, axis=0, keepdims=True)
        dsm_ref[2:3, 0:NSSM] += da * a
        dsm_ref[3:4, 0:NSSM] += ddsk

    rev = lambda i: NBLK - 1 - i
    small = lambda shape: pl.BlockSpec(shape, lambda i: (0,) * len(shape))
    return pl.pallas_call(
        body, grid=(NBLK,),
        in_specs=[pl.BlockSpec((BLK, D_CONV), lambda i: (rev(i), 0)), pl.BlockSpec((BLK, D_SSM), lambda i: (rev(i), 0)),
                  pl.BlockSpec((BLK, 128), lambda i: (rev(i), 0)), pl.BlockSpec((BLK, D_SSM), lambda i: (rev(i), 1)),
                  pl.BlockSpec((None, NSSM * HD, NSTATE), lambda i: (rev(i), 0, 0)), pl.BlockSpec((BLK, D_SSM), lambda i: (rev(i), 0)),
                  small((DEPTH, NSSM)), small((DEPTH, NSSM)), small((DEPTH, NSSM)), small((DEPTH, D_SSM)), ANY_SPEC],
        out_specs=[pl.BlockSpec((BLK, COL_XBC - COL_Z), lambda i: (rev(i), COL_Z // (COL_XBC - COL_Z))),
                   pl.BlockSpec((BLK, D_CONV), lambda i: (rev(i), 0)), small((8, D_SSM))],
        out_shape=_out_hbm([SDS((S, D_IN_PAD), bf16), SDS((S, D_CONV), f32), SDS((8, D_SSM), f32)]),
        scratch_shapes=[pltpu.VMEM((NSSM * HD, NSTATE), f32), pltpu.VMEM((BLK, D_SSM), f32)],
        name="ssd_bwd", input_output_aliases={10: 0}, compiler_params=_cparams(1),
    )(*_in_hbm([xact, z, dt, dmix, hs, y, dt_bias, a_log, d_skip, norm_g, dproj]))


def _my_place():
    return lax.axis_index("x"), lax.axis_index("y"), lax.axis_index("c")


def _dev_index(px, py, pc):
    return 4 * px + 2 * py + pc


def _slab2(kind, ref, idx):
    if kind == "stack":
        return ref.at[idx]
    if kind == "rows128":
        return ref.at[pl.ds(pl.multiple_of(idx * 128, 128), 128), :]
    if kind == "rows512":
        return ref.at[pl.ds(pl.multiple_of(idx * 512, 512), 512), :]
    return ref.at[:, pl.ds(pl.multiple_of(idx * 512, 512), 512)]


def _slab_shape(kind, full_shape):
    if kind == "stack":
        return tuple(full_shape[1:])
    if kind == "rows128":
        return (128, full_shape[1])
    if kind == "rows512":
        return (512, full_shape[1])
    return (full_shape[0], 512)


KIND = dict(w_in="stack", w_out="rows128", w_up="cols512", w_down="rows512", conv_w="stack")
FULL_SHAPE = dict(w_in=(N_DEV, D, D_IN // N_DEV), w_out=(D, D), w_up=(D, D_FF), w_down=(D_FF, D))
HBM_SPEC = pl.BlockSpec(memory_space=pltpu.HBM)
SEM_SPEC = pl.BlockSpec(memory_space=pltpu.SEMAPHORE)
SIDE_EFFECT = pltpu.SideEffectType.DATAFLOW_SIDE_EFFECTING


def _peers_all():
    x, y, c = _my_place()
    return [(x ^ ((r >> 2) & 1), y ^ ((r >> 1) & 1), c ^ (r & 1)) for r in range(1, N_DEV)]


def _split_start(name, bufs, n_copies, plan, deps=()):
    nb = len(bufs)

    def body(*refs):
        ins = refs[:nb]
        send_sems, recv_sems = refs[nb + len(deps)], refs[nb + len(deps) + 1]
        token = refs[-1]
        for i, (src, dst, dev) in enumerate(plan(ins)):
            pltpu.make_async_remote_copy(src_ref=src, dst_ref=dst, send_sem=send_sems.at[i], recv_sem=recv_sems.at[i],
                                         device_id=dev, device_id_type=MESH).start()
        token[...] = jnp.zeros_like(token)

    outs = pl.pallas_call(
        body, name=name,
        out_shape=(pltpu.SemaphoreType.DMA((n_copies,)), pltpu.SemaphoreType.DMA((n_copies,)),
                   *[pltpu.HBM(b.shape, b.dtype) for b in bufs], SDS((8, 128), f32)),
        in_specs=[HBM_SPEC] * nb + [ANY_SPEC] * len(deps),
        out_specs=(SEM_SPEC, SEM_SPEC, *[HBM_SPEC] * nb, pl.BlockSpec(memory_space=pltpu.VMEM)),
        input_output_aliases={i: 2 + i for i in range(nb)},
        compiler_params=pltpu.CompilerParams(has_side_effects=SIDE_EFFECT),
    )(*[pltpu.with_memory_space_constraint(b, pltpu.HBM) for b in bufs], *deps)
    return dict(send=outs[0], recv=outs[1], bufs=list(outs[2:2 + nb]), token=outs[-1], plan=plan, n=n_copies)


def _split_wait(name, started, after):
    bufs = started["bufs"]
    nb = len(bufs)
    plan = started["plan"]

    def body(*refs):
        ins = refs[:nb]
        send_sems, recv_sems = refs[nb], refs[nb + 1]
        for i, (src, dst, dev) in enumerate(plan(ins)):
            cp = pltpu.make_async_remote_copy(src_ref=src, dst_ref=dst, send_sem=send_sems.at[i], recv_sem=recv_sems.at[i],
                                              device_id=dev, device_id_type=MESH)
            cp.wait_send()
            cp.wait_recv()

    outs = pl.pallas_call(
        body, name=name, out_shape=tuple(pltpu.HBM(b.shape, b.dtype) for b in bufs),
        in_specs=[HBM_SPEC] * nb + [SEM_SPEC, SEM_SPEC] + [ANY_SPEC] * len(after), out_specs=(HBM_SPEC,) * nb,
        input_output_aliases={i: i for i in range(nb)},
        compiler_params=pltpu.CompilerParams(has_side_effects=SIDE_EFFECT),
    )(*bufs, started["send"], started["recv"], *after)
    return list(outs)


def _gather_start(name, names, fulls, deps):
    n_t = len(names)

    def plan(refs):
        x, y, c = _my_place()
        my_idx = _dev_index(x, y, c)
        targets = [(x, y, 1 - c), (1 - x, y, c), (x, 1 - y, c), (1 - x, 1 - y, c)]
        slabs = [_slab2(KIND[names[t]], refs[t], my_idx) for t in range(n_t)]
        return [(slabs[t], slabs[t], dev) for t in range(n_t) for dev in targets]

    return _split_start(name, list(fulls), 4 * n_t, plan, deps)


def _gather_finish(name, names, started, after):
    n_t = len(names)
    fulls = _split_wait(name + "_wait", started, after)
    slab_shapes = [SDS(_slab_shape(KIND[n], f.shape), f.dtype) for n, f in zip(names, fulls)]

    def body(*refs):
        ins = refs[:n_t]
        outs = refs[n_t:2 * n_t]
        stage = refs[2 * n_t:3 * n_t]
        load_sems, send_sems, recv_sems = refs[3 * n_t:]
        x, y, c = _my_place()
        chips = [(1 - x, y), (x, 1 - y), (1 - x, 1 - y)]
        pairs = [(t, j) for t in range(n_t) for j in range(3)]
        loads = [pltpu.make_async_copy(_slab2(KIND[names[t]], ins[t], _dev_index(*chips[j], c)), stage[t].at[j], load_sems.at[t, j])
                 for t, j in pairs]
        for cp in loads:
            cp.start()

        def copy(t, j, core):
            return pltpu.make_async_remote_copy(
                src_ref=stage[t].at[j], dst_ref=_slab2(KIND[names[t]], outs[t], _dev_index(*chips[j], core)),
                send_sem=send_sems.at[t, j], recv_sem=recv_sems.at[t, j], device_id=(x, y, 1 - c), device_id_type=MESH)

        sends = [copy(t, j, c) for t, j in pairs]
        for ld, cp in zip(loads, sends):
            ld.wait()
            cp.start()
        for t, j in pairs:
            copy(t, j, 1 - c).wait_recv()
        for cp in sends:
            cp.wait_send()

    return pl.pallas_call(
        body, in_specs=[HBM_SPEC] * n_t, out_specs=[HBM_SPEC] * n_t, out_shape=[pltpu.HBM(b.shape, b.dtype) for b in fulls],
        input_output_aliases={t: t for t in range(n_t)},
        scratch_shapes=[pltpu.VMEM((3,) + s.shape, s.dtype) for s in slab_shapes]
        + [pltpu.SemaphoreType.DMA((n_t, 3)), pltpu.SemaphoreType.DMA((n_t, 3)), pltpu.SemaphoreType.DMA((n_t, 3))],
        name=name + "_pass", compiler_params=pltpu.CompilerParams(vmem_limit_bytes=VMEM_LIMIT),
    )(*fulls)


def _exchange_start(name, names, grads, deps):
    n_t = len(names)
    lands = [lax.empty((N_DEV,) + _slab_shape(KIND[n], g.shape), g.dtype) for n, g in zip(names, grads)]

    def plan(refs):
        my_idx = _dev_index(*_my_place())
        return [(_slab2(KIND[names[t]], refs[t], _dev_index(*peer)), refs[n_t + t].at[my_idx], peer)
                for t in range(n_t) for peer in _peers_all()]

    return _split_start(name, list(grads) + lands, 7 * n_t, plan, deps)


def _small_exchange_start(part, deps):
    land = lax.empty((N_DEV,) + part.shape, part.dtype)

    def plan(refs):
        my_idx = _dev_index(*_my_place())
        return [(refs[0], refs[1].at[my_idx], peer) for peer in _peers_all()]

    return _split_start("small_exchange", [part, land], N_DEV - 1, plan, deps)


def _slab_pieces():
    sh = D_IN // N_DEV
    out = []
    for j in range(N_DEV):
        for first, end, dst in IN_SEGMENTS:
            lo, hi = max(first, sh * j), min(end, sh * (j + 1))
            if lo < hi:
                out.append((j, lo - sh * j, hi - sh * j, dst + lo - first))
    return out


def _w_in_assemble(stacked):
    tr = 256
    sh = D_IN // N_DEV

    def body(i_ref, o_ref):
        o_ref[:, COL_DT:COL_XBC] = jnp.zeros((tr, COL_XBC - COL_DT), bf16)
        for j, lo, hi, dst in _slab_pieces():
            o_ref[:, dst:dst + hi - lo] = i_ref[j, :, lo:hi]

    return pl.pallas_call(
        body, grid=(D // tr,), in_specs=[pl.BlockSpec((N_DEV, tr, sh), lambda i: (0, i, 0))],
        out_specs=pl.BlockSpec((None, tr, D_IN_PAD), lambda i: (0, i, 0)), out_shape=SDS((1, D, D_IN_PAD), bf16),
        name="w_in_assemble", compiler_params=_cparams(1),
    )(*_in_hbm([stacked]))


def _w_in_slabs(dw_in):
    tr = 256
    sh = D_IN // N_DEV

    def body(i_ref, o_ref):
        for j, lo, hi, src in _slab_pieces():
            o_ref[j, :, lo:hi] = i_ref[:, src:src + hi - lo]

    return pl.pallas_call(
        body, grid=(D // tr,), in_specs=[pl.BlockSpec((tr, D_IN_PAD), lambda i: (i, 0))],
        out_specs=pl.BlockSpec((N_DEV, tr, sh), lambda i: (0, i, 0)), out_shape=_out_hbm(SDS((N_DEV, D, sh), bf16)),
        name="w_in_slabs", compiler_params=_cparams(1),
    )(*_in_hbm([dw_in]))


SMALL_NAMES = ("mix_norm_g", "mlp_norm_g", "conv_b", "ssm_norm_g", "q_gain", "k_gain", "sinks", "dt_bias", "a_log", "d_skip",
               "rel_bias", "conv_w")
MISC_LANES = dict(q_gain=(LANE_QG, HD), k_gain=(LANE_KG, HD), sinks=(LANE_SINK, NQ), dt_bias=(LANE_DTB, NSSM),
                  a_log=(LANE_ALOG, NSSM), d_skip=(LANE_DSKIP, NSSM))


def _pack_small_grads(smalls, drel_t, loss):
    def body(*refs):
        o_ref = refs[-1]
        drel_ref, loss_ref = refs[-3], refs[-2]
        o_ref[...] = jnp.zeros_like(o_ref)
        for l in range(DEPTH):
            mixg, mlpg, convb, convw, ssd, attn = refs[6 * l:6 * l + 6]
            o_ref[ROW_MIXG + l:ROW_MIXG + l + 1, :] = mixg[...]
            o_ref[ROW_MLPG + l:ROW_MLPG + l + 1, :] = mlpg[...]
            o_ref[ROW_CONVB + l:ROW_CONVB + l + 1, :] = convb[0:1, :]
            o_ref[ROW_SSMG + l:ROW_SSMG + l + 1, 0:D_SSM] = ssd[0:1, :]
            o_ref[ROW_CONVW + 4 * l:ROW_CONVW + 4 * l + 4, :] = convw[0:4, :]
            row = slice(ROW_MISC + l, ROW_MISC + l + 1)
            o_ref[row, LANE_QG:LANE_QG + HD] = attn[0:1, 0:HD]
            o_ref[row, LANE_KG:LANE_KG + HD] = attn[1:2, 0:HD]
            o_ref[row, LANE_SINK:LANE_SINK + NQ] = attn[2:3, 0:NQ]
            o_ref[row, LANE_DTB:LANE_DTB + NSSM] = ssd[1:2, 0:NSSM]
            o_ref[row, LANE_ALOG:LANE_ALOG + NSSM] = ssd[2:3, 0:NSSM]
            o_ref[row, LANE_DSKIP:LANE_DSKIP + NSSM] = ssd[3:4, 0:NSSM]
        o_ref[ROW_RELB:ROW_RELB + NQ, 0:N_BUCKETS] = drel_ref[...]
        o_ref[ROW_LOSS:ROW_LOSS + 1, 0:1] = loss_ref[0:1, 0:1]

    args = []
    for sm in smalls:
        args += [sm["mix_norm_g"], sm["mlp_norm_g"], sm["conv_b"], sm["conv_w"], sm["ssd"], sm["attn"]]
    args += [drel_t, loss]
    return pl.pallas_call(body, out_shape=SDS((SMALL_ROWS, D), f32), name="pack_small_grads")(*args)


def _adamw_small(part, land, w, m, v):
    n = len(SMALL_NAMES)

    def grad_of(name, g_ref):
        if name == "mix_norm_g":
            return g_ref[ROW_MIXG:ROW_MIXG + DEPTH, :]
        if name == "mlp_norm_g":
            return g_ref[ROW_MLPG:ROW_MLPG + DEPTH, :]
        if name == "conv_b":
            return g_ref[ROW_CONVB:ROW_CONVB + DEPTH, :]
        if name == "ssm_norm_g":
            return g_ref[ROW_SSMG:ROW_SSMG + DEPTH, 0:D_SSM]
        if name == "rel_bias":
            return g_ref[ROW_RELB:ROW_RELB + NQ, 0:N_BUCKETS].T
        lane, width = MISC_LANES[name]
        return g_ref[ROW_MISC:ROW_MISC + DEPTH, lane:lane + width]

    def body(part_ref, land_ref, *refs):
        ws, ms, vs = refs[:n], refs[n:2 * n], refs[2 * n:3 * n]
        loss_ref = refs[3 * n]
        outs = refs[3 * n + 1:-1]
        g_ref = refs[-1]
        me = _dev_index(*_my_place())
        for p in range(N_DEV):
            term = jnp.where(me == p, part_ref[...], land_ref[p])
            if p == 0:
                g_ref[...] = term
            else:
                g_ref[...] += term
        loss_ref[...] = g_ref[ROW_LOSS:ROW_LOSS + 1, 0:128]
        my_cols = pl.ds(pl.multiple_of(me * 128, 128), 128)
        for k, name in enumerate(SMALL_NAMES):
            g_out, d_out, m_out, v_out = outs[4 * k:4 * k + 4]
            if name == "conv_w":
                for l in range(DEPTH):
                    g = g_ref[ROW_CONVW + 4 * l:ROW_CONVW + 4 * l + 4, my_cols]
                    delta, m_new, v_new = _adamw_math(ws[k][l], ms[k][l], vs[k][l], g)
                    g_out[l], d_out[l], m_out[l], v_out[l] = g, delta, m_new, v_new
            else:
                g = grad_of(name, g_ref)
                delta, m_new, v_new = _adamw_math(ws[k][...], ms[k][...], vs[k][...], g)
                g_out[...], d_out[...], m_out[...], v_out[...] = g, delta, m_new, v_new

    ws = [w[name] for name in SMALL_NAMES]
    out_shape = [SDS((1, 128), f32)]
    for a in ws:
        out_shape += [SDS(a.shape, f32)] * 4
    return pl.pallas_call(body, out_shape=out_shape, name="adamw_small", scratch_shapes=[pltpu.VMEM((SMALL_ROWS, D), f32)])(
        part, land, *ws, *[m[name] for name in SMALL_NAMES], *[v[name] for name in SMALL_NAMES])


def _plain(tm, tn):
    return pl.BlockSpec((tm, tn), lambda i, j, k: (i, j))


def _rowblk(tm, width):
    return pl.BlockSpec((tm, width), lambda i, j, k: (i, 0))


def _store_epi(dtype):
    def epi(acc, i, j, ex, outs):
        outs[0][...] = acc.astype(dtype)
    return epi


def _rms_prologue(layer):
    def pro(a_ref, ex, outs):
        xv = a_ref[...]
        r = lax.rsqrt(jnp.mean(xv * xv, axis=-1, keepdims=True) + EPS)
        h = (xv * r * ex[0][layer:layer + 1, :]).astype(bf16)
        outs[-1][...] = h
        return h
    return pro


MLP_TM = 256
MLP_VMEM = 56 * 1024 * 1024


def _resident(shape):
    return pl.BlockSpec((None,) + shape, lambda i: (0, 0, 0), pipeline_mode=pl.Buffered(1))


def _mlp_fwd(layer, x, mix, g, w_out, w_up, w_down, tgt=None):
    tm = MLP_TM
    with_loss = tgt is not None

    def body(x_ref, mix_ref, g_ref, wo_ref, wu_ref, wd_ref, *rest):
        xm_ref, a_ref, r_ref, h_ref = rest[with_loss:with_loss + 4]
        rest = rest[:with_loss] + rest[with_loss + 1:]
        i = pl.program_id(0)
        xv = x_ref[...] + _dot(mix_ref[...], wo_ref[...], NN_DIMS)
        xm_ref[...] = xv
        h = (xv * lax.rsqrt(jnp.mean(xv * xv, axis=-1, keepdims=True) + EPS) * g_ref[layer:layer + 1, :]).astype(bf16)
        h_ref[...] = h
        r = jnp.maximum(_dot(h, wu_ref[...], NN_DIMS), 0.0)
        a = (r * r).astype(bf16)
        a_ref[...] = a
        r_ref[...] = r.astype(bf16)
        y = xv + _dot(a, wd_ref[...], NN_DIMS)
        if not with_loss:
            rest[3][...] = y
            return
        err = y - rest[0][...]
        rest[4][...] = err * (1.0 / D)
        part = 0.5 * jnp.sum(jnp.mean(err * err, axis=-1, keepdims=True), axis=0, keepdims=True)

        @pl.when(i == 0)
        def _():
            rest[5][...] = jnp.zeros_like(rest[5])

        rest[5][...] += jnp.broadcast_to(part, rest[5].shape)

    row = lambda width: pl.BlockSpec((tm, width), lambda i: (i, 0))
    in_specs = [row(D), row(D), pl.BlockSpec((DEPTH, D), lambda i: (0, 0)), _resident((D, D)), _resident((D, D_FF)),
                _resident((D_FF, D))]
    out_specs = [row(D), row(D_FF), row(D_FF), row(D), row(D)]
    out_shape = [SDS((S, D), f32), SDS((S, D_FF), bf16), SDS((S, D_FF), bf16), SDS((S, D), bf16), SDS((S, D), f32)]
    args = [x, mix, g, w_out, w_up, w_down]
    if with_loss:
        in_specs.append(row(D))
        args.append(tgt)
        out_specs.append(pl.BlockSpec((1, 128), lambda i: (0, 0)))
        out_shape.append(SDS((1, 128), f32))
    return pl.pallas_call(
        body, grid=(S // tm,), in_specs=in_specs, out_specs=out_specs, out_shape=_out_hbm(out_shape),
        name="mlp_fwd_loss" if with_loss else "mlp_fwd",
        compiler_params=pltpu.CompilerParams(dimension_semantics=("arbitrary",), vmem_limit_bytes=MLP_VMEM),
    )(*_in_hbm(args[:3]), *args[3:6], *_in_hbm(args[6:]))


def _mlp_bwd_act(layer, dx_out, r_act, x_mid, g, w_down, w_up, w_out, deps):
    tm = MLP_TM

    def body(dxo_ref, r_ref, xm_ref, g_ref, wd_ref, wu_ref, wo_ref, *rest):
        du_ref, dx_ref, dg_ref, dmix_ref = rest[len(deps):]
        dxo = dxo_ref[...]
        du = (_dot(dxo.astype(bf16), wd_ref[...], NT_DIMS) * (2.0 * r_ref[...].astype(f32))).astype(bf16)
        du_ref[...] = du
        dh = _dot(du, wu_ref[...], NT_DIMS)
        _rms_bwd_epilogue(layer)(dh, pl.program_id(0), 0, (xm_ref, g_ref, dxo_ref), (dx_ref, dg_ref))
        dmix_ref[...] = _dot(dx_ref[...].astype(bf16), wo_ref[...], NT_DIMS)

    row = lambda width: pl.BlockSpec((tm, width), lambda i: (i, 0))
    return pl.pallas_call(
        body, grid=(S // tm,),
        in_specs=[row(D), row(D_FF), row(D), pl.BlockSpec((DEPTH, D), lambda i: (0, 0)), _resident((D_FF, D)), _resident((D, D_FF)),
                  _resident((D, D))] + [ANY_SPEC] * len(deps),
        out_specs=[row(D_FF), row(D), pl.BlockSpec((1, D), lambda i: (0, 0)), row(D)],
        out_shape=_out_hbm([SDS((S, D_FF), bf16), SDS((S, D), f32), SDS((1, D), f32), SDS((S, D), f32)]), name="mlp_bwd_act",
        compiler_params=pltpu.CompilerParams(dimension_semantics=("arbitrary",), vmem_limit_bytes=MLP_VMEM),
    )(*_in_hbm([dx_out, r_act, x_mid, g]), w_down, w_up, w_out, *_in_hbm(deps))


def _layer_fwd(l, x, p, get_weights, bias, tgt=None):
    wts = get_weights(l, "in", [x, bias])
    gfull = pl.BlockSpec((DEPTH, D), lambda i, j, k: (0, 0))
    tm = 512

    def inproj_epi(acc, i, j, ex, outs):
        outs[0][...] = acc[:, COL_QKV:COL_Z].astype(bf16)
        outs[1][...] = acc[:, COL_Z:COL_DT].astype(bf16)
        outs[2][...] = acc[:, COL_XBC:D_IN_PAD].astype(bf16)
        outs[3][...] = acc[:, COL_DT:COL_DT + 128]

    qkv, z, xbc, dt, h1 = _matmul(
        "in_proj", "nn", x, wts["w_in"], tm=tm, tn=D_IN_PAD, tk=D, prologue=_rms_prologue(l),
        extras=(p["mix_norm_g"],), extra_specs=(gfull,),
        out_shape=[SDS((S, 768), bf16), SDS((S, 512), bf16), SDS((S, 1024), bf16), SDS((S, 128), f32), SDS((S, D), bf16)],
        out_specs=[_rowblk(tm, 768), _rowblk(tm, 512), _rowblk(tm, 1024), _rowblk(tm, 128), _rowblk(tm, D)], epilogue=inproj_epi)
    attn = _attn_fwd(qkv, p["q_gain"], p["k_gain"], p["sinks"], bias, l)
    xact = _conv_fwd(xbc, wts["conv_w"], p["conv_b"], l)
    mix, hs, y_ssd = _ssd_fwd(xact, z, dt, attn, p["dt_bias"], p["a_log"], p["d_skip"], p["ssm_norm_g"], l)
    wts = dict(wts, **get_weights(l, "rest", [mix]))

    x_mid, a_act, r_act, h2, *result = _mlp_fwd(l, x, mix, p["mlp_norm_g"], wts["w_out"], wts["w_up"], wts["w_down"], tgt)
    saved = dict(x=x, h1=h1, qkv=qkv, z=z, xbc=xbc, dt=dt, xact=xact, mix=mix, hs=hs, y_ssd=y_ssd, x_mid=x_mid, h2=h2,
                 a=a_act, r=r_act, wts=wts)
    return (result[0] if tgt is None else tuple(result)), saved


def _layer_bwd(l, dx_out, sv, p, bias, deps, send):
    wts = sv["wts"]

    dw_down = _matmul("dw_down", "tn", sv["a"], dx_out, tm=512, tn=D, tk=S, out_shape=SDS((D_FF, D), bf16),
                      out_specs=_plain(512, D), epilogue=_store_epi(bf16), deps=deps)
    deps = send(l, dict(w_down=dw_down))
    du, dx_mid, dg_mlp, dmix = _mlp_bwd_act(l, dx_out, sv["r"], sv["x_mid"], p["mlp_norm_g"], wts["w_down"], wts["w_up"],
                                            wts["w_out"], deps)
    dw_up = _matmul("dw_up", "tn", sv["h2"], du, tm=D, tn=512, tk=S, out_shape=SDS((D, D_FF), bf16),
                    out_specs=_plain(D, 512), epilogue=_store_epi(bf16))
    dw_out = _matmul("dw_out", "tn", sv["mix"], dx_mid, tm=D, tn=512, tk=S, out_shape=SDS((D, D), bf16),
                     out_specs=_plain(D, 512), epilogue=_store_epi(bf16))
    deps = send(l, dict(w_up=dw_up, w_out=dw_out))
    gfull = pl.BlockSpec((DEPTH, D), lambda i, j, k: (0, 0))
    grow = pl.BlockSpec((1, D), lambda i, j, k: (0, 0))
    dproj, dbias, dsm_attn = _attn_bwd(sv["qkv"], dmix, p["q_gain"], p["k_gain"], p["sinks"], bias, l, deps)
    dproj, dxact, dsm_ssd = _ssd_bwd(sv["xact"], sv["z"], sv["dt"], dmix, sv["hs"], sv["y_ssd"], p["dt_bias"], p["a_log"],
                                     p["d_skip"], p["ssm_norm_g"], dproj, l)
    dproj, dconv_w, dconv_b = _conv_bwd(sv["xbc"], dxact, wts["conv_w"], p["conv_b"], dproj, l)
    dw_in = _matmul("dw_in", "tn", sv["h1"], dproj, tm=D, tn=1280, tk=S, out_shape=SDS((D, D_IN_PAD), bf16),
                    out_specs=_plain(D, 1280), epilogue=_store_epi(bf16))
    deps = send(l, dict(w_in=_w_in_slabs(dw_in)))
    dx, dg_mix = _matmul(
        "in_proj_dh", "nt", dproj, wts["w_in"], tm=512, tn=D, tk=D_IN_PAD, out_shape=[SDS((S, D), f32), SDS((1, D), f32)],
        out_specs=[_plain(512, D), grow], epilogue=_rms_bwd_epilogue(l),
        extras=(sv["x"], p["mix_norm_g"], dx_mid), extra_specs=(_plain(512, D), gfull, _plain(512, D)), deps=deps)
    small = dict(mix_norm_g=dg_mix, mlp_norm_g=dg_mlp, conv_w=dconv_w, conv_b=dconv_b, ssd=dsm_ssd, attn=dsm_attn, dbias=dbias)
    return dx, small, deps


def _local_step(x, tgt, p, get_weights, send):
    onehot_t = jnp.asarray(_onehot_buckets(), dtype=bf16)
    bias = _bias_build(p["rel_bias"].T, onehot_t).reshape(NQ, BLK, 2 * BLK)
    saved = []
    h = x
    for l in range(DEPTH):
        h, sv = _layer_fwd(l, h, p, get_weights, bias, tgt if l == DEPTH - 1 else None)
        saved.append(sv)
    dx, loss = h
    smalls = [None] * DEPTH
    deps = ()
    for l in reversed(range(DEPTH)):
        dx, smalls[l], deps = _layer_bwd(l, dx, saved[l], p, bias, deps, send)
    drel_t = _bias_grad(smalls[0]["dbias"].reshape(NQ, -1), smalls[1]["dbias"].reshape(NQ, -1), onehot_t)
    return dx, _pack_small_grads(smalls, drel_t, loss)


WEIGHT_ORDER = ("mix_norm_g", "w_in", "q_gain", "k_gain", "sinks", "rel_bias", "conv_w", "conv_b", "dt_bias", "a_log", "d_skip",
                "ssm_norm_g", "w_out", "mlp_norm_g", "w_up", "w_down")


def kernel(x, mix_norm_g, w_in, q_gain, k_gain, sinks, rel_bias, conv_w, conv_b, dt_bias, a_log, d_skip, ssm_norm_g, w_out, mlp_norm_g, w_up, w_down, loss_target, m_mix_norm_g, m_w_in, m_q_gain, m_k_gain, m_sinks, m_rel_bias, m_conv_w, m_conv_b, m_dt_bias, m_a_log, m_d_skip, m_ssm_norm_g, m_w_out, m_mlp_norm_g, m_w_up, m_w_down, v_mix_norm_g, v_w_in, v_q_gain, v_k_gain, v_sinks, v_rel_bias, v_conv_w, v_conv_b, v_dt_bias, v_a_log, v_d_skip, v_ssm_norm_g, v_w_out, v_mlp_norm_g, v_w_up, v_w_down):
    w = dict(mix_norm_g=mix_norm_g, w_in=w_in, q_gain=q_gain, k_gain=k_gain, sinks=sinks, rel_bias=rel_bias, conv_w=conv_w,
             conv_b=conv_b, dt_bias=dt_bias, a_log=a_log, d_skip=d_skip, ssm_norm_g=ssm_norm_g, w_out=w_out,
             mlp_norm_g=mlp_norm_g, w_up=w_up, w_down=w_down)
    m = dict(mix_norm_g=m_mix_norm_g, w_in=m_w_in, q_gain=m_q_gain, k_gain=m_k_gain, sinks=m_sinks, rel_bias=m_rel_bias,
             conv_w=m_conv_w, conv_b=m_conv_b, dt_bias=m_dt_bias, a_log=m_a_log, d_skip=m_d_skip, ssm_norm_g=m_ssm_norm_g,
             w_out=m_w_out, mlp_norm_g=m_mlp_norm_g, w_up=m_w_up, w_down=m_w_down)
    v = dict(mix_norm_g=v_mix_norm_g, w_in=v_w_in, q_gain=v_q_gain, k_gain=v_k_gain, sinks=v_sinks, rel_bias=v_rel_bias,
             conv_w=v_conv_w, conv_b=v_conv_b, dt_bias=v_dt_bias, a_log=v_a_log, d_skip=v_d_skip, ssm_norm_g=v_ssm_norm_g,
             w_out=v_w_out, mlp_norm_g=v_mlp_norm_g, w_up=v_w_up, w_down=v_w_down)
    big = ("w_in", "w_out", "w_up", "w_down")

    my_idx = _dev_index(*_my_place()).astype(jnp.int32).reshape(1)

    fulls = {n: _cast_to_full("cast_" + n, w[n], KIND[n], FULL_SHAPE[n], my_idx, bf16) for n in big}
    conv_full = _cast_to_full("cast_conv_w", conv_w.reshape(1, DEPTH * 4, 128), "stack", (N_DEV, DEPTH * 4, 128), my_idx, f32)[0]
    rest = ["w_out", "w_up", "w_down"]
    g0 = _gather_start("gather0", ["w_in", "conv_w"], [fulls["w_in"][0], conv_full], ())
    g1 = _gather_start("gather1", rest, [fulls[n][0] for n in rest], (g0["token"],))
    g2 = _gather_start("gather2", ["w_in"], [fulls["w_in"][1]], (g1["token"],))
    g3 = _gather_start("gather3", rest, [fulls[n][1] for n in rest], (g2["token"],))
    held = {}
    flat = lambda a: a.reshape(a.shape[0] * a.shape[1], a.shape[2])
    adam_in = {n: (flat(w[n]), flat(m[n]), flat(v[n])) for n in big}

    def get_weights(l, part, after):
        if l == 0 and part == "in":
            full_in, full_conv = _gather_finish("gather0", ["w_in", "conv_w"], g0,
                                                list(after) + [g3["token"], adam_in["w_in"][1], adam_in["w_in"][2]])
            held["conv_w"] = jnp.transpose(full_conv.reshape(N_DEV, DEPTH, 4, 128), (1, 2, 0, 3)).reshape(DEPTH, 4, D_CONV)
            return dict(w_in=_w_in_assemble(full_in), conv_w=held["conv_w"])
        if part == "in":
            return dict(w_in=_w_in_assemble(_gather_finish("gather2", ["w_in"], g2, after)[0]), conv_w=held["conv_w"])
        full = _gather_finish("gather1" if l == 0 else "gather3", rest, g1 if l == 0 else g3, after)
        return {n: f[None] for n, f in zip(rest, full)}

    pending = []

    def send(l, grads):
        names = list(grads)
        started = _exchange_start("exchange%d_%s" % (l, names[0]), names, [grads[n] for n in names], ())
        pending.append((l, names, started))
        return (started["token"],)

    dx, small_part = _local_step(x.reshape(S, D), loss_target.reshape(S, D), w, get_weights, send)

    small = _small_exchange_start(small_part, ())
    tiles = dict(w_in=512, w_out=128, w_up=512, w_down=256)
    outs_of = {n: None for n in big}
    after = [dx, small["token"]]
    for l, names, started in pending:
        bufs = _split_wait("exchange%d_%s_wait" % (l, names[0]), started, after)
        for t, n in enumerate(names):
            outs_of[n] = _adamw_layer("adamw_%s%d" % (n, l), KIND[n], l, *adam_in[n],
                                      bufs[len(names) + t], bufs[t], my_idx, outs_of[n], tiles[n])
        after = [outs_of[names[-1]][0]]
    res = {n: [o.reshape(w[n].shape) for o in outs_of[n]] for n in big}
    small_part, small_land = _split_wait("small_exchange_wait", small, after)
    small_outs = _adamw_small(small_part, small_land, w, m, v)
    loss = small_outs[0][0, 0]
    for k, name in enumerate(SMALL_NAMES):
        res[name] = small_outs[1 + 4 * k:5 + 4 * k]

    result = [loss, dx.reshape(1, S, D)]
    for k in range(4):
        result += [res[name][k] for name in WEIGHT_ORDER]
    return tuple(result)
```

```python
import functools
import math

import numpy as np
import jax
import jax.numpy as jnp
from jax import lax
from jax.experimental import pallas as pl
from jax.experimental.pallas import tpu as pltpu

f32 = jnp.float32
bf16 = jnp.bfloat16
SDS = jax.ShapeDtypeStruct
MESH = pl.DeviceIdType.MESH
HIGHEST = lax.Precision.HIGHEST

S = 2048
D = 1024
DEPTH = 2
BLK = 128
NBLK = S // BLK
HD = 64
NQ = 8
NKV = 2
NSSM = 8
NGRP = 2
NSTATE = 128
D_ATTN = 512
D_SSM = 512
D_CONV = 1024
D_FF = 4096
D_IN = 2312
D_IN_PAD = 2560
COL_QKV, COL_Z, COL_DT, COL_XBC = 0, 768, 1280, 1536
IN_SEGMENTS = ((0, 1280, 0), (1280, 2304, COL_XBC), (2304, 2312, COL_DT))
N_BUCKETS = 32
EPS = 1e-6
N_DEV = 8
VMEM_LIMIT = 48 * 1024 * 1024

ADAM_LR = 0.001
ADAM_B1 = 0.9
ADAM_B2 = 0.999
ADAM_EPS = 1e-08
ADAM_WD = 0.01
ADAM_STEP = 10

NT_DIMS = (((1,), (1,)), ((), ()))
TN_DIMS = (((0,), (0,)), ((), ()))
NN_DIMS = (((1,), (0,)), ((), ()))

ROW_MIXG = 0
ROW_MLPG = 2
ROW_CONVB = 4
ROW_SSMG = 6
ROW_MISC = 8
ROW_RELB = 10
ROW_CONVW = 18
ROW_LOSS = 26
SMALL_ROWS = 32
LANE_QG, LANE_KG, LANE_SINK, LANE_DTB, LANE_ALOG, LANE_DSKIP = 0, 64, 128, 256, 384, 512


def _dot(a, b, dims):
    return lax.dot_general(a, b, dims, preferred_element_type=f32)


def _cparams(n_axes):
    return pltpu.CompilerParams(dimension_semantics=("arbitrary",) * n_axes, vmem_limit_bytes=VMEM_LIMIT)


def _sum11(v):
    return jnp.sum(jnp.sum(v, axis=1, keepdims=True), axis=0, keepdims=True)


def _sigmoid(v):
    return 1.0 / (1.0 + jnp.exp(-v))


ANY_SPEC = pl.BlockSpec(memory_space=pl.ANY)
PIN_MIN = 1024


def _in_hbm(args):
    return [pltpu.with_memory_space_constraint(a, pltpu.HBM) if a.size >= PIN_MIN else a for a in args]


def _out_hbm(out_shape):
    one = lambda s: pltpu.HBM(s.shape, s.dtype) if math.prod(s.shape) >= PIN_MIN else s
    return [one(s) for s in out_shape] if isinstance(out_shape, (list, tuple)) else one(out_shape)


def _matmul(name, mode, a, b, *, layer=0, tm, tn, tk, out_shape, out_specs, epilogue, extras=(), extra_specs=(), deps=(),
            prologue=None):
    extras = tuple(extras) + tuple(deps)
    extra_specs = tuple(extra_specs) + (ANY_SPEC,) * len(deps)
    if mode == "tn":
        t_dim, m_dim = a.shape
        n_dim = b.shape[1]
        grid = (m_dim // tm, n_dim // tn, t_dim // tk)
        a_spec = pl.BlockSpec((tk, tm), lambda i, j, k: (k, i))
        b_spec = pl.BlockSpec((tk, tn), lambda i, j, k: (k, j))
        dims = TN_DIMS
    elif mode == "nn":
        m_dim, k_dim = a.shape
        n_dim = b.shape[-1]
        grid = (m_dim // tm, n_dim // tn, k_dim // tk)
        a_spec = pl.BlockSpec((tm, tk), lambda i, j, k: (i, k))
        b_spec = pl.BlockSpec((None, tk, tn), lambda i, j, k: (layer, k, j))
        dims = NN_DIMS
    else:
        m_dim, k_dim = a.shape
        n_dim = b.shape[-2]
        grid = (m_dim // tm, n_dim // tn, k_dim // tk)
        a_spec = pl.BlockSpec((tm, tk), lambda i, j, k: (i, k))
        b_spec = pl.BlockSpec((None, tn, tk), lambda i, j, k: (layer, j, k))
        dims = NT_DIMS
    nk = grid[2]
    n_ex = len(extras)

    def body(a_ref, b_ref, *rest):
        ex = rest[:n_ex - len(deps)]
        outs = rest[n_ex:-1]
        acc = rest[-1]
        i = pl.program_id(0)
        j = pl.program_id(1)
        k = pl.program_id(2)
        lhs = a_ref[...].astype(bf16) if prologue is None else prologue(a_ref, ex, outs)
        part = _dot(lhs, b_ref[...].astype(bf16), dims)
        if nk == 1:
            epilogue(part, i, j, ex, outs)
        else:
            @pl.when(k == 0)
            def _():
                acc[...] = part

            @pl.when(k > 0)
            def _():
                acc[...] += part

            @pl.when(k == nk - 1)
            def _():
                epilogue(acc[...], i, j, ex, outs)

    return pl.pallas_call(
        body, grid=grid, in_specs=[a_spec, b_spec, *extra_specs], out_specs=out_specs, out_shape=_out_hbm(out_shape),
        scratch_shapes=[pltpu.VMEM((tm, tn) if nk > 1 else (8, 128), f32)], name=name, compiler_params=_cparams(3),
    )(*_in_hbm([a]), b, *_in_hbm(extras))


def _rms_bwd_epilogue(layer):
    def epi(acc, i, j, ex, outs):
        x_ref, g_ref, dres_ref = ex
        dx_ref, dg_ref = outs
        xv = x_ref[...]
        r = lax.rsqrt(jnp.mean(xv * xv, axis=-1, keepdims=True) + EPS)
        xhat = xv * r
        w = acc * g_ref[layer:layer + 1, :]
        dx_ref[...] = dres_ref[...] + r * (w - xhat * jnp.mean(xhat * w, axis=-1, keepdims=True))
        dg = jnp.sum(acc * xhat, axis=0, keepdims=True)

        @pl.when(i == 0)
        def _():
            dg_ref[...] = dg

        @pl.when(i > 0)
        def _():
            dg_ref[...] += dg
    return epi


def _own_slab_spec(kind, tr, cols, nblk):
    if kind == "stack":
        return pl.BlockSpec((None, tr, cols), lambda i, idx: (idx[0], i, 0))
    if kind == "cols512":
        return pl.BlockSpec((tr, cols), lambda i, idx: (i, idx[0]))
    return pl.BlockSpec((tr, cols), lambda i, idx: (idx[0] * nblk + i, 0))


def _cast_to_full(name, w, kind, full_shape, my_idx, dtype):
    n_layers, rows, cols = w.shape
    tr = min(rows, 256)
    nblk = rows // tr

    def body(idx_ref, w_ref, *o_refs):
        for l in range(n_layers):
            o_refs[l][...] = w_ref[l].astype(dtype)

    grid_spec = pltpu.PrefetchScalarGridSpec(
        num_scalar_prefetch=1, grid=(nblk,), in_specs=[pl.BlockSpec((n_layers, tr, cols), lambda i, idx: (0, i, 0))],
        out_specs=[_own_slab_spec(kind, tr, cols, nblk)] * n_layers)
    return pl.pallas_call(body, grid_spec=grid_spec, out_shape=_out_hbm([SDS(full_shape, dtype)] * n_layers), name=name,
                          compiler_params=_cparams(1))(*_in_hbm([my_idx, w]))


def _adamw_math(w, m, v, g):
    m_new = ADAM_B1 * m + (1.0 - ADAM_B1) * g
    v_new = ADAM_B2 * v + (1.0 - ADAM_B2) * (g * g)
    m_hat = m_new / (1.0 - ADAM_B1 ** ADAM_STEP)
    v_hat = v_new / (1.0 - ADAM_B2 ** ADAM_STEP)
    delta = -ADAM_LR * (m_hat / (jnp.sqrt(v_hat) + ADAM_EPS) + ADAM_WD * w)
    return delta, m_new, v_new


def _adamw_layer(name, kind, layer, w, m, v, land, g_full, my_idx, prev, tr):
    rows2, cols = w.shape
    rows = rows2 // DEPTH
    nblk = rows // tr
    own_spec = _own_slab_spec(kind, tr, cols, nblk)
    n_prev = 0 if prev is None else 4

    def body(idx_ref, w_ref, m_ref, v_ref, land_ref, own_ref, *rest):
        g_ref, d_ref, mo_ref, vo_ref = rest[n_prev:]
        me = idx_ref[0]
        g = None
        for p in range(N_DEV):
            part = jnp.where(me == p, own_ref[...], land_ref[p]).astype(f32)
            g = part if g is None else g + part
        delta, m_new, v_new = _adamw_math(w_ref[...], m_ref[...], v_ref[...], g)
        g_ref[...] = g
        d_ref[...] = delta
        mo_ref[...] = m_new
        vo_ref[...] = v_new

    blk = pl.BlockSpec((tr, cols), lambda i, idx: (layer * nblk + i, 0))
    grid_spec = pltpu.PrefetchScalarGridSpec(
        num_scalar_prefetch=1, grid=(nblk,),
        in_specs=[blk, blk, blk, pl.BlockSpec((N_DEV, tr, cols), lambda i, idx: (0, i, 0)), own_spec] + [ANY_SPEC] * n_prev,
        out_specs=[blk, blk, blk, blk])
    aliases = {} if prev is None else {6 + k: k for k in range(4)}
    return pl.pallas_call(
        body, grid_spec=grid_spec, out_shape=_out_hbm([SDS((rows2, cols), f32)] * 4), name=name, input_output_aliases=aliases,
        compiler_params=_cparams(1),
    )(*_in_hbm([my_idx, w, m, v, land, g_full, *([] if prev is None else prev)]))


def _bucket_table():
    qi = np.arange(BLK)[:, None]
    kj = np.arange(2 * BLK)[None, :]
    dist = qi + BLK - kj
    dcl = np.clip(dist, 0, None)
    max_exact = N_BUCKETS // 2
    d_f = np.maximum(dcl, 1).astype(np.float32)
    large = max_exact + (np.log(d_f / np.float32(max_exact)) / np.float32(math.log(128 / max_exact))
                         * np.float32(N_BUCKETS - max_exact)).astype(np.int32)
    large = np.minimum(large, N_BUCKETS - 1)
    bucket = np.where(dcl < max_exact, dcl, large)
    in_window = (dist >= 0) & (dist < BLK)
    return bucket.astype(np.int32), in_window


def _onehot_buckets():
    bucket, _ = _bucket_table()
    oh = (bucket.reshape(-1)[None, :] == np.arange(N_BUCKETS)[:, None]).astype(np.float32)
    return oh


def _bias_build(rel_bias_t, onehot_t):
    def body(r_ref, o_ref, out_ref):
        r = r_ref[...]
        hi = r.astype(bf16)
        r1 = r - hi.astype(f32)
        mid = r1.astype(bf16)
        lo = (r1 - mid.astype(f32)).astype(bf16)
        oh = o_ref[...]
        out_ref[...] = _dot(hi, oh, NN_DIMS) + _dot(mid, oh, NN_DIMS) + _dot(lo, oh, NN_DIMS)

    tn = 4096
    return pl.pallas_call(
        body, grid=(BLK * 2 * BLK // tn,),
        in_specs=[pl.BlockSpec((NQ, N_BUCKETS), lambda i: (0, 0)), pl.BlockSpec((N_BUCKETS, tn), lambda i: (0, i))],
        out_specs=pl.BlockSpec((NQ, tn), lambda i: (0, i)), out_shape=SDS((NQ, BLK * 2 * BLK), f32), name="bias_build",
        compiler_params=_cparams(1),
    )(rel_bias_t, onehot_t)


def _bias_grad(dbias0, dbias1, onehot_t):
    tn = 4096
    nsteps = BLK * 2 * BLK // tn

    def body(a_ref, b_ref, o_ref, out_ref):
        g = a_ref[...] + b_ref[...]
        hi = g.astype(bf16)
        lo = (g - hi.astype(f32)).astype(bf16)
        part = _dot(hi, o_ref[...], NT_DIMS) + _dot(lo, o_ref[...], NT_DIMS)

        @pl.when(pl.program_id(0) == 0)
        def _():
            out_ref[...] = part

        @pl.when(pl.program_id(0) > 0)
        def _():
            out_ref[...] += part

    return pl.pallas_call(
        body, grid=(nsteps,),
        in_specs=[pl.BlockSpec((NQ, tn), lambda i: (0, i)), pl.BlockSpec((NQ, tn), lambda i: (0, i)),
                  pl.BlockSpec((N_BUCKETS, tn), lambda i: (0, i))],
        out_specs=pl.BlockSpec((NQ, N_BUCKETS), lambda i: (0, 0)), out_shape=SDS((NQ, N_BUCKETS), f32), name="bias_grad",
        compiler_params=_cparams(1),
    )(dbias0, dbias1, onehot_t)


def _attn_mask(n):
    qi = lax.broadcasted_iota(jnp.int32, (BLK, 2 * BLK), 0)
    kj = lax.broadcasted_iota(jnp.int32, (BLK, 2 * BLK), 1)
    dist = qi + BLK - kj
    first_key = jnp.where(n > 0, 0, BLK)
    return (dist >= 0) & (dist < BLK) & (kj >= first_key)


def _row_mean(a):
    return jnp.mean(a, axis=-1, keepdims=True)


def _head_norm(t, gain):
    r = lax.rsqrt(_row_mean(t * t) + EPS)
    that = t * r
    return that, r, that * gain


def _softmax_with_sink(s, sink):
    m = jnp.maximum(jnp.max(s, axis=-1, keepdims=True), sink)
    p = jnp.exp(s - m)
    psink = jnp.exp(sink - m)
    inv = 1.0 / (jnp.sum(p, axis=-1, keepdims=True) + psink)
    return p * inv, psink * inv


GQ = NQ // NKV


def _attn_fwd(qkv, q_gain, k_gain, sinks, bias, layer):
    def body(q_ref, kc_ref, kp_ref, vc_ref, vp_ref, qg_ref, kg_ref, sk_ref, bias_ref, o_ref):
        m = pl.program_id(0)
        qg = qg_ref[layer:layer + 1, :]
        kg = kg_ref[layer:layer + 1, :]
        grp = range(NKV)
        chains = [(b, j) for b in range(2) for j in grp]
        masks = [jnp.tile(_attn_mask(2 * m + b), (GQ, 1)) for b in range(2)]
        kblk = [[kp_ref[:, pl.ds(HD * j, HD)].astype(f32), kc_ref[0:BLK, pl.ds(HD * j, HD)].astype(f32),
                 kc_ref[BLK:, pl.ds(HD * j, HD)].astype(f32)] for j in grp]
        vblk = [[vp_ref[:, pl.ds(HD * j, HD)].astype(bf16), vc_ref[0:BLK, pl.ds(HD * j, HD)].astype(bf16),
                 vc_ref[BLK:, pl.ds(HD * j, HD)].astype(bf16)] for j in grp]
        knb = [[_head_norm(kblk[j][t], kg)[2].astype(bf16) for t in range(3)] for j in grp]
        kn_b = {(b, j): jnp.concatenate([knb[j][b], knb[j][b + 1]], axis=0) for b, j in chains}
        vbs = {(b, j): jnp.concatenate([vblk[j][b], vblk[j][b + 1]], axis=0) for b, j in chains}
        rows = {}
        for b, j in chains:
            heads = [GQ * j + g for g in range(GQ)]
            rows[b, j] = (jnp.concatenate([q_ref[pl.ds(BLK * b, BLK), pl.ds(HD * h, HD)] for h in heads], axis=0).astype(f32),
                          jnp.concatenate([jnp.broadcast_to(sk_ref[layer:layer + 1, h:h + 1], (BLK, 1)) for h in heads], axis=0))
        qn_b = {c: _head_norm(rows[c][0], qg)[2].astype(bf16) for c in chains}
        ss = {(b, j): _dot(qn_b[b, j], kn_b[b, j], NT_DIMS) * (HD ** -0.5) + bias_ref[GQ * j:GQ * (j + 1)].reshape(GQ * BLK, 2 * BLK)
              for b, j in chains}
        ps = {(b, j): _softmax_with_sink(jnp.where(masks[b], ss[b, j], -jnp.inf), rows[b, j][1])[0] for b, j in chains}
        outs = {c: _dot(ps[c].astype(bf16), vbs[c], NN_DIMS).astype(bf16) for c in chains}
        for b, j in chains:
            for g in range(GQ):
                o_ref[pl.ds(BLK * b, BLK), pl.ds(HD * (GQ * j + g), HD)] = outs[b, j][BLK * g:BLK * (g + 1), :]

    prev = lambda m: jnp.maximum(2 * m - 1, 0)
    small = lambda shape: pl.BlockSpec(shape, lambda m: (0,) * len(shape))
    return pl.pallas_call(
        body, grid=(NBLK // 2,),
        in_specs=[pl.BlockSpec((2 * BLK, D_ATTN), lambda m: (m, 0)),
                  pl.BlockSpec((2 * BLK, 128), lambda m: (m, 4)), pl.BlockSpec((BLK, 128), lambda m: (prev(m), 4)),
                  pl.BlockSpec((2 * BLK, 128), lambda m: (m, 5)), pl.BlockSpec((BLK, 128), lambda m: (prev(m), 5)),
                  small((DEPTH, HD)), small((DEPTH, HD)), small((DEPTH, NQ)), small((NQ, BLK, 2 * BLK))],
        out_specs=pl.BlockSpec((2 * BLK, D_ATTN), lambda m: (m, 0)), out_shape=_out_hbm(SDS((S, D_ATTN), bf16)),
        name="attn_fwd", compiler_params=_cparams(1),
    )(*_in_hbm([qkv, qkv, qkv, qkv, qkv, q_gain, k_gain, sinks, bias]))


def _attn_bwd(qkv, dmix, q_gain, k_gain, sinks, bias, layer, deps=()):
    def body(q_ref, kc_ref, kp_ref, vc_ref, vp_ref, do_ref, qg_ref, kg_ref, sk_ref, bias_ref, *rest):
        dqkv_ref, dbias_ref, dsm_ref, carry = rest[len(deps):]
        i = pl.program_id(0)
        m = NBLK // 2 - 1 - i
        qg = qg_ref[layer:layer + 1, :]
        kg = kg_ref[layer:layer + 1, :]
        lane = lax.broadcasted_iota(jnp.int32, (1, 128), 1)

        @pl.when(i == 0)
        def _():
            carry[...] = jnp.zeros_like(carry)
            dbias_ref[...] = jnp.zeros_like(dbias_ref)
            dsm_ref[...] = jnp.zeros_like(dsm_ref)

        grp = range(NKV)
        chains = [(b, j) for b in range(2) for j in grp]
        masks = [jnp.tile(_attn_mask(2 * m + b), (GQ, 1)) for b in range(2)]
        kblk = [[kp_ref[:, pl.ds(HD * j, HD)].astype(f32), kc_ref[0:BLK, pl.ds(HD * j, HD)].astype(f32),
                 kc_ref[BLK:, pl.ds(HD * j, HD)].astype(f32)] for j in grp]
        vblk = [[vp_ref[:, pl.ds(HD * j, HD)].astype(bf16), vc_ref[0:BLK, pl.ds(HD * j, HD)].astype(bf16),
                 vc_ref[BLK:, pl.ds(HD * j, HD)].astype(bf16)] for j in grp]
        knorm = [[_head_norm(kblk[j][t], kg) for t in range(3)] for j in grp]
        kn_b = {(b, j): jnp.concatenate([knorm[j][b][2].astype(bf16), knorm[j][b + 1][2].astype(bf16)], axis=0) for b, j in chains}
        vbs = {(b, j): jnp.concatenate([vblk[j][b], vblk[j][b + 1]], axis=0) for b, j in chains}
        rows, do_b = {}, {}
        for b, j in chains:
            heads = [GQ * j + g for g in range(GQ)]
            qrows = pl.ds(BLK * b, BLK)
            rows[b, j] = (jnp.concatenate([q_ref[qrows, pl.ds(HD * h, HD)] for h in heads], axis=0).astype(f32),
                          jnp.concatenate([jnp.broadcast_to(sk_ref[layer:layer + 1, h:h + 1], (BLK, 1)) for h in heads], axis=0))
            do_b[b, j] = jnp.concatenate([do_ref[qrows, pl.ds(HD * h, HD)] for h in heads], axis=0).astype(bf16)
        qnorm = {c: _head_norm(rows[c][0], qg) for c in chains}
        qn_b = {c: qnorm[c][2].astype(bf16) for c in chains}
        ss = {(b, j): _dot(qn_b[b, j], kn_b[b, j], NT_DIMS) * (HD ** -0.5) + bias_ref[GQ * j:GQ * (j + 1)].reshape(GQ * BLK, 2 * BLK)
              for b, j in chains}
        sm = {(b, j): _softmax_with_sink(jnp.where(masks[b], ss[b, j], -jnp.inf), rows[b, j][1]) for b, j in chains}
        dps = {c: _dot(do_b[c], vbs[c], NT_DIMS) for c in chains}
        deltas = {c: jnp.sum(sm[c][0] * dps[c], axis=-1, keepdims=True) for c in chains}
        dss = {c: sm[c][0] * (dps[c] - deltas[c]) for c in chains}
        ds_b = {c: (dss[c] * (HD ** -0.5)).astype(bf16) for c in chains}
        dqn = {c: _dot(ds_b[c], kn_b[c], NN_DIMS) for c in chains}
        dkn = {c: _dot(ds_b[c], qn_b[c], TN_DIMS) for c in chains}
        dvs = {c: _dot(sm[c][0].astype(bf16), do_b[c], TN_DIMS) for c in chains}
        dqg = jnp.zeros((1, HD), f32)
        dkg = jnp.zeros((1, HD), f32)
        dsink = jnp.zeros((1, 128), f32)
        for b, j in chains:
            dbias_ref[GQ * j:GQ * (j + 1)] += dss[b, j].reshape(GQ, BLK, 2 * BLK)
            dsk = sm[b, j][1] * deltas[b, j]
            for g in range(GQ):
                dsink = dsink + jnp.where(lane == GQ * j + g, -_sum11(dsk[BLK * g:BLK * (g + 1), :]), 0.0)
            qhat, rq, _ = qnorm[b, j]
            w = dqn[b, j] * qg
            dq = rq * (w - qhat * _row_mean(qhat * w))
            for g in range(GQ):
                dqkv_ref[pl.ds(BLK * b, BLK), pl.ds(HD * (GQ * j + g), HD)] = dq[BLK * g:BLK * (g + 1), :].astype(bf16)
            dqg = dqg + jnp.sum(dqn[b, j] * qhat, axis=0, keepdims=True)
        for j in grp:
            dkn_t = [dkn[0, j][:BLK, :], dkn[0, j][BLK:, :] + dkn[1, j][:BLK, :], dkn[1, j][BLK:, :]]
            dv_t = [dvs[0, j][:BLK, :], dvs[0, j][BLK:, :] + dvs[1, j][:BLK, :], dvs[1, j][BLK:, :]]
            dk_t = []
            for t in range(3):
                khat, rk, _ = knorm[j][t]
                w = dkn_t[t] * kg
                dk_t.append(rk * (w - khat * _row_mean(khat * w)))
                dkg = dkg + jnp.sum(dkn_t[t] * khat, axis=0, keepdims=True)
            kcols, vcols = pl.ds(D_ATTN + HD * j, HD), pl.ds(D_ATTN + 128 + HD * j, HD)
            dqkv_ref[BLK:, kcols] = (dk_t[2] + carry[:, pl.ds(HD * j, HD)]).astype(bf16)
            dqkv_ref[BLK:, vcols] = (dv_t[2] + carry[:, pl.ds(128 + HD * j, HD)]).astype(bf16)
            dqkv_ref[0:BLK, kcols] = dk_t[1].astype(bf16)
            dqkv_ref[0:BLK, vcols] = dv_t[1].astype(bf16)
            carry[:, pl.ds(HD * j, HD)] = dk_t[0]
            carry[:, pl.ds(128 + HD * j, HD)] = dv_t[0]
        dsm_ref[0:1, 0:HD] += dqg
        dsm_ref[1:2, 0:HD] += dkg
        dsm_ref[2:3, :] += dsink

    rev = lambda i: NBLK // 2 - 1 - i
    prev = lambda i: jnp.maximum(NBLK - 3 - 2 * i, 0)
    small = lambda shape: pl.BlockSpec(shape, lambda i: (0,) * len(shape))
    return pl.pallas_call(
        body, grid=(NBLK // 2,),
        in_specs=[pl.BlockSpec((2 * BLK, D_ATTN), lambda i: (rev(i), 0)),
                  pl.BlockSpec((2 * BLK, 128), lambda i: (rev(i), 4)), pl.BlockSpec((BLK, 128), lambda i: (prev(i), 4)),
                  pl.BlockSpec((2 * BLK, 128), lambda i: (rev(i), 5)), pl.BlockSpec((BLK, 128), lambda i: (prev(i), 5)),
                  pl.BlockSpec((2 * BLK, D_ATTN), lambda i: (rev(i), 0)),
                  small((DEPTH, HD)), small((DEPTH, HD)), small((DEPTH, NQ)), small((NQ, BLK, 2 * BLK))] + [ANY_SPEC] * len(deps),
        out_specs=[pl.BlockSpec((2 * BLK, 768), lambda i: (rev(i), COL_QKV // 768)), small((NQ, BLK, 2 * BLK)), small((8, 128))],
        out_shape=_out_hbm([SDS((S, D_IN_PAD), bf16), SDS((NQ, BLK, 2 * BLK), f32), SDS((8, 128), f32)]),
        scratch_shapes=[pltpu.VMEM((BLK, 256), f32)], name="attn_bwd", compiler_params=_cparams(1),
    )(*_in_hbm([qkv, qkv, qkv, qkv, qkv, dmix, q_gain, k_gain, sinks, bias, *deps]))


CONV_TC = 256


def _shift_down(u, s):
    if s == 0:
        return u
    rows = lax.broadcasted_iota(jnp.int32, u.shape, 0)
    return jnp.where(rows >= s, pltpu.roll(u, s, 0), 0.0)


def _shift_up(u, s):
    if s == 0:
        return u
    rows = lax.broadcasted_iota(jnp.int32, u.shape, 0)
    return jnp.where(rows < u.shape[0] - s, pltpu.roll(u, u.shape[0] - s, 0), 0.0)


def _conv_specs():
    return [pl.BlockSpec((S, CONV_TC), lambda c: (0, c)),
            pl.BlockSpec((None, 4, CONV_TC), lambda c: (0, 0, c)),
            pl.BlockSpec((DEPTH, CONV_TC), lambda c: (0, c))]


def _conv_pre(u, w_ref, b_ref, layer):
    pre = b_ref[layer:layer + 1, :] + w_ref[3:4, :] * u
    for k in range(3):
        pre = pre + w_ref[k:k + 1, :] * _shift_down(u, 3 - k)
    return pre


def _conv_fwd(xbc, conv_w, conv_b, layer):
    def body(u_ref, w_ref, b_ref, o_ref):
        pre = _conv_pre(u_ref[...].astype(f32), w_ref, b_ref, layer)
        o_ref[...] = pre * _sigmoid(pre)

    specs = _conv_specs()
    specs[1] = pl.BlockSpec((None, 4, CONV_TC), lambda c: (layer, 0, c))
    return pl.pallas_call(
        body, grid=(D_CONV // CONV_TC,), in_specs=specs, out_specs=pl.BlockSpec((S, CONV_TC), lambda c: (0, c)),
        out_shape=_out_hbm(SDS((S, D_CONV), f32)), name="conv_fwd", compiler_params=_cparams(1),
    )(*_in_hbm([xbc, conv_w, conv_b]))


def _conv_bwd(xbc, dact, conv_w, conv_b, dproj, layer):
    def body(u_ref, w_ref, b_ref, da_ref, dproj_in, du_ref, dw_ref, db_ref):
        u = u_ref[...].astype(f32)
        pre = _conv_pre(u, w_ref, b_ref, layer)
        sg = _sigmoid(pre)
        dpre = da_ref[...] * (sg * (1.0 + pre * (1.0 - sg)))
        du = w_ref[3:4, :] * dpre
        for k in range(3):
            du = du + w_ref[k:k + 1, :] * _shift_up(dpre, 3 - k)
        du_ref[...] = du.astype(bf16)
        db_ref[...] = jnp.broadcast_to(jnp.sum(dpre, axis=0, keepdims=True), db_ref.shape)
        dw_ref[...] = jnp.zeros_like(dw_ref)
        for k in range(4):
            dw_ref[k:k + 1, :] = jnp.sum(dpre * _shift_down(u, 3 - k), axis=0, keepdims=True)

    specs = _conv_specs()
    specs[1] = pl.BlockSpec((None, 4, CONV_TC), lambda c: (layer, 0, c))
    col = pl.BlockSpec((S, CONV_TC), lambda c: (0, c))
    row8 = pl.BlockSpec((8, CONV_TC), lambda c: (0, c))
    return pl.pallas_call(
        body, grid=(D_CONV // CONV_TC,), in_specs=[*specs, col, ANY_SPEC],
        out_specs=[pl.BlockSpec((S, CONV_TC), lambda c: (0, COL_XBC // CONV_TC + c)), row8, row8],
        out_shape=_out_hbm([SDS((S, D_IN_PAD), bf16), SDS((8, D_CONV), f32), SDS((8, D_CONV), f32)]), name="conv_bwd",
        input_output_aliases={4: 0}, compiler_params=_cparams(1),
    )(*_in_hbm([xbc, conv_w, conv_b, dact, dproj]))


def _tri():
    return (lax.broadcasted_iota(jnp.int32, (BLK, BLK), 0) >= lax.broadcasted_iota(jnp.int32, (BLK, BLK), 1))


def _ssd_scalars(dt_ref, dtb_ref, alog_ref, layer):
    raw = dt_ref[:, 0:NSSM] + dtb_ref[layer:layer + 1, :]
    dtv = jnp.maximum(raw, 0.0) + jnp.log(1.0 + jnp.exp(-jnp.abs(raw)))
    a = -jnp.exp(alog_ref[layer:layer + 1, :])
    acs = jnp.dot(_tri().astype(f32), dtv * a, preferred_element_type=f32, precision=HIGHEST)
    return raw, dtv, a, acs


HG = NSSM // NGRP
GW = HG * HD


def _lane_expand(cols, g):
    lane_head = lax.broadcasted_iota(jnp.int32, (1, GW), 1) // HD
    out = cols[:, HG * g + HG - 1:HG * g + HG]
    for r in range(HG - 2, -1, -1):
        out = jnp.where(lane_head == r, cols[:, HG * g + r:HG * g + r + 1], out)
    return out


def _row_expand(vals, g):
    row_head = lax.broadcasted_iota(jnp.int32, (GW, 1), 0) // HD
    out = vals[:, HG * g + HG - 1:HG * g + HG]
    for r in range(HG - 2, -1, -1):
        out = jnp.where(row_head == r, vals[:, HG * g + r:HG * g + r + 1], out)
    return out


def _head_rowsums(a, g):
    sel = (lax.broadcasted_iota(jnp.int32, (GW, NSSM), 0) // HD + HG * g == lax.broadcasted_iota(jnp.int32, (GW, NSSM), 1)).astype(bf16)
    hi = a.astype(bf16)
    lo = (a - hi.astype(f32)).astype(bf16)
    return _dot(hi, sel, NN_DIMS) + _dot(lo, sel, NN_DIMS)


def _head_blocksums(v, g):
    sel = (lax.broadcasted_iota(jnp.int32, (GW, NSSM), 0) // HD + HG * g == lax.broadcasted_iota(jnp.int32, (GW, NSSM), 1)).astype(bf16)
    hi = v.astype(bf16)
    lo = (v - hi.astype(f32)).astype(bf16)
    return _dot(hi, sel, TN_DIMS) + _dot(lo, sel, TN_DIMS)


def _ssd_chunk_common(xc_ref, dt_ref, dtb_ref, alog_ref, h_rows, layer):
    raw, dtv, a, acs = _ssd_scalars(dt_ref, dtb_ref, alog_ref, layer)
    acs_t = acs.T
    last = acs[BLK - 1:BLK, :]
    c = dict(raw=raw, dtv=dtv, a=a, acs=acs, last=last, dte=jnp.exp(last - acs), e_all=jnp.exp(acs), cd=jnp.exp(last))
    grp, heads, tri = range(NGRP), range(NSSM), _tri()
    c["bm"] = [xc_ref[:, pl.ds(D_SSM + NSTATE * g, NSTATE)] for g in grp]
    c["bm_b"] = [c["bm"][g].astype(bf16) for g in grp]
    c["cm_b"] = [xc_ref[:, pl.ds(D_SSM + NGRP * NSTATE + NSTATE * g, NSTATE)].astype(bf16) for g in grp]
    c["cb"] = [_dot(c["cm_b"][g], c["bm_b"][g], NT_DIMS) for g in grp]
    c["x"] = [xc_ref[:, pl.ds(GW * g, GW)] for g in grp]
    c["dt"] = [_lane_expand(dtv, g) for g in grp]
    c["xdt"] = [c["x"][g] * c["dt"][g] for g in grp]
    c["xdt_b"] = [c["xdt"][g].astype(bf16) for g in grp]
    c["prev"] = [h_rows(g) for g in grp]
    c["prev_b"] = [c["prev"][g].astype(bf16) for g in grp]
    c["e"] = [_lane_expand(c["e_all"], g) for g in grp]
    c["y_off"] = [_dot(c["cm_b"][g], c["prev_b"][g], NT_DIMS) * c["e"][g] for g in grp]
    c["decay"] = [jnp.exp(jnp.where(tri, acs[:, h:h + 1] - acs_t[h:h + 1, :], -jnp.inf)) for h in heads]
    c["m"] = [c["cb"][h // HG] * c["decay"][h] for h in heads]
    c["m_b"] = [c["m"][h].astype(bf16) for h in heads]
    c["dte_x"] = [_lane_expand(c["dte"], g) for g in grp]
    c["xdte_b"] = [(c["xdt"][g] * c["dte_x"][g]).astype(bf16) for g in grp]
    return c


def _ssd_fwd(xact, z, dt, attn, dt_bias, a_log, d_skip, norm_g, layer):
    def body(xc_ref, z_ref, dt_ref, at_ref, dtb_ref, alog_ref, dsk_ref, ng_ref, mix_ref, hs_ref, y_ref, h_ref):
        n = pl.program_id(0)

        @pl.when(n == 0)
        def _():
            h_ref[...] = jnp.zeros_like(h_ref)

        hs_ref[...] = h_ref[...]
        c = _ssd_chunk_common(xc_ref, dt_ref, dtb_ref, alog_ref, lambda g: h_ref[pl.ds(GW * g, GW), :], layer)
        grp, heads = range(NGRP), range(NSSM)
        y_diag = [_dot(c["m_b"][h], c["xdt_b"][h // HG][:, HD * (h % HG):HD * (h % HG + 1)], NN_DIMS) for h in heads]
        new_st = [_dot(c["xdte_b"][g], c["bm_b"][g], TN_DIMS) for g in grp]
        for h in heads:
            y_ref[:, pl.ds(HD * h, HD)] = y_diag[h]
        dskip = dsk_ref[layer:layer + 1, :]
        for g in grp:
            cols = pl.ds(GW * g, GW)
            y_ref[:, cols] = y_ref[:, cols] + c["y_off"][g] + c["x"][g] * _lane_expand(dskip, g)
            h_ref[cols, :] = c["prev"][g] * _row_expand(c["cd"], g) + new_st[g]
        zv = z_ref[...].astype(f32)
        yz = y_ref[...] * (zv * _sigmoid(zv))
        mix_ref[:, 0:D_ATTN] = at_ref[...]
        for g in grp:
            yg = yz[:, GW * g:GW * (g + 1)]
            rs = lax.rsqrt(jnp.mean(yg * yg, axis=-1, keepdims=True) + EPS)
            mix_ref[:, D_ATTN + GW * g:D_ATTN + GW * (g + 1)] = (yg * rs * ng_ref[layer:layer + 1, GW * g:GW * (g + 1)]).astype(bf16)

    small = lambda shape: pl.BlockSpec(shape, lambda n: (0,) * len(shape))
    return pl.pallas_call(
        body, grid=(NBLK,),
        in_specs=[pl.BlockSpec((BLK, D_CONV), lambda n: (n, 0)), pl.BlockSpec((BLK, D_SSM), lambda n: (n, 0)),
                  pl.BlockSpec((BLK, 128), lambda n: (n, 0)), pl.BlockSpec((BLK, D_ATTN), lambda n: (n, 0)),
                  small((DEPTH, NSSM)), small((DEPTH, NSSM)), small((DEPTH, NSSM)), small((DEPTH, D_SSM))],
        out_specs=[pl.BlockSpec((BLK, D), lambda n: (n, 0)), pl.BlockSpec((None, NSSM * HD, NSTATE), lambda n: (n, 0, 0)),
                   pl.BlockSpec((BLK, D_SSM), lambda n: (n, 0))],
        out_shape=_out_hbm([SDS((S, D), bf16), SDS((NBLK, NSSM * HD, NSTATE), f32), SDS((S, D_SSM), f32)]),
        scratch_shapes=[pltpu.VMEM((NSSM * HD, NSTATE), f32)],
        name="ssd_fwd", compiler_params=_cparams(1),
    )(*_in_hbm([xact, z, dt, attn, dt_bias, a_log, d_skip, norm_g]))


def _ssd_bwd(xact, z, dt, dmix, hs, y, dt_bias, a_log, d_skip, norm_g, dproj, layer):
    def body(xc_ref, z_ref, dt_ref, do_ref, hs_ref, y_ref, dtb_ref, alog_ref, dsk_ref, ng_ref, dproj_in,
             dzdt_ref, dx_ref, dsm_ref, dh_ref, dy_ref):
        i = pl.program_id(0)

        @pl.when(i == 0)
        def _():
            dh_ref[...] = jnp.zeros_like(dh_ref)
            dsm_ref[...] = jnp.zeros_like(dsm_ref)

        c = _ssd_chunk_common(xc_ref, dt_ref, dtb_ref, alog_ref, lambda g: hs_ref[pl.ds(GW * g, GW), :], layer)
        raw, dtv, a = c["raw"], c["dtv"], c["a"]
        grp, heads = range(NGRP), range(NSSM)
        dskip = dsk_ref[layer:layer + 1, :]
        lane8 = lax.broadcasted_iota(jnp.int32, (1, NSSM), 1)
        sub8 = lax.broadcasted_iota(jnp.int32, (NSSM, 1), 0)

        zv = z_ref[...].astype(f32)
        sz = _sigmoid(zv)
        gz = zv * sz
        yv = y_ref[...]
        yz = yv * gz
        for g in grp:
            sl = slice(GW * g, GW * (g + 1))
            yg = yz[:, sl]
            rs = lax.rsqrt(jnp.mean(yg * yg, axis=-1, keepdims=True) + EPS)
            yhat = yg * rs
            dog = do_ref[:, sl]
            w = dog * ng_ref[layer:layer + 1, sl]
            dyz = rs * (w - yhat * jnp.mean(yhat * w, axis=-1, keepdims=True))
            dsm_ref[0:1, sl] += jnp.sum(dog * yhat, axis=0, keepdims=True)
            dy_ref[:, sl] = dyz * gz[:, sl]
            dzdt_ref[:, sl] = (dyz * yv[:, sl] * (sz[:, sl] * (1.0 + zv[:, sl] * (1.0 - sz[:, sl])))).astype(bf16)

        dy = [dy_ref[:, pl.ds(GW * g, GW)] for g in grp]
        dy_b = [dy[g].astype(bf16) for g in grp]
        hl = lambda h: slice(HD * (h % HG), HD * (h % HG + 1))
        dt_off_b = [(dy[g] * c["e"][g]).astype(bf16) for g in grp]
        dcm = [_dot(dt_off_b[g], c["prev_b"][g], NN_DIMS) for g in grp]
        dprev = [_dot(dt_off_b[g], c["cm_b"][g], TN_DIMS) for g in grp]
        yoff_rs = [_head_rowsums(dy[g] * c["y_off"][g], g) for g in grp]
        dhn = [dh_ref[pl.ds(GW * g, GW), :] for g in grp]
        dhn_b = [dhn[g].astype(bf16) for g in grp]
        dprev = [dprev[g] + dhn[g] * _row_expand(c["cd"], g) for g in grp]
        dhn_prev = [dhn[g] * c["prev"][g] for g in grp]
        u = [_dot(c["bm_b"][g], dhn_b[g], NT_DIMS) for g in grp]
        dbm = [_dot(c["xdte_b"][g], dhn_b[g], NN_DIMS) for g in grp]
        ddte_rs = [_head_rowsums(c["xdt"][g] * u[g], g) for g in grp]
        dm = [_dot(dy_b[h // HG][:, hl(h)], c["xdt_b"][h // HG][:, hl(h)], NT_DIMS) for h in heads]
        dxdt_in = [_dot(c["m_b"][h], dy_b[h // HG][:, hl(h)], TN_DIMS) for h in heads]
        dseg = [dm[h] * c["m"][h] for h in heads]
        dmd = [dm[h] * c["decay"][h] for h in heads]
        for h in heads:
            dx_ref[:, pl.ds(HD * h, HD)] = dxdt_in[h]

        tmp = (ddte_rs[0] + ddte_rs[1]) * c["dte"]
        dacs = yoff_rs[0] + yoff_rs[1] - tmp
        dacs_cols = jnp.zeros((NSSM, BLK), f32)
        ddtv = jnp.zeros((BLK, NSSM), f32)
        ddsk = jnp.zeros((BLK, NSSM), f32)
        hp = jnp.zeros((1, NSSM), f32)
        for g in grp:
            cols = pl.ds(GW * g, GW)
            dxdt = dx_ref[:, cols] + u[g] * c["dte_x"][g]
            dx_ref[:, cols] = dy[g] * _lane_expand(dskip, g) + dxdt * c["dt"][g]
            ddtv = ddtv + _head_rowsums(dxdt * c["x"][g], g)
            ddsk = ddsk + _head_rowsums(dy[g] * c["x"][g], g)
            dcb = dmd[HG * g]
            for r in range(1, HG):
                dcb = dcb + dmd[HG * g + r]
            dcb_b = dcb.astype(bf16)
            dx_ref[:, pl.ds(D_SSM + NSTATE * g, NSTATE)] = dbm[g] + _dot(dcb_b, c["cm_b"][g], TN_DIMS)
            dx_ref[:, pl.ds(D_SSM + NGRP * NSTATE + NSTATE * g, NSTATE)] = dcm[g] + _dot(dcb_b, c["bm_b"][g], NN_DIMS)
            dh_ref[cols, :] = dprev[g]
            hp = hp + _head_blocksums(jnp.sum(dhn_prev[g], axis=1, keepdims=True), g)
            for r in range(HG):
                h = HG * g + r
                dacs = dacs + (lane8 == h).astype(f32) * jnp.sum(dseg[h], axis=1, keepdims=True)
                dacs_cols = dacs_cols + (sub8 == h).astype(f32) * jnp.sum(dseg[h], axis=0, keepdims=True)
        dlast = hp * c["cd"] + jnp.sum(tmp, axis=0, keepdims=True)
        ddsk = jnp.sum(ddsk, axis=0, keepdims=True)

        row = lax.broadcasted_iota(jnp.int32, (BLK, 1), 0)
        dacs = dacs - dacs_cols.T + jnp.where(row == BLK - 1, dlast, 0.0)
        dda = lax.dot_general(_tri().astype(f32), dacs, TN_DIMS, preferred_element_type=f32, precision=HIGHEST)
        ddtv = ddtv + dda * a
        da = jnp.sum(dda * dtv, axis=0, keepdims=True)
        draw = ddtv * _sigmoid(raw)
        dzdt_ref[:, D_SSM:] = jnp.zeros((BLK, COL_XBC - COL_DT), bf16)
        dzdt_ref[:, D_SSM:D_SSM + NSSM] = draw.astype(bf16)
        dsm_ref[1:2, 0:NSSM] += jnp.sum(draw, axis=0, keepdims=True)
        dsm_ref[2:3, 0:NSSM] += da * a
        dsm_ref[3:4, 0:NSSM] += ddsk

    rev = lambda i: NBLK - 1 - i
    small = lambda shape: pl.BlockSpec(shape, lambda i: (0,) * len(shape))
    return pl.pallas_call(
        body, grid=(NBLK,),
        in_specs=[pl.BlockSpec((BLK, D_CONV), lambda i: (rev(i), 0)), pl.BlockSpec((BLK, D_SSM), lambda i: (rev(i), 0)),
                  pl.BlockSpec((BLK, 128), lambda i: (rev(i), 0)), pl.BlockSpec((BLK, D_SSM), lambda i: (rev(i), 1)),
                  pl.BlockSpec((None, NSSM * HD, NSTATE), lambda i: (rev(i), 0, 0)), pl.BlockSpec((BLK, D_SSM), lambda i: (rev(i), 0)),
                  small((DEPTH, NSSM)), small((DEPTH, NSSM)), small((DEPTH, NSSM)), small((DEPTH, D_SSM)), ANY_SPEC],
        out_specs=[pl.BlockSpec((BLK, COL_XBC - COL_Z), lambda i: (rev(i), COL_Z // (COL_XBC - COL_Z))),
                   pl.BlockSpec((BLK, D_CONV), lambda i: (rev(i), 0)), small((8, D_SSM))],
        out_shape=_out_hbm([SDS((S, D_IN_PAD), bf16), SDS((S, D_CONV), f32), SDS((8, D_SSM), f32)]),
        scratch_shapes=[pltpu.VMEM((NSSM * HD, NSTATE), f32), pltpu.VMEM((BLK, D_SSM), f32)],
        name="ssd_bwd", input_output_aliases={10: 0}, compiler_params=_cparams(1),
    )(*_in_hbm([xact, z, dt, dmix, hs, y, dt_bias, a_log, d_skip, norm_g, dproj]))


def _my_place():
    return lax.axis_index("x"), lax.axis_index("y"), lax.axis_index("c")


def _dev_index(px, py, pc):
    return 4 * px + 2 * py + pc


def _slab2(kind, ref, idx):
    if kind == "stack":
        return ref.at[idx]
    if kind == "rows128":
        return ref.at[pl.ds(pl.multiple_of(idx * 128, 128), 128), :]
    if kind == "rows512":
        return ref.at[pl.ds(pl.multiple_of(idx * 512, 512), 512), :]
    return ref.at[:, pl.ds(pl.multiple_of(idx * 512, 512), 512)]


def _slab_shape(kind, full_shape):
    if kind == "stack":
        return tuple(full_shape[1:])
    if kind == "rows128":
        return (128, full_shape[1])
    if kind == "rows512":
        return (512, full_shape[1])
    return (full_shape[0], 512)


KIND = dict(w_in="stack", w_out="rows128", w_up="cols512", w_down="rows512", conv_w="stack")
FULL_SHAPE = dict(w_in=(N_DEV, D, D_IN // N_DEV), w_out=(D, D), w_up=(D, D_FF), w_down=(D_FF, D))
HBM_SPEC = pl.BlockSpec(memory_space=pltpu.HBM)
SEM_SPEC = pl.BlockSpec(memory_space=pltpu.SEMAPHORE)
SIDE_EFFECT = pltpu.SideEffectType.DATAFLOW_SIDE_EFFECTING


def _peers_all():
    x, y, c = _my_place()
    return [(x ^ ((r >> 2) & 1), y ^ ((r >> 1) & 1), c ^ (r & 1)) for r in range(1, N_DEV)]


def _split_start(name, bufs, n_copies, plan, deps=()):
    nb = len(bufs)

    def body(*refs):
        ins = refs[:nb]
        send_sems, recv_sems = refs[nb + len(deps)], refs[nb + len(deps) + 1]
        token = refs[-1]
        for i, (src, dst, dev) in enumerate(plan(ins)):
            pltpu.make_async_remote_copy(src_ref=src, dst_ref=dst, send_sem=send_sems.at[i], recv_sem=recv_sems.at[i],
                                         device_id=dev, device_id_type=MESH).start()
        token[...] = jnp.zeros_like(token)

    outs = pl.pallas_call(
        body, name=name,
        out_shape=(pltpu.SemaphoreType.DMA((n_copies,)), pltpu.SemaphoreType.DMA((n_copies,)),
                   *[pltpu.HBM(b.shape, b.dtype) for b in bufs], SDS((8, 128), f32)),
        in_specs=[HBM_SPEC] * nb + [ANY_SPEC] * len(deps),
        out_specs=(SEM_SPEC, SEM_SPEC, *[HBM_SPEC] * nb, pl.BlockSpec(memory_space=pltpu.VMEM)),
        input_output_aliases={i: 2 + i for i in range(nb)},
        compiler_params=pltpu.CompilerParams(has_side_effects=SIDE_EFFECT),
    )(*[pltpu.with_memory_space_constraint(b, pltpu.HBM) for b in bufs], *deps)
    return dict(send=outs[0], recv=outs[1], bufs=list(outs[2:2 + nb]), token=outs[-1], plan=plan, n=n_copies)


def _split_wait(name, started, after):
    bufs = started["bufs"]
    nb = len(bufs)
    plan = started["plan"]

    def body(*refs):
        ins = refs[:nb]
        send_sems, recv_sems = refs[nb], refs[nb + 1]
        for i, (src, dst, dev) in enumerate(plan(ins)):
            cp = pltpu.make_async_remote_copy(src_ref=src, dst_ref=dst, send_sem=send_sems.at[i], recv_sem=recv_sems.at[i],
                                              device_id=dev, device_id_type=MESH)
            cp.wait_send()
            cp.wait_recv()

    outs = pl.pallas_call(
        body, name=name, out_shape=tuple(pltpu.HBM(b.shape, b.dtype) for b in bufs),
        in_specs=[HBM_SPEC] * nb + [SEM_SPEC, SEM_SPEC] + [ANY_SPEC] * len(after), out_specs=(HBM_SPEC,) * nb,
        input_output_aliases={i: i for i in range(nb)},
        compiler_params=pltpu.CompilerParams(has_side_effects=SIDE_EFFECT),
    )(*bufs, started["send"], started["recv"], *after)
    return list(outs)


def _gather_start(name, names, fulls, deps):
    n_t = len(names)

    def plan(refs):
        x, y, c = _my_place()
        my_idx = _dev_index(x, y, c)
        targets = [(x, y, 1 - c), (1 - x, y, c), (x, 1 - y, c), (1 - x, 1 - y, c)]
        slabs = [_slab2(KIND[names[t]], refs[t], my_idx) for t in range(n_t)]
        return [(slabs[t], slabs[t], dev) for t in range(n_t) for dev in targets]

    return _split_start(name, list(fulls), 4 * n_t, plan, deps)


def _gather_finish(name, names, started, after):
    n_t = len(names)
    fulls = _split_wait(name + "_wait", started, after)
    slab_shapes = [SDS(_slab_shape(KIND[n], f.shape), f.dtype) for n, f in zip(names, fulls)]

    def body(*refs):
        ins = refs[:n_t]
        outs = refs[n_t:2 * n_t]
        stage = refs[2 * n_t:3 * n_t]
        load_sems, send_sems, recv_sems = refs[3 * n_t:]
        x, y, c = _my_place()
        chips = [(1 - x, y), (x, 1 - y), (1 - x, 1 - y)]
        pairs = [(t, j) for t in range(n_t) for j in range(3)]
        loads = [pltpu.make_async_copy(_slab2(KIND[names[t]], ins[t], _dev_index(*chips[j], c)), stage[t].at[j], load_sems.at[t, j])
                 for t, j in pairs]
        for cp in loads:
            cp.start()

        def copy(t, j, core):
            return pltpu.make_async_remote_copy(
                src_ref=stage[t].at[j], dst_ref=_slab2(KIND[names[t]], outs[t], _dev_index(*chips[j], core)),
                send_sem=send_sems.at[t, j], recv_sem=recv_sems.at[t, j], device_id=(x, y, 1 - c), device_id_type=MESH)

        sends = [copy(t, j, c) for t, j in pairs]
        for ld, cp in zip(loads, sends):
            ld.wait()
            cp.start()
        for t, j in pairs:
            copy(t, j, 1 - c).wait_recv()
        for cp in sends:
            cp.wait_send()

    return pl.pallas_call(
        body, in_specs=[HBM_SPEC] * n_t, out_specs=[HBM_SPEC] * n_t, out_shape=[pltpu.HBM(b.shape, b.dtype) for b in fulls],
        input_output_aliases={t: t for t in range(n_t)},
        scratch_shapes=[pltpu.VMEM((3,) + s.shape, s.dtype) for s in slab_shapes]
        + [pltpu.SemaphoreType.DMA((n_t, 3)), pltpu.SemaphoreType.DMA((n_t, 3)), pltpu.SemaphoreType.DMA((n_t, 3))],
        name=name + "_pass", compiler_params=pltpu.CompilerParams(vmem_limit_bytes=VMEM_LIMIT),
    )(*fulls)


def _exchange_start(name, names, grads, deps):
    n_t = len(names)
    lands = [lax.empty((N_DEV,) + _slab_shape(KIND[n], g.shape), g.dtype) for n, g in zip(names, grads)]

    def plan(refs):
        my_idx = _dev_index(*_my_place())
        return [(_slab2(KIND[names[t]], refs[t], _dev_index(*peer)), refs[n_t + t].at[my_idx], peer)
                for t in range(n_t) for peer in _peers_all()]

    return _split_start(name, list(grads) + lands, 7 * n_t, plan, deps)


def _small_exchange_start(part, deps):
    land = lax.empty((N_DEV,) + part.shape, part.dtype)

    def plan(refs):
        my_idx = _dev_index(*_my_place())
        return [(refs[0], refs[1].at[my_idx], peer) for peer in _peers_all()]

    return _split_start("small_exchange", [part, land], N_DEV - 1, plan, deps)


def _slab_pieces():
    sh = D_IN // N_DEV
    out = []
    for j in range(N_DEV):
        for first, end, dst in IN_SEGMENTS:
            lo, hi = max(first, sh * j), min(end, sh * (j + 1))
            if lo < hi:
                out.append((j, lo - sh * j, hi - sh * j, dst + lo - first))
    return out


def _w_in_assemble(stacked):
    tr = 256
    sh = D_IN // N_DEV

    def body(i_ref, o_ref):
        o_ref[:, COL_DT:COL_XBC] = jnp.zeros((tr, COL_XBC - COL_DT), bf16)
        for j, lo, hi, dst in _slab_pieces():
            o_ref[:, dst:dst + hi - lo] = i_ref[j, :, lo:hi]

    return pl.pallas_call(
        body, grid=(D // tr,), in_specs=[pl.BlockSpec((N_DEV, tr, sh), lambda i: (0, i, 0))],
        out_specs=pl.BlockSpec((None, tr, D_IN_PAD), lambda i: (0, i, 0)), out_shape=SDS((1, D, D_IN_PAD), bf16),
        name="w_in_assemble", compiler_params=_cparams(1),
    )(*_in_hbm([stacked]))


def _w_in_slabs(dw_in):
    tr = 256
    sh = D_IN // N_DEV

    def body(i_ref, o_ref):
        for j, lo, hi, src in _slab_pieces():
            o_ref[j, :, lo:hi] = i_ref[:, src:src + hi - lo]

    return pl.pallas_call(
        body, grid=(D // tr,), in_specs=[pl.BlockSpec((tr, D_IN_PAD), lambda i: (i, 0))],
        out_specs=pl.BlockSpec((N_DEV, tr, sh), lambda i: (0, i, 0)), out_shape=_out_hbm(SDS((N_DEV, D, sh), bf16)),
        name="w_in_slabs", compiler_params=_cparams(1),
    )(*_in_hbm([dw_in]))


SMALL_NAMES = ("mix_norm_g", "mlp_norm_g", "conv_b", "ssm_norm_g", "q_gain", "k_gain", "sinks", "dt_bias", "a_log", "d_skip",
               "rel_bias", "conv_w")
MISC_LANES = dict(q_gain=(LANE_QG, HD), k_gain=(LANE_KG, HD), sinks=(LANE_SINK, NQ), dt_bias=(LANE_DTB, NSSM),
                  a_log=(LANE_ALOG, NSSM), d_skip=(LANE_DSKIP, NSSM))


def _pack_small_grads(smalls, drel_t, loss):
    def body(*refs):
        o_ref = refs[-1]
        drel_ref, loss_ref = refs[-3], refs[-2]
        o_ref[...] = jnp.zeros_like(o_ref)
        for l in range(DEPTH):
            mixg, mlpg, convb, convw, ssd, attn = refs[6 * l:6 * l + 6]
            o_ref[ROW_MIXG + l:ROW_MIXG + l + 1, :] = mixg[...]
            o_ref[ROW_MLPG + l:ROW_MLPG + l + 1, :] = mlpg[...]
            o_ref[ROW_CONVB + l:ROW_CONVB + l + 1, :] = convb[0:1, :]
            o_ref[ROW_SSMG + l:ROW_SSMG + l + 1, 0:D_SSM] = ssd[0:1, :]
            o_ref[ROW_CONVW + 4 * l:ROW_CONVW + 4 * l + 4, :] = convw[0:4, :]
            row = slice(ROW_MISC + l, ROW_MISC + l + 1)
            o_ref[row, LANE_QG:LANE_QG + HD] = attn[0:1, 0:HD]
            o_ref[row, LANE_KG:LANE_KG + HD] = attn[1:2, 0:HD]
            o_ref[row, LANE_SINK:LANE_SINK + NQ] = attn[2:3, 0:NQ]
            o_ref[row, LANE_DTB:LANE_DTB + NSSM] = ssd[1:2, 0:NSSM]
            o_ref[row, LANE_ALOG:LANE_ALOG + NSSM] = ssd[2:3, 0:NSSM]
            o_ref[row, LANE_DSKIP:LANE_DSKIP + NSSM] = ssd[3:4, 0:NSSM]
        o_ref[ROW_RELB:ROW_RELB + NQ, 0:N_BUCKETS] = drel_ref[...]
        o_ref[ROW_LOSS:ROW_LOSS + 1, 0:1] = loss_ref[0:1, 0:1]

    args = []
    for sm in smalls:
        args += [sm["mix_norm_g"], sm["mlp_norm_g"], sm["conv_b"], sm["conv_w"], sm["ssd"], sm["attn"]]
    args += [drel_t, loss]
    return pl.pallas_call(body, out_shape=SDS((SMALL_ROWS, D), f32), name="pack_small_grads")(*args)


def _adamw_small(part, land, w, m, v):
    n = len(SMALL_NAMES)

    def grad_of(name, g_ref):
        if name == "mix_norm_g":
            return g_ref[ROW_MIXG:ROW_MIXG + DEPTH, :]
        if name == "mlp_norm_g":
            return g_ref[ROW_MLPG:ROW_MLPG + DEPTH, :]
        if name == "conv_b":
            return g_ref[ROW_CONVB:ROW_CONVB + DEPTH, :]
        if name == "ssm_norm_g":
            return g_ref[ROW_SSMG:ROW_SSMG + DEPTH, 0:D_SSM]
        if name == "rel_bias":
            return g_ref[ROW_RELB:ROW_RELB + NQ, 0:N_BUCKETS].T
        lane, width = MISC_LANES[name]
        return g_ref[ROW_MISC:ROW_MISC + DEPTH, lane:lane + width]

    def body(part_ref, land_ref, *refs):
        ws, ms, vs = refs[:n], refs[n:2 * n], refs[2 * n:3 * n]
        loss_ref = refs[3 * n]
        outs = refs[3 * n + 1:-1]
        g_ref = refs[-1]
        me = _dev_index(*_my_place())
        for p in range(N_DEV):
            term = jnp.where(me == p, part_ref[...], land_ref[p])
            if p == 0:
                g_ref[...] = term
            else:
                g_ref[...] += term
        loss_ref[...] = g_ref[ROW_LOSS:ROW_LOSS + 1, 0:128]
        my_cols = pl.ds(pl.multiple_of(me * 128, 128), 128)
        for k, name in enumerate(SMALL_NAMES):
            g_out, d_out, m_out, v_out = outs[4 * k:4 * k + 4]
            if name == "conv_w":
                for l in range(DEPTH):
                    g = g_ref[ROW_CONVW + 4 * l:ROW_CONVW + 4 * l + 4, my_cols]
                    delta, m_new, v_new = _adamw_math(ws[k][l], ms[k][l], vs[k][l], g)
                    g_out[l], d_out[l], m_out[l], v_out[l] = g, delta, m_new, v_new
            else:
                g = grad_of(name, g_ref)
                delta, m_new, v_new = _adamw_math(ws[k][...], ms[k][...], vs[k][...], g)
                g_out[...], d_out[...], m_out[...], v_out[...] = g, delta, m_new, v_new

    ws = [w[name] for name in SMALL_NAMES]
    out_shape = [SDS((1, 128), f32)]
    for a in ws:
        out_shape += [SDS(a.shape, f32)] * 4
    return pl.pallas_call(body, out_shape=out_shape, name="adamw_small", scratch_shapes=[pltpu.VMEM((SMALL_ROWS, D), f32)])(
        part, land, *ws, *[m[name] for name in SMALL_NAMES], *[v[name] for name in SMALL_NAMES])


def _plain(tm, tn):
    return pl.BlockSpec((tm, tn), lambda i, j, k: (i, j))


def _rowblk(tm, width):
    return pl.BlockSpec((tm, width), lambda i, j, k: (i, 0))


def _store_epi(dtype):
    def epi(acc, i, j, ex, outs):
        outs[0][...] = acc.astype(dtype)
    return epi


def _rms_prologue(layer):
    def pro(a_ref, ex, outs):
        xv = a_ref[...]
        r = lax.rsqrt(jnp.mean(xv * xv, axis=-1, keepdims=True) + EPS)
        h = (xv * r * ex[0][layer:layer + 1, :]).astype(bf16)
        outs[-1][...] = h
        return h
    return pro


MLP_TM = 256
MLP_VMEM = 56 * 1024 * 1024


def _resident(shape):
    return pl.BlockSpec((None,) + shape, lambda i: (0, 0, 0), pipeline_mode=pl.Buffered(1))


def _mlp_fwd(layer, x, mix, g, w_out, w_up, w_down, tgt=None):
    tm = MLP_TM
    with_loss = tgt is not None

    def body(x_ref, mix_ref, g_ref, wo_ref, wu_ref, wd_ref, *rest):
        xm_ref, a_ref, r_ref, h_ref = rest[with_loss:with_loss + 4]
        rest = rest[:with_loss] + rest[with_loss + 1:]
        i = pl.program_id(0)
        xv = x_ref[...] + _dot(mix_ref[...], wo_ref[...], NN_DIMS)
        xm_ref[...] = xv
        h = (xv * lax.rsqrt(jnp.mean(xv * xv, axis=-1, keepdims=True) + EPS) * g_ref[layer:layer + 1, :]).astype(bf16)
        h_ref[...] = h
        r = jnp.maximum(_dot(h, wu_ref[...], NN_DIMS), 0.0)
        a = (r * r).astype(bf16)
        a_ref[...] = a
        r_ref[...] = r.astype(bf16)
        y = xv + _dot(a, wd_ref[...], NN_DIMS)
        if not with_loss:
            rest[3][...] = y
            return
        err = y - rest[0][...]
        rest[4][...] = err * (1.0 / D)
        part = 0.5 * jnp.sum(jnp.mean(err * err, axis=-1, keepdims=True), axis=0, keepdims=True)

        @pl.when(i == 0)
        def _():
            rest[5][...] = jnp.zeros_like(rest[5])

        rest[5][...] += jnp.broadcast_to(part, rest[5].shape)

    row = lambda width: pl.BlockSpec((tm, width), lambda i: (i, 0))
    in_specs = [row(D), row(D), pl.BlockSpec((DEPTH, D), lambda i: (0, 0)), _resident((D, D)), _resident((D, D_FF)),
                _resident((D_FF, D))]
    out_specs = [row(D), row(D_FF), row(D_FF), row(D), row(D)]
    out_shape = [SDS((S, D), f32), SDS((S, D_FF), bf16), SDS((S, D_FF), bf16), SDS((S, D), bf16), SDS((S, D), f32)]
    args = [x, mix, g, w_out, w_up, w_down]
    if with_loss:
        in_specs.append(row(D))
        args.append(tgt)
        out_specs.append(pl.BlockSpec((1, 128), lambda i: (0, 0)))
        out_shape.append(SDS((1, 128), f32))
    return pl.pallas_call(
        body, grid=(S // tm,), in_specs=in_specs, out_specs=out_specs, out_shape=_out_hbm(out_shape),
        name="mlp_fwd_loss" if with_loss else "mlp_fwd",
        compiler_params=pltpu.CompilerParams(dimension_semantics=("arbitrary",), vmem_limit_bytes=MLP_VMEM),
    )(*_in_hbm(args[:3]), *args[3:6], *_in_hbm(args[6:]))


def _mlp_bwd_act(layer, dx_out, r_act, x_mid, g, w_down, w_up, w_out, deps):
    tm = MLP_TM

    def body(dxo_ref, r_ref, xm_ref, g_ref, wd_ref, wu_ref, wo_ref, *rest):
        du_ref, dx_ref, dg_ref, dmix_ref = rest[len(deps):]
        dxo = dxo_ref[...]
        du = (_dot(dxo.astype(bf16), wd_ref[...], NT_DIMS) * (2.0 * r_ref[...].astype(f32))).astype(bf16)
        du_ref[...] = du
        dh = _dot(du, wu_ref[...], NT_DIMS)
        _rms_bwd_epilogue(layer)(dh, pl.program_id(0), 0, (xm_ref, g_ref, dxo_ref), (dx_ref, dg_ref))
        dmix_ref[...] = _dot(dx_ref[...].astype(bf16), wo_ref[...], NT_DIMS)

    row = lambda width: pl.BlockSpec((tm, width), lambda i: (i, 0))
    return pl.pallas_call(
        body, grid=(S // tm,),
        in_specs=[row(D), row(D_FF), row(D), pl.BlockSpec((DEPTH, D), lambda i: (0, 0)), _resident((D_FF, D)), _resident((D, D_FF)),
                  _resident((D, D))] + [ANY_SPEC] * len(deps),
        out_specs=[row(D_FF), row(D), pl.BlockSpec((1, D), lambda i: (0, 0)), row(D)],
        out_shape=_out_hbm([SDS((S, D_FF), bf16), SDS((S, D), f32), SDS((1, D), f32), SDS((S, D), f32)]), name="mlp_bwd_act",
        compiler_params=pltpu.CompilerParams(dimension_semantics=("arbitrary",), vmem_limit_bytes=MLP_VMEM),
    )(*_in_hbm([dx_out, r_act, x_mid, g]), w_down, w_up, w_out, *_in_hbm(deps))


def _layer_fwd(l, x, p, get_weights, bias, tgt=None):
    wts = get_weights(l, "in", [x, bias])
    gfull = pl.BlockSpec((DEPTH, D), lambda i, j, k: (0, 0))
    tm = 512

    def inproj_epi(acc, i, j, ex, outs):
        outs[0][...] = acc[:, COL_QKV:COL_Z].astype(bf16)
        outs[1][...] = acc[:, COL_Z:COL_DT].astype(bf16)
        outs[2][...] = acc[:, COL_XBC:D_IN_PAD].astype(bf16)
        outs[3][...] = acc[:, COL_DT:COL_DT + 128]

    qkv, z, xbc, dt, h1 = _matmul(
        "in_proj", "nn", x, wts["w_in"], tm=tm, tn=D_IN_PAD, tk=D, prologue=_rms_prologue(l),
        extras=(p["mix_norm_g"],), extra_specs=(gfull,),
        out_shape=[SDS((S, 768), bf16), SDS((S, 512), bf16), SDS((S, 1024), bf16), SDS((S, 128), f32), SDS((S, D), bf16)],
        out_specs=[_rowblk(tm, 768), _rowblk(tm, 512), _rowblk(tm, 1024), _rowblk(tm, 128), _rowblk(tm, D)], epilogue=inproj_epi)
    attn = _attn_fwd(qkv, p["q_gain"], p["k_gain"], p["sinks"], bias, l)
    xact = _conv_fwd(xbc, wts["conv_w"], p["conv_b"], l)
    mix, hs, y_ssd = _ssd_fwd(xact, z, dt, attn, p["dt_bias"], p["a_log"], p["d_skip"], p["ssm_norm_g"], l)
    wts = dict(wts, **get_weights(l, "rest", [mix]))

    x_mid, a_act, r_act, h2, *result = _mlp_fwd(l, x, mix, p["mlp_norm_g"], wts["w_out"], wts["w_up"], wts["w_down"], tgt)
    saved = dict(x=x, h1=h1, qkv=qkv, z=z, xbc=xbc, dt=dt, xact=xact, mix=mix, hs=hs, y_ssd=y_ssd, x_mid=x_mid, h2=h2,
                 a=a_act, r=r_act, wts=wts)
    return (result[0] if tgt is None else tuple(result)), saved


def _layer_bwd(l, dx_out, sv, p, bias, deps, send):
    wts = sv["wts"]

    dw_down = _matmul("dw_down", "tn", sv["a"], dx_out, tm=512, tn=D, tk=S, out_shape=SDS((D_FF, D), bf16),
                      out_specs=_plain(512, D), epilogue=_store_epi(bf16), deps=deps)
    deps = send(l, dict(w_down=dw_down))
    du, dx_mid, dg_mlp, dmix = _mlp_bwd_act(l, dx_out, sv["r"], sv["x_mid"], p["mlp_norm_g"], wts["w_down"], wts["w_up"],
                                            wts["w_out"], deps)
    dw_up = _matmul("dw_up", "tn", sv["h2"], du, tm=D, tn=512, tk=S, out_shape=SDS((D, D_FF), bf16),
                    out_specs=_plain(D, 512), epilogue=_store_epi(bf16))
    dw_out = _matmul("dw_out", "tn", sv["mix"], dx_mid, tm=D, tn=512, tk=S, out_shape=SDS((D, D), bf16),
                     out_specs=_plain(D, 512), epilogue=_store_epi(bf16))
    deps = send(l, dict(w_up=dw_up, w_out=dw_out))
    gfull = pl.BlockSpec((DEPTH, D), lambda i, j, k: (0, 0))
    grow = pl.BlockSpec((1, D), lambda i, j, k: (0, 0))
    dproj, dbias, dsm_attn = _attn_bwd(sv["qkv"], dmix, p["q_gain"], p["k_gain"], p["sinks"], bias, l, deps)
    dproj, dxact, dsm_ssd = _ssd_bwd(sv["xact"], sv["z"], sv["dt"], dmix, sv["hs"], sv["y_ssd"], p["dt_bias"], p["a_log"],
                                     p["d_skip"], p["ssm_norm_g"], dproj, l)
    dproj, dconv_w, dconv_b = _conv_bwd(sv["xbc"], dxact, wts["conv_w"], p["conv_b"], dproj, l)
    dw_in = _matmul("dw_in", "tn", sv["h1"], dproj, tm=D, tn=1280, tk=S, out_shape=SDS((D, D_IN_PAD), bf16),
                    out_specs=_plain(D, 1280), epilogue=_store_epi(bf16))
    deps = send(l, dict(w_in=_w_in_slabs(dw_in)))
    dx, dg_mix = _matmul(
        "in_proj_dh", "nt", dproj, wts["w_in"], tm=512, tn=D, tk=D_IN_PAD, out_shape=[SDS((S, D), f32), SDS((1, D), f32)],
        out_specs=[_plain(512, D), grow], epilogue=_rms_bwd_epilogue(l),
        extras=(sv["x"], p["mix_norm_g"], dx_mid), extra_specs=(_plain(512, D), gfull, _plain(512, D)), deps=deps)
    small = dict(mix_norm_g=dg_mix, mlp_norm_g=dg_mlp, conv_w=dconv_w, conv_b=dconv_b, ssd=dsm_ssd, attn=dsm_attn, dbias=dbias)
    return dx, small, deps


def _local_step(x, tgt, p, get_weights, send):
    onehot_t = jnp.asarray(_onehot_buckets(), dtype=bf16)
    bias = _bias_build(p["rel_bias"].T, onehot_t).reshape(NQ, BLK, 2 * BLK)
    saved = []
    h = x
    for l in range(DEPTH):
        h, sv = _layer_fwd(l, h, p, get_weights, bias, tgt if l == DEPTH - 1 else None)
        saved.append(sv)
    dx, loss = h
    smalls = [None] * DEPTH
    deps = ()
    for l in reversed(range(DEPTH)):
        dx, smalls[l], deps = _layer_bwd(l, dx, saved[l], p, bias, deps, send)
    drel_t = _bias_grad(smalls[0]["dbias"].reshape(NQ, -1), smalls[1]["dbias"].reshape(NQ, -1), onehot_t)
    return dx, _pack_small_grads(smalls, drel_t, loss)


WEIGHT_ORDER = ("mix_norm_g", "w_in", "q_gain", "k_gain", "sinks", "rel_bias", "conv_w", "conv_b", "dt_bias", "a_log", "d_skip",
                "ssm_norm_g", "w_out", "mlp_norm_g", "w_up", "w_down")


def kernel(x, mix_norm_g, w_in, q_gain, k_gain, sinks, rel_bias, conv_w, conv_b, dt_bias, a_log, d_skip, ssm_norm_g, w_out, mlp_norm_g, w_up, w_down, loss_target, m_mix_norm_g, m_w_in, m_q_gain, m_k_gain, m_sinks, m_rel_bias, m_conv_w, m_conv_b, m_dt_bias, m_a_log, m_d_skip, m_ssm_norm_g, m_w_out, m_mlp_norm_g, m_w_up, m_w_down, v_mix_norm_g, v_w_in, v_q_gain, v_k_gain, v_sinks, v_rel_bias, v_conv_w, v_conv_b, v_dt_bias, v_a_log, v_d_skip, v_ssm_norm_g, v_w_out, v_mlp_norm_g, v_w_up, v_w_down):
    w = dict(mix_norm_g=mix_norm_g, w_in=w_in, q_gain=q_gain, k_gain=k_gain, sinks=sinks, rel_bias=rel_bias, conv_w=conv_w,
             conv_b=conv_b, dt_bias=dt_bias, a_log=a_log, d_skip=d_skip, ssm_norm_g=ssm_norm_g, w_out=w_out,
             mlp_norm_g=mlp_norm_g, w_up=w_up, w_down=w_down)
    m = dict(mix_norm_g=m_mix_norm_g, w_in=m_w_in, q_gain=m_q_gain, k_gain=m_k_gain, sinks=m_sinks, rel_bias=m_rel_bias,
             conv_w=m_conv_w, conv_b=m_conv_b, dt_bias=m_dt_bias, a_log=m_a_log, d_skip=m_d_skip, ssm_norm_g=m_ssm_norm_g,
             w_out=m_w_out, mlp_norm_g=m_mlp_norm_g, w_up=m_w_up, w_down=m_w_down)
    v = dict(mix_norm_g=v_mix_norm_g, w_in=v_w_in, q_gain=v_q_gain, k_gain=v_k_gain, sinks=v_sinks, rel_bias=v_rel_bias,
             conv_w=v_conv_w, conv_b=v_conv_b, dt_bias=v_dt_bias, a_log=v_a_log, d_skip=v_d_skip, ssm_norm_g=v_ssm_norm_g,
             w_out=v_w_out, mlp_norm_g=v_mlp_norm_g, w_up=v_w_up, w_down=v_w_down)
    big = ("w_in", "w_out", "w_up", "w_down")

    my_idx = _dev_index(*_my_place()).astype(jnp.int32).reshape(1)

    fulls = {n: _cast_to_full("cast_" + n, w[n], KIND[n], FULL_SHAPE[n], my_idx, bf16) for n in big}
    conv_full = _cast_to_full("cast_conv_w", conv_w.reshape(1, DEPTH * 4, 128), "stack", (N_DEV, DEPTH * 4, 128), my_idx, f32)[0]
    rest = ["w_out", "w_up", "w_down"]
    g0 = _gather_start("gather0", ["w_in", "conv_w"], [fulls["w_in"][0], conv_full], ())
    g1 = _gather_start("gather1", rest, [fulls[n][0] for n in rest], (g0["token"],))
    g2 = _gather_start("gather2", ["w_in"], [fulls["w_in"][1]], (g1["token"],))
    g3 = _gather_start("gather3", rest, [fulls[n][1] for n in rest], (g2["token"],))
    held = {}
    flat = lambda a: a.reshape(a.shape[0] * a.shape[1], a.shape[2])
    adam_in = {n: (flat(w[n]), flat(m[n]), flat(v[n])) for n in big}

    def get_weights(l, part, after):
        if l == 0 and part == "in":
            full_in, full_conv = _gather_finish("gather0", ["w_in", "conv_w"], g0,
                                                list(after) + [g3["token"], adam_in["w_in"][1], adam_in["w_in"][2]])
            held["conv_w"] = jnp.transpose(full_conv.reshape(N_DEV, DEPTH, 4, 128), (1, 2, 0, 3)).reshape(DEPTH, 4, D_CONV)
            return dict(w_in=_w_in_assemble(full_in), conv_w=held["conv_w"])
        if part == "in":
            return dict(w_in=_w_in_assemble(_gather_finish("gather2", ["w_in"], g2, after)[0]), conv_w=held["conv_w"])
        full = _gather_finish("gather1" if l == 0 else "gather3", rest, g1 if l == 0 else g3, after)
        return {n: f[None] for n, f in zip(rest, full)}

    pending = []

    def send(l, grads):
        names = list(grads)
        started = _exchange_start("exchange%d_%s" % (l, names[0]), names, [grads[n] for n in names], ())
        pending.append((l, names, started))
        return (started["token"],)

    dx, small_part = _local_step(x.reshape(S, D), loss_target.reshape(S, D), w, get_weights, send)

    small = _small_exchange_start(small_part, ())
    tiles = dict(w_in=512, w_out=128, w_up=512, w_down=256)
    outs_of = {n: None for n in big}
    after = [dx, small["token"]]
    for l, names, started in pending:
        bufs = _split_wait("exchange%d_%s_wait" % (l, names[0]), started, after)
        for t, n in enumerate(names):
            outs_of[n] = _adamw_layer("adamw_%s%d" % (n, l), KIND[n], l, *adam_in[n],
                                      bufs[len(names) + t], bufs[t], my_idx, outs_of[n], tiles[n])
        after = [outs_of[names[-1]][0]]
    res = {n: [o.reshape(w[n].shape) for o in outs_of[n]] for n in big}
    small_part, small_land = _split_wait("small_exchange_wait", small, after)
    small_outs = _adamw_small(small_part, small_land, w, m, v)
    loss = small_outs[0][0, 0]
    for k, name in enumerate(SMALL_NAMES):
        res[name] = small_outs[1 + 4 * k:5 + 4 * k]

    result = [loss, dx.reshape(1, S, D)]
    for k in range(4):
        result += [res[name][k] for name in WEIGHT_ORDER]
    return tuple(result)
```

```python
import functools
import math

import numpy as np
import jax
import jax.numpy as jnp
from jax import lax
from jax.experimental import pallas as pl
from jax.experimental.pallas import tpu as pltpu

f32 = jnp.float32
bf16 = jnp.bfloat16
SDS = jax.ShapeDtypeStruct
MESH = pl.DeviceIdType.MESH
HIGHEST = lax.Precision.HIGHEST

S = 2048
D = 1024
DEPTH = 2
BLK = 128
NBLK = S // BLK
HD = 64
NQ = 8
NKV = 2
NSSM = 8
NGRP = 2
NSTATE = 128
D_ATTN = 512
D_SSM = 512
D_CONV = 1024
D_FF = 4096
D_IN = 2312
D_IN_PAD = 2560
COL_QKV, COL_Z, COL_DT, COL_XBC = 0, 768, 1280, 1536
IN_SEGMENTS = ((0, 1280, 0), (1280, 2304, COL_XBC), (2304, 2312, COL_DT))
N_BUCKETS = 32
EPS = 1e-6
N_DEV = 8
VMEM_LIMIT = 48 * 1024 * 1024

ADAM_LR = 0.001
ADAM_B1 = 0.9
ADAM_B2 = 0.999
ADAM_EPS = 1e-08
ADAM_WD = 0.01
ADAM_STEP = 10

NT_DIMS = (((1,), (1,)), ((), ()))
TN_DIMS = (((0,), (0,)), ((), ()))
NN_DIMS = (((1,), (0,)), ((), ()))

ROW_MIXG = 0
ROW_MLPG = 2
ROW_CONVB = 4
ROW_SSMG = 6
ROW_MISC = 8
ROW_RELB = 10
ROW_CONVW = 18
ROW_LOSS = 26
SMALL_ROWS = 32
LANE_QG, LANE_KG, LANE_SINK, LANE_DTB, LANE_ALOG, LANE_DSKIP = 0, 64, 128, 256, 384, 512


def _dot(a, b, dims):
    return lax.dot_general(a, b, dims, preferred_element_type=f32)


def _cparams(n_axes):
    return pltpu.CompilerParams(dimension_semantics=("arbitrary",) * n_axes, vmem_limit_bytes=VMEM_LIMIT)


def _sum11(v):
    return jnp.sum(jnp.sum(v, axis=1, keepdims=True), axis=0, keepdims=True)


def _sigmoid(v):
    return 1.0 / (1.0 + jnp.exp(-v))


ANY_SPEC = pl.BlockSpec(memory_space=pl.ANY)


def _in_hbm(args):
    return [pltpu.with_memory_space_constraint(a, pltpu.HBM) if a.size >= 65536 else a for a in args]


def _out_hbm(out_shape):
    one = lambda s: pltpu.HBM(s.shape, s.dtype) if math.prod(s.shape) >= 65536 else s
    return [one(s) for s in out_shape] if isinstance(out_shape, (list, tuple)) else one(out_shape)


def _matmul(name, mode, a, b, *, layer=0, tm, tn, tk, out_shape, out_specs, epilogue, extras=(), extra_specs=(), deps=(),
            prologue=None, pin_out=True):
    extras = tuple(extras) + tuple(deps)
    extra_specs = tuple(extra_specs) + (ANY_SPEC,) * len(deps)
    if mode == "tn":
        t_dim, m_dim = a.shape
        n_dim = b.shape[1]
        grid = (m_dim // tm, n_dim // tn, t_dim // tk)
        a_spec = pl.BlockSpec((tk, tm), lambda i, j, k: (k, i))
        b_spec = pl.BlockSpec((tk, tn), lambda i, j, k: (k, j))
        dims = TN_DIMS
    elif mode == "nn":
        m_dim, k_dim = a.shape
        n_dim = b.shape[-1]
        grid = (m_dim // tm, n_dim // tn, k_dim // tk)
        a_spec = pl.BlockSpec((tm, tk), lambda i, j, k: (i, k))
        b_spec = pl.BlockSpec((None, tk, tn), lambda i, j, k: (layer, k, j))
        dims = NN_DIMS
    else:
        m_dim, k_dim = a.shape
        n_dim = b.shape[-2]
        grid = (m_dim // tm, n_dim // tn, k_dim // tk)
        a_spec = pl.BlockSpec((tm, tk), lambda i, j, k: (i, k))
        b_spec = pl.BlockSpec((None, tn, tk), lambda i, j, k: (layer, j, k))
        dims = NT_DIMS
    nk = grid[2]
    n_ex = len(extras)

    def body(a_ref, b_ref, *rest):
        ex = rest[:n_ex - len(deps)]
        outs = rest[n_ex:-1]
        acc = rest[-1]
        i = pl.program_id(0)
        j = pl.program_id(1)
        k = pl.program_id(2)
        lhs = a_ref[...].astype(bf16) if prologue is None else prologue(a_ref, ex, outs)
        part = _dot(lhs, b_ref[...].astype(bf16), dims)
        if nk == 1:
            epilogue(part, i, j, ex, outs)
        else:
            @pl.when(k == 0)
            def _():
                acc[...] = part

            @pl.when(k > 0)
            def _():
                acc[...] += part

            @pl.when(k == nk - 1)
            def _():
                epilogue(acc[...], i, j, ex, outs)

    return pl.pallas_call(
        body, grid=grid, in_specs=[a_spec, b_spec, *extra_specs], out_specs=out_specs,
        out_shape=_out_hbm(out_shape) if pin_out else out_shape,
        scratch_shapes=[pltpu.VMEM((tm, tn) if nk > 1 else (8, 128), f32)], name=name, compiler_params=_cparams(3),
    )(*_in_hbm([a]), b, *_in_hbm(extras))


def _rms_bwd_epilogue(layer):
    def epi(acc, i, j, ex, outs):
        x_ref, g_ref, dres_ref = ex
        dx_ref, dg_ref = outs
        xv = x_ref[...]
        r = lax.rsqrt(jnp.mean(xv * xv, axis=-1, keepdims=True) + EPS)
        xhat = xv * r
        w = acc * g_ref[layer:layer + 1, :]
        dx_ref[...] = dres_ref[...] + r * (w - xhat * jnp.mean(xhat * w, axis=-1, keepdims=True))
        dg = jnp.sum(acc * xhat, axis=0, keepdims=True)

        @pl.when(i == 0)
        def _():
            dg_ref[...] = dg

        @pl.when(i > 0)
        def _():
            dg_ref[...] += dg
    return epi


def _own_slab_spec(kind, tr, cols, nblk):
    if kind == "stack":
        return pl.BlockSpec((None, tr, cols), lambda i, idx: (idx[0], i, 0))
    if kind == "cols512":
        return pl.BlockSpec((tr, cols), lambda i, idx: (i, idx[0]))
    return pl.BlockSpec((tr, cols), lambda i, idx: (idx[0] * nblk + i, 0))


def _cast_to_full(name, w, kind, full_shape, my_idx, dtype):
    n_layers, rows, cols = w.shape
    tr = min(rows, 256)
    nblk = rows // tr

    def body(idx_ref, w_ref, *o_refs):
        for l in range(n_layers):
            o_refs[l][...] = w_ref[l].astype(dtype)

    grid_spec = pltpu.PrefetchScalarGridSpec(
        num_scalar_prefetch=1, grid=(nblk,), in_specs=[pl.BlockSpec((n_layers, tr, cols), lambda i, idx: (0, i, 0))],
        out_specs=[_own_slab_spec(kind, tr, cols, nblk)] * n_layers)
    return pl.pallas_call(body, grid_spec=grid_spec, out_shape=_out_hbm([SDS(full_shape, dtype)] * n_layers), name=name,
                          compiler_params=_cparams(1))(*_in_hbm([my_idx, w]))


def _adamw_math(w, m, v, g):
    m_new = ADAM_B1 * m + (1.0 - ADAM_B1) * g
    v_new = ADAM_B2 * v + (1.0 - ADAM_B2) * (g * g)
    m_hat = m_new / (1.0 - ADAM_B1 ** ADAM_STEP)
    v_hat = v_new / (1.0 - ADAM_B2 ** ADAM_STEP)
    delta = -ADAM_LR * (m_hat / (jnp.sqrt(v_hat) + ADAM_EPS) + ADAM_WD * w)
    return delta, m_new, v_new


def _adamw_layer(name, kind, layer, w, m, v, land, g_full, my_idx, prev, tr):
    rows2, cols = w.shape
    rows = rows2 // DEPTH
    nblk = rows // tr
    own_spec = _own_slab_spec(kind, tr, cols, nblk)
    n_prev = 0 if prev is None else 4

    def body(idx_ref, w_ref, m_ref, v_ref, land_ref, own_ref, *rest):
        g_ref, d_ref, mo_ref, vo_ref = rest[n_prev:]
        me = idx_ref[0]
        g = None
        for p in range(N_DEV):
            part = jnp.where(me == p, own_ref[...], land_ref[p]).astype(f32)
            g = part if g is None else g + part
        delta, m_new, v_new = _adamw_math(w_ref[...], m_ref[...], v_ref[...], g)
        g_ref[...] = g
        d_ref[...] = delta
        mo_ref[...] = m_new
        vo_ref[...] = v_new

    blk = pl.BlockSpec((tr, cols), lambda i, idx: (layer * nblk + i, 0))
    grid_spec = pltpu.PrefetchScalarGridSpec(
        num_scalar_prefetch=1, grid=(nblk,),
        in_specs=[blk, blk, blk, pl.BlockSpec((N_DEV, tr, cols), lambda i, idx: (0, i, 0)), own_spec] + [ANY_SPEC] * n_prev,
        out_specs=[blk, blk, blk, blk])
    aliases = {} if prev is None else {6 + k: k for k in range(4)}
    return pl.pallas_call(
        body, grid_spec=grid_spec, out_shape=_out_hbm([SDS((rows2, cols), f32)] * 4), name=name, input_output_aliases=aliases,
        compiler_params=_cparams(1),
    )(*_in_hbm([my_idx, w, m, v, land, g_full, *([] if prev is None else prev)]))


def _bucket_table():
    qi = np.arange(BLK)[:, None]
    kj = np.arange(2 * BLK)[None, :]
    dist = qi + BLK - kj
    dcl = np.clip(dist, 0, None)
    max_exact = N_BUCKETS // 2
    d_f = np.maximum(dcl, 1).astype(np.float32)
    large = max_exact + (np.log(d_f / np.float32(max_exact)) / np.float32(math.log(128 / max_exact))
                         * np.float32(N_BUCKETS - max_exact)).astype(np.int32)
    large = np.minimum(large, N_BUCKETS - 1)
    bucket = np.where(dcl < max_exact, dcl, large)
    in_window = (dist >= 0) & (dist < BLK)
    return bucket.astype(np.int32), in_window


def _onehot_buckets():
    bucket, _ = _bucket_table()
    oh = (bucket.reshape(-1)[None, :] == np.arange(N_BUCKETS)[:, None]).astype(np.float32)
    return oh


def _bias_build(rel_bias_t, onehot_t):
    def body(r_ref, o_ref, out_ref):
        r = r_ref[...]
        hi = r.astype(bf16)
        r1 = r - hi.astype(f32)
        mid = r1.astype(bf16)
        lo = (r1 - mid.astype(f32)).astype(bf16)
        oh = o_ref[...]
        out_ref[...] = _dot(hi, oh, NN_DIMS) + _dot(mid, oh, NN_DIMS) + _dot(lo, oh, NN_DIMS)

    tn = 4096
    return pl.pallas_call(
        body, grid=(BLK * 2 * BLK // tn,),
        in_specs=[pl.BlockSpec((NQ, N_BUCKETS), lambda i: (0, 0)), pl.BlockSpec((N_BUCKETS, tn), lambda i: (0, i))],
        out_specs=pl.BlockSpec((NQ, tn), lambda i: (0, i)), out_shape=SDS((NQ, BLK * 2 * BLK), f32), name="bias_build",
        compiler_params=_cparams(1),
    )(rel_bias_t, onehot_t)


def _bias_grad(dbias0, dbias1, onehot_t):
    tn = 4096
    nsteps = BLK * 2 * BLK // tn

    def body(a_ref, b_ref, o_ref, out_ref):
        g = a_ref[...] + b_ref[...]
        hi = g.astype(bf16)
        lo = (g - hi.astype(f32)).astype(bf16)
        part = _dot(hi, o_ref[...], NT_DIMS) + _dot(lo, o_ref[...], NT_DIMS)

        @pl.when(pl.program_id(0) == 0)
        def _():
            out_ref[...] = part

        @pl.when(pl.program_id(0) > 0)
        def _():
            out_ref[...] += part

    return pl.pallas_call(
        body, grid=(nsteps,),
        in_specs=[pl.BlockSpec((NQ, tn), lambda i: (0, i)), pl.BlockSpec((NQ, tn), lambda i: (0, i)),
                  pl.BlockSpec((N_BUCKETS, tn), lambda i: (0, i))],
        out_specs=pl.BlockSpec((NQ, N_BUCKETS), lambda i: (0, 0)), out_shape=SDS((NQ, N_BUCKETS), f32), name="bias_grad",
        compiler_params=_cparams(1),
    )(dbias0, dbias1, onehot_t)


def _attn_mask(n):
    qi = lax.broadcasted_iota(jnp.int32, (BLK, 2 * BLK), 0)
    kj = lax.broadcasted_iota(jnp.int32, (BLK, 2 * BLK), 1)
    dist = qi + BLK - kj
    first_key = jnp.where(n > 0, 0, BLK)
    return (dist >= 0) & (dist < BLK) & (kj >= first_key)


def _row_mean(a):
    return jnp.mean(a, axis=-1, keepdims=True)


def _head_norm(t, gain):
    r = lax.rsqrt(_row_mean(t * t) + EPS)
    that = t * r
    return that, r, that * gain


def _softmax_with_sink(s, sink):
    m = jnp.maximum(jnp.max(s, axis=-1, keepdims=True), sink)
    p = jnp.exp(s - m)
    psink = jnp.exp(sink - m)
    inv = 1.0 / (jnp.sum(p, axis=-1, keepdims=True) + psink)
    return p * inv, psink * inv


GQ = NQ // NKV


def _attn_fwd(qkv, q_gain, k_gain, sinks, bias, layer):
    def body(q_ref, kc_ref, kp_ref, vc_ref, vp_ref, qg_ref, kg_ref, sk_ref, bias_ref, o_ref):
        m = pl.program_id(0)
        qg = qg_ref[layer:layer + 1, :]
        kg = kg_ref[layer:layer + 1, :]
        grp = range(NKV)
        chains = [(b, j) for b in range(2) for j in grp]
        masks = [jnp.tile(_attn_mask(2 * m + b), (GQ, 1)) for b in range(2)]
        kblk = [[kp_ref[:, pl.ds(HD * j, HD)].astype(f32), kc_ref[0:BLK, pl.ds(HD * j, HD)].astype(f32),
                 kc_ref[BLK:, pl.ds(HD * j, HD)].astype(f32)] for j in grp]
        vblk = [[vp_ref[:, pl.ds(HD * j, HD)].astype(bf16), vc_ref[0:BLK, pl.ds(HD * j, HD)].astype(bf16),
                 vc_ref[BLK:, pl.ds(HD * j, HD)].astype(bf16)] for j in grp]
        knb = [[_head_norm(kblk[j][t], kg)[2].astype(bf16) for t in range(3)] for j in grp]
        kn_b = {(b, j): jnp.concatenate([knb[j][b], knb[j][b + 1]], axis=0) for b, j in chains}
        vbs = {(b, j): jnp.concatenate([vblk[j][b], vblk[j][b + 1]], axis=0) for b, j in chains}
        rows = {}
        for b, j in chains:
            heads = [GQ * j + g for g in range(GQ)]
            rows[b, j] = (jnp.concatenate([q_ref[pl.ds(BLK * b, BLK), pl.ds(HD * h, HD)] for h in heads], axis=0).astype(f32),
                          jnp.concatenate([jnp.broadcast_to(sk_ref[layer:layer + 1, h:h + 1], (BLK, 1)) for h in heads], axis=0))
        qn_b = {c: _head_norm(rows[c][0], qg)[2].astype(bf16) for c in chains}
        ss = {(b, j): _dot(qn_b[b, j], kn_b[b, j], NT_DIMS) * (HD ** -0.5) + bias_ref[GQ * j:GQ * (j + 1)].reshape(GQ * BLK, 2 * BLK)
              for b, j in chains}
        ps = {(b, j): _softmax_with_sink(jnp.where(masks[b], ss[b, j], -jnp.inf), rows[b, j][1])[0] for b, j in chains}
        outs = {c: _dot(ps[c].astype(bf16), vbs[c], NN_DIMS).astype(bf16) for c in chains}
        for b, j in chains:
            for g in range(GQ):
                o_ref[pl.ds(BLK * b, BLK), pl.ds(HD * (GQ * j + g), HD)] = outs[b, j][BLK * g:BLK * (g + 1), :]

    prev = lambda m: jnp.maximum(2 * m - 1, 0)
    small = lambda shape: pl.BlockSpec(shape, lambda m: (0,) * len(shape))
    return pl.pallas_call(
        body, grid=(NBLK // 2,),
        in_specs=[pl.BlockSpec((2 * BLK, D_ATTN), lambda m: (m, 0)),
                  pl.BlockSpec((2 * BLK, 128), lambda m: (m, 4)), pl.BlockSpec((BLK, 128), lambda m: (prev(m), 4)),
                  pl.BlockSpec((2 * BLK, 128), lambda m: (m, 5)), pl.BlockSpec((BLK, 128), lambda m: (prev(m), 5)),
                  small((DEPTH, HD)), small((DEPTH, HD)), small((DEPTH, NQ)), small((NQ, BLK, 2 * BLK))],
        out_specs=pl.BlockSpec((2 * BLK, D_ATTN), lambda m: (m, 0)), out_shape=_out_hbm(SDS((S, D_ATTN), bf16)),
        name="attn_fwd", compiler_params=_cparams(1),
    )(*_in_hbm([qkv, qkv, qkv, qkv, qkv, q_gain, k_gain, sinks, bias]))


def _attn_bwd(qkv, dmix, q_gain, k_gain, sinks, bias, layer, deps=()):
    def body(q_ref, kc_ref, kp_ref, vc_ref, vp_ref, do_ref, qg_ref, kg_ref, sk_ref, bias_ref, *rest):
        dqkv_ref, dbias_ref, dsm_ref, carry = rest[len(deps):]
        i = pl.program_id(0)
        m = NBLK // 2 - 1 - i
        qg = qg_ref[layer:layer + 1, :]
        kg = kg_ref[layer:layer + 1, :]
        lane = lax.broadcasted_iota(jnp.int32, (1, 128), 1)

        @pl.when(i == 0)
        def _():
            carry[...] = jnp.zeros_like(carry)
            dbias_ref[...] = jnp.zeros_like(dbias_ref)
            dsm_ref[...] = jnp.zeros_like(dsm_ref)

        grp = range(NKV)
        chains = [(b, j) for b in range(2) for j in grp]
        masks = [jnp.tile(_attn_mask(2 * m + b), (GQ, 1)) for b in range(2)]
        kblk = [[kp_ref[:, pl.ds(HD * j, HD)].astype(f32), kc_ref[0:BLK, pl.ds(HD * j, HD)].astype(f32),
                 kc_ref[BLK:, pl.ds(HD * j, HD)].astype(f32)] for j in grp]
        vblk = [[vp_ref[:, pl.ds(HD * j, HD)].astype(bf16), vc_ref[0:BLK, pl.ds(HD * j, HD)].astype(bf16),
                 vc_ref[BLK:, pl.ds(HD * j, HD)].astype(bf16)] for j in grp]
        knorm = [[_head_norm(kblk[j][t], kg) for t in range(3)] for j in grp]
        kn_b = {(b, j): jnp.concatenate([knorm[j][b][2].astype(bf16), knorm[j][b + 1][2].astype(bf16)], axis=0) for b, j in chains}
        vbs = {(b, j): jnp.concatenate([vblk[j][b], vblk[j][b + 1]], axis=0) for b, j in chains}
        rows, do_b = {}, {}
        for b, j in chains:
            heads = [GQ * j + g for g in range(GQ)]
            qrows = pl.ds(BLK * b, BLK)
            rows[b, j] = (jnp.concatenate([q_ref[qrows, pl.ds(HD * h, HD)] for h in heads], axis=0).astype(f32),
                          jnp.concatenate([jnp.broadcast_to(sk_ref[layer:layer + 1, h:h + 1], (BLK, 1)) for h in heads], axis=0))
            do_b[b, j] = jnp.concatenate([do_ref[qrows, pl.ds(HD * h, HD)] for h in heads], axis=0).astype(bf16)
        qnorm = {c: _head_norm(rows[c][0], qg) for c in chains}
        qn_b = {c: qnorm[c][2].astype(bf16) for c in chains}
        ss = {(b, j): _dot(qn_b[b, j], kn_b[b, j], NT_DIMS) * (HD ** -0.5) + bias_ref[GQ * j:GQ * (j + 1)].reshape(GQ * BLK, 2 * BLK)
              for b, j in chains}
        sm = {(b, j): _softmax_with_sink(jnp.where(masks[b], ss[b, j], -jnp.inf), rows[b, j][1]) for b, j in chains}
        dps = {c: _dot(do_b[c], vbs[c], NT_DIMS) for c in chains}
        deltas = {c: jnp.sum(sm[c][0] * dps[c], axis=-1, keepdims=True) for c in chains}
        dss = {c: sm[c][0] * (dps[c] - deltas[c]) for c in chains}
        ds_b = {c: (dss[c] * (HD ** -0.5)).astype(bf16) for c in chains}
        dqn = {c: _dot(ds_b[c], kn_b[c], NN_DIMS) for c in chains}
        dkn = {c: _dot(ds_b[c], qn_b[c], TN_DIMS) for c in chains}
        dvs = {c: _dot(sm[c][0].astype(bf16), do_b[c], TN_DIMS) for c in chains}
        dqg = jnp.zeros((1, HD), f32)
        dkg = jnp.zeros((1, HD), f32)
        dsink = jnp.zeros((1, 128), f32)
        for b, j in chains:
            dbias_ref[GQ * j:GQ * (j + 1)] += dss[b, j].reshape(GQ, BLK, 2 * BLK)
            dsk = sm[b, j][1] * deltas[b, j]
            for g in range(GQ):
                dsink = dsink + jnp.where(lane == GQ * j + g, -_sum11(dsk[BLK * g:BLK * (g + 1), :]), 0.0)
            qhat, rq, _ = qnorm[b, j]
            w = dqn[b, j] * qg
            dq = rq * (w - qhat * _row_mean(qhat * w))
            for g in range(GQ):
                dqkv_ref[pl.ds(BLK * b, BLK), pl.ds(HD * (GQ * j + g), HD)] = dq[BLK * g:BLK * (g + 1), :].astype(bf16)
            dqg = dqg + jnp.sum(dqn[b, j] * qhat, axis=0, keepdims=True)
        for j in grp:
            dkn_t = [dkn[0, j][:BLK, :], dkn[0, j][BLK:, :] + dkn[1, j][:BLK, :], dkn[1, j][BLK:, :]]
            dv_t = [dvs[0, j][:BLK, :], dvs[0, j][BLK:, :] + dvs[1, j][:BLK, :], dvs[1, j][BLK:, :]]
            dk_t = []
            for t in range(3):
                khat, rk, _ = knorm[j][t]
                w = dkn_t[t] * kg
                dk_t.append(rk * (w - khat * _row_mean(khat * w)))
                dkg = dkg + jnp.sum(dkn_t[t] * khat, axis=0, keepdims=True)
            kcols, vcols = pl.ds(D_ATTN + HD * j, HD), pl.ds(D_ATTN + 128 + HD * j, HD)
            dqkv_ref[BLK:, kcols] = (dk_t[2] + carry[:, pl.ds(HD * j, HD)]).astype(bf16)
            dqkv_ref[BLK:, vcols] = (dv_t[2] + carry[:, pl.ds(128 + HD * j, HD)]).astype(bf16)
            dqkv_ref[0:BLK, kcols] = dk_t[1].astype(bf16)
            dqkv_ref[0:BLK, vcols] = dv_t[1].astype(bf16)
            carry[:, pl.ds(HD * j, HD)] = dk_t[0]
            carry[:, pl.ds(128 + HD * j, HD)] = dv_t[0]
        dsm_ref[0:1, 0:HD] += dqg
        dsm_ref[1:2, 0:HD] += dkg
        dsm_ref[2:3, :] += dsink

    rev = lambda i: NBLK // 2 - 1 - i
    prev = lambda i: jnp.maximum(NBLK - 3 - 2 * i, 0)
    small = lambda shape: pl.BlockSpec(shape, lambda i: (0,) * len(shape))
    return pl.pallas_call(
        body, grid=(NBLK // 2,),
        in_specs=[pl.BlockSpec((2 * BLK, D_ATTN), lambda i: (rev(i), 0)),
                  pl.BlockSpec((2 * BLK, 128), lambda i: (rev(i), 4)), pl.BlockSpec((BLK, 128), lambda i: (prev(i), 4)),
                  pl.BlockSpec((2 * BLK, 128), lambda i: (rev(i), 5)), pl.BlockSpec((BLK, 128), lambda i: (prev(i), 5)),
                  pl.BlockSpec((2 * BLK, D_ATTN), lambda i: (rev(i), 0)),
                  small((DEPTH, HD)), small((DEPTH, HD)), small((DEPTH, NQ)), small((NQ, BLK, 2 * BLK))] + [ANY_SPEC] * len(deps),
        out_specs=[pl.BlockSpec((2 * BLK, 768), lambda i: (rev(i), COL_QKV // 768)), small((NQ, BLK, 2 * BLK)), small((8, 128))],
        out_shape=_out_hbm([SDS((S, D_IN_PAD), bf16), SDS((NQ, BLK, 2 * BLK), f32), SDS((8, 128), f32)]),
        scratch_shapes=[pltpu.VMEM((BLK, 256), f32)], name="attn_bwd", compiler_params=_cparams(1),
    )(*_in_hbm([qkv, qkv, qkv, qkv, qkv, dmix, q_gain, k_gain, sinks, bias, *deps]))


CONV_TC = 256


def _shift_down(u, s):
    if s == 0:
        return u
    rows = lax.broadcasted_iota(jnp.int32, u.shape, 0)
    return jnp.where(rows >= s, pltpu.roll(u, s, 0), 0.0)


def _shift_up(u, s):
    if s == 0:
        return u
    rows = lax.broadcasted_iota(jnp.int32, u.shape, 0)
    return jnp.where(rows < u.shape[0] - s, pltpu.roll(u, u.shape[0] - s, 0), 0.0)


def _conv_specs():
    return [pl.BlockSpec((S, CONV_TC), lambda c: (0, c)),
            pl.BlockSpec((None, 4, CONV_TC), lambda c: (0, 0, c)),
            pl.BlockSpec((DEPTH, CONV_TC), lambda c: (0, c))]


def _conv_pre(u, w_ref, b_ref, layer):
    pre = b_ref[layer:layer + 1, :] + w_ref[3:4, :] * u
    for k in range(3):
        pre = pre + w_ref[k:k + 1, :] * _shift_down(u, 3 - k)
    return pre


def _conv_fwd(xbc, conv_w, conv_b, layer):
    def body(u_ref, w_ref, b_ref, o_ref):
        pre = _conv_pre(u_ref[...].astype(f32), w_ref, b_ref, layer)
        o_ref[...] = pre * _sigmoid(pre)

    specs = _conv_specs()
    specs[1] = pl.BlockSpec((None, 4, CONV_TC), lambda c: (layer, 0, c))
    return pl.pallas_call(
        body, grid=(D_CONV // CONV_TC,), in_specs=specs, out_specs=pl.BlockSpec((S, CONV_TC), lambda c: (0, c)),
        out_shape=_out_hbm(SDS((S, D_CONV), f32)), name="conv_fwd", compiler_params=_cparams(1),
    )(*_in_hbm([xbc, conv_w, conv_b]))


def _conv_bwd(xbc, dact, conv_w, conv_b, dproj, layer):
    def body(u_ref, w_ref, b_ref, da_ref, dproj_in, du_ref, dw_ref, db_ref):
        u = u_ref[...].astype(f32)
        pre = _conv_pre(u, w_ref, b_ref, layer)
        sg = _sigmoid(pre)
        dpre = da_ref[...] * (sg * (1.0 + pre * (1.0 - sg)))
        du = w_ref[3:4, :] * dpre
        for k in range(3):
            du = du + w_ref[k:k + 1, :] * _shift_up(dpre, 3 - k)
        du_ref[...] = du.astype(bf16)
        db_ref[...] = jnp.broadcast_to(jnp.sum(dpre, axis=0, keepdims=True), db_ref.shape)
        dw_ref[...] = jnp.zeros_like(dw_ref)
        for k in range(4):
            dw_ref[k:k + 1, :] = jnp.sum(dpre * _shift_down(u, 3 - k), axis=0, keepdims=True)

    specs = _conv_specs()
    specs[1] = pl.BlockSpec((None, 4, CONV_TC), lambda c: (layer, 0, c))
    col = pl.BlockSpec((S, CONV_TC), lambda c: (0, c))
    row8 = pl.BlockSpec((8, CONV_TC), lambda c: (0, c))
    return pl.pallas_call(
        body, grid=(D_CONV // CONV_TC,), in_specs=[*specs, col, ANY_SPEC],
        out_specs=[pl.BlockSpec((S, CONV_TC), lambda c: (0, COL_XBC // CONV_TC + c)), row8, row8],
        out_shape=_out_hbm([SDS((S, D_IN_PAD), bf16), SDS((8, D_CONV), f32), SDS((8, D_CONV), f32)]), name="conv_bwd",
        input_output_aliases={4: 0}, compiler_params=_cparams(1),
    )(*_in_hbm([xbc, conv_w, conv_b, dact, dproj]))


def _tri():
    return (lax.broadcasted_iota(jnp.int32, (BLK, BLK), 0) >= lax.broadcasted_iota(jnp.int32, (BLK, BLK), 1))


def _ssd_scalars(dt_ref, dtb_ref, alog_ref, layer):
    raw = dt_ref[:, 0:NSSM] + dtb_ref[layer:layer + 1, :]
    dtv = jnp.maximum(raw, 0.0) + jnp.log(1.0 + jnp.exp(-jnp.abs(raw)))
    a = -jnp.exp(alog_ref[layer:layer + 1, :])
    acs = jnp.dot(_tri().astype(f32), dtv * a, preferred_element_type=f32, precision=HIGHEST)
    return raw, dtv, a, acs


HG = NSSM // NGRP
GW = HG * HD


def _lane_expand(cols, g):
    lane_head = lax.broadcasted_iota(jnp.int32, (1, GW), 1) // HD
    out = cols[:, HG * g + HG - 1:HG * g + HG]
    for r in range(HG - 2, -1, -1):
        out = jnp.where(lane_head == r, cols[:, HG * g + r:HG * g + r + 1], out)
    return out


def _row_expand(vals, g):
    row_head = lax.broadcasted_iota(jnp.int32, (GW, 1), 0) // HD
    out = vals[:, HG * g + HG - 1:HG * g + HG]
    for r in range(HG - 2, -1, -1):
        out = jnp.where(row_head == r, vals[:, HG * g + r:HG * g + r + 1], out)
    return out


def _head_rowsums(a, g):
    sel = (lax.broadcasted_iota(jnp.int32, (GW, NSSM), 0) // HD + HG * g == lax.broadcasted_iota(jnp.int32, (GW, NSSM), 1)).astype(bf16)
    hi = a.astype(bf16)
    lo = (a - hi.astype(f32)).astype(bf16)
    return _dot(hi, sel, NN_DIMS) + _dot(lo, sel, NN_DIMS)


def _head_blocksums(v, g):
    sel = (lax.broadcasted_iota(jnp.int32, (GW, NSSM), 0) // HD + HG * g == lax.broadcasted_iota(jnp.int32, (GW, NSSM), 1)).astype(bf16)
    hi = v.astype(bf16)
    lo = (v - hi.astype(f32)).astype(bf16)
    return _dot(hi, sel, TN_DIMS) + _dot(lo, sel, TN_DIMS)


def _ssd_chunk_common(xc_ref, dt_ref, dtb_ref, alog_ref, h_rows, layer):
    raw, dtv, a, acs = _ssd_scalars(dt_ref, dtb_ref, alog_ref, layer)
    acs_t = acs.T
    last = acs[BLK - 1:BLK, :]
    c = dict(raw=raw, dtv=dtv, a=a, acs=acs, last=last, dte=jnp.exp(last - acs), e_all=jnp.exp(acs), cd=jnp.exp(last))
    grp, heads, tri = range(NGRP), range(NSSM), _tri()
    c["bm"] = [xc_ref[:, pl.ds(D_SSM + NSTATE * g, NSTATE)] for g in grp]
    c["bm_b"] = [c["bm"][g].astype(bf16) for g in grp]
    c["cm_b"] = [xc_ref[:, pl.ds(D_SSM + NGRP * NSTATE + NSTATE * g, NSTATE)].astype(bf16) for g in grp]
    c["cb"] = [_dot(c["cm_b"][g], c["bm_b"][g], NT_DIMS) for g in grp]
    c["x"] = [xc_ref[:, pl.ds(GW * g, GW)] for g in grp]
    c["dt"] = [_lane_expand(dtv, g) for g in grp]
    c["xdt"] = [c["x"][g] * c["dt"][g] for g in grp]
    c["xdt_b"] = [c["xdt"][g].astype(bf16) for g in grp]
    c["prev"] = [h_rows(g) for g in grp]
    c["prev_b"] = [c["prev"][g].astype(bf16) for g in grp]
    c["e"] = [_lane_expand(c["e_all"], g) for g in grp]
    c["y_off"] = [_dot(c["cm_b"][g], c["prev_b"][g], NT_DIMS) * c["e"][g] for g in grp]
    c["decay"] = [jnp.exp(jnp.where(tri, acs[:, h:h + 1] - acs_t[h:h + 1, :], -jnp.inf)) for h in heads]
    c["m"] = [c["cb"][h // HG] * c["decay"][h] for h in heads]
    c["m_b"] = [c["m"][h].astype(bf16) for h in heads]
    c["dte_x"] = [_lane_expand(c["dte"], g) for g in grp]
    c["xdte_b"] = [(c["xdt"][g] * c["dte_x"][g]).astype(bf16) for g in grp]
    return c


def _ssd_fwd(xact, z, dt, attn, dt_bias, a_log, d_skip, norm_g, layer):
    def body(xc_ref, z_ref, dt_ref, at_ref, dtb_ref, alog_ref, dsk_ref, ng_ref, mix_ref, hs_ref, y_ref, h_ref):
        n = pl.program_id(0)

        @pl.when(n == 0)
        def _():
            h_ref[...] = jnp.zeros_like(h_ref)

        hs_ref[...] = h_ref[...]
        c = _ssd_chunk_common(xc_ref, dt_ref, dtb_ref, alog_ref, lambda g: h_ref[pl.ds(GW * g, GW), :], layer)
        grp, heads = range(NGRP), range(NSSM)
        y_diag = [_dot(c["m_b"][h], c["xdt_b"][h // HG][:, HD * (h % HG):HD * (h % HG + 1)], NN_DIMS) for h in heads]
        new_st = [_dot(c["xdte_b"][g], c["bm_b"][g], TN_DIMS) for g in grp]
        for h in heads:
            y_ref[:, pl.ds(HD * h, HD)] = y_diag[h]
        dskip = dsk_ref[layer:layer + 1, :]
        for g in grp:
            cols = pl.ds(GW * g, GW)
            y_ref[:, cols] = y_ref[:, cols] + c["y_off"][g] + c["x"][g] * _lane_expand(dskip, g)
            h_ref[cols, :] = c["prev"][g] * _row_expand(c["cd"], g) + new_st[g]
        zv = z_ref[...].astype(f32)
        yz = y_ref[...] * (zv * _sigmoid(zv))
        mix_ref[:, 0:D_ATTN] = at_ref[...]
        for g in grp:
            yg = yz[:, GW * g:GW * (g + 1)]
            rs = lax.rsqrt(jnp.mean(yg * yg, axis=-1, keepdims=True) + EPS)
            mix_ref[:, D_ATTN + GW * g:D_ATTN + GW * (g + 1)] = (yg * rs * ng_ref[layer:layer + 1, GW * g:GW * (g + 1)]).astype(bf16)

    small = lambda shape: pl.BlockSpec(shape, lambda n: (0,) * len(shape))
    return pl.pallas_call(
        body, grid=(NBLK,),
        in_specs=[pl.BlockSpec((BLK, D_CONV), lambda n: (n, 0)), pl.BlockSpec((BLK, D_SSM), lambda n: (n, 0)),
                  pl.BlockSpec((BLK, 128), lambda n: (n, 0)), pl.BlockSpec((BLK, D_ATTN), lambda n: (n, 0)),
                  small((DEPTH, NSSM)), small((DEPTH, NSSM)), small((DEPTH, NSSM)), small((DEPTH, D_SSM))],
        out_specs=[pl.BlockSpec((BLK, D), lambda n: (n, 0)), pl.BlockSpec((None, NSSM * HD, NSTATE), lambda n: (n, 0, 0)),
                   pl.BlockSpec((BLK, D_SSM), lambda n: (n, 0))],
        out_shape=_out_hbm([SDS((S, D), bf16), SDS((NBLK, NSSM * HD, NSTATE), f32), SDS((S, D_SSM), f32)]),
        scratch_shapes=[pltpu.VMEM((NSSM * HD, NSTATE), f32)],
        name="ssd_fwd", compiler_params=_cparams(1),
    )(*_in_hbm([xact, z, dt, attn, dt_bias, a_log, d_skip, norm_g]))


def _ssd_bwd(xact, z, dt, dmix, hs, y, dt_bias, a_log, d_skip, norm_g, dproj, layer):
    def body(xc_ref, z_ref, dt_ref, do_ref, hs_ref, y_ref, dtb_ref, alog_ref, dsk_ref, ng_ref, dproj_in,
             dzdt_ref, dx_ref, dsm_ref, dh_ref, dy_ref):
        i = pl.program_id(0)

        @pl.when(i == 0)
        def _():
            dh_ref[...] = jnp.zeros_like(dh_ref)
            dsm_ref[...] = jnp.zeros_like(dsm_ref)

        c = _ssd_chunk_common(xc_ref, dt_ref, dtb_ref, alog_ref, lambda g: hs_ref[pl.ds(GW * g, GW), :], layer)
        raw, dtv, a = c["raw"], c["dtv"], c["a"]
        grp, heads = range(NGRP), range(NSSM)
        dskip = dsk_ref[layer:layer + 1, :]
        lane8 = lax.broadcasted_iota(jnp.int32, (1, NSSM), 1)
        sub8 = lax.broadcasted_iota(jnp.int32, (NSSM, 1), 0)

        zv = z_ref[...].astype(f32)
        sz = _sigmoid(zv)
        gz = zv * sz
        yv = y_ref[...]
        yz = yv * gz
        for g in grp:
            sl = slice(GW * g, GW * (g + 1))
            yg = yz[:, sl]
            rs = lax.rsqrt(jnp.mean(yg * yg, axis=-1, keepdims=True) + EPS)
            yhat = yg * rs
            dog = do_ref[:, sl]
            w = dog * ng_ref[layer:layer + 1, sl]
            dyz = rs * (w - yhat * jnp.mean(yhat * w, axis=-1, keepdims=True))
            dsm_ref[0:1, sl] += jnp.sum(dog * yhat, axis=0, keepdims=True)
            dy_ref[:, sl] = dyz * gz[:, sl]
            dzdt_ref[:, sl] = (dyz * yv[:, sl] * (sz[:, sl] * (1.0 + zv[:, sl] * (1.0 - sz[:, sl])))).astype(bf16)

        dy = [dy_ref[:, pl.ds(GW * g, GW)] for g in grp]
        dy_b = [dy[g].astype(bf16) for g in grp]
        hl = lambda h: slice(HD * (h % HG), HD * (h % HG + 1))
        dt_off_b = [(dy[g] * c["e"][g]).astype(bf16) for g in grp]
        dcm = [_dot(dt_off_b[g], c["prev_b"][g], NN_DIMS) for g in grp]
        dprev = [_dot(dt_off_b[g], c["cm_b"][g], TN_DIMS) for g in grp]
        yoff_rs = [_head_rowsums(dy[g] * c["y_off"][g], g) for g in grp]
        dhn = [dh_ref[pl.ds(GW * g, GW), :] for g in grp]
        dhn_b = [dhn[g].astype(bf16) for g in grp]
        dprev = [dprev[g] + dhn[g] * _row_expand(c["cd"], g) for g in grp]
        dhn_prev = [dhn[g] * c["prev"][g] for g in grp]
        u = [_dot(c["bm_b"][g], dhn_b[g], NT_DIMS) for g in grp]
        dbm = [_dot(c["xdte_b"][g], dhn_b[g], NN_DIMS) for g in grp]
        ddte_rs = [_head_rowsums(c["xdt"][g] * u[g], g) for g in grp]
        dm = [_dot(dy_b[h // HG][:, hl(h)], c["xdt_b"][h // HG][:, hl(h)], NT_DIMS) for h in heads]
        dxdt_in = [_dot(c["m_b"][h], dy_b[h // HG][:, hl(h)], TN_DIMS) for h in heads]
        dseg = [dm[h] * c["m"][h] for h in heads]
        dmd = [dm[h] * c["decay"][h] for h in heads]
        for h in heads:
            dx_ref[:, pl.ds(HD * h, HD)] = dxdt_in[h]

        tmp = (ddte_rs[0] + ddte_rs[1]) * c["dte"]
        dacs = yoff_rs[0] + yoff_rs[1] - tmp
        dacs_cols = jnp.zeros((NSSM, BLK), f32)
        ddtv = jnp.zeros((BLK, NSSM), f32)
        ddsk = jnp.zeros((BLK, NSSM), f32)
        hp = jnp.zeros((1, NSSM), f32)
        for g in grp:
            cols = pl.ds(GW * g, GW)
            dxdt = dx_ref[:, cols] + u[g] * c["dte_x"][g]
            dx_ref[:, cols] = dy[g] * _lane_expand(dskip, g) + dxdt * c["dt"][g]
            ddtv = ddtv + _head_rowsums(dxdt * c["x"][g], g)
            ddsk = ddsk + _head_rowsums(dy[g] * c["x"][g], g)
            dcb = dmd[HG * g]
            for r in range(1, HG):
                dcb = dcb + dmd[HG * g + r]
            dcb_b = dcb.astype(bf16)
            dx_ref[:, pl.ds(D_SSM + NSTATE * g, NSTATE)] = dbm[g] + _dot(dcb_b, c["cm_b"][g], TN_DIMS)
            dx_ref[:, pl.ds(D_SSM + NGRP * NSTATE + NSTATE * g, NSTATE)] = dcm[g] + _dot(dcb_b, c["bm_b"][g], NN_DIMS)
            dh_ref[cols, :] = dprev[g]
            hp = hp + _head_blocksums(jnp.sum(dhn_prev[g], axis=1, keepdims=True), g)
            for r in range(HG):
                h = HG * g + r
                dacs = dacs + (lane8 == h).astype(f32) * jnp.sum(dseg[h], axis=1, keepdims=True)
                dacs_cols = dacs_cols + (sub8 == h).astype(f32) * jnp.sum(dseg[h], axis=0, keepdims=True)
        dlast = hp * c["cd"] + jnp.sum(tmp, axis=0, keepdims=True)
        ddsk = jnp.sum(ddsk, axis=0, keepdims=True)

        row = lax.broadcasted_iota(jnp.int32, (BLK, 1), 0)
        dacs = dacs - dacs_cols.T + jnp.where(row == BLK - 1, dlast, 0.0)
        dda = lax.dot_general(_tri().astype(f32), dacs, TN_DIMS, preferred_element_type=f32, precision=HIGHEST)
        ddtv = ddtv + dda * a
        da = jnp.sum(dda * dtv, axis=0, keepdims=True)
        draw = ddtv * _sigmoid(raw)
        dzdt_ref[:, D_SSM:] = jnp.zeros((BLK, COL_XBC - COL_DT), bf16)
        dzdt_ref[:, D_SSM:D_SSM + NSSM] = draw.astype(bf16)
        dsm_ref[1:2, 0:NSSM] += jnp.sum(draw, axis=0, keepdims=True)
        dsm_ref[2:3, 0:NSSM] += da * a
        dsm_ref[3:4, 0:NSSM] += ddsk

    rev = lambda i: NBLK - 1 - i
    small = lambda shape: pl.BlockSpec(shape, lambda i: (0,) * len(shape))
    return pl.pallas_call(
        body, grid=(NBLK,),
        in_specs=[pl.BlockSpec((BLK, D_CONV), lambda i: (rev(i), 0)), pl.BlockSpec((BLK, D_SSM), lambda i: (rev(i), 0)),
                  pl.BlockSpec((BLK, 128), lambda i: (rev(i), 0)), pl.BlockSpec((BLK, D_SSM), lambda i: (rev(i), 1)),
                  pl.BlockSpec((None, NSSM * HD, NSTATE), lambda i: (rev(i), 0, 0)), pl.BlockSpec((BLK, D_SSM), lambda i: (rev(i), 0)),
                  small((DEPTH, NSSM)), small((DEPTH, NSSM)), small((DEPTH, NSSM)), small((DEPTH, D_SSM)), ANY_SPEC],
        out_specs=[pl.BlockSpec((BLK, COL_XBC - COL_Z), lambda i: (rev(i), COL_Z // (COL_XBC - COL_Z))),
                   pl.BlockSpec((BLK, D_CONV), lambda i: (rev(i), 0)), small((8, D_SSM))],
        out_shape=_out_hbm([SDS((S, D_IN_PAD), bf16), SDS((S, D_CONV), f32), SDS((8, D_SSM), f32)]),
        scratch_shapes=[pltpu.VMEM((NSSM * HD, NSTATE), f32), pltpu.VMEM((BLK, D_SSM), f32)],
        name="ssd_bwd", input_output_aliases={10: 0}, compiler_params=_cparams(1),
    )(*_in_hbm([xact, z, dt, dmix, hs, y, dt_bias, a_log, d_skip, norm_g, dproj]))


def _my_place():
    return lax.axis_index("x"), lax.axis_index("y"), lax.axis_index("c")


def _dev_index(px, py, pc):
    return 4 * px + 2 * py + pc


def _slab2(kind, ref, idx):
    if kind == "stack":
        return ref.at[idx]
    if kind == "rows128":
        return ref.at[pl.ds(pl.multiple_of(idx * 128, 128), 128), :]
    if kind == "rows512":
        return ref.at[pl.ds(pl.multiple_of(idx * 512, 512), 512), :]
    return ref.at[:, pl.ds(pl.multiple_of(idx * 512, 512), 512)]


def _slab_shape(kind, full_shape):
    if kind == "stack":
        return tuple(full_shape[1:])
    if kind == "rows128":
        return (128, full_shape[1])
    if kind == "rows512":
        return (512, full_shape[1])
    return (full_shape[0], 512)


KIND = dict(w_in="stack", w_out="rows128", w_up="cols512", w_down="rows512", conv_w="stack")
FULL_SHAPE = dict(w_in=(N_DEV, D, D_IN // N_DEV), w_out=(D, D), w_up=(D, D_FF), w_down=(D_FF, D))
HBM_SPEC = pl.BlockSpec(memory_space=pltpu.HBM)
SEM_SPEC = pl.BlockSpec(memory_space=pltpu.SEMAPHORE)
SIDE_EFFECT = pltpu.SideEffectType.DATAFLOW_SIDE_EFFECTING


def _peers_all():
    x, y, c = _my_place()
    return [(x ^ ((r >> 2) & 1), y ^ ((r >> 1) & 1), c ^ (r & 1)) for r in range(1, N_DEV)]


def _split_start(name, bufs, n_copies, plan, deps=()):
    nb = len(bufs)

    def body(*refs):
        ins = refs[:nb]
        send_sems, recv_sems = refs[nb + len(deps)], refs[nb + len(deps) + 1]
        token = refs[-1]
        for i, (src, dst, dev) in enumerate(plan(ins)):
            pltpu.make_async_remote_copy(src_ref=src, dst_ref=dst, send_sem=send_sems.at[i], recv_sem=recv_sems.at[i],
                                         device_id=dev, device_id_type=MESH).start()
        token[...] = jnp.zeros_like(token)

    outs = pl.pallas_call(
        body, name=name,
        out_shape=(pltpu.SemaphoreType.DMA((n_copies,)), pltpu.SemaphoreType.DMA((n_copies,)),
                   *[pltpu.HBM(b.shape, b.dtype) for b in bufs], SDS((8, 128), f32)),
        in_specs=[HBM_SPEC] * nb + [ANY_SPEC] * len(deps),
        out_specs=(SEM_SPEC, SEM_SPEC, *[HBM_SPEC] * nb, pl.BlockSpec(memory_space=pltpu.VMEM)),
        input_output_aliases={i: 2 + i for i in range(nb)},
        compiler_params=pltpu.CompilerParams(has_side_effects=SIDE_EFFECT),
    )(*[pltpu.with_memory_space_constraint(b, pltpu.HBM) for b in bufs], *deps)
    return dict(send=outs[0], recv=outs[1], bufs=list(outs[2:2 + nb]), token=outs[-1], plan=plan, n=n_copies)


def _split_wait(name, started, after):
    bufs = started["bufs"]
    nb = len(bufs)
    plan = started["plan"]

    def body(*refs):
        ins = refs[:nb]
        send_sems, recv_sems = refs[nb], refs[nb + 1]
        for i, (src, dst, dev) in enumerate(plan(ins)):
            cp = pltpu.make_async_remote_copy(src_ref=src, dst_ref=dst, send_sem=send_sems.at[i], recv_sem=recv_sems.at[i],
                                              device_id=dev, device_id_type=MESH)
            cp.wait_send()
            cp.wait_recv()

    outs = pl.pallas_call(
        body, name=name, out_shape=tuple(pltpu.HBM(b.shape, b.dtype) for b in bufs),
        in_specs=[HBM_SPEC] * nb + [SEM_SPEC, SEM_SPEC] + [ANY_SPEC] * len(after), out_specs=(HBM_SPEC,) * nb,
        input_output_aliases={i: i for i in range(nb)},
        compiler_params=pltpu.CompilerParams(has_side_effects=SIDE_EFFECT),
    )(*bufs, started["send"], started["recv"], *after)
    return list(outs)


def _gather_start(name, names, fulls, deps):
    n_t = len(names)

    def plan(refs):
        x, y, c = _my_place()
        my_idx = _dev_index(x, y, c)
        targets = [(x, y, 1 - c), (1 - x, y, c), (x, 1 - y, c), (1 - x, 1 - y, c)]
        slabs = [_slab2(KIND[names[t]], refs[t], my_idx) for t in range(n_t)]
        return [(slabs[t], slabs[t], dev) for t in range(n_t) for dev in targets]

    return _split_start(name, list(fulls), 4 * n_t, plan, deps)


def _gather_finish(name, names, started, after):
    n_t = len(names)
    fulls = _split_wait(name + "_wait", started, after)
    slab_shapes = [SDS(_slab_shape(KIND[n], f.shape), f.dtype) for n, f in zip(names, fulls)]

    def body(*refs):
        ins = refs[:n_t]
        outs = refs[n_t:2 * n_t]
        stage = refs[2 * n_t:3 * n_t]
        load_sems, send_sems, recv_sems = refs[3 * n_t:]
        x, y, c = _my_place()
        chips = [(1 - x, y), (x, 1 - y), (1 - x, 1 - y)]
        pairs = [(t, j) for t in range(n_t) for j in range(3)]
        loads = [pltpu.make_async_copy(_slab2(KIND[names[t]], ins[t], _dev_index(*chips[j], c)), stage[t].at[j], load_sems.at[t, j])
                 for t, j in pairs]
        for cp in loads:
            cp.start()

        def copy(t, j, core):
            return pltpu.make_async_remote_copy(
                src_ref=stage[t].at[j], dst_ref=_slab2(KIND[names[t]], outs[t], _dev_index(*chips[j], core)),
                send_sem=send_sems.at[t, j], recv_sem=recv_sems.at[t, j], device_id=(x, y, 1 - c), device_id_type=MESH)

        sends = [copy(t, j, c) for t, j in pairs]
        for ld, cp in zip(loads, sends):
            ld.wait()
            cp.start()
        for t, j in pairs:
            copy(t, j, 1 - c).wait_recv()
        for cp in sends:
            cp.wait_send()

    return pl.pallas_call(
        body, in_specs=[HBM_SPEC] * n_t, out_specs=[HBM_SPEC] * n_t, out_shape=[pltpu.HBM(b.shape, b.dtype) for b in fulls],
        input_output_aliases={t: t for t in range(n_t)},
        scratch_shapes=[pltpu.VMEM((3,) + s.shape, s.dtype) for s in slab_shapes]
        + [pltpu.SemaphoreType.DMA((n_t, 3)), pltpu.SemaphoreType.DMA((n_t, 3)), pltpu.SemaphoreType.DMA((n_t, 3))],
        name=name + "_pass", compiler_params=pltpu.CompilerParams(vmem_limit_bytes=VMEM_LIMIT),
    )(*fulls)


def _exchange_start(name, names, grads, deps):
    n_t = len(names)
    lands = [lax.empty((N_DEV,) + _slab_shape(KIND[n], g.shape), g.dtype) for n, g in zip(names, grads)]

    def plan(refs):
        my_idx = _dev_index(*_my_place())
        return [(_slab2(KIND[names[t]], refs[t], _dev_index(*peer)), refs[n_t + t].at[my_idx], peer)
                for t in range(n_t) for peer in _peers_all()]

    return _split_start(name, list(grads) + lands, 7 * n_t, plan, deps)


def _small_exchange_start(part, deps):
    land = lax.empty((N_DEV,) + part.shape, part.dtype)

    def plan(refs):
        my_idx = _dev_index(*_my_place())
        return [(refs[0], refs[1].at[my_idx], peer) for peer in _peers_all()]

    return _split_start("small_exchange", [part, land], N_DEV - 1, plan, deps)


def _slab_pieces():
    sh = D_IN // N_DEV
    out = []
    for j in range(N_DEV):
        for first, end, dst in IN_SEGMENTS:
            lo, hi = max(first, sh * j), min(end, sh * (j + 1))
            if lo < hi:
                out.append((j, lo - sh * j, hi - sh * j, dst + lo - first))
    return out


def _w_in_assemble(stacked):
    tr = 256
    sh = D_IN // N_DEV

    def body(i_ref, o_ref):
        o_ref[:, COL_DT:COL_XBC] = jnp.zeros((tr, COL_XBC - COL_DT), bf16)
        for j, lo, hi, dst in _slab_pieces():
            o_ref[:, dst:dst + hi - lo] = i_ref[j, :, lo:hi]

    return pl.pallas_call(
        body, grid=(D // tr,), in_specs=[pl.BlockSpec((N_DEV, tr, sh), lambda i: (0, i, 0))],
        out_specs=pl.BlockSpec((None, tr, D_IN_PAD), lambda i: (0, i, 0)), out_shape=SDS((1, D, D_IN_PAD), bf16),
        name="w_in_assemble", compiler_params=_cparams(1),
    )(*_in_hbm([stacked]))


def _w_in_slabs(dw_in):
    tr = 256
    sh = D_IN // N_DEV

    def body(i_ref, o_ref):
        for j, lo, hi, src in _slab_pieces():
            o_ref[j, :, lo:hi] = i_ref[:, src:src + hi - lo]

    return pl.pallas_call(
        body, grid=(D // tr,), in_specs=[pl.BlockSpec((tr, D_IN_PAD), lambda i: (i, 0))],
        out_specs=pl.BlockSpec((N_DEV, tr, sh), lambda i: (0, i, 0)), out_shape=_out_hbm(SDS((N_DEV, D, sh), bf16)),
        name="w_in_slabs", compiler_params=_cparams(1),
    )(dw_in)


SMALL_NAMES = ("mix_norm_g", "mlp_norm_g", "conv_b", "ssm_norm_g", "q_gain", "k_gain", "sinks", "dt_bias", "a_log", "d_skip",
               "rel_bias", "conv_w")
MISC_LANES = dict(q_gain=(LANE_QG, HD), k_gain=(LANE_KG, HD), sinks=(LANE_SINK, NQ), dt_bias=(LANE_DTB, NSSM),
                  a_log=(LANE_ALOG, NSSM), d_skip=(LANE_DSKIP, NSSM))


def _pack_small_grads(smalls, drel_t, loss):
    def body(*refs):
        o_ref = refs[-1]
        drel_ref, loss_ref = refs[-3], refs[-2]
        o_ref[...] = jnp.zeros_like(o_ref)
        for l in range(DEPTH):
            mixg, mlpg, convb, convw, ssd, attn = refs[6 * l:6 * l + 6]
            o_ref[ROW_MIXG + l:ROW_MIXG + l + 1, :] = mixg[...]
            o_ref[ROW_MLPG + l:ROW_MLPG + l + 1, :] = mlpg[...]
            o_ref[ROW_CONVB + l:ROW_CONVB + l + 1, :] = convb[0:1, :]
            o_ref[ROW_SSMG + l:ROW_SSMG + l + 1, 0:D_SSM] = ssd[0:1, :]
            o_ref[ROW_CONVW + 4 * l:ROW_CONVW + 4 * l + 4, :] = convw[0:4, :]
            row = slice(ROW_MISC + l, ROW_MISC + l + 1)
            o_ref[row, LANE_QG:LANE_QG + HD] = attn[0:1, 0:HD]
            o_ref[row, LANE_KG:LANE_KG + HD] = attn[1:2, 0:HD]
            o_ref[row, LANE_SINK:LANE_SINK + NQ] = attn[2:3, 0:NQ]
            o_ref[row, LANE_DTB:LANE_DTB + NSSM] = ssd[1:2, 0:NSSM]
            o_ref[row, LANE_ALOG:LANE_ALOG + NSSM] = ssd[2:3, 0:NSSM]
            o_ref[row, LANE_DSKIP:LANE_DSKIP + NSSM] = ssd[3:4, 0:NSSM]
        o_ref[ROW_RELB:ROW_RELB + NQ, 0:N_BUCKETS] = drel_ref[...]
        o_ref[ROW_LOSS:ROW_LOSS + 1, 0:1] = loss_ref[0:1, 0:1]

    args = []
    for sm in smalls:
        args += [sm["mix_norm_g"], sm["mlp_norm_g"], sm["conv_b"], sm["conv_w"], sm["ssd"], sm["attn"]]
    args += [drel_t, loss]
    return pl.pallas_call(body, out_shape=SDS((SMALL_ROWS, D), f32), name="pack_small_grads")(*args)


def _adamw_small(part, land, w, m, v):
    n = len(SMALL_NAMES)

    def grad_of(name, g_ref):
        if name == "mix_norm_g":
            return g_ref[ROW_MIXG:ROW_MIXG + DEPTH, :]
        if name == "mlp_norm_g":
            return g_ref[ROW_MLPG:ROW_MLPG + DEPTH, :]
        if name == "conv_b":
            return g_ref[ROW_CONVB:ROW_CONVB + DEPTH, :]
        if name == "ssm_norm_g":
            return g_ref[ROW_SSMG:ROW_SSMG + DEPTH, 0:D_SSM]
        if name == "rel_bias":
            return g_ref[ROW_RELB:ROW_RELB + NQ, 0:N_BUCKETS].T
        lane, width = MISC_LANES[name]
        return g_ref[ROW_MISC:ROW_MISC + DEPTH, lane:lane + width]

    def body(part_ref, land_ref, *refs):
        ws, ms, vs = refs[:n], refs[n:2 * n], refs[2 * n:3 * n]
        loss_ref = refs[3 * n]
        outs = refs[3 * n + 1:-1]
        g_ref = refs[-1]
        me = _dev_index(*_my_place())
        for p in range(N_DEV):
            term = jnp.where(me == p, part_ref[...], land_ref[p])
            if p == 0:
                g_ref[...] = term
            else:
                g_ref[...] += term
        loss_ref[...] = g_ref[ROW_LOSS:ROW_LOSS + 1, 0:128]
        my_cols = pl.ds(pl.multiple_of(me * 128, 128), 128)
        for k, name in enumerate(SMALL_NAMES):
            g_out, d_out, m_out, v_out = outs[4 * k:4 * k + 4]
            if name == "conv_w":
                for l in range(DEPTH):
                    g = g_ref[ROW_CONVW + 4 * l:ROW_CONVW + 4 * l + 4, my_cols]
                    delta, m_new, v_new = _adamw_math(ws[k][l], ms[k][l], vs[k][l], g)
                    g_out[l], d_out[l], m_out[l], v_out[l] = g, delta, m_new, v_new
            else:
                g = grad_of(name, g_ref)
                delta, m_new, v_new = _adamw_math(ws[k][...], ms[k][...], vs[k][...], g)
                g_out[...], d_out[...], m_out[...], v_out[...] = g, delta, m_new, v_new

    ws = [w[name] for name in SMALL_NAMES]
    out_shape = [SDS((1, 128), f32)]
    for a in ws:
        out_shape += [SDS(a.shape, f32)] * 4
    return pl.pallas_call(body, out_shape=out_shape, name="adamw_small", scratch_shapes=[pltpu.VMEM((SMALL_ROWS, D), f32)])(
        part, land, *ws, *[m[name] for name in SMALL_NAMES], *[v[name] for name in SMALL_NAMES])


def _plain(tm, tn):
    return pl.BlockSpec((tm, tn), lambda i, j, k: (i, j))


def _rowblk(tm, width):
    return pl.BlockSpec((tm, width), lambda i, j, k: (i, 0))


def _store_epi(dtype):
    def epi(acc, i, j, ex, outs):
        outs[0][...] = acc.astype(dtype)
    return epi


def _rms_prologue(layer):
    def pro(a_ref, ex, outs):
        xv = a_ref[...]
        r = lax.rsqrt(jnp.mean(xv * xv, axis=-1, keepdims=True) + EPS)
        h = (xv * r * ex[0][layer:layer + 1, :]).astype(bf16)
        outs[-1][...] = h
        return h
    return pro


MLP_TM = 256
MLP_VMEM = 56 * 1024 * 1024


def _resident(shape):
    return pl.BlockSpec((None,) + shape, lambda i: (0, 0, 0), pipeline_mode=pl.Buffered(1))


def _mlp_fwd(layer, x, mix, g, w_out, w_up, w_down, tgt=None):
    tm = MLP_TM
    with_loss = tgt is not None

    def body(x_ref, mix_ref, g_ref, wo_ref, wu_ref, wd_ref, *rest):
        xm_ref, a_ref, r_ref, h_ref = rest[with_loss:with_loss + 4]
        rest = rest[:with_loss] + rest[with_loss + 1:]
        i = pl.program_id(0)
        xv = x_ref[...] + _dot(mix_ref[...], wo_ref[...], NN_DIMS)
        xm_ref[...] = xv
        h = (xv * lax.rsqrt(jnp.mean(xv * xv, axis=-1, keepdims=True) + EPS) * g_ref[layer:layer + 1, :]).astype(bf16)
        h_ref[...] = h
        r = jnp.maximum(_dot(h, wu_ref[...], NN_DIMS), 0.0)
        a = (r * r).astype(bf16)
        a_ref[...] = a
        r_ref[...] = r.astype(bf16)
        y = xv + _dot(a, wd_ref[...], NN_DIMS)
        if not with_loss:
            rest[3][...] = y
            return
        err = y - rest[0][...]
        rest[4][...] = err * (1.0 / D)
        part = 0.5 * jnp.sum(jnp.mean(err * err, axis=-1, keepdims=True), axis=0, keepdims=True)

        @pl.when(i == 0)
        def _():
            rest[5][...] = jnp.zeros_like(rest[5])

        rest[5][...] += jnp.broadcast_to(part, rest[5].shape)

    row = lambda width: pl.BlockSpec((tm, width), lambda i: (i, 0))
    in_specs = [row(D), row(D), pl.BlockSpec((DEPTH, D), lambda i: (0, 0)), _resident((D, D)), _resident((D, D_FF)),
                _resident((D_FF, D))]
    out_specs = [row(D), row(D_FF), row(D_FF), row(D), row(D)]
    out_shape = [SDS((S, D), f32), SDS((S, D_FF), bf16), SDS((S, D_FF), bf16), SDS((S, D), bf16), SDS((S, D), f32)]
    args = [x, mix, g, w_out, w_up, w_down]
    if with_loss:
        in_specs.append(row(D))
        args.append(tgt)
        out_specs.append(pl.BlockSpec((1, 128), lambda i: (0, 0)))
        out_shape.append(SDS((1, 128), f32))
    return pl.pallas_call(
        body, grid=(S // tm,), in_specs=in_specs, out_specs=out_specs, out_shape=_out_hbm(out_shape),
        name="mlp_fwd_loss" if with_loss else "mlp_fwd",
        compiler_params=pltpu.CompilerParams(dimension_semantics=("arbitrary",), vmem_limit_bytes=MLP_VMEM),
    )(*_in_hbm(args[:3]), *args[3:6], *_in_hbm(args[6:]))


def _mlp_bwd_act(layer, dx_out, r_act, x_mid, g, w_down, w_up, w_out, deps):
    tm = MLP_TM

    def body(dxo_ref, r_ref, xm_ref, g_ref, wd_ref, wu_ref, wo_ref, *rest):
        du_ref, dx_ref, dg_ref, dmix_ref = rest[len(deps):]
        dxo = dxo_ref[...]
        du = (_dot(dxo.astype(bf16), wd_ref[...], NT_DIMS) * (2.0 * r_ref[...].astype(f32))).astype(bf16)
        du_ref[...] = du
        dh = _dot(du, wu_ref[...], NT_DIMS)
        _rms_bwd_epilogue(layer)(dh, pl.program_id(0), 0, (xm_ref, g_ref, dxo_ref), (dx_ref, dg_ref))
        dmix_ref[...] = _dot(dx_ref[...].astype(bf16), wo_ref[...], NT_DIMS)

    row = lambda width: pl.BlockSpec((tm, width), lambda i: (i, 0))
    return pl.pallas_call(
        body, grid=(S // tm,),
        in_specs=[row(D), row(D_FF), row(D), pl.BlockSpec((DEPTH, D), lambda i: (0, 0)), _resident((D_FF, D)), _resident((D, D_FF)),
                  _resident((D, D))] + [ANY_SPEC] * len(deps),
        out_specs=[row(D_FF), row(D), pl.BlockSpec((1, D), lambda i: (0, 0)), row(D)],
        out_shape=_out_hbm([SDS((S, D_FF), bf16), SDS((S, D), f32), SDS((1, D), f32), SDS((S, D), f32)]), name="mlp_bwd_act",
        compiler_params=pltpu.CompilerParams(dimension_semantics=("arbitrary",), vmem_limit_bytes=MLP_VMEM),
    )(*_in_hbm([dx_out, r_act, x_mid, g]), w_down, w_up, w_out, *_in_hbm(deps))


def _layer_fwd(l, x, p, get_weights, bias, tgt=None):
    wts = get_weights(l, "in", [x, bias])
    gfull = pl.BlockSpec((DEPTH, D), lambda i, j, k: (0, 0))
    tm = 512

    def inproj_epi(acc, i, j, ex, outs):
        outs[0][...] = acc[:, COL_QKV:COL_Z].astype(bf16)
        outs[1][...] = acc[:, COL_Z:COL_DT].astype(bf16)
        outs[2][...] = acc[:, COL_XBC:D_IN_PAD].astype(bf16)
        outs[3][...] = acc[:, COL_DT:COL_DT + 128]

    qkv, z, xbc, dt, h1 = _matmul(
        "in_proj", "nn", x, wts["w_in"], tm=tm, tn=D_IN_PAD, tk=D, prologue=_rms_prologue(l),
        extras=(p["mix_norm_g"],), extra_specs=(gfull,),
        out_shape=[SDS((S, 768), bf16), SDS((S, 512), bf16), SDS((S, 1024), bf16), SDS((S, 128), f32), SDS((S, D), bf16)],
        out_specs=[_rowblk(tm, 768), _rowblk(tm, 512), _rowblk(tm, 1024), _rowblk(tm, 128), _rowblk(tm, D)], epilogue=inproj_epi)
    attn = _attn_fwd(qkv, p["q_gain"], p["k_gain"], p["sinks"], bias, l)
    xact = _conv_fwd(xbc, wts["conv_w"], p["conv_b"], l)
    mix, hs, y_ssd = _ssd_fwd(xact, z, dt, attn, p["dt_bias"], p["a_log"], p["d_skip"], p["ssm_norm_g"], l)
    wts = dict(wts, **get_weights(l, "rest", [mix]))

    x_mid, a_act, r_act, h2, *result = _mlp_fwd(l, x, mix, p["mlp_norm_g"], wts["w_out"], wts["w_up"], wts["w_down"], tgt)
    saved = dict(x=x, h1=h1, qkv=qkv, z=z, xbc=xbc, dt=dt, xact=xact, mix=mix, hs=hs, y_ssd=y_ssd, x_mid=x_mid, h2=h2,
                 a=a_act, r=r_act, wts=wts)
    return (result[0] if tgt is None else tuple(result)), saved


def _layer_bwd(l, dx_out, sv, p, bias, deps, send):
    wts = sv["wts"]

    dw_down = _matmul("dw_down", "tn", sv["a"], dx_out, tm=512, tn=D, tk=S, out_shape=SDS((D_FF, D), bf16),
                      out_specs=_plain(512, D), epilogue=_store_epi(bf16), deps=deps)
    deps = send(l, dict(w_down=dw_down))
    du, dx_mid, dg_mlp, dmix = _mlp_bwd_act(l, dx_out, sv["r"], sv["x_mid"], p["mlp_norm_g"], wts["w_down"], wts["w_up"],
                                            wts["w_out"], deps)
    dw_up = _matmul("dw_up", "tn", sv["h2"], du, tm=D, tn=512, tk=S, out_shape=SDS((D, D_FF), bf16),
                    out_specs=_plain(D, 512), epilogue=_store_epi(bf16))
    dw_out = _matmul("dw_out", "tn", sv["mix"], dx_mid, tm=D, tn=512, tk=S, out_shape=SDS((D, D), bf16),
                     out_specs=_plain(D, 512), epilogue=_store_epi(bf16))
    deps = send(l, dict(w_up=dw_up, w_out=dw_out))
    gfull = pl.BlockSpec((DEPTH, D), lambda i, j, k: (0, 0))
    grow = pl.BlockSpec((1, D), lambda i, j, k: (0, 0))
    dproj, dbias, dsm_attn = _attn_bwd(sv["qkv"], dmix, p["q_gain"], p["k_gain"], p["sinks"], bias, l, deps)
    dproj, dxact, dsm_ssd = _ssd_bwd(sv["xact"], sv["z"], sv["dt"], dmix, sv["hs"], sv["y_ssd"], p["dt_bias"], p["a_log"],
                                     p["d_skip"], p["ssm_norm_g"], dproj, l)
    dproj, dconv_w, dconv_b = _conv_bwd(sv["xbc"], dxact, wts["conv_w"], p["conv_b"], dproj, l)
    dw_in = _matmul("dw_in", "tn", sv["h1"], dproj, tm=D, tn=1280, tk=S, out_shape=SDS((D, D_IN_PAD), bf16),
                    out_specs=_plain(D, 1280), epilogue=_store_epi(bf16), pin_out=False)
    deps = send(l, dict(w_in=_w_in_slabs(dw_in)))
    dx, dg_mix = _matmul(
        "in_proj_dh", "nt", dproj, wts["w_in"], tm=512, tn=D, tk=D_IN_PAD, out_shape=[SDS((S, D), f32), SDS((1, D), f32)],
        out_specs=[_plain(512, D), grow], epilogue=_rms_bwd_epilogue(l),
        extras=(sv["x"], p["mix_norm_g"], dx_mid), extra_specs=(_plain(512, D), gfull, _plain(512, D)), deps=deps)
    small = dict(mix_norm_g=dg_mix, mlp_norm_g=dg_mlp, conv_w=dconv_w, conv_b=dconv_b, ssd=dsm_ssd, attn=dsm_attn, dbias=dbias)
    return dx, small, deps


def _local_step(x, tgt, p, get_weights, send):
    onehot_t = jnp.asarray(_onehot_buckets(), dtype=bf16)
    bias = _bias_build(p["rel_bias"].T, onehot_t).reshape(NQ, BLK, 2 * BLK)
    saved = []
    h = x
    for l in range(DEPTH):
        h, sv = _layer_fwd(l, h, p, get_weights, bias, tgt if l == DEPTH - 1 else None)
        saved.append(sv)
    dx, loss = h
    smalls = [None] * DEPTH
    deps = ()
    for l in reversed(range(DEPTH)):
        dx, smalls[l], deps = _layer_bwd(l, dx, saved[l], p, bias, deps, send)
    drel_t = _bias_grad(smalls[0]["dbias"].reshape(NQ, -1), smalls[1]["dbias"].reshape(NQ, -1), onehot_t)
    return dx, _pack_small_grads(smalls, drel_t, loss)


WEIGHT_ORDER = ("mix_norm_g", "w_in", "q_gain", "k_gain", "sinks", "rel_bias", "conv_w", "conv_b", "dt_bias", "a_log", "d_skip",
                "ssm_norm_g", "w_out", "mlp_norm_g", "w_up", "w_down")


def kernel(x, mix_norm_g, w_in, q_gain, k_gain, sinks, rel_bias, conv_w, conv_b, dt_bias, a_log, d_skip, ssm_norm_g, w_out, mlp_norm_g, w_up, w_down, loss_target, m_mix_norm_g, m_w_in, m_q_gain, m_k_gain, m_sinks, m_rel_bias, m_conv_w, m_conv_b, m_dt_bias, m_a_log, m_d_skip, m_ssm_norm_g, m_w_out, m_mlp_norm_g, m_w_up, m_w_down, v_mix_norm_g, v_w_in, v_q_gain, v_k_gain, v_sinks, v_rel_bias, v_conv_w, v_conv_b, v_dt_bias, v_a_log, v_d_skip, v_ssm_norm_g, v_w_out, v_mlp_norm_g, v_w_up, v_w_down):
    w = dict(mix_norm_g=mix_norm_g, w_in=w_in, q_gain=q_gain, k_gain=k_gain, sinks=sinks, rel_bias=rel_bias, conv_w=conv_w,
             conv_b=conv_b, dt_bias=dt_bias, a_log=a_log, d_skip=d_skip, ssm_norm_g=ssm_norm_g, w_out=w_out,
             mlp_norm_g=mlp_norm_g, w_up=w_up, w_down=w_down)
    m = dict(mix_norm_g=m_mix_norm_g, w_in=m_w_in, q_gain=m_q_gain, k_gain=m_k_gain, sinks=m_sinks, rel_bias=m_rel_bias,
             conv_w=m_conv_w, conv_b=m_conv_b, dt_bias=m_dt_bias, a_log=m_a_log, d_skip=m_d_skip, ssm_norm_g=m_ssm_norm_g,
             w_out=m_w_out, mlp_norm_g=m_mlp_norm_g, w_up=m_w_up, w_down=m_w_down)
    v = dict(mix_norm_g=v_mix_norm_g, w_in=v_w_in, q_gain=v_q_gain, k_gain=v_k_gain, sinks=v_sinks, rel_bias=v_rel_bias,
             conv_w=v_conv_w, conv_b=v_conv_b, dt_bias=v_dt_bias, a_log=v_a_log, d_skip=v_d_skip, ssm_norm_g=v_ssm_norm_g,
             w_out=v_w_out, mlp_norm_g=v_mlp_norm_g, w_up=v_w_up, w_down=v_w_down)
    big = ("w_in", "w_out", "w_up", "w_down")

    my_idx = _dev_index(*_my_place()).astype(jnp.int32).reshape(1)

    fulls = {n: _cast_to_full("cast_" + n, w[n], KIND[n], FULL_SHAPE[n], my_idx, bf16) for n in big}
    conv_full = _cast_to_full("cast_conv_w", conv_w.reshape(1, DEPTH * 4, 128), "stack", (N_DEV, DEPTH * 4, 128), my_idx, f32)[0]
    rest = ["w_out", "w_up", "w_down"]
    g0 = _gather_start("gather0", ["w_in", "conv_w"], [fulls["w_in"][0], conv_full], ())
    g1 = _gather_start("gather1", rest, [fulls[n][0] for n in rest], (g0["token"],))
    g2 = _gather_start("gather2", ["w_in"], [fulls["w_in"][1]], (g1["token"],))
    g3 = _gather_start("gather3", rest, [fulls[n][1] for n in rest], (g2["token"],))
    held = {}
    flat = lambda a: a.reshape(a.shape[0] * a.shape[1], a.shape[2])
    adam_in = {n: (flat(w[n]), flat(m[n]), flat(v[n])) for n in big}

    def get_weights(l, part, after):
        if l == 0 and part == "in":
            full_in, full_conv = _gather_finish("gather0", ["w_in", "conv_w"], g0,
                                                list(after) + [g3["token"], adam_in["w_in"][1], adam_in["w_in"][2]])
            held["conv_w"] = jnp.transpose(full_conv.reshape(N_DEV, DEPTH, 4, 128), (1, 2, 0, 3)).reshape(DEPTH, 4, D_CONV)
            return dict(w_in=_w_in_assemble(full_in), conv_w=held["conv_w"])
        if part == "in":
            return dict(w_in=_w_in_assemble(_gather_finish("gather2", ["w_in"], g2, after)[0]), conv_w=held["conv_w"])
        full = _gather_finish("gather1" if l == 0 else "gather3", rest, g1 if l == 0 else g3, after)
        return {n: f[None] for n, f in zip(rest, full)}

    pending = []

    def send(l, grads):
        names = list(grads)
        started = _exchange_start("exchange%d_%s" % (l, names[0]), names, [grads[n] for n in names], ())
        pending.append((l, names, started))
        return (started["token"],)

    dx, small_part = _local_step(x.reshape(S, D), loss_target.reshape(S, D), w, get_weights, send)

    small = _small_exchange_start(small_part, ())
    tiles = dict(w_in=512, w_out=128, w_up=512, w_down=256)
    outs_of = {n: None for n in big}
    after = [dx, small["token"]]
    for l, names, started in pending:
        bufs = _split_wait("exchange%d_%s_wait" % (l, names[0]), started, after)
        for t, n in enumerate(names):
            outs_of[n] = _adamw_layer("adamw_%s%d" % (n, l), KIND[n], l, *adam_in[n],
                                      bufs[len(names) + t], bufs[t], my_idx, outs_of[n], tiles[n])
        after = [outs_of[names[-1]][0]]
    res = {n: [o.reshape(w[n].shape) for o in outs_of[n]] for n in big}
    small_part, small_land = _split_wait("small_exchange_wait", small, after)
    small_outs = _adamw_small(small_part, small_land, w, m, v)
    loss = small_outs[0][0, 0]
    for k, name in enumerate(SMALL_NAMES):
        res[name] = small_outs[1 + 4 * k:5 + 4 * k]

    result = [loss, dx.reshape(1, S, D)]
    for k in range(4):
        result += [res[name][k] for name in WEIGHT_ORDER]
    return tuple(result)
```

```python
import functools
import math

import numpy as np
import jax
import jax.numpy as jnp
from jax import lax
from jax.experimental import pallas as pl
from jax.experimental.pallas import tpu as pltpu

f32 = jnp.float32
bf16 = jnp.bfloat16
SDS = jax.ShapeDtypeStruct
MESH = pl.DeviceIdType.MESH
HIGHEST = lax.Precision.HIGHEST

S = 2048
D = 1024
DEPTH = 2
BLK = 128
NBLK = S // BLK
HD = 64
NQ = 8
NKV = 2
NSSM = 8
NGRP = 2
NSTATE = 128
D_ATTN = 512
D_SSM = 512
D_CONV = 1024
D_FF = 4096
D_IN = 2312
D_IN_PAD = 2560
COL_QKV, COL_Z, COL_DT, COL_XBC = 0, 768, 1280, 1536
IN_SEGMENTS = ((0, 1280, 0), (1280, 2304, COL_XBC), (2304, 2312, COL_DT))
N_BUCKETS = 32
EPS = 1e-6
N_DEV = 8
VMEM_LIMIT = 48 * 1024 * 1024

ADAM_LR = 0.001
ADAM_B1 = 0.9
ADAM_B2 = 0.999
ADAM_EPS = 1e-08
ADAM_WD = 0.01
ADAM_STEP = 10

NT_DIMS = (((1,), (1,)), ((), ()))
TN_DIMS = (((0,), (0,)), ((), ()))
NN_DIMS = (((1,), (0,)), ((), ()))

ROW_MIXG = 0
ROW_MLPG = 2
ROW_CONVB = 4
ROW_SSMG = 6
ROW_MISC = 8
ROW_RELB = 10
ROW_CONVW = 18
ROW_LOSS = 26
SMALL_ROWS = 32
LANE_QG, LANE_KG, LANE_SINK, LANE_DTB, LANE_ALOG, LANE_DSKIP = 0, 64, 128, 256, 384, 512


def _dot(a, b, dims):
    return lax.dot_general(a, b, dims, preferred_element_type=f32)


def _cparams(n_axes):
    return pltpu.CompilerParams(dimension_semantics=("arbitrary",) * n_axes, vmem_limit_bytes=VMEM_LIMIT)


def _sum11(v):
    return jnp.sum(jnp.sum(v, axis=1, keepdims=True), axis=0, keepdims=True)


def _sigmoid(v):
    return 1.0 / (1.0 + jnp.exp(-v))


ANY_SPEC = pl.BlockSpec(memory_space=pl.ANY)


def _in_hbm(args):
    return [pltpu.with_memory_space_constraint(a, pltpu.HBM) if a.size >= 65536 else a for a in args]


def _out_hbm(out_shape):
    one = lambda s: pltpu.HBM(s.shape, s.dtype) if math.prod(s.shape) >= 65536 else s
    return [one(s) for s in out_shape] if isinstance(out_shape, (list, tuple)) else one(out_shape)


def _matmul(name, mode, a, b, *, layer=0, tm, tn, tk, out_shape, out_specs, epilogue, extras=(), extra_specs=(), deps=(),
            prologue=None, pin_out=True):
    extras = tuple(extras) + tuple(deps)
    extra_specs = tuple(extra_specs) + (ANY_SPEC,) * len(deps)
    if mode == "tn":
        t_dim, m_dim = a.shape
        n_dim = b.shape[1]
        grid = (m_dim // tm, n_dim // tn, t_dim // tk)
        a_spec = pl.BlockSpec((tk, tm), lambda i, j, k: (k, i))
        b_spec = pl.BlockSpec((tk, tn), lambda i, j, k: (k, j))
        dims = TN_DIMS
    elif mode == "nn":
        m_dim, k_dim = a.shape
        n_dim = b.shape[-1]
        grid = (m_dim // tm, n_dim // tn, k_dim // tk)
        a_spec = pl.BlockSpec((tm, tk), lambda i, j, k: (i, k))
        b_spec = pl.BlockSpec((None, tk, tn), lambda i, j, k: (layer, k, j))
        dims = NN_DIMS
    else:
        m_dim, k_dim = a.shape
        n_dim = b.shape[-2]
        grid = (m_dim // tm, n_dim // tn, k_dim // tk)
        a_spec = pl.BlockSpec((tm, tk), lambda i, j, k: (i, k))
        b_spec = pl.BlockSpec((None, tn, tk), lambda i, j, k: (layer, j, k))
        dims = NT_DIMS
    nk = grid[2]
    n_ex = len(extras)

    def body(a_ref, b_ref, *rest):
        ex = rest[:n_ex - len(deps)]
        outs = rest[n_ex:-1]
        acc = rest[-1]
        i = pl.program_id(0)
        j = pl.program_id(1)
        k = pl.program_id(2)
        lhs = a_ref[...].astype(bf16) if prologue is None else prologue(a_ref, ex, outs)
        part = _dot(lhs, b_ref[...].astype(bf16), dims)
        if nk == 1:
            epilogue(part, i, j, ex, outs)
        else:
            @pl.when(k == 0)
            def _():
                acc[...] = part

            @pl.when(k > 0)
            def _():
                acc[...] += part

            @pl.when(k == nk - 1)
            def _():
                epilogue(acc[...], i, j, ex, outs)

    return pl.pallas_call(
        body, grid=grid, in_specs=[a_spec, b_spec, *extra_specs], out_specs=out_specs,
        out_shape=_out_hbm(out_shape) if pin_out else out_shape,
        scratch_shapes=[pltpu.VMEM((tm, tn) if nk > 1 else (8, 128), f32)], name=name, compiler_params=_cparams(3),
    )(*_in_hbm([a]), b, *_in_hbm(extras))


def _rms_bwd_epilogue(layer):
    def epi(acc, i, j, ex, outs):
        x_ref, g_ref, dres_ref = ex
        dx_ref, dg_ref = outs
        xv = x_ref[...]
        r = lax.rsqrt(jnp.mean(xv * xv, axis=-1, keepdims=True) + EPS)
        xhat = xv * r
        w = acc * g_ref[layer:layer + 1, :]
        dx_ref[...] = dres_ref[...] + r * (w - xhat * jnp.mean(xhat * w, axis=-1, keepdims=True))
        dg = jnp.sum(acc * xhat, axis=0, keepdims=True)

        @pl.when(i == 0)
        def _():
            dg_ref[...] = dg

        @pl.when(i > 0)
        def _():
            dg_ref[...] += dg
    return epi


def _own_slab_spec(kind, tr, cols, nblk):
    if kind == "stack":
        return pl.BlockSpec((None, tr, cols), lambda i, idx: (idx[0], i, 0))
    if kind == "cols512":
        return pl.BlockSpec((tr, cols), lambda i, idx: (i, idx[0]))
    return pl.BlockSpec((tr, cols), lambda i, idx: (idx[0] * nblk + i, 0))


def _cast_to_full(name, w, kind, full_shape, my_idx, dtype):
    n_layers, rows, cols = w.shape
    tr = min(rows, 256)
    nblk = rows // tr

    def body(idx_ref, w_ref, *o_refs):
        for l in range(n_layers):
            o_refs[l][...] = w_ref[l].astype(dtype)

    grid_spec = pltpu.PrefetchScalarGridSpec(
        num_scalar_prefetch=1, grid=(nblk,), in_specs=[pl.BlockSpec((n_layers, tr, cols), lambda i, idx: (0, i, 0))],
        out_specs=[_own_slab_spec(kind, tr, cols, nblk)] * n_layers)
    return pl.pallas_call(body, grid_spec=grid_spec, out_shape=_out_hbm([SDS(full_shape, dtype)] * n_layers), name=name,
                          compiler_params=_cparams(1))(*_in_hbm([my_idx, w]))


def _adamw_math(w, m, v, g):
    m_new = ADAM_B1 * m + (1.0 - ADAM_B1) * g
    v_new = ADAM_B2 * v + (1.0 - ADAM_B2) * (g * g)
    m_hat = m_new / (1.0 - ADAM_B1 ** ADAM_STEP)
    v_hat = v_new / (1.0 - ADAM_B2 ** ADAM_STEP)
    delta = -ADAM_LR * (m_hat / (jnp.sqrt(v_hat) + ADAM_EPS) + ADAM_WD * w)
    return delta, m_new, v_new


def _adamw_layer(name, kind, layer, w, m, v, land, g_full, my_idx, prev, tr):
    rows2, cols = w.shape
    rows = rows2 // DEPTH
    nblk = rows // tr
    own_spec = _own_slab_spec(kind, tr, cols, nblk)
    n_prev = 0 if prev is None else 4

    def body(idx_ref, w_ref, m_ref, v_ref, land_ref, own_ref, *rest):
        g_ref, d_ref, mo_ref, vo_ref = rest[n_prev:]
        me = idx_ref[0]
        g = None
        for p in range(N_DEV):
            part = jnp.where(me == p, own_ref[...], land_ref[p]).astype(f32)
            g = part if g is None else g + part
        delta, m_new, v_new = _adamw_math(w_ref[...], m_ref[...], v_ref[...], g)
        g_ref[...] = g
        d_ref[...] = delta
        mo_ref[...] = m_new
        vo_ref[...] = v_new

    blk = pl.BlockSpec((tr, cols), lambda i, idx: (layer * nblk + i, 0))
    grid_spec = pltpu.PrefetchScalarGridSpec(
        num_scalar_prefetch=1, grid=(nblk,),
        in_specs=[blk, blk, blk, pl.BlockSpec((N_DEV, tr, cols), lambda i, idx: (0, i, 0)), own_spec] + [ANY_SPEC] * n_prev,
        out_specs=[blk, blk, blk, blk])
    aliases = {} if prev is None else {6 + k: k for k in range(4)}
    return pl.pallas_call(
        body, grid_spec=grid_spec, out_shape=_out_hbm([SDS((rows2, cols), f32)] * 4), name=name, input_output_aliases=aliases,
        compiler_params=_cparams(1),
    )(*_in_hbm([my_idx, w, m, v, land, g_full, *([] if prev is None else prev)]))


def _bucket_table():
    qi = np.arange(BLK)[:, None]
    kj = np.arange(2 * BLK)[None, :]
    dist = qi + BLK - kj
    dcl = np.clip(dist, 0, None)
    max_exact = N_BUCKETS // 2
    d_f = np.maximum(dcl, 1).astype(np.float32)
    large = max_exact + (np.log(d_f / np.float32(max_exact)) / np.float32(math.log(128 / max_exact))
                         * np.float32(N_BUCKETS - max_exact)).astype(np.int32)
    large = np.minimum(large, N_BUCKETS - 1)
    bucket = np.where(dcl < max_exact, dcl, large)
    in_window = (dist >= 0) & (dist < BLK)
    return bucket.astype(np.int32), in_window


def _onehot_buckets():
    bucket, _ = _bucket_table()
    oh = (bucket.reshape(-1)[None, :] == np.arange(N_BUCKETS)[:, None]).astype(np.float32)
    return oh


def _bias_build(rel_bias_t, onehot_t):
    def body(r_ref, o_ref, out_ref):
        r = r_ref[...]
        hi = r.astype(bf16)
        r1 = r - hi.astype(f32)
        mid = r1.astype(bf16)
        lo = (r1 - mid.astype(f32)).astype(bf16)
        oh = o_ref[...]
        out_ref[...] = _dot(hi, oh, NN_DIMS) + _dot(mid, oh, NN_DIMS) + _dot(lo, oh, NN_DIMS)

    tn = 4096
    return pl.pallas_call(
        body, grid=(BLK * 2 * BLK // tn,),
        in_specs=[pl.BlockSpec((NQ, N_BUCKETS), lambda i: (0, 0)), pl.BlockSpec((N_BUCKETS, tn), lambda i: (0, i))],
        out_specs=pl.BlockSpec((NQ, tn), lambda i: (0, i)), out_shape=SDS((NQ, BLK * 2 * BLK), f32), name="bias_build",
        compiler_params=_cparams(1),
    )(rel_bias_t, onehot_t)


def _bias_grad(dbias0, dbias1, onehot_t):
    tn = 4096
    nsteps = BLK * 2 * BLK // tn

    def body(a_ref, b_ref, o_ref, out_ref):
        g = a_ref[...] + b_ref[...]
        hi = g.astype(bf16)
        lo = (g - hi.astype(f32)).astype(bf16)
        part = _dot(hi, o_ref[...], NT_DIMS) + _dot(lo, o_ref[...], NT_DIMS)

        @pl.when(pl.program_id(0) == 0)
        def _():
            out_ref[...] = part

        @pl.when(pl.program_id(0) > 0)
        def _():
            out_ref[...] += part

    return pl.pallas_call(
        body, grid=(nsteps,),
        in_specs=[pl.BlockSpec((NQ, tn), lambda i: (0, i)), pl.BlockSpec((NQ, tn), lambda i: (0, i)),
                  pl.BlockSpec((N_BUCKETS, tn), lambda i: (0, i))],
        out_specs=pl.BlockSpec((NQ, N_BUCKETS), lambda i: (0, 0)), out_shape=SDS((NQ, N_BUCKETS), f32), name="bias_grad",
        compiler_params=_cparams(1),
    )(dbias0, dbias1, onehot_t)


def _attn_mask(n):
    qi = lax.broadcasted_iota(jnp.int32, (BLK, 2 * BLK), 0)
    kj = lax.broadcasted_iota(jnp.int32, (BLK, 2 * BLK), 1)
    dist = qi + BLK - kj
    first_key = jnp.where(n > 0, 0, BLK)
    return (dist >= 0) & (dist < BLK) & (kj >= first_key)


def _row_mean(a):
    return jnp.mean(a, axis=-1, keepdims=True)


def _head_norm(t, gain):
    r = lax.rsqrt(_row_mean(t * t) + EPS)
    that = t * r
    return that, r, that * gain


def _softmax_with_sink(s, sink):
    m = jnp.maximum(jnp.max(s, axis=-1, keepdims=True), sink)
    p = jnp.exp(s - m)
    psink = jnp.exp(sink - m)
    inv = 1.0 / (jnp.sum(p, axis=-1, keepdims=True) + psink)
    return p * inv, psink * inv


GQ = NQ // NKV


def _attn_fwd(qkv, q_gain, k_gain, sinks, bias, layer):
    def body(q_ref, kc_ref, kp_ref, vc_ref, vp_ref, qg_ref, kg_ref, sk_ref, bias_ref, o_ref):
        m = pl.program_id(0)
        qg = qg_ref[layer:layer + 1, :]
        kg = kg_ref[layer:layer + 1, :]
        grp = range(NKV)
        chains = [(b, j) for b in range(2) for j in grp]
        masks = [jnp.tile(_attn_mask(2 * m + b), (GQ, 1)) for b in range(2)]
        kblk = [[kp_ref[:, pl.ds(HD * j, HD)].astype(f32), kc_ref[0:BLK, pl.ds(HD * j, HD)].astype(f32),
                 kc_ref[BLK:, pl.ds(HD * j, HD)].astype(f32)] for j in grp]
        vblk = [[vp_ref[:, pl.ds(HD * j, HD)].astype(bf16), vc_ref[0:BLK, pl.ds(HD * j, HD)].astype(bf16),
                 vc_ref[BLK:, pl.ds(HD * j, HD)].astype(bf16)] for j in grp]
        knb = [[_head_norm(kblk[j][t], kg)[2].astype(bf16) for t in range(3)] for j in grp]
        kn_b = {(b, j): jnp.concatenate([knb[j][b], knb[j][b + 1]], axis=0) for b, j in chains}
        vbs = {(b, j): jnp.concatenate([vblk[j][b], vblk[j][b + 1]], axis=0) for b, j in chains}
        rows = {}
        for b, j in chains:
            heads = [GQ * j + g for g in range(GQ)]
            rows[b, j] = (jnp.concatenate([q_ref[pl.ds(BLK * b, BLK), pl.ds(HD * h, HD)] for h in heads], axis=0).astype(f32),
                          jnp.concatenate([jnp.broadcast_to(sk_ref[layer:layer + 1, h:h + 1], (BLK, 1)) for h in heads], axis=0))
        qn_b = {c: _head_norm(rows[c][0], qg)[2].astype(bf16) for c in chains}
        ss = {(b, j): _dot(qn_b[b, j], kn_b[b, j], NT_DIMS) * (HD ** -0.5) + bias_ref[GQ * j:GQ * (j + 1)].reshape(GQ * BLK, 2 * BLK)
              for b, j in chains}
        ps = {(b, j): _softmax_with_sink(jnp.where(masks[b], ss[b, j], -jnp.inf), rows[b, j][1])[0] for b, j in chains}
        outs = {c: _dot(ps[c].astype(bf16), vbs[c], NN_DIMS).astype(bf16) for c in chains}
        for b, j in chains:
            for g in range(GQ):
                o_ref[pl.ds(BLK * b, BLK), pl.ds(HD * (GQ * j + g), HD)] = outs[b, j][BLK * g:BLK * (g + 1), :]

    prev = lambda m: jnp.maximum(2 * m - 1, 0)
    small = lambda shape: pl.BlockSpec(shape, lambda m: (0,) * len(shape))
    return pl.pallas_call(
        body, grid=(NBLK // 2,),
        in_specs=[pl.BlockSpec((2 * BLK, D_ATTN), lambda m: (m, 0)),
                  pl.BlockSpec((2 * BLK, 128), lambda m: (m, 4)), pl.BlockSpec((BLK, 128), lambda m: (prev(m), 4)),
                  pl.BlockSpec((2 * BLK, 128), lambda m: (m, 5)), pl.BlockSpec((BLK, 128), lambda m: (prev(m), 5)),
                  small((DEPTH, HD)), small((DEPTH, HD)), small((DEPTH, NQ)), small((NQ, BLK, 2 * BLK))],
        out_specs=pl.BlockSpec((2 * BLK, D_ATTN), lambda m: (m, 0)), out_shape=SDS((S, D_ATTN), bf16),
        name="attn_fwd", compiler_params=_cparams(1),
    )(*_in_hbm([qkv, qkv, qkv, qkv, qkv, q_gain, k_gain, sinks, bias]))


def _attn_bwd(qkv, dmix, q_gain, k_gain, sinks, bias, layer, deps=()):
    def body(q_ref, kc_ref, kp_ref, vc_ref, vp_ref, do_ref, qg_ref, kg_ref, sk_ref, bias_ref, *rest):
        dqkv_ref, dbias_ref, dsm_ref, carry = rest[len(deps):]
        i = pl.program_id(0)
        m = NBLK // 2 - 1 - i
        qg = qg_ref[layer:layer + 1, :]
        kg = kg_ref[layer:layer + 1, :]
        lane = lax.broadcasted_iota(jnp.int32, (1, 128), 1)

        @pl.when(i == 0)
        def _():
            carry[...] = jnp.zeros_like(carry)
            dbias_ref[...] = jnp.zeros_like(dbias_ref)
            dsm_ref[...] = jnp.zeros_like(dsm_ref)

        grp = range(NKV)
        chains = [(b, j) for b in range(2) for j in grp]
        masks = [jnp.tile(_attn_mask(2 * m + b), (GQ, 1)) for b in range(2)]
        kblk = [[kp_ref[:, pl.ds(HD * j, HD)].astype(f32), kc_ref[0:BLK, pl.ds(HD * j, HD)].astype(f32),
                 kc_ref[BLK:, pl.ds(HD * j, HD)].astype(f32)] for j in grp]
        vblk = [[vp_ref[:, pl.ds(HD * j, HD)].astype(bf16), vc_ref[0:BLK, pl.ds(HD * j, HD)].astype(bf16),
                 vc_ref[BLK:, pl.ds(HD * j, HD)].astype(bf16)] for j in grp]
        knorm = [[_head_norm(kblk[j][t], kg) for t in range(3)] for j in grp]
        kn_b = {(b, j): jnp.concatenate([knorm[j][b][2].astype(bf16), knorm[j][b + 1][2].astype(bf16)], axis=0) for b, j in chains}
        vbs = {(b, j): jnp.concatenate([vblk[j][b], vblk[j][b + 1]], axis=0) for b, j in chains}
        rows, do_b = {}, {}
        for b, j in chains:
            heads = [GQ * j + g for g in range(GQ)]
            qrows = pl.ds(BLK * b, BLK)
            rows[b, j] = (jnp.concatenate([q_ref[qrows, pl.ds(HD * h, HD)] for h in heads], axis=0).astype(f32),
                          jnp.concatenate([jnp.broadcast_to(sk_ref[layer:layer + 1, h:h + 1], (BLK, 1)) for h in heads], axis=0))
            do_b[b, j] = jnp.concatenate([do_ref[qrows, pl.ds(HD * h, HD)] for h in heads], axis=0).astype(bf16)
        qnorm = {c: _head_norm(rows[c][0], qg) for c in chains}
        qn_b = {c: qnorm[c][2].astype(bf16) for c in chains}
        ss = {(b, j): _dot(qn_b[b, j], kn_b[b, j], NT_DIMS) * (HD ** -0.5) + bias_ref[GQ * j:GQ * (j + 1)].reshape(GQ * BLK, 2 * BLK)
              for b, j in chains}
        sm = {(b, j): _softmax_with_sink(jnp.where(masks[b], ss[b, j], -jnp.inf), rows[b, j][1]) for b, j in chains}
        dps = {c: _dot(do_b[c], vbs[c], NT_DIMS) for c in chains}
        deltas = {c: jnp.sum(sm[c][0] * dps[c], axis=-1, keepdims=True) for c in chains}
        dss = {c: sm[c][0] * (dps[c] - deltas[c]) for c in chains}
        ds_b = {c: (dss[c] * (HD ** -0.5)).astype(bf16) for c in chains}
        dqn = {c: _dot(ds_b[c], kn_b[c], NN_DIMS) for c in chains}
        dkn = {c: _dot(ds_b[c], qn_b[c], TN_DIMS) for c in chains}
        dvs = {c: _dot(sm[c][0].astype(bf16), do_b[c], TN_DIMS) for c in chains}
        dqg = jnp.zeros((1, HD), f32)
        dkg = jnp.zeros((1, HD), f32)
        dsink = jnp.zeros((1, 128), f32)
        for b, j in chains:
            dbias_ref[GQ * j:GQ * (j + 1)] += dss[b, j].reshape(GQ, BLK, 2 * BLK)
            dsk = sm[b, j][1] * deltas[b, j]
            for g in range(GQ):
                dsink = dsink + jnp.where(lane == GQ * j + g, -_sum11(dsk[BLK * g:BLK * (g + 1), :]), 0.0)
            qhat, rq, _ = qnorm[b, j]
            w = dqn[b, j] * qg
            dq = rq * (w - qhat * _row_mean(qhat * w))
            for g in range(GQ):
                dqkv_ref[pl.ds(BLK * b, BLK), pl.ds(HD * (GQ * j + g), HD)] = dq[BLK * g:BLK * (g + 1), :].astype(bf16)
            dqg = dqg + jnp.sum(dqn[b, j] * qhat, axis=0, keepdims=True)
        for j in grp:
            dkn_t = [dkn[0, j][:BLK, :], dkn[0, j][BLK:, :] + dkn[1, j][:BLK, :], dkn[1, j][BLK:, :]]
            dv_t = [dvs[0, j][:BLK, :], dvs[0, j][BLK:, :] + dvs[1, j][:BLK, :], dvs[1, j][BLK:, :]]
            dk_t = []
            for t in range(3):
                khat, rk, _ = knorm[j][t]
                w = dkn_t[t] * kg
                dk_t.append(rk * (w - khat * _row_mean(khat * w)))
                dkg = dkg + jnp.sum(dkn_t[t] * khat, axis=0, keepdims=True)
            kcols, vcols = pl.ds(D_ATTN + HD * j, HD), pl.ds(D_ATTN + 128 + HD * j, HD)
            dqkv_ref[BLK:, kcols] = (dk_t[2] + carry[:, pl.ds(HD * j, HD)]).astype(bf16)
            dqkv_ref[BLK:, vcols] = (dv_t[2] + carry[:, pl.ds(128 + HD * j, HD)]).astype(bf16)
            dqkv_ref[0:BLK, kcols] = dk_t[1].astype(bf16)
            dqkv_ref[0:BLK, vcols] = dv_t[1].astype(bf16)
            carry[:, pl.ds(HD * j, HD)] = dk_t[0]
            carry[:, pl.ds(128 + HD * j, HD)] = dv_t[0]
        dsm_ref[0:1, 0:HD] += dqg
        dsm_ref[1:2, 0:HD] += dkg
        dsm_ref[2:3, :] += dsink

    rev = lambda i: NBLK // 2 - 1 - i
    prev = lambda i: jnp.maximum(NBLK - 3 - 2 * i, 0)
    small = lambda shape: pl.BlockSpec(shape, lambda i: (0,) * len(shape))
    return pl.pallas_call(
        body, grid=(NBLK // 2,),
        in_specs=[pl.BlockSpec((2 * BLK, D_ATTN), lambda i: (rev(i), 0)),
                  pl.BlockSpec((2 * BLK, 128), lambda i: (rev(i), 4)), pl.BlockSpec((BLK, 128), lambda i: (prev(i), 4)),
                  pl.BlockSpec((2 * BLK, 128), lambda i: (rev(i), 5)), pl.BlockSpec((BLK, 128), lambda i: (prev(i), 5)),
                  pl.BlockSpec((2 * BLK, D_ATTN), lambda i: (rev(i), 0)),
                  small((DEPTH, HD)), small((DEPTH, HD)), small((DEPTH, NQ)), small((NQ, BLK, 2 * BLK))] + [ANY_SPEC] * len(deps),
        out_specs=[pl.BlockSpec((2 * BLK, 768), lambda i: (rev(i), COL_QKV // 768)), small((NQ, BLK, 2 * BLK)), small((8, 128))],
        out_shape=_out_hbm([SDS((S, D_IN_PAD), bf16), SDS((NQ, BLK, 2 * BLK), f32), SDS((8, 128), f32)]),
        scratch_shapes=[pltpu.VMEM((BLK, 256), f32)], name="attn_bwd", compiler_params=_cparams(1),
    )(*_in_hbm([qkv, qkv, qkv, qkv, qkv, dmix, q_gain, k_gain, sinks, bias, *deps]))


CONV_TC = 256


def _shift_down(u, s):
    if s == 0:
        return u
    rows = lax.broadcasted_iota(jnp.int32, u.shape, 0)
    return jnp.where(rows >= s, pltpu.roll(u, s, 0), 0.0)


def _shift_up(u, s):
    if s == 0:
        return u
    rows = lax.broadcasted_iota(jnp.int32, u.shape, 0)
    return jnp.where(rows < u.shape[0] - s, pltpu.roll(u, u.shape[0] - s, 0), 0.0)


def _conv_specs():
    return [pl.BlockSpec((S, CONV_TC), lambda c: (0, c)),
            pl.BlockSpec((None, 4, CONV_TC), lambda c: (0, 0, c)),
            pl.BlockSpec((DEPTH, CONV_TC), lambda c: (0, c))]


def _conv_pre(u, w_ref, b_ref, layer):
    pre = b_ref[layer:layer + 1, :] + w_ref[3:4, :] * u
    for k in range(3):
        pre = pre + w_ref[k:k + 1, :] * _shift_down(u, 3 - k)
    return pre


def _conv_fwd(xbc, conv_w, conv_b, layer):
    def body(u_ref, w_ref, b_ref, o_ref):
        pre = _conv_pre(u_ref[...].astype(f32), w_ref, b_ref, layer)
        o_ref[...] = pre * _sigmoid(pre)

    specs = _conv_specs()
    specs[1] = pl.BlockSpec((None, 4, CONV_TC), lambda c: (layer, 0, c))
    return pl.pallas_call(
        body, grid=(D_CONV // CONV_TC,), in_specs=specs, out_specs=pl.BlockSpec((S, CONV_TC), lambda c: (0, c)),
        out_shape=_out_hbm(SDS((S, D_CONV), f32)), name="conv_fwd", compiler_params=_cparams(1),
    )(*_in_hbm([xbc, conv_w, conv_b]))


def _conv_bwd(xbc, dact, conv_w, conv_b, dproj, layer):
    def body(u_ref, w_ref, b_ref, da_ref, dproj_in, du_ref, dw_ref, db_ref):
        u = u_ref[...].astype(f32)
        pre = _conv_pre(u, w_ref, b_ref, layer)
        sg = _sigmoid(pre)
        dpre = da_ref[...] * (sg * (1.0 + pre * (1.0 - sg)))
        du = w_ref[3:4, :] * dpre
        for k in range(3):
            du = du + w_ref[k:k + 1, :] * _shift_up(dpre, 3 - k)
        du_ref[...] = du.astype(bf16)
        db_ref[...] = jnp.broadcast_to(jnp.sum(dpre, axis=0, keepdims=True), db_ref.shape)
        dw_ref[...] = jnp.zeros_like(dw_ref)
        for k in range(4):
            dw_ref[k:k + 1, :] = jnp.sum(dpre * _shift_down(u, 3 - k), axis=0, keepdims=True)

    specs = _conv_specs()
    specs[1] = pl.BlockSpec((None, 4, CONV_TC), lambda c: (layer, 0, c))
    col = pl.BlockSpec((S, CONV_TC), lambda c: (0, c))
    row8 = pl.BlockSpec((8, CONV_TC), lambda c: (0, c))
    return pl.pallas_call(
        body, grid=(D_CONV // CONV_TC,), in_specs=[*specs, col, ANY_SPEC],
        out_specs=[pl.BlockSpec((S, CONV_TC), lambda c: (0, COL_XBC // CONV_TC + c)), row8, row8],
        out_shape=_out_hbm([SDS((S, D_IN_PAD), bf16), SDS((8, D_CONV), f32), SDS((8, D_CONV), f32)]), name="conv_bwd",
        input_output_aliases={4: 0}, compiler_params=_cparams(1),
    )(*_in_hbm([xbc, conv_w, conv_b]), dact, *_in_hbm([dproj]))


def _tri():
    return (lax.broadcasted_iota(jnp.int32, (BLK, BLK), 0) >= lax.broadcasted_iota(jnp.int32, (BLK, BLK), 1))


def _ssd_scalars(dt_ref, dtb_ref, alog_ref, layer):
    raw = dt_ref[:, 0:NSSM] + dtb_ref[layer:layer + 1, :]
    dtv = jnp.maximum(raw, 0.0) + jnp.log(1.0 + jnp.exp(-jnp.abs(raw)))
    a = -jnp.exp(alog_ref[layer:layer + 1, :])
    acs = jnp.dot(_tri().astype(f32), dtv * a, preferred_element_type=f32, precision=HIGHEST)
    return raw, dtv, a, acs


HG = NSSM // NGRP
GW = HG * HD


def _lane_expand(cols, g):
    lane_head = lax.broadcasted_iota(jnp.int32, (1, GW), 1) // HD
    out = cols[:, HG * g + HG - 1:HG * g + HG]
    for r in range(HG - 2, -1, -1):
        out = jnp.where(lane_head == r, cols[:, HG * g + r:HG * g + r + 1], out)
    return out


def _row_expand(vals, g):
    row_head = lax.broadcasted_iota(jnp.int32, (GW, 1), 0) // HD
    out = vals[:, HG * g + HG - 1:HG * g + HG]
    for r in range(HG - 2, -1, -1):
        out = jnp.where(row_head == r, vals[:, HG * g + r:HG * g + r + 1], out)
    return out


def _head_rowsums(a, g):
    sel = (lax.broadcasted_iota(jnp.int32, (GW, NSSM), 0) // HD + HG * g == lax.broadcasted_iota(jnp.int32, (GW, NSSM), 1)).astype(bf16)
    hi = a.astype(bf16)
    lo = (a - hi.astype(f32)).astype(bf16)
    return _dot(hi, sel, NN_DIMS) + _dot(lo, sel, NN_DIMS)


def _head_blocksums(v, g):
    sel = (lax.broadcasted_iota(jnp.int32, (GW, NSSM), 0) // HD + HG * g == lax.broadcasted_iota(jnp.int32, (GW, NSSM), 1)).astype(bf16)
    hi = v.astype(bf16)
    lo = (v - hi.astype(f32)).astype(bf16)
    return _dot(hi, sel, TN_DIMS) + _dot(lo, sel, TN_DIMS)


def _ssd_chunk_common(xc_ref, dt_ref, dtb_ref, alog_ref, h_rows, layer):
    raw, dtv, a, acs = _ssd_scalars(dt_ref, dtb_ref, alog_ref, layer)
    acs_t = acs.T
    last = acs[BLK - 1:BLK, :]
    c = dict(raw=raw, dtv=dtv, a=a, acs=acs, last=last, dte=jnp.exp(last - acs), e_all=jnp.exp(acs), cd=jnp.exp(last))
    grp, heads, tri = range(NGRP), range(NSSM), _tri()
    c["bm"] = [xc_ref[:, pl.ds(D_SSM + NSTATE * g, NSTATE)] for g in grp]
    c["bm_b"] = [c["bm"][g].astype(bf16) for g in grp]
    c["cm_b"] = [xc_ref[:, pl.ds(D_SSM + NGRP * NSTATE + NSTATE * g, NSTATE)].astype(bf16) for g in grp]
    c["cb"] = [_dot(c["cm_b"][g], c["bm_b"][g], NT_DIMS) for g in grp]
    c["x"] = [xc_ref[:, pl.ds(GW * g, GW)] for g in grp]
    c["dt"] = [_lane_expand(dtv, g) for g in grp]
    c["xdt"] = [c["x"][g] * c["dt"][g] for g in grp]
    c["xdt_b"] = [c["xdt"][g].astype(bf16) for g in grp]
    c["prev"] = [h_rows(g) for g in grp]
    c["prev_b"] = [c["prev"][g].astype(bf16) for g in grp]
    c["e"] = [_lane_expand(c["e_all"], g) for g in grp]
    c["y_off"] = [_dot(c["cm_b"][g], c["prev_b"][g], NT_DIMS) * c["e"][g] for g in grp]
    c["decay"] = [jnp.exp(jnp.where(tri, acs[:, h:h + 1] - acs_t[h:h + 1, :], -jnp.inf)) for h in heads]
    c["m"] = [c["cb"][h // HG] * c["decay"][h] for h in heads]
    c["m_b"] = [c["m"][h].astype(bf16) for h in heads]
    c["dte_x"] = [_lane_expand(c["dte"], g) for g in grp]
    c["xdte_b"] = [(c["xdt"][g] * c["dte_x"][g]).astype(bf16) for g in grp]
    return c


def _ssd_fwd(xact, z, dt, attn, dt_bias, a_log, d_skip, norm_g, layer):
    def body(xc_ref, z_ref, dt_ref, at_ref, dtb_ref, alog_ref, dsk_ref, ng_ref, mix_ref, hs_ref, y_ref, h_ref):
        n = pl.program_id(0)

        @pl.when(n == 0)
        def _():
            h_ref[...] = jnp.zeros_like(h_ref)

        hs_ref[...] = h_ref[...]
        c = _ssd_chunk_common(xc_ref, dt_ref, dtb_ref, alog_ref, lambda g: h_ref[pl.ds(GW * g, GW), :], layer)
        grp, heads = range(NGRP), range(NSSM)
        y_diag = [_dot(c["m_b"][h], c["xdt_b"][h // HG][:, HD * (h % HG):HD * (h % HG + 1)], NN_DIMS) for h in heads]
        new_st = [_dot(c["xdte_b"][g], c["bm_b"][g], TN_DIMS) for g in grp]
        for h in heads:
            y_ref[:, pl.ds(HD * h, HD)] = y_diag[h]
        dskip = dsk_ref[layer:layer + 1, :]
        for g in grp:
            cols = pl.ds(GW * g, GW)
            y_ref[:, cols] = y_ref[:, cols] + c["y_off"][g] + c["x"][g] * _lane_expand(dskip, g)
            h_ref[cols, :] = c["prev"][g] * _row_expand(c["cd"], g) + new_st[g]
        zv = z_ref[...].astype(f32)
        yz = y_ref[...] * (zv * _sigmoid(zv))
        mix_ref[:, 0:D_ATTN] = at_ref[...]
        for g in grp:
            yg = yz[:, GW * g:GW * (g + 1)]
            rs = lax.rsqrt(jnp.mean(yg * yg, axis=-1, keepdims=True) + EPS)
            mix_ref[:, D_ATTN + GW * g:D_ATTN + GW * (g + 1)] = (yg * rs * ng_ref[layer:layer + 1, GW * g:GW * (g + 1)]).astype(bf16)

    small = lambda shape: pl.BlockSpec(shape, lambda n: (0,) * len(shape))
    return pl.pallas_call(
        body, grid=(NBLK,),
        in_specs=[pl.BlockSpec((BLK, D_CONV), lambda n: (n, 0)), pl.BlockSpec((BLK, D_SSM), lambda n: (n, 0)),
                  pl.BlockSpec((BLK, 128), lambda n: (n, 0)), pl.BlockSpec((BLK, D_ATTN), lambda n: (n, 0)),
                  small((DEPTH, NSSM)), small((DEPTH, NSSM)), small((DEPTH, NSSM)), small((DEPTH, D_SSM))],
        out_specs=[pl.BlockSpec((BLK, D), lambda n: (n, 0)), pl.BlockSpec((None, NSSM * HD, NSTATE), lambda n: (n, 0, 0)),
                   pl.BlockSpec((BLK, D_SSM), lambda n: (n, 0))],
        out_shape=_out_hbm([SDS((S, D), bf16), SDS((NBLK, NSSM * HD, NSTATE), f32), SDS((S, D_SSM), f32)]),
        scratch_shapes=[pltpu.VMEM((NSSM * HD, NSTATE), f32)],
        name="ssd_fwd", compiler_params=_cparams(1),
    )(*_in_hbm([xact, z, dt]), attn, *_in_hbm([dt_bias, a_log, d_skip, norm_g]))


def _ssd_bwd(xact, z, dt, dmix, hs, y, dt_bias, a_log, d_skip, norm_g, dproj, layer):
    def body(xc_ref, z_ref, dt_ref, do_ref, hs_ref, y_ref, dtb_ref, alog_ref, dsk_ref, ng_ref, dproj_in,
             dzdt_ref, dx_ref, dsm_ref, dh_ref, dy_ref):
        i = pl.program_id(0)

        @pl.when(i == 0)
        def _():
            dh_ref[...] = jnp.zeros_like(dh_ref)
            dsm_ref[...] = jnp.zeros_like(dsm_ref)

        c = _ssd_chunk_common(xc_ref, dt_ref, dtb_ref, alog_ref, lambda g: hs_ref[pl.ds(GW * g, GW), :], layer)
        raw, dtv, a = c["raw"], c["dtv"], c["a"]
        grp, heads = range(NGRP), range(NSSM)
        dskip = dsk_ref[layer:layer + 1, :]
        lane8 = lax.broadcasted_iota(jnp.int32, (1, NSSM), 1)
        sub8 = lax.broadcasted_iota(jnp.int32, (NSSM, 1), 0)

        zv = z_ref[...].astype(f32)
        sz = _sigmoid(zv)
        gz = zv * sz
        yv = y_ref[...]
        yz = yv * gz
        for g in grp:
            sl = slice(GW * g, GW * (g + 1))
            yg = yz[:, sl]
            rs = lax.rsqrt(jnp.mean(yg * yg, axis=-1, keepdims=True) + EPS)
            yhat = yg * rs
            dog = do_ref[:, sl]
            w = dog * ng_ref[layer:layer + 1, sl]
            dyz = rs * (w - yhat * jnp.mean(yhat * w, axis=-1, keepdims=True))
            dsm_ref[0:1, sl] += jnp.sum(dog * yhat, axis=0, keepdims=True)
            dy_ref[:, sl] = dyz * gz[:, sl]
            dzdt_ref[:, sl] = (dyz * yv[:, sl] * (sz[:, sl] * (1.0 + zv[:, sl] * (1.0 - sz[:, sl])))).astype(bf16)

        dy = [dy_ref[:, pl.ds(GW * g, GW)] for g in grp]
        dy_b = [dy[g].astype(bf16) for g in grp]
        hl = lambda h: slice(HD * (h % HG), HD * (h % HG + 1))
        dt_off_b = [(dy[g] * c["e"][g]).astype(bf16) for g in grp]
        dcm = [_dot(dt_off_b[g], c["prev_b"][g], NN_DIMS) for g in grp]
        dprev = [_dot(dt_off_b[g], c["cm_b"][g], TN_DIMS) for g in grp]
        yoff_rs = [_head_rowsums(dy[g] * c["y_off"][g], g) for g in grp]
        dhn = [dh_ref[pl.ds(GW * g, GW), :] for g in grp]
        dhn_b = [dhn[g].astype(bf16) for g in grp]
        dprev = [dprev[g] + dhn[g] * _row_expand(c["cd"], g) for g in grp]
        dhn_prev = [dhn[g] * c["prev"][g] for g in grp]
        u = [_dot(c["bm_b"][g], dhn_b[g], NT_DIMS) for g in grp]
        dbm = [_dot(c["xdte_b"][g], dhn_b[g], NN_DIMS) for g in grp]
        ddte_rs = [_head_rowsums(c["xdt"][g] * u[g], g) for g in grp]
        dm = [_dot(dy_b[h // HG][:, hl(h)], c["xdt_b"][h // HG][:, hl(h)], NT_DIMS) for h in heads]
        dxdt_in = [_dot(c["m_b"][h], dy_b[h // HG][:, hl(h)], TN_DIMS) for h in heads]
        dseg = [dm[h] * c["m"][h] for h in heads]
        dmd = [dm[h] * c["decay"][h] for h in heads]
        for h in heads:
            dx_ref[:, pl.ds(HD * h, HD)] = dxdt_in[h]

        tmp = (ddte_rs[0] + ddte_rs[1]) * c["dte"]
        dacs = yoff_rs[0] + yoff_rs[1] - tmp
        dacs_cols = jnp.zeros((NSSM, BLK), f32)
        ddtv = jnp.zeros((BLK, NSSM), f32)
        ddsk = jnp.zeros((BLK, NSSM), f32)
        hp = jnp.zeros((1, NSSM), f32)
        for g in grp:
            cols = pl.ds(GW * g, GW)
            dxdt = dx_ref[:, cols] + u[g] * c["dte_x"][g]
            dx_ref[:, cols] = dy[g] * _lane_expand(dskip, g) + dxdt * c["dt"][g]
            ddtv = ddtv + _head_rowsums(dxdt * c["x"][g], g)
            ddsk = ddsk + _head_rowsums(dy[g] * c["x"][g], g)
            dcb = dmd[HG * g]
            for r in range(1, HG):
                dcb = dcb + dmd[HG * g + r]
            dcb_b = dcb.astype(bf16)
            dx_ref[:, pl.ds(D_SSM + NSTATE * g, NSTATE)] = dbm[g] + _dot(dcb_b, c["cm_b"][g], TN_DIMS)
            dx_ref[:, pl.ds(D_SSM + NGRP * NSTATE + NSTATE * g, NSTATE)] = dcm[g] + _dot(dcb_b, c["bm_b"][g], NN_DIMS)
            dh_ref[cols, :] = dprev[g]
            hp = hp + _head_blocksums(jnp.sum(dhn_prev[g], axis=1, keepdims=True), g)
            for r in range(HG):
                h = HG * g + r
                dacs = dacs + (lane8 == h).astype(f32) * jnp.sum(dseg[h], axis=1, keepdims=True)
                dacs_cols = dacs_cols + (sub8 == h).astype(f32) * jnp.sum(dseg[h], axis=0, keepdims=True)
        dlast = hp * c["cd"] + jnp.sum(tmp, axis=0, keepdims=True)
        ddsk = jnp.sum(ddsk, axis=0, keepdims=True)

        row = lax.broadcasted_iota(jnp.int32, (BLK, 1), 0)
        dacs = dacs - dacs_cols.T + jnp.where(row == BLK - 1, dlast, 0.0)
        dda = lax.dot_general(_tri().astype(f32), dacs, TN_DIMS, preferred_element_type=f32, precision=HIGHEST)
        ddtv = ddtv + dda * a
        da = jnp.sum(dda * dtv, axis=0, keepdims=True)
        draw = ddtv * _sigmoid(raw)
        dzdt_ref[:, D_SSM:] = jnp.zeros((BLK, COL_XBC - COL_DT), bf16)
        dzdt_ref[:, D_SSM:D_SSM + NSSM] = draw.astype(bf16)
        dsm_ref[1:2, 0:NSSM] += jnp.sum(draw, axis=0, keepdims=True)
        dsm_ref[2:3, 0:NSSM] += da * a
        dsm_ref[3:4, 0:NSSM] += ddsk

    rev = lambda i: NBLK - 1 - i
    small = lambda shape: pl.BlockSpec(shape, lambda i: (0,) * len(shape))
    return pl.pallas_call(
        body, grid=(NBLK,),
        in_specs=[pl.BlockSpec((BLK, D_CONV), lambda i: (rev(i), 0)), pl.BlockSpec((BLK, D_SSM), lambda i: (rev(i), 0)),
                  pl.BlockSpec((BLK, 128), lambda i: (rev(i), 0)), pl.BlockSpec((BLK, D_SSM), lambda i: (rev(i), 1)),
                  pl.BlockSpec((None, NSSM * HD, NSTATE), lambda i: (rev(i), 0, 0)), pl.BlockSpec((BLK, D_SSM), lambda i: (rev(i), 0)),
                  small((DEPTH, NSSM)), small((DEPTH, NSSM)), small((DEPTH, NSSM)), small((DEPTH, D_SSM)), ANY_SPEC],
        out_specs=[pl.BlockSpec((BLK, COL_XBC - COL_Z), lambda i: (rev(i), COL_Z // (COL_XBC - COL_Z))),
                   pl.BlockSpec((BLK, D_CONV), lambda i: (rev(i), 0)), small((8, D_SSM))],
        out_shape=[pltpu.HBM((S, D_IN_PAD), bf16), SDS((S, D_CONV), f32), SDS((8, D_SSM), f32)],
        scratch_shapes=[pltpu.VMEM((NSSM * HD, NSTATE), f32), pltpu.VMEM((BLK, D_SSM), f32)],
        name="ssd_bwd", input_output_aliases={10: 0}, compiler_params=_cparams(1),
    )(*_in_hbm([xact, z, dt, dmix, hs, y, dt_bias, a_log, d_skip, norm_g, dproj]))


def _my_place():
    return lax.axis_index("x"), lax.axis_index("y"), lax.axis_index("c")


def _dev_index(px, py, pc):
    return 4 * px + 2 * py + pc


def _slab2(kind, ref, idx):
    if kind == "stack":
        return ref.at[idx]
    if kind == "rows128":
        return ref.at[pl.ds(pl.multiple_of(idx * 128, 128), 128), :]
    if kind == "rows512":
        return ref.at[pl.ds(pl.multiple_of(idx * 512, 512), 512), :]
    return ref.at[:, pl.ds(pl.multiple_of(idx * 512, 512), 512)]


def _slab_shape(kind, full_shape):
    if kind == "stack":
        return tuple(full_shape[1:])
    if kind == "rows128":
        return (128, full_shape[1])
    if kind == "rows512":
        return (512, full_shape[1])
    return (full_shape[0], 512)


KIND = dict(w_in="stack", w_out="rows128", w_up="cols512", w_down="rows512", conv_w="stack")
FULL_SHAPE = dict(w_in=(N_DEV, D, D_IN // N_DEV), w_out=(D, D), w_up=(D, D_FF), w_down=(D_FF, D))
HBM_SPEC = pl.BlockSpec(memory_space=pltpu.HBM)
SEM_SPEC = pl.BlockSpec(memory_space=pltpu.SEMAPHORE)
SIDE_EFFECT = pltpu.SideEffectType.DATAFLOW_SIDE_EFFECTING


def _peers_all():
    x, y, c = _my_place()
    return [(x ^ ((r >> 2) & 1), y ^ ((r >> 1) & 1), c ^ (r & 1)) for r in range(1, N_DEV)]


def _split_start(name, bufs, n_copies, plan, deps=()):
    nb = len(bufs)

    def body(*refs):
        ins = refs[:nb]
        send_sems, recv_sems = refs[nb + len(deps)], refs[nb + len(deps) + 1]
        token = refs[-1]
        for i, (src, dst, dev) in enumerate(plan(ins)):
            pltpu.make_async_remote_copy(src_ref=src, dst_ref=dst, send_sem=send_sems.at[i], recv_sem=recv_sems.at[i],
                                         device_id=dev, device_id_type=MESH).start()
        token[...] = jnp.zeros_like(token)

    outs = pl.pallas_call(
        body, name=name,
        out_shape=(pltpu.SemaphoreType.DMA((n_copies,)), pltpu.SemaphoreType.DMA((n_copies,)),
                   *[pltpu.HBM(b.shape, b.dtype) for b in bufs], SDS((8, 128), f32)),
        in_specs=[HBM_SPEC] * nb + [ANY_SPEC] * len(deps),
        out_specs=(SEM_SPEC, SEM_SPEC, *[HBM_SPEC] * nb, pl.BlockSpec(memory_space=pltpu.VMEM)),
        input_output_aliases={i: 2 + i for i in range(nb)},
        compiler_params=pltpu.CompilerParams(has_side_effects=SIDE_EFFECT),
    )(*[pltpu.with_memory_space_constraint(b, pltpu.HBM) for b in bufs], *deps)
    return dict(send=outs[0], recv=outs[1], bufs=list(outs[2:2 + nb]), token=outs[-1], plan=plan, n=n_copies)


def _split_wait(name, started, after):
    bufs = started["bufs"]
    nb = len(bufs)
    plan = started["plan"]

    def body(*refs):
        ins = refs[:nb]
        send_sems, recv_sems = refs[nb], refs[nb + 1]
        for i, (src, dst, dev) in enumerate(plan(ins)):
            cp = pltpu.make_async_remote_copy(src_ref=src, dst_ref=dst, send_sem=send_sems.at[i], recv_sem=recv_sems.at[i],
                                              device_id=dev, device_id_type=MESH)
            cp.wait_send()
            cp.wait_recv()

    outs = pl.pallas_call(
        body, name=name, out_shape=tuple(pltpu.HBM(b.shape, b.dtype) for b in bufs),
        in_specs=[HBM_SPEC] * nb + [SEM_SPEC, SEM_SPEC] + [ANY_SPEC] * len(after), out_specs=(HBM_SPEC,) * nb,
        input_output_aliases={i: i for i in range(nb)},
        compiler_params=pltpu.CompilerParams(has_side_effects=SIDE_EFFECT),
    )(*bufs, started["send"], started["recv"], *after)
    return list(outs)


def _gather_start(name, names, fulls, deps):
    n_t = len(names)

    def plan(refs):
        x, y, c = _my_place()
        my_idx = _dev_index(x, y, c)
        targets = [(x, y, 1 - c), (1 - x, y, c), (x, 1 - y, c), (1 - x, 1 - y, c)]
        slabs = [_slab2(KIND[names[t]], refs[t], my_idx) for t in range(n_t)]
        return [(slabs[t], slabs[t], dev) for t in range(n_t) for dev in targets]

    return _split_start(name, list(fulls), 4 * n_t, plan, deps)


def _gather_finish(name, names, started, after):
    n_t = len(names)
    fulls = _split_wait(name + "_wait", started, after)
    slab_shapes = [SDS(_slab_shape(KIND[n], f.shape), f.dtype) for n, f in zip(names, fulls)]

    def body(*refs):
        ins = refs[:n_t]
        outs = refs[n_t:2 * n_t]
        stage = refs[2 * n_t:3 * n_t]
        load_sems, send_sems, recv_sems = refs[3 * n_t:]
        x, y, c = _my_place()
        chips = [(1 - x, y), (x, 1 - y), (1 - x, 1 - y)]
        pairs = [(t, j) for t in range(n_t) for j in range(3)]
        loads = [pltpu.make_async_copy(_slab2(KIND[names[t]], ins[t], _dev_index(*chips[j], c)), stage[t].at[j], load_sems.at[t, j])
                 for t, j in pairs]
        for cp in loads:
            cp.start()

        def copy(t, j, core):
            return pltpu.make_async_remote_copy(
                src_ref=stage[t].at[j], dst_ref=_slab2(KIND[names[t]], outs[t], _dev_index(*chips[j], core)),
                send_sem=send_sems.at[t, j], recv_sem=recv_sems.at[t, j], device_id=(x, y, 1 - c), device_id_type=MESH)

        sends = [copy(t, j, c) for t, j in pairs]
        for ld, cp in zip(loads, sends):
            ld.wait()
            cp.start()
        for t, j in pairs:
            copy(t, j, 1 - c).wait_recv()
        for cp in sends:
            cp.wait_send()

    return pl.pallas_call(
        body, in_specs=[HBM_SPEC] * n_t, out_specs=[HBM_SPEC] * n_t, out_shape=[pltpu.HBM(b.shape, b.dtype) for b in fulls],
        input_output_aliases={t: t for t in range(n_t)},
        scratch_shapes=[pltpu.VMEM((3,) + s.shape, s.dtype) for s in slab_shapes]
        + [pltpu.SemaphoreType.DMA((n_t, 3)), pltpu.SemaphoreType.DMA((n_t, 3)), pltpu.SemaphoreType.DMA((n_t, 3))],
        name=name + "_pass", compiler_params=pltpu.CompilerParams(vmem_limit_bytes=VMEM_LIMIT),
    )(*fulls)


def _exchange_start(name, names, grads, deps):
    n_t = len(names)
    lands = [lax.empty((N_DEV,) + _slab_shape(KIND[n], g.shape), g.dtype) for n, g in zip(names, grads)]

    def plan(refs):
        my_idx = _dev_index(*_my_place())
        return [(_slab2(KIND[names[t]], refs[t], _dev_index(*peer)), refs[n_t + t].at[my_idx], peer)
                for t in range(n_t) for peer in _peers_all()]

    return _split_start(name, list(grads) + lands, 7 * n_t, plan, deps)


def _small_exchange_start(part, deps):
    land = lax.empty((N_DEV,) + part.shape, part.dtype)

    def plan(refs):
        my_idx = _dev_index(*_my_place())
        return [(refs[0], refs[1].at[my_idx], peer) for peer in _peers_all()]

    return _split_start("small_exchange", [part, land], N_DEV - 1, plan, deps)


def _slab_pieces():
    sh = D_IN // N_DEV
    out = []
    for j in range(N_DEV):
        for first, end, dst in IN_SEGMENTS:
            lo, hi = max(first, sh * j), min(end, sh * (j + 1))
            if lo < hi:
                out.append((j, lo - sh * j, hi - sh * j, dst + lo - first))
    return out


def _w_in_assemble(stacked):
    tr = 256
    sh = D_IN // N_DEV

    def body(i_ref, o_ref):
        o_ref[:, COL_DT:COL_XBC] = jnp.zeros((tr, COL_XBC - COL_DT), bf16)
        for j, lo, hi, dst in _slab_pieces():
            o_ref[:, dst:dst + hi - lo] = i_ref[j, :, lo:hi]

    return pl.pallas_call(
        body, grid=(D // tr,), in_specs=[pl.BlockSpec((N_DEV, tr, sh), lambda i: (0, i, 0))],
        out_specs=pl.BlockSpec((None, tr, D_IN_PAD), lambda i: (0, i, 0)), out_shape=SDS((1, D, D_IN_PAD), bf16),
        name="w_in_assemble", compiler_params=_cparams(1),
    )(*_in_hbm([stacked]))


def _w_in_slabs(dw_in):
    tr = 256
    sh = D_IN // N_DEV

    def body(i_ref, o_ref):
        for j, lo, hi, src in _slab_pieces():
            o_ref[j, :, lo:hi] = i_ref[:, src:src + hi - lo]

    return pl.pallas_call(
        body, grid=(D // tr,), in_specs=[pl.BlockSpec((tr, D_IN_PAD), lambda i: (i, 0))],
        out_specs=pl.BlockSpec((N_DEV, tr, sh), lambda i: (0, i, 0)), out_shape=_out_hbm(SDS((N_DEV, D, sh), bf16)),
        name="w_in_slabs", compiler_params=_cparams(1),
    )(dw_in)


SMALL_NAMES = ("mix_norm_g", "mlp_norm_g", "conv_b", "ssm_norm_g", "q_gain", "k_gain", "sinks", "dt_bias", "a_log", "d_skip",
               "rel_bias", "conv_w")
MISC_LANES = dict(q_gain=(LANE_QG, HD), k_gain=(LANE_KG, HD), sinks=(LANE_SINK, NQ), dt_bias=(LANE_DTB, NSSM),
                  a_log=(LANE_ALOG, NSSM), d_skip=(LANE_DSKIP, NSSM))


def _pack_small_grads(smalls, drel_t, loss):
    def body(*refs):
        o_ref = refs[-1]
        drel_ref, loss_ref = refs[-3], refs[-2]
        o_ref[...] = jnp.zeros_like(o_ref)
        for l in range(DEPTH):
            mixg, mlpg, convb, convw, ssd, attn = refs[6 * l:6 * l + 6]
            o_ref[ROW_MIXG + l:ROW_MIXG + l + 1, :] = mixg[...]
            o_ref[ROW_MLPG + l:ROW_MLPG + l + 1, :] = mlpg[...]
            o_ref[ROW_CONVB + l:ROW_CONVB + l + 1, :] = convb[0:1, :]
            o_ref[ROW_SSMG + l:ROW_SSMG + l + 1, 0:D_SSM] = ssd[0:1, :]
            o_ref[ROW_CONVW + 4 * l:ROW_CONVW + 4 * l + 4, :] = convw[0:4, :]
            row = slice(ROW_MISC + l, ROW_MISC + l + 1)
            o_ref[row, LANE_QG:LANE_QG + HD] = attn[0:1, 0:HD]
            o_ref[row, LANE_KG:LANE_KG + HD] = attn[1:2, 0:HD]
            o_ref[row, LANE_SINK:LANE_SINK + NQ] = attn[2:3, 0:NQ]
            o_ref[row, LANE_DTB:LANE_DTB + NSSM] = ssd[1:2, 0:NSSM]
            o_ref[row, LANE_ALOG:LANE_ALOG + NSSM] = ssd[2:3, 0:NSSM]
            o_ref[row, LANE_DSKIP:LANE_DSKIP + NSSM] = ssd[3:4, 0:NSSM]
        o_ref[ROW_RELB:ROW_RELB + NQ, 0:N_BUCKETS] = drel_ref[...]
        o_ref[ROW_LOSS:ROW_LOSS + 1, 0:1] = loss_ref[0:1, 0:1]

    args = []
    for sm in smalls:
        args += [sm["mix_norm_g"], sm["mlp_norm_g"], sm["conv_b"], sm["conv_w"], sm["ssd"], sm["attn"]]
    args += [drel_t, loss]
    return pl.pallas_call(body, out_shape=SDS((SMALL_ROWS, D), f32), name="pack_small_grads")(*args)


def _adamw_small(part, land, w, m, v):
    n = len(SMALL_NAMES)

    def grad_of(name, g_ref):
        if name == "mix_norm_g":
            return g_ref[ROW_MIXG:ROW_MIXG + DEPTH, :]
        if name == "mlp_norm_g":
            return g_ref[ROW_MLPG:ROW_MLPG + DEPTH, :]
        if name == "conv_b":
            return g_ref[ROW_CONVB:ROW_CONVB + DEPTH, :]
        if name == "ssm_norm_g":
            return g_ref[ROW_SSMG:ROW_SSMG + DEPTH, 0:D_SSM]
        if name == "rel_bias":
            return g_ref[ROW_RELB:ROW_RELB + NQ, 0:N_BUCKETS].T
        lane, width = MISC_LANES[name]
        return g_ref[ROW_MISC:ROW_MISC + DEPTH, lane:lane + width]

    def body(part_ref, land_ref, *refs):
        ws, ms, vs = refs[:n], refs[n:2 * n], refs[2 * n:3 * n]
        loss_ref = refs[3 * n]
        outs = refs[3 * n + 1:-1]
        g_ref = refs[-1]
        me = _dev_index(*_my_place())
        for p in range(N_DEV):
            term = jnp.where(me == p, part_ref[...], land_ref[p])
            if p == 0:
                g_ref[...] = term
            else:
                g_ref[...] += term
        loss_ref[...] = g_ref[ROW_LOSS:ROW_LOSS + 1, 0:128]
        my_cols = pl.ds(pl.multiple_of(me * 128, 128), 128)
        for k, name in enumerate(SMALL_NAMES):
            g_out, d_out, m_out, v_out = outs[4 * k:4 * k + 4]
            if name == "conv_w":
                for l in range(DEPTH):
                    g = g_ref[ROW_CONVW + 4 * l:ROW_CONVW + 4 * l + 4, my_cols]
                    delta, m_new, v_new = _adamw_math(ws[k][l], ms[k][l], vs[k][l], g)
                    g_out[l], d_out[l], m_out[l], v_out[l] = g, delta, m_new, v_new
            else:
                g = grad_of(name, g_ref)
                delta, m_new, v_new = _adamw_math(ws[k][...], ms[k][...], vs[k][...], g)
                g_out[...], d_out[...], m_out[...], v_out[...] = g, delta, m_new, v_new

    ws = [w[name] for name in SMALL_NAMES]
    out_shape = [SDS((1, 128), f32)]
    for a in ws:
        out_shape += [SDS(a.shape, f32)] * 4
    return pl.pallas_call(body, out_shape=out_shape, name="adamw_small", scratch_shapes=[pltpu.VMEM((SMALL_ROWS, D), f32)])(
        part, land, *ws, *[m[name] for name in SMALL_NAMES], *[v[name] for name in SMALL_NAMES])


def _plain(tm, tn):
    return pl.BlockSpec((tm, tn), lambda i, j, k: (i, j))


def _rowblk(tm, width):
    return pl.BlockSpec((tm, width), lambda i, j, k: (i, 0))


def _store_epi(dtype):
    def epi(acc, i, j, ex, outs):
        outs[0][...] = acc.astype(dtype)
    return epi


def _rms_prologue(layer):
    def pro(a_ref, ex, outs):
        xv = a_ref[...]
        r = lax.rsqrt(jnp.mean(xv * xv, axis=-1, keepdims=True) + EPS)
        h = (xv * r * ex[0][layer:layer + 1, :]).astype(bf16)
        outs[-1][...] = h
        return h
    return pro


MLP_TM = 256
MLP_VMEM = 56 * 1024 * 1024


def _resident(shape):
    return pl.BlockSpec((None,) + shape, lambda i: (0, 0, 0), pipeline_mode=pl.Buffered(1))


def _mlp_fwd(layer, x, mix, g, w_out, w_up, w_down, tgt=None):
    tm = MLP_TM
    with_loss = tgt is not None

    def body(x_ref, mix_ref, g_ref, wo_ref, wu_ref, wd_ref, *rest):
        xm_ref, a_ref, r_ref, h_ref = rest[with_loss:with_loss + 4]
        rest = rest[:with_loss] + rest[with_loss + 1:]
        i = pl.program_id(0)
        xv = x_ref[...] + _dot(mix_ref[...], wo_ref[...], NN_DIMS)
        xm_ref[...] = xv
        h = (xv * lax.rsqrt(jnp.mean(xv * xv, axis=-1, keepdims=True) + EPS) * g_ref[layer:layer + 1, :]).astype(bf16)
        h_ref[...] = h
        r = jnp.maximum(_dot(h, wu_ref[...], NN_DIMS), 0.0)
        a = (r * r).astype(bf16)
        a_ref[...] = a
        r_ref[...] = r.astype(bf16)
        y = xv + _dot(a, wd_ref[...], NN_DIMS)
        if not with_loss:
            rest[3][...] = y
            return
        err = y - rest[0][...]
        rest[4][...] = err * (1.0 / D)
        part = 0.5 * jnp.sum(jnp.mean(err * err, axis=-1, keepdims=True), axis=0, keepdims=True)

        @pl.when(i == 0)
        def _():
            rest[5][...] = jnp.zeros_like(rest[5])

        rest[5][...] += jnp.broadcast_to(part, rest[5].shape)

    row = lambda width: pl.BlockSpec((tm, width), lambda i: (i, 0))
    in_specs = [row(D), row(D), pl.BlockSpec((DEPTH, D), lambda i: (0, 0)), _resident((D, D)), _resident((D, D_FF)),
                _resident((D_FF, D))]
    out_specs = [row(D), row(D_FF), row(D_FF), row(D), row(D)]
    out_shape = [SDS((S, D), f32), SDS((S, D_FF), bf16), SDS((S, D_FF), bf16), SDS((S, D), bf16), SDS((S, D), f32)]
    args = [x, mix, g, w_out, w_up, w_down]
    if with_loss:
        in_specs.append(row(D))
        args.append(tgt)
        out_specs.append(pl.BlockSpec((1, 128), lambda i: (0, 0)))
        out_shape.append(SDS((1, 128), f32))
    return pl.pallas_call(
        body, grid=(S // tm,), in_specs=in_specs, out_specs=out_specs, out_shape=_out_hbm(out_shape),
        name="mlp_fwd_loss" if with_loss else "mlp_fwd",
        compiler_params=pltpu.CompilerParams(dimension_semantics=("arbitrary",), vmem_limit_bytes=MLP_VMEM),
    )(*_in_hbm(args[:3]), *args[3:6], *_in_hbm(args[6:]))


def _mlp_bwd_act(layer, dx_out, r_act, x_mid, g, w_down, w_up, w_out, deps):
    tm = MLP_TM

    def body(dxo_ref, r_ref, xm_ref, g_ref, wd_ref, wu_ref, wo_ref, *rest):
        du_ref, dx_ref, dg_ref, dmix_ref = rest[len(deps):]
        dxo = dxo_ref[...]
        du = (_dot(dxo.astype(bf16), wd_ref[...], NT_DIMS) * (2.0 * r_ref[...].astype(f32))).astype(bf16)
        du_ref[...] = du
        dh = _dot(du, wu_ref[...], NT_DIMS)
        _rms_bwd_epilogue(layer)(dh, pl.program_id(0), 0, (xm_ref, g_ref, dxo_ref), (dx_ref, dg_ref))
        dmix_ref[...] = _dot(dx_ref[...].astype(bf16), wo_ref[...], NT_DIMS)

    row = lambda width: pl.BlockSpec((tm, width), lambda i: (i, 0))
    return pl.pallas_call(
        body, grid=(S // tm,),
        in_specs=[row(D), row(D_FF), row(D), pl.BlockSpec((DEPTH, D), lambda i: (0, 0)), _resident((D_FF, D)), _resident((D, D_FF)),
                  _resident((D, D))] + [ANY_SPEC] * len(deps),
        out_specs=[row(D_FF), row(D), pl.BlockSpec((1, D), lambda i: (0, 0)), row(D)],
        out_shape=_out_hbm([SDS((S, D_FF), bf16), SDS((S, D), f32), SDS((1, D), f32), SDS((S, D), f32)]), name="mlp_bwd_act",
        compiler_params=pltpu.CompilerParams(dimension_semantics=("arbitrary",), vmem_limit_bytes=MLP_VMEM),
    )(*_in_hbm([dx_out, r_act, x_mid, g]), w_down, w_up, w_out, *_in_hbm(deps))


def _layer_fwd(l, x, p, get_weights, bias, tgt=None):
    wts = get_weights(l, "in", [x, bias])
    gfull = pl.BlockSpec((DEPTH, D), lambda i, j, k: (0, 0))
    tm = 512

    def inproj_epi(acc, i, j, ex, outs):
        outs[0][...] = acc[:, COL_QKV:COL_Z].astype(bf16)
        outs[1][...] = acc[:, COL_Z:COL_DT].astype(bf16)
        outs[2][...] = acc[:, COL_XBC:D_IN_PAD].astype(bf16)
        outs[3][...] = acc[:, COL_DT:COL_DT + 128]

    qkv, z, xbc, dt, h1 = _matmul(
        "in_proj", "nn", x, wts["w_in"], tm=tm, tn=D_IN_PAD, tk=D, prologue=_rms_prologue(l),
        extras=(p["mix_norm_g"],), extra_specs=(gfull,),
        out_shape=[SDS((S, 768), bf16), SDS((S, 512), bf16), SDS((S, 1024), bf16), SDS((S, 128), f32), SDS((S, D), bf16)],
        out_specs=[_rowblk(tm, 768), _rowblk(tm, 512), _rowblk(tm, 1024), _rowblk(tm, 128), _rowblk(tm, D)], epilogue=inproj_epi)
    attn = _attn_fwd(qkv, p["q_gain"], p["k_gain"], p["sinks"], bias, l)
    xact = _conv_fwd(xbc, wts["conv_w"], p["conv_b"], l)
    mix, hs, y_ssd = _ssd_fwd(xact, z, dt, attn, p["dt_bias"], p["a_log"], p["d_skip"], p["ssm_norm_g"], l)
    wts = dict(wts, **get_weights(l, "rest", [mix]))

    x_mid, a_act, r_act, h2, *result = _mlp_fwd(l, x, mix, p["mlp_norm_g"], wts["w_out"], wts["w_up"], wts["w_down"], tgt)
    saved = dict(x=x, h1=h1, qkv=qkv, z=z, xbc=xbc, dt=dt, xact=xact, mix=mix, hs=hs, y_ssd=y_ssd, x_mid=x_mid, h2=h2,
                 a=a_act, r=r_act, wts=wts)
    return (result[0] if tgt is None else tuple(result)), saved


def _layer_bwd(l, dx_out, sv, p, bias, deps, send):
    wts = sv["wts"]

    dw_down = _matmul("dw_down", "tn", sv["a"], dx_out, tm=512, tn=D, tk=S, out_shape=SDS((D_FF, D), bf16),
                      out_specs=_plain(512, D), epilogue=_store_epi(bf16), deps=deps)
    deps = send(l, dict(w_down=dw_down))
    du, dx_mid, dg_mlp, dmix = _mlp_bwd_act(l, dx_out, sv["r"], sv["x_mid"], p["mlp_norm_g"], wts["w_down"], wts["w_up"],
                                            wts["w_out"], deps)
    dw_up = _matmul("dw_up", "tn", sv["h2"], du, tm=D, tn=512, tk=S, out_shape=SDS((D, D_FF), bf16),
                    out_specs=_plain(D, 512), epilogue=_store_epi(bf16))
    dw_out = _matmul("dw_out", "tn", sv["mix"], dx_mid, tm=D, tn=512, tk=S, out_shape=SDS((D, D), bf16),
                     out_specs=_plain(D, 512), epilogue=_store_epi(bf16))
    deps = send(l, dict(w_up=dw_up, w_out=dw_out))
    gfull = pl.BlockSpec((DEPTH, D), lambda i, j, k: (0, 0))
    grow = pl.BlockSpec((1, D), lambda i, j, k: (0, 0))
    dproj, dbias, dsm_attn = _attn_bwd(sv["qkv"], dmix, p["q_gain"], p["k_gain"], p["sinks"], bias, l, deps)
    dproj, dxact, dsm_ssd = _ssd_bwd(sv["xact"], sv["z"], sv["dt"], dmix, sv["hs"], sv["y_ssd"], p["dt_bias"], p["a_log"],
                                     p["d_skip"], p["ssm_norm_g"], dproj, l)
    dproj, dconv_w, dconv_b = _conv_bwd(sv["xbc"], dxact, wts["conv_w"], p["conv_b"], dproj, l)
    dw_in = _matmul("dw_in", "tn", sv["h1"], dproj, tm=D, tn=1280, tk=S, out_shape=SDS((D, D_IN_PAD), bf16),
                    out_specs=_plain(D, 1280), epilogue=_store_epi(bf16), pin_out=False)
    deps = send(l, dict(w_in=_w_in_slabs(dw_in)))
    dx, dg_mix = _matmul(
        "in_proj_dh", "nt", dproj, wts["w_in"], tm=512, tn=D, tk=D_IN_PAD, out_shape=[SDS((S, D), f32), SDS((1, D), f32)],
        out_specs=[_plain(512, D), grow], epilogue=_rms_bwd_epilogue(l),
        extras=(sv["x"], p["mix_norm_g"], dx_mid), extra_specs=(_plain(512, D), gfull, _plain(512, D)), deps=deps)
    small = dict(mix_norm_g=dg_mix, mlp_norm_g=dg_mlp, conv_w=dconv_w, conv_b=dconv_b, ssd=dsm_ssd, attn=dsm_attn, dbias=dbias)
    return dx, small, deps


def _local_step(x, tgt, p, get_weights, send):
    onehot_t = jnp.asarray(_onehot_buckets(), dtype=bf16)
    bias = _bias_build(p["rel_bias"].T, onehot_t).reshape(NQ, BLK, 2 * BLK)
    saved = []
    h = x
    for l in range(DEPTH):
        h, sv = _layer_fwd(l, h, p, get_weights, bias, tgt if l == DEPTH - 1 else None)
        saved.append(sv)
    dx, loss = h
    smalls = [None] * DEPTH
    deps = ()
    for l in reversed(range(DEPTH)):
        dx, smalls[l], deps = _layer_bwd(l, dx, saved[l], p, bias, deps, send)
    drel_t = _bias_grad(smalls[0]["dbias"].reshape(NQ, -1), smalls[1]["dbias"].reshape(NQ, -1), onehot_t)
    return dx, _pack_small_grads(smalls, drel_t, loss)


WEIGHT_ORDER = ("mix_norm_g", "w_in", "q_gain", "k_gain", "sinks", "rel_bias", "conv_w", "conv_b", "dt_bias", "a_log", "d_skip",
                "ssm_norm_g", "w_out", "mlp_norm_g", "w_up", "w_down")


def kernel(x, mix_norm_g, w_in, q_gain, k_gain, sinks, rel_bias, conv_w, conv_b, dt_bias, a_log, d_skip, ssm_norm_g, w_out, mlp_norm_g, w_up, w_down, loss_target, m_mix_norm_g, m_w_in, m_q_gain, m_k_gain, m_sinks, m_rel_bias, m_conv_w, m_conv_b, m_dt_bias, m_a_log, m_d_skip, m_ssm_norm_g, m_w_out, m_mlp_norm_g, m_w_up, m_w_down, v_mix_norm_g, v_w_in, v_q_gain, v_k_gain, v_sinks, v_rel_bias, v_conv_w, v_conv_b, v_dt_bias, v_a_log, v_d_skip, v_ssm_norm_g, v_w_out, v_mlp_norm_g, v_w_up, v_w_down):
    w = dict(mix_norm_g=mix_norm_g, w_in=w_in, q_gain=q_gain, k_gain=k_gain, sinks=sinks, rel_bias=rel_bias, conv_w=conv_w,
             conv_b=conv_b, dt_bias=dt_bias, a_log=a_log, d_skip=d_skip, ssm_norm_g=ssm_norm_g, w_out=w_out,
             mlp_norm_g=mlp_norm_g, w_up=w_up, w_down=w_down)
    m = dict(mix_norm_g=m_mix_norm_g, w_in=m_w_in, q_gain=m_q_gain, k_gain=m_k_gain, sinks=m_sinks, rel_bias=m_rel_bias,
             conv_w=m_conv_w, conv_b=m_conv_b, dt_bias=m_dt_bias, a_log=m_a_log, d_skip=m_d_skip, ssm_norm_g=m_ssm_norm_g,
             w_out=m_w_out, mlp_norm_g=m_mlp_norm_g, w_up=m_w_up, w_down=m_w_down)
    v = dict(mix_norm_g=v_mix_norm_g, w_in=v_w_in, q_gain=v_q_gain, k_gain=v_k_gain, sinks=v_sinks, rel_bias=v_rel_bias,
             conv_w=v_conv_w, conv_b=v_conv_b, dt_bias=v_dt_bias, a_log=v_a_log, d_skip=v_d_skip, ssm_norm_g=v_ssm_norm_g,
             w_out=v_w_out, mlp_norm_g=v_mlp_norm_g, w_up=v_w_up, w_down=v_w_down)
    big = ("w_in", "w_out", "w_up", "w_down")

    my_idx = _dev_index(*_my_place()).astype(jnp.int32).reshape(1)

    fulls = {n: _cast_to_full("cast_" + n, w[n], KIND[n], FULL_SHAPE[n], my_idx, bf16) for n in big}
    conv_full = _cast_to_full("cast_conv_w", conv_w.reshape(1, DEPTH * 4, 128), "stack", (N_DEV, DEPTH * 4, 128), my_idx, f32)[0]
    rest = ["w_out", "w_up", "w_down"]
    g0 = _gather_start("gather0", ["w_in", "conv_w"], [fulls["w_in"][0], conv_full], ())
    g1 = _gather_start("gather1", rest, [fulls[n][0] for n in rest], (g0["token"],))
    g2 = _gather_start("gather2", ["w_in"], [fulls["w_in"][1]], (g1["token"],))
    g3 = _gather_start("gather3", rest, [fulls[n][1] for n in rest], (g2["token"],))
    held = {}
    flat = lambda a: a.reshape(a.shape[0] * a.shape[1], a.shape[2])
    adam_in = {n: (flat(w[n]), flat(m[n]), flat(v[n])) for n in big}

    def get_weights(l, part, after):
        if l == 0 and part == "in":
            full_in, full_conv = _gather_finish("gather0", ["w_in", "conv_w"], g0,
                                                list(after) + [g3["token"], adam_in["w_in"][1], adam_in["w_in"][2]])
            held["conv_w"] = jnp.transpose(full_conv.reshape(N_DEV, DEPTH, 4, 128), (1, 2, 0, 3)).reshape(DEPTH, 4, D_CONV)
            return dict(w_in=_w_in_assemble(full_in), conv_w=held["conv_w"])
        if part == "in":
            return dict(w_in=_w_in_assemble(_gather_finish("gather2", ["w_in"], g2, after)[0]), conv_w=held["conv_w"])
        full = _gather_finish("gather1" if l == 0 else "gather3", rest, g1 if l == 0 else g3, after)
        return {n: f[None] for n, f in zip(rest, full)}

    pending = []

    def send(l, grads):
        names = list(grads)
        started = _exchange_start("exchange%d_%s" % (l, names[0]), names, [grads[n] for n in names], ())
        pending.append((l, names, started))
        return (started["token"],)

    dx, small_part = _local_step(x.reshape(S, D), loss_target.reshape(S, D), w, get_weights, send)

    small = _small_exchange_start(small_part, ())
    tiles = dict(w_in=512, w_out=128, w_up=512, w_down=256)
    outs_of = {n: None for n in big}
    after = [dx, small["token"]]
    for l, names, started in pending:
        bufs = _split_wait("exchange%d_%s_wait" % (l, names[0]), started, after)
        for t, n in enumerate(names):
            outs_of[n] = _adamw_layer("adamw_%s%d" % (n, l), KIND[n], l, *adam_in[n],
                                      bufs[len(names) + t], bufs[t], my_idx, outs_of[n], tiles[n])
        after = [outs_of[names[-1]][0]]
    res = {n: [o.reshape(w[n].shape) for o in outs_of[n]] for n in big}
    small_part, small_land = _split_wait("small_exchange_wait", small, after)
    small_outs = _adamw_small(small_part, small_land, w, m, v)
    loss = small_outs[0][0, 0]
    for k, name in enumerate(SMALL_NAMES):
        res[name] = small_outs[1 + 4 * k:5 + 4 * k]

    result = [loss, dx.reshape(1, S, D)]
    for k in range(4):
        result += [res[name][k] for name in WEIGHT_ORDER]
    return tuple(result)
```

```python
import functools
import math

import numpy as np
import jax
import jax.numpy as jnp
from jax import lax
from jax.experimental import pallas as pl
from jax.experimental.pallas import tpu as pltpu

f32 = jnp.float32
bf16 = jnp.bfloat16
SDS = jax.ShapeDtypeStruct
MESH = pl.DeviceIdType.MESH
HIGHEST = lax.Precision.HIGHEST

S = 2048
D = 1024
DEPTH = 2
BLK = 128
NBLK = S // BLK
HD = 64
NQ = 8
NKV = 2
NSSM = 8
NGRP = 2
NSTATE = 128
D_ATTN = 512
D_SSM = 512
D_CONV = 1024
D_FF = 4096
D_IN = 2312
D_IN_PAD = 2560
COL_QKV, COL_Z, COL_DT, COL_XBC = 0, 768, 1280, 1536
IN_SEGMENTS = ((0, 1280, 0), (1280, 2304, COL_XBC), (2304, 2312, COL_DT))
N_BUCKETS = 32
EPS = 1e-6
N_DEV = 8
VMEM_LIMIT = 48 * 1024 * 1024

ADAM_LR = 0.001
ADAM_B1 = 0.9
ADAM_B2 = 0.999
ADAM_EPS = 1e-08
ADAM_WD = 0.01
ADAM_STEP = 10

NT_DIMS = (((1,), (1,)), ((), ()))
TN_DIMS = (((0,), (0,)), ((), ()))
NN_DIMS = (((1,), (0,)), ((), ()))

ROW_MIXG = 0
ROW_MLPG = 2
ROW_CONVB = 4
ROW_SSMG = 6
ROW_MISC = 8
ROW_RELB = 10
ROW_CONVW = 18
ROW_LOSS = 26
SMALL_ROWS = 32
LANE_QG, LANE_KG, LANE_SINK, LANE_DTB, LANE_ALOG, LANE_DSKIP = 0, 64, 128, 256, 384, 512


def _dot(a, b, dims):
    return lax.dot_general(a, b, dims, preferred_element_type=f32)


def _cparams(n_axes):
    return pltpu.CompilerParams(dimension_semantics=("arbitrary",) * n_axes, vmem_limit_bytes=VMEM_LIMIT)


def _sum11(v):
    return jnp.sum(jnp.sum(v, axis=1, keepdims=True), axis=0, keepdims=True)


def _sigmoid(v):
    return 1.0 / (1.0 + jnp.exp(-v))


ANY_SPEC = pl.BlockSpec(memory_space=pl.ANY)


def _in_hbm(args):
    return [pltpu.with_memory_space_constraint(a, pltpu.HBM) if a.size >= 65536 else a for a in args]


def _out_hbm(out_shape):
    one = lambda s: pltpu.HBM(s.shape, s.dtype) if math.prod(s.shape) >= 65536 else s
    return [one(s) for s in out_shape] if isinstance(out_shape, (list, tuple)) else one(out_shape)


def _matmul(name, mode, a, b, *, layer=0, tm, tn, tk, out_shape, out_specs, epilogue, extras=(), extra_specs=(), deps=(),
            prologue=None, pin_out=True):
    extras = tuple(extras) + tuple(deps)
    extra_specs = tuple(extra_specs) + (ANY_SPEC,) * len(deps)
    if mode == "tn":
        t_dim, m_dim = a.shape
        n_dim = b.shape[1]
        grid = (m_dim // tm, n_dim // tn, t_dim // tk)
        a_spec = pl.BlockSpec((tk, tm), lambda i, j, k: (k, i))
        b_spec = pl.BlockSpec((tk, tn), lambda i, j, k: (k, j))
        dims = TN_DIMS
    elif mode == "nn":
        m_dim, k_dim = a.shape
        n_dim = b.shape[-1]
        grid = (m_dim // tm, n_dim // tn, k_dim // tk)
        a_spec = pl.BlockSpec((tm, tk), lambda i, j, k: (i, k))
        b_spec = pl.BlockSpec((None, tk, tn), lambda i, j, k: (layer, k, j))
        dims = NN_DIMS
    else:
        m_dim, k_dim = a.shape
        n_dim = b.shape[-2]
        grid = (m_dim // tm, n_dim // tn, k_dim // tk)
        a_spec = pl.BlockSpec((tm, tk), lambda i, j, k: (i, k))
        b_spec = pl.BlockSpec((None, tn, tk), lambda i, j, k: (layer, j, k))
        dims = NT_DIMS
    nk = grid[2]
    n_ex = len(extras)

    def body(a_ref, b_ref, *rest):
        ex = rest[:n_ex - len(deps)]
        outs = rest[n_ex:-1]
        acc = rest[-1]
        i = pl.program_id(0)
        j = pl.program_id(1)
        k = pl.program_id(2)
        lhs = a_ref[...].astype(bf16) if prologue is None else prologue(a_ref, ex, outs)
        part = _dot(lhs, b_ref[...].astype(bf16), dims)
        if nk == 1:
            epilogue(part, i, j, ex, outs)
        else:
            @pl.when(k == 0)
            def _():
                acc[...] = part

            @pl.when(k > 0)
            def _():
                acc[...] += part

            @pl.when(k == nk - 1)
            def _():
                epilogue(acc[...], i, j, ex, outs)

    return pl.pallas_call(
        body, grid=grid, in_specs=[a_spec, b_spec, *extra_specs], out_specs=out_specs,
        out_shape=_out_hbm(out_shape) if pin_out else out_shape,
        scratch_shapes=[pltpu.VMEM((tm, tn) if nk > 1 else (8, 128), f32)], name=name, compiler_params=_cparams(3),
    )(*_in_hbm([a]), b, *_in_hbm(extras))


def _rms_bwd_epilogue(layer):
    def epi(acc, i, j, ex, outs):
        x_ref, g_ref, dres_ref = ex
        dx_ref, dg_ref = outs
        xv = x_ref[...]
        r = lax.rsqrt(jnp.mean(xv * xv, axis=-1, keepdims=True) + EPS)
        xhat = xv * r
        w = acc * g_ref[layer:layer + 1, :]
        dx_ref[...] = dres_ref[...] + r * (w - xhat * jnp.mean(xhat * w, axis=-1, keepdims=True))
        dg = jnp.sum(acc * xhat, axis=0, keepdims=True)

        @pl.when(i == 0)
        def _():
            dg_ref[...] = dg

        @pl.when(i > 0)
        def _():
            dg_ref[...] += dg
    return epi


def _own_slab_spec(kind, tr, cols, nblk):
    if kind == "stack":
        return pl.BlockSpec((None, tr, cols), lambda i, idx: (idx[0], i, 0))
    if kind == "cols512":
        return pl.BlockSpec((tr, cols), lambda i, idx: (i, idx[0]))
    return pl.BlockSpec((tr, cols), lambda i, idx: (idx[0] * nblk + i, 0))


def _cast_to_full(name, w, kind, full_shape, my_idx, dtype):
    n_layers, rows, cols = w.shape
    tr = min(rows, 256)
    nblk = rows // tr

    def body(idx_ref, w_ref, *o_refs):
        for l in range(n_layers):
            o_refs[l][...] = w_ref[l].astype(dtype)

    grid_spec = pltpu.PrefetchScalarGridSpec(
        num_scalar_prefetch=1, grid=(nblk,), in_specs=[pl.BlockSpec((n_layers, tr, cols), lambda i, idx: (0, i, 0))],
        out_specs=[_own_slab_spec(kind, tr, cols, nblk)] * n_layers)
    return pl.pallas_call(body, grid_spec=grid_spec, out_shape=_out_hbm([SDS(full_shape, dtype)] * n_layers), name=name,
                          compiler_params=_cparams(1))(*_in_hbm([my_idx, w]))


def _adamw_math(w, m, v, g):
    m_new = ADAM_B1 * m + (1.0 - ADAM_B1) * g
    v_new = ADAM_B2 * v + (1.0 - ADAM_B2) * (g * g)
    m_hat = m_new / (1.0 - ADAM_B1 ** ADAM_STEP)
    v_hat = v_new / (1.0 - ADAM_B2 ** ADAM_STEP)
    delta = -ADAM_LR * (m_hat / (jnp.sqrt(v_hat) + ADAM_EPS) + ADAM_WD * w)
    return delta, m_new, v_new


def _adamw_layer(name, kind, layer, w, m, v, land, g_full, my_idx, prev, tr, pin_out=True):
    rows2, cols = w.shape
    rows = rows2 // DEPTH
    nblk = rows // tr
    own_spec = _own_slab_spec(kind, tr, cols, nblk)
    n_prev = 0 if prev is None else 4

    def body(idx_ref, w_ref, m_ref, v_ref, land_ref, own_ref, *rest):
        g_ref, d_ref, mo_ref, vo_ref = rest[n_prev:]
        me = idx_ref[0]
        g = None
        for p in range(N_DEV):
            part = jnp.where(me == p, own_ref[...], land_ref[p]).astype(f32)
            g = part if g is None else g + part
        delta, m_new, v_new = _adamw_math(w_ref[...], m_ref[...], v_ref[...], g)
        g_ref[...] = g
        d_ref[...] = delta
        mo_ref[...] = m_new
        vo_ref[...] = v_new

    blk = pl.BlockSpec((tr, cols), lambda i, idx: (layer * nblk + i, 0))
    grid_spec = pltpu.PrefetchScalarGridSpec(
        num_scalar_prefetch=1, grid=(nblk,),
        in_specs=[blk, blk, blk, pl.BlockSpec((N_DEV, tr, cols), lambda i, idx: (0, i, 0)), own_spec] + [ANY_SPEC] * n_prev,
        out_specs=[blk, blk, blk, blk])
    aliases = {} if prev is None else {6 + k: k for k in range(4)}
    results = [SDS((rows2, cols), f32)] * 4
    return pl.pallas_call(
        body, grid_spec=grid_spec, out_shape=_out_hbm(results) if pin_out else results, name=name, input_output_aliases=aliases,
        compiler_params=_cparams(1),
    )(*_in_hbm([my_idx, w, m, v, land, g_full]), *([] if prev is None else _in_hbm(prev) if pin_out else prev))


def _bucket_table():
    qi = np.arange(BLK)[:, None]
    kj = np.arange(2 * BLK)[None, :]
    dist = qi + BLK - kj
    dcl = np.clip(dist, 0, None)
    max_exact = N_BUCKETS // 2
    d_f = np.maximum(dcl, 1).astype(np.float32)
    large = max_exact + (np.log(d_f / np.float32(max_exact)) / np.float32(math.log(128 / max_exact))
                         * np.float32(N_BUCKETS - max_exact)).astype(np.int32)
    large = np.minimum(large, N_BUCKETS - 1)
    bucket = np.where(dcl < max_exact, dcl, large)
    in_window = (dist >= 0) & (dist < BLK)
    return bucket.astype(np.int32), in_window


def _onehot_buckets():
    bucket, _ = _bucket_table()
    oh = (bucket.reshape(-1)[None, :] == np.arange(N_BUCKETS)[:, None]).astype(np.float32)
    return oh


def _bias_build(rel_bias_t, onehot_t):
    def body(r_ref, o_ref, out_ref):
        r = r_ref[...]
        hi = r.astype(bf16)
        r1 = r - hi.astype(f32)
        mid = r1.astype(bf16)
        lo = (r1 - mid.astype(f32)).astype(bf16)
        oh = o_ref[...]
        out_ref[...] = _dot(hi, oh, NN_DIMS) + _dot(mid, oh, NN_DIMS) + _dot(lo, oh, NN_DIMS)

    tn = 4096
    return pl.pallas_call(
        body, grid=(BLK * 2 * BLK // tn,),
        in_specs=[pl.BlockSpec((NQ, N_BUCKETS), lambda i: (0, 0)), pl.BlockSpec((N_BUCKETS, tn), lambda i: (0, i))],
        out_specs=pl.BlockSpec((NQ, tn), lambda i: (0, i)), out_shape=SDS((NQ, BLK * 2 * BLK), f32), name="bias_build",
        compiler_params=_cparams(1),
    )(rel_bias_t, onehot_t)


def _bias_grad(dbias0, dbias1, onehot_t):
    tn = 4096
    nsteps = BLK * 2 * BLK // tn

    def body(a_ref, b_ref, o_ref, out_ref):
        g = a_ref[...] + b_ref[...]
        hi = g.astype(bf16)
        lo = (g - hi.astype(f32)).astype(bf16)
        part = _dot(hi, o_ref[...], NT_DIMS) + _dot(lo, o_ref[...], NT_DIMS)

        @pl.when(pl.program_id(0) == 0)
        def _():
            out_ref[...] = part

        @pl.when(pl.program_id(0) > 0)
        def _():
            out_ref[...] += part

    return pl.pallas_call(
        body, grid=(nsteps,),
        in_specs=[pl.BlockSpec((NQ, tn), lambda i: (0, i)), pl.BlockSpec((NQ, tn), lambda i: (0, i)),
                  pl.BlockSpec((N_BUCKETS, tn), lambda i: (0, i))],
        out_specs=pl.BlockSpec((NQ, N_BUCKETS), lambda i: (0, 0)), out_shape=SDS((NQ, N_BUCKETS), f32), name="bias_grad",
        compiler_params=_cparams(1),
    )(dbias0, dbias1, onehot_t)


def _attn_mask(n):
    qi = lax.broadcasted_iota(jnp.int32, (BLK, 2 * BLK), 0)
    kj = lax.broadcasted_iota(jnp.int32, (BLK, 2 * BLK), 1)
    dist = qi + BLK - kj
    first_key = jnp.where(n > 0, 0, BLK)
    return (dist >= 0) & (dist < BLK) & (kj >= first_key)


def _row_mean(a):
    return jnp.mean(a, axis=-1, keepdims=True)


def _head_norm(t, gain):
    r = lax.rsqrt(_row_mean(t * t) + EPS)
    that = t * r
    return that, r, that * gain


def _softmax_with_sink(s, sink):
    m = jnp.maximum(jnp.max(s, axis=-1, keepdims=True), sink)
    p = jnp.exp(s - m)
    psink = jnp.exp(sink - m)
    inv = 1.0 / (jnp.sum(p, axis=-1, keepdims=True) + psink)
    return p * inv, psink * inv


GQ = NQ // NKV


def _attn_fwd(qkv, q_gain, k_gain, sinks, bias, layer):
    def body(q_ref, kc_ref, kp_ref, vc_ref, vp_ref, qg_ref, kg_ref, sk_ref, bias_ref, o_ref):
        m = pl.program_id(0)
        qg = qg_ref[layer:layer + 1, :]
        kg = kg_ref[layer:layer + 1, :]
        grp = range(NKV)
        chains = [(b, j) for b in range(2) for j in grp]
        masks = [jnp.tile(_attn_mask(2 * m + b), (GQ, 1)) for b in range(2)]
        kblk = [[kp_ref[:, pl.ds(HD * j, HD)].astype(f32), kc_ref[0:BLK, pl.ds(HD * j, HD)].astype(f32),
                 kc_ref[BLK:, pl.ds(HD * j, HD)].astype(f32)] for j in grp]
        vblk = [[vp_ref[:, pl.ds(HD * j, HD)].astype(bf16), vc_ref[0:BLK, pl.ds(HD * j, HD)].astype(bf16),
                 vc_ref[BLK:, pl.ds(HD * j, HD)].astype(bf16)] for j in grp]
        knb = [[_head_norm(kblk[j][t], kg)[2].astype(bf16) for t in range(3)] for j in grp]
        kn_b = {(b, j): jnp.concatenate([knb[j][b], knb[j][b + 1]], axis=0) for b, j in chains}
        vbs = {(b, j): jnp.concatenate([vblk[j][b], vblk[j][b + 1]], axis=0) for b, j in chains}
        rows = {}
        for b, j in chains:
            heads = [GQ * j + g for g in range(GQ)]
            rows[b, j] = (jnp.concatenate([q_ref[pl.ds(BLK * b, BLK), pl.ds(HD * h, HD)] for h in heads], axis=0).astype(f32),
                          jnp.concatenate([jnp.broadcast_to(sk_ref[layer:layer + 1, h:h + 1], (BLK, 1)) for h in heads], axis=0))
        qn_b = {c: _head_norm(rows[c][0], qg)[2].astype(bf16) for c in chains}
        ss = {(b, j): _dot(qn_b[b, j], kn_b[b, j], NT_DIMS) * (HD ** -0.5) + bias_ref[GQ * j:GQ * (j + 1)].reshape(GQ * BLK, 2 * BLK)
              for b, j in chains}
        ps = {(b, j): _softmax_with_sink(jnp.where(masks[b], ss[b, j], -jnp.inf), rows[b, j][1])[0] for b, j in chains}
        outs = {c: _dot(ps[c].astype(bf16), vbs[c], NN_DIMS).astype(bf16) for c in chains}
        for b, j in chains:
            for g in range(GQ):
                o_ref[pl.ds(BLK * b, BLK), pl.ds(HD * (GQ * j + g), HD)] = outs[b, j][BLK * g:BLK * (g + 1), :]

    prev = lambda m: jnp.maximum(2 * m - 1, 0)
    small = lambda shape: pl.BlockSpec(shape, lambda m: (0,) * len(shape))
    return pl.pallas_call(
        body, grid=(NBLK // 2,),
        in_specs=[pl.BlockSpec((2 * BLK, D_ATTN), lambda m: (m, 0)),
                  pl.BlockSpec((2 * BLK, 128), lambda m: (m, 4)), pl.BlockSpec((BLK, 128), lambda m: (prev(m), 4)),
                  pl.BlockSpec((2 * BLK, 128), lambda m: (m, 5)), pl.BlockSpec((BLK, 128), lambda m: (prev(m), 5)),
                  small((DEPTH, HD)), small((DEPTH, HD)), small((DEPTH, NQ)), small((NQ, BLK, 2 * BLK))],
        out_specs=pl.BlockSpec((2 * BLK, D_ATTN), lambda m: (m, 0)), out_shape=_out_hbm(SDS((S, D_ATTN), bf16)),
        name="attn_fwd", compiler_params=_cparams(1),
    )(*_in_hbm([qkv, qkv, qkv, qkv, qkv, q_gain, k_gain, sinks, bias]))


def _attn_bwd(qkv, dmix, q_gain, k_gain, sinks, bias, layer, deps=()):
    def body(q_ref, kc_ref, kp_ref, vc_ref, vp_ref, do_ref, qg_ref, kg_ref, sk_ref, bias_ref, *rest):
        dqkv_ref, dbias_ref, dsm_ref, carry = rest[len(deps):]
        i = pl.program_id(0)
        m = NBLK // 2 - 1 - i
        qg = qg_ref[layer:layer + 1, :]
        kg = kg_ref[layer:layer + 1, :]
        lane = lax.broadcasted_iota(jnp.int32, (1, 128), 1)

        @pl.when(i == 0)
        def _():
            carry[...] = jnp.zeros_like(carry)
            dbias_ref[...] = jnp.zeros_like(dbias_ref)
            dsm_ref[...] = jnp.zeros_like(dsm_ref)

        grp = range(NKV)
        chains = [(b, j) for b in range(2) for j in grp]
        masks = [jnp.tile(_attn_mask(2 * m + b), (GQ, 1)) for b in range(2)]
        kblk = [[kp_ref[:, pl.ds(HD * j, HD)].astype(f32), kc_ref[0:BLK, pl.ds(HD * j, HD)].astype(f32),
                 kc_ref[BLK:, pl.ds(HD * j, HD)].astype(f32)] for j in grp]
        vblk = [[vp_ref[:, pl.ds(HD * j, HD)].astype(bf16), vc_ref[0:BLK, pl.ds(HD * j, HD)].astype(bf16),
                 vc_ref[BLK:, pl.ds(HD * j, HD)].astype(bf16)] for j in grp]
        knorm = [[_head_norm(kblk[j][t], kg) for t in range(3)] for j in grp]
        kn_b = {(b, j): jnp.concatenate([knorm[j][b][2].astype(bf16), knorm[j][b + 1][2].astype(bf16)], axis=0) for b, j in chains}
        vbs = {(b, j): jnp.concatenate([vblk[j][b], vblk[j][b + 1]], axis=0) for b, j in chains}
        rows, do_b = {}, {}
        for b, j in chains:
            heads = [GQ * j + g for g in range(GQ)]
            qrows = pl.ds(BLK * b, BLK)
            rows[b, j] = (jnp.concatenate([q_ref[qrows, pl.ds(HD * h, HD)] for h in heads], axis=0).astype(f32),
                          jnp.concatenate([jnp.broadcast_to(sk_ref[layer:layer + 1, h:h + 1], (BLK, 1)) for h in heads], axis=0))
            do_b[b, j] = jnp.concatenate([do_ref[qrows, pl.ds(HD * h, HD)] for h in heads], axis=0).astype(bf16)
        qnorm = {c: _head_norm(rows[c][0], qg) for c in chains}
        qn_b = {c: qnorm[c][2].astype(bf16) for c in chains}
        ss = {(b, j): _dot(qn_b[b, j], kn_b[b, j], NT_DIMS) * (HD ** -0.5) + bias_ref[GQ * j:GQ * (j + 1)].reshape(GQ * BLK, 2 * BLK)
              for b, j in chains}
        sm = {(b, j): _softmax_with_sink(jnp.where(masks[b], ss[b, j], -jnp.inf), rows[b, j][1]) for b, j in chains}
        dps = {c: _dot(do_b[c], vbs[c], NT_DIMS) for c in chains}
        deltas = {c: jnp.sum(sm[c][0] * dps[c], axis=-1, keepdims=True) for c in chains}
        dss = {c: sm[c][0] * (dps[c] - deltas[c]) for c in chains}
        ds_b = {c: (dss[c] * (HD ** -0.5)).astype(bf16) for c in chains}
        dqn = {c: _dot(ds_b[c], kn_b[c], NN_DIMS) for c in chains}
        dkn = {c: _dot(ds_b[c], qn_b[c], TN_DIMS) for c in chains}
        dvs = {c: _dot(sm[c][0].astype(bf16), do_b[c], TN_DIMS) for c in chains}
        dqg = jnp.zeros((1, HD), f32)
        dkg = jnp.zeros((1, HD), f32)
        dsink = jnp.zeros((1, 128), f32)
        for b, j in chains:
            dbias_ref[GQ * j:GQ * (j + 1)] += dss[b, j].reshape(GQ, BLK, 2 * BLK)
            dsk = sm[b, j][1] * deltas[b, j]
            for g in range(GQ):
                dsink = dsink + jnp.where(lane == GQ * j + g, -_sum11(dsk[BLK * g:BLK * (g + 1), :]), 0.0)
            qhat, rq, _ = qnorm[b, j]
            w = dqn[b, j] * qg
            dq = rq * (w - qhat * _row_mean(qhat * w))
            for g in range(GQ):
                dqkv_ref[pl.ds(BLK * b, BLK), pl.ds(HD * (GQ * j + g), HD)] = dq[BLK * g:BLK * (g + 1), :].astype(bf16)
            dqg = dqg + jnp.sum(dqn[b, j] * qhat, axis=0, keepdims=True)
        for j in grp:
            dkn_t = [dkn[0, j][:BLK, :], dkn[0, j][BLK:, :] + dkn[1, j][:BLK, :], dkn[1, j][BLK:, :]]
            dv_t = [dvs[0, j][:BLK, :], dvs[0, j][BLK:, :] + dvs[1, j][:BLK, :], dvs[1, j][BLK:, :]]
            dk_t = []
            for t in range(3):
                khat, rk, _ = knorm[j][t]
                w = dkn_t[t] * kg
                dk_t.append(rk * (w - khat * _row_mean(khat * w)))
                dkg = dkg + jnp.sum(dkn_t[t] * khat, axis=0, keepdims=True)
            kcols, vcols = pl.ds(D_ATTN + HD * j, HD), pl.ds(D_ATTN + 128 + HD * j, HD)
            dqkv_ref[BLK:, kcols] = (dk_t[2] + carry[:, pl.ds(HD * j, HD)]).astype(bf16)
            dqkv_ref[BLK:, vcols] = (dv_t[2] + carry[:, pl.ds(128 + HD * j, HD)]).astype(bf16)
            dqkv_ref[0:BLK, kcols] = dk_t[1].astype(bf16)
            dqkv_ref[0:BLK, vcols] = dv_t[1].astype(bf16)
            carry[:, pl.ds(HD * j, HD)] = dk_t[0]
            carry[:, pl.ds(128 + HD * j, HD)] = dv_t[0]
        dsm_ref[0:1, 0:HD] += dqg
        dsm_ref[1:2, 0:HD] += dkg
        dsm_ref[2:3, :] += dsink

    rev = lambda i: NBLK // 2 - 1 - i
    prev = lambda i: jnp.maximum(NBLK - 3 - 2 * i, 0)
    small = lambda shape: pl.BlockSpec(shape, lambda i: (0,) * len(shape))
    return pl.pallas_call(
        body, grid=(NBLK // 2,),
        in_specs=[pl.BlockSpec((2 * BLK, D_ATTN), lambda i: (rev(i), 0)),
                  pl.BlockSpec((2 * BLK, 128), lambda i: (rev(i), 4)), pl.BlockSpec((BLK, 128), lambda i: (prev(i), 4)),
                  pl.BlockSpec((2 * BLK, 128), lambda i: (rev(i), 5)), pl.BlockSpec((BLK, 128), lambda i: (prev(i), 5)),
                  pl.BlockSpec((2 * BLK, D_ATTN), lambda i: (rev(i), 0)),
                  small((DEPTH, HD)), small((DEPTH, HD)), small((DEPTH, NQ)), small((NQ, BLK, 2 * BLK))] + [ANY_SPEC] * len(deps),
        out_specs=[pl.BlockSpec((2 * BLK, 768), lambda i: (rev(i), COL_QKV // 768)), small((NQ, BLK, 2 * BLK)), small((8, 128))],
        out_shape=_out_hbm([SDS((S, D_IN_PAD), bf16), SDS((NQ, BLK, 2 * BLK), f32), SDS((8, 128), f32)]),
        scratch_shapes=[pltpu.VMEM((BLK, 256), f32)], name="attn_bwd", compiler_params=_cparams(1),
    )(*_in_hbm([qkv, qkv, qkv, qkv, qkv, dmix, q_gain, k_gain, sinks, bias, *deps]))


CONV_TC = 256


def _shift_down(u, s):
    if s == 0:
        return u
    rows = lax.broadcasted_iota(jnp.int32, u.shape, 0)
    return jnp.where(rows >= s, pltpu.roll(u, s, 0), 0.0)


def _shift_up(u, s):
    if s == 0:
        return u
    rows = lax.broadcasted_iota(jnp.int32, u.shape, 0)
    return jnp.where(rows < u.shape[0] - s, pltpu.roll(u, u.shape[0] - s, 0), 0.0)


def _conv_specs():
    return [pl.BlockSpec((S, CONV_TC), lambda c: (0, c)),
            pl.BlockSpec((None, 4, CONV_TC), lambda c: (0, 0, c)),
            pl.BlockSpec((DEPTH, CONV_TC), lambda c: (0, c))]


def _conv_pre(u, w_ref, b_ref, layer):
    pre = b_ref[layer:layer + 1, :] + w_ref[3:4, :] * u
    for k in range(3):
        pre = pre + w_ref[k:k + 1, :] * _shift_down(u, 3 - k)
    return pre


def _conv_fwd(xbc, conv_w, conv_b, layer):
    def body(u_ref, w_ref, b_ref, o_ref):
        pre = _conv_pre(u_ref[...].astype(f32), w_ref, b_ref, layer)
        o_ref[...] = pre * _sigmoid(pre)

    specs = _conv_specs()
    specs[1] = pl.BlockSpec((None, 4, CONV_TC), lambda c: (layer, 0, c))
    return pl.pallas_call(
        body, grid=(D_CONV // CONV_TC,), in_specs=specs, out_specs=pl.BlockSpec((S, CONV_TC), lambda c: (0, c)),
        out_shape=_out_hbm(SDS((S, D_CONV), f32)), name="conv_fwd", compiler_params=_cparams(1),
    )(*_in_hbm([xbc, conv_w, conv_b]))


def _conv_bwd(xbc, dact, conv_w, conv_b, dproj, layer):
    def body(u_ref, w_ref, b_ref, da_ref, dproj_in, du_ref, dw_ref, db_ref):
        u = u_ref[...].astype(f32)
        pre = _conv_pre(u, w_ref, b_ref, layer)
        sg = _sigmoid(pre)
        dpre = da_ref[...] * (sg * (1.0 + pre * (1.0 - sg)))
        du = w_ref[3:4, :] * dpre
        for k in range(3):
            du = du + w_ref[k:k + 1, :] * _shift_up(dpre, 3 - k)
        du_ref[...] = du.astype(bf16)
        db_ref[...] = jnp.broadcast_to(jnp.sum(dpre, axis=0, keepdims=True), db_ref.shape)
        dw_ref[...] = jnp.zeros_like(dw_ref)
        for k in range(4):
            dw_ref[k:k + 1, :] = jnp.sum(dpre * _shift_down(u, 3 - k), axis=0, keepdims=True)

    specs = _conv_specs()
    specs[1] = pl.BlockSpec((None, 4, CONV_TC), lambda c: (layer, 0, c))
    col = pl.BlockSpec((S, CONV_TC), lambda c: (0, c))
    row8 = pl.BlockSpec((8, CONV_TC), lambda c: (0, c))
    return pl.pallas_call(
        body, grid=(D_CONV // CONV_TC,), in_specs=[*specs, col, ANY_SPEC],
        out_specs=[pl.BlockSpec((S, CONV_TC), lambda c: (0, COL_XBC // CONV_TC + c)), row8, row8],
        out_shape=_out_hbm([SDS((S, D_IN_PAD), bf16), SDS((8, D_CONV), f32), SDS((8, D_CONV), f32)]), name="conv_bwd",
        input_output_aliases={4: 0}, compiler_params=_cparams(1),
    )(*_in_hbm([xbc, conv_w, conv_b, dact, dproj]))


def _tri():
    return (lax.broadcasted_iota(jnp.int32, (BLK, BLK), 0) >= lax.broadcasted_iota(jnp.int32, (BLK, BLK), 1))


def _ssd_scalars(dt_ref, dtb_ref, alog_ref, layer):
    raw = dt_ref[:, 0:NSSM] + dtb_ref[layer:layer + 1, :]
    dtv = jnp.maximum(raw, 0.0) + jnp.log(1.0 + jnp.exp(-jnp.abs(raw)))
    a = -jnp.exp(alog_ref[layer:layer + 1, :])
    acs = jnp.dot(_tri().astype(f32), dtv * a, preferred_element_type=f32, precision=HIGHEST)
    return raw, dtv, a, acs


HG = NSSM // NGRP
GW = HG * HD


def _lane_expand(cols, g):
    lane_head = lax.broadcasted_iota(jnp.int32, (1, GW), 1) // HD
    out = cols[:, HG * g + HG - 1:HG * g + HG]
    for r in range(HG - 2, -1, -1):
        out = jnp.where(lane_head == r, cols[:, HG * g + r:HG * g + r + 1], out)
    return out


def _row_expand(vals, g):
    row_head = lax.broadcasted_iota(jnp.int32, (GW, 1), 0) // HD
    out = vals[:, HG * g + HG - 1:HG * g + HG]
    for r in range(HG - 2, -1, -1):
        out = jnp.where(row_head == r, vals[:, HG * g + r:HG * g + r + 1], out)
    return out


def _head_rowsums(a, g):
    sel = (lax.broadcasted_iota(jnp.int32, (GW, NSSM), 0) // HD + HG * g == lax.broadcasted_iota(jnp.int32, (GW, NSSM), 1)).astype(bf16)
    hi = a.astype(bf16)
    lo = (a - hi.astype(f32)).astype(bf16)
    return _dot(hi, sel, NN_DIMS) + _dot(lo, sel, NN_DIMS)


def _head_blocksums(v, g):
    sel = (lax.broadcasted_iota(jnp.int32, (GW, NSSM), 0) // HD + HG * g == lax.broadcasted_iota(jnp.int32, (GW, NSSM), 1)).astype(bf16)
    hi = v.astype(bf16)
    lo = (v - hi.astype(f32)).astype(bf16)
    return _dot(hi, sel, TN_DIMS) + _dot(lo, sel, TN_DIMS)


def _ssd_chunk_common(xc_ref, dt_ref, dtb_ref, alog_ref, h_rows, layer):
    raw, dtv, a, acs = _ssd_scalars(dt_ref, dtb_ref, alog_ref, layer)
    acs_t = acs.T
    last = acs[BLK - 1:BLK, :]
    c = dict(raw=raw, dtv=dtv, a=a, acs=acs, last=last, dte=jnp.exp(last - acs), e_all=jnp.exp(acs), cd=jnp.exp(last))
    grp, heads, tri = range(NGRP), range(NSSM), _tri()
    c["bm"] = [xc_ref[:, pl.ds(D_SSM + NSTATE * g, NSTATE)] for g in grp]
    c["bm_b"] = [c["bm"][g].astype(bf16) for g in grp]
    c["cm_b"] = [xc_ref[:, pl.ds(D_SSM + NGRP * NSTATE + NSTATE * g, NSTATE)].astype(bf16) for g in grp]
    c["cb"] = [_dot(c["cm_b"][g], c["bm_b"][g], NT_DIMS) for g in grp]
    c["x"] = [xc_ref[:, pl.ds(GW * g, GW)] for g in grp]
    c["dt"] = [_lane_expand(dtv, g) for g in grp]
    c["xdt"] = [c["x"][g] * c["dt"][g] for g in grp]
    c["xdt_b"] = [c["xdt"][g].astype(bf16) for g in grp]
    c["prev"] = [h_rows(g) for g in grp]
    c["prev_b"] = [c["prev"][g].astype(bf16) for g in grp]
    c["e"] = [_lane_expand(c["e_all"], g) for g in grp]
    c["y_off"] = [_dot(c["cm_b"][g], c["prev_b"][g], NT_DIMS) * c["e"][g] for g in grp]
    c["decay"] = [jnp.exp(jnp.where(tri, acs[:, h:h + 1] - acs_t[h:h + 1, :], -jnp.inf)) for h in heads]
    c["m"] = [c["cb"][h // HG] * c["decay"][h] for h in heads]
    c["m_b"] = [c["m"][h].astype(bf16) for h in heads]
    c["dte_x"] = [_lane_expand(c["dte"], g) for g in grp]
    c["xdte_b"] = [(c["xdt"][g] * c["dte_x"][g]).astype(bf16) for g in grp]
    return c


def _ssd_fwd(xact, z, dt, attn, dt_bias, a_log, d_skip, norm_g, layer):
    def body(xc_ref, z_ref, dt_ref, at_ref, dtb_ref, alog_ref, dsk_ref, ng_ref, mix_ref, hs_ref, y_ref, h_ref):
        n = pl.program_id(0)

        @pl.when(n == 0)
        def _():
            h_ref[...] = jnp.zeros_like(h_ref)

        hs_ref[...] = h_ref[...]
        c = _ssd_chunk_common(xc_ref, dt_ref, dtb_ref, alog_ref, lambda g: h_ref[pl.ds(GW * g, GW), :], layer)
        grp, heads = range(NGRP), range(NSSM)
        y_diag = [_dot(c["m_b"][h], c["xdt_b"][h // HG][:, HD * (h % HG):HD * (h % HG + 1)], NN_DIMS) for h in heads]
        new_st = [_dot(c["xdte_b"][g], c["bm_b"][g], TN_DIMS) for g in grp]
        for h in heads:
            y_ref[:, pl.ds(HD * h, HD)] = y_diag[h]
        dskip = dsk_ref[layer:layer + 1, :]
        for g in grp:
            cols = pl.ds(GW * g, GW)
            y_ref[:, cols] = y_ref[:, cols] + c["y_off"][g] + c["x"][g] * _lane_expand(dskip, g)
            h_ref[cols, :] = c["prev"][g] * _row_expand(c["cd"], g) + new_st[g]
        zv = z_ref[...].astype(f32)
        yz = y_ref[...] * (zv * _sigmoid(zv))
        mix_ref[:, 0:D_ATTN] = at_ref[...]
        for g in grp:
            yg = yz[:, GW * g:GW * (g + 1)]
            rs = lax.rsqrt(jnp.mean(yg * yg, axis=-1, keepdims=True) + EPS)
            mix_ref[:, D_ATTN + GW * g:D_ATTN + GW * (g + 1)] = (yg * rs * ng_ref[layer:layer + 1, GW * g:GW * (g + 1)]).astype(bf16)

    small = lambda shape: pl.BlockSpec(shape, lambda n: (0,) * len(shape))
    return pl.pallas_call(
        body, grid=(NBLK,),
        in_specs=[pl.BlockSpec((BLK, D_CONV), lambda n: (n, 0)), pl.BlockSpec((BLK, D_SSM), lambda n: (n, 0)),
                  pl.BlockSpec((BLK, 128), lambda n: (n, 0)), pl.BlockSpec((BLK, D_ATTN), lambda n: (n, 0)),
                  small((DEPTH, NSSM)), small((DEPTH, NSSM)), small((DEPTH, NSSM)), small((DEPTH, D_SSM))],
        out_specs=[pl.BlockSpec((BLK, D), lambda n: (n, 0)), pl.BlockSpec((None, NSSM * HD, NSTATE), lambda n: (n, 0, 0)),
                   pl.BlockSpec((BLK, D_SSM), lambda n: (n, 0))],
        out_shape=_out_hbm([SDS((S, D), bf16), SDS((NBLK, NSSM * HD, NSTATE), f32), SDS((S, D_SSM), f32)]),
        scratch_shapes=[pltpu.VMEM((NSSM * HD, NSTATE), f32)],
        name="ssd_fwd", compiler_params=_cparams(1),
    )(*_in_hbm([xact, z, dt, attn, dt_bias, a_log, d_skip, norm_g]))


def _ssd_bwd(xact, z, dt, dmix, hs, y, dt_bias, a_log, d_skip, norm_g, dproj, layer):
    def body(xc_ref, z_ref, dt_ref, do_ref, hs_ref, y_ref, dtb_ref, alog_ref, dsk_ref, ng_ref, dproj_in,
             dzdt_ref, dx_ref, dsm_ref, dh_ref, dy_ref):
        i = pl.program_id(0)

        @pl.when(i == 0)
        def _():
            dh_ref[...] = jnp.zeros_like(dh_ref)
            dsm_ref[...] = jnp.zeros_like(dsm_ref)

        c = _ssd_chunk_common(xc_ref, dt_ref, dtb_ref, alog_ref, lambda g: hs_ref[pl.ds(GW * g, GW), :], layer)
        raw, dtv, a = c["raw"], c["dtv"], c["a"]
        grp, heads = range(NGRP), range(NSSM)
        dskip = dsk_ref[layer:layer + 1, :]
        lane8 = lax.broadcasted_iota(jnp.int32, (1, NSSM), 1)
        sub8 = lax.broadcasted_iota(jnp.int32, (NSSM, 1), 0)

        zv = z_ref[...].astype(f32)
        sz = _sigmoid(zv)
        gz = zv * sz
        yv = y_ref[...]
        yz = yv * gz
        for g in grp:
            sl = slice(GW * g, GW * (g + 1))
            yg = yz[:, sl]
            rs = lax.rsqrt(jnp.mean(yg * yg, axis=-1, keepdims=True) + EPS)
            yhat = yg * rs
            dog = do_ref[:, sl]
            w = dog * ng_ref[layer:layer + 1, sl]
            dyz = rs * (w - yhat * jnp.mean(yhat * w, axis=-1, keepdims=True))
            dsm_ref[0:1, sl] += jnp.sum(dog * yhat, axis=0, keepdims=True)
            dy_ref[:, sl] = dyz * gz[:, sl]
            dzdt_ref[:, sl] = (dyz * yv[:, sl] * (sz[:, sl] * (1.0 + zv[:, sl] * (1.0 - sz[:, sl])))).astype(bf16)

        dy = [dy_ref[:, pl.ds(GW * g, GW)] for g in grp]
        dy_b = [dy[g].astype(bf16) for g in grp]
        hl = lambda h: slice(HD * (h % HG), HD * (h % HG + 1))
        dt_off_b = [(dy[g] * c["e"][g]).astype(bf16) for g in grp]
        dcm = [_dot(dt_off_b[g], c["prev_b"][g], NN_DIMS) for g in grp]
        dprev = [_dot(dt_off_b[g], c["cm_b"][g], TN_DIMS) for g in grp]
        yoff_rs = [_head_rowsums(dy[g] * c["y_off"][g], g) for g in grp]
        dhn = [dh_ref[pl.ds(GW * g, GW), :] for g in grp]
        dhn_b = [dhn[g].astype(bf16) for g in grp]
        dprev = [dprev[g] + dhn[g] * _row_expand(c["cd"], g) for g in grp]
        dhn_prev = [dhn[g] * c["prev"][g] for g in grp]
        u = [_dot(c["bm_b"][g], dhn_b[g], NT_DIMS) for g in grp]
        dbm = [_dot(c["xdte_b"][g], dhn_b[g], NN_DIMS) for g in grp]
        ddte_rs = [_head_rowsums(c["xdt"][g] * u[g], g) for g in grp]
        dm = [_dot(dy_b[h // HG][:, hl(h)], c["xdt_b"][h // HG][:, hl(h)], NT_DIMS) for h in heads]
        dxdt_in = [_dot(c["m_b"][h], dy_b[h // HG][:, hl(h)], TN_DIMS) for h in heads]
        dseg = [dm[h] * c["m"][h] for h in heads]
        dmd = [dm[h] * c["decay"][h] for h in heads]
        for h in heads:
            dx_ref[:, pl.ds(HD * h, HD)] = dxdt_in[h]

        tmp = (ddte_rs[0] + ddte_rs[1]) * c["dte"]
        dacs = yoff_rs[0] + yoff_rs[1] - tmp
        dacs_cols = jnp.zeros((NSSM, BLK), f32)
        ddtv = jnp.zeros((BLK, NSSM), f32)
        ddsk = jnp.zeros((BLK, NSSM), f32)
        hp = jnp.zeros((1, NSSM), f32)
        for g in grp:
            cols = pl.ds(GW * g, GW)
            dxdt = dx_ref[:, cols] + u[g] * c["dte_x"][g]
            dx_ref[:, cols] = dy[g] * _lane_expand(dskip, g) + dxdt * c["dt"][g]
            ddtv = ddtv + _head_rowsums(dxdt * c["x"][g], g)
            ddsk = ddsk + _head_rowsums(dy[g] * c["x"][g], g)
            dcb = dmd[HG * g]
            for r in range(1, HG):
                dcb = dcb + dmd[HG * g + r]
            dcb_b = dcb.astype(bf16)
            dx_ref[:, pl.ds(D_SSM + NSTATE * g, NSTATE)] = dbm[g] + _dot(dcb_b, c["cm_b"][g], TN_DIMS)
            dx_ref[:, pl.ds(D_SSM + NGRP * NSTATE + NSTATE * g, NSTATE)] = dcm[g] + _dot(dcb_b, c["bm_b"][g], NN_DIMS)
            dh_ref[cols, :] = dprev[g]
            hp = hp + _head_blocksums(jnp.sum(dhn_prev[g], axis=1, keepdims=True), g)
            for r in range(HG):
                h = HG * g + r
                dacs = dacs + (lane8 == h).astype(f32) * jnp.sum(dseg[h], axis=1, keepdims=True)
                dacs_cols = dacs_cols + (sub8 == h).astype(f32) * jnp.sum(dseg[h], axis=0, keepdims=True)
        dlast = hp * c["cd"] + jnp.sum(tmp, axis=0, keepdims=True)
        ddsk = jnp.sum(ddsk, axis=0, keepdims=True)

        row = lax.broadcasted_iota(jnp.int32, (BLK, 1), 0)
        dacs = dacs - dacs_cols.T + jnp.where(row == BLK - 1, dlast, 0.0)
        dda = lax.dot_general(_tri().astype(f32), dacs, TN_DIMS, preferred_element_type=f32, precision=HIGHEST)
        ddtv = ddtv + dda * a
        da = jnp.sum(dda * dtv, axis=0, keepdims=True)
        draw = ddtv * _sigmoid(raw)
        dzdt_ref[:, D_SSM:] = jnp.zeros((BLK, COL_XBC - COL_DT), bf16)
        dzdt_ref[:, D_SSM:D_SSM + NSSM] = draw.astype(bf16)
        dsm_ref[1:2, 0:NSSM] += jnp.sum(draw, axis=0, keepdims=True)
        dsm_ref[2:3, 0:NSSM] += da * a
        dsm_ref[3:4, 0:NSSM] += ddsk

    rev = lambda i: NBLK - 1 - i
    small = lambda shape: pl.BlockSpec(shape, lambda i: (0,) * len(shape))
    return pl.pallas_call(
        body, grid=(NBLK,),
        in_specs=[pl.BlockSpec((BLK, D_CONV), lambda i: (rev(i), 0)), pl.BlockSpec((BLK, D_SSM), lambda i: (rev(i), 0)),
                  pl.BlockSpec((BLK, 128), lambda i: (rev(i), 0)), pl.BlockSpec((BLK, D_SSM), lambda i: (rev(i), 1)),
                  pl.BlockSpec((None, NSSM * HD, NSTATE), lambda i: (rev(i), 0, 0)), pl.BlockSpec((BLK, D_SSM), lambda i: (rev(i), 0)),
                  small((DEPTH, NSSM)), small((DEPTH, NSSM)), small((DEPTH, NSSM)), small((DEPTH, D_SSM)), ANY_SPEC],
        out_specs=[pl.BlockSpec((BLK, COL_XBC - COL_Z), lambda i: (rev(i), COL_Z // (COL_XBC - COL_Z))),
                   pl.BlockSpec((BLK, D_CONV), lambda i: (rev(i), 0)), small((8, D_SSM))],
        out_shape=_out_hbm([SDS((S, D_IN_PAD), bf16), SDS((S, D_CONV), f32), SDS((8, D_SSM), f32)]),
        scratch_shapes=[pltpu.VMEM((NSSM * HD, NSTATE), f32), pltpu.VMEM((BLK, D_SSM), f32)],
        name="ssd_bwd", input_output_aliases={10: 0}, compiler_params=_cparams(1),
    )(*_in_hbm([xact, z, dt, dmix, hs, y, dt_bias, a_log, d_skip, norm_g, dproj]))


def _my_place():
    return lax.axis_index("x"), lax.axis_index("y"), lax.axis_index("c")


def _dev_index(px, py, pc):
    return 4 * px + 2 * py + pc


def _slab2(kind, ref, idx):
    if kind == "stack":
        return ref.at[idx]
    if kind == "rows128":
        return ref.at[pl.ds(pl.multiple_of(idx * 128, 128), 128), :]
    if kind == "rows512":
        return ref.at[pl.ds(pl.multiple_of(idx * 512, 512), 512), :]
    return ref.at[:, pl.ds(pl.multiple_of(idx * 512, 512), 512)]


def _slab_shape(kind, full_shape):
    if kind == "stack":
        return tuple(full_shape[1:])
    if kind == "rows128":
        return (128, full_shape[1])
    if kind == "rows512":
        return (512, full_shape[1])
    return (full_shape[0], 512)


KIND = dict(w_in="stack", w_out="rows128", w_up="cols512", w_down="rows512", conv_w="stack")
FULL_SHAPE = dict(w_in=(N_DEV, D, D_IN // N_DEV), w_out=(D, D), w_up=(D, D_FF), w_down=(D_FF, D))
HBM_SPEC = pl.BlockSpec(memory_space=pltpu.HBM)
SEM_SPEC = pl.BlockSpec(memory_space=pltpu.SEMAPHORE)
SIDE_EFFECT = pltpu.SideEffectType.DATAFLOW_SIDE_EFFECTING


def _peers_all():
    x, y, c = _my_place()
    return [(x ^ ((r >> 2) & 1), y ^ ((r >> 1) & 1), c ^ (r & 1)) for r in range(1, N_DEV)]


def _split_start(name, bufs, n_copies, plan, deps=()):
    nb = len(bufs)

    def body(*refs):
        ins = refs[:nb]
        send_sems, recv_sems = refs[nb + len(deps)], refs[nb + len(deps) + 1]
        token = refs[-1]
        for i, (src, dst, dev) in enumerate(plan(ins)):
            pltpu.make_async_remote_copy(src_ref=src, dst_ref=dst, send_sem=send_sems.at[i], recv_sem=recv_sems.at[i],
                                         device_id=dev, device_id_type=MESH).start()
        token[...] = jnp.zeros_like(token)

    outs = pl.pallas_call(
        body, name=name,
        out_shape=(pltpu.SemaphoreType.DMA((n_copies,)), pltpu.SemaphoreType.DMA((n_copies,)),
                   *[pltpu.HBM(b.shape, b.dtype) for b in bufs], SDS((8, 128), f32)),
        in_specs=[HBM_SPEC] * nb + [ANY_SPEC] * len(deps),
        out_specs=(SEM_SPEC, SEM_SPEC, *[HBM_SPEC] * nb, pl.BlockSpec(memory_space=pltpu.VMEM)),
        input_output_aliases={i: 2 + i for i in range(nb)},
        compiler_params=pltpu.CompilerParams(has_side_effects=SIDE_EFFECT),
    )(*[pltpu.with_memory_space_constraint(b, pltpu.HBM) for b in bufs], *deps)
    return dict(send=outs[0], recv=outs[1], bufs=list(outs[2:2 + nb]), token=outs[-1], plan=plan, n=n_copies)


def _split_wait(name, started, after):
    bufs = started["bufs"]
    nb = len(bufs)
    plan = started["plan"]

    def body(*refs):
        ins = refs[:nb]
        send_sems, recv_sems = refs[nb], refs[nb + 1]
        for i, (src, dst, dev) in enumerate(plan(ins)):
            cp = pltpu.make_async_remote_copy(src_ref=src, dst_ref=dst, send_sem=send_sems.at[i], recv_sem=recv_sems.at[i],
                                              device_id=dev, device_id_type=MESH)
            cp.wait_send()
            cp.wait_recv()

    outs = pl.pallas_call(
        body, name=name, out_shape=tuple(pltpu.HBM(b.shape, b.dtype) for b in bufs),
        in_specs=[HBM_SPEC] * nb + [SEM_SPEC, SEM_SPEC] + [ANY_SPEC] * len(after), out_specs=(HBM_SPEC,) * nb,
        input_output_aliases={i: i for i in range(nb)},
        compiler_params=pltpu.CompilerParams(has_side_effects=SIDE_EFFECT),
    )(*bufs, started["send"], started["recv"], *after)
    return list(outs)


def _gather_start(name, names, fulls, deps):
    n_t = len(names)

    def plan(refs):
        x, y, c = _my_place()
        my_idx = _dev_index(x, y, c)
        targets = [(x, y, 1 - c), (1 - x, y, c), (x, 1 - y, c), (1 - x, 1 - y, c)]
        slabs = [_slab2(KIND[names[t]], refs[t], my_idx) for t in range(n_t)]
        return [(slabs[t], slabs[t], dev) for t in range(n_t) for dev in targets]

    return _split_start(name, list(fulls), 4 * n_t, plan, deps)


def _gather_finish(name, names, started, after):
    n_t = len(names)
    fulls = _split_wait(name + "_wait", started, after)
    slab_shapes = [SDS(_slab_shape(KIND[n], f.shape), f.dtype) for n, f in zip(names, fulls)]

    def body(*refs):
        ins = refs[:n_t]
        outs = refs[n_t:2 * n_t]
        stage = refs[2 * n_t:3 * n_t]
        load_sems, send_sems, recv_sems = refs[3 * n_t:]
        x, y, c = _my_place()
        chips = [(1 - x, y), (x, 1 - y), (1 - x, 1 - y)]
        pairs = [(t, j) for t in range(n_t) for j in range(3)]
        loads = [pltpu.make_async_copy(_slab2(KIND[names[t]], ins[t], _dev_index(*chips[j], c)), stage[t].at[j], load_sems.at[t, j])
                 for t, j in pairs]
        for cp in loads:
            cp.start()

        def copy(t, j, core):
            return pltpu.make_async_remote_copy(
                src_ref=stage[t].at[j], dst_ref=_slab2(KIND[names[t]], outs[t], _dev_index(*chips[j], core)),
                send_sem=send_sems.at[t, j], recv_sem=recv_sems.at[t, j], device_id=(x, y, 1 - c), device_id_type=MESH)

        sends = [copy(t, j, c) for t, j in pairs]
        for ld, cp in zip(loads, sends):
            ld.wait()
            cp.start()
        for t, j in pairs:
            copy(t, j, 1 - c).wait_recv()
        for cp in sends:
            cp.wait_send()

    return pl.pallas_call(
        body, in_specs=[HBM_SPEC] * n_t, out_specs=[HBM_SPEC] * n_t, out_shape=[pltpu.HBM(b.shape, b.dtype) for b in fulls],
        input_output_aliases={t: t for t in range(n_t)},
        scratch_shapes=[pltpu.VMEM((3,) + s.shape, s.dtype) for s in slab_shapes]
        + [pltpu.SemaphoreType.DMA((n_t, 3)), pltpu.SemaphoreType.DMA((n_t, 3)), pltpu.SemaphoreType.DMA((n_t, 3))],
        name=name + "_pass", compiler_params=pltpu.CompilerParams(vmem_limit_bytes=VMEM_LIMIT),
    )(*fulls)


def _exchange_start(name, names, grads, deps):
    n_t = len(names)
    lands = [lax.empty((N_DEV,) + _slab_shape(KIND[n], g.shape), g.dtype) for n, g in zip(names, grads)]

    def plan(refs):
        my_idx = _dev_index(*_my_place())
        return [(_slab2(KIND[names[t]], refs[t], _dev_index(*peer)), refs[n_t + t].at[my_idx], peer)
                for t in range(n_t) for peer in _peers_all()]

    return _split_start(name, list(grads) + lands, 7 * n_t, plan, deps)


def _small_exchange_start(part, deps):
    land = lax.empty((N_DEV,) + part.shape, part.dtype)

    def plan(refs):
        my_idx = _dev_index(*_my_place())
        return [(refs[0], refs[1].at[my_idx], peer) for peer in _peers_all()]

    return _split_start("small_exchange", [part, land], N_DEV - 1, plan, deps)


def _slab_pieces():
    sh = D_IN // N_DEV
    out = []
    for j in range(N_DEV):
        for first, end, dst in IN_SEGMENTS:
            lo, hi = max(first, sh * j), min(end, sh * (j + 1))
            if lo < hi:
                out.append((j, lo - sh * j, hi - sh * j, dst + lo - first))
    return out


def _w_in_assemble(stacked):
    tr = 256
    sh = D_IN // N_DEV

    def body(i_ref, o_ref):
        o_ref[:, COL_DT:COL_XBC] = jnp.zeros((tr, COL_XBC - COL_DT), bf16)
        for j, lo, hi, dst in _slab_pieces():
            o_ref[:, dst:dst + hi - lo] = i_ref[j, :, lo:hi]

    return pl.pallas_call(
        body, grid=(D // tr,), in_specs=[pl.BlockSpec((N_DEV, tr, sh), lambda i: (0, i, 0))],
        out_specs=pl.BlockSpec((None, tr, D_IN_PAD), lambda i: (0, i, 0)), out_shape=SDS((1, D, D_IN_PAD), bf16),
        name="w_in_assemble", compiler_params=_cparams(1),
    )(*_in_hbm([stacked]))


def _w_in_slabs(dw_in):
    tr = 256
    sh = D_IN // N_DEV

    def body(i_ref, o_ref):
        for j, lo, hi, src in _slab_pieces():
            o_ref[j, :, lo:hi] = i_ref[:, src:src + hi - lo]

    return pl.pallas_call(
        body, grid=(D // tr,), in_specs=[pl.BlockSpec((tr, D_IN_PAD), lambda i: (i, 0))],
        out_specs=pl.BlockSpec((N_DEV, tr, sh), lambda i: (0, i, 0)), out_shape=_out_hbm(SDS((N_DEV, D, sh), bf16)),
        name="w_in_slabs", compiler_params=_cparams(1),
    )(dw_in)


SMALL_NAMES = ("mix_norm_g", "mlp_norm_g", "conv_b", "ssm_norm_g", "q_gain", "k_gain", "sinks", "dt_bias", "a_log", "d_skip",
               "rel_bias", "conv_w")
MISC_LANES = dict(q_gain=(LANE_QG, HD), k_gain=(LANE_KG, HD), sinks=(LANE_SINK, NQ), dt_bias=(LANE_DTB, NSSM),
                  a_log=(LANE_ALOG, NSSM), d_skip=(LANE_DSKIP, NSSM))


def _pack_small_grads(smalls, drel_t, loss):
    def body(*refs):
        o_ref = refs[-1]
        drel_ref, loss_ref = refs[-3], refs[-2]
        o_ref[...] = jnp.zeros_like(o_ref)
        for l in range(DEPTH):
            mixg, mlpg, convb, convw, ssd, attn = refs[6 * l:6 * l + 6]
            o_ref[ROW_MIXG + l:ROW_MIXG + l + 1, :] = mixg[...]
            o_ref[ROW_MLPG + l:ROW_MLPG + l + 1, :] = mlpg[...]
            o_ref[ROW_CONVB + l:ROW_CONVB + l + 1, :] = convb[0:1, :]
            o_ref[ROW_SSMG + l:ROW_SSMG + l + 1, 0:D_SSM] = ssd[0:1, :]
            o_ref[ROW_CONVW + 4 * l:ROW_CONVW + 4 * l + 4, :] = convw[0:4, :]
            row = slice(ROW_MISC + l, ROW_MISC + l + 1)
            o_ref[row, LANE_QG:LANE_QG + HD] = attn[0:1, 0:HD]
            o_ref[row, LANE_KG:LANE_KG + HD] = attn[1:2, 0:HD]
            o_ref[row, LANE_SINK:LANE_SINK + NQ] = attn[2:3, 0:NQ]
            o_ref[row, LANE_DTB:LANE_DTB + NSSM] = ssd[1:2, 0:NSSM]
            o_ref[row, LANE_ALOG:LANE_ALOG + NSSM] = ssd[2:3, 0:NSSM]
            o_ref[row, LANE_DSKIP:LANE_DSKIP + NSSM] = ssd[3:4, 0:NSSM]
        o_ref[ROW_RELB:ROW_RELB + NQ, 0:N_BUCKETS] = drel_ref[...]
        o_ref[ROW_LOSS:ROW_LOSS + 1, 0:1] = loss_ref[0:1, 0:1]

    args = []
    for sm in smalls:
        args += [sm["mix_norm_g"], sm["mlp_norm_g"], sm["conv_b"], sm["conv_w"], sm["ssd"], sm["attn"]]
    args += [drel_t, loss]
    return pl.pallas_call(body, out_shape=SDS((SMALL_ROWS, D), f32), name="pack_small_grads")(*args)


def _adamw_small(part, land, w, m, v):
    n = len(SMALL_NAMES)

    def grad_of(name, g_ref):
        if name == "mix_norm_g":
            return g_ref[ROW_MIXG:ROW_MIXG + DEPTH, :]
        if name == "mlp_norm_g":
            return g_ref[ROW_MLPG:ROW_MLPG + DEPTH, :]
        if name == "conv_b":
            return g_ref[ROW_CONVB:ROW_CONVB + DEPTH, :]
        if name == "ssm_norm_g":
            return g_ref[ROW_SSMG:ROW_SSMG + DEPTH, 0:D_SSM]
        if name == "rel_bias":
            return g_ref[ROW_RELB:ROW_RELB + NQ, 0:N_BUCKETS].T
        lane, width = MISC_LANES[name]
        return g_ref[ROW_MISC:ROW_MISC + DEPTH, lane:lane + width]

    def body(part_ref, land_ref, *refs):
        ws, ms, vs = refs[:n], refs[n:2 * n], refs[2 * n:3 * n]
        loss_ref = refs[3 * n]
        outs = refs[3 * n + 1:-1]
        g_ref = refs[-1]
        me = _dev_index(*_my_place())
        for p in range(N_DEV):
            term = jnp.where(me == p, part_ref[...], land_ref[p])
            if p == 0:
                g_ref[...] = term
            else:
                g_ref[...] += term
        loss_ref[...] = g_ref[ROW_LOSS:ROW_LOSS + 1, 0:128]
        my_cols = pl.ds(pl.multiple_of(me * 128, 128), 128)
        for k, name in enumerate(SMALL_NAMES):
            g_out, d_out, m_out, v_out = outs[4 * k:4 * k + 4]
            if name == "conv_w":
                for l in range(DEPTH):
                    g = g_ref[ROW_CONVW + 4 * l:ROW_CONVW + 4 * l + 4, my_cols]
                    delta, m_new, v_new = _adamw_math(ws[k][l], ms[k][l], vs[k][l], g)
                    g_out[l], d_out[l], m_out[l], v_out[l] = g, delta, m_new, v_new
            else:
                g = grad_of(name, g_ref)
                delta, m_new, v_new = _adamw_math(ws[k][...], ms[k][...], vs[k][...], g)
                g_out[...], d_out[...], m_out[...], v_out[...] = g, delta, m_new, v_new

    ws = [w[name] for name in SMALL_NAMES]
    out_shape = [SDS((1, 128), f32)]
    for a in ws:
        out_shape += [SDS(a.shape, f32)] * 4
    return pl.pallas_call(body, out_shape=out_shape, name="adamw_small", scratch_shapes=[pltpu.VMEM((SMALL_ROWS, D), f32)])(
        part, land, *ws, *[m[name] for name in SMALL_NAMES], *[v[name] for name in SMALL_NAMES])


def _plain(tm, tn):
    return pl.BlockSpec((tm, tn), lambda i, j, k: (i, j))


def _rowblk(tm, width):
    return pl.BlockSpec((tm, width), lambda i, j, k: (i, 0))


def _store_epi(dtype):
    def epi(acc, i, j, ex, outs):
        outs[0][...] = acc.astype(dtype)
    return epi


def _rms_prologue(layer):
    def pro(a_ref, ex, outs):
        xv = a_ref[...]
        r = lax.rsqrt(jnp.mean(xv * xv, axis=-1, keepdims=True) + EPS)
        h = (xv * r * ex[0][layer:layer + 1, :]).astype(bf16)
        outs[-1][...] = h
        return h
    return pro


MLP_TM = 256
MLP_VMEM = 56 * 1024 * 1024


def _resident(shape):
    return pl.BlockSpec((None,) + shape, lambda i: (0, 0, 0), pipeline_mode=pl.Buffered(1))


def _mlp_fwd(layer, x, mix, g, w_out, w_up, w_down, tgt=None):
    tm = MLP_TM
    with_loss = tgt is not None

    def body(x_ref, mix_ref, g_ref, wo_ref, wu_ref, wd_ref, *rest):
        xm_ref, a_ref, r_ref, h_ref = rest[with_loss:with_loss + 4]
        rest = rest[:with_loss] + rest[with_loss + 1:]
        i = pl.program_id(0)
        xv = x_ref[...] + _dot(mix_ref[...], wo_ref[...], NN_DIMS)
        xm_ref[...] = xv
        h = (xv * lax.rsqrt(jnp.mean(xv * xv, axis=-1, keepdims=True) + EPS) * g_ref[layer:layer + 1, :]).astype(bf16)
        h_ref[...] = h
        r = jnp.maximum(_dot(h, wu_ref[...], NN_DIMS), 0.0)
        a = (r * r).astype(bf16)
        a_ref[...] = a
        r_ref[...] = r.astype(bf16)
        y = xv + _dot(a, wd_ref[...], NN_DIMS)
        if not with_loss:
            rest[3][...] = y
            return
        err = y - rest[0][...]
        rest[4][...] = err * (1.0 / D)
        part = 0.5 * jnp.sum(jnp.mean(err * err, axis=-1, keepdims=True), axis=0, keepdims=True)

        @pl.when(i == 0)
        def _():
            rest[5][...] = jnp.zeros_like(rest[5])

        rest[5][...] += jnp.broadcast_to(part, rest[5].shape)

    row = lambda width: pl.BlockSpec((tm, width), lambda i: (i, 0))
    in_specs = [row(D), row(D), pl.BlockSpec((DEPTH, D), lambda i: (0, 0)), _resident((D, D)), _resident((D, D_FF)),
                _resident((D_FF, D))]
    out_specs = [row(D), row(D_FF), row(D_FF), row(D), row(D)]
    out_shape = [SDS((S, D), f32), SDS((S, D_FF), bf16), SDS((S, D_FF), bf16), SDS((S, D), bf16), SDS((S, D), f32)]
    args = [x, mix, g, w_out, w_up, w_down]
    if with_loss:
        in_specs.append(row(D))
        args.append(tgt)
        out_specs.append(pl.BlockSpec((1, 128), lambda i: (0, 0)))
        out_shape.append(SDS((1, 128), f32))
    return pl.pallas_call(
        body, grid=(S // tm,), in_specs=in_specs, out_specs=out_specs, out_shape=_out_hbm(out_shape),
        name="mlp_fwd_loss" if with_loss else "mlp_fwd",
        compiler_params=pltpu.CompilerParams(dimension_semantics=("arbitrary",), vmem_limit_bytes=MLP_VMEM),
    )(*_in_hbm(args[:3]), *args[3:6], *_in_hbm(args[6:]))


def _mlp_bwd_act(layer, dx_out, r_act, x_mid, g, w_down, w_up, w_out, deps):
    tm = MLP_TM

    def body(dxo_ref, r_ref, xm_ref, g_ref, wd_ref, wu_ref, wo_ref, *rest):
        du_ref, dx_ref, dg_ref, dmix_ref = rest[len(deps):]
        dxo = dxo_ref[...]
        du = (_dot(dxo.astype(bf16), wd_ref[...], NT_DIMS) * (2.0 * r_ref[...].astype(f32))).astype(bf16)
        du_ref[...] = du
        dh = _dot(du, wu_ref[...], NT_DIMS)
        _rms_bwd_epilogue(layer)(dh, pl.program_id(0), 0, (xm_ref, g_ref, dxo_ref), (dx_ref, dg_ref))
        dmix_ref[...] = _dot(dx_ref[...].astype(bf16), wo_ref[...], NT_DIMS)

    row = lambda width: pl.BlockSpec((tm, width), lambda i: (i, 0))
    return pl.pallas_call(
        body, grid=(S // tm,),
        in_specs=[row(D), row(D_FF), row(D), pl.BlockSpec((DEPTH, D), lambda i: (0, 0)), _resident((D_FF, D)), _resident((D, D_FF)),
                  _resident((D, D))] + [ANY_SPEC] * len(deps),
        out_specs=[row(D_FF), row(D), pl.BlockSpec((1, D), lambda i: (0, 0)), row(D)],
        out_shape=_out_hbm([SDS((S, D_FF), bf16), SDS((S, D), f32), SDS((1, D), f32), SDS((S, D), f32)]), name="mlp_bwd_act",
        compiler_params=pltpu.CompilerParams(dimension_semantics=("arbitrary",), vmem_limit_bytes=MLP_VMEM),
    )(*_in_hbm([dx_out, r_act, x_mid, g]), w_down, w_up, w_out, *_in_hbm(deps))


def _layer_fwd(l, x, p, get_weights, bias, tgt=None):
    wts = get_weights(l, "in", [x, bias])
    gfull = pl.BlockSpec((DEPTH, D), lambda i, j, k: (0, 0))
    tm = 512

    def inproj_epi(acc, i, j, ex, outs):
        outs[0][...] = acc[:, COL_QKV:COL_Z].astype(bf16)
        outs[1][...] = acc[:, COL_Z:COL_DT].astype(bf16)
        outs[2][...] = acc[:, COL_XBC:D_IN_PAD].astype(bf16)
        outs[3][...] = acc[:, COL_DT:COL_DT + 128]

    qkv, z, xbc, dt, h1 = _matmul(
        "in_proj", "nn", x, wts["w_in"], tm=tm, tn=D_IN_PAD, tk=D, prologue=_rms_prologue(l),
        extras=(p["mix_norm_g"],), extra_specs=(gfull,),
        out_shape=[SDS((S, 768), bf16), SDS((S, 512), bf16), SDS((S, 1024), bf16), SDS((S, 128), f32), SDS((S, D), bf16)],
        out_specs=[_rowblk(tm, 768), _rowblk(tm, 512), _rowblk(tm, 1024), _rowblk(tm, 128), _rowblk(tm, D)], epilogue=inproj_epi)
    attn = _attn_fwd(qkv, p["q_gain"], p["k_gain"], p["sinks"], bias, l)
    xact = _conv_fwd(xbc, wts["conv_w"], p["conv_b"], l)
    mix, hs, y_ssd = _ssd_fwd(xact, z, dt, attn, p["dt_bias"], p["a_log"], p["d_skip"], p["ssm_norm_g"], l)
    wts = dict(wts, **get_weights(l, "rest", [mix]))

    x_mid, a_act, r_act, h2, *result = _mlp_fwd(l, x, mix, p["mlp_norm_g"], wts["w_out"], wts["w_up"], wts["w_down"], tgt)
    saved = dict(x=x, h1=h1, qkv=qkv, z=z, xbc=xbc, dt=dt, xact=xact, mix=mix, hs=hs, y_ssd=y_ssd, x_mid=x_mid, h2=h2,
                 a=a_act, r=r_act, wts=wts)
    return (result[0] if tgt is None else tuple(result)), saved


def _layer_bwd(l, dx_out, sv, p, bias, deps, send):
    wts = sv["wts"]

    dw_down = _matmul("dw_down", "tn", sv["a"], dx_out, tm=512, tn=D, tk=S, out_shape=SDS((D_FF, D), bf16),
                      out_specs=_plain(512, D), epilogue=_store_epi(bf16), deps=deps)
    deps = send(l, dict(w_down=dw_down))
    du, dx_mid, dg_mlp, dmix = _mlp_bwd_act(l, dx_out, sv["r"], sv["x_mid"], p["mlp_norm_g"], wts["w_down"], wts["w_up"],
                                            wts["w_out"], deps)
    dw_up = _matmul("dw_up", "tn", sv["h2"], du, tm=D, tn=512, tk=S, out_shape=SDS((D, D_FF), bf16),
                    out_specs=_plain(D, 512), epilogue=_store_epi(bf16))
    dw_out = _matmul("dw_out", "tn", sv["mix"], dx_mid, tm=D, tn=512, tk=S, out_shape=SDS((D, D), bf16),
                     out_specs=_plain(D, 512), epilogue=_store_epi(bf16))
    deps = send(l, dict(w_up=dw_up, w_out=dw_out))
    gfull = pl.BlockSpec((DEPTH, D), lambda i, j, k: (0, 0))
    grow = pl.BlockSpec((1, D), lambda i, j, k: (0, 0))
    dproj, dbias, dsm_attn = _attn_bwd(sv["qkv"], dmix, p["q_gain"], p["k_gain"], p["sinks"], bias, l, deps)
    dproj, dxact, dsm_ssd = _ssd_bwd(sv["xact"], sv["z"], sv["dt"], dmix, sv["hs"], sv["y_ssd"], p["dt_bias"], p["a_log"],
                                     p["d_skip"], p["ssm_norm_g"], dproj, l)
    dproj, dconv_w, dconv_b = _conv_bwd(sv["xbc"], dxact, wts["conv_w"], p["conv_b"], dproj, l)
    dw_in = _matmul("dw_in", "tn", sv["h1"], dproj, tm=D, tn=1280, tk=S, out_shape=SDS((D, D_IN_PAD), bf16),
                    out_specs=_plain(D, 1280), epilogue=_store_epi(bf16), pin_out=False)
    deps = send(l, dict(w_in=_w_in_slabs(dw_in)))
    dx, dg_mix = _matmul(
        "in_proj_dh", "nt", dproj, wts["w_in"], tm=512, tn=D, tk=D_IN_PAD, out_shape=[SDS((S, D), f32), SDS((1, D), f32)],
        out_specs=[_plain(512, D), grow], epilogue=_rms_bwd_epilogue(l),
        extras=(sv["x"], p["mix_norm_g"], dx_mid), extra_specs=(_plain(512, D), gfull, _plain(512, D)), deps=deps)
    small = dict(mix_norm_g=dg_mix, mlp_norm_g=dg_mlp, conv_w=dconv_w, conv_b=dconv_b, ssd=dsm_ssd, attn=dsm_attn, dbias=dbias)
    return dx, small, deps


def _local_step(x, tgt, p, get_weights, send):
    onehot_t = jnp.asarray(_onehot_buckets(), dtype=bf16)
    bias = _bias_build(p["rel_bias"].T, onehot_t).reshape(NQ, BLK, 2 * BLK)
    saved = []
    h = x
    for l in range(DEPTH):
        h, sv = _layer_fwd(l, h, p, get_weights, bias, tgt if l == DEPTH - 1 else None)
        saved.append(sv)
    dx, loss = h
    smalls = [None] * DEPTH
    deps = ()
    for l in reversed(range(DEPTH)):
        dx, smalls[l], deps = _layer_bwd(l, dx, saved[l], p, bias, deps, send)
    drel_t = _bias_grad(smalls[0]["dbias"].reshape(NQ, -1), smalls[1]["dbias"].reshape(NQ, -1), onehot_t)
    return dx, _pack_small_grads(smalls, drel_t, loss)


WEIGHT_ORDER = ("mix_norm_g", "w_in", "q_gain", "k_gain", "sinks", "rel_bias", "conv_w", "conv_b", "dt_bias", "a_log", "d_skip",
                "ssm_norm_g", "w_out", "mlp_norm_g", "w_up", "w_down")


def kernel(x, mix_norm_g, w_in, q_gain, k_gain, sinks, rel_bias, conv_w, conv_b, dt_bias, a_log, d_skip, ssm_norm_g, w_out, mlp_norm_g, w_up, w_down, loss_target, m_mix_norm_g, m_w_in, m_q_gain, m_k_gain, m_sinks, m_rel_bias, m_conv_w, m_conv_b, m_dt_bias, m_a_log, m_d_skip, m_ssm_norm_g, m_w_out, m_mlp_norm_g, m_w_up, m_w_down, v_mix_norm_g, v_w_in, v_q_gain, v_k_gain, v_sinks, v_rel_bias, v_conv_w, v_conv_b, v_dt_bias, v_a_log, v_d_skip, v_ssm_norm_g, v_w_out, v_mlp_norm_g, v_w_up, v_w_down):
    w = dict(mix_norm_g=mix_norm_g, w_in=w_in, q_gain=q_gain, k_gain=k_gain, sinks=sinks, rel_bias=rel_bias, conv_w=conv_w,
             conv_b=conv_b, dt_bias=dt_bias, a_log=a_log, d_skip=d_skip, ssm_norm_g=ssm_norm_g, w_out=w_out,
             mlp_norm_g=mlp_norm_g, w_up=w_up, w_down=w_down)
    m = dict(mix_norm_g=m_mix_norm_g, w_in=m_w_in, q_gain=m_q_gain, k_gain=m_k_gain, sinks=m_sinks, rel_bias=m_rel_bias,
             conv_w=m_conv_w, conv_b=m_conv_b, dt_bias=m_dt_bias, a_log=m_a_log, d_skip=m_d_skip, ssm_norm_g=m_ssm_norm_g,
             w_out=m_w_out, mlp_norm_g=m_mlp_norm_g, w_up=m_w_up, w_down=m_w_down)
    v = dict(mix_norm_g=v_mix_norm_g, w_in=v_w_in, q_gain=v_q_gain, k_gain=v_k_gain, sinks=v_sinks, rel_bias=v_rel_bias,
             conv_w=v_conv_w, conv_b=v_conv_b, dt_bias=v_dt_bias, a_log=v_a_log, d_skip=v_d_skip, ssm_norm_g=v_ssm_norm_g,
             w_out=v_w_out, mlp_norm_g=v_mlp_norm_g, w_up=v_w_up, w_down=v_w_down)
    big = ("w_in", "w_out", "w_up", "w_down")

    my_idx = _dev_index(*_my_place()).astype(jnp.int32).reshape(1)

    fulls = {n: _cast_to_full("cast_" + n, w[n], KIND[n], FULL_SHAPE[n], my_idx, bf16) for n in big}
    conv_full = _cast_to_full("cast_conv_w", conv_w.reshape(1, DEPTH * 4, 128), "stack", (N_DEV, DEPTH * 4, 128), my_idx, f32)[0]
    rest = ["w_out", "w_up", "w_down"]
    g0 = _gather_start("gather0", ["w_in", "conv_w"], [fulls["w_in"][0], conv_full], ())
    g1 = _gather_start("gather1", rest, [fulls[n][0] for n in rest], (g0["token"],))
    g2 = _gather_start("gather2", ["w_in"], [fulls["w_in"][1]], (g1["token"],))
    g3 = _gather_start("gather3", rest, [fulls[n][1] for n in rest], (g2["token"],))
    held = {}
    flat = lambda a: a.reshape(a.shape[0] * a.shape[1], a.shape[2])
    adam_in = {n: (flat(w[n]), flat(m[n]), flat(v[n])) for n in big}

    def get_weights(l, part, after):
        if l == 0 and part == "in":
            full_in, full_conv = _gather_finish("gather0", ["w_in", "conv_w"], g0,
                                                list(after) + [g3["token"], adam_in["w_in"][1], adam_in["w_in"][2]])
            held["conv_w"] = jnp.transpose(full_conv.reshape(N_DEV, DEPTH, 4, 128), (1, 2, 0, 3)).reshape(DEPTH, 4, D_CONV)
            return dict(w_in=_w_in_assemble(full_in), conv_w=held["conv_w"])
        if part == "in":
            return dict(w_in=_w_in_assemble(_gather_finish("gather2", ["w_in"], g2, after)[0]), conv_w=held["conv_w"])
        full = _gather_finish("gather1" if l == 0 else "gather3", rest, g1 if l == 0 else g3, after)
        return {n: f[None] for n, f in zip(rest, full)}

    pending = []

    def send(l, grads):
        names = list(grads)
        started = _exchange_start("exchange%d_%s" % (l, names[0]), names, [grads[n] for n in names], ())
        pending.append((l, names, started))
        return (started["token"],)

    dx, small_part = _local_step(x.reshape(S, D), loss_target.reshape(S, D), w, get_weights, send)

    small = _small_exchange_start(small_part, ())
    tiles = dict(w_in=512, w_out=128, w_up=512, w_down=256)
    outs_of = {n: None for n in big}
    after = [dx, small["token"]]
    for l, names, started in pending:
        bufs = _split_wait("exchange%d_%s_wait" % (l, names[0]), started, after)
        for t, n in enumerate(names):
            outs_of[n] = _adamw_layer("adamw_%s%d" % (n, l), KIND[n], l, *adam_in[n],
                                      bufs[len(names) + t], bufs[t], my_idx, outs_of[n], tiles[n], pin_out=n != "w_in")
        after = [outs_of[names[-1]][0]]
    res = {n: [o.reshape(w[n].shape) for o in outs_of[n]] for n in big}
    small_part, small_land = _split_wait("small_exchange_wait", small, after)
    small_outs = _adamw_small(small_part, small_land, w, m, v)
    loss = small_outs[0][0, 0]
    for k, name in enumerate(SMALL_NAMES):
        res[name] = small_outs[1 + 4 * k:5 + 4 * k]

    result = [loss, dx.reshape(1, S, D)]
    for k in range(4):
        result += [res[name][k] for name in WEIGHT_ORDER]
    return tuple(result)
```
